```python
import jax, jax.numpy as jnp
from jax import lax
import numpy as np

D_MODEL = 1024
BATCH = 8
SEQ = 8192
DEPTH = 1

CHUNK = 64
Q_BLOCK = 128
EPS = 1e-6
D_FF = 2816
N_MOD = 9
CONV_WIDTH = 512
CONV_GROUPS = 8
CONV_K = 3
MLA_HEADS = 4
QK_NOPE = 128
QK_ROPE = 64
V_HEAD = 128
Q_LORA = 384
KV_LORA = 256
ROPE_THETA = 10000.0
MLA_WIDTH = MLA_HEADS * V_HEAD
MIX_WIDTH = CONV_WIDTH + MLA_WIDTH
IN_COLS = 3 * CONV_WIDTH + Q_LORA + KV_LORA + QK_ROPE

kernel_name = "hymba_conv_mla_macaron_adaln_block"


def rmsnorm(x, g):
    xf = x.astype(jnp.float32)
    y = xf * lax.rsqrt(jnp.mean(xf * xf, axis=-1, keepdims=True) + EPS)
    return (y * g.astype(jnp.float32)).astype(x.dtype)


def group_rmsnorm(y, g, n_groups):
    b, s, w = y.shape
    yf = y.astype(jnp.float32).reshape(b, s, n_groups, w // n_groups)
    yf = yf * lax.rsqrt(jnp.mean(yf * yf, axis=-1, keepdims=True) + EPS)
    return (yf.reshape(b, s, w) * g.astype(jnp.float32)).astype(y.dtype)


def modulate(h, shift, scale):
    return h * (1.0 + scale[:, None, :]) + shift[:, None, :]


def swiglu(h, w1, w3, w2):
    return (jax.nn.silu(h @ w1) * (h @ w3)) @ w2


def rope(x, cos, sin):
    half = x.shape[-1] // 2
    x1, x2 = x[..., :half], x[..., half:]
    return jnp.concatenate([x1 * cos - x2 * sin, x1 * sin + x2 * cos], axis=-1)


def short_conv_mixer(xb, xc, xu, conv_w):
    u = xc * xu
    s = u.shape[1]
    up = jnp.pad(u, ((0, 0), (CONV_K - 1, 0), (0, 0)))
    y = conv_w[0] * up[:, 0:s]
    for k in range(1, CONV_K):
        y = y + conv_w[k] * up[:, k:k + s]
    return xb * y


def mla(cq, ckv, kr, positions, q_norm_g, w_uq, kv_norm_g, w_ukv):
    b, s, _ = cq.shape
    q = (rmsnorm(cq, q_norm_g) @ w_uq).reshape(b, s, MLA_HEADS, QK_NOPE + QK_ROPE)
    q_nope, q_rope = q[..., :QK_NOPE], q[..., QK_NOPE:]
    kv = (rmsnorm(ckv, kv_norm_g) @ w_ukv).reshape(b, s, MLA_HEADS, QK_NOPE + V_HEAD)
    k_nope, v = kv[..., :QK_NOPE], kv[..., QK_NOPE:]

    inv_freq = ROPE_THETA ** (-jnp.arange(0, QK_ROPE, 2, dtype=jnp.float32) / QK_ROPE)
    ang = positions.astype(jnp.float32)[..., None] * inv_freq
    cos = jnp.cos(ang).astype(cq.dtype)
    sin = jnp.sin(ang).astype(cq.dtype)
    q_rope = rope(q_rope, cos[:, :, None, :], sin[:, :, None, :])
    k_rope = rope(kr, cos, sin)

    scale = (QK_NOPE + QK_ROPE) ** -0.5
    nblk = s // Q_BLOCK
    qn_b = q_nope.reshape(b, nblk, Q_BLOCK, MLA_HEADS, QK_NOPE).transpose(1, 0, 2, 3, 4)
    qr_b = q_rope.reshape(b, nblk, Q_BLOCK, MLA_HEADS, QK_ROPE).transpose(1, 0, 2, 3, 4)
    k_chunk = jnp.arange(s) // CHUNK

    def block(args):
        i, qn, qr = args
        sc = (jnp.einsum('bqhd,bkhd->bhqk', qn, k_nope)
              + jnp.einsum('bqhd,bkd->bhqk', qr, k_rope)).astype(jnp.float32) * scale
        q_chunk = (i * Q_BLOCK + jnp.arange(Q_BLOCK)) // CHUNK
        mask = k_chunk[None, :] <= q_chunk[:, None]
        sc = jnp.where(mask[None, None], sc, jnp.float32(-1e30))
        p = jax.nn.softmax(sc, axis=-1).astype(v.dtype)
        return jnp.einsum('bhqk,bkhd->bqhd', p, v)

    o = lax.map(block, (jnp.arange(nblk), qn_b, qr_b))
    return o.transpose(1, 0, 2, 3, 4).reshape(b, s, MLA_WIDTH)


def _fwd_setup_inputs(seed: int = 0) -> dict:
    key = jax.random.key(seed)
    ks = iter(jax.random.split(key, 32))
    f32 = jnp.float32
    L = DEPTH

    def nrm(shape, fan_in, scale=1.0):
        return jax.random.normal(next(ks), shape, f32) * (scale * fan_in ** -0.5)

    def gain(shape):
        return 1.0 + 0.05 * jax.random.normal(next(ks), shape, f32)

    x = jax.random.normal(next(ks), (BATCH, SEQ, D_MODEL), f32)
    c = jax.random.normal(next(ks), (BATCH, D_MODEL), f32)
    offset = jax.random.randint(next(ks), (BATCH, 1), 0, 4096)
    positions = (offset + jnp.arange(SEQ)[None, :]).astype(jnp.int32)
    return {
        "x": x,
        "c": c,
        "positions": positions,
        "ada_w": nrm((L, D_MODEL, N_MOD * D_MODEL), D_MODEL, 0.5),
        "ada_b": 0.02 * jax.random.normal(next(ks), (L, N_MOD * D_MODEL), f32),
        "norm_ffn1_g": gain((L, D_MODEL)),
        "ffn1_w1": nrm((L, D_MODEL, D_FF), D_MODEL),
        "ffn1_w3": nrm((L, D_MODEL, D_FF), D_MODEL),
        "ffn1_w2": nrm((L, D_FF, D_MODEL), D_FF),
        "norm_mix_g": gain((L, D_MODEL)),
        "w_in": nrm((L, D_MODEL, IN_COLS), D_MODEL),
        "conv_w": nrm((L, CONV_K, CONV_WIDTH), CONV_K),
        "q_norm_g": gain((L, Q_LORA)),
        "w_uq": nrm((L, Q_LORA, MLA_HEADS * (QK_NOPE + QK_ROPE)), Q_LORA),
        "kv_norm_g": gain((L, KV_LORA)),
        "w_ukv": nrm((L, KV_LORA, MLA_HEADS * (QK_NOPE + V_HEAD)), KV_LORA),
        "out_norm_g": gain((L, MIX_WIDTH)),
        "w_out": nrm((L, MIX_WIDTH, D_MODEL), MIX_WIDTH),
        "norm_ffn2_g": gain((L, D_MODEL)),
        "ffn2_w1": nrm((L, D_MODEL, D_FF), D_MODEL),
        "ffn2_w3": nrm((L, D_MODEL, D_FF), D_MODEL),
        "ffn2_w2": nrm((L, D_FF, D_MODEL), D_FF),
        "final_norm_g": gain((D_MODEL,)),
    }


def _fwd_reference(x, c, positions, ada_w, ada_b, norm_ffn1_g, ffn1_w1, ffn1_w3, ffn1_w2,
              norm_mix_g, w_in, conv_w, q_norm_g, w_uq, kv_norm_g, w_ukv, out_norm_g,
              w_out, norm_ffn2_g, ffn2_w1, ffn2_w3, ffn2_w2, final_norm_g):
    b = x.shape[0]
    cuts = [CONV_WIDTH, 2 * CONV_WIDTH, 3 * CONV_WIDTH,
            3 * CONV_WIDTH + Q_LORA, 3 * CONV_WIDTH + Q_LORA + KV_LORA]
    for l in range(DEPTH):
        mod = (jax.nn.silu(c) @ ada_w[l] + ada_b[l]).reshape(b, N_MOD, D_MODEL)
        sh1, sc1, g1 = mod[:, 0], mod[:, 1], mod[:, 2]
        sh2, sc2, g2 = mod[:, 3], mod[:, 4], mod[:, 5]
        sh3, sc3, g3 = mod[:, 6], mod[:, 7], mod[:, 8]

        h = modulate(rmsnorm(x, norm_ffn1_g[l]), sh1, sc1)
        x = x + 0.5 * g1[:, None, :] * swiglu(h, ffn1_w1[l], ffn1_w3[l], ffn1_w2[l])

        h = modulate(rmsnorm(x, norm_mix_g[l]), sh2, sc2)
        z = h @ w_in[l]
        xb, xc, xu, cq, ckv, kr = jnp.split(z, cuts, axis=-1)
        y_a = short_conv_mixer(xb, xc, xu, conv_w[l])
        y_b = mla(cq, ckv, kr, positions, q_norm_g[l], w_uq[l], kv_norm_g[l], w_ukv[l])
        y_a = group_rmsnorm(y_a, out_norm_g[l, :CONV_WIDTH], CONV_GROUPS)
        y_b = group_rmsnorm(y_b, out_norm_g[l, CONV_WIDTH:], MLA_HEADS)
        y = jnp.concatenate([y_a, y_b], axis=-1) @ w_out[l]
        x = x + g2[:, None, :] * y

        h = modulate(rmsnorm(x, norm_ffn2_g[l]), sh3, sc3)
        x = x + 0.5 * g3[:, None, :] * swiglu(h, ffn2_w1[l], ffn2_w3[l], ffn2_w2[l])
    return rmsnorm(x, final_norm_g)


import jax as _jax
import jax.numpy as _jnp

TWIN_FORMAT = 'train_step'
FWD_PARAMS = ['x', 'c', 'positions', 'ada_w', 'ada_b', 'norm_ffn1_g', 'ffn1_w1', 'ffn1_w3', 'ffn1_w2', 'norm_mix_g', 'w_in', 'conv_w', 'q_norm_g', 'w_uq', 'kv_norm_g', 'w_ukv', 'out_norm_g', 'w_out', 'norm_ffn2_g', 'ffn2_w1', 'ffn2_w3', 'ffn2_w2', 'final_norm_g']
TWIN_WEIGHTS = ['ada_w', 'ada_b', 'norm_ffn1_g', 'ffn1_w1', 'ffn1_w3', 'ffn1_w2', 'norm_mix_g', 'w_in', 'conv_w', 'q_norm_g', 'w_uq', 'kv_norm_g', 'w_ukv', 'out_norm_g', 'w_out', 'norm_ffn2_g', 'ffn2_w1', 'ffn2_w3', 'ffn2_w2', 'final_norm_g']
TWIN_DIFF_INPUT = 'x'
TWIN_INPUTS = ['x', 'c', 'positions', 'ada_w', 'ada_b', 'norm_ffn1_g', 'ffn1_w1', 'ffn1_w3', 'ffn1_w2', 'norm_mix_g', 'w_in', 'conv_w', 'q_norm_g', 'w_uq', 'kv_norm_g', 'w_ukv', 'out_norm_g', 'w_out', 'norm_ffn2_g', 'ffn2_w1', 'ffn2_w3', 'ffn2_w2', 'final_norm_g', 'loss_target', 'm_ada_w', 'm_ada_b', 'm_norm_ffn1_g', 'm_ffn1_w1', 'm_ffn1_w3', 'm_ffn1_w2', 'm_norm_mix_g', 'm_w_in', 'm_conv_w', 'm_q_norm_g', 'm_w_uq', 'm_kv_norm_g', 'm_w_ukv', 'm_out_norm_g', 'm_w_out', 'm_norm_ffn2_g', 'm_ffn2_w1', 'm_ffn2_w3', 'm_ffn2_w2', 'm_final_norm_g', 'v_ada_w', 'v_ada_b', 'v_norm_ffn1_g', 'v_ffn1_w1', 'v_ffn1_w3', 'v_ffn1_w2', 'v_norm_mix_g', 'v_w_in', 'v_conv_w', 'v_q_norm_g', 'v_w_uq', 'v_kv_norm_g', 'v_w_ukv', 'v_out_norm_g', 'v_w_out', 'v_norm_ffn2_g', 'v_ffn2_w1', 'v_ffn2_w3', 'v_ffn2_w2', 'v_final_norm_g']
TWIN_OUTPUTS = ['loss', 'grad_x', 'grad_ada_w', 'grad_ada_b', 'grad_norm_ffn1_g', 'grad_ffn1_w1', 'grad_ffn1_w3', 'grad_ffn1_w2', 'grad_norm_mix_g', 'grad_w_in', 'grad_conv_w', 'grad_q_norm_g', 'grad_w_uq', 'grad_kv_norm_g', 'grad_w_ukv', 'grad_out_norm_g', 'grad_w_out', 'grad_norm_ffn2_g', 'grad_ffn2_w1', 'grad_ffn2_w3', 'grad_ffn2_w2', 'grad_final_norm_g', 'delta_ada_w', 'delta_ada_b', 'delta_norm_ffn1_g', 'delta_ffn1_w1', 'delta_ffn1_w3', 'delta_ffn1_w2', 'delta_norm_mix_g', 'delta_w_in', 'delta_conv_w', 'delta_q_norm_g', 'delta_w_uq', 'delta_kv_norm_g', 'delta_w_ukv', 'delta_out_norm_g', 'delta_w_out', 'delta_norm_ffn2_g', 'delta_ffn2_w1', 'delta_ffn2_w3', 'delta_ffn2_w2', 'delta_final_norm_g', 'new_m_ada_w', 'new_m_ada_b', 'new_m_norm_ffn1_g', 'new_m_ffn1_w1', 'new_m_ffn1_w3', 'new_m_ffn1_w2', 'new_m_norm_mix_g', 'new_m_w_in', 'new_m_conv_w', 'new_m_q_norm_g', 'new_m_w_uq', 'new_m_kv_norm_g', 'new_m_w_ukv', 'new_m_out_norm_g', 'new_m_w_out', 'new_m_norm_ffn2_g', 'new_m_ffn2_w1', 'new_m_ffn2_w3', 'new_m_ffn2_w2', 'new_m_final_norm_g', 'new_v_ada_w', 'new_v_ada_b', 'new_v_norm_ffn1_g', 'new_v_ffn1_w1', 'new_v_ffn1_w3', 'new_v_ffn1_w2', 'new_v_norm_mix_g', 'new_v_w_in', 'new_v_conv_w', 'new_v_q_norm_g', 'new_v_w_uq', 'new_v_kv_norm_g', 'new_v_w_ukv', 'new_v_out_norm_g', 'new_v_w_out', 'new_v_norm_ffn2_g', 'new_v_ffn2_w1', 'new_v_ffn2_w3', 'new_v_ffn2_w2', 'new_v_final_norm_g']
TWIN_LEAF_KINDS = {'loss': 'loss', 'grad_x': 'grad_x', 'grad_ada_w': 'grad_w', 'grad_ada_b': 'grad_w', 'grad_norm_ffn1_g': 'grad_w', 'grad_ffn1_w1': 'grad_w', 'grad_ffn1_w3': 'grad_w', 'grad_ffn1_w2': 'grad_w', 'grad_norm_mix_g': 'grad_w', 'grad_w_in': 'grad_w', 'grad_conv_w': 'grad_w', 'grad_q_norm_g': 'grad_w', 'grad_w_uq': 'grad_w', 'grad_kv_norm_g': 'grad_w', 'grad_w_ukv': 'grad_w', 'grad_out_norm_g': 'grad_w', 'grad_w_out': 'grad_w', 'grad_norm_ffn2_g': 'grad_w', 'grad_ffn2_w1': 'grad_w', 'grad_ffn2_w3': 'grad_w', 'grad_ffn2_w2': 'grad_w', 'grad_final_norm_g': 'grad_w', 'delta_ada_w': 'delta_w', 'delta_ada_b': 'delta_w', 'delta_norm_ffn1_g': 'delta_w', 'delta_ffn1_w1': 'delta_w', 'delta_ffn1_w3': 'delta_w', 'delta_ffn1_w2': 'delta_w', 'delta_norm_mix_g': 'delta_w', 'delta_w_in': 'delta_w', 'delta_conv_w': 'delta_w', 'delta_q_norm_g': 'delta_w', 'delta_w_uq': 'delta_w', 'delta_kv_norm_g': 'delta_w', 'delta_w_ukv': 'delta_w', 'delta_out_norm_g': 'delta_w', 'delta_w_out': 'delta_w', 'delta_norm_ffn2_g': 'delta_w', 'delta_ffn2_w1': 'delta_w', 'delta_ffn2_w3': 'delta_w', 'delta_ffn2_w2': 'delta_w', 'delta_final_norm_g': 'delta_w', 'new_m_ada_w': 'new_m', 'new_m_ada_b': 'new_m', 'new_m_norm_ffn1_g': 'new_m', 'new_m_ffn1_w1': 'new_m', 'new_m_ffn1_w3': 'new_m', 'new_m_ffn1_w2': 'new_m', 'new_m_norm_mix_g': 'new_m', 'new_m_w_in': 'new_m', 'new_m_conv_w': 'new_m', 'new_m_q_norm_g': 'new_m', 'new_m_w_uq': 'new_m', 'new_m_kv_norm_g': 'new_m', 'new_m_w_ukv': 'new_m', 'new_m_out_norm_g': 'new_m', 'new_m_w_out': 'new_m', 'new_m_norm_ffn2_g': 'new_m', 'new_m_ffn2_w1': 'new_m', 'new_m_ffn2_w3': 'new_m', 'new_m_ffn2_w2': 'new_m', 'new_m_final_norm_g': 'new_m', 'new_v_ada_w': 'new_v', 'new_v_ada_b': 'new_v', 'new_v_norm_ffn1_g': 'new_v', 'new_v_ffn1_w1': 'new_v', 'new_v_ffn1_w3': 'new_v', 'new_v_ffn1_w2': 'new_v', 'new_v_norm_mix_g': 'new_v', 'new_v_w_in': 'new_v', 'new_v_conv_w': 'new_v', 'new_v_q_norm_g': 'new_v', 'new_v_w_uq': 'new_v', 'new_v_kv_norm_g': 'new_v', 'new_v_w_ukv': 'new_v', 'new_v_out_norm_g': 'new_v', 'new_v_w_out': 'new_v', 'new_v_norm_ffn2_g': 'new_v', 'new_v_ffn2_w1': 'new_v', 'new_v_ffn2_w3': 'new_v', 'new_v_ffn2_w2': 'new_v', 'new_v_final_norm_g': 'new_v'}


def _forward(args):
    return _fwd_reference(*[args[k] for k in FWD_PARAMS])


def _output_shape():
    def fwd():
        inp = _fwd_setup_inputs(0)
        return _fwd_reference(*[inp[k] for k in FWD_PARAMS])
    out = _jax.eval_shape(fwd)
    return out.shape, out.dtype

N_MICROBATCH = 1
ADAM_LR = 0.001
ADAM_B1 = 0.9
ADAM_B2 = 0.999
ADAM_EPS = 1e-08
ADAM_WD = 0.01
ADAM_STEP = 10
PER_EXAMPLE_BATCH_AXIS = {'x': 0, 'c': 0, 'positions': 0, 'loss_target': 0}
SHARED_INPUTS = []
_WEIGHT_DTYPES = {'ada_w': _jnp.float32, 'ada_b': _jnp.float32, 'norm_ffn1_g': _jnp.float32, 'ffn1_w1': _jnp.float32, 'ffn1_w3': _jnp.float32, 'ffn1_w2': _jnp.float32, 'norm_mix_g': _jnp.float32, 'w_in': _jnp.float32, 'conv_w': _jnp.float32, 'q_norm_g': _jnp.float32, 'w_uq': _jnp.float32, 'kv_norm_g': _jnp.float32, 'w_ukv': _jnp.float32, 'out_norm_g': _jnp.float32, 'w_out': _jnp.float32, 'norm_ffn2_g': _jnp.float32, 'ffn2_w1': _jnp.float32, 'ffn2_w3': _jnp.float32, 'ffn2_w2': _jnp.float32, 'final_norm_g': _jnp.float32}
MOMENT_SCALE = {'ada_w': 1.394759e-01, 'ada_b': 2.722790e-01, 'norm_ffn1_g': 4.155496e-02, 'ffn1_w1': 1.749669e-02, 'ffn1_w3': 1.708617e-02, 'ffn1_w2': 2.827460e-02, 'norm_mix_g': 9.681326e-02, 'w_in': 9.172623e-02, 'conv_w': 8.228571e-02, 'q_norm_g': 3.863973e-02, 'w_uq': 2.691604e-02, 'kv_norm_g': 2.477027e-01, 'w_ukv': 1.128716e-01, 'out_norm_g': 1.436872e-01, 'w_out': 1.191030e-01, 'norm_ffn2_g': 3.776150e-02, 'ffn2_w1': 1.620111e-02, 'ffn2_w3': 1.584418e-02, 'ffn2_w2': 2.634969e-02, 'final_norm_g': 6.420824e+01}


def _to_microbatches(a, axis):
    t = _jnp.moveaxis(a, axis, 0)
    t = t.reshape((N_MICROBATCH, t.shape[0] // N_MICROBATCH) + t.shape[1:])
    return _jnp.moveaxis(t, 1, axis + 1)


def setup_inputs(seed: int = 0) -> dict:
    inp = _fwd_setup_inputs(seed)
    key = _jax.random.fold_in(_jax.random.key(seed), 7919)
    shape, _ = _output_shape()
    out = dict(inp)
    out["loss_target"] = _jax.random.normal(_jax.random.fold_in(key, 0), shape, _jnp.float32)
    for i, name in enumerate(TWIN_WEIGHTS):
        w = inp[name].astype(_jnp.float32)
        if MOMENT_SCALE is None:
            s = _jnp.sqrt(_jnp.mean(_jnp.square(w)) + 1e-30)
        else:
            s = MOMENT_SCALE[name]
        km, kv = _jax.random.split(_jax.random.fold_in(key, i + 1))
        out[name] = w
        out["m_" + name] = s * _jax.random.normal(km, w.shape, _jnp.float32)
        out["v_" + name] = (s * s) * _jax.random.uniform(kv, w.shape, _jnp.float32, 0.5, 1.5)
    if N_MICROBATCH > 1:
        for name, axis in PER_EXAMPLE_BATCH_AXIS.items():
            out[name] = _to_microbatches(out[name], axis)
    return {'x': out['x'], 'c': out['c'], 'positions': out['positions'], 'ada_w': out['ada_w'], 'ada_b': out['ada_b'], 'norm_ffn1_g': out['norm_ffn1_g'], 'ffn1_w1': out['ffn1_w1'], 'ffn1_w3': out['ffn1_w3'], 'ffn1_w2': out['ffn1_w2'], 'norm_mix_g': out['norm_mix_g'], 'w_in': out['w_in'], 'conv_w': out['conv_w'], 'q_norm_g': out['q_norm_g'], 'w_uq': out['w_uq'], 'kv_norm_g': out['kv_norm_g'], 'w_ukv': out['w_ukv'], 'out_norm_g': out['out_norm_g'], 'w_out': out['w_out'], 'norm_ffn2_g': out['norm_ffn2_g'], 'ffn2_w1': out['ffn2_w1'], 'ffn2_w3': out['ffn2_w3'], 'ffn2_w2': out['ffn2_w2'], 'final_norm_g': out['final_norm_g'], 'loss_target': out['loss_target'], 'm_ada_w': out['m_ada_w'], 'm_ada_b': out['m_ada_b'], 'm_norm_ffn1_g': out['m_norm_ffn1_g'], 'm_ffn1_w1': out['m_ffn1_w1'], 'm_ffn1_w3': out['m_ffn1_w3'], 'm_ffn1_w2': out['m_ffn1_w2'], 'm_norm_mix_g': out['m_norm_mix_g'], 'm_w_in': out['m_w_in'], 'm_conv_w': out['m_conv_w'], 'm_q_norm_g': out['m_q_norm_g'], 'm_w_uq': out['m_w_uq'], 'm_kv_norm_g': out['m_kv_norm_g'], 'm_w_ukv': out['m_w_ukv'], 'm_out_norm_g': out['m_out_norm_g'], 'm_w_out': out['m_w_out'], 'm_norm_ffn2_g': out['m_norm_ffn2_g'], 'm_ffn2_w1': out['m_ffn2_w1'], 'm_ffn2_w3': out['m_ffn2_w3'], 'm_ffn2_w2': out['m_ffn2_w2'], 'm_final_norm_g': out['m_final_norm_g'], 'v_ada_w': out['v_ada_w'], 'v_ada_b': out['v_ada_b'], 'v_norm_ffn1_g': out['v_norm_ffn1_g'], 'v_ffn1_w1': out['v_ffn1_w1'], 'v_ffn1_w3': out['v_ffn1_w3'], 'v_ffn1_w2': out['v_ffn1_w2'], 'v_norm_mix_g': out['v_norm_mix_g'], 'v_w_in': out['v_w_in'], 'v_conv_w': out['v_conv_w'], 'v_q_norm_g': out['v_q_norm_g'], 'v_w_uq': out['v_w_uq'], 'v_kv_norm_g': out['v_kv_norm_g'], 'v_w_ukv': out['v_w_ukv'], 'v_out_norm_g': out['v_out_norm_g'], 'v_w_out': out['v_w_out'], 'v_norm_ffn2_g': out['v_norm_ffn2_g'], 'v_ffn2_w1': out['v_ffn2_w1'], 'v_ffn2_w3': out['v_ffn2_w3'], 'v_ffn2_w2': out['v_ffn2_w2'], 'v_final_norm_g': out['v_final_norm_g']}


def _loss(weights, diff, rest, loss_target):
    with _jax.named_scope("forward"):
        args = {**rest, TWIN_DIFF_INPUT: diff, **{k: w.astype(_WEIGHT_DTYPES[k]) for k, w in weights.items()}}
        y = _forward(args)
    with _jax.named_scope("loss_head"):
        err = _jnp.square(y.astype(_jnp.float32) - loss_target)
        return 0.5 * _jnp.sum(_jnp.mean(err, axis=-1)) if err.ndim else 0.5 * err


def _adamw(w, g, m, v):
    m = ADAM_B1 * m + (1.0 - ADAM_B1) * g
    v = ADAM_B2 * v + (1.0 - ADAM_B2) * _jnp.square(g)
    m_hat = m / (1.0 - ADAM_B1 ** ADAM_STEP)
    v_hat = v / (1.0 - ADAM_B2 ** ADAM_STEP)
    delta = -ADAM_LR * (m_hat / (_jnp.sqrt(v_hat) + ADAM_EPS) + ADAM_WD * w)
    return delta, m, v


def reference(x, c, positions, ada_w, ada_b, norm_ffn1_g, ffn1_w1, ffn1_w3, ffn1_w2, norm_mix_g, w_in, conv_w, q_norm_g, w_uq, kv_norm_g, w_ukv, out_norm_g, w_out, norm_ffn2_g, ffn2_w1, ffn2_w3, ffn2_w2, final_norm_g, loss_target, m_ada_w, m_ada_b, m_norm_ffn1_g, m_ffn1_w1, m_ffn1_w3, m_ffn1_w2, m_norm_mix_g, m_w_in, m_conv_w, m_q_norm_g, m_w_uq, m_kv_norm_g, m_w_ukv, m_out_norm_g, m_w_out, m_norm_ffn2_g, m_ffn2_w1, m_ffn2_w3, m_ffn2_w2, m_final_norm_g, v_ada_w, v_ada_b, v_norm_ffn1_g, v_ffn1_w1, v_ffn1_w3, v_ffn1_w2, v_norm_mix_g, v_w_in, v_conv_w, v_q_norm_g, v_w_uq, v_kv_norm_g, v_w_ukv, v_out_norm_g, v_w_out, v_norm_ffn2_g, v_ffn2_w1, v_ffn2_w3, v_ffn2_w2, v_final_norm_g):
    given = dict(x=x, c=c, positions=positions, ada_w=ada_w, ada_b=ada_b, norm_ffn1_g=norm_ffn1_g, ffn1_w1=ffn1_w1, ffn1_w3=ffn1_w3, ffn1_w2=ffn1_w2, norm_mix_g=norm_mix_g, w_in=w_in, conv_w=conv_w, q_norm_g=q_norm_g, w_uq=w_uq, kv_norm_g=kv_norm_g, w_ukv=w_ukv, out_norm_g=out_norm_g, w_out=w_out, norm_ffn2_g=norm_ffn2_g, ffn2_w1=ffn2_w1, ffn2_w3=ffn2_w3, ffn2_w2=ffn2_w2, final_norm_g=final_norm_g, loss_target=loss_target, m_ada_w=m_ada_w, m_ada_b=m_ada_b, m_norm_ffn1_g=m_norm_ffn1_g, m_ffn1_w1=m_ffn1_w1, m_ffn1_w3=m_ffn1_w3, m_ffn1_w2=m_ffn1_w2, m_norm_mix_g=m_norm_mix_g, m_w_in=m_w_in, m_conv_w=m_conv_w, m_q_norm_g=m_q_norm_g, m_w_uq=m_w_uq, m_kv_norm_g=m_kv_norm_g, m_w_ukv=m_w_ukv, m_out_norm_g=m_out_norm_g, m_w_out=m_w_out, m_norm_ffn2_g=m_norm_ffn2_g, m_ffn2_w1=m_ffn2_w1, m_ffn2_w3=m_ffn2_w3, m_ffn2_w2=m_ffn2_w2, m_final_norm_g=m_final_norm_g, v_ada_w=v_ada_w, v_ada_b=v_ada_b, v_norm_ffn1_g=v_norm_ffn1_g, v_ffn1_w1=v_ffn1_w1, v_ffn1_w3=v_ffn1_w3, v_ffn1_w2=v_ffn1_w2, v_norm_mix_g=v_norm_mix_g, v_w_in=v_w_in, v_conv_w=v_conv_w, v_q_norm_g=v_q_norm_g, v_w_uq=v_w_uq, v_kv_norm_g=v_kv_norm_g, v_w_ukv=v_w_ukv, v_out_norm_g=v_out_norm_g, v_w_out=v_w_out, v_norm_ffn2_g=v_norm_ffn2_g, v_ffn2_w1=v_ffn2_w1, v_ffn2_w3=v_ffn2_w3, v_ffn2_w2=v_ffn2_w2, v_final_norm_g=v_final_norm_g)
    weights = {n: given[n] for n in TWIN_WEIGHTS}
    shared = {n: given[n] for n in SHARED_INPUTS}
    per_example = {n: given[n] for n in ['x', 'c', 'positions']}
    grad_fn = _jax.value_and_grad(_loss, argnums=(0, 1))

    def one_microbatch(ex, loss_target):
        ex = dict(ex)
        diff = ex.pop(TWIN_DIFF_INPUT)
        return grad_fn(weights, diff, {**shared, **ex}, loss_target)

    if N_MICROBATCH == 1:
        loss, (grad_w, grad_x) = one_microbatch(per_example, given["loss_target"])
    else:
        def body(carry, xs):
            loss_sum, grad_sum = carry
            l_k, (gw_k, gx_k) = one_microbatch(xs[0], xs[1])
            with _jax.named_scope("update"):
                return (loss_sum + l_k, _jax.tree.map(_jnp.add, grad_sum, gw_k)), gx_k

        init = (_jnp.zeros((), _jnp.float32), _jax.tree.map(_jnp.zeros_like, weights))
        (loss, grad_w), grad_x = _jax.lax.scan(body, init, (per_example, given["loss_target"]))
    with _jax.named_scope("update"):
        delta_w, new_m, new_v = {}, {}, {}
        for n in TWIN_WEIGHTS:
            delta_w[n], new_m[n], new_v[n] = _adamw(weights[n], grad_w[n], given["m_" + n], given["v_" + n])
    return (loss, grad_x, *[grad_w[n] for n in TWIN_WEIGHTS], *[delta_w[n] for n in TWIN_WEIGHTS],
            *[new_m[n] for n in TWIN_WEIGHTS], *[new_v[n] for n in TWIN_WEIGHTS])
```

```python
import functools

import numpy as np
import jax
import jax.numpy as jnp
from jax import lax
from jax.experimental import pallas as pl
from jax.experimental.pallas import tpu as pltpu

F32 = jnp.float32
BF16 = jnp.bfloat16
MESH = pl.DeviceIdType.MESH

D = 1024
FF = 2816
CONV_W = 512
CONV_GROUP = 64
HEADS = 4
QK_NOPE = 128
QK_ROPE = 64
V_HEAD = 128
Q_LORA = 384
KV_LORA = 256
HEAD_SLOT = 256
IN_COLS = 3 * CONV_W + Q_LORA + KV_LORA + QK_ROPE
Z_COLS = 2304
EPS = 1e-6
ROPE_THETA = 10000.0
CHUNK = 64
ATT_SCALE = (QK_NOPE + QK_ROPE) ** -0.5
NEG = -1e30
N_MOD = 9

LR, B1, B2, AEPS, WD, STEP = 0.001, 0.9, 0.999, 1e-08, 0.01, 10

N_CHIPS = 4
N_DEV = 8
VMEM_LIMIT = 56 << 20


def _params(sem, vmem=VMEM_LIMIT):
    return pltpu.CompilerParams(dimension_semantics=sem, vmem_limit_bytes=vmem)


def _rms(v):
    return lax.rsqrt(jnp.mean(v * v, axis=-1, keepdims=True) + EPS)


def _rsum8(v):
    t, n = v.shape
    return jnp.sum(v.reshape(t // 8, 8, n), axis=0)


def _all_rows(ref):
    ref[...] = jnp.broadcast_to(jnp.sum(ref[...], axis=0, keepdims=True), ref.shape)


def _gsum(v, gmat):
    hi = v.astype(BF16)
    lo = (v - hi.astype(F32)).astype(BF16)
    return (jnp.dot(hi, gmat, preferred_element_type=F32)
            + jnp.dot(lo, gmat, preferred_element_type=F32))


def _silu_parts(a):
    sg = jax.nn.sigmoid(a)
    return sg, a * sg


def _rope(xr, cs, sn, lane):
    rh = jnp.where(lane < 32, -pltpu.roll(xr, 96, 1), pltpu.roll(xr, 32, 1))
    return xr * cs + rh * sn


def _rope_t(g, cs, sn, lane):
    y = g * sn
    rt = jnp.where(lane < 32, pltpu.roll(y, 96, 1), jnp.where(lane < 64, -pltpu.roll(y, 32, 1), 0.0))
    return g * cs + rt


def _row_tile(rows, pref):
    t = min(rows, pref)
    while rows % t or t % 8:
        t -= 8
    return t


def _ffn_up(x, ng, sh, sc, w1, w3, *, name):
    S = x.shape[0]
    tm, tn = _row_tile(S, 512), FF // 2

    def body(x_ref, g_ref, sh_ref, sc_ref, w1_ref, w3_ref, h_ref, a_ref, b_ref, u_ref, hs):
        @pl.when(pl.program_id(1) == 0)
        def _():
            xv = x_ref[...]
            h = ((xv * _rms(xv)) * g_ref[...]) * (1.0 + sc_ref[...]) + sh_ref[...]
            hb = h.astype(BF16)
            hs[...] = hb
            h_ref[...] = hb

        h = hs[...]
        a = jnp.dot(h, w1_ref[...], preferred_element_type=F32)
        b = jnp.dot(h, w3_ref[...], preferred_element_type=F32)
        _, sa = _silu_parts(a)
        a_ref[...] = a.astype(BF16)
        b_ref[...] = b.astype(BF16)
        u_ref[...] = (sa * b).astype(BF16)

    row = pl.BlockSpec((tm, D), lambda i, j: (i, 0))
    vec = pl.BlockSpec((1, D), lambda i, j: (0, 0))
    wsp = pl.BlockSpec((D, tn), lambda i, j: (0, j))
    osp = pl.BlockSpec((tm, tn), lambda i, j: (i, j))
    return pl.pallas_call(
        body, name=name, grid=(S // tm, FF // tn),
        in_specs=[row, vec, vec, vec, wsp, wsp],
        out_specs=[row, osp, osp, osp],
        out_shape=[jax.ShapeDtypeStruct((S, D), BF16)] + [jax.ShapeDtypeStruct((S, FF), BF16)] * 3,
        scratch_shapes=[pltpu.VMEM((tm, D), BF16)],
        compiler_params=_params(("parallel", "arbitrary")),
    )(x, ng, sh, sc, w1, w3)


def _ffn_down(u, w2, x, gate, *, name):
    S = x.shape[0]
    tm = _row_tile(S, 512)

    def body(u_ref, w2_ref, x_ref, g_ref, xo_ref, f_ref):
        f = jnp.dot(u_ref[...], w2_ref[...], preferred_element_type=F32)
        xo_ref[...] = x_ref[...] + (0.5 * g_ref[...]) * f
        f_ref[...] = f.astype(BF16)

    return pl.pallas_call(
        body, name=name, grid=(S // tm,),
        in_specs=[pl.BlockSpec((tm, FF), lambda i: (i, 0)), pl.BlockSpec((FF, D), lambda i: (0, 0)),
                  pl.BlockSpec((tm, D), lambda i: (i, 0)), pl.BlockSpec((1, D), lambda i: (0, 0))],
        out_specs=[pl.BlockSpec((tm, D), lambda i: (i, 0))] * 2,
        out_shape=[jax.ShapeDtypeStruct((S, D), F32), jax.ShapeDtypeStruct((S, D), BF16)],
        compiler_params=_params(("parallel",)),
    )(u, w2, x, gate)


def _ffn_bwd_du(dx, gate, f, w2t, a, b, *, name):
    S = dx.shape[0]
    tm, tn = _row_tile(S, 256), FF // 2
    n_i = S // tm

    def body(dx_ref, g_ref, f_ref, w_ref, a_ref, b_ref, df_ref, da_ref, db_ref, dg_ref, dfs):
        i, j = pl.program_id(0), pl.program_id(1)

        @pl.when((i == 0) & (j == 0))
        def _():
            dg_ref[...] = jnp.zeros_like(dg_ref)

        @pl.when(j == 0)
        def _():
            dxv = dx_ref[...]
            dfb = (dxv * (0.5 * g_ref[...])).astype(BF16)
            dfs[...] = dfb
            df_ref[...] = dfb
            dg_ref[...] += _rsum8(dxv * (0.5 * f_ref[...].astype(F32)))

        du = jnp.dot(dfs[...], w_ref[...], preferred_element_type=F32)
        av = a_ref[...].astype(F32)
        sg, sa = _silu_parts(av)
        da_ref[...] = (du * b_ref[...].astype(F32) * (sg * (1.0 + av * (1.0 - sg)))).astype(BF16)
        db_ref[...] = (du * sa).astype(BF16)

        @pl.when((i == n_i - 1) & (j == FF // tn - 1))
        def _():
            _all_rows(dg_ref)

    row = pl.BlockSpec((tm, D), lambda i, j: (i, 0))
    blk = pl.BlockSpec((tm, tn), lambda i, j: (i, j))
    return pl.pallas_call(
        body, name=name, grid=(n_i, FF // tn),
        in_specs=[row, pl.BlockSpec((1, D), lambda i, j: (0, 0)), row,
                  pl.BlockSpec((D, tn), lambda i, j: (0, j)), blk, blk],
        out_specs=[row, blk, blk, pl.BlockSpec((8, D), lambda i, j: (0, 0))],
        out_shape=[jax.ShapeDtypeStruct((S, D), BF16), jax.ShapeDtypeStruct((S, FF), BF16),
                   jax.ShapeDtypeStruct((S, FF), BF16), jax.ShapeDtypeStruct((8, D), F32)],
        scratch_shapes=[pltpu.VMEM((tm, D), BF16)],
        compiler_params=_params(("arbitrary", "arbitrary")),
    )(dx, gate, f, w2t, a, b)


def _tn_matmul(a, b, *, tm, tn, name):
    S, M = a.shape
    N = b.shape[1]
    ts = _row_tile(S, 1024)
    ns = S // ts

    def body(a_ref, b_ref, o_ref, acc):
        s = pl.program_id(2)
        p = lax.dot_general(a_ref[...], b_ref[...], (((0,), (0,)), ((), ())), preferred_element_type=F32)

        @pl.when(s == 0)
        def _():
            acc[...] = p

        @pl.when(s > 0)
        def _():
            acc[...] += p

        @pl.when(s == ns - 1)
        def _():
            o_ref[...] = acc[...]

    return pl.pallas_call(
        body, name=name, grid=(M // tm, N // tn, ns),
        in_specs=[pl.BlockSpec((ts, tm), lambda i, j, s: (s, i)), pl.BlockSpec((ts, tn), lambda i, j, s: (s, j))],
        out_specs=pl.BlockSpec((tm, tn), lambda i, j, s: (i, j)),
        out_shape=jax.ShapeDtypeStruct((M, N), F32),
        scratch_shapes=[pltpu.VMEM((tm, tn), F32)],
        compiler_params=_params(("parallel", "parallel", "arbitrary")),
    )(a, b)


def _dh_normbwd(pairs, x, ng, sc, dx_next, *, name):
    S = x.shape[0]
    tm = _row_tile(S, 256)
    n_i = S // tm
    n_p = len(pairs)

    def body(*refs):
        a_refs, w_refs = refs[:n_p], refs[n_p:2 * n_p]
        x_ref, g_ref, sc_ref, dxn_ref, dx_ref, p_ref = refs[2 * n_p:]
        i = pl.program_id(0)
        dh = jnp.dot(a_refs[0][...], w_refs[0][...], preferred_element_type=F32)
        for k in range(1, n_p):
            dh = dh + jnp.dot(a_refs[k][...], w_refs[k][...], preferred_element_type=F32)
        xv = x_ref[...]
        r = _rms(xv)
        xh = xv * r
        g = g_ref[...]
        dn = dh * (1.0 + sc_ref[...])
        dy = dn * g
        dx_ref[...] = dxn_ref[...] + r * (dy - xh * jnp.mean(dy * xh, axis=-1, keepdims=True))

        @pl.when(i == 0)
        def _():
            p_ref[...] = jnp.zeros_like(p_ref)

        p_ref[:, 0:D] += _rsum8(dh * (xh * g))
        p_ref[:, D:2 * D] += _rsum8(dh)
        p_ref[:, 2 * D:3 * D] += _rsum8(dn * xh)

        @pl.when(i == n_i - 1)
        def _():
            _all_rows(p_ref)

    row = pl.BlockSpec((tm, D), lambda i: (i, 0))
    vec = pl.BlockSpec((1, D), lambda i: (0, 0))
    in_specs = ([pl.BlockSpec((tm, a.shape[1]), lambda i: (i, 0)) for a, _ in pairs]
                + [pl.BlockSpec(w.shape, lambda i: (0, 0)) for _, w in pairs] + [row, vec, vec, row])
    return pl.pallas_call(
        body, name=name, grid=(n_i,), in_specs=in_specs,
        out_specs=[row, pl.BlockSpec((8, 3 * D), lambda i: (0, 0))],
        out_shape=[jax.ShapeDtypeStruct((S, D), F32), jax.ShapeDtypeStruct((8, 3 * D), F32)],
        compiler_params=_params(("arbitrary",)),
    )(*[a for a, _ in pairs], *[w for _, w in pairs], x, ng, sc, dx_next)


def _final_loss(x3, gfin, tgt):
    S = x3.shape[0]
    tm = _row_tile(S, 512)
    n_i = S // tm

    def body(x_ref, g_ref, t_ref, dx_ref, dg_ref, loss_ref, lacc):
        i = pl.program_id(0)
        xv = x_ref[...]
        r = _rms(xv)
        xh = xv * r
        g = g_ref[...]
        e = xh * g - t_ref[...]
        dout = e * (1.0 / D)
        dy = dout * g
        dx_ref[...] = r * (dy - xh * jnp.mean(dy * xh, axis=-1, keepdims=True))

        @pl.when(i == 0)
        def _():
            dg_ref[...] = jnp.zeros_like(dg_ref)
            lacc[...] = jnp.zeros_like(lacc)

        dg_ref[...] += _rsum8(dout * xh)
        lacc[...] += _rsum8(e * e)

        @pl.when(i == n_i - 1)
        def _():
            _all_rows(dg_ref)
            tot = jnp.sum(jnp.sum(lacc[...], axis=0, keepdims=True), axis=1, keepdims=True)
            loss_ref[...] = jnp.broadcast_to(tot * (0.5 / D), loss_ref.shape)

    row = pl.BlockSpec((tm, D), lambda i: (i, 0))
    return pl.pallas_call(
        body, name="final_loss", grid=(n_i,),
        in_specs=[row, pl.BlockSpec((1, D), lambda i: (0, 0)), row],
        out_specs=[row, pl.BlockSpec((8, D), lambda i: (0, 0)), pl.BlockSpec((8, 128), lambda i: (0, 0))],
        out_shape=[jax.ShapeDtypeStruct((S, D), F32), jax.ShapeDtypeStruct((8, D), F32),
                   jax.ShapeDtypeStruct((8, 128), F32)],
        scratch_shapes=[pltpu.VMEM((8, D), F32)],
        compiler_params=_params(("arbitrary",)),
    )(x3, gfin, tgt)


def _mix_in(x, ng, sh, sc, w_in):
    S = x.shape[0]
    tm = _row_tile(S, 512)

    def body(x_ref, g_ref, sh_ref, sc_ref, w_ref, h_ref, z_ref):
        xv = x_ref[...]
        hb = (((xv * _rms(xv)) * g_ref[...]) * (1.0 + sc_ref[...]) + sh_ref[...]).astype(BF16)
        h_ref[...] = hb
        z_ref[...] = jnp.dot(hb, w_ref[...], preferred_element_type=F32)

    row = pl.BlockSpec((tm, D), lambda i: (i, 0))
    vec = pl.BlockSpec((1, D), lambda i: (0, 0))
    return pl.pallas_call(
        body, name="mix_in", grid=(S // tm,),
        in_specs=[row, vec, vec, vec, pl.BlockSpec((D, Z_COLS), lambda i: (0, 0))],
        out_specs=[row, pl.BlockSpec((tm, Z_COLS), lambda i: (i, 0))],
        out_shape=[jax.ShapeDtypeStruct((S, D), BF16), jax.ShapeDtypeStruct((S, Z_COLS), F32)],
        compiler_params=_params(("parallel",)),
    )(x, ng, sh, sc, w_in)


def _conv_taps(u, halo, rows):
    u1 = jnp.where(rows == 0, halo[7:8, :], pltpu.roll(u, 1, 0))
    u2 = jnp.where(rows == 0, halo[6:7, :], jnp.where(rows == 1, halo[7:8, :], pltpu.roll(u, 2, 0)))
    return u1, u2


def _mix_mid(z, conv_w, gq, gkv, wuq, wukv, cs, sn):
    S = z.shape[0]
    tm = _row_tile(S, 512)
    hb = tm // 8

    def body(z_ref, zh_ref, cw_ref, gq_ref, gkv_ref, wuq_ref, wukv_ref, cs_ref, sn_ref,
             ya_ref, q_ref, k_ref, v_ref, cqn_ref, ckvn_ref):
        i = pl.program_id(0)
        xb = z_ref[:, 0:CONV_W]
        u = z_ref[:, CONV_W:2 * CONV_W] * z_ref[:, 2 * CONV_W:3 * CONV_W]
        halo = zh_ref[:, CONV_W:2 * CONV_W] * zh_ref[:, 2 * CONV_W:3 * CONV_W]
        halo = jnp.where(i > 0, halo, 0.0)
        rows = lax.broadcasted_iota(jnp.int32, (tm, CONV_W), 0)
        u1, u2 = _conv_taps(u, halo, rows)
        y = cw_ref[0:1, :] * u2 + cw_ref[1:2, :] * u1 + cw_ref[2:3, :] * u
        ya_ref[...] = xb * y

        lane = lax.broadcasted_iota(jnp.int32, (tm, 128), 1)
        cs_v, sn_v = cs_ref[...], sn_ref[...]
        cq = z_ref[:, 3 * CONV_W:3 * CONV_W + Q_LORA]
        cqn = ((cq * _rms(cq)) * gq_ref[...]).astype(BF16)
        cqn_ref[...] = cqn
        q = jnp.dot(cqn, wuq_ref[...], preferred_element_type=F32)
        for h in range(HEADS):
            o = h * HEAD_SLOT
            q_ref[:, o:o + 128] = q[:, o:o + 128].astype(BF16)
            q_ref[:, o + 128:o + 256] = _rope(q[:, o + 128:o + 256], cs_v, sn_v, lane).astype(BF16)

        c0 = 3 * CONV_W + Q_LORA
        ckv = z_ref[:, c0:c0 + KV_LORA]
        ckvn = ((ckv * _rms(ckv)) * gkv_ref[...]).astype(BF16)
        ckvn_ref[...] = ckvn
        kv = jnp.dot(ckvn, wukv_ref[...], preferred_element_type=F32)
        krot = _rope(z_ref[:, c0 + KV_LORA:Z_COLS], cs_v, sn_v, lane).astype(BF16)
        for h in range(HEADS):
            o = h * HEAD_SLOT
            k_ref[:, o:o + 128] = kv[:, h * 128:(h + 1) * 128].astype(BF16)
            k_ref[:, o + 128:o + 256] = krot
        v_ref[...] = kv[:, HEADS * 128:].astype(BF16)

    def rows_of(n):
        return pl.BlockSpec((tm, n), lambda i: (i, 0))

    def whole(shape):
        return pl.BlockSpec(shape, lambda i: (0, 0))

    return pl.pallas_call(
        body, name="mix_mid", grid=(S // tm,),
        in_specs=[rows_of(Z_COLS), pl.BlockSpec((8, Z_COLS), lambda i: (jnp.maximum(i * hb - 1, 0), 0)),
                  whole((8, CONV_W)), whole((1, Q_LORA)), whole((1, KV_LORA)),
                  whole((Q_LORA, HEADS * HEAD_SLOT)), whole((KV_LORA, 2 * HEADS * 128)),
                  rows_of(128), rows_of(128)],
        out_specs=[rows_of(CONV_W), rows_of(HEADS * HEAD_SLOT), rows_of(HEADS * HEAD_SLOT), rows_of(HEADS * V_HEAD),
                   rows_of(Q_LORA), rows_of(KV_LORA)],
        out_shape=[jax.ShapeDtypeStruct((S, CONV_W), F32), jax.ShapeDtypeStruct((S, HEADS * HEAD_SLOT), BF16),
                   jax.ShapeDtypeStruct((S, HEADS * HEAD_SLOT), BF16), jax.ShapeDtypeStruct((S, HEADS * V_HEAD), BF16),
                   jax.ShapeDtypeStruct((S, Q_LORA), BF16), jax.ShapeDtypeStruct((S, KV_LORA), BF16)],
        compiler_params=_params(("parallel",)),
    )(z, z, conv_w, gq, gkv, wuq, wukv, cs, sn)


def _att_block(S):
    return min(512, max(S // 2, 128))


def _pair_tables(n, k_major):
    if k_major:
        pairs = [(qi, ki) for ki in range(n) for qi in range(ki, n)]
    else:
        pairs = [(qi, ki) for qi in range(n) for ki in range(qi + 1)]
    return (jnp.asarray(np.array([p[0] for p in pairs], np.int32)),
            jnp.asarray(np.array([p[1] for p in pairs], np.int32)), len(pairs))


def _chunk_mask(bq):
    r = lax.broadcasted_iota(jnp.int32, (bq, bq), 0) // CHUNK
    c = lax.broadcasted_iota(jnp.int32, (bq, bq), 1) // CHUNK
    return c <= r


def _attention(q, k, v):
    S = q.shape[0]
    bq = _att_block(S)
    n = S // bq
    qi_t, ki_t, n_pairs = _pair_tables(n, k_major=False)

    def body(qi_ref, ki_ref, q_ref, k_ref, v_ref, o_ref, lse_ref, m_s, l_s, acc_s):
        p_id = pl.program_id(1)
        qi, ki = qi_ref[p_id], ki_ref[p_id]

        @pl.when(ki == 0)
        def _():
            m_s[...] = jnp.full_like(m_s, NEG)
            l_s[...] = jnp.zeros_like(l_s)
            acc_s[...] = jnp.zeros_like(acc_s)

        s = lax.dot_general(q_ref[...], k_ref[...], (((1,), (1,)), ((), ())),
                            preferred_element_type=F32) * ATT_SCALE

        def update(sv):
            m_prev = m_s[...]
            m_new = jnp.maximum(m_prev, jnp.max(sv, axis=1, keepdims=True))
            alpha = jnp.exp(m_prev - m_new)
            p = jnp.exp(sv - m_new)
            l_s[...] = alpha * l_s[...] + jnp.sum(p, axis=1, keepdims=True)
            acc_s[...] = alpha * acc_s[...] + jnp.dot(p.astype(BF16), v_ref[...], preferred_element_type=F32)
            m_s[...] = m_new

        @pl.when(ki < qi)
        def _():
            update(s)

        @pl.when(ki == qi)
        def _():
            update(jnp.where(_chunk_mask(bq), s, NEG))
            l = l_s[...]
            o_ref[...] = acc_s[...] / l
            lse_ref[...] = jnp.broadcast_to(m_s[...] + jnp.log(l), lse_ref.shape)

    grid_spec = pltpu.PrefetchScalarGridSpec(
        num_scalar_prefetch=2, grid=(HEADS, n_pairs),
        in_specs=[pl.BlockSpec((bq, HEAD_SLOT), lambda h, p, qt, kt: (qt[p], h)),
                  pl.BlockSpec((bq, HEAD_SLOT), lambda h, p, qt, kt: (kt[p], h)),
                  pl.BlockSpec((bq, V_HEAD), lambda h, p, qt, kt: (kt[p], h))],
        out_specs=[pl.BlockSpec((bq, V_HEAD), lambda h, p, qt, kt: (qt[p], h))] * 2,
        scratch_shapes=[pltpu.VMEM((bq, 1), F32), pltpu.VMEM((bq, 1), F32), pltpu.VMEM((bq, V_HEAD), F32)])
    return pl.pallas_call(
        body, name="attention", grid_spec=grid_spec,
        out_shape=[jax.ShapeDtypeStruct((S, HEADS * V_HEAD), F32)] * 2,
        compiler_params=_params(("parallel", "arbitrary")),
    )(qi_t, ki_t, q, k, v)


def _attention_bwd(q, k, v, do, lse, delta):
    S = q.shape[0]
    bq = _att_block(S)
    n = S // bq
    qi_t, ki_t, n_pairs = _pair_tables(n, k_major=True)

    def body(qi_ref, ki_ref, q_ref, k_ref, v_ref, do_ref, lse_ref, dl_ref, dq_ref, dk_ref, dv_ref, dk_s, dv_s):
        p_id = pl.program_id(1)
        qi, ki = qi_ref[p_id], ki_ref[p_id]

        @pl.when(qi == ki)
        def _():
            dk_s[...] = jnp.zeros_like(dk_s)
            dv_s[...] = jnp.zeros_like(dv_s)

        qv, kv, dov = q_ref[...], k_ref[...], do_ref[...]
        s = lax.dot_general(qv, kv, (((1,), (1,)), ((), ())), preferred_element_type=F32) * ATT_SCALE
        dp = lax.dot_general(dov, v_ref[...], (((1,), (1,)), ((), ())), preferred_element_type=F32)
        rows = pl.ds(pl.multiple_of(qi * bq, bq), bq)

        def update(sv):
            p = jnp.exp(sv - lse_ref[:, 0:1])
            dv_s[...] += lax.dot_general(p.astype(BF16), dov, (((0,), (0,)), ((), ())), preferred_element_type=F32)
            ds = (p * (dp - dl_ref[:, 0:1]) * ATT_SCALE).astype(BF16)
            dk_s[...] += lax.dot_general(ds, qv, (((0,), (0,)), ((), ())), preferred_element_type=F32)
            dq = jnp.dot(ds, kv, preferred_element_type=F32)

            @pl.when(ki == 0)
            def _():
                dq_ref[rows, :] = dq

            @pl.when(ki > 0)
            def _():
                dq_ref[rows, :] += dq

        @pl.when(qi > ki)
        def _():
            update(s)

        @pl.when(qi == ki)
        def _():
            update(jnp.where(_chunk_mask(bq), s, NEG))

        @pl.when(qi == n - 1)
        def _():
            dk_ref[...] = dk_s[...]
            dv_ref[...] = dv_s[...]

    grid_spec = pltpu.PrefetchScalarGridSpec(
        num_scalar_prefetch=2, grid=(HEADS, n_pairs),
        in_specs=[pl.BlockSpec((bq, HEAD_SLOT), lambda h, p, qt, kt: (qt[p], h)),
                  pl.BlockSpec((bq, HEAD_SLOT), lambda h, p, qt, kt: (kt[p], h)),
                  pl.BlockSpec((bq, V_HEAD), lambda h, p, qt, kt: (kt[p], h)),
                  pl.BlockSpec((bq, V_HEAD), lambda h, p, qt, kt: (qt[p], h)),
                  pl.BlockSpec((bq, V_HEAD), lambda h, p, qt, kt: (qt[p], h)),
                  pl.BlockSpec((bq, V_HEAD), lambda h, p, qt, kt: (qt[p], h))],
        out_specs=[pl.BlockSpec((S, HEAD_SLOT), lambda h, p, qt, kt: (0, h)),
                   pl.BlockSpec((bq, HEAD_SLOT), lambda h, p, qt, kt: (kt[p], h)),
                   pl.BlockSpec((bq, V_HEAD), lambda h, p, qt, kt: (kt[p], h))],
        scratch_shapes=[pltpu.VMEM((bq, HEAD_SLOT), F32), pltpu.VMEM((bq, V_HEAD), F32)])
    return pl.pallas_call(
        body, name="attention_bwd", grid_spec=grid_spec,
        out_shape=[jax.ShapeDtypeStruct((S, HEADS * HEAD_SLOT), F32), jax.ShapeDtypeStruct((S, HEADS * HEAD_SLOT), F32),
                   jax.ShapeDtypeStruct((S, HEADS * V_HEAD), F32)],
        compiler_params=_params(("parallel", "arbitrary")),
    )(qi_t, ki_t, q, k, v, do, lse, delta)


def _group_mats():
    def blockdiag(n, g):
        idx = np.arange(n) // g
        return jnp.asarray((idx[:, None] == idx[None, :]).astype(np.float32), dtype=BF16)
    return blockdiag(CONV_W, CONV_GROUP), blockdiag(HEADS * V_HEAD, V_HEAD)


def _mix_out(ya, o, gout, w_out, x, gate, ga, gb):
    S = x.shape[0]
    tm = _row_tile(S, 512)

    def body(ya_ref, o_ref, go_ref, w_ref, x_ref, g_ref, ga_ref, gb_ref, xo_ref, yn_ref, yo_ref):
        yav, ov = ya_ref[...], o_ref[...]
        ra = lax.rsqrt(_gsum(yav * yav, ga_ref[...]) * (1.0 / CONV_GROUP) + EPS)
        rb = lax.rsqrt(_gsum(ov * ov, gb_ref[...]) * (1.0 / V_HEAD) + EPS)
        na = ((yav * ra) * go_ref[:, 0:CONV_W]).astype(BF16)
        nb = ((ov * rb) * go_ref[:, CONV_W:]).astype(BF16)
        yn_ref[:, 0:CONV_W] = na
        yn_ref[:, CONV_W:] = nb
        yo = (jnp.dot(na, w_ref[0:CONV_W, :], preferred_element_type=F32)
              + jnp.dot(nb, w_ref[CONV_W:, :], preferred_element_type=F32))
        xo_ref[...] = x_ref[...] + g_ref[...] * yo
        yo_ref[...] = yo.astype(BF16)

    row = pl.BlockSpec((tm, D), lambda i: (i, 0))
    half = pl.BlockSpec((tm, CONV_W), lambda i: (i, 0))
    vec = pl.BlockSpec((1, D), lambda i: (0, 0))
    sq = pl.BlockSpec((CONV_W, CONV_W), lambda i: (0, 0))
    return pl.pallas_call(
        body, name="mix_out", grid=(S // tm,),
        in_specs=[half, half, vec, pl.BlockSpec((D, D), lambda i: (0, 0)), row, vec, sq, sq],
        out_specs=[row, row, row],
        out_shape=[jax.ShapeDtypeStruct((S, D), F32), jax.ShapeDtypeStruct((S, D), BF16),
                   jax.ShapeDtypeStruct((S, D), BF16)],
        compiler_params=_params(("parallel",)),
    )(ya, o, gout, w_out, x, gate, ga, gb)


def _mix_out_bwd(dx, gate, yo, w_out_t, ya, o, gout, ga, gb):
    S = dx.shape[0]
    tm = _row_tile(S, 256)
    n_i = S // tm

    def norm_bwd(v, dn, gain, gmat, inv_n):
        r = lax.rsqrt(_gsum(v * v, gmat) * inv_n + EPS)
        vh = v * r
        dy = dn * gain
        return r * (dy - vh * (_gsum(dy * vh, gmat) * inv_n)), dn * vh

    def body(dx_ref, g_ref, yo_ref, w_ref, ya_ref, o_ref, go_ref, ga_ref, gb_ref,
             dyo_ref, dya_ref, do_ref, dl_ref, p_ref):
        i = pl.program_id(0)
        dxv = dx_ref[...]
        dyo = (dxv * g_ref[...]).astype(BF16)
        dyo_ref[...] = dyo
        dyn = jnp.dot(dyo, w_ref[...], preferred_element_type=F32)
        dya, dga = norm_bwd(ya_ref[...], dyn[:, 0:CONV_W], go_ref[:, 0:CONV_W], ga_ref[...], 1.0 / CONV_GROUP)
        ov = o_ref[...]
        do, dgb = norm_bwd(ov, dyn[:, CONV_W:], go_ref[:, CONV_W:], gb_ref[...], 1.0 / V_HEAD)
        dya_ref[...] = dya
        do_ref[...] = do.astype(BF16)
        dl_ref[...] = _gsum(do * ov, gb_ref[...])

        @pl.when(i == 0)
        def _():
            p_ref[...] = jnp.zeros_like(p_ref)

        p_ref[:, 0:D] += _rsum8(dxv * yo_ref[...].astype(F32))
        p_ref[:, D:D + CONV_W] += _rsum8(dga)
        p_ref[:, D + CONV_W:2 * D] += _rsum8(dgb)

        @pl.when(i == n_i - 1)
        def _():
            _all_rows(p_ref)

    row = pl.BlockSpec((tm, D), lambda i: (i, 0))
    half = pl.BlockSpec((tm, CONV_W), lambda i: (i, 0))
    vec = pl.BlockSpec((1, D), lambda i: (0, 0))
    sq = pl.BlockSpec((CONV_W, CONV_W), lambda i: (0, 0))
    return pl.pallas_call(
        body, name="mix_out_bwd", grid=(n_i,),
        in_specs=[row, vec, row, pl.BlockSpec((D, D), lambda i: (0, 0)), half, half, vec, sq, sq],
        out_specs=[row, half, half, half, pl.BlockSpec((8, 2 * D), lambda i: (0, 0))],
        out_shape=[jax.ShapeDtypeStruct((S, D), BF16), jax.ShapeDtypeStruct((S, CONV_W), F32),
                   jax.ShapeDtypeStruct((S, CONV_W), BF16), jax.ShapeDtypeStruct((S, CONV_W), F32),
                   jax.ShapeDtypeStruct((8, 2 * D), F32)],
        compiler_params=_params(("arbitrary",)),
    )(dx, gate, yo, w_out_t, ya, o, gout, ga, gb)


MID_SUMS = 3 * CONV_W + Q_LORA + KV_LORA


def _mix_mid_bwd(z, dya, conv_w, gq, gkv, wuq_t, wukv_t, cs, sn, dq, dk, dv):
    S = z.shape[0]
    tm = _row_tile(S, 256)
    n_i = S // tm
    hb = tm // 8
    last_blk = S // 8 - 1

    def latent_bwd(cv, dcn, gain):
        r = _rms(cv)
        ch = cv * r
        dy = dcn * gain
        return r * (dy - ch * jnp.mean(dy * ch, axis=-1, keepdims=True)), dcn * ch

    def body(z_ref, zp_ref, zn_ref, dya_ref, dyan_ref, cw_ref, gq_ref, gkv_ref, wuq_ref, wukv_ref, cs_ref, sn_ref,
             dq_ref, dk_ref, dv_ref, dz_ref, dqf_ref, dkvf_ref, p_ref):
        i = pl.program_id(0)
        xb, xc, xu = z_ref[:, 0:CONV_W], z_ref[:, CONV_W:2 * CONV_W], z_ref[:, 2 * CONV_W:3 * CONV_W]
        u = xc * xu
        halo = jnp.where(i > 0, zp_ref[:, CONV_W:2 * CONV_W] * zp_ref[:, 2 * CONV_W:3 * CONV_W], 0.0)
        rows = lax.broadcasted_iota(jnp.int32, (tm, CONV_W), 0)
        u1, u2 = _conv_taps(u, halo, rows)
        w0, w1, w2 = cw_ref[0:1, :], cw_ref[1:2, :], cw_ref[2:3, :]
        y = w0 * u2 + w1 * u1 + w2 * u
        dyav = dya_ref[...]
        dy = dyav * xb
        nxt = jnp.where(i < n_i - 1, dyan_ref[...] * zn_ref[:, 0:CONV_W], 0.0)
        dy1 = jnp.where(rows == tm - 1, nxt[0:1, :], pltpu.roll(dy, tm - 1, 0))
        dy2 = jnp.where(rows == tm - 1, nxt[1:2, :], jnp.where(rows == tm - 2, nxt[0:1, :], pltpu.roll(dy, tm - 2, 0)))
        du = w2 * dy + w1 * dy1 + w0 * dy2
        dz_ref[:, 0:CONV_W] = (dyav * y).astype(BF16)
        dz_ref[:, CONV_W:2 * CONV_W] = (du * xu).astype(BF16)
        dz_ref[:, 2 * CONV_W:3 * CONV_W] = (du * xc).astype(BF16)

        lane = lax.broadcasted_iota(jnp.int32, (tm, 128), 1)
        cs_v, sn_v = cs_ref[...], sn_ref[...]
        dkr = jnp.zeros((tm, 128), F32)
        for h in range(HEADS):
            o = h * HEAD_SLOT
            dqf_ref[:, o:o + 128] = dq_ref[:, o:o + 128].astype(BF16)
            dqf_ref[:, o + 128:o + 256] = _rope_t(dq_ref[:, o + 128:o + 256], cs_v, sn_v, lane).astype(BF16)
            dkvf_ref[:, h * 128:(h + 1) * 128] = dk_ref[:, o:o + 128].astype(BF16)
            dkr = dkr + dk_ref[:, o + 128:o + 256]
        dkvf_ref[:, HEADS * 128:] = dv_ref[...].astype(BF16)

        c0 = 3 * CONV_W
        dcqn = jnp.dot(dqf_ref[...], wuq_ref[...], preferred_element_type=F32)
        dcq, dgq = latent_bwd(z_ref[:, c0:c0 + Q_LORA], dcqn, gq_ref[...])
        dz_ref[:, c0:c0 + Q_LORA] = dcq.astype(BF16)
        c1 = c0 + Q_LORA
        dckvn = jnp.dot(dkvf_ref[...], wukv_ref[...], preferred_element_type=F32)
        dckv, dgkv = latent_bwd(z_ref[:, c1:c1 + KV_LORA], dckvn, gkv_ref[...])
        dz_ref[:, c1:c1 + KV_LORA] = dckv.astype(BF16)
        dz_ref[:, c1 + KV_LORA:Z_COLS] = _rope_t(dkr, cs_v, sn_v, lane).astype(BF16)

        @pl.when(i == 0)
        def _():
            p_ref[...] = jnp.zeros_like(p_ref)

        p_ref[:, 0:CONV_W] += _rsum8(dy * u2)
        p_ref[:, CONV_W:2 * CONV_W] += _rsum8(dy * u1)
        p_ref[:, 2 * CONV_W:3 * CONV_W] += _rsum8(dy * u)
        p_ref[:, c0:c0 + Q_LORA] += _rsum8(dgq)
        p_ref[:, c1:c1 + KV_LORA] += _rsum8(dgkv)

        @pl.when(i == n_i - 1)
        def _():
            _all_rows(p_ref)

    def rows_of(n):
        return pl.BlockSpec((tm, n), lambda i: (i, 0))

    def whole(shape):
        return pl.BlockSpec(shape, lambda i: (0, 0))

    def prev8(n):
        return pl.BlockSpec((8, n), lambda i: (jnp.maximum(i * hb - 1, 0), 0))

    def next8(n):
        return pl.BlockSpec((8, n), lambda i: (jnp.minimum((i + 1) * hb, last_blk), 0))

    return pl.pallas_call(
        body, name="mix_mid_bwd", grid=(n_i,),
        in_specs=[rows_of(Z_COLS), prev8(Z_COLS), next8(Z_COLS), rows_of(CONV_W), next8(CONV_W),
                  whole((8, CONV_W)), whole((1, Q_LORA)), whole((1, KV_LORA)),
                  whole((HEADS * HEAD_SLOT, Q_LORA)), whole((2 * HEADS * 128, KV_LORA)),
                  rows_of(128), rows_of(128),
                  rows_of(HEADS * HEAD_SLOT), rows_of(HEADS * HEAD_SLOT), rows_of(HEADS * V_HEAD)],
        out_specs=[rows_of(Z_COLS), rows_of(HEADS * HEAD_SLOT), rows_of(2 * HEADS * 128), whole((8, MID_SUMS))],
        out_shape=[jax.ShapeDtypeStruct((S, Z_COLS), BF16), jax.ShapeDtypeStruct((S, HEADS * HEAD_SLOT), BF16),
                   jax.ShapeDtypeStruct((S, 2 * HEADS * 128), BF16), jax.ShapeDtypeStruct((8, MID_SUMS), F32)],
        compiler_params=_params(("arbitrary",)),
    )(z, z, z, dya, dya, conv_w, gq, gkv, wuq_t, wukv_t, cs, sn, dq, dk, dv)


ADA_Q = N_MOD * D // N_CHIPS
ADA_TN = 768


def _ada_forward(c_all, ada_w_q, ada_b_q):
    def body(c_ref, w_ref, b_ref, o_ref):
        cv = c_ref[...]
        sc = (cv * jax.nn.sigmoid(cv)).astype(BF16)
        o_ref[...] = jnp.dot(sc, w_ref[...].astype(BF16), preferred_element_type=F32) + b_ref[...]

    return pl.pallas_call(
        body, name="ada_forward", grid=(ADA_Q // ADA_TN,),
        in_specs=[pl.BlockSpec((16, D), lambda j: (0, 0)), pl.BlockSpec((D, ADA_TN), lambda j: (0, j)),
                  pl.BlockSpec((1, ADA_TN), lambda j: (0, j))],
        out_specs=pl.BlockSpec((16, ADA_TN), lambda j: (0, j)),
        out_shape=jax.ShapeDtypeStruct((16, ADA_Q), F32),
        compiler_params=_params(("parallel",)),
    )(c_all, ada_w_q, ada_b_q)


def _ada_wgrad(c_all, dmod_q):
    def body(c_ref, d_ref, o_ref):
        cv = c_ref[...]
        sc = (cv * jax.nn.sigmoid(cv)).astype(BF16)
        o_ref[...] = lax.dot_general(sc, d_ref[...].astype(BF16), (((0,), (0,)), ((), ())),
                                     preferred_element_type=F32)

    return pl.pallas_call(
        body, name="ada_wgrad", grid=(ADA_Q // ADA_TN,),
        in_specs=[pl.BlockSpec((16, D), lambda j: (0, 0)), pl.BlockSpec((16, ADA_TN), lambda j: (0, j))],
        out_specs=pl.BlockSpec((D, ADA_TN), lambda j: (0, j)),
        out_shape=jax.ShapeDtypeStruct((D, ADA_Q), F32),
        compiler_params=_params(("parallel",)),
    )(c_all, dmod_q)


def _sum_devices(parts):
    n = parts.shape[1]

    def body(p_ref, o_ref):
        o_ref[...] = jnp.broadcast_to(jnp.sum(p_ref[...], axis=0, keepdims=True), o_ref.shape)

    return pl.pallas_call(
        body, name="sum_devices",
        in_specs=[pl.BlockSpec((N_DEV, n), lambda: (0, 0))], out_specs=pl.BlockSpec((N_DEV, n), lambda: (0, 0)),
        out_shape=jax.ShapeDtypeStruct((N_DEV, n), F32),
    )(parts)


def _adamw(w, g, m, v, *, name):
    rows, cols = w.shape
    tr = _row_tile(rows, 256)

    def body(w_ref, g_ref, m_ref, v_ref, d_ref, mo_ref, vo_ref):
        gv = g_ref[...]
        mn = B1 * m_ref[...] + (1.0 - B1) * gv
        vn = B2 * v_ref[...] + (1.0 - B2) * (gv * gv)
        m_hat = mn / (1.0 - B1 ** STEP)
        v_hat = vn / (1.0 - B2 ** STEP)
        d_ref[...] = -LR * (m_hat / (jnp.sqrt(v_hat) + AEPS) + WD * w_ref[...])
        mo_ref[...] = mn
        vo_ref[...] = vn

    blk = pl.BlockSpec((tr, cols), lambda i: (i, 0))
    return pl.pallas_call(
        body, name=name, grid=(rows // tr,), in_specs=[blk] * 4, out_specs=[blk] * 3,
        out_shape=[jax.ShapeDtypeStruct((rows, cols), F32)] * 3,
        compiler_params=_params(("parallel",)),
    )(w, g, m, v)


def _place():
    return lax.axis_index("x"), lax.axis_index("y"), lax.axis_index("c")


def _small_allgather(v, *, name):
    m, n = v.shape

    def body(x_ref, out_ref, send_sems, recv_sems, local_sem):
        x, y, c = _place()
        me, sibling = (x, y, c), (x, y, 1 - c)
        chips = [(1 - x, y), (x, 1 - y), (1 - x, 1 - y)]

        def rows(px, py, pc):
            return out_ref.at[pl.ds((4 * px + 2 * py + pc) * m, m), :]

        def copy(k, block, to, src=None):
            return pltpu.make_async_remote_copy(
                src_ref=rows(*block) if src is None else src, dst_ref=rows(*block),
                send_sem=send_sems.at[k], recv_sem=recv_sems.at[k], device_id=to, device_id_type=MESH)

        mine = pltpu.make_async_copy(x_ref, rows(*me), local_sem)
        mine.start()
        first = [copy(0, me, sibling, src=x_ref)]
        first += [copy(1 + j, me, (*chip, c), src=x_ref) for j, chip in enumerate(chips)]
        for cp in first:
            cp.start()
        passed = [copy(4 + j, (*chip, c), sibling) for j, chip in enumerate(chips)]
        for j, chip in enumerate(chips):
            copy(1 + j, (*chip, c), me).wait_recv()
            passed[j].start()
        copy(0, sibling, me).wait_recv()
        for j, chip in enumerate(chips):
            copy(4 + j, (*chip, 1 - c), me).wait_recv()
        for cp in first + passed:
            cp.wait_send()
        mine.wait()

    return pl.pallas_call(
        body, name=name,
        out_shape=jax.ShapeDtypeStruct((N_DEV * m, n), v.dtype),
        in_specs=[pl.BlockSpec(memory_space=pltpu.VMEM)], out_specs=pl.BlockSpec(memory_space=pltpu.VMEM),
        scratch_shapes=[pltpu.SemaphoreType.DMA((7,)), pltpu.SemaphoreType.DMA((7,)), pltpu.SemaphoreType.DMA],
    )(v)


def _weights_allgather(slab):
    R, n = slab.shape
    H = R // 2

    def body(x_ref, out_ref, send_sems, recv_sems, local_sem):
        x, y, c = _place()
        sibling = (x, y, 1 - c)
        chips = [(1 - x, y), (x, 1 - y), (1 - x, 1 - y)]

        def half(chip, hc):
            return out_ref.at[2 * chip[0] + chip[1], pl.ds(hc * H, H), :]

        def copy(k, chip, hc, to, src=None):
            return pltpu.make_async_remote_copy(
                src_ref=half(chip, hc) if src is None else src, dst_ref=half(chip, hc),
                send_sem=send_sems.at[k], recv_sem=recv_sems.at[k], device_id=to, device_id_type=MESH)

        mine = pltpu.make_async_copy(x_ref, out_ref.at[2 * x + y], local_sem)
        mine.start()
        first = [copy(j, (x, y), c, (*chip, c), src=x_ref.at[pl.ds(c * H, H), :]) for j, chip in enumerate(chips)]
        for cp in first:
            cp.start()
        passed = [copy(3 + j, chip, c, sibling) for j, chip in enumerate(chips)]
        for j, chip in enumerate(chips):
            copy(j, chip, c, (x, y, c)).wait_recv()
            passed[j].start()
        for j, chip in enumerate(chips):
            copy(3 + j, chip, 1 - c, (x, y, c)).wait_recv()
        for cp in first + passed:
            cp.wait_send()
        mine.wait()

    return pl.pallas_call(
        body, name="weights_allgather",
        out_shape=jax.ShapeDtypeStruct((N_CHIPS, R, n), slab.dtype),
        in_specs=[pl.BlockSpec(memory_space=pl.ANY)], out_specs=pl.BlockSpec(memory_space=pl.ANY),
        scratch_shapes=[pltpu.SemaphoreType.DMA((6,)), pltpu.SemaphoreType.DMA((6,)), pltpu.SemaphoreType.DMA],
    )(slab)


def _pair_exchange(g):
    _, R, n = g.shape
    H = R // 2

    def body(g_ref, t_ref, send_sem, recv_sem):
        x, y, c = _place()
        cp = pltpu.make_async_remote_copy(
            src_ref=g_ref.at[:, pl.ds((1 - c) * H, H), :], dst_ref=t_ref,
            send_sem=send_sem, recv_sem=recv_sem, device_id=(x, y, 1 - c), device_id_type=MESH)
        cp.start()
        cp.wait()

    return pl.pallas_call(
        body, name="grads_pair_exchange",
        out_shape=jax.ShapeDtypeStruct((N_CHIPS, H, n), g.dtype),
        in_specs=[pl.BlockSpec(memory_space=pl.ANY)], out_specs=pl.BlockSpec(memory_space=pl.ANY),
        scratch_shapes=[pltpu.SemaphoreType.DMA, pltpu.SemaphoreType.DMA],
    )(g)


def _pair_add(place, g, t):
    _, R, n = g.shape
    H = R // 2
    tr = _row_tile(H, 288)
    nb = H // tr

    def body(pl_ref, g_ref, t_ref, pf_ref, pb_ref):
        s = g_ref[...] + t_ref[...]
        pf_ref[...] = s
        pb_ref[...] = s.astype(BF16)

    grid_spec = pltpu.PrefetchScalarGridSpec(
        num_scalar_prefetch=1, grid=(N_CHIPS, nb),
        in_specs=[pl.BlockSpec((1, tr, n), lambda q, r, p: (q, p[0] * nb + r, 0)),
                  pl.BlockSpec((1, tr, n), lambda q, r, p: (q, r, 0))],
        out_specs=[pl.BlockSpec((1, tr, n), lambda q, r, p: (q, r, 0))] * 2)
    return pl.pallas_call(
        body, name="grads_pair_add", grid_spec=grid_spec,
        out_shape=[jax.ShapeDtypeStruct((N_CHIPS, H, n), F32), jax.ShapeDtypeStruct((N_CHIPS, H, n), BF16)],
        compiler_params=_params(("parallel", "parallel")),
    )(place, g, t)


def _chip_exchange(pb):
    _, H, n = pb.shape

    def body(p_ref, t_ref, send_sems, recv_sems, local_sem):
        x, y, c = _place()
        myq = 2 * x + y
        chips = [(1 - x, y), (x, 1 - y), (1 - x, 1 - y)]
        mine = pltpu.make_async_copy(p_ref.at[myq], t_ref.at[myq], local_sem)
        mine.start()
        sends = [pltpu.make_async_remote_copy(
            src_ref=p_ref.at[2 * chip[0] + chip[1]], dst_ref=t_ref.at[myq],
            send_sem=send_sems.at[j], recv_sem=recv_sems.at[j], device_id=(*chip, c), device_id_type=MESH)
            for j, chip in enumerate(chips)]
        for cp in sends:
            cp.start()
        for j, chip in enumerate(chips):
            slot = t_ref.at[2 * chip[0] + chip[1]]
            pltpu.make_async_remote_copy(
                src_ref=slot, dst_ref=slot, send_sem=send_sems.at[j], recv_sem=recv_sems.at[j],
                device_id=(*chip, c), device_id_type=MESH).wait_recv()
        for cp in sends:
            cp.wait_send()
        mine.wait()

    return pl.pallas_call(
        body, name="grads_chip_exchange",
        out_shape=jax.ShapeDtypeStruct(pb.shape, pb.dtype),
        in_specs=[pl.BlockSpec(memory_space=pl.ANY)], out_specs=pl.BlockSpec(memory_space=pl.ANY),
        scratch_shapes=[pltpu.SemaphoreType.DMA((3,)), pltpu.SemaphoreType.DMA((3,)), pltpu.SemaphoreType.DMA],
    )(pb)


def _chip_add(place, pf, t):
    _, H, n = pf.shape
    tr = _row_tile(H, 288)

    def body(pl_ref, pf_ref, t_ref, o_ref):
        myq = pl_ref[1]
        own = pf_ref[0]
        acc = jnp.where(myq == 0, own, t_ref[0].astype(F32))
        for q in range(1, N_CHIPS):
            acc = acc + jnp.where(myq == q, own, t_ref[q].astype(F32))
        o_ref[...] = acc

    grid_spec = pltpu.PrefetchScalarGridSpec(
        num_scalar_prefetch=1, grid=(H // tr,),
        in_specs=[pl.BlockSpec((1, tr, n), lambda r, p: (p[1], r, 0)),
                  pl.BlockSpec((N_CHIPS, tr, n), lambda r, p: (0, r, 0))],
        out_specs=pl.BlockSpec((tr, n), lambda r, p: (r, 0)))
    return pl.pallas_call(
        body, name="grads_chip_add", grid_spec=grid_spec,
        out_shape=jax.ShapeDtypeStruct((H, n), F32),
        compiler_params=_params(("parallel",)),
    )(place, pf, t)


def _pair_share(rh):
    H, n = rh.shape

    def body(r_ref, o_ref, send_sem, recv_sem, local_sem):
        x, y, c = _place()
        mine = pltpu.make_async_copy(r_ref, o_ref.at[pl.ds(c * H, H), :], local_sem)
        mine.start()
        cp = pltpu.make_async_remote_copy(
            src_ref=r_ref, dst_ref=o_ref.at[pl.ds(c * H, H), :],
            send_sem=send_sem, recv_sem=recv_sem, device_id=(x, y, 1 - c), device_id_type=MESH)
        cp.start()
        other = o_ref.at[pl.ds((1 - c) * H, H), :]
        pltpu.make_async_remote_copy(
            src_ref=other, dst_ref=other, send_sem=send_sem, recv_sem=recv_sem,
            device_id=(x, y, 1 - c), device_id_type=MESH).wait_recv()
        cp.wait_send()
        mine.wait()

    return pl.pallas_call(
        body, name="grads_pair_share",
        out_shape=jax.ShapeDtypeStruct((2 * H, n), rh.dtype),
        in_specs=[pl.BlockSpec(memory_space=pl.ANY)], out_specs=pl.BlockSpec(memory_space=pl.ANY),
        scratch_shapes=[pltpu.SemaphoreType.DMA, pltpu.SemaphoreType.DMA, pltpu.SemaphoreType.DMA],
    )(rh)


SLAB_W = 1024
BULK = [("ffn1_w1", D, FF // 4, True), ("ffn1_w3", D, FF // 4, True), ("ffn1_w2", FF // 4, D, False),
        ("w_in", D, IN_COLS // 4, True), ("w_uq", Q_LORA, 768 // 4, True), ("w_ukv", KV_LORA, 1024 // 4, True),
        ("w_out", D // 4, D, False),
        ("ffn2_w1", D, FF // 4, True), ("ffn2_w3", D, FF // 4, True), ("ffn2_w2", FF // 4, D, False)]


def _slab_rows(k, n):
    rows = k * n // SLAB_W
    return rows, rows + (-rows) % 16


SLAB_R = sum(_slab_rows(k, n)[1] for _, k, n, _ in BULK)


def _pack_shards(parts, lead):
    out = []
    for p, (_, k, n, _) in zip(parts, BULK):
        rows, padded = _slab_rows(k, n)
        r = p.reshape(*lead, rows, SLAB_W)
        if padded != rows:
            r = jnp.pad(r, [(0, 0)] * len(lead) + [(0, padded - rows), (0, 0)])
        out.append(r)
    return jnp.concatenate(out, axis=len(lead))


def _unpack_slab(slab, lead):
    out, r0 = {}, 0
    for name, k, n, _ in BULK:
        rows, padded = _slab_rows(k, n)
        out[name] = lax.slice_in_dim(slab, r0, r0 + rows, axis=len(lead)).reshape(*lead, k, n)
        r0 += padded
    return out


def _full_weight(parts, by_cols):
    if by_cols:
        return jnp.transpose(parts, (1, 0, 2)).reshape(parts.shape[1], -1)
    return parts.reshape(-1, parts.shape[2])


def _quarters(g, by_cols):
    if by_cols:
        k, n = g.shape
        return jnp.transpose(g.reshape(k, N_CHIPS, n // N_CHIPS), (1, 0, 2))
    return g.reshape(N_CHIPS, g.shape[0] // N_CHIPS, g.shape[1])


def _pad_heads(w_uq):
    w = w_uq.reshape(Q_LORA, HEADS, QK_NOPE + QK_ROPE)
    return jnp.pad(w, ((0, 0), (0, 0), (0, HEAD_SLOT - QK_NOPE - QK_ROPE))).reshape(Q_LORA, HEADS * HEAD_SLOT)


def _unpad_heads(g):
    return g.reshape(Q_LORA, HEADS, HEAD_SLOT)[:, :, :QK_NOPE + QK_ROPE].reshape(Q_LORA, HEADS * (QK_NOPE + QK_ROPE))


def _split_kv(w_ukv):
    return jnp.transpose(w_ukv.reshape(KV_LORA, HEADS, 2, 128), (0, 2, 1, 3)).reshape(KV_LORA, 2 * HEADS * 128)


def _merge_kv(g):
    return jnp.transpose(g.reshape(KV_LORA, 2, HEADS, 128), (0, 2, 1, 3)).reshape(KV_LORA, 2 * HEADS * 128)


def _rope_tables(positions):
    inv_freq = ROPE_THETA ** (-jnp.arange(0, QK_ROPE, 2, dtype=F32) / QK_ROPE)
    ang = positions.astype(F32)[:, None] * inv_freq
    cos, sin, zero = jnp.cos(ang), jnp.sin(ang), jnp.zeros((positions.shape[0], 64), F32)
    return jnp.concatenate([cos, cos, zero], axis=1), jnp.concatenate([sin, sin, zero], axis=1)


def _local_step(x, positions, target, mod, vec, conv_w, w):
    row = lambda k: mod[k:k + 1]
    sh1, sc1, g1, sh2, sc2, g2, sh3, sc3, g3 = [row(k) for k in range(N_MOD)]
    cs, sn = _rope_tables(positions)
    cw8 = jnp.pad(conv_w, ((0, 5), (0, 0)))
    ga, gb = _group_mats()
    w_in = jnp.pad(w["w_in"], ((0, 0), (0, Z_COLS - IN_COLS)))
    wuq = _pad_heads(w["w_uq"])
    wukv = _split_kv(w["w_ukv"])

    h1, a1, b1, u1 = _ffn_up(x, vec["norm_ffn1_g"], sh1, sc1, w["ffn1_w1"], w["ffn1_w3"], name="ffn1_up")
    x1, f1 = _ffn_down(u1, w["ffn1_w2"], x, g1, name="ffn1_down")
    h2, z = _mix_in(x1, vec["norm_mix_g"], sh2, sc2, w_in)
    ya, q, k, v, cqn, ckvn = _mix_mid(z, cw8, vec["q_norm_g"], vec["kv_norm_g"], wuq, wukv, cs, sn)
    o, lse = _attention(q, k, v)
    x2, yn, yo = _mix_out(ya, o, vec["out_norm_g"], w["w_out"], x1, g2, ga, gb)
    h3, a3, b3, u3 = _ffn_up(x2, vec["norm_ffn2_g"], sh3, sc3, w["ffn2_w1"], w["ffn2_w3"], name="ffn2_up")
    x3, f3 = _ffn_down(u3, w["ffn2_w2"], x2, g3, name="ffn2_down")
    dx3, dgfin, loss_blk = _final_loss(x3, vec["final_norm_g"], target)

    grads = {}

    def ffn_bwd(tag, dx_out, x_in, h, a, b, u, f, gate, ng, sc, w1, w3, w2):
        df, da, db, dgate = _ffn_bwd_du(dx_out, gate, f, w2.T, a, b, name=tag + "_bwd_du")
        grads[tag + "_w2"] = _tn_matmul(u, df, tm=FF // 2, tn=D, name=tag + "_dw2")
        grads[tag + "_w1"] = _tn_matmul(h, da, tm=D, tn=FF // 2, name=tag + "_dw1")
        grads[tag + "_w3"] = _tn_matmul(h, db, tm=D, tn=FF // 2, name=tag + "_dw3")
        dx_in, sums = _dh_normbwd([(da, w1.T), (db, w3.T)], x_in, ng, sc, dx_out, name=tag + "_bwd_dh")
        return dx_in, dgate, sums

    dx2, dg3, s3 = ffn_bwd("ffn2", dx3, x2, h3, a3, b3, u3, f3, g3, vec["norm_ffn2_g"], sc3,
                           w["ffn2_w1"], w["ffn2_w3"], w["ffn2_w2"])
    dyo, dya, do, delta, s_out = _mix_out_bwd(dx2, g2, yo, w["w_out"].T, ya, o, vec["out_norm_g"], ga, gb)
    grads["w_out"] = _tn_matmul(yn, dyo, tm=D, tn=D, name="dw_out")
    dq, dk, dv = _attention_bwd(q, k, v, do, lse, delta)
    dz, dqf, dkvf, s_mid = _mix_mid_bwd(z, dya, cw8, vec["q_norm_g"], vec["kv_norm_g"], wuq.T, wukv.T, cs, sn,
                                        dq, dk, dv)
    grads["w_uq"] = _unpad_heads(_tn_matmul(cqn, dqf, tm=Q_LORA, tn=HEADS * HEAD_SLOT, name="dw_uq"))
    grads["w_ukv"] = _merge_kv(_tn_matmul(ckvn, dkvf, tm=KV_LORA, tn=2 * HEADS * 128, name="dw_ukv"))
    grads["w_in"] = _tn_matmul(h2, dz, tm=D, tn=Z_COLS // 2, name="dw_in")[:, :IN_COLS]
    dx1, s2 = _dh_normbwd([(dz, w_in.T)], x1, vec["norm_mix_g"], sc2, dx2, name="mix_bwd_dh")
    dx0, dg1, s1 = ffn_bwd("ffn1", dx1, x, h1, a1, b1, u1, f1, g1, vec["norm_ffn1_g"], sc1,
                           w["ffn1_w1"], w["ffn1_w3"], w["ffn1_w2"])

    def part(s, k):
        return s[0:1, k * D:(k + 1) * D]

    dmod = jnp.concatenate([part(s1, 1), part(s1, 0), dg1[0:1], part(s2, 1), part(s2, 0), part(s_out, 0),
                            part(s3, 1), part(s3, 0), dg3[0:1]], axis=1)
    small = {"norm_ffn1_g": part(s1, 2), "norm_mix_g": part(s2, 2), "out_norm_g": part(s_out, 1),
             "norm_ffn2_g": part(s3, 2), "final_norm_g": dgfin[0:1],
             "q_norm_g": s_mid[0:1, 3 * CONV_W:3 * CONV_W + Q_LORA],
             "kv_norm_g": s_mid[0:1, 3 * CONV_W + Q_LORA:MID_SUMS], "conv_w": s_mid[0:1, 0:3 * CONV_W]}
    return loss_blk, dx0, grads, dmod, small


SMALL = [("norm_ffn1_g", D), ("norm_mix_g", D), ("out_norm_g", D), ("norm_ffn2_g", D), ("final_norm_g", D),
         ("q_norm_g", Q_LORA), ("kv_norm_g", KV_LORA), ("conv_w", 3 * CONV_W)]
WEIGHTS = ['ada_w', 'ada_b', 'norm_ffn1_g', 'ffn1_w1', 'ffn1_w3', 'ffn1_w2', 'norm_mix_g', 'w_in', 'conv_w',
           'q_norm_g', 'w_uq', 'kv_norm_g', 'w_ukv', 'out_norm_g', 'w_out', 'norm_ffn2_g', 'ffn2_w1', 'ffn2_w3',
           'ffn2_w2', 'final_norm_g']


def kernel(x, c, positions, ada_w, ada_b, norm_ffn1_g, ffn1_w1, ffn1_w3, ffn1_w2, norm_mix_g, w_in, conv_w, q_norm_g, w_uq, kv_norm_g, w_ukv, out_norm_g, w_out, norm_ffn2_g, ffn2_w1, ffn2_w3, ffn2_w2, final_norm_g, loss_target, m_ada_w, m_ada_b, m_norm_ffn1_g, m_ffn1_w1, m_ffn1_w3, m_ffn1_w2, m_norm_mix_g, m_w_in, m_conv_w, m_q_norm_g, m_w_uq, m_kv_norm_g, m_w_ukv, m_out_norm_g, m_w_out, m_norm_ffn2_g, m_ffn2_w1, m_ffn2_w3, m_ffn2_w2, m_final_norm_g, v_ada_w, v_ada_b, v_norm_ffn1_g, v_ffn1_w1, v_ffn1_w3, v_ffn1_w2, v_norm_mix_g, v_w_in, v_conv_w, v_q_norm_g, v_w_uq, v_kv_norm_g, v_w_ukv, v_out_norm_g, v_w_out, v_norm_ffn2_g, v_ffn2_w1, v_ffn2_w3, v_ffn2_w2, v_final_norm_g):
    args = dict(locals())
    wts = {n: args[n] for n in WEIGHTS}
    mom = {n: args["m_" + n] for n in WEIGHTS}
    var = {n: args["v_" + n] for n in WEIGHTS}
    ax, ay, ac = _place()
    myq = 2 * ax + ay
    me = 2 * myq + ac
    place = jnp.stack([ac, myq]).astype(jnp.int32)

    shards = [wts[name][0].astype(BF16) for name, *_ in BULK]
    gathered = _unpack_slab(_weights_allgather(_pack_shards(shards, ())), (N_CHIPS,))
    full = {name: _full_weight(gathered[name], by_cols) for name, _, _, by_cols in BULK}

    mine = jnp.concatenate([c, conv_w[0].reshape(1, 3 * CONV_W // N_CHIPS)], axis=1)
    seen = _small_allgather(jnp.pad(mine, ((0, 7), (0, 0))), name="gather_cond").reshape(N_DEV, 8, -1)[:, 0]
    c_all = jnp.pad(seen[:, :D], ((0, 8), (0, 0)))
    conv_full = jnp.transpose(seen[0::2, D:].reshape(N_CHIPS, 3, CONV_W // N_CHIPS), (1, 0, 2)).reshape(3, CONV_W)
    ada_b_q = lax.dynamic_slice_in_dim(ada_b, myq * ADA_Q, ADA_Q, axis=1)
    mod_q = _ada_forward(c_all, ada_w[0], ada_b_q)
    mod_all = _small_allgather(mod_q, name="gather_mod").reshape(N_DEV, 16, ADA_Q)
    mod_rows = jnp.transpose(mod_all[0::2, :N_DEV], (1, 0, 2)).reshape(N_DEV, N_MOD * D)
    mod = lax.dynamic_slice_in_dim(mod_rows, me, 1, axis=0).reshape(N_MOD, D)

    vec = {n: wts[n] for n in ("norm_ffn1_g", "norm_mix_g", "q_norm_g", "kv_norm_g", "out_norm_g", "norm_ffn2_g")}
    vec["final_norm_g"] = final_norm_g.reshape(1, D)
    loss_blk, grad_x, grads, dmod, small = _local_step(x[0], positions[0], loss_target[0], mod, vec, conv_full, full)
    loss = lax.psum(loss_blk[0, 0], ("x", "y", "c"))

    slab = _pack_shards([_quarters(grads[name], by_cols) for name, _, _, by_cols in BULK], (N_CHIPS,))
    pf, pb = _pair_add(place, slab, _pair_exchange(slab))
    reduced = _pair_share(_chip_add(place, pf, _chip_exchange(pb)))
    gq = _unpack_slab(reduced, ())

    rows = jnp.concatenate([dmod] + [small[n] for n, _ in SMALL], axis=1)
    every = _small_allgather(jnp.pad(rows, ((0, 7), (0, 0))), name="gather_small").reshape(N_DEV, 8, -1)[:, 0]
    total = _sum_devices(every)[0:1]
    dmod_q = lax.dynamic_slice_in_dim(every[:, :N_MOD * D], myq * ADA_Q, ADA_Q, axis=1)
    g = {name: gq[name] for name, *_ in BULK}
    g["ada_w"] = _ada_wgrad(c_all, jnp.pad(dmod_q, ((0, 8), (0, 0))))
    g["ada_b"] = total[:, :N_MOD * D]
    off = N_MOD * D
    for n, width in SMALL:
        g[n] = total[:, off:off + width]
        off += width
    g["conv_w"] = lax.dynamic_slice_in_dim(g["conv_w"].reshape(3, CONV_W), myq * (CONV_W // N_CHIPS),
                                           CONV_W // N_CHIPS, axis=1)
    g["final_norm_g"] = g["final_norm_g"].reshape(D)

    delta, new_m, new_v = {}, {}, {}
    for name in ["ada_w"] + [b[0] for b in BULK]:
        shape = wts[name].shape
        delta[name], new_m[name], new_v[name] = [
            r.reshape(shape) for r in _adamw(wts[name][0], g[name], mom[name][0], var[name][0], name="adamw_" + name)]
        g[name] = g[name].reshape(shape)
    smalls = ["ada_b"] + [n for n, _ in SMALL]

    def packed(d):
        flat = jnp.concatenate([d[n].reshape(1, -1) for n in smalls], axis=1)
        return jnp.pad(flat.reshape(-1, D), ((0, 1), (0, 0)))

    res = _adamw(packed(wts), packed(g), packed(mom), packed(var), name="adamw_small")
    off = 0
    for n in smalls:
        size = wts[n].size
        for d, r in zip((delta, new_m, new_v), res):
            d[n] = r.reshape(-1)[off:off + size].reshape(wts[n].shape)
        g[n] = g[n].reshape(wts[n].shape)
        off += size

    return (loss, grad_x[None], *[g[n] for n in WEIGHTS], *[delta[n] for n in WEIGHTS],
            *[new_m[n] for n in WEIGHTS], *[new_v[n] for n in WEIGHTS])
```

```python
import functools

import numpy as np
import jax
import jax.numpy as jnp
from jax import lax
from jax.experimental import pallas as pl
from jax.experimental.pallas import tpu as pltpu

F32 = jnp.float32
BF16 = jnp.bfloat16
MESH = pl.DeviceIdType.MESH

D = 1024
FF = 2816
CONV_W = 512
CONV_GROUP = 64
HEADS = 4
QK_NOPE = 128
QK_ROPE = 64
V_HEAD = 128
Q_LORA = 384
KV_LORA = 256
HEAD_SLOT = 256
IN_COLS = 3 * CONV_W + Q_LORA + KV_LORA + QK_ROPE
Z_COLS = 2304
EPS = 1e-6
ROPE_THETA = 10000.0
CHUNK = 64
ATT_SCALE = (QK_NOPE + QK_ROPE) ** -0.5
NEG = -1e30
EXP2_SCALE = ATT_SCALE * 1.4426950408889634
N_MOD = 9

LR, B1, B2, AEPS, WD, STEP = 0.001, 0.9, 0.999, 1e-08, 0.01, 10

N_CHIPS = 4
N_DEV = 8
VMEM_LIMIT = 56 << 20


def _params(sem, vmem=VMEM_LIMIT):
    return pltpu.CompilerParams(dimension_semantics=sem, vmem_limit_bytes=vmem)


def _rms(v):
    return lax.rsqrt(jnp.mean(v * v, axis=-1, keepdims=True) + EPS)


def _rsum8(v):
    t, n = v.shape
    return jnp.sum(v.reshape(t // 8, 8, n), axis=0)


def _all_rows(ref):
    ref[...] = jnp.broadcast_to(jnp.sum(ref[...], axis=0, keepdims=True), ref.shape)


def _gsum(v, gmat):
    hi = v.astype(BF16)
    lo = (v - hi.astype(F32)).astype(BF16)
    return (jnp.dot(hi, gmat, preferred_element_type=F32)
            + jnp.dot(lo, gmat, preferred_element_type=F32))


def _dot_nt(a, b):
    return lax.dot_general(a, b, (((1,), (1,)), ((), ())), preferred_element_type=F32)


def _silu_parts(a):
    sg = jax.nn.sigmoid(a)
    return sg, a * sg


def _rope(xr, cs, sn, lane):
    rh = jnp.where(lane < 32, -pltpu.roll(xr, 96, 1), pltpu.roll(xr, 32, 1))
    return xr * cs + rh * sn


def _rope_t(g, cs, sn, lane):
    y = g * sn
    rt = jnp.where(lane < 32, pltpu.roll(y, 96, 1), jnp.where(lane < 64, -pltpu.roll(y, 32, 1), 0.0))
    return g * cs + rt


def _row_tile(rows, pref):
    t = min(rows, pref)
    while rows % t or t % 8:
        t -= 8
    return t


def _ffn_up(x, ng, sh, sc, w1, w3, *, name):
    S = x.shape[0]
    tm, tn = _row_tile(S, 512), FF // 2

    def body(x_ref, g_ref, sh_ref, sc_ref, w1_ref, w3_ref, h_ref, a_ref, b_ref, u_ref, hs):
        @pl.when(pl.program_id(1) == 0)
        def _():
            xv = x_ref[...]
            h = ((xv * _rms(xv)) * g_ref[...]) * (1.0 + sc_ref[...]) + sh_ref[...]
            hb = h.astype(BF16)
            hs[...] = hb
            h_ref[...] = hb

        h = hs[...]
        a = jnp.dot(h, w1_ref[...], preferred_element_type=F32)
        b = jnp.dot(h, w3_ref[...], preferred_element_type=F32)
        _, sa = _silu_parts(a)
        a_ref[...] = a.astype(BF16)
        b_ref[...] = b.astype(BF16)
        u_ref[...] = (sa * b).astype(BF16)

    row = pl.BlockSpec((tm, D), lambda i, j: (i, 0))
    vec = pl.BlockSpec((1, D), lambda i, j: (0, 0))
    wsp = pl.BlockSpec((D, tn), lambda i, j: (0, j))
    osp = pl.BlockSpec((tm, tn), lambda i, j: (i, j))
    return pl.pallas_call(
        body, name=name, grid=(S // tm, FF // tn),
        in_specs=[row, vec, vec, vec, wsp, wsp],
        out_specs=[row, osp, osp, osp],
        out_shape=[jax.ShapeDtypeStruct((S, D), BF16)] + [jax.ShapeDtypeStruct((S, FF), BF16)] * 3,
        scratch_shapes=[pltpu.VMEM((tm, D), BF16)],
        compiler_params=_params(("parallel", "arbitrary")),
    )(x, ng, sh, sc, w1, w3)


def _ffn_down(u, w2, x, gate, *, name):
    S = x.shape[0]
    tm = _row_tile(S, 512)

    def body(u_ref, w2_ref, x_ref, g_ref, xo_ref, f_ref):
        f = jnp.dot(u_ref[...], w2_ref[...], preferred_element_type=F32)
        xo_ref[...] = x_ref[...] + (0.5 * g_ref[...]) * f
        f_ref[...] = f.astype(BF16)

    return pl.pallas_call(
        body, name=name, grid=(S // tm,),
        in_specs=[pl.BlockSpec((tm, FF), lambda i: (i, 0)), pl.BlockSpec((FF, D), lambda i: (0, 0)),
                  pl.BlockSpec((tm, D), lambda i: (i, 0)), pl.BlockSpec((1, D), lambda i: (0, 0))],
        out_specs=[pl.BlockSpec((tm, D), lambda i: (i, 0))] * 2,
        out_shape=[jax.ShapeDtypeStruct((S, D), F32), jax.ShapeDtypeStruct((S, D), BF16)],
        compiler_params=_params(("parallel",)),
    )(u, w2, x, gate)


def _ffn_bwd_du(dx, gate, f, w2, a, b, *, name):
    S = dx.shape[0]
    tm, tn = _row_tile(S, 256), FF // 2
    n_i = S // tm

    def body(dx_ref, g_ref, f_ref, w_ref, a_ref, b_ref, df_ref, da_ref, db_ref, dg_ref, dfs):
        i, j = pl.program_id(0), pl.program_id(1)

        @pl.when((i == 0) & (j == 0))
        def _():
            dg_ref[...] = jnp.zeros_like(dg_ref)

        @pl.when(j == 0)
        def _():
            dxv = dx_ref[...]
            dfb = (dxv * (0.5 * g_ref[...])).astype(BF16)
            dfs[...] = dfb
            df_ref[...] = dfb
            dg_ref[...] += _rsum8(dxv * (0.5 * f_ref[...].astype(F32)))

        du = _dot_nt(dfs[...], w_ref[...])
        av = a_ref[...].astype(F32)
        sg, sa = _silu_parts(av)
        da_ref[...] = (du * b_ref[...].astype(F32) * (sg * (1.0 + av * (1.0 - sg)))).astype(BF16)
        db_ref[...] = (du * sa).astype(BF16)

        @pl.when((i == n_i - 1) & (j == FF // tn - 1))
        def _():
            _all_rows(dg_ref)

    row = pl.BlockSpec((tm, D), lambda i, j: (i, 0))
    blk = pl.BlockSpec((tm, tn), lambda i, j: (i, j))
    return pl.pallas_call(
        body, name=name, grid=(n_i, FF // tn),
        in_specs=[row, pl.BlockSpec((1, D), lambda i, j: (0, 0)), row,
                  pl.BlockSpec((tn, D), lambda i, j: (j, 0)), blk, blk],
        out_specs=[row, blk, blk, pl.BlockSpec((8, D), lambda i, j: (0, 0))],
        out_shape=[jax.ShapeDtypeStruct((S, D), BF16), jax.ShapeDtypeStruct((S, FF), BF16),
                   jax.ShapeDtypeStruct((S, FF), BF16), jax.ShapeDtypeStruct((8, D), F32)],
        scratch_shapes=[pltpu.VMEM((tm, D), BF16)],
        compiler_params=_params(("arbitrary", "arbitrary")),
    )(dx, gate, f, w2, a, b)


def _tn_matmul(a, b, *, tm, tn, name):
    S, M = a.shape
    N = b.shape[1]
    ts = _row_tile(S, 1024)
    ns = S // ts

    def body(a_ref, b_ref, o_ref, acc):
        s = pl.program_id(2)
        p = lax.dot_general(a_ref[...], b_ref[...], (((0,), (0,)), ((), ())), preferred_element_type=F32)

        @pl.when(s == 0)
        def _():
            acc[...] = p

        @pl.when(s > 0)
        def _():
            acc[...] += p

        @pl.when(s == ns - 1)
        def _():
            o_ref[...] = acc[...]

    return pl.pallas_call(
        body, name=name, grid=(M // tm, N // tn, ns),
        in_specs=[pl.BlockSpec((ts, tm), lambda i, j, s: (s, i)), pl.BlockSpec((ts, tn), lambda i, j, s: (s, j))],
        out_specs=pl.BlockSpec((tm, tn), lambda i, j, s: (i, j)),
        out_shape=jax.ShapeDtypeStruct((M, N), F32),
        scratch_shapes=[pltpu.VMEM((tm, tn), F32)],
        compiler_params=_params(("parallel", "parallel", "arbitrary")),
    )(a, b)


def _dh_normbwd(pairs, x, ng, sc, dx_next, *, name):
    S = x.shape[0]
    tm = _row_tile(S, 256)
    n_i = S // tm
    n_p = len(pairs)

    def body(*refs):
        a_refs, w_refs = refs[:n_p], refs[n_p:2 * n_p]
        x_ref, g_ref, sc_ref, dxn_ref, dx_ref, p_ref = refs[2 * n_p:]
        i = pl.program_id(0)
        dh = _dot_nt(a_refs[0][...], w_refs[0][...])
        for k in range(1, n_p):
            dh = dh + _dot_nt(a_refs[k][...], w_refs[k][...])
        xv = x_ref[...]
        r = _rms(xv)
        xh = xv * r
        g = g_ref[...]
        dn = dh * (1.0 + sc_ref[...])
        dy = dn * g
        dx_ref[...] = dxn_ref[...] + r * (dy - xh * jnp.mean(dy * xh, axis=-1, keepdims=True))

        @pl.when(i == 0)
        def _():
            p_ref[...] = jnp.zeros_like(p_ref)

        p_ref[:, 0:D] += _rsum8(dh * (xh * g))
        p_ref[:, D:2 * D] += _rsum8(dh)
        p_ref[:, 2 * D:3 * D] += _rsum8(dn * xh)

        @pl.when(i == n_i - 1)
        def _():
            _all_rows(p_ref)

    row = pl.BlockSpec((tm, D), lambda i: (i, 0))
    vec = pl.BlockSpec((1, D), lambda i: (0, 0))
    in_specs = ([pl.BlockSpec((tm, a.shape[1]), lambda i: (i, 0)) for a, _ in pairs]
                + [pl.BlockSpec(w.shape, lambda i: (0, 0)) for _, w in pairs] + [row, vec, vec, row])
    return pl.pallas_call(
        body, name=name, grid=(n_i,), in_specs=in_specs,
        out_specs=[row, pl.BlockSpec((8, 3 * D), lambda i: (0, 0))],
        out_shape=[jax.ShapeDtypeStruct((S, D), F32), jax.ShapeDtypeStruct((8, 3 * D), F32)],
        compiler_params=_params(("arbitrary",)),
    )(*[a for a, _ in pairs], *[w for _, w in pairs], x, ng, sc, dx_next)


def _final_loss(x3, gfin, tgt):
    S = x3.shape[0]
    tm = _row_tile(S, 512)
    n_i = S // tm

    def body(x_ref, g_ref, t_ref, dx_ref, dg_ref, loss_ref, lacc):
        i = pl.program_id(0)
        xv = x_ref[...]
        r = _rms(xv)
        xh = xv * r
        g = g_ref[...]
        e = xh * g - t_ref[...]
        dout = e * (1.0 / D)
        dy = dout * g
        dx_ref[...] = r * (dy - xh * jnp.mean(dy * xh, axis=-1, keepdims=True))

        @pl.when(i == 0)
        def _():
            dg_ref[...] = jnp.zeros_like(dg_ref)
            lacc[...] = jnp.zeros_like(lacc)

        dg_ref[...] += _rsum8(dout * xh)
        lacc[...] += _rsum8(e * e)

        @pl.when(i == n_i - 1)
        def _():
            _all_rows(dg_ref)
            tot = jnp.sum(jnp.sum(lacc[...], axis=0, keepdims=True), axis=1, keepdims=True)
            loss_ref[...] = jnp.broadcast_to(tot * (0.5 / D), loss_ref.shape)

    row = pl.BlockSpec((tm, D), lambda i: (i, 0))
    return pl.pallas_call(
        body, name="final_loss", grid=(n_i,),
        in_specs=[row, pl.BlockSpec((1, D), lambda i: (0, 0)), row],
        out_specs=[row, pl.BlockSpec((8, D), lambda i: (0, 0)), pl.BlockSpec((8, 128), lambda i: (0, 0))],
        out_shape=[jax.ShapeDtypeStruct((S, D), F32), jax.ShapeDtypeStruct((8, D), F32),
                   jax.ShapeDtypeStruct((8, 128), F32)],
        scratch_shapes=[pltpu.VMEM((8, D), F32)],
        compiler_params=_params(("arbitrary",)),
    )(x3, gfin, tgt)


def _mix_in(x, ng, sh, sc, w_in):
    S = x.shape[0]
    tm = _row_tile(S, 512)

    def body(x_ref, g_ref, sh_ref, sc_ref, w_ref, h_ref, z_ref):
        xv = x_ref[...]
        hb = (((xv * _rms(xv)) * g_ref[...]) * (1.0 + sc_ref[...]) + sh_ref[...]).astype(BF16)
        h_ref[...] = hb
        z_ref[...] = jnp.dot(hb, w_ref[...], preferred_element_type=F32)

    row = pl.BlockSpec((tm, D), lambda i: (i, 0))
    vec = pl.BlockSpec((1, D), lambda i: (0, 0))
    return pl.pallas_call(
        body, name="mix_in", grid=(S // tm,),
        in_specs=[row, vec, vec, vec, pl.BlockSpec((D, Z_COLS), lambda i: (0, 0))],
        out_specs=[row, pl.BlockSpec((tm, Z_COLS), lambda i: (i, 0))],
        out_shape=[jax.ShapeDtypeStruct((S, D), BF16), jax.ShapeDtypeStruct((S, Z_COLS), F32)],
        compiler_params=_params(("parallel",)),
    )(x, ng, sh, sc, w_in)


def _conv_taps(u, halo, rows):
    u1 = jnp.where(rows == 0, halo[7:8, :], pltpu.roll(u, 1, 0))
    u2 = jnp.where(rows == 0, halo[6:7, :], jnp.where(rows == 1, halo[7:8, :], pltpu.roll(u, 2, 0)))
    return u1, u2


def _mix_mid(z, conv_w, gq, gkv, wuq, wukv, cs, sn):
    S = z.shape[0]
    tm = _row_tile(S, 512)
    hb = tm // 8

    def body(z_ref, zh_ref, cw_ref, gq_ref, gkv_ref, wuq_ref, wukv_ref, cs_ref, sn_ref,
             ya_ref, q_ref, k_ref, v_ref, cqn_ref, ckvn_ref):
        i = pl.program_id(0)
        xb = z_ref[:, 0:CONV_W]
        u = z_ref[:, CONV_W:2 * CONV_W] * z_ref[:, 2 * CONV_W:3 * CONV_W]
        halo = zh_ref[:, CONV_W:2 * CONV_W] * zh_ref[:, 2 * CONV_W:3 * CONV_W]
        halo = jnp.where(i > 0, halo, 0.0)
        rows = lax.broadcasted_iota(jnp.int32, (tm, CONV_W), 0)
        u1, u2 = _conv_taps(u, halo, rows)
        y = cw_ref[0:1, :] * u2 + cw_ref[1:2, :] * u1 + cw_ref[2:3, :] * u
        ya_ref[...] = xb * y

        lane = lax.broadcasted_iota(jnp.int32, (tm, 128), 1)
        cs_v, sn_v = cs_ref[...], sn_ref[...]
        cq = z_ref[:, 3 * CONV_W:3 * CONV_W + Q_LORA]
        cqn = ((cq * _rms(cq)) * gq_ref[...]).astype(BF16)
        cqn_ref[...] = cqn
        q = jnp.dot(cqn, wuq_ref[...], preferred_element_type=F32)
        for h in range(HEADS):
            o = h * HEAD_SLOT
            q_ref[:, o:o + 128] = q[:, o:o + 128].astype(BF16)
            q_ref[:, o + 128:o + 256] = _rope(q[:, o + 128:o + 256], cs_v, sn_v, lane).astype(BF16)

        c0 = 3 * CONV_W + Q_LORA
        ckv = z_ref[:, c0:c0 + KV_LORA]
        ckvn = ((ckv * _rms(ckv)) * gkv_ref[...]).astype(BF16)
        ckvn_ref[...] = ckvn
        kv = jnp.dot(ckvn, wukv_ref[...], preferred_element_type=F32)
        krot = _rope(z_ref[:, c0 + KV_LORA:Z_COLS], cs_v, sn_v, lane).astype(BF16)
        for h in range(HEADS):
            o = h * HEAD_SLOT
            k_ref[:, o:o + 128] = kv[:, h * 128:(h + 1) * 128].astype(BF16)
            k_ref[:, o + 128:o + 256] = krot
        v_ref[...] = kv[:, HEADS * 128:].astype(BF16)

    def rows_of(n):
        return pl.BlockSpec((tm, n), lambda i: (i, 0))

    def whole(shape):
        return pl.BlockSpec(shape, lambda i: (0, 0))

    return pl.pallas_call(
        body, name="mix_mid", grid=(S // tm,),
        in_specs=[rows_of(Z_COLS), pl.BlockSpec((8, Z_COLS), lambda i: (jnp.maximum(i * hb - 1, 0), 0)),
                  whole((8, CONV_W)), whole((1, Q_LORA)), whole((1, KV_LORA)),
                  whole((Q_LORA, HEADS * HEAD_SLOT)), whole((KV_LORA, 2 * HEADS * 128)),
                  rows_of(128), rows_of(128)],
        out_specs=[rows_of(CONV_W), rows_of(HEADS * HEAD_SLOT), rows_of(HEADS * HEAD_SLOT), rows_of(HEADS * V_HEAD),
                   rows_of(Q_LORA), rows_of(KV_LORA)],
        out_shape=[jax.ShapeDtypeStruct((S, CONV_W), F32), jax.ShapeDtypeStruct((S, HEADS * HEAD_SLOT), BF16),
                   jax.ShapeDtypeStruct((S, HEADS * HEAD_SLOT), BF16), jax.ShapeDtypeStruct((S, HEADS * V_HEAD), BF16),
                   jax.ShapeDtypeStruct((S, Q_LORA), BF16), jax.ShapeDtypeStruct((S, KV_LORA), BF16)],
        compiler_params=_params(("parallel",)),
    )(z, z, conv_w, gq, gkv, wuq, wukv, cs, sn)


def _att_blocks(S):
    bk = min(512, max(S // 4, 128))
    return 2 * bk, bk


def _pair_tables(S, k_major):
    bq, bk = _att_blocks(S)
    nq, nk = S // bq, S // bk
    vis = lambda qi, ki: ki * bk < (qi + 1) * bq
    if k_major:
        pairs = [(qi, ki) for ki in range(nk) for qi in range(nq) if vis(qi, ki)]
    else:
        pairs = [(qi, ki) for qi in range(nq) for ki in range(nk) if vis(qi, ki)]
    cols = [[p[0] for p in pairs], [p[1] for p in pairs], [int((p[1] + 1) * bk > p[0] * bq) for p in pairs]]
    return [jnp.asarray(np.array(c, np.int32)) for c in cols], len(pairs)


def _chunk_mask(qi, ki, bq, bk):
    r = (qi * bq + lax.broadcasted_iota(jnp.int32, (bq, bk), 0)) // CHUNK
    c = (ki * bk + lax.broadcasted_iota(jnp.int32, (bq, bk), 1)) // CHUNK
    return c <= r


def _attention(q, k, v):
    S = q.shape[0]
    bq, bk = _att_blocks(S)
    last_k = bq // bk - 1
    tables, n_pairs = _pair_tables(S, k_major=False)

    def body(qi_ref, ki_ref, mk_ref, q_ref, k_ref, v_ref, o_ref, lse_ref, m_s, l_s, acc_s):
        p_id = pl.program_id(1)
        qi, ki = qi_ref[p_id], ki_ref[p_id]

        @pl.when(ki == 0)
        def _():
            m_s[...] = jnp.full_like(m_s, NEG)
            l_s[...] = jnp.zeros_like(l_s)
            acc_s[...] = jnp.zeros_like(acc_s)

        def update(masked):
            s = lax.dot_general(q_ref[...], k_ref[...], (((1,), (1,)), ((), ())), preferred_element_type=F32)
            if masked:
                s = jnp.where(_chunk_mask(qi, ki, bq, bk), s, NEG)
            m_prev = m_s[...]
            m_new = jnp.maximum(m_prev, jnp.max(s, axis=1, keepdims=True))
            alpha = jnp.exp2((m_prev - m_new) * EXP2_SCALE)
            p = jnp.exp2((s - jnp.tile(m_new, (1, bk // 128))) * EXP2_SCALE)
            l_s[...] = alpha * l_s[...] + jnp.sum(p, axis=1, keepdims=True)
            acc_s[...] = alpha * acc_s[...] + jnp.dot(p.astype(BF16), v_ref[...], preferred_element_type=F32)
            m_s[...] = m_new

        @pl.when(mk_ref[p_id] == 0)
        def _():
            update(False)

        @pl.when(mk_ref[p_id] == 1)
        def _():
            update(True)

        @pl.when(ki == qi * (last_k + 1) + last_k)
        def _():
            l = l_s[...]
            o_ref[...] = acc_s[...] / l
            lse_ref[...] = m_s[...] * EXP2_SCALE + jnp.log2(l)

    grid_spec = pltpu.PrefetchScalarGridSpec(
        num_scalar_prefetch=3, grid=(HEADS, n_pairs),
        in_specs=[pl.BlockSpec((bq, HEAD_SLOT), lambda h, p, qt, kt, mt: (qt[p], h)),
                  pl.BlockSpec((bk, HEAD_SLOT), lambda h, p, qt, kt, mt: (kt[p], h)),
                  pl.BlockSpec((bk, V_HEAD), lambda h, p, qt, kt, mt: (kt[p], h))],
        out_specs=[pl.BlockSpec((bq, V_HEAD), lambda h, p, qt, kt, mt: (qt[p], h))] * 2,
        scratch_shapes=[pltpu.VMEM((bq, V_HEAD), F32)] * 3)
    return pl.pallas_call(
        body, name="attention", grid_spec=grid_spec,
        out_shape=[jax.ShapeDtypeStruct((S, HEADS * V_HEAD), F32)] * 2,
        compiler_params=_params(("arbitrary", "arbitrary")),
    )(*tables, q, k, v)


def _attention_bwd(q, k, v, do, lse2, delta):
    S = q.shape[0]
    bq, bk = _att_blocks(S)
    nq = S // bq
    tables, n_pairs = _pair_tables(S, k_major=True)

    def body(qi_ref, ki_ref, mk_ref, q_ref, k_ref, v_ref, do_ref, lse_ref, dl_ref, dq_hbm, dk_ref, dv_ref,
             dq_s, dk_s, dv_s, sem):
        head, p_id = pl.program_id(0), pl.program_id(1)
        qi, ki = qi_ref[p_id], ki_ref[p_id]
        rows = pl.ds(pl.multiple_of(qi * bq, bq), bq)

        @pl.when(qi * bq <= ki * bk)
        def _():
            dk_s[...] = jnp.zeros_like(dk_s)
            dv_s[...] = jnp.zeros_like(dv_s)

        def update(masked):
            qv, kv, dov = q_ref[...], k_ref[...], do_ref[...]
            s = lax.dot_general(qv, kv, (((1,), (1,)), ((), ())), preferred_element_type=F32)
            dp = lax.dot_general(dov, v_ref[...], (((1,), (1,)), ((), ())), preferred_element_type=F32)
            if masked:
                s = jnp.where(_chunk_mask(qi, ki, bq, bk), s, NEG)
            p = jnp.exp2(s * EXP2_SCALE - jnp.tile(lse_ref[...], (1, bk // 128)))
            dv_s[...] += lax.dot_general(p.astype(BF16), dov, (((0,), (0,)), ((), ())), preferred_element_type=F32)
            ds = (p * (dp - jnp.tile(dl_ref[...], (1, bk // 128)))).astype(BF16)
            dk_s[...] += lax.dot_general(ds, qv, (((0,), (0,)), ((), ())), preferred_element_type=F32)
            dq = jnp.dot(ds, kv, preferred_element_type=F32)

            @pl.when(ki == 0)
            def _():
                dq_s[rows, :] = dq

            @pl.when(ki > 0)
            def _():
                dq_s[rows, :] += dq

        @pl.when(mk_ref[p_id] == 0)
        def _():
            update(False)

        @pl.when(mk_ref[p_id] == 1)
        def _():
            update(True)

        @pl.when(qi == nq - 1)
        def _():
            dk_ref[...] = dk_s[...] * ATT_SCALE
            dv_ref[...] = dv_s[...]

        @pl.when(p_id == n_pairs - 1)
        def _():
            dq_s[...] = dq_s[...] * ATT_SCALE
            out = pltpu.make_async_copy(
                dq_s, dq_hbm.at[:, pl.ds(pl.multiple_of(head * HEAD_SLOT, HEAD_SLOT), HEAD_SLOT)], sem)
            out.start()
            out.wait()

    grid_spec = pltpu.PrefetchScalarGridSpec(
        num_scalar_prefetch=3, grid=(HEADS, n_pairs),
        in_specs=[pl.BlockSpec((bq, HEAD_SLOT), lambda h, p, qt, kt, mt: (qt[p], h)),
                  pl.BlockSpec((bk, HEAD_SLOT), lambda h, p, qt, kt, mt: (kt[p], h)),
                  pl.BlockSpec((bk, V_HEAD), lambda h, p, qt, kt, mt: (kt[p], h)),
                  pl.BlockSpec((bq, V_HEAD), lambda h, p, qt, kt, mt: (qt[p], h)),
                  pl.BlockSpec((bq, V_HEAD), lambda h, p, qt, kt, mt: (qt[p], h)),
                  pl.BlockSpec((bq, V_HEAD), lambda h, p, qt, kt, mt: (qt[p], h))],
        out_specs=[pl.BlockSpec(memory_space=pl.ANY),
                   pl.BlockSpec((bk, HEAD_SLOT), lambda h, p, qt, kt, mt: (kt[p], h)),
                   pl.BlockSpec((bk, V_HEAD), lambda h, p, qt, kt, mt: (kt[p], h))],
        scratch_shapes=[pltpu.VMEM((S, HEAD_SLOT), F32), pltpu.VMEM((bk, HEAD_SLOT), F32),
                        pltpu.VMEM((bk, V_HEAD), F32), pltpu.SemaphoreType.DMA])
    return pl.pallas_call(
        body, name="attention_bwd", grid_spec=grid_spec,
        out_shape=[jax.ShapeDtypeStruct((S, HEADS * HEAD_SLOT), F32), jax.ShapeDtypeStruct((S, HEADS * HEAD_SLOT), F32),
                   jax.ShapeDtypeStruct((S, HEADS * V_HEAD), F32)],
        compiler_params=_params(("arbitrary", "arbitrary")),
    )(*tables, q, k, v, do, lse2, delta)


def _group_mats():
    def blockdiag(n, g):
        idx = np.arange(n) // g
        return jnp.asarray((idx[:, None] == idx[None, :]).astype(np.float32), dtype=BF16)
    return blockdiag(CONV_W, CONV_GROUP), blockdiag(HEADS * V_HEAD, V_HEAD)


def _mix_out(ya, o, gout, w_out, x, gate, ga, gb):
    S = x.shape[0]
    tm = _row_tile(S, 512)

    def body(ya_ref, o_ref, go_ref, w_ref, x_ref, g_ref, ga_ref, gb_ref, xo_ref, yn_ref, yo_ref):
        yav, ov = ya_ref[...], o_ref[...]
        ra = lax.rsqrt(_gsum(yav * yav, ga_ref[...]) * (1.0 / CONV_GROUP) + EPS)
        rb = lax.rsqrt(_gsum(ov * ov, gb_ref[...]) * (1.0 / V_HEAD) + EPS)
        na = ((yav * ra) * go_ref[:, 0:CONV_W]).astype(BF16)
        nb = ((ov * rb) * go_ref[:, CONV_W:]).astype(BF16)
        yn_ref[:, 0:CONV_W] = na
        yn_ref[:, CONV_W:] = nb
        yo = (jnp.dot(na, w_ref[0:CONV_W, :], preferred_element_type=F32)
              + jnp.dot(nb, w_ref[CONV_W:, :], preferred_element_type=F32))
        xo_ref[...] = x_ref[...] + g_ref[...] * yo
        yo_ref[...] = yo.astype(BF16)

    row = pl.BlockSpec((tm, D), lambda i: (i, 0))
    half = pl.BlockSpec((tm, CONV_W), lambda i: (i, 0))
    vec = pl.BlockSpec((1, D), lambda i: (0, 0))
    sq = pl.BlockSpec((CONV_W, CONV_W), lambda i: (0, 0))
    return pl.pallas_call(
        body, name="mix_out", grid=(S // tm,),
        in_specs=[half, half, vec, pl.BlockSpec((D, D), lambda i: (0, 0)), row, vec, sq, sq],
        out_specs=[row, row, row],
        out_shape=[jax.ShapeDtypeStruct((S, D), F32), jax.ShapeDtypeStruct((S, D), BF16),
                   jax.ShapeDtypeStruct((S, D), BF16)],
        compiler_params=_params(("parallel",)),
    )(ya, o, gout, w_out, x, gate, ga, gb)


def _mix_out_bwd(dx, gate, yo, w_out, ya, o, gout, ga, gb):
    S = dx.shape[0]
    tm = _row_tile(S, 256)
    n_i = S // tm

    def norm_bwd(v, dn, gain, gmat, inv_n):
        r = lax.rsqrt(_gsum(v * v, gmat) * inv_n + EPS)
        vh = v * r
        dy = dn * gain
        return r * (dy - vh * (_gsum(dy * vh, gmat) * inv_n)), dn * vh

    def body(dx_ref, g_ref, yo_ref, w_ref, ya_ref, o_ref, go_ref, ga_ref, gb_ref,
             dyo_ref, dya_ref, do_ref, dl_ref, p_ref):
        i = pl.program_id(0)
        dxv = dx_ref[...]
        dyo = (dxv * g_ref[...]).astype(BF16)
        dyo_ref[...] = dyo
        dyn = _dot_nt(dyo, w_ref[...])
        dya, dga = norm_bwd(ya_ref[...], dyn[:, 0:CONV_W], go_ref[:, 0:CONV_W], ga_ref[...], 1.0 / CONV_GROUP)
        ov = o_ref[...]
        do, dgb = norm_bwd(ov, dyn[:, CONV_W:], go_ref[:, CONV_W:], gb_ref[...], 1.0 / V_HEAD)
        dya_ref[...] = dya
        do_ref[...] = do.astype(BF16)
        dl_ref[...] = _gsum(do * ov, gb_ref[...])

        @pl.when(i == 0)
        def _():
            p_ref[...] = jnp.zeros_like(p_ref)

        p_ref[:, 0:D] += _rsum8(dxv * yo_ref[...].astype(F32))
        p_ref[:, D:D + CONV_W] += _rsum8(dga)
        p_ref[:, D + CONV_W:2 * D] += _rsum8(dgb)

        @pl.when(i == n_i - 1)
        def _():
            _all_rows(p_ref)

    row = pl.BlockSpec((tm, D), lambda i: (i, 0))
    half = pl.BlockSpec((tm, CONV_W), lambda i: (i, 0))
    vec = pl.BlockSpec((1, D), lambda i: (0, 0))
    sq = pl.BlockSpec((CONV_W, CONV_W), lambda i: (0, 0))
    return pl.pallas_call(
        body, name="mix_out_bwd", grid=(n_i,),
        in_specs=[row, vec, row, pl.BlockSpec((D, D), lambda i: (0, 0)), half, half, vec, sq, sq],
        out_specs=[row, half, half, half, pl.BlockSpec((8, 2 * D), lambda i: (0, 0))],
        out_shape=[jax.ShapeDtypeStruct((S, D), BF16), jax.ShapeDtypeStruct((S, CONV_W), F32),
                   jax.ShapeDtypeStruct((S, CONV_W), BF16), jax.ShapeDtypeStruct((S, CONV_W), F32),
                   jax.ShapeDtypeStruct((8, 2 * D), F32)],
        compiler_params=_params(("arbitrary",)),
    )(dx, gate, yo, w_out, ya, o, gout, ga, gb)


MID_SUMS = 3 * CONV_W + Q_LORA + KV_LORA


def _mix_mid_bwd(z, dya, conv_w, gq, gkv, wuq, wukv, cs, sn, dq, dk, dv):
    S = z.shape[0]
    tm = _row_tile(S, 256)
    n_i = S // tm
    hb = tm // 8
    last_blk = S // 8 - 1

    def latent_bwd(cv, dcn, gain):
        r = _rms(cv)
        ch = cv * r
        dy = dcn * gain
        return r * (dy - ch * jnp.mean(dy * ch, axis=-1, keepdims=True)), dcn * ch

    def body(z_ref, zp_ref, zn_ref, dya_ref, dyan_ref, cw_ref, gq_ref, gkv_ref, wuq_ref, wukv_ref, cs_ref, sn_ref,
             dq_ref, dk_ref, dv_ref, dz_ref, dqf_ref, dkvf_ref, p_ref):
        i = pl.program_id(0)
        xb, xc, xu = z_ref[:, 0:CONV_W], z_ref[:, CONV_W:2 * CONV_W], z_ref[:, 2 * CONV_W:3 * CONV_W]
        u = xc * xu
        halo = jnp.where(i > 0, zp_ref[:, CONV_W:2 * CONV_W] * zp_ref[:, 2 * CONV_W:3 * CONV_W], 0.0)
        rows = lax.broadcasted_iota(jnp.int32, (tm, CONV_W), 0)
        u1, u2 = _conv_taps(u, halo, rows)
        w0, w1, w2 = cw_ref[0:1, :], cw_ref[1:2, :], cw_ref[2:3, :]
        y = w0 * u2 + w1 * u1 + w2 * u
        dyav = dya_ref[...]
        dy = dyav * xb
        nxt = jnp.where(i < n_i - 1, dyan_ref[...] * zn_ref[:, 0:CONV_W], 0.0)
        dy1 = jnp.where(rows == tm - 1, nxt[0:1, :], pltpu.roll(dy, tm - 1, 0))
        dy2 = jnp.where(rows == tm - 1, nxt[1:2, :], jnp.where(rows == tm - 2, nxt[0:1, :], pltpu.roll(dy, tm - 2, 0)))
        du = w2 * dy + w1 * dy1 + w0 * dy2
        dz_ref[:, 0:CONV_W] = (dyav * y).astype(BF16)
        dz_ref[:, CONV_W:2 * CONV_W] = (du * xu).astype(BF16)
        dz_ref[:, 2 * CONV_W:3 * CONV_W] = (du * xc).astype(BF16)

        lane = lax.broadcasted_iota(jnp.int32, (tm, 128), 1)
        cs_v, sn_v = cs_ref[...], sn_ref[...]
        dkr = jnp.zeros((tm, 128), F32)
        for h in range(HEADS):
            o = h * HEAD_SLOT
            dqf_ref[:, o:o + 128] = dq_ref[:, o:o + 128].astype(BF16)
            dqf_ref[:, o + 128:o + 256] = _rope_t(dq_ref[:, o + 128:o + 256], cs_v, sn_v, lane).astype(BF16)
            dkvf_ref[:, h * 128:(h + 1) * 128] = dk_ref[:, o:o + 128].astype(BF16)
            dkr = dkr + dk_ref[:, o + 128:o + 256]
        dkvf_ref[:, HEADS * 128:] = dv_ref[...].astype(BF16)

        c0 = 3 * CONV_W
        dcqn = _dot_nt(dqf_ref[...], wuq_ref[...])
        dcq, dgq = latent_bwd(z_ref[:, c0:c0 + Q_LORA], dcqn, gq_ref[...])
        dz_ref[:, c0:c0 + Q_LORA] = dcq.astype(BF16)
        c1 = c0 + Q_LORA
        dckvn = _dot_nt(dkvf_ref[...], wukv_ref[...])
        dckv, dgkv = latent_bwd(z_ref[:, c1:c1 + KV_LORA], dckvn, gkv_ref[...])
        dz_ref[:, c1:c1 + KV_LORA] = dckv.astype(BF16)
        dz_ref[:, c1 + KV_LORA:Z_COLS] = _rope_t(dkr, cs_v, sn_v, lane).astype(BF16)

        @pl.when(i == 0)
        def _():
            p_ref[...] = jnp.zeros_like(p_ref)

        p_ref[:, 0:CONV_W] += _rsum8(dy * u2)
        p_ref[:, CONV_W:2 * CONV_W] += _rsum8(dy * u1)
        p_ref[:, 2 * CONV_W:3 * CONV_W] += _rsum8(dy * u)
        p_ref[:, c0:c0 + Q_LORA] += _rsum8(dgq)
        p_ref[:, c1:c1 + KV_LORA] += _rsum8(dgkv)

        @pl.when(i == n_i - 1)
        def _():
            _all_rows(p_ref)

    def rows_of(n):
        return pl.BlockSpec((tm, n), lambda i: (i, 0))

    def whole(shape):
        return pl.BlockSpec(shape, lambda i: (0, 0))

    def prev8(n):
        return pl.BlockSpec((8, n), lambda i: (jnp.maximum(i * hb - 1, 0), 0))

    def next8(n):
        return pl.BlockSpec((8, n), lambda i: (jnp.minimum((i + 1) * hb, last_blk), 0))

    return pl.pallas_call(
        body, name="mix_mid_bwd", grid=(n_i,),
        in_specs=[rows_of(Z_COLS), prev8(Z_COLS), next8(Z_COLS), rows_of(CONV_W), next8(CONV_W),
                  whole((8, CONV_W)), whole((1, Q_LORA)), whole((1, KV_LORA)),
                  whole((Q_LORA, HEADS * HEAD_SLOT)), whole((KV_LORA, 2 * HEADS * 128)),
                  rows_of(128), rows_of(128),
                  rows_of(HEADS * HEAD_SLOT), rows_of(HEADS * HEAD_SLOT), rows_of(HEADS * V_HEAD)],
        out_specs=[rows_of(Z_COLS), rows_of(HEADS * HEAD_SLOT), rows_of(2 * HEADS * 128), whole((8, MID_SUMS))],
        out_shape=[jax.ShapeDtypeStruct((S, Z_COLS), BF16), jax.ShapeDtypeStruct((S, HEADS * HEAD_SLOT), BF16),
                   jax.ShapeDtypeStruct((S, 2 * HEADS * 128), BF16), jax.ShapeDtypeStruct((8, MID_SUMS), F32)],
        compiler_params=_params(("arbitrary",)),
    )(z, z, z, dya, dya, conv_w, gq, gkv, wuq, wukv, cs, sn, dq, dk, dv)


ADA_Q = N_MOD * D // N_CHIPS
ADA_TN = 768


def _ada_forward(c_all, ada_w_q, ada_b_q):
    def body(c_ref, w_ref, b_ref, o_ref):
        cv = c_ref[...]
        sc = (cv * jax.nn.sigmoid(cv)).astype(BF16)
        o_ref[...] = jnp.dot(sc, w_ref[...].astype(BF16), preferred_element_type=F32) + b_ref[...]

    return pl.pallas_call(
        body, name="ada_forward", grid=(ADA_Q // ADA_TN,),
        in_specs=[pl.BlockSpec((16, D), lambda j: (0, 0)), pl.BlockSpec((D, ADA_TN), lambda j: (0, j)),
                  pl.BlockSpec((1, ADA_TN), lambda j: (0, j))],
        out_specs=pl.BlockSpec((16, ADA_TN), lambda j: (0, j)),
        out_shape=jax.ShapeDtypeStruct((16, ADA_Q), F32),
        compiler_params=_params(("parallel",)),
    )(c_all, ada_w_q, ada_b_q)


def _ada_wgrad(c_all, dmod_q):
    def body(c_ref, d_ref, o_ref):
        cv = c_ref[...]
        sc = (cv * jax.nn.sigmoid(cv)).astype(BF16)
        o_ref[...] = lax.dot_general(sc, d_ref[...].astype(BF16), (((0,), (0,)), ((), ())),
                                     preferred_element_type=F32)

    return pl.pallas_call(
        body, name="ada_wgrad", grid=(ADA_Q // ADA_TN,),
        in_specs=[pl.BlockSpec((16, D), lambda j: (0, 0)), pl.BlockSpec((16, ADA_TN), lambda j: (0, j))],
        out_specs=pl.BlockSpec((D, ADA_TN), lambda j: (0, j)),
        out_shape=jax.ShapeDtypeStruct((D, ADA_Q), F32),
        compiler_params=_params(("parallel",)),
    )(c_all, dmod_q)


def _sum_devices(parts):
    n = parts.shape[1]

    def body(p_ref, o_ref):
        o_ref[...] = jnp.broadcast_to(jnp.sum(p_ref[...], axis=0, keepdims=True), o_ref.shape)

    return pl.pallas_call(
        body, name="sum_devices",
        in_specs=[pl.BlockSpec((N_DEV, n), lambda: (0, 0))], out_specs=pl.BlockSpec((N_DEV, n), lambda: (0, 0)),
        out_shape=jax.ShapeDtypeStruct((N_DEV, n), F32),
    )(parts)


def _adamw(w, g, m, v, *, name):
    rows, cols = w.shape
    tr = _row_tile(rows, 256)

    def body(w_ref, g_ref, m_ref, v_ref, d_ref, mo_ref, vo_ref):
        gv = g_ref[...]
        mn = B1 * m_ref[...] + (1.0 - B1) * gv
        vn = B2 * v_ref[...] + (1.0 - B2) * (gv * gv)
        m_hat = mn / (1.0 - B1 ** STEP)
        v_hat = vn / (1.0 - B2 ** STEP)
        d_ref[...] = -LR * (m_hat / (jnp.sqrt(v_hat) + AEPS) + WD * w_ref[...])
        mo_ref[...] = mn
        vo_ref[...] = vn

    blk = pl.BlockSpec((tr, cols), lambda i: (i, 0))
    return pl.pallas_call(
        body, name=name, grid=(rows // tr,), in_specs=[blk] * 4, out_specs=[blk] * 3,
        out_shape=[jax.ShapeDtypeStruct((rows, cols), F32)] * 3,
        compiler_params=_params(("parallel",)),
    )(w, g, m, v)


def _place():
    return lax.axis_index("x"), lax.axis_index("y"), lax.axis_index("c")


def _small_allgather(v, *, name):
    m, n = v.shape

    def body(x_ref, out_ref, send_sems, recv_sems, local_sem):
        x, y, c = _place()
        me, sibling = (x, y, c), (x, y, 1 - c)
        chips = [(1 - x, y), (x, 1 - y), (1 - x, 1 - y)]

        def rows(px, py, pc):
            return out_ref.at[pl.ds((4 * px + 2 * py + pc) * m, m), :]

        def copy(k, block, to, src=None):
            return pltpu.make_async_remote_copy(
                src_ref=rows(*block) if src is None else src, dst_ref=rows(*block),
                send_sem=send_sems.at[k], recv_sem=recv_sems.at[k], device_id=to, device_id_type=MESH)

        mine = pltpu.make_async_copy(x_ref, rows(*me), local_sem)
        mine.start()
        first = [copy(0, me, sibling, src=x_ref)]
        first += [copy(1 + j, me, (*chip, c), src=x_ref) for j, chip in enumerate(chips)]
        for cp in first:
            cp.start()
        passed = [copy(4 + j, (*chip, c), sibling) for j, chip in enumerate(chips)]
        for j, chip in enumerate(chips):
            copy(1 + j, (*chip, c), me).wait_recv()
            passed[j].start()
        copy(0, sibling, me).wait_recv()
        for j, chip in enumerate(chips):
            copy(4 + j, (*chip, 1 - c), me).wait_recv()
        for cp in first + passed:
            cp.wait_send()
        mine.wait()

    return pl.pallas_call(
        body, name=name,
        out_shape=jax.ShapeDtypeStruct((N_DEV * m, n), v.dtype),
        in_specs=[pl.BlockSpec(memory_space=pltpu.VMEM)], out_specs=pl.BlockSpec(memory_space=pltpu.VMEM),
        scratch_shapes=[pltpu.SemaphoreType.DMA((7,)), pltpu.SemaphoreType.DMA((7,)), pltpu.SemaphoreType.DMA],
    )(v)


def _weights_allgather(slab):
    R, n = slab.shape
    H = R // 2

    def body(x_ref, out_ref, send_sems, recv_sems, local_sem):
        x, y, c = _place()
        sibling = (x, y, 1 - c)
        chips = [(1 - x, y), (x, 1 - y), (1 - x, 1 - y)]

        def half(chip, hc):
            return out_ref.at[2 * chip[0] + chip[1], pl.ds(hc * H, H), :]

        def copy(k, chip, hc, to, src=None):
            return pltpu.make_async_remote_copy(
                src_ref=half(chip, hc) if src is None else src, dst_ref=half(chip, hc),
                send_sem=send_sems.at[k], recv_sem=recv_sems.at[k], device_id=to, device_id_type=MESH)

        mine = pltpu.make_async_copy(x_ref, out_ref.at[2 * x + y], local_sem)
        mine.start()
        first = [copy(j, (x, y), c, (*chip, c), src=x_ref.at[pl.ds(c * H, H), :]) for j, chip in enumerate(chips)]
        for cp in first:
            cp.start()
        passed = [copy(3 + j, chip, c, sibling) for j, chip in enumerate(chips)]
        for j, chip in enumerate(chips):
            copy(j, chip, c, (x, y, c)).wait_recv()
            passed[j].start()
        for j, chip in enumerate(chips):
            copy(3 + j, chip, 1 - c, (x, y, c)).wait_recv()
        for cp in first + passed:
            cp.wait_send()
        mine.wait()

    return pl.pallas_call(
        body, name="weights_allgather",
        out_shape=jax.ShapeDtypeStruct((N_CHIPS, R, n), slab.dtype),
        in_specs=[pl.BlockSpec(memory_space=pl.ANY)], out_specs=pl.BlockSpec(memory_space=pl.ANY),
        scratch_shapes=[pltpu.SemaphoreType.DMA((6,)), pltpu.SemaphoreType.DMA((6,)), pltpu.SemaphoreType.DMA],
    )(slab)


def _pair_exchange(g):
    _, R, n = g.shape
    H = R // 2

    def body(g_ref, t_ref, send_sem, recv_sem):
        x, y, c = _place()
        cp = pltpu.make_async_remote_copy(
            src_ref=g_ref.at[:, pl.ds((1 - c) * H, H), :], dst_ref=t_ref,
            send_sem=send_sem, recv_sem=recv_sem, device_id=(x, y, 1 - c), device_id_type=MESH)
        cp.start()
        cp.wait()

    return pl.pallas_call(
        body, name="grads_pair_exchange",
        out_shape=jax.ShapeDtypeStruct((N_CHIPS, H, n), g.dtype),
        in_specs=[pl.BlockSpec(memory_space=pl.ANY)], out_specs=pl.BlockSpec(memory_space=pl.ANY),
        scratch_shapes=[pltpu.SemaphoreType.DMA, pltpu.SemaphoreType.DMA],
    )(g)


def _pair_add(place, g, t):
    _, R, n = g.shape
    H = R // 2
    tr = _row_tile(H, 288)
    nb = H // tr

    def body(pl_ref, g_ref, t_ref, pf_ref, pb_ref):
        s = g_ref[...] + t_ref[...]
        pf_ref[...] = s
        pb_ref[...] = s.astype(BF16)

    grid_spec = pltpu.PrefetchScalarGridSpec(
        num_scalar_prefetch=1, grid=(N_CHIPS, nb),
        in_specs=[pl.BlockSpec((1, tr, n), lambda q, r, p: (q, p[0] * nb + r, 0)),
                  pl.BlockSpec((1, tr, n), lambda q, r, p: (q, r, 0))],
        out_specs=[pl.BlockSpec((1, tr, n), lambda q, r, p: (q, r, 0))] * 2)
    return pl.pallas_call(
        body, name="grads_pair_add", grid_spec=grid_spec,
        out_shape=[jax.ShapeDtypeStruct((N_CHIPS, H, n), F32), jax.ShapeDtypeStruct((N_CHIPS, H, n), BF16)],
        compiler_params=_params(("parallel", "parallel")),
    )(place, g, t)


def _chip_exchange(pb):
    _, H, n = pb.shape

    def body(p_ref, t_ref, send_sems, recv_sems, local_sem):
        x, y, c = _place()
        myq = 2 * x + y
        chips = [(1 - x, y), (x, 1 - y), (1 - x, 1 - y)]
        mine = pltpu.make_async_copy(p_ref.at[myq], t_ref.at[myq], local_sem)
        mine.start()
        sends = [pltpu.make_async_remote_copy(
            src_ref=p_ref.at[2 * chip[0] + chip[1]], dst_ref=t_ref.at[myq],
            send_sem=send_sems.at[j], recv_sem=recv_sems.at[j], device_id=(*chip, c), device_id_type=MESH)
            for j, chip in enumerate(chips)]
        for cp in sends:
            cp.start()
        for j, chip in enumerate(chips):
            slot = t_ref.at[2 * chip[0] + chip[1]]
            pltpu.make_async_remote_copy(
                src_ref=slot, dst_ref=slot, send_sem=send_sems.at[j], recv_sem=recv_sems.at[j],
                device_id=(*chip, c), device_id_type=MESH).wait_recv()
        for cp in sends:
            cp.wait_send()
        mine.wait()

    return pl.pallas_call(
        body, name="grads_chip_exchange",
        out_shape=jax.ShapeDtypeStruct(pb.shape, pb.dtype),
        in_specs=[pl.BlockSpec(memory_space=pl.ANY)], out_specs=pl.BlockSpec(memory_space=pl.ANY),
        scratch_shapes=[pltpu.SemaphoreType.DMA((3,)), pltpu.SemaphoreType.DMA((3,)), pltpu.SemaphoreType.DMA],
    )(pb)


def _chip_add(place, pf, t):
    _, H, n = pf.shape
    tr = _row_tile(H, 288)

    def body(pl_ref, pf_ref, t_ref, o_ref):
        myq = pl_ref[1]
        own = pf_ref[0]
        acc = jnp.where(myq == 0, own, t_ref[0].astype(F32))
        for q in range(1, N_CHIPS):
            acc = acc + jnp.where(myq == q, own, t_ref[q].astype(F32))
        o_ref[...] = acc

    grid_spec = pltpu.PrefetchScalarGridSpec(
        num_scalar_prefetch=1, grid=(H // tr,),
        in_specs=[pl.BlockSpec((1, tr, n), lambda r, p: (p[1], r, 0)),
                  pl.BlockSpec((N_CHIPS, tr, n), lambda r, p: (0, r, 0))],
        out_specs=pl.BlockSpec((tr, n), lambda r, p: (r, 0)))
    return pl.pallas_call(
        body, name="grads_chip_add", grid_spec=grid_spec,
        out_shape=jax.ShapeDtypeStruct((H, n), F32),
        compiler_params=_params(("parallel",)),
    )(place, pf, t)


def _pair_share(rh):
    H, n = rh.shape

    def body(r_ref, o_ref, send_sem, recv_sem, local_sem):
        x, y, c = _place()
        mine = pltpu.make_async_copy(r_ref, o_ref.at[pl.ds(c * H, H), :], local_sem)
        mine.start()
        cp = pltpu.make_async_remote_copy(
            src_ref=r_ref, dst_ref=o_ref.at[pl.ds(c * H, H), :],
            send_sem=send_sem, recv_sem=recv_sem, device_id=(x, y, 1 - c), device_id_type=MESH)
        cp.start()
        other = o_ref.at[pl.ds((1 - c) * H, H), :]
        pltpu.make_async_remote_copy(
            src_ref=other, dst_ref=other, send_sem=send_sem, recv_sem=recv_sem,
            device_id=(x, y, 1 - c), device_id_type=MESH).wait_recv()
        cp.wait_send()
        mine.wait()

    return pl.pallas_call(
        body, name="grads_pair_share",
        out_shape=jax.ShapeDtypeStruct((2 * H, n), rh.dtype),
        in_specs=[pl.BlockSpec(memory_space=pl.ANY)], out_specs=pl.BlockSpec(memory_space=pl.ANY),
        scratch_shapes=[pltpu.SemaphoreType.DMA, pltpu.SemaphoreType.DMA, pltpu.SemaphoreType.DMA],
    )(rh)


SLAB_W = 1024
BULK = [("ffn1_w1", D, FF // 4, True), ("ffn1_w3", D, FF // 4, True), ("ffn1_w2", FF // 4, D, False),
        ("w_in", D, IN_COLS // 4, True), ("w_uq", Q_LORA, 768 // 4, True), ("w_ukv", KV_LORA, 1024 // 4, True),
        ("w_out", D // 4, D, False),
        ("ffn2_w1", D, FF // 4, True), ("ffn2_w3", D, FF // 4, True), ("ffn2_w2", FF // 4, D, False)]


def _slab_rows(k, n):
    rows = k * n // SLAB_W
    return rows, rows + (-rows) % 16


SLAB_R = sum(_slab_rows(k, n)[1] for _, k, n, _ in BULK)


def _pack_shards(parts, lead):
    out = []
    for p, (_, k, n, _) in zip(parts, BULK):
        rows, padded = _slab_rows(k, n)
        r = p.reshape(*lead, rows, SLAB_W)
        if padded != rows:
            r = jnp.pad(r, [(0, 0)] * len(lead) + [(0, padded - rows), (0, 0)])
        out.append(r)
    return jnp.concatenate(out, axis=len(lead))


def _unpack_slab(slab, lead):
    out, r0 = {}, 0
    for name, k, n, _ in BULK:
        rows, padded = _slab_rows(k, n)
        out[name] = lax.slice_in_dim(slab, r0, r0 + rows, axis=len(lead)).reshape(*lead, k, n)
        r0 += padded
    return out


def _full_weight(parts, by_cols):
    if by_cols:
        return jnp.transpose(parts, (1, 0, 2)).reshape(parts.shape[1], -1)
    return parts.reshape(-1, parts.shape[2])


def _quarters(g, by_cols):
    if by_cols:
        k, n = g.shape
        return jnp.transpose(g.reshape(k, N_CHIPS, n // N_CHIPS), (1, 0, 2))
    return g.reshape(N_CHIPS, g.shape[0] // N_CHIPS, g.shape[1])


def _pad_heads(w_uq):
    w = w_uq.reshape(Q_LORA, HEADS, QK_NOPE + QK_ROPE)
    return jnp.pad(w, ((0, 0), (0, 0), (0, HEAD_SLOT - QK_NOPE - QK_ROPE))).reshape(Q_LORA, HEADS * HEAD_SLOT)


def _unpad_heads(g):
    return g.reshape(Q_LORA, HEADS, HEAD_SLOT)[:, :, :QK_NOPE + QK_ROPE].reshape(Q_LORA, HEADS * (QK_NOPE + QK_ROPE))


def _split_kv(w_ukv):
    return jnp.transpose(w_ukv.reshape(KV_LORA, HEADS, 2, 128), (0, 2, 1, 3)).reshape(KV_LORA, 2 * HEADS * 128)


def _merge_kv(g):
    return jnp.transpose(g.reshape(KV_LORA, 2, HEADS, 128), (0, 2, 1, 3)).reshape(KV_LORA, 2 * HEADS * 128)


def _rope_tables(positions):
    inv_freq = ROPE_THETA ** (-jnp.arange(0, QK_ROPE, 2, dtype=F32) / QK_ROPE)
    ang = positions.astype(F32)[:, None] * inv_freq
    cos, sin, zero = jnp.cos(ang), jnp.sin(ang), jnp.zeros((positions.shape[0], 64), F32)
    return jnp.concatenate([cos, cos, zero], axis=1), jnp.concatenate([sin, sin, zero], axis=1)


def _local_step(x, positions, target, mod, vec, conv_w, w):
    row = lambda k: mod[k:k + 1]
    sh1, sc1, g1, sh2, sc2, g2, sh3, sc3, g3 = [row(k) for k in range(N_MOD)]
    cs, sn = _rope_tables(positions)
    cw8 = jnp.pad(conv_w, ((0, 5), (0, 0)))
    ga, gb = _group_mats()
    w_in = jnp.pad(w["w_in"], ((0, 0), (0, Z_COLS - IN_COLS)))
    wuq = _pad_heads(w["w_uq"])
    wukv = _split_kv(w["w_ukv"])

    h1, a1, b1, u1 = _ffn_up(x, vec["norm_ffn1_g"], sh1, sc1, w["ffn1_w1"], w["ffn1_w3"], name="ffn1_up")
    x1, f1 = _ffn_down(u1, w["ffn1_w2"], x, g1, name="ffn1_down")
    h2, z = _mix_in(x1, vec["norm_mix_g"], sh2, sc2, w_in)
    ya, q, k, v, cqn, ckvn = _mix_mid(z, cw8, vec["q_norm_g"], vec["kv_norm_g"], wuq, wukv, cs, sn)
    o, lse = _attention(q, k, v)
    x2, yn, yo = _mix_out(ya, o, vec["out_norm_g"], w["w_out"], x1, g2, ga, gb)
    h3, a3, b3, u3 = _ffn_up(x2, vec["norm_ffn2_g"], sh3, sc3, w["ffn2_w1"], w["ffn2_w3"], name="ffn2_up")
    x3, f3 = _ffn_down(u3, w["ffn2_w2"], x2, g3, name="ffn2_down")
    dx3, dgfin, loss_blk = _final_loss(x3, vec["final_norm_g"], target)

    grads = {}

    def ffn_bwd(tag, dx_out, x_in, h, a, b, u, f, gate, ng, sc, w1, w3, w2):
        df, da, db, dgate = _ffn_bwd_du(dx_out, gate, f, w2, a, b, name=tag + "_bwd_du")
        grads[tag + "_w2"] = _tn_matmul(u, df, tm=FF // 2, tn=D, name=tag + "_dw2")
        grads[tag + "_w1"] = _tn_matmul(h, da, tm=D, tn=FF // 2, name=tag + "_dw1")
        grads[tag + "_w3"] = _tn_matmul(h, db, tm=D, tn=FF // 2, name=tag + "_dw3")
        dx_in, sums = _dh_normbwd([(da, w1), (db, w3)], x_in, ng, sc, dx_out, name=tag + "_bwd_dh")
        return dx_in, dgate, sums

    dx2, dg3, s3 = ffn_bwd("ffn2", dx3, x2, h3, a3, b3, u3, f3, g3, vec["norm_ffn2_g"], sc3,
                           w["ffn2_w1"], w["ffn2_w3"], w["ffn2_w2"])
    dyo, dya, do, delta, s_out = _mix_out_bwd(dx2, g2, yo, w["w_out"], ya, o, vec["out_norm_g"], ga, gb)
    grads["w_out"] = _tn_matmul(yn, dyo, tm=D, tn=D, name="dw_out")
    dq, dk, dv = _attention_bwd(q, k, v, do, lse, delta)
    dz, dqf, dkvf, s_mid = _mix_mid_bwd(z, dya, cw8, vec["q_norm_g"], vec["kv_norm_g"], wuq, wukv, cs, sn,
                                        dq, dk, dv)
    grads["w_uq"] = _unpad_heads(_tn_matmul(cqn, dqf, tm=Q_LORA, tn=HEADS * HEAD_SLOT, name="dw_uq"))
    grads["w_ukv"] = _merge_kv(_tn_matmul(ckvn, dkvf, tm=KV_LORA, tn=2 * HEADS * 128, name="dw_ukv"))
    grads["w_in"] = _tn_matmul(h2, dz, tm=D, tn=Z_COLS // 2, name="dw_in")[:, :IN_COLS]
    dx1, s2 = _dh_normbwd([(dz, w_in)], x1, vec["norm_mix_g"], sc2, dx2, name="mix_bwd_dh")
    dx0, dg1, s1 = ffn_bwd("ffn1", dx1, x, h1, a1, b1, u1, f1, g1, vec["norm_ffn1_g"], sc1,
                           w["ffn1_w1"], w["ffn1_w3"], w["ffn1_w2"])

    def part(s, k):
        return s[0:1, k * D:(k + 1) * D]

    dmod = jnp.concatenate([part(s1, 1), part(s1, 0), dg1[0:1], part(s2, 1), part(s2, 0), part(s_out, 0),
                            part(s3, 1), part(s3, 0), dg3[0:1]], axis=1)
    small = {"norm_ffn1_g": part(s1, 2), "norm_mix_g": part(s2, 2), "out_norm_g": part(s_out, 1),
             "norm_ffn2_g": part(s3, 2), "final_norm_g": dgfin[0:1],
             "q_norm_g": s_mid[0:1, 3 * CONV_W:3 * CONV_W + Q_LORA],
             "kv_norm_g": s_mid[0:1, 3 * CONV_W + Q_LORA:MID_SUMS], "conv_w": s_mid[0:1, 0:3 * CONV_W]}
    return loss_blk, dx0, grads, dmod, small


SMALL = [("norm_ffn1_g", D), ("norm_mix_g", D), ("out_norm_g", D), ("norm_ffn2_g", D), ("final_norm_g", D),
         ("q_norm_g", Q_LORA), ("kv_norm_g", KV_LORA), ("conv_w", 3 * CONV_W)]
WEIGHTS = ['ada_w', 'ada_b', 'norm_ffn1_g', 'ffn1_w1', 'ffn1_w3', 'ffn1_w2', 'norm_mix_g', 'w_in', 'conv_w',
           'q_norm_g', 'w_uq', 'kv_norm_g', 'w_ukv', 'out_norm_g', 'w_out', 'norm_ffn2_g', 'ffn2_w1', 'ffn2_w3',
           'ffn2_w2', 'final_norm_g']


def kernel(x, c, positions, ada_w, ada_b, norm_ffn1_g, ffn1_w1, ffn1_w3, ffn1_w2, norm_mix_g, w_in, conv_w, q_norm_g, w_uq, kv_norm_g, w_ukv, out_norm_g, w_out, norm_ffn2_g, ffn2_w1, ffn2_w3, ffn2_w2, final_norm_g, loss_target, m_ada_w, m_ada_b, m_norm_ffn1_g, m_ffn1_w1, m_ffn1_w3, m_ffn1_w2, m_norm_mix_g, m_w_in, m_conv_w, m_q_norm_g, m_w_uq, m_kv_norm_g, m_w_ukv, m_out_norm_g, m_w_out, m_norm_ffn2_g, m_ffn2_w1, m_ffn2_w3, m_ffn2_w2, m_final_norm_g, v_ada_w, v_ada_b, v_norm_ffn1_g, v_ffn1_w1, v_ffn1_w3, v_ffn1_w2, v_norm_mix_g, v_w_in, v_conv_w, v_q_norm_g, v_w_uq, v_kv_norm_g, v_w_ukv, v_out_norm_g, v_w_out, v_norm_ffn2_g, v_ffn2_w1, v_ffn2_w3, v_ffn2_w2, v_final_norm_g):
    args = dict(locals())
    wts = {n: args[n] for n in WEIGHTS}
    mom = {n: args["m_" + n] for n in WEIGHTS}
    var = {n: args["v_" + n] for n in WEIGHTS}
    ax, ay, ac = _place()
    myq = 2 * ax + ay
    me = 2 * myq + ac
    place = jnp.stack([ac, myq]).astype(jnp.int32)

    shards = [wts[name][0].astype(BF16) for name, *_ in BULK]
    gathered = _unpack_slab(_weights_allgather(_pack_shards(shards, ())), (N_CHIPS,))
    full = {name: _full_weight(gathered[name], by_cols) for name, _, _, by_cols in BULK}

    mine = jnp.concatenate([c, conv_w[0].reshape(1, 3 * CONV_W // N_CHIPS)], axis=1)
    seen = _small_allgather(jnp.pad(mine, ((0, 7), (0, 0))), name="gather_cond").reshape(N_DEV, 8, -1)[:, 0]
    c_all = jnp.pad(seen[:, :D], ((0, 8), (0, 0)))
    conv_full = jnp.transpose(seen[0::2, D:].reshape(N_CHIPS, 3, CONV_W // N_CHIPS), (1, 0, 2)).reshape(3, CONV_W)
    ada_b_q = lax.dynamic_slice_in_dim(ada_b, myq * ADA_Q, ADA_Q, axis=1)
    mod_q = _ada_forward(c_all, ada_w[0], ada_b_q)
    mod_all = _small_allgather(mod_q, name="gather_mod").reshape(N_DEV, 16, ADA_Q)
    mod_rows = jnp.transpose(mod_all[0::2, :N_DEV], (1, 0, 2)).reshape(N_DEV, N_MOD * D)
    mod = lax.dynamic_slice_in_dim(mod_rows, me, 1, axis=0).reshape(N_MOD, D)

    vec = {n: wts[n] for n in ("norm_ffn1_g", "norm_mix_g", "q_norm_g", "kv_norm_g", "out_norm_g", "norm_ffn2_g")}
    vec["final_norm_g"] = final_norm_g.reshape(1, D)
    loss_blk, grad_x, grads, dmod, small = _local_step(x[0], positions[0], loss_target[0], mod, vec, conv_full, full)
    loss = lax.psum(loss_blk[0, 0], ("x", "y", "c"))

    slab = _pack_shards([_quarters(grads[name], by_cols) for name, _, _, by_cols in BULK], (N_CHIPS,))
    pf, pb = _pair_add(place, slab, _pair_exchange(slab))
    reduced = _pair_share(_chip_add(place, pf, _chip_exchange(pb)))
    gq = _unpack_slab(reduced, ())

    rows = jnp.concatenate([dmod] + [small[n] for n, _ in SMALL], axis=1)
    every = _small_allgather(jnp.pad(rows, ((0, 7), (0, 0))), name="gather_small").reshape(N_DEV, 8, -1)[:, 0]
    total = _sum_devices(every)[0:1]
    dmod_q = lax.dynamic_slice_in_dim(every[:, :N_MOD * D], myq * ADA_Q, ADA_Q, axis=1)
    g = {name: gq[name] for name, *_ in BULK}
    g["ada_w"] = _ada_wgrad(c_all, jnp.pad(dmod_q, ((0, 8), (0, 0))))
    g["ada_b"] = total[:, :N_MOD * D]
    off = N_MOD * D
    for n, width in SMALL:
        g[n] = total[:, off:off + width]
        off += width
    g["conv_w"] = lax.dynamic_slice_in_dim(g["conv_w"].reshape(3, CONV_W), myq * (CONV_W // N_CHIPS),
                                           CONV_W // N_CHIPS, axis=1)
    g["final_norm_g"] = g["final_norm_g"].reshape(D)

    delta, new_m, new_v = {}, {}, {}
    for name in ["ada_w"] + [b[0] for b in BULK]:
        shape = wts[name].shape
        delta[name], new_m[name], new_v[name] = [
            r.reshape(shape) for r in _adamw(wts[name][0], g[name], mom[name][0], var[name][0], name="adamw_" + name)]
        g[name] = g[name].reshape(shape)
    smalls = ["ada_b"] + [n for n, _ in SMALL]

    def packed(d):
        flat = jnp.concatenate([d[n].reshape(1, -1) for n in smalls], axis=1)
        return jnp.pad(flat.reshape(-1, D), ((0, 1), (0, 0)))

    res = _adamw(packed(wts), packed(g), packed(mom), packed(var), name="adamw_small")
    off = 0
    for n in smalls:
        size = wts[n].size
        for d, r in zip((delta, new_m, new_v), res):
            d[n] = r.reshape(-1)[off:off + size].reshape(wts[n].shape)
        g[n] = g[n].reshape(wts[n].shape)
        off += size

    return (loss, grad_x[None], *[g[n] for n in WEIGHTS], *[delta[n] for n in WEIGHTS],
            *[new_m[n] for n in WEIGHTS], *[new_v[n] for n in WEIGHTS])
```

```python
import functools

import numpy as np
import jax
import jax.numpy as jnp
from jax import lax
from jax.experimental import pallas as pl
from jax.experimental.pallas import tpu as pltpu

F32 = jnp.float32
BF16 = jnp.bfloat16
MESH = pl.DeviceIdType.MESH

D = 1024
FF = 2816
CONV_W = 512
CONV_GROUP = 64
HEADS = 4
QK_NOPE = 128
QK_ROPE = 64
V_HEAD = 128
Q_LORA = 384
KV_LORA = 256
HEAD_SLOT = 256
IN_COLS = 3 * CONV_W + Q_LORA + KV_LORA + QK_ROPE
Z_COLS = 2304
EPS = 1e-6
ROPE_THETA = 10000.0
CHUNK = 64
ATT_SCALE = (QK_NOPE + QK_ROPE) ** -0.5
NEG = -1e30
EXP2_SCALE = ATT_SCALE * 1.4426950408889634
N_MOD = 9

LR, B1, B2, AEPS, WD, STEP = 0.001, 0.9, 0.999, 1e-08, 0.01, 10

N_CHIPS = 4
N_DEV = 8
VMEM_LIMIT = 56 << 20


def _params(sem, vmem=VMEM_LIMIT):
    return pltpu.CompilerParams(dimension_semantics=sem, vmem_limit_bytes=vmem)


def _rms(v):
    return lax.rsqrt(jnp.mean(v * v, axis=-1, keepdims=True) + EPS)


def _rsum8(v):
    t, n = v.shape
    return jnp.sum(v.reshape(t // 8, 8, n), axis=0)


def _all_rows(ref):
    ref[...] = jnp.broadcast_to(jnp.sum(ref[...], axis=0, keepdims=True), ref.shape)


def _gsum(v, gmat):
    hi = v.astype(BF16)
    lo = (v - hi.astype(F32)).astype(BF16)
    return (jnp.dot(hi, gmat, preferred_element_type=F32)
            + jnp.dot(lo, gmat, preferred_element_type=F32))


def _dot_nt(a, b):
    return lax.dot_general(a, b, (((1,), (1,)), ((), ())), preferred_element_type=F32)


def _silu_parts(a):
    sg = jax.nn.sigmoid(a)
    return sg, a * sg


def _rope(xr, cs, sn, lane):
    rh = jnp.where(lane < 32, -pltpu.roll(xr, 96, 1), pltpu.roll(xr, 32, 1))
    return xr * cs + rh * sn


def _rope_t(g, cs, sn, lane):
    y = g * sn
    rt = jnp.where(lane < 32, pltpu.roll(y, 96, 1), jnp.where(lane < 64, -pltpu.roll(y, 32, 1), 0.0))
    return g * cs + rt


def _row_tile(rows, pref, mult=8):
    t = min(rows, pref) // mult * mult
    while rows % t:
        t -= mult
    return t


def _place():
    return lax.axis_index("x"), lax.axis_index("y"), lax.axis_index("c")


ANY = pl.BlockSpec(memory_space=pl.ANY)


def _hosted_call(body, *, name, grid, in_specs, out_specs, out_shape, scratch_shapes, semantics, args, comm=None):
    n_in, n_out, n_scr = len(in_specs), len(out_specs), len(scratch_shapes)
    if comm is None:
        res = pl.pallas_call(body, name=name, grid=grid, in_specs=in_specs, out_specs=out_specs, out_shape=out_shape,
                             scratch_shapes=scratch_shapes, compiler_params=_params(semantics))(*args)
        return list(res), []
    n_ci, n_co = len(comm.inputs), len(comm.out_shapes)
    total = int(np.prod(grid))

    def hosted(*refs):
        ins, refs = refs[:n_in], refs[n_in:]
        cins, refs = refs[:n_ci], refs[n_ci:]
        outs, refs = refs[:n_out], refs[n_out:]
        couts, refs = refs[:n_co], refs[n_co:]
        scratch, sems = refs[:n_scr], refs[n_scr]
        step = pl.program_id(0)
        for ax in range(1, len(grid)):
            step = step * grid[ax] + pl.program_id(ax)

        @pl.when(step == 0)
        def _():
            comm.start(cins, couts, sems)

        body(*ins, *outs, *scratch)

        @pl.when(step == total - 1)
        def _():
            comm.finish(cins, couts, sems)

    res = pl.pallas_call(
        hosted, name=name, grid=grid, in_specs=list(in_specs) + [ANY] * n_ci,
        out_specs=list(out_specs) + [ANY] * n_co, out_shape=list(out_shape) + list(comm.out_shapes),
        scratch_shapes=list(scratch_shapes) + [pltpu.SemaphoreType.DMA((comm.n_sems,))],
        compiler_params=_params(("arbitrary",) * len(grid)))(*args, *comm.inputs)
    return list(res[:n_out]), list(res[n_out:])


def _run_comm(comm, *, name):
    n_ci = len(comm.inputs)

    def body(*refs):
        cins, couts, sems = refs[:n_ci], refs[n_ci:-1], refs[-1]
        comm.start(cins, couts, sems)
        comm.finish(cins, couts, sems)

    return pl.pallas_call(
        body, name=name, out_shape=list(comm.out_shapes), in_specs=[ANY] * n_ci,
        out_specs=[ANY] * len(comm.out_shapes), scratch_shapes=[pltpu.SemaphoreType.DMA((comm.n_sems,))],
    )(*comm.inputs)


class _Gather:
    def __init__(self, slabs):
        self.inputs = list(slabs)
        self.out_shapes = [jax.ShapeDtypeStruct((N_CHIPS,) + s.shape, s.dtype) for s in slabs]
        self.n_sems = 13 * len(slabs)

    @staticmethod
    def _copy(out, sems, base, k, chip, hc, to, src=None):
        H = out.shape[1] // 2
        half = out.at[2 * chip[0] + chip[1], pl.ds(hc * H, H), :]
        return pltpu.make_async_remote_copy(
            src_ref=half if src is None else src, dst_ref=half, send_sem=sems.at[base + k],
            recv_sem=sems.at[base + 6 + k], device_id=to, device_id_type=MESH)

    def _firsts(self, src, out, sems, base):
        x, y, c = _place()
        H = src.shape[0] // 2
        chips = [(1 - x, y), (x, 1 - y), (1 - x, 1 - y)]
        mine = pltpu.make_async_copy(src, out.at[2 * x + y], sems.at[base + 12])
        return mine, [self._copy(out, sems, base, j, (x, y), c, (*chip, c), src=src.at[pl.ds(c * H, H), :])
                      for j, chip in enumerate(chips)]

    def start(self, ins, outs, sems):
        for i, (src, out) in enumerate(zip(ins, outs)):
            mine, firsts = self._firsts(src, out, sems, 13 * i)
            mine.start()
            for cp in firsts:
                cp.start()

    def finish(self, ins, outs, sems):
        x, y, c = _place()
        chips = [(1 - x, y), (x, 1 - y), (1 - x, 1 - y)]
        passed = []
        for i, out in enumerate(outs):
            for j, chip in enumerate(chips):
                self._copy(out, sems, 13 * i, j, chip, c, (x, y, c)).wait_recv()
                cp = self._copy(out, sems, 13 * i, 3 + j, chip, c, (x, y, 1 - c))
                cp.start()
                passed.append(cp)
        for i, out in enumerate(outs):
            for j, chip in enumerate(chips):
                self._copy(out, sems, 13 * i, 3 + j, chip, 1 - c, (x, y, c)).wait_recv()
        for cp in passed:
            cp.wait_send()
        for i, (src, out) in enumerate(zip(ins, outs)):
            mine, firsts = self._firsts(src, out, sems, 13 * i)
            for cp in firsts:
                cp.wait_send()
            mine.wait()


class _PairExchange:
    def __init__(self, arrays):
        self.inputs = list(arrays)
        self.out_shapes = [jax.ShapeDtypeStruct((N_CHIPS, a.shape[1] // 2, a.shape[2]), a.dtype) for a in arrays]
        self.n_sems = 2 * len(arrays)

    def _copies(self, ins, outs, sems):
        x, y, c = _place()
        return [pltpu.make_async_remote_copy(
            src_ref=g.at[:, pl.ds((1 - c) * t.shape[1], t.shape[1]), :], dst_ref=t, send_sem=sems.at[2 * i],
            recv_sem=sems.at[2 * i + 1], device_id=(x, y, 1 - c), device_id_type=MESH)
            for i, (g, t) in enumerate(zip(ins, outs))]

    def start(self, ins, outs, sems):
        for cp in self._copies(ins, outs, sems):
            cp.start()

    def finish(self, ins, outs, sems):
        for cp in self._copies(ins, outs, sems):
            cp.wait()


class _ChipExchange:
    def __init__(self, arrays):
        self.inputs = list(arrays)
        self.out_shapes = [jax.ShapeDtypeStruct(a.shape, a.dtype) for a in arrays]
        self.n_sems = 7 * len(arrays)

    def _copies(self, p, t, sems, base):
        x, y, c = _place()
        myq = 2 * x + y
        chips = [(1 - x, y), (x, 1 - y), (1 - x, 1 - y)]
        mine = pltpu.make_async_copy(p.at[myq], t.at[myq], sems.at[base + 6])
        sends = [pltpu.make_async_remote_copy(
            src_ref=p.at[2 * chip[0] + chip[1]], dst_ref=t.at[myq], send_sem=sems.at[base + j],
            recv_sem=sems.at[base + 3 + j], device_id=(*chip, c), device_id_type=MESH) for j, chip in enumerate(chips)]
        lands = [pltpu.make_async_remote_copy(
            src_ref=t.at[2 * chip[0] + chip[1]], dst_ref=t.at[2 * chip[0] + chip[1]], send_sem=sems.at[base + j],
            recv_sem=sems.at[base + 3 + j], device_id=(*chip, c), device_id_type=MESH) for j, chip in enumerate(chips)]
        return mine, sends, lands

    def start(self, ins, outs, sems):
        for i, (p, t) in enumerate(zip(ins, outs)):
            mine, sends, _ = self._copies(p, t, sems, 7 * i)
            mine.start()
            for cp in sends:
                cp.start()

    def finish(self, ins, outs, sems):
        for i, (p, t) in enumerate(zip(ins, outs)):
            mine, sends, lands = self._copies(p, t, sems, 7 * i)
            for cp in lands:
                cp.wait_recv()
            for cp in sends:
                cp.wait_send()
            mine.wait()


class _PairShare:
    def __init__(self, arrays):
        self.inputs = list(arrays)
        self.out_shapes = [jax.ShapeDtypeStruct((2 * a.shape[0],) + a.shape[1:], a.dtype) for a in arrays]
        self.n_sems = 3 * len(arrays)

    def _copies(self, r, o, sems, base):
        x, y, c = _place()
        H = r.shape[0]
        mine = pltpu.make_async_copy(r, o.at[pl.ds(c * H, H), :], sems.at[base + 2])
        send = pltpu.make_async_remote_copy(
            src_ref=r, dst_ref=o.at[pl.ds(c * H, H), :], send_sem=sems.at[base], recv_sem=sems.at[base + 1],
            device_id=(x, y, 1 - c), device_id_type=MESH)
        other = o.at[pl.ds((1 - c) * H, H), :]
        land = pltpu.make_async_remote_copy(
            src_ref=other, dst_ref=other, send_sem=sems.at[base], recv_sem=sems.at[base + 1],
            device_id=(x, y, 1 - c), device_id_type=MESH)
        return mine, send, land

    def start(self, ins, outs, sems):
        for i, (r, o) in enumerate(zip(ins, outs)):
            mine, send, _ = self._copies(r, o, sems, 3 * i)
            mine.start()
            send.start()

    def finish(self, ins, outs, sems):
        for i, (r, o) in enumerate(zip(ins, outs)):
            mine, send, land = self._copies(r, o, sems, 3 * i)
            land.wait_recv()
            send.wait_send()
            mine.wait()


def _ffn_up(x, ng, sh, sc, w1, w3, *, name, comm=None):
    S = x.shape[0]
    tm, tn = _row_tile(S, 512), FF // 2

    def body(x_ref, g_ref, sh_ref, sc_ref, w1_ref, w3_ref, h_ref, a_ref, b_ref, u_ref, hs):
        @pl.when(pl.program_id(1) == 0)
        def _():
            xv = x_ref[...]
            h = ((xv * _rms(xv)) * g_ref[...]) * (1.0 + sc_ref[...]) + sh_ref[...]
            hb = h.astype(BF16)
            hs[...] = hb
            h_ref[...] = hb

        h = hs[...]
        a = jnp.dot(h, w1_ref[...], preferred_element_type=F32)
        b = jnp.dot(h, w3_ref[...], preferred_element_type=F32)
        _, sa = _silu_parts(a)
        a_ref[...] = a.astype(BF16)
        b_ref[...] = b.astype(BF16)
        u_ref[...] = (sa * b).astype(BF16)

    row = pl.BlockSpec((tm, D), lambda i, j: (i, 0))
    vec = pl.BlockSpec((1, D), lambda i, j: (0, 0))
    wsp = pl.BlockSpec((D, tn), lambda i, j: (0, j))
    osp = pl.BlockSpec((tm, tn), lambda i, j: (i, j))
    return _hosted_call(
        body, name=name, grid=(S // tm, FF // tn),
        in_specs=[row, vec, vec, vec, wsp, wsp],
        out_specs=[row, osp, osp, osp],
        out_shape=[jax.ShapeDtypeStruct((S, D), BF16)] + [jax.ShapeDtypeStruct((S, FF), BF16)] * 3,
        scratch_shapes=[pltpu.VMEM((tm, D), BF16)],
        semantics=("parallel", "arbitrary"), args=(x, ng, sh, sc, w1, w3), comm=comm)


def _ffn_down(u, w2, x, gate, *, name):
    S = x.shape[0]
    tm = _row_tile(S, 512)

    def body(u_ref, w2_ref, x_ref, g_ref, xo_ref, f_ref):
        f = jnp.dot(u_ref[...], w2_ref[...], preferred_element_type=F32)
        xo_ref[...] = x_ref[...] + (0.5 * g_ref[...]) * f
        f_ref[...] = f.astype(BF16)

    return pl.pallas_call(
        body, name=name, grid=(S // tm,),
        in_specs=[pl.BlockSpec((tm, FF), lambda i: (i, 0)), pl.BlockSpec((FF, D), lambda i: (0, 0)),
                  pl.BlockSpec((tm, D), lambda i: (i, 0)), pl.BlockSpec((1, D), lambda i: (0, 0))],
        out_specs=[pl.BlockSpec((tm, D), lambda i: (i, 0))] * 2,
        out_shape=[jax.ShapeDtypeStruct((S, D), F32), jax.ShapeDtypeStruct((S, D), BF16)],
        compiler_params=_params(("parallel",)),
    )(u, w2, x, gate)


def _ffn_bwd_du(dx, gate, f, w2, a, b, *, name, comm=None):
    S = dx.shape[0]
    tm, tn = _row_tile(S, 256), FF // 2
    n_i = S // tm

    def body(dx_ref, g_ref, f_ref, w_ref, a_ref, b_ref, df_ref, da_ref, db_ref, dg_ref, dfs):
        i, j = pl.program_id(0), pl.program_id(1)

        @pl.when((i == 0) & (j == 0))
        def _():
            dg_ref[...] = jnp.zeros_like(dg_ref)

        @pl.when(j == 0)
        def _():
            dxv = dx_ref[...]
            dfb = (dxv * (0.5 * g_ref[...])).astype(BF16)
            dfs[...] = dfb
            df_ref[...] = dfb
            dg_ref[...] += _rsum8(dxv * (0.5 * f_ref[...].astype(F32)))

        du = _dot_nt(dfs[...], w_ref[...])
        av = a_ref[...].astype(F32)
        sg, sa = _silu_parts(av)
        da_ref[...] = (du * b_ref[...].astype(F32) * (sg * (1.0 + av * (1.0 - sg)))).astype(BF16)
        db_ref[...] = (du * sa).astype(BF16)

        @pl.when((i == n_i - 1) & (j == FF // tn - 1))
        def _():
            _all_rows(dg_ref)

    row = pl.BlockSpec((tm, D), lambda i, j: (i, 0))
    blk = pl.BlockSpec((tm, tn), lambda i, j: (i, j))
    return _hosted_call(
        body, name=name, grid=(n_i, FF // tn),
        in_specs=[row, pl.BlockSpec((1, D), lambda i, j: (0, 0)), row,
                  pl.BlockSpec((tn, D), lambda i, j: (j, 0)), blk, blk],
        out_specs=[row, blk, blk, pl.BlockSpec((8, D), lambda i, j: (0, 0))],
        out_shape=[jax.ShapeDtypeStruct((S, D), BF16), jax.ShapeDtypeStruct((S, FF), BF16),
                   jax.ShapeDtypeStruct((S, FF), BF16), jax.ShapeDtypeStruct((8, D), F32)],
        scratch_shapes=[pltpu.VMEM((tm, D), BF16)],
        semantics=("arbitrary", "arbitrary"), args=(dx, gate, f, w2, a, b), comm=comm)


def _tn_matmul(a, b, *, tm, tn, name, comm=None):
    S, M = a.shape
    N = b.shape[1]
    ts = _row_tile(S, 1024)
    ns = S // ts

    def body(a_ref, b_ref, o_ref, acc):
        s = pl.program_id(2)
        p = lax.dot_general(a_ref[...], b_ref[...], (((0,), (0,)), ((), ())), preferred_element_type=F32)

        @pl.when(s == 0)
        def _():
            acc[...] = p

        @pl.when(s > 0)
        def _():
            acc[...] += p

        @pl.when(s == ns - 1)
        def _():
            o_ref[...] = acc[...]

    (out,), couts = _hosted_call(
        body, name=name, grid=(M // tm, N // tn, ns),
        in_specs=[pl.BlockSpec((ts, tm), lambda i, j, s: (s, i)), pl.BlockSpec((ts, tn), lambda i, j, s: (s, j))],
        out_specs=[pl.BlockSpec((tm, tn), lambda i, j, s: (i, j))],
        out_shape=[jax.ShapeDtypeStruct((M, N), F32)],
        scratch_shapes=[pltpu.VMEM((tm, tn), F32)],
        semantics=("parallel", "parallel", "arbitrary"), args=(a, b), comm=comm)
    return out if comm is None else (out, couts)


def _dh_normbwd(pairs, x, ng, sc, dx_next, *, name, comm=None):
    S = x.shape[0]
    tm = _row_tile(S, 256)
    n_i = S // tm
    n_p = len(pairs)

    def body(*refs):
        a_refs, w_refs = refs[:n_p], refs[n_p:2 * n_p]
        x_ref, g_ref, sc_ref, dxn_ref, dx_ref, p_ref = refs[2 * n_p:]
        i = pl.program_id(0)
        dh = _dot_nt(a_refs[0][...], w_refs[0][...])
        for k in range(1, n_p):
            dh = dh + _dot_nt(a_refs[k][...], w_refs[k][...])
        xv = x_ref[...]
        r = _rms(xv)
        xh = xv * r
        g = g_ref[...]
        dn = dh * (1.0 + sc_ref[...])
        dy = dn * g
        dx_ref[...] = dxn_ref[...] + r * (dy - xh * jnp.mean(dy * xh, axis=-1, keepdims=True))

        @pl.when(i == 0)
        def _():
            p_ref[...] = jnp.zeros_like(p_ref)

        p_ref[:, 0:D] += _rsum8(dh * (xh * g))
        p_ref[:, D:2 * D] += _rsum8(dh)
        p_ref[:, 2 * D:3 * D] += _rsum8(dn * xh)

        @pl.when(i == n_i - 1)
        def _():
            _all_rows(p_ref)

    row = pl.BlockSpec((tm, D), lambda i: (i, 0))
    vec = pl.BlockSpec((1, D), lambda i: (0, 0))
    in_specs = ([pl.BlockSpec((tm, a.shape[1]), lambda i: (i, 0)) for a, _ in pairs]
                + [pl.BlockSpec(w.shape, lambda i: (0, 0)) for _, w in pairs] + [row, vec, vec, row])
    return _hosted_call(
        body, name=name, grid=(n_i,), in_specs=in_specs,
        out_specs=[row, pl.BlockSpec((8, 3 * D), lambda i: (0, 0))],
        out_shape=[jax.ShapeDtypeStruct((S, D), F32), jax.ShapeDtypeStruct((8, 3 * D), F32)],
        scratch_shapes=[], semantics=("arbitrary",),
        args=(*[a for a, _ in pairs], *[w for _, w in pairs], x, ng, sc, dx_next), comm=comm)


def _final_loss(x3, gfin, tgt):
    S = x3.shape[0]
    tm = _row_tile(S, 512)
    n_i = S // tm

    def body(x_ref, g_ref, t_ref, dx_ref, dg_ref, loss_ref, lacc):
        i = pl.program_id(0)
        xv = x_ref[...]
        r = _rms(xv)
        xh = xv * r
        g = g_ref[...]
        e = xh * g - t_ref[...]
        dout = e * (1.0 / D)
        dy = dout * g
        dx_ref[...] = r * (dy - xh * jnp.mean(dy * xh, axis=-1, keepdims=True))

        @pl.when(i == 0)
        def _():
            dg_ref[...] = jnp.zeros_like(dg_ref)
            lacc[...] = jnp.zeros_like(lacc)

        dg_ref[...] += _rsum8(dout * xh)
        lacc[...] += _rsum8(e * e)

        @pl.when(i == n_i - 1)
        def _():
            _all_rows(dg_ref)
            tot = jnp.sum(jnp.sum(lacc[...], axis=0, keepdims=True), axis=1, keepdims=True)
            loss_ref[...] = jnp.broadcast_to(tot * (0.5 / D), loss_ref.shape)

    row = pl.BlockSpec((tm, D), lambda i: (i, 0))
    return pl.pallas_call(
        body, name="final_loss", grid=(n_i,),
        in_specs=[row, pl.BlockSpec((1, D), lambda i: (0, 0)), row],
        out_specs=[row, pl.BlockSpec((8, D), lambda i: (0, 0)), pl.BlockSpec((8, 128), lambda i: (0, 0))],
        out_shape=[jax.ShapeDtypeStruct((S, D), F32), jax.ShapeDtypeStruct((8, D), F32),
                   jax.ShapeDtypeStruct((8, 128), F32)],
        scratch_shapes=[pltpu.VMEM((8, D), F32)],
        compiler_params=_params(("arbitrary",)),
    )(x3, gfin, tgt)


def _mix_in(x, ng, sh, sc, w_in):
    S = x.shape[0]
    tm = _row_tile(S, 512)

    def body(x_ref, g_ref, sh_ref, sc_ref, w_ref, h_ref, z_ref):
        xv = x_ref[...]
        hb = (((xv * _rms(xv)) * g_ref[...]) * (1.0 + sc_ref[...]) + sh_ref[...]).astype(BF16)
        h_ref[...] = hb
        z_ref[...] = jnp.dot(hb, w_ref[...], preferred_element_type=F32)

    row = pl.BlockSpec((tm, D), lambda i: (i, 0))
    vec = pl.BlockSpec((1, D), lambda i: (0, 0))
    return pl.pallas_call(
        body, name="mix_in", grid=(S // tm,),
        in_specs=[row, vec, vec, vec, pl.BlockSpec((D, Z_COLS), lambda i: (0, 0))],
        out_specs=[row, pl.BlockSpec((tm, Z_COLS), lambda i: (i, 0))],
        out_shape=[jax.ShapeDtypeStruct((S, D), BF16), jax.ShapeDtypeStruct((S, Z_COLS), F32)],
        compiler_params=_params(("parallel",)),
    )(x, ng, sh, sc, w_in)


def _conv_taps(u, halo, rows):
    u1 = jnp.where(rows == 0, halo[7:8, :], pltpu.roll(u, 1, 0))
    u2 = jnp.where(rows == 0, halo[6:7, :], jnp.where(rows == 1, halo[7:8, :], pltpu.roll(u, 2, 0)))
    return u1, u2


def _mix_mid(z, conv_w, gq, gkv, wuq, wukv, cs, sn):
    S = z.shape[0]
    tm = _row_tile(S, 512)
    hb = tm // 8

    def body(z_ref, zh_ref, cw_ref, gq_ref, gkv_ref, wuq_ref, wukv_ref, cs_ref, sn_ref,
             ya_ref, q_ref, k_ref, v_ref, cqn_ref, ckvn_ref):
        i = pl.program_id(0)
        xb = z_ref[:, 0:CONV_W]
        u = z_ref[:, CONV_W:2 * CONV_W] * z_ref[:, 2 * CONV_W:3 * CONV_W]
        halo = zh_ref[:, CONV_W:2 * CONV_W] * zh_ref[:, 2 * CONV_W:3 * CONV_W]
        halo = jnp.where(i > 0, halo, 0.0)
        rows = lax.broadcasted_iota(jnp.int32, (tm, CONV_W), 0)
        u1, u2 = _conv_taps(u, halo, rows)
        y = cw_ref[0:1, :] * u2 + cw_ref[1:2, :] * u1 + cw_ref[2:3, :] * u
        ya_ref[...] = xb * y

        lane = lax.broadcasted_iota(jnp.int32, (tm, 128), 1)
        cs_v, sn_v = cs_ref[...], sn_ref[...]
        cq = z_ref[:, 3 * CONV_W:3 * CONV_W + Q_LORA]
        cqn = ((cq * _rms(cq)) * gq_ref[...]).astype(BF16)
        cqn_ref[...] = cqn
        q = jnp.dot(cqn, wuq_ref[...], preferred_element_type=F32)
        for h in range(HEADS):
            o = h * HEAD_SLOT
            q_ref[:, o:o + 128] = q[:, o:o + 128].astype(BF16)
            q_ref[:, o + 128:o + 256] = _rope(q[:, o + 128:o + 256], cs_v, sn_v, lane).astype(BF16)

        c0 = 3 * CONV_W + Q_LORA
        ckv = z_ref[:, c0:c0 + KV_LORA]
        ckvn = ((ckv * _rms(ckv)) * gkv_ref[...]).astype(BF16)
        ckvn_ref[...] = ckvn
        kv = jnp.dot(ckvn, wukv_ref[...], preferred_element_type=F32)
        krot = _rope(z_ref[:, c0 + KV_LORA:Z_COLS], cs_v, sn_v, lane).astype(BF16)
        for h in range(HEADS):
            o = h * HEAD_SLOT
            k_ref[:, o:o + 128] = kv[:, h * 128:(h + 1) * 128].astype(BF16)
            k_ref[:, o + 128:o + 256] = krot
        v_ref[...] = kv[:, HEADS * 128:].astype(BF16)

    def rows_of(n):
        return pl.BlockSpec((tm, n), lambda i: (i, 0))

    def whole(shape):
        return pl.BlockSpec(shape, lambda i: (0, 0))

    return pl.pallas_call(
        body, name="mix_mid", grid=(S // tm,),
        in_specs=[rows_of(Z_COLS), pl.BlockSpec((8, Z_COLS), lambda i: (jnp.maximum(i * hb - 1, 0), 0)),
                  whole((8, CONV_W)), whole((1, Q_LORA)), whole((1, KV_LORA)),
                  whole((Q_LORA, HEADS * HEAD_SLOT)), whole((KV_LORA, 2 * HEADS * 128)),
                  rows_of(128), rows_of(128)],
        out_specs=[rows_of(CONV_W), rows_of(HEADS * HEAD_SLOT), rows_of(HEADS * HEAD_SLOT), rows_of(HEADS * V_HEAD),
                   rows_of(Q_LORA), rows_of(KV_LORA)],
        out_shape=[jax.ShapeDtypeStruct((S, CONV_W), F32), jax.ShapeDtypeStruct((S, HEADS * HEAD_SLOT), BF16),
                   jax.ShapeDtypeStruct((S, HEADS * HEAD_SLOT), BF16), jax.ShapeDtypeStruct((S, HEADS * V_HEAD), BF16),
                   jax.ShapeDtypeStruct((S, Q_LORA), BF16), jax.ShapeDtypeStruct((S, KV_LORA), BF16)],
        compiler_params=_params(("parallel",)),
    )(z, z, conv_w, gq, gkv, wuq, wukv, cs, sn)


def _att_blocks(S):
    bk = min(512, max(S // 4, 128))
    return 2 * bk, bk


def _pair_tables(S, k_major):
    bq, bk = _att_blocks(S)
    nq, nk = S // bq, S // bk
    vis = lambda qi, ki: ki * bk < (qi + 1) * bq
    if k_major:
        pairs = [(qi, ki) for ki in range(nk) for qi in range(nq) if vis(qi, ki)]
    else:
        pairs = [(qi, ki) for qi in range(nq) for ki in range(nk) if vis(qi, ki)]
    cols = [[p[0] for p in pairs], [p[1] for p in pairs], [int((p[1] + 1) * bk > p[0] * bq) for p in pairs]]
    return [jnp.asarray(np.array(c, np.int32)) for c in cols], len(pairs)


def _chunk_mask(qi, ki, bq, bk):
    r = (qi * bq + lax.broadcasted_iota(jnp.int32, (bq, bk), 0)) // CHUNK
    c = (ki * bk + lax.broadcasted_iota(jnp.int32, (bq, bk), 1)) // CHUNK
    return c <= r


def _attention(q, k, v):
    S = q.shape[0]
    bq, bk = _att_blocks(S)
    last_k = bq // bk - 1
    tables, n_pairs = _pair_tables(S, k_major=False)

    def body(qi_ref, ki_ref, mk_ref, q_ref, k_ref, v_ref, o_ref, lse_ref, m_s, l_s, acc_s):
        p_id = pl.program_id(1)
        qi, ki = qi_ref[p_id], ki_ref[p_id]

        @pl.when(ki == 0)
        def _():
            m_s[...] = jnp.full_like(m_s, NEG)
            l_s[...] = jnp.zeros_like(l_s)
            acc_s[...] = jnp.zeros_like(acc_s)

        def update(masked):
            s = lax.dot_general(q_ref[...], k_ref[...], (((1,), (1,)), ((), ())), preferred_element_type=F32)
            if masked:
                s = jnp.where(_chunk_mask(qi, ki, bq, bk), s, NEG)
            m_prev = m_s[...]
            m_new = jnp.maximum(m_prev, jnp.max(s, axis=1, keepdims=True))
            alpha = jnp.exp2((m_prev - m_new) * EXP2_SCALE)
            p = jnp.exp2((s - jnp.tile(m_new, (1, bk // 128))) * EXP2_SCALE)
            l_s[...] = alpha * l_s[...] + jnp.sum(p, axis=1, keepdims=True)
            acc_s[...] = alpha * acc_s[...] + jnp.dot(p.astype(BF16), v_ref[...], preferred_element_type=F32)
            m_s[...] = m_new

        @pl.when(mk_ref[p_id] == 0)
        def _():
            update(False)

        @pl.when(mk_ref[p_id] == 1)
        def _():
            update(True)

        @pl.when(ki == qi * (last_k + 1) + last_k)
        def _():
            l = l_s[...]
            o_ref[...] = acc_s[...] / l
            lse_ref[...] = m_s[...] * EXP2_SCALE + jnp.log2(l)

    grid_spec = pltpu.PrefetchScalarGridSpec(
        num_scalar_prefetch=3, grid=(HEADS, n_pairs),
        in_specs=[pl.BlockSpec((bq, HEAD_SLOT), lambda h, p, qt, kt, mt: (qt[p], h)),
                  pl.BlockSpec((bk, HEAD_SLOT), lambda h, p, qt, kt, mt: (kt[p], h)),
                  pl.BlockSpec((bk, V_HEAD), lambda h, p, qt, kt, mt: (kt[p], h))],
        out_specs=[pl.BlockSpec((bq, V_HEAD), lambda h, p, qt, kt, mt: (qt[p], h))] * 2,
        scratch_shapes=[pltpu.VMEM((bq, V_HEAD), F32)] * 3)
    return pl.pallas_call(
        body, name="attention", grid_spec=grid_spec,
        out_shape=[jax.ShapeDtypeStruct((S, HEADS * V_HEAD), F32)] * 2,
        compiler_params=_params(("arbitrary", "arbitrary")),
    )(*tables, q, k, v)


def _attention_bwd(q, k, v, do, lse2, delta):
    S = q.shape[0]
    bq, bk = _att_blocks(S)
    nq = S // bq
    tables, n_pairs = _pair_tables(S, k_major=True)

    def body(qi_ref, ki_ref, mk_ref, q_ref, k_ref, v_ref, do_ref, lse_ref, dl_ref, dq_hbm, dk_ref, dv_ref,
             dq_s, dk_s, dv_s, sem):
        head, p_id = pl.program_id(0), pl.program_id(1)
        qi, ki = qi_ref[p_id], ki_ref[p_id]
        rows = pl.ds(pl.multiple_of(qi * bq, bq), bq)

        @pl.when(qi * bq <= ki * bk)
        def _():
            dk_s[...] = jnp.zeros_like(dk_s)
            dv_s[...] = jnp.zeros_like(dv_s)

        def update(masked):
            qv, kv, dov = q_ref[...], k_ref[...], do_ref[...]
            s = lax.dot_general(qv, kv, (((1,), (1,)), ((), ())), preferred_element_type=F32)
            dp = lax.dot_general(dov, v_ref[...], (((1,), (1,)), ((), ())), preferred_element_type=F32)
            if masked:
                s = jnp.where(_chunk_mask(qi, ki, bq, bk), s, NEG)
            p = jnp.exp2(s * EXP2_SCALE - jnp.tile(lse_ref[...], (1, bk // 128)))
            dv_s[...] += lax.dot_general(p.astype(BF16), dov, (((0,), (0,)), ((), ())), preferred_element_type=F32)
            ds = (p * (dp - jnp.tile(dl_ref[...], (1, bk // 128)))).astype(BF16)
            dk_s[...] += lax.dot_general(ds, qv, (((0,), (0,)), ((), ())), preferred_element_type=F32)
            dq = jnp.dot(ds, kv, preferred_element_type=F32)

            @pl.when(ki == 0)
            def _():
                dq_s[rows, :] = dq

            @pl.when(ki > 0)
            def _():
                dq_s[rows, :] += dq

        @pl.when(mk_ref[p_id] == 0)
        def _():
            update(False)

        @pl.when(mk_ref[p_id] == 1)
        def _():
            update(True)

        @pl.when(qi == nq - 1)
        def _():
            dk_ref[...] = dk_s[...] * ATT_SCALE
            dv_ref[...] = dv_s[...]

        @pl.when(p_id == n_pairs - 1)
        def _():
            dq_s[...] = dq_s[...] * ATT_SCALE
            out = pltpu.make_async_copy(
                dq_s, dq_hbm.at[:, pl.ds(pl.multiple_of(head * HEAD_SLOT, HEAD_SLOT), HEAD_SLOT)], sem)
            out.start()
            out.wait()

    grid_spec = pltpu.PrefetchScalarGridSpec(
        num_scalar_prefetch=3, grid=(HEADS, n_pairs),
        in_specs=[pl.BlockSpec((bq, HEAD_SLOT), lambda h, p, qt, kt, mt: (qt[p], h)),
                  pl.BlockSpec((bk, HEAD_SLOT), lambda h, p, qt, kt, mt: (kt[p], h)),
                  pl.BlockSpec((bk, V_HEAD), lambda h, p, qt, kt, mt: (kt[p], h)),
                  pl.BlockSpec((bq, V_HEAD), lambda h, p, qt, kt, mt: (qt[p], h)),
                  pl.BlockSpec((bq, V_HEAD), lambda h, p, qt, kt, mt: (qt[p], h)),
                  pl.BlockSpec((bq, V_HEAD), lambda h, p, qt, kt, mt: (qt[p], h))],
        out_specs=[pl.BlockSpec(memory_space=pl.ANY),
                   pl.BlockSpec((bk, HEAD_SLOT), lambda h, p, qt, kt, mt: (kt[p], h)),
                   pl.BlockSpec((bk, V_HEAD), lambda h, p, qt, kt, mt: (kt[p], h))],
        scratch_shapes=[pltpu.VMEM((S, HEAD_SLOT), F32), pltpu.VMEM((bk, HEAD_SLOT), F32),
                        pltpu.VMEM((bk, V_HEAD), F32), pltpu.SemaphoreType.DMA])
    return pl.pallas_call(
        body, name="attention_bwd", grid_spec=grid_spec,
        out_shape=[jax.ShapeDtypeStruct((S, HEADS * HEAD_SLOT), F32), jax.ShapeDtypeStruct((S, HEADS * HEAD_SLOT), F32),
                   jax.ShapeDtypeStruct((S, HEADS * V_HEAD), F32)],
        compiler_params=_params(("arbitrary", "arbitrary")),
    )(*tables, q, k, v, do, lse2, delta)


def _group_mats():
    def blockdiag(n, g):
        idx = np.arange(n) // g
        return jnp.asarray((idx[:, None] == idx[None, :]).astype(np.float32), dtype=BF16)
    return blockdiag(CONV_W, CONV_GROUP), blockdiag(HEADS * V_HEAD, V_HEAD)


def _mix_out(ya, o, gout, w_out, x, gate, ga, gb):
    S = x.shape[0]
    tm = _row_tile(S, 512)

    def body(ya_ref, o_ref, go_ref, w_ref, x_ref, g_ref, ga_ref, gb_ref, xo_ref, yn_ref, yo_ref):
        yav, ov = ya_ref[...], o_ref[...]
        ra = lax.rsqrt(_gsum(yav * yav, ga_ref[...]) * (1.0 / CONV_GROUP) + EPS)
        rb = lax.rsqrt(_gsum(ov * ov, gb_ref[...]) * (1.0 / V_HEAD) + EPS)
        na = ((yav * ra) * go_ref[:, 0:CONV_W]).astype(BF16)
        nb = ((ov * rb) * go_ref[:, CONV_W:]).astype(BF16)
        yn_ref[:, 0:CONV_W] = na
        yn_ref[:, CONV_W:] = nb
        yo = (jnp.dot(na, w_ref[0:CONV_W, :], preferred_element_type=F32)
              + jnp.dot(nb, w_ref[CONV_W:, :], preferred_element_type=F32))
        xo_ref[...] = x_ref[...] + g_ref[...] * yo
        yo_ref[...] = yo.astype(BF16)

    row = pl.BlockSpec((tm, D), lambda i: (i, 0))
    half = pl.BlockSpec((tm, CONV_W), lambda i: (i, 0))
    vec = pl.BlockSpec((1, D), lambda i: (0, 0))
    sq = pl.BlockSpec((CONV_W, CONV_W), lambda i: (0, 0))
    return pl.pallas_call(
        body, name="mix_out", grid=(S // tm,),
        in_specs=[half, half, vec, pl.BlockSpec((D, D), lambda i: (0, 0)), row, vec, sq, sq],
        out_specs=[row, row, row],
        out_shape=[jax.ShapeDtypeStruct((S, D), F32), jax.ShapeDtypeStruct((S, D), BF16),
                   jax.ShapeDtypeStruct((S, D), BF16)],
        compiler_params=_params(("parallel",)),
    )(ya, o, gout, w_out, x, gate, ga, gb)


def _mix_out_bwd(dx, gate, yo, w_out, ya, o, gout, ga, gb, comm=None):
    S = dx.shape[0]
    tm = _row_tile(S, 256)
    n_i = S // tm

    def norm_bwd(v, dn, gain, gmat, inv_n):
        r = lax.rsqrt(_gsum(v * v, gmat) * inv_n + EPS)
        vh = v * r
        dy = dn * gain
        return r * (dy - vh * (_gsum(dy * vh, gmat) * inv_n)), dn * vh

    def body(dx_ref, g_ref, yo_ref, w_ref, ya_ref, o_ref, go_ref, ga_ref, gb_ref,
             dyo_ref, dya_ref, do_ref, dl_ref, p_ref):
        i = pl.program_id(0)
        dxv = dx_ref[...]
        dyo = (dxv * g_ref[...]).astype(BF16)
        dyo_ref[...] = dyo
        dyn = _dot_nt(dyo, w_ref[...])
        dya, dga = norm_bwd(ya_ref[...], dyn[:, 0:CONV_W], go_ref[:, 0:CONV_W], ga_ref[...], 1.0 / CONV_GROUP)
        ov = o_ref[...]
        do, dgb = norm_bwd(ov, dyn[:, CONV_W:], go_ref[:, CONV_W:], gb_ref[...], 1.0 / V_HEAD)
        dya_ref[...] = dya
        do_ref[...] = do.astype(BF16)
        dl_ref[...] = _gsum(do * ov, gb_ref[...])

        @pl.when(i == 0)
        def _():
            p_ref[...] = jnp.zeros_like(p_ref)

        p_ref[:, 0:D] += _rsum8(dxv * yo_ref[...].astype(F32))
        p_ref[:, D:D + CONV_W] += _rsum8(dga)
        p_ref[:, D + CONV_W:2 * D] += _rsum8(dgb)

        @pl.when(i == n_i - 1)
        def _():
            _all_rows(p_ref)

    row = pl.BlockSpec((tm, D), lambda i: (i, 0))
    half = pl.BlockSpec((tm, CONV_W), lambda i: (i, 0))
    vec = pl.BlockSpec((1, D), lambda i: (0, 0))
    sq = pl.BlockSpec((CONV_W, CONV_W), lambda i: (0, 0))
    return _hosted_call(
        body, name="mix_out_bwd", grid=(n_i,),
        in_specs=[row, vec, row, pl.BlockSpec((D, D), lambda i: (0, 0)), half, half, vec, sq, sq],
        out_specs=[row, half, half, half, pl.BlockSpec((8, 2 * D), lambda i: (0, 0))],
        out_shape=[jax.ShapeDtypeStruct((S, D), BF16), jax.ShapeDtypeStruct((S, CONV_W), F32),
                   jax.ShapeDtypeStruct((S, CONV_W), BF16), jax.ShapeDtypeStruct((S, CONV_W), F32),
                   jax.ShapeDtypeStruct((8, 2 * D), F32)],
        scratch_shapes=[], semantics=("arbitrary",), args=(dx, gate, yo, w_out, ya, o, gout, ga, gb), comm=comm)


MID_SUMS = 3 * CONV_W + Q_LORA + KV_LORA


def _mix_mid_bwd(z, dya, conv_w, gq, gkv, wuq, wukv, cs, sn, dq, dk, dv, comm=None):
    S = z.shape[0]
    tm = _row_tile(S, 256)
    n_i = S // tm
    hb = tm // 8
    last_blk = S // 8 - 1

    def latent_bwd(cv, dcn, gain):
        r = _rms(cv)
        ch = cv * r
        dy = dcn * gain
        return r * (dy - ch * jnp.mean(dy * ch, axis=-1, keepdims=True)), dcn * ch

    def body(z_ref, zp_ref, zn_ref, dya_ref, dyan_ref, cw_ref, gq_ref, gkv_ref, wuq_ref, wukv_ref, cs_ref, sn_ref,
             dq_ref, dk_ref, dv_ref, dz_ref, dqf_ref, dkvf_ref, p_ref):
        i = pl.program_id(0)
        xb, xc, xu = z_ref[:, 0:CONV_W], z_ref[:, CONV_W:2 * CONV_W], z_ref[:, 2 * CONV_W:3 * CONV_W]
        u = xc * xu
        halo = jnp.where(i > 0, zp_ref[:, CONV_W:2 * CONV_W] * zp_ref[:, 2 * CONV_W:3 * CONV_W], 0.0)
        rows = lax.broadcasted_iota(jnp.int32, (tm, CONV_W), 0)
        u1, u2 = _conv_taps(u, halo, rows)
        w0, w1, w2 = cw_ref[0:1, :], cw_ref[1:2, :], cw_ref[2:3, :]
        y = w0 * u2 + w1 * u1 + w2 * u
        dyav = dya_ref[...]
        dy = dyav * xb
        nxt = jnp.where(i < n_i - 1, dyan_ref[...] * zn_ref[:, 0:CONV_W], 0.0)
        dy1 = jnp.where(rows == tm - 1, nxt[0:1, :], pltpu.roll(dy, tm - 1, 0))
        dy2 = jnp.where(rows == tm - 1, nxt[1:2, :], jnp.where(rows == tm - 2, nxt[0:1, :], pltpu.roll(dy, tm - 2, 0)))
        du = w2 * dy + w1 * dy1 + w0 * dy2
        dz_ref[:, 0:CONV_W] = (dyav * y).astype(BF16)
        dz_ref[:, CONV_W:2 * CONV_W] = (du * xu).astype(BF16)
        dz_ref[:, 2 * CONV_W:3 * CONV_W] = (du * xc).astype(BF16)

        lane = lax.broadcasted_iota(jnp.int32, (tm, 128), 1)
        cs_v, sn_v = cs_ref[...], sn_ref[...]
        dkr = jnp.zeros((tm, 128), F32)
        for h in range(HEADS):
            o = h * HEAD_SLOT
            dqf_ref[:, o:o + 128] = dq_ref[:, o:o + 128].astype(BF16)
            dqf_ref[:, o + 128:o + 256] = _rope_t(dq_ref[:, o + 128:o + 256], cs_v, sn_v, lane).astype(BF16)
            dkvf_ref[:, h * 128:(h + 1) * 128] = dk_ref[:, o:o + 128].astype(BF16)
            dkr = dkr + dk_ref[:, o + 128:o + 256]
        dkvf_ref[:, HEADS * 128:] = dv_ref[...].astype(BF16)

        c0 = 3 * CONV_W
        dcqn = _dot_nt(dqf_ref[...], wuq_ref[...])
        dcq, dgq = latent_bwd(z_ref[:, c0:c0 + Q_LORA], dcqn, gq_ref[...])
        dz_ref[:, c0:c0 + Q_LORA] = dcq.astype(BF16)
        c1 = c0 + Q_LORA
        dckvn = _dot_nt(dkvf_ref[...], wukv_ref[...])
        dckv, dgkv = latent_bwd(z_ref[:, c1:c1 + KV_LORA], dckvn, gkv_ref[...])
        dz_ref[:, c1:c1 + KV_LORA] = dckv.astype(BF16)
        dz_ref[:, c1 + KV_LORA:Z_COLS] = _rope_t(dkr, cs_v, sn_v, lane).astype(BF16)

        @pl.when(i == 0)
        def _():
            p_ref[...] = jnp.zeros_like(p_ref)

        p_ref[:, 0:CONV_W] += _rsum8(dy * u2)
        p_ref[:, CONV_W:2 * CONV_W] += _rsum8(dy * u1)
        p_ref[:, 2 * CONV_W:3 * CONV_W] += _rsum8(dy * u)
        p_ref[:, c0:c0 + Q_LORA] += _rsum8(dgq)
        p_ref[:, c1:c1 + KV_LORA] += _rsum8(dgkv)

        @pl.when(i == n_i - 1)
        def _():
            _all_rows(p_ref)

    def rows_of(n):
        return pl.BlockSpec((tm, n), lambda i: (i, 0))

    def whole(shape):
        return pl.BlockSpec(shape, lambda i: (0, 0))

    def prev8(n):
        return pl.BlockSpec((8, n), lambda i: (jnp.maximum(i * hb - 1, 0), 0))

    def next8(n):
        return pl.BlockSpec((8, n), lambda i: (jnp.minimum((i + 1) * hb, last_blk), 0))

    return _hosted_call(
        body, name="mix_mid_bwd", grid=(n_i,),
        in_specs=[rows_of(Z_COLS), prev8(Z_COLS), next8(Z_COLS), rows_of(CONV_W), next8(CONV_W),
                  whole((8, CONV_W)), whole((1, Q_LORA)), whole((1, KV_LORA)),
                  whole((Q_LORA, HEADS * HEAD_SLOT)), whole((KV_LORA, 2 * HEADS * 128)),
                  rows_of(128), rows_of(128),
                  rows_of(HEADS * HEAD_SLOT), rows_of(HEADS * HEAD_SLOT), rows_of(HEADS * V_HEAD)],
        out_specs=[rows_of(Z_COLS), rows_of(HEADS * HEAD_SLOT), rows_of(2 * HEADS * 128), whole((8, MID_SUMS))],
        out_shape=[jax.ShapeDtypeStruct((S, Z_COLS), BF16), jax.ShapeDtypeStruct((S, HEADS * HEAD_SLOT), BF16),
                   jax.ShapeDtypeStruct((S, 2 * HEADS * 128), BF16), jax.ShapeDtypeStruct((8, MID_SUMS), F32)],
        scratch_shapes=[], semantics=("arbitrary",),
        args=(z, z, z, dya, dya, conv_w, gq, gkv, wuq, wukv, cs, sn, dq, dk, dv), comm=comm)


ADA_Q = N_MOD * D // N_CHIPS
ADA_TN = 768


def _ada_forward(c_all, ada_w_q, ada_b_q):
    def body(c_ref, w_ref, b_ref, o_ref):
        cv = c_ref[...]
        sc = (cv * jax.nn.sigmoid(cv)).astype(BF16)
        o_ref[...] = jnp.dot(sc, w_ref[...].astype(BF16), preferred_element_type=F32) + b_ref[...]

    return pl.pallas_call(
        body, name="ada_forward", grid=(ADA_Q // ADA_TN,),
        in_specs=[pl.BlockSpec((16, D), lambda j: (0, 0)), pl.BlockSpec((D, ADA_TN), lambda j: (0, j)),
                  pl.BlockSpec((1, ADA_TN), lambda j: (0, j))],
        out_specs=pl.BlockSpec((16, ADA_TN), lambda j: (0, j)),
        out_shape=jax.ShapeDtypeStruct((16, ADA_Q), F32),
        compiler_params=_params(("parallel",)),
    )(c_all, ada_w_q, ada_b_q)


def _ada_wgrad(c_all, dmod_q):
    def body(c_ref, d_ref, o_ref):
        cv = c_ref[...]
        sc = (cv * jax.nn.sigmoid(cv)).astype(BF16)
        o_ref[...] = lax.dot_general(sc, d_ref[...].astype(BF16), (((0,), (0,)), ((), ())),
                                     preferred_element_type=F32)

    return pl.pallas_call(
        body, name="ada_wgrad", grid=(ADA_Q // ADA_TN,),
        in_specs=[pl.BlockSpec((16, D), lambda j: (0, 0)), pl.BlockSpec((16, ADA_TN), lambda j: (0, j))],
        out_specs=pl.BlockSpec((D, ADA_TN), lambda j: (0, j)),
        out_shape=jax.ShapeDtypeStruct((D, ADA_Q), F32),
        compiler_params=_params(("parallel",)),
    )(c_all, dmod_q)


def _sum_devices(parts):
    n = parts.shape[1]

    def body(p_ref, o_ref):
        o_ref[...] = jnp.broadcast_to(jnp.sum(p_ref[...], axis=0, keepdims=True), o_ref.shape)

    return pl.pallas_call(
        body, name="sum_devices",
        in_specs=[pl.BlockSpec((N_DEV, n), lambda: (0, 0))], out_specs=pl.BlockSpec((N_DEV, n), lambda: (0, 0)),
        out_shape=jax.ShapeDtypeStruct((N_DEV, n), F32),
    )(parts)


def _adamw(w, g, m, v, *, name):
    rows, cols = w.shape
    tr = _row_tile(rows, 256)

    def body(w_ref, g_ref, m_ref, v_ref, d_ref, mo_ref, vo_ref):
        gv = g_ref[...]
        mn = B1 * m_ref[...] + (1.0 - B1) * gv
        vn = B2 * v_ref[...] + (1.0 - B2) * (gv * gv)
        m_hat = mn / (1.0 - B1 ** STEP)
        v_hat = vn / (1.0 - B2 ** STEP)
        d_ref[...] = -LR * (m_hat / (jnp.sqrt(v_hat) + AEPS) + WD * w_ref[...])
        mo_ref[...] = mn
        vo_ref[...] = vn

    blk = pl.BlockSpec((tr, cols), lambda i: (i, 0))
    return pl.pallas_call(
        body, name=name, grid=(rows // tr,), in_specs=[blk] * 4, out_specs=[blk] * 3,
        out_shape=[jax.ShapeDtypeStruct((rows, cols), F32)] * 3,
        compiler_params=_params(("parallel",)),
    )(w, g, m, v)


def _small_allgather(v, *, name):
    m, n = v.shape

    def body(x_ref, out_ref, send_sems, recv_sems, local_sem):
        x, y, c = _place()
        me, sibling = (x, y, c), (x, y, 1 - c)
        chips = [(1 - x, y), (x, 1 - y), (1 - x, 1 - y)]

        def rows(px, py, pc):
            return out_ref.at[pl.ds((4 * px + 2 * py + pc) * m, m), :]

        def copy(k, block, to, src=None):
            return pltpu.make_async_remote_copy(
                src_ref=rows(*block) if src is None else src, dst_ref=rows(*block),
                send_sem=send_sems.at[k], recv_sem=recv_sems.at[k], device_id=to, device_id_type=MESH)

        mine = pltpu.make_async_copy(x_ref, rows(*me), local_sem)
        mine.start()
        first = [copy(0, me, sibling, src=x_ref)]
        first += [copy(1 + j, me, (*chip, c), src=x_ref) for j, chip in enumerate(chips)]
        for cp in first:
            cp.start()
        passed = [copy(4 + j, (*chip, c), sibling) for j, chip in enumerate(chips)]
        for j, chip in enumerate(chips):
            copy(1 + j, (*chip, c), me).wait_recv()
            passed[j].start()
        copy(0, sibling, me).wait_recv()
        for j, chip in enumerate(chips):
            copy(4 + j, (*chip, 1 - c), me).wait_recv()
        for cp in first + passed:
            cp.wait_send()
        mine.wait()

    return pl.pallas_call(
        body, name=name,
        out_shape=jax.ShapeDtypeStruct((N_DEV * m, n), v.dtype),
        in_specs=[pl.BlockSpec(memory_space=pltpu.VMEM)], out_specs=pl.BlockSpec(memory_space=pltpu.VMEM),
        scratch_shapes=[pltpu.SemaphoreType.DMA((7,)), pltpu.SemaphoreType.DMA((7,)), pltpu.SemaphoreType.DMA],
    )(v)


def _pair_add(place, g, t, *, name):
    _, R, n = g.shape
    H = R // 2
    tr = _row_tile(H, 288, mult=16)
    nb = H // tr

    def body(pl_ref, g_ref, t_ref, pf_ref, pb_ref):
        s = g_ref[...] + t_ref[...]
        pf_ref[...] = s
        pb_ref[...] = s.astype(BF16)

    grid_spec = pltpu.PrefetchScalarGridSpec(
        num_scalar_prefetch=1, grid=(N_CHIPS, nb),
        in_specs=[pl.BlockSpec((1, tr, n), lambda q, r, p: (q, p[0] * nb + r, 0)),
                  pl.BlockSpec((1, tr, n), lambda q, r, p: (q, r, 0))],
        out_specs=[pl.BlockSpec((1, tr, n), lambda q, r, p: (q, r, 0))] * 2)
    return pl.pallas_call(
        body, name=name, grid_spec=grid_spec,
        out_shape=[jax.ShapeDtypeStruct((N_CHIPS, H, n), F32), jax.ShapeDtypeStruct((N_CHIPS, H, n), BF16)],
        compiler_params=_params(("parallel", "parallel")),
    )(place, g, t)


def _chip_add(place, pf, t, *, name):
    _, H, n = pf.shape
    tr = _row_tile(H, 288, mult=16)

    def body(pl_ref, pf_ref, t_ref, o_ref):
        myq = pl_ref[1]
        own = pf_ref[0]
        acc = jnp.where(myq == 0, own, t_ref[0].astype(F32))
        for q in range(1, N_CHIPS):
            acc = acc + jnp.where(myq == q, own, t_ref[q].astype(F32))
        o_ref[...] = acc

    grid_spec = pltpu.PrefetchScalarGridSpec(
        num_scalar_prefetch=1, grid=(H // tr,),
        in_specs=[pl.BlockSpec((1, tr, n), lambda r, p: (p[1], r, 0)),
                  pl.BlockSpec((N_CHIPS, tr, n), lambda r, p: (0, r, 0))],
        out_specs=pl.BlockSpec((tr, n), lambda r, p: (r, 0)))
    return pl.pallas_call(
        body, name=name, grid_spec=grid_spec,
        out_shape=jax.ShapeDtypeStruct((H, n), F32),
        compiler_params=_params(("parallel",)),
    )(place, pf, t)


SLAB_W = 1024
BULK = [("ffn1_w1", D, FF // 4, True), ("ffn1_w3", D, FF // 4, True), ("ffn1_w2", FF // 4, D, False),
        ("w_in", D, IN_COLS // 4, True), ("w_uq", Q_LORA, 768 // 4, True), ("w_ukv", KV_LORA, 1024 // 4, True),
        ("w_out", D // 4, D, False),
        ("ffn2_w1", D, FF // 4, True), ("ffn2_w3", D, FF // 4, True), ("ffn2_w2", FF // 4, D, False)]


def _slab_rows(k, n):
    rows = k * n // SLAB_W
    return rows, rows + (-rows) % 16


def _group(*names):
    return [b for b in BULK if b[0] in names]


W_FIRST = _group("ffn1_w1", "ffn1_w3")
W_REST = [b for b in BULK if b not in W_FIRST]
G_FFN2 = _group("ffn2_w1", "ffn2_w3", "ffn2_w2")
G_MIX = _group("w_in", "w_uq", "w_ukv", "w_out")
G_FFN1 = _group("ffn1_w1", "ffn1_w3", "ffn1_w2")


def _pack_shards(parts, specs, lead):
    out = []
    for p, (_, k, n, _) in zip(parts, specs):
        rows, padded = _slab_rows(k, n)
        r = p.reshape(*lead, rows, SLAB_W)
        if padded != rows:
            r = jnp.pad(r, [(0, 0)] * len(lead) + [(0, padded - rows), (0, 0)])
        out.append(r)
    return jnp.concatenate(out, axis=len(lead))


def _unpack_slab(slab, specs, lead):
    out, r0 = {}, 0
    for name, k, n, _ in specs:
        rows, padded = _slab_rows(k, n)
        out[name] = lax.slice_in_dim(slab, r0, r0 + rows, axis=len(lead)).reshape(*lead, k, n)
        r0 += padded
    return out


def _full_weight(parts, by_cols):
    if by_cols:
        return jnp.transpose(parts, (1, 0, 2)).reshape(parts.shape[1], -1)
    return parts.reshape(-1, parts.shape[2])


def _quarters(g, by_cols):
    if by_cols:
        k, n = g.shape
        return jnp.transpose(g.reshape(k, N_CHIPS, n // N_CHIPS), (1, 0, 2))
    return g.reshape(N_CHIPS, g.shape[0] // N_CHIPS, g.shape[1])


def _pad_heads(w_uq):
    w = w_uq.reshape(Q_LORA, HEADS, QK_NOPE + QK_ROPE)
    return jnp.pad(w, ((0, 0), (0, 0), (0, HEAD_SLOT - QK_NOPE - QK_ROPE))).reshape(Q_LORA, HEADS * HEAD_SLOT)


def _unpad_heads(g):
    return g.reshape(Q_LORA, HEADS, HEAD_SLOT)[:, :, :QK_NOPE + QK_ROPE].reshape(Q_LORA, HEADS * (QK_NOPE + QK_ROPE))


def _split_kv(w_ukv):
    return jnp.transpose(w_ukv.reshape(KV_LORA, HEADS, 2, 128), (0, 2, 1, 3)).reshape(KV_LORA, 2 * HEADS * 128)


def _merge_kv(g):
    return jnp.transpose(g.reshape(KV_LORA, 2, HEADS, 128), (0, 2, 1, 3)).reshape(KV_LORA, 2 * HEADS * 128)


def _rope_tables(positions):
    inv_freq = ROPE_THETA ** (-jnp.arange(0, QK_ROPE, 2, dtype=F32) / QK_ROPE)
    ang = positions.astype(F32)[:, None] * inv_freq
    cos, sin, zero = jnp.cos(ang), jnp.sin(ang), jnp.zeros((positions.shape[0], 64), F32)
    return jnp.concatenate([cos, cos, zero], axis=1), jnp.concatenate([sin, sin, zero], axis=1)


def _reduce_tail(place, pf, t2, tag, host=None):
    rh = _chip_add(place, pf, t2, name=tag + "_chip_add")
    if host is None:
        return None, _run_comm(_PairShare([rh]), name=tag + "_pair_share")[0]
    out, (red,) = host(_PairShare([rh]))
    return out, red


def _local_step(x, positions, target, mod, vec, conv_w, w_first, rest_shards, place):
    row = lambda k: mod[k:k + 1]
    sh1, sc1, g1, sh2, sc2, g2, sh3, sc3, g3 = [row(k) for k in range(N_MOD)]
    cs, sn = _rope_tables(positions)
    cw8 = jnp.pad(conv_w, ((0, 5), (0, 0)))
    ga, gb = _group_mats()
    dist = place is not None
    comm = lambda prog: prog if dist else None

    gather = comm(_Gather([_pack_shards([rest_shards[b[0]] for b in W_REST], W_REST, ())]) if dist else None)
    (h1, a1, b1, u1), got = _ffn_up(x, vec["norm_ffn1_g"], sh1, sc1, w_first["ffn1_w1"], w_first["ffn1_w3"],
                                    name="ffn1_up", comm=gather)
    if dist:
        parts = _unpack_slab(got[0], W_REST, (N_CHIPS,))
        w = {name: _full_weight(parts[name], by_cols) for name, _, _, by_cols in W_REST}
    else:
        w = dict(rest_shards)
    w.update(w_first)
    w_in = jnp.pad(w["w_in"], ((0, 0), (0, Z_COLS - IN_COLS)))
    wuq = _pad_heads(w["w_uq"])
    wukv = _split_kv(w["w_ukv"])
    x1, f1 = _ffn_down(u1, w["ffn1_w2"], x, g1, name="ffn1_down")
    h2, z = _mix_in(x1, vec["norm_mix_g"], sh2, sc2, w_in)
    ya, q, k, v, cqn, ckvn = _mix_mid(z, cw8, vec["q_norm_g"], vec["kv_norm_g"], wuq, wukv, cs, sn)
    o, lse = _attention(q, k, v)
    x2, yn, yo = _mix_out(ya, o, vec["out_norm_g"], w["w_out"], x1, g2, ga, gb)
    (h3, a3, b3, u3), _ = _ffn_up(x2, vec["norm_ffn2_g"], sh3, sc3, w["ffn2_w1"], w["ffn2_w3"], name="ffn2_up")
    x3, f3 = _ffn_down(u3, w["ffn2_w2"], x2, g3, name="ffn2_down")
    dx3, dgfin, loss_blk = _final_loss(x3, vec["final_norm_g"], target)

    grads, reduced = {}, {}

    def tn(a, b, tm, tn_, name, prog=None):
        if prog is None:
            return _tn_matmul(a, b, tm=tm, tn=tn_, name=name), None
        return _tn_matmul(a, b, tm=tm, tn=tn_, name=name, comm=prog)

    def slab_of(specs):
        return _pack_shards([_quarters(grads[n], by_cols) for n, _, _, by_cols in specs], specs, (N_CHIPS,))

    (df3, da3, db3, dg3), _ = _ffn_bwd_du(dx3, g3, f3, w["ffn2_w2"], a3, b3, name="ffn2_bwd_du")
    grads["ffn2_w2"], _ = tn(u3, df3, FF // 2, D, "ffn2_dw2")
    grads["ffn2_w1"], _ = tn(h3, da3, D, FF // 2, "ffn2_dw1")
    grads["ffn2_w3"], _ = tn(h3, db3, D, FF // 2, "ffn2_dw3")
    (dx2, s3), _ = _dh_normbwd([(da3, w["ffn2_w1"]), (db3, w["ffn2_w3"])], x2, vec["norm_ffn2_g"], sc3, dx3,
                               name="ffn2_bwd_dh")

    p1 = slab_of(G_FFN2) if dist else None
    (dyo, dya, do, delta, s_out), t1 = _mix_out_bwd(dx2, g2, yo, w["w_out"], ya, o, vec["out_norm_g"], ga, gb,
                                                    comm=comm(_PairExchange([p1]) if dist else None))
    grads["w_out"], _ = tn(yn, dyo, D, D, "dw_out")
    if dist:
        pf1, pb1 = _pair_add(place, p1, t1[0], name="ffn2g_pair_add")
    dq, dk, dv = _attention_bwd(q, k, v, do, lse, delta)
    (dz, dqf, dkvf, s_mid), t2 = _mix_mid_bwd(z, dya, cw8, vec["q_norm_g"], vec["kv_norm_g"], wuq, wukv, cs, sn,
                                              dq, dk, dv, comm=comm(_ChipExchange([pb1]) if dist else None))
    g_uq, _ = tn(cqn, dqf, Q_LORA, HEADS * HEAD_SLOT, "dw_uq")
    g_ukv, _ = tn(ckvn, dkvf, KV_LORA, 2 * HEADS * 128, "dw_ukv")
    grads["w_uq"], grads["w_ukv"] = _unpad_heads(g_uq), _merge_kv(g_ukv)
    if dist:
        g_in, red = _reduce_tail(place, pf1, t2[0], "ffn2g",
                                 host=lambda prog: tn(h2, dz, D, Z_COLS // 2, "dw_in", prog))
        reduced.update(_unpack_slab(red, G_FFN2, ()))
    else:
        g_in, _ = tn(h2, dz, D, Z_COLS // 2, "dw_in")
    grads["w_in"] = g_in[:, :IN_COLS]
    (dx1, s2), _ = _dh_normbwd([(dz, w_in)], x1, vec["norm_mix_g"], sc2, dx2, name="mix_bwd_dh")

    p2 = slab_of(G_MIX) if dist else None
    (df1, da1, db1, dg1), t1 = _ffn_bwd_du(dx1, g1, f1, w["ffn1_w2"], a1, b1, name="ffn1_bwd_du",
                                           comm=comm(_PairExchange([p2]) if dist else None))
    if dist:
        pf2, pb2 = _pair_add(place, p2, t1[0], name="mixg_pair_add")
        grads["ffn1_w2"], t2 = tn(u1, df1, FF // 2, D, "ffn1_dw2", _ChipExchange([pb2]))
        grads["ffn1_w1"], red = _reduce_tail(place, pf2, t2[0], "mixg",
                                             host=lambda prog: tn(h1, da1, D, FF // 2, "ffn1_dw1", prog))
        reduced.update(_unpack_slab(red, G_MIX, ()))
    else:
        grads["ffn1_w2"], _ = tn(u1, df1, FF // 2, D, "ffn1_dw2")
        grads["ffn1_w1"], _ = tn(h1, da1, D, FF // 2, "ffn1_dw1")
    grads["ffn1_w3"], _ = tn(h1, db1, D, FF // 2, "ffn1_dw3")

    p3 = slab_of(G_FFN1) if dist else None
    (dx0, s1), t1 = _dh_normbwd([(da1, w["ffn1_w1"]), (db1, w["ffn1_w3"])], x, vec["norm_ffn1_g"], sc1, dx1,
                                name="ffn1_bwd_dh", comm=comm(_PairExchange([p3]) if dist else None))
    if dist:
        pf3, pb3 = _pair_add(place, p3, t1[0], name="ffn1g_pair_add")
        t2 = _run_comm(_ChipExchange([pb3]), name="ffn1g_chip_exchange")
        _, red = _reduce_tail(place, pf3, t2[0], "ffn1g")
        reduced.update(_unpack_slab(red, G_FFN1, ()))
    else:
        reduced = grads

    def part(s, k):
        return s[0:1, k * D:(k + 1) * D]

    dmod = jnp.concatenate([part(s1, 1), part(s1, 0), dg1[0:1], part(s2, 1), part(s2, 0), part(s_out, 0),
                            part(s3, 1), part(s3, 0), dg3[0:1]], axis=1)
    small = {"norm_ffn1_g": part(s1, 2), "norm_mix_g": part(s2, 2), "out_norm_g": part(s_out, 1),
             "norm_ffn2_g": part(s3, 2), "final_norm_g": dgfin[0:1],
             "q_norm_g": s_mid[0:1, 3 * CONV_W:3 * CONV_W + Q_LORA],
             "kv_norm_g": s_mid[0:1, 3 * CONV_W + Q_LORA:MID_SUMS], "conv_w": s_mid[0:1, 0:3 * CONV_W]}
    return loss_blk, dx0, reduced, dmod, small


SMALL = [("norm_ffn1_g", D), ("norm_mix_g", D), ("out_norm_g", D), ("norm_ffn2_g", D), ("final_norm_g", D),
         ("q_norm_g", Q_LORA), ("kv_norm_g", KV_LORA), ("conv_w", 3 * CONV_W)]
WEIGHTS = ['ada_w', 'ada_b', 'norm_ffn1_g', 'ffn1_w1', 'ffn1_w3', 'ffn1_w2', 'norm_mix_g', 'w_in', 'conv_w',
           'q_norm_g', 'w_uq', 'kv_norm_g', 'w_ukv', 'out_norm_g', 'w_out', 'norm_ffn2_g', 'ffn2_w1', 'ffn2_w3',
           'ffn2_w2', 'final_norm_g']


def kernel(x, c, positions, ada_w, ada_b, norm_ffn1_g, ffn1_w1, ffn1_w3, ffn1_w2, norm_mix_g, w_in, conv_w, q_norm_g, w_uq, kv_norm_g, w_ukv, out_norm_g, w_out, norm_ffn2_g, ffn2_w1, ffn2_w3, ffn2_w2, final_norm_g, loss_target, m_ada_w, m_ada_b, m_norm_ffn1_g, m_ffn1_w1, m_ffn1_w3, m_ffn1_w2, m_norm_mix_g, m_w_in, m_conv_w, m_q_norm_g, m_w_uq, m_kv_norm_g, m_w_ukv, m_out_norm_g, m_w_out, m_norm_ffn2_g, m_ffn2_w1, m_ffn2_w3, m_ffn2_w2, m_final_norm_g, v_ada_w, v_ada_b, v_norm_ffn1_g, v_ffn1_w1, v_ffn1_w3, v_ffn1_w2, v_norm_mix_g, v_w_in, v_conv_w, v_q_norm_g, v_w_uq, v_kv_norm_g, v_w_ukv, v_out_norm_g, v_w_out, v_norm_ffn2_g, v_ffn2_w1, v_ffn2_w3, v_ffn2_w2, v_final_norm_g):
    args = dict(locals())
    wts = {n: args[n] for n in WEIGHTS}
    mom = {n: args["m_" + n] for n in WEIGHTS}
    var = {n: args["v_" + n] for n in WEIGHTS}
    ax, ay, ac = _place()
    myq = 2 * ax + ay
    me = 2 * myq + ac
    place = jnp.stack([ac, myq]).astype(jnp.int32)

    shards = {name: wts[name][0].astype(BF16) for name, *_ in BULK}
    first = _run_comm(_Gather([_pack_shards([shards[b[0]] for b in W_FIRST], W_FIRST, ())]), name="gather_ffn1")[0]
    first = _unpack_slab(first, W_FIRST, (N_CHIPS,))
    w_first = {name: _full_weight(first[name], by_cols) for name, _, _, by_cols in W_FIRST}

    mine = jnp.concatenate([c, conv_w[0].reshape(1, 3 * CONV_W // N_CHIPS)], axis=1)
    seen = _small_allgather(jnp.pad(mine, ((0, 7), (0, 0))), name="gather_cond").reshape(N_DEV, 8, -1)[:, 0]
    c_all = jnp.pad(seen[:, :D], ((0, 8), (0, 0)))
    conv_full = jnp.transpose(seen[0::2, D:].reshape(N_CHIPS, 3, CONV_W // N_CHIPS), (1, 0, 2)).reshape(3, CONV_W)
    ada_b_q = lax.dynamic_slice_in_dim(ada_b, myq * ADA_Q, ADA_Q, axis=1)
    mod_q = _ada_forward(c_all, ada_w[0], ada_b_q)
    mod_all = _small_allgather(mod_q, name="gather_mod").reshape(N_DEV, 16, ADA_Q)
    mod_rows = jnp.transpose(mod_all[0::2, :N_DEV], (1, 0, 2)).reshape(N_DEV, N_MOD * D)
    mod = lax.dynamic_slice_in_dim(mod_rows, me, 1, axis=0).reshape(N_MOD, D)

    vec = {n: wts[n] for n in ("norm_ffn1_g", "norm_mix_g", "q_norm_g", "kv_norm_g", "out_norm_g", "norm_ffn2_g")}
    vec["final_norm_g"] = final_norm_g.reshape(1, D)
    loss_blk, grad_x, gq, dmod, small = _local_step(x[0], positions[0], loss_target[0], mod, vec, conv_full, w_first,
                                                    {b[0]: shards[b[0]] for b in W_REST}, place)
    loss = lax.psum(loss_blk[0, 0], ("x", "y", "c"))

    rows = jnp.concatenate([dmod] + [small[n] for n, _ in SMALL], axis=1)
    every = _small_allgather(jnp.pad(rows, ((0, 7), (0, 0))), name="gather_small").reshape(N_DEV, 8, -1)[:, 0]
    total = _sum_devices(every)[0:1]
    dmod_q = lax.dynamic_slice_in_dim(every[:, :N_MOD * D], myq * ADA_Q, ADA_Q, axis=1)
    g = {name: gq[name] for name, *_ in BULK}
    g["ada_w"] = _ada_wgrad(c_all, jnp.pad(dmod_q, ((0, 8), (0, 0))))
    g["ada_b"] = total[:, :N_MOD * D]
    off = N_MOD * D
    for n, width in SMALL:
        g[n] = total[:, off:off + width]
        off += width
    g["conv_w"] = lax.dynamic_slice_in_dim(g["conv_w"].reshape(3, CONV_W), myq * (CONV_W // N_CHIPS),
                                           CONV_W // N_CHIPS, axis=1)
    g["final_norm_g"] = g["final_norm_g"].reshape(D)

    delta, new_m, new_v = {}, {}, {}
    for name in ["ada_w"] + [b[0] for b in BULK]:
        shape = wts[name].shape
        delta[name], new_m[name], new_v[name] = [
            r.reshape(shape) for r in _adamw(wts[name][0], g[name], mom[name][0], var[name][0], name="adamw_" + name)]
        g[name] = g[name].reshape(shape)
    smalls = ["ada_b"] + [n for n, _ in SMALL]

    def packed(d):
        flat = jnp.concatenate([d[n].reshape(1, -1) for n in smalls], axis=1)
        return jnp.pad(flat.reshape(-1, D), ((0, 1), (0, 0)))

    res = _adamw(packed(wts), packed(g), packed(mom), packed(var), name="adamw_small")
    off = 0
    for n in smalls:
        size = wts[n].size
        for d, r in zip((delta, new_m, new_v), res):
            d[n] = r.reshape(-1)[off:off + size].reshape(wts[n].shape)
        g[n] = g[n].reshape(wts[n].shape)
        off += size

    return (loss, grad_x[None], *[g[n] for n in WEIGHTS], *[delta[n] for n in WEIGHTS],
            *[new_m[n] for n in WEIGHTS], *[new_v[n] for n in WEIGHTS])
```

```python
import functools

import numpy as np
import jax
import jax.numpy as jnp
from jax import lax
from jax.experimental import pallas as pl
from jax.experimental.pallas import tpu as pltpu

F32 = jnp.float32
BF16 = jnp.bfloat16
MESH = pl.DeviceIdType.MESH

D = 1024
FF = 2816
CONV_W = 512
CONV_GROUP = 64
HEADS = 4
QK_NOPE = 128
QK_ROPE = 64
V_HEAD = 128
Q_LORA = 384
KV_LORA = 256
HEAD_SLOT = 256
IN_COLS = 3 * CONV_W + Q_LORA + KV_LORA + QK_ROPE
Z_COLS = 2304
EPS = 1e-6
ROPE_THETA = 10000.0
CHUNK = 64
ATT_SCALE = (QK_NOPE + QK_ROPE) ** -0.5
NEG = -1e30
EXP2_SCALE = ATT_SCALE * 1.4426950408889634
N_MOD = 9

LR, B1, B2, AEPS, WD, STEP = 0.001, 0.9, 0.999, 1e-08, 0.01, 10

N_CHIPS = 4
N_DEV = 8
VMEM_LIMIT = 56 << 20


def _params(sem, vmem=VMEM_LIMIT):
    return pltpu.CompilerParams(dimension_semantics=sem, vmem_limit_bytes=vmem)


def _rms(v):
    return lax.rsqrt(jnp.mean(v * v, axis=-1, keepdims=True) + EPS)


def _rsum8(v):
    t, n = v.shape
    return jnp.sum(v.reshape(t // 8, 8, n), axis=0)


def _all_rows(ref):
    ref[...] = jnp.broadcast_to(jnp.sum(ref[...], axis=0, keepdims=True), ref.shape)


def _gsum(v, gmat):
    hi = v.astype(BF16)
    lo = (v - hi.astype(F32)).astype(BF16)
    return (jnp.dot(hi, gmat, preferred_element_type=F32)
            + jnp.dot(lo, gmat, preferred_element_type=F32))


def _dot_nt(a, b):
    return lax.dot_general(a, b, (((1,), (1,)), ((), ())), preferred_element_type=F32)


def _silu_parts(a):
    sg = jax.nn.sigmoid(a)
    return sg, a * sg


def _rope(xr, cs, sn, lane):
    rh = jnp.where(lane < 32, -pltpu.roll(xr, 96, 1), pltpu.roll(xr, 32, 1))
    return xr * cs + rh * sn


def _rope_t(g, cs, sn, lane):
    y = g * sn
    rt = jnp.where(lane < 32, pltpu.roll(y, 96, 1), jnp.where(lane < 64, -pltpu.roll(y, 32, 1), 0.0))
    return g * cs + rt


def _row_tile(rows, pref, mult=8):
    t = min(rows, pref) // mult * mult
    while rows % t:
        t -= mult
    return t


def _place():
    return lax.axis_index("x"), lax.axis_index("y"), lax.axis_index("c")


ANY = pl.BlockSpec(memory_space=pl.ANY)


def _hosted_call(body, *, name, grid, in_specs, out_specs, out_shape, scratch_shapes, semantics, args, comm=None):
    n_in, n_out, n_scr = len(in_specs), len(out_specs), len(scratch_shapes)
    if comm is None:
        res = pl.pallas_call(body, name=name, grid=grid, in_specs=in_specs, out_specs=out_specs, out_shape=out_shape,
                             scratch_shapes=scratch_shapes, compiler_params=_params(semantics))(*args)
        return list(res), []
    n_ci, n_co = len(comm.inputs), len(comm.out_shapes)
    total = int(np.prod(grid))

    def hosted(*refs):
        ins, refs = refs[:n_in], refs[n_in:]
        cins, refs = refs[:n_ci], refs[n_ci:]
        outs, refs = refs[:n_out], refs[n_out:]
        couts, refs = refs[:n_co], refs[n_co:]
        scratch, sems = refs[:n_scr], refs[n_scr]
        step = pl.program_id(0)
        for ax in range(1, len(grid)):
            step = step * grid[ax] + pl.program_id(ax)

        @pl.when(step == 0)
        def _():
            comm.start(cins, couts, sems)

        body(*ins, *outs, *scratch)

        @pl.when(step == total - 1)
        def _():
            comm.finish(cins, couts, sems)

    res = pl.pallas_call(
        hosted, name=name, grid=grid, in_specs=list(in_specs) + [ANY] * n_ci,
        out_specs=list(out_specs) + [ANY] * n_co, out_shape=list(out_shape) + list(comm.out_shapes),
        scratch_shapes=list(scratch_shapes) + [pltpu.SemaphoreType.DMA((comm.n_sems,))],
        compiler_params=_params(("arbitrary",) * len(grid)))(*args, *comm.inputs)
    return list(res[:n_out]), list(res[n_out:])


def _run_comm(comm, *, name):
    n_ci = len(comm.inputs)

    def body(*refs):
        cins, couts, sems = refs[:n_ci], refs[n_ci:-1], refs[-1]
        comm.start(cins, couts, sems)
        comm.finish(cins, couts, sems)

    return pl.pallas_call(
        body, name=name, out_shape=list(comm.out_shapes), in_specs=[ANY] * n_ci,
        out_specs=[ANY] * len(comm.out_shapes), scratch_shapes=[pltpu.SemaphoreType.DMA((comm.n_sems,))],
    )(*comm.inputs)


class _Gather:
    def __init__(self, slabs):
        self.inputs = list(slabs)
        self.out_shapes = [jax.ShapeDtypeStruct((N_CHIPS,) + s.shape, s.dtype) for s in slabs]
        self.n_sems = 12 * len(slabs)

    @staticmethod
    def _copy(out, sems, base, k, chip, hc, to, src=None):
        H = out.shape[1] // 2
        half = out.at[2 * chip[0] + chip[1], pl.ds(hc * H, H), :]
        return pltpu.make_async_remote_copy(
            src_ref=half if src is None else src, dst_ref=half, send_sem=sems.at[base + k],
            recv_sem=sems.at[base + 6 + k], device_id=to, device_id_type=MESH)

    def _firsts(self, src, out, sems, base):
        x, y, c = _place()
        H = src.shape[0] // 2
        chips = [(1 - x, y), (x, 1 - y), (1 - x, 1 - y)]
        return [self._copy(out, sems, base, j, (x, y), c, (*chip, c), src=src.at[pl.ds(c * H, H), :])
                for j, chip in enumerate(chips)]

    def start(self, ins, outs, sems):
        for i, (src, out) in enumerate(zip(ins, outs)):
            for cp in self._firsts(src, out, sems, 12 * i):
                cp.start()

    def finish(self, ins, outs, sems):
        x, y, c = _place()
        chips = [(1 - x, y), (x, 1 - y), (1 - x, 1 - y)]
        passed = []
        for i, out in enumerate(outs):
            for j, chip in enumerate(chips):
                self._copy(out, sems, 12 * i, j, chip, c, (x, y, c)).wait_recv()
                cp = self._copy(out, sems, 12 * i, 3 + j, chip, c, (x, y, 1 - c))
                cp.start()
                passed.append(cp)
        for i, out in enumerate(outs):
            for j, chip in enumerate(chips):
                self._copy(out, sems, 12 * i, 3 + j, chip, 1 - c, (x, y, c)).wait_recv()
        for cp in passed:
            cp.wait_send()
        for i, (src, out) in enumerate(zip(ins, outs)):
            for cp in self._firsts(src, out, sems, 12 * i):
                cp.wait_send()


class _PairExchange:
    def __init__(self, arrays):
        self.inputs = list(arrays)
        self.out_shapes = [jax.ShapeDtypeStruct((N_CHIPS, a.shape[1] // 2, a.shape[2]), a.dtype) for a in arrays]
        self.n_sems = 2 * len(arrays)

    def _copies(self, ins, outs, sems):
        x, y, c = _place()
        return [pltpu.make_async_remote_copy(
            src_ref=g.at[:, pl.ds((1 - c) * t.shape[1], t.shape[1]), :], dst_ref=t, send_sem=sems.at[2 * i],
            recv_sem=sems.at[2 * i + 1], device_id=(x, y, 1 - c), device_id_type=MESH)
            for i, (g, t) in enumerate(zip(ins, outs))]

    def start(self, ins, outs, sems):
        for cp in self._copies(ins, outs, sems):
            cp.start()

    def finish(self, ins, outs, sems):
        for cp in self._copies(ins, outs, sems):
            cp.wait()


class _ChipExchange:
    def __init__(self, arrays):
        self.inputs = list(arrays)
        self.out_shapes = [jax.ShapeDtypeStruct(a.shape, a.dtype) for a in arrays]
        self.n_sems = 6 * len(arrays)

    def _copies(self, p, t, sems, base):
        x, y, c = _place()
        myq = 2 * x + y
        chips = [(1 - x, y), (x, 1 - y), (1 - x, 1 - y)]
        sends = [pltpu.make_async_remote_copy(
            src_ref=p.at[2 * chip[0] + chip[1]], dst_ref=t.at[myq], send_sem=sems.at[base + j],
            recv_sem=sems.at[base + 3 + j], device_id=(*chip, c), device_id_type=MESH) for j, chip in enumerate(chips)]
        lands = [pltpu.make_async_remote_copy(
            src_ref=t.at[2 * chip[0] + chip[1]], dst_ref=t.at[2 * chip[0] + chip[1]], send_sem=sems.at[base + j],
            recv_sem=sems.at[base + 3 + j], device_id=(*chip, c), device_id_type=MESH) for j, chip in enumerate(chips)]
        return sends, lands

    def start(self, ins, outs, sems):
        for i, (p, t) in enumerate(zip(ins, outs)):
            for cp in self._copies(p, t, sems, 6 * i)[0]:
                cp.start()

    def finish(self, ins, outs, sems):
        for i, (p, t) in enumerate(zip(ins, outs)):
            sends, lands = self._copies(p, t, sems, 6 * i)
            for cp in lands:
                cp.wait_recv()
            for cp in sends:
                cp.wait_send()


class _PairShare:
    def __init__(self, arrays):
        self.inputs = list(arrays)
        self.out_shapes = [jax.ShapeDtypeStruct(a.shape, a.dtype) for a in arrays]
        self.n_sems = 2 * len(arrays)

    def _copies(self, ins, outs, sems):
        x, y, c = _place()
        return [pltpu.make_async_remote_copy(
            src_ref=r, dst_ref=o, send_sem=sems.at[2 * i], recv_sem=sems.at[2 * i + 1],
            device_id=(x, y, 1 - c), device_id_type=MESH) for i, (r, o) in enumerate(zip(ins, outs))]

    def start(self, ins, outs, sems):
        for cp in self._copies(ins, outs, sems):
            cp.start()

    def finish(self, ins, outs, sems):
        for cp in self._copies(ins, outs, sems):
            cp.wait()


def _ffn_up(x, ng, sh, sc, w1, w3, *, name, comm=None):
    S = x.shape[0]
    tm, tn = _row_tile(S, 512), FF // 2

    def body(x_ref, g_ref, sh_ref, sc_ref, w1_ref, w3_ref, h_ref, a_ref, b_ref, u_ref, hs):
        @pl.when(pl.program_id(1) == 0)
        def _():
            xv = x_ref[...]
            h = ((xv * _rms(xv)) * g_ref[...]) * (1.0 + sc_ref[...]) + sh_ref[...]
            hb = h.astype(BF16)
            hs[...] = hb
            h_ref[...] = hb

        h = hs[...]
        a = jnp.dot(h, w1_ref[...], preferred_element_type=F32)
        b = jnp.dot(h, w3_ref[...], preferred_element_type=F32)
        _, sa = _silu_parts(a)
        a_ref[...] = a.astype(BF16)
        b_ref[...] = b.astype(BF16)
        u_ref[...] = (sa * b).astype(BF16)

    row = pl.BlockSpec((tm, D), lambda i, j: (i, 0))
    vec = pl.BlockSpec((1, D), lambda i, j: (0, 0))
    wsp = pl.BlockSpec((D, tn), lambda i, j: (0, j))
    osp = pl.BlockSpec((tm, tn), lambda i, j: (i, j))
    return _hosted_call(
        body, name=name, grid=(S // tm, FF // tn),
        in_specs=[row, vec, vec, vec, wsp, wsp],
        out_specs=[row, osp, osp, osp],
        out_shape=[jax.ShapeDtypeStruct((S, D), BF16)] + [jax.ShapeDtypeStruct((S, FF), BF16)] * 3,
        scratch_shapes=[pltpu.VMEM((tm, D), BF16)],
        semantics=("parallel", "arbitrary"), args=(x, ng, sh, sc, w1, w3), comm=comm)


def _ffn_down(u, w2, x, gate, *, name):
    S = x.shape[0]
    tm = _row_tile(S, 512)

    def body(u_ref, w2_ref, x_ref, g_ref, xo_ref, f_ref):
        f = jnp.dot(u_ref[...], w2_ref[...], preferred_element_type=F32)
        xo_ref[...] = x_ref[...] + (0.5 * g_ref[...]) * f
        f_ref[...] = f.astype(BF16)

    return pl.pallas_call(
        body, name=name, grid=(S // tm,),
        in_specs=[pl.BlockSpec((tm, FF), lambda i: (i, 0)), pl.BlockSpec((FF, D), lambda i: (0, 0)),
                  pl.BlockSpec((tm, D), lambda i: (i, 0)), pl.BlockSpec((1, D), lambda i: (0, 0))],
        out_specs=[pl.BlockSpec((tm, D), lambda i: (i, 0))] * 2,
        out_shape=[jax.ShapeDtypeStruct((S, D), F32), jax.ShapeDtypeStruct((S, D), BF16)],
        compiler_params=_params(("parallel",)),
    )(u, w2, x, gate)


def _ffn_bwd_du(dx, gate, f, w2, a, b, *, name, comm=None):
    S = dx.shape[0]
    tm, tn = _row_tile(S, 256), FF // 2
    n_i = S // tm

    def body(dx_ref, g_ref, f_ref, w_ref, a_ref, b_ref, df_ref, da_ref, db_ref, dg_ref, dfs):
        i, j = pl.program_id(0), pl.program_id(1)

        @pl.when((i == 0) & (j == 0))
        def _():
            dg_ref[...] = jnp.zeros_like(dg_ref)

        @pl.when(j == 0)
        def _():
            dxv = dx_ref[...]
            dfb = (dxv * (0.5 * g_ref[...])).astype(BF16)
            dfs[...] = dfb
            df_ref[...] = dfb
            dg_ref[...] += _rsum8(dxv * (0.5 * f_ref[...].astype(F32)))

        du = _dot_nt(dfs[...], w_ref[...])
        av = a_ref[...].astype(F32)
        sg, sa = _silu_parts(av)
        da_ref[...] = (du * b_ref[...].astype(F32) * (sg * (1.0 + av * (1.0 - sg)))).astype(BF16)
        db_ref[...] = (du * sa).astype(BF16)

        @pl.when((i == n_i - 1) & (j == FF // tn - 1))
        def _():
            _all_rows(dg_ref)

    row = pl.BlockSpec((tm, D), lambda i, j: (i, 0))
    blk = pl.BlockSpec((tm, tn), lambda i, j: (i, j))
    return _hosted_call(
        body, name=name, grid=(n_i, FF // tn),
        in_specs=[row, pl.BlockSpec((1, D), lambda i, j: (0, 0)), row,
                  pl.BlockSpec((tn, D), lambda i, j: (j, 0)), blk, blk],
        out_specs=[row, blk, blk, pl.BlockSpec((8, D), lambda i, j: (0, 0))],
        out_shape=[jax.ShapeDtypeStruct((S, D), BF16), jax.ShapeDtypeStruct((S, FF), BF16),
                   jax.ShapeDtypeStruct((S, FF), BF16), jax.ShapeDtypeStruct((8, D), F32)],
        scratch_shapes=[pltpu.VMEM((tm, D), BF16)],
        semantics=("arbitrary", "arbitrary"), args=(dx, gate, f, w2, a, b), comm=comm)


def _tn_matmul(a, b, *, tm, tn, name, comm=None):
    S, M = a.shape
    N = b.shape[1]
    ts = _row_tile(S, 1024)
    ns = S // ts

    def body(a_ref, b_ref, o_ref, acc):
        s = pl.program_id(2)
        p = lax.dot_general(a_ref[...], b_ref[...], (((0,), (0,)), ((), ())), preferred_element_type=F32)

        @pl.when(s == 0)
        def _():
            acc[...] = p

        @pl.when(s > 0)
        def _():
            acc[...] += p

        @pl.when(s == ns - 1)
        def _():
            o_ref[...] = acc[...]

    (out,), couts = _hosted_call(
        body, name=name, grid=(M // tm, N // tn, ns),
        in_specs=[pl.BlockSpec((ts, tm), lambda i, j, s: (s, i)), pl.BlockSpec((ts, tn), lambda i, j, s: (s, j))],
        out_specs=[pl.BlockSpec((tm, tn), lambda i, j, s: (i, j))],
        out_shape=[jax.ShapeDtypeStruct((M, N), F32)],
        scratch_shapes=[pltpu.VMEM((tm, tn), F32)],
        semantics=("parallel", "parallel", "arbitrary"), args=(a, b), comm=comm)
    return out if comm is None else (out, couts)


def _dh_normbwd(pairs, x, ng, sc, dx_next, *, name, comm=None):
    S = x.shape[0]
    tm = _row_tile(S, 256)
    n_i = S // tm
    n_p = len(pairs)

    def body(*refs):
        a_refs, w_refs = refs[:n_p], refs[n_p:2 * n_p]
        x_ref, g_ref, sc_ref, dxn_ref, dx_ref, p_ref = refs[2 * n_p:]
        i = pl.program_id(0)
        dh = _dot_nt(a_refs[0][...], w_refs[0][...])
        for k in range(1, n_p):
            dh = dh + _dot_nt(a_refs[k][...], w_refs[k][...])
        xv = x_ref[...]
        r = _rms(xv)
        xh = xv * r
        g = g_ref[...]
        dn = dh * (1.0 + sc_ref[...])
        dy = dn * g
        dx_ref[...] = dxn_ref[...] + r * (dy - xh * jnp.mean(dy * xh, axis=-1, keepdims=True))

        @pl.when(i == 0)
        def _():
            p_ref[...] = jnp.zeros_like(p_ref)

        p_ref[:, 0:D] += _rsum8(dh * (xh * g))
        p_ref[:, D:2 * D] += _rsum8(dh)
        p_ref[:, 2 * D:3 * D] += _rsum8(dn * xh)

        @pl.when(i == n_i - 1)
        def _():
            _all_rows(p_ref)

    row = pl.BlockSpec((tm, D), lambda i: (i, 0))
    vec = pl.BlockSpec((1, D), lambda i: (0, 0))
    in_specs = ([pl.BlockSpec((tm, a.shape[1]), lambda i: (i, 0)) for a, _ in pairs]
                + [pl.BlockSpec(w.shape, lambda i: (0, 0)) for _, w in pairs] + [row, vec, vec, row])
    return _hosted_call(
        body, name=name, grid=(n_i,), in_specs=in_specs,
        out_specs=[row, pl.BlockSpec((8, 3 * D), lambda i: (0, 0))],
        out_shape=[jax.ShapeDtypeStruct((S, D), F32), jax.ShapeDtypeStruct((8, 3 * D), F32)],
        scratch_shapes=[], semantics=("arbitrary",),
        args=(*[a for a, _ in pairs], *[w for _, w in pairs], x, ng, sc, dx_next), comm=comm)


def _final_loss(x3, gfin, tgt):
    S = x3.shape[0]
    tm = _row_tile(S, 512)
    n_i = S // tm

    def body(x_ref, g_ref, t_ref, dx_ref, dg_ref, loss_ref, lacc):
        i = pl.program_id(0)
        xv = x_ref[...]
        r = _rms(xv)
        xh = xv * r
        g = g_ref[...]
        e = xh * g - t_ref[...]
        dout = e * (1.0 / D)
        dy = dout * g
        dx_ref[...] = r * (dy - xh * jnp.mean(dy * xh, axis=-1, keepdims=True))

        @pl.when(i == 0)
        def _():
            dg_ref[...] = jnp.zeros_like(dg_ref)
            lacc[...] = jnp.zeros_like(lacc)

        dg_ref[...] += _rsum8(dout * xh)
        lacc[...] += _rsum8(e * e)

        @pl.when(i == n_i - 1)
        def _():
            _all_rows(dg_ref)
            tot = jnp.sum(jnp.sum(lacc[...], axis=0, keepdims=True), axis=1, keepdims=True)
            loss_ref[...] = jnp.broadcast_to(tot * (0.5 / D), loss_ref.shape)

    row = pl.BlockSpec((tm, D), lambda i: (i, 0))
    return pl.pallas_call(
        body, name="final_loss", grid=(n_i,),
        in_specs=[row, pl.BlockSpec((1, D), lambda i: (0, 0)), row],
        out_specs=[row, pl.BlockSpec((8, D), lambda i: (0, 0)), pl.BlockSpec((8, 128), lambda i: (0, 0))],
        out_shape=[jax.ShapeDtypeStruct((S, D), F32), jax.ShapeDtypeStruct((8, D), F32),
                   jax.ShapeDtypeStruct((8, 128), F32)],
        scratch_shapes=[pltpu.VMEM((8, D), F32)],
        compiler_params=_params(("arbitrary",)),
    )(x3, gfin, tgt)


def _mix_in(x, ng, sh, sc, w_in):
    S = x.shape[0]
    tm = _row_tile(S, 512)

    def body(x_ref, g_ref, sh_ref, sc_ref, w_ref, h_ref, z_ref):
        xv = x_ref[...]
        hb = (((xv * _rms(xv)) * g_ref[...]) * (1.0 + sc_ref[...]) + sh_ref[...]).astype(BF16)
        h_ref[...] = hb
        z_ref[...] = jnp.dot(hb, w_ref[...], preferred_element_type=F32)

    row = pl.BlockSpec((tm, D), lambda i: (i, 0))
    vec = pl.BlockSpec((1, D), lambda i: (0, 0))
    return pl.pallas_call(
        body, name="mix_in", grid=(S // tm,),
        in_specs=[row, vec, vec, vec, pl.BlockSpec((D, Z_COLS), lambda i: (0, 0))],
        out_specs=[row, pl.BlockSpec((tm, Z_COLS), lambda i: (i, 0))],
        out_shape=[jax.ShapeDtypeStruct((S, D), BF16), jax.ShapeDtypeStruct((S, Z_COLS), F32)],
        compiler_params=_params(("parallel",)),
    )(x, ng, sh, sc, w_in)


def _conv_taps(u, halo, rows):
    u1 = jnp.where(rows == 0, halo[7:8, :], pltpu.roll(u, 1, 0))
    u2 = jnp.where(rows == 0, halo[6:7, :], jnp.where(rows == 1, halo[7:8, :], pltpu.roll(u, 2, 0)))
    return u1, u2


def _mix_mid(z, conv_w, gq, gkv, wuq, wukv, cs, sn):
    S = z.shape[0]
    tm = _row_tile(S, 512)
    hb = tm // 8

    def body(z_ref, zh_ref, cw_ref, gq_ref, gkv_ref, wuq_ref, wukv_ref, cs_ref, sn_ref,
             ya_ref, q_ref, k_ref, v_ref, cqn_ref, ckvn_ref):
        i = pl.program_id(0)
        xb = z_ref[:, 0:CONV_W]
        u = z_ref[:, CONV_W:2 * CONV_W] * z_ref[:, 2 * CONV_W:3 * CONV_W]
        halo = zh_ref[:, CONV_W:2 * CONV_W] * zh_ref[:, 2 * CONV_W:3 * CONV_W]
        halo = jnp.where(i > 0, halo, 0.0)
        rows = lax.broadcasted_iota(jnp.int32, (tm, CONV_W), 0)
        u1, u2 = _conv_taps(u, halo, rows)
        y = cw_ref[0:1, :] * u2 + cw_ref[1:2, :] * u1 + cw_ref[2:3, :] * u
        ya_ref[...] = xb * y

        lane = lax.broadcasted_iota(jnp.int32, (tm, 128), 1)
        cs_v, sn_v = cs_ref[...], sn_ref[...]
        cq = z_ref[:, 3 * CONV_W:3 * CONV_W + Q_LORA]
        cqn = ((cq * _rms(cq)) * gq_ref[...]).astype(BF16)
        cqn_ref[...] = cqn
        q = jnp.dot(cqn, wuq_ref[...], preferred_element_type=F32)
        for h in range(HEADS):
            o = h * HEAD_SLOT
            q_ref[:, o:o + 128] = q[:, o:o + 128].astype(BF16)
            q_ref[:, o + 128:o + 256] = _rope(q[:, o + 128:o + 256], cs_v, sn_v, lane).astype(BF16)

        c0 = 3 * CONV_W + Q_LORA
        ckv = z_ref[:, c0:c0 + KV_LORA]
        ckvn = ((ckv * _rms(ckv)) * gkv_ref[...]).astype(BF16)
        ckvn_ref[...] = ckvn
        kv = jnp.dot(ckvn, wukv_ref[...], preferred_element_type=F32)
        krot = _rope(z_ref[:, c0 + KV_LORA:Z_COLS], cs_v, sn_v, lane).astype(BF16)
        for h in range(HEADS):
            o = h * HEAD_SLOT
            k_ref[:, o:o + 128] = kv[:, h * 128:(h + 1) * 128].astype(BF16)
            k_ref[:, o + 128:o + 256] = krot
        v_ref[...] = kv[:, HEADS * 128:].astype(BF16)

    def rows_of(n):
        return pl.BlockSpec((tm, n), lambda i: (i, 0))

    def whole(shape):
        return pl.BlockSpec(shape, lambda i: (0, 0))

    return pl.pallas_call(
        body, name="mix_mid", grid=(S // tm,),
        in_specs=[rows_of(Z_COLS), pl.BlockSpec((8, Z_COLS), lambda i: (jnp.maximum(i * hb - 1, 0), 0)),
                  whole((8, CONV_W)), whole((1, Q_LORA)), whole((1, KV_LORA)),
                  whole((Q_LORA, HEADS * HEAD_SLOT)), whole((KV_LORA, 2 * HEADS * 128)),
                  rows_of(128), rows_of(128)],
        out_specs=[rows_of(CONV_W), rows_of(HEADS * HEAD_SLOT), rows_of(HEADS * HEAD_SLOT), rows_of(HEADS * V_HEAD),
                   rows_of(Q_LORA), rows_of(KV_LORA)],
        out_shape=[jax.ShapeDtypeStruct((S, CONV_W), F32), jax.ShapeDtypeStruct((S, HEADS * HEAD_SLOT), BF16),
                   jax.ShapeDtypeStruct((S, HEADS * HEAD_SLOT), BF16), jax.ShapeDtypeStruct((S, HEADS * V_HEAD), BF16),
                   jax.ShapeDtypeStruct((S, Q_LORA), BF16), jax.ShapeDtypeStruct((S, KV_LORA), BF16)],
        compiler_params=_params(("parallel",)),
    )(z, z, conv_w, gq, gkv, wuq, wukv, cs, sn)


def _att_blocks(S):
    bk = min(512, max(S // 4, 128))
    return 2 * bk, bk


def _pair_tables(S, k_major):
    bq, bk = _att_blocks(S)
    nq, nk = S // bq, S // bk
    vis = lambda qi, ki: ki * bk < (qi + 1) * bq
    if k_major:
        pairs = [(qi, ki) for ki in range(nk) for qi in range(nq) if vis(qi, ki)]
    else:
        pairs = [(qi, ki) for qi in range(nq) for ki in range(nk) if vis(qi, ki)]
    cols = [[p[0] for p in pairs], [p[1] for p in pairs], [int((p[1] + 1) * bk > p[0] * bq) for p in pairs]]
    return [jnp.asarray(np.array(c, np.int32)) for c in cols], len(pairs)


def _chunk_mask(qi, ki, bq, bk):
    r = (qi * bq + lax.broadcasted_iota(jnp.int32, (bq, bk), 0)) // CHUNK
    c = (ki * bk + lax.broadcasted_iota(jnp.int32, (bq, bk), 1)) // CHUNK
    return c <= r


def _attention(q, k, v):
    S = q.shape[0]
    bq, bk = _att_blocks(S)
    last_k = bq // bk - 1
    tables, n_pairs = _pair_tables(S, k_major=False)

    def body(qi_ref, ki_ref, mk_ref, q_ref, k_ref, v_ref, o_ref, lse_ref, m_s, l_s, acc_s):
        p_id = pl.program_id(1)
        qi, ki = qi_ref[p_id], ki_ref[p_id]

        @pl.when(ki == 0)
        def _():
            m_s[...] = jnp.full_like(m_s, NEG)
            l_s[...] = jnp.zeros_like(l_s)
            acc_s[...] = jnp.zeros_like(acc_s)

        def update(masked):
            s = lax.dot_general(q_ref[...], k_ref[...], (((1,), (1,)), ((), ())), preferred_element_type=F32)
            if masked:
                s = jnp.where(_chunk_mask(qi, ki, bq, bk), s, NEG)
            m_prev = m_s[...]
            m_new = jnp.maximum(m_prev, jnp.max(s, axis=1, keepdims=True))
            alpha = jnp.exp2((m_prev - m_new) * EXP2_SCALE)
            p = jnp.exp2((s - jnp.tile(m_new, (1, bk // 128))) * EXP2_SCALE)
            l_s[...] = alpha * l_s[...] + jnp.sum(p, axis=1, keepdims=True)
            acc_s[...] = alpha * acc_s[...] + jnp.dot(p.astype(BF16), v_ref[...], preferred_element_type=F32)
            m_s[...] = m_new

        @pl.when(mk_ref[p_id] == 0)
        def _():
            update(False)

        @pl.when(mk_ref[p_id] == 1)
        def _():
            update(True)

        @pl.when(ki == qi * (last_k + 1) + last_k)
        def _():
            l = l_s[...]
            o_ref[...] = acc_s[...] / l
            lse_ref[...] = m_s[...] * EXP2_SCALE + jnp.log2(l)

    grid_spec = pltpu.PrefetchScalarGridSpec(
        num_scalar_prefetch=3, grid=(HEADS, n_pairs),
        in_specs=[pl.BlockSpec((bq, HEAD_SLOT), lambda h, p, qt, kt, mt: (qt[p], h)),
                  pl.BlockSpec((bk, HEAD_SLOT), lambda h, p, qt, kt, mt: (kt[p], h)),
                  pl.BlockSpec((bk, V_HEAD), lambda h, p, qt, kt, mt: (kt[p], h))],
        out_specs=[pl.BlockSpec((bq, V_HEAD), lambda h, p, qt, kt, mt: (qt[p], h))] * 2,
        scratch_shapes=[pltpu.VMEM((bq, V_HEAD), F32)] * 3)
    return pl.pallas_call(
        body, name="attention", grid_spec=grid_spec,
        out_shape=[jax.ShapeDtypeStruct((S, HEADS * V_HEAD), F32)] * 2,
        compiler_params=_params(("arbitrary", "arbitrary")),
    )(*tables, q, k, v)


def _attention_bwd(q, k, v, do, lse2, delta):
    S = q.shape[0]
    bq, bk = _att_blocks(S)
    nq = S // bq
    tables, n_pairs = _pair_tables(S, k_major=True)

    def body(qi_ref, ki_ref, mk_ref, q_ref, k_ref, v_ref, do_ref, lse_ref, dl_ref, dq_hbm, dk_ref, dv_ref,
             dq_s, dk_s, dv_s, sem):
        head, p_id = pl.program_id(0), pl.program_id(1)
        qi, ki = qi_ref[p_id], ki_ref[p_id]
        rows = pl.ds(pl.multiple_of(qi * bq, bq), bq)

        @pl.when(qi * bq <= ki * bk)
        def _():
            dk_s[...] = jnp.zeros_like(dk_s)
            dv_s[...] = jnp.zeros_like(dv_s)

        def update(masked):
            qv, kv, dov = q_ref[...], k_ref[...], do_ref[...]
            s = lax.dot_general(qv, kv, (((1,), (1,)), ((), ())), preferred_element_type=F32)
            dp = lax.dot_general(dov, v_ref[...], (((1,), (1,)), ((), ())), preferred_element_type=F32)
            if masked:
                s = jnp.where(_chunk_mask(qi, ki, bq, bk), s, NEG)
            p = jnp.exp2(s * EXP2_SCALE - jnp.tile(lse_ref[...], (1, bk // 128)))
            dv_s[...] += lax.dot_general(p.astype(BF16), dov, (((0,), (0,)), ((), ())), preferred_element_type=F32)
            ds = (p * (dp - jnp.tile(dl_ref[...], (1, bk // 128)))).astype(BF16)
            dk_s[...] += lax.dot_general(ds, qv, (((0,), (0,)), ((), ())), preferred_element_type=F32)
            dq = jnp.dot(ds, kv, preferred_element_type=F32)

            @pl.when(ki == 0)
            def _():
                dq_s[rows, :] = dq

            @pl.when(ki > 0)
            def _():
                dq_s[rows, :] += dq

        @pl.when(mk_ref[p_id] == 0)
        def _():
            update(False)

        @pl.when(mk_ref[p_id] == 1)
        def _():
            update(True)

        @pl.when(qi == nq - 1)
        def _():
            dk_ref[...] = dk_s[...] * ATT_SCALE
            dv_ref[...] = dv_s[...]

        @pl.when(p_id == n_pairs - 1)
        def _():
            dq_s[...] = dq_s[...] * ATT_SCALE
            out = pltpu.make_async_copy(
                dq_s, dq_hbm.at[:, pl.ds(pl.multiple_of(head * HEAD_SLOT, HEAD_SLOT), HEAD_SLOT)], sem)
            out.start()
            out.wait()

    grid_spec = pltpu.PrefetchScalarGridSpec(
        num_scalar_prefetch=3, grid=(HEADS, n_pairs),
        in_specs=[pl.BlockSpec((bq, HEAD_SLOT), lambda h, p, qt, kt, mt: (qt[p], h)),
                  pl.BlockSpec((bk, HEAD_SLOT), lambda h, p, qt, kt, mt: (kt[p], h)),
                  pl.BlockSpec((bk, V_HEAD), lambda h, p, qt, kt, mt: (kt[p], h)),
                  pl.BlockSpec((bq, V_HEAD), lambda h, p, qt, kt, mt: (qt[p], h)),
                  pl.BlockSpec((bq, V_HEAD), lambda h, p, qt, kt, mt: (qt[p], h)),
                  pl.BlockSpec((bq, V_HEAD), lambda h, p, qt, kt, mt: (qt[p], h))],
        out_specs=[pl.BlockSpec(memory_space=pl.ANY),
                   pl.BlockSpec((bk, HEAD_SLOT), lambda h, p, qt, kt, mt: (kt[p], h)),
                   pl.BlockSpec((bk, V_HEAD), lambda h, p, qt, kt, mt: (kt[p], h))],
        scratch_shapes=[pltpu.VMEM((S, HEAD_SLOT), F32), pltpu.VMEM((bk, HEAD_SLOT), F32),
                        pltpu.VMEM((bk, V_HEAD), F32), pltpu.SemaphoreType.DMA])
    return pl.pallas_call(
        body, name="attention_bwd", grid_spec=grid_spec,
        out_shape=[jax.ShapeDtypeStruct((S, HEADS * HEAD_SLOT), F32), jax.ShapeDtypeStruct((S, HEADS * HEAD_SLOT), F32),
                   jax.ShapeDtypeStruct((S, HEADS * V_HEAD), F32)],
        compiler_params=_params(("arbitrary", "arbitrary")),
    )(*tables, q, k, v, do, lse2, delta)


def _group_mats():
    def blockdiag(n, g):
        idx = np.arange(n) // g
        return jnp.asarray((idx[:, None] == idx[None, :]).astype(np.float32), dtype=BF16)
    return blockdiag(CONV_W, CONV_GROUP), blockdiag(HEADS * V_HEAD, V_HEAD)


def _mix_out(ya, o, gout, w_out, x, gate, ga, gb):
    S = x.shape[0]
    tm = _row_tile(S, 512)

    def body(ya_ref, o_ref, go_ref, w_ref, x_ref, g_ref, ga_ref, gb_ref, xo_ref, yn_ref, yo_ref):
        yav, ov = ya_ref[...], o_ref[...]
        ra = lax.rsqrt(_gsum(yav * yav, ga_ref[...]) * (1.0 / CONV_GROUP) + EPS)
        rb = lax.rsqrt(_gsum(ov * ov, gb_ref[...]) * (1.0 / V_HEAD) + EPS)
        na = ((yav * ra) * go_ref[:, 0:CONV_W]).astype(BF16)
        nb = ((ov * rb) * go_ref[:, CONV_W:]).astype(BF16)
        yn_ref[:, 0:CONV_W] = na
        yn_ref[:, CONV_W:] = nb
        yo = (jnp.dot(na, w_ref[0:CONV_W, :], preferred_element_type=F32)
              + jnp.dot(nb, w_ref[CONV_W:, :], preferred_element_type=F32))
        xo_ref[...] = x_ref[...] + g_ref[...] * yo
        yo_ref[...] = yo.astype(BF16)

    row = pl.BlockSpec((tm, D), lambda i: (i, 0))
    half = pl.BlockSpec((tm, CONV_W), lambda i: (i, 0))
    vec = pl.BlockSpec((1, D), lambda i: (0, 0))
    sq = pl.BlockSpec((CONV_W, CONV_W), lambda i: (0, 0))
    return pl.pallas_call(
        body, name="mix_out", grid=(S // tm,),
        in_specs=[half, half, vec, pl.BlockSpec((D, D), lambda i: (0, 0)), row, vec, sq, sq],
        out_specs=[row, row, row],
        out_shape=[jax.ShapeDtypeStruct((S, D), F32), jax.ShapeDtypeStruct((S, D), BF16),
                   jax.ShapeDtypeStruct((S, D), BF16)],
        compiler_params=_params(("parallel",)),
    )(ya, o, gout, w_out, x, gate, ga, gb)


def _mix_out_bwd(dx, gate, yo, w_out, ya, o, gout, ga, gb, comm=None):
    S = dx.shape[0]
    tm = _row_tile(S, 256)
    n_i = S // tm

    def norm_bwd(v, dn, gain, gmat, inv_n):
        r = lax.rsqrt(_gsum(v * v, gmat) * inv_n + EPS)
        vh = v * r
        dy = dn * gain
        return r * (dy - vh * (_gsum(dy * vh, gmat) * inv_n)), dn * vh

    def body(dx_ref, g_ref, yo_ref, w_ref, ya_ref, o_ref, go_ref, ga_ref, gb_ref,
             dyo_ref, dya_ref, do_ref, dl_ref, p_ref):
        i = pl.program_id(0)
        dxv = dx_ref[...]
        dyo = (dxv * g_ref[...]).astype(BF16)
        dyo_ref[...] = dyo
        dyn = _dot_nt(dyo, w_ref[...])
        dya, dga = norm_bwd(ya_ref[...], dyn[:, 0:CONV_W], go_ref[:, 0:CONV_W], ga_ref[...], 1.0 / CONV_GROUP)
        ov = o_ref[...]
        do, dgb = norm_bwd(ov, dyn[:, CONV_W:], go_ref[:, CONV_W:], gb_ref[...], 1.0 / V_HEAD)
        dya_ref[...] = dya
        do_ref[...] = do.astype(BF16)
        dl_ref[...] = _gsum(do * ov, gb_ref[...])

        @pl.when(i == 0)
        def _():
            p_ref[...] = jnp.zeros_like(p_ref)

        p_ref[:, 0:D] += _rsum8(dxv * yo_ref[...].astype(F32))
        p_ref[:, D:D + CONV_W] += _rsum8(dga)
        p_ref[:, D + CONV_W:2 * D] += _rsum8(dgb)

        @pl.when(i == n_i - 1)
        def _():
            _all_rows(p_ref)

    row = pl.BlockSpec((tm, D), lambda i: (i, 0))
    half = pl.BlockSpec((tm, CONV_W), lambda i: (i, 0))
    vec = pl.BlockSpec((1, D), lambda i: (0, 0))
    sq = pl.BlockSpec((CONV_W, CONV_W), lambda i: (0, 0))
    return _hosted_call(
        body, name="mix_out_bwd", grid=(n_i,),
        in_specs=[row, vec, row, pl.BlockSpec((D, D), lambda i: (0, 0)), half, half, vec, sq, sq],
        out_specs=[row, half, half, half, pl.BlockSpec((8, 2 * D), lambda i: (0, 0))],
        out_shape=[jax.ShapeDtypeStruct((S, D), BF16), jax.ShapeDtypeStruct((S, CONV_W), F32),
                   jax.ShapeDtypeStruct((S, CONV_W), BF16), jax.ShapeDtypeStruct((S, CONV_W), F32),
                   jax.ShapeDtypeStruct((8, 2 * D), F32)],
        scratch_shapes=[], semantics=("arbitrary",), args=(dx, gate, yo, w_out, ya, o, gout, ga, gb), comm=comm)


MID_SUMS = 3 * CONV_W + Q_LORA + KV_LORA


def _mix_mid_bwd(z, dya, conv_w, gq, gkv, wuq, wukv, cs, sn, dq, dk, dv, comm=None):
    S = z.shape[0]
    tm = _row_tile(S, 256)
    n_i = S // tm
    hb = tm // 8
    last_blk = S // 8 - 1

    def latent_bwd(cv, dcn, gain):
        r = _rms(cv)
        ch = cv * r
        dy = dcn * gain
        return r * (dy - ch * jnp.mean(dy * ch, axis=-1, keepdims=True)), dcn * ch

    def body(z_ref, zp_ref, zn_ref, dya_ref, dyan_ref, cw_ref, gq_ref, gkv_ref, wuq_ref, wukv_ref, cs_ref, sn_ref,
             dq_ref, dk_ref, dv_ref, dz_ref, dqf_ref, dkvf_ref, p_ref):
        i = pl.program_id(0)
        xb, xc, xu = z_ref[:, 0:CONV_W], z_ref[:, CONV_W:2 * CONV_W], z_ref[:, 2 * CONV_W:3 * CONV_W]
        u = xc * xu
        halo = jnp.where(i > 0, zp_ref[:, CONV_W:2 * CONV_W] * zp_ref[:, 2 * CONV_W:3 * CONV_W], 0.0)
        rows = lax.broadcasted_iota(jnp.int32, (tm, CONV_W), 0)
        u1, u2 = _conv_taps(u, halo, rows)
        w0, w1, w2 = cw_ref[0:1, :], cw_ref[1:2, :], cw_ref[2:3, :]
        y = w0 * u2 + w1 * u1 + w2 * u
        dyav = dya_ref[...]
        dy = dyav * xb
        nxt = jnp.where(i < n_i - 1, dyan_ref[...] * zn_ref[:, 0:CONV_W], 0.0)
        dy1 = jnp.where(rows == tm - 1, nxt[0:1, :], pltpu.roll(dy, tm - 1, 0))
        dy2 = jnp.where(rows == tm - 1, nxt[1:2, :], jnp.where(rows == tm - 2, nxt[0:1, :], pltpu.roll(dy, tm - 2, 0)))
        du = w2 * dy + w1 * dy1 + w0 * dy2
        dz_ref[:, 0:CONV_W] = (dyav * y).astype(BF16)
        dz_ref[:, CONV_W:2 * CONV_W] = (du * xu).astype(BF16)
        dz_ref[:, 2 * CONV_W:3 * CONV_W] = (du * xc).astype(BF16)

        lane = lax.broadcasted_iota(jnp.int32, (tm, 128), 1)
        cs_v, sn_v = cs_ref[...], sn_ref[...]
        dkr = jnp.zeros((tm, 128), F32)
        for h in range(HEADS):
            o = h * HEAD_SLOT
            dqf_ref[:, o:o + 128] = dq_ref[:, o:o + 128].astype(BF16)
            dqf_ref[:, o + 128:o + 256] = _rope_t(dq_ref[:, o + 128:o + 256], cs_v, sn_v, lane).astype(BF16)
            dkvf_ref[:, h * 128:(h + 1) * 128] = dk_ref[:, o:o + 128].astype(BF16)
            dkr = dkr + dk_ref[:, o + 128:o + 256]
        dkvf_ref[:, HEADS * 128:] = dv_ref[...].astype(BF16)

        c0 = 3 * CONV_W
        dcqn = _dot_nt(dqf_ref[...], wuq_ref[...])
        dcq, dgq = latent_bwd(z_ref[:, c0:c0 + Q_LORA], dcqn, gq_ref[...])
        dz_ref[:, c0:c0 + Q_LORA] = dcq.astype(BF16)
        c1 = c0 + Q_LORA
        dckvn = _dot_nt(dkvf_ref[...], wukv_ref[...])
        dckv, dgkv = latent_bwd(z_ref[:, c1:c1 + KV_LORA], dckvn, gkv_ref[...])
        dz_ref[:, c1:c1 + KV_LORA] = dckv.astype(BF16)
        dz_ref[:, c1 + KV_LORA:Z_COLS] = _rope_t(dkr, cs_v, sn_v, lane).astype(BF16)

        @pl.when(i == 0)
        def _():
            p_ref[...] = jnp.zeros_like(p_ref)

        p_ref[:, 0:CONV_W] += _rsum8(dy * u2)
        p_ref[:, CONV_W:2 * CONV_W] += _rsum8(dy * u1)
        p_ref[:, 2 * CONV_W:3 * CONV_W] += _rsum8(dy * u)
        p_ref[:, c0:c0 + Q_LORA] += _rsum8(dgq)
        p_ref[:, c1:c1 + KV_LORA] += _rsum8(dgkv)

        @pl.when(i == n_i - 1)
        def _():
            _all_rows(p_ref)

    def rows_of(n):
        return pl.BlockSpec((tm, n), lambda i: (i, 0))

    def whole(shape):
        return pl.BlockSpec(shape, lambda i: (0, 0))

    def prev8(n):
        return pl.BlockSpec((8, n), lambda i: (jnp.maximum(i * hb - 1, 0), 0))

    def next8(n):
        return pl.BlockSpec((8, n), lambda i: (jnp.minimum((i + 1) * hb, last_blk), 0))

    return _hosted_call(
        body, name="mix_mid_bwd", grid=(n_i,),
        in_specs=[rows_of(Z_COLS), prev8(Z_COLS), next8(Z_COLS), rows_of(CONV_W), next8(CONV_W),
                  whole((8, CONV_W)), whole((1, Q_LORA)), whole((1, KV_LORA)),
                  whole((Q_LORA, HEADS * HEAD_SLOT)), whole((KV_LORA, 2 * HEADS * 128)),
                  rows_of(128), rows_of(128),
                  rows_of(HEADS * HEAD_SLOT), rows_of(HEADS * HEAD_SLOT), rows_of(HEADS * V_HEAD)],
        out_specs=[rows_of(Z_COLS), rows_of(HEADS * HEAD_SLOT), rows_of(2 * HEADS * 128), whole((8, MID_SUMS))],
        out_shape=[jax.ShapeDtypeStruct((S, Z_COLS), BF16), jax.ShapeDtypeStruct((S, HEADS * HEAD_SLOT), BF16),
                   jax.ShapeDtypeStruct((S, 2 * HEADS * 128), BF16), jax.ShapeDtypeStruct((8, MID_SUMS), F32)],
        scratch_shapes=[], semantics=("arbitrary",),
        args=(z, z, z, dya, dya, conv_w, gq, gkv, wuq, wukv, cs, sn, dq, dk, dv), comm=comm)


ADA_Q = N_MOD * D // N_CHIPS
ADA_TN = 768


def _ada_forward(c_all, ada_w_q, ada_b_q):
    def body(c_ref, w_ref, b_ref, o_ref):
        cv = c_ref[...]
        sc = (cv * jax.nn.sigmoid(cv)).astype(BF16)
        o_ref[...] = jnp.dot(sc, w_ref[...].astype(BF16), preferred_element_type=F32) + b_ref[...]

    return pl.pallas_call(
        body, name="ada_forward", grid=(ADA_Q // ADA_TN,),
        in_specs=[pl.BlockSpec((16, D), lambda j: (0, 0)), pl.BlockSpec((D, ADA_TN), lambda j: (0, j)),
                  pl.BlockSpec((1, ADA_TN), lambda j: (0, j))],
        out_specs=pl.BlockSpec((16, ADA_TN), lambda j: (0, j)),
        out_shape=jax.ShapeDtypeStruct((16, ADA_Q), F32),
        compiler_params=_params(("parallel",)),
    )(c_all, ada_w_q, ada_b_q)


def _ada_wgrad(c_all, dmod_q):
    def body(c_ref, d_ref, o_ref):
        cv = c_ref[...]
        sc = (cv * jax.nn.sigmoid(cv)).astype(BF16)
        o_ref[...] = lax.dot_general(sc, d_ref[...].astype(BF16), (((0,), (0,)), ((), ())),
                                     preferred_element_type=F32)

    return pl.pallas_call(
        body, name="ada_wgrad", grid=(ADA_Q // ADA_TN,),
        in_specs=[pl.BlockSpec((16, D), lambda j: (0, 0)), pl.BlockSpec((16, ADA_TN), lambda j: (0, j))],
        out_specs=pl.BlockSpec((D, ADA_TN), lambda j: (0, j)),
        out_shape=jax.ShapeDtypeStruct((D, ADA_Q), F32),
        compiler_params=_params(("parallel",)),
    )(c_all, dmod_q)


def _sum_devices(parts):
    n = parts.shape[1]

    def body(p_ref, o_ref):
        o_ref[...] = jnp.broadcast_to(jnp.sum(p_ref[...], axis=0, keepdims=True), o_ref.shape)

    return pl.pallas_call(
        body, name="sum_devices",
        in_specs=[pl.BlockSpec((N_DEV, n), lambda: (0, 0))], out_specs=pl.BlockSpec((N_DEV, n), lambda: (0, 0)),
        out_shape=jax.ShapeDtypeStruct((N_DEV, n), F32),
    )(parts)


def _adamw(w, g, m, v, *, name):
    rows, cols = w.shape
    tr = _row_tile(rows, 256)

    def body(w_ref, g_ref, m_ref, v_ref, d_ref, mo_ref, vo_ref):
        gv = g_ref[...]
        mn = B1 * m_ref[...] + (1.0 - B1) * gv
        vn = B2 * v_ref[...] + (1.0 - B2) * (gv * gv)
        m_hat = mn / (1.0 - B1 ** STEP)
        v_hat = vn / (1.0 - B2 ** STEP)
        d_ref[...] = -LR * (m_hat / (jnp.sqrt(v_hat) + AEPS) + WD * w_ref[...])
        mo_ref[...] = mn
        vo_ref[...] = vn

    blk = pl.BlockSpec((tr, cols), lambda i: (i, 0))
    return pl.pallas_call(
        body, name=name, grid=(rows // tr,), in_specs=[blk] * 4, out_specs=[blk] * 3,
        out_shape=[jax.ShapeDtypeStruct((rows, cols), F32)] * 3,
        compiler_params=_params(("parallel",)),
    )(w, g, m, v)


def _small_allgather(v, *, name):
    m, n = v.shape

    def body(x_ref, out_ref, send_sems, recv_sems, local_sem):
        x, y, c = _place()
        me, sibling = (x, y, c), (x, y, 1 - c)
        chips = [(1 - x, y), (x, 1 - y), (1 - x, 1 - y)]

        def rows(px, py, pc):
            return out_ref.at[pl.ds((4 * px + 2 * py + pc) * m, m), :]

        def copy(k, block, to, src=None):
            return pltpu.make_async_remote_copy(
                src_ref=rows(*block) if src is None else src, dst_ref=rows(*block),
                send_sem=send_sems.at[k], recv_sem=recv_sems.at[k], device_id=to, device_id_type=MESH)

        mine = pltpu.make_async_copy(x_ref, rows(*me), local_sem)
        mine.start()
        first = [copy(0, me, sibling, src=x_ref)]
        first += [copy(1 + j, me, (*chip, c), src=x_ref) for j, chip in enumerate(chips)]
        for cp in first:
            cp.start()
        passed = [copy(4 + j, (*chip, c), sibling) for j, chip in enumerate(chips)]
        for j, chip in enumerate(chips):
            copy(1 + j, (*chip, c), me).wait_recv()
            passed[j].start()
        copy(0, sibling, me).wait_recv()
        for j, chip in enumerate(chips):
            copy(4 + j, (*chip, 1 - c), me).wait_recv()
        for cp in first + passed:
            cp.wait_send()
        mine.wait()

    return pl.pallas_call(
        body, name=name,
        out_shape=jax.ShapeDtypeStruct((N_DEV * m, n), v.dtype),
        in_specs=[pl.BlockSpec(memory_space=pltpu.VMEM)], out_specs=pl.BlockSpec(memory_space=pltpu.VMEM),
        scratch_shapes=[pltpu.SemaphoreType.DMA((7,)), pltpu.SemaphoreType.DMA((7,)), pltpu.SemaphoreType.DMA],
    )(v)


def _pair_add(place, g, t, *, name):
    _, R, n = g.shape
    H = R // 2
    tr = _row_tile(H, 288, mult=16)
    nb = H // tr

    def body(pl_ref, g_ref, t_ref, pf_ref, pb_ref):
        s = g_ref[...] + t_ref[...]
        pf_ref[...] = s
        pb_ref[...] = s.astype(BF16)

    grid_spec = pltpu.PrefetchScalarGridSpec(
        num_scalar_prefetch=1, grid=(N_CHIPS, nb),
        in_specs=[pl.BlockSpec((1, tr, n), lambda q, r, p: (q, p[0] * nb + r, 0)),
                  pl.BlockSpec((1, tr, n), lambda q, r, p: (q, r, 0))],
        out_specs=[pl.BlockSpec((1, tr, n), lambda q, r, p: (q, r, 0))] * 2)
    return pl.pallas_call(
        body, name=name, grid_spec=grid_spec,
        out_shape=[jax.ShapeDtypeStruct((N_CHIPS, H, n), F32), jax.ShapeDtypeStruct((N_CHIPS, H, n), BF16)],
        compiler_params=_params(("parallel", "parallel")),
    )(place, g, t)


def _chip_add(place, pf, t, *, name):
    _, H, n = pf.shape
    tr = _row_tile(H, 288, mult=16)

    def body(pl_ref, pf_ref, t1_ref, t2_ref, t3_ref, o_ref):
        o_ref[...] = ((pf_ref[0] + t1_ref[0].astype(F32)) + t2_ref[0].astype(F32)) + t3_ref[0].astype(F32)

    def slot(j):
        return pl.BlockSpec((1, tr, n), lambda r, p: (p[1] ^ j, r, 0))

    grid_spec = pltpu.PrefetchScalarGridSpec(
        num_scalar_prefetch=1, grid=(H // tr,),
        in_specs=[pl.BlockSpec((1, tr, n), lambda r, p: (p[1], r, 0)), slot(1), slot(2), slot(3)],
        out_specs=pl.BlockSpec((tr, n), lambda r, p: (r, 0)))
    return pl.pallas_call(
        body, name=name, grid_spec=grid_spec,
        out_shape=jax.ShapeDtypeStruct((H, n), F32),
        compiler_params=_params(("parallel",)),
    )(place, pf, t, t, t)


SLAB_W = 1024
BULK = [("ffn1_w1", D, FF // 4, True), ("ffn1_w3", D, FF // 4, True), ("ffn1_w2", FF // 4, D, False),
        ("w_in", D, IN_COLS // 4, True), ("w_uq", Q_LORA, 768 // 4, True), ("w_ukv", KV_LORA, 1024 // 4, True),
        ("w_out", D // 4, D, False),
        ("ffn2_w1", D, FF // 4, True), ("ffn2_w3", D, FF // 4, True), ("ffn2_w2", FF // 4, D, False)]


def _slab_rows(k, n):
    rows = k * n // SLAB_W
    return rows, rows + (-rows) % 16


def _group(*names):
    return [b for b in BULK if b[0] in names]


W_FIRST = _group("ffn1_w1", "ffn1_w3")
W_REST = [b for b in BULK if b not in W_FIRST]
G_FFN2 = _group("ffn2_w1", "ffn2_w3", "ffn2_w2")
G_MIX = _group("w_in", "w_uq", "w_ukv", "w_out")
G_FFN1 = _group("ffn1_w1", "ffn1_w3", "ffn1_w2")


def _pack_shards(parts, specs, lead):
    out = []
    for p, (_, k, n, _) in zip(parts, specs):
        rows, padded = _slab_rows(k, n)
        r = p.reshape(*lead, rows, SLAB_W)
        if padded != rows:
            r = jnp.pad(r, [(0, 0)] * len(lead) + [(0, padded - rows), (0, 0)])
        out.append(r)
    return jnp.concatenate(out, axis=len(lead))


def _unpack_slab(slab, specs, lead):
    out, r0 = {}, 0
    for name, k, n, _ in specs:
        rows, padded = _slab_rows(k, n)
        out[name] = lax.slice_in_dim(slab, r0, r0 + rows, axis=len(lead)).reshape(*lead, k, n)
        r0 += padded
    return out


def _full_weight(parts, by_cols):
    if by_cols:
        return jnp.transpose(parts, (1, 0, 2)).reshape(parts.shape[1], -1)
    return parts.reshape(-1, parts.shape[2])


def _quarters(g, by_cols):
    if by_cols:
        k, n = g.shape
        return jnp.transpose(g.reshape(k, N_CHIPS, n // N_CHIPS), (1, 0, 2))
    return g.reshape(N_CHIPS, g.shape[0] // N_CHIPS, g.shape[1])


def _pad_heads(w_uq):
    w = w_uq.reshape(Q_LORA, HEADS, QK_NOPE + QK_ROPE)
    return jnp.pad(w, ((0, 0), (0, 0), (0, HEAD_SLOT - QK_NOPE - QK_ROPE))).reshape(Q_LORA, HEADS * HEAD_SLOT)


def _unpad_heads(g):
    return g.reshape(Q_LORA, HEADS, HEAD_SLOT)[:, :, :QK_NOPE + QK_ROPE].reshape(Q_LORA, HEADS * (QK_NOPE + QK_ROPE))


def _split_kv(w_ukv):
    return jnp.transpose(w_ukv.reshape(KV_LORA, HEADS, 2, 128), (0, 2, 1, 3)).reshape(KV_LORA, 2 * HEADS * 128)


def _merge_kv(g):
    return jnp.transpose(g.reshape(KV_LORA, 2, HEADS, 128), (0, 2, 1, 3)).reshape(KV_LORA, 2 * HEADS * 128)


def _rope_tables(positions):
    inv_freq = ROPE_THETA ** (-jnp.arange(0, QK_ROPE, 2, dtype=F32) / QK_ROPE)
    ang = positions.astype(F32)[:, None] * inv_freq
    cos, sin, zero = jnp.cos(ang), jnp.sin(ang), jnp.zeros((positions.shape[0], 64), F32)
    return jnp.concatenate([cos, cos, zero], axis=1), jnp.concatenate([sin, sin, zero], axis=1)


def _reduce_tail(place, pf, t2, tag, host=None):
    rh = _chip_add(place, pf, t2, name=tag + "_chip_add")
    if host is None:
        out, (other,) = None, _run_comm(_PairShare([rh]), name=tag + "_pair_share")
    else:
        out, (other,) = host(_PairShare([rh]))
    south = place[0] == 0
    return out, jnp.concatenate([jnp.where(south, rh, other), jnp.where(south, other, rh)], axis=0)


def _local_step(x, positions, target, mod, vec, conv_w, w_first, rest_shards, place):
    row = lambda k: mod[k:k + 1]
    sh1, sc1, g1, sh2, sc2, g2, sh3, sc3, g3 = [row(k) for k in range(N_MOD)]
    cs, sn = _rope_tables(positions)
    cw8 = jnp.pad(conv_w, ((0, 5), (0, 0)))
    ga, gb = _group_mats()
    dist = place is not None
    comm = lambda prog: prog if dist else None

    own = _pack_shards([rest_shards[b[0]] for b in W_REST], W_REST, ()) if dist else None
    (h1, a1, b1, u1), got = _ffn_up(x, vec["norm_ffn1_g"], sh1, sc1, w_first["ffn1_w1"], w_first["ffn1_w3"],
                                    name="ffn1_up", comm=comm(_Gather([own]) if dist else None))
    if dist:
        parts = _unpack_slab(lax.dynamic_update_slice_in_dim(got[0], own[None], place[1], axis=0), W_REST, (N_CHIPS,))
        w = {name: _full_weight(parts[name], by_cols) for name, _, _, by_cols in W_REST}
    else:
        w = dict(rest_shards)
    w.update(w_first)
    w_in = jnp.pad(w["w_in"], ((0, 0), (0, Z_COLS - IN_COLS)))
    wuq = _pad_heads(w["w_uq"])
    wukv = _split_kv(w["w_ukv"])
    x1, f1 = _ffn_down(u1, w["ffn1_w2"], x, g1, name="ffn1_down")
    h2, z = _mix_in(x1, vec["norm_mix_g"], sh2, sc2, w_in)
    ya, q, k, v, cqn, ckvn = _mix_mid(z, cw8, vec["q_norm_g"], vec["kv_norm_g"], wuq, wukv, cs, sn)
    o, lse = _attention(q, k, v)
    x2, yn, yo = _mix_out(ya, o, vec["out_norm_g"], w["w_out"], x1, g2, ga, gb)
    (h3, a3, b3, u3), _ = _ffn_up(x2, vec["norm_ffn2_g"], sh3, sc3, w["ffn2_w1"], w["ffn2_w3"], name="ffn2_up")
    x3, f3 = _ffn_down(u3, w["ffn2_w2"], x2, g3, name="ffn2_down")
    dx3, dgfin, loss_blk = _final_loss(x3, vec["final_norm_g"], target)

    grads, reduced = {}, {}

    def tn(a, b, tm, tn_, name, prog=None):
        if prog is None:
            return _tn_matmul(a, b, tm=tm, tn=tn_, name=name), None
        return _tn_matmul(a, b, tm=tm, tn=tn_, name=name, comm=prog)

    def slab_of(specs):
        return _pack_shards([_quarters(grads[n], by_cols) for n, _, _, by_cols in specs], specs, (N_CHIPS,))

    (df3, da3, db3, dg3), _ = _ffn_bwd_du(dx3, g3, f3, w["ffn2_w2"], a3, b3, name="ffn2_bwd_du")
    grads["ffn2_w2"], _ = tn(u3, df3, FF // 2, D, "ffn2_dw2")
    grads["ffn2_w1"], _ = tn(h3, da3, D, FF // 2, "ffn2_dw1")
    grads["ffn2_w3"], _ = tn(h3, db3, D, FF // 2, "ffn2_dw3")
    (dx2, s3), _ = _dh_normbwd([(da3, w["ffn2_w1"]), (db3, w["ffn2_w3"])], x2, vec["norm_ffn2_g"], sc3, dx3,
                               name="ffn2_bwd_dh")

    p1 = slab_of(G_FFN2) if dist else None
    (dyo, dya, do, delta, s_out), t1 = _mix_out_bwd(dx2, g2, yo, w["w_out"], ya, o, vec["out_norm_g"], ga, gb,
                                                    comm=comm(_PairExchange([p1]) if dist else None))
    grads["w_out"], _ = tn(yn, dyo, D, D, "dw_out")
    if dist:
        pf1, pb1 = _pair_add(place, p1, t1[0], name="ffn2g_pair_add")
    dq, dk, dv = _attention_bwd(q, k, v, do, lse, delta)
    (dz, dqf, dkvf, s_mid), t2 = _mix_mid_bwd(z, dya, cw8, vec["q_norm_g"], vec["kv_norm_g"], wuq, wukv, cs, sn,
                                              dq, dk, dv, comm=comm(_ChipExchange([pb1]) if dist else None))
    g_uq, _ = tn(cqn, dqf, Q_LORA, HEADS * HEAD_SLOT, "dw_uq")
    g_ukv, _ = tn(ckvn, dkvf, KV_LORA, 2 * HEADS * 128, "dw_ukv")
    grads["w_uq"], grads["w_ukv"] = _unpad_heads(g_uq), _merge_kv(g_ukv)
    if dist:
        g_in, red = _reduce_tail(place, pf1, t2[0], "ffn2g",
                                 host=lambda prog: tn(h2, dz, D, Z_COLS // 2, "dw_in", prog))
        reduced.update(_unpack_slab(red, G_FFN2, ()))
    else:
        g_in, _ = tn(h2, dz, D, Z_COLS // 2, "dw_in")
    grads["w_in"] = g_in[:, :IN_COLS]
    (dx1, s2), _ = _dh_normbwd([(dz, w_in)], x1, vec["norm_mix_g"], sc2, dx2, name="mix_bwd_dh")

    p2 = slab_of(G_MIX) if dist else None
    (df1, da1, db1, dg1), t1 = _ffn_bwd_du(dx1, g1, f1, w["ffn1_w2"], a1, b1, name="ffn1_bwd_du",
                                           comm=comm(_PairExchange([p2]) if dist else None))
    if dist:
        pf2, pb2 = _pair_add(place, p2, t1[0], name="mixg_pair_add")
        grads["ffn1_w2"], t2 = tn(u1, df1, FF // 2, D, "ffn1_dw2", _ChipExchange([pb2]))
        grads["ffn1_w1"], red = _reduce_tail(place, pf2, t2[0], "mixg",
                                             host=lambda prog: tn(h1, da1, D, FF // 2, "ffn1_dw1", prog))
        reduced.update(_unpack_slab(red, G_MIX, ()))
    else:
        grads["ffn1_w2"], _ = tn(u1, df1, FF // 2, D, "ffn1_dw2")
        grads["ffn1_w1"], _ = tn(h1, da1, D, FF // 2, "ffn1_dw1")
    grads["ffn1_w3"], _ = tn(h1, db1, D, FF // 2, "ffn1_dw3")

    p3 = slab_of(G_FFN1) if dist else None
    (dx0, s1), t1 = _dh_normbwd([(da1, w["ffn1_w1"]), (db1, w["ffn1_w3"])], x, vec["norm_ffn1_g"], sc1, dx1,
                                name="ffn1_bwd_dh", comm=comm(_PairExchange([p3]) if dist else None))
    if dist:
        pf3, pb3 = _pair_add(place, p3, t1[0], name="ffn1g_pair_add")
        t2 = _run_comm(_ChipExchange([pb3]), name="ffn1g_chip_exchange")
        _, red = _reduce_tail(place, pf3, t2[0], "ffn1g")
        reduced.update(_unpack_slab(red, G_FFN1, ()))
    else:
        reduced = grads

    def part(s, k):
        return s[0:1, k * D:(k + 1) * D]

    dmod = jnp.concatenate([part(s1, 1), part(s1, 0), dg1[0:1], part(s2, 1), part(s2, 0), part(s_out, 0),
                            part(s3, 1), part(s3, 0), dg3[0:1]], axis=1)
    small = {"norm_ffn1_g": part(s1, 2), "norm_mix_g": part(s2, 2), "out_norm_g": part(s_out, 1),
             "norm_ffn2_g": part(s3, 2), "final_norm_g": dgfin[0:1],
             "q_norm_g": s_mid[0:1, 3 * CONV_W:3 * CONV_W + Q_LORA],
             "kv_norm_g": s_mid[0:1, 3 * CONV_W + Q_LORA:MID_SUMS], "conv_w": s_mid[0:1, 0:3 * CONV_W]}
    return loss_blk, dx0, reduced, dmod, small


SMALL = [("norm_ffn1_g", D), ("norm_mix_g", D), ("out_norm_g", D), ("norm_ffn2_g", D), ("final_norm_g", D),
         ("q_norm_g", Q_LORA), ("kv_norm_g", KV_LORA), ("conv_w", 3 * CONV_W)]
WEIGHTS = ['ada_w', 'ada_b', 'norm_ffn1_g', 'ffn1_w1', 'ffn1_w3', 'ffn1_w2', 'norm_mix_g', 'w_in', 'conv_w',
           'q_norm_g', 'w_uq', 'kv_norm_g', 'w_ukv', 'out_norm_g', 'w_out', 'norm_ffn2_g', 'ffn2_w1', 'ffn2_w3',
           'ffn2_w2', 'final_norm_g']


def kernel(x, c, positions, ada_w, ada_b, norm_ffn1_g, ffn1_w1, ffn1_w3, ffn1_w2, norm_mix_g, w_in, conv_w, q_norm_g, w_uq, kv_norm_g, w_ukv, out_norm_g, w_out, norm_ffn2_g, ffn2_w1, ffn2_w3, ffn2_w2, final_norm_g, loss_target, m_ada_w, m_ada_b, m_norm_ffn1_g, m_ffn1_w1, m_ffn1_w3, m_ffn1_w2, m_norm_mix_g, m_w_in, m_conv_w, m_q_norm_g, m_w_uq, m_kv_norm_g, m_w_ukv, m_out_norm_g, m_w_out, m_norm_ffn2_g, m_ffn2_w1, m_ffn2_w3, m_ffn2_w2, m_final_norm_g, v_ada_w, v_ada_b, v_norm_ffn1_g, v_ffn1_w1, v_ffn1_w3, v_ffn1_w2, v_norm_mix_g, v_w_in, v_conv_w, v_q_norm_g, v_w_uq, v_kv_norm_g, v_w_ukv, v_out_norm_g, v_w_out, v_norm_ffn2_g, v_ffn2_w1, v_ffn2_w3, v_ffn2_w2, v_final_norm_g):
    args = dict(locals())
    wts = {n: args[n] for n in WEIGHTS}
    mom = {n: args["m_" + n] for n in WEIGHTS}
    var = {n: args["v_" + n] for n in WEIGHTS}
    ax, ay, ac = _place()
    myq = 2 * ax + ay
    me = 2 * myq + ac
    place = jnp.stack([ac, myq]).astype(jnp.int32)

    shards = {name: wts[name][0].astype(BF16) for name, *_ in BULK}
    own = _pack_shards([shards[b[0]] for b in W_FIRST], W_FIRST, ())
    first = _run_comm(_Gather([own]), name="gather_ffn1")[0]
    first = _unpack_slab(lax.dynamic_update_slice_in_dim(first, own[None], myq, axis=0), W_FIRST, (N_CHIPS,))
    w_first = {name: _full_weight(first[name], by_cols) for name, _, _, by_cols in W_FIRST}

    mine = jnp.concatenate([c, conv_w[0].reshape(1, 3 * CONV_W // N_CHIPS)], axis=1)
    seen = _small_allgather(jnp.pad(mine, ((0, 7), (0, 0))), name="gather_cond").reshape(N_DEV, 8, -1)[:, 0]
    c_all = jnp.pad(seen[:, :D], ((0, 8), (0, 0)))
    conv_full = jnp.transpose(seen[0::2, D:].reshape(N_CHIPS, 3, CONV_W // N_CHIPS), (1, 0, 2)).reshape(3, CONV_W)
    ada_b_q = lax.dynamic_slice_in_dim(ada_b, myq * ADA_Q, ADA_Q, axis=1)
    mod_q = _ada_forward(c_all, ada_w[0], ada_b_q)
    mod_all = _small_allgather(mod_q, name="gather_mod").reshape(N_DEV, 16, ADA_Q)
    mod_rows = jnp.transpose(mod_all[0::2, :N_DEV], (1, 0, 2)).reshape(N_DEV, N_MOD * D)
    mod = lax.dynamic_slice_in_dim(mod_rows, me, 1, axis=0).reshape(N_MOD, D)

    vec = {n: wts[n] for n in ("norm_ffn1_g", "norm_mix_g", "q_norm_g", "kv_norm_g", "out_norm_g", "norm_ffn2_g")}
    vec["final_norm_g"] = final_norm_g.reshape(1, D)
    loss_blk, grad_x, gq, dmod, small = _local_step(x[0], positions[0], loss_target[0], mod, vec, conv_full, w_first,
                                                    {b[0]: shards[b[0]] for b in W_REST}, place)
    loss = lax.psum(loss_blk[0, 0], ("x", "y", "c"))

    rows = jnp.concatenate([dmod] + [small[n] for n, _ in SMALL], axis=1)
    every = _small_allgather(jnp.pad(rows, ((0, 7), (0, 0))), name="gather_small").reshape(N_DEV, 8, -1)[:, 0]
    total = _sum_devices(every)[0:1]
    dmod_q = lax.dynamic_slice_in_dim(every[:, :N_MOD * D], myq * ADA_Q, ADA_Q, axis=1)
    g = {name: gq[name] for name, *_ in BULK}
    g["ada_w"] = _ada_wgrad(c_all, jnp.pad(dmod_q, ((0, 8), (0, 0))))
    g["ada_b"] = total[:, :N_MOD * D]
    off = N_MOD * D
    for n, width in SMALL:
        g[n] = total[:, off:off + width]
        off += width
    g["conv_w"] = lax.dynamic_slice_in_dim(g["conv_w"].reshape(3, CONV_W), myq * (CONV_W // N_CHIPS),
                                           CONV_W // N_CHIPS, axis=1)
    g["final_norm_g"] = g["final_norm_g"].reshape(D)

    delta, new_m, new_v = {}, {}, {}
    for name in ["ada_w"] + [b[0] for b in BULK]:
        shape = wts[name].shape
        delta[name], new_m[name], new_v[name] = [
            r.reshape(shape) for r in _adamw(wts[name][0], g[name], mom[name][0], var[name][0], name="adamw_" + name)]
        g[name] = g[name].reshape(shape)
    smalls = ["ada_b"] + [n for n, _ in SMALL]

    def packed(d):
        flat = jnp.concatenate([d[n].reshape(1, -1) for n in smalls], axis=1)
        return jnp.pad(flat.reshape(-1, D), ((0, 1), (0, 0)))

    res = _adamw(packed(wts), packed(g), packed(mom), packed(var), name="adamw_small")
    off = 0
    for n in smalls:
        size = wts[n].size
        for d, r in zip((delta, new_m, new_v), res):
            d[n] = r.reshape(-1)[off:off + size].reshape(wts[n].shape)
        g[n] = g[n].reshape(wts[n].shape)
        off += size

    return (loss, grad_x[None], *[g[n] for n in WEIGHTS], *[delta[n] for n in WEIGHTS],
            *[new_m[n] for n in WEIGHTS], *[new_v[n] for n in WEIGHTS])
```

```python
import functools

import numpy as np
import jax
import jax.numpy as jnp
from jax import lax
from jax.experimental import pallas as pl
from jax.experimental.pallas import tpu as pltpu

F32 = jnp.float32
BF16 = jnp.bfloat16
MESH = pl.DeviceIdType.MESH

D = 1024
FF = 2816
CONV_W = 512
CONV_GROUP = 64
HEADS = 4
QK_NOPE = 128
QK_ROPE = 64
V_HEAD = 128
Q_LORA = 384
KV_LORA = 256
HEAD_SLOT = 256
IN_COLS = 3 * CONV_W + Q_LORA + KV_LORA + QK_ROPE
Z_COLS = 2304
EPS = 1e-6
ROPE_THETA = 10000.0
CHUNK = 64
ATT_SCALE = (QK_NOPE + QK_ROPE) ** -0.5
NEG = -1e30
EXP2_SCALE = ATT_SCALE * 1.4426950408889634
N_MOD = 9

LR, B1, B2, AEPS, WD, STEP = 0.001, 0.9, 0.999, 1e-08, 0.01, 10

N_CHIPS = 4
N_DEV = 8
VMEM_LIMIT = 56 << 20


def _params(sem, vmem=VMEM_LIMIT):
    return pltpu.CompilerParams(dimension_semantics=sem, vmem_limit_bytes=vmem)


def _rms(v):
    return lax.rsqrt(jnp.mean(v * v, axis=-1, keepdims=True) + EPS)


def _rsum8(v):
    t, n = v.shape
    return jnp.sum(v.reshape(t // 8, 8, n), axis=0)


def _all_rows(ref):
    ref[...] = jnp.broadcast_to(jnp.sum(ref[...], axis=0, keepdims=True), ref.shape)


def _gsum(v, gmat):
    hi = v.astype(BF16)
    lo = (v - hi.astype(F32)).astype(BF16)
    return (jnp.dot(hi, gmat, preferred_element_type=F32)
            + jnp.dot(lo, gmat, preferred_element_type=F32))


def _dot_nt(a, b):
    return lax.dot_general(a, b, (((1,), (1,)), ((), ())), preferred_element_type=F32)


def _silu_parts(a):
    sg = jax.nn.sigmoid(a)
    return sg, a * sg


def _rope(xr, cs, sn, lane):
    rh = jnp.where(lane < 32, -pltpu.roll(xr, 96, 1), pltpu.roll(xr, 32, 1))
    return xr * cs + rh * sn


def _rope_t(g, cs, sn, lane):
    y = g * sn
    rt = jnp.where(lane < 32, pltpu.roll(y, 96, 1), jnp.where(lane < 64, -pltpu.roll(y, 32, 1), 0.0))
    return g * cs + rt


def _row_tile(rows, pref, mult=8):
    t = min(rows, pref) // mult * mult
    while rows % t:
        t -= mult
    return t


def _place():
    return lax.axis_index("x"), lax.axis_index("y"), lax.axis_index("c")


ANY = pl.BlockSpec(memory_space=pl.ANY)


def _hosted_call(body, *, name, grid, in_specs, out_specs, out_shape, scratch_shapes, semantics, args, comm=None):
    n_in, n_out, n_scr = len(in_specs), len(out_specs), len(scratch_shapes)
    if comm is None:
        res = pl.pallas_call(body, name=name, grid=grid, in_specs=in_specs, out_specs=out_specs, out_shape=out_shape,
                             scratch_shapes=scratch_shapes, compiler_params=_params(semantics))(*args)
        return list(res), []
    n_ci, n_co = len(comm.inputs), len(comm.out_shapes)
    total = int(np.prod(grid))

    def hosted(*refs):
        ins, refs = refs[:n_in], refs[n_in:]
        cins, refs = refs[:n_ci], refs[n_ci:]
        outs, refs = refs[:n_out], refs[n_out:]
        couts, refs = refs[:n_co], refs[n_co:]
        scratch, sems = refs[:n_scr], refs[n_scr]
        step = pl.program_id(0)
        for ax in range(1, len(grid)):
            step = step * grid[ax] + pl.program_id(ax)

        @pl.when(step == 0)
        def _():
            comm.start(cins, couts, sems)

        body(*ins, *outs, *scratch)

        @pl.when(step == total - 1)
        def _():
            comm.finish(cins, couts, sems)

    res = pl.pallas_call(
        hosted, name=name, grid=grid, in_specs=list(in_specs) + [ANY] * n_ci,
        out_specs=list(out_specs) + [ANY] * n_co, out_shape=list(out_shape) + list(comm.out_shapes),
        scratch_shapes=list(scratch_shapes) + [pltpu.SemaphoreType.DMA((comm.n_sems,))],
        compiler_params=_params(("arbitrary",) * len(grid)))(*args, *comm.inputs)
    return list(res[:n_out]), list(res[n_out:])


def _run_comm(comm, *, name):
    n_ci = len(comm.inputs)

    def body(*refs):
        cins, couts, sems = refs[:n_ci], refs[n_ci:-1], refs[-1]
        comm.start(cins, couts, sems)
        comm.finish(cins, couts, sems)

    return pl.pallas_call(
        body, name=name, out_shape=list(comm.out_shapes), in_specs=[ANY] * n_ci,
        out_specs=[ANY] * len(comm.out_shapes), scratch_shapes=[pltpu.SemaphoreType.DMA((comm.n_sems,))],
    )(*comm.inputs)


class _Gather:
    def __init__(self, slabs):
        self.inputs = list(slabs)
        self.out_shapes = [jax.ShapeDtypeStruct((N_CHIPS,) + s.shape, s.dtype) for s in slabs]
        self.n_sems = 12 * len(slabs)

    @staticmethod
    def _copy(out, sems, base, k, chip, hc, to, src=None):
        H = out.shape[1] // 2
        half = out.at[2 * chip[0] + chip[1], pl.ds(hc * H, H), :]
        return pltpu.make_async_remote_copy(
            src_ref=half if src is None else src, dst_ref=half, send_sem=sems.at[base + k],
            recv_sem=sems.at[base + 6 + k], device_id=to, device_id_type=MESH)

    def _firsts(self, src, out, sems, base):
        x, y, c = _place()
        H = src.shape[0] // 2
        chips = [(1 - x, y), (x, 1 - y), (1 - x, 1 - y)]
        return [self._copy(out, sems, base, j, (x, y), c, (*chip, c), src=src.at[pl.ds(c * H, H), :])
                for j, chip in enumerate(chips)]

    def start(self, ins, outs, sems):
        for i, (src, out) in enumerate(zip(ins, outs)):
            for cp in self._firsts(src, out, sems, 12 * i):
                cp.start()

    def finish(self, ins, outs, sems):
        x, y, c = _place()
        chips = [(1 - x, y), (x, 1 - y), (1 - x, 1 - y)]
        passed = []
        for i, out in enumerate(outs):
            for j, chip in enumerate(chips):
                self._copy(out, sems, 12 * i, j, chip, c, (x, y, c)).wait_recv()
                cp = self._copy(out, sems, 12 * i, 3 + j, chip, c, (x, y, 1 - c))
                cp.start()
                passed.append(cp)
        for i, out in enumerate(outs):
            for j, chip in enumerate(chips):
                self._copy(out, sems, 12 * i, 3 + j, chip, 1 - c, (x, y, c)).wait_recv()
        for cp in passed:
            cp.wait_send()
        for i, (src, out) in enumerate(zip(ins, outs)):
            for cp in self._firsts(src, out, sems, 12 * i):
                cp.wait_send()


class _PairExchange:
    def __init__(self, arrays):
        self.inputs = list(arrays)
        self.out_shapes = [jax.ShapeDtypeStruct((N_CHIPS, a.shape[1] // 2, a.shape[2]), a.dtype) for a in arrays]
        self.n_sems = 2 * len(arrays)

    def _copies(self, ins, outs, sems):
        x, y, c = _place()
        return [pltpu.make_async_remote_copy(
            src_ref=g.at[:, pl.ds((1 - c) * t.shape[1], t.shape[1]), :], dst_ref=t, send_sem=sems.at[2 * i],
            recv_sem=sems.at[2 * i + 1], device_id=(x, y, 1 - c), device_id_type=MESH)
            for i, (g, t) in enumerate(zip(ins, outs))]

    def start(self, ins, outs, sems):
        for cp in self._copies(ins, outs, sems):
            cp.start()

    def finish(self, ins, outs, sems):
        for cp in self._copies(ins, outs, sems):
            cp.wait()


class _ChipExchange:
    def __init__(self, arrays):
        self.inputs = list(arrays)
        self.out_shapes = [jax.ShapeDtypeStruct(a.shape, a.dtype) for a in arrays]
        self.n_sems = 6 * len(arrays)

    def _copies(self, p, t, sems, base):
        x, y, c = _place()
        myq = 2 * x + y
        chips = [(1 - x, y), (x, 1 - y), (1 - x, 1 - y)]
        sends = [pltpu.make_async_remote_copy(
            src_ref=p.at[2 * chip[0] + chip[1]], dst_ref=t.at[myq], send_sem=sems.at[base + j],
            recv_sem=sems.at[base + 3 + j], device_id=(*chip, c), device_id_type=MESH) for j, chip in enumerate(chips)]
        lands = [pltpu.make_async_remote_copy(
            src_ref=t.at[2 * chip[0] + chip[1]], dst_ref=t.at[2 * chip[0] + chip[1]], send_sem=sems.at[base + j],
            recv_sem=sems.at[base + 3 + j], device_id=(*chip, c), device_id_type=MESH) for j, chip in enumerate(chips)]
        return sends, lands

    def start(self, ins, outs, sems):
        for i, (p, t) in enumerate(zip(ins, outs)):
            for cp in self._copies(p, t, sems, 6 * i)[0]:
                cp.start()

    def finish(self, ins, outs, sems):
        for i, (p, t) in enumerate(zip(ins, outs)):
            sends, lands = self._copies(p, t, sems, 6 * i)
            for cp in lands:
                cp.wait_recv()
            for cp in sends:
                cp.wait_send()


class _PairShare:
    def __init__(self, arrays):
        self.inputs = list(arrays)
        self.out_shapes = [jax.ShapeDtypeStruct(a.shape, a.dtype) for a in arrays]
        self.n_sems = 2 * len(arrays)

    def _copies(self, ins, outs, sems):
        x, y, c = _place()
        return [pltpu.make_async_remote_copy(
            src_ref=r, dst_ref=o, send_sem=sems.at[2 * i], recv_sem=sems.at[2 * i + 1],
            device_id=(x, y, 1 - c), device_id_type=MESH) for i, (r, o) in enumerate(zip(ins, outs))]

    def start(self, ins, outs, sems):
        for cp in self._copies(ins, outs, sems):
            cp.start()

    def finish(self, ins, outs, sems):
        for cp in self._copies(ins, outs, sems):
            cp.wait()


def _ffn_up(x, ng, sh, sc, w1, w3, *, name, comm=None):
    S = x.shape[0]
    tm, tn = _row_tile(S, 512), FF // 2

    def body(x_ref, g_ref, sh_ref, sc_ref, w1_ref, w3_ref, h_ref, a_ref, b_ref, u_ref, hs):
        @pl.when(pl.program_id(1) == 0)
        def _():
            xv = x_ref[...]
            h = ((xv * _rms(xv)) * g_ref[...]) * (1.0 + sc_ref[...]) + sh_ref[...]
            hb = h.astype(BF16)
            hs[...] = hb
            h_ref[...] = hb

        h = hs[...]
        a = jnp.dot(h, w1_ref[...], preferred_element_type=F32)
        b = jnp.dot(h, w3_ref[...], preferred_element_type=F32)
        _, sa = _silu_parts(a)
        a_ref[...] = a.astype(BF16)
        b_ref[...] = b.astype(BF16)
        u_ref[...] = (sa * b).astype(BF16)

    row = pl.BlockSpec((tm, D), lambda i, j: (i, 0))
    vec = pl.BlockSpec((1, D), lambda i, j: (0, 0))
    wsp = pl.BlockSpec((D, tn), lambda i, j: (0, j))
    osp = pl.BlockSpec((tm, tn), lambda i, j: (i, j))
    return _hosted_call(
        body, name=name, grid=(S // tm, FF // tn),
        in_specs=[row, vec, vec, vec, wsp, wsp],
        out_specs=[row, osp, osp, osp],
        out_shape=[jax.ShapeDtypeStruct((S, D), BF16)] + [jax.ShapeDtypeStruct((S, FF), BF16)] * 3,
        scratch_shapes=[pltpu.VMEM((tm, D), BF16)],
        semantics=("parallel", "arbitrary"), args=(x, ng, sh, sc, w1, w3), comm=comm)


def _ffn_down(u, w2, x, gate, *, name):
    S = x.shape[0]
    tm = _row_tile(S, 512)

    def body(u_ref, w2_ref, x_ref, g_ref, xo_ref, f_ref):
        f = jnp.dot(u_ref[...], w2_ref[...], preferred_element_type=F32)
        xo_ref[...] = x_ref[...] + (0.5 * g_ref[...]) * f
        f_ref[...] = f.astype(BF16)

    return pl.pallas_call(
        body, name=name, grid=(S // tm,),
        in_specs=[pl.BlockSpec((tm, FF), lambda i: (i, 0)), pl.BlockSpec((FF, D), lambda i: (0, 0)),
                  pl.BlockSpec((tm, D), lambda i: (i, 0)), pl.BlockSpec((1, D), lambda i: (0, 0))],
        out_specs=[pl.BlockSpec((tm, D), lambda i: (i, 0))] * 2,
        out_shape=[jax.ShapeDtypeStruct((S, D), F32), jax.ShapeDtypeStruct((S, D), BF16)],
        compiler_params=_params(("parallel",)),
    )(u, w2, x, gate)


def _ffn_bwd_du(dx, gate, f, w2, a, b, *, name, comm=None):
    S = dx.shape[0]
    tm, tn = _row_tile(S, 256), FF // 2
    n_i = S // tm

    def body(dx_ref, g_ref, f_ref, w_ref, a_ref, b_ref, df_ref, da_ref, db_ref, dg_ref, dfs):
        i, j = pl.program_id(0), pl.program_id(1)

        @pl.when((i == 0) & (j == 0))
        def _():
            dg_ref[...] = jnp.zeros_like(dg_ref)

        @pl.when(j == 0)
        def _():
            dxv = dx_ref[...]
            dfb = (dxv * (0.5 * g_ref[...])).astype(BF16)
            dfs[...] = dfb
            df_ref[...] = dfb
            dg_ref[...] += _rsum8(dxv * (0.5 * f_ref[...].astype(F32)))

        du = _dot_nt(dfs[...], w_ref[pl.ds(pl.multiple_of(j * tn, tn), tn), :])
        av = a_ref[...].astype(F32)
        sg, sa = _silu_parts(av)
        da_ref[...] = (du * b_ref[...].astype(F32) * (sg * (1.0 + av * (1.0 - sg)))).astype(BF16)
        db_ref[...] = (du * sa).astype(BF16)

        @pl.when((i == n_i - 1) & (j == FF // tn - 1))
        def _():
            _all_rows(dg_ref)

    row = pl.BlockSpec((tm, D), lambda i, j: (i, 0))
    blk = pl.BlockSpec((tm, tn), lambda i, j: (i, j))
    return _hosted_call(
        body, name=name, grid=(n_i, FF // tn),
        in_specs=[row, pl.BlockSpec((1, D), lambda i, j: (0, 0)), row,
                  pl.BlockSpec((FF, D), lambda i, j: (0, 0)), blk, blk],
        out_specs=[row, blk, blk, pl.BlockSpec((8, D), lambda i, j: (0, 0))],
        out_shape=[jax.ShapeDtypeStruct((S, D), BF16), jax.ShapeDtypeStruct((S, FF), BF16),
                   jax.ShapeDtypeStruct((S, FF), BF16), jax.ShapeDtypeStruct((8, D), F32)],
        scratch_shapes=[pltpu.VMEM((tm, D), BF16)],
        semantics=("arbitrary", "arbitrary"), args=(dx, gate, f, w2, a, b), comm=comm)


def _tn_matmul(a, b, *, tm, tn, name, comm=None):
    S, M = a.shape
    N = b.shape[1]
    ts = _row_tile(S, 1024)
    ns = S // ts

    def body(a_ref, b_ref, o_ref, acc):
        s = pl.program_id(2)
        p = lax.dot_general(a_ref[...], b_ref[...], (((0,), (0,)), ((), ())), preferred_element_type=F32)

        @pl.when(s == 0)
        def _():
            acc[...] = p

        @pl.when(s > 0)
        def _():
            acc[...] += p

        @pl.when(s == ns - 1)
        def _():
            o_ref[...] = acc[...]

    (out,), couts = _hosted_call(
        body, name=name, grid=(M // tm, N // tn, ns),
        in_specs=[pl.BlockSpec((ts, tm), lambda i, j, s: (s, i)), pl.BlockSpec((ts, tn), lambda i, j, s: (s, j))],
        out_specs=[pl.BlockSpec((tm, tn), lambda i, j, s: (i, j))],
        out_shape=[jax.ShapeDtypeStruct((M, N), F32)],
        scratch_shapes=[pltpu.VMEM((tm, tn), F32)],
        semantics=("parallel", "parallel", "arbitrary"), args=(a, b), comm=comm)
    return out if comm is None else (out, couts)


def _dh_normbwd(pairs, x, ng, sc, dx_next, *, name, comm=None):
    S = x.shape[0]
    tm = _row_tile(S, 256)
    n_i = S // tm
    n_p = len(pairs)

    def body(*refs):
        a_refs, w_refs = refs[:n_p], refs[n_p:2 * n_p]
        x_ref, g_ref, sc_ref, dxn_ref, dx_ref, p_ref = refs[2 * n_p:]
        i = pl.program_id(0)
        dh = _dot_nt(a_refs[0][...], w_refs[0][...])
        for k in range(1, n_p):
            dh = dh + _dot_nt(a_refs[k][...], w_refs[k][...])
        xv = x_ref[...]
        r = _rms(xv)
        xh = xv * r
        g = g_ref[...]
        dn = dh * (1.0 + sc_ref[...])
        dy = dn * g
        dx_ref[...] = dxn_ref[...] + r * (dy - xh * jnp.mean(dy * xh, axis=-1, keepdims=True))

        @pl.when(i == 0)
        def _():
            p_ref[...] = jnp.zeros_like(p_ref)

        p_ref[:, 0:D] += _rsum8(dh * (xh * g))
        p_ref[:, D:2 * D] += _rsum8(dh)
        p_ref[:, 2 * D:3 * D] += _rsum8(dn * xh)

        @pl.when(i == n_i - 1)
        def _():
            _all_rows(p_ref)

    row = pl.BlockSpec((tm, D), lambda i: (i, 0))
    vec = pl.BlockSpec((1, D), lambda i: (0, 0))
    in_specs = ([pl.BlockSpec((tm, a.shape[1]), lambda i: (i, 0)) for a, _ in pairs]
                + [pl.BlockSpec(w.shape, lambda i: (0, 0)) for _, w in pairs] + [row, vec, vec, row])
    return _hosted_call(
        body, name=name, grid=(n_i,), in_specs=in_specs,
        out_specs=[row, pl.BlockSpec((8, 3 * D), lambda i: (0, 0))],
        out_shape=[jax.ShapeDtypeStruct((S, D), F32), jax.ShapeDtypeStruct((8, 3 * D), F32)],
        scratch_shapes=[], semantics=("arbitrary",),
        args=(*[a for a, _ in pairs], *[w for _, w in pairs], x, ng, sc, dx_next), comm=comm)


def _final_loss(x3, gfin, tgt):
    S = x3.shape[0]
    tm = _row_tile(S, 512)
    n_i = S // tm

    def body(x_ref, g_ref, t_ref, dx_ref, dg_ref, loss_ref, lacc):
        i = pl.program_id(0)
        xv = x_ref[...]
        r = _rms(xv)
        xh = xv * r
        g = g_ref[...]
        e = xh * g - t_ref[...]
        dout = e * (1.0 / D)
        dy = dout * g
        dx_ref[...] = r * (dy - xh * jnp.mean(dy * xh, axis=-1, keepdims=True))

        @pl.when(i == 0)
        def _():
            dg_ref[...] = jnp.zeros_like(dg_ref)
            lacc[...] = jnp.zeros_like(lacc)

        dg_ref[...] += _rsum8(dout * xh)
        lacc[...] += _rsum8(e * e)

        @pl.when(i == n_i - 1)
        def _():
            _all_rows(dg_ref)
            tot = jnp.sum(jnp.sum(lacc[...], axis=0, keepdims=True), axis=1, keepdims=True)
            loss_ref[...] = jnp.broadcast_to(tot * (0.5 / D), loss_ref.shape)

    row = pl.BlockSpec((tm, D), lambda i: (i, 0))
    return pl.pallas_call(
        body, name="final_loss", grid=(n_i,),
        in_specs=[row, pl.BlockSpec((1, D), lambda i: (0, 0)), row],
        out_specs=[row, pl.BlockSpec((8, D), lambda i: (0, 0)), pl.BlockSpec((8, 128), lambda i: (0, 0))],
        out_shape=[jax.ShapeDtypeStruct((S, D), F32), jax.ShapeDtypeStruct((8, D), F32),
                   jax.ShapeDtypeStruct((8, 128), F32)],
        scratch_shapes=[pltpu.VMEM((8, D), F32)],
        compiler_params=_params(("arbitrary",)),
    )(x3, gfin, tgt)


def _mix_in(x, ng, sh, sc, w_in):
    S = x.shape[0]
    tm = _row_tile(S, 512)

    def body(x_ref, g_ref, sh_ref, sc_ref, w_ref, h_ref, z_ref):
        xv = x_ref[...]
        hb = (((xv * _rms(xv)) * g_ref[...]) * (1.0 + sc_ref[...]) + sh_ref[...]).astype(BF16)
        h_ref[...] = hb
        z_ref[...] = jnp.dot(hb, w_ref[...], preferred_element_type=F32)

    row = pl.BlockSpec((tm, D), lambda i: (i, 0))
    vec = pl.BlockSpec((1, D), lambda i: (0, 0))
    return pl.pallas_call(
        body, name="mix_in", grid=(S // tm,),
        in_specs=[row, vec, vec, vec, pl.BlockSpec((D, Z_COLS), lambda i: (0, 0))],
        out_specs=[row, pl.BlockSpec((tm, Z_COLS), lambda i: (i, 0))],
        out_shape=[jax.ShapeDtypeStruct((S, D), BF16), jax.ShapeDtypeStruct((S, Z_COLS), F32)],
        compiler_params=_params(("parallel",)),
    )(x, ng, sh, sc, w_in)


def _conv_taps(u, halo, rows):
    u1 = jnp.where(rows == 0, halo[7:8, :], pltpu.roll(u, 1, 0))
    u2 = jnp.where(rows == 0, halo[6:7, :], jnp.where(rows == 1, halo[7:8, :], pltpu.roll(u, 2, 0)))
    return u1, u2


def _mix_mid(z, conv_w, gq, gkv, wuq, wukv, cs, sn):
    S = z.shape[0]
    tm = _row_tile(S, 512)
    hb = tm // 8

    def body(z_ref, zh_ref, cw_ref, gq_ref, gkv_ref, wuq_ref, wukv_ref, cs_ref, sn_ref,
             ya_ref, q_ref, k_ref, v_ref, cqn_ref, ckvn_ref):
        i = pl.program_id(0)
        xb = z_ref[:, 0:CONV_W]
        u = z_ref[:, CONV_W:2 * CONV_W] * z_ref[:, 2 * CONV_W:3 * CONV_W]
        halo = zh_ref[:, CONV_W:2 * CONV_W] * zh_ref[:, 2 * CONV_W:3 * CONV_W]
        halo = jnp.where(i > 0, halo, 0.0)
        rows = lax.broadcasted_iota(jnp.int32, (tm, CONV_W), 0)
        u1, u2 = _conv_taps(u, halo, rows)
        y = cw_ref[0:1, :] * u2 + cw_ref[1:2, :] * u1 + cw_ref[2:3, :] * u
        ya_ref[...] = xb * y

        lane = lax.broadcasted_iota(jnp.int32, (tm, 128), 1)
        cs_v, sn_v = cs_ref[...], sn_ref[...]
        cq = z_ref[:, 3 * CONV_W:3 * CONV_W + Q_LORA]
        cqn = ((cq * _rms(cq)) * gq_ref[...]).astype(BF16)
        cqn_ref[...] = cqn
        q = jnp.dot(cqn, wuq_ref[...], preferred_element_type=F32)
        for h in range(HEADS):
            o = h * HEAD_SLOT
            q_ref[:, o:o + 128] = q[:, o:o + 128].astype(BF16)
            q_ref[:, o + 128:o + 256] = _rope(q[:, o + 128:o + 256], cs_v, sn_v, lane).astype(BF16)

        c0 = 3 * CONV_W + Q_LORA
        ckv = z_ref[:, c0:c0 + KV_LORA]
        ckvn = ((ckv * _rms(ckv)) * gkv_ref[...]).astype(BF16)
        ckvn_ref[...] = ckvn
        kv = jnp.dot(ckvn, wukv_ref[...], preferred_element_type=F32)
        krot = _rope(z_ref[:, c0 + KV_LORA:Z_COLS], cs_v, sn_v, lane).astype(BF16)
        for h in range(HEADS):
            o = h * HEAD_SLOT
            k_ref[:, o:o + 128] = kv[:, h * 128:(h + 1) * 128].astype(BF16)
            k_ref[:, o + 128:o + 256] = krot
        v_ref[...] = kv[:, HEADS * 128:].astype(BF16)

    def rows_of(n):
        return pl.BlockSpec((tm, n), lambda i: (i, 0))

    def whole(shape):
        return pl.BlockSpec(shape, lambda i: (0, 0))

    return pl.pallas_call(
        body, name="mix_mid", grid=(S // tm,),
        in_specs=[rows_of(Z_COLS), pl.BlockSpec((8, Z_COLS), lambda i: (jnp.maximum(i * hb - 1, 0), 0)),
                  whole((8, CONV_W)), whole((1, Q_LORA)), whole((1, KV_LORA)),
                  whole((Q_LORA, HEADS * HEAD_SLOT)), whole((KV_LORA, 2 * HEADS * 128)),
                  rows_of(128), rows_of(128)],
        out_specs=[rows_of(CONV_W), rows_of(HEADS * HEAD_SLOT), rows_of(HEADS * HEAD_SLOT), rows_of(HEADS * V_HEAD),
                   rows_of(Q_LORA), rows_of(KV_LORA)],
        out_shape=[jax.ShapeDtypeStruct((S, CONV_W), F32), jax.ShapeDtypeStruct((S, HEADS * HEAD_SLOT), BF16),
                   jax.ShapeDtypeStruct((S, HEADS * HEAD_SLOT), BF16), jax.ShapeDtypeStruct((S, HEADS * V_HEAD), BF16),
                   jax.ShapeDtypeStruct((S, Q_LORA), BF16), jax.ShapeDtypeStruct((S, KV_LORA), BF16)],
        compiler_params=_params(("parallel",)),
    )(z, z, conv_w, gq, gkv, wuq, wukv, cs, sn)


def _att_blocks(S):
    bk = min(512, max(S // 4, 128))
    return 2 * bk, bk


def _pair_tables(S, k_major):
    bq, bk = _att_blocks(S)
    nq, nk = S // bq, S // bk
    vis = lambda qi, ki: ki * bk < (qi + 1) * bq
    if k_major:
        pairs = [(qi, ki) for ki in range(nk) for qi in range(nq) if vis(qi, ki)]
    else:
        pairs = [(qi, ki) for qi in range(nq) for ki in range(nk) if vis(qi, ki)]
    cols = [[p[0] for p in pairs], [p[1] for p in pairs], [int((p[1] + 1) * bk > p[0] * bq) for p in pairs]]
    return [jnp.asarray(np.array(c, np.int32)) for c in cols], len(pairs)


def _chunk_mask(qi, ki, bq, bk):
    r = (qi * bq + lax.broadcasted_iota(jnp.int32, (bq, bk), 0)) // CHUNK
    c = (ki * bk + lax.broadcasted_iota(jnp.int32, (bq, bk), 1)) // CHUNK
    return c <= r


def _attention(q, k, v):
    S = q.shape[0]
    bq, bk = _att_blocks(S)
    last_k = bq // bk - 1
    tables, n_pairs = _pair_tables(S, k_major=False)

    def body(qi_ref, ki_ref, mk_ref, q_ref, k_ref, v_ref, o_ref, lse_ref, m_s, l_s, acc_s):
        p_id = pl.program_id(1)
        qi, ki = qi_ref[p_id], ki_ref[p_id]

        @pl.when(ki == 0)
        def _():
            m_s[...] = jnp.full_like(m_s, NEG)
            l_s[...] = jnp.zeros_like(l_s)
            acc_s[...] = jnp.zeros_like(acc_s)

        def update(masked):
            s = lax.dot_general(q_ref[...], k_ref[...], (((1,), (1,)), ((), ())), preferred_element_type=F32)
            if masked:
                s = jnp.where(_chunk_mask(qi, ki, bq, bk), s, NEG)
            m_prev = m_s[...]
            m_new = jnp.maximum(m_prev, jnp.max(s, axis=1, keepdims=True))
            alpha = jnp.exp2((m_prev - m_new) * EXP2_SCALE)
            p = jnp.exp2((s - jnp.tile(m_new, (1, bk // 128))) * EXP2_SCALE)
            l_s[...] = alpha * l_s[...] + jnp.sum(p, axis=1, keepdims=True)
            acc_s[...] = alpha * acc_s[...] + jnp.dot(p.astype(BF16), v_ref[...], preferred_element_type=F32)
            m_s[...] = m_new

        @pl.when(mk_ref[p_id] == 0)
        def _():
            update(False)

        @pl.when(mk_ref[p_id] == 1)
        def _():
            update(True)

        @pl.when(ki == qi * (last_k + 1) + last_k)
        def _():
            l = l_s[...]
            o_ref[...] = acc_s[...] / l
            lse_ref[...] = m_s[...] * EXP2_SCALE + jnp.log2(l)

    grid_spec = pltpu.PrefetchScalarGridSpec(
        num_scalar_prefetch=3, grid=(HEADS, n_pairs),
        in_specs=[pl.BlockSpec((bq, HEAD_SLOT), lambda h, p, qt, kt, mt: (qt[p], h)),
                  pl.BlockSpec((bk, HEAD_SLOT), lambda h, p, qt, kt, mt: (kt[p], h)),
                  pl.BlockSpec((bk, V_HEAD), lambda h, p, qt, kt, mt: (kt[p], h))],
        out_specs=[pl.BlockSpec((bq, V_HEAD), lambda h, p, qt, kt, mt: (qt[p], h))] * 2,
        scratch_shapes=[pltpu.VMEM((bq, V_HEAD), F32)] * 3)
    return pl.pallas_call(
        body, name="attention", grid_spec=grid_spec,
        out_shape=[jax.ShapeDtypeStruct((S, HEADS * V_HEAD), F32)] * 2,
        compiler_params=_params(("arbitrary", "arbitrary")),
    )(*tables, q, k, v)


def _attention_bwd(q, k, v, do, lse2, delta):
    S = q.shape[0]
    bq, bk = _att_blocks(S)
    nq = S // bq
    tables, n_pairs = _pair_tables(S, k_major=True)

    def body(qi_ref, ki_ref, mk_ref, q_ref, k_ref, v_ref, do_ref, lse_ref, dl_ref, dq_hbm, dk_ref, dv_ref,
             dq_s, dk_s, dv_s, sem):
        head, p_id = pl.program_id(0), pl.program_id(1)
        qi, ki = qi_ref[p_id], ki_ref[p_id]
        rows = pl.ds(pl.multiple_of(qi * bq, bq), bq)

        @pl.when(qi * bq <= ki * bk)
        def _():
            dk_s[...] = jnp.zeros_like(dk_s)
            dv_s[...] = jnp.zeros_like(dv_s)

        def update(masked):
            qv, kv, dov = q_ref[...], k_ref[...], do_ref[...]
            s = lax.dot_general(qv, kv, (((1,), (1,)), ((), ())), preferred_element_type=F32)
            dp = lax.dot_general(dov, v_ref[...], (((1,), (1,)), ((), ())), preferred_element_type=F32)
            if masked:
                s = jnp.where(_chunk_mask(qi, ki, bq, bk), s, NEG)
            p = jnp.exp2(s * EXP2_SCALE - jnp.tile(lse_ref[...], (1, bk // 128)))
            dv_s[...] += lax.dot_general(p.astype(BF16), dov, (((0,), (0,)), ((), ())), preferred_element_type=F32)
            ds = (p * (dp - jnp.tile(dl_ref[...], (1, bk // 128)))).astype(BF16)
            dk_s[...] += lax.dot_general(ds, qv, (((0,), (0,)), ((), ())), preferred_element_type=F32)
            dq = jnp.dot(ds, kv, preferred_element_type=F32)

            @pl.when(ki == 0)
            def _():
                dq_s[rows, :] = dq

            @pl.when(ki > 0)
            def _():
                dq_s[rows, :] += dq

        @pl.when(mk_ref[p_id] == 0)
        def _():
            update(False)

        @pl.when(mk_ref[p_id] == 1)
        def _():
            update(True)

        @pl.when(qi == nq - 1)
        def _():
            dk_ref[...] = dk_s[...] * ATT_SCALE
            dv_ref[...] = dv_s[...]

        @pl.when(p_id == n_pairs - 1)
        def _():
            dq_s[...] = dq_s[...] * ATT_SCALE
            out = pltpu.make_async_copy(
                dq_s, dq_hbm.at[:, pl.ds(pl.multiple_of(head * HEAD_SLOT, HEAD_SLOT), HEAD_SLOT)], sem)
            out.start()
            out.wait()

    grid_spec = pltpu.PrefetchScalarGridSpec(
        num_scalar_prefetch=3, grid=(HEADS, n_pairs),
        in_specs=[pl.BlockSpec((bq, HEAD_SLOT), lambda h, p, qt, kt, mt: (qt[p], h)),
                  pl.BlockSpec((bk, HEAD_SLOT), lambda h, p, qt, kt, mt: (kt[p], h)),
                  pl.BlockSpec((bk, V_HEAD), lambda h, p, qt, kt, mt: (kt[p], h)),
                  pl.BlockSpec((bq, V_HEAD), lambda h, p, qt, kt, mt: (qt[p], h)),
                  pl.BlockSpec((bq, V_HEAD), lambda h, p, qt, kt, mt: (qt[p], h)),
                  pl.BlockSpec((bq, V_HEAD), lambda h, p, qt, kt, mt: (qt[p], h))],
        out_specs=[pl.BlockSpec(memory_space=pl.ANY),
                   pl.BlockSpec((bk, HEAD_SLOT), lambda h, p, qt, kt, mt: (kt[p], h)),
                   pl.BlockSpec((bk, V_HEAD), lambda h, p, qt, kt, mt: (kt[p], h))],
        scratch_shapes=[pltpu.VMEM((S, HEAD_SLOT), F32), pltpu.VMEM((bk, HEAD_SLOT), F32),
                        pltpu.VMEM((bk, V_HEAD), F32), pltpu.SemaphoreType.DMA])
    return pl.pallas_call(
        body, name="attention_bwd", grid_spec=grid_spec,
        out_shape=[jax.ShapeDtypeStruct((S, HEADS * HEAD_SLOT), F32), jax.ShapeDtypeStruct((S, HEADS * HEAD_SLOT), F32),
                   jax.ShapeDtypeStruct((S, HEADS * V_HEAD), F32)],
        compiler_params=_params(("arbitrary", "arbitrary")),
    )(*tables, q, k, v, do, lse2, delta)


def _group_mats():
    def blockdiag(n, g):
        idx = np.arange(n) // g
        return jnp.asarray((idx[:, None] == idx[None, :]).astype(np.float32), dtype=BF16)
    return blockdiag(CONV_W, CONV_GROUP), blockdiag(HEADS * V_HEAD, V_HEAD)


def _mix_out(ya, o, gout, w_out, x, gate, ga, gb):
    S = x.shape[0]
    tm = _row_tile(S, 512)

    def body(ya_ref, o_ref, go_ref, w_ref, x_ref, g_ref, ga_ref, gb_ref, xo_ref, yn_ref, yo_ref):
        yav, ov = ya_ref[...], o_ref[...]
        ra = lax.rsqrt(_gsum(yav * yav, ga_ref[...]) * (1.0 / CONV_GROUP) + EPS)
        rb = lax.rsqrt(_gsum(ov * ov, gb_ref[...]) * (1.0 / V_HEAD) + EPS)
        na = ((yav * ra) * go_ref[:, 0:CONV_W]).astype(BF16)
        nb = ((ov * rb) * go_ref[:, CONV_W:]).astype(BF16)
        yn_ref[:, 0:CONV_W] = na
        yn_ref[:, CONV_W:] = nb
        yo = (jnp.dot(na, w_ref[0:CONV_W, :], preferred_element_type=F32)
              + jnp.dot(nb, w_ref[CONV_W:, :], preferred_element_type=F32))
        xo_ref[...] = x_ref[...] + g_ref[...] * yo
        yo_ref[...] = yo.astype(BF16)

    row = pl.BlockSpec((tm, D), lambda i: (i, 0))
    half = pl.BlockSpec((tm, CONV_W), lambda i: (i, 0))
    vec = pl.BlockSpec((1, D), lambda i: (0, 0))
    sq = pl.BlockSpec((CONV_W, CONV_W), lambda i: (0, 0))
    return pl.pallas_call(
        body, name="mix_out", grid=(S // tm,),
        in_specs=[half, half, vec, pl.BlockSpec((D, D), lambda i: (0, 0)), row, vec, sq, sq],
        out_specs=[row, row, row],
        out_shape=[jax.ShapeDtypeStruct((S, D), F32), jax.ShapeDtypeStruct((S, D), BF16),
                   jax.ShapeDtypeStruct((S, D), BF16)],
        compiler_params=_params(("parallel",)),
    )(ya, o, gout, w_out, x, gate, ga, gb)


def _mix_out_bwd(dx, gate, yo, w_out, ya, o, gout, ga, gb, comm=None):
    S = dx.shape[0]
    tm = _row_tile(S, 256)
    n_i = S // tm

    def norm_bwd(v, dn, gain, gmat, inv_n):
        r = lax.rsqrt(_gsum(v * v, gmat) * inv_n + EPS)
        vh = v * r
        dy = dn * gain
        return r * (dy - vh * (_gsum(dy * vh, gmat) * inv_n)), dn * vh

    def body(dx_ref, g_ref, yo_ref, w_ref, ya_ref, o_ref, go_ref, ga_ref, gb_ref,
             dyo_ref, dya_ref, do_ref, dl_ref, p_ref):
        i = pl.program_id(0)
        dxv = dx_ref[...]
        dyo = (dxv * g_ref[...]).astype(BF16)
        dyo_ref[...] = dyo
        dyn = _dot_nt(dyo, w_ref[...])
        dya, dga = norm_bwd(ya_ref[...], dyn[:, 0:CONV_W], go_ref[:, 0:CONV_W], ga_ref[...], 1.0 / CONV_GROUP)
        ov = o_ref[...]
        do, dgb = norm_bwd(ov, dyn[:, CONV_W:], go_ref[:, CONV_W:], gb_ref[...], 1.0 / V_HEAD)
        dya_ref[...] = dya
        do_ref[...] = do.astype(BF16)
        dl_ref[...] = _gsum(do * ov, gb_ref[...])

        @pl.when(i == 0)
        def _():
            p_ref[...] = jnp.zeros_like(p_ref)

        p_ref[:, 0:D] += _rsum8(dxv * yo_ref[...].astype(F32))
        p_ref[:, D:D + CONV_W] += _rsum8(dga)
        p_ref[:, D + CONV_W:2 * D] += _rsum8(dgb)

        @pl.when(i == n_i - 1)
        def _():
            _all_rows(p_ref)

    row = pl.BlockSpec((tm, D), lambda i: (i, 0))
    half = pl.BlockSpec((tm, CONV_W), lambda i: (i, 0))
    vec = pl.BlockSpec((1, D), lambda i: (0, 0))
    sq = pl.BlockSpec((CONV_W, CONV_W), lambda i: (0, 0))
    return _hosted_call(
        body, name="mix_out_bwd", grid=(n_i,),
        in_specs=[row, vec, row, pl.BlockSpec((D, D), lambda i: (0, 0)), half, half, vec, sq, sq],
        out_specs=[row, half, half, half, pl.BlockSpec((8, 2 * D), lambda i: (0, 0))],
        out_shape=[jax.ShapeDtypeStruct((S, D), BF16), jax.ShapeDtypeStruct((S, CONV_W), F32),
                   jax.ShapeDtypeStruct((S, CONV_W), BF16), jax.ShapeDtypeStruct((S, CONV_W), F32),
                   jax.ShapeDtypeStruct((8, 2 * D), F32)],
        scratch_shapes=[], semantics=("arbitrary",), args=(dx, gate, yo, w_out, ya, o, gout, ga, gb), comm=comm)


MID_SUMS = 3 * CONV_W + Q_LORA + KV_LORA


def _mix_mid_bwd(z, dya, conv_w, gq, gkv, wuq, wukv, cs, sn, dq, dk, dv, comm=None):
    S = z.shape[0]
    tm = _row_tile(S, 256)
    n_i = S // tm
    hb = tm // 8
    last_blk = S // 8 - 1

    def latent_bwd(cv, dcn, gain):
        r = _rms(cv)
        ch = cv * r
        dy = dcn * gain
        return r * (dy - ch * jnp.mean(dy * ch, axis=-1, keepdims=True)), dcn * ch

    def body(z_ref, zp_ref, zn_ref, dya_ref, dyan_ref, cw_ref, gq_ref, gkv_ref, wuq_ref, wukv_ref, cs_ref, sn_ref,
             dq_ref, dk_ref, dv_ref, dz_ref, dqf_ref, dkvf_ref, p_ref):
        i = pl.program_id(0)
        xb, xc, xu = z_ref[:, 0:CONV_W], z_ref[:, CONV_W:2 * CONV_W], z_ref[:, 2 * CONV_W:3 * CONV_W]
        u = xc * xu
        halo = jnp.where(i > 0, zp_ref[:, CONV_W:2 * CONV_W] * zp_ref[:, 2 * CONV_W:3 * CONV_W], 0.0)
        rows = lax.broadcasted_iota(jnp.int32, (tm, CONV_W), 0)
        u1, u2 = _conv_taps(u, halo, rows)
        w0, w1, w2 = cw_ref[0:1, :], cw_ref[1:2, :], cw_ref[2:3, :]
        y = w0 * u2 + w1 * u1 + w2 * u
        dyav = dya_ref[...]
        dy = dyav * xb
        nxt = jnp.where(i < n_i - 1, dyan_ref[...] * zn_ref[:, 0:CONV_W], 0.0)
        dy1 = jnp.where(rows == tm - 1, nxt[0:1, :], pltpu.roll(dy, tm - 1, 0))
        dy2 = jnp.where(rows == tm - 1, nxt[1:2, :], jnp.where(rows == tm - 2, nxt[0:1, :], pltpu.roll(dy, tm - 2, 0)))
        du = w2 * dy + w1 * dy1 + w0 * dy2
        dz_ref[:, 0:CONV_W] = (dyav * y).astype(BF16)
        dz_ref[:, CONV_W:2 * CONV_W] = (du * xu).astype(BF16)
        dz_ref[:, 2 * CONV_W:3 * CONV_W] = (du * xc).astype(BF16)

        lane = lax.broadcasted_iota(jnp.int32, (tm, 128), 1)
        cs_v, sn_v = cs_ref[...], sn_ref[...]
        dkr = jnp.zeros((tm, 128), F32)
        for h in range(HEADS):
            o = h * HEAD_SLOT
            dqf_ref[:, o:o + 128] = dq_ref[:, o:o + 128].astype(BF16)
            dqf_ref[:, o + 128:o + 256] = _rope_t(dq_ref[:, o + 128:o + 256], cs_v, sn_v, lane).astype(BF16)
            dkvf_ref[:, h * 128:(h + 1) * 128] = dk_ref[:, o:o + 128].astype(BF16)
            dkr = dkr + dk_ref[:, o + 128:o + 256]
        dkvf_ref[:, HEADS * 128:] = dv_ref[...].astype(BF16)

        c0 = 3 * CONV_W
        dcqn = _dot_nt(dqf_ref[...], wuq_ref[...])
        dcq, dgq = latent_bwd(z_ref[:, c0:c0 + Q_LORA], dcqn, gq_ref[...])
        dz_ref[:, c0:c0 + Q_LORA] = dcq.astype(BF16)
        c1 = c0 + Q_LORA
        dckvn = _dot_nt(dkvf_ref[...], wukv_ref[...])
        dckv, dgkv = latent_bwd(z_ref[:, c1:c1 + KV_LORA], dckvn, gkv_ref[...])
        dz_ref[:, c1:c1 + KV_LORA] = dckv.astype(BF16)
        dz_ref[:, c1 + KV_LORA:Z_COLS] = _rope_t(dkr, cs_v, sn_v, lane).astype(BF16)

        @pl.when(i == 0)
        def _():
            p_ref[...] = jnp.zeros_like(p_ref)

        p_ref[:, 0:CONV_W] += _rsum8(dy * u2)
        p_ref[:, CONV_W:2 * CONV_W] += _rsum8(dy * u1)
        p_ref[:, 2 * CONV_W:3 * CONV_W] += _rsum8(dy * u)
        p_ref[:, c0:c0 + Q_LORA] += _rsum8(dgq)
        p_ref[:, c1:c1 + KV_LORA] += _rsum8(dgkv)

        @pl.when(i == n_i - 1)
        def _():
            _all_rows(p_ref)

    def rows_of(n):
        return pl.BlockSpec((tm, n), lambda i: (i, 0))

    def whole(shape):
        return pl.BlockSpec(shape, lambda i: (0, 0))

    def prev8(n):
        return pl.BlockSpec((8, n), lambda i: (jnp.maximum(i * hb - 1, 0), 0))

    def next8(n):
        return pl.BlockSpec((8, n), lambda i: (jnp.minimum((i + 1) * hb, last_blk), 0))

    return _hosted_call(
        body, name="mix_mid_bwd", grid=(n_i,),
        in_specs=[rows_of(Z_COLS), prev8(Z_COLS), next8(Z_COLS), rows_of(CONV_W), next8(CONV_W),
                  whole((8, CONV_W)), whole((1, Q_LORA)), whole((1, KV_LORA)),
                  whole((Q_LORA, HEADS * HEAD_SLOT)), whole((KV_LORA, 2 * HEADS * 128)),
                  rows_of(128), rows_of(128),
                  rows_of(HEADS * HEAD_SLOT), rows_of(HEADS * HEAD_SLOT), rows_of(HEADS * V_HEAD)],
        out_specs=[rows_of(Z_COLS), rows_of(HEADS * HEAD_SLOT), rows_of(2 * HEADS * 128), whole((8, MID_SUMS))],
        out_shape=[jax.ShapeDtypeStruct((S, Z_COLS), BF16), jax.ShapeDtypeStruct((S, HEADS * HEAD_SLOT), BF16),
                   jax.ShapeDtypeStruct((S, 2 * HEADS * 128), BF16), jax.ShapeDtypeStruct((8, MID_SUMS), F32)],
        scratch_shapes=[], semantics=("arbitrary",),
        args=(z, z, z, dya, dya, conv_w, gq, gkv, wuq, wukv, cs, sn, dq, dk, dv), comm=comm)


ADA_Q = N_MOD * D // N_CHIPS
ADA_TN = 768


def _ada_forward(c_all, ada_w_q, ada_b_q):
    def body(c_ref, w_ref, b_ref, o_ref):
        cv = c_ref[...]
        sc = (cv * jax.nn.sigmoid(cv)).astype(BF16)
        o_ref[...] = jnp.dot(sc, w_ref[...].astype(BF16), preferred_element_type=F32) + b_ref[...]

    return pl.pallas_call(
        body, name="ada_forward", grid=(ADA_Q // ADA_TN,),
        in_specs=[pl.BlockSpec((16, D), lambda j: (0, 0)), pl.BlockSpec((D, ADA_TN), lambda j: (0, j)),
                  pl.BlockSpec((1, ADA_TN), lambda j: (0, j))],
        out_specs=pl.BlockSpec((16, ADA_TN), lambda j: (0, j)),
        out_shape=jax.ShapeDtypeStruct((16, ADA_Q), F32),
        compiler_params=_params(("parallel",)),
    )(c_all, ada_w_q, ada_b_q)


def _ada_wgrad(c_all, dmod_q):
    def body(c_ref, d_ref, o_ref):
        cv = c_ref[...]
        sc = (cv * jax.nn.sigmoid(cv)).astype(BF16)
        o_ref[...] = lax.dot_general(sc, d_ref[...].astype(BF16), (((0,), (0,)), ((), ())),
                                     preferred_element_type=F32)

    return pl.pallas_call(
        body, name="ada_wgrad", grid=(ADA_Q // ADA_TN,),
        in_specs=[pl.BlockSpec((16, D), lambda j: (0, 0)), pl.BlockSpec((16, ADA_TN), lambda j: (0, j))],
        out_specs=pl.BlockSpec((D, ADA_TN), lambda j: (0, j)),
        out_shape=jax.ShapeDtypeStruct((D, ADA_Q), F32),
        compiler_params=_params(("parallel",)),
    )(c_all, dmod_q)


def _sum_devices(parts):
    n = parts.shape[1]

    def body(p_ref, o_ref):
        o_ref[...] = jnp.broadcast_to(jnp.sum(p_ref[...], axis=0, keepdims=True), o_ref.shape)

    return pl.pallas_call(
        body, name="sum_devices",
        in_specs=[pl.BlockSpec((N_DEV, n), lambda: (0, 0))], out_specs=pl.BlockSpec((N_DEV, n), lambda: (0, 0)),
        out_shape=jax.ShapeDtypeStruct((N_DEV, n), F32),
    )(parts)


def _adamw(w, g, m, v, *, name):
    rows, cols = w.shape
    tr = _row_tile(rows, 256)

    def body(w_ref, g_ref, m_ref, v_ref, d_ref, mo_ref, vo_ref):
        gv = g_ref[...]
        mn = B1 * m_ref[...] + (1.0 - B1) * gv
        vn = B2 * v_ref[...] + (1.0 - B2) * (gv * gv)
        m_hat = mn / (1.0 - B1 ** STEP)
        v_hat = vn / (1.0 - B2 ** STEP)
        d_ref[...] = -LR * (m_hat / (jnp.sqrt(v_hat) + AEPS) + WD * w_ref[...])
        mo_ref[...] = mn
        vo_ref[...] = vn

    blk = pl.BlockSpec((tr, cols), lambda i: (i, 0))
    return pl.pallas_call(
        body, name=name, grid=(rows // tr,), in_specs=[blk] * 4, out_specs=[blk] * 3,
        out_shape=[jax.ShapeDtypeStruct((rows, cols), F32)] * 3,
        compiler_params=_params(("parallel",)),
    )(w, g, m, v)


def _small_allgather(v, *, name):
    m, n = v.shape

    def body(x_ref, out_ref, send_sems, recv_sems, local_sem):
        x, y, c = _place()
        me, sibling = (x, y, c), (x, y, 1 - c)
        chips = [(1 - x, y), (x, 1 - y), (1 - x, 1 - y)]

        def rows(px, py, pc):
            return out_ref.at[pl.ds((4 * px + 2 * py + pc) * m, m), :]

        def copy(k, block, to, src=None):
            return pltpu.make_async_remote_copy(
                src_ref=rows(*block) if src is None else src, dst_ref=rows(*block),
                send_sem=send_sems.at[k], recv_sem=recv_sems.at[k], device_id=to, device_id_type=MESH)

        mine = pltpu.make_async_copy(x_ref, rows(*me), local_sem)
        mine.start()
        first = [copy(0, me, sibling, src=x_ref)]
        first += [copy(1 + j, me, (*chip, c), src=x_ref) for j, chip in enumerate(chips)]
        for cp in first:
            cp.start()
        passed = [copy(4 + j, (*chip, c), sibling) for j, chip in enumerate(chips)]
        for j, chip in enumerate(chips):
            copy(1 + j, (*chip, c), me).wait_recv()
            passed[j].start()
        copy(0, sibling, me).wait_recv()
        for j, chip in enumerate(chips):
            copy(4 + j, (*chip, 1 - c), me).wait_recv()
        for cp in first + passed:
            cp.wait_send()
        mine.wait()

    return pl.pallas_call(
        body, name=name,
        out_shape=jax.ShapeDtypeStruct((N_DEV * m, n), v.dtype),
        in_specs=[pl.BlockSpec(memory_space=pltpu.VMEM)], out_specs=pl.BlockSpec(memory_space=pltpu.VMEM),
        scratch_shapes=[pltpu.SemaphoreType.DMA((7,)), pltpu.SemaphoreType.DMA((7,)), pltpu.SemaphoreType.DMA],
    )(v)


ADD_BLOCKS = 2


def _pair_add(place, gs, ts, *, name):
    n_a = len(gs)

    def body(pl_ref, *refs):
        g_refs, t_refs = refs[:n_a], refs[n_a:2 * n_a]
        pf_refs, pb_refs = refs[2 * n_a:3 * n_a], refs[3 * n_a:]
        for g_ref, t_ref, pf_ref, pb_ref in zip(g_refs, t_refs, pf_refs, pb_refs):
            s = g_ref[...] + t_ref[...]
            pf_ref[...] = s
            pb_ref[...] = s.astype(BF16)

    def blk(t, own_half):
        tr = t.shape[1] // ADD_BLOCKS
        if own_half:
            return pl.BlockSpec((1, tr, t.shape[2]), lambda q, r, p: (q, p[0] * ADD_BLOCKS + r, 0))
        return pl.BlockSpec((1, tr, t.shape[2]), lambda q, r, p: (q, r, 0))

    grid_spec = pltpu.PrefetchScalarGridSpec(
        num_scalar_prefetch=1, grid=(N_CHIPS, ADD_BLOCKS),
        in_specs=[blk(t, True) for t in ts] + [blk(t, False) for t in ts],
        out_specs=[blk(t, False) for t in ts] * 2)
    res = pl.pallas_call(
        body, name=name, grid_spec=grid_spec,
        out_shape=[jax.ShapeDtypeStruct(t.shape, F32) for t in ts] + [jax.ShapeDtypeStruct(t.shape, BF16) for t in ts],
        compiler_params=_params(("parallel", "parallel")),
    )(place, *gs, *ts)
    return list(res[:n_a]), list(res[n_a:])


def _chip_add(place, pfs, ts, *, name):
    n_a = len(pfs)

    def body(pl_ref, *refs):
        pf_refs, t_refs, o_refs = refs[:n_a], refs[n_a:4 * n_a], refs[4 * n_a:]
        for i, (pf_ref, o_ref) in enumerate(zip(pf_refs, o_refs)):
            t1, t2, t3 = t_refs[3 * i:3 * i + 3]
            o_ref[...] = ((pf_ref[0] + t1[0].astype(F32)) + t2[0].astype(F32)) + t3[0].astype(F32)

    def slot(t, j):
        return pl.BlockSpec((1, t.shape[1] // ADD_BLOCKS, t.shape[2]), lambda r, p: (p[1] ^ j, r, 0))

    grid_spec = pltpu.PrefetchScalarGridSpec(
        num_scalar_prefetch=1, grid=(ADD_BLOCKS,),
        in_specs=[slot(t, 0) for t in pfs] + [slot(t, j) for t in ts for j in (1, 2, 3)],
        out_specs=[pl.BlockSpec((t.shape[1] // ADD_BLOCKS, t.shape[2]), lambda r, p: (r, 0)) for t in pfs])
    res = pl.pallas_call(
        body, name=name, grid_spec=grid_spec,
        out_shape=[jax.ShapeDtypeStruct(t.shape[1:], F32) for t in pfs],
        compiler_params=_params(("parallel",)),
    )(place, *pfs, *[t for t in ts for _ in range(3)])
    return list(res)


BULK = [("ffn1_w1", D, FF // 4, True), ("ffn1_w3", D, FF // 4, True), ("ffn1_w2", FF // 4, D, False),
        ("w_in", D, IN_COLS // 4, True), ("w_uq", Q_LORA, 768 // 4, True), ("w_ukv", KV_LORA, 1024 // 4, True),
        ("w_out", D // 4, D, False),
        ("ffn2_w1", D, FF // 4, True), ("ffn2_w3", D, FF // 4, True), ("ffn2_w2", FF // 4, D, False)]


def _group(*names):
    return [b for b in BULK if b[0] in names]


W_FIRST = _group("ffn1_w1", "ffn1_w3")
W_REST = [b for b in BULK if b not in W_FIRST]
G_FFN2 = _group("ffn2_w1", "ffn2_w3", "ffn2_w2")
G_MIX = _group("w_in", "w_uq", "w_ukv", "w_out")
G_FFN1 = _group("ffn1_w1", "ffn1_w3", "ffn1_w2")


def _gathered_weights(specs, shards, got, myq):
    out = {}
    for (name, _, _, by_cols), part in zip(specs, got):
        part = lax.dynamic_update_slice_in_dim(part, shards[name][None], myq, axis=0)
        out[name] = _full_weight(part, by_cols)
    return out


def _full_weight(parts, by_cols):
    if by_cols:
        return jnp.transpose(parts, (1, 0, 2)).reshape(parts.shape[1], -1)
    return parts.reshape(-1, parts.shape[2])


def _quarters(g, by_cols):
    if by_cols:
        k, n = g.shape
        return jnp.transpose(g.reshape(k, N_CHIPS, n // N_CHIPS), (1, 0, 2))
    return g.reshape(N_CHIPS, g.shape[0] // N_CHIPS, g.shape[1])


def _pad_heads(w_uq):
    w = w_uq.reshape(Q_LORA, HEADS, QK_NOPE + QK_ROPE)
    return jnp.pad(w, ((0, 0), (0, 0), (0, HEAD_SLOT - QK_NOPE - QK_ROPE))).reshape(Q_LORA, HEADS * HEAD_SLOT)


def _unpad_heads(g):
    return g.reshape(Q_LORA, HEADS, HEAD_SLOT)[:, :, :QK_NOPE + QK_ROPE].reshape(Q_LORA, HEADS * (QK_NOPE + QK_ROPE))


def _split_kv(w_ukv):
    return jnp.transpose(w_ukv.reshape(KV_LORA, HEADS, 2, 128), (0, 2, 1, 3)).reshape(KV_LORA, 2 * HEADS * 128)


def _merge_kv(g):
    return jnp.transpose(g.reshape(KV_LORA, 2, HEADS, 128), (0, 2, 1, 3)).reshape(KV_LORA, 2 * HEADS * 128)


def _rope_tables(positions):
    inv_freq = ROPE_THETA ** (-jnp.arange(0, QK_ROPE, 2, dtype=F32) / QK_ROPE)
    ang = positions.astype(F32)[:, None] * inv_freq
    cos, sin, zero = jnp.cos(ang), jnp.sin(ang), jnp.zeros((positions.shape[0], 64), F32)
    return jnp.concatenate([cos, cos, zero], axis=1), jnp.concatenate([sin, sin, zero], axis=1)


def _reduce_tail(place, specs, pfs, t2s, tag, host=None):
    rhs = _chip_add(place, pfs, t2s, name=tag + "_chip_add")
    if host is None:
        out, others = None, _run_comm(_PairShare(rhs), name=tag + "_pair_share")
    else:
        out, others = host(_PairShare(rhs))
    south = place[0] == 0
    return out, {b[0]: jnp.concatenate([jnp.where(south, rh, ot), jnp.where(south, ot, rh)], axis=0)
                 for b, rh, ot in zip(specs, rhs, others)}


def _local_step(x, positions, target, mod, vec, conv_w, w_first, rest_shards, place):
    row = lambda k: mod[k:k + 1]
    sh1, sc1, g1, sh2, sc2, g2, sh3, sc3, g3 = [row(k) for k in range(N_MOD)]
    cs, sn = _rope_tables(positions)
    cw8 = jnp.pad(conv_w, ((0, 5), (0, 0)))
    ga, gb = _group_mats()
    dist = place is not None
    comm = lambda prog: prog if dist else None

    gather = _Gather([rest_shards[b[0]] for b in W_REST]) if dist else None
    (h1, a1, b1, u1), got = _ffn_up(x, vec["norm_ffn1_g"], sh1, sc1, w_first["ffn1_w1"], w_first["ffn1_w3"],
                                    name="ffn1_up", comm=gather)
    if dist:
        w = _gathered_weights(W_REST, rest_shards, got, place[1])
    else:
        w = dict(rest_shards)
    w.update(w_first)
    w_in = jnp.pad(w["w_in"], ((0, 0), (0, Z_COLS - IN_COLS)))
    wuq = _pad_heads(w["w_uq"])
    wukv = _split_kv(w["w_ukv"])
    x1, f1 = _ffn_down(u1, w["ffn1_w2"], x, g1, name="ffn1_down")
    h2, z = _mix_in(x1, vec["norm_mix_g"], sh2, sc2, w_in)
    ya, q, k, v, cqn, ckvn = _mix_mid(z, cw8, vec["q_norm_g"], vec["kv_norm_g"], wuq, wukv, cs, sn)
    o, lse = _attention(q, k, v)
    x2, yn, yo = _mix_out(ya, o, vec["out_norm_g"], w["w_out"], x1, g2, ga, gb)
    (h3, a3, b3, u3), _ = _ffn_up(x2, vec["norm_ffn2_g"], sh3, sc3, w["ffn2_w1"], w["ffn2_w3"], name="ffn2_up")
    x3, f3 = _ffn_down(u3, w["ffn2_w2"], x2, g3, name="ffn2_down")
    dx3, dgfin, loss_blk = _final_loss(x3, vec["final_norm_g"], target)

    grads, reduced = {}, {}

    def tn(a, b, tm, tn_, name, prog=None):
        if prog is None:
            return _tn_matmul(a, b, tm=tm, tn=tn_, name=name), None
        return _tn_matmul(a, b, tm=tm, tn=tn_, name=name, comm=prog)

    def slab_of(specs):
        return [_quarters(grads[n], by_cols) for n, _, _, by_cols in specs]

    (df3, da3, db3, dg3), _ = _ffn_bwd_du(dx3, g3, f3, w["ffn2_w2"], a3, b3, name="ffn2_bwd_du")
    grads["ffn2_w2"], _ = tn(u3, df3, FF // 2, D, "ffn2_dw2")
    grads["ffn2_w1"], _ = tn(h3, da3, D, FF // 2, "ffn2_dw1")
    grads["ffn2_w3"], _ = tn(h3, db3, D, FF // 2, "ffn2_dw3")
    (dx2, s3), _ = _dh_normbwd([(da3, w["ffn2_w1"]), (db3, w["ffn2_w3"])], x2, vec["norm_ffn2_g"], sc3, dx3,
                               name="ffn2_bwd_dh")

    p1 = slab_of(G_FFN2) if dist else None
    (dyo, dya, do, delta, s_out), t1 = _mix_out_bwd(dx2, g2, yo, w["w_out"], ya, o, vec["out_norm_g"], ga, gb,
                                                    comm=comm(_PairExchange(p1) if dist else None))
    grads["w_out"], _ = tn(yn, dyo, D, D, "dw_out")
    if dist:
        pf1, pb1 = _pair_add(place, p1, t1, name="ffn2g_pair_add")
    dq, dk, dv = _attention_bwd(q, k, v, do, lse, delta)
    (dz, dqf, dkvf, s_mid), t2 = _mix_mid_bwd(z, dya, cw8, vec["q_norm_g"], vec["kv_norm_g"], wuq, wukv, cs, sn,
                                              dq, dk, dv, comm=comm(_ChipExchange(pb1) if dist else None))
    g_uq, _ = tn(cqn, dqf, Q_LORA, HEADS * HEAD_SLOT, "dw_uq")
    g_ukv, _ = tn(ckvn, dkvf, KV_LORA, 2 * HEADS * 128, "dw_ukv")
    grads["w_uq"], grads["w_ukv"] = _unpad_heads(g_uq), _merge_kv(g_ukv)
    if dist:
        g_in, red = _reduce_tail(place, G_FFN2, pf1, t2, "ffn2g",
                                 host=lambda prog: tn(h2, dz, D, Z_COLS // 2, "dw_in", prog))
        reduced.update(red)
    else:
        g_in, _ = tn(h2, dz, D, Z_COLS // 2, "dw_in")
    grads["w_in"] = g_in[:, :IN_COLS]
    (dx1, s2), _ = _dh_normbwd([(dz, w_in)], x1, vec["norm_mix_g"], sc2, dx2, name="mix_bwd_dh")

    p2 = slab_of(G_MIX) if dist else None
    (df1, da1, db1, dg1), t1 = _ffn_bwd_du(dx1, g1, f1, w["ffn1_w2"], a1, b1, name="ffn1_bwd_du",
                                           comm=comm(_PairExchange(p2) if dist else None))
    if dist:
        pf2, pb2 = _pair_add(place, p2, t1, name="mixg_pair_add")
        grads["ffn1_w2"], t2 = tn(u1, df1, FF // 2, D, "ffn1_dw2", _ChipExchange(pb2))
        grads["ffn1_w1"], red = _reduce_tail(place, G_MIX, pf2, t2, "mixg",
                                             host=lambda prog: tn(h1, da1, D, FF // 2, "ffn1_dw1", prog))
        reduced.update(red)
    else:
        grads["ffn1_w2"], _ = tn(u1, df1, FF // 2, D, "ffn1_dw2")
        grads["ffn1_w1"], _ = tn(h1, da1, D, FF // 2, "ffn1_dw1")
    grads["ffn1_w3"], _ = tn(h1, db1, D, FF // 2, "ffn1_dw3")

    p3 = slab_of(G_FFN1) if dist else None
    (dx0, s1), t1 = _dh_normbwd([(da1, w["ffn1_w1"]), (db1, w["ffn1_w3"])], x, vec["norm_ffn1_g"], sc1, dx1,
                                name="ffn1_bwd_dh", comm=comm(_PairExchange(p3) if dist else None))
    if dist:
        pf3, pb3 = _pair_add(place, p3, t1, name="ffn1g_pair_add")
        t2 = _run_comm(_ChipExchange(pb3), name="ffn1g_chip_exchange")
        _, red = _reduce_tail(place, G_FFN1, pf3, t2, "ffn1g")
        reduced.update(red)
    else:
        reduced = grads

    def part(s, k):
        return s[0:1, k * D:(k + 1) * D]

    dmod = jnp.concatenate([part(s1, 1), part(s1, 0), dg1[0:1], part(s2, 1), part(s2, 0), part(s_out, 0),
                            part(s3, 1), part(s3, 0), dg3[0:1]], axis=1)
    small = {"norm_ffn1_g": part(s1, 2), "norm_mix_g": part(s2, 2), "out_norm_g": part(s_out, 1),
             "norm_ffn2_g": part(s3, 2), "final_norm_g": dgfin[0:1],
             "q_norm_g": s_mid[0:1, 3 * CONV_W:3 * CONV_W + Q_LORA],
             "kv_norm_g": s_mid[0:1, 3 * CONV_W + Q_LORA:MID_SUMS], "conv_w": s_mid[0:1, 0:3 * CONV_W]}
    return loss_blk, dx0, reduced, dmod, small


SMALL = [("norm_ffn1_g", D), ("norm_mix_g", D), ("out_norm_g", D), ("norm_ffn2_g", D), ("final_norm_g", D),
         ("q_norm_g", Q_LORA), ("kv_norm_g", KV_LORA), ("conv_w", 3 * CONV_W)]
WEIGHTS = ['ada_w', 'ada_b', 'norm_ffn1_g', 'ffn1_w1', 'ffn1_w3', 'ffn1_w2', 'norm_mix_g', 'w_in', 'conv_w',
           'q_norm_g', 'w_uq', 'kv_norm_g', 'w_ukv', 'out_norm_g', 'w_out', 'norm_ffn2_g', 'ffn2_w1', 'ffn2_w3',
           'ffn2_w2', 'final_norm_g']


def kernel(x, c, positions, ada_w, ada_b, norm_ffn1_g, ffn1_w1, ffn1_w3, ffn1_w2, norm_mix_g, w_in, conv_w, q_norm_g, w_uq, kv_norm_g, w_ukv, out_norm_g, w_out, norm_ffn2_g, ffn2_w1, ffn2_w3, ffn2_w2, final_norm_g, loss_target, m_ada_w, m_ada_b, m_norm_ffn1_g, m_ffn1_w1, m_ffn1_w3, m_ffn1_w2, m_norm_mix_g, m_w_in, m_conv_w, m_q_norm_g, m_w_uq, m_kv_norm_g, m_w_ukv, m_out_norm_g, m_w_out, m_norm_ffn2_g, m_ffn2_w1, m_ffn2_w3, m_ffn2_w2, m_final_norm_g, v_ada_w, v_ada_b, v_norm_ffn1_g, v_ffn1_w1, v_ffn1_w3, v_ffn1_w2, v_norm_mix_g, v_w_in, v_conv_w, v_q_norm_g, v_w_uq, v_kv_norm_g, v_w_ukv, v_out_norm_g, v_w_out, v_norm_ffn2_g, v_ffn2_w1, v_ffn2_w3, v_ffn2_w2, v_final_norm_g):
    args = dict(locals())
    wts = {n: args[n] for n in WEIGHTS}
    mom = {n: args["m_" + n] for n in WEIGHTS}
    var = {n: args["v_" + n] for n in WEIGHTS}
    ax, ay, ac = _place()
    myq = 2 * ax + ay
    me = 2 * myq + ac
    place = jnp.stack([ac, myq]).astype(jnp.int32)

    shards = {name: wts[name][0].astype(BF16) for name, *_ in BULK}
    first = _run_comm(_Gather([shards[b[0]] for b in W_FIRST]), name="gather_ffn1")
    w_first = _gathered_weights(W_FIRST, shards, first, myq)

    mine = jnp.concatenate([c, conv_w[0].reshape(1, 3 * CONV_W // N_CHIPS)], axis=1)
    seen = _small_allgather(jnp.pad(mine, ((0, 7), (0, 0))), name="gather_cond").reshape(N_DEV, 8, -1)[:, 0]
    c_all = jnp.pad(seen[:, :D], ((0, 8), (0, 0)))
    conv_full = jnp.transpose(seen[0::2, D:].reshape(N_CHIPS, 3, CONV_W // N_CHIPS), (1, 0, 2)).reshape(3, CONV_W)
    ada_b_q = lax.dynamic_slice_in_dim(ada_b, myq * ADA_Q, ADA_Q, axis=1)
    mod_q = _ada_forward(c_all, ada_w[0], ada_b_q)
    mod_all = _small_allgather(mod_q, name="gather_mod").reshape(N_DEV, 16, ADA_Q)
    mod_rows = jnp.transpose(mod_all[0::2, :N_DEV], (1, 0, 2)).reshape(N_DEV, N_MOD * D)
    mod = lax.dynamic_slice_in_dim(mod_rows, me, 1, axis=0).reshape(N_MOD, D)

    vec = {n: wts[n] for n in ("norm_ffn1_g", "norm_mix_g", "q_norm_g", "kv_norm_g", "out_norm_g", "norm_ffn2_g")}
    vec["final_norm_g"] = final_norm_g.reshape(1, D)
    loss_blk, grad_x, gq, dmod, small = _local_step(x[0], positions[0], loss_target[0], mod, vec, conv_full, w_first,
                                                    {b[0]: shards[b[0]] for b in W_REST}, place)
    loss = lax.psum(loss_blk[0, 0], ("x", "y", "c"))

    rows = jnp.concatenate([dmod] + [small[n] for n, _ in SMALL], axis=1)
    every = _small_allgather(jnp.pad(rows, ((0, 7), (0, 0))), name="gather_small").reshape(N_DEV, 8, -1)[:, 0]
    total = _sum_devices(every)[0:1]
    dmod_q = lax.dynamic_slice_in_dim(every[:, :N_MOD * D], myq * ADA_Q, ADA_Q, axis=1)
    g = {name: gq[name] for name, *_ in BULK}
    g["ada_w"] = _ada_wgrad(c_all, jnp.pad(dmod_q, ((0, 8), (0, 0))))
    g["ada_b"] = total[:, :N_MOD * D]
    off = N_MOD * D
    for n, width in SMALL:
        g[n] = total[:, off:off + width]
        off += width
    g["conv_w"] = lax.dynamic_slice_in_dim(g["conv_w"].reshape(3, CONV_W), myq * (CONV_W // N_CHIPS),
                                           CONV_W // N_CHIPS, axis=1)
    g["final_norm_g"] = g["final_norm_g"].reshape(D)

    delta, new_m, new_v = {}, {}, {}
    for name in ["ada_w"] + [b[0] for b in BULK]:
        shape = wts[name].shape
        delta[name], new_m[name], new_v[name] = [
            r.reshape(shape) for r in _adamw(wts[name][0], g[name], mom[name][0], var[name][0], name="adamw_" + name)]
        g[name] = g[name].reshape(shape)
    smalls = ["ada_b"] + [n for n, _ in SMALL]

    def packed(d):
        flat = jnp.concatenate([d[n].reshape(1, -1) for n in smalls], axis=1)
        return jnp.pad(flat.reshape(-1, D), ((0, 1), (0, 0)))

    res = _adamw(packed(wts), packed(g), packed(mom), packed(var), name="adamw_small")
    off = 0
    for n in smalls:
        size = wts[n].size
        for d, r in zip((delta, new_m, new_v), res):
            d[n] = r.reshape(-1)[off:off + size].reshape(wts[n].shape)
        g[n] = g[n].reshape(wts[n].shape)
        off += size

    return (loss, grad_x[None], *[g[n] for n in WEIGHTS], *[delta[n] for n in WEIGHTS],
            *[new_m[n] for n in WEIGHTS], *[new_v[n] for n in WEIGHTS])
```

```python
import functools

import numpy as np
import jax
import jax.numpy as jnp
from jax import lax
from jax.experimental import pallas as pl
from jax.experimental.pallas import tpu as pltpu

F32 = jnp.float32
BF16 = jnp.bfloat16
MESH = pl.DeviceIdType.MESH

D = 1024
FF = 2816
CONV_W = 512
CONV_GROUP = 64
HEADS = 4
QK_NOPE = 128
QK_ROPE = 64
V_HEAD = 128
Q_LORA = 384
KV_LORA = 256
HEAD_SLOT = 256
IN_COLS = 3 * CONV_W + Q_LORA + KV_LORA + QK_ROPE
Z_COLS = 2304
EPS = 1e-6
ROPE_THETA = 10000.0
CHUNK = 64
ATT_SCALE = (QK_NOPE + QK_ROPE) ** -0.5
NEG = -1e30
EXP2_SCALE = ATT_SCALE * 1.4426950408889634
N_MOD = 9

LR, B1, B2, AEPS, WD, STEP = 0.001, 0.9, 0.999, 1e-08, 0.01, 10

N_CHIPS = 4
N_DEV = 8
VMEM_LIMIT = 56 << 20


def _params(sem, vmem=VMEM_LIMIT):
    return pltpu.CompilerParams(dimension_semantics=sem, vmem_limit_bytes=vmem)


def _rms(v):
    return lax.rsqrt(jnp.mean(v * v, axis=-1, keepdims=True) + EPS)


def _rsum8(v):
    t, n = v.shape
    return jnp.sum(v.reshape(t // 8, 8, n), axis=0)


def _all_rows(ref):
    ref[...] = jnp.broadcast_to(jnp.sum(ref[...], axis=0, keepdims=True), ref.shape)


def _gsum(v, gmat):
    hi = v.astype(BF16)
    lo = (v - hi.astype(F32)).astype(BF16)
    return (jnp.dot(hi, gmat, preferred_element_type=F32)
            + jnp.dot(lo, gmat, preferred_element_type=F32))


def _dot_nt(a, b):
    return lax.dot_general(a, b, (((1,), (1,)), ((), ())), preferred_element_type=F32)


def _silu_parts(a):
    sg = jax.nn.sigmoid(a)
    return sg, a * sg


def _rope(xr, cs, sn, lane):
    rh = jnp.where(lane < 32, -pltpu.roll(xr, 96, 1), pltpu.roll(xr, 32, 1))
    return xr * cs + rh * sn


def _rope_t(g, cs, sn, lane):
    y = g * sn
    rt = jnp.where(lane < 32, pltpu.roll(y, 96, 1), jnp.where(lane < 64, -pltpu.roll(y, 32, 1), 0.0))
    return g * cs + rt


def _row_tile(rows, pref, mult=8):
    t = min(rows, pref) // mult * mult
    while rows % t:
        t -= mult
    return t


def _place():
    return lax.axis_index("x"), lax.axis_index("y"), lax.axis_index("c")


ANY = pl.BlockSpec(memory_space=pl.ANY)


def _hosted_call(body, *, name, grid, in_specs, out_specs, out_shape, scratch_shapes, semantics, args, comm=None):
    n_in, n_out, n_scr = len(in_specs), len(out_specs), len(scratch_shapes)
    if comm is None:
        res = pl.pallas_call(body, name=name, grid=grid, in_specs=in_specs, out_specs=out_specs, out_shape=out_shape,
                             scratch_shapes=scratch_shapes, compiler_params=_params(semantics))(*args)
        return list(res), []
    n_ci, n_co = len(comm.inputs), len(comm.out_shapes)
    total = int(np.prod(grid))

    def hosted(*refs):
        ins, refs = refs[:n_in], refs[n_in:]
        cins, refs = refs[:n_ci], refs[n_ci:]
        outs, refs = refs[:n_out], refs[n_out:]
        couts, refs = refs[:n_co], refs[n_co:]
        scratch, sems = refs[:n_scr], refs[n_scr]
        step = pl.program_id(0)
        for ax in range(1, len(grid)):
            step = step * grid[ax] + pl.program_id(ax)

        @pl.when(step == 0)
        def _():
            comm.start(cins, couts, sems)

        body(*ins, *outs, *scratch)

        @pl.when(step == total - 1)
        def _():
            comm.finish(cins, couts, sems)

    res = pl.pallas_call(
        hosted, name=name, grid=grid, in_specs=list(in_specs) + [ANY] * n_ci,
        out_specs=list(out_specs) + [ANY] * n_co, out_shape=list(out_shape) + list(comm.out_shapes),
        scratch_shapes=list(scratch_shapes) + [pltpu.SemaphoreType.DMA((comm.n_sems,))],
        compiler_params=_params(("arbitrary",) * len(grid)))(*args, *comm.inputs)
    return list(res[:n_out]), list(res[n_out:])


def _run_comm(comm, *, name):
    n_ci = len(comm.inputs)

    def body(*refs):
        cins, couts, sems = refs[:n_ci], refs[n_ci:-1], refs[-1]
        comm.start(cins, couts, sems)
        comm.finish(cins, couts, sems)

    return pl.pallas_call(
        body, name=name, out_shape=list(comm.out_shapes), in_specs=[ANY] * n_ci,
        out_specs=[ANY] * len(comm.out_shapes), scratch_shapes=[pltpu.SemaphoreType.DMA((comm.n_sems,))],
    )(*comm.inputs)


class _Gather:
    def __init__(self, slabs):
        self.inputs = list(slabs)
        self.out_shapes = [jax.ShapeDtypeStruct((N_CHIPS,) + s.shape, s.dtype) for s in slabs]
        self.n_sems = 12 * len(slabs)

    @staticmethod
    def _copy(out, sems, base, k, chip, hc, to, src=None):
        H = out.shape[1] // 2
        half = out.at[2 * chip[0] + chip[1], pl.ds(hc * H, H), :]
        return pltpu.make_async_remote_copy(
            src_ref=half if src is None else src, dst_ref=half, send_sem=sems.at[base + k],
            recv_sem=sems.at[base + 6 + k], device_id=to, device_id_type=MESH)

    def _firsts(self, src, out, sems, base):
        x, y, c = _place()
        H = src.shape[0] // 2
        chips = [(1 - x, y), (x, 1 - y), (1 - x, 1 - y)]
        return [self._copy(out, sems, base, j, (x, y), c, (*chip, c), src=src.at[pl.ds(c * H, H), :])
                for j, chip in enumerate(chips)]

    def start(self, ins, outs, sems):
        for i, (src, out) in enumerate(zip(ins, outs)):
            for cp in self._firsts(src, out, sems, 12 * i):
                cp.start()

    def finish(self, ins, outs, sems):
        x, y, c = _place()
        chips = [(1 - x, y), (x, 1 - y), (1 - x, 1 - y)]
        passed = []
        for i, out in enumerate(outs):
            for j, chip in enumerate(chips):
                self._copy(out, sems, 12 * i, j, chip, c, (x, y, c)).wait_recv()
                cp = self._copy(out, sems, 12 * i, 3 + j, chip, c, (x, y, 1 - c))
                cp.start()
                passed.append(cp)
        for i, out in enumerate(outs):
            for j, chip in enumerate(chips):
                self._copy(out, sems, 12 * i, 3 + j, chip, 1 - c, (x, y, c)).wait_recv()
        for cp in passed:
            cp.wait_send()
        for i, (src, out) in enumerate(zip(ins, outs)):
            for cp in self._firsts(src, out, sems, 12 * i):
                cp.wait_send()


class _PairExchange:
    def __init__(self, arrays):
        self.inputs = list(arrays)
        self.out_shapes = [jax.ShapeDtypeStruct((N_CHIPS, a.shape[1] // 2, a.shape[2]), a.dtype) for a in arrays]
        self.n_sems = 2 * len(arrays)

    def _copies(self, ins, outs, sems):
        x, y, c = _place()
        return [pltpu.make_async_remote_copy(
            src_ref=g.at[:, pl.ds((1 - c) * t.shape[1], t.shape[1]), :], dst_ref=t, send_sem=sems.at[2 * i],
            recv_sem=sems.at[2 * i + 1], device_id=(x, y, 1 - c), device_id_type=MESH)
            for i, (g, t) in enumerate(zip(ins, outs))]

    def start(self, ins, outs, sems):
        for cp in self._copies(ins, outs, sems):
            cp.start()

    def finish(self, ins, outs, sems):
        for cp in self._copies(ins, outs, sems):
            cp.wait()


class _ChipExchange:
    def __init__(self, arrays):
        self.inputs = list(arrays)
        self.out_shapes = [jax.ShapeDtypeStruct(a.shape, a.dtype) for a in arrays]
        self.n_sems = 6 * len(arrays)

    def _copies(self, p, t, sems, base):
        x, y, c = _place()
        myq = 2 * x + y
        chips = [(1 - x, y), (x, 1 - y), (1 - x, 1 - y)]
        sends = [pltpu.make_async_remote_copy(
            src_ref=p.at[2 * chip[0] + chip[1]], dst_ref=t.at[myq], send_sem=sems.at[base + j],
            recv_sem=sems.at[base + 3 + j], device_id=(*chip, c), device_id_type=MESH) for j, chip in enumerate(chips)]
        lands = [pltpu.make_async_remote_copy(
            src_ref=t.at[2 * chip[0] + chip[1]], dst_ref=t.at[2 * chip[0] + chip[1]], send_sem=sems.at[base + j],
            recv_sem=sems.at[base + 3 + j], device_id=(*chip, c), device_id_type=MESH) for j, chip in enumerate(chips)]
        return sends, lands

    def start(self, ins, outs, sems):
        for i, (p, t) in enumerate(zip(ins, outs)):
            for cp in self._copies(p, t, sems, 6 * i)[0]:
                cp.start()

    def finish(self, ins, outs, sems):
        for i, (p, t) in enumerate(zip(ins, outs)):
            sends, lands = self._copies(p, t, sems, 6 * i)
            for cp in lands:
                cp.wait_recv()
            for cp in sends:
                cp.wait_send()


class _PairShare:
    def __init__(self, arrays):
        self.inputs = list(arrays)
        self.out_shapes = [jax.ShapeDtypeStruct(a.shape, a.dtype) for a in arrays]
        self.n_sems = 2 * len(arrays)

    def _copies(self, ins, outs, sems):
        x, y, c = _place()
        return [pltpu.make_async_remote_copy(
            src_ref=r, dst_ref=o, send_sem=sems.at[2 * i], recv_sem=sems.at[2 * i + 1],
            device_id=(x, y, 1 - c), device_id_type=MESH) for i, (r, o) in enumerate(zip(ins, outs))]

    def start(self, ins, outs, sems):
        for cp in self._copies(ins, outs, sems):
            cp.start()

    def finish(self, ins, outs, sems):
        for cp in self._copies(ins, outs, sems):
            cp.wait()


def _ffn_up(x, ng, sh, sc, w1, w3, *, name, comm=None):
    S = x.shape[0]
    tm, tn = _row_tile(S, 512), FF // 2

    def body(x_ref, g_ref, sh_ref, sc_ref, w1_ref, w3_ref, h_ref, a_ref, b_ref, u_ref, hs):
        @pl.when(pl.program_id(1) == 0)
        def _():
            xv = x_ref[...]
            h = ((xv * _rms(xv)) * g_ref[...]) * (1.0 + sc_ref[...]) + sh_ref[...]
            hb = h.astype(BF16)
            hs[...] = hb
            h_ref[...] = hb

        h = hs[...]
        a = jnp.dot(h, w1_ref[...], preferred_element_type=F32)
        b = jnp.dot(h, w3_ref[...], preferred_element_type=F32)
        _, sa = _silu_parts(a)
        a_ref[...] = a.astype(BF16)
        b_ref[...] = b.astype(BF16)
        u_ref[...] = (sa * b).astype(BF16)

    row = pl.BlockSpec((tm, D), lambda i, j: (i, 0))
    vec = pl.BlockSpec((1, D), lambda i, j: (0, 0))
    wsp = pl.BlockSpec((D, tn), lambda i, j: (0, j))
    osp = pl.BlockSpec((tm, tn), lambda i, j: (i, j))
    return _hosted_call(
        body, name=name, grid=(S // tm, FF // tn),
        in_specs=[row, vec, vec, vec, wsp, wsp],
        out_specs=[row, osp, osp, osp],
        out_shape=[jax.ShapeDtypeStruct((S, D), BF16)] + [jax.ShapeDtypeStruct((S, FF), BF16)] * 3,
        scratch_shapes=[pltpu.VMEM((tm, D), BF16)],
        semantics=("parallel", "arbitrary"), args=(x, ng, sh, sc, w1, w3), comm=comm)


def _ffn_down(u, w2, x, gate, *, name):
    S = x.shape[0]
    tm = _row_tile(S, 512)

    def body(u_ref, w2_ref, x_ref, g_ref, xo_ref, f_ref):
        f = jnp.dot(u_ref[...], w2_ref[...], preferred_element_type=F32)
        xo_ref[...] = x_ref[...] + (0.5 * g_ref[...]) * f
        f_ref[...] = f.astype(BF16)

    return pl.pallas_call(
        body, name=name, grid=(S // tm,),
        in_specs=[pl.BlockSpec((tm, FF), lambda i: (i, 0)), pl.BlockSpec((FF, D), lambda i: (0, 0)),
                  pl.BlockSpec((tm, D), lambda i: (i, 0)), pl.BlockSpec((1, D), lambda i: (0, 0))],
        out_specs=[pl.BlockSpec((tm, D), lambda i: (i, 0))] * 2,
        out_shape=[jax.ShapeDtypeStruct((S, D), F32), jax.ShapeDtypeStruct((S, D), BF16)],
        compiler_params=_params(("parallel",)),
    )(u, w2, x, gate)


def _ffn_bwd_du(dx, gate, f, w2, a, b, *, name, comm=None):
    S = dx.shape[0]
    tm, tn = _row_tile(S, 512), FF // 2
    n_i = S // tm

    def body(dx_ref, g_ref, f_ref, w_ref, a_ref, b_ref, df_ref, da_ref, db_ref, dg_ref, dfs):
        i, j = pl.program_id(0), pl.program_id(1)

        @pl.when((i == 0) & (j == 0))
        def _():
            dg_ref[...] = jnp.zeros_like(dg_ref)

        @pl.when(j == 0)
        def _():
            dxv = dx_ref[...]
            dfb = (dxv * (0.5 * g_ref[...])).astype(BF16)
            dfs[...] = dfb
            df_ref[...] = dfb
            dg_ref[...] += _rsum8(dxv * (0.5 * f_ref[...].astype(F32)))

        du = _dot_nt(dfs[...], w_ref[pl.ds(pl.multiple_of(j * tn, tn), tn), :])
        av = a_ref[...].astype(F32)
        sg, sa = _silu_parts(av)
        da_ref[...] = (du * b_ref[...].astype(F32) * (sg * (1.0 + av * (1.0 - sg)))).astype(BF16)
        db_ref[...] = (du * sa).astype(BF16)

        @pl.when((i == n_i - 1) & (j == FF // tn - 1))
        def _():
            _all_rows(dg_ref)

    row = pl.BlockSpec((tm, D), lambda i, j: (i, 0))
    blk = pl.BlockSpec((tm, tn), lambda i, j: (i, j))
    return _hosted_call(
        body, name=name, grid=(n_i, FF // tn),
        in_specs=[row, pl.BlockSpec((1, D), lambda i, j: (0, 0)), row,
                  pl.BlockSpec((FF, D), lambda i, j: (0, 0)), blk, blk],
        out_specs=[row, blk, blk, pl.BlockSpec((8, D), lambda i, j: (0, 0))],
        out_shape=[jax.ShapeDtypeStruct((S, D), BF16), jax.ShapeDtypeStruct((S, FF), BF16),
                   jax.ShapeDtypeStruct((S, FF), BF16), jax.ShapeDtypeStruct((8, D), F32)],
        scratch_shapes=[pltpu.VMEM((tm, D), BF16)],
        semantics=("arbitrary", "arbitrary"), args=(dx, gate, f, w2, a, b), comm=comm)


def _tn_matmul(a, b, *, tm, tn, name, comm=None):
    S, M = a.shape
    N = b.shape[1]
    ts = _row_tile(S, 1024)
    ns = S // ts

    def body(a_ref, b_ref, o_ref, acc):
        s = pl.program_id(2)
        p = lax.dot_general(a_ref[...], b_ref[...], (((0,), (0,)), ((), ())), preferred_element_type=F32)

        @pl.when(s == 0)
        def _():
            acc[...] = p

        @pl.when(s > 0)
        def _():
            acc[...] += p

        @pl.when(s == ns - 1)
        def _():
            o_ref[...] = acc[...]

    (out,), couts = _hosted_call(
        body, name=name, grid=(M // tm, N // tn, ns),
        in_specs=[pl.BlockSpec((ts, tm), lambda i, j, s: (s, i)), pl.BlockSpec((ts, tn), lambda i, j, s: (s, j))],
        out_specs=[pl.BlockSpec((tm, tn), lambda i, j, s: (i, j))],
        out_shape=[jax.ShapeDtypeStruct((M, N), F32)],
        scratch_shapes=[pltpu.VMEM((tm, tn), F32)],
        semantics=("parallel", "parallel", "arbitrary"), args=(a, b), comm=comm)
    return out if comm is None else (out, couts)


def _dh_normbwd(pairs, x, ng, sc, dx_next, *, name, comm=None):
    S = x.shape[0]
    tm = _row_tile(S, 256)
    n_i = S // tm
    n_p = len(pairs)

    def body(*refs):
        a_refs, w_refs = refs[:n_p], refs[n_p:2 * n_p]
        x_ref, g_ref, sc_ref, dxn_ref, dx_ref, p_ref = refs[2 * n_p:]
        i = pl.program_id(0)
        dh = _dot_nt(a_refs[0][...], w_refs[0][...])
        for k in range(1, n_p):
            dh = dh + _dot_nt(a_refs[k][...], w_refs[k][...])
        xv = x_ref[...]
        r = _rms(xv)
        xh = xv * r
        g = g_ref[...]
        dn = dh * (1.0 + sc_ref[...])
        dy = dn * g
        dx_ref[...] = dxn_ref[...] + r * (dy - xh * jnp.mean(dy * xh, axis=-1, keepdims=True))

        @pl.when(i == 0)
        def _():
            p_ref[...] = jnp.zeros_like(p_ref)

        p_ref[:, 0:D] += _rsum8(dh * (xh * g))
        p_ref[:, D:2 * D] += _rsum8(dh)
        p_ref[:, 2 * D:3 * D] += _rsum8(dn * xh)

        @pl.when(i == n_i - 1)
        def _():
            _all_rows(p_ref)

    row = pl.BlockSpec((tm, D), lambda i: (i, 0))
    vec = pl.BlockSpec((1, D), lambda i: (0, 0))
    in_specs = ([pl.BlockSpec((tm, a.shape[1]), lambda i: (i, 0)) for a, _ in pairs]
                + [pl.BlockSpec(w.shape, lambda i: (0, 0)) for _, w in pairs] + [row, vec, vec, row])
    return _hosted_call(
        body, name=name, grid=(n_i,), in_specs=in_specs,
        out_specs=[row, pl.BlockSpec((8, 3 * D), lambda i: (0, 0))],
        out_shape=[jax.ShapeDtypeStruct((S, D), F32), jax.ShapeDtypeStruct((8, 3 * D), F32)],
        scratch_shapes=[], semantics=("arbitrary",),
        args=(*[a for a, _ in pairs], *[w for _, w in pairs], x, ng, sc, dx_next), comm=comm)


def _final_loss(x3, gfin, tgt):
    S = x3.shape[0]
    tm = _row_tile(S, 512)
    n_i = S // tm

    def body(x_ref, g_ref, t_ref, dx_ref, dg_ref, loss_ref, lacc):
        i = pl.program_id(0)
        xv = x_ref[...]
        r = _rms(xv)
        xh = xv * r
        g = g_ref[...]
        e = xh * g - t_ref[...]
        dout = e * (1.0 / D)
        dy = dout * g
        dx_ref[...] = r * (dy - xh * jnp.mean(dy * xh, axis=-1, keepdims=True))

        @pl.when(i == 0)
        def _():
            dg_ref[...] = jnp.zeros_like(dg_ref)
            lacc[...] = jnp.zeros_like(lacc)

        dg_ref[...] += _rsum8(dout * xh)
        lacc[...] += _rsum8(e * e)

        @pl.when(i == n_i - 1)
        def _():
            _all_rows(dg_ref)
            tot = jnp.sum(jnp.sum(lacc[...], axis=0, keepdims=True), axis=1, keepdims=True)
            loss_ref[...] = jnp.broadcast_to(tot * (0.5 / D), loss_ref.shape)

    row = pl.BlockSpec((tm, D), lambda i: (i, 0))
    return pl.pallas_call(
        body, name="final_loss", grid=(n_i,),
        in_specs=[row, pl.BlockSpec((1, D), lambda i: (0, 0)), row],
        out_specs=[row, pl.BlockSpec((8, D), lambda i: (0, 0)), pl.BlockSpec((8, 128), lambda i: (0, 0))],
        out_shape=[jax.ShapeDtypeStruct((S, D), F32), jax.ShapeDtypeStruct((8, D), F32),
                   jax.ShapeDtypeStruct((8, 128), F32)],
        scratch_shapes=[pltpu.VMEM((8, D), F32)],
        compiler_params=_params(("arbitrary",)),
    )(x3, gfin, tgt)


def _mix_in(x, ng, sh, sc, w_in):
    S = x.shape[0]
    tm = _row_tile(S, 512)

    def body(x_ref, g_ref, sh_ref, sc_ref, w_ref, h_ref, z_ref):
        xv = x_ref[...]
        hb = (((xv * _rms(xv)) * g_ref[...]) * (1.0 + sc_ref[...]) + sh_ref[...]).astype(BF16)
        h_ref[...] = hb
        z_ref[...] = jnp.dot(hb, w_ref[...], preferred_element_type=F32)

    row = pl.BlockSpec((tm, D), lambda i: (i, 0))
    vec = pl.BlockSpec((1, D), lambda i: (0, 0))
    return pl.pallas_call(
        body, name="mix_in", grid=(S // tm,),
        in_specs=[row, vec, vec, vec, pl.BlockSpec((D, Z_COLS), lambda i: (0, 0))],
        out_specs=[row, pl.BlockSpec((tm, Z_COLS), lambda i: (i, 0))],
        out_shape=[jax.ShapeDtypeStruct((S, D), BF16), jax.ShapeDtypeStruct((S, Z_COLS), F32)],
        compiler_params=_params(("parallel",)),
    )(x, ng, sh, sc, w_in)


def _conv_taps(u, halo, rows):
    u1 = jnp.where(rows == 0, halo[7:8, :], pltpu.roll(u, 1, 0))
    u2 = jnp.where(rows == 0, halo[6:7, :], jnp.where(rows == 1, halo[7:8, :], pltpu.roll(u, 2, 0)))
    return u1, u2


def _mix_mid(z, conv_w, gq, gkv, wuq, wukv, cs, sn):
    S = z.shape[0]
    tm = _row_tile(S, 512)
    hb = tm // 8

    def body(z_ref, zh_ref, cw_ref, gq_ref, gkv_ref, wuq_ref, wukv_ref, cs_ref, sn_ref,
             ya_ref, q_ref, k_ref, v_ref, cqn_ref, ckvn_ref):
        i = pl.program_id(0)
        xb = z_ref[:, 0:CONV_W]
        u = z_ref[:, CONV_W:2 * CONV_W] * z_ref[:, 2 * CONV_W:3 * CONV_W]
        halo = zh_ref[:, CONV_W:2 * CONV_W] * zh_ref[:, 2 * CONV_W:3 * CONV_W]
        halo = jnp.where(i > 0, halo, 0.0)
        rows = lax.broadcasted_iota(jnp.int32, (tm, CONV_W), 0)
        u1, u2 = _conv_taps(u, halo, rows)
        y = cw_ref[0:1, :] * u2 + cw_ref[1:2, :] * u1 + cw_ref[2:3, :] * u
        ya_ref[...] = xb * y

        lane = lax.broadcasted_iota(jnp.int32, (tm, 128), 1)
        cs_v, sn_v = cs_ref[...], sn_ref[...]
        cq = z_ref[:, 3 * CONV_W:3 * CONV_W + Q_LORA]
        cqn = ((cq * _rms(cq)) * gq_ref[...]).astype(BF16)
        cqn_ref[...] = cqn
        q = jnp.dot(cqn, wuq_ref[...], preferred_element_type=F32)
        for h in range(HEADS):
            o = h * HEAD_SLOT
            q_ref[:, o:o + 128] = q[:, o:o + 128].astype(BF16)
            q_ref[:, o + 128:o + 256] = _rope(q[:, o + 128:o + 256], cs_v, sn_v, lane).astype(BF16)

        c0 = 3 * CONV_W + Q_LORA
        ckv = z_ref[:, c0:c0 + KV_LORA]
        ckvn = ((ckv * _rms(ckv)) * gkv_ref[...]).astype(BF16)
        ckvn_ref[...] = ckvn
        kv = jnp.dot(ckvn, wukv_ref[...], preferred_element_type=F32)
        krot = _rope(z_ref[:, c0 + KV_LORA:Z_COLS], cs_v, sn_v, lane).astype(BF16)
        for h in range(HEADS):
            o = h * HEAD_SLOT
            k_ref[:, o:o + 128] = kv[:, h * 128:(h + 1) * 128].astype(BF16)
            k_ref[:, o + 128:o + 256] = krot
        v_ref[...] = kv[:, HEADS * 128:].astype(BF16)

    def rows_of(n):
        return pl.BlockSpec((tm, n), lambda i: (i, 0))

    def whole(shape):
        return pl.BlockSpec(shape, lambda i: (0, 0))

    return pl.pallas_call(
        body, name="mix_mid", grid=(S // tm,),
        in_specs=[rows_of(Z_COLS), pl.BlockSpec((8, Z_COLS), lambda i: (jnp.maximum(i * hb - 1, 0), 0)),
                  whole((8, CONV_W)), whole((1, Q_LORA)), whole((1, KV_LORA)),
                  whole((Q_LORA, HEADS * HEAD_SLOT)), whole((KV_LORA, 2 * HEADS * 128)),
                  rows_of(128), rows_of(128)],
        out_specs=[rows_of(CONV_W), rows_of(HEADS * HEAD_SLOT), rows_of(HEADS * HEAD_SLOT), rows_of(HEADS * V_HEAD),
                   rows_of(Q_LORA), rows_of(KV_LORA)],
        out_shape=[jax.ShapeDtypeStruct((S, CONV_W), F32), jax.ShapeDtypeStruct((S, HEADS * HEAD_SLOT), BF16),
                   jax.ShapeDtypeStruct((S, HEADS * HEAD_SLOT), BF16), jax.ShapeDtypeStruct((S, HEADS * V_HEAD), BF16),
                   jax.ShapeDtypeStruct((S, Q_LORA), BF16), jax.ShapeDtypeStruct((S, KV_LORA), BF16)],
        compiler_params=_params(("parallel",)),
    )(z, z, conv_w, gq, gkv, wuq, wukv, cs, sn)


def _att_blocks(S):
    bk = min(512, max(S // 4, 128))
    return 2 * bk, bk


def _pair_tables(S, k_major):
    bq, bk = _att_blocks(S)
    nq, nk = S // bq, S // bk
    vis = lambda qi, ki: ki * bk < (qi + 1) * bq
    if k_major:
        pairs = [(qi, ki) for ki in range(nk) for qi in range(nq) if vis(qi, ki)]
    else:
        pairs = [(qi, ki) for qi in range(nq) for ki in range(nk) if vis(qi, ki)]
    cols = [[p[0] for p in pairs], [p[1] for p in pairs], [int((p[1] + 1) * bk > p[0] * bq) for p in pairs]]
    return [jnp.asarray(np.array(c, np.int32)) for c in cols], len(pairs)


def _chunk_mask(qi, ki, bq, bk):
    r = (qi * bq + lax.broadcasted_iota(jnp.int32, (bq, bk), 0)) // CHUNK
    c = (ki * bk + lax.broadcasted_iota(jnp.int32, (bq, bk), 1)) // CHUNK
    return c <= r


def _attention(q, k, v):
    S = q.shape[0]
    bq, bk = _att_blocks(S)
    last_k = bq // bk - 1
    tables, n_pairs = _pair_tables(S, k_major=False)

    def body(qi_ref, ki_ref, mk_ref, q_ref, k_ref, v_ref, o_ref, lse_ref, m_s, l_s, acc_s):
        p_id = pl.program_id(1)
        qi, ki = qi_ref[p_id], ki_ref[p_id]

        @pl.when(ki == 0)
        def _():
            m_s[...] = jnp.full_like(m_s, NEG)
            l_s[...] = jnp.zeros_like(l_s)
            acc_s[...] = jnp.zeros_like(acc_s)

        def update(masked):
            s = lax.dot_general(q_ref[...], k_ref[...], (((1,), (1,)), ((), ())), preferred_element_type=F32)
            if masked:
                s = jnp.where(_chunk_mask(qi, ki, bq, bk), s, NEG)
            m_prev = m_s[...]
            m_new = jnp.maximum(m_prev, jnp.max(s, axis=1, keepdims=True))
            alpha = jnp.exp2((m_prev - m_new) * EXP2_SCALE)
            p = jnp.exp2((s - jnp.tile(m_new, (1, bk // 128))) * EXP2_SCALE)
            l_s[...] = alpha * l_s[...] + jnp.sum(p, axis=1, keepdims=True)
            acc_s[...] = alpha * acc_s[...] + jnp.dot(p.astype(BF16), v_ref[...], preferred_element_type=F32)
            m_s[...] = m_new

        @pl.when(mk_ref[p_id] == 0)
        def _():
            update(False)

        @pl.when(mk_ref[p_id] == 1)
        def _():
            update(True)

        @pl.when(ki == qi * (last_k + 1) + last_k)
        def _():
            l = l_s[...]
            o_ref[...] = acc_s[...] / l
            lse_ref[...] = m_s[...] * EXP2_SCALE + jnp.log2(l)

    grid_spec = pltpu.PrefetchScalarGridSpec(
        num_scalar_prefetch=3, grid=(HEADS, n_pairs),
        in_specs=[pl.BlockSpec((bq, HEAD_SLOT), lambda h, p, qt, kt, mt: (qt[p], h)),
                  pl.BlockSpec((bk, HEAD_SLOT), lambda h, p, qt, kt, mt: (kt[p], h)),
                  pl.BlockSpec((bk, V_HEAD), lambda h, p, qt, kt, mt: (kt[p], h))],
        out_specs=[pl.BlockSpec((bq, V_HEAD), lambda h, p, qt, kt, mt: (qt[p], h))] * 2,
        scratch_shapes=[pltpu.VMEM((bq, V_HEAD), F32)] * 3)
    return pl.pallas_call(
        body, name="attention", grid_spec=grid_spec,
        out_shape=[jax.ShapeDtypeStruct((S, HEADS * V_HEAD), F32)] * 2,
        compiler_params=_params(("arbitrary", "arbitrary")),
    )(*tables, q, k, v)


def _attention_bwd(q, k, v, do, lse2, delta):
    S = q.shape[0]
    bq, bk = _att_blocks(S)
    nq = S // bq
    tables, n_pairs = _pair_tables(S, k_major=True)

    def body(qi_ref, ki_ref, mk_ref, q_ref, k_ref, v_ref, do_ref, lse_ref, dl_ref, dq_hbm, dk_ref, dv_ref,
             dq_s, dk_s, dv_s, sem):
        head, p_id = pl.program_id(0), pl.program_id(1)
        qi, ki = qi_ref[p_id], ki_ref[p_id]
        rows = pl.ds(pl.multiple_of(qi * bq, bq), bq)

        @pl.when(qi * bq <= ki * bk)
        def _():
            dk_s[...] = jnp.zeros_like(dk_s)
            dv_s[...] = jnp.zeros_like(dv_s)

        def update(masked):
            qv, kv, dov = q_ref[...], k_ref[...], do_ref[...]
            s = lax.dot_general(qv, kv, (((1,), (1,)), ((), ())), preferred_element_type=F32)
            dp = lax.dot_general(dov, v_ref[...], (((1,), (1,)), ((), ())), preferred_element_type=F32)
            if masked:
                s = jnp.where(_chunk_mask(qi, ki, bq, bk), s, NEG)
            p = jnp.exp2(s * EXP2_SCALE - jnp.tile(lse_ref[...], (1, bk // 128)))
            dv_s[...] += lax.dot_general(p.astype(BF16), dov, (((0,), (0,)), ((), ())), preferred_element_type=F32)
            ds = (p * (dp - jnp.tile(dl_ref[...], (1, bk // 128)))).astype(BF16)
            dk_s[...] += lax.dot_general(ds, qv, (((0,), (0,)), ((), ())), preferred_element_type=F32)
            dq = jnp.dot(ds, kv, preferred_element_type=F32)

            @pl.when(ki == 0)
            def _():
                dq_s[rows, :] = dq

            @pl.when(ki > 0)
            def _():
                dq_s[rows, :] += dq

        @pl.when(mk_ref[p_id] == 0)
        def _():
            update(False)

        @pl.when(mk_ref[p_id] == 1)
        def _():
            update(True)

        @pl.when(qi == nq - 1)
        def _():
            dk_ref[...] = dk_s[...] * ATT_SCALE
            dv_ref[...] = dv_s[...]

        @pl.when(p_id == n_pairs - 1)
        def _():
            dq_s[...] = dq_s[...] * ATT_SCALE
            out = pltpu.make_async_copy(
                dq_s, dq_hbm.at[:, pl.ds(pl.multiple_of(head * HEAD_SLOT, HEAD_SLOT), HEAD_SLOT)], sem)
            out.start()
            out.wait()

    grid_spec = pltpu.PrefetchScalarGridSpec(
        num_scalar_prefetch=3, grid=(HEADS, n_pairs),
        in_specs=[pl.BlockSpec((bq, HEAD_SLOT), lambda h, p, qt, kt, mt: (qt[p], h)),
                  pl.BlockSpec((bk, HEAD_SLOT), lambda h, p, qt, kt, mt: (kt[p], h)),
                  pl.BlockSpec((bk, V_HEAD), lambda h, p, qt, kt, mt: (kt[p], h)),
                  pl.BlockSpec((bq, V_HEAD), lambda h, p, qt, kt, mt: (qt[p], h)),
                  pl.BlockSpec((bq, V_HEAD), lambda h, p, qt, kt, mt: (qt[p], h)),
                  pl.BlockSpec((bq, V_HEAD), lambda h, p, qt, kt, mt: (qt[p], h))],
        out_specs=[pl.BlockSpec(memory_space=pl.ANY),
                   pl.BlockSpec((bk, HEAD_SLOT), lambda h, p, qt, kt, mt: (kt[p], h)),
                   pl.BlockSpec((bk, V_HEAD), lambda h, p, qt, kt, mt: (kt[p], h))],
        scratch_shapes=[pltpu.VMEM((S, HEAD_SLOT), F32), pltpu.VMEM((bk, HEAD_SLOT), F32),
                        pltpu.VMEM((bk, V_HEAD), F32), pltpu.SemaphoreType.DMA])
    return pl.pallas_call(
        body, name="attention_bwd", grid_spec=grid_spec,
        out_shape=[jax.ShapeDtypeStruct((S, HEADS * HEAD_SLOT), F32), jax.ShapeDtypeStruct((S, HEADS * HEAD_SLOT), F32),
                   jax.ShapeDtypeStruct((S, HEADS * V_HEAD), F32)],
        compiler_params=_params(("arbitrary", "arbitrary")),
    )(*tables, q, k, v, do, lse2, delta)


def _group_mats():
    def blockdiag(n, g):
        idx = np.arange(n) // g
        return jnp.asarray((idx[:, None] == idx[None, :]).astype(np.float32), dtype=BF16)
    return blockdiag(CONV_W, CONV_GROUP), blockdiag(HEADS * V_HEAD, V_HEAD)


def _mix_out(ya, o, gout, w_out, x, gate, ga, gb):
    S = x.shape[0]
    tm = _row_tile(S, 512)

    def body(ya_ref, o_ref, go_ref, w_ref, x_ref, g_ref, ga_ref, gb_ref, xo_ref, yn_ref, yo_ref):
        yav, ov = ya_ref[...], o_ref[...]
        ra = lax.rsqrt(_gsum(yav * yav, ga_ref[...]) * (1.0 / CONV_GROUP) + EPS)
        rb = lax.rsqrt(_gsum(ov * ov, gb_ref[...]) * (1.0 / V_HEAD) + EPS)
        na = ((yav * ra) * go_ref[:, 0:CONV_W]).astype(BF16)
        nb = ((ov * rb) * go_ref[:, CONV_W:]).astype(BF16)
        yn_ref[:, 0:CONV_W] = na
        yn_ref[:, CONV_W:] = nb
        yo = (jnp.dot(na, w_ref[0:CONV_W, :], preferred_element_type=F32)
              + jnp.dot(nb, w_ref[CONV_W:, :], preferred_element_type=F32))
        xo_ref[...] = x_ref[...] + g_ref[...] * yo
        yo_ref[...] = yo.astype(BF16)

    row = pl.BlockSpec((tm, D), lambda i: (i, 0))
    half = pl.BlockSpec((tm, CONV_W), lambda i: (i, 0))
    vec = pl.BlockSpec((1, D), lambda i: (0, 0))
    sq = pl.BlockSpec((CONV_W, CONV_W), lambda i: (0, 0))
    return pl.pallas_call(
        body, name="mix_out", grid=(S // tm,),
        in_specs=[half, half, vec, pl.BlockSpec((D, D), lambda i: (0, 0)), row, vec, sq, sq],
        out_specs=[row, row, row],
        out_shape=[jax.ShapeDtypeStruct((S, D), F32), jax.ShapeDtypeStruct((S, D), BF16),
                   jax.ShapeDtypeStruct((S, D), BF16)],
        compiler_params=_params(("parallel",)),
    )(ya, o, gout, w_out, x, gate, ga, gb)


def _mix_out_bwd(dx, gate, yo, w_out, ya, o, gout, ga, gb, comm=None):
    S = dx.shape[0]
    tm = _row_tile(S, 256)
    n_i = S // tm

    def norm_bwd(v, dn, gain, gmat, inv_n):
        r = lax.rsqrt(_gsum(v * v, gmat) * inv_n + EPS)
        vh = v * r
        dy = dn * gain
        return r * (dy - vh * (_gsum(dy * vh, gmat) * inv_n)), dn * vh

    def body(dx_ref, g_ref, yo_ref, w_ref, ya_ref, o_ref, go_ref, ga_ref, gb_ref,
             dyo_ref, dya_ref, do_ref, dl_ref, p_ref):
        i = pl.program_id(0)
        dxv = dx_ref[...]
        dyo = (dxv * g_ref[...]).astype(BF16)
        dyo_ref[...] = dyo
        dyn = _dot_nt(dyo, w_ref[...])
        dya, dga = norm_bwd(ya_ref[...], dyn[:, 0:CONV_W], go_ref[:, 0:CONV_W], ga_ref[...], 1.0 / CONV_GROUP)
        ov = o_ref[...]
        do, dgb = norm_bwd(ov, dyn[:, CONV_W:], go_ref[:, CONV_W:], gb_ref[...], 1.0 / V_HEAD)
        dya_ref[...] = dya
        do_ref[...] = do.astype(BF16)
        dl_ref[...] = _gsum(do * ov, gb_ref[...])

        @pl.when(i == 0)
        def _():
            p_ref[...] = jnp.zeros_like(p_ref)

        p_ref[:, 0:D] += _rsum8(dxv * yo_ref[...].astype(F32))
        p_ref[:, D:D + CONV_W] += _rsum8(dga)
        p_ref[:, D + CONV_W:2 * D] += _rsum8(dgb)

        @pl.when(i == n_i - 1)
        def _():
            _all_rows(p_ref)

    row = pl.BlockSpec((tm, D), lambda i: (i, 0))
    half = pl.BlockSpec((tm, CONV_W), lambda i: (i, 0))
    vec = pl.BlockSpec((1, D), lambda i: (0, 0))
    sq = pl.BlockSpec((CONV_W, CONV_W), lambda i: (0, 0))
    return _hosted_call(
        body, name="mix_out_bwd", grid=(n_i,),
        in_specs=[row, vec, row, pl.BlockSpec((D, D), lambda i: (0, 0)), half, half, vec, sq, sq],
        out_specs=[row, half, half, half, pl.BlockSpec((8, 2 * D), lambda i: (0, 0))],
        out_shape=[jax.ShapeDtypeStruct((S, D), BF16), jax.ShapeDtypeStruct((S, CONV_W), F32),
                   jax.ShapeDtypeStruct((S, CONV_W), BF16), jax.ShapeDtypeStruct((S, CONV_W), F32),
                   jax.ShapeDtypeStruct((8, 2 * D), F32)],
        scratch_shapes=[], semantics=("arbitrary",), args=(dx, gate, yo, w_out, ya, o, gout, ga, gb), comm=comm)


MID_SUMS = 3 * CONV_W + Q_LORA + KV_LORA


def _mix_mid_bwd(z, dya, conv_w, gq, gkv, wuq, wukv, cs, sn, dq, dk, dv, comm=None):
    S = z.shape[0]
    tm = _row_tile(S, 256)
    n_i = S // tm
    hb = tm // 8
    last_blk = S // 8 - 1

    def latent_bwd(cv, dcn, gain):
        r = _rms(cv)
        ch = cv * r
        dy = dcn * gain
        return r * (dy - ch * jnp.mean(dy * ch, axis=-1, keepdims=True)), dcn * ch

    def body(z_ref, zp_ref, zn_ref, dya_ref, dyan_ref, cw_ref, gq_ref, gkv_ref, wuq_ref, wukv_ref, cs_ref, sn_ref,
             dq_ref, dk_ref, dv_ref, dz_ref, dqf_ref, dkvf_ref, p_ref):
        i = pl.program_id(0)
        xb, xc, xu = z_ref[:, 0:CONV_W], z_ref[:, CONV_W:2 * CONV_W], z_ref[:, 2 * CONV_W:3 * CONV_W]
        u = xc * xu
        halo = jnp.where(i > 0, zp_ref[:, CONV_W:2 * CONV_W] * zp_ref[:, 2 * CONV_W:3 * CONV_W], 0.0)
        rows = lax.broadcasted_iota(jnp.int32, (tm, CONV_W), 0)
        u1, u2 = _conv_taps(u, halo, rows)
        w0, w1, w2 = cw_ref[0:1, :], cw_ref[1:2, :], cw_ref[2:3, :]
        y = w0 * u2 + w1 * u1 + w2 * u
        dyav = dya_ref[...]
        dy = dyav * xb
        nxt = jnp.where(i < n_i - 1, dyan_ref[...] * zn_ref[:, 0:CONV_W], 0.0)
        dy1 = jnp.where(rows == tm - 1, nxt[0:1, :], pltpu.roll(dy, tm - 1, 0))
        dy2 = jnp.where(rows == tm - 1, nxt[1:2, :], jnp.where(rows == tm - 2, nxt[0:1, :], pltpu.roll(dy, tm - 2, 0)))
        du = w2 * dy + w1 * dy1 + w0 * dy2
        dz_ref[:, 0:CONV_W] = (dyav * y).astype(BF16)
        dz_ref[:, CONV_W:2 * CONV_W] = (du * xu).astype(BF16)
        dz_ref[:, 2 * CONV_W:3 * CONV_W] = (du * xc).astype(BF16)

        lane = lax.broadcasted_iota(jnp.int32, (tm, 128), 1)
        cs_v, sn_v = cs_ref[...], sn_ref[...]
        dkr = jnp.zeros((tm, 128), F32)
        for h in range(HEADS):
            o = h * HEAD_SLOT
            dqf_ref[:, o:o + 128] = dq_ref[:, o:o + 128].astype(BF16)
            dqf_ref[:, o + 128:o + 256] = _rope_t(dq_ref[:, o + 128:o + 256], cs_v, sn_v, lane).astype(BF16)
            dkvf_ref[:, h * 128:(h + 1) * 128] = dk_ref[:, o:o + 128].astype(BF16)
            dkr = dkr + dk_ref[:, o + 128:o + 256]
        dkvf_ref[:, HEADS * 128:] = dv_ref[...].astype(BF16)

        c0 = 3 * CONV_W
        dcqn = _dot_nt(dqf_ref[...], wuq_ref[...])
        dcq, dgq = latent_bwd(z_ref[:, c0:c0 + Q_LORA], dcqn, gq_ref[...])
        dz_ref[:, c0:c0 + Q_LORA] = dcq.astype(BF16)
        c1 = c0 + Q_LORA
        dckvn = _dot_nt(dkvf_ref[...], wukv_ref[...])
        dckv, dgkv = latent_bwd(z_ref[:, c1:c1 + KV_LORA], dckvn, gkv_ref[...])
        dz_ref[:, c1:c1 + KV_LORA] = dckv.astype(BF16)
        dz_ref[:, c1 + KV_LORA:Z_COLS] = _rope_t(dkr, cs_v, sn_v, lane).astype(BF16)

        @pl.when(i == 0)
        def _():
            p_ref[...] = jnp.zeros_like(p_ref)

        p_ref[:, 0:CONV_W] += _rsum8(dy * u2)
        p_ref[:, CONV_W:2 * CONV_W] += _rsum8(dy * u1)
        p_ref[:, 2 * CONV_W:3 * CONV_W] += _rsum8(dy * u)
        p_ref[:, c0:c0 + Q_LORA] += _rsum8(dgq)
        p_ref[:, c1:c1 + KV_LORA] += _rsum8(dgkv)

        @pl.when(i == n_i - 1)
        def _():
            _all_rows(p_ref)

    def rows_of(n):
        return pl.BlockSpec((tm, n), lambda i: (i, 0))

    def whole(shape):
        return pl.BlockSpec(shape, lambda i: (0, 0))

    def prev8(n):
        return pl.BlockSpec((8, n), lambda i: (jnp.maximum(i * hb - 1, 0), 0))

    def next8(n):
        return pl.BlockSpec((8, n), lambda i: (jnp.minimum((i + 1) * hb, last_blk), 0))

    return _hosted_call(
        body, name="mix_mid_bwd", grid=(n_i,),
        in_specs=[rows_of(Z_COLS), prev8(Z_COLS), next8(Z_COLS), rows_of(CONV_W), next8(CONV_W),
                  whole((8, CONV_W)), whole((1, Q_LORA)), whole((1, KV_LORA)),
                  whole((Q_LORA, HEADS * HEAD_SLOT)), whole((KV_LORA, 2 * HEADS * 128)),
                  rows_of(128), rows_of(128),
                  rows_of(HEADS * HEAD_SLOT), rows_of(HEADS * HEAD_SLOT), rows_of(HEADS * V_HEAD)],
        out_specs=[rows_of(Z_COLS), rows_of(HEADS * HEAD_SLOT), rows_of(2 * HEADS * 128), whole((8, MID_SUMS))],
        out_shape=[jax.ShapeDtypeStruct((S, Z_COLS), BF16), jax.ShapeDtypeStruct((S, HEADS * HEAD_SLOT), BF16),
                   jax.ShapeDtypeStruct((S, 2 * HEADS * 128), BF16), jax.ShapeDtypeStruct((8, MID_SUMS), F32)],
        scratch_shapes=[], semantics=("arbitrary",),
        args=(z, z, z, dya, dya, conv_w, gq, gkv, wuq, wukv, cs, sn, dq, dk, dv), comm=comm)


ADA_Q = N_MOD * D // N_CHIPS
ADA_TN = 768


def _ada_forward(c_all, ada_w_q, ada_b_q):
    def body(c_ref, w_ref, b_ref, o_ref):
        cv = c_ref[...]
        sc = (cv * jax.nn.sigmoid(cv)).astype(BF16)
        o_ref[...] = jnp.dot(sc, w_ref[...].astype(BF16), preferred_element_type=F32) + b_ref[...]

    return pl.pallas_call(
        body, name="ada_forward", grid=(ADA_Q // ADA_TN,),
        in_specs=[pl.BlockSpec((16, D), lambda j: (0, 0)), pl.BlockSpec((D, ADA_TN), lambda j: (0, j)),
                  pl.BlockSpec((1, ADA_TN), lambda j: (0, j))],
        out_specs=pl.BlockSpec((16, ADA_TN), lambda j: (0, j)),
        out_shape=jax.ShapeDtypeStruct((16, ADA_Q), F32),
        compiler_params=_params(("parallel",)),
    )(c_all, ada_w_q, ada_b_q)


def _ada_wgrad(c_all, dmod_q):
    def body(c_ref, d_ref, o_ref):
        cv = c_ref[...]
        sc = (cv * jax.nn.sigmoid(cv)).astype(BF16)
        o_ref[...] = lax.dot_general(sc, d_ref[...].astype(BF16), (((0,), (0,)), ((), ())),
                                     preferred_element_type=F32)

    return pl.pallas_call(
        body, name="ada_wgrad", grid=(ADA_Q // ADA_TN,),
        in_specs=[pl.BlockSpec((16, D), lambda j: (0, 0)), pl.BlockSpec((16, ADA_TN), lambda j: (0, j))],
        out_specs=pl.BlockSpec((D, ADA_TN), lambda j: (0, j)),
        out_shape=jax.ShapeDtypeStruct((D, ADA_Q), F32),
        compiler_params=_params(("parallel",)),
    )(c_all, dmod_q)


def _sum_devices(parts):
    n = parts.shape[1]

    def body(p_ref, o_ref):
        o_ref[...] = jnp.broadcast_to(jnp.sum(p_ref[...], axis=0, keepdims=True), o_ref.shape)

    return pl.pallas_call(
        body, name="sum_devices",
        in_specs=[pl.BlockSpec((N_DEV, n), lambda: (0, 0))], out_specs=pl.BlockSpec((N_DEV, n), lambda: (0, 0)),
        out_shape=jax.ShapeDtypeStruct((N_DEV, n), F32),
    )(parts)


def _adamw(w, g, m, v, *, name):
    _, rows, cols = w.shape
    tr = _row_tile(rows, 256)

    def body(w_ref, g_ref, m_ref, v_ref, go_ref, d_ref, mo_ref, vo_ref):
        gv = g_ref[...]
        mn = B1 * m_ref[0] + (1.0 - B1) * gv
        vn = B2 * v_ref[0] + (1.0 - B2) * (gv * gv)
        m_hat = mn / (1.0 - B1 ** STEP)
        v_hat = vn / (1.0 - B2 ** STEP)
        go_ref[0] = gv
        d_ref[0] = -LR * (m_hat / (jnp.sqrt(v_hat) + AEPS) + WD * w_ref[0])
        mo_ref[0] = mn
        vo_ref[0] = vn

    blk = pl.BlockSpec((1, tr, cols), lambda i: (0, i, 0))
    return pl.pallas_call(
        body, name=name, grid=(rows // tr,),
        in_specs=[blk, pl.BlockSpec((tr, cols), lambda i: (i, 0)), blk, blk], out_specs=[blk] * 4,
        out_shape=[jax.ShapeDtypeStruct((1, rows, cols), F32)] * 4,
        compiler_params=_params(("parallel",)),
    )(w, g, m, v)


def _small_allgather(v, *, name):
    m, n = v.shape

    def body(x_ref, out_ref, send_sems, recv_sems, local_sem):
        x, y, c = _place()
        me, sibling = (x, y, c), (x, y, 1 - c)
        chips = [(1 - x, y), (x, 1 - y), (1 - x, 1 - y)]

        def rows(px, py, pc):
            return out_ref.at[pl.ds((4 * px + 2 * py + pc) * m, m), :]

        def copy(k, block, to, src=None):
            return pltpu.make_async_remote_copy(
                src_ref=rows(*block) if src is None else src, dst_ref=rows(*block),
                send_sem=send_sems.at[k], recv_sem=recv_sems.at[k], device_id=to, device_id_type=MESH)

        mine = pltpu.make_async_copy(x_ref, rows(*me), local_sem)
        mine.start()
        first = [copy(0, me, sibling, src=x_ref)]
        first += [copy(1 + j, me, (*chip, c), src=x_ref) for j, chip in enumerate(chips)]
        for cp in first:
            cp.start()
        passed = [copy(4 + j, (*chip, c), sibling) for j, chip in enumerate(chips)]
        for j, chip in enumerate(chips):
            copy(1 + j, (*chip, c), me).wait_recv()
            passed[j].start()
        copy(0, sibling, me).wait_recv()
        for j, chip in enumerate(chips):
            copy(4 + j, (*chip, 1 - c), me).wait_recv()
        for cp in first + passed:
            cp.wait_send()
        mine.wait()

    return pl.pallas_call(
        body, name=name,
        out_shape=jax.ShapeDtypeStruct((N_DEV * m, n), v.dtype),
        in_specs=[pl.BlockSpec(memory_space=pltpu.VMEM)], out_specs=pl.BlockSpec(memory_space=pltpu.VMEM),
        scratch_shapes=[pltpu.SemaphoreType.DMA((7,)), pltpu.SemaphoreType.DMA((7,)), pltpu.SemaphoreType.DMA],
    )(v)


ADD_BLOCKS = 2


def _pair_add(place, gs, ts, *, name):
    n_a = len(gs)

    def body(pl_ref, *refs):
        g_refs, t_refs = refs[:n_a], refs[n_a:2 * n_a]
        pf_refs, pb_refs = refs[2 * n_a:3 * n_a], refs[3 * n_a:]
        for g_ref, t_ref, pf_ref, pb_ref in zip(g_refs, t_refs, pf_refs, pb_refs):
            s = g_ref[...] + t_ref[...]
            pf_ref[...] = s
            pb_ref[...] = s.astype(BF16)

    def blk(t, own_half):
        tr = t.shape[1] // ADD_BLOCKS
        if own_half:
            return pl.BlockSpec((1, tr, t.shape[2]), lambda q, r, p: (q, p[0] * ADD_BLOCKS + r, 0))
        return pl.BlockSpec((1, tr, t.shape[2]), lambda q, r, p: (q, r, 0))

    grid_spec = pltpu.PrefetchScalarGridSpec(
        num_scalar_prefetch=1, grid=(N_CHIPS, ADD_BLOCKS),
        in_specs=[blk(t, True) for t in ts] + [blk(t, False) for t in ts],
        out_specs=[blk(t, False) for t in ts] * 2)
    res = pl.pallas_call(
        body, name=name, grid_spec=grid_spec,
        out_shape=[jax.ShapeDtypeStruct(t.shape, F32) for t in ts] + [jax.ShapeDtypeStruct(t.shape, BF16) for t in ts],
        compiler_params=_params(("parallel", "parallel")),
    )(place, *gs, *ts)
    return list(res[:n_a]), list(res[n_a:])


def _chip_add(place, pfs, ts, *, name):
    n_a = len(pfs)

    def body(pl_ref, *refs):
        pf_refs, t_refs, o_refs = refs[:n_a], refs[n_a:4 * n_a], refs[4 * n_a:]
        for i, (pf_ref, o_ref) in enumerate(zip(pf_refs, o_refs)):
            t1, t2, t3 = t_refs[3 * i:3 * i + 3]
            o_ref[...] = ((pf_ref[0] + t1[0].astype(F32)) + t2[0].astype(F32)) + t3[0].astype(F32)

    def slot(t, j):
        return pl.BlockSpec((1, t.shape[1] // ADD_BLOCKS, t.shape[2]), lambda r, p: (p[1] ^ j, r, 0))

    grid_spec = pltpu.PrefetchScalarGridSpec(
        num_scalar_prefetch=1, grid=(ADD_BLOCKS,),
        in_specs=[slot(t, 0) for t in pfs] + [slot(t, j) for t in ts for j in (1, 2, 3)],
        out_specs=[pl.BlockSpec((t.shape[1] // ADD_BLOCKS, t.shape[2]), lambda r, p: (r, 0)) for t in pfs])
    res = pl.pallas_call(
        body, name=name, grid_spec=grid_spec,
        out_shape=[jax.ShapeDtypeStruct(t.shape[1:], F32) for t in pfs],
        compiler_params=_params(("parallel",)),
    )(place, *pfs, *[t for t in ts for _ in range(3)])
    return list(res)


BULK = [("ffn1_w1", D, FF // 4, True), ("ffn1_w3", D, FF // 4, True), ("ffn1_w2", FF // 4, D, False),
        ("w_in", D, IN_COLS // 4, True), ("w_uq", Q_LORA, 768 // 4, True), ("w_ukv", KV_LORA, 1024 // 4, True),
        ("w_out", D // 4, D, False),
        ("ffn2_w1", D, FF // 4, True), ("ffn2_w3", D, FF // 4, True), ("ffn2_w2", FF // 4, D, False)]


def _group(*names):
    return [b for b in BULK if b[0] in names]


W_FIRST = _group("ffn1_w1", "ffn1_w3")
W_REST = [b for b in BULK if b not in W_FIRST]
G_FFN2 = _group("ffn2_w1", "ffn2_w3", "ffn2_w2")
G_MIX = _group("w_in", "w_uq", "w_ukv", "w_out")
G_FFN1 = _group("ffn1_w1", "ffn1_w3", "ffn1_w2")


def _gathered_weights(specs, shards, got, myq):
    out = {}
    for (name, _, _, by_cols), part in zip(specs, got):
        part = lax.dynamic_update_slice_in_dim(part, shards[name][None], myq, axis=0)
        out[name] = _full_weight(part, by_cols)
    return out


def _full_weight(parts, by_cols):
    if by_cols:
        return jnp.transpose(parts, (1, 0, 2)).reshape(parts.shape[1], -1)
    return parts.reshape(-1, parts.shape[2])


def _quarters(g, by_cols):
    if by_cols:
        k, n = g.shape
        return jnp.transpose(g.reshape(k, N_CHIPS, n // N_CHIPS), (1, 0, 2))
    return g.reshape(N_CHIPS, g.shape[0] // N_CHIPS, g.shape[1])


def _pad_heads(w_uq):
    w = w_uq.reshape(Q_LORA, HEADS, QK_NOPE + QK_ROPE)
    return jnp.pad(w, ((0, 0), (0, 0), (0, HEAD_SLOT - QK_NOPE - QK_ROPE))).reshape(Q_LORA, HEADS * HEAD_SLOT)


def _unpad_heads(g):
    return g.reshape(Q_LORA, HEADS, HEAD_SLOT)[:, :, :QK_NOPE + QK_ROPE].reshape(Q_LORA, HEADS * (QK_NOPE + QK_ROPE))


def _split_kv(w_ukv):
    return jnp.transpose(w_ukv.reshape(KV_LORA, HEADS, 2, 128), (0, 2, 1, 3)).reshape(KV_LORA, 2 * HEADS * 128)


def _merge_kv(g):
    return jnp.transpose(g.reshape(KV_LORA, 2, HEADS, 128), (0, 2, 1, 3)).reshape(KV_LORA, 2 * HEADS * 128)


def _rope_tables(positions):
    inv_freq = ROPE_THETA ** (-jnp.arange(0, QK_ROPE, 2, dtype=F32) / QK_ROPE)
    ang = positions.astype(F32)[:, None] * inv_freq
    cos, sin, zero = jnp.cos(ang), jnp.sin(ang), jnp.zeros((positions.shape[0], 64), F32)
    return jnp.concatenate([cos, cos, zero], axis=1), jnp.concatenate([sin, sin, zero], axis=1)


def _reduce_tail(place, specs, pfs, t2s, tag, host=None):
    rhs = _chip_add(place, pfs, t2s, name=tag + "_chip_add")
    if host is None:
        out, others = None, _run_comm(_PairShare(rhs), name=tag + "_pair_share")
    else:
        out, others = host(_PairShare(rhs))
    south = place[0] == 0
    return out, {b[0]: jnp.concatenate([jnp.where(south, rh, ot), jnp.where(south, ot, rh)], axis=0)
                 for b, rh, ot in zip(specs, rhs, others)}


def _local_step(x, positions, target, mod, vec, conv_w, w_first, rest_shards, place):
    row = lambda k: mod[k:k + 1]
    sh1, sc1, g1, sh2, sc2, g2, sh3, sc3, g3 = [row(k) for k in range(N_MOD)]
    cs, sn = _rope_tables(positions)
    cw8 = jnp.pad(conv_w, ((0, 5), (0, 0)))
    ga, gb = _group_mats()
    dist = place is not None
    comm = lambda prog: prog if dist else None

    gather = _Gather([rest_shards[b[0]] for b in W_REST]) if dist else None
    (h1, a1, b1, u1), got = _ffn_up(x, vec["norm_ffn1_g"], sh1, sc1, w_first["ffn1_w1"], w_first["ffn1_w3"],
                                    name="ffn1_up", comm=gather)
    if dist:
        w = _gathered_weights(W_REST, rest_shards, got, place[1])
    else:
        w = dict(rest_shards)
    w.update(w_first)
    w_in = jnp.pad(w["w_in"], ((0, 0), (0, Z_COLS - IN_COLS)))
    wuq = _pad_heads(w["w_uq"])
    wukv = _split_kv(w["w_ukv"])
    x1, f1 = _ffn_down(u1, w["ffn1_w2"], x, g1, name="ffn1_down")
    h2, z = _mix_in(x1, vec["norm_mix_g"], sh2, sc2, w_in)
    ya, q, k, v, cqn, ckvn = _mix_mid(z, cw8, vec["q_norm_g"], vec["kv_norm_g"], wuq, wukv, cs, sn)
    o, lse = _attention(q, k, v)
    x2, yn, yo = _mix_out(ya, o, vec["out_norm_g"], w["w_out"], x1, g2, ga, gb)
    (h3, a3, b3, u3), _ = _ffn_up(x2, vec["norm_ffn2_g"], sh3, sc3, w["ffn2_w1"], w["ffn2_w3"], name="ffn2_up")
    x3, f3 = _ffn_down(u3, w["ffn2_w2"], x2, g3, name="ffn2_down")
    dx3, dgfin, loss_blk = _final_loss(x3, vec["final_norm_g"], target)

    grads, reduced = {}, {}

    def tn(a, b, tm, tn_, name, prog=None):
        if prog is None:
            return _tn_matmul(a, b, tm=tm, tn=tn_, name=name), None
        return _tn_matmul(a, b, tm=tm, tn=tn_, name=name, comm=prog)

    def slab_of(specs):
        return [_quarters(grads[n], by_cols) for n, _, _, by_cols in specs]

    (df3, da3, db3, dg3), _ = _ffn_bwd_du(dx3, g3, f3, w["ffn2_w2"], a3, b3, name="ffn2_bwd_du")
    grads["ffn2_w2"], _ = tn(u3, df3, FF // 2, D, "ffn2_dw2")
    grads["ffn2_w1"], _ = tn(h3, da3, D, FF // 2, "ffn2_dw1")
    grads["ffn2_w3"], _ = tn(h3, db3, D, FF // 2, "ffn2_dw3")
    (dx2, s3), _ = _dh_normbwd([(da3, w["ffn2_w1"]), (db3, w["ffn2_w3"])], x2, vec["norm_ffn2_g"], sc3, dx3,
                               name="ffn2_bwd_dh")

    p1 = slab_of(G_FFN2) if dist else None
    (dyo, dya, do, delta, s_out), t1 = _mix_out_bwd(dx2, g2, yo, w["w_out"], ya, o, vec["out_norm_g"], ga, gb,
                                                    comm=comm(_PairExchange(p1) if dist else None))
    grads["w_out"], _ = tn(yn, dyo, D, D, "dw_out")
    if dist:
        pf1, pb1 = _pair_add(place, p1, t1, name="ffn2g_pair_add")
    dq, dk, dv = _attention_bwd(q, k, v, do, lse, delta)
    (dz, dqf, dkvf, s_mid), t2 = _mix_mid_bwd(z, dya, cw8, vec["q_norm_g"], vec["kv_norm_g"], wuq, wukv, cs, sn,
                                              dq, dk, dv, comm=comm(_ChipExchange(pb1) if dist else None))
    g_uq, _ = tn(cqn, dqf, Q_LORA, HEADS * HEAD_SLOT, "dw_uq")
    g_ukv, _ = tn(ckvn, dkvf, KV_LORA, 2 * HEADS * 128, "dw_ukv")
    grads["w_uq"], grads["w_ukv"] = _unpad_heads(g_uq), _merge_kv(g_ukv)
    if dist:
        g_in, red = _reduce_tail(place, G_FFN2, pf1, t2, "ffn2g",
                                 host=lambda prog: tn(h2, dz, D, Z_COLS // 2, "dw_in", prog))
        reduced.update(red)
    else:
        g_in, _ = tn(h2, dz, D, Z_COLS // 2, "dw_in")
    grads["w_in"] = g_in[:, :IN_COLS]
    (dx1, s2), _ = _dh_normbwd([(dz, w_in)], x1, vec["norm_mix_g"], sc2, dx2, name="mix_bwd_dh")

    p2 = slab_of(G_MIX) if dist else None
    (df1, da1, db1, dg1), t1 = _ffn_bwd_du(dx1, g1, f1, w["ffn1_w2"], a1, b1, name="ffn1_bwd_du",
                                           comm=comm(_PairExchange(p2) if dist else None))
    if dist:
        pf2, pb2 = _pair_add(place, p2, t1, name="mixg_pair_add")
        grads["ffn1_w2"], t2 = tn(u1, df1, FF // 2, D, "ffn1_dw2", _ChipExchange(pb2))
        grads["ffn1_w1"], red = _reduce_tail(place, G_MIX, pf2, t2, "mixg",
                                             host=lambda prog: tn(h1, da1, D, FF // 2, "ffn1_dw1", prog))
        reduced.update(red)
    else:
        grads["ffn1_w2"], _ = tn(u1, df1, FF // 2, D, "ffn1_dw2")
        grads["ffn1_w1"], _ = tn(h1, da1, D, FF // 2, "ffn1_dw1")
    grads["ffn1_w3"], _ = tn(h1, db1, D, FF // 2, "ffn1_dw3")

    p3 = slab_of(G_FFN1) if dist else None
    (dx0, s1), t1 = _dh_normbwd([(da1, w["ffn1_w1"]), (db1, w["ffn1_w3"])], x, vec["norm_ffn1_g"], sc1, dx1,
                                name="ffn1_bwd_dh", comm=comm(_PairExchange(p3) if dist else None))
    if dist:
        pf3, pb3 = _pair_add(place, p3, t1, name="ffn1g_pair_add")
        t2 = _run_comm(_ChipExchange(pb3), name="ffn1g_chip_exchange")
        _, red = _reduce_tail(place, G_FFN1, pf3, t2, "ffn1g")
        reduced.update(red)
    else:
        reduced = grads

    def part(s, k):
        return s[0:1, k * D:(k + 1) * D]

    dmod = jnp.concatenate([part(s1, 1), part(s1, 0), dg1[0:1], part(s2, 1), part(s2, 0), part(s_out, 0),
                            part(s3, 1), part(s3, 0), dg3[0:1]], axis=1)
    small = {"norm_ffn1_g": part(s1, 2), "norm_mix_g": part(s2, 2), "out_norm_g": part(s_out, 1),
             "norm_ffn2_g": part(s3, 2), "final_norm_g": dgfin[0:1],
             "q_norm_g": s_mid[0:1, 3 * CONV_W:3 * CONV_W + Q_LORA],
             "kv_norm_g": s_mid[0:1, 3 * CONV_W + Q_LORA:MID_SUMS], "conv_w": s_mid[0:1, 0:3 * CONV_W]}
    return loss_blk, dx0, reduced, dmod, small


SMALL = [("norm_ffn1_g", D), ("norm_mix_g", D), ("out_norm_g", D), ("norm_ffn2_g", D), ("final_norm_g", D),
         ("q_norm_g", Q_LORA), ("kv_norm_g", KV_LORA), ("conv_w", 3 * CONV_W)]
WEIGHTS = ['ada_w', 'ada_b', 'norm_ffn1_g', 'ffn1_w1', 'ffn1_w3', 'ffn1_w2', 'norm_mix_g', 'w_in', 'conv_w',
           'q_norm_g', 'w_uq', 'kv_norm_g', 'w_ukv', 'out_norm_g', 'w_out', 'norm_ffn2_g', 'ffn2_w1', 'ffn2_w3',
           'ffn2_w2', 'final_norm_g']


def kernel(x, c, positions, ada_w, ada_b, norm_ffn1_g, ffn1_w1, ffn1_w3, ffn1_w2, norm_mix_g, w_in, conv_w, q_norm_g, w_uq, kv_norm_g, w_ukv, out_norm_g, w_out, norm_ffn2_g, ffn2_w1, ffn2_w3, ffn2_w2, final_norm_g, loss_target, m_ada_w, m_ada_b, m_norm_ffn1_g, m_ffn1_w1, m_ffn1_w3, m_ffn1_w2, m_norm_mix_g, m_w_in, m_conv_w, m_q_norm_g, m_w_uq, m_kv_norm_g, m_w_ukv, m_out_norm_g, m_w_out, m_norm_ffn2_g, m_ffn2_w1, m_ffn2_w3, m_ffn2_w2, m_final_norm_g, v_ada_w, v_ada_b, v_norm_ffn1_g, v_ffn1_w1, v_ffn1_w3, v_ffn1_w2, v_norm_mix_g, v_w_in, v_conv_w, v_q_norm_g, v_w_uq, v_kv_norm_g, v_w_ukv, v_out_norm_g, v_w_out, v_norm_ffn2_g, v_ffn2_w1, v_ffn2_w3, v_ffn2_w2, v_final_norm_g):
    args = dict(locals())
    wts = {n: args[n] for n in WEIGHTS}
    mom = {n: args["m_" + n] for n in WEIGHTS}
    var = {n: args["v_" + n] for n in WEIGHTS}
    ax, ay, ac = _place()
    myq = 2 * ax + ay
    me = 2 * myq + ac
    place = jnp.stack([ac, myq]).astype(jnp.int32)

    shards = {name: wts[name][0].astype(BF16) for name, *_ in BULK}
    first = _run_comm(_Gather([shards[b[0]] for b in W_FIRST]), name="gather_ffn1")
    w_first = _gathered_weights(W_FIRST, shards, first, myq)

    mine = jnp.concatenate([c, conv_w[0].reshape(1, 3 * CONV_W // N_CHIPS)], axis=1)
    seen = _small_allgather(jnp.pad(mine, ((0, 7), (0, 0))), name="gather_cond").reshape(N_DEV, 8, -1)[:, 0]
    c_all = jnp.pad(seen[:, :D], ((0, 8), (0, 0)))
    conv_full = jnp.transpose(seen[0::2, D:].reshape(N_CHIPS, 3, CONV_W // N_CHIPS), (1, 0, 2)).reshape(3, CONV_W)
    ada_b_q = lax.dynamic_slice_in_dim(ada_b, myq * ADA_Q, ADA_Q, axis=1)
    mod_q = _ada_forward(c_all, ada_w[0], ada_b_q)
    mod_all = _small_allgather(mod_q, name="gather_mod").reshape(N_DEV, 16, ADA_Q)
    mod_rows = jnp.transpose(mod_all[0::2, :N_DEV], (1, 0, 2)).reshape(N_DEV, N_MOD * D)
    mod = lax.dynamic_slice_in_dim(mod_rows, me, 1, axis=0).reshape(N_MOD, D)

    vec = {n: wts[n] for n in ("norm_ffn1_g", "norm_mix_g", "q_norm_g", "kv_norm_g", "out_norm_g", "norm_ffn2_g")}
    vec["final_norm_g"] = final_norm_g.reshape(1, D)
    loss_blk, grad_x, gq, dmod, small = _local_step(x[0], positions[0], loss_target[0], mod, vec, conv_full, w_first,
                                                    {b[0]: shards[b[0]] for b in W_REST}, place)
    loss = lax.psum(loss_blk[0, 0], ("x", "y", "c"))

    rows = jnp.concatenate([dmod] + [small[n] for n, _ in SMALL], axis=1)
    width = rows.shape[1]
    fold = -(-width // (8 * 128)) * 128
    rows = jnp.pad(rows, ((0, 0), (0, 8 * fold - width))).reshape(8, fold)
    every = _small_allgather(rows, name="gather_small").reshape(N_DEV, 8 * fold)[:, :width]
    total = _sum_devices(every)[0:1]
    dmod_q = lax.dynamic_slice_in_dim(every[:, :N_MOD * D], myq * ADA_Q, ADA_Q, axis=1)
    g = {name: gq[name] for name, *_ in BULK}
    g["ada_w"] = _ada_wgrad(c_all, jnp.pad(dmod_q, ((0, 8), (0, 0))))
    g["ada_b"] = total[:, :N_MOD * D]
    off = N_MOD * D
    for n, width in SMALL:
        g[n] = total[:, off:off + width]
        off += width
    g["conv_w"] = lax.dynamic_slice_in_dim(g["conv_w"].reshape(3, CONV_W), myq * (CONV_W // N_CHIPS),
                                           CONV_W // N_CHIPS, axis=1)
    g["final_norm_g"] = g["final_norm_g"].reshape(D)

    delta, new_m, new_v = {}, {}, {}
    for name in ["ada_w"] + [b[0] for b in BULK]:
        g[name], delta[name], new_m[name], new_v[name] = _adamw(wts[name], g[name], mom[name], var[name],
                                                                name="adamw_" + name)
    smalls = ["ada_b"] + [n for n, _ in SMALL]

    def packed(d):
        flat = jnp.concatenate([d[n].reshape(1, -1) for n in smalls], axis=1)
        return jnp.pad(flat.reshape(-1, D), ((0, 1), (0, 0)))

    res = _adamw(packed(wts)[None], packed(g), packed(mom)[None], packed(var)[None], name="adamw_small")[1:]
    off = 0
    for n in smalls:
        size = wts[n].size
        for d, r in zip((delta, new_m, new_v), res):
            d[n] = r.reshape(-1)[off:off + size].reshape(wts[n].shape)
        g[n] = g[n].reshape(wts[n].shape)
        off += size

    return (loss, grad_x[None], *[g[n] for n in WEIGHTS], *[delta[n] for n in WEIGHTS],
            *[new_m[n] for n in WEIGHTS], *[new_v[n] for n in WEIGHTS])
```

```python
import functools

import numpy as np
import jax
import jax.numpy as jnp
from jax import lax
from jax.experimental import pallas as pl
from jax.experimental.pallas import tpu as pltpu

F32 = jnp.float32
BF16 = jnp.bfloat16
MESH = pl.DeviceIdType.MESH

D = 1024
FF = 2816
CONV_W = 512
CONV_GROUP = 64
HEADS = 4
QK_NOPE = 128
QK_ROPE = 64
V_HEAD = 128
Q_LORA = 384
KV_LORA = 256
HEAD_SLOT = 256
IN_COLS = 3 * CONV_W + Q_LORA + KV_LORA + QK_ROPE
Z_COLS = 2304
EPS = 1e-6
ROPE_THETA = 10000.0
CHUNK = 64
ATT_SCALE = (QK_NOPE + QK_ROPE) ** -0.5
NEG = -1e30
EXP2_SCALE = ATT_SCALE * 1.4426950408889634
N_MOD = 9

LR, B1, B2, AEPS, WD, STEP = 0.001, 0.9, 0.999, 1e-08, 0.01, 10

N_CHIPS = 4
N_DEV = 8
VMEM_LIMIT = 56 << 20


def _params(sem, vmem=VMEM_LIMIT):
    return pltpu.CompilerParams(dimension_semantics=sem, vmem_limit_bytes=vmem)


def _rms(v):
    return lax.rsqrt(jnp.mean(v * v, axis=-1, keepdims=True) + EPS)


def _rsum8(v):
    t, n = v.shape
    return jnp.sum(v.reshape(t // 8, 8, n), axis=0)


def _all_rows(ref):
    ref[...] = jnp.broadcast_to(jnp.sum(ref[...], axis=0, keepdims=True), ref.shape)


def _gsum(v, gmat):
    hi = v.astype(BF16)
    lo = (v - hi.astype(F32)).astype(BF16)
    return (jnp.dot(hi, gmat, preferred_element_type=F32)
            + jnp.dot(lo, gmat, preferred_element_type=F32))


def _dot_nt(a, b):
    return lax.dot_general(a, b, (((1,), (1,)), ((), ())), preferred_element_type=F32)


def _silu_parts(a):
    sg = jax.nn.sigmoid(a)
    return sg, a * sg


def _rope(xr, cs, sn, lane):
    rh = jnp.where(lane < 32, -pltpu.roll(xr, 96, 1), pltpu.roll(xr, 32, 1))
    return xr * cs + rh * sn


def _rope_t(g, cs, sn, lane):
    y = g * sn
    rt = jnp.where(lane < 32, pltpu.roll(y, 96, 1), jnp.where(lane < 64, -pltpu.roll(y, 32, 1), 0.0))
    return g * cs + rt


def _row_tile(rows, pref, mult=8):
    t = min(rows, pref) // mult * mult
    while rows % t:
        t -= mult
    return t


def _place():
    return lax.axis_index("x"), lax.axis_index("y"), lax.axis_index("c")


ANY = pl.BlockSpec(memory_space=pl.ANY)


def _hosted_call(body, *, name, grid, in_specs, out_specs, out_shape, scratch_shapes, semantics, args, comm=None):
    n_in, n_out, n_scr = len(in_specs), len(out_specs), len(scratch_shapes)
    if comm is None:
        res = pl.pallas_call(body, name=name, grid=grid, in_specs=in_specs, out_specs=out_specs, out_shape=out_shape,
                             scratch_shapes=scratch_shapes, compiler_params=_params(semantics))(*args)
        return list(res), []
    n_ci, n_co = len(comm.inputs), len(comm.out_shapes)
    total = int(np.prod(grid))

    def hosted(*refs):
        ins, refs = refs[:n_in], refs[n_in:]
        cins, refs = refs[:n_ci], refs[n_ci:]
        outs, refs = refs[:n_out], refs[n_out:]
        couts, refs = refs[:n_co], refs[n_co:]
        scratch, sems = refs[:n_scr], refs[n_scr]
        step = pl.program_id(0)
        for ax in range(1, len(grid)):
            step = step * grid[ax] + pl.program_id(ax)

        @pl.when(step == 0)
        def _():
            comm.start(cins, couts, sems)

        body(*ins, *outs, *scratch)

        @pl.when(step == total - 1)
        def _():
            comm.finish(cins, couts, sems)

    res = pl.pallas_call(
        hosted, name=name, grid=grid, in_specs=list(in_specs) + [ANY] * n_ci,
        out_specs=list(out_specs) + [ANY] * n_co, out_shape=list(out_shape) + list(comm.out_shapes),
        scratch_shapes=list(scratch_shapes) + [pltpu.SemaphoreType.DMA((comm.n_sems,))],
        compiler_params=_params(("arbitrary",) * len(grid)))(*args, *comm.inputs)
    return list(res[:n_out]), list(res[n_out:])


def _run_comm(comm, *, name):
    n_ci = len(comm.inputs)

    def body(*refs):
        cins, couts, sems = refs[:n_ci], refs[n_ci:-1], refs[-1]
        comm.start(cins, couts, sems)
        comm.finish(cins, couts, sems)

    return pl.pallas_call(
        body, name=name, out_shape=list(comm.out_shapes), in_specs=[ANY] * n_ci,
        out_specs=[ANY] * len(comm.out_shapes), scratch_shapes=[pltpu.SemaphoreType.DMA((comm.n_sems,))],
    )(*comm.inputs)


class _Gather:
    def __init__(self, slabs):
        self.inputs = list(slabs)
        self.out_shapes = [jax.ShapeDtypeStruct((N_CHIPS,) + s.shape, s.dtype) for s in slabs]
        self.n_sems = 12 * len(slabs)

    @staticmethod
    def _copy(out, sems, base, k, chip, hc, to, src=None):
        H = out.shape[1] // 2
        half = out.at[2 * chip[0] + chip[1], pl.ds(hc * H, H), :]
        return pltpu.make_async_remote_copy(
            src_ref=half if src is None else src, dst_ref=half, send_sem=sems.at[base + k],
            recv_sem=sems.at[base + 6 + k], device_id=to, device_id_type=MESH)

    def _firsts(self, src, out, sems, base):
        x, y, c = _place()
        H = src.shape[0] // 2
        chips = [(1 - x, y), (x, 1 - y), (1 - x, 1 - y)]
        return [self._copy(out, sems, base, j, (x, y), c, (*chip, c), src=src.at[pl.ds(c * H, H), :])
                for j, chip in enumerate(chips)]

    def start(self, ins, outs, sems):
        for i, (src, out) in enumerate(zip(ins, outs)):
            for cp in self._firsts(src, out, sems, 12 * i):
                cp.start()

    def finish(self, ins, outs, sems):
        x, y, c = _place()
        chips = [(1 - x, y), (x, 1 - y), (1 - x, 1 - y)]
        passed = []
        for i, out in enumerate(outs):
            for j, chip in enumerate(chips):
                self._copy(out, sems, 12 * i, j, chip, c, (x, y, c)).wait_recv()
                cp = self._copy(out, sems, 12 * i, 3 + j, chip, c, (x, y, 1 - c))
                cp.start()
                passed.append(cp)
        for i, out in enumerate(outs):
            for j, chip in enumerate(chips):
                self._copy(out, sems, 12 * i, 3 + j, chip, 1 - c, (x, y, c)).wait_recv()
        for cp in passed:
            cp.wait_send()
        for i, (src, out) in enumerate(zip(ins, outs)):
            for cp in self._firsts(src, out, sems, 12 * i):
                cp.wait_send()


class _PairExchange:
    def __init__(self, arrays):
        self.inputs = list(arrays)
        self.out_shapes = [jax.ShapeDtypeStruct((N_CHIPS, a.shape[1] // 2, a.shape[2]), a.dtype) for a in arrays]
        self.n_sems = 2 * len(arrays)

    def _copies(self, ins, outs, sems):
        x, y, c = _place()
        return [pltpu.make_async_remote_copy(
            src_ref=g.at[:, pl.ds((1 - c) * t.shape[1], t.shape[1]), :], dst_ref=t, send_sem=sems.at[2 * i],
            recv_sem=sems.at[2 * i + 1], device_id=(x, y, 1 - c), device_id_type=MESH)
            for i, (g, t) in enumerate(zip(ins, outs))]

    def start(self, ins, outs, sems):
        for cp in self._copies(ins, outs, sems):
            cp.start()

    def finish(self, ins, outs, sems):
        for cp in self._copies(ins, outs, sems):
            cp.wait()


class _ChipExchange:
    def __init__(self, arrays):
        self.inputs = list(arrays)
        self.out_shapes = [jax.ShapeDtypeStruct(a.shape, a.dtype) for a in arrays]
        self.n_sems = 6 * len(arrays)

    def _copies(self, p, t, sems, base):
        x, y, c = _place()
        myq = 2 * x + y
        chips = [(1 - x, y), (x, 1 - y), (1 - x, 1 - y)]
        sends = [pltpu.make_async_remote_copy(
            src_ref=p.at[2 * chip[0] + chip[1]], dst_ref=t.at[myq], send_sem=sems.at[base + j],
            recv_sem=sems.at[base + 3 + j], device_id=(*chip, c), device_id_type=MESH) for j, chip in enumerate(chips)]
        lands = [pltpu.make_async_remote_copy(
            src_ref=t.at[2 * chip[0] + chip[1]], dst_ref=t.at[2 * chip[0] + chip[1]], send_sem=sems.at[base + j],
            recv_sem=sems.at[base + 3 + j], device_id=(*chip, c), device_id_type=MESH) for j, chip in enumerate(chips)]
        return sends, lands

    def start(self, ins, outs, sems):
        for i, (p, t) in enumerate(zip(ins, outs)):
            for cp in self._copies(p, t, sems, 6 * i)[0]:
                cp.start()

    def finish(self, ins, outs, sems):
        for i, (p, t) in enumerate(zip(ins, outs)):
            sends, lands = self._copies(p, t, sems, 6 * i)
            for cp in lands:
                cp.wait_recv()
            for cp in sends:
                cp.wait_send()


class _PairShare:
    def __init__(self, arrays):
        self.inputs = list(arrays)
        self.out_shapes = [jax.ShapeDtypeStruct(a.shape, a.dtype) for a in arrays]
        self.n_sems = 2 * len(arrays)

    def _copies(self, ins, outs, sems):
        x, y, c = _place()
        return [pltpu.make_async_remote_copy(
            src_ref=r, dst_ref=o, send_sem=sems.at[2 * i], recv_sem=sems.at[2 * i + 1],
            device_id=(x, y, 1 - c), device_id_type=MESH) for i, (r, o) in enumerate(zip(ins, outs))]

    def start(self, ins, outs, sems):
        for cp in self._copies(ins, outs, sems):
            cp.start()

    def finish(self, ins, outs, sems):
        for cp in self._copies(ins, outs, sems):
            cp.wait()


def _ffn_up(x, ng, sh, sc, w1, w3, *, name, comm=None):
    S = x.shape[0]
    tm, tn = _row_tile(S, 512), FF // 2

    def body(x_ref, g_ref, sh_ref, sc_ref, w1_ref, w3_ref, h_ref, a_ref, b_ref, u_ref, hs):
        @pl.when(pl.program_id(1) == 0)
        def _():
            xv = x_ref[...]
            h = ((xv * _rms(xv)) * g_ref[...]) * (1.0 + sc_ref[...]) + sh_ref[...]
            hb = h.astype(BF16)
            hs[...] = hb
            h_ref[...] = hb

        h = hs[...]
        a = _dot_nt(h, w1_ref[...])
        b = _dot_nt(h, w3_ref[...])
        _, sa = _silu_parts(a)
        a_ref[...] = a.astype(BF16)
        b_ref[...] = b.astype(BF16)
        u_ref[...] = (sa * b).astype(BF16)

    row = pl.BlockSpec((tm, D), lambda i, j: (i, 0))
    vec = pl.BlockSpec((1, D), lambda i, j: (0, 0))
    wsp = pl.BlockSpec((tn, D), lambda i, j: (j, 0))
    osp = pl.BlockSpec((tm, tn), lambda i, j: (i, j))
    return _hosted_call(
        body, name=name, grid=(S // tm, FF // tn),
        in_specs=[row, vec, vec, vec, wsp, wsp],
        out_specs=[row, osp, osp, osp],
        out_shape=[jax.ShapeDtypeStruct((S, D), BF16)] + [jax.ShapeDtypeStruct((S, FF), BF16)] * 3,
        scratch_shapes=[pltpu.VMEM((tm, D), BF16)],
        semantics=("parallel", "arbitrary"), args=(x, ng, sh, sc, w1, w3), comm=comm)


def _ffn_down(u, w2, x, gate, *, name):
    S = x.shape[0]
    tm = _row_tile(S, 512)

    def body(u_ref, w2_ref, x_ref, g_ref, xo_ref, f_ref):
        f = jnp.dot(u_ref[...], w2_ref[...], preferred_element_type=F32)
        xo_ref[...] = x_ref[...] + (0.5 * g_ref[...]) * f
        f_ref[...] = f.astype(BF16)

    return pl.pallas_call(
        body, name=name, grid=(S // tm,),
        in_specs=[pl.BlockSpec((tm, FF), lambda i: (i, 0)), pl.BlockSpec((FF, D), lambda i: (0, 0)),
                  pl.BlockSpec((tm, D), lambda i: (i, 0)), pl.BlockSpec((1, D), lambda i: (0, 0))],
        out_specs=[pl.BlockSpec((tm, D), lambda i: (i, 0))] * 2,
        out_shape=[jax.ShapeDtypeStruct((S, D), F32), jax.ShapeDtypeStruct((S, D), BF16)],
        compiler_params=_params(("parallel",)),
    )(u, w2, x, gate)


def _ffn_bwd_du(dx, gate, f, w2, a, b, *, name, comm=None):
    S = dx.shape[0]
    tm, tn = _row_tile(S, 512), FF // 2
    n_i = S // tm

    def body(dx_ref, g_ref, f_ref, w_ref, a_ref, b_ref, df_ref, da_ref, db_ref, dg_ref, dfs):
        i, j = pl.program_id(0), pl.program_id(1)

        @pl.when((i == 0) & (j == 0))
        def _():
            dg_ref[...] = jnp.zeros_like(dg_ref)

        @pl.when(j == 0)
        def _():
            dxv = dx_ref[...]
            dfb = (dxv * (0.5 * g_ref[...])).astype(BF16)
            dfs[...] = dfb
            df_ref[...] = dfb
            dg_ref[...] += _rsum8(dxv * (0.5 * f_ref[...].astype(F32)))

        du = _dot_nt(dfs[...], w_ref[pl.ds(pl.multiple_of(j * tn, tn), tn), :])
        av = a_ref[...].astype(F32)
        sg, sa = _silu_parts(av)
        da_ref[...] = (du * b_ref[...].astype(F32) * (sg * (1.0 + av * (1.0 - sg)))).astype(BF16)
        db_ref[...] = (du * sa).astype(BF16)

        @pl.when((i == n_i - 1) & (j == FF // tn - 1))
        def _():
            _all_rows(dg_ref)

    row = pl.BlockSpec((tm, D), lambda i, j: (i, 0))
    blk = pl.BlockSpec((tm, tn), lambda i, j: (i, j))
    return _hosted_call(
        body, name=name, grid=(n_i, FF // tn),
        in_specs=[row, pl.BlockSpec((1, D), lambda i, j: (0, 0)), row,
                  pl.BlockSpec((FF, D), lambda i, j: (0, 0)), blk, blk],
        out_specs=[row, blk, blk, pl.BlockSpec((8, D), lambda i, j: (0, 0))],
        out_shape=[jax.ShapeDtypeStruct((S, D), BF16), jax.ShapeDtypeStruct((S, FF), BF16),
                   jax.ShapeDtypeStruct((S, FF), BF16), jax.ShapeDtypeStruct((8, D), F32)],
        scratch_shapes=[pltpu.VMEM((tm, D), BF16)],
        semantics=("arbitrary", "arbitrary"), args=(dx, gate, f, w2, a, b), comm=comm)


def _tn_matmul(a, b, *, tm, tn, name, comm=None):
    S, M = a.shape
    N = b.shape[1]
    ts = _row_tile(S, 1024)
    ns = S // ts

    def body(a_ref, b_ref, o_ref, acc):
        s = pl.program_id(2)
        p = lax.dot_general(a_ref[...], b_ref[...], (((0,), (0,)), ((), ())), preferred_element_type=F32)

        @pl.when(s == 0)
        def _():
            acc[...] = p

        @pl.when(s > 0)
        def _():
            acc[...] += p

        @pl.when(s == ns - 1)
        def _():
            o_ref[...] = acc[...]

    (out,), couts = _hosted_call(
        body, name=name, grid=(M // tm, N // tn, ns),
        in_specs=[pl.BlockSpec((ts, tm), lambda i, j, s: (s, i)), pl.BlockSpec((ts, tn), lambda i, j, s: (s, j))],
        out_specs=[pl.BlockSpec((tm, tn), lambda i, j, s: (i, j))],
        out_shape=[jax.ShapeDtypeStruct((M, N), F32)],
        scratch_shapes=[pltpu.VMEM((tm, tn), F32)],
        semantics=("parallel", "parallel", "arbitrary"), args=(a, b), comm=comm)
    return out if comm is None else (out, couts)


def _dh_normbwd(pairs, x, ng, sc, dx_next, *, name, comm=None):
    S = x.shape[0]
    tm = _row_tile(S, 256)
    n_i = S // tm
    n_p = len(pairs)

    def body(*refs):
        a_refs, w_refs = refs[:n_p], refs[n_p:2 * n_p]
        x_ref, g_ref, sc_ref, dxn_ref, dx_ref, p_ref = refs[2 * n_p:]
        i = pl.program_id(0)
        dh = jnp.dot(a_refs[0][...], w_refs[0][...], preferred_element_type=F32)
        for k in range(1, n_p):
            dh = dh + jnp.dot(a_refs[k][...], w_refs[k][...], preferred_element_type=F32)
        xv = x_ref[...]
        r = _rms(xv)
        xh = xv * r
        g = g_ref[...]
        dn = dh * (1.0 + sc_ref[...])
        dy = dn * g
        dx_ref[...] = dxn_ref[...] + r * (dy - xh * jnp.mean(dy * xh, axis=-1, keepdims=True))

        @pl.when(i == 0)
        def _():
            p_ref[...] = jnp.zeros_like(p_ref)

        p_ref[:, 0:D] += _rsum8(dh * (xh * g))
        p_ref[:, D:2 * D] += _rsum8(dh)
        p_ref[:, 2 * D:3 * D] += _rsum8(dn * xh)

        @pl.when(i == n_i - 1)
        def _():
            _all_rows(p_ref)

    row = pl.BlockSpec((tm, D), lambda i: (i, 0))
    vec = pl.BlockSpec((1, D), lambda i: (0, 0))
    in_specs = ([pl.BlockSpec((tm, a.shape[1]), lambda i: (i, 0)) for a, _ in pairs]
                + [pl.BlockSpec(w.shape, lambda i: (0, 0)) for _, w in pairs] + [row, vec, vec, row])
    return _hosted_call(
        body, name=name, grid=(n_i,), in_specs=in_specs,
        out_specs=[row, pl.BlockSpec((8, 3 * D), lambda i: (0, 0))],
        out_shape=[jax.ShapeDtypeStruct((S, D), F32), jax.ShapeDtypeStruct((8, 3 * D), F32)],
        scratch_shapes=[], semantics=("arbitrary",),
        args=(*[a for a, _ in pairs], *[w for _, w in pairs], x, ng, sc, dx_next), comm=comm)


def _final_loss(x3, gfin, tgt):
    S = x3.shape[0]
    tm = _row_tile(S, 512)
    n_i = S // tm

    def body(x_ref, g_ref, t_ref, dx_ref, dg_ref, loss_ref, lacc):
        i = pl.program_id(0)
        xv = x_ref[...]
        r = _rms(xv)
        xh = xv * r
        g = g_ref[...]
        e = xh * g - t_ref[...]
        dout = e * (1.0 / D)
        dy = dout * g
        dx_ref[...] = r * (dy - xh * jnp.mean(dy * xh, axis=-1, keepdims=True))

        @pl.when(i == 0)
        def _():
            dg_ref[...] = jnp.zeros_like(dg_ref)
            lacc[...] = jnp.zeros_like(lacc)

        dg_ref[...] += _rsum8(dout * xh)
        lacc[...] += _rsum8(e * e)

        @pl.when(i == n_i - 1)
        def _():
            _all_rows(dg_ref)
            tot = jnp.sum(jnp.sum(lacc[...], axis=0, keepdims=True), axis=1, keepdims=True)
            loss_ref[...] = jnp.broadcast_to(tot * (0.5 / D), loss_ref.shape)

    row = pl.BlockSpec((tm, D), lambda i: (i, 0))
    return pl.pallas_call(
        body, name="final_loss", grid=(n_i,),
        in_specs=[row, pl.BlockSpec((1, D), lambda i: (0, 0)), row],
        out_specs=[row, pl.BlockSpec((8, D), lambda i: (0, 0)), pl.BlockSpec((8, 128), lambda i: (0, 0))],
        out_shape=[jax.ShapeDtypeStruct((S, D), F32), jax.ShapeDtypeStruct((8, D), F32),
                   jax.ShapeDtypeStruct((8, 128), F32)],
        scratch_shapes=[pltpu.VMEM((8, D), F32)],
        compiler_params=_params(("arbitrary",)),
    )(x3, gfin, tgt)


def _mix_in(x, ng, sh, sc, w_in):
    S = x.shape[0]
    tm = _row_tile(S, 512)

    def body(x_ref, g_ref, sh_ref, sc_ref, w_ref, h_ref, z_ref):
        xv = x_ref[...]
        hb = (((xv * _rms(xv)) * g_ref[...]) * (1.0 + sc_ref[...]) + sh_ref[...]).astype(BF16)
        h_ref[...] = hb
        z_ref[...] = _dot_nt(hb, w_ref[...])

    row = pl.BlockSpec((tm, D), lambda i: (i, 0))
    vec = pl.BlockSpec((1, D), lambda i: (0, 0))
    return pl.pallas_call(
        body, name="mix_in", grid=(S // tm,),
        in_specs=[row, vec, vec, vec, pl.BlockSpec((Z_COLS, D), lambda i: (0, 0))],
        out_specs=[row, pl.BlockSpec((tm, Z_COLS), lambda i: (i, 0))],
        out_shape=[jax.ShapeDtypeStruct((S, D), BF16), jax.ShapeDtypeStruct((S, Z_COLS), F32)],
        compiler_params=_params(("parallel",)),
    )(x, ng, sh, sc, w_in)


def _conv_taps(u, halo, rows):
    u1 = jnp.where(rows == 0, halo[7:8, :], pltpu.roll(u, 1, 0))
    u2 = jnp.where(rows == 0, halo[6:7, :], jnp.where(rows == 1, halo[7:8, :], pltpu.roll(u, 2, 0)))
    return u1, u2


def _mix_mid(z, conv_w, gq, gkv, wuq, wukv, cs, sn):
    S = z.shape[0]
    tm = _row_tile(S, 512)
    hb = tm // 8

    def body(z_ref, zh_ref, cw_ref, gq_ref, gkv_ref, wuq_ref, wukv_ref, cs_ref, sn_ref,
             ya_ref, q_ref, k_ref, v_ref, cqn_ref, ckvn_ref):
        i = pl.program_id(0)
        xb = z_ref[:, 0:CONV_W]
        u = z_ref[:, CONV_W:2 * CONV_W] * z_ref[:, 2 * CONV_W:3 * CONV_W]
        halo = zh_ref[:, CONV_W:2 * CONV_W] * zh_ref[:, 2 * CONV_W:3 * CONV_W]
        halo = jnp.where(i > 0, halo, 0.0)
        rows = lax.broadcasted_iota(jnp.int32, (tm, CONV_W), 0)
        u1, u2 = _conv_taps(u, halo, rows)
        y = cw_ref[0:1, :] * u2 + cw_ref[1:2, :] * u1 + cw_ref[2:3, :] * u
        ya_ref[...] = xb * y

        lane = lax.broadcasted_iota(jnp.int32, (tm, 128), 1)
        cs_v, sn_v = cs_ref[...], sn_ref[...]
        cq = z_ref[:, 3 * CONV_W:3 * CONV_W + Q_LORA]
        cqn = ((cq * _rms(cq)) * gq_ref[...]).astype(BF16)
        cqn_ref[...] = cqn
        q = _dot_nt(cqn, wuq_ref[...])
        for h in range(HEADS):
            o = h * HEAD_SLOT
            q_ref[:, o:o + 128] = q[:, o:o + 128].astype(BF16)
            q_ref[:, o + 128:o + 256] = _rope(q[:, o + 128:o + 256], cs_v, sn_v, lane).astype(BF16)

        c0 = 3 * CONV_W + Q_LORA
        ckv = z_ref[:, c0:c0 + KV_LORA]
        ckvn = ((ckv * _rms(ckv)) * gkv_ref[...]).astype(BF16)
        ckvn_ref[...] = ckvn
        kv = jnp.dot(ckvn, wukv_ref[...], preferred_element_type=F32)
        krot = _rope(z_ref[:, c0 + KV_LORA:Z_COLS], cs_v, sn_v, lane).astype(BF16)
        for h in range(HEADS):
            o = h * HEAD_SLOT
            k_ref[:, o:o + 128] = kv[:, h * 128:(h + 1) * 128].astype(BF16)
            k_ref[:, o + 128:o + 256] = krot
        v_ref[...] = kv[:, HEADS * 128:].astype(BF16)

    def rows_of(n):
        return pl.BlockSpec((tm, n), lambda i: (i, 0))

    def whole(shape):
        return pl.BlockSpec(shape, lambda i: (0, 0))

    return pl.pallas_call(
        body, name="mix_mid", grid=(S // tm,),
        in_specs=[rows_of(Z_COLS), pl.BlockSpec((8, Z_COLS), lambda i: (jnp.maximum(i * hb - 1, 0), 0)),
                  whole((8, CONV_W)), whole((1, Q_LORA)), whole((1, KV_LORA)),
                  whole((HEADS * HEAD_SLOT, Q_LORA)), whole((KV_LORA, 2 * HEADS * 128)),
                  rows_of(128), rows_of(128)],
        out_specs=[rows_of(CONV_W), rows_of(HEADS * HEAD_SLOT), rows_of(HEADS * HEAD_SLOT), rows_of(HEADS * V_HEAD),
                   rows_of(Q_LORA), rows_of(KV_LORA)],
        out_shape=[jax.ShapeDtypeStruct((S, CONV_W), F32), jax.ShapeDtypeStruct((S, HEADS * HEAD_SLOT), BF16),
                   jax.ShapeDtypeStruct((S, HEADS * HEAD_SLOT), BF16), jax.ShapeDtypeStruct((S, HEADS * V_HEAD), BF16),
                   jax.ShapeDtypeStruct((S, Q_LORA), BF16), jax.ShapeDtypeStruct((S, KV_LORA), BF16)],
        compiler_params=_params(("parallel",)),
    )(z, z, conv_w, gq, gkv, wuq, wukv, cs, sn)


def _att_blocks(S):
    bk = min(512, max(S // 4, 128))
    return 2 * bk, bk


def _pair_tables(S, k_major):
    bq, bk = _att_blocks(S)
    nq, nk = S // bq, S // bk
    vis = lambda qi, ki: ki * bk < (qi + 1) * bq
    if k_major:
        pairs = [(qi, ki) for ki in range(nk) for qi in range(nq) if vis(qi, ki)]
    else:
        pairs = [(qi, ki) for qi in range(nq) for ki in range(nk) if vis(qi, ki)]
    cols = [[p[0] for p in pairs], [p[1] for p in pairs], [int((p[1] + 1) * bk > p[0] * bq) for p in pairs]]
    return [jnp.asarray(np.array(c, np.int32)) for c in cols], len(pairs)


def _chunk_mask(qi, ki, bq, bk):
    r = (qi * bq + lax.broadcasted_iota(jnp.int32, (bq, bk), 0)) // CHUNK
    c = (ki * bk + lax.broadcasted_iota(jnp.int32, (bq, bk), 1)) // CHUNK
    return c <= r


def _attention(q, k, v):
    S = q.shape[0]
    bq, bk = _att_blocks(S)
    last_k = bq // bk - 1
    tables, n_pairs = _pair_tables(S, k_major=False)

    def body(qi_ref, ki_ref, mk_ref, q_ref, k_ref, v_ref, o_ref, lse_ref, m_s, l_s, acc_s):
        p_id = pl.program_id(1)
        qi, ki = qi_ref[p_id], ki_ref[p_id]

        @pl.when(ki == 0)
        def _():
            m_s[...] = jnp.full_like(m_s, NEG)
            l_s[...] = jnp.zeros_like(l_s)
            acc_s[...] = jnp.zeros_like(acc_s)

        def update(masked):
            s = lax.dot_general(q_ref[...], k_ref[...], (((1,), (1,)), ((), ())), preferred_element_type=F32)
            if masked:
                s = jnp.where(_chunk_mask(qi, ki, bq, bk), s, NEG)
            m_prev = m_s[...]
            m_new = jnp.maximum(m_prev, jnp.max(s, axis=1, keepdims=True))
            alpha = jnp.exp2((m_prev - m_new) * EXP2_SCALE)
            p = jnp.exp2((s - jnp.tile(m_new, (1, bk // 128))) * EXP2_SCALE)
            l_s[...] = alpha * l_s[...] + jnp.sum(p, axis=1, keepdims=True)
            acc_s[...] = alpha * acc_s[...] + jnp.dot(p.astype(BF16), v_ref[...], preferred_element_type=F32)
            m_s[...] = m_new

        @pl.when(mk_ref[p_id] == 0)
        def _():
            update(False)

        @pl.when(mk_ref[p_id] == 1)
        def _():
            update(True)

        @pl.when(ki == qi * (last_k + 1) + last_k)
        def _():
            l = l_s[...]
            o_ref[...] = acc_s[...] / l
            lse_ref[...] = m_s[...] * EXP2_SCALE + jnp.log2(l)

    grid_spec = pltpu.PrefetchScalarGridSpec(
        num_scalar_prefetch=3, grid=(HEADS, n_pairs),
        in_specs=[pl.BlockSpec((bq, HEAD_SLOT), lambda h, p, qt, kt, mt: (qt[p], h)),
                  pl.BlockSpec((bk, HEAD_SLOT), lambda h, p, qt, kt, mt: (kt[p], h)),
                  pl.BlockSpec((bk, V_HEAD), lambda h, p, qt, kt, mt: (kt[p], h))],
        out_specs=[pl.BlockSpec((bq, V_HEAD), lambda h, p, qt, kt, mt: (qt[p], h))] * 2,
        scratch_shapes=[pltpu.VMEM((bq, V_HEAD), F32)] * 3)
    return pl.pallas_call(
        body, name="attention", grid_spec=grid_spec,
        out_shape=[jax.ShapeDtypeStruct((S, HEADS * V_HEAD), F32)] * 2,
        compiler_params=_params(("arbitrary", "arbitrary")),
    )(*tables, q, k, v)


def _attention_bwd(q, k, v, do, lse2, delta):
    S = q.shape[0]
    bq, bk = _att_blocks(S)
    nq = S // bq
    tables, n_pairs = _pair_tables(S, k_major=True)

    def body(qi_ref, ki_ref, mk_ref, q_ref, k_ref, v_ref, do_ref, lse_ref, dl_ref, dq_hbm, dk_ref, dv_ref,
             dq_s, dk_s, dv_s, sem):
        head, p_id = pl.program_id(0), pl.program_id(1)
        qi, ki = qi_ref[p_id], ki_ref[p_id]
        rows = pl.ds(pl.multiple_of(qi * bq, bq), bq)

        @pl.when(qi * bq <= ki * bk)
        def _():
            dk_s[...] = jnp.zeros_like(dk_s)
            dv_s[...] = jnp.zeros_like(dv_s)

        def update(masked):
            qv, kv, dov = q_ref[...], k_ref[...], do_ref[...]
            s = lax.dot_general(qv, kv, (((1,), (1,)), ((), ())), preferred_element_type=F32)
            dp = lax.dot_general(dov, v_ref[...], (((1,), (1,)), ((), ())), preferred_element_type=F32)
            if masked:
                s = jnp.where(_chunk_mask(qi, ki, bq, bk), s, NEG)
            p = jnp.exp2(s * EXP2_SCALE - jnp.tile(lse_ref[...], (1, bk // 128)))
            dv_s[...] += lax.dot_general(p.astype(BF16), dov, (((0,), (0,)), ((), ())), preferred_element_type=F32)
            ds = (p * (dp - jnp.tile(dl_ref[...], (1, bk // 128)))).astype(BF16)
            dk_s[...] += lax.dot_general(ds, qv, (((0,), (0,)), ((), ())), preferred_element_type=F32)
            dq = jnp.dot(ds, kv, preferred_element_type=F32)

            @pl.when(ki == 0)
            def _():
                dq_s[rows, :] = dq

            @pl.when(ki > 0)
            def _():
                dq_s[rows, :] += dq

        @pl.when(mk_ref[p_id] == 0)
        def _():
            update(False)

        @pl.when(mk_ref[p_id] == 1)
        def _():
            update(True)

        @pl.when(qi == nq - 1)
        def _():
            dk_ref[...] = dk_s[...] * ATT_SCALE
            dv_ref[...] = dv_s[...]

        @pl.when(p_id == n_pairs - 1)
        def _():
            dq_s[...] = dq_s[...] * ATT_SCALE
            out = pltpu.make_async_copy(
                dq_s, dq_hbm.at[:, pl.ds(pl.multiple_of(head * HEAD_SLOT, HEAD_SLOT), HEAD_SLOT)], sem)
            out.start()
            out.wait()

    grid_spec = pltpu.PrefetchScalarGridSpec(
        num_scalar_prefetch=3, grid=(HEADS, n_pairs),
        in_specs=[pl.BlockSpec((bq, HEAD_SLOT), lambda h, p, qt, kt, mt: (qt[p], h)),
                  pl.BlockSpec((bk, HEAD_SLOT), lambda h, p, qt, kt, mt: (kt[p], h)),
                  pl.BlockSpec((bk, V_HEAD), lambda h, p, qt, kt, mt: (kt[p], h)),
                  pl.BlockSpec((bq, V_HEAD), lambda h, p, qt, kt, mt: (qt[p], h)),
                  pl.BlockSpec((bq, V_HEAD), lambda h, p, qt, kt, mt: (qt[p], h)),
                  pl.BlockSpec((bq, V_HEAD), lambda h, p, qt, kt, mt: (qt[p], h))],
        out_specs=[pl.BlockSpec(memory_space=pl.ANY),
                   pl.BlockSpec((bk, HEAD_SLOT), lambda h, p, qt, kt, mt: (kt[p], h)),
                   pl.BlockSpec((bk, V_HEAD), lambda h, p, qt, kt, mt: (kt[p], h))],
        scratch_shapes=[pltpu.VMEM((S, HEAD_SLOT), F32), pltpu.VMEM((bk, HEAD_SLOT), F32),
                        pltpu.VMEM((bk, V_HEAD), F32), pltpu.SemaphoreType.DMA])
    return pl.pallas_call(
        body, name="attention_bwd", grid_spec=grid_spec,
        out_shape=[jax.ShapeDtypeStruct((S, HEADS * HEAD_SLOT), F32), jax.ShapeDtypeStruct((S, HEADS * HEAD_SLOT), F32),
                   jax.ShapeDtypeStruct((S, HEADS * V_HEAD), F32)],
        compiler_params=_params(("arbitrary", "arbitrary")),
    )(*tables, q, k, v, do, lse2, delta)


def _group_mats():
    def blockdiag(n, g):
        idx = np.arange(n) // g
        return jnp.asarray((idx[:, None] == idx[None, :]).astype(np.float32), dtype=BF16)
    return blockdiag(CONV_W, CONV_GROUP), blockdiag(HEADS * V_HEAD, V_HEAD)


def _mix_out(ya, o, gout, w_out, x, gate, ga, gb):
    S = x.shape[0]
    tm = _row_tile(S, 512)

    def body(ya_ref, o_ref, go_ref, w_ref, x_ref, g_ref, ga_ref, gb_ref, xo_ref, yn_ref, yo_ref):
        yav, ov = ya_ref[...], o_ref[...]
        ra = lax.rsqrt(_gsum(yav * yav, ga_ref[...]) * (1.0 / CONV_GROUP) + EPS)
        rb = lax.rsqrt(_gsum(ov * ov, gb_ref[...]) * (1.0 / V_HEAD) + EPS)
        na = ((yav * ra) * go_ref[:, 0:CONV_W]).astype(BF16)
        nb = ((ov * rb) * go_ref[:, CONV_W:]).astype(BF16)
        yn_ref[:, 0:CONV_W] = na
        yn_ref[:, CONV_W:] = nb
        yo = (jnp.dot(na, w_ref[0:CONV_W, :], preferred_element_type=F32)
              + jnp.dot(nb, w_ref[CONV_W:, :], preferred_element_type=F32))
        xo_ref[...] = x_ref[...] + g_ref[...] * yo
        yo_ref[...] = yo.astype(BF16)

    row = pl.BlockSpec((tm, D), lambda i: (i, 0))
    half = pl.BlockSpec((tm, CONV_W), lambda i: (i, 0))
    vec = pl.BlockSpec((1, D), lambda i: (0, 0))
    sq = pl.BlockSpec((CONV_W, CONV_W), lambda i: (0, 0))
    return pl.pallas_call(
        body, name="mix_out", grid=(S // tm,),
        in_specs=[half, half, vec, pl.BlockSpec((D, D), lambda i: (0, 0)), row, vec, sq, sq],
        out_specs=[row, row, row],
        out_shape=[jax.ShapeDtypeStruct((S, D), F32), jax.ShapeDtypeStruct((S, D), BF16),
                   jax.ShapeDtypeStruct((S, D), BF16)],
        compiler_params=_params(("parallel",)),
    )(ya, o, gout, w_out, x, gate, ga, gb)


def _mix_out_bwd(dx, gate, yo, w_out, ya, o, gout, ga, gb, comm=None):
    S = dx.shape[0]
    tm = _row_tile(S, 256)
    n_i = S // tm

    def norm_bwd(v, dn, gain, gmat, inv_n):
        r = lax.rsqrt(_gsum(v * v, gmat) * inv_n + EPS)
        vh = v * r
        dy = dn * gain
        return r * (dy - vh * (_gsum(dy * vh, gmat) * inv_n)), dn * vh

    def body(dx_ref, g_ref, yo_ref, w_ref, ya_ref, o_ref, go_ref, ga_ref, gb_ref,
             dyo_ref, dya_ref, do_ref, dl_ref, p_ref):
        i = pl.program_id(0)
        dxv = dx_ref[...]
        dyo = (dxv * g_ref[...]).astype(BF16)
        dyo_ref[...] = dyo
        dyn = _dot_nt(dyo, w_ref[...])
        dya, dga = norm_bwd(ya_ref[...], dyn[:, 0:CONV_W], go_ref[:, 0:CONV_W], ga_ref[...], 1.0 / CONV_GROUP)
        ov = o_ref[...]
        do, dgb = norm_bwd(ov, dyn[:, CONV_W:], go_ref[:, CONV_W:], gb_ref[...], 1.0 / V_HEAD)
        dya_ref[...] = dya
        do_ref[...] = do.astype(BF16)
        dl_ref[...] = _gsum(do * ov, gb_ref[...])

        @pl.when(i == 0)
        def _():
            p_ref[...] = jnp.zeros_like(p_ref)

        p_ref[:, 0:D] += _rsum8(dxv * yo_ref[...].astype(F32))
        p_ref[:, D:D + CONV_W] += _rsum8(dga)
        p_ref[:, D + CONV_W:2 * D] += _rsum8(dgb)

        @pl.when(i == n_i - 1)
        def _():
            _all_rows(p_ref)

    row = pl.BlockSpec((tm, D), lambda i: (i, 0))
    half = pl.BlockSpec((tm, CONV_W), lambda i: (i, 0))
    vec = pl.BlockSpec((1, D), lambda i: (0, 0))
    sq = pl.BlockSpec((CONV_W, CONV_W), lambda i: (0, 0))
    return _hosted_call(
        body, name="mix_out_bwd", grid=(n_i,),
        in_specs=[row, vec, row, pl.BlockSpec((D, D), lambda i: (0, 0)), half, half, vec, sq, sq],
        out_specs=[row, half, half, half, pl.BlockSpec((8, 2 * D), lambda i: (0, 0))],
        out_shape=[jax.ShapeDtypeStruct((S, D), BF16), jax.ShapeDtypeStruct((S, CONV_W), F32),
                   jax.ShapeDtypeStruct((S, CONV_W), BF16), jax.ShapeDtypeStruct((S, CONV_W), F32),
                   jax.ShapeDtypeStruct((8, 2 * D), F32)],
        scratch_shapes=[], semantics=("arbitrary",), args=(dx, gate, yo, w_out, ya, o, gout, ga, gb), comm=comm)


MID_SUMS = 3 * CONV_W + Q_LORA + KV_LORA


def _mix_mid_bwd(z, dya, conv_w, gq, gkv, wuq, wukv, cs, sn, dq, dk, dv, comm=None):
    S = z.shape[0]
    tm = _row_tile(S, 256)
    n_i = S // tm
    hb = tm // 8
    last_blk = S // 8 - 1

    def latent_bwd(cv, dcn, gain):
        r = _rms(cv)
        ch = cv * r
        dy = dcn * gain
        return r * (dy - ch * jnp.mean(dy * ch, axis=-1, keepdims=True)), dcn * ch

    def body(z_ref, zp_ref, zn_ref, dya_ref, dyan_ref, cw_ref, gq_ref, gkv_ref, wuq_ref, wukv_ref, cs_ref, sn_ref,
             dq_ref, dk_ref, dv_ref, dz_ref, dqf_ref, dkvf_ref, p_ref):
        i = pl.program_id(0)
        xb, xc, xu = z_ref[:, 0:CONV_W], z_ref[:, CONV_W:2 * CONV_W], z_ref[:, 2 * CONV_W:3 * CONV_W]
        u = xc * xu
        halo = jnp.where(i > 0, zp_ref[:, CONV_W:2 * CONV_W] * zp_ref[:, 2 * CONV_W:3 * CONV_W], 0.0)
        rows = lax.broadcasted_iota(jnp.int32, (tm, CONV_W), 0)
        u1, u2 = _conv_taps(u, halo, rows)
        w0, w1, w2 = cw_ref[0:1, :], cw_ref[1:2, :], cw_ref[2:3, :]
        y = w0 * u2 + w1 * u1 + w2 * u
        dyav = dya_ref[...]
        dy = dyav * xb
        nxt = jnp.where(i < n_i - 1, dyan_ref[...] * zn_ref[:, 0:CONV_W], 0.0)
        dy1 = jnp.where(rows == tm - 1, nxt[0:1, :], pltpu.roll(dy, tm - 1, 0))
        dy2 = jnp.where(rows == tm - 1, nxt[1:2, :], jnp.where(rows == tm - 2, nxt[0:1, :], pltpu.roll(dy, tm - 2, 0)))
        du = w2 * dy + w1 * dy1 + w0 * dy2
        dz_ref[:, 0:CONV_W] = (dyav * y).astype(BF16)
        dz_ref[:, CONV_W:2 * CONV_W] = (du * xu).astype(BF16)
        dz_ref[:, 2 * CONV_W:3 * CONV_W] = (du * xc).astype(BF16)

        lane = lax.broadcasted_iota(jnp.int32, (tm, 128), 1)
        cs_v, sn_v = cs_ref[...], sn_ref[...]
        dkr = jnp.zeros((tm, 128), F32)
        for h in range(HEADS):
            o = h * HEAD_SLOT
            dqf_ref[:, o:o + 128] = dq_ref[:, o:o + 128].astype(BF16)
            dqf_ref[:, o + 128:o + 256] = _rope_t(dq_ref[:, o + 128:o + 256], cs_v, sn_v, lane).astype(BF16)
            dkvf_ref[:, h * 128:(h + 1) * 128] = dk_ref[:, o:o + 128].astype(BF16)
            dkr = dkr + dk_ref[:, o + 128:o + 256]
        dkvf_ref[:, HEADS * 128:] = dv_ref[...].astype(BF16)

        c0 = 3 * CONV_W
        dcqn = jnp.dot(dqf_ref[...], wuq_ref[...], preferred_element_type=F32)
        dcq, dgq = latent_bwd(z_ref[:, c0:c0 + Q_LORA], dcqn, gq_ref[...])
        dz_ref[:, c0:c0 + Q_LORA] = dcq.astype(BF16)
        c1 = c0 + Q_LORA
        dckvn = _dot_nt(dkvf_ref[...], wukv_ref[...])
        dckv, dgkv = latent_bwd(z_ref[:, c1:c1 + KV_LORA], dckvn, gkv_ref[...])
        dz_ref[:, c1:c1 + KV_LORA] = dckv.astype(BF16)
        dz_ref[:, c1 + KV_LORA:Z_COLS] = _rope_t(dkr, cs_v, sn_v, lane).astype(BF16)

        @pl.when(i == 0)
        def _():
            p_ref[...] = jnp.zeros_like(p_ref)

        p_ref[:, 0:CONV_W] += _rsum8(dy * u2)
        p_ref[:, CONV_W:2 * CONV_W] += _rsum8(dy * u1)
        p_ref[:, 2 * CONV_W:3 * CONV_W] += _rsum8(dy * u)
        p_ref[:, c0:c0 + Q_LORA] += _rsum8(dgq)
        p_ref[:, c1:c1 + KV_LORA] += _rsum8(dgkv)

        @pl.when(i == n_i - 1)
        def _():
            _all_rows(p_ref)

    def rows_of(n):
        return pl.BlockSpec((tm, n), lambda i: (i, 0))

    def whole(shape):
        return pl.BlockSpec(shape, lambda i: (0, 0))

    def prev8(n):
        return pl.BlockSpec((8, n), lambda i: (jnp.maximum(i * hb - 1, 0), 0))

    def next8(n):
        return pl.BlockSpec((8, n), lambda i: (jnp.minimum((i + 1) * hb, last_blk), 0))

    return _hosted_call(
        body, name="mix_mid_bwd", grid=(n_i,),
        in_specs=[rows_of(Z_COLS), prev8(Z_COLS), next8(Z_COLS), rows_of(CONV_W), next8(CONV_W),
                  whole((8, CONV_W)), whole((1, Q_LORA)), whole((1, KV_LORA)),
                  whole((HEADS * HEAD_SLOT, Q_LORA)), whole((KV_LORA, 2 * HEADS * 128)),
                  rows_of(128), rows_of(128),
                  rows_of(HEADS * HEAD_SLOT), rows_of(HEADS * HEAD_SLOT), rows_of(HEADS * V_HEAD)],
        out_specs=[rows_of(Z_COLS), rows_of(HEADS * HEAD_SLOT), rows_of(2 * HEADS * 128), whole((8, MID_SUMS))],
        out_shape=[jax.ShapeDtypeStruct((S, Z_COLS), BF16), jax.ShapeDtypeStruct((S, HEADS * HEAD_SLOT), BF16),
                   jax.ShapeDtypeStruct((S, 2 * HEADS * 128), BF16), jax.ShapeDtypeStruct((8, MID_SUMS), F32)],
        scratch_shapes=[], semantics=("arbitrary",),
        args=(z, z, z, dya, dya, conv_w, gq, gkv, wuq, wukv, cs, sn, dq, dk, dv), comm=comm)


ADA_Q = N_MOD * D // N_CHIPS
ADA_TN = 768


def _ada_forward(c_all, ada_w_q, ada_b_q):
    def body(c_ref, w_ref, b_ref, o_ref):
        cv = c_ref[...]
        sc = (cv * jax.nn.sigmoid(cv)).astype(BF16)
        o_ref[...] = jnp.dot(sc, w_ref[...].astype(BF16), preferred_element_type=F32) + b_ref[...]

    return pl.pallas_call(
        body, name="ada_forward", grid=(ADA_Q // ADA_TN,),
        in_specs=[pl.BlockSpec((16, D), lambda j: (0, 0)), pl.BlockSpec((D, ADA_TN), lambda j: (0, j)),
                  pl.BlockSpec((1, ADA_TN), lambda j: (0, j))],
        out_specs=pl.BlockSpec((16, ADA_TN), lambda j: (0, j)),
        out_shape=jax.ShapeDtypeStruct((16, ADA_Q), F32),
        compiler_params=_params(("parallel",)),
    )(c_all, ada_w_q, ada_b_q)


def _ada_wgrad(c_all, dmod_q):
    def body(c_ref, d_ref, o_ref):
        cv = c_ref[...]
        sc = (cv * jax.nn.sigmoid(cv)).astype(BF16)
        o_ref[...] = lax.dot_general(sc, d_ref[...].astype(BF16), (((0,), (0,)), ((), ())),
                                     preferred_element_type=F32)

    return pl.pallas_call(
        body, name="ada_wgrad", grid=(ADA_Q // ADA_TN,),
        in_specs=[pl.BlockSpec((16, D), lambda j: (0, 0)), pl.BlockSpec((16, ADA_TN), lambda j: (0, j))],
        out_specs=pl.BlockSpec((D, ADA_TN), lambda j: (0, j)),
        out_shape=jax.ShapeDtypeStruct((D, ADA_Q), F32),
        compiler_params=_params(("parallel",)),
    )(c_all, dmod_q)


def _sum_devices(parts):
    n = parts.shape[1]

    def body(p_ref, o_ref):
        o_ref[...] = jnp.broadcast_to(jnp.sum(p_ref[...], axis=0, keepdims=True), o_ref.shape)

    return pl.pallas_call(
        body, name="sum_devices",
        in_specs=[pl.BlockSpec((N_DEV, n), lambda: (0, 0))], out_specs=pl.BlockSpec((N_DEV, n), lambda: (0, 0)),
        out_shape=jax.ShapeDtypeStruct((N_DEV, n), F32),
    )(parts)


def _adamw(w, g, m, v, *, name):
    _, rows, cols = w.shape
    tr = _row_tile(rows, 256)

    def body(w_ref, g_ref, m_ref, v_ref, go_ref, d_ref, mo_ref, vo_ref):
        gv = g_ref[...]
        mn = B1 * m_ref[0] + (1.0 - B1) * gv
        vn = B2 * v_ref[0] + (1.0 - B2) * (gv * gv)
        m_hat = mn / (1.0 - B1 ** STEP)
        v_hat = vn / (1.0 - B2 ** STEP)
        go_ref[0] = gv
        d_ref[0] = -LR * (m_hat / (jnp.sqrt(v_hat) + AEPS) + WD * w_ref[0])
        mo_ref[0] = mn
        vo_ref[0] = vn

    blk = pl.BlockSpec((1, tr, cols), lambda i: (0, i, 0))
    return pl.pallas_call(
        body, name=name, grid=(rows // tr,),
        in_specs=[blk, pl.BlockSpec((tr, cols), lambda i: (i, 0)), blk, blk], out_specs=[blk] * 4,
        out_shape=[jax.ShapeDtypeStruct((1, rows, cols), F32)] * 4,
        compiler_params=_params(("parallel",)),
    )(w, g, m, v)


def _small_allgather(v, *, name):
    m, n = v.shape

    def body(x_ref, out_ref, send_sems, recv_sems, local_sem):
        x, y, c = _place()
        me, sibling = (x, y, c), (x, y, 1 - c)
        chips = [(1 - x, y), (x, 1 - y), (1 - x, 1 - y)]

        def rows(px, py, pc):
            return out_ref.at[pl.ds((4 * px + 2 * py + pc) * m, m), :]

        def copy(k, block, to, src=None):
            return pltpu.make_async_remote_copy(
                src_ref=rows(*block) if src is None else src, dst_ref=rows(*block),
                send_sem=send_sems.at[k], recv_sem=recv_sems.at[k], device_id=to, device_id_type=MESH)

        mine = pltpu.make_async_copy(x_ref, rows(*me), local_sem)
        mine.start()
        first = [copy(0, me, sibling, src=x_ref)]
        first += [copy(1 + j, me, (*chip, c), src=x_ref) for j, chip in enumerate(chips)]
        for cp in first:
            cp.start()
        passed = [copy(4 + j, (*chip, c), sibling) for j, chip in enumerate(chips)]
        for j, chip in enumerate(chips):
            copy(1 + j, (*chip, c), me).wait_recv()
            passed[j].start()
        copy(0, sibling, me).wait_recv()
        for j, chip in enumerate(chips):
            copy(4 + j, (*chip, 1 - c), me).wait_recv()
        for cp in first + passed:
            cp.wait_send()
        mine.wait()

    return pl.pallas_call(
        body, name=name,
        out_shape=jax.ShapeDtypeStruct((N_DEV * m, n), v.dtype),
        in_specs=[pl.BlockSpec(memory_space=pltpu.VMEM)], out_specs=pl.BlockSpec(memory_space=pltpu.VMEM),
        scratch_shapes=[pltpu.SemaphoreType.DMA((7,)), pltpu.SemaphoreType.DMA((7,)), pltpu.SemaphoreType.DMA],
    )(v)


ADD_BLOCKS = 2


def _pair_add(place, gs, ts, *, name):
    n_a = len(gs)

    def body(pl_ref, *refs):
        g_refs, t_refs = refs[:n_a], refs[n_a:2 * n_a]
        pf_refs, pb_refs = refs[2 * n_a:3 * n_a], refs[3 * n_a:]
        for g_ref, t_ref, pf_ref, pb_ref in zip(g_refs, t_refs, pf_refs, pb_refs):
            s = g_ref[...] + t_ref[...]
            pf_ref[...] = s
            pb_ref[...] = s.astype(BF16)

    def blk(t, own_half):
        tr = t.shape[1] // ADD_BLOCKS
        if own_half:
            return pl.BlockSpec((1, tr, t.shape[2]), lambda q, r, p: (q, p[0] * ADD_BLOCKS + r, 0))
        return pl.BlockSpec((1, tr, t.shape[2]), lambda q, r, p: (q, r, 0))

    grid_spec = pltpu.PrefetchScalarGridSpec(
        num_scalar_prefetch=1, grid=(N_CHIPS, ADD_BLOCKS),
        in_specs=[blk(t, True) for t in ts] + [blk(t, False) for t in ts],
        out_specs=[blk(t, False) for t in ts] * 2)
    res = pl.pallas_call(
        body, name=name, grid_spec=grid_spec,
        out_shape=[jax.ShapeDtypeStruct(t.shape, F32) for t in ts] + [jax.ShapeDtypeStruct(t.shape, BF16) for t in ts],
        compiler_params=_params(("parallel", "parallel")),
    )(place, *gs, *ts)
    return list(res[:n_a]), list(res[n_a:])


def _chip_add(place, pfs, ts, *, name):
    n_a = len(pfs)

    def body(pl_ref, *refs):
        pf_refs, t_refs, o_refs = refs[:n_a], refs[n_a:4 * n_a], refs[4 * n_a:]
        for i, (pf_ref, o_ref) in enumerate(zip(pf_refs, o_refs)):
            t1, t2, t3 = t_refs[3 * i:3 * i + 3]
            o_ref[...] = ((pf_ref[0] + t1[0].astype(F32)) + t2[0].astype(F32)) + t3[0].astype(F32)

    def slot(t, j):
        return pl.BlockSpec((1, t.shape[1] // ADD_BLOCKS, t.shape[2]), lambda r, p: (p[1] ^ j, r, 0))

    grid_spec = pltpu.PrefetchScalarGridSpec(
        num_scalar_prefetch=1, grid=(ADD_BLOCKS,),
        in_specs=[slot(t, 0) for t in pfs] + [slot(t, j) for t in ts for j in (1, 2, 3)],
        out_specs=[pl.BlockSpec((t.shape[1] // ADD_BLOCKS, t.shape[2]), lambda r, p: (r, 0)) for t in pfs])
    res = pl.pallas_call(
        body, name=name, grid_spec=grid_spec,
        out_shape=[jax.ShapeDtypeStruct(t.shape[1:], F32) for t in pfs],
        compiler_params=_params(("parallel",)),
    )(place, *pfs, *[t for t in ts for _ in range(3)])
    return list(res)


BULK = [("ffn1_w1", "colsT"), ("ffn1_w3", "colsT"), ("ffn1_w2", "rows"), ("w_in", "cols"), ("w_uq", "colsT"),
        ("w_ukv", "cols"), ("w_out", "rows"), ("ffn2_w1", "colsT"), ("ffn2_w3", "colsT"), ("ffn2_w2", "rows")]
KIND = dict(BULK)


def _group(*names):
    return [b for b in BULK if b[0] in names]


W_FIRST = _group("ffn1_w1", "ffn1_w3")
W_REST = [b for b in BULK if b not in W_FIRST]
G_FFN2 = _group("ffn2_w1", "ffn2_w3", "ffn2_w2")
G_MIX = _group("w_in", "w_uq", "w_ukv", "w_out")
G_FFN1 = _group("ffn1_w1", "ffn1_w3", "ffn1_w2")


def _gathered_weights(specs, shards, got, myq):
    out = {}
    for (name, kind), part in zip(specs, got):
        part = lax.dynamic_update_slice_in_dim(part, shards[name][None], myq, axis=0)
        out[name] = _full_weight(part, kind)
    return out


def _working_shard(w, kind):
    return jnp.swapaxes(w, 1, 2)[0] if kind == "colsT" else w[0]


def _full_weight(parts, kind):
    if kind == "cols":
        return jnp.transpose(parts, (1, 0, 2)).reshape(parts.shape[1], -1)
    return parts.reshape(-1, parts.shape[2])


def _quarters(g, kind):
    if kind == "cols":
        k, n = g.shape
        return jnp.transpose(g.reshape(k, N_CHIPS, n // N_CHIPS), (1, 0, 2))
    return g.reshape(N_CHIPS, g.shape[0] // N_CHIPS, g.shape[1])


def _pad_heads(w_uq_t):
    w = w_uq_t.reshape(HEADS, QK_NOPE + QK_ROPE, Q_LORA)
    return jnp.pad(w, ((0, 0), (0, HEAD_SLOT - QK_NOPE - QK_ROPE), (0, 0))).reshape(HEADS * HEAD_SLOT, Q_LORA)


def _unpad_heads(g):
    return g.reshape(HEADS, HEAD_SLOT, Q_LORA)[:, :QK_NOPE + QK_ROPE].reshape(HEADS * (QK_NOPE + QK_ROPE), Q_LORA)


def _split_kv(w_ukv):
    return jnp.transpose(w_ukv.reshape(KV_LORA, HEADS, 2, 128), (0, 2, 1, 3)).reshape(KV_LORA, 2 * HEADS * 128)


def _merge_kv(g):
    return jnp.transpose(g.reshape(KV_LORA, 2, HEADS, 128), (0, 2, 1, 3)).reshape(KV_LORA, 2 * HEADS * 128)


def _rope_tables(positions):
    inv_freq = ROPE_THETA ** (-jnp.arange(0, QK_ROPE, 2, dtype=F32) / QK_ROPE)
    ang = positions.astype(F32)[:, None] * inv_freq
    cos, sin, zero = jnp.cos(ang), jnp.sin(ang), jnp.zeros((positions.shape[0], 64), F32)
    return jnp.concatenate([cos, cos, zero], axis=1), jnp.concatenate([sin, sin, zero], axis=1)


def _reduce_tail(place, specs, pfs, t2s, tag, host=None):
    rhs = _chip_add(place, pfs, t2s, name=tag + "_chip_add")
    if host is None:
        out, others = None, _run_comm(_PairShare(rhs), name=tag + "_pair_share")
    else:
        out, others = host(_PairShare(rhs))
    south = place[0] == 0
    return out, {b[0]: jnp.concatenate([jnp.where(south, rh, ot), jnp.where(south, ot, rh)], axis=0)
                 for b, rh, ot in zip(specs, rhs, others)}


def _local_step(x, positions, target, mod, vec, conv_w, w_first, rest_shards, place):
    row = lambda k: mod[k:k + 1]
    sh1, sc1, g1, sh2, sc2, g2, sh3, sc3, g3 = [row(k) for k in range(N_MOD)]
    cs, sn = _rope_tables(positions)
    cw8 = jnp.pad(conv_w, ((0, 5), (0, 0)))
    ga, gb = _group_mats()
    dist = place is not None
    comm = lambda prog: prog if dist else None

    gather = _Gather([rest_shards[b[0]] for b in W_REST]) if dist else None
    (h1, a1, b1, u1), got = _ffn_up(x, vec["norm_ffn1_g"], sh1, sc1, w_first["ffn1_w1"], w_first["ffn1_w3"],
                                    name="ffn1_up", comm=gather)
    if dist:
        w = _gathered_weights(W_REST, rest_shards, got, place[1])
    else:
        w = dict(rest_shards)
    w.update(w_first)
    w_in = jnp.pad(w["w_in"].T, ((0, Z_COLS - IN_COLS), (0, 0)))
    wuq = _pad_heads(w["w_uq"])
    wukv = _split_kv(w["w_ukv"])
    x1, f1 = _ffn_down(u1, w["ffn1_w2"], x, g1, name="ffn1_down")
    h2, z = _mix_in(x1, vec["norm_mix_g"], sh2, sc2, w_in)
    ya, q, k, v, cqn, ckvn = _mix_mid(z, cw8, vec["q_norm_g"], vec["kv_norm_g"], wuq, wukv, cs, sn)
    o, lse = _attention(q, k, v)
    x2, yn, yo = _mix_out(ya, o, vec["out_norm_g"], w["w_out"], x1, g2, ga, gb)
    (h3, a3, b3, u3), _ = _ffn_up(x2, vec["norm_ffn2_g"], sh3, sc3, w["ffn2_w1"], w["ffn2_w3"], name="ffn2_up")
    x3, f3 = _ffn_down(u3, w["ffn2_w2"], x2, g3, name="ffn2_down")
    dx3, dgfin, loss_blk = _final_loss(x3, vec["final_norm_g"], target)

    grads, reduced = {}, {}

    def tn(a, b, tm, tn_, name, prog=None):
        if prog is None:
            return _tn_matmul(a, b, tm=tm, tn=tn_, name=name), None
        return _tn_matmul(a, b, tm=tm, tn=tn_, name=name, comm=prog)

    def slab_of(specs):
        return [_quarters(grads[n], kind) for n, kind in specs]

    (df3, da3, db3, dg3), _ = _ffn_bwd_du(dx3, g3, f3, w["ffn2_w2"], a3, b3, name="ffn2_bwd_du")
    grads["ffn2_w2"], _ = tn(u3, df3, FF // 2, D, "ffn2_dw2")
    grads["ffn2_w1"], _ = tn(da3, h3, FF // 2, D, "ffn2_dw1")
    grads["ffn2_w3"], _ = tn(db3, h3, FF // 2, D, "ffn2_dw3")
    (dx2, s3), _ = _dh_normbwd([(da3, w["ffn2_w1"]), (db3, w["ffn2_w3"])], x2, vec["norm_ffn2_g"], sc3, dx3,
                               name="ffn2_bwd_dh")

    p1 = slab_of(G_FFN2) if dist else None
    (dyo, dya, do, delta, s_out), t1 = _mix_out_bwd(dx2, g2, yo, w["w_out"], ya, o, vec["out_norm_g"], ga, gb,
                                                    comm=comm(_PairExchange(p1) if dist else None))
    grads["w_out"], _ = tn(yn, dyo, D, D, "dw_out")
    if dist:
        pf1, pb1 = _pair_add(place, p1, t1, name="ffn2g_pair_add")
    dq, dk, dv = _attention_bwd(q, k, v, do, lse, delta)
    (dz, dqf, dkvf, s_mid), t2 = _mix_mid_bwd(z, dya, cw8, vec["q_norm_g"], vec["kv_norm_g"], wuq, wukv, cs, sn,
                                              dq, dk, dv, comm=comm(_ChipExchange(pb1) if dist else None))
    g_uq, _ = tn(dqf, cqn, HEADS * HEAD_SLOT, Q_LORA, "dw_uq")
    g_ukv, _ = tn(ckvn, dkvf, KV_LORA, 2 * HEADS * 128, "dw_ukv")
    grads["w_uq"], grads["w_ukv"] = _unpad_heads(g_uq), _merge_kv(g_ukv)
    if dist:
        g_in, red = _reduce_tail(place, G_FFN2, pf1, t2, "ffn2g",
                                 host=lambda prog: tn(h2, dz, D, Z_COLS // 2, "dw_in", prog))
        reduced.update(red)
    else:
        g_in, _ = tn(h2, dz, D, Z_COLS // 2, "dw_in")
    grads["w_in"] = g_in[:, :IN_COLS]
    (dx1, s2), _ = _dh_normbwd([(dz, w_in)], x1, vec["norm_mix_g"], sc2, dx2, name="mix_bwd_dh")

    p2 = slab_of(G_MIX) if dist else None
    (df1, da1, db1, dg1), t1 = _ffn_bwd_du(dx1, g1, f1, w["ffn1_w2"], a1, b1, name="ffn1_bwd_du",
                                           comm=comm(_PairExchange(p2) if dist else None))
    if dist:
        pf2, pb2 = _pair_add(place, p2, t1, name="mixg_pair_add")
        grads["ffn1_w2"], t2 = tn(u1, df1, FF // 2, D, "ffn1_dw2", _ChipExchange(pb2))
        grads["ffn1_w1"], red = _reduce_tail(place, G_MIX, pf2, t2, "mixg",
                                             host=lambda prog: tn(da1, h1, FF // 2, D, "ffn1_dw1", prog))
        reduced.update(red)
    else:
        grads["ffn1_w2"], _ = tn(u1, df1, FF // 2, D, "ffn1_dw2")
        grads["ffn1_w1"], _ = tn(da1, h1, FF // 2, D, "ffn1_dw1")
    grads["ffn1_w3"], _ = tn(db1, h1, FF // 2, D, "ffn1_dw3")

    p3 = slab_of(G_FFN1) if dist else None
    (dx0, s1), t1 = _dh_normbwd([(da1, w["ffn1_w1"]), (db1, w["ffn1_w3"])], x, vec["norm_ffn1_g"], sc1, dx1,
                                name="ffn1_bwd_dh", comm=comm(_PairExchange(p3) if dist else None))
    if dist:
        pf3, pb3 = _pair_add(place, p3, t1, name="ffn1g_pair_add")
        t2 = _run_comm(_ChipExchange(pb3), name="ffn1g_chip_exchange")
        _, red = _reduce_tail(place, G_FFN1, pf3, t2, "ffn1g")
        reduced.update(red)
    else:
        reduced = grads

    def part(s, k):
        return s[0:1, k * D:(k + 1) * D]

    dmod = jnp.concatenate([part(s1, 1), part(s1, 0), dg1[0:1], part(s2, 1), part(s2, 0), part(s_out, 0),
                            part(s3, 1), part(s3, 0), dg3[0:1]], axis=1)
    small = {"norm_ffn1_g": part(s1, 2), "norm_mix_g": part(s2, 2), "out_norm_g": part(s_out, 1),
             "norm_ffn2_g": part(s3, 2), "final_norm_g": dgfin[0:1],
             "q_norm_g": s_mid[0:1, 3 * CONV_W:3 * CONV_W + Q_LORA],
             "kv_norm_g": s_mid[0:1, 3 * CONV_W + Q_LORA:MID_SUMS], "conv_w": s_mid[0:1, 0:3 * CONV_W]}
    return loss_blk, dx0, reduced, dmod, small


SMALL = [("norm_ffn1_g", D), ("norm_mix_g", D), ("out_norm_g", D), ("norm_ffn2_g", D), ("final_norm_g", D),
         ("q_norm_g", Q_LORA), ("kv_norm_g", KV_LORA), ("conv_w", 3 * CONV_W)]
WEIGHTS = ['ada_w', 'ada_b', 'norm_ffn1_g', 'ffn1_w1', 'ffn1_w3', 'ffn1_w2', 'norm_mix_g', 'w_in', 'conv_w',
           'q_norm_g', 'w_uq', 'kv_norm_g', 'w_ukv', 'out_norm_g', 'w_out', 'norm_ffn2_g', 'ffn2_w1', 'ffn2_w3',
           'ffn2_w2', 'final_norm_g']


def kernel(x, c, positions, ada_w, ada_b, norm_ffn1_g, ffn1_w1, ffn1_w3, ffn1_w2, norm_mix_g, w_in, conv_w, q_norm_g, w_uq, kv_norm_g, w_ukv, out_norm_g, w_out, norm_ffn2_g, ffn2_w1, ffn2_w3, ffn2_w2, final_norm_g, loss_target, m_ada_w, m_ada_b, m_norm_ffn1_g, m_ffn1_w1, m_ffn1_w3, m_ffn1_w2, m_norm_mix_g, m_w_in, m_conv_w, m_q_norm_g, m_w_uq, m_kv_norm_g, m_w_ukv, m_out_norm_g, m_w_out, m_norm_ffn2_g, m_ffn2_w1, m_ffn2_w3, m_ffn2_w2, m_final_norm_g, v_ada_w, v_ada_b, v_norm_ffn1_g, v_ffn1_w1, v_ffn1_w3, v_ffn1_w2, v_norm_mix_g, v_w_in, v_conv_w, v_q_norm_g, v_w_uq, v_kv_norm_g, v_w_ukv, v_out_norm_g, v_w_out, v_norm_ffn2_g, v_ffn2_w1, v_ffn2_w3, v_ffn2_w2, v_final_norm_g):
    args = dict(locals())
    wts = {n: args[n] for n in WEIGHTS}
    mom = {n: args["m_" + n] for n in WEIGHTS}
    var = {n: args["v_" + n] for n in WEIGHTS}
    ax, ay, ac = _place()
    myq = 2 * ax + ay
    me = 2 * myq + ac
    place = jnp.stack([ac, myq]).astype(jnp.int32)

    shards = {name: _working_shard(wts[name], kind).astype(BF16) for name, kind in BULK}
    first = _run_comm(_Gather([shards[b[0]] for b in W_FIRST]), name="gather_ffn1")
    w_first = _gathered_weights(W_FIRST, shards, first, myq)

    mine = jnp.concatenate([c, conv_w[0].reshape(1, 3 * CONV_W // N_CHIPS)], axis=1)
    seen = _small_allgather(jnp.pad(mine, ((0, 7), (0, 0))), name="gather_cond").reshape(N_DEV, 8, -1)[:, 0]
    c_all = jnp.pad(seen[:, :D], ((0, 8), (0, 0)))
    conv_full = jnp.transpose(seen[0::2, D:].reshape(N_CHIPS, 3, CONV_W // N_CHIPS), (1, 0, 2)).reshape(3, CONV_W)
    ada_b_q = lax.dynamic_slice_in_dim(ada_b, myq * ADA_Q, ADA_Q, axis=1)
    mod_q = _ada_forward(c_all, ada_w[0], ada_b_q)
    mod_all = _small_allgather(mod_q, name="gather_mod").reshape(N_DEV, 16, ADA_Q)
    mod_rows = jnp.transpose(mod_all[0::2, :N_DEV], (1, 0, 2)).reshape(N_DEV, N_MOD * D)
    mod = lax.dynamic_slice_in_dim(mod_rows, me, 1, axis=0).reshape(N_MOD, D)

    vec = {n: wts[n] for n in ("norm_ffn1_g", "norm_mix_g", "q_norm_g", "kv_norm_g", "out_norm_g", "norm_ffn2_g")}
    vec["final_norm_g"] = final_norm_g.reshape(1, D)
    loss_blk, grad_x, gq, dmod, small = _local_step(x[0], positions[0], loss_target[0], mod, vec, conv_full, w_first,
                                                    {b[0]: shards[b[0]] for b in W_REST}, place)
    loss = lax.psum(loss_blk[0, 0], ("x", "y", "c"))

    rows = jnp.concatenate([dmod] + [small[n] for n, _ in SMALL], axis=1)
    width = rows.shape[1]
    fold = -(-width // (8 * 128)) * 128
    rows = jnp.pad(rows, ((0, 0), (0, 8 * fold - width))).reshape(8, fold)
    every = _small_allgather(rows, name="gather_small").reshape(N_DEV, 8 * fold)[:, :width]
    total = _sum_devices(every)[0:1]
    dmod_q = lax.dynamic_slice_in_dim(every[:, :N_MOD * D], myq * ADA_Q, ADA_Q, axis=1)
    g = {name: gq[name] for name, *_ in BULK}
    g["ada_w"] = _ada_wgrad(c_all, jnp.pad(dmod_q, ((0, 8), (0, 0))))
    g["ada_b"] = total[:, :N_MOD * D]
    off = N_MOD * D
    for n, width in SMALL:
        g[n] = total[:, off:off + width]
        off += width
    g["conv_w"] = lax.dynamic_slice_in_dim(g["conv_w"].reshape(3, CONV_W), myq * (CONV_W // N_CHIPS),
                                           CONV_W // N_CHIPS, axis=1)
    g["final_norm_g"] = g["final_norm_g"].reshape(D)

    delta, new_m, new_v = {}, {}, {}
    for name in ["ada_w"] + [b[0] for b in BULK]:
        view = (lambda a: jnp.swapaxes(a, 1, 2)) if KIND.get(name) == "colsT" else (lambda a: a)
        g[name], delta[name], new_m[name], new_v[name] = [
            view(r) for r in _adamw(view(wts[name]), g[name], view(mom[name]), view(var[name]), name="adamw_" + name)]
    smalls = ["ada_b"] + [n for n, _ in SMALL]

    def packed(d):
        flat = jnp.concatenate([d[n].reshape(1, -1) for n in smalls], axis=1)
        return jnp.pad(flat.reshape(-1, D), ((0, 1), (0, 0)))

    res = _adamw(packed(wts)[None], packed(g), packed(mom)[None], packed(var)[None], name="adamw_small")[1:]
    off = 0
    for n in smalls:
        size = wts[n].size
        for d, r in zip((delta, new_m, new_v), res):
            d[n] = r.reshape(-1)[off:off + size].reshape(wts[n].shape)
        g[n] = g[n].reshape(wts[n].shape)
        off += size

    return (loss, grad_x[None], *[g[n] for n in WEIGHTS], *[delta[n] for n in WEIGHTS],
            *[new_m[n] for n in WEIGHTS], *[new_v[n] for n in WEIGHTS])
```

```python
import functools

import numpy as np
import jax
import jax.numpy as jnp
from jax import lax
from jax.experimental import pallas as pl
from jax.experimental.pallas import tpu as pltpu

F32 = jnp.float32
BF16 = jnp.bfloat16
MESH = pl.DeviceIdType.MESH

D = 1024
FF = 2816
CONV_W = 512
CONV_GROUP = 64
HEADS = 4
QK_NOPE = 128
QK_ROPE = 64
V_HEAD = 128
Q_LORA = 384
KV_LORA = 256
HEAD_SLOT = 256
IN_COLS = 3 * CONV_W + Q_LORA + KV_LORA + QK_ROPE
Z_COLS = 2304
EPS = 1e-6
ROPE_THETA = 10000.0
CHUNK = 64
ATT_SCALE = (QK_NOPE + QK_ROPE) ** -0.5
NEG = -1e30
EXP2_SCALE = ATT_SCALE * 1.4426950408889634
N_MOD = 9

LR, B1, B2, AEPS, WD, STEP = 0.001, 0.9, 0.999, 1e-08, 0.01, 10

N_CHIPS = 4
N_DEV = 8
VMEM_LIMIT = 56 << 20


def _params(sem, vmem=VMEM_LIMIT):
    return pltpu.CompilerParams(dimension_semantics=sem, vmem_limit_bytes=vmem)


def _rms(v):
    return lax.rsqrt(jnp.mean(v * v, axis=-1, keepdims=True) + EPS)


def _rsum8(v):
    t, n = v.shape
    return jnp.sum(v.reshape(t // 8, 8, n), axis=0)


def _all_rows(ref):
    ref[...] = jnp.broadcast_to(jnp.sum(ref[...], axis=0, keepdims=True), ref.shape)


def _gsum(v, gmat, split=False):
    hi = v.astype(BF16)
    out = jnp.dot(hi, gmat, preferred_element_type=F32)
    if split:
        out = out + jnp.dot((v - hi.astype(F32)).astype(BF16), gmat, preferred_element_type=F32)
    return out


def _dot_nt(a, b):
    return lax.dot_general(a, b, (((1,), (1,)), ((), ())), preferred_element_type=F32)


def _silu_parts(a):
    sg = jax.nn.sigmoid(a)
    return sg, a * sg


def _rope(xr, cs, sn, lane):
    rh = jnp.where(lane < 32, -pltpu.roll(xr, 96, 1), pltpu.roll(xr, 32, 1))
    return xr * cs + rh * sn


def _rope_t(g, cs, sn, lane):
    y = g * sn
    rt = jnp.where(lane < 32, pltpu.roll(y, 96, 1), jnp.where(lane < 64, -pltpu.roll(y, 32, 1), 0.0))
    return g * cs + rt


def _row_tile(rows, pref, mult=8):
    t = min(rows, pref) // mult * mult
    while rows % t:
        t -= mult
    return t


def _place():
    return lax.axis_index("x"), lax.axis_index("y"), lax.axis_index("c")


ANY = pl.BlockSpec(memory_space=pl.ANY)


def _hosted_call(body, *, name, grid, in_specs, out_specs, out_shape, scratch_shapes, semantics, args, comm=None):
    n_in, n_out, n_scr = len(in_specs), len(out_specs), len(scratch_shapes)
    if comm is None:
        res = pl.pallas_call(body, name=name, grid=grid, in_specs=in_specs, out_specs=out_specs, out_shape=out_shape,
                             scratch_shapes=scratch_shapes, compiler_params=_params(semantics))(*args)
        return list(res), []
    n_ci, n_co = len(comm.inputs), len(comm.out_shapes)
    total = int(np.prod(grid))

    def hosted(*refs):
        ins, refs = refs[:n_in], refs[n_in:]
        cins, refs = refs[:n_ci], refs[n_ci:]
        outs, refs = refs[:n_out], refs[n_out:]
        couts, refs = refs[:n_co], refs[n_co:]
        scratch, sems = refs[:n_scr], refs[n_scr]
        step = pl.program_id(0)
        for ax in range(1, len(grid)):
            step = step * grid[ax] + pl.program_id(ax)

        @pl.when(step == 0)
        def _():
            comm.start(cins, couts, sems)

        body(*ins, *outs, *scratch)

        @pl.when(step == total - 1)
        def _():
            comm.finish(cins, couts, sems)

    res = pl.pallas_call(
        hosted, name=name, grid=grid, in_specs=list(in_specs) + [ANY] * n_ci,
        out_specs=list(out_specs) + [ANY] * n_co, out_shape=list(out_shape) + list(comm.out_shapes),
        scratch_shapes=list(scratch_shapes) + [pltpu.SemaphoreType.DMA((comm.n_sems,))],
        compiler_params=_params(("arbitrary",) * len(grid)))(*args, *comm.inputs)
    return list(res[:n_out]), list(res[n_out:])


def _run_comm(comm, *, name):
    n_ci = len(comm.inputs)

    def body(*refs):
        cins, couts, sems = refs[:n_ci], refs[n_ci:-1], refs[-1]
        comm.start(cins, couts, sems)
        comm.finish(cins, couts, sems)

    return list(pl.pallas_call(
        body, name=name, out_shape=list(comm.out_shapes), in_specs=[ANY] * n_ci,
        out_specs=[ANY] * len(comm.out_shapes), scratch_shapes=[pltpu.SemaphoreType.DMA((comm.n_sems,))],
    )(*comm.inputs))


class _Gather:
    def __init__(self, slabs):
        self.inputs = list(slabs)
        self.out_shapes = [jax.ShapeDtypeStruct((N_CHIPS,) + s.shape, s.dtype) for s in slabs]
        self.n_sems = 12 * len(slabs)

    @staticmethod
    def _copy(out, sems, base, k, chip, hc, to, src=None):
        H = out.shape[1] // 2
        half = out.at[2 * chip[0] + chip[1], pl.ds(hc * H, H), :]
        return pltpu.make_async_remote_copy(
            src_ref=half if src is None else src, dst_ref=half, send_sem=sems.at[base + k],
            recv_sem=sems.at[base + 6 + k], device_id=to, device_id_type=MESH)

    def _firsts(self, src, out, sems, base):
        x, y, c = _place()
        H = src.shape[0] // 2
        chips = [(1 - x, y), (x, 1 - y), (1 - x, 1 - y)]
        return [self._copy(out, sems, base, j, (x, y), c, (*chip, c), src=src.at[pl.ds(c * H, H), :])
                for j, chip in enumerate(chips)]

    def start(self, ins, outs, sems):
        for i, (src, out) in enumerate(zip(ins, outs)):
            for cp in self._firsts(src, out, sems, 12 * i):
                cp.start()

    def finish(self, ins, outs, sems):
        x, y, c = _place()
        chips = [(1 - x, y), (x, 1 - y), (1 - x, 1 - y)]
        passed = []
        for i, out in enumerate(outs):
            for j, chip in enumerate(chips):
                self._copy(out, sems, 12 * i, j, chip, c, (x, y, c)).wait_recv()
                cp = self._copy(out, sems, 12 * i, 3 + j, chip, c, (x, y, 1 - c))
                cp.start()
                passed.append(cp)
        for i, out in enumerate(outs):
            for j, chip in enumerate(chips):
                self._copy(out, sems, 12 * i, 3 + j, chip, 1 - c, (x, y, c)).wait_recv()
        for cp in passed:
            cp.wait_send()
        for i, (src, out) in enumerate(zip(ins, outs)):
            for cp in self._firsts(src, out, sems, 12 * i):
                cp.wait_send()


class _PairExchange:
    def __init__(self, arrays):
        self.inputs = list(arrays)
        self.out_shapes = [jax.ShapeDtypeStruct((N_CHIPS, a.shape[1] // 2, a.shape[2]), a.dtype) for a in arrays]
        self.n_sems = 2 * len(arrays)

    def _copies(self, ins, outs, sems):
        x, y, c = _place()
        return [pltpu.make_async_remote_copy(
            src_ref=g.at[:, pl.ds((1 - c) * t.shape[1], t.shape[1]), :], dst_ref=t, send_sem=sems.at[2 * i],
            recv_sem=sems.at[2 * i + 1], device_id=(x, y, 1 - c), device_id_type=MESH)
            for i, (g, t) in enumerate(zip(ins, outs))]

    def start(self, ins, outs, sems):
        for cp in self._copies(ins, outs, sems):
            cp.start()

    def finish(self, ins, outs, sems):
        for cp in self._copies(ins, outs, sems):
            cp.wait()


class _ChipExchange:
    def __init__(self, arrays):
        self.inputs = list(arrays)
        self.out_shapes = [jax.ShapeDtypeStruct(a.shape, a.dtype) for a in arrays]
        self.n_sems = 6 * len(arrays)

    def _copies(self, p, t, sems, base):
        x, y, c = _place()
        myq = 2 * x + y
        chips = [(1 - x, y), (x, 1 - y), (1 - x, 1 - y)]
        sends = [pltpu.make_async_remote_copy(
            src_ref=p.at[2 * chip[0] + chip[1]], dst_ref=t.at[myq], send_sem=sems.at[base + j],
            recv_sem=sems.at[base + 3 + j], device_id=(*chip, c), device_id_type=MESH) for j, chip in enumerate(chips)]
        lands = [pltpu.make_async_remote_copy(
            src_ref=t.at[2 * chip[0] + chip[1]], dst_ref=t.at[2 * chip[0] + chip[1]], send_sem=sems.at[base + j],
            recv_sem=sems.at[base + 3 + j], device_id=(*chip, c), device_id_type=MESH) for j, chip in enumerate(chips)]
        return sends, lands

    def start(self, ins, outs, sems):
        for i, (p, t) in enumerate(zip(ins, outs)):
            for cp in self._copies(p, t, sems, 6 * i)[0]:
                cp.start()

    def finish(self, ins, outs, sems):
        for i, (p, t) in enumerate(zip(ins, outs)):
            sends, lands = self._copies(p, t, sems, 6 * i)
            for cp in lands:
                cp.wait_recv()
            for cp in sends:
                cp.wait_send()


class _SemView:
    def __init__(self, sems, base):
        self._sems, self._base = sems, base

    @property
    def at(self):
        return self

    def __getitem__(self, k):
        return self._sems.at[self._base + k]


class _Multi:
    def __init__(self, progs):
        self.progs = list(progs)
        self.inputs = [a for p in self.progs for a in p.inputs]
        self.out_shapes = [s for p in self.progs for s in p.out_shapes]
        self.n_sems = sum(p.n_sems for p in self.progs)

    def _each(self, ins, outs, sems):
        i = o = s = 0
        for p in self.progs:
            ni, no = len(p.inputs), len(p.out_shapes)
            yield p, ins[i:i + ni], outs[o:o + no], _SemView(sems, s)
            i, o, s = i + ni, o + no, s + p.n_sems

    def start(self, ins, outs, sems):
        for p, a, b, c in self._each(ins, outs, sems):
            p.start(a, b, c)

    def finish(self, ins, outs, sems):
        for p, a, b, c in self._each(ins, outs, sems):
            p.finish(a, b, c)


class _PairShare:
    def __init__(self, arrays):
        self.inputs = list(arrays)
        self.out_shapes = [jax.ShapeDtypeStruct(a.shape, a.dtype) for a in arrays]
        self.n_sems = 2 * len(arrays)

    def _copies(self, ins, outs, sems):
        x, y, c = _place()
        return [pltpu.make_async_remote_copy(
            src_ref=r, dst_ref=o, send_sem=sems.at[2 * i], recv_sem=sems.at[2 * i + 1],
            device_id=(x, y, 1 - c), device_id_type=MESH) for i, (r, o) in enumerate(zip(ins, outs))]

    def start(self, ins, outs, sems):
        for cp in self._copies(ins, outs, sems):
            cp.start()

    def finish(self, ins, outs, sems):
        for cp in self._copies(ins, outs, sems):
            cp.wait()


def _ffn_up(x, ng, sh, sc, w1, w3, *, name, comm=None):
    S = x.shape[0]
    tm, tn = _row_tile(S, 512), FF // 2

    def body(x_ref, g_ref, sh_ref, sc_ref, w1_ref, w3_ref, h_ref, a_ref, b_ref, u_ref, hs):
        @pl.when(pl.program_id(1) == 0)
        def _():
            xv = x_ref[...]
            h = ((xv * _rms(xv)) * g_ref[...]) * (1.0 + sc_ref[...]) + sh_ref[...]
            hb = h.astype(BF16)
            hs[...] = hb
            h_ref[...] = hb

        h = hs[...]
        a = _dot_nt(h, w1_ref[...])
        b = _dot_nt(h, w3_ref[...])
        _, sa = _silu_parts(a)
        a_ref[...] = a.astype(BF16)
        b_ref[...] = b.astype(BF16)
        u_ref[...] = (sa * b).astype(BF16)

    row = pl.BlockSpec((tm, D), lambda i, j: (i, 0))
    vec = pl.BlockSpec((1, D), lambda i, j: (0, 0))
    wsp = pl.BlockSpec((tn, D), lambda i, j: (j, 0))
    osp = pl.BlockSpec((tm, tn), lambda i, j: (i, j))
    return _hosted_call(
        body, name=name, grid=(S // tm, FF // tn),
        in_specs=[row, vec, vec, vec, wsp, wsp],
        out_specs=[row, osp, osp, osp],
        out_shape=[jax.ShapeDtypeStruct((S, D), BF16)] + [jax.ShapeDtypeStruct((S, FF), BF16)] * 3,
        scratch_shapes=[pltpu.VMEM((tm, D), BF16)],
        semantics=("parallel", "arbitrary"), args=(x, ng, sh, sc, w1, w3), comm=comm)


def _ffn_down(u, w2, x, gate, *, name):
    S = x.shape[0]
    tm = _row_tile(S, 512)

    def body(u_ref, w2_ref, x_ref, g_ref, xo_ref, f_ref):
        f = jnp.dot(u_ref[...], w2_ref[...], preferred_element_type=F32)
        xo_ref[...] = x_ref[...] + (0.5 * g_ref[...]) * f
        f_ref[...] = f.astype(BF16)

    return pl.pallas_call(
        body, name=name, grid=(S // tm,),
        in_specs=[pl.BlockSpec((tm, FF), lambda i: (i, 0)), pl.BlockSpec((FF, D), lambda i: (0, 0)),
                  pl.BlockSpec((tm, D), lambda i: (i, 0)), pl.BlockSpec((1, D), lambda i: (0, 0))],
        out_specs=[pl.BlockSpec((tm, D), lambda i: (i, 0))] * 2,
        out_shape=[jax.ShapeDtypeStruct((S, D), F32), jax.ShapeDtypeStruct((S, D), BF16)],
        compiler_params=_params(("parallel",)),
    )(u, w2, x, gate)


def _ffn_bwd_du(dx, gate, f, w2, a, b, *, name, comm=None):
    S = dx.shape[0]
    tm, tn = _row_tile(S, 512), FF // 2
    n_i = S // tm

    def body(dx_ref, g_ref, f_ref, w_ref, a_ref, b_ref, df_ref, da_ref, db_ref, dg_ref, dfs):
        i, j = pl.program_id(0), pl.program_id(1)

        @pl.when((i == 0) & (j == 0))
        def _():
            dg_ref[...] = jnp.zeros_like(dg_ref)

        @pl.when(j == 0)
        def _():
            dxv = dx_ref[...]
            dfb = (dxv * (0.5 * g_ref[...])).astype(BF16)
            dfs[...] = dfb
            df_ref[...] = dfb
            dg_ref[...] += _rsum8(dxv * (0.5 * f_ref[...].astype(F32)))

        du = _dot_nt(dfs[...], w_ref[pl.ds(pl.multiple_of(j * tn, tn), tn), :])
        av = a_ref[...].astype(F32)
        sg, sa = _silu_parts(av)
        da_ref[...] = (du * b_ref[...].astype(F32) * (sg * (1.0 + av * (1.0 - sg)))).astype(BF16)
        db_ref[...] = (du * sa).astype(BF16)

        @pl.when((i == n_i - 1) & (j == FF // tn - 1))
        def _():
            _all_rows(dg_ref)

    row = pl.BlockSpec((tm, D), lambda i, j: (i, 0))
    blk = pl.BlockSpec((tm, tn), lambda i, j: (i, j))
    return _hosted_call(
        body, name=name, grid=(n_i, FF // tn),
        in_specs=[row, pl.BlockSpec((1, D), lambda i, j: (0, 0)), row,
                  pl.BlockSpec((FF, D), lambda i, j: (0, 0)), blk, blk],
        out_specs=[row, blk, blk, pl.BlockSpec((8, D), lambda i, j: (0, 0))],
        out_shape=[jax.ShapeDtypeStruct((S, D), BF16), jax.ShapeDtypeStruct((S, FF), BF16),
                   jax.ShapeDtypeStruct((S, FF), BF16), jax.ShapeDtypeStruct((8, D), F32)],
        scratch_shapes=[pltpu.VMEM((tm, D), BF16)],
        semantics=("arbitrary", "arbitrary"), args=(dx, gate, f, w2, a, b), comm=comm)


def _tn_matmul(a, b, *, tm, tn, name, comm=None):
    S, M = a.shape
    N = b.shape[1]
    ts = _row_tile(S, 1024)
    ns = S // ts

    def body(a_ref, b_ref, o_ref, acc):
        s = pl.program_id(2)
        p = lax.dot_general(a_ref[...], b_ref[...], (((0,), (0,)), ((), ())), preferred_element_type=F32)

        @pl.when(s == 0)
        def _():
            acc[...] = p

        @pl.when(s > 0)
        def _():
            acc[...] += p

        @pl.when(s == ns - 1)
        def _():
            o_ref[...] = acc[...]

    (out,), couts = _hosted_call(
        body, name=name, grid=(M // tm, N // tn, ns),
        in_specs=[pl.BlockSpec((ts, tm), lambda i, j, s: (s, i)), pl.BlockSpec((ts, tn), lambda i, j, s: (s, j))],
        out_specs=[pl.BlockSpec((tm, tn), lambda i, j, s: (i, j))],
        out_shape=[jax.ShapeDtypeStruct((M, N), F32)],
        scratch_shapes=[pltpu.VMEM((tm, tn), F32)],
        semantics=("parallel", "parallel", "arbitrary"), args=(a, b), comm=comm)
    return out if comm is None else (out, couts)


def _dh_normbwd(pairs, x, ng, sc, dx_next, *, name, comm=None):
    S = x.shape[0]
    tm = _row_tile(S, 256)
    n_i = S // tm
    n_p = len(pairs)

    def body(*refs):
        a_refs, w_refs = refs[:n_p], refs[n_p:2 * n_p]
        x_ref, g_ref, sc_ref, dxn_ref, dx_ref, p_ref = refs[2 * n_p:]
        i = pl.program_id(0)
        dh = jnp.dot(a_refs[0][...], w_refs[0][...], preferred_element_type=F32)
        for k in range(1, n_p):
            dh = dh + jnp.dot(a_refs[k][...], w_refs[k][...], preferred_element_type=F32)
        xv = x_ref[...]
        r = _rms(xv)
        xh = xv * r
        g = g_ref[...]
        dn = dh * (1.0 + sc_ref[...])
        dy = dn * g
        dx_ref[...] = dxn_ref[...] + r * (dy - xh * jnp.mean(dy * xh, axis=-1, keepdims=True))

        @pl.when(i == 0)
        def _():
            p_ref[...] = jnp.zeros_like(p_ref)

        p_ref[:, 0:D] += _rsum8(dh * (xh * g))
        p_ref[:, D:2 * D] += _rsum8(dh)
        p_ref[:, 2 * D:3 * D] += _rsum8(dn * xh)

        @pl.when(i == n_i - 1)
        def _():
            _all_rows(p_ref)

    row = pl.BlockSpec((tm, D), lambda i: (i, 0))
    vec = pl.BlockSpec((1, D), lambda i: (0, 0))
    in_specs = ([pl.BlockSpec((tm, a.shape[1]), lambda i: (i, 0)) for a, _ in pairs]
                + [pl.BlockSpec(w.shape, lambda i: (0, 0)) for _, w in pairs] + [row, vec, vec, row])
    return _hosted_call(
        body, name=name, grid=(n_i,), in_specs=in_specs,
        out_specs=[row, pl.BlockSpec((8, 3 * D), lambda i: (0, 0))],
        out_shape=[jax.ShapeDtypeStruct((S, D), F32), jax.ShapeDtypeStruct((8, 3 * D), F32)],
        scratch_shapes=[], semantics=("arbitrary",),
        args=(*[a for a, _ in pairs], *[w for _, w in pairs], x, ng, sc, dx_next), comm=comm)


def _final_loss(x3, gfin, tgt):
    S = x3.shape[0]
    tm = _row_tile(S, 512)
    n_i = S // tm

    def body(x_ref, g_ref, t_ref, dx_ref, dg_ref, loss_ref, lacc):
        i = pl.program_id(0)
        xv = x_ref[...]
        r = _rms(xv)
        xh = xv * r
        g = g_ref[...]
        e = xh * g - t_ref[...]
        dout = e * (1.0 / D)
        dy = dout * g
        dx_ref[...] = r * (dy - xh * jnp.mean(dy * xh, axis=-1, keepdims=True))

        @pl.when(i == 0)
        def _():
            dg_ref[...] = jnp.zeros_like(dg_ref)
            lacc[...] = jnp.zeros_like(lacc)

        dg_ref[...] += _rsum8(dout * xh)
        lacc[...] += _rsum8(e * e)

        @pl.when(i == n_i - 1)
        def _():
            _all_rows(dg_ref)
            tot = jnp.sum(jnp.sum(lacc[...], axis=0, keepdims=True), axis=1, keepdims=True)
            loss_ref[...] = jnp.broadcast_to(tot * (0.5 / D), loss_ref.shape)

    row = pl.BlockSpec((tm, D), lambda i: (i, 0))
    return pl.pallas_call(
        body, name="final_loss", grid=(n_i,),
        in_specs=[row, pl.BlockSpec((1, D), lambda i: (0, 0)), row],
        out_specs=[row, pl.BlockSpec((8, D), lambda i: (0, 0)), pl.BlockSpec((8, 128), lambda i: (0, 0))],
        out_shape=[jax.ShapeDtypeStruct((S, D), F32), jax.ShapeDtypeStruct((8, D), F32),
                   jax.ShapeDtypeStruct((8, 128), F32)],
        scratch_shapes=[pltpu.VMEM((8, D), F32)],
        compiler_params=_params(("arbitrary",)),
    )(x3, gfin, tgt)


def _mix_in(x, ng, sh, sc, w_in):
    S = x.shape[0]
    tm = _row_tile(S, 512)

    def body(x_ref, g_ref, sh_ref, sc_ref, w_ref, h_ref, z_ref):
        xv = x_ref[...]
        hb = (((xv * _rms(xv)) * g_ref[...]) * (1.0 + sc_ref[...]) + sh_ref[...]).astype(BF16)
        h_ref[...] = hb
        z_ref[...] = _dot_nt(hb, w_ref[...])

    row = pl.BlockSpec((tm, D), lambda i: (i, 0))
    vec = pl.BlockSpec((1, D), lambda i: (0, 0))
    return pl.pallas_call(
        body, name="mix_in", grid=(S // tm,),
        in_specs=[row, vec, vec, vec, pl.BlockSpec((Z_COLS, D), lambda i: (0, 0))],
        out_specs=[row, pl.BlockSpec((tm, Z_COLS), lambda i: (i, 0))],
        out_shape=[jax.ShapeDtypeStruct((S, D), BF16), jax.ShapeDtypeStruct((S, Z_COLS), F32)],
        compiler_params=_params(("parallel",)),
    )(x, ng, sh, sc, w_in)


def _conv_taps(u, halo, rows):
    u1 = jnp.where(rows == 0, halo[7:8, :], pltpu.roll(u, 1, 0))
    u2 = jnp.where(rows == 0, halo[6:7, :], jnp.where(rows == 1, halo[7:8, :], pltpu.roll(u, 2, 0)))
    return u1, u2


def _mix_mid(z, conv_w, gq, gkv, wuq, wukv, cs, sn):
    S = z.shape[0]
    tm = _row_tile(S, 512)
    hb = tm // 8

    def body(z_ref, zh_ref, cw_ref, gq_ref, gkv_ref, wuq_ref, wukv_ref, cs_ref, sn_ref,
             ya_ref, q_ref, k_ref, v_ref, cqn_ref, ckvn_ref):
        i = pl.program_id(0)
        xb = z_ref[:, 0:CONV_W]
        u = z_ref[:, CONV_W:2 * CONV_W] * z_ref[:, 2 * CONV_W:3 * CONV_W]
        halo = zh_ref[:, CONV_W:2 * CONV_W] * zh_ref[:, 2 * CONV_W:3 * CONV_W]
        halo = jnp.where(i > 0, halo, 0.0)
        rows = lax.broadcasted_iota(jnp.int32, (tm, CONV_W), 0)
        u1, u2 = _conv_taps(u, halo, rows)
        y = cw_ref[0:1, :] * u2 + cw_ref[1:2, :] * u1 + cw_ref[2:3, :] * u
        ya_ref[...] = xb * y

        lane = lax.broadcasted_iota(jnp.int32, (tm, 128), 1)
        cs_v, sn_v = cs_ref[...], sn_ref[...]
        cq = z_ref[:, 3 * CONV_W:3 * CONV_W + Q_LORA]
        cqn = ((cq * _rms(cq)) * gq_ref[...]).astype(BF16)
        cqn_ref[...] = cqn
        q = _dot_nt(cqn, wuq_ref[...])
        for h in range(HEADS):
            o = h * HEAD_SLOT
            q_ref[:, o:o + 128] = q[:, o:o + 128].astype(BF16)
            q_ref[:, o + 128:o + 256] = _rope(q[:, o + 128:o + 256], cs_v, sn_v, lane).astype(BF16)

        c0 = 3 * CONV_W + Q_LORA
        ckv = z_ref[:, c0:c0 + KV_LORA]
        ckvn = ((ckv * _rms(ckv)) * gkv_ref[...]).astype(BF16)
        ckvn_ref[...] = ckvn
        kv = jnp.dot(ckvn, wukv_ref[...], preferred_element_type=F32)
        krot = _rope(z_ref[:, c0 + KV_LORA:Z_COLS], cs_v, sn_v, lane).astype(BF16)
        for h in range(HEADS):
            o = h * HEAD_SLOT
            k_ref[:, o:o + 128] = kv[:, h * 128:(h + 1) * 128].astype(BF16)
            k_ref[:, o + 128:o + 256] = krot
        v_ref[...] = kv[:, HEADS * 128:].astype(BF16)

    def rows_of(n):
        return pl.BlockSpec((tm, n), lambda i: (i, 0))

    def whole(shape):
        return pl.BlockSpec(shape, lambda i: (0, 0))

    return pl.pallas_call(
        body, name="mix_mid", grid=(S // tm,),
        in_specs=[rows_of(Z_COLS), pl.BlockSpec((8, Z_COLS), lambda i: (jnp.maximum(i * hb - 1, 0), 0)),
                  whole((8, CONV_W)), whole((1, Q_LORA)), whole((1, KV_LORA)),
                  whole((HEADS * HEAD_SLOT, Q_LORA)), whole((KV_LORA, 2 * HEADS * 128)),
                  rows_of(128), rows_of(128)],
        out_specs=[rows_of(CONV_W), rows_of(HEADS * HEAD_SLOT), rows_of(HEADS * HEAD_SLOT), rows_of(HEADS * V_HEAD),
                   rows_of(Q_LORA), rows_of(KV_LORA)],
        out_shape=[jax.ShapeDtypeStruct((S, CONV_W), F32), jax.ShapeDtypeStruct((S, HEADS * HEAD_SLOT), BF16),
                   jax.ShapeDtypeStruct((S, HEADS * HEAD_SLOT), BF16), jax.ShapeDtypeStruct((S, HEADS * V_HEAD), BF16),
                   jax.ShapeDtypeStruct((S, Q_LORA), BF16), jax.ShapeDtypeStruct((S, KV_LORA), BF16)],
        compiler_params=_params(("parallel",)),
    )(z, z, conv_w, gq, gkv, wuq, wukv, cs, sn)


def _att_blocks(S):
    bk = min(512, max(S // 4, 128))
    return 2 * bk, bk


def _pair_tables(S, k_major):
    bq, bk = _att_blocks(S)
    nq, nk = S // bq, S // bk
    vis = lambda qi, ki: ki * bk < (qi + 1) * bq
    if k_major:
        pairs = [(qi, ki) for ki in range(nk) for qi in range(nq) if vis(qi, ki)]
    else:
        pairs = [(qi, ki) for qi in range(nq) for ki in range(nk) if vis(qi, ki)]
    cols = [[p[0] for p in pairs], [p[1] for p in pairs], [int((p[1] + 1) * bk > p[0] * bq) for p in pairs]]
    return [jnp.asarray(np.array(c, np.int32)) for c in cols], len(pairs)


def _chunk_mask(qi, ki, bq, bk):
    r = (qi * bq + lax.broadcasted_iota(jnp.int32, (bq, bk), 0)) // CHUNK
    c = (ki * bk + lax.broadcasted_iota(jnp.int32, (bq, bk), 1)) // CHUNK
    return c <= r


def _attention(q, k, v):
    S = q.shape[0]
    bq, bk = _att_blocks(S)
    last_k = bq // bk - 1
    tables, n_pairs = _pair_tables(S, k_major=False)

    def body(qi_ref, ki_ref, mk_ref, q_ref, k_ref, v_ref, o_ref, lse_ref, m_s, l_s, acc_s):
        p_id = pl.program_id(1)
        qi, ki = qi_ref[p_id], ki_ref[p_id]

        @pl.when(ki == 0)
        def _():
            m_s[...] = jnp.full_like(m_s, NEG)
            l_s[...] = jnp.zeros_like(l_s)
            acc_s[...] = jnp.zeros_like(acc_s)

        def update(masked):
            s = lax.dot_general(q_ref[...], k_ref[...], (((1,), (1,)), ((), ())), preferred_element_type=F32)
            if masked:
                s = jnp.where(_chunk_mask(qi, ki, bq, bk), s, NEG)
            m_prev = m_s[...]
            m_new = jnp.maximum(m_prev, jnp.max(s, axis=1, keepdims=True))
            alpha = jnp.exp2((m_prev - m_new) * EXP2_SCALE)
            p = jnp.exp2((s - jnp.tile(m_new, (1, bk // 128))) * EXP2_SCALE)
            l_s[...] = alpha * l_s[...] + jnp.sum(p, axis=1, keepdims=True)
            acc_s[...] = alpha * acc_s[...] + jnp.dot(p.astype(BF16), v_ref[...], preferred_element_type=F32)
            m_s[...] = m_new

        @pl.when(mk_ref[p_id] == 0)
        def _():
            update(False)

        @pl.when(mk_ref[p_id] == 1)
        def _():
            update(True)

        @pl.when(ki == qi * (last_k + 1) + last_k)
        def _():
            l = l_s[...]
            o_ref[...] = acc_s[...] / l
            lse_ref[...] = m_s[...] * EXP2_SCALE + jnp.log2(l)

    grid_spec = pltpu.PrefetchScalarGridSpec(
        num_scalar_prefetch=3, grid=(HEADS, n_pairs),
        in_specs=[pl.BlockSpec((bq, HEAD_SLOT), lambda h, p, qt, kt, mt: (qt[p], h)),
                  pl.BlockSpec((bk, HEAD_SLOT), lambda h, p, qt, kt, mt: (kt[p], h)),
                  pl.BlockSpec((bk, V_HEAD), lambda h, p, qt, kt, mt: (kt[p], h))],
        out_specs=[pl.BlockSpec((bq, V_HEAD), lambda h, p, qt, kt, mt: (qt[p], h))] * 2,
        scratch_shapes=[pltpu.VMEM((bq, V_HEAD), F32)] * 3)
    return pl.pallas_call(
        body, name="attention", grid_spec=grid_spec,
        out_shape=[jax.ShapeDtypeStruct((S, HEADS * V_HEAD), F32)] * 2,
        compiler_params=_params(("arbitrary", "arbitrary")),
    )(*tables, q, k, v)


def _attention_bwd(q, k, v, do, lse2, delta):
    S = q.shape[0]
    bq, bk = _att_blocks(S)
    nq = S // bq
    tables, n_pairs = _pair_tables(S, k_major=True)

    def body(qi_ref, ki_ref, mk_ref, q_ref, k_ref, v_ref, do_ref, lse_ref, dl_ref, dq_hbm, dk_ref, dv_ref,
             dq_s, dk_s, dv_s, sem):
        head, p_id = pl.program_id(0), pl.program_id(1)
        qi, ki = qi_ref[p_id], ki_ref[p_id]
        rows = pl.ds(pl.multiple_of(qi * bq, bq), bq)

        @pl.when(qi * bq <= ki * bk)
        def _():
            dk_s[...] = jnp.zeros_like(dk_s)
            dv_s[...] = jnp.zeros_like(dv_s)

        def update(masked):
            qv, kv, dov = q_ref[...], k_ref[...], do_ref[...]
            s = lax.dot_general(qv, kv, (((1,), (1,)), ((), ())), preferred_element_type=F32)
            dp = lax.dot_general(dov, v_ref[...], (((1,), (1,)), ((), ())), preferred_element_type=F32)
            if masked:
                s = jnp.where(_chunk_mask(qi, ki, bq, bk), s, NEG)
            p = jnp.exp2(s * EXP2_SCALE - jnp.tile(lse_ref[...], (1, bk // 128)))
            dv_s[...] += lax.dot_general(p.astype(BF16), dov, (((0,), (0,)), ((), ())), preferred_element_type=F32)
            ds = (p * (dp - jnp.tile(dl_ref[...], (1, bk // 128)))).astype(BF16)
            dk_s[...] += lax.dot_general(ds, qv, (((0,), (0,)), ((), ())), preferred_element_type=F32)
            dq = jnp.dot(ds, kv, preferred_element_type=F32)

            @pl.when(ki == 0)
            def _():
                dq_s[rows, :] = dq

            @pl.when(ki > 0)
            def _():
                dq_s[rows, :] += dq

        @pl.when(mk_ref[p_id] == 0)
        def _():
            update(False)

        @pl.when(mk_ref[p_id] == 1)
        def _():
            update(True)

        @pl.when(qi == nq - 1)
        def _():
            dk_ref[...] = dk_s[...] * ATT_SCALE
            dv_ref[...] = dv_s[...]

        @pl.when(p_id == n_pairs - 1)
        def _():
            dq_s[...] = dq_s[...] * ATT_SCALE
            out = pltpu.make_async_copy(
                dq_s, dq_hbm.at[:, pl.ds(pl.multiple_of(head * HEAD_SLOT, HEAD_SLOT), HEAD_SLOT)], sem)
            out.start()
            out.wait()

    grid_spec = pltpu.PrefetchScalarGridSpec(
        num_scalar_prefetch=3, grid=(HEADS, n_pairs),
        in_specs=[pl.BlockSpec((bq, HEAD_SLOT), lambda h, p, qt, kt, mt: (qt[p], h)),
                  pl.BlockSpec((bk, HEAD_SLOT), lambda h, p, qt, kt, mt: (kt[p], h)),
                  pl.BlockSpec((bk, V_HEAD), lambda h, p, qt, kt, mt: (kt[p], h)),
                  pl.BlockSpec((bq, V_HEAD), lambda h, p, qt, kt, mt: (qt[p], h)),
                  pl.BlockSpec((bq, V_HEAD), lambda h, p, qt, kt, mt: (qt[p], h)),
                  pl.BlockSpec((bq, V_HEAD), lambda h, p, qt, kt, mt: (qt[p], h))],
        out_specs=[pl.BlockSpec(memory_space=pl.ANY),
                   pl.BlockSpec((bk, HEAD_SLOT), lambda h, p, qt, kt, mt: (kt[p], h)),
                   pl.BlockSpec((bk, V_HEAD), lambda h, p, qt, kt, mt: (kt[p], h))],
        scratch_shapes=[pltpu.VMEM((S, HEAD_SLOT), F32), pltpu.VMEM((bk, HEAD_SLOT), F32),
                        pltpu.VMEM((bk, V_HEAD), F32), pltpu.SemaphoreType.DMA])
    return pl.pallas_call(
        body, name="attention_bwd", grid_spec=grid_spec,
        out_shape=[jax.ShapeDtypeStruct((S, HEADS * HEAD_SLOT), F32), jax.ShapeDtypeStruct((S, HEADS * HEAD_SLOT), F32),
                   jax.ShapeDtypeStruct((S, HEADS * V_HEAD), F32)],
        compiler_params=_params(("arbitrary", "arbitrary")),
    )(*tables, q, k, v, do, lse2, delta)


def _group_mats():
    def blockdiag(n, g):
        idx = np.arange(n) // g
        return jnp.asarray((idx[:, None] == idx[None, :]).astype(np.float32), dtype=BF16)
    return blockdiag(CONV_W, CONV_GROUP), blockdiag(HEADS * V_HEAD, V_HEAD)


def _mix_out(ya, o, gout, w_out, x, gate, ga, gb):
    S = x.shape[0]
    tm = _row_tile(S, 512)

    def body(ya_ref, o_ref, go_ref, w_ref, x_ref, g_ref, ga_ref, gb_ref, xo_ref, yn_ref, yo_ref):
        yav, ov = ya_ref[...], o_ref[...]
        ra = lax.rsqrt(_gsum(yav * yav, ga_ref[...]) * (1.0 / CONV_GROUP) + EPS)
        rb = lax.rsqrt(_gsum(ov * ov, gb_ref[...]) * (1.0 / V_HEAD) + EPS)
        na = ((yav * ra) * go_ref[:, 0:CONV_W]).astype(BF16)
        nb = ((ov * rb) * go_ref[:, CONV_W:]).astype(BF16)
        yn_ref[:, 0:CONV_W] = na
        yn_ref[:, CONV_W:] = nb
        yo = (jnp.dot(na, w_ref[0:CONV_W, :], preferred_element_type=F32)
              + jnp.dot(nb, w_ref[CONV_W:, :], preferred_element_type=F32))
        xo_ref[...] = x_ref[...] + g_ref[...] * yo
        yo_ref[...] = yo.astype(BF16)

    row = pl.BlockSpec((tm, D), lambda i: (i, 0))
    half = pl.BlockSpec((tm, CONV_W), lambda i: (i, 0))
    vec = pl.BlockSpec((1, D), lambda i: (0, 0))
    sq = pl.BlockSpec((CONV_W, CONV_W), lambda i: (0, 0))
    return pl.pallas_call(
        body, name="mix_out", grid=(S // tm,),
        in_specs=[half, half, vec, pl.BlockSpec((D, D), lambda i: (0, 0)), row, vec, sq, sq],
        out_specs=[row, row, row],
        out_shape=[jax.ShapeDtypeStruct((S, D), F32), jax.ShapeDtypeStruct((S, D), BF16),
                   jax.ShapeDtypeStruct((S, D), BF16)],
        compiler_params=_params(("parallel",)),
    )(ya, o, gout, w_out, x, gate, ga, gb)


def _mix_out_bwd(dx, gate, yo, w_out, ya, o, gout, ga, gb, comm=None):
    S = dx.shape[0]
    tm = _row_tile(S, 256)
    n_i = S // tm

    def norm_bwd(v, dn, gain, gmat, inv_n):
        r = lax.rsqrt(_gsum(v * v, gmat) * inv_n + EPS)
        vh = v * r
        dy = dn * gain
        return r * (dy - vh * (_gsum(dy * vh, gmat) * inv_n)), dn * vh

    def body(dx_ref, g_ref, yo_ref, w_ref, ya_ref, o_ref, go_ref, ga_ref, gb_ref,
             dyo_ref, dya_ref, do_ref, dl_ref, p_ref):
        i = pl.program_id(0)
        dxv = dx_ref[...]
        dyo = (dxv * g_ref[...]).astype(BF16)
        dyo_ref[...] = dyo
        dyn = _dot_nt(dyo, w_ref[...])
        dya, dga = norm_bwd(ya_ref[...], dyn[:, 0:CONV_W], go_ref[:, 0:CONV_W], ga_ref[...], 1.0 / CONV_GROUP)
        ov = o_ref[...]
        do, dgb = norm_bwd(ov, dyn[:, CONV_W:], go_ref[:, CONV_W:], gb_ref[...], 1.0 / V_HEAD)
        dya_ref[...] = dya
        do_ref[...] = do.astype(BF16)
        dl_ref[...] = _gsum(do * ov, gb_ref[...], split=True)

        @pl.when(i == 0)
        def _():
            p_ref[...] = jnp.zeros_like(p_ref)

        p_ref[:, 0:D] += _rsum8(dxv * yo_ref[...].astype(F32))
        p_ref[:, D:D + CONV_W] += _rsum8(dga)
        p_ref[:, D + CONV_W:2 * D] += _rsum8(dgb)

        @pl.when(i == n_i - 1)
        def _():
            _all_rows(p_ref)

    row = pl.BlockSpec((tm, D), lambda i: (i, 0))
    half = pl.BlockSpec((tm, CONV_W), lambda i: (i, 0))
    vec = pl.BlockSpec((1, D), lambda i: (0, 0))
    sq = pl.BlockSpec((CONV_W, CONV_W), lambda i: (0, 0))
    return _hosted_call(
        body, name="mix_out_bwd", grid=(n_i,),
        in_specs=[row, vec, row, pl.BlockSpec((D, D), lambda i: (0, 0)), half, half, vec, sq, sq],
        out_specs=[row, half, half, half, pl.BlockSpec((8, 2 * D), lambda i: (0, 0))],
        out_shape=[jax.ShapeDtypeStruct((S, D), BF16), jax.ShapeDtypeStruct((S, CONV_W), F32),
                   jax.ShapeDtypeStruct((S, CONV_W), BF16), jax.ShapeDtypeStruct((S, CONV_W), F32),
                   jax.ShapeDtypeStruct((8, 2 * D), F32)],
        scratch_shapes=[], semantics=("arbitrary",), args=(dx, gate, yo, w_out, ya, o, gout, ga, gb), comm=comm)


MID_SUMS = 3 * CONV_W + Q_LORA + KV_LORA


def _mix_mid_bwd(z, dya, conv_w, gq, gkv, wuq, wukv, cs, sn, dq, dk, dv, comm=None):
    S = z.shape[0]
    tm = _row_tile(S, 256)
    n_i = S // tm
    hb = tm // 8
    last_blk = S // 8 - 1

    def latent_bwd(cv, dcn, gain):
        r = _rms(cv)
        ch = cv * r
        dy = dcn * gain
        return r * (dy - ch * jnp.mean(dy * ch, axis=-1, keepdims=True)), dcn * ch

    def body(z_ref, zp_ref, zn_ref, dya_ref, dyan_ref, cw_ref, gq_ref, gkv_ref, wuq_ref, wukv_ref, cs_ref, sn_ref,
             dq_ref, dk_ref, dv_ref, dz_ref, dqf_ref, dkvf_ref, p_ref):
        i = pl.program_id(0)
        xb, xc, xu = z_ref[:, 0:CONV_W], z_ref[:, CONV_W:2 * CONV_W], z_ref[:, 2 * CONV_W:3 * CONV_W]
        u = xc * xu
        halo = jnp.where(i > 0, zp_ref[:, CONV_W:2 * CONV_W] * zp_ref[:, 2 * CONV_W:3 * CONV_W], 0.0)
        rows = lax.broadcasted_iota(jnp.int32, (tm, CONV_W), 0)
        u1, u2 = _conv_taps(u, halo, rows)
        w0, w1, w2 = cw_ref[0:1, :], cw_ref[1:2, :], cw_ref[2:3, :]
        y = w0 * u2 + w1 * u1 + w2 * u
        dyav = dya_ref[...]
        dy = dyav * xb
        nxt = jnp.where(i < n_i - 1, dyan_ref[...] * zn_ref[:, 0:CONV_W], 0.0)
        dy1 = jnp.where(rows == tm - 1, nxt[0:1, :], pltpu.roll(dy, tm - 1, 0))
        dy2 = jnp.where(rows == tm - 1, nxt[1:2, :], jnp.where(rows == tm - 2, nxt[0:1, :], pltpu.roll(dy, tm - 2, 0)))
        du = w2 * dy + w1 * dy1 + w0 * dy2
        dz_ref[:, 0:CONV_W] = (dyav * y).astype(BF16)
        dz_ref[:, CONV_W:2 * CONV_W] = (du * xu).astype(BF16)
        dz_ref[:, 2 * CONV_W:3 * CONV_W] = (du * xc).astype(BF16)

        lane = lax.broadcasted_iota(jnp.int32, (tm, 128), 1)
        cs_v, sn_v = cs_ref[...], sn_ref[...]
        dkr = jnp.zeros((tm, 128), F32)
        for h in range(HEADS):
            o = h * HEAD_SLOT
            dqf_ref[:, o:o + 128] = dq_ref[:, o:o + 128].astype(BF16)
            dqf_ref[:, o + 128:o + 256] = _rope_t(dq_ref[:, o + 128:o + 256], cs_v, sn_v, lane).astype(BF16)
            dkvf_ref[:, h * 128:(h + 1) * 128] = dk_ref[:, o:o + 128].astype(BF16)
            dkr = dkr + dk_ref[:, o + 128:o + 256]
        dkvf_ref[:, HEADS * 128:] = dv_ref[...].astype(BF16)

        c0 = 3 * CONV_W
        dcqn = jnp.dot(dqf_ref[...], wuq_ref[...], preferred_element_type=F32)
        dcq, dgq = latent_bwd(z_ref[:, c0:c0 + Q_LORA], dcqn, gq_ref[...])
        dz_ref[:, c0:c0 + Q_LORA] = dcq.astype(BF16)
        c1 = c0 + Q_LORA
        dckvn = _dot_nt(dkvf_ref[...], wukv_ref[...])
        dckv, dgkv = latent_bwd(z_ref[:, c1:c1 + KV_LORA], dckvn, gkv_ref[...])
        dz_ref[:, c1:c1 + KV_LORA] = dckv.astype(BF16)
        dz_ref[:, c1 + KV_LORA:Z_COLS] = _rope_t(dkr, cs_v, sn_v, lane).astype(BF16)

        @pl.when(i == 0)
        def _():
            p_ref[...] = jnp.zeros_like(p_ref)

        p_ref[:, 0:CONV_W] += _rsum8(dy * u2)
        p_ref[:, CONV_W:2 * CONV_W] += _rsum8(dy * u1)
        p_ref[:, 2 * CONV_W:3 * CONV_W] += _rsum8(dy * u)
        p_ref[:, c0:c0 + Q_LORA] += _rsum8(dgq)
        p_ref[:, c1:c1 + KV_LORA] += _rsum8(dgkv)

        @pl.when(i == n_i - 1)
        def _():
            _all_rows(p_ref)

    def rows_of(n):
        return pl.BlockSpec((tm, n), lambda i: (i, 0))

    def whole(shape):
        return pl.BlockSpec(shape, lambda i: (0, 0))

    def prev8(n):
        return pl.BlockSpec((8, n), lambda i: (jnp.maximum(i * hb - 1, 0), 0))

    def next8(n):
        return pl.BlockSpec((8, n), lambda i: (jnp.minimum((i + 1) * hb, last_blk), 0))

    return _hosted_call(
        body, name="mix_mid_bwd", grid=(n_i,),
        in_specs=[rows_of(Z_COLS), prev8(Z_COLS), next8(Z_COLS), rows_of(CONV_W), next8(CONV_W),
                  whole((8, CONV_W)), whole((1, Q_LORA)), whole((1, KV_LORA)),
                  whole((HEADS * HEAD_SLOT, Q_LORA)), whole((KV_LORA, 2 * HEADS * 128)),
                  rows_of(128), rows_of(128),
                  rows_of(HEADS * HEAD_SLOT), rows_of(HEADS * HEAD_SLOT), rows_of(HEADS * V_HEAD)],
        out_specs=[rows_of(Z_COLS), rows_of(HEADS * HEAD_SLOT), rows_of(2 * HEADS * 128), whole((8, MID_SUMS))],
        out_shape=[jax.ShapeDtypeStruct((S, Z_COLS), BF16), jax.ShapeDtypeStruct((S, HEADS * HEAD_SLOT), BF16),
                   jax.ShapeDtypeStruct((S, 2 * HEADS * 128), BF16), jax.ShapeDtypeStruct((8, MID_SUMS), F32)],
        scratch_shapes=[], semantics=("arbitrary",),
        args=(z, z, z, dya, dya, conv_w, gq, gkv, wuq, wukv, cs, sn, dq, dk, dv), comm=comm)


ADA_Q = N_MOD * D // N_CHIPS
ADA_TN = 768


def _ada_forward(c_all, ada_w_q, ada_b_q):
    def body(c_ref, w_ref, b_ref, o_ref):
        cv = c_ref[...]
        sc = (cv * jax.nn.sigmoid(cv)).astype(BF16)
        o_ref[...] = jnp.dot(sc, w_ref[...].astype(BF16), preferred_element_type=F32) + b_ref[...]

    return pl.pallas_call(
        body, name="ada_forward", grid=(ADA_Q // ADA_TN,),
        in_specs=[pl.BlockSpec((16, D), lambda j: (0, 0)), pl.BlockSpec((D, ADA_TN), lambda j: (0, j)),
                  pl.BlockSpec((1, ADA_TN), lambda j: (0, j))],
        out_specs=pl.BlockSpec((16, ADA_TN), lambda j: (0, j)),
        out_shape=jax.ShapeDtypeStruct((16, ADA_Q), F32),
        compiler_params=_params(("parallel",)),
    )(c_all, ada_w_q, ada_b_q)


def _ada_wgrad(c_all, dmod_q):
    def body(c_ref, d_ref, o_ref):
        cv = c_ref[...]
        sc = (cv * jax.nn.sigmoid(cv)).astype(BF16)
        o_ref[...] = lax.dot_general(sc, d_ref[...].astype(BF16), (((0,), (0,)), ((), ())),
                                     preferred_element_type=F32)

    return pl.pallas_call(
        body, name="ada_wgrad", grid=(ADA_Q // ADA_TN,),
        in_specs=[pl.BlockSpec((16, D), lambda j: (0, 0)), pl.BlockSpec((16, ADA_TN), lambda j: (0, j))],
        out_specs=pl.BlockSpec((D, ADA_TN), lambda j: (0, j)),
        out_shape=jax.ShapeDtypeStruct((D, ADA_Q), F32),
        compiler_params=_params(("parallel",)),
    )(c_all, dmod_q)


def _sum_devices(parts):
    n = parts.shape[1]

    def body(p_ref, o_ref):
        o_ref[...] = jnp.broadcast_to(jnp.sum(p_ref[...], axis=0, keepdims=True), o_ref.shape)

    return pl.pallas_call(
        body, name="sum_devices",
        in_specs=[pl.BlockSpec((N_DEV, n), lambda: (0, 0))], out_specs=pl.BlockSpec((N_DEV, n), lambda: (0, 0)),
        out_shape=jax.ShapeDtypeStruct((N_DEV, n), F32),
    )(parts)


def _adamw(w, g, m, v, *, name):
    _, rows, cols = w.shape
    tr = _row_tile(rows, 256)

    def body(w_ref, g_ref, m_ref, v_ref, go_ref, d_ref, mo_ref, vo_ref):
        gv = g_ref[...]
        mn = B1 * m_ref[0] + (1.0 - B1) * gv
        vn = B2 * v_ref[0] + (1.0 - B2) * (gv * gv)
        m_hat = mn / (1.0 - B1 ** STEP)
        v_hat = vn / (1.0 - B2 ** STEP)
        go_ref[0] = gv
        d_ref[0] = -LR * (m_hat / (jnp.sqrt(v_hat) + AEPS) + WD * w_ref[0])
        mo_ref[0] = mn
        vo_ref[0] = vn

    blk = pl.BlockSpec((1, tr, cols), lambda i: (0, i, 0))
    return pl.pallas_call(
        body, name=name, grid=(rows // tr,),
        in_specs=[blk, pl.BlockSpec((tr, cols), lambda i: (i, 0)), blk, blk], out_specs=[blk] * 4,
        out_shape=[jax.ShapeDtypeStruct((1, rows, cols), F32)] * 4,
        compiler_params=_params(("parallel",)),
    )(w, g, m, v)


def _small_allgather(v, *, name):
    m, n = v.shape

    def body(x_ref, out_ref, send_sems, recv_sems, local_sem):
        x, y, c = _place()
        me, sibling = (x, y, c), (x, y, 1 - c)
        chips = [(1 - x, y), (x, 1 - y), (1 - x, 1 - y)]

        def rows(px, py, pc):
            return out_ref.at[pl.ds((4 * px + 2 * py + pc) * m, m), :]

        def copy(k, block, to, src=None):
            return pltpu.make_async_remote_copy(
                src_ref=rows(*block) if src is None else src, dst_ref=rows(*block),
                send_sem=send_sems.at[k], recv_sem=recv_sems.at[k], device_id=to, device_id_type=MESH)

        mine = pltpu.make_async_copy(x_ref, rows(*me), local_sem)
        mine.start()
        first = [copy(0, me, sibling, src=x_ref)]
        first += [copy(1 + j, me, (*chip, c), src=x_ref) for j, chip in enumerate(chips)]
        for cp in first:
            cp.start()
        passed = [copy(4 + j, (*chip, c), sibling) for j, chip in enumerate(chips)]
        for j, chip in enumerate(chips):
            copy(1 + j, (*chip, c), me).wait_recv()
            passed[j].start()
        copy(0, sibling, me).wait_recv()
        for j, chip in enumerate(chips):
            copy(4 + j, (*chip, 1 - c), me).wait_recv()
        for cp in first + passed:
            cp.wait_send()
        mine.wait()

    return pl.pallas_call(
        body, name=name,
        out_shape=jax.ShapeDtypeStruct((N_DEV * m, n), v.dtype),
        in_specs=[pl.BlockSpec(memory_space=pltpu.VMEM)], out_specs=pl.BlockSpec(memory_space=pltpu.VMEM),
        scratch_shapes=[pltpu.SemaphoreType.DMA((7,)), pltpu.SemaphoreType.DMA((7,)), pltpu.SemaphoreType.DMA],
    )(v)


ADD_BLOCKS = 2


def _pair_add(place, gs, ts, *, name):
    n_a = len(gs)

    def body(pl_ref, *refs):
        g_refs, t_refs = refs[:n_a], refs[n_a:2 * n_a]
        pf_refs, pb_refs = refs[2 * n_a:3 * n_a], refs[3 * n_a:]
        for g_ref, t_ref, pf_ref, pb_ref in zip(g_refs, t_refs, pf_refs, pb_refs):
            s = g_ref[...] + t_ref[...]
            pf_ref[...] = s
            pb_ref[...] = s.astype(BF16)

    def blk(t, own_half):
        tr = t.shape[1] // ADD_BLOCKS
        if own_half:
            return pl.BlockSpec((1, tr, t.shape[2]), lambda q, r, p: (q, p[0] * ADD_BLOCKS + r, 0))
        return pl.BlockSpec((1, tr, t.shape[2]), lambda q, r, p: (q, r, 0))

    grid_spec = pltpu.PrefetchScalarGridSpec(
        num_scalar_prefetch=1, grid=(N_CHIPS, ADD_BLOCKS),
        in_specs=[blk(t, True) for t in ts] + [blk(t, False) for t in ts],
        out_specs=[blk(t, False) for t in ts] * 2)
    res = pl.pallas_call(
        body, name=name, grid_spec=grid_spec,
        out_shape=[jax.ShapeDtypeStruct(t.shape, F32) for t in ts] + [jax.ShapeDtypeStruct(t.shape, BF16) for t in ts],
        compiler_params=_params(("parallel", "parallel")),
    )(place, *gs, *ts)
    return list(res[:n_a]), list(res[n_a:])


def _chip_add(place, pfs, ts, *, name):
    n_a = len(pfs)

    def body(pl_ref, *refs):
        pf_refs, t_refs, o_refs = refs[:n_a], refs[n_a:4 * n_a], refs[4 * n_a:]
        for i, (pf_ref, o_ref) in enumerate(zip(pf_refs, o_refs)):
            t1, t2, t3 = t_refs[3 * i:3 * i + 3]
            o_ref[...] = ((pf_ref[0] + t1[0].astype(F32)) + t2[0].astype(F32)) + t3[0].astype(F32)

    def slot(t, j):
        return pl.BlockSpec((1, t.shape[1] // ADD_BLOCKS, t.shape[2]), lambda r, p: (p[1] ^ j, r, 0))

    grid_spec = pltpu.PrefetchScalarGridSpec(
        num_scalar_prefetch=1, grid=(ADD_BLOCKS,),
        in_specs=[slot(t, 0) for t in pfs] + [slot(t, j) for t in ts for j in (1, 2, 3)],
        out_specs=[pl.BlockSpec((t.shape[1] // ADD_BLOCKS, t.shape[2]), lambda r, p: (r, 0)) for t in pfs])
    res = pl.pallas_call(
        body, name=name, grid_spec=grid_spec,
        out_shape=[jax.ShapeDtypeStruct(t.shape[1:], F32) for t in pfs],
        compiler_params=_params(("parallel",)),
    )(place, *pfs, *[t for t in ts for _ in range(3)])
    return list(res)


BULK = [("ffn1_w1", "colsT"), ("ffn1_w3", "colsT"), ("ffn1_w2", "rows"), ("w_in", "cols"), ("w_uq", "colsT"),
        ("w_ukv", "cols"), ("w_out", "rows"), ("ffn2_w1", "colsT"), ("ffn2_w3", "colsT"), ("ffn2_w2", "rows")]
KIND = dict(BULK)


def _group(*names):
    return [b for b in BULK if b[0] in names]


W_FIRST = _group("ffn1_w1", "ffn1_w3")
W_REST = [b for b in BULK if b not in W_FIRST]
G_FFN2 = _group("ffn2_w1", "ffn2_w3", "ffn2_w2")
G_MIX = _group("w_in", "w_uq", "w_ukv", "w_out")
G_FFN1 = _group("ffn1_w1", "ffn1_w3", "ffn1_w2")


def _gathered_weights(specs, shards, got, myq):
    out = {}
    for (name, kind), part in zip(specs, got):
        part = lax.dynamic_update_slice_in_dim(part, shards[name][None], myq, axis=0)
        out[name] = _full_weight(part, kind)
    return out


def _working_shard(w, kind):
    return jnp.swapaxes(w, 1, 2)[0] if kind == "colsT" else w[0]


def _full_weight(parts, kind):
    if kind == "cols":
        return jnp.transpose(parts, (1, 0, 2)).reshape(parts.shape[1], -1)
    return parts.reshape(-1, parts.shape[2])


def _quarters(g, kind):
    if kind == "cols":
        k, n = g.shape
        return jnp.transpose(g.reshape(k, N_CHIPS, n // N_CHIPS), (1, 0, 2))
    return g.reshape(N_CHIPS, g.shape[0] // N_CHIPS, g.shape[1])


def _pad_heads(w_uq_t):
    w = w_uq_t.reshape(HEADS, QK_NOPE + QK_ROPE, Q_LORA)
    return jnp.pad(w, ((0, 0), (0, HEAD_SLOT - QK_NOPE - QK_ROPE), (0, 0))).reshape(HEADS * HEAD_SLOT, Q_LORA)


def _unpad_heads(g):
    return g.reshape(HEADS, HEAD_SLOT, Q_LORA)[:, :QK_NOPE + QK_ROPE].reshape(HEADS * (QK_NOPE + QK_ROPE), Q_LORA)


def _split_kv(w_ukv):
    return jnp.transpose(w_ukv.reshape(KV_LORA, HEADS, 2, 128), (0, 2, 1, 3)).reshape(KV_LORA, 2 * HEADS * 128)


def _merge_kv(g):
    return jnp.transpose(g.reshape(KV_LORA, 2, HEADS, 128), (0, 2, 1, 3)).reshape(KV_LORA, 2 * HEADS * 128)


def _rope_tables(positions):
    inv_freq = ROPE_THETA ** (-jnp.arange(0, QK_ROPE, 2, dtype=F32) / QK_ROPE)
    ang = positions.astype(F32)[:, None] * inv_freq
    cos, sin, zero = jnp.cos(ang), jnp.sin(ang), jnp.zeros((positions.shape[0], 64), F32)
    return jnp.concatenate([cos, cos, zero], axis=1), jnp.concatenate([sin, sin, zero], axis=1)


def _reduce_tail(place, specs, pfs, t2s, tag, host=None):
    rhs = _chip_add(place, pfs, t2s, name=tag + "_chip_add")
    if host is None:
        out, others = None, _run_comm(_PairShare(rhs), name=tag + "_pair_share")
    else:
        out, others = host(_PairShare(rhs))
    return out, _assemble(place, specs, rhs, others)


def _assemble(place, specs, rhs, others):
    south = place[0] == 0
    return {b[0]: jnp.concatenate([jnp.where(south, rh, ot), jnp.where(south, ot, rh)], axis=0)
            for b, rh, ot in zip(specs, rhs, others)}


def _local_step(x, positions, target, mod, vec, conv_w, w_first, rest_shards, place):
    row = lambda k: mod[k:k + 1]
    sh1, sc1, g1, sh2, sc2, g2, sh3, sc3, g3 = [row(k) for k in range(N_MOD)]
    cs, sn = _rope_tables(positions)
    cw8 = jnp.pad(conv_w, ((0, 5), (0, 0)))
    ga, gb = _group_mats()
    dist = place is not None
    comm = lambda prog: prog if dist else None

    gather = _Gather([rest_shards[b[0]] for b in W_REST]) if dist else None
    (h1, a1, b1, u1), got = _ffn_up(x, vec["norm_ffn1_g"], sh1, sc1, w_first["ffn1_w1"], w_first["ffn1_w3"],
                                    name="ffn1_up", comm=gather)
    if dist:
        w = _gathered_weights(W_REST, rest_shards, got, place[1])
    else:
        w = dict(rest_shards)
    w.update(w_first)
    w_in = jnp.pad(w["w_in"].T, ((0, Z_COLS - IN_COLS), (0, 0)))
    wuq = _pad_heads(w["w_uq"])
    wukv = _split_kv(w["w_ukv"])
    x1, f1 = _ffn_down(u1, w["ffn1_w2"], x, g1, name="ffn1_down")
    h2, z = _mix_in(x1, vec["norm_mix_g"], sh2, sc2, w_in)
    ya, q, k, v, cqn, ckvn = _mix_mid(z, cw8, vec["q_norm_g"], vec["kv_norm_g"], wuq, wukv, cs, sn)
    o, lse = _attention(q, k, v)
    x2, yn, yo = _mix_out(ya, o, vec["out_norm_g"], w["w_out"], x1, g2, ga, gb)
    (h3, a3, b3, u3), _ = _ffn_up(x2, vec["norm_ffn2_g"], sh3, sc3, w["ffn2_w1"], w["ffn2_w3"], name="ffn2_up")
    x3, f3 = _ffn_down(u3, w["ffn2_w2"], x2, g3, name="ffn2_down")
    dx3, dgfin, loss_blk = _final_loss(x3, vec["final_norm_g"], target)

    grads, reduced = {}, {}

    def tn(a, b, tm, tn_, name, prog=None):
        if prog is None:
            return _tn_matmul(a, b, tm=tm, tn=tn_, name=name), None
        return _tn_matmul(a, b, tm=tm, tn=tn_, name=name, comm=prog)

    def slab_of(specs):
        return [_quarters(grads[n], kind) for n, kind in specs]

    (df3, da3, db3, dg3), _ = _ffn_bwd_du(dx3, g3, f3, w["ffn2_w2"], a3, b3, name="ffn2_bwd_du")
    grads["ffn2_w2"], _ = tn(u3, df3, FF // 2, D, "ffn2_dw2")
    grads["ffn2_w1"], _ = tn(da3, h3, FF // 2, D, "ffn2_dw1")
    grads["ffn2_w3"], _ = tn(db3, h3, FF // 2, D, "ffn2_dw3")
    (dx2, s3), _ = _dh_normbwd([(da3, w["ffn2_w1"]), (db3, w["ffn2_w3"])], x2, vec["norm_ffn2_g"], sc3, dx3,
                               name="ffn2_bwd_dh")

    p1 = slab_of(G_FFN2) if dist else None
    (dyo, dya, do, delta, s_out), t1 = _mix_out_bwd(dx2, g2, yo, w["w_out"], ya, o, vec["out_norm_g"], ga, gb,
                                                    comm=comm(_PairExchange(p1) if dist else None))
    grads["w_out"], _ = tn(yn, dyo, D, D, "dw_out")
    if dist:
        pf1, pb1 = _pair_add(place, p1, t1, name="ffn2g_pair_add")
    dq, dk, dv = _attention_bwd(q, k, v, do, lse, delta)
    (dz, dqf, dkvf, s_mid), t2 = _mix_mid_bwd(z, dya, cw8, vec["q_norm_g"], vec["kv_norm_g"], wuq, wukv, cs, sn,
                                              dq, dk, dv, comm=comm(_ChipExchange(pb1) if dist else None))
    g_uq, _ = tn(dqf, cqn, HEADS * HEAD_SLOT, Q_LORA, "dw_uq")
    g_ukv, _ = tn(ckvn, dkvf, KV_LORA, 2 * HEADS * 128, "dw_ukv")
    grads["w_uq"], grads["w_ukv"] = _unpad_heads(g_uq), _merge_kv(g_ukv)
    if dist:
        g_in, red = _reduce_tail(place, G_FFN2, pf1, t2, "ffn2g",
                                 host=lambda prog: tn(h2, dz, D, Z_COLS // 2, "dw_in", prog))
        reduced.update(red)
    else:
        g_in, _ = tn(h2, dz, D, Z_COLS // 2, "dw_in")
    grads["w_in"] = g_in[:, :IN_COLS]
    (dx1, s2), _ = _dh_normbwd([(dz, w_in)], x1, vec["norm_mix_g"], sc2, dx2, name="mix_bwd_dh")

    p2 = slab_of(G_MIX) if dist else None
    (df1, da1, db1, dg1), t1 = _ffn_bwd_du(dx1, g1, f1, w["ffn1_w2"], a1, b1, name="ffn1_bwd_du",
                                           comm=comm(_PairExchange(p2) if dist else None))
    dh_pairs = [(da1, w["ffn1_w1"]), (db1, w["ffn1_w3"])]
    if dist:
        pf2, pb2 = _pair_add(place, p2, t1, name="mixg_pair_add")
        grads["ffn1_w2"], t2 = tn(u1, df1, FF // 2, D, "ffn1_dw2", _ChipExchange(pb2))
        rh_mix = _chip_add(place, pf2, t2, name="mixg_chip_add")
        q_w2 = [_quarters(grads["ffn1_w2"], KIND["ffn1_w2"])]
        grads["ffn1_w1"], got = tn(da1, h1, FF // 2, D, "ffn1_dw1", _Multi([_PairShare(rh_mix), _PairExchange(q_w2)]))
        reduced.update(_assemble(place, G_MIX, rh_mix, got[:len(rh_mix)]))
        pf_w2, pb_w2 = _pair_add(place, q_w2, got[len(rh_mix):], name="ffn1w2_pair_add")
        q_w1 = [_quarters(grads["ffn1_w1"], KIND["ffn1_w1"])]
        grads["ffn1_w3"], got = tn(db1, h1, FF // 2, D, "ffn1_dw3", _Multi([_ChipExchange(pb_w2), _PairExchange(q_w1)]))
        t2_w2 = got[:1]
        pf_w1, pb_w1 = _pair_add(place, q_w1, got[1:], name="ffn1w1_pair_add")
        q_w3 = [_quarters(grads["ffn1_w3"], KIND["ffn1_w3"])]
        (dx0, s1), got = _dh_normbwd(dh_pairs, x, vec["norm_ffn1_g"], sc1, dx1, name="ffn1_bwd_dh",
                                     comm=_Multi([_ChipExchange(pb_w1), _PairExchange(q_w3)]))
        t2_w1 = got[:1]
        pf_w3, pb_w3 = _pair_add(place, q_w3, got[1:], name="ffn1w3_pair_add")
        t2_w3 = _run_comm(_ChipExchange(pb_w3), name="ffn1w3_chip_exchange")
        rh = _chip_add(place, pf_w1 + pf_w3 + pf_w2, t2_w1 + t2_w3 + t2_w2, name="ffn1g_chip_add")
        reduced.update(_assemble(place, G_FFN1, rh, _run_comm(_PairShare(rh), name="ffn1g_pair_share")))
    else:
        grads["ffn1_w2"], _ = tn(u1, df1, FF // 2, D, "ffn1_dw2")
        grads["ffn1_w1"], _ = tn(da1, h1, FF // 2, D, "ffn1_dw1")
        grads["ffn1_w3"], _ = tn(db1, h1, FF // 2, D, "ffn1_dw3")
        (dx0, s1), _ = _dh_normbwd(dh_pairs, x, vec["norm_ffn1_g"], sc1, dx1, name="ffn1_bwd_dh")
        reduced = grads

    def part(s, k):
        return s[0:1, k * D:(k + 1) * D]

    dmod = jnp.concatenate([part(s1, 1), part(s1, 0), dg1[0:1], part(s2, 1), part(s2, 0), part(s_out, 0),
                            part(s3, 1), part(s3, 0), dg3[0:1]], axis=1)
    small = {"norm_ffn1_g": part(s1, 2), "norm_mix_g": part(s2, 2), "out_norm_g": part(s_out, 1),
             "norm_ffn2_g": part(s3, 2), "final_norm_g": dgfin[0:1],
             "q_norm_g": s_mid[0:1, 3 * CONV_W:3 * CONV_W + Q_LORA],
             "kv_norm_g": s_mid[0:1, 3 * CONV_W + Q_LORA:MID_SUMS], "conv_w": s_mid[0:1, 0:3 * CONV_W]}
    return loss_blk, dx0, reduced, dmod, small


SMALL = [("norm_ffn1_g", D), ("norm_mix_g", D), ("out_norm_g", D), ("norm_ffn2_g", D), ("final_norm_g", D),
         ("q_norm_g", Q_LORA), ("kv_norm_g", KV_LORA), ("conv_w", 3 * CONV_W)]
WEIGHTS = ['ada_w', 'ada_b', 'norm_ffn1_g', 'ffn1_w1', 'ffn1_w3', 'ffn1_w2', 'norm_mix_g', 'w_in', 'conv_w',
           'q_norm_g', 'w_uq', 'kv_norm_g', 'w_ukv', 'out_norm_g', 'w_out', 'norm_ffn2_g', 'ffn2_w1', 'ffn2_w3',
           'ffn2_w2', 'final_norm_g']


def kernel(x, c, positions, ada_w, ada_b, norm_ffn1_g, ffn1_w1, ffn1_w3, ffn1_w2, norm_mix_g, w_in, conv_w, q_norm_g, w_uq, kv_norm_g, w_ukv, out_norm_g, w_out, norm_ffn2_g, ffn2_w1, ffn2_w3, ffn2_w2, final_norm_g, loss_target, m_ada_w, m_ada_b, m_norm_ffn1_g, m_ffn1_w1, m_ffn1_w3, m_ffn1_w2, m_norm_mix_g, m_w_in, m_conv_w, m_q_norm_g, m_w_uq, m_kv_norm_g, m_w_ukv, m_out_norm_g, m_w_out, m_norm_ffn2_g, m_ffn2_w1, m_ffn2_w3, m_ffn2_w2, m_final_norm_g, v_ada_w, v_ada_b, v_norm_ffn1_g, v_ffn1_w1, v_ffn1_w3, v_ffn1_w2, v_norm_mix_g, v_w_in, v_conv_w, v_q_norm_g, v_w_uq, v_kv_norm_g, v_w_ukv, v_out_norm_g, v_w_out, v_norm_ffn2_g, v_ffn2_w1, v_ffn2_w3, v_ffn2_w2, v_final_norm_g):
    args = dict(locals())
    wts = {n: args[n] for n in WEIGHTS}
    mom = {n: args["m_" + n] for n in WEIGHTS}
    var = {n: args["v_" + n] for n in WEIGHTS}
    ax, ay, ac = _place()
    myq = 2 * ax + ay
    me = 2 * myq + ac
    place = jnp.stack([ac, myq]).astype(jnp.int32)

    shards = {name: _working_shard(wts[name], kind).astype(BF16) for name, kind in BULK}
    first = _run_comm(_Gather([shards[b[0]] for b in W_FIRST]), name="gather_ffn1")
    w_first = _gathered_weights(W_FIRST, shards, first, myq)

    mine = jnp.concatenate([c, conv_w[0].reshape(1, 3 * CONV_W // N_CHIPS)], axis=1)
    seen = _small_allgather(jnp.pad(mine, ((0, 7), (0, 0))), name="gather_cond").reshape(N_DEV, 8, -1)[:, 0]
    c_all = jnp.pad(seen[:, :D], ((0, 8), (0, 0)))
    conv_full = jnp.transpose(seen[0::2, D:].reshape(N_CHIPS, 3, CONV_W // N_CHIPS), (1, 0, 2)).reshape(3, CONV_W)
    ada_b_q = lax.dynamic_slice_in_dim(ada_b, myq * ADA_Q, ADA_Q, axis=1)
    mod_q = _ada_forward(c_all, ada_w[0], ada_b_q)
    mod_all = _small_allgather(mod_q, name="gather_mod").reshape(N_DEV, 16, ADA_Q)
    mod_rows = jnp.transpose(mod_all[0::2, :N_DEV], (1, 0, 2)).reshape(N_DEV, N_MOD * D)
    mod = lax.dynamic_slice_in_dim(mod_rows, me, 1, axis=0).reshape(N_MOD, D)

    vec = {n: wts[n] for n in ("norm_ffn1_g", "norm_mix_g", "q_norm_g", "kv_norm_g", "out_norm_g", "norm_ffn2_g")}
    vec["final_norm_g"] = final_norm_g.reshape(1, D)
    loss_blk, grad_x, gq, dmod, small = _local_step(x[0], positions[0], loss_target[0], mod, vec, conv_full, w_first,
                                                    {b[0]: shards[b[0]] for b in W_REST}, place)
    loss = lax.psum(loss_blk[0, 0], ("x", "y", "c"))

    rows = jnp.concatenate([dmod] + [small[n] for n, _ in SMALL], axis=1)
    width = rows.shape[1]
    fold = -(-width // (8 * 128)) * 128
    rows = jnp.pad(rows, ((0, 0), (0, 8 * fold - width))).reshape(8, fold)
    every = _small_allgather(rows, name="gather_small").reshape(N_DEV, 8 * fold)[:, :width]
    total = _sum_devices(every)[0:1]
    dmod_q = lax.dynamic_slice_in_dim(every[:, :N_MOD * D], myq * ADA_Q, ADA_Q, axis=1)
    g = {name: gq[name] for name, *_ in BULK}
    g["ada_w"] = _ada_wgrad(c_all, jnp.pad(dmod_q, ((0, 8), (0, 0))))
    g["ada_b"] = total[:, :N_MOD * D]
    off = N_MOD * D
    for n, width in SMALL:
        g[n] = total[:, off:off + width]
        off += width
    g["conv_w"] = lax.dynamic_slice_in_dim(g["conv_w"].reshape(3, CONV_W), myq * (CONV_W // N_CHIPS),
                                           CONV_W // N_CHIPS, axis=1)
    g["final_norm_g"] = g["final_norm_g"].reshape(D)

    delta, new_m, new_v = {}, {}, {}
    for name in ["ada_w"] + [b[0] for b in BULK]:
        view = (lambda a: jnp.swapaxes(a, 1, 2)) if KIND.get(name) == "colsT" else (lambda a: a)
        g[name], delta[name], new_m[name], new_v[name] = [
            view(r) for r in _adamw(view(wts[name]), g[name], view(mom[name]), view(var[name]), name="adamw_" + name)]
    smalls = ["ada_b"] + [n for n, _ in SMALL]

    def packed(d):
        flat = jnp.concatenate([d[n].reshape(1, -1) for n in smalls], axis=1)
        return jnp.pad(flat.reshape(-1, D), ((0, 1), (0, 0)))

    res = _adamw(packed(wts)[None], packed(g), packed(mom)[None], packed(var)[None], name="adamw_small")[1:]
    off = 0
    for n in smalls:
        size = wts[n].size
        for d, r in zip((delta, new_m, new_v), res):
            d[n] = r.reshape(-1)[off:off + size].reshape(wts[n].shape)
        g[n] = g[n].reshape(wts[n].shape)
        off += size

    return (loss, grad_x[None], *[g[n] for n in WEIGHTS], *[delta[n] for n in WEIGHTS],
            *[new_m[n] for n in WEIGHTS], *[new_v[n] for n in WEIGHTS])
```

```python
import functools

import numpy as np
import jax
import jax.numpy as jnp
from jax import lax
from jax.experimental import pallas as pl
from jax.experimental.pallas import tpu as pltpu

F32 = jnp.float32
BF16 = jnp.bfloat16
MESH = pl.DeviceIdType.MESH

D = 1024
FF = 2816
CONV_W = 512
CONV_GROUP = 64
HEADS = 4
QK_NOPE = 128
QK_ROPE = 64
V_HEAD = 128
Q_LORA = 384
KV_LORA = 256
HEAD_SLOT = 256
IN_COLS = 3 * CONV_W + Q_LORA + KV_LORA + QK_ROPE
Z_COLS = 2304
EPS = 1e-6
ROPE_THETA = 10000.0
CHUNK = 64
ATT_SCALE = (QK_NOPE + QK_ROPE) ** -0.5
NEG = -1e30
EXP2_SCALE = ATT_SCALE * 1.4426950408889634
N_MOD = 9

LR, B1, B2, AEPS, WD, STEP = 0.001, 0.9, 0.999, 1e-08, 0.01, 10

N_CHIPS = 4
N_DEV = 8
VMEM_LIMIT = 56 << 20


def _params(sem, vmem=VMEM_LIMIT):
    return pltpu.CompilerParams(dimension_semantics=sem, vmem_limit_bytes=vmem)


def _rms(v):
    return lax.rsqrt(jnp.mean(v * v, axis=-1, keepdims=True) + EPS)


def _rsum8(v):
    t, n = v.shape
    return jnp.sum(v.reshape(t // 8, 8, n), axis=0)


def _all_rows(ref):
    ref[...] = jnp.broadcast_to(jnp.sum(ref[...], axis=0, keepdims=True), ref.shape)


def _gsum(v, gmat, split=False):
    hi = v.astype(BF16)
    out = jnp.dot(hi, gmat, preferred_element_type=F32)
    if split:
        out = out + jnp.dot((v - hi.astype(F32)).astype(BF16), gmat, preferred_element_type=F32)
    return out


def _dot_nt(a, b):
    return lax.dot_general(a, b, (((1,), (1,)), ((), ())), preferred_element_type=F32)


def _silu_parts(a):
    sg = jax.nn.sigmoid(a)
    return sg, a * sg


def _rope(xr, cs, sn, lane):
    rh = jnp.where(lane < 32, -pltpu.roll(xr, 96, 1), pltpu.roll(xr, 32, 1))
    return xr * cs + rh * sn


def _rope_t(g, cs, sn, lane):
    y = g * sn
    rt = jnp.where(lane < 32, pltpu.roll(y, 96, 1), jnp.where(lane < 64, -pltpu.roll(y, 32, 1), 0.0))
    return g * cs + rt


def _row_tile(rows, pref, mult=8):
    t = min(rows, pref) // mult * mult
    while rows % t:
        t -= mult
    return t


def _place():
    return lax.axis_index("x"), lax.axis_index("y"), lax.axis_index("c")


ANY = pl.BlockSpec(memory_space=pl.ANY)


def _hosted_call(body, *, name, grid, in_specs, out_specs, out_shape, scratch_shapes, semantics, args, comm=None):
    n_in, n_out, n_scr = len(in_specs), len(out_specs), len(scratch_shapes)
    if comm is None:
        res = pl.pallas_call(body, name=name, grid=grid, in_specs=in_specs, out_specs=out_specs, out_shape=out_shape,
                             scratch_shapes=scratch_shapes, compiler_params=_params(semantics))(*args)
        return list(res), []
    n_ci, n_co = len(comm.inputs), len(comm.out_shapes)
    total = int(np.prod(grid))

    def hosted(*refs):
        ins, refs = refs[:n_in], refs[n_in:]
        cins, refs = refs[:n_ci], refs[n_ci:]
        outs, refs = refs[:n_out], refs[n_out:]
        couts, refs = refs[:n_co], refs[n_co:]
        scratch, sems = refs[:n_scr], refs[n_scr]
        step = pl.program_id(0)
        for ax in range(1, len(grid)):
            step = step * grid[ax] + pl.program_id(ax)

        @pl.when(step == 0)
        def _():
            comm.start(cins, couts, sems)

        body(*ins, *outs, *scratch)

        @pl.when(step == total - 1)
        def _():
            comm.finish(cins, couts, sems)

    res = pl.pallas_call(
        hosted, name=name, grid=grid, in_specs=list(in_specs) + [ANY] * n_ci,
        out_specs=list(out_specs) + [ANY] * n_co, out_shape=list(out_shape) + list(comm.out_shapes),
        scratch_shapes=list(scratch_shapes) + [pltpu.SemaphoreType.DMA((comm.n_sems,))],
        compiler_params=_params(("arbitrary",) * len(grid)))(*args, *comm.inputs)
    return list(res[:n_out]), list(res[n_out:])


def _run_comm(comm, *, name):
    n_ci = len(comm.inputs)

    def body(*refs):
        cins, couts, sems = refs[:n_ci], refs[n_ci:-1], refs[-1]
        comm.start(cins, couts, sems)
        comm.finish(cins, couts, sems)

    return list(pl.pallas_call(
        body, name=name, out_shape=list(comm.out_shapes), in_specs=[ANY] * n_ci,
        out_specs=[ANY] * len(comm.out_shapes), scratch_shapes=[pltpu.SemaphoreType.DMA((comm.n_sems,))],
    )(*comm.inputs))


class _Gather:
    def __init__(self, slabs):
        self.inputs = list(slabs)
        self.out_shapes = [jax.ShapeDtypeStruct((N_CHIPS,) + s.shape, s.dtype) for s in slabs]
        self.n_sems = 12 * len(slabs)

    @staticmethod
    def _copy(out, sems, base, k, chip, hc, to, src=None):
        H = out.shape[1] // 2
        half = out.at[2 * chip[0] + chip[1], pl.ds(hc * H, H), :]
        return pltpu.make_async_remote_copy(
            src_ref=half if src is None else src, dst_ref=half, send_sem=sems.at[base + k],
            recv_sem=sems.at[base + 6 + k], device_id=to, device_id_type=MESH)

    def _firsts(self, src, out, sems, base):
        x, y, c = _place()
        H = src.shape[0] // 2
        chips = [(1 - x, y), (x, 1 - y), (1 - x, 1 - y)]
        return [self._copy(out, sems, base, j, (x, y), c, (*chip, c), src=src.at[pl.ds(c * H, H), :])
                for j, chip in enumerate(chips)]

    def start(self, ins, outs, sems):
        for i, (src, out) in enumerate(zip(ins, outs)):
            for cp in self._firsts(src, out, sems, 12 * i):
                cp.start()

    def finish(self, ins, outs, sems):
        x, y, c = _place()
        chips = [(1 - x, y), (x, 1 - y), (1 - x, 1 - y)]
        passed = []
        for i, out in enumerate(outs):
            for j, chip in enumerate(chips):
                self._copy(out, sems, 12 * i, j, chip, c, (x, y, c)).wait_recv()
                cp = self._copy(out, sems, 12 * i, 3 + j, chip, c, (x, y, 1 - c))
                cp.start()
                passed.append(cp)
        for i, out in enumerate(outs):
            for j, chip in enumerate(chips):
                self._copy(out, sems, 12 * i, 3 + j, chip, 1 - c, (x, y, c)).wait_recv()
        for cp in passed:
            cp.wait_send()
        for i, (src, out) in enumerate(zip(ins, outs)):
            for cp in self._firsts(src, out, sems, 12 * i):
                cp.wait_send()


class _PairExchange:
    def __init__(self, arrays):
        self.inputs = list(arrays)
        self.out_shapes = [jax.ShapeDtypeStruct((N_CHIPS, a.shape[1] // 2, a.shape[2]), a.dtype) for a in arrays]
        self.n_sems = 2 * len(arrays)

    def _copies(self, ins, outs, sems):
        x, y, c = _place()
        return [pltpu.make_async_remote_copy(
            src_ref=g.at[:, pl.ds((1 - c) * t.shape[1], t.shape[1]), :], dst_ref=t, send_sem=sems.at[2 * i],
            recv_sem=sems.at[2 * i + 1], device_id=(x, y, 1 - c), device_id_type=MESH)
            for i, (g, t) in enumerate(zip(ins, outs))]

    def start(self, ins, outs, sems):
        for cp in self._copies(ins, outs, sems):
            cp.start()

    def finish(self, ins, outs, sems):
        for cp in self._copies(ins, outs, sems):
            cp.wait()


class _ChipExchange:
    def __init__(self, arrays):
        self.inputs = list(arrays)
        self.out_shapes = [jax.ShapeDtypeStruct(a.shape, a.dtype) for a in arrays]
        self.n_sems = 6 * len(arrays)

    def _copies(self, p, t, sems, base):
        x, y, c = _place()
        myq = 2 * x + y
        chips = [(1 - x, y), (x, 1 - y), (1 - x, 1 - y)]
        sends = [pltpu.make_async_remote_copy(
            src_ref=p.at[2 * chip[0] + chip[1]], dst_ref=t.at[myq], send_sem=sems.at[base + j],
            recv_sem=sems.at[base + 3 + j], device_id=(*chip, c), device_id_type=MESH) for j, chip in enumerate(chips)]
        lands = [pltpu.make_async_remote_copy(
            src_ref=t.at[2 * chip[0] + chip[1]], dst_ref=t.at[2 * chip[0] + chip[1]], send_sem=sems.at[base + j],
            recv_sem=sems.at[base + 3 + j], device_id=(*chip, c), device_id_type=MESH) for j, chip in enumerate(chips)]
        return sends, lands

    def start(self, ins, outs, sems):
        for i, (p, t) in enumerate(zip(ins, outs)):
            for cp in self._copies(p, t, sems, 6 * i)[0]:
                cp.start()

    def finish(self, ins, outs, sems):
        for i, (p, t) in enumerate(zip(ins, outs)):
            sends, lands = self._copies(p, t, sems, 6 * i)
            for cp in lands:
                cp.wait_recv()
            for cp in sends:
                cp.wait_send()


class _SemView:
    def __init__(self, sems, base):
        self._sems, self._base = sems, base

    @property
    def at(self):
        return self

    def __getitem__(self, k):
        return self._sems.at[self._base + k]


class _Multi:
    def __init__(self, progs):
        self.progs = list(progs)
        self.inputs = [a for p in self.progs for a in p.inputs]
        self.out_shapes = [s for p in self.progs for s in p.out_shapes]
        self.n_sems = sum(p.n_sems for p in self.progs)

    def _each(self, ins, outs, sems):
        i = o = s = 0
        for p in self.progs:
            ni, no = len(p.inputs), len(p.out_shapes)
            yield p, ins[i:i + ni], outs[o:o + no], _SemView(sems, s)
            i, o, s = i + ni, o + no, s + p.n_sems

    def start(self, ins, outs, sems):
        for p, a, b, c in self._each(ins, outs, sems):
            p.start(a, b, c)

    def finish(self, ins, outs, sems):
        for p, a, b, c in self._each(ins, outs, sems):
            p.finish(a, b, c)


class _PairShare:
    def __init__(self, arrays):
        self.inputs = list(arrays)
        self.out_shapes = [jax.ShapeDtypeStruct(a.shape, a.dtype) for a in arrays]
        self.n_sems = 2 * len(arrays)

    def _copies(self, ins, outs, sems):
        x, y, c = _place()
        return [pltpu.make_async_remote_copy(
            src_ref=r, dst_ref=o, send_sem=sems.at[2 * i], recv_sem=sems.at[2 * i + 1],
            device_id=(x, y, 1 - c), device_id_type=MESH) for i, (r, o) in enumerate(zip(ins, outs))]

    def start(self, ins, outs, sems):
        for cp in self._copies(ins, outs, sems):
            cp.start()

    def finish(self, ins, outs, sems):
        for cp in self._copies(ins, outs, sems):
            cp.wait()


def _ffn_up(x, ng, sh, sc, w1, w3, *, name, comm=None):
    S = x.shape[0]
    tm, tn = _row_tile(S, 512), FF // 2

    def body(x_ref, g_ref, sh_ref, sc_ref, w1_ref, w3_ref, h_ref, a_ref, b_ref, u_ref, hs):
        @pl.when(pl.program_id(1) == 0)
        def _():
            xv = x_ref[...]
            h = ((xv * _rms(xv)) * g_ref[...]) * (1.0 + sc_ref[...]) + sh_ref[...]
            hb = h.astype(BF16)
            hs[...] = hb
            h_ref[...] = hb

        h = hs[...]
        a = _dot_nt(h, w1_ref[...])
        b = _dot_nt(h, w3_ref[...])
        _, sa = _silu_parts(a)
        a_ref[...] = a.astype(BF16)
        b_ref[...] = b.astype(BF16)
        u_ref[...] = (sa * b).astype(BF16)

    row = pl.BlockSpec((tm, D), lambda i, j: (i, 0))
    vec = pl.BlockSpec((1, D), lambda i, j: (0, 0))
    wsp = pl.BlockSpec((tn, D), lambda i, j: (j, 0))
    osp = pl.BlockSpec((tm, tn), lambda i, j: (i, j))
    return _hosted_call(
        body, name=name, grid=(S // tm, FF // tn),
        in_specs=[row, vec, vec, vec, wsp, wsp],
        out_specs=[row, osp, osp, osp],
        out_shape=[jax.ShapeDtypeStruct((S, D), BF16)] + [jax.ShapeDtypeStruct((S, FF), BF16)] * 3,
        scratch_shapes=[pltpu.VMEM((tm, D), BF16)],
        semantics=("parallel", "arbitrary"), args=(x, ng, sh, sc, w1, w3), comm=comm)


def _ffn_down(u, w2, x, gate, *, name):
    S = x.shape[0]
    tm = _row_tile(S, 512)

    def body(u_ref, w2_ref, x_ref, g_ref, xo_ref, f_ref):
        f = jnp.dot(u_ref[...], w2_ref[...], preferred_element_type=F32)
        xo_ref[...] = x_ref[...] + (0.5 * g_ref[...]) * f
        f_ref[...] = f.astype(BF16)

    return pl.pallas_call(
        body, name=name, grid=(S // tm,),
        in_specs=[pl.BlockSpec((tm, FF), lambda i: (i, 0)), pl.BlockSpec((FF, D), lambda i: (0, 0)),
                  pl.BlockSpec((tm, D), lambda i: (i, 0)), pl.BlockSpec((1, D), lambda i: (0, 0))],
        out_specs=[pl.BlockSpec((tm, D), lambda i: (i, 0))] * 2,
        out_shape=[jax.ShapeDtypeStruct((S, D), F32), jax.ShapeDtypeStruct((S, D), BF16)],
        compiler_params=_params(("parallel",)),
    )(u, w2, x, gate)


def _ffn_bwd_du(dx, gate, f, w2, a, b, *, name, comm=None):
    S = dx.shape[0]
    tm, tn = _row_tile(S, 512), FF // 2
    n_i = S // tm

    def body(dx_ref, g_ref, f_ref, w_ref, a_ref, b_ref, df_ref, da_ref, db_ref, dg_ref, dfs):
        i, j = pl.program_id(0), pl.program_id(1)

        @pl.when((i == 0) & (j == 0))
        def _():
            dg_ref[...] = jnp.zeros_like(dg_ref)

        @pl.when(j == 0)
        def _():
            dxv = dx_ref[...]
            dfb = (dxv * (0.5 * g_ref[...])).astype(BF16)
            dfs[...] = dfb
            df_ref[...] = dfb
            dg_ref[...] += _rsum8(dxv * (0.5 * f_ref[...].astype(F32)))

        du = _dot_nt(dfs[...], w_ref[pl.ds(pl.multiple_of(j * tn, tn), tn), :])
        av = a_ref[...].astype(F32)
        sg, sa = _silu_parts(av)
        da_ref[...] = (du * b_ref[...].astype(F32) * (sg * (1.0 + av * (1.0 - sg)))).astype(BF16)
        db_ref[...] = (du * sa).astype(BF16)

        @pl.when((i == n_i - 1) & (j == FF // tn - 1))
        def _():
            _all_rows(dg_ref)

    row = pl.BlockSpec((tm, D), lambda i, j: (i, 0))
    blk = pl.BlockSpec((tm, tn), lambda i, j: (i, j))
    return _hosted_call(
        body, name=name, grid=(n_i, FF // tn),
        in_specs=[row, pl.BlockSpec((1, D), lambda i, j: (0, 0)), row,
                  pl.BlockSpec((FF, D), lambda i, j: (0, 0)), blk, blk],
        out_specs=[row, blk, blk, pl.BlockSpec((8, D), lambda i, j: (0, 0))],
        out_shape=[jax.ShapeDtypeStruct((S, D), BF16), jax.ShapeDtypeStruct((S, FF), BF16),
                   jax.ShapeDtypeStruct((S, FF), BF16), jax.ShapeDtypeStruct((8, D), F32)],
        scratch_shapes=[pltpu.VMEM((tm, D), BF16)],
        semantics=("arbitrary", "arbitrary"), args=(dx, gate, f, w2, a, b), comm=comm)


def _tn_matmul(a, b, *, tm, tn, name, comm=None):
    S, M = a.shape
    N = b.shape[1]
    ts = _row_tile(S, 1024)
    ns = S // ts

    def body(a_ref, b_ref, o_ref, acc):
        s = pl.program_id(2)
        p = lax.dot_general(a_ref[...], b_ref[...], (((0,), (0,)), ((), ())), preferred_element_type=F32)

        @pl.when(s == 0)
        def _():
            acc[...] = p

        @pl.when(s > 0)
        def _():
            acc[...] += p

        @pl.when(s == ns - 1)
        def _():
            o_ref[...] = acc[...]

    (out,), couts = _hosted_call(
        body, name=name, grid=(M // tm, N // tn, ns),
        in_specs=[pl.BlockSpec((ts, tm), lambda i, j, s: (s, i)), pl.BlockSpec((ts, tn), lambda i, j, s: (s, j))],
        out_specs=[pl.BlockSpec((tm, tn), lambda i, j, s: (i, j))],
        out_shape=[jax.ShapeDtypeStruct((M, N), F32)],
        scratch_shapes=[pltpu.VMEM((tm, tn), F32)],
        semantics=("parallel", "parallel", "arbitrary"), args=(a, b), comm=comm)
    return out if comm is None else (out, couts)


def _dh_normbwd(pairs, x, ng, sc, dx_next, *, name, comm=None):
    S = x.shape[0]
    tm = _row_tile(S, 256)
    n_i = S // tm
    n_p = len(pairs)

    def body(*refs):
        a_refs, w_refs = refs[:n_p], refs[n_p:2 * n_p]
        x_ref, g_ref, sc_ref, dxn_ref, dx_ref, p_ref = refs[2 * n_p:]
        i = pl.program_id(0)
        dh = jnp.dot(a_refs[0][...], w_refs[0][...], preferred_element_type=F32)
        for k in range(1, n_p):
            dh = dh + jnp.dot(a_refs[k][...], w_refs[k][...], preferred_element_type=F32)
        xv = x_ref[...]
        r = _rms(xv)
        xh = xv * r
        g = g_ref[...]
        dn = dh * (1.0 + sc_ref[...])
        dy = dn * g
        dx_ref[...] = dxn_ref[...] + r * (dy - xh * jnp.mean(dy * xh, axis=-1, keepdims=True))

        @pl.when(i == 0)
        def _():
            p_ref[...] = jnp.zeros_like(p_ref)

        p_ref[:, 0:D] += _rsum8(dh * (xh * g))
        p_ref[:, D:2 * D] += _rsum8(dh)
        p_ref[:, 2 * D:3 * D] += _rsum8(dn * xh)

        @pl.when(i == n_i - 1)
        def _():
            _all_rows(p_ref)

    row = pl.BlockSpec((tm, D), lambda i: (i, 0))
    vec = pl.BlockSpec((1, D), lambda i: (0, 0))
    in_specs = ([pl.BlockSpec((tm, a.shape[1]), lambda i: (i, 0)) for a, _ in pairs]
                + [pl.BlockSpec(w.shape, lambda i: (0, 0)) for _, w in pairs] + [row, vec, vec, row])
    return _hosted_call(
        body, name=name, grid=(n_i,), in_specs=in_specs,
        out_specs=[row, pl.BlockSpec((8, 3 * D), lambda i: (0, 0))],
        out_shape=[jax.ShapeDtypeStruct((S, D), F32), jax.ShapeDtypeStruct((8, 3 * D), F32)],
        scratch_shapes=[], semantics=("arbitrary",),
        args=(*[a for a, _ in pairs], *[w for _, w in pairs], x, ng, sc, dx_next), comm=comm)


def _final_loss(x3, gfin, tgt):
    S = x3.shape[0]
    tm = _row_tile(S, 512)
    n_i = S // tm

    def body(x_ref, g_ref, t_ref, dx_ref, dg_ref, loss_ref, lacc):
        i = pl.program_id(0)
        xv = x_ref[...]
        r = _rms(xv)
        xh = xv * r
        g = g_ref[...]
        e = xh * g - t_ref[...]
        dout = e * (1.0 / D)
        dy = dout * g
        dx_ref[...] = r * (dy - xh * jnp.mean(dy * xh, axis=-1, keepdims=True))

        @pl.when(i == 0)
        def _():
            dg_ref[...] = jnp.zeros_like(dg_ref)
            lacc[...] = jnp.zeros_like(lacc)

        dg_ref[...] += _rsum8(dout * xh)
        lacc[...] += _rsum8(e * e)

        @pl.when(i == n_i - 1)
        def _():
            _all_rows(dg_ref)
            tot = jnp.sum(jnp.sum(lacc[...], axis=0, keepdims=True), axis=1, keepdims=True)
            loss_ref[...] = jnp.broadcast_to(tot * (0.5 / D), loss_ref.shape)

    row = pl.BlockSpec((tm, D), lambda i: (i, 0))
    return pl.pallas_call(
        body, name="final_loss", grid=(n_i,),
        in_specs=[row, pl.BlockSpec((1, D), lambda i: (0, 0)), row],
        out_specs=[row, pl.BlockSpec((8, D), lambda i: (0, 0)), pl.BlockSpec((8, 128), lambda i: (0, 0))],
        out_shape=[jax.ShapeDtypeStruct((S, D), F32), jax.ShapeDtypeStruct((8, D), F32),
                   jax.ShapeDtypeStruct((8, 128), F32)],
        scratch_shapes=[pltpu.VMEM((8, D), F32)],
        compiler_params=_params(("arbitrary",)),
    )(x3, gfin, tgt)


def _mix_in(x, ng, sh, sc, w_in):
    S = x.shape[0]
    tm = _row_tile(S, 512)

    def body(x_ref, g_ref, sh_ref, sc_ref, w_ref, h_ref, z_ref):
        xv = x_ref[...]
        hb = (((xv * _rms(xv)) * g_ref[...]) * (1.0 + sc_ref[...]) + sh_ref[...]).astype(BF16)
        h_ref[...] = hb
        z_ref[...] = _dot_nt(hb, w_ref[...])

    row = pl.BlockSpec((tm, D), lambda i: (i, 0))
    vec = pl.BlockSpec((1, D), lambda i: (0, 0))
    return pl.pallas_call(
        body, name="mix_in", grid=(S // tm,),
        in_specs=[row, vec, vec, vec, pl.BlockSpec((Z_COLS, D), lambda i: (0, 0))],
        out_specs=[row, pl.BlockSpec((tm, Z_COLS), lambda i: (i, 0))],
        out_shape=[jax.ShapeDtypeStruct((S, D), BF16), jax.ShapeDtypeStruct((S, Z_COLS), F32)],
        compiler_params=_params(("parallel",)),
    )(x, ng, sh, sc, w_in)


def _conv_taps(u, halo, rows):
    u1 = jnp.where(rows == 0, halo[7:8, :], pltpu.roll(u, 1, 0))
    u2 = jnp.where(rows == 0, halo[6:7, :], jnp.where(rows == 1, halo[7:8, :], pltpu.roll(u, 2, 0)))
    return u1, u2


def _mix_mid(z, conv_w, gq, gkv, wuq, wukv, cs, sn):
    S = z.shape[0]
    tm = _row_tile(S, 512)
    hb = tm // 8

    def body(z_ref, zh_ref, cw_ref, gq_ref, gkv_ref, wuq_ref, wukv_ref, cs_ref, sn_ref,
             ya_ref, q_ref, k_ref, v_ref, cqn_ref, ckvn_ref):
        i = pl.program_id(0)
        xb = z_ref[:, 0:CONV_W]
        u = z_ref[:, CONV_W:2 * CONV_W] * z_ref[:, 2 * CONV_W:3 * CONV_W]
        halo = zh_ref[:, CONV_W:2 * CONV_W] * zh_ref[:, 2 * CONV_W:3 * CONV_W]
        halo = jnp.where(i > 0, halo, 0.0)
        rows = lax.broadcasted_iota(jnp.int32, (tm, CONV_W), 0)
        u1, u2 = _conv_taps(u, halo, rows)
        y = cw_ref[0:1, :] * u2 + cw_ref[1:2, :] * u1 + cw_ref[2:3, :] * u
        ya_ref[...] = xb * y

        lane = lax.broadcasted_iota(jnp.int32, (tm, 128), 1)
        cs_v, sn_v = cs_ref[...], sn_ref[...]
        cq = z_ref[:, 3 * CONV_W:3 * CONV_W + Q_LORA]
        cqn = ((cq * _rms(cq)) * gq_ref[...]).astype(BF16)
        cqn_ref[...] = cqn
        q = _dot_nt(cqn, wuq_ref[...])
        for h in range(HEADS):
            o = h * HEAD_SLOT
            q_ref[:, o:o + 128] = q[:, o:o + 128].astype(BF16)
            q_ref[:, o + 128:o + 256] = _rope(q[:, o + 128:o + 256], cs_v, sn_v, lane).astype(BF16)

        c0 = 3 * CONV_W + Q_LORA
        ckv = z_ref[:, c0:c0 + KV_LORA]
        ckvn = ((ckv * _rms(ckv)) * gkv_ref[...]).astype(BF16)
        ckvn_ref[...] = ckvn
        kv = jnp.dot(ckvn, wukv_ref[...], preferred_element_type=F32)
        krot = _rope(z_ref[:, c0 + KV_LORA:Z_COLS], cs_v, sn_v, lane).astype(BF16)
        for h in range(HEADS):
            o = h * HEAD_SLOT
            k_ref[:, o:o + 128] = kv[:, h * 128:(h + 1) * 128].astype(BF16)
            k_ref[:, o + 128:o + 256] = krot
        v_ref[...] = kv[:, HEADS * 128:].astype(BF16)

    def rows_of(n):
        return pl.BlockSpec((tm, n), lambda i: (i, 0))

    def whole(shape):
        return pl.BlockSpec(shape, lambda i: (0, 0))

    return pl.pallas_call(
        body, name="mix_mid", grid=(S // tm,),
        in_specs=[rows_of(Z_COLS), pl.BlockSpec((8, Z_COLS), lambda i: (jnp.maximum(i * hb - 1, 0), 0)),
                  whole((8, CONV_W)), whole((1, Q_LORA)), whole((1, KV_LORA)),
                  whole((HEADS * HEAD_SLOT, Q_LORA)), whole((KV_LORA, 2 * HEADS * 128)),
                  rows_of(128), rows_of(128)],
        out_specs=[rows_of(CONV_W), rows_of(HEADS * HEAD_SLOT), rows_of(HEADS * HEAD_SLOT), rows_of(HEADS * V_HEAD),
                   rows_of(Q_LORA), rows_of(KV_LORA)],
        out_shape=[jax.ShapeDtypeStruct((S, CONV_W), F32), jax.ShapeDtypeStruct((S, HEADS * HEAD_SLOT), BF16),
                   jax.ShapeDtypeStruct((S, HEADS * HEAD_SLOT), BF16), jax.ShapeDtypeStruct((S, HEADS * V_HEAD), BF16),
                   jax.ShapeDtypeStruct((S, Q_LORA), BF16), jax.ShapeDtypeStruct((S, KV_LORA), BF16)],
        compiler_params=_params(("parallel",)),
    )(z, z, conv_w, gq, gkv, wuq, wukv, cs, sn)


def _att_blocks(S):
    bk = min(1024, max(S // 4, 128))
    return bk, bk


def _pair_tables(S, k_major):
    bq, bk = _att_blocks(S)
    nq, nk = S // bq, S // bk
    vis = lambda qi, ki: ki * bk < (qi + 1) * bq
    if k_major:
        pairs = [(qi, ki) for ki in range(nk) for qi in range(nq) if vis(qi, ki)]
    else:
        pairs = [(qi, ki) for qi in range(nq) for ki in range(nk) if vis(qi, ki)]
    cols = [[p[0] for p in pairs], [p[1] for p in pairs], [int((p[1] + 1) * bk > p[0] * bq) for p in pairs]]
    return [jnp.asarray(np.array(c, np.int32)) for c in cols], len(pairs)


def _chunk_mask(qi, ki, bq, bk):
    r = (qi * bq + lax.broadcasted_iota(jnp.int32, (bq, bk), 0)) // CHUNK
    c = (ki * bk + lax.broadcasted_iota(jnp.int32, (bq, bk), 1)) // CHUNK
    return c <= r


def _attention(q, k, v):
    S = q.shape[0]
    bq, bk = _att_blocks(S)
    last_k = bq // bk - 1
    tables, n_pairs = _pair_tables(S, k_major=False)

    def body(qi_ref, ki_ref, mk_ref, q_ref, k_ref, v_ref, o_ref, lse_ref, m_s, l_s, acc_s):
        p_id = pl.program_id(1)
        qi, ki = qi_ref[p_id], ki_ref[p_id]

        @pl.when(ki == 0)
        def _():
            m_s[...] = jnp.full_like(m_s, NEG)
            l_s[...] = jnp.zeros_like(l_s)
            acc_s[...] = jnp.zeros_like(acc_s)

        def update(masked):
            s = lax.dot_general(q_ref[...], k_ref[...], (((1,), (1,)), ((), ())), preferred_element_type=F32)
            if masked:
                s = jnp.where(_chunk_mask(qi, ki, bq, bk), s, NEG)
            m_prev = m_s[...]
            m_new = jnp.maximum(m_prev, jnp.max(s, axis=1, keepdims=True))
            alpha = jnp.exp2((m_prev - m_new) * EXP2_SCALE)
            p = jnp.exp2((s - jnp.tile(m_new, (1, bk // 128))) * EXP2_SCALE)
            l_s[...] = alpha * l_s[...] + jnp.sum(p, axis=1, keepdims=True)
            acc_s[...] = alpha * acc_s[...] + jnp.dot(p.astype(BF16), v_ref[...], preferred_element_type=F32)
            m_s[...] = m_new

        @pl.when(mk_ref[p_id] == 0)
        def _():
            update(False)

        @pl.when(mk_ref[p_id] == 1)
        def _():
            update(True)

        @pl.when(ki == qi * (last_k + 1) + last_k)
        def _():
            l = l_s[...]
            o_ref[...] = acc_s[...] / l
            lse_ref[...] = m_s[...] * EXP2_SCALE + jnp.log2(l)

    grid_spec = pltpu.PrefetchScalarGridSpec(
        num_scalar_prefetch=3, grid=(HEADS, n_pairs),
        in_specs=[pl.BlockSpec((bq, HEAD_SLOT), lambda h, p, qt, kt, mt: (qt[p], h)),
                  pl.BlockSpec((bk, HEAD_SLOT), lambda h, p, qt, kt, mt: (kt[p], h)),
                  pl.BlockSpec((bk, V_HEAD), lambda h, p, qt, kt, mt: (kt[p], h))],
        out_specs=[pl.BlockSpec((bq, V_HEAD), lambda h, p, qt, kt, mt: (qt[p], h))] * 2,
        scratch_shapes=[pltpu.VMEM((bq, V_HEAD), F32)] * 3)
    return pl.pallas_call(
        body, name="attention", grid_spec=grid_spec,
        out_shape=[jax.ShapeDtypeStruct((S, HEADS * V_HEAD), F32)] * 2,
        compiler_params=_params(("arbitrary", "arbitrary")),
    )(*tables, q, k, v)


def _attention_bwd(q, k, v, do, lse2, delta):
    S = q.shape[0]
    bq, bk = _att_blocks(S)
    nq = S // bq
    tables, n_pairs = _pair_tables(S, k_major=True)

    def body(qi_ref, ki_ref, mk_ref, q_ref, k_ref, v_ref, do_ref, lse_ref, dl_ref, dq_hbm, dk_ref, dv_ref,
             dq_s, dk_s, dv_s, sem):
        head, p_id = pl.program_id(0), pl.program_id(1)
        qi, ki = qi_ref[p_id], ki_ref[p_id]
        rows = pl.ds(pl.multiple_of(qi * bq, bq), bq)

        @pl.when(qi * bq <= ki * bk)
        def _():
            dk_s[...] = jnp.zeros_like(dk_s)
            dv_s[...] = jnp.zeros_like(dv_s)

        def update(masked):
            qv, kv, dov = q_ref[...], k_ref[...], do_ref[...]
            s = lax.dot_general(qv, kv, (((1,), (1,)), ((), ())), preferred_element_type=F32)
            dp = lax.dot_general(dov, v_ref[...], (((1,), (1,)), ((), ())), preferred_element_type=F32)
            if masked:
                s = jnp.where(_chunk_mask(qi, ki, bq, bk), s, NEG)
            p = jnp.exp2(s * EXP2_SCALE - jnp.tile(lse_ref[...], (1, bk // 128)))
            dv_s[...] += lax.dot_general(p.astype(BF16), dov, (((0,), (0,)), ((), ())), preferred_element_type=F32)
            ds = (p * (dp - jnp.tile(dl_ref[...], (1, bk // 128)))).astype(BF16)
            dk_s[...] += lax.dot_general(ds, qv, (((0,), (0,)), ((), ())), preferred_element_type=F32)
            dq = jnp.dot(ds, kv, preferred_element_type=F32)

            @pl.when(ki == 0)
            def _():
                dq_s[rows, :] = dq

            @pl.when(ki > 0)
            def _():
                dq_s[rows, :] += dq

        @pl.when(mk_ref[p_id] == 0)
        def _():
            update(False)

        @pl.when(mk_ref[p_id] == 1)
        def _():
            update(True)

        @pl.when(qi == nq - 1)
        def _():
            dk_ref[...] = dk_s[...] * ATT_SCALE
            dv_ref[...] = dv_s[...]

        @pl.when(p_id == n_pairs - 1)
        def _():
            dq_s[...] = dq_s[...] * ATT_SCALE
            out = pltpu.make_async_copy(
                dq_s, dq_hbm.at[:, pl.ds(pl.multiple_of(head * HEAD_SLOT, HEAD_SLOT), HEAD_SLOT)], sem)
            out.start()
            out.wait()

    grid_spec = pltpu.PrefetchScalarGridSpec(
        num_scalar_prefetch=3, grid=(HEADS, n_pairs),
        in_specs=[pl.BlockSpec((bq, HEAD_SLOT), lambda h, p, qt, kt, mt: (qt[p], h)),
                  pl.BlockSpec((bk, HEAD_SLOT), lambda h, p, qt, kt, mt: (kt[p], h)),
                  pl.BlockSpec((bk, V_HEAD), lambda h, p, qt, kt, mt: (kt[p], h)),
                  pl.BlockSpec((bq, V_HEAD), lambda h, p, qt, kt, mt: (qt[p], h)),
                  pl.BlockSpec((bq, V_HEAD), lambda h, p, qt, kt, mt: (qt[p], h)),
                  pl.BlockSpec((bq, V_HEAD), lambda h, p, qt, kt, mt: (qt[p], h))],
        out_specs=[pl.BlockSpec(memory_space=pl.ANY),
                   pl.BlockSpec((bk, HEAD_SLOT), lambda h, p, qt, kt, mt: (kt[p], h)),
                   pl.BlockSpec((bk, V_HEAD), lambda h, p, qt, kt, mt: (kt[p], h))],
        scratch_shapes=[pltpu.VMEM((S, HEAD_SLOT), F32), pltpu.VMEM((bk, HEAD_SLOT), F32),
                        pltpu.VMEM((bk, V_HEAD), F32), pltpu.SemaphoreType.DMA])
    return pl.pallas_call(
        body, name="attention_bwd", grid_spec=grid_spec,
        out_shape=[jax.ShapeDtypeStruct((S, HEADS * HEAD_SLOT), F32), jax.ShapeDtypeStruct((S, HEADS * HEAD_SLOT), F32),
                   jax.ShapeDtypeStruct((S, HEADS * V_HEAD), F32)],
        compiler_params=_params(("arbitrary", "arbitrary")),
    )(*tables, q, k, v, do, lse2, delta)


def _group_mats():
    def blockdiag(n, g):
        idx = np.arange(n) // g
        return jnp.asarray((idx[:, None] == idx[None, :]).astype(np.float32), dtype=BF16)
    return blockdiag(CONV_W, CONV_GROUP), blockdiag(HEADS * V_HEAD, V_HEAD)


def _mix_out(ya, o, gout, w_out, x, gate, ga, gb):
    S = x.shape[0]
    tm = _row_tile(S, 512)

    def body(ya_ref, o_ref, go_ref, w_ref, x_ref, g_ref, ga_ref, gb_ref, xo_ref, yn_ref, yo_ref):
        yav, ov = ya_ref[...], o_ref[...]
        ra = lax.rsqrt(_gsum(yav * yav, ga_ref[...]) * (1.0 / CONV_GROUP) + EPS)
        rb = lax.rsqrt(_gsum(ov * ov, gb_ref[...]) * (1.0 / V_HEAD) + EPS)
        na = ((yav * ra) * go_ref[:, 0:CONV_W]).astype(BF16)
        nb = ((ov * rb) * go_ref[:, CONV_W:]).astype(BF16)
        yn_ref[:, 0:CONV_W] = na
        yn_ref[:, CONV_W:] = nb
        yo = (jnp.dot(na, w_ref[0:CONV_W, :], preferred_element_type=F32)
              + jnp.dot(nb, w_ref[CONV_W:, :], preferred_element_type=F32))
        xo_ref[...] = x_ref[...] + g_ref[...] * yo
        yo_ref[...] = yo.astype(BF16)

    row = pl.BlockSpec((tm, D), lambda i: (i, 0))
    half = pl.BlockSpec((tm, CONV_W), lambda i: (i, 0))
    vec = pl.BlockSpec((1, D), lambda i: (0, 0))
    sq = pl.BlockSpec((CONV_W, CONV_W), lambda i: (0, 0))
    return pl.pallas_call(
        body, name="mix_out", grid=(S // tm,),
        in_specs=[half, half, vec, pl.BlockSpec((D, D), lambda i: (0, 0)), row, vec, sq, sq],
        out_specs=[row, row, row],
        out_shape=[jax.ShapeDtypeStruct((S, D), F32), jax.ShapeDtypeStruct((S, D), BF16),
                   jax.ShapeDtypeStruct((S, D), BF16)],
        compiler_params=_params(("parallel",)),
    )(ya, o, gout, w_out, x, gate, ga, gb)


def _mix_out_bwd(dx, gate, yo, w_out, ya, o, gout, ga, gb, comm=None):
    S = dx.shape[0]
    tm = _row_tile(S, 256)
    n_i = S // tm

    def norm_bwd(v, dn, gain, gmat, inv_n):
        r = lax.rsqrt(_gsum(v * v, gmat) * inv_n + EPS)
        vh = v * r
        dy = dn * gain
        return r * (dy - vh * (_gsum(dy * vh, gmat) * inv_n)), dn * vh

    def body(dx_ref, g_ref, yo_ref, w_ref, ya_ref, o_ref, go_ref, ga_ref, gb_ref,
             dyo_ref, dya_ref, do_ref, dl_ref, p_ref):
        i = pl.program_id(0)
        dxv = dx_ref[...]
        dyo = (dxv * g_ref[...]).astype(BF16)
        dyo_ref[...] = dyo
        dyn = _dot_nt(dyo, w_ref[...])
        dya, dga = norm_bwd(ya_ref[...], dyn[:, 0:CONV_W], go_ref[:, 0:CONV_W], ga_ref[...], 1.0 / CONV_GROUP)
        ov = o_ref[...]
        do, dgb = norm_bwd(ov, dyn[:, CONV_W:], go_ref[:, CONV_W:], gb_ref[...], 1.0 / V_HEAD)
        dya_ref[...] = dya
        do_ref[...] = do.astype(BF16)
        dl_ref[...] = _gsum(do * ov, gb_ref[...], split=True)

        @pl.when(i == 0)
        def _():
            p_ref[...] = jnp.zeros_like(p_ref)

        p_ref[:, 0:D] += _rsum8(dxv * yo_ref[...].astype(F32))
        p_ref[:, D:D + CONV_W] += _rsum8(dga)
        p_ref[:, D + CONV_W:2 * D] += _rsum8(dgb)

        @pl.when(i == n_i - 1)
        def _():
            _all_rows(p_ref)

    row = pl.BlockSpec((tm, D), lambda i: (i, 0))
    half = pl.BlockSpec((tm, CONV_W), lambda i: (i, 0))
    vec = pl.BlockSpec((1, D), lambda i: (0, 0))
    sq = pl.BlockSpec((CONV_W, CONV_W), lambda i: (0, 0))
    return _hosted_call(
        body, name="mix_out_bwd", grid=(n_i,),
        in_specs=[row, vec, row, pl.BlockSpec((D, D), lambda i: (0, 0)), half, half, vec, sq, sq],
        out_specs=[row, half, half, half, pl.BlockSpec((8, 2 * D), lambda i: (0, 0))],
        out_shape=[jax.ShapeDtypeStruct((S, D), BF16), jax.ShapeDtypeStruct((S, CONV_W), F32),
                   jax.ShapeDtypeStruct((S, CONV_W), BF16), jax.ShapeDtypeStruct((S, CONV_W), F32),
                   jax.ShapeDtypeStruct((8, 2 * D), F32)],
        scratch_shapes=[], semantics=("arbitrary",), args=(dx, gate, yo, w_out, ya, o, gout, ga, gb), comm=comm)


MID_SUMS = 3 * CONV_W + Q_LORA + KV_LORA


def _mix_mid_bwd(z, dya, conv_w, gq, gkv, wuq, wukv, cs, sn, dq, dk, dv, comm=None):
    S = z.shape[0]
    tm = _row_tile(S, 256)
    n_i = S // tm
    hb = tm // 8
    last_blk = S // 8 - 1

    def latent_bwd(cv, dcn, gain):
        r = _rms(cv)
        ch = cv * r
        dy = dcn * gain
        return r * (dy - ch * jnp.mean(dy * ch, axis=-1, keepdims=True)), dcn * ch

    def body(z_ref, zp_ref, zn_ref, dya_ref, dyan_ref, cw_ref, gq_ref, gkv_ref, wuq_ref, wukv_ref, cs_ref, sn_ref,
             dq_ref, dk_ref, dv_ref, dz_ref, dqf_ref, dkvf_ref, p_ref):
        i = pl.program_id(0)
        xb, xc, xu = z_ref[:, 0:CONV_W], z_ref[:, CONV_W:2 * CONV_W], z_ref[:, 2 * CONV_W:3 * CONV_W]
        u = xc * xu
        halo = jnp.where(i > 0, zp_ref[:, CONV_W:2 * CONV_W] * zp_ref[:, 2 * CONV_W:3 * CONV_W], 0.0)
        rows = lax.broadcasted_iota(jnp.int32, (tm, CONV_W), 0)
        u1, u2 = _conv_taps(u, halo, rows)
        w0, w1, w2 = cw_ref[0:1, :], cw_ref[1:2, :], cw_ref[2:3, :]
        y = w0 * u2 + w1 * u1 + w2 * u
        dyav = dya_ref[...]
        dy = dyav * xb
        nxt = jnp.where(i < n_i - 1, dyan_ref[...] * zn_ref[:, 0:CONV_W], 0.0)
        dy1 = jnp.where(rows == tm - 1, nxt[0:1, :], pltpu.roll(dy, tm - 1, 0))
        dy2 = jnp.where(rows == tm - 1, nxt[1:2, :], jnp.where(rows == tm - 2, nxt[0:1, :], pltpu.roll(dy, tm - 2, 0)))
        du = w2 * dy + w1 * dy1 + w0 * dy2
        dz_ref[:, 0:CONV_W] = (dyav * y).astype(BF16)
        dz_ref[:, CONV_W:2 * CONV_W] = (du * xu).astype(BF16)
        dz_ref[:, 2 * CONV_W:3 * CONV_W] = (du * xc).astype(BF16)

        lane = lax.broadcasted_iota(jnp.int32, (tm, 128), 1)
        cs_v, sn_v = cs_ref[...], sn_ref[...]
        dkr = jnp.zeros((tm, 128), F32)
        for h in range(HEADS):
            o = h * HEAD_SLOT
            dqf_ref[:, o:o + 128] = dq_ref[:, o:o + 128].astype(BF16)
            dqf_ref[:, o + 128:o + 256] = _rope_t(dq_ref[:, o + 128:o + 256], cs_v, sn_v, lane).astype(BF16)
            dkvf_ref[:, h * 128:(h + 1) * 128] = dk_ref[:, o:o + 128].astype(BF16)
            dkr = dkr + dk_ref[:, o + 128:o + 256]
        dkvf_ref[:, HEADS * 128:] = dv_ref[...].astype(BF16)

        c0 = 3 * CONV_W
        dcqn = jnp.dot(dqf_ref[...], wuq_ref[...], preferred_element_type=F32)
        dcq, dgq = latent_bwd(z_ref[:, c0:c0 + Q_LORA], dcqn, gq_ref[...])
        dz_ref[:, c0:c0 + Q_LORA] = dcq.astype(BF16)
        c1 = c0 + Q_LORA
        dckvn = _dot_nt(dkvf_ref[...], wukv_ref[...])
        dckv, dgkv = latent_bwd(z_ref[:, c1:c1 + KV_LORA], dckvn, gkv_ref[...])
        dz_ref[:, c1:c1 + KV_LORA] = dckv.astype(BF16)
        dz_ref[:, c1 + KV_LORA:Z_COLS] = _rope_t(dkr, cs_v, sn_v, lane).astype(BF16)

        @pl.when(i == 0)
        def _():
            p_ref[...] = jnp.zeros_like(p_ref)

        p_ref[:, 0:CONV_W] += _rsum8(dy * u2)
        p_ref[:, CONV_W:2 * CONV_W] += _rsum8(dy * u1)
        p_ref[:, 2 * CONV_W:3 * CONV_W] += _rsum8(dy * u)
        p_ref[:, c0:c0 + Q_LORA] += _rsum8(dgq)
        p_ref[:, c1:c1 + KV_LORA] += _rsum8(dgkv)

        @pl.when(i == n_i - 1)
        def _():
            _all_rows(p_ref)

    def rows_of(n):
        return pl.BlockSpec((tm, n), lambda i: (i, 0))

    def whole(shape):
        return pl.BlockSpec(shape, lambda i: (0, 0))

    def prev8(n):
        return pl.BlockSpec((8, n), lambda i: (jnp.maximum(i * hb - 1, 0), 0))

    def next8(n):
        return pl.BlockSpec((8, n), lambda i: (jnp.minimum((i + 1) * hb, last_blk), 0))

    return _hosted_call(
        body, name="mix_mid_bwd", grid=(n_i,),
        in_specs=[rows_of(Z_COLS), prev8(Z_COLS), next8(Z_COLS), rows_of(CONV_W), next8(CONV_W),
                  whole((8, CONV_W)), whole((1, Q_LORA)), whole((1, KV_LORA)),
                  whole((HEADS * HEAD_SLOT, Q_LORA)), whole((KV_LORA, 2 * HEADS * 128)),
                  rows_of(128), rows_of(128),
                  rows_of(HEADS * HEAD_SLOT), rows_of(HEADS * HEAD_SLOT), rows_of(HEADS * V_HEAD)],
        out_specs=[rows_of(Z_COLS), rows_of(HEADS * HEAD_SLOT), rows_of(2 * HEADS * 128), whole((8, MID_SUMS))],
        out_shape=[jax.ShapeDtypeStruct((S, Z_COLS), BF16), jax.ShapeDtypeStruct((S, HEADS * HEAD_SLOT), BF16),
                   jax.ShapeDtypeStruct((S, 2 * HEADS * 128), BF16), jax.ShapeDtypeStruct((8, MID_SUMS), F32)],
        scratch_shapes=[], semantics=("arbitrary",),
        args=(z, z, z, dya, dya, conv_w, gq, gkv, wuq, wukv, cs, sn, dq, dk, dv), comm=comm)


ADA_Q = N_MOD * D // N_CHIPS
ADA_TN = 768


def _ada_forward(c_all, ada_w_q, ada_b_q):
    def body(c_ref, w_ref, b_ref, o_ref):
        cv = c_ref[...]
        sc = (cv * jax.nn.sigmoid(cv)).astype(BF16)
        o_ref[...] = jnp.dot(sc, w_ref[...].astype(BF16), preferred_element_type=F32) + b_ref[...]

    return pl.pallas_call(
        body, name="ada_forward", grid=(ADA_Q // ADA_TN,),
        in_specs=[pl.BlockSpec((16, D), lambda j: (0, 0)), pl.BlockSpec((D, ADA_TN), lambda j: (0, j)),
                  pl.BlockSpec((1, ADA_TN), lambda j: (0, j))],
        out_specs=pl.BlockSpec((16, ADA_TN), lambda j: (0, j)),
        out_shape=jax.ShapeDtypeStruct((16, ADA_Q), F32),
        compiler_params=_params(("parallel",)),
    )(c_all, ada_w_q, ada_b_q)


def _ada_wgrad(c_all, dmod_q):
    def body(c_ref, d_ref, o_ref):
        cv = c_ref[...]
        sc = (cv * jax.nn.sigmoid(cv)).astype(BF16)
        o_ref[...] = lax.dot_general(sc, d_ref[...].astype(BF16), (((0,), (0,)), ((), ())),
                                     preferred_element_type=F32)

    return pl.pallas_call(
        body, name="ada_wgrad", grid=(ADA_Q // ADA_TN,),
        in_specs=[pl.BlockSpec((16, D), lambda j: (0, 0)), pl.BlockSpec((16, ADA_TN), lambda j: (0, j))],
        out_specs=pl.BlockSpec((D, ADA_TN), lambda j: (0, j)),
        out_shape=jax.ShapeDtypeStruct((D, ADA_Q), F32),
        compiler_params=_params(("parallel",)),
    )(c_all, dmod_q)


def _sum_devices(parts):
    n = parts.shape[1]

    def body(p_ref, o_ref):
        o_ref[...] = jnp.broadcast_to(jnp.sum(p_ref[...], axis=0, keepdims=True), o_ref.shape)

    return pl.pallas_call(
        body, name="sum_devices",
        in_specs=[pl.BlockSpec((N_DEV, n), lambda: (0, 0))], out_specs=pl.BlockSpec((N_DEV, n), lambda: (0, 0)),
        out_shape=jax.ShapeDtypeStruct((N_DEV, n), F32),
    )(parts)


def _adamw(w, g, m, v, *, name):
    _, rows, cols = w.shape
    tr = _row_tile(rows, 256)

    def body(w_ref, g_ref, m_ref, v_ref, go_ref, d_ref, mo_ref, vo_ref):
        gv = g_ref[...]
        mn = B1 * m_ref[0] + (1.0 - B1) * gv
        vn = B2 * v_ref[0] + (1.0 - B2) * (gv * gv)
        m_hat = mn / (1.0 - B1 ** STEP)
        v_hat = vn / (1.0 - B2 ** STEP)
        go_ref[0] = gv
        d_ref[0] = -LR * (m_hat / (jnp.sqrt(v_hat) + AEPS) + WD * w_ref[0])
        mo_ref[0] = mn
        vo_ref[0] = vn

    blk = pl.BlockSpec((1, tr, cols), lambda i: (0, i, 0))
    return pl.pallas_call(
        body, name=name, grid=(rows // tr,),
        in_specs=[blk, pl.BlockSpec((tr, cols), lambda i: (i, 0)), blk, blk], out_specs=[blk] * 4,
        out_shape=[jax.ShapeDtypeStruct((1, rows, cols), F32)] * 4,
        compiler_params=_params(("parallel",)),
    )(w, g, m, v)


def _small_allgather(v, *, name):
    m, n = v.shape

    def body(x_ref, out_ref, send_sems, recv_sems, local_sem):
        x, y, c = _place()
        me, sibling = (x, y, c), (x, y, 1 - c)
        chips = [(1 - x, y), (x, 1 - y), (1 - x, 1 - y)]

        def rows(px, py, pc):
            return out_ref.at[pl.ds((4 * px + 2 * py + pc) * m, m), :]

        def copy(k, block, to, src=None):
            return pltpu.make_async_remote_copy(
                src_ref=rows(*block) if src is None else src, dst_ref=rows(*block),
                send_sem=send_sems.at[k], recv_sem=recv_sems.at[k], device_id=to, device_id_type=MESH)

        mine = pltpu.make_async_copy(x_ref, rows(*me), local_sem)
        mine.start()
        first = [copy(0, me, sibling, src=x_ref)]
        first += [copy(1 + j, me, (*chip, c), src=x_ref) for j, chip in enumerate(chips)]
        for cp in first:
            cp.start()
        passed = [copy(4 + j, (*chip, c), sibling) for j, chip in enumerate(chips)]
        for j, chip in enumerate(chips):
            copy(1 + j, (*chip, c), me).wait_recv()
            passed[j].start()
        copy(0, sibling, me).wait_recv()
        for j, chip in enumerate(chips):
            copy(4 + j, (*chip, 1 - c), me).wait_recv()
        for cp in first + passed:
            cp.wait_send()
        mine.wait()

    return pl.pallas_call(
        body, name=name,
        out_shape=jax.ShapeDtypeStruct((N_DEV * m, n), v.dtype),
        in_specs=[pl.BlockSpec(memory_space=pltpu.VMEM)], out_specs=pl.BlockSpec(memory_space=pltpu.VMEM),
        scratch_shapes=[pltpu.SemaphoreType.DMA((7,)), pltpu.SemaphoreType.DMA((7,)), pltpu.SemaphoreType.DMA],
    )(v)


ADD_BLOCKS = 2


def _pair_add(place, gs, ts, *, name):
    n_a = len(gs)

    def body(pl_ref, *refs):
        g_refs, t_refs = refs[:n_a], refs[n_a:2 * n_a]
        pf_refs, pb_refs = refs[2 * n_a:3 * n_a], refs[3 * n_a:]
        for g_ref, t_ref, pf_ref, pb_ref in zip(g_refs, t_refs, pf_refs, pb_refs):
            s = g_ref[...] + t_ref[...]
            pf_ref[...] = s
            pb_ref[...] = s.astype(BF16)

    def blk(t, own_half):
        tr = t.shape[1] // ADD_BLOCKS
        if own_half:
            return pl.BlockSpec((1, tr, t.shape[2]), lambda q, r, p: (q, p[0] * ADD_BLOCKS + r, 0))
        return pl.BlockSpec((1, tr, t.shape[2]), lambda q, r, p: (q, r, 0))

    grid_spec = pltpu.PrefetchScalarGridSpec(
        num_scalar_prefetch=1, grid=(N_CHIPS, ADD_BLOCKS),
        in_specs=[blk(t, True) for t in ts] + [blk(t, False) for t in ts],
        out_specs=[blk(t, False) for t in ts] * 2)
    res = pl.pallas_call(
        body, name=name, grid_spec=grid_spec,
        out_shape=[jax.ShapeDtypeStruct(t.shape, F32) for t in ts] + [jax.ShapeDtypeStruct(t.shape, BF16) for t in ts],
        compiler_params=_params(("parallel", "parallel")),
    )(place, *gs, *ts)
    return list(res[:n_a]), list(res[n_a:])


def _chip_add(place, pfs, ts, *, name):
    n_a = len(pfs)

    def body(pl_ref, *refs):
        pf_refs, t_refs, o_refs = refs[:n_a], refs[n_a:4 * n_a], refs[4 * n_a:]
        for i, (pf_ref, o_ref) in enumerate(zip(pf_refs, o_refs)):
            t1, t2, t3 = t_refs[3 * i:3 * i + 3]
            o_ref[...] = ((pf_ref[0] + t1[0].astype(F32)) + t2[0].astype(F32)) + t3[0].astype(F32)

    def slot(t, j):
        return pl.BlockSpec((1, t.shape[1] // ADD_BLOCKS, t.shape[2]), lambda r, p: (p[1] ^ j, r, 0))

    grid_spec = pltpu.PrefetchScalarGridSpec(
        num_scalar_prefetch=1, grid=(ADD_BLOCKS,),
        in_specs=[slot(t, 0) for t in pfs] + [slot(t, j) for t in ts for j in (1, 2, 3)],
        out_specs=[pl.BlockSpec((t.shape[1] // ADD_BLOCKS, t.shape[2]), lambda r, p: (r, 0)) for t in pfs])
    res = pl.pallas_call(
        body, name=name, grid_spec=grid_spec,
        out_shape=[jax.ShapeDtypeStruct(t.shape[1:], F32) for t in pfs],
        compiler_params=_params(("parallel",)),
    )(place, *pfs, *[t for t in ts for _ in range(3)])
    return list(res)


BULK = [("ffn1_w1", "colsT"), ("ffn1_w3", "colsT"), ("ffn1_w2", "rows"), ("w_in", "cols"), ("w_uq", "colsT"),
        ("w_ukv", "cols"), ("w_out", "rows"), ("ffn2_w1", "colsT"), ("ffn2_w3", "colsT"), ("ffn2_w2", "rows")]
KIND = dict(BULK)


def _group(*names):
    return [b for b in BULK if b[0] in names]


W_FIRST = _group("ffn1_w1", "ffn1_w3")
W_REST = [b for b in BULK if b not in W_FIRST]
G_FFN2 = _group("ffn2_w1", "ffn2_w3", "ffn2_w2")
G_MIX = _group("w_in", "w_uq", "w_ukv", "w_out")
G_FFN1 = _group("ffn1_w1", "ffn1_w3", "ffn1_w2")


def _gathered_weights(specs, shards, got, myq):
    out = {}
    for (name, kind), part in zip(specs, got):
        part = lax.dynamic_update_slice_in_dim(part, shards[name][None], myq, axis=0)
        out[name] = _full_weight(part, kind)
    return out


def _working_shard(w, kind):
    return jnp.swapaxes(w, 1, 2)[0] if kind == "colsT" else w[0]


def _full_weight(parts, kind):
    if kind == "cols":
        return jnp.transpose(parts, (1, 0, 2)).reshape(parts.shape[1], -1)
    return parts.reshape(-1, parts.shape[2])


def _quarters(g, kind):
    if kind == "cols":
        k, n = g.shape
        return jnp.transpose(g.reshape(k, N_CHIPS, n // N_CHIPS), (1, 0, 2))
    return g.reshape(N_CHIPS, g.shape[0] // N_CHIPS, g.shape[1])


def _pad_heads(w_uq_t):
    w = w_uq_t.reshape(HEADS, QK_NOPE + QK_ROPE, Q_LORA)
    return jnp.pad(w, ((0, 0), (0, HEAD_SLOT - QK_NOPE - QK_ROPE), (0, 0))).reshape(HEADS * HEAD_SLOT, Q_LORA)


def _unpad_heads(g):
    return g.reshape(HEADS, HEAD_SLOT, Q_LORA)[:, :QK_NOPE + QK_ROPE].reshape(HEADS * (QK_NOPE + QK_ROPE), Q_LORA)


def _split_kv(w_ukv):
    return jnp.transpose(w_ukv.reshape(KV_LORA, HEADS, 2, 128), (0, 2, 1, 3)).reshape(KV_LORA, 2 * HEADS * 128)


def _merge_kv(g):
    return jnp.transpose(g.reshape(KV_LORA, 2, HEADS, 128), (0, 2, 1, 3)).reshape(KV_LORA, 2 * HEADS * 128)


def _rope_tables(positions):
    inv_freq = ROPE_THETA ** (-jnp.arange(0, QK_ROPE, 2, dtype=F32) / QK_ROPE)
    ang = positions.astype(F32)[:, None] * inv_freq
    cos, sin, zero = jnp.cos(ang), jnp.sin(ang), jnp.zeros((positions.shape[0], 64), F32)
    return jnp.concatenate([cos, cos, zero], axis=1), jnp.concatenate([sin, sin, zero], axis=1)


def _reduce_tail(place, specs, pfs, t2s, tag, host=None):
    rhs = _chip_add(place, pfs, t2s, name=tag + "_chip_add")
    if host is None:
        out, others = None, _run_comm(_PairShare(rhs), name=tag + "_pair_share")
    else:
        out, others = host(_PairShare(rhs))
    return out, _assemble(place, specs, rhs, others)


def _assemble(place, specs, rhs, others):
    south = place[0] == 0
    return {b[0]: jnp.concatenate([jnp.where(south, rh, ot), jnp.where(south, ot, rh)], axis=0)
            for b, rh, ot in zip(specs, rhs, others)}


def _local_step(x, positions, target, mod, vec, conv_w, w_first, rest_shards, place):
    row = lambda k: mod[k:k + 1]
    sh1, sc1, g1, sh2, sc2, g2, sh3, sc3, g3 = [row(k) for k in range(N_MOD)]
    cs, sn = _rope_tables(positions)
    cw8 = jnp.pad(conv_w, ((0, 5), (0, 0)))
    ga, gb = _group_mats()
    dist = place is not None
    comm = lambda prog: prog if dist else None

    gather = _Gather([rest_shards[b[0]] for b in W_REST]) if dist else None
    (h1, a1, b1, u1), got = _ffn_up(x, vec["norm_ffn1_g"], sh1, sc1, w_first["ffn1_w1"], w_first["ffn1_w3"],
                                    name="ffn1_up", comm=gather)
    if dist:
        w = _gathered_weights(W_REST, rest_shards, got, place[1])
    else:
        w = dict(rest_shards)
    w.update(w_first)
    w_in = jnp.pad(w["w_in"].T, ((0, Z_COLS - IN_COLS), (0, 0)))
    wuq = _pad_heads(w["w_uq"])
    wukv = _split_kv(w["w_ukv"])
    x1, f1 = _ffn_down(u1, w["ffn1_w2"], x, g1, name="ffn1_down")
    h2, z = _mix_in(x1, vec["norm_mix_g"], sh2, sc2, w_in)
    ya, q, k, v, cqn, ckvn = _mix_mid(z, cw8, vec["q_norm_g"], vec["kv_norm_g"], wuq, wukv, cs, sn)
    o, lse = _attention(q, k, v)
    x2, yn, yo = _mix_out(ya, o, vec["out_norm_g"], w["w_out"], x1, g2, ga, gb)
    (h3, a3, b3, u3), _ = _ffn_up(x2, vec["norm_ffn2_g"], sh3, sc3, w["ffn2_w1"], w["ffn2_w3"], name="ffn2_up")
    x3, f3 = _ffn_down(u3, w["ffn2_w2"], x2, g3, name="ffn2_down")
    dx3, dgfin, loss_blk = _final_loss(x3, vec["final_norm_g"], target)

    grads, reduced = {}, {}

    def tn(a, b, tm, tn_, name, prog=None):
        if prog is None:
            return _tn_matmul(a, b, tm=tm, tn=tn_, name=name), None
        return _tn_matmul(a, b, tm=tm, tn=tn_, name=name, comm=prog)

    def slab_of(specs):
        return [_quarters(grads[n], kind) for n, kind in specs]

    (df3, da3, db3, dg3), _ = _ffn_bwd_du(dx3, g3, f3, w["ffn2_w2"], a3, b3, name="ffn2_bwd_du")
    grads["ffn2_w2"], _ = tn(u3, df3, FF // 2, D, "ffn2_dw2")
    grads["ffn2_w1"], _ = tn(da3, h3, FF // 2, D, "ffn2_dw1")
    grads["ffn2_w3"], _ = tn(db3, h3, FF // 2, D, "ffn2_dw3")
    (dx2, s3), _ = _dh_normbwd([(da3, w["ffn2_w1"]), (db3, w["ffn2_w3"])], x2, vec["norm_ffn2_g"], sc3, dx3,
                               name="ffn2_bwd_dh")

    p1 = slab_of(G_FFN2) if dist else None
    (dyo, dya, do, delta, s_out), t1 = _mix_out_bwd(dx2, g2, yo, w["w_out"], ya, o, vec["out_norm_g"], ga, gb,
                                                    comm=comm(_PairExchange(p1) if dist else None))
    grads["w_out"], _ = tn(yn, dyo, D, D, "dw_out")
    if dist:
        pf1, pb1 = _pair_add(place, p1, t1, name="ffn2g_pair_add")
    dq, dk, dv = _attention_bwd(q, k, v, do, lse, delta)
    (dz, dqf, dkvf, s_mid), t2 = _mix_mid_bwd(z, dya, cw8, vec["q_norm_g"], vec["kv_norm_g"], wuq, wukv, cs, sn,
                                              dq, dk, dv, comm=comm(_ChipExchange(pb1) if dist else None))
    g_uq, _ = tn(dqf, cqn, HEADS * HEAD_SLOT, Q_LORA, "dw_uq")
    g_ukv, _ = tn(ckvn, dkvf, KV_LORA, 2 * HEADS * 128, "dw_ukv")
    grads["w_uq"], grads["w_ukv"] = _unpad_heads(g_uq), _merge_kv(g_ukv)
    if dist:
        g_in, red = _reduce_tail(place, G_FFN2, pf1, t2, "ffn2g",
                                 host=lambda prog: tn(h2, dz, D, Z_COLS // 2, "dw_in", prog))
        reduced.update(red)
    else:
        g_in, _ = tn(h2, dz, D, Z_COLS // 2, "dw_in")
    grads["w_in"] = g_in[:, :IN_COLS]
    (dx1, s2), _ = _dh_normbwd([(dz, w_in)], x1, vec["norm_mix_g"], sc2, dx2, name="mix_bwd_dh")

    p2 = slab_of(G_MIX) if dist else None
    (df1, da1, db1, dg1), t1 = _ffn_bwd_du(dx1, g1, f1, w["ffn1_w2"], a1, b1, name="ffn1_bwd_du",
                                           comm=comm(_PairExchange(p2) if dist else None))
    dh_pairs = [(da1, w["ffn1_w1"]), (db1, w["ffn1_w3"])]
    if dist:
        pf2, pb2 = _pair_add(place, p2, t1, name="mixg_pair_add")
        grads["ffn1_w2"], t2 = tn(u1, df1, FF // 2, D, "ffn1_dw2", _ChipExchange(pb2))
        rh_mix = _chip_add(place, pf2, t2, name="mixg_chip_add")
        q_w2 = [_quarters(grads["ffn1_w2"], KIND["ffn1_w2"])]
        grads["ffn1_w1"], got = tn(da1, h1, FF // 2, D, "ffn1_dw1", _Multi([_PairShare(rh_mix), _PairExchange(q_w2)]))
        reduced.update(_assemble(place, G_MIX, rh_mix, got[:len(rh_mix)]))
        pf_w2, pb_w2 = _pair_add(place, q_w2, got[len(rh_mix):], name="ffn1w2_pair_add")
        q_w1 = [_quarters(grads["ffn1_w1"], KIND["ffn1_w1"])]
        grads["ffn1_w3"], got = tn(db1, h1, FF // 2, D, "ffn1_dw3", _Multi([_ChipExchange(pb_w2), _PairExchange(q_w1)]))
        t2_w2 = got[:1]
        pf_w1, pb_w1 = _pair_add(place, q_w1, got[1:], name="ffn1w1_pair_add")
        q_w3 = [_quarters(grads["ffn1_w3"], KIND["ffn1_w3"])]
        (dx0, s1), got = _dh_normbwd(dh_pairs, x, vec["norm_ffn1_g"], sc1, dx1, name="ffn1_bwd_dh",
                                     comm=_Multi([_ChipExchange(pb_w1), _PairExchange(q_w3)]))
        t2_w1 = got[:1]
        pf_w3, pb_w3 = _pair_add(place, q_w3, got[1:], name="ffn1w3_pair_add")
        t2_w3 = _run_comm(_ChipExchange(pb_w3), name="ffn1w3_chip_exchange")
        rh = _chip_add(place, pf_w1 + pf_w3 + pf_w2, t2_w1 + t2_w3 + t2_w2, name="ffn1g_chip_add")
        reduced.update(_assemble(place, G_FFN1, rh, _run_comm(_PairShare(rh), name="ffn1g_pair_share")))
    else:
        grads["ffn1_w2"], _ = tn(u1, df1, FF // 2, D, "ffn1_dw2")
        grads["ffn1_w1"], _ = tn(da1, h1, FF // 2, D, "ffn1_dw1")
        grads["ffn1_w3"], _ = tn(db1, h1, FF // 2, D, "ffn1_dw3")
        (dx0, s1), _ = _dh_normbwd(dh_pairs, x, vec["norm_ffn1_g"], sc1, dx1, name="ffn1_bwd_dh")
        reduced = grads

    def part(s, k):
        return s[0:1, k * D:(k + 1) * D]

    dmod = jnp.concatenate([part(s1, 1), part(s1, 0), dg1[0:1], part(s2, 1), part(s2, 0), part(s_out, 0),
                            part(s3, 1), part(s3, 0), dg3[0:1]], axis=1)
    small = {"norm_ffn1_g": part(s1, 2), "norm_mix_g": part(s2, 2), "out_norm_g": part(s_out, 1),
             "norm_ffn2_g": part(s3, 2), "final_norm_g": dgfin[0:1],
             "q_norm_g": s_mid[0:1, 3 * CONV_W:3 * CONV_W + Q_LORA],
             "kv_norm_g": s_mid[0:1, 3 * CONV_W + Q_LORA:MID_SUMS], "conv_w": s_mid[0:1, 0:3 * CONV_W]}
    return loss_blk, dx0, reduced, dmod, small


SMALL = [("norm_ffn1_g", D), ("norm_mix_g", D), ("out_norm_g", D), ("norm_ffn2_g", D), ("final_norm_g", D),
         ("q_norm_g", Q_LORA), ("kv_norm_g", KV_LORA), ("conv_w", 3 * CONV_W)]
WEIGHTS = ['ada_w', 'ada_b', 'norm_ffn1_g', 'ffn1_w1', 'ffn1_w3', 'ffn1_w2', 'norm_mix_g', 'w_in', 'conv_w',
           'q_norm_g', 'w_uq', 'kv_norm_g', 'w_ukv', 'out_norm_g', 'w_out', 'norm_ffn2_g', 'ffn2_w1', 'ffn2_w3',
           'ffn2_w2', 'final_norm_g']


def kernel(x, c, positions, ada_w, ada_b, norm_ffn1_g, ffn1_w1, ffn1_w3, ffn1_w2, norm_mix_g, w_in, conv_w, q_norm_g, w_uq, kv_norm_g, w_ukv, out_norm_g, w_out, norm_ffn2_g, ffn2_w1, ffn2_w3, ffn2_w2, final_norm_g, loss_target, m_ada_w, m_ada_b, m_norm_ffn1_g, m_ffn1_w1, m_ffn1_w3, m_ffn1_w2, m_norm_mix_g, m_w_in, m_conv_w, m_q_norm_g, m_w_uq, m_kv_norm_g, m_w_ukv, m_out_norm_g, m_w_out, m_norm_ffn2_g, m_ffn2_w1, m_ffn2_w3, m_ffn2_w2, m_final_norm_g, v_ada_w, v_ada_b, v_norm_ffn1_g, v_ffn1_w1, v_ffn1_w3, v_ffn1_w2, v_norm_mix_g, v_w_in, v_conv_w, v_q_norm_g, v_w_uq, v_kv_norm_g, v_w_ukv, v_out_norm_g, v_w_out, v_norm_ffn2_g, v_ffn2_w1, v_ffn2_w3, v_ffn2_w2, v_final_norm_g):
    args = dict(locals())
    wts = {n: args[n] for n in WEIGHTS}
    mom = {n: args["m_" + n] for n in WEIGHTS}
    var = {n: args["v_" + n] for n in WEIGHTS}
    ax, ay, ac = _place()
    myq = 2 * ax + ay
    me = 2 * myq + ac
    place = jnp.stack([ac, myq]).astype(jnp.int32)

    shards = {name: _working_shard(wts[name], kind).astype(BF16) for name, kind in BULK}
    first = _run_comm(_Gather([shards[b[0]] for b in W_FIRST]), name="gather_ffn1")
    w_first = _gathered_weights(W_FIRST, shards, first, myq)

    mine = jnp.concatenate([c, conv_w[0].reshape(1, 3 * CONV_W // N_CHIPS)], axis=1)
    seen = _small_allgather(jnp.pad(mine, ((0, 7), (0, 0))), name="gather_cond").reshape(N_DEV, 8, -1)[:, 0]
    c_all = jnp.pad(seen[:, :D], ((0, 8), (0, 0)))
    conv_full = jnp.transpose(seen[0::2, D:].reshape(N_CHIPS, 3, CONV_W // N_CHIPS), (1, 0, 2)).reshape(3, CONV_W)
    ada_b_q = lax.dynamic_slice_in_dim(ada_b, myq * ADA_Q, ADA_Q, axis=1)
    mod_q = _ada_forward(c_all, ada_w[0], ada_b_q)
    mod_all = _small_allgather(mod_q, name="gather_mod").reshape(N_DEV, 16, ADA_Q)
    mod_rows = jnp.transpose(mod_all[0::2, :N_DEV], (1, 0, 2)).reshape(N_DEV, N_MOD * D)
    mod = lax.dynamic_slice_in_dim(mod_rows, me, 1, axis=0).reshape(N_MOD, D)

    vec = {n: wts[n] for n in ("norm_ffn1_g", "norm_mix_g", "q_norm_g", "kv_norm_g", "out_norm_g", "norm_ffn2_g")}
    vec["final_norm_g"] = final_norm_g.reshape(1, D)
    loss_blk, grad_x, gq, dmod, small = _local_step(x[0], positions[0], loss_target[0], mod, vec, conv_full, w_first,
                                                    {b[0]: shards[b[0]] for b in W_REST}, place)
    loss = lax.psum(loss_blk[0, 0], ("x", "y", "c"))

    rows = jnp.concatenate([dmod] + [small[n] for n, _ in SMALL], axis=1)
    width = rows.shape[1]
    fold = -(-width // (8 * 128)) * 128
    rows = jnp.pad(rows, ((0, 0), (0, 8 * fold - width))).reshape(8, fold)
    every = _small_allgather(rows, name="gather_small").reshape(N_DEV, 8 * fold)[:, :width]
    total = _sum_devices(every)[0:1]
    dmod_q = lax.dynamic_slice_in_dim(every[:, :N_MOD * D], myq * ADA_Q, ADA_Q, axis=1)
    g = {name: gq[name] for name, *_ in BULK}
    g["ada_w"] = _ada_wgrad(c_all, jnp.pad(dmod_q, ((0, 8), (0, 0))))
    g["ada_b"] = total[:, :N_MOD * D]
    off = N_MOD * D
    for n, width in SMALL:
        g[n] = total[:, off:off + width]
        off += width
    g["conv_w"] = lax.dynamic_slice_in_dim(g["conv_w"].reshape(3, CONV_W), myq * (CONV_W // N_CHIPS),
                                           CONV_W // N_CHIPS, axis=1)
    g["final_norm_g"] = g["final_norm_g"].reshape(D)

    delta, new_m, new_v = {}, {}, {}
    for name in ["ada_w"] + [b[0] for b in BULK]:
        view = (lambda a: jnp.swapaxes(a, 1, 2)) if KIND.get(name) == "colsT" else (lambda a: a)
        g[name], delta[name], new_m[name], new_v[name] = [
            view(r) for r in _adamw(view(wts[name]), g[name], view(mom[name]), view(var[name]), name="adamw_" + name)]
    smalls = ["ada_b"] + [n for n, _ in SMALL]

    def packed(d):
        flat = jnp.concatenate([d[n].reshape(1, -1) for n in smalls], axis=1)
        return jnp.pad(flat.reshape(-1, D), ((0, 1), (0, 0)))

    res = _adamw(packed(wts)[None], packed(g), packed(mom)[None], packed(var)[None], name="adamw_small")[1:]
    off = 0
    for n in smalls:
        size = wts[n].size
        for d, r in zip((delta, new_m, new_v), res):
            d[n] = r.reshape(-1)[off:off + size].reshape(wts[n].shape)
        g[n] = g[n].reshape(wts[n].shape)
        off += size

    return (loss, grad_x[None], *[g[n] for n in WEIGHTS], *[delta[n] for n in WEIGHTS],
            *[new_m[n] for n in WEIGHTS], *[new_v[n] for n in WEIGHTS])
```

```python
import functools

import numpy as np
import jax
import jax.numpy as jnp
from jax import lax
from jax.experimental import pallas as pl
from jax.experimental.pallas import tpu as pltpu

F32 = jnp.float32
BF16 = jnp.bfloat16
MESH = pl.DeviceIdType.MESH

D = 1024
FF = 2816
CONV_W = 512
CONV_GROUP = 64
HEADS = 4
QK_NOPE = 128
QK_ROPE = 64
V_HEAD = 128
Q_LORA = 384
KV_LORA = 256
HEAD_SLOT = 256
IN_COLS = 3 * CONV_W + Q_LORA + KV_LORA + QK_ROPE
Z_COLS = 2304
EPS = 1e-6
ROPE_THETA = 10000.0
CHUNK = 64
ATT_SCALE = (QK_NOPE + QK_ROPE) ** -0.5
NEG = -1e30
EXP2_SCALE = ATT_SCALE * 1.4426950408889634
N_MOD = 9

LR, B1, B2, AEPS, WD, STEP = 0.001, 0.9, 0.999, 1e-08, 0.01, 10

N_CHIPS = 4
N_DEV = 8
VMEM_LIMIT = 56 << 20


def _params(sem, vmem=VMEM_LIMIT):
    return pltpu.CompilerParams(dimension_semantics=sem, vmem_limit_bytes=vmem)


def _rms(v):
    return lax.rsqrt(jnp.mean(v * v, axis=-1, keepdims=True) + EPS)


def _rsum8(v):
    t, n = v.shape
    return jnp.sum(v.reshape(t // 8, 8, n), axis=0)


def _all_rows(ref):
    ref[...] = jnp.broadcast_to(jnp.sum(ref[...], axis=0, keepdims=True), ref.shape)


def _gsum(v, gmat, split=False):
    hi = v.astype(BF16)
    out = jnp.dot(hi, gmat, preferred_element_type=F32)
    if split:
        out = out + jnp.dot((v - hi.astype(F32)).astype(BF16), gmat, preferred_element_type=F32)
    return out


def _dot_nt(a, b):
    return lax.dot_general(a, b, (((1,), (1,)), ((), ())), preferred_element_type=F32)


def _silu_parts(a):
    sg = jax.nn.sigmoid(a)
    return sg, a * sg


def _rope(xr, cs, sn, lane):
    rh = jnp.where(lane < 32, -pltpu.roll(xr, 96, 1), pltpu.roll(xr, 32, 1))
    return xr * cs + rh * sn


def _rope_t(g, cs, sn, lane):
    y = g * sn
    rt = jnp.where(lane < 32, pltpu.roll(y, 96, 1), jnp.where(lane < 64, -pltpu.roll(y, 32, 1), 0.0))
    return g * cs + rt


def _row_tile(rows, pref, mult=8):
    t = min(rows, pref) // mult * mult
    while rows % t:
        t -= mult
    return t


def _place():
    return lax.axis_index("x"), lax.axis_index("y"), lax.axis_index("c")


ANY = pl.BlockSpec(memory_space=pl.ANY)


def _hosted_call(body, *, name, grid, in_specs, out_specs, out_shape, scratch_shapes, semantics, args, comm=None):
    n_in, n_out, n_scr = len(in_specs), len(out_specs), len(scratch_shapes)
    if comm is None:
        res = pl.pallas_call(body, name=name, grid=grid, in_specs=in_specs, out_specs=out_specs, out_shape=out_shape,
                             scratch_shapes=scratch_shapes, compiler_params=_params(semantics))(*args)
        return list(res), []
    n_ci, n_co = len(comm.inputs), len(comm.out_shapes)
    total = int(np.prod(grid))

    def hosted(*refs):
        ins, refs = refs[:n_in], refs[n_in:]
        cins, refs = refs[:n_ci], refs[n_ci:]
        outs, refs = refs[:n_out], refs[n_out:]
        couts, refs = refs[:n_co], refs[n_co:]
        scratch, sems = refs[:n_scr], refs[n_scr]
        step = pl.program_id(0)
        for ax in range(1, len(grid)):
            step = step * grid[ax] + pl.program_id(ax)

        @pl.when(step == 0)
        def _():
            comm.start(cins, couts, sems)

        body(*ins, *outs, *scratch)

        @pl.when(step == total - 1)
        def _():
            comm.finish(cins, couts, sems)

    res = pl.pallas_call(
        hosted, name=name, grid=grid, in_specs=list(in_specs) + [ANY] * n_ci,
        out_specs=list(out_specs) + [ANY] * n_co, out_shape=list(out_shape) + list(comm.out_shapes),
        scratch_shapes=list(scratch_shapes) + [pltpu.SemaphoreType.DMA((comm.n_sems,))],
        compiler_params=_params(("arbitrary",) * len(grid)))(*args, *comm.inputs)
    return list(res[:n_out]), list(res[n_out:])


def _run_comm(comm, *, name):
    n_ci = len(comm.inputs)

    def body(*refs):
        cins, couts, sems = refs[:n_ci], refs[n_ci:-1], refs[-1]
        comm.start(cins, couts, sems)
        comm.finish(cins, couts, sems)

    return list(pl.pallas_call(
        body, name=name, out_shape=list(comm.out_shapes), in_specs=[ANY] * n_ci,
        out_specs=[ANY] * len(comm.out_shapes), scratch_shapes=[pltpu.SemaphoreType.DMA((comm.n_sems,))],
    )(*comm.inputs))


class _Gather:
    def __init__(self, slabs):
        self.inputs = list(slabs)
        self.out_shapes = [jax.ShapeDtypeStruct((N_CHIPS,) + s.shape, s.dtype) for s in slabs]
        self.n_sems = 12 * len(slabs)

    @staticmethod
    def _copy(out, sems, base, k, chip, hc, to, src=None):
        H = out.shape[1] // 2
        half = out.at[2 * chip[0] + chip[1], pl.ds(hc * H, H), :]
        return pltpu.make_async_remote_copy(
            src_ref=half if src is None else src, dst_ref=half, send_sem=sems.at[base + k],
            recv_sem=sems.at[base + 6 + k], device_id=to, device_id_type=MESH)

    def _firsts(self, src, out, sems, base):
        x, y, c = _place()
        H = src.shape[0] // 2
        chips = [(1 - x, y), (x, 1 - y), (1 - x, 1 - y)]
        return [self._copy(out, sems, base, j, (x, y), c, (*chip, c), src=src.at[pl.ds(c * H, H), :])
                for j, chip in enumerate(chips)]

    def start(self, ins, outs, sems):
        for i, (src, out) in enumerate(zip(ins, outs)):
            for cp in self._firsts(src, out, sems, 12 * i):
                cp.start()

    def finish(self, ins, outs, sems):
        x, y, c = _place()
        chips = [(1 - x, y), (x, 1 - y), (1 - x, 1 - y)]
        passed = []
        for i, out in enumerate(outs):
            for j, chip in enumerate(chips):
                self._copy(out, sems, 12 * i, j, chip, c, (x, y, c)).wait_recv()
                cp = self._copy(out, sems, 12 * i, 3 + j, chip, c, (x, y, 1 - c))
                cp.start()
                passed.append(cp)
        for i, out in enumerate(outs):
            for j, chip in enumerate(chips):
                self._copy(out, sems, 12 * i, 3 + j, chip, 1 - c, (x, y, c)).wait_recv()
        for cp in passed:
            cp.wait_send()
        for i, (src, out) in enumerate(zip(ins, outs)):
            for cp in self._firsts(src, out, sems, 12 * i):
                cp.wait_send()


class _PairExchange:
    def __init__(self, arrays):
        self.inputs = list(arrays)
        self.out_shapes = [jax.ShapeDtypeStruct((N_CHIPS, a.shape[1] // 2, a.shape[2]), a.dtype) for a in arrays]
        self.n_sems = 2 * len(arrays)

    def _copies(self, ins, outs, sems):
        x, y, c = _place()
        return [pltpu.make_async_remote_copy(
            src_ref=g.at[:, pl.ds((1 - c) * t.shape[1], t.shape[1]), :], dst_ref=t, send_sem=sems.at[2 * i],
            recv_sem=sems.at[2 * i + 1], device_id=(x, y, 1 - c), device_id_type=MESH)
            for i, (g, t) in enumerate(zip(ins, outs))]

    def start(self, ins, outs, sems):
        for cp in self._copies(ins, outs, sems):
            cp.start()

    def finish(self, ins, outs, sems):
        for cp in self._copies(ins, outs, sems):
            cp.wait()


class _ChipExchange:
    def __init__(self, arrays):
        self.inputs = list(arrays)
        self.out_shapes = [jax.ShapeDtypeStruct(a.shape, a.dtype) for a in arrays]
        self.n_sems = 6 * len(arrays)

    def _copies(self, p, t, sems, base):
        x, y, c = _place()
        myq = 2 * x + y
        chips = [(1 - x, y), (x, 1 - y), (1 - x, 1 - y)]
        sends = [pltpu.make_async_remote_copy(
            src_ref=p.at[2 * chip[0] + chip[1]], dst_ref=t.at[myq], send_sem=sems.at[base + j],
            recv_sem=sems.at[base + 3 + j], device_id=(*chip, c), device_id_type=MESH) for j, chip in enumerate(chips)]
        lands = [pltpu.make_async_remote_copy(
            src_ref=t.at[2 * chip[0] + chip[1]], dst_ref=t.at[2 * chip[0] + chip[1]], send_sem=sems.at[base + j],
            recv_sem=sems.at[base + 3 + j], device_id=(*chip, c), device_id_type=MESH) for j, chip in enumerate(chips)]
        return sends, lands

    def start(self, ins, outs, sems):
        for i, (p, t) in enumerate(zip(ins, outs)):
            for cp in self._copies(p, t, sems, 6 * i)[0]:
                cp.start()

    def finish(self, ins, outs, sems):
        for i, (p, t) in enumerate(zip(ins, outs)):
            sends, lands = self._copies(p, t, sems, 6 * i)
            for cp in lands:
                cp.wait_recv()
            for cp in sends:
                cp.wait_send()


class _SemView:
    def __init__(self, sems, base):
        self._sems, self._base = sems, base

    @property
    def at(self):
        return self

    def __getitem__(self, k):
        return self._sems.at[self._base + k]


class _Multi:
    def __init__(self, progs):
        self.progs = list(progs)
        self.inputs = [a for p in self.progs for a in p.inputs]
        self.out_shapes = [s for p in self.progs for s in p.out_shapes]
        self.n_sems = sum(p.n_sems for p in self.progs)

    def _each(self, ins, outs, sems):
        i = o = s = 0
        for p in self.progs:
            ni, no = len(p.inputs), len(p.out_shapes)
            yield p, ins[i:i + ni], outs[o:o + no], _SemView(sems, s)
            i, o, s = i + ni, o + no, s + p.n_sems

    def start(self, ins, outs, sems):
        for p, a, b, c in self._each(ins, outs, sems):
            p.start(a, b, c)

    def finish(self, ins, outs, sems):
        for p, a, b, c in self._each(ins, outs, sems):
            p.finish(a, b, c)


class _PairShare:
    def __init__(self, arrays):
        self.inputs = list(arrays)
        self.out_shapes = [jax.ShapeDtypeStruct(a.shape, a.dtype) for a in arrays]
        self.n_sems = 2 * len(arrays)

    def _copies(self, ins, outs, sems):
        x, y, c = _place()
        return [pltpu.make_async_remote_copy(
            src_ref=r, dst_ref=o, send_sem=sems.at[2 * i], recv_sem=sems.at[2 * i + 1],
            device_id=(x, y, 1 - c), device_id_type=MESH) for i, (r, o) in enumerate(zip(ins, outs))]

    def start(self, ins, outs, sems):
        for cp in self._copies(ins, outs, sems):
            cp.start()

    def finish(self, ins, outs, sems):
        for cp in self._copies(ins, outs, sems):
            cp.wait()


def _ffn_up(x, ng, sh, sc, w1, w3, *, name, comm=None):
    S = x.shape[0]
    tm, tn = _row_tile(S, 512), FF // 2

    def body(x_ref, g_ref, sh_ref, sc_ref, w1_ref, w3_ref, h_ref, a_ref, b_ref, u_ref, hs):
        @pl.when(pl.program_id(1) == 0)
        def _():
            xv = x_ref[...]
            h = ((xv * _rms(xv)) * g_ref[...]) * (1.0 + sc_ref[...]) + sh_ref[...]
            hb = h.astype(BF16)
            hs[...] = hb
            h_ref[...] = hb

        h = hs[...]
        a = _dot_nt(h, w1_ref[...])
        b = _dot_nt(h, w3_ref[...])
        _, sa = _silu_parts(a)
        a_ref[...] = a.astype(BF16)
        b_ref[...] = b.astype(BF16)
        u_ref[...] = (sa * b).astype(BF16)

    row = pl.BlockSpec((tm, D), lambda i, j: (i, 0))
    vec = pl.BlockSpec((1, D), lambda i, j: (0, 0))
    wsp = pl.BlockSpec((tn, D), lambda i, j: (j, 0))
    osp = pl.BlockSpec((tm, tn), lambda i, j: (i, j))
    return _hosted_call(
        body, name=name, grid=(S // tm, FF // tn),
        in_specs=[row, vec, vec, vec, wsp, wsp],
        out_specs=[row, osp, osp, osp],
        out_shape=[jax.ShapeDtypeStruct((S, D), BF16)] + [jax.ShapeDtypeStruct((S, FF), BF16)] * 3,
        scratch_shapes=[pltpu.VMEM((tm, D), BF16)],
        semantics=("parallel", "arbitrary"), args=(x, ng, sh, sc, w1, w3), comm=comm)


def _ffn_down(u, w2, x, gate, *, name):
    S = x.shape[0]
    tm = _row_tile(S, 512)

    def body(u_ref, w2_ref, x_ref, g_ref, xo_ref, f_ref):
        f = jnp.dot(u_ref[...], w2_ref[...], preferred_element_type=F32)
        xo_ref[...] = x_ref[...] + (0.5 * g_ref[...]) * f
        f_ref[...] = f.astype(BF16)

    return pl.pallas_call(
        body, name=name, grid=(S // tm,),
        in_specs=[pl.BlockSpec((tm, FF), lambda i: (i, 0)), pl.BlockSpec((FF, D), lambda i: (0, 0)),
                  pl.BlockSpec((tm, D), lambda i: (i, 0)), pl.BlockSpec((1, D), lambda i: (0, 0))],
        out_specs=[pl.BlockSpec((tm, D), lambda i: (i, 0))] * 2,
        out_shape=[jax.ShapeDtypeStruct((S, D), F32), jax.ShapeDtypeStruct((S, D), BF16)],
        compiler_params=_params(("parallel",)),
    )(u, w2, x, gate)


def _ffn_bwd_du(dx, gate, f, w2, a, b, *, name, comm=None):
    S = dx.shape[0]
    tm, tn = _row_tile(S, 512), FF // 2
    n_i = S // tm

    def body(dx_ref, g_ref, f_ref, w_ref, a_ref, b_ref, df_ref, da_ref, db_ref, dg_ref, dfs):
        i, j = pl.program_id(0), pl.program_id(1)

        @pl.when((i == 0) & (j == 0))
        def _():
            dg_ref[...] = jnp.zeros_like(dg_ref)

        @pl.when(j == 0)
        def _():
            dxv = dx_ref[...]
            dfb = (dxv * (0.5 * g_ref[...])).astype(BF16)
            dfs[...] = dfb
            df_ref[...] = dfb
            dg_ref[...] += _rsum8(dxv * (0.5 * f_ref[...].astype(F32)))

        du = _dot_nt(dfs[...], w_ref[pl.ds(pl.multiple_of(j * tn, tn), tn), :])
        av = a_ref[...].astype(F32)
        sg, sa = _silu_parts(av)
        da_ref[...] = (du * b_ref[...].astype(F32) * (sg * (1.0 + av * (1.0 - sg)))).astype(BF16)
        db_ref[...] = (du * sa).astype(BF16)

        @pl.when((i == n_i - 1) & (j == FF // tn - 1))
        def _():
            _all_rows(dg_ref)

    row = pl.BlockSpec((tm, D), lambda i, j: (i, 0))
    blk = pl.BlockSpec((tm, tn), lambda i, j: (i, j))
    return _hosted_call(
        body, name=name, grid=(n_i, FF // tn),
        in_specs=[row, pl.BlockSpec((1, D), lambda i, j: (0, 0)), row,
                  pl.BlockSpec((FF, D), lambda i, j: (0, 0)), blk, blk],
        out_specs=[row, blk, blk, pl.BlockSpec((8, D), lambda i, j: (0, 0))],
        out_shape=[jax.ShapeDtypeStruct((S, D), BF16), jax.ShapeDtypeStruct((S, FF), BF16),
                   jax.ShapeDtypeStruct((S, FF), BF16), jax.ShapeDtypeStruct((8, D), F32)],
        scratch_shapes=[pltpu.VMEM((tm, D), BF16)],
        semantics=("arbitrary", "arbitrary"), args=(dx, gate, f, w2, a, b), comm=comm)


def _tn_matmul(a, b, *, tm, tn, name, comm=None):
    S, M = a.shape
    N = b.shape[1]
    ts = _row_tile(S, 2048)
    ns = S // ts

    def body(a_ref, b_ref, o_ref):
        s = pl.program_id(2)
        p = lax.dot_general(a_ref[...], b_ref[...], (((0,), (0,)), ((), ())), preferred_element_type=F32)

        @pl.when(s == 0)
        def _():
            o_ref[...] = p

        @pl.when(s > 0)
        def _():
            o_ref[...] += p

    (out,), couts = _hosted_call(
        body, name=name, grid=(M // tm, N // tn, ns),
        in_specs=[pl.BlockSpec((ts, tm), lambda i, j, s: (s, i)), pl.BlockSpec((ts, tn), lambda i, j, s: (s, j))],
        out_specs=[pl.BlockSpec((tm, tn), lambda i, j, s: (i, j))],
        out_shape=[jax.ShapeDtypeStruct((M, N), F32)],
        scratch_shapes=[], semantics=("parallel", "parallel", "arbitrary"), args=(a, b), comm=comm)
    return out if comm is None else (out, couts)


def _dh_normbwd(pairs, x, ng, sc, dx_next, *, name, comm=None):
    S = x.shape[0]
    n_p = len(pairs)
    tm = _row_tile(S, 512 if n_p == 1 else 256)
    n_i = S // tm

    def body(*refs):
        a_refs, w_refs = refs[:n_p], refs[n_p:2 * n_p]
        x_ref, g_ref, sc_ref, dxn_ref, dx_ref, p_ref = refs[2 * n_p:]
        i = pl.program_id(0)
        dh = jnp.dot(a_refs[0][...], w_refs[0][...], preferred_element_type=F32)
        for k in range(1, n_p):
            dh = dh + jnp.dot(a_refs[k][...], w_refs[k][...], preferred_element_type=F32)
        xv = x_ref[...]
        r = _rms(xv)
        xh = xv * r
        g = g_ref[...]
        dn = dh * (1.0 + sc_ref[...])
        dy = dn * g
        dx_ref[...] = dxn_ref[...] + r * (dy - xh * jnp.mean(dy * xh, axis=-1, keepdims=True))

        @pl.when(i == 0)
        def _():
            p_ref[...] = jnp.zeros_like(p_ref)

        p_ref[:, 0:D] += _rsum8(dh * (xh * g))
        p_ref[:, D:2 * D] += _rsum8(dh)
        p_ref[:, 2 * D:3 * D] += _rsum8(dn * xh)

        @pl.when(i == n_i - 1)
        def _():
            _all_rows(p_ref)

    row = pl.BlockSpec((tm, D), lambda i: (i, 0))
    vec = pl.BlockSpec((1, D), lambda i: (0, 0))
    in_specs = ([pl.BlockSpec((tm, a.shape[1]), lambda i: (i, 0)) for a, _ in pairs]
                + [pl.BlockSpec(w.shape, lambda i: (0, 0)) for _, w in pairs] + [row, vec, vec, row])
    return _hosted_call(
        body, name=name, grid=(n_i,), in_specs=in_specs,
        out_specs=[row, pl.BlockSpec((8, 3 * D), lambda i: (0, 0))],
        out_shape=[jax.ShapeDtypeStruct((S, D), F32), jax.ShapeDtypeStruct((8, 3 * D), F32)],
        scratch_shapes=[], semantics=("arbitrary",),
        args=(*[a for a, _ in pairs], *[w for _, w in pairs], x, ng, sc, dx_next), comm=comm)


def _final_loss(x3, gfin, tgt):
    S = x3.shape[0]
    tm = _row_tile(S, 512)
    n_i = S // tm

    def body(x_ref, g_ref, t_ref, dx_ref, dg_ref, loss_ref, lacc):
        i = pl.program_id(0)
        xv = x_ref[...]
        r = _rms(xv)
        xh = xv * r
        g = g_ref[...]
        e = xh * g - t_ref[...]
        dout = e * (1.0 / D)
        dy = dout * g
        dx_ref[...] = r * (dy - xh * jnp.mean(dy * xh, axis=-1, keepdims=True))

        @pl.when(i == 0)
        def _():
            dg_ref[...] = jnp.zeros_like(dg_ref)
            lacc[...] = jnp.zeros_like(lacc)

        dg_ref[...] += _rsum8(dout * xh)
        lacc[...] += _rsum8(e * e)

        @pl.when(i == n_i - 1)
        def _():
            _all_rows(dg_ref)
            tot = jnp.sum(jnp.sum(lacc[...], axis=0, keepdims=True), axis=1, keepdims=True)
            loss_ref[...] = jnp.broadcast_to(tot * (0.5 / D), loss_ref.shape)

    row = pl.BlockSpec((tm, D), lambda i: (i, 0))
    return pl.pallas_call(
        body, name="final_loss", grid=(n_i,),
        in_specs=[row, pl.BlockSpec((1, D), lambda i: (0, 0)), row],
        out_specs=[row, pl.BlockSpec((8, D), lambda i: (0, 0)), pl.BlockSpec((8, 128), lambda i: (0, 0))],
        out_shape=[jax.ShapeDtypeStruct((S, D), F32), jax.ShapeDtypeStruct((8, D), F32),
                   jax.ShapeDtypeStruct((8, 128), F32)],
        scratch_shapes=[pltpu.VMEM((8, D), F32)],
        compiler_params=_params(("arbitrary",)),
    )(x3, gfin, tgt)


def _mix_in(x, ng, sh, sc, w_in):
    S = x.shape[0]
    tm = _row_tile(S, 512)

    def body(x_ref, g_ref, sh_ref, sc_ref, w_ref, h_ref, z_ref):
        xv = x_ref[...]
        hb = (((xv * _rms(xv)) * g_ref[...]) * (1.0 + sc_ref[...]) + sh_ref[...]).astype(BF16)
        h_ref[...] = hb
        z_ref[...] = _dot_nt(hb, w_ref[...])

    row = pl.BlockSpec((tm, D), lambda i: (i, 0))
    vec = pl.BlockSpec((1, D), lambda i: (0, 0))
    return pl.pallas_call(
        body, name="mix_in", grid=(S // tm,),
        in_specs=[row, vec, vec, vec, pl.BlockSpec((Z_COLS, D), lambda i: (0, 0))],
        out_specs=[row, pl.BlockSpec((tm, Z_COLS), lambda i: (i, 0))],
        out_shape=[jax.ShapeDtypeStruct((S, D), BF16), jax.ShapeDtypeStruct((S, Z_COLS), F32)],
        compiler_params=_params(("parallel",)),
    )(x, ng, sh, sc, w_in)


def _conv_taps(u, halo, rows):
    u1 = jnp.where(rows == 0, halo[7:8, :], pltpu.roll(u, 1, 0))
    u2 = jnp.where(rows == 0, halo[6:7, :], jnp.where(rows == 1, halo[7:8, :], pltpu.roll(u, 2, 0)))
    return u1, u2


def _mix_mid(z, conv_w, gq, gkv, wuq, wukv, cs, sn):
    S = z.shape[0]
    tm = _row_tile(S, 512)
    hb = tm // 8

    def body(z_ref, zh_ref, cw_ref, gq_ref, gkv_ref, wuq_ref, wukv_ref, cs_ref, sn_ref,
             ya_ref, q_ref, k_ref, v_ref, cqn_ref, ckvn_ref):
        i = pl.program_id(0)
        xb = z_ref[:, 0:CONV_W]
        u = z_ref[:, CONV_W:2 * CONV_W] * z_ref[:, 2 * CONV_W:3 * CONV_W]
        halo = zh_ref[:, CONV_W:2 * CONV_W] * zh_ref[:, 2 * CONV_W:3 * CONV_W]
        halo = jnp.where(i > 0, halo, 0.0)
        rows = lax.broadcasted_iota(jnp.int32, (tm, CONV_W), 0)
        u1, u2 = _conv_taps(u, halo, rows)
        y = cw_ref[0:1, :] * u2 + cw_ref[1:2, :] * u1 + cw_ref[2:3, :] * u
        ya_ref[...] = xb * y

        lane = lax.broadcasted_iota(jnp.int32, (tm, 128), 1)
        cs_v, sn_v = cs_ref[...], sn_ref[...]
        cq = z_ref[:, 3 * CONV_W:3 * CONV_W + Q_LORA]
        cqn = ((cq * _rms(cq)) * gq_ref[...]).astype(BF16)
        cqn_ref[...] = cqn
        q = _dot_nt(cqn, wuq_ref[...])
        for h in range(HEADS):
            o = h * HEAD_SLOT
            q_ref[:, o:o + 128] = q[:, o:o + 128].astype(BF16)
            q_ref[:, o + 128:o + 256] = _rope(q[:, o + 128:o + 256], cs_v, sn_v, lane).astype(BF16)

        c0 = 3 * CONV_W + Q_LORA
        ckv = z_ref[:, c0:c0 + KV_LORA]
        ckvn = ((ckv * _rms(ckv)) * gkv_ref[...]).astype(BF16)
        ckvn_ref[...] = ckvn
        kv = jnp.dot(ckvn, wukv_ref[...], preferred_element_type=F32)
        krot = _rope(z_ref[:, c0 + KV_LORA:Z_COLS], cs_v, sn_v, lane).astype(BF16)
        for h in range(HEADS):
            o = h * HEAD_SLOT
            k_ref[:, o:o + 128] = kv[:, h * 128:(h + 1) * 128].astype(BF16)
            k_ref[:, o + 128:o + 256] = krot
        v_ref[...] = kv[:, HEADS * 128:].astype(BF16)

    def rows_of(n):
        return pl.BlockSpec((tm, n), lambda i: (i, 0))

    def whole(shape):
        return pl.BlockSpec(shape, lambda i: (0, 0))

    return pl.pallas_call(
        body, name="mix_mid", grid=(S // tm,),
        in_specs=[rows_of(Z_COLS), pl.BlockSpec((8, Z_COLS), lambda i: (jnp.maximum(i * hb - 1, 0), 0)),
                  whole((8, CONV_W)), whole((1, Q_LORA)), whole((1, KV_LORA)),
                  whole((HEADS * HEAD_SLOT, Q_LORA)), whole((KV_LORA, 2 * HEADS * 128)),
                  rows_of(128), rows_of(128)],
        out_specs=[rows_of(CONV_W), rows_of(HEADS * HEAD_SLOT), rows_of(HEADS * HEAD_SLOT), rows_of(HEADS * V_HEAD),
                   rows_of(Q_LORA), rows_of(KV_LORA)],
        out_shape=[jax.ShapeDtypeStruct((S, CONV_W), F32), jax.ShapeDtypeStruct((S, HEADS * HEAD_SLOT), BF16),
                   jax.ShapeDtypeStruct((S, HEADS * HEAD_SLOT), BF16), jax.ShapeDtypeStruct((S, HEADS * V_HEAD), BF16),
                   jax.ShapeDtypeStruct((S, Q_LORA), BF16), jax.ShapeDtypeStruct((S, KV_LORA), BF16)],
        compiler_params=_params(("parallel",)),
    )(z, z, conv_w, gq, gkv, wuq, wukv, cs, sn)


def _att_blocks(S):
    bk = min(1024, max(S // 4, 128))
    return bk, bk


def _pair_tables(S, k_major):
    bq, bk = _att_blocks(S)
    nq, nk = S // bq, S // bk
    vis = lambda qi, ki: ki * bk < (qi + 1) * bq
    if k_major:
        pairs = [(qi, ki) for ki in range(nk) for qi in range(nq) if vis(qi, ki)]
    else:
        pairs = [(qi, ki) for qi in range(nq) for ki in range(nk) if vis(qi, ki)]
    cols = [[p[0] for p in pairs], [p[1] for p in pairs], [int((p[1] + 1) * bk > p[0] * bq) for p in pairs]]
    return [jnp.asarray(np.array(c, np.int32)) for c in cols], len(pairs)


def _chunk_mask(qi, ki, bq, bk):
    r = (qi * bq + lax.broadcasted_iota(jnp.int32, (bq, bk), 0)) // CHUNK
    c = (ki * bk + lax.broadcasted_iota(jnp.int32, (bq, bk), 1)) // CHUNK
    return c <= r


def _attention(q, k, v):
    S = q.shape[0]
    bq, bk = _att_blocks(S)
    last_k = bq // bk - 1
    tables, n_pairs = _pair_tables(S, k_major=False)

    def body(qi_ref, ki_ref, mk_ref, q_ref, k_ref, v_ref, o_ref, lse_ref, m_s, l_s, acc_s):
        p_id = pl.program_id(1)
        qi, ki = qi_ref[p_id], ki_ref[p_id]

        @pl.when(ki == 0)
        def _():
            m_s[...] = jnp.full_like(m_s, NEG)
            l_s[...] = jnp.zeros_like(l_s)
            acc_s[...] = jnp.zeros_like(acc_s)

        def update(masked):
            s = lax.dot_general(q_ref[...], k_ref[...], (((1,), (1,)), ((), ())), preferred_element_type=F32)
            if masked:
                s = jnp.where(_chunk_mask(qi, ki, bq, bk), s, NEG)
            m_prev = m_s[...]
            m_new = jnp.maximum(m_prev, jnp.max(s, axis=1, keepdims=True))
            alpha = jnp.exp2((m_prev - m_new) * EXP2_SCALE)
            p = jnp.exp2((s - jnp.tile(m_new, (1, bk // 128))) * EXP2_SCALE)
            l_s[...] = alpha * l_s[...] + jnp.sum(p, axis=1, keepdims=True)
            acc_s[...] = alpha * acc_s[...] + jnp.dot(p.astype(BF16), v_ref[...], preferred_element_type=F32)
            m_s[...] = m_new

        @pl.when(mk_ref[p_id] == 0)
        def _():
            update(False)

        @pl.when(mk_ref[p_id] == 1)
        def _():
            update(True)

        @pl.when(ki == qi * (last_k + 1) + last_k)
        def _():
            l = l_s[...]
            o_ref[...] = acc_s[...] / l
            lse_ref[...] = m_s[...] * EXP2_SCALE + jnp.log2(l)

    grid_spec = pltpu.PrefetchScalarGridSpec(
        num_scalar_prefetch=3, grid=(HEADS, n_pairs),
        in_specs=[pl.BlockSpec((bq, HEAD_SLOT), lambda h, p, qt, kt, mt: (qt[p], h)),
                  pl.BlockSpec((bk, HEAD_SLOT), lambda h, p, qt, kt, mt: (kt[p], h)),
                  pl.BlockSpec((bk, V_HEAD), lambda h, p, qt, kt, mt: (kt[p], h))],
        out_specs=[pl.BlockSpec((bq, V_HEAD), lambda h, p, qt, kt, mt: (qt[p], h))] * 2,
        scratch_shapes=[pltpu.VMEM((bq, V_HEAD), F32)] * 3)
    return pl.pallas_call(
        body, name="attention", grid_spec=grid_spec,
        out_shape=[jax.ShapeDtypeStruct((S, HEADS * V_HEAD), F32)] * 2,
        compiler_params=_params(("arbitrary", "arbitrary")),
    )(*tables, q, k, v)


def _attention_bwd(q, k, v, do, lse2, delta):
    S = q.shape[0]
    bq, bk = _att_blocks(S)
    nq = S // bq
    tables, n_pairs = _pair_tables(S, k_major=True)

    def body(qi_ref, ki_ref, mk_ref, q_ref, k_ref, v_ref, do_ref, lse_ref, dl_ref, dq_hbm, dk_ref, dv_ref,
             dq_s, dk_s, dv_s, sem):
        head, p_id = pl.program_id(0), pl.program_id(1)
        qi, ki = qi_ref[p_id], ki_ref[p_id]
        rows = pl.ds(pl.multiple_of(qi * bq, bq), bq)

        @pl.when(qi * bq <= ki * bk)
        def _():
            dk_s[...] = jnp.zeros_like(dk_s)
            dv_s[...] = jnp.zeros_like(dv_s)

        def update(masked):
            qv, kv, dov = q_ref[...], k_ref[...], do_ref[...]
            s = lax.dot_general(qv, kv, (((1,), (1,)), ((), ())), preferred_element_type=F32)
            dp = lax.dot_general(dov, v_ref[...], (((1,), (1,)), ((), ())), preferred_element_type=F32)
            if masked:
                s = jnp.where(_chunk_mask(qi, ki, bq, bk), s, NEG)
            p = jnp.exp2(s * EXP2_SCALE - jnp.tile(lse_ref[...], (1, bk // 128)))
            dv_s[...] += lax.dot_general(p.astype(BF16), dov, (((0,), (0,)), ((), ())), preferred_element_type=F32)
            ds = (p * (dp - jnp.tile(dl_ref[...], (1, bk // 128)))).astype(BF16)
            dk_s[...] += lax.dot_general(ds, qv, (((0,), (0,)), ((), ())), preferred_element_type=F32)
            dq = jnp.dot(ds, kv, preferred_element_type=F32)

            @pl.when(ki == 0)
            def _():
                dq_s[rows, :] = dq

            @pl.when(ki > 0)
            def _():
                dq_s[rows, :] += dq

        @pl.when(mk_ref[p_id] == 0)
        def _():
            update(False)

        @pl.when(mk_ref[p_id] == 1)
        def _():
            update(True)

        @pl.when(qi == nq - 1)
        def _():
            dk_ref[...] = dk_s[...] * ATT_SCALE
            dv_ref[...] = dv_s[...]

        @pl.when(p_id == n_pairs - 1)
        def _():
            dq_s[...] = dq_s[...] * ATT_SCALE
            out = pltpu.make_async_copy(
                dq_s, dq_hbm.at[:, pl.ds(pl.multiple_of(head * HEAD_SLOT, HEAD_SLOT), HEAD_SLOT)], sem)
            out.start()
            out.wait()

    grid_spec = pltpu.PrefetchScalarGridSpec(
        num_scalar_prefetch=3, grid=(HEADS, n_pairs),
        in_specs=[pl.BlockSpec((bq, HEAD_SLOT), lambda h, p, qt, kt, mt: (qt[p], h)),
                  pl.BlockSpec((bk, HEAD_SLOT), lambda h, p, qt, kt, mt: (kt[p], h)),
                  pl.BlockSpec((bk, V_HEAD), lambda h, p, qt, kt, mt: (kt[p], h)),
                  pl.BlockSpec((bq, V_HEAD), lambda h, p, qt, kt, mt: (qt[p], h)),
                  pl.BlockSpec((bq, V_HEAD), lambda h, p, qt, kt, mt: (qt[p], h)),
                  pl.BlockSpec((bq, V_HEAD), lambda h, p, qt, kt, mt: (qt[p], h))],
        out_specs=[pl.BlockSpec(memory_space=pl.ANY),
                   pl.BlockSpec((bk, HEAD_SLOT), lambda h, p, qt, kt, mt: (kt[p], h)),
                   pl.BlockSpec((bk, V_HEAD), lambda h, p, qt, kt, mt: (kt[p], h))],
        scratch_shapes=[pltpu.VMEM((S, HEAD_SLOT), F32), pltpu.VMEM((bk, HEAD_SLOT), F32),
                        pltpu.VMEM((bk, V_HEAD), F32), pltpu.SemaphoreType.DMA])
    return pl.pallas_call(
        body, name="attention_bwd", grid_spec=grid_spec,
        out_shape=[jax.ShapeDtypeStruct((S, HEADS * HEAD_SLOT), F32), jax.ShapeDtypeStruct((S, HEADS * HEAD_SLOT), F32),
                   jax.ShapeDtypeStruct((S, HEADS * V_HEAD), F32)],
        compiler_params=_params(("arbitrary", "arbitrary")),
    )(*tables, q, k, v, do, lse2, delta)


def _group_mats():
    def blockdiag(n, g):
        idx = np.arange(n) // g
        return jnp.asarray((idx[:, None] == idx[None, :]).astype(np.float32), dtype=BF16)
    return blockdiag(CONV_W, CONV_GROUP), blockdiag(HEADS * V_HEAD, V_HEAD)


def _mix_out(ya, o, gout, w_out, x, gate, ga, gb):
    S = x.shape[0]
    tm = _row_tile(S, 512)

    def body(ya_ref, o_ref, go_ref, w_ref, x_ref, g_ref, ga_ref, gb_ref, xo_ref, yn_ref, yo_ref):
        yav, ov = ya_ref[...], o_ref[...]
        ra = lax.rsqrt(_gsum(yav * yav, ga_ref[...]) * (1.0 / CONV_GROUP) + EPS)
        rb = lax.rsqrt(_gsum(ov * ov, gb_ref[...]) * (1.0 / V_HEAD) + EPS)
        na = ((yav * ra) * go_ref[:, 0:CONV_W]).astype(BF16)
        nb = ((ov * rb) * go_ref[:, CONV_W:]).astype(BF16)
        yn_ref[:, 0:CONV_W] = na
        yn_ref[:, CONV_W:] = nb
        yo = (jnp.dot(na, w_ref[0:CONV_W, :], preferred_element_type=F32)
              + jnp.dot(nb, w_ref[CONV_W:, :], preferred_element_type=F32))
        xo_ref[...] = x_ref[...] + g_ref[...] * yo
        yo_ref[...] = yo.astype(BF16)

    row = pl.BlockSpec((tm, D), lambda i: (i, 0))
    half = pl.BlockSpec((tm, CONV_W), lambda i: (i, 0))
    vec = pl.BlockSpec((1, D), lambda i: (0, 0))
    sq = pl.BlockSpec((CONV_W, CONV_W), lambda i: (0, 0))
    return pl.pallas_call(
        body, name="mix_out", grid=(S // tm,),
        in_specs=[half, half, vec, pl.BlockSpec((D, D), lambda i: (0, 0)), row, vec, sq, sq],
        out_specs=[row, row, row],
        out_shape=[jax.ShapeDtypeStruct((S, D), F32), jax.ShapeDtypeStruct((S, D), BF16),
                   jax.ShapeDtypeStruct((S, D), BF16)],
        compiler_params=_params(("parallel",)),
    )(ya, o, gout, w_out, x, gate, ga, gb)


def _mix_out_bwd(dx, gate, yo, w_out, ya, o, gout, ga, gb, comm=None):
    S = dx.shape[0]
    tm = _row_tile(S, 512)
    n_i = S // tm

    def norm_bwd(v, dn, gain, gmat, inv_n):
        r = lax.rsqrt(_gsum(v * v, gmat) * inv_n + EPS)
        vh = v * r
        dy = dn * gain
        return r * (dy - vh * (_gsum(dy * vh, gmat) * inv_n)), dn * vh

    def body(dx_ref, g_ref, yo_ref, w_ref, ya_ref, o_ref, go_ref, ga_ref, gb_ref,
             dyo_ref, dya_ref, do_ref, dl_ref, p_ref):
        i = pl.program_id(0)
        dxv = dx_ref[...]
        dyo = (dxv * g_ref[...]).astype(BF16)
        dyo_ref[...] = dyo
        dyn = _dot_nt(dyo, w_ref[...])
        dya, dga = norm_bwd(ya_ref[...], dyn[:, 0:CONV_W], go_ref[:, 0:CONV_W], ga_ref[...], 1.0 / CONV_GROUP)
        ov = o_ref[...]
        do, dgb = norm_bwd(ov, dyn[:, CONV_W:], go_ref[:, CONV_W:], gb_ref[...], 1.0 / V_HEAD)
        dya_ref[...] = dya
        do_ref[...] = do.astype(BF16)
        dl_ref[...] = _gsum(do * ov, gb_ref[...], split=True)

        @pl.when(i == 0)
        def _():
            p_ref[...] = jnp.zeros_like(p_ref)

        p_ref[:, 0:D] += _rsum8(dxv * yo_ref[...].astype(F32))
        p_ref[:, D:D + CONV_W] += _rsum8(dga)
        p_ref[:, D + CONV_W:2 * D] += _rsum8(dgb)

        @pl.when(i == n_i - 1)
        def _():
            _all_rows(p_ref)

    row = pl.BlockSpec((tm, D), lambda i: (i, 0))
    half = pl.BlockSpec((tm, CONV_W), lambda i: (i, 0))
    vec = pl.BlockSpec((1, D), lambda i: (0, 0))
    sq = pl.BlockSpec((CONV_W, CONV_W), lambda i: (0, 0))
    return _hosted_call(
        body, name="mix_out_bwd", grid=(n_i,),
        in_specs=[row, vec, row, pl.BlockSpec((D, D), lambda i: (0, 0)), half, half, vec, sq, sq],
        out_specs=[row, half, half, half, pl.BlockSpec((8, 2 * D), lambda i: (0, 0))],
        out_shape=[jax.ShapeDtypeStruct((S, D), BF16), jax.ShapeDtypeStruct((S, CONV_W), F32),
                   jax.ShapeDtypeStruct((S, CONV_W), BF16), jax.ShapeDtypeStruct((S, CONV_W), F32),
                   jax.ShapeDtypeStruct((8, 2 * D), F32)],
        scratch_shapes=[], semantics=("arbitrary",), args=(dx, gate, yo, w_out, ya, o, gout, ga, gb), comm=comm)


MID_SUMS = 3 * CONV_W + Q_LORA + KV_LORA


def _mix_mid_bwd(z, dya, conv_w, gq, gkv, wuq, wukv, cs, sn, dq, dk, dv, comm=None):
    S = z.shape[0]
    tm = _row_tile(S, 256)
    n_i = S // tm
    hb = tm // 8
    last_blk = S // 8 - 1

    def latent_bwd(cv, dcn, gain):
        r = _rms(cv)
        ch = cv * r
        dy = dcn * gain
        return r * (dy - ch * jnp.mean(dy * ch, axis=-1, keepdims=True)), dcn * ch

    def body(z_ref, zp_ref, zn_ref, dya_ref, dyan_ref, cw_ref, gq_ref, gkv_ref, wuq_ref, wukv_ref, cs_ref, sn_ref,
             dq_ref, dk_ref, dv_ref, dz_ref, dqf_ref, dkvf_ref, p_ref):
        i = pl.program_id(0)
        xb, xc, xu = z_ref[:, 0:CONV_W], z_ref[:, CONV_W:2 * CONV_W], z_ref[:, 2 * CONV_W:3 * CONV_W]
        u = xc * xu
        halo = jnp.where(i > 0, zp_ref[:, CONV_W:2 * CONV_W] * zp_ref[:, 2 * CONV_W:3 * CONV_W], 0.0)
        rows = lax.broadcasted_iota(jnp.int32, (tm, CONV_W), 0)
        u1, u2 = _conv_taps(u, halo, rows)
        w0, w1, w2 = cw_ref[0:1, :], cw_ref[1:2, :], cw_ref[2:3, :]
        y = w0 * u2 + w1 * u1 + w2 * u
        dyav = dya_ref[...]
        dy = dyav * xb
        nxt = jnp.where(i < n_i - 1, dyan_ref[...] * zn_ref[:, 0:CONV_W], 0.0)
        dy1 = jnp.where(rows == tm - 1, nxt[0:1, :], pltpu.roll(dy, tm - 1, 0))
        dy2 = jnp.where(rows == tm - 1, nxt[1:2, :], jnp.where(rows == tm - 2, nxt[0:1, :], pltpu.roll(dy, tm - 2, 0)))
        du = w2 * dy + w1 * dy1 + w0 * dy2
        dz_ref[:, 0:CONV_W] = (dyav * y).astype(BF16)
        dz_ref[:, CONV_W:2 * CONV_W] = (du * xu).astype(BF16)
        dz_ref[:, 2 * CONV_W:3 * CONV_W] = (du * xc).astype(BF16)

        lane = lax.broadcasted_iota(jnp.int32, (tm, 128), 1)
        cs_v, sn_v = cs_ref[...], sn_ref[...]
        dkr = jnp.zeros((tm, 128), F32)
        for h in range(HEADS):
            o = h * HEAD_SLOT
            dqf_ref[:, o:o + 128] = dq_ref[:, o:o + 128].astype(BF16)
            dqf_ref[:, o + 128:o + 256] = _rope_t(dq_ref[:, o + 128:o + 256], cs_v, sn_v, lane).astype(BF16)
            dkvf_ref[:, h * 128:(h + 1) * 128] = dk_ref[:, o:o + 128].astype(BF16)
            dkr = dkr + dk_ref[:, o + 128:o + 256]
        dkvf_ref[:, HEADS * 128:] = dv_ref[...].astype(BF16)

        c0 = 3 * CONV_W
        dcqn = jnp.dot(dqf_ref[...], wuq_ref[...], preferred_element_type=F32)
        dcq, dgq = latent_bwd(z_ref[:, c0:c0 + Q_LORA], dcqn, gq_ref[...])
        dz_ref[:, c0:c0 + Q_LORA] = dcq.astype(BF16)
        c1 = c0 + Q_LORA
        dckvn = _dot_nt(dkvf_ref[...], wukv_ref[...])
        dckv, dgkv = latent_bwd(z_ref[:, c1:c1 + KV_LORA], dckvn, gkv_ref[...])
        dz_ref[:, c1:c1 + KV_LORA] = dckv.astype(BF16)
        dz_ref[:, c1 + KV_LORA:Z_COLS] = _rope_t(dkr, cs_v, sn_v, lane).astype(BF16)

        @pl.when(i == 0)
        def _():
            p_ref[...] = jnp.zeros_like(p_ref)

        p_ref[:, 0:CONV_W] += _rsum8(dy * u2)
        p_ref[:, CONV_W:2 * CONV_W] += _rsum8(dy * u1)
        p_ref[:, 2 * CONV_W:3 * CONV_W] += _rsum8(dy * u)
        p_ref[:, c0:c0 + Q_LORA] += _rsum8(dgq)
        p_ref[:, c1:c1 + KV_LORA] += _rsum8(dgkv)

        @pl.when(i == n_i - 1)
        def _():
            _all_rows(p_ref)

    def rows_of(n):
        return pl.BlockSpec((tm, n), lambda i: (i, 0))

    def whole(shape):
        return pl.BlockSpec(shape, lambda i: (0, 0))

    def prev8(n):
        return pl.BlockSpec((8, n), lambda i: (jnp.maximum(i * hb - 1, 0), 0))

    def next8(n):
        return pl.BlockSpec((8, n), lambda i: (jnp.minimum((i + 1) * hb, last_blk), 0))

    return _hosted_call(
        body, name="mix_mid_bwd", grid=(n_i,),
        in_specs=[rows_of(Z_COLS), prev8(Z_COLS), next8(Z_COLS), rows_of(CONV_W), next8(CONV_W),
                  whole((8, CONV_W)), whole((1, Q_LORA)), whole((1, KV_LORA)),
                  whole((HEADS * HEAD_SLOT, Q_LORA)), whole((KV_LORA, 2 * HEADS * 128)),
                  rows_of(128), rows_of(128),
                  rows_of(HEADS * HEAD_SLOT), rows_of(HEADS * HEAD_SLOT), rows_of(HEADS * V_HEAD)],
        out_specs=[rows_of(Z_COLS), rows_of(HEADS * HEAD_SLOT), rows_of(2 * HEADS * 128), whole((8, MID_SUMS))],
        out_shape=[jax.ShapeDtypeStruct((S, Z_COLS), BF16), jax.ShapeDtypeStruct((S, HEADS * HEAD_SLOT), BF16),
                   jax.ShapeDtypeStruct((S, 2 * HEADS * 128), BF16), jax.ShapeDtypeStruct((8, MID_SUMS), F32)],
        scratch_shapes=[], semantics=("arbitrary",),
        args=(z, z, z, dya, dya, conv_w, gq, gkv, wuq, wukv, cs, sn, dq, dk, dv), comm=comm)


ADA_Q = N_MOD * D // N_CHIPS
ADA_TN = 768


def _ada_forward(c_all, ada_w_q, ada_b_q):
    def body(c_ref, w_ref, b_ref, o_ref):
        cv = c_ref[...]
        sc = (cv * jax.nn.sigmoid(cv)).astype(BF16)
        o_ref[...] = jnp.dot(sc, w_ref[...].astype(BF16), preferred_element_type=F32) + b_ref[...]

    return pl.pallas_call(
        body, name="ada_forward", grid=(ADA_Q // ADA_TN,),
        in_specs=[pl.BlockSpec((16, D), lambda j: (0, 0)), pl.BlockSpec((D, ADA_TN), lambda j: (0, j)),
                  pl.BlockSpec((1, ADA_TN), lambda j: (0, j))],
        out_specs=pl.BlockSpec((16, ADA_TN), lambda j: (0, j)),
        out_shape=jax.ShapeDtypeStruct((16, ADA_Q), F32),
        compiler_params=_params(("parallel",)),
    )(c_all, ada_w_q, ada_b_q)


def _ada_wgrad(c_all, dmod_q):
    def body(c_ref, d_ref, o_ref):
        cv = c_ref[...]
        sc = (cv * jax.nn.sigmoid(cv)).astype(BF16)
        o_ref[...] = lax.dot_general(sc, d_ref[...].astype(BF16), (((0,), (0,)), ((), ())),
                                     preferred_element_type=F32)

    return pl.pallas_call(
        body, name="ada_wgrad", grid=(ADA_Q // ADA_TN,),
        in_specs=[pl.BlockSpec((16, D), lambda j: (0, 0)), pl.BlockSpec((16, ADA_TN), lambda j: (0, j))],
        out_specs=pl.BlockSpec((D, ADA_TN), lambda j: (0, j)),
        out_shape=jax.ShapeDtypeStruct((D, ADA_Q), F32),
        compiler_params=_params(("parallel",)),
    )(c_all, dmod_q)


def _sum_devices(parts):
    n = parts.shape[1]

    def body(p_ref, o_ref):
        o_ref[...] = jnp.broadcast_to(jnp.sum(p_ref[...], axis=0, keepdims=True), o_ref.shape)

    return pl.pallas_call(
        body, name="sum_devices",
        in_specs=[pl.BlockSpec((N_DEV, n), lambda: (0, 0))], out_specs=pl.BlockSpec((N_DEV, n), lambda: (0, 0)),
        out_shape=jax.ShapeDtypeStruct((N_DEV, n), F32),
    )(parts)


def _adamw(w, g, m, v, *, name):
    _, rows, cols = w.shape
    tr = _row_tile(rows, 256)

    def body(w_ref, g_ref, m_ref, v_ref, go_ref, d_ref, mo_ref, vo_ref):
        gv = g_ref[...]
        mn = B1 * m_ref[0] + (1.0 - B1) * gv
        vn = B2 * v_ref[0] + (1.0 - B2) * (gv * gv)
        m_hat = mn / (1.0 - B1 ** STEP)
        v_hat = vn / (1.0 - B2 ** STEP)
        go_ref[0] = gv
        d_ref[0] = -LR * (m_hat / (jnp.sqrt(v_hat) + AEPS) + WD * w_ref[0])
        mo_ref[0] = mn
        vo_ref[0] = vn

    blk = pl.BlockSpec((1, tr, cols), lambda i: (0, i, 0))
    return pl.pallas_call(
        body, name=name, grid=(rows // tr,),
        in_specs=[blk, pl.BlockSpec((tr, cols), lambda i: (i, 0)), blk, blk], out_specs=[blk] * 4,
        out_shape=[jax.ShapeDtypeStruct((1, rows, cols), F32)] * 4,
        compiler_params=_params(("parallel",)),
    )(w, g, m, v)


def _small_allgather(v, *, name):
    m, n = v.shape

    def body(x_ref, out_ref, send_sems, recv_sems, local_sem):
        x, y, c = _place()
        me, sibling = (x, y, c), (x, y, 1 - c)
        chips = [(1 - x, y), (x, 1 - y), (1 - x, 1 - y)]

        def rows(px, py, pc):
            return out_ref.at[pl.ds((4 * px + 2 * py + pc) * m, m), :]

        def copy(k, block, to, src=None):
            return pltpu.make_async_remote_copy(
                src_ref=rows(*block) if src is None else src, dst_ref=rows(*block),
                send_sem=send_sems.at[k], recv_sem=recv_sems.at[k], device_id=to, device_id_type=MESH)

        mine = pltpu.make_async_copy(x_ref, rows(*me), local_sem)
        mine.start()
        first = [copy(0, me, sibling, src=x_ref)]
        first += [copy(1 + j, me, (*chip, c), src=x_ref) for j, chip in enumerate(chips)]
        for cp in first:
            cp.start()
        passed = [copy(4 + j, (*chip, c), sibling) for j, chip in enumerate(chips)]
        for j, chip in enumerate(chips):
            copy(1 + j, (*chip, c), me).wait_recv()
            passed[j].start()
        copy(0, sibling, me).wait_recv()
        for j, chip in enumerate(chips):
            copy(4 + j, (*chip, 1 - c), me).wait_recv()
        for cp in first + passed:
            cp.wait_send()
        mine.wait()

    return pl.pallas_call(
        body, name=name,
        out_shape=jax.ShapeDtypeStruct((N_DEV * m, n), v.dtype),
        in_specs=[pl.BlockSpec(memory_space=pltpu.VMEM)], out_specs=pl.BlockSpec(memory_space=pltpu.VMEM),
        scratch_shapes=[pltpu.SemaphoreType.DMA((7,)), pltpu.SemaphoreType.DMA((7,)), pltpu.SemaphoreType.DMA],
    )(v)


ADD_BLOCKS = 2


def _pair_add(place, gs, ts, *, name):
    n_a = len(gs)

    def body(pl_ref, *refs):
        g_refs, t_refs = refs[:n_a], refs[n_a:2 * n_a]
        pf_refs, pb_refs = refs[2 * n_a:3 * n_a], refs[3 * n_a:]
        for g_ref, t_ref, pf_ref, pb_ref in zip(g_refs, t_refs, pf_refs, pb_refs):
            s = g_ref[...] + t_ref[...]
            pf_ref[...] = s
            pb_ref[...] = s.astype(BF16)

    def blk(t, own_half):
        tr = t.shape[1] // ADD_BLOCKS
        if own_half:
            return pl.BlockSpec((1, tr, t.shape[2]), lambda q, r, p: (q, p[0] * ADD_BLOCKS + r, 0))
        return pl.BlockSpec((1, tr, t.shape[2]), lambda q, r, p: (q, r, 0))

    grid_spec = pltpu.PrefetchScalarGridSpec(
        num_scalar_prefetch=1, grid=(N_CHIPS, ADD_BLOCKS),
        in_specs=[blk(t, True) for t in ts] + [blk(t, False) for t in ts],
        out_specs=[blk(t, False) for t in ts] * 2)
    res = pl.pallas_call(
        body, name=name, grid_spec=grid_spec,
        out_shape=[jax.ShapeDtypeStruct(t.shape, F32) for t in ts] + [jax.ShapeDtypeStruct(t.shape, BF16) for t in ts],
        compiler_params=_params(("parallel", "parallel")),
    )(place, *gs, *ts)
    return list(res[:n_a]), list(res[n_a:])


def _chip_add(place, pfs, ts, *, name):
    n_a = len(pfs)

    def body(pl_ref, *refs):
        pf_refs, t_refs, o_refs = refs[:n_a], refs[n_a:4 * n_a], refs[4 * n_a:]
        for i, (pf_ref, o_ref) in enumerate(zip(pf_refs, o_refs)):
            t1, t2, t3 = t_refs[3 * i:3 * i + 3]
            o_ref[...] = ((pf_ref[0] + t1[0].astype(F32)) + t2[0].astype(F32)) + t3[0].astype(F32)

    def slot(t, j):
        return pl.BlockSpec((1, t.shape[1] // ADD_BLOCKS, t.shape[2]), lambda r, p: (p[1] ^ j, r, 0))

    grid_spec = pltpu.PrefetchScalarGridSpec(
        num_scalar_prefetch=1, grid=(ADD_BLOCKS,),
        in_specs=[slot(t, 0) for t in pfs] + [slot(t, j) for t in ts for j in (1, 2, 3)],
        out_specs=[pl.BlockSpec((t.shape[1] // ADD_BLOCKS, t.shape[2]), lambda r, p: (r, 0)) for t in pfs])
    res = pl.pallas_call(
        body, name=name, grid_spec=grid_spec,
        out_shape=[jax.ShapeDtypeStruct(t.shape[1:], F32) for t in pfs],
        compiler_params=_params(("parallel",)),
    )(place, *pfs, *[t for t in ts for _ in range(3)])
    return list(res)


BULK = [("ffn1_w1", "colsT"), ("ffn1_w3", "colsT"), ("ffn1_w2", "rows"), ("w_in", "cols"), ("w_uq", "colsT"),
        ("w_ukv", "cols"), ("w_out", "rows"), ("ffn2_w1", "colsT"), ("ffn2_w3", "colsT"), ("ffn2_w2", "rows")]
KIND = dict(BULK)


def _group(*names):
    return [b for b in BULK if b[0] in names]


W_FIRST = _group("ffn1_w1", "ffn1_w3")
W_REST = [b for b in BULK if b not in W_FIRST]
G_FFN2 = _group("ffn2_w1", "ffn2_w3", "ffn2_w2")
G_MIX = _group("w_in", "w_uq", "w_ukv", "w_out")
G_FFN1 = _group("ffn1_w1", "ffn1_w3", "ffn1_w2")


def _gathered_weights(specs, shards, got, myq):
    out = {}
    for (name, kind), part in zip(specs, got):
        part = lax.dynamic_update_slice_in_dim(part, shards[name][None], myq, axis=0)
        out[name] = _full_weight(part, kind)
    return out


def _working_shard(w, kind):
    return jnp.swapaxes(w, 1, 2)[0] if kind == "colsT" else w[0]


def _full_weight(parts, kind):
    if kind == "cols":
        return jnp.transpose(parts, (1, 0, 2)).reshape(parts.shape[1], -1)
    return parts.reshape(-1, parts.shape[2])


def _quarters(g, kind):
    if kind == "cols":
        k, n = g.shape
        return jnp.transpose(g.reshape(k, N_CHIPS, n // N_CHIPS), (1, 0, 2))
    return g.reshape(N_CHIPS, g.shape[0] // N_CHIPS, g.shape[1])


def _pad_heads(w_uq_t):
    w = w_uq_t.reshape(HEADS, QK_NOPE + QK_ROPE, Q_LORA)
    return jnp.pad(w, ((0, 0), (0, HEAD_SLOT - QK_NOPE - QK_ROPE), (0, 0))).reshape(HEADS * HEAD_SLOT, Q_LORA)


def _unpad_heads(g):
    return g.reshape(HEADS, HEAD_SLOT, Q_LORA)[:, :QK_NOPE + QK_ROPE].reshape(HEADS * (QK_NOPE + QK_ROPE), Q_LORA)


def _split_kv(w_ukv):
    return jnp.transpose(w_ukv.reshape(KV_LORA, HEADS, 2, 128), (0, 2, 1, 3)).reshape(KV_LORA, 2 * HEADS * 128)


def _merge_kv(g):
    return jnp.transpose(g.reshape(KV_LORA, 2, HEADS, 128), (0, 2, 1, 3)).reshape(KV_LORA, 2 * HEADS * 128)


def _rope_tables(positions):
    inv_freq = ROPE_THETA ** (-jnp.arange(0, QK_ROPE, 2, dtype=F32) / QK_ROPE)
    ang = positions.astype(F32)[:, None] * inv_freq
    cos, sin, zero = jnp.cos(ang), jnp.sin(ang), jnp.zeros((positions.shape[0], 64), F32)
    return jnp.concatenate([cos, cos, zero], axis=1), jnp.concatenate([sin, sin, zero], axis=1)


def _reduce_tail(place, specs, pfs, t2s, tag, host=None):
    rhs = _chip_add(place, pfs, t2s, name=tag + "_chip_add")
    if host is None:
        out, others = None, _run_comm(_PairShare(rhs), name=tag + "_pair_share")
    else:
        out, others = host(_PairShare(rhs))
    return out, _assemble(place, specs, rhs, others)


def _assemble(place, specs, rhs, others):
    south = place[0] == 0
    return {b[0]: jnp.concatenate([jnp.where(south, rh, ot), jnp.where(south, ot, rh)], axis=0)
            for b, rh, ot in zip(specs, rhs, others)}


def _local_step(x, positions, target, mod, vec, conv_w, w_first, rest_shards, place):
    row = lambda k: mod[k:k + 1]
    sh1, sc1, g1, sh2, sc2, g2, sh3, sc3, g3 = [row(k) for k in range(N_MOD)]
    cs, sn = _rope_tables(positions)
    cw8 = jnp.pad(conv_w, ((0, 5), (0, 0)))
    ga, gb = _group_mats()
    dist = place is not None
    comm = lambda prog: prog if dist else None

    gather = _Gather([rest_shards[b[0]] for b in W_REST]) if dist else None
    (h1, a1, b1, u1), got = _ffn_up(x, vec["norm_ffn1_g"], sh1, sc1, w_first["ffn1_w1"], w_first["ffn1_w3"],
                                    name="ffn1_up", comm=gather)
    if dist:
        w = _gathered_weights(W_REST, rest_shards, got, place[1])
    else:
        w = dict(rest_shards)
    w.update(w_first)
    w_in = jnp.pad(w["w_in"].T, ((0, Z_COLS - IN_COLS), (0, 0)))
    wuq = _pad_heads(w["w_uq"])
    wukv = _split_kv(w["w_ukv"])
    x1, f1 = _ffn_down(u1, w["ffn1_w2"], x, g1, name="ffn1_down")
    h2, z = _mix_in(x1, vec["norm_mix_g"], sh2, sc2, w_in)
    ya, q, k, v, cqn, ckvn = _mix_mid(z, cw8, vec["q_norm_g"], vec["kv_norm_g"], wuq, wukv, cs, sn)
    o, lse = _attention(q, k, v)
    x2, yn, yo = _mix_out(ya, o, vec["out_norm_g"], w["w_out"], x1, g2, ga, gb)
    (h3, a3, b3, u3), _ = _ffn_up(x2, vec["norm_ffn2_g"], sh3, sc3, w["ffn2_w1"], w["ffn2_w3"], name="ffn2_up")
    x3, f3 = _ffn_down(u3, w["ffn2_w2"], x2, g3, name="ffn2_down")
    dx3, dgfin, loss_blk = _final_loss(x3, vec["final_norm_g"], target)

    grads, reduced = {}, {}

    def tn(a, b, tm, tn_, name, prog=None):
        if prog is None:
            return _tn_matmul(a, b, tm=tm, tn=tn_, name=name), None
        return _tn_matmul(a, b, tm=tm, tn=tn_, name=name, comm=prog)

    def slab_of(specs):
        return [_quarters(grads[n], kind) for n, kind in specs]

    (df3, da3, db3, dg3), _ = _ffn_bwd_du(dx3, g3, f3, w["ffn2_w2"], a3, b3, name="ffn2_bwd_du")
    grads["ffn2_w2"], _ = tn(u3, df3, FF // 2, D, "ffn2_dw2")
    grads["ffn2_w1"], _ = tn(da3, h3, FF // 2, D, "ffn2_dw1")
    grads["ffn2_w3"], _ = tn(db3, h3, FF // 2, D, "ffn2_dw3")
    (dx2, s3), _ = _dh_normbwd([(da3, w["ffn2_w1"]), (db3, w["ffn2_w3"])], x2, vec["norm_ffn2_g"], sc3, dx3,
                               name="ffn2_bwd_dh")

    p1 = slab_of(G_FFN2) if dist else None
    (dyo, dya, do, delta, s_out), t1 = _mix_out_bwd(dx2, g2, yo, w["w_out"], ya, o, vec["out_norm_g"], ga, gb,
                                                    comm=comm(_PairExchange(p1) if dist else None))
    grads["w_out"], _ = tn(yn, dyo, D, D, "dw_out")
    if dist:
        pf1, pb1 = _pair_add(place, p1, t1, name="ffn2g_pair_add")
    dq, dk, dv = _attention_bwd(q, k, v, do, lse, delta)
    (dz, dqf, dkvf, s_mid), t2 = _mix_mid_bwd(z, dya, cw8, vec["q_norm_g"], vec["kv_norm_g"], wuq, wukv, cs, sn,
                                              dq, dk, dv, comm=comm(_ChipExchange(pb1) if dist else None))
    g_uq, _ = tn(dqf, cqn, HEADS * HEAD_SLOT, Q_LORA, "dw_uq")
    g_ukv, _ = tn(ckvn, dkvf, KV_LORA, 2 * HEADS * 128, "dw_ukv")
    grads["w_uq"], grads["w_ukv"] = _unpad_heads(g_uq), _merge_kv(g_ukv)
    if dist:
        g_in, red = _reduce_tail(place, G_FFN2, pf1, t2, "ffn2g",
                                 host=lambda prog: tn(h2, dz, D, Z_COLS // 2, "dw_in", prog))
        reduced.update(red)
    else:
        g_in, _ = tn(h2, dz, D, Z_COLS // 2, "dw_in")
    grads["w_in"] = g_in[:, :IN_COLS]
    (dx1, s2), _ = _dh_normbwd([(dz, w_in)], x1, vec["norm_mix_g"], sc2, dx2, name="mix_bwd_dh")

    p2 = slab_of(G_MIX) if dist else None
    (df1, da1, db1, dg1), t1 = _ffn_bwd_du(dx1, g1, f1, w["ffn1_w2"], a1, b1, name="ffn1_bwd_du",
                                           comm=comm(_PairExchange(p2) if dist else None))
    dh_pairs = [(da1, w["ffn1_w1"]), (db1, w["ffn1_w3"])]
    if dist:
        pf2, pb2 = _pair_add(place, p2, t1, name="mixg_pair_add")
        grads["ffn1_w2"], t2 = tn(u1, df1, FF // 2, D, "ffn1_dw2", _ChipExchange(pb2))
        rh_mix = _chip_add(place, pf2, t2, name="mixg_chip_add")
        q_w2 = [_quarters(grads["ffn1_w2"], KIND["ffn1_w2"])]
        grads["ffn1_w1"], got = tn(da1, h1, FF // 2, D, "ffn1_dw1", _Multi([_PairShare(rh_mix), _PairExchange(q_w2)]))
        reduced.update(_assemble(place, G_MIX, rh_mix, got[:len(rh_mix)]))
        pf_w2, pb_w2 = _pair_add(place, q_w2, got[len(rh_mix):], name="ffn1w2_pair_add")
        q_w1 = [_quarters(grads["ffn1_w1"], KIND["ffn1_w1"])]
        grads["ffn1_w3"], got = tn(db1, h1, FF // 2, D, "ffn1_dw3", _Multi([_ChipExchange(pb_w2), _PairExchange(q_w1)]))
        t2_w2 = got[:1]
        pf_w1, pb_w1 = _pair_add(place, q_w1, got[1:], name="ffn1w1_pair_add")
        q_w3 = [_quarters(grads["ffn1_w3"], KIND["ffn1_w3"])]
        (dx0, s1), got = _dh_normbwd(dh_pairs, x, vec["norm_ffn1_g"], sc1, dx1, name="ffn1_bwd_dh",
                                     comm=_Multi([_ChipExchange(pb_w1), _PairExchange(q_w3)]))
        t2_w1 = got[:1]
        pf_w3, pb_w3 = _pair_add(place, q_w3, got[1:], name="ffn1w3_pair_add")
        t2_w3 = _run_comm(_ChipExchange(pb_w3), name="ffn1w3_chip_exchange")
        rh = _chip_add(place, pf_w1 + pf_w3 + pf_w2, t2_w1 + t2_w3 + t2_w2, name="ffn1g_chip_add")
        reduced.update(_assemble(place, G_FFN1, rh, _run_comm(_PairShare(rh), name="ffn1g_pair_share")))
    else:
        grads["ffn1_w2"], _ = tn(u1, df1, FF // 2, D, "ffn1_dw2")
        grads["ffn1_w1"], _ = tn(da1, h1, FF // 2, D, "ffn1_dw1")
        grads["ffn1_w3"], _ = tn(db1, h1, FF // 2, D, "ffn1_dw3")
        (dx0, s1), _ = _dh_normbwd(dh_pairs, x, vec["norm_ffn1_g"], sc1, dx1, name="ffn1_bwd_dh")
        reduced = grads

    def part(s, k):
        return s[0:1, k * D:(k + 1) * D]

    dmod = jnp.concatenate([part(s1, 1), part(s1, 0), dg1[0:1], part(s2, 1), part(s2, 0), part(s_out, 0),
                            part(s3, 1), part(s3, 0), dg3[0:1]], axis=1)
    small = {"norm_ffn1_g": part(s1, 2), "norm_mix_g": part(s2, 2), "out_norm_g": part(s_out, 1),
             "norm_ffn2_g": part(s3, 2), "final_norm_g": dgfin[0:1],
             "q_norm_g": s_mid[0:1, 3 * CONV_W:3 * CONV_W + Q_LORA],
             "kv_norm_g": s_mid[0:1, 3 * CONV_W + Q_LORA:MID_SUMS], "conv_w": s_mid[0:1, 0:3 * CONV_W]}
    return loss_blk, dx0, reduced, dmod, small


SMALL = [("norm_ffn1_g", D), ("norm_mix_g", D), ("out_norm_g", D), ("norm_ffn2_g", D), ("final_norm_g", D),
         ("q_norm_g", Q_LORA), ("kv_norm_g", KV_LORA), ("conv_w", 3 * CONV_W)]
WEIGHTS = ['ada_w', 'ada_b', 'norm_ffn1_g', 'ffn1_w1', 'ffn1_w3', 'ffn1_w2', 'norm_mix_g', 'w_in', 'conv_w',
           'q_norm_g', 'w_uq', 'kv_norm_g', 'w_ukv', 'out_norm_g', 'w_out', 'norm_ffn2_g', 'ffn2_w1', 'ffn2_w3',
           'ffn2_w2', 'final_norm_g']


def kernel(x, c, positions, ada_w, ada_b, norm_ffn1_g, ffn1_w1, ffn1_w3, ffn1_w2, norm_mix_g, w_in, conv_w, q_norm_g, w_uq, kv_norm_g, w_ukv, out_norm_g, w_out, norm_ffn2_g, ffn2_w1, ffn2_w3, ffn2_w2, final_norm_g, loss_target, m_ada_w, m_ada_b, m_norm_ffn1_g, m_ffn1_w1, m_ffn1_w3, m_ffn1_w2, m_norm_mix_g, m_w_in, m_conv_w, m_q_norm_g, m_w_uq, m_kv_norm_g, m_w_ukv, m_out_norm_g, m_w_out, m_norm_ffn2_g, m_ffn2_w1, m_ffn2_w3, m_ffn2_w2, m_final_norm_g, v_ada_w, v_ada_b, v_norm_ffn1_g, v_ffn1_w1, v_ffn1_w3, v_ffn1_w2, v_norm_mix_g, v_w_in, v_conv_w, v_q_norm_g, v_w_uq, v_kv_norm_g, v_w_ukv, v_out_norm_g, v_w_out, v_norm_ffn2_g, v_ffn2_w1, v_ffn2_w3, v_ffn2_w2, v_final_norm_g):
    args = dict(locals())
    wts = {n: args[n] for n in WEIGHTS}
    mom = {n: args["m_" + n] for n in WEIGHTS}
    var = {n: args["v_" + n] for n in WEIGHTS}
    ax, ay, ac = _place()
    myq = 2 * ax + ay
    me = 2 * myq + ac
    place = jnp.stack([ac, myq]).astype(jnp.int32)

    shards = {name: _working_shard(wts[name], kind).astype(BF16) for name, kind in BULK}
    first = _run_comm(_Gather([shards[b[0]] for b in W_FIRST]), name="gather_ffn1")
    w_first = _gathered_weights(W_FIRST, shards, first, myq)

    mine = jnp.concatenate([c, conv_w[0].reshape(1, 3 * CONV_W // N_CHIPS)], axis=1)
    seen = _small_allgather(jnp.pad(mine, ((0, 7), (0, 0))), name="gather_cond").reshape(N_DEV, 8, -1)[:, 0]
    c_all = jnp.pad(seen[:, :D], ((0, 8), (0, 0)))
    conv_full = jnp.transpose(seen[0::2, D:].reshape(N_CHIPS, 3, CONV_W // N_CHIPS), (1, 0, 2)).reshape(3, CONV_W)
    ada_b_q = lax.dynamic_slice_in_dim(ada_b, myq * ADA_Q, ADA_Q, axis=1)
    mod_q = _ada_forward(c_all, ada_w[0], ada_b_q)
    mod_all = _small_allgather(mod_q, name="gather_mod").reshape(N_DEV, 16, ADA_Q)
    mod_rows = jnp.transpose(mod_all[0::2, :N_DEV], (1, 0, 2)).reshape(N_DEV, N_MOD * D)
    mod = lax.dynamic_slice_in_dim(mod_rows, me, 1, axis=0).reshape(N_MOD, D)

    vec = {n: wts[n] for n in ("norm_ffn1_g", "norm_mix_g", "q_norm_g", "kv_norm_g", "out_norm_g", "norm_ffn2_g")}
    vec["final_norm_g"] = final_norm_g.reshape(1, D)
    loss_blk, grad_x, gq, dmod, small = _local_step(x[0], positions[0], loss_target[0], mod, vec, conv_full, w_first,
                                                    {b[0]: shards[b[0]] for b in W_REST}, place)
    loss = lax.psum(loss_blk[0, 0], ("x", "y", "c"))

    rows = jnp.concatenate([dmod] + [small[n] for n, _ in SMALL], axis=1)
    width = rows.shape[1]
    fold = -(-width // (8 * 128)) * 128
    rows = jnp.pad(rows, ((0, 0), (0, 8 * fold - width))).reshape(8, fold)
    every = _small_allgather(rows, name="gather_small").reshape(N_DEV, 8 * fold)[:, :width]
    total = _sum_devices(every)[0:1]
    dmod_q = lax.dynamic_slice_in_dim(every[:, :N_MOD * D], myq * ADA_Q, ADA_Q, axis=1)
    g = {name: gq[name] for name, *_ in BULK}
    g["ada_w"] = _ada_wgrad(c_all, jnp.pad(dmod_q, ((0, 8), (0, 0))))
    g["ada_b"] = total[:, :N_MOD * D]
    off = N_MOD * D
    for n, width in SMALL:
        g[n] = total[:, off:off + width]
        off += width
    g["conv_w"] = lax.dynamic_slice_in_dim(g["conv_w"].reshape(3, CONV_W), myq * (CONV_W // N_CHIPS),
                                           CONV_W // N_CHIPS, axis=1)
    g["final_norm_g"] = g["final_norm_g"].reshape(D)

    delta, new_m, new_v = {}, {}, {}
    for name in ["ada_w"] + [b[0] for b in BULK]:
        view = (lambda a: jnp.swapaxes(a, 1, 2)) if KIND.get(name) == "colsT" else (lambda a: a)
        g[name], delta[name], new_m[name], new_v[name] = [
            view(r) for r in _adamw(view(wts[name]), g[name], view(mom[name]), view(var[name]), name="adamw_" + name)]
    smalls = ["ada_b"] + [n for n, _ in SMALL]

    def packed(d):
        flat = jnp.concatenate([d[n].reshape(1, -1) for n in smalls], axis=1)
        return jnp.pad(flat.reshape(-1, D), ((0, 1), (0, 0)))

    res = _adamw(packed(wts)[None], packed(g), packed(mom)[None], packed(var)[None], name="adamw_small")[1:]
    off = 0
    for n in smalls:
        size = wts[n].size
        for d, r in zip((delta, new_m, new_v), res):
            d[n] = r.reshape(-1)[off:off + size].reshape(wts[n].shape)
        g[n] = g[n].reshape(wts[n].shape)
        off += size

    return (loss, grad_x[None], *[g[n] for n in WEIGHTS], *[delta[n] for n in WEIGHTS],
            *[new_m[n] for n in WEIGHTS], *[new_v[n] for n in WEIGHTS])
```

```python
import functools

import numpy as np
import jax
import jax.numpy as jnp
from jax import lax
from jax.experimental import pallas as pl
from jax.experimental.pallas import tpu as pltpu

F32 = jnp.float32
BF16 = jnp.bfloat16
MESH = pl.DeviceIdType.MESH

D = 1024
FF = 2816
CONV_W = 512
CONV_GROUP = 64
HEADS = 4
QK_NOPE = 128
QK_ROPE = 64
V_HEAD = 128
Q_LORA = 384
KV_LORA = 256
HEAD_SLOT = 256
IN_COLS = 3 * CONV_W + Q_LORA + KV_LORA + QK_ROPE
Z_COLS = 2304
EPS = 1e-6
ROPE_THETA = 10000.0
CHUNK = 64
ATT_SCALE = (QK_NOPE + QK_ROPE) ** -0.5
NEG = -1e30
EXP2_SCALE = ATT_SCALE * 1.4426950408889634
N_MOD = 9

LR, B1, B2, AEPS, WD, STEP = 0.001, 0.9, 0.999, 1e-08, 0.01, 10

N_CHIPS = 4
N_DEV = 8
VMEM_LIMIT = 56 << 20


def _params(sem, vmem=VMEM_LIMIT):
    return pltpu.CompilerParams(dimension_semantics=sem, vmem_limit_bytes=vmem)


def _rms(v):
    return lax.rsqrt(jnp.mean(v * v, axis=-1, keepdims=True) + EPS)


def _rsum8(v):
    t, n = v.shape
    return jnp.sum(v.reshape(t // 8, 8, n), axis=0)


def _all_rows(ref):
    ref[...] = jnp.broadcast_to(jnp.sum(ref[...], axis=0, keepdims=True), ref.shape)


def _gsum(v, gmat, split=False):
    hi = v.astype(BF16)
    out = jnp.dot(hi, gmat, preferred_element_type=F32)
    if split:
        out = out + jnp.dot((v - hi.astype(F32)).astype(BF16), gmat, preferred_element_type=F32)
    return out


def _dot_nt(a, b):
    return lax.dot_general(a, b, (((1,), (1,)), ((), ())), preferred_element_type=F32)


def _silu_parts(a):
    sg = jax.nn.sigmoid(a)
    return sg, a * sg


def _rope(xr, cs, sn, lane):
    rh = jnp.where(lane < 32, -pltpu.roll(xr, 96, 1), pltpu.roll(xr, 32, 1))
    return xr * cs + rh * sn


def _rope_t(g, cs, sn, lane):
    y = g * sn
    rt = jnp.where(lane < 32, pltpu.roll(y, 96, 1), jnp.where(lane < 64, -pltpu.roll(y, 32, 1), 0.0))
    return g * cs + rt


def _row_tile(rows, pref, mult=8):
    t = min(rows, pref) // mult * mult
    while rows % t:
        t -= mult
    return t


def _place():
    return lax.axis_index("x"), lax.axis_index("y"), lax.axis_index("c")


ANY = pl.BlockSpec(memory_space=pl.ANY)


def _hosted_call(body, *, name, grid, in_specs, out_specs, out_shape, scratch_shapes, semantics, args, comm=None):
    n_in, n_out, n_scr = len(in_specs), len(out_specs), len(scratch_shapes)
    if comm is None:
        res = pl.pallas_call(body, name=name, grid=grid, in_specs=in_specs, out_specs=out_specs, out_shape=out_shape,
                             scratch_shapes=scratch_shapes, compiler_params=_params(semantics))(*args)
        return list(res), []
    n_ci, n_co = len(comm.inputs), len(comm.out_shapes)
    total = int(np.prod(grid))

    def hosted(*refs):
        ins, refs = refs[:n_in], refs[n_in:]
        cins, refs = refs[:n_ci], refs[n_ci:]
        outs, refs = refs[:n_out], refs[n_out:]
        couts, refs = refs[:n_co], refs[n_co:]
        scratch, sems = refs[:n_scr], refs[n_scr]
        step = pl.program_id(0)
        for ax in range(1, len(grid)):
            step = step * grid[ax] + pl.program_id(ax)

        @pl.when(step == 0)
        def _():
            comm.start(cins, couts, sems)

        body(*ins, *outs, *scratch)

        @pl.when(step == total - 1)
        def _():
            comm.finish(cins, couts, sems)

    res = pl.pallas_call(
        hosted, name=name, grid=grid, in_specs=list(in_specs) + [ANY] * n_ci,
        out_specs=list(out_specs) + [ANY] * n_co, out_shape=list(out_shape) + list(comm.out_shapes),
        scratch_shapes=list(scratch_shapes) + [pltpu.SemaphoreType.DMA((comm.n_sems,))],
        compiler_params=_params(("arbitrary",) * len(grid)))(*args, *comm.inputs)
    return list(res[:n_out]), list(res[n_out:])


def _run_comm(comm, *, name):
    n_ci = len(comm.inputs)

    def body(*refs):
        cins, couts, sems = refs[:n_ci], refs[n_ci:-1], refs[-1]
        comm.start(cins, couts, sems)
        comm.finish(cins, couts, sems)

    return list(pl.pallas_call(
        body, name=name, out_shape=list(comm.out_shapes), in_specs=[ANY] * n_ci,
        out_specs=[ANY] * len(comm.out_shapes), scratch_shapes=[pltpu.SemaphoreType.DMA((comm.n_sems,))],
    )(*comm.inputs))


class _Gather:
    def __init__(self, slabs):
        self.inputs = list(slabs)
        self.out_shapes = [jax.ShapeDtypeStruct((N_CHIPS,) + s.shape, s.dtype) for s in slabs]
        self.n_sems = 12 * len(slabs)

    @staticmethod
    def _copy(out, sems, base, k, chip, hc, to, src=None):
        H = out.shape[1] // 2
        half = out.at[2 * chip[0] + chip[1], pl.ds(hc * H, H), :]
        return pltpu.make_async_remote_copy(
            src_ref=half if src is None else src, dst_ref=half, send_sem=sems.at[base + k],
            recv_sem=sems.at[base + 6 + k], device_id=to, device_id_type=MESH)

    def _firsts(self, src, out, sems, base):
        x, y, c = _place()
        H = src.shape[0] // 2
        chips = [(1 - x, y), (x, 1 - y), (1 - x, 1 - y)]
        return [self._copy(out, sems, base, j, (x, y), c, (*chip, c), src=src.at[pl.ds(c * H, H), :])
                for j, chip in enumerate(chips)]

    def start(self, ins, outs, sems):
        for i, (src, out) in enumerate(zip(ins, outs)):
            for cp in self._firsts(src, out, sems, 12 * i):
                cp.start()

    def finish(self, ins, outs, sems):
        x, y, c = _place()
        chips = [(1 - x, y), (x, 1 - y), (1 - x, 1 - y)]
        passed = []
        for i, out in enumerate(outs):
            for j, chip in enumerate(chips):
                self._copy(out, sems, 12 * i, j, chip, c, (x, y, c)).wait_recv()
                cp = self._copy(out, sems, 12 * i, 3 + j, chip, c, (x, y, 1 - c))
                cp.start()
                passed.append(cp)
        for i, out in enumerate(outs):
            for j, chip in enumerate(chips):
                self._copy(out, sems, 12 * i, 3 + j, chip, 1 - c, (x, y, c)).wait_recv()
        for cp in passed:
            cp.wait_send()
        for i, (src, out) in enumerate(zip(ins, outs)):
            for cp in self._firsts(src, out, sems, 12 * i):
                cp.wait_send()


class _PairExchange:
    def __init__(self, arrays):
        self.inputs = list(arrays)
        self.out_shapes = [jax.ShapeDtypeStruct((N_CHIPS, a.shape[1] // 2, a.shape[2]), a.dtype) for a in arrays]
        self.n_sems = 2 * len(arrays)

    def _copies(self, ins, outs, sems):
        x, y, c = _place()
        return [pltpu.make_async_remote_copy(
            src_ref=g.at[:, pl.ds((1 - c) * t.shape[1], t.shape[1]), :], dst_ref=t, send_sem=sems.at[2 * i],
            recv_sem=sems.at[2 * i + 1], device_id=(x, y, 1 - c), device_id_type=MESH)
            for i, (g, t) in enumerate(zip(ins, outs))]

    def start(self, ins, outs, sems):
        for cp in self._copies(ins, outs, sems):
            cp.start()

    def finish(self, ins, outs, sems):
        for cp in self._copies(ins, outs, sems):
            cp.wait()


class _ChipExchange:
    def __init__(self, arrays):
        self.inputs = list(arrays)
        self.out_shapes = [jax.ShapeDtypeStruct(a.shape, a.dtype) for a in arrays]
        self.n_sems = 6 * len(arrays)

    def _copies(self, p, t, sems, base):
        x, y, c = _place()
        myq = 2 * x + y
        chips = [(1 - x, y), (x, 1 - y), (1 - x, 1 - y)]
        sends = [pltpu.make_async_remote_copy(
            src_ref=p.at[2 * chip[0] + chip[1]], dst_ref=t.at[myq], send_sem=sems.at[base + j],
            recv_sem=sems.at[base + 3 + j], device_id=(*chip, c), device_id_type=MESH) for j, chip in enumerate(chips)]
        lands = [pltpu.make_async_remote_copy(
            src_ref=t.at[2 * chip[0] + chip[1]], dst_ref=t.at[2 * chip[0] + chip[1]], send_sem=sems.at[base + j],
            recv_sem=sems.at[base + 3 + j], device_id=(*chip, c), device_id_type=MESH) for j, chip in enumerate(chips)]
        return sends, lands

    def start(self, ins, outs, sems):
        for i, (p, t) in enumerate(zip(ins, outs)):
            for cp in self._copies(p, t, sems, 6 * i)[0]:
                cp.start()

    def finish(self, ins, outs, sems):
        for i, (p, t) in enumerate(zip(ins, outs)):
            sends, lands = self._copies(p, t, sems, 6 * i)
            for cp in lands:
                cp.wait_recv()
            for cp in sends:
                cp.wait_send()


class _SemView:
    def __init__(self, sems, base):
        self._sems, self._base = sems, base

    @property
    def at(self):
        return self

    def __getitem__(self, k):
        return self._sems.at[self._base + k]


class _Multi:
    def __init__(self, progs):
        self.progs = list(progs)
        self.inputs = [a for p in self.progs for a in p.inputs]
        self.out_shapes = [s for p in self.progs for s in p.out_shapes]
        self.n_sems = sum(p.n_sems for p in self.progs)

    def _each(self, ins, outs, sems):
        i = o = s = 0
        for p in self.progs:
            ni, no = len(p.inputs), len(p.out_shapes)
            yield p, ins[i:i + ni], outs[o:o + no], _SemView(sems, s)
            i, o, s = i + ni, o + no, s + p.n_sems

    def start(self, ins, outs, sems):
        for p, a, b, c in self._each(ins, outs, sems):
            p.start(a, b, c)

    def finish(self, ins, outs, sems):
        for p, a, b, c in self._each(ins, outs, sems):
            p.finish(a, b, c)


class _PairShare:
    def __init__(self, arrays):
        self.inputs = list(arrays)
        self.out_shapes = [jax.ShapeDtypeStruct(a.shape, a.dtype) for a in arrays]
        self.n_sems = 2 * len(arrays)

    def _copies(self, ins, outs, sems):
        x, y, c = _place()
        return [pltpu.make_async_remote_copy(
            src_ref=r, dst_ref=o, send_sem=sems.at[2 * i], recv_sem=sems.at[2 * i + 1],
            device_id=(x, y, 1 - c), device_id_type=MESH) for i, (r, o) in enumerate(zip(ins, outs))]

    def start(self, ins, outs, sems):
        for cp in self._copies(ins, outs, sems):
            cp.start()

    def finish(self, ins, outs, sems):
        for cp in self._copies(ins, outs, sems):
            cp.wait()


def _ffn_up(x, ng, sh, sc, w1, w3, *, name, comm=None):
    S = x.shape[0]
    tm, tn = _row_tile(S, 512), FF // 2

    def body(x_ref, g_ref, sh_ref, sc_ref, w1_ref, w3_ref, h_ref, a_ref, b_ref, u_ref, hs):
        @pl.when(pl.program_id(1) == 0)
        def _():
            xv = x_ref[...]
            h = ((xv * _rms(xv)) * g_ref[...]) * (1.0 + sc_ref[...]) + sh_ref[...]
            hb = h.astype(BF16)
            hs[...] = hb
            h_ref[...] = hb

        h = hs[...]
        cols = pl.ds(pl.multiple_of(pl.program_id(1) * tn, tn), tn)
        a = _dot_nt(h, w1_ref[cols, :])
        b = _dot_nt(h, w3_ref[cols, :])
        _, sa = _silu_parts(a)
        a_ref[...] = a.astype(BF16)
        b_ref[...] = b.astype(BF16)
        u_ref[...] = (sa * b).astype(BF16)

    row = pl.BlockSpec((tm, D), lambda i, j: (i, 0))
    vec = pl.BlockSpec((1, D), lambda i, j: (0, 0))
    wsp = pl.BlockSpec((FF, D), lambda i, j: (0, 0))
    osp = pl.BlockSpec((tm, tn), lambda i, j: (i, j))
    return _hosted_call(
        body, name=name, grid=(S // tm, FF // tn),
        in_specs=[row, vec, vec, vec, wsp, wsp],
        out_specs=[row, osp, osp, osp],
        out_shape=[jax.ShapeDtypeStruct((S, D), BF16)] + [jax.ShapeDtypeStruct((S, FF), BF16)] * 3,
        scratch_shapes=[pltpu.VMEM((tm, D), BF16)],
        semantics=("parallel", "arbitrary"), args=(x, ng, sh, sc, w1, w3), comm=comm)


def _ffn_down(u, w2, x, gate, *, name, comm=None):
    S = x.shape[0]
    tm = _row_tile(S, 512)

    def body(u_ref, w2_ref, x_ref, g_ref, xo_ref, f_ref):
        f = jnp.dot(u_ref[...], w2_ref[...], preferred_element_type=F32)
        xo_ref[...] = x_ref[...] + (0.5 * g_ref[...]) * f
        f_ref[...] = f.astype(BF16)

    return _hosted_call(
        body, name=name, grid=(S // tm,),
        in_specs=[pl.BlockSpec((tm, FF), lambda i: (i, 0)), pl.BlockSpec((FF, D), lambda i: (0, 0)),
                  pl.BlockSpec((tm, D), lambda i: (i, 0)), pl.BlockSpec((1, D), lambda i: (0, 0))],
        out_specs=[pl.BlockSpec((tm, D), lambda i: (i, 0))] * 2,
        out_shape=[jax.ShapeDtypeStruct((S, D), F32), jax.ShapeDtypeStruct((S, D), BF16)],
        scratch_shapes=[], semantics=("parallel",), args=(u, w2, x, gate), comm=comm)


def _ffn_bwd_du(dx, gate, f, w2, a, b, *, name, comm=None):
    S = dx.shape[0]
    tm, tn = _row_tile(S, 512), FF // 2
    n_i = S // tm

    def body(dx_ref, g_ref, f_ref, w_ref, a_ref, b_ref, df_ref, da_ref, db_ref, dg_ref, dfs):
        i, j = pl.program_id(0), pl.program_id(1)

        @pl.when((i == 0) & (j == 0))
        def _():
            dg_ref[...] = jnp.zeros_like(dg_ref)

        @pl.when(j == 0)
        def _():
            dxv = dx_ref[...]
            dfb = (dxv * (0.5 * g_ref[...])).astype(BF16)
            dfs[...] = dfb
            df_ref[...] = dfb
            dg_ref[...] += _rsum8(dxv * (0.5 * f_ref[...].astype(F32)))

        du = _dot_nt(dfs[...], w_ref[pl.ds(pl.multiple_of(j * tn, tn), tn), :])
        av = a_ref[...].astype(F32)
        sg, sa = _silu_parts(av)
        da_ref[...] = (du * b_ref[...].astype(F32) * (sg * (1.0 + av * (1.0 - sg)))).astype(BF16)
        db_ref[...] = (du * sa).astype(BF16)

        @pl.when((i == n_i - 1) & (j == FF // tn - 1))
        def _():
            _all_rows(dg_ref)

    row = pl.BlockSpec((tm, D), lambda i, j: (i, 0))
    blk = pl.BlockSpec((tm, tn), lambda i, j: (i, j))
    return _hosted_call(
        body, name=name, grid=(n_i, FF // tn),
        in_specs=[row, pl.BlockSpec((1, D), lambda i, j: (0, 0)), row,
                  pl.BlockSpec((FF, D), lambda i, j: (0, 0)), blk, blk],
        out_specs=[row, blk, blk, pl.BlockSpec((8, D), lambda i, j: (0, 0))],
        out_shape=[jax.ShapeDtypeStruct((S, D), BF16), jax.ShapeDtypeStruct((S, FF), BF16),
                   jax.ShapeDtypeStruct((S, FF), BF16), jax.ShapeDtypeStruct((8, D), F32)],
        scratch_shapes=[pltpu.VMEM((tm, D), BF16)],
        semantics=("arbitrary", "arbitrary"), args=(dx, gate, f, w2, a, b), comm=comm)


def _tn_matmul(a, b, *, tm, tn, name, comm=None):
    S, M = a.shape
    N = b.shape[1]
    ts = _row_tile(S, 2048)
    ns = S // ts

    def body(a_ref, b_ref, o_ref):
        s = pl.program_id(2)
        p = lax.dot_general(a_ref[...], b_ref[...], (((0,), (0,)), ((), ())), preferred_element_type=F32)

        @pl.when(s == 0)
        def _():
            o_ref[...] = p

        @pl.when(s > 0)
        def _():
            o_ref[...] += p

    (out,), couts = _hosted_call(
        body, name=name, grid=(M // tm, N // tn, ns),
        in_specs=[pl.BlockSpec((ts, tm), lambda i, j, s: (s, i)), pl.BlockSpec((ts, tn), lambda i, j, s: (s, j))],
        out_specs=[pl.BlockSpec((tm, tn), lambda i, j, s: (i, j))],
        out_shape=[jax.ShapeDtypeStruct((M, N), F32)],
        scratch_shapes=[], semantics=("parallel", "parallel", "arbitrary"), args=(a, b), comm=comm)
    return out if comm is None else (out, couts)


def _dh_normbwd(pairs, x, ng, sc, dx_next, *, name, comm=None):
    S = x.shape[0]
    n_p = len(pairs)
    tm = _row_tile(S, 512 if n_p == 1 else 256)
    n_i = S // tm

    def body(*refs):
        a_refs, w_refs = refs[:n_p], refs[n_p:2 * n_p]
        x_ref, g_ref, sc_ref, dxn_ref, dx_ref, p_ref = refs[2 * n_p:]
        i = pl.program_id(0)
        dh = jnp.dot(a_refs[0][...], w_refs[0][...], preferred_element_type=F32)
        for k in range(1, n_p):
            dh = dh + jnp.dot(a_refs[k][...], w_refs[k][...], preferred_element_type=F32)
        xv = x_ref[...]
        r = _rms(xv)
        xh = xv * r
        g = g_ref[...]
        dn = dh * (1.0 + sc_ref[...])
        dy = dn * g
        dx_ref[...] = dxn_ref[...] + r * (dy - xh * jnp.mean(dy * xh, axis=-1, keepdims=True))

        @pl.when(i == 0)
        def _():
            p_ref[...] = jnp.zeros_like(p_ref)

        p_ref[:, 0:D] += _rsum8(dh * (xh * g))
        p_ref[:, D:2 * D] += _rsum8(dh)
        p_ref[:, 2 * D:3 * D] += _rsum8(dn * xh)

        @pl.when(i == n_i - 1)
        def _():
            _all_rows(p_ref)

    row = pl.BlockSpec((tm, D), lambda i: (i, 0))
    vec = pl.BlockSpec((1, D), lambda i: (0, 0))
    in_specs = ([pl.BlockSpec((tm, a.shape[1]), lambda i: (i, 0)) for a, _ in pairs]
                + [pl.BlockSpec(w.shape, lambda i: (0, 0)) for _, w in pairs] + [row, vec, vec, row])
    return _hosted_call(
        body, name=name, grid=(n_i,), in_specs=in_specs,
        out_specs=[row, pl.BlockSpec((8, 3 * D), lambda i: (0, 0))],
        out_shape=[jax.ShapeDtypeStruct((S, D), F32), jax.ShapeDtypeStruct((8, 3 * D), F32)],
        scratch_shapes=[], semantics=("arbitrary",),
        args=(*[a for a, _ in pairs], *[w for _, w in pairs], x, ng, sc, dx_next), comm=comm)


def _final_loss(x3, gfin, tgt):
    S = x3.shape[0]
    tm = _row_tile(S, 512)
    n_i = S // tm

    def body(x_ref, g_ref, t_ref, dx_ref, dg_ref, loss_ref, lacc):
        i = pl.program_id(0)
        xv = x_ref[...]
        r = _rms(xv)
        xh = xv * r
        g = g_ref[...]
        e = xh * g - t_ref[...]
        dout = e * (1.0 / D)
        dy = dout * g
        dx_ref[...] = r * (dy - xh * jnp.mean(dy * xh, axis=-1, keepdims=True))

        @pl.when(i == 0)
        def _():
            dg_ref[...] = jnp.zeros_like(dg_ref)
            lacc[...] = jnp.zeros_like(lacc)

        dg_ref[...] += _rsum8(dout * xh)
        lacc[...] += _rsum8(e * e)

        @pl.when(i == n_i - 1)
        def _():
            _all_rows(dg_ref)
            tot = jnp.sum(jnp.sum(lacc[...], axis=0, keepdims=True), axis=1, keepdims=True)
            loss_ref[...] = jnp.broadcast_to(tot * (0.5 / D), loss_ref.shape)

    row = pl.BlockSpec((tm, D), lambda i: (i, 0))
    return pl.pallas_call(
        body, name="final_loss", grid=(n_i,),
        in_specs=[row, pl.BlockSpec((1, D), lambda i: (0, 0)), row],
        out_specs=[row, pl.BlockSpec((8, D), lambda i: (0, 0)), pl.BlockSpec((8, 128), lambda i: (0, 0))],
        out_shape=[jax.ShapeDtypeStruct((S, D), F32), jax.ShapeDtypeStruct((8, D), F32),
                   jax.ShapeDtypeStruct((8, 128), F32)],
        scratch_shapes=[pltpu.VMEM((8, D), F32)],
        compiler_params=_params(("arbitrary",)),
    )(x3, gfin, tgt)


def _mix_in(x, ng, sh, sc, w_in, comm=None):
    S = x.shape[0]
    tm = _row_tile(S, 512)

    def body(x_ref, g_ref, sh_ref, sc_ref, w_ref, h_ref, z_ref):
        xv = x_ref[...]
        hb = (((xv * _rms(xv)) * g_ref[...]) * (1.0 + sc_ref[...]) + sh_ref[...]).astype(BF16)
        h_ref[...] = hb
        z_ref[...] = _dot_nt(hb, w_ref[...])

    row = pl.BlockSpec((tm, D), lambda i: (i, 0))
    vec = pl.BlockSpec((1, D), lambda i: (0, 0))
    return _hosted_call(
        body, name="mix_in", grid=(S // tm,),
        in_specs=[row, vec, vec, vec, pl.BlockSpec((Z_COLS, D), lambda i: (0, 0))],
        out_specs=[row, pl.BlockSpec((tm, Z_COLS), lambda i: (i, 0))],
        out_shape=[jax.ShapeDtypeStruct((S, D), BF16), jax.ShapeDtypeStruct((S, Z_COLS), F32)],
        scratch_shapes=[], semantics=("parallel",), args=(x, ng, sh, sc, w_in), comm=comm)


def _conv_taps(u, halo, rows):
    u1 = jnp.where(rows == 0, halo[7:8, :], pltpu.roll(u, 1, 0))
    u2 = jnp.where(rows == 0, halo[6:7, :], jnp.where(rows == 1, halo[7:8, :], pltpu.roll(u, 2, 0)))
    return u1, u2


def _mix_mid(z, conv_w, gq, gkv, wuq, wukv, cs, sn, comm=None):
    S = z.shape[0]
    tm = _row_tile(S, 512)
    hb = tm // 8

    def body(z_ref, zh_ref, cw_ref, gq_ref, gkv_ref, wuq_ref, wukv_ref, cs_ref, sn_ref,
             ya_ref, q_ref, k_ref, v_ref, cqn_ref, ckvn_ref):
        i = pl.program_id(0)
        xb = z_ref[:, 0:CONV_W]
        u = z_ref[:, CONV_W:2 * CONV_W] * z_ref[:, 2 * CONV_W:3 * CONV_W]
        halo = zh_ref[:, CONV_W:2 * CONV_W] * zh_ref[:, 2 * CONV_W:3 * CONV_W]
        halo = jnp.where(i > 0, halo, 0.0)
        rows = lax.broadcasted_iota(jnp.int32, (tm, CONV_W), 0)
        u1, u2 = _conv_taps(u, halo, rows)
        y = cw_ref[0:1, :] * u2 + cw_ref[1:2, :] * u1 + cw_ref[2:3, :] * u
        ya_ref[...] = xb * y

        lane = lax.broadcasted_iota(jnp.int32, (tm, 128), 1)
        cs_v, sn_v = cs_ref[...], sn_ref[...]
        cq = z_ref[:, 3 * CONV_W:3 * CONV_W + Q_LORA]
        cqn = ((cq * _rms(cq)) * gq_ref[...]).astype(BF16)
        cqn_ref[...] = cqn
        q = _dot_nt(cqn, wuq_ref[...])
        for h in range(HEADS):
            o = h * HEAD_SLOT
            q_ref[:, o:o + 128] = q[:, o:o + 128].astype(BF16)
            q_ref[:, o + 128:o + 256] = _rope(q[:, o + 128:o + 256], cs_v, sn_v, lane).astype(BF16)

        c0 = 3 * CONV_W + Q_LORA
        ckv = z_ref[:, c0:c0 + KV_LORA]
        ckvn = ((ckv * _rms(ckv)) * gkv_ref[...]).astype(BF16)
        ckvn_ref[...] = ckvn
        kv = jnp.dot(ckvn, wukv_ref[...], preferred_element_type=F32)
        krot = _rope(z_ref[:, c0 + KV_LORA:Z_COLS], cs_v, sn_v, lane).astype(BF16)
        for h in range(HEADS):
            o = h * HEAD_SLOT
            k_ref[:, o:o + 128] = kv[:, h * 128:(h + 1) * 128].astype(BF16)
            k_ref[:, o + 128:o + 256] = krot
        v_ref[...] = kv[:, HEADS * 128:].astype(BF16)

    def rows_of(n):
        return pl.BlockSpec((tm, n), lambda i: (i, 0))

    def whole(shape):
        return pl.BlockSpec(shape, lambda i: (0, 0))

    return _hosted_call(
        body, name="mix_mid", grid=(S // tm,),
        in_specs=[rows_of(Z_COLS), pl.BlockSpec((8, Z_COLS), lambda i: (jnp.maximum(i * hb - 1, 0), 0)),
                  whole((8, CONV_W)), whole((1, Q_LORA)), whole((1, KV_LORA)),
                  whole((HEADS * HEAD_SLOT, Q_LORA)), whole((KV_LORA, 2 * HEADS * 128)),
                  rows_of(128), rows_of(128)],
        out_specs=[rows_of(CONV_W), rows_of(HEADS * HEAD_SLOT), rows_of(HEADS * HEAD_SLOT), rows_of(HEADS * V_HEAD),
                   rows_of(Q_LORA), rows_of(KV_LORA)],
        out_shape=[jax.ShapeDtypeStruct((S, CONV_W), F32), jax.ShapeDtypeStruct((S, HEADS * HEAD_SLOT), BF16),
                   jax.ShapeDtypeStruct((S, HEADS * HEAD_SLOT), BF16), jax.ShapeDtypeStruct((S, HEADS * V_HEAD), BF16),
                   jax.ShapeDtypeStruct((S, Q_LORA), BF16), jax.ShapeDtypeStruct((S, KV_LORA), BF16)],
        scratch_shapes=[], semantics=("parallel",), args=(z, z, conv_w, gq, gkv, wuq, wukv, cs, sn), comm=comm)


def _att_blocks(S):
    bk = min(1024, max(S // 4, 128))
    return bk, bk


def _pair_tables(S, k_major):
    bq, bk = _att_blocks(S)
    nq, nk = S // bq, S // bk
    vis = lambda qi, ki: ki * bk < (qi + 1) * bq
    if k_major:
        pairs = [(qi, ki) for ki in range(nk) for qi in range(nq) if vis(qi, ki)]
    else:
        pairs = [(qi, ki) for qi in range(nq) for ki in range(nk) if vis(qi, ki)]
    cols = [[p[0] for p in pairs], [p[1] for p in pairs], [int((p[1] + 1) * bk > p[0] * bq) for p in pairs]]
    return [jnp.asarray(np.array(c, np.int32)) for c in cols], len(pairs)


def _chunk_mask(qi, ki, bq, bk):
    r = (qi * bq + lax.broadcasted_iota(jnp.int32, (bq, bk), 0)) // CHUNK
    c = (ki * bk + lax.broadcasted_iota(jnp.int32, (bq, bk), 1)) // CHUNK
    return c <= r


def _attention(q, k, v):
    S = q.shape[0]
    bq, bk = _att_blocks(S)
    last_k = bq // bk - 1
    tables, n_pairs = _pair_tables(S, k_major=False)

    def body(qi_ref, ki_ref, mk_ref, q_ref, k_ref, v_ref, o_ref, lse_ref, m_s, l_s, acc_s):
        p_id = pl.program_id(1)
        qi, ki = qi_ref[p_id], ki_ref[p_id]

        @pl.when(ki == 0)
        def _():
            m_s[...] = jnp.full_like(m_s, NEG)
            l_s[...] = jnp.zeros_like(l_s)
            acc_s[...] = jnp.zeros_like(acc_s)

        def update(masked):
            s = lax.dot_general(q_ref[...], k_ref[...], (((1,), (1,)), ((), ())), preferred_element_type=F32)
            if masked:
                s = jnp.where(_chunk_mask(qi, ki, bq, bk), s, NEG)
            m_prev = m_s[...]
            m_new = jnp.maximum(m_prev, jnp.max(s, axis=1, keepdims=True))
            alpha = jnp.exp2((m_prev - m_new) * EXP2_SCALE)
            p = jnp.exp2((s - jnp.tile(m_new, (1, bk // 128))) * EXP2_SCALE)
            l_s[...] = alpha * l_s[...] + jnp.sum(p, axis=1, keepdims=True)
            acc_s[...] = alpha * acc_s[...] + jnp.dot(p.astype(BF16), v_ref[...], preferred_element_type=F32)
            m_s[...] = m_new

        @pl.when(mk_ref[p_id] == 0)
        def _():
            update(False)

        @pl.when(mk_ref[p_id] == 1)
        def _():
            update(True)

        @pl.when(ki == qi * (last_k + 1) + last_k)
        def _():
            l = l_s[...]
            o_ref[...] = acc_s[...] / l
            lse_ref[...] = m_s[...] * EXP2_SCALE + jnp.log2(l)

    grid_spec = pltpu.PrefetchScalarGridSpec(
        num_scalar_prefetch=3, grid=(HEADS, n_pairs),
        in_specs=[pl.BlockSpec((bq, HEAD_SLOT), lambda h, p, qt, kt, mt: (qt[p], h)),
                  pl.BlockSpec((bk, HEAD_SLOT), lambda h, p, qt, kt, mt: (kt[p], h)),
                  pl.BlockSpec((bk, V_HEAD), lambda h, p, qt, kt, mt: (kt[p], h))],
        out_specs=[pl.BlockSpec((bq, V_HEAD), lambda h, p, qt, kt, mt: (qt[p], h))] * 2,
        scratch_shapes=[pltpu.VMEM((bq, V_HEAD), F32)] * 3)
    return pl.pallas_call(
        body, name="attention", grid_spec=grid_spec,
        out_shape=[jax.ShapeDtypeStruct((S, HEADS * V_HEAD), F32)] * 2,
        compiler_params=_params(("arbitrary", "arbitrary")),
    )(*tables, q, k, v)


def _attention_bwd(q, k, v, do, lse2, delta):
    S = q.shape[0]
    bq, bk = _att_blocks(S)
    nq = S // bq
    tables, n_pairs = _pair_tables(S, k_major=True)

    def body(qi_ref, ki_ref, mk_ref, q_ref, k_ref, v_ref, do_ref, lse_ref, dl_ref, dq_hbm, dk_ref, dv_ref,
             dq_s, dk_s, dv_s, sem):
        head, p_id = pl.program_id(0), pl.program_id(1)
        qi, ki = qi_ref[p_id], ki_ref[p_id]
        rows = pl.ds(pl.multiple_of(qi * bq, bq), bq)

        @pl.when(qi * bq <= ki * bk)
        def _():
            dk_s[...] = jnp.zeros_like(dk_s)
            dv_s[...] = jnp.zeros_like(dv_s)

        def update(masked):
            qv, kv, dov = q_ref[...], k_ref[...], do_ref[...]
            s = lax.dot_general(qv, kv, (((1,), (1,)), ((), ())), preferred_element_type=F32)
            dp = lax.dot_general(dov, v_ref[...], (((1,), (1,)), ((), ())), preferred_element_type=F32)
            if masked:
                s = jnp.where(_chunk_mask(qi, ki, bq, bk), s, NEG)
            p = jnp.exp2(s * EXP2_SCALE - jnp.tile(lse_ref[...], (1, bk // 128)))
            dv_s[...] += lax.dot_general(p.astype(BF16), dov, (((0,), (0,)), ((), ())), preferred_element_type=F32)
            ds = (p * (dp - jnp.tile(dl_ref[...], (1, bk // 128)))).astype(BF16)
            dk_s[...] += lax.dot_general(ds, qv, (((0,), (0,)), ((), ())), preferred_element_type=F32)
            dq = jnp.dot(ds, kv, preferred_element_type=F32)

            @pl.when(ki == 0)
            def _():
                dq_s[rows, :] = dq

            @pl.when(ki > 0)
            def _():
                dq_s[rows, :] += dq

        @pl.when(mk_ref[p_id] == 0)
        def _():
            update(False)

        @pl.when(mk_ref[p_id] == 1)
        def _():
            update(True)

        @pl.when(qi == nq - 1)
        def _():
            dk_ref[...] = dk_s[...] * ATT_SCALE
            dv_ref[...] = dv_s[...]

        @pl.when(p_id == n_pairs - 1)
        def _():
            dq_s[...] = dq_s[...] * ATT_SCALE
            out = pltpu.make_async_copy(
                dq_s, dq_hbm.at[:, pl.ds(pl.multiple_of(head * HEAD_SLOT, HEAD_SLOT), HEAD_SLOT)], sem)
            out.start()
            out.wait()

    grid_spec = pltpu.PrefetchScalarGridSpec(
        num_scalar_prefetch=3, grid=(HEADS, n_pairs),
        in_specs=[pl.BlockSpec((bq, HEAD_SLOT), lambda h, p, qt, kt, mt: (qt[p], h)),
                  pl.BlockSpec((bk, HEAD_SLOT), lambda h, p, qt, kt, mt: (kt[p], h)),
                  pl.BlockSpec((bk, V_HEAD), lambda h, p, qt, kt, mt: (kt[p], h)),
                  pl.BlockSpec((bq, V_HEAD), lambda h, p, qt, kt, mt: (qt[p], h)),
                  pl.BlockSpec((bq, V_HEAD), lambda h, p, qt, kt, mt: (qt[p], h)),
                  pl.BlockSpec((bq, V_HEAD), lambda h, p, qt, kt, mt: (qt[p], h))],
        out_specs=[pl.BlockSpec(memory_space=pl.ANY),
                   pl.BlockSpec((bk, HEAD_SLOT), lambda h, p, qt, kt, mt: (kt[p], h)),
                   pl.BlockSpec((bk, V_HEAD), lambda h, p, qt, kt, mt: (kt[p], h))],
        scratch_shapes=[pltpu.VMEM((S, HEAD_SLOT), F32), pltpu.VMEM((bk, HEAD_SLOT), F32),
                        pltpu.VMEM((bk, V_HEAD), F32), pltpu.SemaphoreType.DMA])
    return pl.pallas_call(
        body, name="attention_bwd", grid_spec=grid_spec,
        out_shape=[jax.ShapeDtypeStruct((S, HEADS * HEAD_SLOT), F32), jax.ShapeDtypeStruct((S, HEADS * HEAD_SLOT), F32),
                   jax.ShapeDtypeStruct((S, HEADS * V_HEAD), F32)],
        compiler_params=_params(("arbitrary", "arbitrary")),
    )(*tables, q, k, v, do, lse2, delta)


def _group_mats():
    def blockdiag(n, g):
        idx = np.arange(n) // g
        return jnp.asarray((idx[:, None] == idx[None, :]).astype(np.float32), dtype=BF16)
    return blockdiag(CONV_W, CONV_GROUP), blockdiag(HEADS * V_HEAD, V_HEAD)


def _mix_out(ya, o, gout, w_out, x, gate, ga, gb):
    S = x.shape[0]
    tm = _row_tile(S, 512)

    def body(ya_ref, o_ref, go_ref, w_ref, x_ref, g_ref, ga_ref, gb_ref, xo_ref, yn_ref, yo_ref):
        yav, ov = ya_ref[...], o_ref[...]
        ra = lax.rsqrt(_gsum(yav * yav, ga_ref[...]) * (1.0 / CONV_GROUP) + EPS)
        rb = lax.rsqrt(_gsum(ov * ov, gb_ref[...]) * (1.0 / V_HEAD) + EPS)
        na = ((yav * ra) * go_ref[:, 0:CONV_W]).astype(BF16)
        nb = ((ov * rb) * go_ref[:, CONV_W:]).astype(BF16)
        yn_ref[:, 0:CONV_W] = na
        yn_ref[:, CONV_W:] = nb
        yo = (jnp.dot(na, w_ref[0:CONV_W, :], preferred_element_type=F32)
              + jnp.dot(nb, w_ref[CONV_W:, :], preferred_element_type=F32))
        xo_ref[...] = x_ref[...] + g_ref[...] * yo
        yo_ref[...] = yo.astype(BF16)

    row = pl.BlockSpec((tm, D), lambda i: (i, 0))
    half = pl.BlockSpec((tm, CONV_W), lambda i: (i, 0))
    vec = pl.BlockSpec((1, D), lambda i: (0, 0))
    sq = pl.BlockSpec((CONV_W, CONV_W), lambda i: (0, 0))
    return pl.pallas_call(
        body, name="mix_out", grid=(S // tm,),
        in_specs=[half, half, vec, pl.BlockSpec((D, D), lambda i: (0, 0)), row, vec, sq, sq],
        out_specs=[row, row, row],
        out_shape=[jax.ShapeDtypeStruct((S, D), F32), jax.ShapeDtypeStruct((S, D), BF16),
                   jax.ShapeDtypeStruct((S, D), BF16)],
        compiler_params=_params(("parallel",)),
    )(ya, o, gout, w_out, x, gate, ga, gb)


def _mix_out_bwd(dx, gate, yo, w_out, ya, o, gout, ga, gb, comm=None):
    S = dx.shape[0]
    tm = _row_tile(S, 512)
    n_i = S // tm

    def norm_bwd(v, dn, gain, gmat, inv_n):
        r = lax.rsqrt(_gsum(v * v, gmat) * inv_n + EPS)
        vh = v * r
        dy = dn * gain
        return r * (dy - vh * (_gsum(dy * vh, gmat) * inv_n)), dn * vh

    def body(dx_ref, g_ref, yo_ref, w_ref, ya_ref, o_ref, go_ref, ga_ref, gb_ref,
             dyo_ref, dya_ref, do_ref, dl_ref, p_ref):
        i = pl.program_id(0)
        dxv = dx_ref[...]
        dyo = (dxv * g_ref[...]).astype(BF16)
        dyo_ref[...] = dyo
        dyn = _dot_nt(dyo, w_ref[...])
        dya, dga = norm_bwd(ya_ref[...], dyn[:, 0:CONV_W], go_ref[:, 0:CONV_W], ga_ref[...], 1.0 / CONV_GROUP)
        ov = o_ref[...]
        do, dgb = norm_bwd(ov, dyn[:, CONV_W:], go_ref[:, CONV_W:], gb_ref[...], 1.0 / V_HEAD)
        dya_ref[...] = dya
        do_ref[...] = do.astype(BF16)
        dl_ref[...] = _gsum(do * ov, gb_ref[...], split=True)

        @pl.when(i == 0)
        def _():
            p_ref[...] = jnp.zeros_like(p_ref)

        p_ref[:, 0:D] += _rsum8(dxv * yo_ref[...].astype(F32))
        p_ref[:, D:D + CONV_W] += _rsum8(dga)
        p_ref[:, D + CONV_W:2 * D] += _rsum8(dgb)

        @pl.when(i == n_i - 1)
        def _():
            _all_rows(p_ref)

    row = pl.BlockSpec((tm, D), lambda i: (i, 0))
    half = pl.BlockSpec((tm, CONV_W), lambda i: (i, 0))
    vec = pl.BlockSpec((1, D), lambda i: (0, 0))
    sq = pl.BlockSpec((CONV_W, CONV_W), lambda i: (0, 0))
    return _hosted_call(
        body, name="mix_out_bwd", grid=(n_i,),
        in_specs=[row, vec, row, pl.BlockSpec((D, D), lambda i: (0, 0)), half, half, vec, sq, sq],
        out_specs=[row, half, half, half, pl.BlockSpec((8, 2 * D), lambda i: (0, 0))],
        out_shape=[jax.ShapeDtypeStruct((S, D), BF16), jax.ShapeDtypeStruct((S, CONV_W), F32),
                   jax.ShapeDtypeStruct((S, CONV_W), BF16), jax.ShapeDtypeStruct((S, CONV_W), F32),
                   jax.ShapeDtypeStruct((8, 2 * D), F32)],
        scratch_shapes=[], semantics=("arbitrary",), args=(dx, gate, yo, w_out, ya, o, gout, ga, gb), comm=comm)


MID_SUMS = 3 * CONV_W + Q_LORA + KV_LORA


def _mix_mid_bwd(z, dya, conv_w, gq, gkv, wuq, wukv, cs, sn, dq, dk, dv, comm=None):
    S = z.shape[0]
    tm = _row_tile(S, 256)
    n_i = S // tm
    hb = tm // 8
    last_blk = S // 8 - 1

    def latent_bwd(cv, dcn, gain):
        r = _rms(cv)
        ch = cv * r
        dy = dcn * gain
        return r * (dy - ch * jnp.mean(dy * ch, axis=-1, keepdims=True)), dcn * ch

    def body(z_ref, zp_ref, zn_ref, dya_ref, dyan_ref, cw_ref, gq_ref, gkv_ref, wuq_ref, wukv_ref, cs_ref, sn_ref,
             dq_ref, dk_ref, dv_ref, dz_ref, dqf_ref, dkvf_ref, p_ref):
        i = pl.program_id(0)
        xb, xc, xu = z_ref[:, 0:CONV_W], z_ref[:, CONV_W:2 * CONV_W], z_ref[:, 2 * CONV_W:3 * CONV_W]
        u = xc * xu
        halo = jnp.where(i > 0, zp_ref[:, CONV_W:2 * CONV_W] * zp_ref[:, 2 * CONV_W:3 * CONV_W], 0.0)
        rows = lax.broadcasted_iota(jnp.int32, (tm, CONV_W), 0)
        u1, u2 = _conv_taps(u, halo, rows)
        w0, w1, w2 = cw_ref[0:1, :], cw_ref[1:2, :], cw_ref[2:3, :]
        y = w0 * u2 + w1 * u1 + w2 * u
        dyav = dya_ref[...]
        dy = dyav * xb
        nxt = jnp.where(i < n_i - 1, dyan_ref[...] * zn_ref[:, 0:CONV_W], 0.0)
        dy1 = jnp.where(rows == tm - 1, nxt[0:1, :], pltpu.roll(dy, tm - 1, 0))
        dy2 = jnp.where(rows == tm - 1, nxt[1:2, :], jnp.where(rows == tm - 2, nxt[0:1, :], pltpu.roll(dy, tm - 2, 0)))
        du = w2 * dy + w1 * dy1 + w0 * dy2
        dz_ref[:, 0:CONV_W] = (dyav * y).astype(BF16)
        dz_ref[:, CONV_W:2 * CONV_W] = (du * xu).astype(BF16)
        dz_ref[:, 2 * CONV_W:3 * CONV_W] = (du * xc).astype(BF16)

        lane = lax.broadcasted_iota(jnp.int32, (tm, 128), 1)
        cs_v, sn_v = cs_ref[...], sn_ref[...]
        dkr = jnp.zeros((tm, 128), F32)
        for h in range(HEADS):
            o = h * HEAD_SLOT
            dqf_ref[:, o:o + 128] = dq_ref[:, o:o + 128].astype(BF16)
            dqf_ref[:, o + 128:o + 256] = _rope_t(dq_ref[:, o + 128:o + 256], cs_v, sn_v, lane).astype(BF16)
            dkvf_ref[:, h * 128:(h + 1) * 128] = dk_ref[:, o:o + 128].astype(BF16)
            dkr = dkr + dk_ref[:, o + 128:o + 256]
        dkvf_ref[:, HEADS * 128:] = dv_ref[...].astype(BF16)

        c0 = 3 * CONV_W
        dcqn = jnp.dot(dqf_ref[...], wuq_ref[...], preferred_element_type=F32)
        dcq, dgq = latent_bwd(z_ref[:, c0:c0 + Q_LORA], dcqn, gq_ref[...])
        dz_ref[:, c0:c0 + Q_LORA] = dcq.astype(BF16)
        c1 = c0 + Q_LORA
        dckvn = _dot_nt(dkvf_ref[...], wukv_ref[...])
        dckv, dgkv = latent_bwd(z_ref[:, c1:c1 + KV_LORA], dckvn, gkv_ref[...])
        dz_ref[:, c1:c1 + KV_LORA] = dckv.astype(BF16)
        dz_ref[:, c1 + KV_LORA:Z_COLS] = _rope_t(dkr, cs_v, sn_v, lane).astype(BF16)

        @pl.when(i == 0)
        def _():
            p_ref[...] = jnp.zeros_like(p_ref)

        p_ref[:, 0:CONV_W] += _rsum8(dy * u2)
        p_ref[:, CONV_W:2 * CONV_W] += _rsum8(dy * u1)
        p_ref[:, 2 * CONV_W:3 * CONV_W] += _rsum8(dy * u)
        p_ref[:, c0:c0 + Q_LORA] += _rsum8(dgq)
        p_ref[:, c1:c1 + KV_LORA] += _rsum8(dgkv)

        @pl.when(i == n_i - 1)
        def _():
            _all_rows(p_ref)

    def rows_of(n):
        return pl.BlockSpec((tm, n), lambda i: (i, 0))

    def whole(shape):
        return pl.BlockSpec(shape, lambda i: (0, 0))

    def prev8(n):
        return pl.BlockSpec((8, n), lambda i: (jnp.maximum(i * hb - 1, 0), 0))

    def next8(n):
        return pl.BlockSpec((8, n), lambda i: (jnp.minimum((i + 1) * hb, last_blk), 0))

    return _hosted_call(
        body, name="mix_mid_bwd", grid=(n_i,),
        in_specs=[rows_of(Z_COLS), prev8(Z_COLS), next8(Z_COLS), rows_of(CONV_W), next8(CONV_W),
                  whole((8, CONV_W)), whole((1, Q_LORA)), whole((1, KV_LORA)),
                  whole((HEADS * HEAD_SLOT, Q_LORA)), whole((KV_LORA, 2 * HEADS * 128)),
                  rows_of(128), rows_of(128),
                  rows_of(HEADS * HEAD_SLOT), rows_of(HEADS * HEAD_SLOT), rows_of(HEADS * V_HEAD)],
        out_specs=[rows_of(Z_COLS), rows_of(HEADS * HEAD_SLOT), rows_of(2 * HEADS * 128), whole((8, MID_SUMS))],
        out_shape=[jax.ShapeDtypeStruct((S, Z_COLS), BF16), jax.ShapeDtypeStruct((S, HEADS * HEAD_SLOT), BF16),
                   jax.ShapeDtypeStruct((S, 2 * HEADS * 128), BF16), jax.ShapeDtypeStruct((8, MID_SUMS), F32)],
        scratch_shapes=[], semantics=("arbitrary",),
        args=(z, z, z, dya, dya, conv_w, gq, gkv, wuq, wukv, cs, sn, dq, dk, dv), comm=comm)


ADA_Q = N_MOD * D // N_CHIPS
ADA_TN = 768


def _ada_forward(c_all, ada_w_q, ada_b_q):
    def body(c_ref, w_ref, b_ref, o_ref):
        cv = c_ref[...]
        sc = (cv * jax.nn.sigmoid(cv)).astype(BF16)
        o_ref[...] = jnp.dot(sc, w_ref[...].astype(BF16), preferred_element_type=F32) + b_ref[...]

    return pl.pallas_call(
        body, name="ada_forward", grid=(ADA_Q // ADA_TN,),
        in_specs=[pl.BlockSpec((16, D), lambda j: (0, 0)), pl.BlockSpec((D, ADA_TN), lambda j: (0, j)),
                  pl.BlockSpec((1, ADA_TN), lambda j: (0, j))],
        out_specs=pl.BlockSpec((16, ADA_TN), lambda j: (0, j)),
        out_shape=jax.ShapeDtypeStruct((16, ADA_Q), F32),
        compiler_params=_params(("parallel",)),
    )(c_all, ada_w_q, ada_b_q)


def _ada_wgrad(c_all, dmod_q):
    def body(c_ref, d_ref, o_ref):
        cv = c_ref[...]
        sc = (cv * jax.nn.sigmoid(cv)).astype(BF16)
        o_ref[...] = lax.dot_general(sc, d_ref[...].astype(BF16), (((0,), (0,)), ((), ())),
                                     preferred_element_type=F32)

    return pl.pallas_call(
        body, name="ada_wgrad", grid=(ADA_Q // ADA_TN,),
        in_specs=[pl.BlockSpec((16, D), lambda j: (0, 0)), pl.BlockSpec((16, ADA_TN), lambda j: (0, j))],
        out_specs=pl.BlockSpec((D, ADA_TN), lambda j: (0, j)),
        out_shape=jax.ShapeDtypeStruct((D, ADA_Q), F32),
        compiler_params=_params(("parallel",)),
    )(c_all, dmod_q)


def _sum_devices(parts):
    n = parts.shape[1]

    def body(p_ref, o_ref):
        o_ref[...] = jnp.broadcast_to(jnp.sum(p_ref[...], axis=0, keepdims=True), o_ref.shape)

    return pl.pallas_call(
        body, name="sum_devices",
        in_specs=[pl.BlockSpec((N_DEV, n), lambda: (0, 0))], out_specs=pl.BlockSpec((N_DEV, n), lambda: (0, 0)),
        out_shape=jax.ShapeDtypeStruct((N_DEV, n), F32),
    )(parts)


def _adamw(w, g, m, v, *, name):
    _, rows, cols = w.shape
    tr = _row_tile(rows, 256)

    def body(w_ref, g_ref, m_ref, v_ref, go_ref, d_ref, mo_ref, vo_ref):
        gv = g_ref[...]
        mn = B1 * m_ref[0] + (1.0 - B1) * gv
        vn = B2 * v_ref[0] + (1.0 - B2) * (gv * gv)
        m_hat = mn / (1.0 - B1 ** STEP)
        v_hat = vn / (1.0 - B2 ** STEP)
        go_ref[0] = gv
        d_ref[0] = -LR * (m_hat / (jnp.sqrt(v_hat) + AEPS) + WD * w_ref[0])
        mo_ref[0] = mn
        vo_ref[0] = vn

    blk = pl.BlockSpec((1, tr, cols), lambda i: (0, i, 0))
    return pl.pallas_call(
        body, name=name, grid=(rows // tr,),
        in_specs=[blk, pl.BlockSpec((tr, cols), lambda i: (i, 0)), blk, blk], out_specs=[blk] * 4,
        out_shape=[jax.ShapeDtypeStruct((1, rows, cols), F32)] * 4,
        compiler_params=_params(("parallel",)),
    )(w, g, m, v)


def _small_allgather(v, *, name):
    m, n = v.shape

    def body(x_ref, out_ref, send_sems, recv_sems, local_sem):
        x, y, c = _place()
        me, sibling = (x, y, c), (x, y, 1 - c)
        chips = [(1 - x, y), (x, 1 - y), (1 - x, 1 - y)]

        def rows(px, py, pc):
            return out_ref.at[pl.ds((4 * px + 2 * py + pc) * m, m), :]

        def copy(k, block, to, src=None):
            return pltpu.make_async_remote_copy(
                src_ref=rows(*block) if src is None else src, dst_ref=rows(*block),
                send_sem=send_sems.at[k], recv_sem=recv_sems.at[k], device_id=to, device_id_type=MESH)

        mine = pltpu.make_async_copy(x_ref, rows(*me), local_sem)
        mine.start()
        first = [copy(0, me, sibling, src=x_ref)]
        first += [copy(1 + j, me, (*chip, c), src=x_ref) for j, chip in enumerate(chips)]
        for cp in first:
            cp.start()
        passed = [copy(4 + j, (*chip, c), sibling) for j, chip in enumerate(chips)]
        for j, chip in enumerate(chips):
            copy(1 + j, (*chip, c), me).wait_recv()
            passed[j].start()
        copy(0, sibling, me).wait_recv()
        for j, chip in enumerate(chips):
            copy(4 + j, (*chip, 1 - c), me).wait_recv()
        for cp in first + passed:
            cp.wait_send()
        mine.wait()

    return pl.pallas_call(
        body, name=name,
        out_shape=jax.ShapeDtypeStruct((N_DEV * m, n), v.dtype),
        in_specs=[pl.BlockSpec(memory_space=pltpu.VMEM)], out_specs=pl.BlockSpec(memory_space=pltpu.VMEM),
        scratch_shapes=[pltpu.SemaphoreType.DMA((7,)), pltpu.SemaphoreType.DMA((7,)), pltpu.SemaphoreType.DMA],
    )(v)


ADD_BLOCKS = 2


def _pair_add(place, gs, ts, *, name):
    n_a = len(gs)

    def body(pl_ref, *refs):
        g_refs, t_refs = refs[:n_a], refs[n_a:2 * n_a]
        pf_refs, pb_refs = refs[2 * n_a:3 * n_a], refs[3 * n_a:]
        for g_ref, t_ref, pf_ref, pb_ref in zip(g_refs, t_refs, pf_refs, pb_refs):
            s = g_ref[...] + t_ref[...]
            pf_ref[...] = s
            pb_ref[...] = s.astype(BF16)

    def blk(t, own_half):
        tr = t.shape[1] // ADD_BLOCKS
        if own_half:
            return pl.BlockSpec((1, tr, t.shape[2]), lambda q, r, p: (q, p[0] * ADD_BLOCKS + r, 0))
        return pl.BlockSpec((1, tr, t.shape[2]), lambda q, r, p: (q, r, 0))

    grid_spec = pltpu.PrefetchScalarGridSpec(
        num_scalar_prefetch=1, grid=(N_CHIPS, ADD_BLOCKS),
        in_specs=[blk(t, True) for t in ts] + [blk(t, False) for t in ts],
        out_specs=[blk(t, False) for t in ts] * 2)
    res = pl.pallas_call(
        body, name=name, grid_spec=grid_spec,
        out_shape=[jax.ShapeDtypeStruct(t.shape, F32) for t in ts] + [jax.ShapeDtypeStruct(t.shape, BF16) for t in ts],
        compiler_params=_params(("parallel", "parallel")),
    )(place, *gs, *ts)
    return list(res[:n_a]), list(res[n_a:])


def _chip_add(place, pfs, ts, *, name):
    n_a = len(pfs)

    def body(pl_ref, *refs):
        pf_refs, t_refs, o_refs = refs[:n_a], refs[n_a:4 * n_a], refs[4 * n_a:]
        for i, (pf_ref, o_ref) in enumerate(zip(pf_refs, o_refs)):
            t1, t2, t3 = t_refs[3 * i:3 * i + 3]
            o_ref[...] = ((pf_ref[0] + t1[0].astype(F32)) + t2[0].astype(F32)) + t3[0].astype(F32)

    def slot(t, j):
        return pl.BlockSpec((1, t.shape[1] // ADD_BLOCKS, t.shape[2]), lambda r, p: (p[1] ^ j, r, 0))

    grid_spec = pltpu.PrefetchScalarGridSpec(
        num_scalar_prefetch=1, grid=(ADD_BLOCKS,),
        in_specs=[slot(t, 0) for t in pfs] + [slot(t, j) for t in ts for j in (1, 2, 3)],
        out_specs=[pl.BlockSpec((t.shape[1] // ADD_BLOCKS, t.shape[2]), lambda r, p: (r, 0)) for t in pfs])
    res = pl.pallas_call(
        body, name=name, grid_spec=grid_spec,
        out_shape=[jax.ShapeDtypeStruct(t.shape[1:], F32) for t in pfs],
        compiler_params=_params(("parallel",)),
    )(place, *pfs, *[t for t in ts for _ in range(3)])
    return list(res)


BULK = [("ffn1_w1", "colsT"), ("ffn1_w3", "colsT"), ("ffn1_w2", "rows"), ("w_in", "cols"), ("w_uq", "colsT"),
        ("w_ukv", "cols"), ("w_out", "rows"), ("ffn2_w1", "colsT"), ("ffn2_w3", "colsT"), ("ffn2_w2", "rows")]
KIND = dict(BULK)


def _group(*names):
    return [b for b in BULK if b[0] in names]


W_FIRST = _group("ffn1_w1", "ffn1_w3")
W_REST = [b for b in BULK if b not in W_FIRST]
W_MIX = _group("ffn1_w2", "w_in", "w_uq", "w_ukv", "w_out")
W_FFN2 = _group("ffn2_w1", "ffn2_w3", "ffn2_w2")
G_FFN2 = _group("ffn2_w1", "ffn2_w3", "ffn2_w2")
G_MIX = _group("w_in", "w_uq", "w_ukv", "w_out")
G_FFN1 = _group("ffn1_w1", "ffn1_w3", "ffn1_w2")


def _gathered_weights(specs, shards, got, myq):
    out = {}
    for (name, kind), part in zip(specs, got):
        part = lax.dynamic_update_slice_in_dim(part, shards[name][None], myq, axis=0)
        out[name] = _full_weight(part, kind)
    return out


def _working_shard(w, kind):
    return jnp.swapaxes(w, 1, 2)[0] if kind == "colsT" else w[0]


def _full_weight(parts, kind):
    if kind == "cols":
        return jnp.transpose(parts, (1, 0, 2)).reshape(parts.shape[1], -1)
    return parts.reshape(-1, parts.shape[2])


def _quarters(g, kind):
    if kind == "cols":
        k, n = g.shape
        return jnp.transpose(g.reshape(k, N_CHIPS, n // N_CHIPS), (1, 0, 2))
    return g.reshape(N_CHIPS, g.shape[0] // N_CHIPS, g.shape[1])


def _pad_heads(w_uq_t):
    w = w_uq_t.reshape(HEADS, QK_NOPE + QK_ROPE, Q_LORA)
    return jnp.pad(w, ((0, 0), (0, HEAD_SLOT - QK_NOPE - QK_ROPE), (0, 0))).reshape(HEADS * HEAD_SLOT, Q_LORA)


def _unpad_heads(g):
    return g.reshape(HEADS, HEAD_SLOT, Q_LORA)[:, :QK_NOPE + QK_ROPE].reshape(HEADS * (QK_NOPE + QK_ROPE), Q_LORA)


def _split_kv(w_ukv):
    return jnp.transpose(w_ukv.reshape(KV_LORA, HEADS, 2, 128), (0, 2, 1, 3)).reshape(KV_LORA, 2 * HEADS * 128)


def _merge_kv(g):
    return jnp.transpose(g.reshape(KV_LORA, 2, HEADS, 128), (0, 2, 1, 3)).reshape(KV_LORA, 2 * HEADS * 128)


def _rope_tables(positions):
    inv_freq = ROPE_THETA ** (-jnp.arange(0, QK_ROPE, 2, dtype=F32) / QK_ROPE)
    ang = positions.astype(F32)[:, None] * inv_freq
    cos, sin, zero = jnp.cos(ang), jnp.sin(ang), jnp.zeros((positions.shape[0], 64), F32)
    return jnp.concatenate([cos, cos, zero], axis=1), jnp.concatenate([sin, sin, zero], axis=1)


def _reduce_tail(place, specs, pfs, t2s, tag, host=None):
    rhs = _chip_add(place, pfs, t2s, name=tag + "_chip_add")
    if host is None:
        out, others = None, _run_comm(_PairShare(rhs), name=tag + "_pair_share")
    else:
        out, others = host(_PairShare(rhs))
    return out, _assemble(place, specs, rhs, others)


def _assemble(place, specs, rhs, others):
    south = place[0] == 0
    return {b[0]: jnp.concatenate([jnp.where(south, rh, ot), jnp.where(south, ot, rh)], axis=0)
            for b, rh, ot in zip(specs, rhs, others)}


def _local_step(x, positions, target, mod, vec, conv_w, w_first, rest_shards, place):
    row = lambda k: mod[k:k + 1]
    sh1, sc1, g1, sh2, sc2, g2, sh3, sc3, g3 = [row(k) for k in range(N_MOD)]
    cs, sn = _rope_tables(positions)
    cw8 = jnp.pad(conv_w, ((0, 5), (0, 0)))
    ga, gb = _group_mats()
    dist = place is not None
    comm = lambda prog: prog if dist else None

    w = dict(w_first) if dist else {**rest_shards, **w_first}

    def gather(specs):
        return _Gather([rest_shards[b[0]] for b in specs]) if dist else None

    def arrived(specs, got):
        if dist:
            w.update(_gathered_weights(specs, rest_shards, got, place[1]))

    (h1, a1, b1, u1), got = _ffn_up(x, vec["norm_ffn1_g"], sh1, sc1, w["ffn1_w1"], w["ffn1_w3"],
                                    name="ffn1_up", comm=gather(W_MIX))
    arrived(W_MIX, got)
    w_in = jnp.pad(w["w_in"].T, ((0, Z_COLS - IN_COLS), (0, 0)))
    wuq = _pad_heads(w["w_uq"])
    wukv = _split_kv(w["w_ukv"])
    (x1, f1), got = _ffn_down(u1, w["ffn1_w2"], x, g1, name="ffn1_down", comm=gather(W_FFN2[0:1]))
    arrived(W_FFN2[0:1], got)
    (h2, z), got = _mix_in(x1, vec["norm_mix_g"], sh2, sc2, w_in, comm=gather(W_FFN2[1:2]))
    arrived(W_FFN2[1:2], got)
    (ya, q, k, v, cqn, ckvn), got = _mix_mid(z, cw8, vec["q_norm_g"], vec["kv_norm_g"], wuq, wukv, cs, sn,
                                             comm=gather(W_FFN2[2:3]))
    arrived(W_FFN2[2:3], got)
    o, lse = _attention(q, k, v)
    x2, yn, yo = _mix_out(ya, o, vec["out_norm_g"], w["w_out"], x1, g2, ga, gb)
    (h3, a3, b3, u3), _ = _ffn_up(x2, vec["norm_ffn2_g"], sh3, sc3, w["ffn2_w1"], w["ffn2_w3"], name="ffn2_up")
    (x3, f3), _ = _ffn_down(u3, w["ffn2_w2"], x2, g3, name="ffn2_down")
    dx3, dgfin, loss_blk = _final_loss(x3, vec["final_norm_g"], target)

    grads, reduced = {}, {}

    def tn(a, b, tm, tn_, name, prog=None):
        if prog is None:
            return _tn_matmul(a, b, tm=tm, tn=tn_, name=name), None
        return _tn_matmul(a, b, tm=tm, tn=tn_, name=name, comm=prog)

    def slab_of(specs):
        return [_quarters(grads[n], kind) for n, kind in specs]

    (df3, da3, db3, dg3), _ = _ffn_bwd_du(dx3, g3, f3, w["ffn2_w2"], a3, b3, name="ffn2_bwd_du")
    grads["ffn2_w2"], _ = tn(u3, df3, FF // 2, D, "ffn2_dw2")
    grads["ffn2_w1"], _ = tn(da3, h3, FF // 2, D, "ffn2_dw1")
    grads["ffn2_w3"], _ = tn(db3, h3, FF // 2, D, "ffn2_dw3")
    (dx2, s3), _ = _dh_normbwd([(da3, w["ffn2_w1"]), (db3, w["ffn2_w3"])], x2, vec["norm_ffn2_g"], sc3, dx3,
                               name="ffn2_bwd_dh")

    p1 = slab_of(G_FFN2) if dist else None
    (dyo, dya, do, delta, s_out), t1 = _mix_out_bwd(dx2, g2, yo, w["w_out"], ya, o, vec["out_norm_g"], ga, gb,
                                                    comm=comm(_PairExchange(p1) if dist else None))
    grads["w_out"], _ = tn(yn, dyo, D, D, "dw_out")
    if dist:
        pf1, pb1 = _pair_add(place, p1, t1, name="ffn2g_pair_add")
    dq, dk, dv = _attention_bwd(q, k, v, do, lse, delta)
    (dz, dqf, dkvf, s_mid), t2 = _mix_mid_bwd(z, dya, cw8, vec["q_norm_g"], vec["kv_norm_g"], wuq, wukv, cs, sn,
                                              dq, dk, dv, comm=comm(_ChipExchange(pb1) if dist else None))
    g_uq, _ = tn(dqf, cqn, HEADS * HEAD_SLOT, Q_LORA, "dw_uq")
    g_ukv, _ = tn(ckvn, dkvf, KV_LORA, 2 * HEADS * 128, "dw_ukv")
    grads["w_uq"], grads["w_ukv"] = _unpad_heads(g_uq), _merge_kv(g_ukv)
    if dist:
        g_in, red = _reduce_tail(place, G_FFN2, pf1, t2, "ffn2g",
                                 host=lambda prog: tn(h2, dz, D, Z_COLS // 2, "dw_in", prog))
        reduced.update(red)
    else:
        g_in, _ = tn(h2, dz, D, Z_COLS // 2, "dw_in")
    grads["w_in"] = g_in[:, :IN_COLS]
    (dx1, s2), _ = _dh_normbwd([(dz, w_in)], x1, vec["norm_mix_g"], sc2, dx2, name="mix_bwd_dh")

    p2 = slab_of(G_MIX) if dist else None
    (df1, da1, db1, dg1), t1 = _ffn_bwd_du(dx1, g1, f1, w["ffn1_w2"], a1, b1, name="ffn1_bwd_du",
                                           comm=comm(_PairExchange(p2) if dist else None))
    dh_pairs = [(da1, w["ffn1_w1"]), (db1, w["ffn1_w3"])]
    if dist:
        pf2, pb2 = _pair_add(place, p2, t1, name="mixg_pair_add")
        grads["ffn1_w2"], t2 = tn(u1, df1, FF // 2, D, "ffn1_dw2", _ChipExchange(pb2))
        rh_mix = _chip_add(place, pf2, t2, name="mixg_chip_add")
        q_w2 = [_quarters(grads["ffn1_w2"], KIND["ffn1_w2"])]
        grads["ffn1_w1"], got = tn(da1, h1, FF // 2, D, "ffn1_dw1", _Multi([_PairShare(rh_mix), _PairExchange(q_w2)]))
        reduced.update(_assemble(place, G_MIX, rh_mix, got[:len(rh_mix)]))
        pf_w2, pb_w2 = _pair_add(place, q_w2, got[len(rh_mix):], name="ffn1w2_pair_add")
        q_w1 = [_quarters(grads["ffn1_w1"], KIND["ffn1_w1"])]
        grads["ffn1_w3"], got = tn(db1, h1, FF // 2, D, "ffn1_dw3", _Multi([_ChipExchange(pb_w2), _PairExchange(q_w1)]))
        t2_w2 = got[:1]
        pf_w1, pb_w1 = _pair_add(place, q_w1, got[1:], name="ffn1w1_pair_add")
        q_w3 = [_quarters(grads["ffn1_w3"], KIND["ffn1_w3"])]
        (dx0, s1), got = _dh_normbwd(dh_pairs, x, vec["norm_ffn1_g"], sc1, dx1, name="ffn1_bwd_dh",
                                     comm=_Multi([_ChipExchange(pb_w1), _PairExchange(q_w3)]))
        t2_w1 = got[:1]
        pf_w3, pb_w3 = _pair_add(place, q_w3, got[1:], name="ffn1w3_pair_add")
        t2_w3 = _run_comm(_ChipExchange(pb_w3), name="ffn1w3_chip_exchange")
        rh = _chip_add(place, pf_w1 + pf_w3 + pf_w2, t2_w1 + t2_w3 + t2_w2, name="ffn1g_chip_add")
        reduced.update(_assemble(place, G_FFN1, rh, _run_comm(_PairShare(rh), name="ffn1g_pair_share")))
    else:
        grads["ffn1_w2"], _ = tn(u1, df1, FF // 2, D, "ffn1_dw2")
        grads["ffn1_w1"], _ = tn(da1, h1, FF // 2, D, "ffn1_dw1")
        grads["ffn1_w3"], _ = tn(db1, h1, FF // 2, D, "ffn1_dw3")
        (dx0, s1), _ = _dh_normbwd(dh_pairs, x, vec["norm_ffn1_g"], sc1, dx1, name="ffn1_bwd_dh")
        reduced = grads

    def part(s, k):
        return s[0:1, k * D:(k + 1) * D]

    dmod = jnp.concatenate([part(s1, 1), part(s1, 0), dg1[0:1], part(s2, 1), part(s2, 0), part(s_out, 0),
                            part(s3, 1), part(s3, 0), dg3[0:1]], axis=1)
    small = {"norm_ffn1_g": part(s1, 2), "norm_mix_g": part(s2, 2), "out_norm_g": part(s_out, 1),
             "norm_ffn2_g": part(s3, 2), "final_norm_g": dgfin[0:1],
             "q_norm_g": s_mid[0:1, 3 * CONV_W:3 * CONV_W + Q_LORA],
             "kv_norm_g": s_mid[0:1, 3 * CONV_W + Q_LORA:MID_SUMS], "conv_w": s_mid[0:1, 0:3 * CONV_W]}
    return loss_blk, dx0, reduced, dmod, small


SMALL = [("norm_ffn1_g", D), ("norm_mix_g", D), ("out_norm_g", D), ("norm_ffn2_g", D), ("final_norm_g", D),
         ("q_norm_g", Q_LORA), ("kv_norm_g", KV_LORA), ("conv_w", 3 * CONV_W)]
WEIGHTS = ['ada_w', 'ada_b', 'norm_ffn1_g', 'ffn1_w1', 'ffn1_w3', 'ffn1_w2', 'norm_mix_g', 'w_in', 'conv_w',
           'q_norm_g', 'w_uq', 'kv_norm_g', 'w_ukv', 'out_norm_g', 'w_out', 'norm_ffn2_g', 'ffn2_w1', 'ffn2_w3',
           'ffn2_w2', 'final_norm_g']


def kernel(x, c, positions, ada_w, ada_b, norm_ffn1_g, ffn1_w1, ffn1_w3, ffn1_w2, norm_mix_g, w_in, conv_w, q_norm_g, w_uq, kv_norm_g, w_ukv, out_norm_g, w_out, norm_ffn2_g, ffn2_w1, ffn2_w3, ffn2_w2, final_norm_g, loss_target, m_ada_w, m_ada_b, m_norm_ffn1_g, m_ffn1_w1, m_ffn1_w3, m_ffn1_w2, m_norm_mix_g, m_w_in, m_conv_w, m_q_norm_g, m_w_uq, m_kv_norm_g, m_w_ukv, m_out_norm_g, m_w_out, m_norm_ffn2_g, m_ffn2_w1, m_ffn2_w3, m_ffn2_w2, m_final_norm_g, v_ada_w, v_ada_b, v_norm_ffn1_g, v_ffn1_w1, v_ffn1_w3, v_ffn1_w2, v_norm_mix_g, v_w_in, v_conv_w, v_q_norm_g, v_w_uq, v_kv_norm_g, v_w_ukv, v_out_norm_g, v_w_out, v_norm_ffn2_g, v_ffn2_w1, v_ffn2_w3, v_ffn2_w2, v_final_norm_g):
    args = dict(locals())
    wts = {n: args[n] for n in WEIGHTS}
    mom = {n: args["m_" + n] for n in WEIGHTS}
    var = {n: args["v_" + n] for n in WEIGHTS}
    ax, ay, ac = _place()
    myq = 2 * ax + ay
    me = 2 * myq + ac
    place = jnp.stack([ac, myq]).astype(jnp.int32)

    shards = {name: _working_shard(wts[name], kind).astype(BF16) for name, kind in BULK}
    first = _run_comm(_Gather([shards[b[0]] for b in W_FIRST]), name="gather_ffn1")
    w_first = _gathered_weights(W_FIRST, shards, first, myq)

    mine = jnp.concatenate([c, conv_w[0].reshape(1, 3 * CONV_W // N_CHIPS)], axis=1)
    seen = _small_allgather(jnp.pad(mine, ((0, 7), (0, 0))), name="gather_cond").reshape(N_DEV, 8, -1)[:, 0]
    c_all = jnp.pad(seen[:, :D], ((0, 8), (0, 0)))
    conv_full = jnp.transpose(seen[0::2, D:].reshape(N_CHIPS, 3, CONV_W // N_CHIPS), (1, 0, 2)).reshape(3, CONV_W)
    ada_b_q = lax.dynamic_slice_in_dim(ada_b, myq * ADA_Q, ADA_Q, axis=1)
    mod_q = _ada_forward(c_all, ada_w[0], ada_b_q)
    mod_all = _small_allgather(mod_q, name="gather_mod").reshape(N_DEV, 16, ADA_Q)
    mod_rows = jnp.transpose(mod_all[0::2, :N_DEV], (1, 0, 2)).reshape(N_DEV, N_MOD * D)
    mod = lax.dynamic_slice_in_dim(mod_rows, me, 1, axis=0).reshape(N_MOD, D)

    vec = {n: wts[n] for n in ("norm_ffn1_g", "norm_mix_g", "q_norm_g", "kv_norm_g", "out_norm_g", "norm_ffn2_g")}
    vec["final_norm_g"] = final_norm_g.reshape(1, D)
    loss_blk, grad_x, gq, dmod, small = _local_step(x[0], positions[0], loss_target[0], mod, vec, conv_full, w_first,
                                                    {b[0]: shards[b[0]] for b in W_REST}, place)
    loss = lax.psum(loss_blk[0, 0], ("x", "y", "c"))

    rows = jnp.concatenate([dmod] + [small[n] for n, _ in SMALL], axis=1)
    width = rows.shape[1]
    fold = -(-width // (8 * 128)) * 128
    rows = jnp.pad(rows, ((0, 0), (0, 8 * fold - width))).reshape(8, fold)
    every = _small_allgather(rows, name="gather_small").reshape(N_DEV, 8 * fold)[:, :width]
    total = _sum_devices(every)[0:1]
    dmod_q = lax.dynamic_slice_in_dim(every[:, :N_MOD * D], myq * ADA_Q, ADA_Q, axis=1)
    g = {name: gq[name] for name, *_ in BULK}
    g["ada_w"] = _ada_wgrad(c_all, jnp.pad(dmod_q, ((0, 8), (0, 0))))
    g["ada_b"] = total[:, :N_MOD * D]
    off = N_MOD * D
    for n, width in SMALL:
        g[n] = total[:, off:off + width]
        off += width
    g["conv_w"] = lax.dynamic_slice_in_dim(g["conv_w"].reshape(3, CONV_W), myq * (CONV_W // N_CHIPS),
                                           CONV_W // N_CHIPS, axis=1)
    g["final_norm_g"] = g["final_norm_g"].reshape(D)

    delta, new_m, new_v = {}, {}, {}
    for name in ["ada_w"] + [b[0] for b in BULK]:
        view = (lambda a: jnp.swapaxes(a, 1, 2)) if KIND.get(name) == "colsT" else (lambda a: a)
        g[name], delta[name], new_m[name], new_v[name] = [
            view(r) for r in _adamw(view(wts[name]), g[name], view(mom[name]), view(var[name]), name="adamw_" + name)]
    smalls = ["ada_b"] + [n for n, _ in SMALL]

    def packed(d):
        flat = jnp.concatenate([d[n].reshape(1, -1) for n in smalls], axis=1)
        return jnp.pad(flat.reshape(-1, D), ((0, 1), (0, 0)))

    res = _adamw(packed(wts)[None], packed(g), packed(mom)[None], packed(var)[None], name="adamw_small")[1:]
    off = 0
    for n in smalls:
        size = wts[n].size
        for d, r in zip((delta, new_m, new_v), res):
            d[n] = r.reshape(-1)[off:off + size].reshape(wts[n].shape)
        g[n] = g[n].reshape(wts[n].shape)
        off += size

    return (loss, grad_x[None], *[g[n] for n in WEIGHTS], *[delta[n] for n in WEIGHTS],
            *[new_m[n] for n in WEIGHTS], *[new_v[n] for n in WEIGHTS])
```

```python
import functools

import numpy as np
import jax
import jax.numpy as jnp
from jax import lax
from jax.experimental import pallas as pl
from jax.experimental.pallas import tpu as pltpu

F32 = jnp.float32
BF16 = jnp.bfloat16
MESH = pl.DeviceIdType.MESH

D = 1024
FF = 2816
CONV_W = 512
CONV_GROUP = 64
HEADS = 4
QK_NOPE = 128
QK_ROPE = 64
V_HEAD = 128
Q_LORA = 384
KV_LORA = 256
HEAD_SLOT = 256
IN_COLS = 3 * CONV_W + Q_LORA + KV_LORA + QK_ROPE
Z_COLS = 2304
EPS = 1e-6
ROPE_THETA = 10000.0
CHUNK = 64
ATT_SCALE = (QK_NOPE + QK_ROPE) ** -0.5
NEG = -1e30
EXP2_SCALE = ATT_SCALE * 1.4426950408889634
N_MOD = 9

LR, B1, B2, AEPS, WD, STEP = 0.001, 0.9, 0.999, 1e-08, 0.01, 10

N_CHIPS = 4
N_DEV = 8
VMEM_LIMIT = 56 << 20


def _params(sem, vmem=VMEM_LIMIT):
    return pltpu.CompilerParams(dimension_semantics=sem, vmem_limit_bytes=vmem)


def _rms(v):
    return lax.rsqrt(jnp.mean(v * v, axis=-1, keepdims=True) + EPS)


def _rsum8(v):
    t, n = v.shape
    return jnp.sum(v.reshape(t // 8, 8, n), axis=0)


def _all_rows(ref):
    ref[...] = jnp.broadcast_to(jnp.sum(ref[...], axis=0, keepdims=True), ref.shape)


def _gsum(v, gmat, split=False):
    hi = v.astype(BF16)
    out = jnp.dot(hi, gmat, preferred_element_type=F32)
    if split:
        out = out + jnp.dot((v - hi.astype(F32)).astype(BF16), gmat, preferred_element_type=F32)
    return out


def _dot_nt(a, b):
    return lax.dot_general(a, b, (((1,), (1,)), ((), ())), preferred_element_type=F32)


def _silu_parts(a):
    sg = jax.nn.sigmoid(a)
    return sg, a * sg


def _rope(xr, cs, sn, lane):
    rh = jnp.where(lane < 32, -pltpu.roll(xr, 96, 1), pltpu.roll(xr, 32, 1))
    return xr * cs + rh * sn


def _rope_t(g, cs, sn, lane):
    y = g * sn
    rt = jnp.where(lane < 32, pltpu.roll(y, 96, 1), jnp.where(lane < 64, -pltpu.roll(y, 32, 1), 0.0))
    return g * cs + rt


def _row_tile(rows, pref, mult=8):
    t = min(rows, pref) // mult * mult
    while rows % t:
        t -= mult
    return t


def _place():
    return lax.axis_index("x"), lax.axis_index("y"), lax.axis_index("c")


ANY = pl.BlockSpec(memory_space=pl.ANY)


def _hosted_call(body, *, name, grid, in_specs, out_specs, out_shape, scratch_shapes, semantics, args, comm=None):
    n_in, n_out, n_scr = len(in_specs), len(out_specs), len(scratch_shapes)
    if comm is None:
        res = pl.pallas_call(body, name=name, grid=grid, in_specs=in_specs, out_specs=out_specs, out_shape=out_shape,
                             scratch_shapes=scratch_shapes, compiler_params=_params(semantics))(*args)
        return list(res), []
    n_ci, n_co = len(comm.inputs), len(comm.out_shapes)
    total = int(np.prod(grid))

    def hosted(*refs):
        ins, refs = refs[:n_in], refs[n_in:]
        cins, refs = refs[:n_ci], refs[n_ci:]
        outs, refs = refs[:n_out], refs[n_out:]
        couts, refs = refs[:n_co], refs[n_co:]
        scratch, sems = refs[:n_scr], refs[n_scr]
        step = pl.program_id(0)
        for ax in range(1, len(grid)):
            step = step * grid[ax] + pl.program_id(ax)

        @pl.when(step == 0)
        def _():
            comm.start(cins, couts, sems)

        body(*ins, *outs, *scratch)

        @pl.when(step == total - 1)
        def _():
            comm.finish(cins, couts, sems)

    res = pl.pallas_call(
        hosted, name=name, grid=grid, in_specs=list(in_specs) + [ANY] * n_ci,
        out_specs=list(out_specs) + [ANY] * n_co, out_shape=list(out_shape) + list(comm.out_shapes),
        scratch_shapes=list(scratch_shapes) + [pltpu.SemaphoreType.DMA((comm.n_sems,))],
        compiler_params=_params(("arbitrary",) * len(grid)))(*args, *comm.inputs)
    return list(res[:n_out]), list(res[n_out:])


def _run_comm(comm, *, name):
    n_ci = len(comm.inputs)

    def body(*refs):
        cins, couts, sems = refs[:n_ci], refs[n_ci:-1], refs[-1]
        comm.start(cins, couts, sems)
        comm.finish(cins, couts, sems)

    return list(pl.pallas_call(
        body, name=name, out_shape=list(comm.out_shapes), in_specs=[ANY] * n_ci,
        out_specs=[ANY] * len(comm.out_shapes), scratch_shapes=[pltpu.SemaphoreType.DMA((comm.n_sems,))],
    )(*comm.inputs))


class _Gather:
    def __init__(self, slabs):
        self.inputs = list(slabs)
        self.out_shapes = [jax.ShapeDtypeStruct((N_CHIPS,) + s.shape, s.dtype) for s in slabs]
        self.n_sems = 12 * len(slabs)

    @staticmethod
    def _copy(out, sems, base, k, chip, hc, to, src=None):
        H = out.shape[1] // 2
        half = out.at[2 * chip[0] + chip[1], pl.ds(hc * H, H), :]
        return pltpu.make_async_remote_copy(
            src_ref=half if src is None else src, dst_ref=half, send_sem=sems.at[base + k],
            recv_sem=sems.at[base + 6 + k], device_id=to, device_id_type=MESH)

    def _firsts(self, src, out, sems, base):
        x, y, c = _place()
        H = src.shape[0] // 2
        chips = [(1 - x, y), (x, 1 - y), (1 - x, 1 - y)]
        return [self._copy(out, sems, base, j, (x, y), c, (*chip, c), src=src.at[pl.ds(c * H, H), :])
                for j, chip in enumerate(chips)]

    def start(self, ins, outs, sems):
        for i, (src, out) in enumerate(zip(ins, outs)):
            for cp in self._firsts(src, out, sems, 12 * i):
                cp.start()

    def finish(self, ins, outs, sems):
        x, y, c = _place()
        chips = [(1 - x, y), (x, 1 - y), (1 - x, 1 - y)]
        passed = []
        for i, out in enumerate(outs):
            for j, chip in enumerate(chips):
                self._copy(out, sems, 12 * i, j, chip, c, (x, y, c)).wait_recv()
                cp = self._copy(out, sems, 12 * i, 3 + j, chip, c, (x, y, 1 - c))
                cp.start()
                passed.append(cp)
        for i, out in enumerate(outs):
            for j, chip in enumerate(chips):
                self._copy(out, sems, 12 * i, 3 + j, chip, 1 - c, (x, y, c)).wait_recv()
        for cp in passed:
            cp.wait_send()
        for i, (src, out) in enumerate(zip(ins, outs)):
            for cp in self._firsts(src, out, sems, 12 * i):
                cp.wait_send()


class _PairExchange:
    def __init__(self, arrays):
        self.inputs = list(arrays)
        self.out_shapes = [jax.ShapeDtypeStruct((N_CHIPS, a.shape[1] // 2, a.shape[2]), a.dtype) for a in arrays]
        self.n_sems = 2 * len(arrays)

    def _copies(self, ins, outs, sems):
        x, y, c = _place()
        return [pltpu.make_async_remote_copy(
            src_ref=g.at[:, pl.ds((1 - c) * t.shape[1], t.shape[1]), :], dst_ref=t, send_sem=sems.at[2 * i],
            recv_sem=sems.at[2 * i + 1], device_id=(x, y, 1 - c), device_id_type=MESH)
            for i, (g, t) in enumerate(zip(ins, outs))]

    def start(self, ins, outs, sems):
        for cp in self._copies(ins, outs, sems):
            cp.start()

    def finish(self, ins, outs, sems):
        for cp in self._copies(ins, outs, sems):
            cp.wait()


class _ChipExchange:
    def __init__(self, arrays):
        self.inputs = list(arrays)
        self.out_shapes = [jax.ShapeDtypeStruct(a.shape, a.dtype) for a in arrays]
        self.n_sems = 6 * len(arrays)

    def _copies(self, p, t, sems, base):
        x, y, c = _place()
        myq = 2 * x + y
        chips = [(1 - x, y), (x, 1 - y), (1 - x, 1 - y)]
        sends = [pltpu.make_async_remote_copy(
            src_ref=p.at[2 * chip[0] + chip[1]], dst_ref=t.at[myq], send_sem=sems.at[base + j],
            recv_sem=sems.at[base + 3 + j], device_id=(*chip, c), device_id_type=MESH) for j, chip in enumerate(chips)]
        lands = [pltpu.make_async_remote_copy(
            src_ref=t.at[2 * chip[0] + chip[1]], dst_ref=t.at[2 * chip[0] + chip[1]], send_sem=sems.at[base + j],
            recv_sem=sems.at[base + 3 + j], device_id=(*chip, c), device_id_type=MESH) for j, chip in enumerate(chips)]
        return sends, lands

    def start(self, ins, outs, sems):
        for i, (p, t) in enumerate(zip(ins, outs)):
            for cp in self._copies(p, t, sems, 6 * i)[0]:
                cp.start()

    def finish(self, ins, outs, sems):
        for i, (p, t) in enumerate(zip(ins, outs)):
            sends, lands = self._copies(p, t, sems, 6 * i)
            for cp in lands:
                cp.wait_recv()
            for cp in sends:
                cp.wait_send()


class _SemView:
    def __init__(self, sems, base):
        self._sems, self._base = sems, base

    @property
    def at(self):
        return self

    def __getitem__(self, k):
        return self._sems.at[self._base + k]


class _Multi:
    def __init__(self, progs):
        self.progs = list(progs)
        self.inputs = [a for p in self.progs for a in p.inputs]
        self.out_shapes = [s for p in self.progs for s in p.out_shapes]
        self.n_sems = sum(p.n_sems for p in self.progs)

    def _each(self, ins, outs, sems):
        i = o = s = 0
        for p in self.progs:
            ni, no = len(p.inputs), len(p.out_shapes)
            yield p, ins[i:i + ni], outs[o:o + no], _SemView(sems, s)
            i, o, s = i + ni, o + no, s + p.n_sems

    def start(self, ins, outs, sems):
        for p, a, b, c in self._each(ins, outs, sems):
            p.start(a, b, c)

    def finish(self, ins, outs, sems):
        for p, a, b, c in self._each(ins, outs, sems):
            p.finish(a, b, c)


class _PairShare:
    def __init__(self, arrays):
        self.inputs = list(arrays)
        self.out_shapes = [jax.ShapeDtypeStruct(a.shape, a.dtype) for a in arrays]
        self.n_sems = 2 * len(arrays)

    def _copies(self, ins, outs, sems):
        x, y, c = _place()
        return [pltpu.make_async_remote_copy(
            src_ref=r, dst_ref=o, send_sem=sems.at[2 * i], recv_sem=sems.at[2 * i + 1],
            device_id=(x, y, 1 - c), device_id_type=MESH) for i, (r, o) in enumerate(zip(ins, outs))]

    def start(self, ins, outs, sems):
        for cp in self._copies(ins, outs, sems):
            cp.start()

    def finish(self, ins, outs, sems):
        for cp in self._copies(ins, outs, sems):
            cp.wait()


def _ffn_up(x, ng, sh, sc, w1, w3, *, name, comm=None):
    S = x.shape[0]
    tm, tn = _row_tile(S, 512), FF // 2

    def body(x_ref, g_ref, sh_ref, sc_ref, w1_ref, w3_ref, h_ref, a_ref, b_ref, u_ref, hs):
        @pl.when(pl.program_id(1) == 0)
        def _():
            xv = x_ref[...]
            h = ((xv * _rms(xv)) * g_ref[...]) * (1.0 + sc_ref[...]) + sh_ref[...]
            hb = h.astype(BF16)
            hs[...] = hb
            h_ref[...] = hb

        h = hs[...]
        cols = pl.ds(pl.multiple_of(pl.program_id(1) * tn, tn), tn)
        a = _dot_nt(h, w1_ref[cols, :])
        b = _dot_nt(h, w3_ref[cols, :])
        _, sa = _silu_parts(a)
        a_ref[...] = a.astype(BF16)
        b_ref[...] = b.astype(BF16)
        u_ref[...] = (sa * b).astype(BF16)

    row = pl.BlockSpec((tm, D), lambda i, j: (i, 0))
    vec = pl.BlockSpec((1, D), lambda i, j: (0, 0))
    wsp = pl.BlockSpec((FF, D), lambda i, j: (0, 0))
    osp = pl.BlockSpec((tm, tn), lambda i, j: (i, j))
    return _hosted_call(
        body, name=name, grid=(S // tm, FF // tn),
        in_specs=[row, vec, vec, vec, wsp, wsp],
        out_specs=[row, osp, osp, osp],
        out_shape=[jax.ShapeDtypeStruct((S, D), BF16)] + [jax.ShapeDtypeStruct((S, FF), BF16)] * 3,
        scratch_shapes=[pltpu.VMEM((tm, D), BF16)],
        semantics=("parallel", "arbitrary"), args=(x, ng, sh, sc, w1, w3), comm=comm)


def _ffn_down(u, w2, x, gate, *, name, comm=None):
    S = x.shape[0]
    tm = _row_tile(S, 512)

    def body(u_ref, w2_ref, x_ref, g_ref, xo_ref, f_ref):
        f = jnp.dot(u_ref[...], w2_ref[...], preferred_element_type=F32)
        xo_ref[...] = x_ref[...] + (0.5 * g_ref[...]) * f
        f_ref[...] = f.astype(BF16)

    return _hosted_call(
        body, name=name, grid=(S // tm,),
        in_specs=[pl.BlockSpec((tm, FF), lambda i: (i, 0)), pl.BlockSpec((FF, D), lambda i: (0, 0)),
                  pl.BlockSpec((tm, D), lambda i: (i, 0)), pl.BlockSpec((1, D), lambda i: (0, 0))],
        out_specs=[pl.BlockSpec((tm, D), lambda i: (i, 0))] * 2,
        out_shape=[jax.ShapeDtypeStruct((S, D), F32), jax.ShapeDtypeStruct((S, D), BF16)],
        scratch_shapes=[], semantics=("parallel",), args=(u, w2, x, gate), comm=comm)


def _ffn_bwd_du(dx, gate, f, w2, a, b, *, name, comm=None):
    S = dx.shape[0]
    tm, tn = _row_tile(S, 512), FF // 2
    n_i = S // tm

    def body(dx_ref, g_ref, f_ref, w_ref, a_ref, b_ref, df_ref, da_ref, db_ref, dg_ref, dfs):
        i, j = pl.program_id(0), pl.program_id(1)

        @pl.when((i == 0) & (j == 0))
        def _():
            dg_ref[...] = jnp.zeros_like(dg_ref)

        @pl.when(j == 0)
        def _():
            dxv = dx_ref[...]
            dfb = (dxv * (0.5 * g_ref[...])).astype(BF16)
            dfs[...] = dfb
            df_ref[...] = dfb
            dg_ref[...] += _rsum8(dxv * (0.5 * f_ref[...].astype(F32)))

        du = _dot_nt(dfs[...], w_ref[pl.ds(pl.multiple_of(j * tn, tn), tn), :])
        av = a_ref[...].astype(F32)
        sg, sa = _silu_parts(av)
        da_ref[...] = (du * b_ref[...].astype(F32) * (sg * (1.0 + av * (1.0 - sg)))).astype(BF16)
        db_ref[...] = (du * sa).astype(BF16)

        @pl.when((i == n_i - 1) & (j == FF // tn - 1))
        def _():
            _all_rows(dg_ref)

    row = pl.BlockSpec((tm, D), lambda i, j: (i, 0))
    blk = pl.BlockSpec((tm, tn), lambda i, j: (i, j))
    return _hosted_call(
        body, name=name, grid=(n_i, FF // tn),
        in_specs=[row, pl.BlockSpec((1, D), lambda i, j: (0, 0)), row,
                  pl.BlockSpec((FF, D), lambda i, j: (0, 0)), blk, blk],
        out_specs=[row, blk, blk, pl.BlockSpec((8, D), lambda i, j: (0, 0))],
        out_shape=[jax.ShapeDtypeStruct((S, D), BF16), jax.ShapeDtypeStruct((S, FF), BF16),
                   jax.ShapeDtypeStruct((S, FF), BF16), jax.ShapeDtypeStruct((8, D), F32)],
        scratch_shapes=[pltpu.VMEM((tm, D), BF16)],
        semantics=("arbitrary", "arbitrary"), args=(dx, gate, f, w2, a, b), comm=comm)


def _tn_matmul(a, b, *, tm, tn, name, comm=None):
    S, M = a.shape
    N = b.shape[1]
    ts = _row_tile(S, 2048)
    ns = S // ts

    def body(a_ref, b_ref, o_ref):
        s = pl.program_id(2)
        p = lax.dot_general(a_ref[...], b_ref[...], (((0,), (0,)), ((), ())), preferred_element_type=F32)

        @pl.when(s == 0)
        def _():
            o_ref[...] = p

        @pl.when(s > 0)
        def _():
            o_ref[...] += p

    (out,), couts = _hosted_call(
        body, name=name, grid=(M // tm, N // tn, ns),
        in_specs=[pl.BlockSpec((ts, tm), lambda i, j, s: (s, i)), pl.BlockSpec((ts, tn), lambda i, j, s: (s, j))],
        out_specs=[pl.BlockSpec((tm, tn), lambda i, j, s: (i, j))],
        out_shape=[jax.ShapeDtypeStruct((M, N), F32)],
        scratch_shapes=[], semantics=("parallel", "parallel", "arbitrary"), args=(a, b), comm=comm)
    return out if comm is None else (out, couts)


def _dh_normbwd(pairs, x, ng, sc, dx_next, *, name, comm=None):
    S = x.shape[0]
    n_p = len(pairs)
    tm = _row_tile(S, 512 if n_p == 1 else 256)
    n_i = S // tm

    def body(*refs):
        a_refs, w_refs = refs[:n_p], refs[n_p:2 * n_p]
        x_ref, g_ref, sc_ref, dxn_ref, dx_ref, p_ref = refs[2 * n_p:]
        i = pl.program_id(0)
        dh = jnp.dot(a_refs[0][...], w_refs[0][...], preferred_element_type=F32)
        for k in range(1, n_p):
            dh = dh + jnp.dot(a_refs[k][...], w_refs[k][...], preferred_element_type=F32)
        xv = x_ref[...]
        r = _rms(xv)
        xh = xv * r
        g = g_ref[...]
        dn = dh * (1.0 + sc_ref[...])
        dy = dn * g
        dx_ref[...] = dxn_ref[...] + r * (dy - xh * jnp.mean(dy * xh, axis=-1, keepdims=True))

        @pl.when(i == 0)
        def _():
            p_ref[...] = jnp.zeros_like(p_ref)

        p_ref[:, 0:D] += _rsum8(dh * (xh * g))
        p_ref[:, D:2 * D] += _rsum8(dh)
        p_ref[:, 2 * D:3 * D] += _rsum8(dn * xh)

        @pl.when(i == n_i - 1)
        def _():
            _all_rows(p_ref)

    row = pl.BlockSpec((tm, D), lambda i: (i, 0))
    vec = pl.BlockSpec((1, D), lambda i: (0, 0))
    in_specs = ([pl.BlockSpec((tm, a.shape[1]), lambda i: (i, 0)) for a, _ in pairs]
                + [pl.BlockSpec(w.shape, lambda i: (0, 0)) for _, w in pairs] + [row, vec, vec, row])
    return _hosted_call(
        body, name=name, grid=(n_i,), in_specs=in_specs,
        out_specs=[row, pl.BlockSpec((8, 3 * D), lambda i: (0, 0))],
        out_shape=[jax.ShapeDtypeStruct((S, D), F32), jax.ShapeDtypeStruct((8, 3 * D), F32)],
        scratch_shapes=[], semantics=("arbitrary",),
        args=(*[a for a, _ in pairs], *[w for _, w in pairs], x, ng, sc, dx_next), comm=comm)


def _ffn_down_loss(u, w2, x, gate, gfin, tgt):
    S = x.shape[0]
    tm = _row_tile(S, 512)
    n_i = S // tm

    def body(u_ref, w2_ref, x_ref, gt_ref, g_ref, t_ref, dx_ref, f_ref, dg_ref, loss_ref, lacc):
        i = pl.program_id(0)
        f = jnp.dot(u_ref[...], w2_ref[...], preferred_element_type=F32)
        f_ref[...] = f.astype(BF16)
        xv = x_ref[...] + (0.5 * gt_ref[...]) * f
        r = _rms(xv)
        xh = xv * r
        g = g_ref[...]
        e = xh * g - t_ref[...]
        dout = e * (1.0 / D)
        dy = dout * g
        dx_ref[...] = r * (dy - xh * jnp.mean(dy * xh, axis=-1, keepdims=True))

        @pl.when(i == 0)
        def _():
            dg_ref[...] = jnp.zeros_like(dg_ref)
            lacc[...] = jnp.zeros_like(lacc)

        dg_ref[...] += _rsum8(dout * xh)
        lacc[...] += _rsum8(e * e)

        @pl.when(i == n_i - 1)
        def _():
            _all_rows(dg_ref)
            tot = jnp.sum(jnp.sum(lacc[...], axis=0, keepdims=True), axis=1, keepdims=True)
            loss_ref[...] = jnp.broadcast_to(tot * (0.5 / D), loss_ref.shape)

    row = pl.BlockSpec((tm, D), lambda i: (i, 0))
    vec = pl.BlockSpec((1, D), lambda i: (0, 0))
    return pl.pallas_call(
        body, name="ffn2_down_loss", grid=(n_i,),
        in_specs=[pl.BlockSpec((tm, FF), lambda i: (i, 0)), pl.BlockSpec((FF, D), lambda i: (0, 0)), row, vec, vec, row],
        out_specs=[row, row, pl.BlockSpec((8, D), lambda i: (0, 0)), pl.BlockSpec((8, 128), lambda i: (0, 0))],
        out_shape=[jax.ShapeDtypeStruct((S, D), F32), jax.ShapeDtypeStruct((S, D), BF16),
                   jax.ShapeDtypeStruct((8, D), F32), jax.ShapeDtypeStruct((8, 128), F32)],
        scratch_shapes=[pltpu.VMEM((8, D), F32)],
        compiler_params=_params(("arbitrary",)),
    )(u, w2, x, gate, gfin, tgt)


def _mix_in(x, ng, sh, sc, w_in, comm=None):
    S = x.shape[0]
    tm = _row_tile(S, 512)

    def body(x_ref, g_ref, sh_ref, sc_ref, w_ref, h_ref, z_ref):
        xv = x_ref[...]
        hb = (((xv * _rms(xv)) * g_ref[...]) * (1.0 + sc_ref[...]) + sh_ref[...]).astype(BF16)
        h_ref[...] = hb
        z_ref[...] = _dot_nt(hb, w_ref[...])

    row = pl.BlockSpec((tm, D), lambda i: (i, 0))
    vec = pl.BlockSpec((1, D), lambda i: (0, 0))
    return _hosted_call(
        body, name="mix_in", grid=(S // tm,),
        in_specs=[row, vec, vec, vec, pl.BlockSpec((Z_COLS, D), lambda i: (0, 0))],
        out_specs=[row, pl.BlockSpec((tm, Z_COLS), lambda i: (i, 0))],
        out_shape=[jax.ShapeDtypeStruct((S, D), BF16), jax.ShapeDtypeStruct((S, Z_COLS), F32)],
        scratch_shapes=[], semantics=("parallel",), args=(x, ng, sh, sc, w_in), comm=comm)


def _conv_taps(u, halo, rows):
    u1 = jnp.where(rows == 0, halo[7:8, :], pltpu.roll(u, 1, 0))
    u2 = jnp.where(rows == 0, halo[6:7, :], jnp.where(rows == 1, halo[7:8, :], pltpu.roll(u, 2, 0)))
    return u1, u2


def _mix_mid(z, conv_w, gq, gkv, wuq, wukv, cs, sn, comm=None):
    S = z.shape[0]
    tm = _row_tile(S, 512)
    hb = tm // 8

    def body(z_ref, zh_ref, cw_ref, gq_ref, gkv_ref, wuq_ref, wukv_ref, cs_ref, sn_ref,
             ya_ref, q_ref, k_ref, v_ref, cqn_ref, ckvn_ref):
        i = pl.program_id(0)
        xb = z_ref[:, 0:CONV_W]
        u = z_ref[:, CONV_W:2 * CONV_W] * z_ref[:, 2 * CONV_W:3 * CONV_W]
        halo = zh_ref[:, CONV_W:2 * CONV_W] * zh_ref[:, 2 * CONV_W:3 * CONV_W]
        halo = jnp.where(i > 0, halo, 0.0)
        rows = lax.broadcasted_iota(jnp.int32, (tm, CONV_W), 0)
        u1, u2 = _conv_taps(u, halo, rows)
        y = cw_ref[0:1, :] * u2 + cw_ref[1:2, :] * u1 + cw_ref[2:3, :] * u
        ya_ref[...] = xb * y

        lane = lax.broadcasted_iota(jnp.int32, (tm, 128), 1)
        cs_v, sn_v = cs_ref[...], sn_ref[...]
        cq = z_ref[:, 3 * CONV_W:3 * CONV_W + Q_LORA]
        cqn = ((cq * _rms(cq)) * gq_ref[...]).astype(BF16)
        cqn_ref[...] = cqn
        q = _dot_nt(cqn, wuq_ref[...])
        for h in range(HEADS):
            o = h * HEAD_SLOT
            q_ref[:, o:o + 128] = q[:, o:o + 128].astype(BF16)
            q_ref[:, o + 128:o + 256] = _rope(q[:, o + 128:o + 256], cs_v, sn_v, lane).astype(BF16)

        c0 = 3 * CONV_W + Q_LORA
        ckv = z_ref[:, c0:c0 + KV_LORA]
        ckvn = ((ckv * _rms(ckv)) * gkv_ref[...]).astype(BF16)
        ckvn_ref[...] = ckvn
        kv = jnp.dot(ckvn, wukv_ref[...], preferred_element_type=F32)
        krot = _rope(z_ref[:, c0 + KV_LORA:Z_COLS], cs_v, sn_v, lane).astype(BF16)
        for h in range(HEADS):
            o = h * HEAD_SLOT
            k_ref[:, o:o + 128] = kv[:, h * 128:(h + 1) * 128].astype(BF16)
            k_ref[:, o + 128:o + 256] = krot
        v_ref[...] = kv[:, HEADS * 128:].astype(BF16)

    def rows_of(n):
        return pl.BlockSpec((tm, n), lambda i: (i, 0))

    def whole(shape):
        return pl.BlockSpec(shape, lambda i: (0, 0))

    return _hosted_call(
        body, name="mix_mid", grid=(S // tm,),
        in_specs=[rows_of(Z_COLS), pl.BlockSpec((8, Z_COLS), lambda i: (jnp.maximum(i * hb - 1, 0), 0)),
                  whole((8, CONV_W)), whole((1, Q_LORA)), whole((1, KV_LORA)),
                  whole((HEADS * HEAD_SLOT, Q_LORA)), whole((KV_LORA, 2 * HEADS * 128)),
                  rows_of(128), rows_of(128)],
        out_specs=[rows_of(CONV_W), rows_of(HEADS * HEAD_SLOT), rows_of(HEADS * HEAD_SLOT), rows_of(HEADS * V_HEAD),
                   rows_of(Q_LORA), rows_of(KV_LORA)],
        out_shape=[jax.ShapeDtypeStruct((S, CONV_W), F32), jax.ShapeDtypeStruct((S, HEADS * HEAD_SLOT), BF16),
                   jax.ShapeDtypeStruct((S, HEADS * HEAD_SLOT), BF16), jax.ShapeDtypeStruct((S, HEADS * V_HEAD), BF16),
                   jax.ShapeDtypeStruct((S, Q_LORA), BF16), jax.ShapeDtypeStruct((S, KV_LORA), BF16)],
        scratch_shapes=[], semantics=("parallel",), args=(z, z, conv_w, gq, gkv, wuq, wukv, cs, sn), comm=comm)


def _att_blocks(S):
    bk = min(1024, max(S // 4, 128))
    return bk, bk


def _pair_tables(S, k_major):
    bq, bk = _att_blocks(S)
    nq, nk = S // bq, S // bk
    vis = lambda qi, ki: ki * bk < (qi + 1) * bq
    if k_major:
        pairs = [(qi, ki) for ki in range(nk) for qi in range(nq) if vis(qi, ki)]
    else:
        pairs = [(qi, ki) for qi in range(nq) for ki in range(nk) if vis(qi, ki)]
    cols = [[p[0] for p in pairs], [p[1] for p in pairs], [int((p[1] + 1) * bk > p[0] * bq) for p in pairs]]
    return [jnp.asarray(np.array(c, np.int32)) for c in cols], len(pairs)


def _chunk_mask(r0, nr, nc):
    r = (r0 + lax.broadcasted_iota(jnp.int32, (nr, nc), 0)) // CHUNK
    c = lax.broadcasted_iota(jnp.int32, (nr, nc), 1) // CHUNK
    return c <= r


def _diag_parts(bq, bk):
    return [(0, bq // 2, bk // 2), (bq // 2, bq // 2, bk)]


def _attention(q, k, v):
    S = q.shape[0]
    bq, bk = _att_blocks(S)
    last_k = bq // bk - 1
    tables, n_pairs = _pair_tables(S, k_major=False)

    def body(qi_ref, ki_ref, mk_ref, q_ref, k_ref, v_ref, o_ref, lse_ref, m_s, l_s, acc_s):
        p_id = pl.program_id(1)
        qi, ki = qi_ref[p_id], ki_ref[p_id]

        @pl.when(ki == 0)
        def _():
            m_s[...] = jnp.full_like(m_s, NEG)
            l_s[...] = jnp.zeros_like(l_s)
            acc_s[...] = jnp.zeros_like(acc_s)

        def update(r0, nr, nc, masked):
            rows = slice(r0, r0 + nr)
            s = _dot_nt(q_ref[rows, :], k_ref[0:nc, :])
            if masked:
                s = jnp.where(_chunk_mask(r0, nr, nc), s, NEG)
            m_prev = m_s[rows, :]
            m_new = jnp.maximum(m_prev, jnp.max(s, axis=1, keepdims=True))
            alpha = jnp.exp2((m_prev - m_new) * EXP2_SCALE)
            p = jnp.exp2((s - jnp.tile(m_new, (1, nc // 128))) * EXP2_SCALE)
            l_s[rows, :] = alpha * l_s[rows, :] + jnp.sum(p, axis=1, keepdims=True)
            acc_s[rows, :] = alpha * acc_s[rows, :] + jnp.dot(p.astype(BF16), v_ref[0:nc, :],
                                                              preferred_element_type=F32)
            m_s[rows, :] = m_new

        @pl.when(mk_ref[p_id] == 0)
        def _():
            update(0, bq, bk, False)

        @pl.when(mk_ref[p_id] == 1)
        def _():
            for part in _diag_parts(bq, bk):
                update(*part, True)

        @pl.when(ki == qi * (last_k + 1) + last_k)
        def _():
            l = l_s[...]
            o_ref[...] = acc_s[...] / l
            lse_ref[...] = m_s[...] * EXP2_SCALE + jnp.log2(l)

    grid_spec = pltpu.PrefetchScalarGridSpec(
        num_scalar_prefetch=3, grid=(HEADS, n_pairs),
        in_specs=[pl.BlockSpec((bq, HEAD_SLOT), lambda h, p, qt, kt, mt: (qt[p], h)),
                  pl.BlockSpec((bk, HEAD_SLOT), lambda h, p, qt, kt, mt: (kt[p], h)),
                  pl.BlockSpec((bk, V_HEAD), lambda h, p, qt, kt, mt: (kt[p], h))],
        out_specs=[pl.BlockSpec((bq, V_HEAD), lambda h, p, qt, kt, mt: (qt[p], h))] * 2,
        scratch_shapes=[pltpu.VMEM((bq, V_HEAD), F32)] * 3)
    return pl.pallas_call(
        body, name="attention", grid_spec=grid_spec,
        out_shape=[jax.ShapeDtypeStruct((S, HEADS * V_HEAD), F32)] * 2,
        compiler_params=_params(("arbitrary", "arbitrary")),
    )(*tables, q, k, v)


def _attention_bwd(q, k, v, do, lse2, delta):
    S = q.shape[0]
    bq, bk = _att_blocks(S)
    nq = S // bq
    tables, n_pairs = _pair_tables(S, k_major=True)

    def body(qi_ref, ki_ref, mk_ref, q_ref, k_ref, v_ref, do_ref, lse_ref, dl_ref, dq_hbm, dk_ref, dv_ref,
             dq_s, dk_s, dv_s, sem):
        head, p_id = pl.program_id(0), pl.program_id(1)
        qi, ki = qi_ref[p_id], ki_ref[p_id]

        @pl.when(qi * bq <= ki * bk)
        def _():
            dk_s[...] = jnp.zeros_like(dk_s)
            dv_s[...] = jnp.zeros_like(dv_s)

        def update(r0, nr, nc, masked):
            rows, cols = slice(r0, r0 + nr), slice(0, nc)
            qv, kv, dov = q_ref[rows, :], k_ref[cols, :], do_ref[rows, :]
            s = _dot_nt(qv, kv)
            dp = _dot_nt(dov, v_ref[cols, :])
            if masked:
                s = jnp.where(_chunk_mask(r0, nr, nc), s, NEG)
            p = jnp.exp2(s * EXP2_SCALE - jnp.tile(lse_ref[rows, :], (1, nc // 128)))
            dv_s[cols, :] += lax.dot_general(p.astype(BF16), dov, (((0,), (0,)), ((), ())),
                                             preferred_element_type=F32)
            ds = (p * (dp - jnp.tile(dl_ref[rows, :], (1, nc // 128)))).astype(BF16)
            dk_s[cols, :] += lax.dot_general(ds, qv, (((0,), (0,)), ((), ())), preferred_element_type=F32)
            dq = jnp.dot(ds, kv, preferred_element_type=F32)
            out_rows = pl.ds(pl.multiple_of(qi * bq + r0, nr), nr)

            @pl.when(ki == 0)
            def _():
                dq_s[out_rows, :] = dq

            @pl.when(ki > 0)
            def _():
                dq_s[out_rows, :] += dq

        @pl.when(mk_ref[p_id] == 0)
        def _():
            update(0, bq, bk, False)

        @pl.when(mk_ref[p_id] == 1)
        def _():
            for part in _diag_parts(bq, bk):
                update(*part, True)

        @pl.when(qi == nq - 1)
        def _():
            dk_ref[...] = dk_s[...] * ATT_SCALE
            dv_ref[...] = dv_s[...]

        @pl.when(p_id == n_pairs - 1)
        def _():
            dq_s[...] = dq_s[...] * ATT_SCALE
            out = pltpu.make_async_copy(
                dq_s, dq_hbm.at[:, pl.ds(pl.multiple_of(head * HEAD_SLOT, HEAD_SLOT), HEAD_SLOT)], sem)
            out.start()
            out.wait()

    grid_spec = pltpu.PrefetchScalarGridSpec(
        num_scalar_prefetch=3, grid=(HEADS, n_pairs),
        in_specs=[pl.BlockSpec((bq, HEAD_SLOT), lambda h, p, qt, kt, mt: (qt[p], h)),
                  pl.BlockSpec((bk, HEAD_SLOT), lambda h, p, qt, kt, mt: (kt[p], h)),
                  pl.BlockSpec((bk, V_HEAD), lambda h, p, qt, kt, mt: (kt[p], h)),
                  pl.BlockSpec((bq, V_HEAD), lambda h, p, qt, kt, mt: (qt[p], h)),
                  pl.BlockSpec((bq, V_HEAD), lambda h, p, qt, kt, mt: (qt[p], h)),
                  pl.BlockSpec((bq, V_HEAD), lambda h, p, qt, kt, mt: (qt[p], h))],
        out_specs=[pl.BlockSpec(memory_space=pl.ANY),
                   pl.BlockSpec((bk, HEAD_SLOT), lambda h, p, qt, kt, mt: (kt[p], h)),
                   pl.BlockSpec((bk, V_HEAD), lambda h, p, qt, kt, mt: (kt[p], h))],
        scratch_shapes=[pltpu.VMEM((S, HEAD_SLOT), F32), pltpu.VMEM((bk, HEAD_SLOT), F32),
                        pltpu.VMEM((bk, V_HEAD), F32), pltpu.SemaphoreType.DMA])
    return pl.pallas_call(
        body, name="attention_bwd", grid_spec=grid_spec,
        out_shape=[jax.ShapeDtypeStruct((S, HEADS * HEAD_SLOT), F32), jax.ShapeDtypeStruct((S, HEADS * HEAD_SLOT), F32),
                   jax.ShapeDtypeStruct((S, HEADS * V_HEAD), F32)],
        compiler_params=_params(("arbitrary", "arbitrary")),
    )(*tables, q, k, v, do, lse2, delta)


def _group_mats():
    def blockdiag(n, g):
        idx = np.arange(n) // g
        return jnp.asarray((idx[:, None] == idx[None, :]).astype(np.float32), dtype=BF16)
    return blockdiag(CONV_W, CONV_GROUP), blockdiag(HEADS * V_HEAD, V_HEAD)


def _mix_out(ya, o, gout, w_out, x, gate, ga, gb):
    S = x.shape[0]
    tm = _row_tile(S, 512)

    def body(ya_ref, o_ref, go_ref, w_ref, x_ref, g_ref, ga_ref, gb_ref, xo_ref, yn_ref, yo_ref):
        yav, ov = ya_ref[...], o_ref[...]
        ra = lax.rsqrt(_gsum(yav * yav, ga_ref[...]) * (1.0 / CONV_GROUP) + EPS)
        rb = lax.rsqrt(_gsum(ov * ov, gb_ref[...]) * (1.0 / V_HEAD) + EPS)
        na = ((yav * ra) * go_ref[:, 0:CONV_W]).astype(BF16)
        nb = ((ov * rb) * go_ref[:, CONV_W:]).astype(BF16)
        yn_ref[:, 0:CONV_W] = na
        yn_ref[:, CONV_W:] = nb
        yo = (jnp.dot(na, w_ref[0:CONV_W, :], preferred_element_type=F32)
              + jnp.dot(nb, w_ref[CONV_W:, :], preferred_element_type=F32))
        xo_ref[...] = x_ref[...] + g_ref[...] * yo
        yo_ref[...] = yo.astype(BF16)

    row = pl.BlockSpec((tm, D), lambda i: (i, 0))
    half = pl.BlockSpec((tm, CONV_W), lambda i: (i, 0))
    vec = pl.BlockSpec((1, D), lambda i: (0, 0))
    sq = pl.BlockSpec((CONV_W, CONV_W), lambda i: (0, 0))
    return pl.pallas_call(
        body, name="mix_out", grid=(S // tm,),
        in_specs=[half, half, vec, pl.BlockSpec((D, D), lambda i: (0, 0)), row, vec, sq, sq],
        out_specs=[row, row, row],
        out_shape=[jax.ShapeDtypeStruct((S, D), F32), jax.ShapeDtypeStruct((S, D), BF16),
                   jax.ShapeDtypeStruct((S, D), BF16)],
        compiler_params=_params(("parallel",)),
    )(ya, o, gout, w_out, x, gate, ga, gb)


def _mix_out_bwd(dx, gate, yo, w_out, ya, o, gout, ga, gb, comm=None):
    S = dx.shape[0]
    tm = _row_tile(S, 512)
    n_i = S // tm

    def norm_bwd(v, dn, gain, gmat, inv_n):
        r = lax.rsqrt(_gsum(v * v, gmat) * inv_n + EPS)
        vh = v * r
        dy = dn * gain
        return r * (dy - vh * (_gsum(dy * vh, gmat) * inv_n)), dn * vh

    def body(dx_ref, g_ref, yo_ref, w_ref, ya_ref, o_ref, go_ref, ga_ref, gb_ref,
             dyo_ref, dya_ref, do_ref, dl_ref, p_ref):
        i = pl.program_id(0)
        dxv = dx_ref[...]
        dyo = (dxv * g_ref[...]).astype(BF16)
        dyo_ref[...] = dyo
        dyn = _dot_nt(dyo, w_ref[...])
        dya, dga = norm_bwd(ya_ref[...], dyn[:, 0:CONV_W], go_ref[:, 0:CONV_W], ga_ref[...], 1.0 / CONV_GROUP)
        ov = o_ref[...]
        do, dgb = norm_bwd(ov, dyn[:, CONV_W:], go_ref[:, CONV_W:], gb_ref[...], 1.0 / V_HEAD)
        dya_ref[...] = dya
        do_ref[...] = do.astype(BF16)
        dl_ref[...] = _gsum(do * ov, gb_ref[...], split=True)

        @pl.when(i == 0)
        def _():
            p_ref[...] = jnp.zeros_like(p_ref)

        p_ref[:, 0:D] += _rsum8(dxv * yo_ref[...].astype(F32))
        p_ref[:, D:D + CONV_W] += _rsum8(dga)
        p_ref[:, D + CONV_W:2 * D] += _rsum8(dgb)

        @pl.when(i == n_i - 1)
        def _():
            _all_rows(p_ref)

    row = pl.BlockSpec((tm, D), lambda i: (i, 0))
    half = pl.BlockSpec((tm, CONV_W), lambda i: (i, 0))
    vec = pl.BlockSpec((1, D), lambda i: (0, 0))
    sq = pl.BlockSpec((CONV_W, CONV_W), lambda i: (0, 0))
    return _hosted_call(
        body, name="mix_out_bwd", grid=(n_i,),
        in_specs=[row, vec, row, pl.BlockSpec((D, D), lambda i: (0, 0)), half, half, vec, sq, sq],
        out_specs=[row, half, half, half, pl.BlockSpec((8, 2 * D), lambda i: (0, 0))],
        out_shape=[jax.ShapeDtypeStruct((S, D), BF16), jax.ShapeDtypeStruct((S, CONV_W), F32),
                   jax.ShapeDtypeStruct((S, CONV_W), BF16), jax.ShapeDtypeStruct((S, CONV_W), F32),
                   jax.ShapeDtypeStruct((8, 2 * D), F32)],
        scratch_shapes=[], semantics=("arbitrary",), args=(dx, gate, yo, w_out, ya, o, gout, ga, gb), comm=comm)


MID_SUMS = 3 * CONV_W + Q_LORA + KV_LORA


def _mix_mid_bwd(z, dya, conv_w, gq, gkv, wuq, wukv, cs, sn, dq, dk, dv, comm=None):
    S = z.shape[0]
    tm = _row_tile(S, 256)
    n_i = S // tm
    hb = tm // 8
    last_blk = S // 8 - 1

    def latent_bwd(cv, dcn, gain):
        r = _rms(cv)
        ch = cv * r
        dy = dcn * gain
        return r * (dy - ch * jnp.mean(dy * ch, axis=-1, keepdims=True)), dcn * ch

    def body(z_ref, zp_ref, zn_ref, dya_ref, dyan_ref, cw_ref, gq_ref, gkv_ref, wuq_ref, wukv_ref, cs_ref, sn_ref,
             dq_ref, dk_ref, dv_ref, dz_ref, dqf_ref, dkvf_ref, p_ref):
        i = pl.program_id(0)
        xb, xc, xu = z_ref[:, 0:CONV_W], z_ref[:, CONV_W:2 * CONV_W], z_ref[:, 2 * CONV_W:3 * CONV_W]
        u = xc * xu
        halo = jnp.where(i > 0, zp_ref[:, CONV_W:2 * CONV_W] * zp_ref[:, 2 * CONV_W:3 * CONV_W], 0.0)
        rows = lax.broadcasted_iota(jnp.int32, (tm, CONV_W), 0)
        u1, u2 = _conv_taps(u, halo, rows)
        w0, w1, w2 = cw_ref[0:1, :], cw_ref[1:2, :], cw_ref[2:3, :]
        y = w0 * u2 + w1 * u1 + w2 * u
        dyav = dya_ref[...]
        dy = dyav * xb
        nxt = jnp.where(i < n_i - 1, dyan_ref[...] * zn_ref[:, 0:CONV_W], 0.0)
        dy1 = jnp.where(rows == tm - 1, nxt[0:1, :], pltpu.roll(dy, tm - 1, 0))
        dy2 = jnp.where(rows == tm - 1, nxt[1:2, :], jnp.where(rows == tm - 2, nxt[0:1, :], pltpu.roll(dy, tm - 2, 0)))
        du = w2 * dy + w1 * dy1 + w0 * dy2
        dz_ref[:, 0:CONV_W] = (dyav * y).astype(BF16)
        dz_ref[:, CONV_W:2 * CONV_W] = (du * xu).astype(BF16)
        dz_ref[:, 2 * CONV_W:3 * CONV_W] = (du * xc).astype(BF16)

        lane = lax.broadcasted_iota(jnp.int32, (tm, 128), 1)
        cs_v, sn_v = cs_ref[...], sn_ref[...]
        dkr = jnp.zeros((tm, 128), F32)
        for h in range(HEADS):
            o = h * HEAD_SLOT
            dqf_ref[:, o:o + 128] = dq_ref[:, o:o + 128].astype(BF16)
            dqf_ref[:, o + 128:o + 256] = _rope_t(dq_ref[:, o + 128:o + 256], cs_v, sn_v, lane).astype(BF16)
            dkvf_ref[:, h * 128:(h + 1) * 128] = dk_ref[:, o:o + 128].astype(BF16)
            dkr = dkr + dk_ref[:, o + 128:o + 256]
        dkvf_ref[:, HEADS * 128:] = dv_ref[...].astype(BF16)

        c0 = 3 * CONV_W
        dcqn = jnp.dot(dqf_ref[...], wuq_ref[...], preferred_element_type=F32)
        dcq, dgq = latent_bwd(z_ref[:, c0:c0 + Q_LORA], dcqn, gq_ref[...])
        dz_ref[:, c0:c0 + Q_LORA] = dcq.astype(BF16)
        c1 = c0 + Q_LORA
        dckvn = _dot_nt(dkvf_ref[...], wukv_ref[...])
        dckv, dgkv = latent_bwd(z_ref[:, c1:c1 + KV_LORA], dckvn, gkv_ref[...])
        dz_ref[:, c1:c1 + KV_LORA] = dckv.astype(BF16)
        dz_ref[:, c1 + KV_LORA:Z_COLS] = _rope_t(dkr, cs_v, sn_v, lane).astype(BF16)

        @pl.when(i == 0)
        def _():
            p_ref[...] = jnp.zeros_like(p_ref)

        p_ref[:, 0:CONV_W] += _rsum8(dy * u2)
        p_ref[:, CONV_W:2 * CONV_W] += _rsum8(dy * u1)
        p_ref[:, 2 * CONV_W:3 * CONV_W] += _rsum8(dy * u)
        p_ref[:, c0:c0 + Q_LORA] += _rsum8(dgq)
        p_ref[:, c1:c1 + KV_LORA] += _rsum8(dgkv)

        @pl.when(i == n_i - 1)
        def _():
            _all_rows(p_ref)

    def rows_of(n):
        return pl.BlockSpec((tm, n), lambda i: (i, 0))

    def whole(shape):
        return pl.BlockSpec(shape, lambda i: (0, 0))

    def prev8(n):
        return pl.BlockSpec((8, n), lambda i: (jnp.maximum(i * hb - 1, 0), 0))

    def next8(n):
        return pl.BlockSpec((8, n), lambda i: (jnp.minimum((i + 1) * hb, last_blk), 0))

    return _hosted_call(
        body, name="mix_mid_bwd", grid=(n_i,),
        in_specs=[rows_of(Z_COLS), prev8(Z_COLS), next8(Z_COLS), rows_of(CONV_W), next8(CONV_W),
                  whole((8, CONV_W)), whole((1, Q_LORA)), whole((1, KV_LORA)),
                  whole((HEADS * HEAD_SLOT, Q_LORA)), whole((KV_LORA, 2 * HEADS * 128)),
                  rows_of(128), rows_of(128),
                  rows_of(HEADS * HEAD_SLOT), rows_of(HEADS * HEAD_SLOT), rows_of(HEADS * V_HEAD)],
        out_specs=[rows_of(Z_COLS), rows_of(HEADS * HEAD_SLOT), rows_of(2 * HEADS * 128), whole((8, MID_SUMS))],
        out_shape=[jax.ShapeDtypeStruct((S, Z_COLS), BF16), jax.ShapeDtypeStruct((S, HEADS * HEAD_SLOT), BF16),
                   jax.ShapeDtypeStruct((S, 2 * HEADS * 128), BF16), jax.ShapeDtypeStruct((8, MID_SUMS), F32)],
        scratch_shapes=[], semantics=("arbitrary",),
        args=(z, z, z, dya, dya, conv_w, gq, gkv, wuq, wukv, cs, sn, dq, dk, dv), comm=comm)


ADA_Q = N_MOD * D // N_CHIPS
ADA_TN = 768


def _ada_forward(c_all, ada_w_q, ada_b_q):
    def body(c_ref, w_ref, b_ref, o_ref):
        cv = c_ref[...]
        sc = (cv * jax.nn.sigmoid(cv)).astype(BF16)
        o_ref[...] = jnp.dot(sc, w_ref[...].astype(BF16), preferred_element_type=F32) + b_ref[...]

    return pl.pallas_call(
        body, name="ada_forward", grid=(ADA_Q // ADA_TN,),
        in_specs=[pl.BlockSpec((16, D), lambda j: (0, 0)), pl.BlockSpec((D, ADA_TN), lambda j: (0, j)),
                  pl.BlockSpec((1, ADA_TN), lambda j: (0, j))],
        out_specs=pl.BlockSpec((16, ADA_TN), lambda j: (0, j)),
        out_shape=jax.ShapeDtypeStruct((16, ADA_Q), F32),
        compiler_params=_params(("parallel",)),
    )(c_all, ada_w_q, ada_b_q)


def _ada_wgrad(c_all, dmod_q):
    def body(c_ref, d_ref, o_ref):
        cv = c_ref[...]
        sc = (cv * jax.nn.sigmoid(cv)).astype(BF16)
        o_ref[...] = lax.dot_general(sc, d_ref[...].astype(BF16), (((0,), (0,)), ((), ())),
                                     preferred_element_type=F32)

    return pl.pallas_call(
        body, name="ada_wgrad", grid=(ADA_Q // ADA_TN,),
        in_specs=[pl.BlockSpec((16, D), lambda j: (0, 0)), pl.BlockSpec((16, ADA_TN), lambda j: (0, j))],
        out_specs=pl.BlockSpec((D, ADA_TN), lambda j: (0, j)),
        out_shape=jax.ShapeDtypeStruct((D, ADA_Q), F32),
        compiler_params=_params(("parallel",)),
    )(c_all, dmod_q)


def _sum_devices(parts):
    n = parts.shape[1]

    def body(p_ref, o_ref):
        o_ref[...] = jnp.broadcast_to(jnp.sum(p_ref[...], axis=0, keepdims=True), o_ref.shape)

    return pl.pallas_call(
        body, name="sum_devices",
        in_specs=[pl.BlockSpec((N_DEV, n), lambda: (0, 0))], out_specs=pl.BlockSpec((N_DEV, n), lambda: (0, 0)),
        out_shape=jax.ShapeDtypeStruct((N_DEV, n), F32),
    )(parts)


def _adamw(w, g, m, v, *, name):
    _, rows, cols = w.shape
    tr = _row_tile(rows, 256)

    def body(w_ref, g_ref, m_ref, v_ref, go_ref, d_ref, mo_ref, vo_ref):
        gv = g_ref[...]
        mn = B1 * m_ref[0] + (1.0 - B1) * gv
        vn = B2 * v_ref[0] + (1.0 - B2) * (gv * gv)
        m_hat = mn / (1.0 - B1 ** STEP)
        v_hat = vn / (1.0 - B2 ** STEP)
        go_ref[0] = gv
        d_ref[0] = -LR * (m_hat / (jnp.sqrt(v_hat) + AEPS) + WD * w_ref[0])
        mo_ref[0] = mn
        vo_ref[0] = vn

    blk = pl.BlockSpec((1, tr, cols), lambda i: (0, i, 0))
    return pl.pallas_call(
        body, name=name, grid=(rows // tr,),
        in_specs=[blk, pl.BlockSpec((tr, cols), lambda i: (i, 0)), blk, blk], out_specs=[blk] * 4,
        out_shape=[jax.ShapeDtypeStruct((1, rows, cols), F32)] * 4,
        compiler_params=_params(("parallel",)),
    )(w, g, m, v)


def _small_allgather(v, *, name):
    m, n = v.shape

    def body(x_ref, out_ref, send_sems, recv_sems, local_sem):
        x, y, c = _place()
        me, sibling = (x, y, c), (x, y, 1 - c)
        chips = [(1 - x, y), (x, 1 - y), (1 - x, 1 - y)]

        def rows(px, py, pc):
            return out_ref.at[pl.ds((4 * px + 2 * py + pc) * m, m), :]

        def copy(k, block, to, src=None):
            return pltpu.make_async_remote_copy(
                src_ref=rows(*block) if src is None else src, dst_ref=rows(*block),
                send_sem=send_sems.at[k], recv_sem=recv_sems.at[k], device_id=to, device_id_type=MESH)

        mine = pltpu.make_async_copy(x_ref, rows(*me), local_sem)
        mine.start()
        first = [copy(0, me, sibling, src=x_ref)]
        first += [copy(1 + j, me, (*chip, c), src=x_ref) for j, chip in enumerate(chips)]
        for cp in first:
            cp.start()
        passed = [copy(4 + j, (*chip, c), sibling) for j, chip in enumerate(chips)]
        for j, chip in enumerate(chips):
            copy(1 + j, (*chip, c), me).wait_recv()
            passed[j].start()
        copy(0, sibling, me).wait_recv()
        for j, chip in enumerate(chips):
            copy(4 + j, (*chip, 1 - c), me).wait_recv()
        for cp in first + passed:
            cp.wait_send()
        mine.wait()

    return pl.pallas_call(
        body, name=name,
        out_shape=jax.ShapeDtypeStruct((N_DEV * m, n), v.dtype),
        in_specs=[pl.BlockSpec(memory_space=pltpu.VMEM)], out_specs=pl.BlockSpec(memory_space=pltpu.VMEM),
        scratch_shapes=[pltpu.SemaphoreType.DMA((7,)), pltpu.SemaphoreType.DMA((7,)), pltpu.SemaphoreType.DMA],
    )(v)


ADD_BLOCKS = 2


def _pair_add(place, gs, ts, *, name):
    n_a = len(gs)

    def body(pl_ref, *refs):
        g_refs, t_refs = refs[:n_a], refs[n_a:2 * n_a]
        pf_refs, pb_refs = refs[2 * n_a:3 * n_a], refs[3 * n_a:]
        for g_ref, t_ref, pf_ref, pb_ref in zip(g_refs, t_refs, pf_refs, pb_refs):
            s = g_ref[...] + t_ref[...]
            pf_ref[...] = s
            pb_ref[...] = s.astype(BF16)

    def blk(t, own_half):
        tr = t.shape[1] // ADD_BLOCKS
        if own_half:
            return pl.BlockSpec((1, tr, t.shape[2]), lambda q, r, p: (q, p[0] * ADD_BLOCKS + r, 0))
        return pl.BlockSpec((1, tr, t.shape[2]), lambda q, r, p: (q, r, 0))

    grid_spec = pltpu.PrefetchScalarGridSpec(
        num_scalar_prefetch=1, grid=(N_CHIPS, ADD_BLOCKS),
        in_specs=[blk(t, True) for t in ts] + [blk(t, False) for t in ts],
        out_specs=[blk(t, False) for t in ts] * 2)
    res = pl.pallas_call(
        body, name=name, grid_spec=grid_spec,
        out_shape=[jax.ShapeDtypeStruct(t.shape, F32) for t in ts] + [jax.ShapeDtypeStruct(t.shape, BF16) for t in ts],
        compiler_params=_params(("parallel", "parallel")),
    )(place, *gs, *ts)
    return list(res[:n_a]), list(res[n_a:])


def _chip_add(place, pfs, ts, *, name):
    n_a = len(pfs)

    def body(pl_ref, *refs):
        pf_refs, t_refs, o_refs = refs[:n_a], refs[n_a:4 * n_a], refs[4 * n_a:]
        for i, (pf_ref, o_ref) in enumerate(zip(pf_refs, o_refs)):
            t1, t2, t3 = t_refs[3 * i:3 * i + 3]
            o_ref[...] = ((pf_ref[0] + t1[0].astype(F32)) + t2[0].astype(F32)) + t3[0].astype(F32)

    def slot(t, j):
        return pl.BlockSpec((1, t.shape[1] // ADD_BLOCKS, t.shape[2]), lambda r, p: (p[1] ^ j, r, 0))

    grid_spec = pltpu.PrefetchScalarGridSpec(
        num_scalar_prefetch=1, grid=(ADD_BLOCKS,),
        in_specs=[slot(t, 0) for t in pfs] + [slot(t, j) for t in ts for j in (1, 2, 3)],
        out_specs=[pl.BlockSpec((t.shape[1] // ADD_BLOCKS, t.shape[2]), lambda r, p: (r, 0)) for t in pfs])
    res = pl.pallas_call(
        body, name=name, grid_spec=grid_spec,
        out_shape=[jax.ShapeDtypeStruct(t.shape[1:], F32) for t in pfs],
        compiler_params=_params(("parallel",)),
    )(place, *pfs, *[t for t in ts for _ in range(3)])
    return list(res)


BULK = [("ffn1_w1", "colsT"), ("ffn1_w3", "colsT"), ("ffn1_w2", "rows"), ("w_in", "cols"), ("w_uq", "colsT"),
        ("w_ukv", "cols"), ("w_out", "rows"), ("ffn2_w1", "colsT"), ("ffn2_w3", "colsT"), ("ffn2_w2", "rows")]
KIND = dict(BULK)


def _group(*names):
    return [b for b in BULK if b[0] in names]


W_FIRST = _group("ffn1_w1", "ffn1_w3")
W_REST = [b for b in BULK if b not in W_FIRST]
W_MIX = _group("ffn1_w2", "w_in", "w_uq", "w_ukv", "w_out")
W_FFN2 = _group("ffn2_w1", "ffn2_w3", "ffn2_w2")
G_FFN2 = _group("ffn2_w1", "ffn2_w3", "ffn2_w2")
G_MIX = _group("w_in", "w_uq", "w_ukv", "w_out")
G_FFN1 = _group("ffn1_w1", "ffn1_w3", "ffn1_w2")


def _gathered_weights(specs, shards, got, myq):
    out = {}
    for (name, kind), part in zip(specs, got):
        part = lax.dynamic_update_slice_in_dim(part, shards[name][None], myq, axis=0)
        out[name] = _full_weight(part, kind)
    return out


def _working_shard(w, kind):
    return jnp.swapaxes(w, 1, 2)[0] if kind == "colsT" else w[0]


def _full_weight(parts, kind):
    if kind == "cols":
        return jnp.transpose(parts, (1, 0, 2)).reshape(parts.shape[1], -1)
    return parts.reshape(-1, parts.shape[2])


def _quarters(g, kind):
    if kind == "cols":
        k, n = g.shape
        return jnp.transpose(g.reshape(k, N_CHIPS, n // N_CHIPS), (1, 0, 2))
    return g.reshape(N_CHIPS, g.shape[0] // N_CHIPS, g.shape[1])


def _pad_heads(w_uq_t):
    w = w_uq_t.reshape(HEADS, QK_NOPE + QK_ROPE, Q_LORA)
    return jnp.pad(w, ((0, 0), (0, HEAD_SLOT - QK_NOPE - QK_ROPE), (0, 0))).reshape(HEADS * HEAD_SLOT, Q_LORA)


def _unpad_heads(g):
    return g.reshape(HEADS, HEAD_SLOT, Q_LORA)[:, :QK_NOPE + QK_ROPE].reshape(HEADS * (QK_NOPE + QK_ROPE), Q_LORA)


def _split_kv(w_ukv):
    return jnp.transpose(w_ukv.reshape(KV_LORA, HEADS, 2, 128), (0, 2, 1, 3)).reshape(KV_LORA, 2 * HEADS * 128)


def _merge_kv(g):
    return jnp.transpose(g.reshape(KV_LORA, 2, HEADS, 128), (0, 2, 1, 3)).reshape(KV_LORA, 2 * HEADS * 128)


def _rope_tables(positions):
    inv_freq = ROPE_THETA ** (-jnp.arange(0, QK_ROPE, 2, dtype=F32) / QK_ROPE)
    ang = positions.astype(F32)[:, None] * inv_freq
    cos, sin, zero = jnp.cos(ang), jnp.sin(ang), jnp.zeros((positions.shape[0], 64), F32)
    return jnp.concatenate([cos, cos, zero], axis=1), jnp.concatenate([sin, sin, zero], axis=1)


def _reduce_tail(place, specs, pfs, t2s, tag, host=None):
    rhs = _chip_add(place, pfs, t2s, name=tag + "_chip_add")
    if host is None:
        out, others = None, _run_comm(_PairShare(rhs), name=tag + "_pair_share")
    else:
        out, others = host(_PairShare(rhs))
    return out, _assemble(place, specs, rhs, others)


def _assemble(place, specs, rhs, others):
    south = place[0] == 0
    return {b[0]: jnp.concatenate([jnp.where(south, rh, ot), jnp.where(south, ot, rh)], axis=0)
            for b, rh, ot in zip(specs, rhs, others)}


def _local_step(x, positions, target, mod, vec, conv_w, w_first, rest_shards, place):
    row = lambda k: mod[k:k + 1]
    sh1, sc1, g1, sh2, sc2, g2, sh3, sc3, g3 = [row(k) for k in range(N_MOD)]
    cs, sn = _rope_tables(positions)
    cw8 = jnp.pad(conv_w, ((0, 5), (0, 0)))
    ga, gb = _group_mats()
    dist = place is not None
    comm = lambda prog: prog if dist else None

    w = dict(w_first) if dist else {**rest_shards, **w_first}

    def gather(specs):
        return _Gather([rest_shards[b[0]] for b in specs]) if dist else None

    def arrived(specs, got):
        if dist:
            w.update(_gathered_weights(specs, rest_shards, got, place[1]))

    (h1, a1, b1, u1), got = _ffn_up(x, vec["norm_ffn1_g"], sh1, sc1, w["ffn1_w1"], w["ffn1_w3"],
                                    name="ffn1_up", comm=gather(W_MIX))
    arrived(W_MIX, got)
    w_in = jnp.pad(w["w_in"].T, ((0, Z_COLS - IN_COLS), (0, 0)))
    wuq = _pad_heads(w["w_uq"])
    wukv = _split_kv(w["w_ukv"])
    (x1, f1), got = _ffn_down(u1, w["ffn1_w2"], x, g1, name="ffn1_down", comm=gather(W_FFN2[0:1]))
    arrived(W_FFN2[0:1], got)
    (h2, z), got = _mix_in(x1, vec["norm_mix_g"], sh2, sc2, w_in, comm=gather(W_FFN2[1:2]))
    arrived(W_FFN2[1:2], got)
    (ya, q, k, v, cqn, ckvn), got = _mix_mid(z, cw8, vec["q_norm_g"], vec["kv_norm_g"], wuq, wukv, cs, sn,
                                             comm=gather(W_FFN2[2:3]))
    arrived(W_FFN2[2:3], got)
    o, lse = _attention(q, k, v)
    x2, yn, yo = _mix_out(ya, o, vec["out_norm_g"], w["w_out"], x1, g2, ga, gb)
    (h3, a3, b3, u3), _ = _ffn_up(x2, vec["norm_ffn2_g"], sh3, sc3, w["ffn2_w1"], w["ffn2_w3"], name="ffn2_up")
    dx3, f3, dgfin, loss_blk = _ffn_down_loss(u3, w["ffn2_w2"], x2, g3, vec["final_norm_g"], target)

    grads, reduced = {}, {}

    def tn(a, b, tm, tn_, name, prog=None):
        if prog is None:
            return _tn_matmul(a, b, tm=tm, tn=tn_, name=name), None
        return _tn_matmul(a, b, tm=tm, tn=tn_, name=name, comm=prog)

    def slab_of(specs):
        return [_quarters(grads[n], kind) for n, kind in specs]

    (df3, da3, db3, dg3), _ = _ffn_bwd_du(dx3, g3, f3, w["ffn2_w2"], a3, b3, name="ffn2_bwd_du")
    grads["ffn2_w2"], _ = tn(u3, df3, FF // 2, D, "ffn2_dw2")
    grads["ffn2_w1"], _ = tn(da3, h3, FF // 2, D, "ffn2_dw1")
    grads["ffn2_w3"], _ = tn(db3, h3, FF // 2, D, "ffn2_dw3")
    (dx2, s3), _ = _dh_normbwd([(da3, w["ffn2_w1"]), (db3, w["ffn2_w3"])], x2, vec["norm_ffn2_g"], sc3, dx3,
                               name="ffn2_bwd_dh")

    p1 = slab_of(G_FFN2) if dist else None
    (dyo, dya, do, delta, s_out), t1 = _mix_out_bwd(dx2, g2, yo, w["w_out"], ya, o, vec["out_norm_g"], ga, gb,
                                                    comm=comm(_PairExchange(p1) if dist else None))
    grads["w_out"], _ = tn(yn, dyo, D, D, "dw_out")
    if dist:
        pf1, pb1 = _pair_add(place, p1, t1, name="ffn2g_pair_add")
    dq, dk, dv = _attention_bwd(q, k, v, do, lse, delta)
    (dz, dqf, dkvf, s_mid), t2 = _mix_mid_bwd(z, dya, cw8, vec["q_norm_g"], vec["kv_norm_g"], wuq, wukv, cs, sn,
                                              dq, dk, dv, comm=comm(_ChipExchange(pb1) if dist else None))
    g_uq, _ = tn(dqf, cqn, HEADS * HEAD_SLOT, Q_LORA, "dw_uq")
    g_ukv, _ = tn(ckvn, dkvf, KV_LORA, 2 * HEADS * 128, "dw_ukv")
    grads["w_uq"], grads["w_ukv"] = _unpad_heads(g_uq), _merge_kv(g_ukv)
    if dist:
        g_in, red = _reduce_tail(place, G_FFN2, pf1, t2, "ffn2g",
                                 host=lambda prog: tn(h2, dz, D, Z_COLS // 2, "dw_in", prog))
        reduced.update(red)
    else:
        g_in, _ = tn(h2, dz, D, Z_COLS // 2, "dw_in")
    grads["w_in"] = g_in[:, :IN_COLS]
    (dx1, s2), _ = _dh_normbwd([(dz, w_in)], x1, vec["norm_mix_g"], sc2, dx2, name="mix_bwd_dh")

    p2 = slab_of(G_MIX) if dist else None
    (df1, da1, db1, dg1), t1 = _ffn_bwd_du(dx1, g1, f1, w["ffn1_w2"], a1, b1, name="ffn1_bwd_du",
                                           comm=comm(_PairExchange(p2) if dist else None))
    dh_pairs = [(da1, w["ffn1_w1"]), (db1, w["ffn1_w3"])]
    if dist:
        pf2, pb2 = _pair_add(place, p2, t1, name="mixg_pair_add")
        grads["ffn1_w2"], t2 = tn(u1, df1, FF // 2, D, "ffn1_dw2", _ChipExchange(pb2))
        rh_mix = _chip_add(place, pf2, t2, name="mixg_chip_add")
        q_w2 = [_quarters(grads["ffn1_w2"], KIND["ffn1_w2"])]
        grads["ffn1_w1"], got = tn(da1, h1, FF // 2, D, "ffn1_dw1", _Multi([_PairShare(rh_mix), _PairExchange(q_w2)]))
        reduced.update(_assemble(place, G_MIX, rh_mix, got[:len(rh_mix)]))
        pf_w2, pb_w2 = _pair_add(place, q_w2, got[len(rh_mix):], name="ffn1w2_pair_add")
        q_w1 = [_quarters(grads["ffn1_w1"], KIND["ffn1_w1"])]
        grads["ffn1_w3"], got = tn(db1, h1, FF // 2, D, "ffn1_dw3", _Multi([_ChipExchange(pb_w2), _PairExchange(q_w1)]))
        t2_w2 = got[:1]
        pf_w1, pb_w1 = _pair_add(place, q_w1, got[1:], name="ffn1w1_pair_add")
        q_w3 = [_quarters(grads["ffn1_w3"], KIND["ffn1_w3"])]
        (dx0, s1), got = _dh_normbwd(dh_pairs, x, vec["norm_ffn1_g"], sc1, dx1, name="ffn1_bwd_dh",
                                     comm=_Multi([_ChipExchange(pb_w1), _PairExchange(q_w3)]))
        t2_w1 = got[:1]
        pf_w3, pb_w3 = _pair_add(place, q_w3, got[1:], name="ffn1w3_pair_add")
        t2_w3 = _run_comm(_ChipExchange(pb_w3), name="ffn1w3_chip_exchange")
        rh = _chip_add(place, pf_w1 + pf_w3 + pf_w2, t2_w1 + t2_w3 + t2_w2, name="ffn1g_chip_add")
        reduced.update(_assemble(place, G_FFN1, rh, _run_comm(_PairShare(rh), name="ffn1g_pair_share")))
    else:
        grads["ffn1_w2"], _ = tn(u1, df1, FF // 2, D, "ffn1_dw2")
        grads["ffn1_w1"], _ = tn(da1, h1, FF // 2, D, "ffn1_dw1")
        grads["ffn1_w3"], _ = tn(db1, h1, FF // 2, D, "ffn1_dw3")
        (dx0, s1), _ = _dh_normbwd(dh_pairs, x, vec["norm_ffn1_g"], sc1, dx1, name="ffn1_bwd_dh")
        reduced = grads

    def part(s, k):
        return s[0:1, k * D:(k + 1) * D]

    dmod = jnp.concatenate([part(s1, 1), part(s1, 0), dg1[0:1], part(s2, 1), part(s2, 0), part(s_out, 0),
                            part(s3, 1), part(s3, 0), dg3[0:1]], axis=1)
    small = {"norm_ffn1_g": part(s1, 2), "norm_mix_g": part(s2, 2), "out_norm_g": part(s_out, 1),
             "norm_ffn2_g": part(s3, 2), "final_norm_g": dgfin[0:1],
             "q_norm_g": s_mid[0:1, 3 * CONV_W:3 * CONV_W + Q_LORA],
             "kv_norm_g": s_mid[0:1, 3 * CONV_W + Q_LORA:MID_SUMS], "conv_w": s_mid[0:1, 0:3 * CONV_W]}
    return loss_blk, dx0, reduced, dmod, small


SMALL = [("norm_ffn1_g", D), ("norm_mix_g", D), ("out_norm_g", D), ("norm_ffn2_g", D), ("final_norm_g", D),
         ("q_norm_g", Q_LORA), ("kv_norm_g", KV_LORA), ("conv_w", 3 * CONV_W)]
WEIGHTS = ['ada_w', 'ada_b', 'norm_ffn1_g', 'ffn1_w1', 'ffn1_w3', 'ffn1_w2', 'norm_mix_g', 'w_in', 'conv_w',
           'q_norm_g', 'w_uq', 'kv_norm_g', 'w_ukv', 'out_norm_g', 'w_out', 'norm_ffn2_g', 'ffn2_w1', 'ffn2_w3',
           'ffn2_w2', 'final_norm_g']


def kernel(x, c, positions, ada_w, ada_b, norm_ffn1_g, ffn1_w1, ffn1_w3, ffn1_w2, norm_mix_g, w_in, conv_w, q_norm_g, w_uq, kv_norm_g, w_ukv, out_norm_g, w_out, norm_ffn2_g, ffn2_w1, ffn2_w3, ffn2_w2, final_norm_g, loss_target, m_ada_w, m_ada_b, m_norm_ffn1_g, m_ffn1_w1, m_ffn1_w3, m_ffn1_w2, m_norm_mix_g, m_w_in, m_conv_w, m_q_norm_g, m_w_uq, m_kv_norm_g, m_w_ukv, m_out_norm_g, m_w_out, m_norm_ffn2_g, m_ffn2_w1, m_ffn2_w3, m_ffn2_w2, m_final_norm_g, v_ada_w, v_ada_b, v_norm_ffn1_g, v_ffn1_w1, v_ffn1_w3, v_ffn1_w2, v_norm_mix_g, v_w_in, v_conv_w, v_q_norm_g, v_w_uq, v_kv_norm_g, v_w_ukv, v_out_norm_g, v_w_out, v_norm_ffn2_g, v_ffn2_w1, v_ffn2_w3, v_ffn2_w2, v_final_norm_g):
    args = dict(locals())
    wts = {n: args[n] for n in WEIGHTS}
    mom = {n: args["m_" + n] for n in WEIGHTS}
    var = {n: args["v_" + n] for n in WEIGHTS}
    ax, ay, ac = _place()
    myq = 2 * ax + ay
    me = 2 * myq + ac
    place = jnp.stack([ac, myq]).astype(jnp.int32)

    shards = {name: _working_shard(wts[name], kind).astype(BF16) for name, kind in BULK}
    first = _run_comm(_Gather([shards[b[0]] for b in W_FIRST]), name="gather_ffn1")
    w_first = _gathered_weights(W_FIRST, shards, first, myq)

    mine = jnp.concatenate([c, conv_w[0].reshape(1, 3 * CONV_W // N_CHIPS)], axis=1)
    seen = _small_allgather(jnp.pad(mine, ((0, 7), (0, 0))), name="gather_cond").reshape(N_DEV, 8, -1)[:, 0]
    c_all = jnp.pad(seen[:, :D], ((0, 8), (0, 0)))
    conv_full = jnp.transpose(seen[0::2, D:].reshape(N_CHIPS, 3, CONV_W // N_CHIPS), (1, 0, 2)).reshape(3, CONV_W)
    ada_b_q = lax.dynamic_slice_in_dim(ada_b, myq * ADA_Q, ADA_Q, axis=1)
    mod_q = _ada_forward(c_all, ada_w[0], ada_b_q)
    mod_all = _small_allgather(mod_q, name="gather_mod").reshape(N_DEV, 16, ADA_Q)
    mod_rows = jnp.transpose(mod_all[0::2, :N_DEV], (1, 0, 2)).reshape(N_DEV, N_MOD * D)
    mod = lax.dynamic_slice_in_dim(mod_rows, me, 1, axis=0).reshape(N_MOD, D)

    vec = {n: wts[n] for n in ("norm_ffn1_g", "norm_mix_g", "q_norm_g", "kv_norm_g", "out_norm_g", "norm_ffn2_g")}
    vec["final_norm_g"] = final_norm_g.reshape(1, D)
    loss_blk, grad_x, gq, dmod, small = _local_step(x[0], positions[0], loss_target[0], mod, vec, conv_full, w_first,
                                                    {b[0]: shards[b[0]] for b in W_REST}, place)
    loss = lax.psum(loss_blk[0, 0], ("x", "y", "c"))

    rows = jnp.concatenate([dmod] + [small[n] for n, _ in SMALL], axis=1)
    width = rows.shape[1]
    fold = -(-width // (8 * 128)) * 128
    rows = jnp.pad(rows, ((0, 0), (0, 8 * fold - width))).reshape(8, fold)
    every = _small_allgather(rows, name="gather_small").reshape(N_DEV, 8 * fold)[:, :width]
    total = _sum_devices(every)[0:1]
    dmod_q = lax.dynamic_slice_in_dim(every[:, :N_MOD * D], myq * ADA_Q, ADA_Q, axis=1)
    g = {name: gq[name] for name, *_ in BULK}
    g["ada_w"] = _ada_wgrad(c_all, jnp.pad(dmod_q, ((0, 8), (0, 0))))
    g["ada_b"] = total[:, :N_MOD * D]
    off = N_MOD * D
    for n, width in SMALL:
        g[n] = total[:, off:off + width]
        off += width
    g["conv_w"] = lax.dynamic_slice_in_dim(g["conv_w"].reshape(3, CONV_W), myq * (CONV_W // N_CHIPS),
                                           CONV_W // N_CHIPS, axis=1)
    g["final_norm_g"] = g["final_norm_g"].reshape(D)

    delta, new_m, new_v = {}, {}, {}
    for name in ["ada_w"] + [b[0] for b in BULK]:
        view = (lambda a: jnp.swapaxes(a, 1, 2)) if KIND.get(name) == "colsT" else (lambda a: a)
        g[name], delta[name], new_m[name], new_v[name] = [
            view(r) for r in _adamw(view(wts[name]), g[name], view(mom[name]), view(var[name]), name="adamw_" + name)]
    smalls = ["ada_b"] + [n for n, _ in SMALL]

    def packed(d):
        flat = jnp.concatenate([d[n].reshape(1, -1) for n in smalls], axis=1)
        return jnp.pad(flat.reshape(-1, D), ((0, 1), (0, 0)))

    res = _adamw(packed(wts)[None], packed(g), packed(mom)[None], packed(var)[None], name="adamw_small")[1:]
    off = 0
    for n in smalls:
        size = wts[n].size
        for d, r in zip((delta, new_m, new_v), res):
            d[n] = r.reshape(-1)[off:off + size].reshape(wts[n].shape)
        g[n] = g[n].reshape(wts[n].shape)
        off += size

    return (loss, grad_x[None], *[g[n] for n in WEIGHTS], *[delta[n] for n in WEIGHTS],
            *[new_m[n] for n in WEIGHTS], *[new_v[n] for n in WEIGHTS])
```

```python
import functools

import numpy as np
import jax
import jax.numpy as jnp
from jax import lax
from jax.experimental import pallas as pl
from jax.experimental.pallas import tpu as pltpu

F32 = jnp.float32
BF16 = jnp.bfloat16
MESH = pl.DeviceIdType.MESH

D = 1024
FF = 2816
CONV_W = 512
CONV_GROUP = 64
HEADS = 4
QK_NOPE = 128
QK_ROPE = 64
V_HEAD = 128
Q_LORA = 384
KV_LORA = 256
HEAD_SLOT = 256
IN_COLS = 3 * CONV_W + Q_LORA + KV_LORA + QK_ROPE
Z_COLS = 2304
EPS = 1e-6
ROPE_THETA = 10000.0
CHUNK = 64
ATT_SCALE = (QK_NOPE + QK_ROPE) ** -0.5
NEG = -1e30
EXP2_SCALE = ATT_SCALE * 1.4426950408889634
N_MOD = 9

LR, B1, B2, AEPS, WD, STEP = 0.001, 0.9, 0.999, 1e-08, 0.01, 10

N_CHIPS = 4
N_DEV = 8
VMEM_LIMIT = 56 << 20


def _params(sem, vmem=VMEM_LIMIT):
    return pltpu.CompilerParams(dimension_semantics=sem, vmem_limit_bytes=vmem)


def _rms(v):
    return lax.rsqrt(jnp.mean(v * v, axis=-1, keepdims=True) + EPS)


def _rsum8(v):
    t, n = v.shape
    return jnp.sum(v.reshape(t // 8, 8, n), axis=0)


def _all_rows(ref):
    ref[...] = jnp.broadcast_to(jnp.sum(ref[...], axis=0, keepdims=True), ref.shape)


def _gsum(v, gmat, split=False):
    hi = v.astype(BF16)
    out = jnp.dot(hi, gmat, preferred_element_type=F32)
    if split:
        out = out + jnp.dot((v - hi.astype(F32)).astype(BF16), gmat, preferred_element_type=F32)
    return out


def _dot_nt(a, b):
    return lax.dot_general(a, b, (((1,), (1,)), ((), ())), preferred_element_type=F32)


def _silu_parts(a):
    sg = jax.nn.sigmoid(a)
    return sg, a * sg


def _rope(xr, cs, sn, lane):
    rh = jnp.where(lane < 32, -pltpu.roll(xr, 96, 1), pltpu.roll(xr, 32, 1))
    return xr * cs + rh * sn


def _rope_t(g, cs, sn, lane):
    y = g * sn
    rt = jnp.where(lane < 32, pltpu.roll(y, 96, 1), jnp.where(lane < 64, -pltpu.roll(y, 32, 1), 0.0))
    return g * cs + rt


def _row_tile(rows, pref, mult=8):
    t = min(rows, pref) // mult * mult
    while rows % t:
        t -= mult
    return t


def _place():
    return lax.axis_index("x"), lax.axis_index("y"), lax.axis_index("c")


ANY = pl.BlockSpec(memory_space=pl.ANY)


def _hosted_call(body, *, name, grid, in_specs, out_specs, out_shape, scratch_shapes, semantics, args, comm=None,
                 prefetch=()):
    n_in, n_out, n_scr, n_pf = len(in_specs), len(out_specs), len(scratch_shapes), len(prefetch)

    def call(fn, in_specs, out_specs, out_shape, scratch_shapes, semantics, operands):
        spec = pltpu.PrefetchScalarGridSpec(num_scalar_prefetch=n_pf, grid=grid, in_specs=list(in_specs),
                                            out_specs=list(out_specs), scratch_shapes=list(scratch_shapes))
        return pl.pallas_call(fn, name=name, grid_spec=spec, out_shape=list(out_shape),
                              compiler_params=_params(semantics))(*prefetch, *operands)

    if comm is None:
        return list(call(body, in_specs, out_specs, out_shape, scratch_shapes, semantics, args)), []
    n_ci, n_co = len(comm.inputs), len(comm.out_shapes)
    total = int(np.prod(grid))

    def hosted(*refs):
        tables, refs = refs[:n_pf], refs[n_pf:]
        ins, refs = refs[:n_in], refs[n_in:]
        cins, refs = refs[:n_ci], refs[n_ci:]
        outs, refs = refs[:n_out], refs[n_out:]
        couts, refs = refs[:n_co], refs[n_co:]
        scratch, sems = refs[:n_scr], refs[n_scr]
        step = pl.program_id(0)
        for ax in range(1, len(grid)):
            step = step * grid[ax] + pl.program_id(ax)

        @pl.when(step == 0)
        def _():
            comm.start(cins, couts, sems)

        body(*tables, *ins, *outs, *scratch)

        @pl.when(step == total - 1)
        def _():
            comm.finish(cins, couts, sems)

    res = call(hosted, list(in_specs) + [ANY] * n_ci, list(out_specs) + [ANY] * n_co,
               list(out_shape) + list(comm.out_shapes),
               list(scratch_shapes) + [pltpu.SemaphoreType.DMA((comm.n_sems,))],
               ("arbitrary",) * len(grid), (*args, *comm.inputs))
    return list(res[:n_out]), list(res[n_out:])


def _run_comm(comm, *, name):
    n_ci = len(comm.inputs)

    def body(*refs):
        cins, couts, sems = refs[:n_ci], refs[n_ci:-1], refs[-1]
        comm.start(cins, couts, sems)
        comm.finish(cins, couts, sems)

    return list(pl.pallas_call(
        body, name=name, out_shape=list(comm.out_shapes), in_specs=[ANY] * n_ci,
        out_specs=[ANY] * len(comm.out_shapes), scratch_shapes=[pltpu.SemaphoreType.DMA((comm.n_sems,))],
    )(*comm.inputs))


class _Gather:
    def __init__(self, slabs):
        self.inputs = list(slabs)
        self.out_shapes = [jax.ShapeDtypeStruct((N_CHIPS,) + s.shape, s.dtype) for s in slabs]
        self.n_sems = 12 * len(slabs)

    @staticmethod
    def _copy(out, sems, base, k, chip, hc, to, src=None):
        H = out.shape[1] // 2
        half = out.at[2 * chip[0] + chip[1], pl.ds(hc * H, H), :]
        return pltpu.make_async_remote_copy(
            src_ref=half if src is None else src, dst_ref=half, send_sem=sems.at[base + k],
            recv_sem=sems.at[base + 6 + k], device_id=to, device_id_type=MESH)

    def _firsts(self, src, out, sems, base):
        x, y, c = _place()
        H = src.shape[0] // 2
        chips = [(1 - x, y), (x, 1 - y), (1 - x, 1 - y)]
        return [self._copy(out, sems, base, j, (x, y), c, (*chip, c), src=src.at[pl.ds(c * H, H), :])
                for j, chip in enumerate(chips)]

    def start(self, ins, outs, sems):
        for i, (src, out) in enumerate(zip(ins, outs)):
            for cp in self._firsts(src, out, sems, 12 * i):
                cp.start()

    def finish(self, ins, outs, sems):
        x, y, c = _place()
        chips = [(1 - x, y), (x, 1 - y), (1 - x, 1 - y)]
        passed = []
        for i, out in enumerate(outs):
            for j, chip in enumerate(chips):
                self._copy(out, sems, 12 * i, j, chip, c, (x, y, c)).wait_recv()
                cp = self._copy(out, sems, 12 * i, 3 + j, chip, c, (x, y, 1 - c))
                cp.start()
                passed.append(cp)
        for i, out in enumerate(outs):
            for j, chip in enumerate(chips):
                self._copy(out, sems, 12 * i, 3 + j, chip, 1 - c, (x, y, c)).wait_recv()
        for cp in passed:
            cp.wait_send()
        for i, (src, out) in enumerate(zip(ins, outs)):
            for cp in self._firsts(src, out, sems, 12 * i):
                cp.wait_send()


class _PairExchange:
    def __init__(self, arrays):
        self.inputs = list(arrays)
        self.out_shapes = [jax.ShapeDtypeStruct((N_CHIPS, a.shape[1] // 2, a.shape[2]), a.dtype) for a in arrays]
        self.n_sems = 2 * len(arrays)

    def _copies(self, ins, outs, sems):
        x, y, c = _place()
        return [pltpu.make_async_remote_copy(
            src_ref=g.at[:, pl.ds((1 - c) * t.shape[1], t.shape[1]), :], dst_ref=t, send_sem=sems.at[2 * i],
            recv_sem=sems.at[2 * i + 1], device_id=(x, y, 1 - c), device_id_type=MESH)
            for i, (g, t) in enumerate(zip(ins, outs))]

    def start(self, ins, outs, sems):
        for cp in self._copies(ins, outs, sems):
            cp.start()

    def finish(self, ins, outs, sems):
        for cp in self._copies(ins, outs, sems):
            cp.wait()


class _ChipExchange:
    def __init__(self, arrays):
        self.inputs = list(arrays)
        self.out_shapes = [jax.ShapeDtypeStruct(a.shape, a.dtype) for a in arrays]
        self.n_sems = 6 * len(arrays)

    def _copies(self, p, t, sems, base):
        x, y, c = _place()
        myq = 2 * x + y
        chips = [(1 - x, y), (x, 1 - y), (1 - x, 1 - y)]
        sends = [pltpu.make_async_remote_copy(
            src_ref=p.at[2 * chip[0] + chip[1]], dst_ref=t.at[myq], send_sem=sems.at[base + j],
            recv_sem=sems.at[base + 3 + j], device_id=(*chip, c), device_id_type=MESH) for j, chip in enumerate(chips)]
        lands = [pltpu.make_async_remote_copy(
            src_ref=t.at[2 * chip[0] + chip[1]], dst_ref=t.at[2 * chip[0] + chip[1]], send_sem=sems.at[base + j],
            recv_sem=sems.at[base + 3 + j], device_id=(*chip, c), device_id_type=MESH) for j, chip in enumerate(chips)]
        return sends, lands

    def start(self, ins, outs, sems):
        for i, (p, t) in enumerate(zip(ins, outs)):
            for cp in self._copies(p, t, sems, 6 * i)[0]:
                cp.start()

    def finish(self, ins, outs, sems):
        for i, (p, t) in enumerate(zip(ins, outs)):
            sends, lands = self._copies(p, t, sems, 6 * i)
            for cp in lands:
                cp.wait_recv()
            for cp in sends:
                cp.wait_send()


class _SemView:
    def __init__(self, sems, base):
        self._sems, self._base = sems, base

    @property
    def at(self):
        return self

    def __getitem__(self, k):
        return self._sems.at[self._base + k]


class _Multi:
    def __init__(self, progs):
        self.progs = list(progs)
        self.inputs = [a for p in self.progs for a in p.inputs]
        self.out_shapes = [s for p in self.progs for s in p.out_shapes]
        self.n_sems = sum(p.n_sems for p in self.progs)

    def _each(self, ins, outs, sems):
        i = o = s = 0
        for p in self.progs:
            ni, no = len(p.inputs), len(p.out_shapes)
            yield p, ins[i:i + ni], outs[o:o + no], _SemView(sems, s)
            i, o, s = i + ni, o + no, s + p.n_sems

    def start(self, ins, outs, sems):
        for p, a, b, c in self._each(ins, outs, sems):
            p.start(a, b, c)

    def finish(self, ins, outs, sems):
        for p, a, b, c in self._each(ins, outs, sems):
            p.finish(a, b, c)


class _PairShare:
    def __init__(self, arrays):
        self.inputs = list(arrays)
        self.out_shapes = [jax.ShapeDtypeStruct(a.shape, a.dtype) for a in arrays]
        self.n_sems = 2 * len(arrays)

    def _copies(self, ins, outs, sems):
        x, y, c = _place()
        return [pltpu.make_async_remote_copy(
            src_ref=r, dst_ref=o, send_sem=sems.at[2 * i], recv_sem=sems.at[2 * i + 1],
            device_id=(x, y, 1 - c), device_id_type=MESH) for i, (r, o) in enumerate(zip(ins, outs))]

    def start(self, ins, outs, sems):
        for cp in self._copies(ins, outs, sems):
            cp.start()

    def finish(self, ins, outs, sems):
        for cp in self._copies(ins, outs, sems):
            cp.wait()


def _ffn_up(x, ng, sh, sc, w1, w3, *, name, comm=None):
    S = x.shape[0]
    tm, tn = _row_tile(S, 512), FF // 2

    def body(x_ref, g_ref, sh_ref, sc_ref, w1_ref, w3_ref, h_ref, a_ref, b_ref, u_ref, hs):
        @pl.when(pl.program_id(1) == 0)
        def _():
            xv = x_ref[...]
            h = ((xv * _rms(xv)) * g_ref[...]) * (1.0 + sc_ref[...]) + sh_ref[...]
            hb = h.astype(BF16)
            hs[...] = hb
            h_ref[...] = hb

        h = hs[...]
        cols = pl.ds(pl.multiple_of(pl.program_id(1) * tn, tn), tn)
        a = _dot_nt(h, w1_ref[cols, :])
        b = _dot_nt(h, w3_ref[cols, :])
        _, sa = _silu_parts(a)
        a_ref[...] = a.astype(BF16)
        b_ref[...] = b.astype(BF16)
        u_ref[...] = (sa * b).astype(BF16)

    row = pl.BlockSpec((tm, D), lambda i, j: (i, 0))
    vec = pl.BlockSpec((1, D), lambda i, j: (0, 0))
    wsp = pl.BlockSpec((FF, D), lambda i, j: (0, 0))
    osp = pl.BlockSpec((tm, tn), lambda i, j: (i, j))
    return _hosted_call(
        body, name=name, grid=(S // tm, FF // tn),
        in_specs=[row, vec, vec, vec, wsp, wsp],
        out_specs=[row, osp, osp, osp],
        out_shape=[jax.ShapeDtypeStruct((S, D), BF16)] + [jax.ShapeDtypeStruct((S, FF), BF16)] * 3,
        scratch_shapes=[pltpu.VMEM((tm, D), BF16)],
        semantics=("parallel", "arbitrary"), args=(x, ng, sh, sc, w1, w3), comm=comm)


def _ffn_down(u, w2, x, gate, *, name, comm=None):
    S = x.shape[0]
    tm = _row_tile(S, 512)

    def body(u_ref, w2_ref, x_ref, g_ref, xo_ref, f_ref):
        f = jnp.dot(u_ref[...], w2_ref[...], preferred_element_type=F32)
        xo_ref[...] = x_ref[...] + (0.5 * g_ref[...]) * f
        f_ref[...] = f.astype(BF16)

    return _hosted_call(
        body, name=name, grid=(S // tm,),
        in_specs=[pl.BlockSpec((tm, FF), lambda i: (i, 0)), pl.BlockSpec((FF, D), lambda i: (0, 0)),
                  pl.BlockSpec((tm, D), lambda i: (i, 0)), pl.BlockSpec((1, D), lambda i: (0, 0))],
        out_specs=[pl.BlockSpec((tm, D), lambda i: (i, 0))] * 2,
        out_shape=[jax.ShapeDtypeStruct((S, D), F32), jax.ShapeDtypeStruct((S, D), BF16)],
        scratch_shapes=[], semantics=("parallel",), args=(u, w2, x, gate), comm=comm)


def _ffn_bwd_du(dx, gate, f, w2, a, b, *, name, comm=None):
    S = dx.shape[0]
    tm, tn = _row_tile(S, 512), FF // 2
    n_i = S // tm

    def body(dx_ref, g_ref, f_ref, w_ref, a_ref, b_ref, df_ref, da_ref, db_ref, dg_ref, dfs):
        i, j = pl.program_id(0), pl.program_id(1)

        @pl.when((i == 0) & (j == 0))
        def _():
            dg_ref[...] = jnp.zeros_like(dg_ref)

        @pl.when(j == 0)
        def _():
            dxv = dx_ref[...]
            dfb = (dxv * (0.5 * g_ref[...])).astype(BF16)
            dfs[...] = dfb
            df_ref[...] = dfb
            dg_ref[...] += _rsum8(dxv * (0.5 * f_ref[...].astype(F32)))

        du = _dot_nt(dfs[...], w_ref[pl.ds(pl.multiple_of(j * tn, tn), tn), :])
        av = a_ref[...].astype(F32)
        sg, sa = _silu_parts(av)
        da_ref[...] = (du * b_ref[...].astype(F32) * (sg * (1.0 + av * (1.0 - sg)))).astype(BF16)
        db_ref[...] = (du * sa).astype(BF16)

        @pl.when((i == n_i - 1) & (j == FF // tn - 1))
        def _():
            _all_rows(dg_ref)

    row = pl.BlockSpec((tm, D), lambda i, j: (i, 0))
    blk = pl.BlockSpec((tm, tn), lambda i, j: (i, j))
    return _hosted_call(
        body, name=name, grid=(n_i, FF // tn),
        in_specs=[row, pl.BlockSpec((1, D), lambda i, j: (0, 0)), row,
                  pl.BlockSpec((FF, D), lambda i, j: (0, 0)), blk, blk],
        out_specs=[row, blk, blk, pl.BlockSpec((8, D), lambda i, j: (0, 0))],
        out_shape=[jax.ShapeDtypeStruct((S, D), BF16), jax.ShapeDtypeStruct((S, FF), BF16),
                   jax.ShapeDtypeStruct((S, FF), BF16), jax.ShapeDtypeStruct((8, D), F32)],
        scratch_shapes=[pltpu.VMEM((tm, D), BF16)],
        semantics=("arbitrary", "arbitrary"), args=(dx, gate, f, w2, a, b), comm=comm)


def _tn_matmul(a, b, *, tm, tn, name, comm=None):
    S, M = a.shape
    N = b.shape[1]
    ts = _row_tile(S, 2048)
    ns = S // ts

    def body(a_ref, b_ref, o_ref):
        s = pl.program_id(2)
        p = lax.dot_general(a_ref[...], b_ref[...], (((0,), (0,)), ((), ())), preferred_element_type=F32)

        @pl.when(s == 0)
        def _():
            o_ref[...] = p

        @pl.when(s > 0)
        def _():
            o_ref[...] += p

    (out,), couts = _hosted_call(
        body, name=name, grid=(M // tm, N // tn, ns),
        in_specs=[pl.BlockSpec((ts, tm), lambda i, j, s: (s, i)), pl.BlockSpec((ts, tn), lambda i, j, s: (s, j))],
        out_specs=[pl.BlockSpec((tm, tn), lambda i, j, s: (i, j))],
        out_shape=[jax.ShapeDtypeStruct((M, N), F32)],
        scratch_shapes=[], semantics=("parallel", "parallel", "arbitrary"), args=(a, b), comm=comm)
    return out if comm is None else (out, couts)


def _dh_normbwd(pairs, x, ng, sc, dx_next, *, name, comm=None):
    S = x.shape[0]
    n_p = len(pairs)
    tm = _row_tile(S, 512 if n_p == 1 else 256)
    n_i = S // tm

    def body(*refs):
        a_refs, w_refs = refs[:n_p], refs[n_p:2 * n_p]
        x_ref, g_ref, sc_ref, dxn_ref, dx_ref, p_ref = refs[2 * n_p:]
        i = pl.program_id(0)
        dh = jnp.dot(a_refs[0][...], w_refs[0][...], preferred_element_type=F32)
        for k in range(1, n_p):
            dh = dh + jnp.dot(a_refs[k][...], w_refs[k][...], preferred_element_type=F32)
        xv = x_ref[...]
        r = _rms(xv)
        xh = xv * r
        g = g_ref[...]
        dn = dh * (1.0 + sc_ref[...])
        dy = dn * g
        dx_ref[...] = dxn_ref[...] + r * (dy - xh * jnp.mean(dy * xh, axis=-1, keepdims=True))

        @pl.when(i == 0)
        def _():
            p_ref[...] = jnp.zeros_like(p_ref)

        p_ref[:, 0:D] += _rsum8(dh * (xh * g))
        p_ref[:, D:2 * D] += _rsum8(dh)
        p_ref[:, 2 * D:3 * D] += _rsum8(dn * xh)

        @pl.when(i == n_i - 1)
        def _():
            _all_rows(p_ref)

    row = pl.BlockSpec((tm, D), lambda i: (i, 0))
    vec = pl.BlockSpec((1, D), lambda i: (0, 0))
    in_specs = ([pl.BlockSpec((tm, a.shape[1]), lambda i: (i, 0)) for a, _ in pairs]
                + [pl.BlockSpec(w.shape, lambda i: (0, 0)) for _, w in pairs] + [row, vec, vec, row])
    return _hosted_call(
        body, name=name, grid=(n_i,), in_specs=in_specs,
        out_specs=[row, pl.BlockSpec((8, 3 * D), lambda i: (0, 0))],
        out_shape=[jax.ShapeDtypeStruct((S, D), F32), jax.ShapeDtypeStruct((8, 3 * D), F32)],
        scratch_shapes=[], semantics=("arbitrary",),
        args=(*[a for a, _ in pairs], *[w for _, w in pairs], x, ng, sc, dx_next), comm=comm)


def _ffn_down_loss(u, w2, x, gate, gfin, tgt):
    S = x.shape[0]
    tm = _row_tile(S, 512)
    n_i = S // tm

    def body(u_ref, w2_ref, x_ref, gt_ref, g_ref, t_ref, dx_ref, f_ref, dg_ref, loss_ref, lacc):
        i = pl.program_id(0)
        f = jnp.dot(u_ref[...], w2_ref[...], preferred_element_type=F32)
        f_ref[...] = f.astype(BF16)
        xv = x_ref[...] + (0.5 * gt_ref[...]) * f
        r = _rms(xv)
        xh = xv * r
        g = g_ref[...]
        e = xh * g - t_ref[...]
        dout = e * (1.0 / D)
        dy = dout * g
        dx_ref[...] = r * (dy - xh * jnp.mean(dy * xh, axis=-1, keepdims=True))

        @pl.when(i == 0)
        def _():
            dg_ref[...] = jnp.zeros_like(dg_ref)
            lacc[...] = jnp.zeros_like(lacc)

        dg_ref[...] += _rsum8(dout * xh)
        lacc[...] += _rsum8(e * e)

        @pl.when(i == n_i - 1)
        def _():
            _all_rows(dg_ref)
            tot = jnp.sum(jnp.sum(lacc[...], axis=0, keepdims=True), axis=1, keepdims=True)
            loss_ref[...] = jnp.broadcast_to(tot * (0.5 / D), loss_ref.shape)

    row = pl.BlockSpec((tm, D), lambda i: (i, 0))
    vec = pl.BlockSpec((1, D), lambda i: (0, 0))
    return pl.pallas_call(
        body, name="ffn2_down_loss", grid=(n_i,),
        in_specs=[pl.BlockSpec((tm, FF), lambda i: (i, 0)), pl.BlockSpec((FF, D), lambda i: (0, 0)), row, vec, vec, row],
        out_specs=[row, row, pl.BlockSpec((8, D), lambda i: (0, 0)), pl.BlockSpec((8, 128), lambda i: (0, 0))],
        out_shape=[jax.ShapeDtypeStruct((S, D), F32), jax.ShapeDtypeStruct((S, D), BF16),
                   jax.ShapeDtypeStruct((8, D), F32), jax.ShapeDtypeStruct((8, 128), F32)],
        scratch_shapes=[pltpu.VMEM((8, D), F32)],
        compiler_params=_params(("arbitrary",)),
    )(u, w2, x, gate, gfin, tgt)


def _mix_in(x, ng, sh, sc, w_in, comm=None):
    S = x.shape[0]
    tm = _row_tile(S, 512)

    def body(x_ref, g_ref, sh_ref, sc_ref, w_ref, h_ref, z_ref):
        xv = x_ref[...]
        hb = (((xv * _rms(xv)) * g_ref[...]) * (1.0 + sc_ref[...]) + sh_ref[...]).astype(BF16)
        h_ref[...] = hb
        z_ref[...] = _dot_nt(hb, w_ref[...])

    row = pl.BlockSpec((tm, D), lambda i: (i, 0))
    vec = pl.BlockSpec((1, D), lambda i: (0, 0))
    return _hosted_call(
        body, name="mix_in", grid=(S // tm,),
        in_specs=[row, vec, vec, vec, pl.BlockSpec((Z_COLS, D), lambda i: (0, 0))],
        out_specs=[row, pl.BlockSpec((tm, Z_COLS), lambda i: (i, 0))],
        out_shape=[jax.ShapeDtypeStruct((S, D), BF16), jax.ShapeDtypeStruct((S, Z_COLS), F32)],
        scratch_shapes=[], semantics=("parallel",), args=(x, ng, sh, sc, w_in), comm=comm)


def _conv_taps(u, halo, rows):
    u1 = jnp.where(rows == 0, halo[7:8, :], pltpu.roll(u, 1, 0))
    u2 = jnp.where(rows == 0, halo[6:7, :], jnp.where(rows == 1, halo[7:8, :], pltpu.roll(u, 2, 0)))
    return u1, u2


def _mix_mid(z, conv_w, gq, gkv, wuq, wukv, cs, sn, comm=None):
    S = z.shape[0]
    tm = _row_tile(S, 512)
    hb = tm // 8

    def body(z_ref, zh_ref, cw_ref, gq_ref, gkv_ref, wuq_ref, wukv_ref, cs_ref, sn_ref,
             ya_ref, q_ref, k_ref, v_ref, cqn_ref, ckvn_ref):
        i = pl.program_id(0)
        xb = z_ref[:, 0:CONV_W]
        u = z_ref[:, CONV_W:2 * CONV_W] * z_ref[:, 2 * CONV_W:3 * CONV_W]
        halo = zh_ref[:, CONV_W:2 * CONV_W] * zh_ref[:, 2 * CONV_W:3 * CONV_W]
        halo = jnp.where(i > 0, halo, 0.0)
        rows = lax.broadcasted_iota(jnp.int32, (tm, CONV_W), 0)
        u1, u2 = _conv_taps(u, halo, rows)
        y = cw_ref[0:1, :] * u2 + cw_ref[1:2, :] * u1 + cw_ref[2:3, :] * u
        ya_ref[...] = xb * y

        lane = lax.broadcasted_iota(jnp.int32, (tm, 128), 1)
        cs_v, sn_v = cs_ref[...], sn_ref[...]
        cq = z_ref[:, 3 * CONV_W:3 * CONV_W + Q_LORA]
        cqn = ((cq * _rms(cq)) * gq_ref[...]).astype(BF16)
        cqn_ref[...] = cqn
        q = _dot_nt(cqn, wuq_ref[...])
        for h in range(HEADS):
            o = h * HEAD_SLOT
            q_ref[:, o:o + 128] = q[:, o:o + 128].astype(BF16)
            q_ref[:, o + 128:o + 256] = _rope(q[:, o + 128:o + 256], cs_v, sn_v, lane).astype(BF16)

        c0 = 3 * CONV_W + Q_LORA
        ckv = z_ref[:, c0:c0 + KV_LORA]
        ckvn = ((ckv * _rms(ckv)) * gkv_ref[...]).astype(BF16)
        ckvn_ref[...] = ckvn
        kv = jnp.dot(ckvn, wukv_ref[...], preferred_element_type=F32)
        krot = _rope(z_ref[:, c0 + KV_LORA:Z_COLS], cs_v, sn_v, lane).astype(BF16)
        for h in range(HEADS):
            o = h * HEAD_SLOT
            k_ref[:, o:o + 128] = kv[:, h * 128:(h + 1) * 128].astype(BF16)
            k_ref[:, o + 128:o + 256] = krot
        v_ref[...] = kv[:, HEADS * 128:].astype(BF16)

    def rows_of(n):
        return pl.BlockSpec((tm, n), lambda i: (i, 0))

    def whole(shape):
        return pl.BlockSpec(shape, lambda i: (0, 0))

    return _hosted_call(
        body, name="mix_mid", grid=(S // tm,),
        in_specs=[rows_of(Z_COLS), pl.BlockSpec((8, Z_COLS), lambda i: (jnp.maximum(i * hb - 1, 0), 0)),
                  whole((8, CONV_W)), whole((1, Q_LORA)), whole((1, KV_LORA)),
                  whole((HEADS * HEAD_SLOT, Q_LORA)), whole((KV_LORA, 2 * HEADS * 128)),
                  rows_of(128), rows_of(128)],
        out_specs=[rows_of(CONV_W), rows_of(HEADS * HEAD_SLOT), rows_of(HEADS * HEAD_SLOT), rows_of(HEADS * V_HEAD),
                   rows_of(Q_LORA), rows_of(KV_LORA)],
        out_shape=[jax.ShapeDtypeStruct((S, CONV_W), F32), jax.ShapeDtypeStruct((S, HEADS * HEAD_SLOT), BF16),
                   jax.ShapeDtypeStruct((S, HEADS * HEAD_SLOT), BF16), jax.ShapeDtypeStruct((S, HEADS * V_HEAD), BF16),
                   jax.ShapeDtypeStruct((S, Q_LORA), BF16), jax.ShapeDtypeStruct((S, KV_LORA), BF16)],
        scratch_shapes=[], semantics=("parallel",), args=(z, z, conv_w, gq, gkv, wuq, wukv, cs, sn), comm=comm)


def _att_blocks(S):
    bk = min(1024, max(S // 4, 128))
    return bk, bk


def _pair_tables(S, k_major):
    bq, bk = _att_blocks(S)
    nq, nk = S // bq, S // bk
    vis = lambda qi, ki: ki * bk < (qi + 1) * bq
    if k_major:
        pairs = [(qi, ki) for ki in range(nk) for qi in range(nq) if vis(qi, ki)]
    else:
        pairs = [(qi, ki) for qi in range(nq) for ki in range(nk) if vis(qi, ki)]
    cols = [[p[0] for p in pairs], [p[1] for p in pairs], [int((p[1] + 1) * bk > p[0] * bq) for p in pairs]]
    return [jnp.asarray(np.array(c, np.int32)) for c in cols], len(pairs)


def _chunk_mask(r0, nr, nc):
    r = (r0 + lax.broadcasted_iota(jnp.int32, (nr, nc), 0)) // CHUNK
    c = lax.broadcasted_iota(jnp.int32, (nr, nc), 1) // CHUNK
    return c <= r


def _diag_parts(bq, bk):
    return [(0, bq // 2, bk // 2), (bq // 2, bq // 2, bk)]


def _attention(q, k, v, comm=None):
    S = q.shape[0]
    bq, bk = _att_blocks(S)
    last_k = bq // bk - 1
    tables, n_pairs = _pair_tables(S, k_major=False)

    def body(qi_ref, ki_ref, mk_ref, q_ref, k_ref, v_ref, o_ref, lse_ref, m_s, l_s, acc_s):
        p_id = pl.program_id(1)
        qi, ki = qi_ref[p_id], ki_ref[p_id]

        @pl.when(ki == 0)
        def _():
            m_s[...] = jnp.full_like(m_s, NEG)
            l_s[...] = jnp.zeros_like(l_s)
            acc_s[...] = jnp.zeros_like(acc_s)

        def update(r0, nr, nc, masked):
            rows = slice(r0, r0 + nr)
            s = _dot_nt(q_ref[rows, :], k_ref[0:nc, :])
            if masked:
                s = jnp.where(_chunk_mask(r0, nr, nc), s, NEG)
            m_prev = m_s[rows, :]
            m_new = jnp.maximum(m_prev, jnp.max(s, axis=1, keepdims=True))
            alpha = jnp.exp2((m_prev - m_new) * EXP2_SCALE)
            p = jnp.exp2((s - jnp.tile(m_new, (1, nc // 128))) * EXP2_SCALE)
            l_s[rows, :] = alpha * l_s[rows, :] + jnp.sum(p, axis=1, keepdims=True)
            acc_s[rows, :] = alpha * acc_s[rows, :] + jnp.dot(p.astype(BF16), v_ref[0:nc, :],
                                                              preferred_element_type=F32)
            m_s[rows, :] = m_new

        @pl.when(mk_ref[p_id] == 0)
        def _():
            update(0, bq, bk, False)

        @pl.when(mk_ref[p_id] == 1)
        def _():
            for part in _diag_parts(bq, bk):
                update(*part, True)

        @pl.when(ki == qi * (last_k + 1) + last_k)
        def _():
            l = l_s[...]
            o_ref[...] = acc_s[...] / l
            lse_ref[...] = m_s[...] * EXP2_SCALE + jnp.log2(l)

    return _hosted_call(
        body, name="attention", grid=(HEADS, n_pairs),
        in_specs=[pl.BlockSpec((bq, HEAD_SLOT), lambda h, p, qt, kt, mt: (qt[p], h)),
                  pl.BlockSpec((bk, HEAD_SLOT), lambda h, p, qt, kt, mt: (kt[p], h)),
                  pl.BlockSpec((bk, V_HEAD), lambda h, p, qt, kt, mt: (kt[p], h))],
        out_specs=[pl.BlockSpec((bq, V_HEAD), lambda h, p, qt, kt, mt: (qt[p], h))] * 2,
        out_shape=[jax.ShapeDtypeStruct((S, HEADS * V_HEAD), F32)] * 2,
        scratch_shapes=[pltpu.VMEM((bq, V_HEAD), F32)] * 3,
        semantics=("arbitrary", "arbitrary"), args=(q, k, v), comm=comm, prefetch=tables)


def _attention_bwd(q, k, v, do, lse2, delta):
    S = q.shape[0]
    bq, bk = _att_blocks(S)
    nq = S // bq
    tables, n_pairs = _pair_tables(S, k_major=True)

    def body(qi_ref, ki_ref, mk_ref, q_ref, k_ref, v_ref, do_ref, lse_ref, dl_ref, dq_hbm, dk_ref, dv_ref,
             dq_s, dk_s, dv_s, sem):
        head, p_id = pl.program_id(0), pl.program_id(1)
        qi, ki = qi_ref[p_id], ki_ref[p_id]

        @pl.when(qi * bq <= ki * bk)
        def _():
            dk_s[...] = jnp.zeros_like(dk_s)
            dv_s[...] = jnp.zeros_like(dv_s)

        def update(r0, nr, nc, masked):
            rows, cols = slice(r0, r0 + nr), slice(0, nc)
            qv, kv, dov = q_ref[rows, :], k_ref[cols, :], do_ref[rows, :]
            s = _dot_nt(qv, kv)
            dp = _dot_nt(dov, v_ref[cols, :])
            if masked:
                s = jnp.where(_chunk_mask(r0, nr, nc), s, NEG)
            p = jnp.exp2(s * EXP2_SCALE - jnp.tile(lse_ref[rows, :], (1, nc // 128)))
            dv_s[cols, :] += lax.dot_general(p.astype(BF16), dov, (((0,), (0,)), ((), ())),
                                             preferred_element_type=F32)
            ds = (p * (dp - jnp.tile(dl_ref[rows, :], (1, nc // 128)))).astype(BF16)
            dk_s[cols, :] += lax.dot_general(ds, qv, (((0,), (0,)), ((), ())), preferred_element_type=F32)
            dq = jnp.dot(ds, kv, preferred_element_type=F32)
            out_rows = pl.ds(pl.multiple_of(qi * bq + r0, nr), nr)

            @pl.when(ki == 0)
            def _():
                dq_s[out_rows, :] = dq

            @pl.when(ki > 0)
            def _():
                dq_s[out_rows, :] += dq

        @pl.when(mk_ref[p_id] == 0)
        def _():
            update(0, bq, bk, False)

        @pl.when(mk_ref[p_id] == 1)
        def _():
            for part in _diag_parts(bq, bk):
                update(*part, True)

        @pl.when(qi == nq - 1)
        def _():
            dk_ref[...] = dk_s[...] * ATT_SCALE
            dv_ref[...] = dv_s[...]

        @pl.when(p_id == n_pairs - 1)
        def _():
            dq_s[...] = dq_s[...] * ATT_SCALE
            out = pltpu.make_async_copy(
                dq_s, dq_hbm.at[:, pl.ds(pl.multiple_of(head * HEAD_SLOT, HEAD_SLOT), HEAD_SLOT)], sem)
            out.start()
            out.wait()

    grid_spec = pltpu.PrefetchScalarGridSpec(
        num_scalar_prefetch=3, grid=(HEADS, n_pairs),
        in_specs=[pl.BlockSpec((bq, HEAD_SLOT), lambda h, p, qt, kt, mt: (qt[p], h)),
                  pl.BlockSpec((bk, HEAD_SLOT), lambda h, p, qt, kt, mt: (kt[p], h)),
                  pl.BlockSpec((bk, V_HEAD), lambda h, p, qt, kt, mt: (kt[p], h)),
                  pl.BlockSpec((bq, V_HEAD), lambda h, p, qt, kt, mt: (qt[p], h)),
                  pl.BlockSpec((bq, V_HEAD), lambda h, p, qt, kt, mt: (qt[p], h)),
                  pl.BlockSpec((bq, V_HEAD), lambda h, p, qt, kt, mt: (qt[p], h))],
        out_specs=[pl.BlockSpec(memory_space=pl.ANY),
                   pl.BlockSpec((bk, HEAD_SLOT), lambda h, p, qt, kt, mt: (kt[p], h)),
                   pl.BlockSpec((bk, V_HEAD), lambda h, p, qt, kt, mt: (kt[p], h))],
        scratch_shapes=[pltpu.VMEM((S, HEAD_SLOT), F32), pltpu.VMEM((bk, HEAD_SLOT), F32),
                        pltpu.VMEM((bk, V_HEAD), F32), pltpu.SemaphoreType.DMA])
    return pl.pallas_call(
        body, name="attention_bwd", grid_spec=grid_spec,
        out_shape=[jax.ShapeDtypeStruct((S, HEADS * HEAD_SLOT), F32), jax.ShapeDtypeStruct((S, HEADS * HEAD_SLOT), F32),
                   jax.ShapeDtypeStruct((S, HEADS * V_HEAD), F32)],
        compiler_params=_params(("arbitrary", "arbitrary")),
    )(*tables, q, k, v, do, lse2, delta)


def _group_mats():
    def blockdiag(n, g):
        idx = np.arange(n) // g
        return jnp.asarray((idx[:, None] == idx[None, :]).astype(np.float32), dtype=BF16)
    return blockdiag(CONV_W, CONV_GROUP), blockdiag(HEADS * V_HEAD, V_HEAD)


def _mix_out(ya, o, gout, w_out, x, gate, ga, gb):
    S = x.shape[0]
    tm = _row_tile(S, 512)

    def body(ya_ref, o_ref, go_ref, w_ref, x_ref, g_ref, ga_ref, gb_ref, xo_ref, yn_ref, yo_ref):
        yav, ov = ya_ref[...], o_ref[...]
        ra = lax.rsqrt(_gsum(yav * yav, ga_ref[...]) * (1.0 / CONV_GROUP) + EPS)
        rb = lax.rsqrt(_gsum(ov * ov, gb_ref[...]) * (1.0 / V_HEAD) + EPS)
        na = ((yav * ra) * go_ref[:, 0:CONV_W]).astype(BF16)
        nb = ((ov * rb) * go_ref[:, CONV_W:]).astype(BF16)
        yn_ref[:, 0:CONV_W] = na
        yn_ref[:, CONV_W:] = nb
        yo = (jnp.dot(na, w_ref[0:CONV_W, :], preferred_element_type=F32)
              + jnp.dot(nb, w_ref[CONV_W:, :], preferred_element_type=F32))
        xo_ref[...] = x_ref[...] + g_ref[...] * yo
        yo_ref[...] = yo.astype(BF16)

    row = pl.BlockSpec((tm, D), lambda i: (i, 0))
    half = pl.BlockSpec((tm, CONV_W), lambda i: (i, 0))
    vec = pl.BlockSpec((1, D), lambda i: (0, 0))
    sq = pl.BlockSpec((CONV_W, CONV_W), lambda i: (0, 0))
    return pl.pallas_call(
        body, name="mix_out", grid=(S // tm,),
        in_specs=[half, half, vec, pl.BlockSpec((D, D), lambda i: (0, 0)), row, vec, sq, sq],
        out_specs=[row, row, row],
        out_shape=[jax.ShapeDtypeStruct((S, D), F32), jax.ShapeDtypeStruct((S, D), BF16),
                   jax.ShapeDtypeStruct((S, D), BF16)],
        compiler_params=_params(("parallel",)),
    )(ya, o, gout, w_out, x, gate, ga, gb)


def _mix_out_bwd(dx, gate, yo, w_out, ya, o, gout, ga, gb, comm=None):
    S = dx.shape[0]
    tm = _row_tile(S, 512)
    n_i = S // tm

    def norm_bwd(v, dn, gain, gmat, inv_n):
        r = lax.rsqrt(_gsum(v * v, gmat) * inv_n + EPS)
        vh = v * r
        dy = dn * gain
        return r * (dy - vh * (_gsum(dy * vh, gmat) * inv_n)), dn * vh

    def body(dx_ref, g_ref, yo_ref, w_ref, ya_ref, o_ref, go_ref, ga_ref, gb_ref,
             dyo_ref, dya_ref, do_ref, dl_ref, p_ref):
        i = pl.program_id(0)
        dxv = dx_ref[...]
        dyo = (dxv * g_ref[...]).astype(BF16)
        dyo_ref[...] = dyo
        dyn = _dot_nt(dyo, w_ref[...])
        dya, dga = norm_bwd(ya_ref[...], dyn[:, 0:CONV_W], go_ref[:, 0:CONV_W], ga_ref[...], 1.0 / CONV_GROUP)
        ov = o_ref[...]
        do, dgb = norm_bwd(ov, dyn[:, CONV_W:], go_ref[:, CONV_W:], gb_ref[...], 1.0 / V_HEAD)
        dya_ref[...] = dya
        do_ref[...] = do.astype(BF16)
        dl_ref[...] = _gsum(do * ov, gb_ref[...], split=True)

        @pl.when(i == 0)
        def _():
            p_ref[...] = jnp.zeros_like(p_ref)

        p_ref[:, 0:D] += _rsum8(dxv * yo_ref[...].astype(F32))
        p_ref[:, D:D + CONV_W] += _rsum8(dga)
        p_ref[:, D + CONV_W:2 * D] += _rsum8(dgb)

        @pl.when(i == n_i - 1)
        def _():
            _all_rows(p_ref)

    row = pl.BlockSpec((tm, D), lambda i: (i, 0))
    half = pl.BlockSpec((tm, CONV_W), lambda i: (i, 0))
    vec = pl.BlockSpec((1, D), lambda i: (0, 0))
    sq = pl.BlockSpec((CONV_W, CONV_W), lambda i: (0, 0))
    return _hosted_call(
        body, name="mix_out_bwd", grid=(n_i,),
        in_specs=[row, vec, row, pl.BlockSpec((D, D), lambda i: (0, 0)), half, half, vec, sq, sq],
        out_specs=[row, half, half, half, pl.BlockSpec((8, 2 * D), lambda i: (0, 0))],
        out_shape=[jax.ShapeDtypeStruct((S, D), BF16), jax.ShapeDtypeStruct((S, CONV_W), F32),
                   jax.ShapeDtypeStruct((S, CONV_W), BF16), jax.ShapeDtypeStruct((S, CONV_W), F32),
                   jax.ShapeDtypeStruct((8, 2 * D), F32)],
        scratch_shapes=[], semantics=("arbitrary",), args=(dx, gate, yo, w_out, ya, o, gout, ga, gb), comm=comm)


MID_SUMS = 3 * CONV_W + Q_LORA + KV_LORA


def _mix_mid_bwd(z, dya, conv_w, gq, gkv, wuq, wukv, cs, sn, dq, dk, dv, comm=None):
    S = z.shape[0]
    tm = _row_tile(S, 256)
    n_i = S // tm
    hb = tm // 8
    last_blk = S // 8 - 1

    def latent_bwd(cv, dcn, gain):
        r = _rms(cv)
        ch = cv * r
        dy = dcn * gain
        return r * (dy - ch * jnp.mean(dy * ch, axis=-1, keepdims=True)), dcn * ch

    def body(z_ref, zp_ref, zn_ref, dya_ref, dyan_ref, cw_ref, gq_ref, gkv_ref, wuq_ref, wukv_ref, cs_ref, sn_ref,
             dq_ref, dk_ref, dv_ref, dz_ref, dqf_ref, dkvf_ref, p_ref):
        i = pl.program_id(0)
        xb, xc, xu = z_ref[:, 0:CONV_W], z_ref[:, CONV_W:2 * CONV_W], z_ref[:, 2 * CONV_W:3 * CONV_W]
        u = xc * xu
        halo = jnp.where(i > 0, zp_ref[:, CONV_W:2 * CONV_W] * zp_ref[:, 2 * CONV_W:3 * CONV_W], 0.0)
        rows = lax.broadcasted_iota(jnp.int32, (tm, CONV_W), 0)
        u1, u2 = _conv_taps(u, halo, rows)
        w0, w1, w2 = cw_ref[0:1, :], cw_ref[1:2, :], cw_ref[2:3, :]
        y = w0 * u2 + w1 * u1 + w2 * u
        dyav = dya_ref[...]
        dy = dyav * xb
        nxt = jnp.where(i < n_i - 1, dyan_ref[...] * zn_ref[:, 0:CONV_W], 0.0)
        dy1 = jnp.where(rows == tm - 1, nxt[0:1, :], pltpu.roll(dy, tm - 1, 0))
        dy2 = jnp.where(rows == tm - 1, nxt[1:2, :], jnp.where(rows == tm - 2, nxt[0:1, :], pltpu.roll(dy, tm - 2, 0)))
        du = w2 * dy + w1 * dy1 + w0 * dy2
        dz_ref[:, 0:CONV_W] = (dyav * y).astype(BF16)
        dz_ref[:, CONV_W:2 * CONV_W] = (du * xu).astype(BF16)
        dz_ref[:, 2 * CONV_W:3 * CONV_W] = (du * xc).astype(BF16)

        lane = lax.broadcasted_iota(jnp.int32, (tm, 128), 1)
        cs_v, sn_v = cs_ref[...], sn_ref[...]
        dkr = jnp.zeros((tm, 128), F32)
        for h in range(HEADS):
            o = h * HEAD_SLOT
            dqf_ref[:, o:o + 128] = dq_ref[:, o:o + 128].astype(BF16)
            dqf_ref[:, o + 128:o + 256] = _rope_t(dq_ref[:, o + 128:o + 256], cs_v, sn_v, lane).astype(BF16)
            dkvf_ref[:, h * 128:(h + 1) * 128] = dk_ref[:, o:o + 128].astype(BF16)
            dkr = dkr + dk_ref[:, o + 128:o + 256]
        dkvf_ref[:, HEADS * 128:] = dv_ref[...].astype(BF16)

        c0 = 3 * CONV_W
        dcqn = jnp.dot(dqf_ref[...], wuq_ref[...], preferred_element_type=F32)
        dcq, dgq = latent_bwd(z_ref[:, c0:c0 + Q_LORA], dcqn, gq_ref[...])
        dz_ref[:, c0:c0 + Q_LORA] = dcq.astype(BF16)
        c1 = c0 + Q_LORA
        dckvn = _dot_nt(dkvf_ref[...], wukv_ref[...])
        dckv, dgkv = latent_bwd(z_ref[:, c1:c1 + KV_LORA], dckvn, gkv_ref[...])
        dz_ref[:, c1:c1 + KV_LORA] = dckv.astype(BF16)
        dz_ref[:, c1 + KV_LORA:Z_COLS] = _rope_t(dkr, cs_v, sn_v, lane).astype(BF16)

        @pl.when(i == 0)
        def _():
            p_ref[...] = jnp.zeros_like(p_ref)

        p_ref[:, 0:CONV_W] += _rsum8(dy * u2)
        p_ref[:, CONV_W:2 * CONV_W] += _rsum8(dy * u1)
        p_ref[:, 2 * CONV_W:3 * CONV_W] += _rsum8(dy * u)
        p_ref[:, c0:c0 + Q_LORA] += _rsum8(dgq)
        p_ref[:, c1:c1 + KV_LORA] += _rsum8(dgkv)

        @pl.when(i == n_i - 1)
        def _():
            _all_rows(p_ref)

    def rows_of(n):
        return pl.BlockSpec((tm, n), lambda i: (i, 0))

    def whole(shape):
        return pl.BlockSpec(shape, lambda i: (0, 0))

    def prev8(n):
        return pl.BlockSpec((8, n), lambda i: (jnp.maximum(i * hb - 1, 0), 0))

    def next8(n):
        return pl.BlockSpec((8, n), lambda i: (jnp.minimum((i + 1) * hb, last_blk), 0))

    return _hosted_call(
        body, name="mix_mid_bwd", grid=(n_i,),
        in_specs=[rows_of(Z_COLS), prev8(Z_COLS), next8(Z_COLS), rows_of(CONV_W), next8(CONV_W),
                  whole((8, CONV_W)), whole((1, Q_LORA)), whole((1, KV_LORA)),
                  whole((HEADS * HEAD_SLOT, Q_LORA)), whole((KV_LORA, 2 * HEADS * 128)),
                  rows_of(128), rows_of(128),
                  rows_of(HEADS * HEAD_SLOT), rows_of(HEADS * HEAD_SLOT), rows_of(HEADS * V_HEAD)],
        out_specs=[rows_of(Z_COLS), rows_of(HEADS * HEAD_SLOT), rows_of(2 * HEADS * 128), whole((8, MID_SUMS))],
        out_shape=[jax.ShapeDtypeStruct((S, Z_COLS), BF16), jax.ShapeDtypeStruct((S, HEADS * HEAD_SLOT), BF16),
                   jax.ShapeDtypeStruct((S, 2 * HEADS * 128), BF16), jax.ShapeDtypeStruct((8, MID_SUMS), F32)],
        scratch_shapes=[], semantics=("arbitrary",),
        args=(z, z, z, dya, dya, conv_w, gq, gkv, wuq, wukv, cs, sn, dq, dk, dv), comm=comm)


ADA_Q = N_MOD * D // N_CHIPS
ADA_TN = 768


def _ada_forward(c_all, ada_w_q, ada_b_q):
    def body(c_ref, w_ref, b_ref, o_ref):
        cv = c_ref[...]
        sc = (cv * jax.nn.sigmoid(cv)).astype(BF16)
        o_ref[...] = jnp.dot(sc, w_ref[...].astype(BF16), preferred_element_type=F32) + b_ref[...]

    return pl.pallas_call(
        body, name="ada_forward", grid=(ADA_Q // ADA_TN,),
        in_specs=[pl.BlockSpec((16, D), lambda j: (0, 0)), pl.BlockSpec((D, ADA_TN), lambda j: (0, j)),
                  pl.BlockSpec((1, ADA_TN), lambda j: (0, j))],
        out_specs=pl.BlockSpec((16, ADA_TN), lambda j: (0, j)),
        out_shape=jax.ShapeDtypeStruct((16, ADA_Q), F32),
        compiler_params=_params(("parallel",)),
    )(c_all, ada_w_q, ada_b_q)


def _ada_wgrad(c_all, dmod_q):
    def body(c_ref, d_ref, o_ref):
        cv = c_ref[...]
        sc = (cv * jax.nn.sigmoid(cv)).astype(BF16)
        o_ref[...] = lax.dot_general(sc, d_ref[...].astype(BF16), (((0,), (0,)), ((), ())),
                                     preferred_element_type=F32)

    return pl.pallas_call(
        body, name="ada_wgrad", grid=(ADA_Q // ADA_TN,),
        in_specs=[pl.BlockSpec((16, D), lambda j: (0, 0)), pl.BlockSpec((16, ADA_TN), lambda j: (0, j))],
        out_specs=pl.BlockSpec((D, ADA_TN), lambda j: (0, j)),
        out_shape=jax.ShapeDtypeStruct((D, ADA_Q), F32),
        compiler_params=_params(("parallel",)),
    )(c_all, dmod_q)


def _sum_devices(parts):
    n = parts.shape[1]

    def body(p_ref, o_ref):
        o_ref[...] = jnp.broadcast_to(jnp.sum(p_ref[...], axis=0, keepdims=True), o_ref.shape)

    return pl.pallas_call(
        body, name="sum_devices",
        in_specs=[pl.BlockSpec((N_DEV, n), lambda: (0, 0))], out_specs=pl.BlockSpec((N_DEV, n), lambda: (0, 0)),
        out_shape=jax.ShapeDtypeStruct((N_DEV, n), F32),
    )(parts)


def _adamw(w, g, m, v, *, name):
    _, rows, cols = w.shape
    tr = _row_tile(rows, 256)

    def body(w_ref, g_ref, m_ref, v_ref, go_ref, d_ref, mo_ref, vo_ref):
        gv = g_ref[...]
        mn = B1 * m_ref[0] + (1.0 - B1) * gv
        vn = B2 * v_ref[0] + (1.0 - B2) * (gv * gv)
        m_hat = mn / (1.0 - B1 ** STEP)
        v_hat = vn / (1.0 - B2 ** STEP)
        go_ref[0] = gv
        d_ref[0] = -LR * (m_hat / (jnp.sqrt(v_hat) + AEPS) + WD * w_ref[0])
        mo_ref[0] = mn
        vo_ref[0] = vn

    blk = pl.BlockSpec((1, tr, cols), lambda i: (0, i, 0))
    return pl.pallas_call(
        body, name=name, grid=(rows // tr,),
        in_specs=[blk, pl.BlockSpec((tr, cols), lambda i: (i, 0)), blk, blk], out_specs=[blk] * 4,
        out_shape=[jax.ShapeDtypeStruct((1, rows, cols), F32)] * 4,
        compiler_params=_params(("parallel",)),
    )(w, g, m, v)


def _small_allgather(v, *, name):
    m, n = v.shape

    def body(x_ref, out_ref, send_sems, recv_sems, local_sem):
        x, y, c = _place()
        me, sibling = (x, y, c), (x, y, 1 - c)
        chips = [(1 - x, y), (x, 1 - y), (1 - x, 1 - y)]

        def rows(px, py, pc):
            return out_ref.at[pl.ds((4 * px + 2 * py + pc) * m, m), :]

        def copy(k, block, to, src=None):
            return pltpu.make_async_remote_copy(
                src_ref=rows(*block) if src is None else src, dst_ref=rows(*block),
                send_sem=send_sems.at[k], recv_sem=recv_sems.at[k], device_id=to, device_id_type=MESH)

        mine = pltpu.make_async_copy(x_ref, rows(*me), local_sem)
        mine.start()
        first = [copy(0, me, sibling, src=x_ref)]
        first += [copy(1 + j, me, (*chip, c), src=x_ref) for j, chip in enumerate(chips)]
        for cp in first:
            cp.start()
        passed = [copy(4 + j, (*chip, c), sibling) for j, chip in enumerate(chips)]
        for j, chip in enumerate(chips):
            copy(1 + j, (*chip, c), me).wait_recv()
            passed[j].start()
        copy(0, sibling, me).wait_recv()
        for j, chip in enumerate(chips):
            copy(4 + j, (*chip, 1 - c), me).wait_recv()
        for cp in first + passed:
            cp.wait_send()
        mine.wait()

    return pl.pallas_call(
        body, name=name,
        out_shape=jax.ShapeDtypeStruct((N_DEV * m, n), v.dtype),
        in_specs=[pl.BlockSpec(memory_space=pltpu.VMEM)], out_specs=pl.BlockSpec(memory_space=pltpu.VMEM),
        scratch_shapes=[pltpu.SemaphoreType.DMA((7,)), pltpu.SemaphoreType.DMA((7,)), pltpu.SemaphoreType.DMA],
    )(v)


ADD_BLOCKS = 2


def _pair_add(place, gs, ts, *, name):
    n_a = len(gs)

    def body(pl_ref, *refs):
        g_refs, t_refs = refs[:n_a], refs[n_a:2 * n_a]
        pf_refs, pb_refs = refs[2 * n_a:3 * n_a], refs[3 * n_a:]
        for g_ref, t_ref, pf_ref, pb_ref in zip(g_refs, t_refs, pf_refs, pb_refs):
            s = g_ref[...] + t_ref[...]
            pf_ref[...] = s
            pb_ref[...] = s.astype(BF16)

    def blk(t, own_half):
        tr = t.shape[1] // ADD_BLOCKS
        if own_half:
            return pl.BlockSpec((1, tr, t.shape[2]), lambda q, r, p: (q, p[0] * ADD_BLOCKS + r, 0))
        return pl.BlockSpec((1, tr, t.shape[2]), lambda q, r, p: (q, r, 0))

    grid_spec = pltpu.PrefetchScalarGridSpec(
        num_scalar_prefetch=1, grid=(N_CHIPS, ADD_BLOCKS),
        in_specs=[blk(t, True) for t in ts] + [blk(t, False) for t in ts],
        out_specs=[blk(t, False) for t in ts] * 2)
    res = pl.pallas_call(
        body, name=name, grid_spec=grid_spec,
        out_shape=[jax.ShapeDtypeStruct(t.shape, F32) for t in ts] + [jax.ShapeDtypeStruct(t.shape, BF16) for t in ts],
        compiler_params=_params(("parallel", "parallel")),
    )(place, *gs, *ts)
    return list(res[:n_a]), list(res[n_a:])


def _chip_add(place, pfs, ts, *, name):
    n_a = len(pfs)

    def body(pl_ref, *refs):
        pf_refs, t_refs, o_refs = refs[:n_a], refs[n_a:4 * n_a], refs[4 * n_a:]
        for i, (pf_ref, o_ref) in enumerate(zip(pf_refs, o_refs)):
            t1, t2, t3 = t_refs[3 * i:3 * i + 3]
            o_ref[...] = ((pf_ref[0] + t1[0].astype(F32)) + t2[0].astype(F32)) + t3[0].astype(F32)

    def slot(t, j):
        return pl.BlockSpec((1, t.shape[1] // ADD_BLOCKS, t.shape[2]), lambda r, p: (p[1] ^ j, r, 0))

    grid_spec = pltpu.PrefetchScalarGridSpec(
        num_scalar_prefetch=1, grid=(ADD_BLOCKS,),
        in_specs=[slot(t, 0) for t in pfs] + [slot(t, j) for t in ts for j in (1, 2, 3)],
        out_specs=[pl.BlockSpec((t.shape[1] // ADD_BLOCKS, t.shape[2]), lambda r, p: (r, 0)) for t in pfs])
    res = pl.pallas_call(
        body, name=name, grid_spec=grid_spec,
        out_shape=[jax.ShapeDtypeStruct(t.shape[1:], F32) for t in pfs],
        compiler_params=_params(("parallel",)),
    )(place, *pfs, *[t for t in ts for _ in range(3)])
    return list(res)


BULK = [("ffn1_w1", "colsT"), ("ffn1_w3", "colsT"), ("ffn1_w2", "rows"), ("w_in", "cols"), ("w_uq", "colsT"),
        ("w_ukv", "cols"), ("w_out", "rows"), ("ffn2_w1", "colsT"), ("ffn2_w3", "colsT"), ("ffn2_w2", "rows")]
KIND = dict(BULK)


def _group(*names):
    return [b for b in BULK if b[0] in names]


W_FIRST = _group("ffn1_w1", "ffn1_w3")
W_REST = [b for b in BULK if b not in W_FIRST]
W_MIX = _group("ffn1_w2", "w_in", "w_uq", "w_ukv", "w_out")
W_FFN2 = _group("ffn2_w1", "ffn2_w3", "ffn2_w2")
G_FFN2 = _group("ffn2_w1", "ffn2_w3", "ffn2_w2")
G_MIX = _group("w_in", "w_uq", "w_ukv", "w_out")
G_FFN1 = _group("ffn1_w1", "ffn1_w3", "ffn1_w2")


def _gathered_weights(specs, shards, got, myq):
    out = {}
    for (name, kind), part in zip(specs, got):
        part = lax.dynamic_update_slice_in_dim(part, shards[name][None], myq, axis=0)
        out[name] = _full_weight(part, kind)
    return out


def _working_shard(w, kind):
    return jnp.swapaxes(w, 1, 2)[0] if kind == "colsT" else w[0]


def _full_weight(parts, kind):
    if kind == "cols":
        return jnp.transpose(parts, (1, 0, 2)).reshape(parts.shape[1], -1)
    return parts.reshape(-1, parts.shape[2])


def _quarters(g, kind):
    if kind == "cols":
        k, n = g.shape
        return jnp.transpose(g.reshape(k, N_CHIPS, n // N_CHIPS), (1, 0, 2))
    return g.reshape(N_CHIPS, g.shape[0] // N_CHIPS, g.shape[1])


def _pad_heads(w_uq_t):
    w = w_uq_t.reshape(HEADS, QK_NOPE + QK_ROPE, Q_LORA)
    return jnp.pad(w, ((0, 0), (0, HEAD_SLOT - QK_NOPE - QK_ROPE), (0, 0))).reshape(HEADS * HEAD_SLOT, Q_LORA)


def _unpad_heads(g):
    return g.reshape(HEADS, HEAD_SLOT, Q_LORA)[:, :QK_NOPE + QK_ROPE].reshape(HEADS * (QK_NOPE + QK_ROPE), Q_LORA)


def _split_kv(w_ukv):
    return jnp.transpose(w_ukv.reshape(KV_LORA, HEADS, 2, 128), (0, 2, 1, 3)).reshape(KV_LORA, 2 * HEADS * 128)


def _merge_kv(g):
    return jnp.transpose(g.reshape(KV_LORA, 2, HEADS, 128), (0, 2, 1, 3)).reshape(KV_LORA, 2 * HEADS * 128)


def _rope_tables(positions):
    inv_freq = ROPE_THETA ** (-jnp.arange(0, QK_ROPE, 2, dtype=F32) / QK_ROPE)
    ang = positions.astype(F32)[:, None] * inv_freq
    cos, sin, zero = jnp.cos(ang), jnp.sin(ang), jnp.zeros((positions.shape[0], 64), F32)
    return jnp.concatenate([cos, cos, zero], axis=1), jnp.concatenate([sin, sin, zero], axis=1)


def _assemble(place, specs, rhs, others):
    south = place[0] == 0
    return {b[0]: jnp.concatenate([jnp.where(south, rh, ot), jnp.where(south, ot, rh)], axis=0)
            for b, rh, ot in zip(specs, rhs, others)}


def _local_step(x, positions, target, mod, vec, conv_w, w_first, rest_shards, place):
    row = lambda k: mod[k:k + 1]
    sh1, sc1, g1, sh2, sc2, g2, sh3, sc3, g3 = [row(k) for k in range(N_MOD)]
    cs, sn = _rope_tables(positions)
    cw8 = jnp.pad(conv_w, ((0, 5), (0, 0)))
    ga, gb = _group_mats()
    dist = place is not None
    comm = lambda prog: prog if dist else None

    w = dict(w_first) if dist else {**rest_shards, **w_first}

    def gather(specs):
        return _Gather([rest_shards[b[0]] for b in specs]) if dist else None

    def arrived(specs, got):
        if dist:
            w.update(_gathered_weights(specs, rest_shards, got, place[1]))

    (h1, a1, b1, u1), got = _ffn_up(x, vec["norm_ffn1_g"], sh1, sc1, w["ffn1_w1"], w["ffn1_w3"],
                                    name="ffn1_up", comm=gather(W_MIX))
    arrived(W_MIX, got)
    w_in = jnp.pad(w["w_in"].T, ((0, Z_COLS - IN_COLS), (0, 0)))
    wuq = _pad_heads(w["w_uq"])
    wukv = _split_kv(w["w_ukv"])
    (x1, f1), _ = _ffn_down(u1, w["ffn1_w2"], x, g1, name="ffn1_down")
    (h2, z), _ = _mix_in(x1, vec["norm_mix_g"], sh2, sc2, w_in)
    (ya, q, k, v, cqn, ckvn), _ = _mix_mid(z, cw8, vec["q_norm_g"], vec["kv_norm_g"], wuq, wukv, cs, sn)
    (o, lse), got = _attention(q, k, v, comm=gather(W_FFN2))
    arrived(W_FFN2, got)
    x2, yn, yo = _mix_out(ya, o, vec["out_norm_g"], w["w_out"], x1, g2, ga, gb)
    (h3, a3, b3, u3), _ = _ffn_up(x2, vec["norm_ffn2_g"], sh3, sc3, w["ffn2_w1"], w["ffn2_w3"], name="ffn2_up")
    dx3, f3, dgfin, loss_blk = _ffn_down_loss(u3, w["ffn2_w2"], x2, g3, vec["final_norm_g"], target)

    grads, reduced = {}, {}

    def tn(a, b, tm, tn_, name, prog=None):
        if prog is None:
            return _tn_matmul(a, b, tm=tm, tn=tn_, name=name), None
        return _tn_matmul(a, b, tm=tm, tn=tn_, name=name, comm=prog)

    def slab_of(specs):
        return [_quarters(grads[n], kind) for n, kind in specs]

    (df3, da3, db3, dg3), _ = _ffn_bwd_du(dx3, g3, f3, w["ffn2_w2"], a3, b3, name="ffn2_bwd_du")
    grads["ffn2_w2"], _ = tn(u3, df3, FF // 2, D, "ffn2_dw2")
    grads["ffn2_w1"], _ = tn(da3, h3, FF // 2, D, "ffn2_dw1")
    grads["ffn2_w3"], _ = tn(db3, h3, FF // 2, D, "ffn2_dw3")
    (dx2, s3), _ = _dh_normbwd([(da3, w["ffn2_w1"]), (db3, w["ffn2_w3"])], x2, vec["norm_ffn2_g"], sc3, dx3,
                               name="ffn2_bwd_dh")

    p1 = slab_of(G_FFN2) if dist else None
    (dyo, dya, do, delta, s_out), t1 = _mix_out_bwd(dx2, g2, yo, w["w_out"], ya, o, vec["out_norm_g"], ga, gb,
                                                    comm=comm(_PairExchange(p1) if dist else None))
    grads["w_out"], _ = tn(yn, dyo, D, D, "dw_out")
    if dist:
        pf1, pb1 = _pair_add(place, p1, t1, name="ffn2g_pair_add")
    dq, dk, dv = _attention_bwd(q, k, v, do, lse, delta)
    (dz, dqf, dkvf, s_mid), t2 = _mix_mid_bwd(z, dya, cw8, vec["q_norm_g"], vec["kv_norm_g"], wuq, wukv, cs, sn,
                                              dq, dk, dv, comm=comm(_ChipExchange(pb1) if dist else None))
    g_uq, _ = tn(dqf, cqn, HEADS * HEAD_SLOT, Q_LORA, "dw_uq")
    g_ukv, _ = tn(ckvn, dkvf, KV_LORA, 2 * HEADS * 128, "dw_ukv")
    grads["w_uq"], grads["w_ukv"] = _unpad_heads(g_uq), _merge_kv(g_ukv)
    g_in, _ = tn(h2, dz, D, Z_COLS // 2, "dw_in")
    grads["w_in"] = g_in[:, :IN_COLS]
    (dx1, s2), _ = _dh_normbwd([(dz, w_in)], x1, vec["norm_mix_g"], sc2, dx2, name="mix_bwd_dh")

    p2 = slab_of(G_MIX) if dist else None
    rh_ffn2 = _chip_add(place, pf1, t2, name="ffn2g_chip_add") if dist else None
    (df1, da1, db1, dg1), got = _ffn_bwd_du(dx1, g1, f1, w["ffn1_w2"], a1, b1, name="ffn1_bwd_du",
                                            comm=comm(_Multi([_PairExchange(p2), _PairShare(rh_ffn2)]) if dist else None))
    if dist:
        t1 = got[:len(p2)]
        reduced.update(_assemble(place, G_FFN2, rh_ffn2, got[len(p2):]))
    dh_pairs = [(da1, w["ffn1_w1"]), (db1, w["ffn1_w3"])]
    if dist:
        pf2, pb2 = _pair_add(place, p2, t1, name="mixg_pair_add")
        grads["ffn1_w2"], t2 = tn(u1, df1, FF // 2, D, "ffn1_dw2", _ChipExchange(pb2))
        rh_mix = _chip_add(place, pf2, t2, name="mixg_chip_add")
        q_w2 = [_quarters(grads["ffn1_w2"], KIND["ffn1_w2"])]
        grads["ffn1_w1"], got = tn(da1, h1, FF // 2, D, "ffn1_dw1", _Multi([_PairShare(rh_mix), _PairExchange(q_w2)]))
        reduced.update(_assemble(place, G_MIX, rh_mix, got[:len(rh_mix)]))
        pf_w2, pb_w2 = _pair_add(place, q_w2, got[len(rh_mix):], name="ffn1w2_pair_add")
        q_w1 = [_quarters(grads["ffn1_w1"], KIND["ffn1_w1"])]
        grads["ffn1_w3"], got = tn(db1, h1, FF // 2, D, "ffn1_dw3", _Multi([_ChipExchange(pb_w2), _PairExchange(q_w1)]))
        t2_w2 = got[:1]
        pf_w1, pb_w1 = _pair_add(place, q_w1, got[1:], name="ffn1w1_pair_add")
        q_w3 = [_quarters(grads["ffn1_w3"], KIND["ffn1_w3"])]
        (dx0, s1), got = _dh_normbwd(dh_pairs, x, vec["norm_ffn1_g"], sc1, dx1, name="ffn1_bwd_dh",
                                     comm=_Multi([_ChipExchange(pb_w1), _PairExchange(q_w3)]))
        t2_w1 = got[:1]
        pf_w3, pb_w3 = _pair_add(place, q_w3, got[1:], name="ffn1w3_pair_add")
        t2_w3 = _run_comm(_ChipExchange(pb_w3), name="ffn1w3_chip_exchange")
        rh = _chip_add(place, pf_w1 + pf_w3 + pf_w2, t2_w1 + t2_w3 + t2_w2, name="ffn1g_chip_add")
        reduced.update(_assemble(place, G_FFN1, rh, _run_comm(_PairShare(rh), name="ffn1g_pair_share")))
    else:
        grads["ffn1_w2"], _ = tn(u1, df1, FF // 2, D, "ffn1_dw2")
        grads["ffn1_w1"], _ = tn(da1, h1, FF // 2, D, "ffn1_dw1")
        grads["ffn1_w3"], _ = tn(db1, h1, FF // 2, D, "ffn1_dw3")
        (dx0, s1), _ = _dh_normbwd(dh_pairs, x, vec["norm_ffn1_g"], sc1, dx1, name="ffn1_bwd_dh")
        reduced = grads

    def part(s, k):
        return s[0:1, k * D:(k + 1) * D]

    dmod = jnp.concatenate([part(s1, 1), part(s1, 0), dg1[0:1], part(s2, 1), part(s2, 0), part(s_out, 0),
                            part(s3, 1), part(s3, 0), dg3[0:1]], axis=1)
    small = {"norm_ffn1_g": part(s1, 2), "norm_mix_g": part(s2, 2), "out_norm_g": part(s_out, 1),
             "norm_ffn2_g": part(s3, 2), "final_norm_g": dgfin[0:1],
             "q_norm_g": s_mid[0:1, 3 * CONV_W:3 * CONV_W + Q_LORA],
             "kv_norm_g": s_mid[0:1, 3 * CONV_W + Q_LORA:MID_SUMS], "conv_w": s_mid[0:1, 0:3 * CONV_W]}
    return loss_blk, dx0, reduced, dmod, small


SMALL = [("norm_ffn1_g", D), ("norm_mix_g", D), ("out_norm_g", D), ("norm_ffn2_g", D), ("final_norm_g", D),
         ("q_norm_g", Q_LORA), ("kv_norm_g", KV_LORA), ("conv_w", 3 * CONV_W)]
WEIGHTS = ['ada_w', 'ada_b', 'norm_ffn1_g', 'ffn1_w1', 'ffn1_w3', 'ffn1_w2', 'norm_mix_g', 'w_in', 'conv_w',
           'q_norm_g', 'w_uq', 'kv_norm_g', 'w_ukv', 'out_norm_g', 'w_out', 'norm_ffn2_g', 'ffn2_w1', 'ffn2_w3',
           'ffn2_w2', 'final_norm_g']


def kernel(x, c, positions, ada_w, ada_b, norm_ffn1_g, ffn1_w1, ffn1_w3, ffn1_w2, norm_mix_g, w_in, conv_w, q_norm_g, w_uq, kv_norm_g, w_ukv, out_norm_g, w_out, norm_ffn2_g, ffn2_w1, ffn2_w3, ffn2_w2, final_norm_g, loss_target, m_ada_w, m_ada_b, m_norm_ffn1_g, m_ffn1_w1, m_ffn1_w3, m_ffn1_w2, m_norm_mix_g, m_w_in, m_conv_w, m_q_norm_g, m_w_uq, m_kv_norm_g, m_w_ukv, m_out_norm_g, m_w_out, m_norm_ffn2_g, m_ffn2_w1, m_ffn2_w3, m_ffn2_w2, m_final_norm_g, v_ada_w, v_ada_b, v_norm_ffn1_g, v_ffn1_w1, v_ffn1_w3, v_ffn1_w2, v_norm_mix_g, v_w_in, v_conv_w, v_q_norm_g, v_w_uq, v_kv_norm_g, v_w_ukv, v_out_norm_g, v_w_out, v_norm_ffn2_g, v_ffn2_w1, v_ffn2_w3, v_ffn2_w2, v_final_norm_g):
    args = dict(locals())
    wts = {n: args[n] for n in WEIGHTS}
    mom = {n: args["m_" + n] for n in WEIGHTS}
    var = {n: args["v_" + n] for n in WEIGHTS}
    ax, ay, ac = _place()
    myq = 2 * ax + ay
    me = 2 * myq + ac
    place = jnp.stack([ac, myq]).astype(jnp.int32)

    shards = {name: _working_shard(wts[name], kind).astype(BF16) for name, kind in BULK}
    first = _run_comm(_Gather([shards[b[0]] for b in W_FIRST]), name="gather_ffn1")
    w_first = _gathered_weights(W_FIRST, shards, first, myq)

    mine = jnp.concatenate([c, conv_w[0].reshape(1, 3 * CONV_W // N_CHIPS)], axis=1)
    seen = _small_allgather(jnp.pad(mine, ((0, 7), (0, 0))), name="gather_cond").reshape(N_DEV, 8, -1)[:, 0]
    c_all = jnp.pad(seen[:, :D], ((0, 8), (0, 0)))
    conv_full = jnp.transpose(seen[0::2, D:].reshape(N_CHIPS, 3, CONV_W // N_CHIPS), (1, 0, 2)).reshape(3, CONV_W)
    ada_b_q = lax.dynamic_slice_in_dim(ada_b, myq * ADA_Q, ADA_Q, axis=1)
    mod_q = _ada_forward(c_all, ada_w[0], ada_b_q)
    mod_all = _small_allgather(mod_q, name="gather_mod").reshape(N_DEV, 16, ADA_Q)
    mod_rows = jnp.transpose(mod_all[0::2, :N_DEV], (1, 0, 2)).reshape(N_DEV, N_MOD * D)
    mod = lax.dynamic_slice_in_dim(mod_rows, me, 1, axis=0).reshape(N_MOD, D)

    vec = {n: wts[n] for n in ("norm_ffn1_g", "norm_mix_g", "q_norm_g", "kv_norm_g", "out_norm_g", "norm_ffn2_g")}
    vec["final_norm_g"] = final_norm_g.reshape(1, D)
    loss_blk, grad_x, gq, dmod, small = _local_step(x[0], positions[0], loss_target[0], mod, vec, conv_full, w_first,
                                                    {b[0]: shards[b[0]] for b in W_REST}, place)
    loss = lax.psum(loss_blk[0, 0], ("x", "y", "c"))

    rows = jnp.concatenate([dmod] + [small[n] for n, _ in SMALL], axis=1)
    width = rows.shape[1]
    fold = -(-width // (8 * 128)) * 128
    rows = jnp.pad(rows, ((0, 0), (0, 8 * fold - width))).reshape(8, fold)
    every = _small_allgather(rows, name="gather_small").reshape(N_DEV, 8 * fold)[:, :width]
    total = _sum_devices(every)[0:1]
    dmod_q = lax.dynamic_slice_in_dim(every[:, :N_MOD * D], myq * ADA_Q, ADA_Q, axis=1)
    g = {name: gq[name] for name, *_ in BULK}
    g["ada_w"] = _ada_wgrad(c_all, jnp.pad(dmod_q, ((0, 8), (0, 0))))
    g["ada_b"] = total[:, :N_MOD * D]
    off = N_MOD * D
    for n, width in SMALL:
        g[n] = total[:, off:off + width]
        off += width
    g["conv_w"] = lax.dynamic_slice_in_dim(g["conv_w"].reshape(3, CONV_W), myq * (CONV_W // N_CHIPS),
                                           CONV_W // N_CHIPS, axis=1)
    g["final_norm_g"] = g["final_norm_g"].reshape(D)

    delta, new_m, new_v = {}, {}, {}
    for name in ["ada_w"] + [b[0] for b in BULK]:
        view = (lambda a: jnp.swapaxes(a, 1, 2)) if KIND.get(name) == "colsT" else (lambda a: a)
        g[name], delta[name], new_m[name], new_v[name] = [
            view(r) for r in _adamw(view(wts[name]), g[name], view(mom[name]), view(var[name]), name="adamw_" + name)]
    smalls = ["ada_b"] + [n for n, _ in SMALL]

    def packed(d):
        flat = jnp.concatenate([d[n].reshape(1, -1) for n in smalls], axis=1)
        return jnp.pad(flat.reshape(-1, D), ((0, 1), (0, 0)))

    res = _adamw(packed(wts)[None], packed(g), packed(mom)[None], packed(var)[None], name="adamw_small")[1:]
    off = 0
    for n in smalls:
        size = wts[n].size
        for d, r in zip((delta, new_m, new_v), res):
            d[n] = r.reshape(-1)[off:off + size].reshape(wts[n].shape)
        g[n] = g[n].reshape(wts[n].shape)
        off += size

    return (loss, grad_x[None], *[g[n] for n in WEIGHTS], *[delta[n] for n in WEIGHTS],
            *[new_m[n] for n in WEIGHTS], *[new_v[n] for n in WEIGHTS])
```

```python
import functools

import numpy as np
import jax
import jax.numpy as jnp
from jax import lax
from jax.experimental import pallas as pl
from jax.experimental.pallas import tpu as pltpu

F32 = jnp.float32
BF16 = jnp.bfloat16
MESH = pl.DeviceIdType.MESH

D = 1024
FF = 2816
CONV_W = 512
CONV_GROUP = 64
HEADS = 4
QK_NOPE = 128
QK_ROPE = 64
V_HEAD = 128
Q_LORA = 384
KV_LORA = 256
HEAD_SLOT = 256
IN_COLS = 3 * CONV_W + Q_LORA + KV_LORA + QK_ROPE
Z_COLS = 2304
EPS = 1e-6
ROPE_THETA = 10000.0
CHUNK = 64
ATT_SCALE = (QK_NOPE + QK_ROPE) ** -0.5
NEG = -1e30
EXP2_SCALE = ATT_SCALE * 1.4426950408889634
N_MOD = 9

LR, B1, B2, AEPS, WD, STEP = 0.001, 0.9, 0.999, 1e-08, 0.01, 10

N_CHIPS = 4
N_DEV = 8
VMEM_LIMIT = 56 << 20


def _params(sem, vmem=VMEM_LIMIT):
    return pltpu.CompilerParams(dimension_semantics=sem, vmem_limit_bytes=vmem)


def _rms(v):
    return lax.rsqrt(jnp.mean(v * v, axis=-1, keepdims=True) + EPS)


def _rsum8(v):
    t, n = v.shape
    return jnp.sum(v.reshape(t // 8, 8, n), axis=0)


def _all_rows(ref):
    ref[...] = jnp.broadcast_to(jnp.sum(ref[...], axis=0, keepdims=True), ref.shape)


def _gsum(v, gmat, split=False):
    hi = v.astype(BF16)
    out = jnp.dot(hi, gmat, preferred_element_type=F32)
    if split:
        out = out + jnp.dot((v - hi.astype(F32)).astype(BF16), gmat, preferred_element_type=F32)
    return out


def _dot_nt(a, b):
    return lax.dot_general(a, b, (((1,), (1,)), ((), ())), preferred_element_type=F32)


def _silu_parts(a):
    sg = jax.nn.sigmoid(a)
    return sg, a * sg


def _rope(xr, cs, sn, lane):
    rh = jnp.where(lane < 32, -pltpu.roll(xr, 96, 1), pltpu.roll(xr, 32, 1))
    return xr * cs + rh * sn


def _rope_t(g, cs, sn, lane):
    y = g * sn
    rt = jnp.where(lane < 32, pltpu.roll(y, 96, 1), jnp.where(lane < 64, -pltpu.roll(y, 32, 1), 0.0))
    return g * cs + rt


def _row_tile(rows, pref, mult=8):
    t = min(rows, pref) // mult * mult
    while rows % t:
        t -= mult
    return t


def _place():
    return lax.axis_index("x"), lax.axis_index("y"), lax.axis_index("c")


ANY = pl.BlockSpec(memory_space=pl.ANY)


def _hosted_call(body, *, name, grid, in_specs, out_specs, out_shape, scratch_shapes, semantics, args, comm=None,
                 prefetch=()):
    n_in, n_out, n_scr, n_pf = len(in_specs), len(out_specs), len(scratch_shapes), len(prefetch)

    def call(fn, in_specs, out_specs, out_shape, scratch_shapes, semantics, operands):
        spec = pltpu.PrefetchScalarGridSpec(num_scalar_prefetch=n_pf, grid=grid, in_specs=list(in_specs),
                                            out_specs=list(out_specs), scratch_shapes=list(scratch_shapes))
        return pl.pallas_call(fn, name=name, grid_spec=spec, out_shape=list(out_shape),
                              compiler_params=_params(semantics))(*prefetch, *operands)

    if comm is None:
        return list(call(body, in_specs, out_specs, out_shape, scratch_shapes, semantics, args)), []
    n_ci, n_co = len(comm.inputs), len(comm.out_shapes)
    total = int(np.prod(grid))

    def hosted(*refs):
        tables, refs = refs[:n_pf], refs[n_pf:]
        ins, refs = refs[:n_in], refs[n_in:]
        cins, refs = refs[:n_ci], refs[n_ci:]
        outs, refs = refs[:n_out], refs[n_out:]
        couts, refs = refs[:n_co], refs[n_co:]
        scratch, sems = refs[:n_scr], refs[n_scr]
        step = pl.program_id(0)
        for ax in range(1, len(grid)):
            step = step * grid[ax] + pl.program_id(ax)

        @pl.when(step == 0)
        def _():
            comm.start(cins, couts, sems)

        body(*tables, *ins, *outs, *scratch)

        @pl.when(step == total - 1)
        def _():
            comm.finish(cins, couts, sems)

    res = call(hosted, list(in_specs) + [ANY] * n_ci, list(out_specs) + [ANY] * n_co,
               list(out_shape) + list(comm.out_shapes),
               list(scratch_shapes) + [pltpu.SemaphoreType.DMA((comm.n_sems,))],
               ("arbitrary",) * len(grid), (*args, *comm.inputs))
    return list(res[:n_out]), list(res[n_out:])


def _run_comm(comm, *, name):
    n_ci = len(comm.inputs)

    def body(*refs):
        cins, couts, sems = refs[:n_ci], refs[n_ci:-1], refs[-1]
        comm.start(cins, couts, sems)
        comm.finish(cins, couts, sems)

    return list(pl.pallas_call(
        body, name=name, out_shape=list(comm.out_shapes), in_specs=[ANY] * n_ci,
        out_specs=[ANY] * len(comm.out_shapes), scratch_shapes=[pltpu.SemaphoreType.DMA((comm.n_sems,))],
    )(*comm.inputs))


class _Gather:
    def __init__(self, slabs):
        self.inputs = list(slabs)
        self.out_shapes = [jax.ShapeDtypeStruct((N_CHIPS,) + s.shape, s.dtype) for s in slabs]
        self.n_sems = 12 * len(slabs)

    @staticmethod
    def _copy(out, sems, base, k, chip, hc, to, src=None):
        H = out.shape[1] // 2
        half = out.at[2 * chip[0] + chip[1], pl.ds(hc * H, H), :]
        return pltpu.make_async_remote_copy(
            src_ref=half if src is None else src, dst_ref=half, send_sem=sems.at[base + k],
            recv_sem=sems.at[base + 6 + k], device_id=to, device_id_type=MESH)

    def _firsts(self, src, out, sems, base):
        x, y, c = _place()
        H = src.shape[0] // 2
        chips = [(1 - x, y), (x, 1 - y), (1 - x, 1 - y)]
        return [self._copy(out, sems, base, j, (x, y), c, (*chip, c), src=src.at[pl.ds(c * H, H), :])
                for j, chip in enumerate(chips)]

    def start(self, ins, outs, sems):
        for i, (src, out) in enumerate(zip(ins, outs)):
            for cp in self._firsts(src, out, sems, 12 * i):
                cp.start()

    def finish(self, ins, outs, sems):
        x, y, c = _place()
        chips = [(1 - x, y), (x, 1 - y), (1 - x, 1 - y)]
        passed = []
        for i, out in enumerate(outs):
            for j, chip in enumerate(chips):
                self._copy(out, sems, 12 * i, j, chip, c, (x, y, c)).wait_recv()
                cp = self._copy(out, sems, 12 * i, 3 + j, chip, c, (x, y, 1 - c))
                cp.start()
                passed.append(cp)
        for i, out in enumerate(outs):
            for j, chip in enumerate(chips):
                self._copy(out, sems, 12 * i, 3 + j, chip, 1 - c, (x, y, c)).wait_recv()
        for cp in passed:
            cp.wait_send()
        for i, (src, out) in enumerate(zip(ins, outs)):
            for cp in self._firsts(src, out, sems, 12 * i):
                cp.wait_send()


class _PairExchange:
    def __init__(self, arrays):
        self.inputs = list(arrays)
        self.out_shapes = [jax.ShapeDtypeStruct((N_CHIPS, a.shape[1] // 2, a.shape[2]), a.dtype) for a in arrays]
        self.n_sems = 2 * len(arrays)

    def _copies(self, ins, outs, sems):
        x, y, c = _place()
        return [pltpu.make_async_remote_copy(
            src_ref=g.at[:, pl.ds((1 - c) * t.shape[1], t.shape[1]), :], dst_ref=t, send_sem=sems.at[2 * i],
            recv_sem=sems.at[2 * i + 1], device_id=(x, y, 1 - c), device_id_type=MESH)
            for i, (g, t) in enumerate(zip(ins, outs))]

    def start(self, ins, outs, sems):
        for cp in self._copies(ins, outs, sems):
            cp.start()

    def finish(self, ins, outs, sems):
        for cp in self._copies(ins, outs, sems):
            cp.wait()


class _ChipExchange:
    def __init__(self, arrays):
        self.inputs = list(arrays)
        self.out_shapes = [jax.ShapeDtypeStruct(a.shape, a.dtype) for a in arrays]
        self.n_sems = 6 * len(arrays)

    def _copies(self, p, t, sems, base):
        x, y, c = _place()
        myq = 2 * x + y
        chips = [(1 - x, y), (x, 1 - y), (1 - x, 1 - y)]
        sends = [pltpu.make_async_remote_copy(
            src_ref=p.at[2 * chip[0] + chip[1]], dst_ref=t.at[myq], send_sem=sems.at[base + j],
            recv_sem=sems.at[base + 3 + j], device_id=(*chip, c), device_id_type=MESH) for j, chip in enumerate(chips)]
        lands = [pltpu.make_async_remote_copy(
            src_ref=t.at[2 * chip[0] + chip[1]], dst_ref=t.at[2 * chip[0] + chip[1]], send_sem=sems.at[base + j],
            recv_sem=sems.at[base + 3 + j], device_id=(*chip, c), device_id_type=MESH) for j, chip in enumerate(chips)]
        return sends, lands

    def start(self, ins, outs, sems):
        for i, (p, t) in enumerate(zip(ins, outs)):
            for cp in self._copies(p, t, sems, 6 * i)[0]:
                cp.start()

    def finish(self, ins, outs, sems):
        for i, (p, t) in enumerate(zip(ins, outs)):
            sends, lands = self._copies(p, t, sems, 6 * i)
            for cp in lands:
                cp.wait_recv()
            for cp in sends:
                cp.wait_send()


class _SemView:
    def __init__(self, sems, base):
        self._sems, self._base = sems, base

    @property
    def at(self):
        return self

    def __getitem__(self, k):
        return self._sems.at[self._base + k]


class _Multi:
    def __init__(self, progs):
        self.progs = list(progs)
        self.inputs = [a for p in self.progs for a in p.inputs]
        self.out_shapes = [s for p in self.progs for s in p.out_shapes]
        self.n_sems = sum(p.n_sems for p in self.progs)

    def _each(self, ins, outs, sems):
        i = o = s = 0
        for p in self.progs:
            ni, no = len(p.inputs), len(p.out_shapes)
            yield p, ins[i:i + ni], outs[o:o + no], _SemView(sems, s)
            i, o, s = i + ni, o + no, s + p.n_sems

    def start(self, ins, outs, sems):
        for p, a, b, c in self._each(ins, outs, sems):
            p.start(a, b, c)

    def finish(self, ins, outs, sems):
        for p, a, b, c in self._each(ins, outs, sems):
            p.finish(a, b, c)


class _PairShare:
    def __init__(self, arrays):
        self.inputs = list(arrays)
        self.out_shapes = [jax.ShapeDtypeStruct(a.shape, a.dtype) for a in arrays]
        self.n_sems = 2 * len(arrays)

    def _copies(self, ins, outs, sems):
        x, y, c = _place()
        return [pltpu.make_async_remote_copy(
            src_ref=r, dst_ref=o, send_sem=sems.at[2 * i], recv_sem=sems.at[2 * i + 1],
            device_id=(x, y, 1 - c), device_id_type=MESH) for i, (r, o) in enumerate(zip(ins, outs))]

    def start(self, ins, outs, sems):
        for cp in self._copies(ins, outs, sems):
            cp.start()

    def finish(self, ins, outs, sems):
        for cp in self._copies(ins, outs, sems):
            cp.wait()


def _ffn_up(x, ng, sh, sc, w1, w3, *, name, comm=None):
    S = x.shape[0]
    tm, tn = _row_tile(S, 512), FF // 2

    def body(x_ref, g_ref, sh_ref, sc_ref, w1_ref, w3_ref, h_ref, a_ref, b_ref, u_ref, hs):
        @pl.when(pl.program_id(1) == 0)
        def _():
            xv = x_ref[...]
            h = ((xv * _rms(xv)) * g_ref[...]) * (1.0 + sc_ref[...]) + sh_ref[...]
            hb = h.astype(BF16)
            hs[...] = hb
            h_ref[...] = hb

        h = hs[...]
        cols = pl.ds(pl.multiple_of(pl.program_id(1) * tn, tn), tn)
        a = _dot_nt(h, w1_ref[cols, :])
        b = _dot_nt(h, w3_ref[cols, :])
        _, sa = _silu_parts(a)
        a_ref[...] = a.astype(BF16)
        b_ref[...] = b.astype(BF16)
        u_ref[...] = (sa * b).astype(BF16)

    row = pl.BlockSpec((tm, D), lambda i, j: (i, 0))
    vec = pl.BlockSpec((1, D), lambda i, j: (0, 0))
    wsp = pl.BlockSpec((FF, D), lambda i, j: (0, 0))
    osp = pl.BlockSpec((tm, tn), lambda i, j: (i, j))
    return _hosted_call(
        body, name=name, grid=(S // tm, FF // tn),
        in_specs=[row, vec, vec, vec, wsp, wsp],
        out_specs=[row, osp, osp, osp],
        out_shape=[jax.ShapeDtypeStruct((S, D), BF16)] + [jax.ShapeDtypeStruct((S, FF), BF16)] * 3,
        scratch_shapes=[pltpu.VMEM((tm, D), BF16)],
        semantics=("parallel", "arbitrary"), args=(x, ng, sh, sc, w1, w3), comm=comm)


def _ffn_down(u, w2, x, gate, *, name, comm=None):
    S = x.shape[0]
    tm = _row_tile(S, 512)

    def body(u_ref, w2_ref, x_ref, g_ref, xo_ref, f_ref):
        f = jnp.dot(u_ref[...], w2_ref[...], preferred_element_type=F32)
        xo_ref[...] = x_ref[...] + (0.5 * g_ref[...]) * f
        f_ref[...] = f.astype(BF16)

    return _hosted_call(
        body, name=name, grid=(S // tm,),
        in_specs=[pl.BlockSpec((tm, FF), lambda i: (i, 0)), pl.BlockSpec((FF, D), lambda i: (0, 0)),
                  pl.BlockSpec((tm, D), lambda i: (i, 0)), pl.BlockSpec((1, D), lambda i: (0, 0))],
        out_specs=[pl.BlockSpec((tm, D), lambda i: (i, 0))] * 2,
        out_shape=[jax.ShapeDtypeStruct((S, D), F32), jax.ShapeDtypeStruct((S, D), BF16)],
        scratch_shapes=[], semantics=("parallel",), args=(u, w2, x, gate), comm=comm)


def _ffn_bwd_du(dx, gate, f, w2, a, b, *, name, comm=None):
    S = dx.shape[0]
    tm, tn = _row_tile(S, 512), FF // 2
    n_i = S // tm

    def body(dx_ref, g_ref, f_ref, w_ref, a_ref, b_ref, df_ref, da_ref, db_ref, dg_ref, dfs):
        i, j = pl.program_id(0), pl.program_id(1)

        @pl.when((i == 0) & (j == 0))
        def _():
            dg_ref[...] = jnp.zeros_like(dg_ref)

        @pl.when(j == 0)
        def _():
            dxv = dx_ref[...]
            dfb = (dxv * (0.5 * g_ref[...])).astype(BF16)
            dfs[...] = dfb
            df_ref[...] = dfb
            dg_ref[...] += _rsum8(dxv * (0.5 * f_ref[...].astype(F32)))

        du = _dot_nt(dfs[...], w_ref[pl.ds(pl.multiple_of(j * tn, tn), tn), :])
        av = a_ref[...].astype(F32)
        sg, sa = _silu_parts(av)
        da_ref[...] = (du * b_ref[...].astype(F32) * (sg * (1.0 + av * (1.0 - sg)))).astype(BF16)
        db_ref[...] = (du * sa).astype(BF16)

        @pl.when((i == n_i - 1) & (j == FF // tn - 1))
        def _():
            _all_rows(dg_ref)

    row = pl.BlockSpec((tm, D), lambda i, j: (i, 0))
    blk = pl.BlockSpec((tm, tn), lambda i, j: (i, j))
    return _hosted_call(
        body, name=name, grid=(n_i, FF // tn),
        in_specs=[row, pl.BlockSpec((1, D), lambda i, j: (0, 0)), row,
                  pl.BlockSpec((FF, D), lambda i, j: (0, 0)), blk, blk],
        out_specs=[row, blk, blk, pl.BlockSpec((8, D), lambda i, j: (0, 0))],
        out_shape=[jax.ShapeDtypeStruct((S, D), BF16), jax.ShapeDtypeStruct((S, FF), BF16),
                   jax.ShapeDtypeStruct((S, FF), BF16), jax.ShapeDtypeStruct((8, D), F32)],
        scratch_shapes=[pltpu.VMEM((tm, D), BF16)],
        semantics=("arbitrary", "arbitrary"), args=(dx, gate, f, w2, a, b), comm=comm)


def _tn_matmul(a, b, *, tm, tn, name, comm=None):
    S, M = a.shape
    N = b.shape[1]
    ts = _row_tile(S, 2048)
    ns = S // ts

    def body(a_ref, b_ref, o_ref):
        s = pl.program_id(2)
        p = lax.dot_general(a_ref[...], b_ref[...], (((0,), (0,)), ((), ())), preferred_element_type=F32)

        @pl.when(s == 0)
        def _():
            o_ref[...] = p

        @pl.when(s > 0)
        def _():
            o_ref[...] += p

    (out,), couts = _hosted_call(
        body, name=name, grid=(M // tm, N // tn, ns),
        in_specs=[pl.BlockSpec((ts, tm), lambda i, j, s: (s, i)), pl.BlockSpec((ts, tn), lambda i, j, s: (s, j))],
        out_specs=[pl.BlockSpec((tm, tn), lambda i, j, s: (i, j))],
        out_shape=[jax.ShapeDtypeStruct((M, N), F32)],
        scratch_shapes=[], semantics=("parallel", "parallel", "arbitrary"), args=(a, b), comm=comm)
    return out if comm is None else (out, couts)


def _dh_normbwd(pairs, x, ng, sc, dx_next, *, name, comm=None):
    S = x.shape[0]
    n_p = len(pairs)
    tm = _row_tile(S, 512 if n_p == 1 else 256)
    n_i = S // tm

    def body(*refs):
        a_refs, w_refs = refs[:n_p], refs[n_p:2 * n_p]
        x_ref, g_ref, sc_ref, dxn_ref, dx_ref, p_ref = refs[2 * n_p:]
        i = pl.program_id(0)
        dh = jnp.dot(a_refs[0][...], w_refs[0][...], preferred_element_type=F32)
        for k in range(1, n_p):
            dh = dh + jnp.dot(a_refs[k][...], w_refs[k][...], preferred_element_type=F32)
        xv = x_ref[...]
        r = _rms(xv)
        xh = xv * r
        g = g_ref[...]
        dn = dh * (1.0 + sc_ref[...])
        dy = dn * g
        dx_ref[...] = dxn_ref[...] + r * (dy - xh * jnp.mean(dy * xh, axis=-1, keepdims=True))

        @pl.when(i == 0)
        def _():
            p_ref[...] = jnp.zeros_like(p_ref)

        p_ref[:, 0:D] += _rsum8(dh * (xh * g))
        p_ref[:, D:2 * D] += _rsum8(dh)
        p_ref[:, 2 * D:3 * D] += _rsum8(dn * xh)

        @pl.when(i == n_i - 1)
        def _():
            _all_rows(p_ref)

    row = pl.BlockSpec((tm, D), lambda i: (i, 0))
    vec = pl.BlockSpec((1, D), lambda i: (0, 0))
    in_specs = ([pl.BlockSpec((tm, a.shape[1]), lambda i: (i, 0)) for a, _ in pairs]
                + [pl.BlockSpec(w.shape, lambda i: (0, 0)) for _, w in pairs] + [row, vec, vec, row])
    return _hosted_call(
        body, name=name, grid=(n_i,), in_specs=in_specs,
        out_specs=[row, pl.BlockSpec((8, 3 * D), lambda i: (0, 0))],
        out_shape=[jax.ShapeDtypeStruct((S, D), F32), jax.ShapeDtypeStruct((8, 3 * D), F32)],
        scratch_shapes=[], semantics=("arbitrary",),
        args=(*[a for a, _ in pairs], *[w for _, w in pairs], x, ng, sc, dx_next), comm=comm)


def _ffn_down_loss(u, w2, x, gate, gfin, tgt):
    S = x.shape[0]
    tm = _row_tile(S, 512)
    n_i = S // tm

    def body(u_ref, w2_ref, x_ref, gt_ref, g_ref, t_ref, dx_ref, f_ref, dg_ref, loss_ref, lacc):
        i = pl.program_id(0)
        f = jnp.dot(u_ref[...], w2_ref[...], preferred_element_type=F32)
        f_ref[...] = f.astype(BF16)
        xv = x_ref[...] + (0.5 * gt_ref[...]) * f
        r = _rms(xv)
        xh = xv * r
        g = g_ref[...]
        e = xh * g - t_ref[...]
        dout = e * (1.0 / D)
        dy = dout * g
        dx_ref[...] = r * (dy - xh * jnp.mean(dy * xh, axis=-1, keepdims=True))

        @pl.when(i == 0)
        def _():
            dg_ref[...] = jnp.zeros_like(dg_ref)
            lacc[...] = jnp.zeros_like(lacc)

        dg_ref[...] += _rsum8(dout * xh)
        lacc[...] += _rsum8(e * e)

        @pl.when(i == n_i - 1)
        def _():
            _all_rows(dg_ref)
            tot = jnp.sum(jnp.sum(lacc[...], axis=0, keepdims=True), axis=1, keepdims=True)
            loss_ref[...] = jnp.broadcast_to(tot * (0.5 / D), loss_ref.shape)

    row = pl.BlockSpec((tm, D), lambda i: (i, 0))
    vec = pl.BlockSpec((1, D), lambda i: (0, 0))
    return pl.pallas_call(
        body, name="ffn2_down_loss", grid=(n_i,),
        in_specs=[pl.BlockSpec((tm, FF), lambda i: (i, 0)), pl.BlockSpec((FF, D), lambda i: (0, 0)), row, vec, vec, row],
        out_specs=[row, row, pl.BlockSpec((8, D), lambda i: (0, 0)), pl.BlockSpec((8, 128), lambda i: (0, 0))],
        out_shape=[jax.ShapeDtypeStruct((S, D), F32), jax.ShapeDtypeStruct((S, D), BF16),
                   jax.ShapeDtypeStruct((8, D), F32), jax.ShapeDtypeStruct((8, 128), F32)],
        scratch_shapes=[pltpu.VMEM((8, D), F32)],
        compiler_params=_params(("arbitrary",)),
    )(u, w2, x, gate, gfin, tgt)


def _mix_in(x, ng, sh, sc, w_in, comm=None):
    S = x.shape[0]
    tm = _row_tile(S, 512)

    def body(x_ref, g_ref, sh_ref, sc_ref, w_ref, h_ref, z_ref):
        xv = x_ref[...]
        hb = (((xv * _rms(xv)) * g_ref[...]) * (1.0 + sc_ref[...]) + sh_ref[...]).astype(BF16)
        h_ref[...] = hb
        z_ref[...] = _dot_nt(hb, w_ref[...])

    row = pl.BlockSpec((tm, D), lambda i: (i, 0))
    vec = pl.BlockSpec((1, D), lambda i: (0, 0))
    return _hosted_call(
        body, name="mix_in", grid=(S // tm,),
        in_specs=[row, vec, vec, vec, pl.BlockSpec((Z_COLS, D), lambda i: (0, 0))],
        out_specs=[row, pl.BlockSpec((tm, Z_COLS), lambda i: (i, 0))],
        out_shape=[jax.ShapeDtypeStruct((S, D), BF16), jax.ShapeDtypeStruct((S, Z_COLS), F32)],
        scratch_shapes=[], semantics=("parallel",), args=(x, ng, sh, sc, w_in), comm=comm)


def _conv_taps(u, halo, rows):
    u1 = jnp.where(rows == 0, halo[7:8, :], pltpu.roll(u, 1, 0))
    u2 = jnp.where(rows == 0, halo[6:7, :], jnp.where(rows == 1, halo[7:8, :], pltpu.roll(u, 2, 0)))
    return u1, u2


def _mix_mid(z, conv_w, gq, gkv, wuq, wukv, cs, sn, comm=None):
    S = z.shape[0]
    tm = _row_tile(S, 512)
    hb = tm // 8

    def body(z_ref, zh_ref, cw_ref, gq_ref, gkv_ref, wuq_ref, wukv_ref, cs_ref, sn_ref,
             ya_ref, q_ref, k_ref, v_ref, cqn_ref, ckvn_ref):
        i = pl.program_id(0)
        xb = z_ref[:, 0:CONV_W]
        u = z_ref[:, CONV_W:2 * CONV_W] * z_ref[:, 2 * CONV_W:3 * CONV_W]
        halo = zh_ref[:, CONV_W:2 * CONV_W] * zh_ref[:, 2 * CONV_W:3 * CONV_W]
        halo = jnp.where(i > 0, halo, 0.0)
        rows = lax.broadcasted_iota(jnp.int32, (tm, CONV_W), 0)
        u1, u2 = _conv_taps(u, halo, rows)
        y = cw_ref[0:1, :] * u2 + cw_ref[1:2, :] * u1 + cw_ref[2:3, :] * u
        ya_ref[...] = xb * y

        lane = lax.broadcasted_iota(jnp.int32, (tm, 128), 1)
        cs_v, sn_v = cs_ref[...], sn_ref[...]
        cq = z_ref[:, 3 * CONV_W:3 * CONV_W + Q_LORA]
        cqn = ((cq * _rms(cq)) * gq_ref[...]).astype(BF16)
        cqn_ref[...] = cqn
        q = _dot_nt(cqn, wuq_ref[...])
        for h in range(HEADS):
            o = h * HEAD_SLOT
            q_ref[:, o:o + 128] = q[:, o:o + 128].astype(BF16)
            q_ref[:, o + 128:o + 256] = _rope(q[:, o + 128:o + 256], cs_v, sn_v, lane).astype(BF16)

        c0 = 3 * CONV_W + Q_LORA
        ckv = z_ref[:, c0:c0 + KV_LORA]
        ckvn = ((ckv * _rms(ckv)) * gkv_ref[...]).astype(BF16)
        ckvn_ref[...] = ckvn
        kv = jnp.dot(ckvn, wukv_ref[...], preferred_element_type=F32)
        krot = _rope(z_ref[:, c0 + KV_LORA:Z_COLS], cs_v, sn_v, lane).astype(BF16)
        for h in range(HEADS):
            o = h * HEAD_SLOT
            k_ref[:, o:o + 128] = kv[:, h * 128:(h + 1) * 128].astype(BF16)
            k_ref[:, o + 128:o + 256] = krot
        v_ref[...] = kv[:, HEADS * 128:].astype(BF16)

    def rows_of(n):
        return pl.BlockSpec((tm, n), lambda i: (i, 0))

    def whole(shape):
        return pl.BlockSpec(shape, lambda i: (0, 0))

    return _hosted_call(
        body, name="mix_mid", grid=(S // tm,),
        in_specs=[rows_of(Z_COLS), pl.BlockSpec((8, Z_COLS), lambda i: (jnp.maximum(i * hb - 1, 0), 0)),
                  whole((8, CONV_W)), whole((1, Q_LORA)), whole((1, KV_LORA)),
                  whole((HEADS * HEAD_SLOT, Q_LORA)), whole((KV_LORA, 2 * HEADS * 128)),
                  rows_of(128), rows_of(128)],
        out_specs=[rows_of(CONV_W), rows_of(HEADS * HEAD_SLOT), rows_of(HEADS * HEAD_SLOT), rows_of(HEADS * V_HEAD),
                   rows_of(Q_LORA), rows_of(KV_LORA)],
        out_shape=[jax.ShapeDtypeStruct((S, CONV_W), F32), jax.ShapeDtypeStruct((S, HEADS * HEAD_SLOT), BF16),
                   jax.ShapeDtypeStruct((S, HEADS * HEAD_SLOT), BF16), jax.ShapeDtypeStruct((S, HEADS * V_HEAD), BF16),
                   jax.ShapeDtypeStruct((S, Q_LORA), BF16), jax.ShapeDtypeStruct((S, KV_LORA), BF16)],
        scratch_shapes=[], semantics=("parallel",), args=(z, z, conv_w, gq, gkv, wuq, wukv, cs, sn), comm=comm)


def _att_blocks(S):
    bk = min(1024, max(S // 4, 128))
    return bk, bk


def _pair_tables(S, k_major):
    bq, bk = _att_blocks(S)
    nq, nk = S // bq, S // bk
    vis = lambda qi, ki: ki * bk < (qi + 1) * bq
    if k_major:
        pairs = [(qi, ki) for ki in range(nk) for qi in range(nq) if vis(qi, ki)]
    else:
        pairs = [(qi, ki) for qi in range(nq) for ki in range(nk) if vis(qi, ki)]
    cols = [[p[0] for p in pairs], [p[1] for p in pairs], [int((p[1] + 1) * bk > p[0] * bq) for p in pairs]]
    return [jnp.asarray(np.array(c, np.int32)) for c in cols], len(pairs)


def _chunk_mask(r0, nr, nc):
    r = (r0 + lax.broadcasted_iota(jnp.int32, (nr, nc), 0)) // CHUNK
    c = lax.broadcasted_iota(jnp.int32, (nr, nc), 1) // CHUNK
    return c <= r


def _diag_parts(bq, bk):
    return [(0, bq // 2, bk // 2), (bq // 2, bq // 2, bk)]


def _attention(q, k, v, comm=None):
    S = q.shape[0]
    bq, bk = _att_blocks(S)
    last_k = bq // bk - 1
    tables, n_pairs = _pair_tables(S, k_major=False)

    def body(qi_ref, ki_ref, mk_ref, q_ref, k_ref, v_ref, o_ref, lse_ref, m_s, l_s, acc_s):
        p_id = pl.program_id(1)
        qi, ki = qi_ref[p_id], ki_ref[p_id]

        @pl.when(ki == 0)
        def _():
            m_s[...] = jnp.full_like(m_s, NEG)
            l_s[...] = jnp.zeros_like(l_s)
            acc_s[...] = jnp.zeros_like(acc_s)

        def update(r0, nr, nc, masked):
            rows = slice(r0, r0 + nr)
            s = _dot_nt(q_ref[rows, :], k_ref[0:nc, :])
            if masked:
                s = jnp.where(_chunk_mask(r0, nr, nc), s, NEG)
            m_prev = m_s[rows, :]
            m_new = jnp.maximum(m_prev, jnp.max(s, axis=1, keepdims=True))
            alpha = jnp.exp2((m_prev - m_new) * EXP2_SCALE)
            p = jnp.exp2((s - jnp.tile(m_new, (1, nc // 128))) * EXP2_SCALE)
            l_s[rows, :] = alpha * l_s[rows, :] + jnp.sum(p, axis=1, keepdims=True)
            acc_s[rows, :] = alpha * acc_s[rows, :] + jnp.dot(p.astype(BF16), v_ref[0:nc, :],
                                                              preferred_element_type=F32)
            m_s[rows, :] = m_new

        @pl.when(mk_ref[p_id] == 0)
        def _():
            update(0, bq, bk, False)

        @pl.when(mk_ref[p_id] == 1)
        def _():
            for part in _diag_parts(bq, bk):
                update(*part, True)

        @pl.when(ki == qi * (last_k + 1) + last_k)
        def _():
            l = l_s[...]
            o_ref[...] = acc_s[...] / l
            lse_ref[...] = m_s[...] * EXP2_SCALE + jnp.log2(l)

    return _hosted_call(
        body, name="attention", grid=(HEADS, n_pairs),
        in_specs=[pl.BlockSpec((bq, HEAD_SLOT), lambda h, p, qt, kt, mt: (qt[p], h)),
                  pl.BlockSpec((bk, HEAD_SLOT), lambda h, p, qt, kt, mt: (kt[p], h)),
                  pl.BlockSpec((bk, V_HEAD), lambda h, p, qt, kt, mt: (kt[p], h))],
        out_specs=[pl.BlockSpec((bq, V_HEAD), lambda h, p, qt, kt, mt: (qt[p], h))] * 2,
        out_shape=[jax.ShapeDtypeStruct((S, HEADS * V_HEAD), F32)] * 2,
        scratch_shapes=[pltpu.VMEM((bq, V_HEAD), F32)] * 3,
        semantics=("arbitrary", "arbitrary"), args=(q, k, v), comm=comm, prefetch=tables)


def _attention_bwd(q, k, v, do, lse2, delta):
    S = q.shape[0]
    bq, bk = _att_blocks(S)
    nq = S // bq
    tables, n_pairs = _pair_tables(S, k_major=True)

    def body(qi_ref, ki_ref, mk_ref, q_ref, k_ref, v_ref, do_ref, lse_ref, dl_ref, dq_hbm, dk_ref, dv_ref,
             dq_s, dk_s, dv_s, sem):
        head, p_id = pl.program_id(0), pl.program_id(1)
        qi, ki = qi_ref[p_id], ki_ref[p_id]

        @pl.when(qi * bq <= ki * bk)
        def _():
            dk_s[...] = jnp.zeros_like(dk_s)
            dv_s[...] = jnp.zeros_like(dv_s)

        def update(r0, nr, nc, masked):
            rows, cols = slice(r0, r0 + nr), slice(0, nc)
            qv, kv, dov = q_ref[rows, :], k_ref[cols, :], do_ref[rows, :]
            s = _dot_nt(qv, kv)
            dp = _dot_nt(dov, v_ref[cols, :])
            if masked:
                s = jnp.where(_chunk_mask(r0, nr, nc), s, NEG)
            p = jnp.exp2(s * EXP2_SCALE - jnp.tile(lse_ref[rows, :], (1, nc // 128)))
            dv_s[cols, :] += lax.dot_general(p.astype(BF16), dov, (((0,), (0,)), ((), ())),
                                             preferred_element_type=F32)
            ds = (p * (dp - jnp.tile(dl_ref[rows, :], (1, nc // 128)))).astype(BF16)
            dk_s[cols, :] += lax.dot_general(ds, qv, (((0,), (0,)), ((), ())), preferred_element_type=F32)
            dq = jnp.dot(ds, kv, preferred_element_type=F32)
            out_rows = pl.ds(pl.multiple_of(qi * bq + r0, nr), nr)

            @pl.when(ki == 0)
            def _():
                dq_s[out_rows, :] = dq

            @pl.when(ki > 0)
            def _():
                dq_s[out_rows, :] += dq

        @pl.when(mk_ref[p_id] == 0)
        def _():
            update(0, bq, bk, False)

        @pl.when(mk_ref[p_id] == 1)
        def _():
            for part in _diag_parts(bq, bk):
                update(*part, True)

        @pl.when(qi == nq - 1)
        def _():
            dk_ref[...] = dk_s[...] * ATT_SCALE
            dv_ref[...] = dv_s[...]

        @pl.when(p_id == n_pairs - 1)
        def _():
            dq_s[...] = dq_s[...] * ATT_SCALE
            out = pltpu.make_async_copy(
                dq_s, dq_hbm.at[:, pl.ds(pl.multiple_of(head * HEAD_SLOT, HEAD_SLOT), HEAD_SLOT)], sem)
            out.start()
            out.wait()

    grid_spec = pltpu.PrefetchScalarGridSpec(
        num_scalar_prefetch=3, grid=(HEADS, n_pairs),
        in_specs=[pl.BlockSpec((bq, HEAD_SLOT), lambda h, p, qt, kt, mt: (qt[p], h)),
                  pl.BlockSpec((bk, HEAD_SLOT), lambda h, p, qt, kt, mt: (kt[p], h)),
                  pl.BlockSpec((bk, V_HEAD), lambda h, p, qt, kt, mt: (kt[p], h)),
                  pl.BlockSpec((bq, V_HEAD), lambda h, p, qt, kt, mt: (qt[p], h)),
                  pl.BlockSpec((bq, V_HEAD), lambda h, p, qt, kt, mt: (qt[p], h)),
                  pl.BlockSpec((bq, V_HEAD), lambda h, p, qt, kt, mt: (qt[p], h))],
        out_specs=[pl.BlockSpec(memory_space=pl.ANY),
                   pl.BlockSpec((bk, HEAD_SLOT), lambda h, p, qt, kt, mt: (kt[p], h)),
                   pl.BlockSpec((bk, V_HEAD), lambda h, p, qt, kt, mt: (kt[p], h))],
        scratch_shapes=[pltpu.VMEM((S, HEAD_SLOT), F32), pltpu.VMEM((bk, HEAD_SLOT), F32),
                        pltpu.VMEM((bk, V_HEAD), F32), pltpu.SemaphoreType.DMA])
    return pl.pallas_call(
        body, name="attention_bwd", grid_spec=grid_spec,
        out_shape=[jax.ShapeDtypeStruct((S, HEADS * HEAD_SLOT), F32), jax.ShapeDtypeStruct((S, HEADS * HEAD_SLOT), F32),
                   jax.ShapeDtypeStruct((S, HEADS * V_HEAD), F32)],
        compiler_params=_params(("arbitrary", "arbitrary")),
    )(*tables, q, k, v, do, lse2, delta)


def _group_mats():
    def blockdiag(n, g):
        idx = np.arange(n) // g
        return jnp.asarray((idx[:, None] == idx[None, :]).astype(np.float32), dtype=BF16)
    return blockdiag(CONV_W, CONV_GROUP), blockdiag(HEADS * V_HEAD, V_HEAD)


def _mix_out(ya, o, gout, w_out, x, gate, ga, gb):
    S = x.shape[0]
    tm = _row_tile(S, 512)

    def body(ya_ref, o_ref, go_ref, w_ref, x_ref, g_ref, ga_ref, gb_ref, xo_ref, yn_ref, yo_ref):
        yav, ov = ya_ref[...], o_ref[...]
        ra = lax.rsqrt(_gsum(yav * yav, ga_ref[...]) * (1.0 / CONV_GROUP) + EPS)
        rb = lax.rsqrt(_gsum(ov * ov, gb_ref[...]) * (1.0 / V_HEAD) + EPS)
        na = ((yav * ra) * go_ref[:, 0:CONV_W]).astype(BF16)
        nb = ((ov * rb) * go_ref[:, CONV_W:]).astype(BF16)
        yn_ref[:, 0:CONV_W] = na
        yn_ref[:, CONV_W:] = nb
        yo = (jnp.dot(na, w_ref[0:CONV_W, :], preferred_element_type=F32)
              + jnp.dot(nb, w_ref[CONV_W:, :], preferred_element_type=F32))
        xo_ref[...] = x_ref[...] + g_ref[...] * yo
        yo_ref[...] = yo.astype(BF16)

    row = pl.BlockSpec((tm, D), lambda i: (i, 0))
    half = pl.BlockSpec((tm, CONV_W), lambda i: (i, 0))
    vec = pl.BlockSpec((1, D), lambda i: (0, 0))
    sq = pl.BlockSpec((CONV_W, CONV_W), lambda i: (0, 0))
    return pl.pallas_call(
        body, name="mix_out", grid=(S // tm,),
        in_specs=[half, half, vec, pl.BlockSpec((D, D), lambda i: (0, 0)), row, vec, sq, sq],
        out_specs=[row, row, row],
        out_shape=[jax.ShapeDtypeStruct((S, D), F32), jax.ShapeDtypeStruct((S, D), BF16),
                   jax.ShapeDtypeStruct((S, D), BF16)],
        compiler_params=_params(("parallel",)),
    )(ya, o, gout, w_out, x, gate, ga, gb)


def _mix_out_bwd(dx, gate, yo, w_out, ya, o, gout, ga, gb, comm=None):
    S = dx.shape[0]
    tm = _row_tile(S, 512)
    n_i = S // tm

    def norm_bwd(v, dn, gain, gmat, inv_n):
        r = lax.rsqrt(_gsum(v * v, gmat) * inv_n + EPS)
        vh = v * r
        dy = dn * gain
        return r * (dy - vh * (_gsum(dy * vh, gmat) * inv_n)), dn * vh

    def body(dx_ref, g_ref, yo_ref, w_ref, ya_ref, o_ref, go_ref, ga_ref, gb_ref,
             dyo_ref, dya_ref, do_ref, dl_ref, p_ref):
        i = pl.program_id(0)
        dxv = dx_ref[...]
        dyo = (dxv * g_ref[...]).astype(BF16)
        dyo_ref[...] = dyo
        dyn = _dot_nt(dyo, w_ref[...])
        dya, dga = norm_bwd(ya_ref[...], dyn[:, 0:CONV_W], go_ref[:, 0:CONV_W], ga_ref[...], 1.0 / CONV_GROUP)
        ov = o_ref[...]
        do, dgb = norm_bwd(ov, dyn[:, CONV_W:], go_ref[:, CONV_W:], gb_ref[...], 1.0 / V_HEAD)
        dya_ref[...] = dya
        do_ref[...] = do.astype(BF16)
        dl_ref[...] = _gsum(do * ov, gb_ref[...], split=True)

        @pl.when(i == 0)
        def _():
            p_ref[...] = jnp.zeros_like(p_ref)

        p_ref[:, 0:D] += _rsum8(dxv * yo_ref[...].astype(F32))
        p_ref[:, D:D + CONV_W] += _rsum8(dga)
        p_ref[:, D + CONV_W:2 * D] += _rsum8(dgb)

        @pl.when(i == n_i - 1)
        def _():
            _all_rows(p_ref)

    row = pl.BlockSpec((tm, D), lambda i: (i, 0))
    half = pl.BlockSpec((tm, CONV_W), lambda i: (i, 0))
    vec = pl.BlockSpec((1, D), lambda i: (0, 0))
    sq = pl.BlockSpec((CONV_W, CONV_W), lambda i: (0, 0))
    return _hosted_call(
        body, name="mix_out_bwd", grid=(n_i,),
        in_specs=[row, vec, row, pl.BlockSpec((D, D), lambda i: (0, 0)), half, half, vec, sq, sq],
        out_specs=[row, half, half, half, pl.BlockSpec((8, 2 * D), lambda i: (0, 0))],
        out_shape=[jax.ShapeDtypeStruct((S, D), BF16), jax.ShapeDtypeStruct((S, CONV_W), F32),
                   jax.ShapeDtypeStruct((S, CONV_W), BF16), jax.ShapeDtypeStruct((S, CONV_W), F32),
                   jax.ShapeDtypeStruct((8, 2 * D), F32)],
        scratch_shapes=[], semantics=("arbitrary",), args=(dx, gate, yo, w_out, ya, o, gout, ga, gb), comm=comm)


MID_SUMS = 3 * CONV_W + Q_LORA + KV_LORA


def _mix_mid_bwd(z, dya, conv_w, gq, gkv, wuq, wukv, cs, sn, dq, dk, dv, comm=None):
    S = z.shape[0]
    tm = _row_tile(S, 256)
    n_i = S // tm
    hb = tm // 8
    last_blk = S // 8 - 1

    def latent_bwd(cv, dcn, gain):
        r = _rms(cv)
        ch = cv * r
        dy = dcn * gain
        return r * (dy - ch * jnp.mean(dy * ch, axis=-1, keepdims=True)), dcn * ch

    def body(z_ref, zp_ref, zn_ref, dya_ref, dyan_ref, cw_ref, gq_ref, gkv_ref, wuq_ref, wukv_ref, cs_ref, sn_ref,
             dq_ref, dk_ref, dv_ref, dz_ref, dqf_ref, dkvf_ref, p_ref):
        i = pl.program_id(0)
        xb, xc, xu = z_ref[:, 0:CONV_W], z_ref[:, CONV_W:2 * CONV_W], z_ref[:, 2 * CONV_W:3 * CONV_W]
        u = xc * xu
        halo = jnp.where(i > 0, zp_ref[:, CONV_W:2 * CONV_W] * zp_ref[:, 2 * CONV_W:3 * CONV_W], 0.0)
        rows = lax.broadcasted_iota(jnp.int32, (tm, CONV_W), 0)
        u1, u2 = _conv_taps(u, halo, rows)
        w0, w1, w2 = cw_ref[0:1, :], cw_ref[1:2, :], cw_ref[2:3, :]
        y = w0 * u2 + w1 * u1 + w2 * u
        dyav = dya_ref[...]
        dy = dyav * xb
        nxt = jnp.where(i < n_i - 1, dyan_ref[...] * zn_ref[:, 0:CONV_W], 0.0)
        dy1 = jnp.where(rows == tm - 1, nxt[0:1, :], pltpu.roll(dy, tm - 1, 0))
        dy2 = jnp.where(rows == tm - 1, nxt[1:2, :], jnp.where(rows == tm - 2, nxt[0:1, :], pltpu.roll(dy, tm - 2, 0)))
        du = w2 * dy + w1 * dy1 + w0 * dy2
        dz_ref[:, 0:CONV_W] = (dyav * y).astype(BF16)
        dz_ref[:, CONV_W:2 * CONV_W] = (du * xu).astype(BF16)
        dz_ref[:, 2 * CONV_W:3 * CONV_W] = (du * xc).astype(BF16)

        lane = lax.broadcasted_iota(jnp.int32, (tm, 128), 1)
        cs_v, sn_v = cs_ref[...], sn_ref[...]
        dkr = jnp.zeros((tm, 128), F32)
        for h in range(HEADS):
            o = h * HEAD_SLOT
            dqf_ref[:, o:o + 128] = dq_ref[:, o:o + 128].astype(BF16)
            dqf_ref[:, o + 128:o + 256] = _rope_t(dq_ref[:, o + 128:o + 256], cs_v, sn_v, lane).astype(BF16)
            dkvf_ref[:, h * 128:(h + 1) * 128] = dk_ref[:, o:o + 128].astype(BF16)
            dkr = dkr + dk_ref[:, o + 128:o + 256]
        dkvf_ref[:, HEADS * 128:] = dv_ref[...].astype(BF16)

        c0 = 3 * CONV_W
        dcqn = jnp.dot(dqf_ref[...], wuq_ref[...], preferred_element_type=F32)
        dcq, dgq = latent_bwd(z_ref[:, c0:c0 + Q_LORA], dcqn, gq_ref[...])
        dz_ref[:, c0:c0 + Q_LORA] = dcq.astype(BF16)
        c1 = c0 + Q_LORA
        dckvn = _dot_nt(dkvf_ref[...], wukv_ref[...])
        dckv, dgkv = latent_bwd(z_ref[:, c1:c1 + KV_LORA], dckvn, gkv_ref[...])
        dz_ref[:, c1:c1 + KV_LORA] = dckv.astype(BF16)
        dz_ref[:, c1 + KV_LORA:Z_COLS] = _rope_t(dkr, cs_v, sn_v, lane).astype(BF16)

        @pl.when(i == 0)
        def _():
            p_ref[...] = jnp.zeros_like(p_ref)

        p_ref[:, 0:CONV_W] += _rsum8(dy * u2)
        p_ref[:, CONV_W:2 * CONV_W] += _rsum8(dy * u1)
        p_ref[:, 2 * CONV_W:3 * CONV_W] += _rsum8(dy * u)
        p_ref[:, c0:c0 + Q_LORA] += _rsum8(dgq)
        p_ref[:, c1:c1 + KV_LORA] += _rsum8(dgkv)

        @pl.when(i == n_i - 1)
        def _():
            _all_rows(p_ref)

    def rows_of(n):
        return pl.BlockSpec((tm, n), lambda i: (i, 0))

    def whole(shape):
        return pl.BlockSpec(shape, lambda i: (0, 0))

    def prev8(n):
        return pl.BlockSpec((8, n), lambda i: (jnp.maximum(i * hb - 1, 0), 0))

    def next8(n):
        return pl.BlockSpec((8, n), lambda i: (jnp.minimum((i + 1) * hb, last_blk), 0))

    return _hosted_call(
        body, name="mix_mid_bwd", grid=(n_i,),
        in_specs=[rows_of(Z_COLS), prev8(Z_COLS), next8(Z_COLS), rows_of(CONV_W), next8(CONV_W),
                  whole((8, CONV_W)), whole((1, Q_LORA)), whole((1, KV_LORA)),
                  whole((HEADS * HEAD_SLOT, Q_LORA)), whole((KV_LORA, 2 * HEADS * 128)),
                  rows_of(128), rows_of(128),
                  rows_of(HEADS * HEAD_SLOT), rows_of(HEADS * HEAD_SLOT), rows_of(HEADS * V_HEAD)],
        out_specs=[rows_of(Z_COLS), rows_of(HEADS * HEAD_SLOT), rows_of(2 * HEADS * 128), whole((8, MID_SUMS))],
        out_shape=[jax.ShapeDtypeStruct((S, Z_COLS), BF16), jax.ShapeDtypeStruct((S, HEADS * HEAD_SLOT), BF16),
                   jax.ShapeDtypeStruct((S, 2 * HEADS * 128), BF16), jax.ShapeDtypeStruct((8, MID_SUMS), F32)],
        scratch_shapes=[], semantics=("arbitrary",),
        args=(z, z, z, dya, dya, conv_w, gq, gkv, wuq, wukv, cs, sn, dq, dk, dv), comm=comm)


ADA_Q = N_MOD * D // N_CHIPS
ADA_TN = 768


def _ada_forward(c_all, ada_w_q, ada_b_q):
    def body(c_ref, w_ref, b_ref, o_ref):
        cv = c_ref[...]
        sc = (cv * jax.nn.sigmoid(cv)).astype(BF16)
        o_ref[...] = jnp.dot(sc, w_ref[...].astype(BF16), preferred_element_type=F32) + b_ref[...]

    return pl.pallas_call(
        body, name="ada_forward", grid=(ADA_Q // ADA_TN,),
        in_specs=[pl.BlockSpec((16, D), lambda j: (0, 0)), pl.BlockSpec((D, ADA_TN), lambda j: (0, j)),
                  pl.BlockSpec((1, ADA_TN), lambda j: (0, j))],
        out_specs=pl.BlockSpec((16, ADA_TN), lambda j: (0, j)),
        out_shape=jax.ShapeDtypeStruct((16, ADA_Q), F32),
        compiler_params=_params(("parallel",)),
    )(c_all, ada_w_q, ada_b_q)


def _ada_wgrad(c_all, dmod_q):
    def body(c_ref, d_ref, o_ref):
        cv = c_ref[...]
        sc = (cv * jax.nn.sigmoid(cv)).astype(BF16)
        o_ref[...] = lax.dot_general(sc, d_ref[...].astype(BF16), (((0,), (0,)), ((), ())),
                                     preferred_element_type=F32)

    return pl.pallas_call(
        body, name="ada_wgrad", grid=(ADA_Q // ADA_TN,),
        in_specs=[pl.BlockSpec((16, D), lambda j: (0, 0)), pl.BlockSpec((16, ADA_TN), lambda j: (0, j))],
        out_specs=pl.BlockSpec((D, ADA_TN), lambda j: (0, j)),
        out_shape=jax.ShapeDtypeStruct((D, ADA_Q), F32),
        compiler_params=_params(("parallel",)),
    )(c_all, dmod_q)


def _sum_devices(parts):
    n = parts.shape[1]

    def body(p_ref, o_ref):
        o_ref[...] = jnp.broadcast_to(jnp.sum(p_ref[...], axis=0, keepdims=True), o_ref.shape)

    return pl.pallas_call(
        body, name="sum_devices",
        in_specs=[pl.BlockSpec((N_DEV, n), lambda: (0, 0))], out_specs=pl.BlockSpec((N_DEV, n), lambda: (0, 0)),
        out_shape=jax.ShapeDtypeStruct((N_DEV, n), F32),
    )(parts)


def _adamw(ws, gs, ms, vs, *, name, comm=None):
    n = len(ws)
    _, rows, cols = ws[0].shape
    tr = _row_tile(rows, 256)

    def body(*refs):
        ins, outs = refs[:4 * n], refs[4 * n:]
        for k in range(n):
            w_ref, g_ref, m_ref, v_ref = ins[k], ins[n + k], ins[2 * n + k], ins[3 * n + k]
            go_ref, d_ref, mo_ref, vo_ref = outs[4 * k:4 * k + 4]
            gv = g_ref[...]
            mn = B1 * m_ref[0] + (1.0 - B1) * gv
            vn = B2 * v_ref[0] + (1.0 - B2) * (gv * gv)
            m_hat = mn / (1.0 - B1 ** STEP)
            v_hat = vn / (1.0 - B2 ** STEP)
            go_ref[0] = gv
            d_ref[0] = -LR * (m_hat / (jnp.sqrt(v_hat) + AEPS) + WD * w_ref[0])
            mo_ref[0] = mn
            vo_ref[0] = vn

    blk = pl.BlockSpec((1, tr, cols), lambda i: (0, i, 0))
    gblk = pl.BlockSpec((tr, cols), lambda i: (i, 0))
    res, couts = _hosted_call(
        body, name=name, grid=(rows // tr,),
        in_specs=[blk] * n + [gblk] * n + [blk] * 2 * n, out_specs=[blk] * 4 * n,
        out_shape=[jax.ShapeDtypeStruct((1, rows, cols), F32)] * 4 * n,
        scratch_shapes=[], semantics=("parallel",), args=(*ws, *gs, *ms, *vs), comm=comm)
    return [res[4 * k:4 * k + 4] for k in range(n)], couts


def _small_allgather(v, *, name):
    m, n = v.shape

    def body(x_ref, out_ref, send_sems, recv_sems, local_sem):
        x, y, c = _place()
        me, sibling = (x, y, c), (x, y, 1 - c)
        chips = [(1 - x, y), (x, 1 - y), (1 - x, 1 - y)]

        def rows(px, py, pc):
            return out_ref.at[pl.ds((4 * px + 2 * py + pc) * m, m), :]

        def copy(k, block, to, src=None):
            return pltpu.make_async_remote_copy(
                src_ref=rows(*block) if src is None else src, dst_ref=rows(*block),
                send_sem=send_sems.at[k], recv_sem=recv_sems.at[k], device_id=to, device_id_type=MESH)

        mine = pltpu.make_async_copy(x_ref, rows(*me), local_sem)
        mine.start()
        first = [copy(0, me, sibling, src=x_ref)]
        first += [copy(1 + j, me, (*chip, c), src=x_ref) for j, chip in enumerate(chips)]
        for cp in first:
            cp.start()
        passed = [copy(4 + j, (*chip, c), sibling) for j, chip in enumerate(chips)]
        for j, chip in enumerate(chips):
            copy(1 + j, (*chip, c), me).wait_recv()
            passed[j].start()
        copy(0, sibling, me).wait_recv()
        for j, chip in enumerate(chips):
            copy(4 + j, (*chip, 1 - c), me).wait_recv()
        for cp in first + passed:
            cp.wait_send()
        mine.wait()

    return pl.pallas_call(
        body, name=name,
        out_shape=jax.ShapeDtypeStruct((N_DEV * m, n), v.dtype),
        in_specs=[pl.BlockSpec(memory_space=pltpu.VMEM)], out_specs=pl.BlockSpec(memory_space=pltpu.VMEM),
        scratch_shapes=[pltpu.SemaphoreType.DMA((7,)), pltpu.SemaphoreType.DMA((7,)), pltpu.SemaphoreType.DMA],
    )(v)


ADD_BLOCKS = 2


def _pair_add(place, gs, ts, *, name):
    n_a = len(gs)

    def body(pl_ref, *refs):
        g_refs, t_refs = refs[:n_a], refs[n_a:2 * n_a]
        pf_refs, pb_refs = refs[2 * n_a:3 * n_a], refs[3 * n_a:]
        for g_ref, t_ref, pf_ref, pb_ref in zip(g_refs, t_refs, pf_refs, pb_refs):
            s = g_ref[...] + t_ref[...]
            pf_ref[...] = s
            pb_ref[...] = s.astype(BF16)

    def blk(t, own_half):
        tr = t.shape[1] // ADD_BLOCKS
        if own_half:
            return pl.BlockSpec((1, tr, t.shape[2]), lambda q, r, p: (q, p[0] * ADD_BLOCKS + r, 0))
        return pl.BlockSpec((1, tr, t.shape[2]), lambda q, r, p: (q, r, 0))

    grid_spec = pltpu.PrefetchScalarGridSpec(
        num_scalar_prefetch=1, grid=(N_CHIPS, ADD_BLOCKS),
        in_specs=[blk(t, True) for t in ts] + [blk(t, False) for t in ts],
        out_specs=[blk(t, False) for t in ts] * 2)
    res = pl.pallas_call(
        body, name=name, grid_spec=grid_spec,
        out_shape=[jax.ShapeDtypeStruct(t.shape, F32) for t in ts] + [jax.ShapeDtypeStruct(t.shape, BF16) for t in ts],
        compiler_params=_params(("parallel", "parallel")),
    )(place, *gs, *ts)
    return list(res[:n_a]), list(res[n_a:])


def _chip_add(place, pfs, ts, *, name):
    n_a = len(pfs)

    def body(pl_ref, *refs):
        pf_refs, t_refs, o_refs = refs[:n_a], refs[n_a:4 * n_a], refs[4 * n_a:]
        for i, (pf_ref, o_ref) in enumerate(zip(pf_refs, o_refs)):
            t1, t2, t3 = t_refs[3 * i:3 * i + 3]
            o_ref[...] = ((pf_ref[0] + t1[0].astype(F32)) + t2[0].astype(F32)) + t3[0].astype(F32)

    def slot(t, j):
        return pl.BlockSpec((1, t.shape[1] // ADD_BLOCKS, t.shape[2]), lambda r, p: (p[1] ^ j, r, 0))

    grid_spec = pltpu.PrefetchScalarGridSpec(
        num_scalar_prefetch=1, grid=(ADD_BLOCKS,),
        in_specs=[slot(t, 0) for t in pfs] + [slot(t, j) for t in ts for j in (1, 2, 3)],
        out_specs=[pl.BlockSpec((t.shape[1] // ADD_BLOCKS, t.shape[2]), lambda r, p: (r, 0)) for t in pfs])
    res = pl.pallas_call(
        body, name=name, grid_spec=grid_spec,
        out_shape=[jax.ShapeDtypeStruct(t.shape[1:], F32) for t in pfs],
        compiler_params=_params(("parallel",)),
    )(place, *pfs, *[t for t in ts for _ in range(3)])
    return list(res)


BULK = [("ffn1_w1", "colsT"), ("ffn1_w3", "colsT"), ("ffn1_w2", "rows"), ("w_in", "cols"), ("w_uq", "colsT"),
        ("w_ukv", "cols"), ("w_out", "rows"), ("ffn2_w1", "colsT"), ("ffn2_w3", "colsT"), ("ffn2_w2", "rows")]
KIND = dict(BULK)


def _group(*names):
    return [b for b in BULK if b[0] in names]


W_FIRST = _group("ffn1_w1", "ffn1_w3")
W_REST = [b for b in BULK if b not in W_FIRST]
W_MIX = _group("ffn1_w2", "w_in", "w_uq", "w_ukv", "w_out")
W_FFN2 = _group("ffn2_w1", "ffn2_w3", "ffn2_w2")
G_FFN2 = _group("ffn2_w1", "ffn2_w3", "ffn2_w2")
G_MIX = _group("w_in", "w_uq", "w_ukv", "w_out")
G_FFN1 = _group("ffn1_w1", "ffn1_w3", "ffn1_w2")


def _gathered_weights(specs, shards, got, myq):
    out = {}
    for (name, kind), part in zip(specs, got):
        part = lax.dynamic_update_slice_in_dim(part, shards[name][None], myq, axis=0)
        out[name] = _full_weight(part, kind)
    return out


def _working_shard(w, kind):
    return jnp.swapaxes(w, 1, 2)[0] if kind == "colsT" else w[0]


def _full_weight(parts, kind):
    if kind == "cols":
        return jnp.transpose(parts, (1, 0, 2)).reshape(parts.shape[1], -1)
    return parts.reshape(-1, parts.shape[2])


def _quarters(g, kind):
    if kind == "cols":
        k, n = g.shape
        return jnp.transpose(g.reshape(k, N_CHIPS, n // N_CHIPS), (1, 0, 2))
    return g.reshape(N_CHIPS, g.shape[0] // N_CHIPS, g.shape[1])


def _pad_heads(w_uq_t):
    w = w_uq_t.reshape(HEADS, QK_NOPE + QK_ROPE, Q_LORA)
    return jnp.pad(w, ((0, 0), (0, HEAD_SLOT - QK_NOPE - QK_ROPE), (0, 0))).reshape(HEADS * HEAD_SLOT, Q_LORA)


def _unpad_heads(g):
    return g.reshape(HEADS, HEAD_SLOT, Q_LORA)[:, :QK_NOPE + QK_ROPE].reshape(HEADS * (QK_NOPE + QK_ROPE), Q_LORA)


def _split_kv(w_ukv):
    return jnp.transpose(w_ukv.reshape(KV_LORA, HEADS, 2, 128), (0, 2, 1, 3)).reshape(KV_LORA, 2 * HEADS * 128)


def _merge_kv(g):
    return jnp.transpose(g.reshape(KV_LORA, 2, HEADS, 128), (0, 2, 1, 3)).reshape(KV_LORA, 2 * HEADS * 128)


def _rope_tables(positions):
    inv_freq = ROPE_THETA ** (-jnp.arange(0, QK_ROPE, 2, dtype=F32) / QK_ROPE)
    ang = positions.astype(F32)[:, None] * inv_freq
    cos, sin, zero = jnp.cos(ang), jnp.sin(ang), jnp.zeros((positions.shape[0], 64), F32)
    return jnp.concatenate([cos, cos, zero], axis=1), jnp.concatenate([sin, sin, zero], axis=1)


def _assemble(place, specs, rhs, others):
    south = place[0] == 0
    return {b[0]: jnp.concatenate([jnp.where(south, rh, ot), jnp.where(south, ot, rh)], axis=0)
            for b, rh, ot in zip(specs, rhs, others)}


def _local_step(x, positions, target, mod, vec, conv_w, w_first, rest_shards, place, tail_host=None):
    row = lambda k: mod[k:k + 1]
    sh1, sc1, g1, sh2, sc2, g2, sh3, sc3, g3 = [row(k) for k in range(N_MOD)]
    cs, sn = _rope_tables(positions)
    cw8 = jnp.pad(conv_w, ((0, 5), (0, 0)))
    ga, gb = _group_mats()
    dist = place is not None
    comm = lambda prog: prog if dist else None

    w = dict(w_first) if dist else {**rest_shards, **w_first}

    def gather(specs):
        return _Gather([rest_shards[b[0]] for b in specs]) if dist else None

    def arrived(specs, got):
        if dist:
            w.update(_gathered_weights(specs, rest_shards, got, place[1]))

    (h1, a1, b1, u1), got = _ffn_up(x, vec["norm_ffn1_g"], sh1, sc1, w["ffn1_w1"], w["ffn1_w3"],
                                    name="ffn1_up", comm=gather(W_MIX))
    arrived(W_MIX, got)
    w_in = jnp.pad(w["w_in"].T, ((0, Z_COLS - IN_COLS), (0, 0)))
    wuq = _pad_heads(w["w_uq"])
    wukv = _split_kv(w["w_ukv"])
    (x1, f1), _ = _ffn_down(u1, w["ffn1_w2"], x, g1, name="ffn1_down")
    (h2, z), _ = _mix_in(x1, vec["norm_mix_g"], sh2, sc2, w_in)
    (ya, q, k, v, cqn, ckvn), _ = _mix_mid(z, cw8, vec["q_norm_g"], vec["kv_norm_g"], wuq, wukv, cs, sn)
    (o, lse), got = _attention(q, k, v, comm=gather(W_FFN2))
    arrived(W_FFN2, got)
    x2, yn, yo = _mix_out(ya, o, vec["out_norm_g"], w["w_out"], x1, g2, ga, gb)
    (h3, a3, b3, u3), _ = _ffn_up(x2, vec["norm_ffn2_g"], sh3, sc3, w["ffn2_w1"], w["ffn2_w3"], name="ffn2_up")
    dx3, f3, dgfin, loss_blk = _ffn_down_loss(u3, w["ffn2_w2"], x2, g3, vec["final_norm_g"], target)

    grads, reduced = {}, {}

    def tn(a, b, tm, tn_, name, prog=None):
        if prog is None:
            return _tn_matmul(a, b, tm=tm, tn=tn_, name=name), None
        return _tn_matmul(a, b, tm=tm, tn=tn_, name=name, comm=prog)

    def slab_of(specs):
        return [_quarters(grads[n], kind) for n, kind in specs]

    (df3, da3, db3, dg3), _ = _ffn_bwd_du(dx3, g3, f3, w["ffn2_w2"], a3, b3, name="ffn2_bwd_du")
    grads["ffn2_w2"], _ = tn(u3, df3, FF // 2, D, "ffn2_dw2")
    grads["ffn2_w1"], _ = tn(da3, h3, FF // 2, D, "ffn2_dw1")
    grads["ffn2_w3"], _ = tn(db3, h3, FF // 2, D, "ffn2_dw3")
    (dx2, s3), _ = _dh_normbwd([(da3, w["ffn2_w1"]), (db3, w["ffn2_w3"])], x2, vec["norm_ffn2_g"], sc3, dx3,
                               name="ffn2_bwd_dh")

    p1 = slab_of(G_FFN2) if dist else None
    (dyo, dya, do, delta, s_out), t1 = _mix_out_bwd(dx2, g2, yo, w["w_out"], ya, o, vec["out_norm_g"], ga, gb,
                                                    comm=comm(_PairExchange(p1) if dist else None))
    grads["w_out"], _ = tn(yn, dyo, D, D, "dw_out")
    if dist:
        pf1, pb1 = _pair_add(place, p1, t1, name="ffn2g_pair_add")
    dq, dk, dv = _attention_bwd(q, k, v, do, lse, delta)
    (dz, dqf, dkvf, s_mid), t2 = _mix_mid_bwd(z, dya, cw8, vec["q_norm_g"], vec["kv_norm_g"], wuq, wukv, cs, sn,
                                              dq, dk, dv, comm=comm(_ChipExchange(pb1) if dist else None))
    g_uq, _ = tn(dqf, cqn, HEADS * HEAD_SLOT, Q_LORA, "dw_uq")
    g_ukv, _ = tn(ckvn, dkvf, KV_LORA, 2 * HEADS * 128, "dw_ukv")
    grads["w_uq"], grads["w_ukv"] = _unpad_heads(g_uq), _merge_kv(g_ukv)
    g_in, _ = tn(h2, dz, D, Z_COLS // 2, "dw_in")
    grads["w_in"] = g_in[:, :IN_COLS]
    (dx1, s2), _ = _dh_normbwd([(dz, w_in)], x1, vec["norm_mix_g"], sc2, dx2, name="mix_bwd_dh")

    p2 = slab_of(G_MIX) if dist else None
    rh_ffn2 = _chip_add(place, pf1, t2, name="ffn2g_chip_add") if dist else None
    (df1, da1, db1, dg1), got = _ffn_bwd_du(dx1, g1, f1, w["ffn1_w2"], a1, b1, name="ffn1_bwd_du",
                                            comm=comm(_Multi([_PairExchange(p2), _PairShare(rh_ffn2)]) if dist else None))
    if dist:
        t1 = got[:len(p2)]
        reduced.update(_assemble(place, G_FFN2, rh_ffn2, got[len(p2):]))
    dh_pairs = [(da1, w["ffn1_w1"]), (db1, w["ffn1_w3"])]
    if dist:
        pf2, pb2 = _pair_add(place, p2, t1, name="mixg_pair_add")
        grads["ffn1_w2"], t2 = tn(u1, df1, FF // 2, D, "ffn1_dw2", _ChipExchange(pb2))
        rh_mix = _chip_add(place, pf2, t2, name="mixg_chip_add")
        q_w2 = [_quarters(grads["ffn1_w2"], KIND["ffn1_w2"])]
        grads["ffn1_w1"], got = tn(da1, h1, FF // 2, D, "ffn1_dw1", _Multi([_PairShare(rh_mix), _PairExchange(q_w2)]))
        reduced.update(_assemble(place, G_MIX, rh_mix, got[:len(rh_mix)]))
        pf_w2, pb_w2 = _pair_add(place, q_w2, got[len(rh_mix):], name="ffn1w2_pair_add")
        q_w1 = [_quarters(grads["ffn1_w1"], KIND["ffn1_w1"])]
        grads["ffn1_w3"], got = tn(db1, h1, FF // 2, D, "ffn1_dw3", _Multi([_ChipExchange(pb_w2), _PairExchange(q_w1)]))
        t2_w2 = got[:1]
        pf_w1, pb_w1 = _pair_add(place, q_w1, got[1:], name="ffn1w1_pair_add")
        q_w3 = [_quarters(grads["ffn1_w3"], KIND["ffn1_w3"])]
        (dx0, s1), got = _dh_normbwd(dh_pairs, x, vec["norm_ffn1_g"], sc1, dx1, name="ffn1_bwd_dh",
                                     comm=_Multi([_ChipExchange(pb_w1), _PairExchange(q_w3)]))
        t2_w1 = got[:1]
        pf_w3, pb_w3 = _pair_add(place, q_w3, got[1:], name="ffn1w3_pair_add")
        t2_w3 = tail_host(_ChipExchange(pb_w3), reduced)
        rh = _chip_add(place, pf_w1 + pf_w3 + pf_w2, t2_w1 + t2_w3 + t2_w2, name="ffn1g_chip_add")
        reduced.update(_assemble(place, G_FFN1, rh, _run_comm(_PairShare(rh), name="ffn1g_pair_share")))
    else:
        grads["ffn1_w2"], _ = tn(u1, df1, FF // 2, D, "ffn1_dw2")
        grads["ffn1_w1"], _ = tn(da1, h1, FF // 2, D, "ffn1_dw1")
        grads["ffn1_w3"], _ = tn(db1, h1, FF // 2, D, "ffn1_dw3")
        (dx0, s1), _ = _dh_normbwd(dh_pairs, x, vec["norm_ffn1_g"], sc1, dx1, name="ffn1_bwd_dh")
        reduced = grads

    def part(s, k):
        return s[0:1, k * D:(k + 1) * D]

    dmod = jnp.concatenate([part(s1, 1), part(s1, 0), dg1[0:1], part(s2, 1), part(s2, 0), part(s_out, 0),
                            part(s3, 1), part(s3, 0), dg3[0:1]], axis=1)
    small = {"norm_ffn1_g": part(s1, 2), "norm_mix_g": part(s2, 2), "out_norm_g": part(s_out, 1),
             "norm_ffn2_g": part(s3, 2), "final_norm_g": dgfin[0:1],
             "q_norm_g": s_mid[0:1, 3 * CONV_W:3 * CONV_W + Q_LORA],
             "kv_norm_g": s_mid[0:1, 3 * CONV_W + Q_LORA:MID_SUMS], "conv_w": s_mid[0:1, 0:3 * CONV_W]}
    return loss_blk, dx0, reduced, dmod, small


SMALL = [("norm_ffn1_g", D), ("norm_mix_g", D), ("out_norm_g", D), ("norm_ffn2_g", D), ("final_norm_g", D),
         ("q_norm_g", Q_LORA), ("kv_norm_g", KV_LORA), ("conv_w", 3 * CONV_W)]
WEIGHTS = ['ada_w', 'ada_b', 'norm_ffn1_g', 'ffn1_w1', 'ffn1_w3', 'ffn1_w2', 'norm_mix_g', 'w_in', 'conv_w',
           'q_norm_g', 'w_uq', 'kv_norm_g', 'w_ukv', 'out_norm_g', 'w_out', 'norm_ffn2_g', 'ffn2_w1', 'ffn2_w3',
           'ffn2_w2', 'final_norm_g']


def kernel(x, c, positions, ada_w, ada_b, norm_ffn1_g, ffn1_w1, ffn1_w3, ffn1_w2, norm_mix_g, w_in, conv_w, q_norm_g, w_uq, kv_norm_g, w_ukv, out_norm_g, w_out, norm_ffn2_g, ffn2_w1, ffn2_w3, ffn2_w2, final_norm_g, loss_target, m_ada_w, m_ada_b, m_norm_ffn1_g, m_ffn1_w1, m_ffn1_w3, m_ffn1_w2, m_norm_mix_g, m_w_in, m_conv_w, m_q_norm_g, m_w_uq, m_kv_norm_g, m_w_ukv, m_out_norm_g, m_w_out, m_norm_ffn2_g, m_ffn2_w1, m_ffn2_w3, m_ffn2_w2, m_final_norm_g, v_ada_w, v_ada_b, v_norm_ffn1_g, v_ffn1_w1, v_ffn1_w3, v_ffn1_w2, v_norm_mix_g, v_w_in, v_conv_w, v_q_norm_g, v_w_uq, v_kv_norm_g, v_w_ukv, v_out_norm_g, v_w_out, v_norm_ffn2_g, v_ffn2_w1, v_ffn2_w3, v_ffn2_w2, v_final_norm_g):
    args = dict(locals())
    wts = {n: args[n] for n in WEIGHTS}
    mom = {n: args["m_" + n] for n in WEIGHTS}
    var = {n: args["v_" + n] for n in WEIGHTS}
    ax, ay, ac = _place()
    myq = 2 * ax + ay
    me = 2 * myq + ac
    place = jnp.stack([ac, myq]).astype(jnp.int32)

    shards = {name: _working_shard(wts[name], kind).astype(BF16) for name, kind in BULK}
    first = _run_comm(_Gather([shards[b[0]] for b in W_FIRST]), name="gather_ffn1")
    w_first = _gathered_weights(W_FIRST, shards, first, myq)

    mine = jnp.concatenate([c, conv_w[0].reshape(1, 3 * CONV_W // N_CHIPS)], axis=1)
    seen = _small_allgather(jnp.pad(mine, ((0, 7), (0, 0))), name="gather_cond").reshape(N_DEV, 8, -1)[:, 0]
    c_all = jnp.pad(seen[:, :D], ((0, 8), (0, 0)))
    conv_full = jnp.transpose(seen[0::2, D:].reshape(N_CHIPS, 3, CONV_W // N_CHIPS), (1, 0, 2)).reshape(3, CONV_W)
    ada_b_q = lax.dynamic_slice_in_dim(ada_b, myq * ADA_Q, ADA_Q, axis=1)
    mod_q = _ada_forward(c_all, ada_w[0], ada_b_q)
    mod_all = _small_allgather(mod_q, name="gather_mod").reshape(N_DEV, 16, ADA_Q)
    mod_rows = jnp.transpose(mod_all[0::2, :N_DEV], (1, 0, 2)).reshape(N_DEV, N_MOD * D)
    mod = lax.dynamic_slice_in_dim(mod_rows, me, 1, axis=0).reshape(N_MOD, D)

    vec = {n: wts[n] for n in ("norm_ffn1_g", "norm_mix_g", "q_norm_g", "kv_norm_g", "out_norm_g", "norm_ffn2_g")}
    vec["final_norm_g"] = final_norm_g.reshape(1, D)
    g, delta, new_m, new_v = {}, {}, {}, {}

    def adam(names, grads, comm=None):
        views = [(lambda a: jnp.swapaxes(a, 1, 2)) if KIND.get(n) == "colsT" else (lambda a: a) for n in names]
        res, couts = _adamw([vw(wts[n]) for n, vw in zip(names, views)], [grads[n] for n in names],
                            [vw(mom[n]) for n, vw in zip(names, views)], [vw(var[n]) for n, vw in zip(names, views)],
                            name="adamw_" + names[0], comm=comm)
        for n, vw, r in zip(names, views, res):
            g[n], delta[n], new_m[n], new_v[n] = [vw(a) for a in r]
        return couts

    loss_blk, grad_x, gq, dmod, small = _local_step(
        x[0], positions[0], loss_target[0], mod, vec, conv_full, w_first, {b[0]: shards[b[0]] for b in W_REST}, place,
        tail_host=lambda prog, grads: adam([b[0] for b in G_FFN2], grads, prog))
    loss = lax.psum(loss_blk[0, 0], ("x", "y", "c"))

    rows = jnp.concatenate([dmod] + [small[n] for n, _ in SMALL], axis=1)
    width = rows.shape[1]
    fold = -(-width // (8 * 128)) * 128
    rows = jnp.pad(rows, ((0, 0), (0, 8 * fold - width))).reshape(8, fold)
    every = _small_allgather(rows, name="gather_small").reshape(N_DEV, 8 * fold)[:, :width]
    total = _sum_devices(every)[0:1]
    dmod_q = lax.dynamic_slice_in_dim(every[:, :N_MOD * D], myq * ADA_Q, ADA_Q, axis=1)
    sg = {name: gq[name] for name, *_ in BULK}
    sg["ada_w"] = _ada_wgrad(c_all, jnp.pad(dmod_q, ((0, 8), (0, 0))))
    sg["ada_b"] = total[:, :N_MOD * D]
    off = N_MOD * D
    for n, width in SMALL:
        sg[n] = total[:, off:off + width]
        off += width
    sg["conv_w"] = lax.dynamic_slice_in_dim(sg["conv_w"].reshape(3, CONV_W), myq * (CONV_W // N_CHIPS),
                                            CONV_W // N_CHIPS, axis=1)

    for name in ["ada_w"] + [b[0] for b in BULK if b not in G_FFN2]:
        adam([name], sg)
    smalls = ["ada_b"] + [n for n, _ in SMALL]

    def packed(d):
        flat = jnp.concatenate([d[n].reshape(1, -1) for n in smalls], axis=1)
        return jnp.pad(flat.reshape(-1, D), ((0, 1), (0, 0)))

    res = _adamw([packed(wts)[None]], [packed(sg)], [packed(mom)[None]], [packed(var)[None]],
                 name="adamw_small")[0][0][1:]
    off = 0
    for n in smalls:
        size = wts[n].size
        for d, r in zip((delta, new_m, new_v), res):
            d[n] = r.reshape(-1)[off:off + size].reshape(wts[n].shape)
        g[n] = sg[n].reshape(wts[n].shape)
        off += size

    return (loss, grad_x[None], *[g[n] for n in WEIGHTS], *[delta[n] for n in WEIGHTS],
            *[new_m[n] for n in WEIGHTS], *[new_v[n] for n in WEIGHTS])
```

```python
import functools

import numpy as np
import jax
import jax.numpy as jnp
from jax import lax
from jax.experimental import pallas as pl
from jax.experimental.pallas import tpu as pltpu

F32 = jnp.float32
BF16 = jnp.bfloat16
MESH = pl.DeviceIdType.MESH

D = 1024
FF = 2816
CONV_W = 512
CONV_GROUP = 64
HEADS = 4
QK_NOPE = 128
QK_ROPE = 64
V_HEAD = 128
Q_LORA = 384
KV_LORA = 256
HEAD_SLOT = 256
IN_COLS = 3 * CONV_W + Q_LORA + KV_LORA + QK_ROPE
Z_COLS = 2304
EPS = 1e-6
ROPE_THETA = 10000.0
CHUNK = 64
ATT_SCALE = (QK_NOPE + QK_ROPE) ** -0.5
NEG = -1e30
EXP2_SCALE = ATT_SCALE * 1.4426950408889634
N_MOD = 9

LR, B1, B2, AEPS, WD, STEP = 0.001, 0.9, 0.999, 1e-08, 0.01, 10

N_CHIPS = 4
N_DEV = 8
VMEM_LIMIT = 56 << 20


def _params(sem, vmem=VMEM_LIMIT):
    return pltpu.CompilerParams(dimension_semantics=sem, vmem_limit_bytes=vmem)


def _rms(v):
    return lax.rsqrt(jnp.mean(v * v, axis=-1, keepdims=True) + EPS)


def _rsum8(v):
    t, n = v.shape
    return jnp.sum(v.reshape(t // 8, 8, n), axis=0)


def _all_rows(ref):
    ref[...] = jnp.broadcast_to(jnp.sum(ref[...], axis=0, keepdims=True), ref.shape)


def _gsum(v, gmat, split=False):
    hi = v.astype(BF16)
    out = jnp.dot(hi, gmat, preferred_element_type=F32)
    if split:
        out = out + jnp.dot((v - hi.astype(F32)).astype(BF16), gmat, preferred_element_type=F32)
    return out


def _dot_nt(a, b):
    return lax.dot_general(a, b, (((1,), (1,)), ((), ())), preferred_element_type=F32)


def _silu_parts(a):
    sg = jax.nn.sigmoid(a)
    return sg, a * sg


def _rope(xr, cs, sn, lane):
    rh = jnp.where(lane < 32, -pltpu.roll(xr, 96, 1), pltpu.roll(xr, 32, 1))
    return xr * cs + rh * sn


def _rope_t(g, cs, sn, lane):
    y = g * sn
    rt = jnp.where(lane < 32, pltpu.roll(y, 96, 1), jnp.where(lane < 64, -pltpu.roll(y, 32, 1), 0.0))
    return g * cs + rt


def _row_tile(rows, pref, mult=8):
    t = min(rows, pref) // mult * mult
    while rows % t:
        t -= mult
    return t


def _place():
    return lax.axis_index("x"), lax.axis_index("y"), lax.axis_index("c")


ANY = pl.BlockSpec(memory_space=pl.ANY)


def _hosted_call(body, *, name, grid, in_specs, out_specs, out_shape, scratch_shapes, semantics, args, comm=None,
                 prefetch=()):
    n_in, n_out, n_scr, n_pf = len(in_specs), len(out_specs), len(scratch_shapes), len(prefetch)

    def call(fn, in_specs, out_specs, out_shape, scratch_shapes, semantics, operands):
        spec = pltpu.PrefetchScalarGridSpec(num_scalar_prefetch=n_pf, grid=grid, in_specs=list(in_specs),
                                            out_specs=list(out_specs), scratch_shapes=list(scratch_shapes))
        return pl.pallas_call(fn, name=name, grid_spec=spec, out_shape=list(out_shape),
                              compiler_params=_params(semantics))(*prefetch, *operands)

    if comm is None:
        return list(call(body, in_specs, out_specs, out_shape, scratch_shapes, semantics, args)), []
    n_ci, n_co = len(comm.inputs), len(comm.out_shapes)
    total = int(np.prod(grid))

    def hosted(*refs):
        tables, refs = refs[:n_pf], refs[n_pf:]
        ins, refs = refs[:n_in], refs[n_in:]
        cins, refs = refs[:n_ci], refs[n_ci:]
        outs, refs = refs[:n_out], refs[n_out:]
        couts, refs = refs[:n_co], refs[n_co:]
        scratch, sems = refs[:n_scr], refs[n_scr]
        step = pl.program_id(0)
        for ax in range(1, len(grid)):
            step = step * grid[ax] + pl.program_id(ax)

        @pl.when(step == 0)
        def _():
            comm.start(cins, couts, sems)

        body(*tables, *ins, *outs, *scratch)

        @pl.when(step == total - 1)
        def _():
            comm.finish(cins, couts, sems)

    res = call(hosted, list(in_specs) + [ANY] * n_ci, list(out_specs) + [ANY] * n_co,
               list(out_shape) + list(comm.out_shapes),
               list(scratch_shapes) + [pltpu.SemaphoreType.DMA((comm.n_sems,))],
               ("arbitrary",) * len(grid), (*args, *comm.inputs))
    return list(res[:n_out]), list(res[n_out:])


def _run_comm(comm, *, name):
    n_ci = len(comm.inputs)

    def body(*refs):
        cins, couts, sems = refs[:n_ci], refs[n_ci:-1], refs[-1]
        comm.start(cins, couts, sems)
        comm.finish(cins, couts, sems)

    return list(pl.pallas_call(
        body, name=name, out_shape=list(comm.out_shapes), in_specs=[ANY] * n_ci,
        out_specs=[ANY] * len(comm.out_shapes), scratch_shapes=[pltpu.SemaphoreType.DMA((comm.n_sems,))],
    )(*comm.inputs))


class _Gather:
    def __init__(self, slabs):
        self.inputs = list(slabs)
        self.out_shapes = [jax.ShapeDtypeStruct((N_CHIPS,) + s.shape, s.dtype) for s in slabs]
        self.n_sems = 12 * len(slabs)

    @staticmethod
    def _copy(out, sems, base, k, chip, hc, to, src=None):
        H = out.shape[1] // 2
        half = out.at[2 * chip[0] + chip[1], pl.ds(hc * H, H), :]
        return pltpu.make_async_remote_copy(
            src_ref=half if src is None else src, dst_ref=half, send_sem=sems.at[base + k],
            recv_sem=sems.at[base + 6 + k], device_id=to, device_id_type=MESH)

    def _firsts(self, src, out, sems, base):
        x, y, c = _place()
        H = src.shape[0] // 2
        chips = [(1 - x, y), (x, 1 - y), (1 - x, 1 - y)]
        return [self._copy(out, sems, base, j, (x, y), c, (*chip, c), src=src.at[pl.ds(c * H, H), :])
                for j, chip in enumerate(chips)]

    def start(self, ins, outs, sems):
        for i, (src, out) in enumerate(zip(ins, outs)):
            for cp in self._firsts(src, out, sems, 12 * i):
                cp.start()

    def finish(self, ins, outs, sems):
        x, y, c = _place()
        chips = [(1 - x, y), (x, 1 - y), (1 - x, 1 - y)]
        passed = []
        for i, out in enumerate(outs):
            for j, chip in enumerate(chips):
                self._copy(out, sems, 12 * i, j, chip, c, (x, y, c)).wait_recv()
                cp = self._copy(out, sems, 12 * i, 3 + j, chip, c, (x, y, 1 - c))
                cp.start()
                passed.append(cp)
        for i, out in enumerate(outs):
            for j, chip in enumerate(chips):
                self._copy(out, sems, 12 * i, 3 + j, chip, 1 - c, (x, y, c)).wait_recv()
        for cp in passed:
            cp.wait_send()
        for i, (src, out) in enumerate(zip(ins, outs)):
            for cp in self._firsts(src, out, sems, 12 * i):
                cp.wait_send()


class _PairExchange:
    def __init__(self, arrays):
        self.inputs = list(arrays)
        self.out_shapes = [jax.ShapeDtypeStruct((N_CHIPS, a.shape[1] // 2, a.shape[2]), a.dtype) for a in arrays]
        self.n_sems = 2 * len(arrays)

    def _copies(self, ins, outs, sems):
        x, y, c = _place()
        return [pltpu.make_async_remote_copy(
            src_ref=g.at[:, pl.ds((1 - c) * t.shape[1], t.shape[1]), :], dst_ref=t, send_sem=sems.at[2 * i],
            recv_sem=sems.at[2 * i + 1], device_id=(x, y, 1 - c), device_id_type=MESH)
            for i, (g, t) in enumerate(zip(ins, outs))]

    def start(self, ins, outs, sems):
        for cp in self._copies(ins, outs, sems):
            cp.start()

    def finish(self, ins, outs, sems):
        for cp in self._copies(ins, outs, sems):
            cp.wait()


class _ChipExchange:
    def __init__(self, arrays):
        self.inputs = list(arrays)
        self.out_shapes = [jax.ShapeDtypeStruct(a.shape, a.dtype) for a in arrays]
        self.n_sems = 6 * len(arrays)

    def _copies(self, p, t, sems, base):
        x, y, c = _place()
        myq = 2 * x + y
        chips = [(1 - x, y), (x, 1 - y), (1 - x, 1 - y)]
        sends = [pltpu.make_async_remote_copy(
            src_ref=p.at[2 * chip[0] + chip[1]], dst_ref=t.at[myq], send_sem=sems.at[base + j],
            recv_sem=sems.at[base + 3 + j], device_id=(*chip, c), device_id_type=MESH) for j, chip in enumerate(chips)]
        lands = [pltpu.make_async_remote_copy(
            src_ref=t.at[2 * chip[0] + chip[1]], dst_ref=t.at[2 * chip[0] + chip[1]], send_sem=sems.at[base + j],
            recv_sem=sems.at[base + 3 + j], device_id=(*chip, c), device_id_type=MESH) for j, chip in enumerate(chips)]
        return sends, lands

    def start(self, ins, outs, sems):
        for i, (p, t) in enumerate(zip(ins, outs)):
            for cp in self._copies(p, t, sems, 6 * i)[0]:
                cp.start()

    def finish(self, ins, outs, sems):
        for i, (p, t) in enumerate(zip(ins, outs)):
            sends, lands = self._copies(p, t, sems, 6 * i)
            for cp in lands:
                cp.wait_recv()
            for cp in sends:
                cp.wait_send()


class _SemView:
    def __init__(self, sems, base):
        self._sems, self._base = sems, base

    @property
    def at(self):
        return self

    def __getitem__(self, k):
        return self._sems.at[self._base + k]


class _Multi:
    def __init__(self, progs):
        self.progs = list(progs)
        self.inputs = [a for p in self.progs for a in p.inputs]
        self.out_shapes = [s for p in self.progs for s in p.out_shapes]
        self.n_sems = sum(p.n_sems for p in self.progs)

    def _each(self, ins, outs, sems):
        i = o = s = 0
        for p in self.progs:
            ni, no = len(p.inputs), len(p.out_shapes)
            yield p, ins[i:i + ni], outs[o:o + no], _SemView(sems, s)
            i, o, s = i + ni, o + no, s + p.n_sems

    def start(self, ins, outs, sems):
        for p, a, b, c in self._each(ins, outs, sems):
            p.start(a, b, c)

    def finish(self, ins, outs, sems):
        for p, a, b, c in self._each(ins, outs, sems):
            p.finish(a, b, c)


class _PairShare:
    def __init__(self, arrays):
        self.inputs = list(arrays)
        self.out_shapes = [jax.ShapeDtypeStruct(a.shape, a.dtype) for a in arrays]
        self.n_sems = 2 * len(arrays)

    def _copies(self, ins, outs, sems):
        x, y, c = _place()
        return [pltpu.make_async_remote_copy(
            src_ref=r, dst_ref=o, send_sem=sems.at[2 * i], recv_sem=sems.at[2 * i + 1],
            device_id=(x, y, 1 - c), device_id_type=MESH) for i, (r, o) in enumerate(zip(ins, outs))]

    def start(self, ins, outs, sems):
        for cp in self._copies(ins, outs, sems):
            cp.start()

    def finish(self, ins, outs, sems):
        for cp in self._copies(ins, outs, sems):
            cp.wait()


def _ffn_up(x, ng, sh, sc, w1, w3, *, name, comm=None):
    S = x.shape[0]
    tm, tn = _row_tile(S, 512), FF // 2

    def body(x_ref, g_ref, sh_ref, sc_ref, w1_ref, w3_ref, h_ref, a_ref, b_ref, u_ref, hs):
        @pl.when(pl.program_id(1) == 0)
        def _():
            xv = x_ref[...]
            h = ((xv * _rms(xv)) * g_ref[...]) * (1.0 + sc_ref[...]) + sh_ref[...]
            hb = h.astype(BF16)
            hs[...] = hb
            h_ref[...] = hb

        h = hs[...]
        cols = pl.ds(pl.multiple_of(pl.program_id(1) * tn, tn), tn)
        a = _dot_nt(h, w1_ref[cols, :])
        b = _dot_nt(h, w3_ref[cols, :])
        _, sa = _silu_parts(a)
        a_ref[...] = a.astype(BF16)
        b_ref[...] = b.astype(BF16)
        u_ref[...] = (sa * b).astype(BF16)

    row = pl.BlockSpec((tm, D), lambda i, j: (i, 0))
    vec = pl.BlockSpec((1, D), lambda i, j: (0, 0))
    wsp = pl.BlockSpec((FF, D), lambda i, j: (0, 0))
    osp = pl.BlockSpec((tm, tn), lambda i, j: (i, j))
    return _hosted_call(
        body, name=name, grid=(S // tm, FF // tn),
        in_specs=[row, vec, vec, vec, wsp, wsp],
        out_specs=[row, osp, osp, osp],
        out_shape=[jax.ShapeDtypeStruct((S, D), BF16)] + [jax.ShapeDtypeStruct((S, FF), BF16)] * 3,
        scratch_shapes=[pltpu.VMEM((tm, D), BF16)],
        semantics=("parallel", "arbitrary"), args=(x, ng, sh, sc, w1, w3), comm=comm)


def _ffn_down(u, w2, x, gate, *, name, comm=None):
    S = x.shape[0]
    tm = _row_tile(S, 512)

    def body(u_ref, w2_ref, x_ref, g_ref, xo_ref, f_ref):
        f = jnp.dot(u_ref[...], w2_ref[...], preferred_element_type=F32)
        xo_ref[...] = x_ref[...] + (0.5 * g_ref[...]) * f
        f_ref[...] = f.astype(BF16)

    return _hosted_call(
        body, name=name, grid=(S // tm,),
        in_specs=[pl.BlockSpec((tm, FF), lambda i: (i, 0)), pl.BlockSpec((FF, D), lambda i: (0, 0)),
                  pl.BlockSpec((tm, D), lambda i: (i, 0)), pl.BlockSpec((1, D), lambda i: (0, 0))],
        out_specs=[pl.BlockSpec((tm, D), lambda i: (i, 0))] * 2,
        out_shape=[jax.ShapeDtypeStruct((S, D), F32), jax.ShapeDtypeStruct((S, D), BF16)],
        scratch_shapes=[], semantics=("parallel",), args=(u, w2, x, gate), comm=comm)


def _ffn_bwd_du(dx, gate, f, w2, a, b, *, name, comm=None):
    S = dx.shape[0]
    tm, tn = _row_tile(S, 512), FF // 2
    n_i = S // tm

    def body(dx_ref, g_ref, f_ref, w_ref, a_ref, b_ref, df_ref, da_ref, db_ref, dg_ref, dfs):
        i, j = pl.program_id(0), pl.program_id(1)

        @pl.when((i == 0) & (j == 0))
        def _():
            dg_ref[...] = jnp.zeros_like(dg_ref)

        @pl.when(j == 0)
        def _():
            dxv = dx_ref[...]
            dfb = (dxv * (0.5 * g_ref[...])).astype(BF16)
            dfs[...] = dfb
            df_ref[...] = dfb
            dg_ref[...] += _rsum8(dxv * (0.5 * f_ref[...].astype(F32)))

        du = _dot_nt(dfs[...], w_ref[pl.ds(pl.multiple_of(j * tn, tn), tn), :])
        av = a_ref[...].astype(F32)
        sg, sa = _silu_parts(av)
        da_ref[...] = (du * b_ref[...].astype(F32) * (sg * (1.0 + av * (1.0 - sg)))).astype(BF16)
        db_ref[...] = (du * sa).astype(BF16)

        @pl.when((i == n_i - 1) & (j == FF // tn - 1))
        def _():
            _all_rows(dg_ref)

    row = pl.BlockSpec((tm, D), lambda i, j: (i, 0))
    blk = pl.BlockSpec((tm, tn), lambda i, j: (i, j))
    return _hosted_call(
        body, name=name, grid=(n_i, FF // tn),
        in_specs=[row, pl.BlockSpec((1, D), lambda i, j: (0, 0)), row,
                  pl.BlockSpec((FF, D), lambda i, j: (0, 0)), blk, blk],
        out_specs=[row, blk, blk, pl.BlockSpec((8, D), lambda i, j: (0, 0))],
        out_shape=[jax.ShapeDtypeStruct((S, D), BF16), jax.ShapeDtypeStruct((S, FF), BF16),
                   jax.ShapeDtypeStruct((S, FF), BF16), jax.ShapeDtypeStruct((8, D), F32)],
        scratch_shapes=[pltpu.VMEM((tm, D), BF16)],
        semantics=("arbitrary", "arbitrary"), args=(dx, gate, f, w2, a, b), comm=comm)


def _tn_matmul(a, b, *, tm, tn, name, comm=None):
    S, M = a.shape
    N = b.shape[1]
    ts = _row_tile(S, 2048)
    ns = S // ts

    def body(a_ref, b_ref, o_ref):
        s = pl.program_id(2)
        p = lax.dot_general(a_ref[...], b_ref[...], (((0,), (0,)), ((), ())), preferred_element_type=F32)

        @pl.when(s == 0)
        def _():
            o_ref[...] = p

        @pl.when(s > 0)
        def _():
            o_ref[...] += p

    (out,), couts = _hosted_call(
        body, name=name, grid=(M // tm, N // tn, ns),
        in_specs=[pl.BlockSpec((ts, tm), lambda i, j, s: (s, i)), pl.BlockSpec((ts, tn), lambda i, j, s: (s, j))],
        out_specs=[pl.BlockSpec((tm, tn), lambda i, j, s: (i, j))],
        out_shape=[jax.ShapeDtypeStruct((M, N), F32)],
        scratch_shapes=[], semantics=("parallel", "parallel", "arbitrary"), args=(a, b), comm=comm)
    return out if comm is None else (out, couts)


def _dh_normbwd(pairs, x, ng, sc, dx_next, *, name, comm=None):
    S = x.shape[0]
    n_p = len(pairs)
    tm = _row_tile(S, 512)
    n_i = S // tm

    def body(*refs):
        a_refs, w_refs = refs[:n_p], refs[n_p:2 * n_p]
        x_ref, g_ref, sc_ref, dxn_ref, dx_ref, p_ref = refs[2 * n_p:]
        i = pl.program_id(0)
        dh = jnp.dot(a_refs[0][...], w_refs[0][...], preferred_element_type=F32)
        for k in range(1, n_p):
            dh = dh + jnp.dot(a_refs[k][...], w_refs[k][...], preferred_element_type=F32)
        xv = x_ref[...]
        r = _rms(xv)
        xh = xv * r
        g = g_ref[...]
        dn = dh * (1.0 + sc_ref[...])
        dy = dn * g
        dx_ref[...] = dxn_ref[...] + r * (dy - xh * jnp.mean(dy * xh, axis=-1, keepdims=True))

        @pl.when(i == 0)
        def _():
            p_ref[...] = jnp.zeros_like(p_ref)

        p_ref[:, 0:D] += _rsum8(dh * (xh * g))
        p_ref[:, D:2 * D] += _rsum8(dh)
        p_ref[:, 2 * D:3 * D] += _rsum8(dn * xh)

        @pl.when(i == n_i - 1)
        def _():
            _all_rows(p_ref)

    row = pl.BlockSpec((tm, D), lambda i: (i, 0))
    vec = pl.BlockSpec((1, D), lambda i: (0, 0))
    in_specs = ([pl.BlockSpec((tm, a.shape[1]), lambda i: (i, 0)) for a, _ in pairs]
                + [pl.BlockSpec(w.shape, lambda i: (0, 0), pipeline_mode=pl.Buffered(1)) for _, w in pairs]
                + [row, vec, vec, row])
    return _hosted_call(
        body, name=name, grid=(n_i,), in_specs=in_specs,
        out_specs=[row, pl.BlockSpec((8, 3 * D), lambda i: (0, 0))],
        out_shape=[jax.ShapeDtypeStruct((S, D), F32), jax.ShapeDtypeStruct((8, 3 * D), F32)],
        scratch_shapes=[], semantics=("arbitrary",),
        args=(*[a for a, _ in pairs], *[w for _, w in pairs], x, ng, sc, dx_next), comm=comm)


def _ffn_down_loss(u, w2, x, gate, gfin, tgt):
    S = x.shape[0]
    tm = _row_tile(S, 512)
    n_i = S // tm

    def body(u_ref, w2_ref, x_ref, gt_ref, g_ref, t_ref, dx_ref, f_ref, dg_ref, loss_ref, lacc):
        i = pl.program_id(0)
        f = jnp.dot(u_ref[...], w2_ref[...], preferred_element_type=F32)
        f_ref[...] = f.astype(BF16)
        xv = x_ref[...] + (0.5 * gt_ref[...]) * f
        r = _rms(xv)
        xh = xv * r
        g = g_ref[...]
        e = xh * g - t_ref[...]
        dout = e * (1.0 / D)
        dy = dout * g
        dx_ref[...] = r * (dy - xh * jnp.mean(dy * xh, axis=-1, keepdims=True))

        @pl.when(i == 0)
        def _():
            dg_ref[...] = jnp.zeros_like(dg_ref)
            lacc[...] = jnp.zeros_like(lacc)

        dg_ref[...] += _rsum8(dout * xh)
        lacc[...] += _rsum8(e * e)

        @pl.when(i == n_i - 1)
        def _():
            _all_rows(dg_ref)
            tot = jnp.sum(jnp.sum(lacc[...], axis=0, keepdims=True), axis=1, keepdims=True)
            loss_ref[...] = jnp.broadcast_to(tot * (0.5 / D), loss_ref.shape)

    row = pl.BlockSpec((tm, D), lambda i: (i, 0))
    vec = pl.BlockSpec((1, D), lambda i: (0, 0))
    return pl.pallas_call(
        body, name="ffn2_down_loss", grid=(n_i,),
        in_specs=[pl.BlockSpec((tm, FF), lambda i: (i, 0)), pl.BlockSpec((FF, D), lambda i: (0, 0)), row, vec, vec, row],
        out_specs=[row, row, pl.BlockSpec((8, D), lambda i: (0, 0)), pl.BlockSpec((8, 128), lambda i: (0, 0))],
        out_shape=[jax.ShapeDtypeStruct((S, D), F32), jax.ShapeDtypeStruct((S, D), BF16),
                   jax.ShapeDtypeStruct((8, D), F32), jax.ShapeDtypeStruct((8, 128), F32)],
        scratch_shapes=[pltpu.VMEM((8, D), F32)],
        compiler_params=_params(("arbitrary",)),
    )(u, w2, x, gate, gfin, tgt)


def _mix_in(x, ng, sh, sc, w_in, comm=None):
    S = x.shape[0]
    tm = _row_tile(S, 512)

    def body(x_ref, g_ref, sh_ref, sc_ref, w_ref, h_ref, z_ref):
        xv = x_ref[...]
        hb = (((xv * _rms(xv)) * g_ref[...]) * (1.0 + sc_ref[...]) + sh_ref[...]).astype(BF16)
        h_ref[...] = hb
        z_ref[...] = _dot_nt(hb, w_ref[...])

    row = pl.BlockSpec((tm, D), lambda i: (i, 0))
    vec = pl.BlockSpec((1, D), lambda i: (0, 0))
    return _hosted_call(
        body, name="mix_in", grid=(S // tm,),
        in_specs=[row, vec, vec, vec, pl.BlockSpec((Z_COLS, D), lambda i: (0, 0))],
        out_specs=[row, pl.BlockSpec((tm, Z_COLS), lambda i: (i, 0))],
        out_shape=[jax.ShapeDtypeStruct((S, D), BF16), jax.ShapeDtypeStruct((S, Z_COLS), F32)],
        scratch_shapes=[], semantics=("parallel",), args=(x, ng, sh, sc, w_in), comm=comm)


def _conv_taps(u, halo, rows):
    u1 = jnp.where(rows == 0, halo[7:8, :], pltpu.roll(u, 1, 0))
    u2 = jnp.where(rows == 0, halo[6:7, :], jnp.where(rows == 1, halo[7:8, :], pltpu.roll(u, 2, 0)))
    return u1, u2


def _mix_mid(z, conv_w, gq, gkv, wuq, wukv, cs, sn, comm=None):
    S = z.shape[0]
    tm = _row_tile(S, 512)
    hb = tm // 8

    def body(z_ref, zh_ref, cw_ref, gq_ref, gkv_ref, wuq_ref, wukv_ref, cs_ref, sn_ref,
             ya_ref, q_ref, k_ref, v_ref, cqn_ref, ckvn_ref):
        i = pl.program_id(0)
        xb = z_ref[:, 0:CONV_W]
        u = z_ref[:, CONV_W:2 * CONV_W] * z_ref[:, 2 * CONV_W:3 * CONV_W]
        halo = zh_ref[:, CONV_W:2 * CONV_W] * zh_ref[:, 2 * CONV_W:3 * CONV_W]
        halo = jnp.where(i > 0, halo, 0.0)
        rows = lax.broadcasted_iota(jnp.int32, (tm, CONV_W), 0)
        u1, u2 = _conv_taps(u, halo, rows)
        y = cw_ref[0:1, :] * u2 + cw_ref[1:2, :] * u1 + cw_ref[2:3, :] * u
        ya_ref[...] = xb * y

        lane = lax.broadcasted_iota(jnp.int32, (tm, 128), 1)
        cs_v, sn_v = cs_ref[...], sn_ref[...]
        cq = z_ref[:, 3 * CONV_W:3 * CONV_W + Q_LORA]
        cqn = ((cq * _rms(cq)) * gq_ref[...]).astype(BF16)
        cqn_ref[...] = cqn
        q = _dot_nt(cqn, wuq_ref[...])
        for h in range(HEADS):
            o = h * HEAD_SLOT
            q_ref[:, o:o + 128] = q[:, o:o + 128].astype(BF16)
            q_ref[:, o + 128:o + 256] = _rope(q[:, o + 128:o + 256], cs_v, sn_v, lane).astype(BF16)

        c0 = 3 * CONV_W + Q_LORA
        ckv = z_ref[:, c0:c0 + KV_LORA]
        ckvn = ((ckv * _rms(ckv)) * gkv_ref[...]).astype(BF16)
        ckvn_ref[...] = ckvn
        kv = jnp.dot(ckvn, wukv_ref[...], preferred_element_type=F32)
        krot = _rope(z_ref[:, c0 + KV_LORA:Z_COLS], cs_v, sn_v, lane).astype(BF16)
        for h in range(HEADS):
            o = h * HEAD_SLOT
            k_ref[:, o:o + 128] = kv[:, h * 128:(h + 1) * 128].astype(BF16)
            k_ref[:, o + 128:o + 256] = krot
        v_ref[...] = kv[:, HEADS * 128:].astype(BF16)

    def rows_of(n):
        return pl.BlockSpec((tm, n), lambda i: (i, 0))

    def whole(shape):
        return pl.BlockSpec(shape, lambda i: (0, 0))

    return _hosted_call(
        body, name="mix_mid", grid=(S // tm,),
        in_specs=[rows_of(Z_COLS), pl.BlockSpec((8, Z_COLS), lambda i: (jnp.maximum(i * hb - 1, 0), 0)),
                  whole((8, CONV_W)), whole((1, Q_LORA)), whole((1, KV_LORA)),
                  whole((HEADS * HEAD_SLOT, Q_LORA)), whole((KV_LORA, 2 * HEADS * 128)),
                  rows_of(128), rows_of(128)],
        out_specs=[rows_of(CONV_W), rows_of(HEADS * HEAD_SLOT), rows_of(HEADS * HEAD_SLOT), rows_of(HEADS * V_HEAD),
                   rows_of(Q_LORA), rows_of(KV_LORA)],
        out_shape=[jax.ShapeDtypeStruct((S, CONV_W), F32), jax.ShapeDtypeStruct((S, HEADS * HEAD_SLOT), BF16),
                   jax.ShapeDtypeStruct((S, HEADS * HEAD_SLOT), BF16), jax.ShapeDtypeStruct((S, HEADS * V_HEAD), BF16),
                   jax.ShapeDtypeStruct((S, Q_LORA), BF16), jax.ShapeDtypeStruct((S, KV_LORA), BF16)],
        scratch_shapes=[], semantics=("parallel",), args=(z, z, conv_w, gq, gkv, wuq, wukv, cs, sn), comm=comm)


def _att_blocks(S):
    bk = min(1024, max(S // 4, 128))
    return bk, bk


def _pair_tables(S, k_major):
    bq, bk = _att_blocks(S)
    nq, nk = S // bq, S // bk
    vis = lambda qi, ki: ki * bk < (qi + 1) * bq
    if k_major:
        pairs = [(qi, ki) for ki in range(nk) for qi in range(nq) if vis(qi, ki)]
    else:
        pairs = [(qi, ki) for qi in range(nq) for ki in range(nk) if vis(qi, ki)]
    cols = [[p[0] for p in pairs], [p[1] for p in pairs], [int((p[1] + 1) * bk > p[0] * bq) for p in pairs]]
    return [jnp.asarray(np.array(c, np.int32)) for c in cols], len(pairs)


def _chunk_mask(r0, nr, nc):
    r = (r0 + lax.broadcasted_iota(jnp.int32, (nr, nc), 0)) // CHUNK
    c = lax.broadcasted_iota(jnp.int32, (nr, nc), 1) // CHUNK
    return c <= r


def _diag_parts(bq, bk):
    return [(0, bq // 2, bk // 2), (bq // 2, bq // 2, bk)]


def _attention(q, k, v, comm=None):
    S = q.shape[0]
    bq, bk = _att_blocks(S)
    last_k = bq // bk - 1
    tables, n_pairs = _pair_tables(S, k_major=False)

    def body(qi_ref, ki_ref, mk_ref, q_ref, k_ref, v_ref, o_ref, lse_ref, m_s, l_s, acc_s):
        p_id = pl.program_id(1)
        qi, ki = qi_ref[p_id], ki_ref[p_id]

        @pl.when(ki == 0)
        def _():
            m_s[...] = jnp.full_like(m_s, NEG)
            l_s[...] = jnp.zeros_like(l_s)
            acc_s[...] = jnp.zeros_like(acc_s)

        def update(r0, nr, nc, masked):
            rows = slice(r0, r0 + nr)
            s = _dot_nt(q_ref[rows, :], k_ref[0:nc, :])
            if masked:
                s = jnp.where(_chunk_mask(r0, nr, nc), s, NEG)
            m_prev = m_s[rows, :]
            m_new = jnp.maximum(m_prev, jnp.max(s, axis=1, keepdims=True))
            alpha = jnp.exp2((m_prev - m_new) * EXP2_SCALE)
            p = jnp.exp2((s - jnp.tile(m_new, (1, nc // 128))) * EXP2_SCALE)
            l_s[rows, :] = alpha * l_s[rows, :] + jnp.sum(p, axis=1, keepdims=True)
            acc_s[rows, :] = alpha * acc_s[rows, :] + jnp.dot(p.astype(BF16), v_ref[0:nc, :],
                                                              preferred_element_type=F32)
            m_s[rows, :] = m_new

        @pl.when(mk_ref[p_id] == 0)
        def _():
            update(0, bq, bk, False)

        @pl.when(mk_ref[p_id] == 1)
        def _():
            for part in _diag_parts(bq, bk):
                update(*part, True)

        @pl.when(ki == qi * (last_k + 1) + last_k)
        def _():
            l = l_s[...]
            o_ref[...] = acc_s[...] / l
            lse_ref[...] = m_s[...] * EXP2_SCALE + jnp.log2(l)

    return _hosted_call(
        body, name="attention", grid=(HEADS, n_pairs),
        in_specs=[pl.BlockSpec((bq, HEAD_SLOT), lambda h, p, qt, kt, mt: (qt[p], h)),
                  pl.BlockSpec((bk, HEAD_SLOT), lambda h, p, qt, kt, mt: (kt[p], h)),
                  pl.BlockSpec((bk, V_HEAD), lambda h, p, qt, kt, mt: (kt[p], h))],
        out_specs=[pl.BlockSpec((bq, V_HEAD), lambda h, p, qt, kt, mt: (qt[p], h))] * 2,
        out_shape=[jax.ShapeDtypeStruct((S, HEADS * V_HEAD), F32)] * 2,
        scratch_shapes=[pltpu.VMEM((bq, V_HEAD), F32)] * 3,
        semantics=("arbitrary", "arbitrary"), args=(q, k, v), comm=comm, prefetch=tables)


def _attention_bwd(q, k, v, do, lse2, delta):
    S = q.shape[0]
    bq, bk = _att_blocks(S)
    nq = S // bq
    tables, n_pairs = _pair_tables(S, k_major=True)

    def body(qi_ref, ki_ref, mk_ref, q_ref, k_ref, v_ref, do_ref, lse_ref, dl_ref, dq_hbm, dk_ref, dv_ref,
             dq_s, dq_b, dk_s, dv_s, sem):
        head, p_id = pl.program_id(0), pl.program_id(1)
        qi, ki = qi_ref[p_id], ki_ref[p_id]

        @pl.when(qi * bq <= ki * bk)
        def _():
            dk_s[...] = jnp.zeros_like(dk_s)
            dv_s[...] = jnp.zeros_like(dv_s)

        def update(r0, nr, nc, masked):
            rows, cols = slice(r0, r0 + nr), slice(0, nc)
            qv, kv, dov = q_ref[rows, :], k_ref[cols, :], do_ref[rows, :]
            s = _dot_nt(qv, kv)
            dp = _dot_nt(dov, v_ref[cols, :])
            if masked:
                s = jnp.where(_chunk_mask(r0, nr, nc), s, NEG)
            p = jnp.exp2(s * EXP2_SCALE - jnp.tile(lse_ref[rows, :], (1, nc // 128)))
            dv_s[cols, :] += lax.dot_general(p.astype(BF16), dov, (((0,), (0,)), ((), ())),
                                             preferred_element_type=F32)
            ds = (p * (dp - jnp.tile(dl_ref[rows, :], (1, nc // 128)))).astype(BF16)
            dk_s[cols, :] += lax.dot_general(ds, qv, (((0,), (0,)), ((), ())), preferred_element_type=F32)
            dq = jnp.dot(ds, kv, preferred_element_type=F32)
            out_rows = pl.ds(pl.multiple_of(qi * bq + r0, nr), nr)

            @pl.when(ki == 0)
            def _():
                dq_s[out_rows, :] = dq

            @pl.when(ki > 0)
            def _():
                dq_s[out_rows, :] += dq

        @pl.when(mk_ref[p_id] == 0)
        def _():
            update(0, bq, bk, False)

        @pl.when(mk_ref[p_id] == 1)
        def _():
            for part in _diag_parts(bq, bk):
                update(*part, True)

        @pl.when(qi == nq - 1)
        def _():
            dk_ref[...] = (dk_s[...] * ATT_SCALE).astype(BF16)
            dv_ref[...] = dv_s[...].astype(BF16)

        @pl.when(p_id == n_pairs - 1)
        def _():
            dq_b[...] = (dq_s[...] * ATT_SCALE).astype(BF16)
            out = pltpu.make_async_copy(
                dq_b, dq_hbm.at[:, pl.ds(pl.multiple_of(head * HEAD_SLOT, HEAD_SLOT), HEAD_SLOT)], sem)
            out.start()
            out.wait()

    grid_spec = pltpu.PrefetchScalarGridSpec(
        num_scalar_prefetch=3, grid=(HEADS, n_pairs),
        in_specs=[pl.BlockSpec((bq, HEAD_SLOT), lambda h, p, qt, kt, mt: (qt[p], h)),
                  pl.BlockSpec((bk, HEAD_SLOT), lambda h, p, qt, kt, mt: (kt[p], h)),
                  pl.BlockSpec((bk, V_HEAD), lambda h, p, qt, kt, mt: (kt[p], h)),
                  pl.BlockSpec((bq, V_HEAD), lambda h, p, qt, kt, mt: (qt[p], h)),
                  pl.BlockSpec((bq, V_HEAD), lambda h, p, qt, kt, mt: (qt[p], h)),
                  pl.BlockSpec((bq, V_HEAD), lambda h, p, qt, kt, mt: (qt[p], h))],
        out_specs=[pl.BlockSpec(memory_space=pl.ANY),
                   pl.BlockSpec((bk, HEAD_SLOT), lambda h, p, qt, kt, mt: (kt[p], h)),
                   pl.BlockSpec((bk, V_HEAD), lambda h, p, qt, kt, mt: (kt[p], h))],
        scratch_shapes=[pltpu.VMEM((S, HEAD_SLOT), F32), pltpu.VMEM((S, HEAD_SLOT), BF16),
                        pltpu.VMEM((bk, HEAD_SLOT), F32), pltpu.VMEM((bk, V_HEAD), F32), pltpu.SemaphoreType.DMA])
    return pl.pallas_call(
        body, name="attention_bwd", grid_spec=grid_spec,
        out_shape=[jax.ShapeDtypeStruct((S, HEADS * HEAD_SLOT), BF16), jax.ShapeDtypeStruct((S, HEADS * HEAD_SLOT), BF16),
                   jax.ShapeDtypeStruct((S, HEADS * V_HEAD), BF16)],
        compiler_params=_params(("arbitrary", "arbitrary")),
    )(*tables, q, k, v, do, lse2, delta)


def _group_mats():
    def blockdiag(n, g):
        idx = np.arange(n) // g
        return jnp.asarray((idx[:, None] == idx[None, :]).astype(np.float32), dtype=BF16)
    return blockdiag(CONV_W, CONV_GROUP), blockdiag(HEADS * V_HEAD, V_HEAD)


def _mix_out(ya, o, gout, w_out, x, gate, ga, gb):
    S = x.shape[0]
    tm = _row_tile(S, 512)

    def body(ya_ref, o_ref, go_ref, w_ref, x_ref, g_ref, ga_ref, gb_ref, xo_ref, yn_ref, yo_ref):
        yav, ov = ya_ref[...], o_ref[...]
        ra = lax.rsqrt(_gsum(yav * yav, ga_ref[...]) * (1.0 / CONV_GROUP) + EPS)
        rb = lax.rsqrt(_gsum(ov * ov, gb_ref[...]) * (1.0 / V_HEAD) + EPS)
        na = ((yav * ra) * go_ref[:, 0:CONV_W]).astype(BF16)
        nb = ((ov * rb) * go_ref[:, CONV_W:]).astype(BF16)
        yn_ref[:, 0:CONV_W] = na
        yn_ref[:, CONV_W:] = nb
        yo = (jnp.dot(na, w_ref[0:CONV_W, :], preferred_element_type=F32)
              + jnp.dot(nb, w_ref[CONV_W:, :], preferred_element_type=F32))
        xo_ref[...] = x_ref[...] + g_ref[...] * yo
        yo_ref[...] = yo.astype(BF16)

    row = pl.BlockSpec((tm, D), lambda i: (i, 0))
    half = pl.BlockSpec((tm, CONV_W), lambda i: (i, 0))
    vec = pl.BlockSpec((1, D), lambda i: (0, 0))
    sq = pl.BlockSpec((CONV_W, CONV_W), lambda i: (0, 0))
    return pl.pallas_call(
        body, name="mix_out", grid=(S // tm,),
        in_specs=[half, half, vec, pl.BlockSpec((D, D), lambda i: (0, 0)), row, vec, sq, sq],
        out_specs=[row, row, row],
        out_shape=[jax.ShapeDtypeStruct((S, D), F32), jax.ShapeDtypeStruct((S, D), BF16),
                   jax.ShapeDtypeStruct((S, D), BF16)],
        compiler_params=_params(("parallel",)),
    )(ya, o, gout, w_out, x, gate, ga, gb)


def _mix_out_bwd(dx, gate, yo, w_out, ya, o, gout, ga, gb, comm=None):
    S = dx.shape[0]
    tm = _row_tile(S, 512)
    n_i = S // tm

    def norm_bwd(v, dn, gain, gmat, inv_n):
        r = lax.rsqrt(_gsum(v * v, gmat) * inv_n + EPS)
        vh = v * r
        dy = dn * gain
        return r * (dy - vh * (_gsum(dy * vh, gmat) * inv_n)), dn * vh

    def body(dx_ref, g_ref, yo_ref, w_ref, ya_ref, o_ref, go_ref, ga_ref, gb_ref,
             dyo_ref, dya_ref, do_ref, dl_ref, p_ref):
        i = pl.program_id(0)
        dxv = dx_ref[...]
        dyo = (dxv * g_ref[...]).astype(BF16)
        dyo_ref[...] = dyo
        dyn = _dot_nt(dyo, w_ref[...])
        dya, dga = norm_bwd(ya_ref[...], dyn[:, 0:CONV_W], go_ref[:, 0:CONV_W], ga_ref[...], 1.0 / CONV_GROUP)
        ov = o_ref[...]
        do, dgb = norm_bwd(ov, dyn[:, CONV_W:], go_ref[:, CONV_W:], gb_ref[...], 1.0 / V_HEAD)
        dya_ref[...] = dya
        do_ref[...] = do.astype(BF16)
        dl_ref[...] = _gsum(do * ov, gb_ref[...], split=True)

        @pl.when(i == 0)
        def _():
            p_ref[...] = jnp.zeros_like(p_ref)

        p_ref[:, 0:D] += _rsum8(dxv * yo_ref[...].astype(F32))
        p_ref[:, D:D + CONV_W] += _rsum8(dga)
        p_ref[:, D + CONV_W:2 * D] += _rsum8(dgb)

        @pl.when(i == n_i - 1)
        def _():
            _all_rows(p_ref)

    row = pl.BlockSpec((tm, D), lambda i: (i, 0))
    half = pl.BlockSpec((tm, CONV_W), lambda i: (i, 0))
    vec = pl.BlockSpec((1, D), lambda i: (0, 0))
    sq = pl.BlockSpec((CONV_W, CONV_W), lambda i: (0, 0))
    return _hosted_call(
        body, name="mix_out_bwd", grid=(n_i,),
        in_specs=[row, vec, row, pl.BlockSpec((D, D), lambda i: (0, 0)), half, half, vec, sq, sq],
        out_specs=[row, half, half, half, pl.BlockSpec((8, 2 * D), lambda i: (0, 0))],
        out_shape=[jax.ShapeDtypeStruct((S, D), BF16), jax.ShapeDtypeStruct((S, CONV_W), F32),
                   jax.ShapeDtypeStruct((S, CONV_W), BF16), jax.ShapeDtypeStruct((S, CONV_W), F32),
                   jax.ShapeDtypeStruct((8, 2 * D), F32)],
        scratch_shapes=[], semantics=("arbitrary",), args=(dx, gate, yo, w_out, ya, o, gout, ga, gb), comm=comm)


MID_SUMS = 3 * CONV_W + Q_LORA + KV_LORA


def _mix_mid_bwd(z, dya, conv_w, gq, gkv, wuq, wukv, cs, sn, dq, dk, dv, comm=None):
    S = z.shape[0]
    tm = _row_tile(S, 256)
    n_i = S // tm
    hb = tm // 8
    last_blk = S // 8 - 1

    def latent_bwd(cv, dcn, gain):
        r = _rms(cv)
        ch = cv * r
        dy = dcn * gain
        return r * (dy - ch * jnp.mean(dy * ch, axis=-1, keepdims=True)), dcn * ch

    def body(z_ref, zp_ref, zn_ref, dya_ref, dyan_ref, cw_ref, gq_ref, gkv_ref, wuq_ref, wukv_ref, cs_ref, sn_ref,
             dq_ref, dk_ref, dv_ref, dz_ref, dqf_ref, dkvf_ref, p_ref):
        i = pl.program_id(0)
        xb, xc, xu = z_ref[:, 0:CONV_W], z_ref[:, CONV_W:2 * CONV_W], z_ref[:, 2 * CONV_W:3 * CONV_W]
        u = xc * xu
        halo = jnp.where(i > 0, zp_ref[:, CONV_W:2 * CONV_W] * zp_ref[:, 2 * CONV_W:3 * CONV_W], 0.0)
        rows = lax.broadcasted_iota(jnp.int32, (tm, CONV_W), 0)
        u1, u2 = _conv_taps(u, halo, rows)
        w0, w1, w2 = cw_ref[0:1, :], cw_ref[1:2, :], cw_ref[2:3, :]
        y = w0 * u2 + w1 * u1 + w2 * u
        dyav = dya_ref[...]
        dy = dyav * xb
        nxt = jnp.where(i < n_i - 1, dyan_ref[...] * zn_ref[:, 0:CONV_W], 0.0)
        dy1 = jnp.where(rows == tm - 1, nxt[0:1, :], pltpu.roll(dy, tm - 1, 0))
        dy2 = jnp.where(rows == tm - 1, nxt[1:2, :], jnp.where(rows == tm - 2, nxt[0:1, :], pltpu.roll(dy, tm - 2, 0)))
        du = w2 * dy + w1 * dy1 + w0 * dy2
        dz_ref[:, 0:CONV_W] = (dyav * y).astype(BF16)
        dz_ref[:, CONV_W:2 * CONV_W] = (du * xu).astype(BF16)
        dz_ref[:, 2 * CONV_W:3 * CONV_W] = (du * xc).astype(BF16)

        lane = lax.broadcasted_iota(jnp.int32, (tm, 128), 1)
        cs_v, sn_v = cs_ref[...], sn_ref[...]
        dkr = jnp.zeros((tm, 128), F32)
        for h in range(HEADS):
            o = h * HEAD_SLOT
            dqf_ref[:, o:o + 128] = dq_ref[:, o:o + 128]
            dqf_ref[:, o + 128:o + 256] = _rope_t(dq_ref[:, o + 128:o + 256].astype(F32), cs_v, sn_v, lane).astype(BF16)
            dkvf_ref[:, h * 128:(h + 1) * 128] = dk_ref[:, o:o + 128]
            dkr = dkr + dk_ref[:, o + 128:o + 256].astype(F32)
        dkvf_ref[:, HEADS * 128:] = dv_ref[...]

        c0 = 3 * CONV_W
        dcqn = jnp.dot(dqf_ref[...], wuq_ref[...], preferred_element_type=F32)
        dcq, dgq = latent_bwd(z_ref[:, c0:c0 + Q_LORA], dcqn, gq_ref[...])
        dz_ref[:, c0:c0 + Q_LORA] = dcq.astype(BF16)
        c1 = c0 + Q_LORA
        dckvn = _dot_nt(dkvf_ref[...], wukv_ref[...])
        dckv, dgkv = latent_bwd(z_ref[:, c1:c1 + KV_LORA], dckvn, gkv_ref[...])
        dz_ref[:, c1:c1 + KV_LORA] = dckv.astype(BF16)
        dz_ref[:, c1 + KV_LORA:Z_COLS] = _rope_t(dkr, cs_v, sn_v, lane).astype(BF16)

        @pl.when(i == 0)
        def _():
            p_ref[...] = jnp.zeros_like(p_ref)

        p_ref[:, 0:CONV_W] += _rsum8(dy * u2)
        p_ref[:, CONV_W:2 * CONV_W] += _rsum8(dy * u1)
        p_ref[:, 2 * CONV_W:3 * CONV_W] += _rsum8(dy * u)
        p_ref[:, c0:c0 + Q_LORA] += _rsum8(dgq)
        p_ref[:, c1:c1 + KV_LORA] += _rsum8(dgkv)

        @pl.when(i == n_i - 1)
        def _():
            _all_rows(p_ref)

    def rows_of(n):
        return pl.BlockSpec((tm, n), lambda i: (i, 0))

    def whole(shape):
        return pl.BlockSpec(shape, lambda i: (0, 0))

    def prev8(n):
        return pl.BlockSpec((8, n), lambda i: (jnp.maximum(i * hb - 1, 0), 0))

    def next8(n):
        return pl.BlockSpec((8, n), lambda i: (jnp.minimum((i + 1) * hb, last_blk), 0))

    return _hosted_call(
        body, name="mix_mid_bwd", grid=(n_i,),
        in_specs=[rows_of(Z_COLS), prev8(Z_COLS), next8(Z_COLS), rows_of(CONV_W), next8(CONV_W),
                  whole((8, CONV_W)), whole((1, Q_LORA)), whole((1, KV_LORA)),
                  whole((HEADS * HEAD_SLOT, Q_LORA)), whole((KV_LORA, 2 * HEADS * 128)),
                  rows_of(128), rows_of(128),
                  rows_of(HEADS * HEAD_SLOT), rows_of(HEADS * HEAD_SLOT), rows_of(HEADS * V_HEAD)],
        out_specs=[rows_of(Z_COLS), rows_of(HEADS * HEAD_SLOT), rows_of(2 * HEADS * 128), whole((8, MID_SUMS))],
        out_shape=[jax.ShapeDtypeStruct((S, Z_COLS), BF16), jax.ShapeDtypeStruct((S, HEADS * HEAD_SLOT), BF16),
                   jax.ShapeDtypeStruct((S, 2 * HEADS * 128), BF16), jax.ShapeDtypeStruct((8, MID_SUMS), F32)],
        scratch_shapes=[], semantics=("arbitrary",),
        args=(z, z, z, dya, dya, conv_w, gq, gkv, wuq, wukv, cs, sn, dq, dk, dv), comm=comm)


ADA_Q = N_MOD * D // N_CHIPS
ADA_TN = 768


def _ada_forward(c_all, ada_w_q, ada_b_q):
    def body(c_ref, w_ref, b_ref, o_ref):
        cv = c_ref[...]
        sc = (cv * jax.nn.sigmoid(cv)).astype(BF16)
        o_ref[...] = jnp.dot(sc, w_ref[...].astype(BF16), preferred_element_type=F32) + b_ref[...]

    return pl.pallas_call(
        body, name="ada_forward", grid=(ADA_Q // ADA_TN,),
        in_specs=[pl.BlockSpec((16, D), lambda j: (0, 0)), pl.BlockSpec((D, ADA_TN), lambda j: (0, j)),
                  pl.BlockSpec((1, ADA_TN), lambda j: (0, j))],
        out_specs=pl.BlockSpec((16, ADA_TN), lambda j: (0, j)),
        out_shape=jax.ShapeDtypeStruct((16, ADA_Q), F32),
        compiler_params=_params(("parallel",)),
    )(c_all, ada_w_q, ada_b_q)


def _ada_wgrad(c_all, dmod_q):
    def body(c_ref, d_ref, o_ref):
        cv = c_ref[...]
        sc = (cv * jax.nn.sigmoid(cv)).astype(BF16)
        o_ref[...] = lax.dot_general(sc, d_ref[...].astype(BF16), (((0,), (0,)), ((), ())),
                                     preferred_element_type=F32)

    return pl.pallas_call(
        body, name="ada_wgrad", grid=(ADA_Q // ADA_TN,),
        in_specs=[pl.BlockSpec((16, D), lambda j: (0, 0)), pl.BlockSpec((16, ADA_TN), lambda j: (0, j))],
        out_specs=pl.BlockSpec((D, ADA_TN), lambda j: (0, j)),
        out_shape=jax.ShapeDtypeStruct((D, ADA_Q), F32),
        compiler_params=_params(("parallel",)),
    )(c_all, dmod_q)


def _sum_devices(parts):
    n = parts.shape[1]

    def body(p_ref, o_ref):
        o_ref[...] = jnp.broadcast_to(jnp.sum(p_ref[...], axis=0, keepdims=True), o_ref.shape)

    return pl.pallas_call(
        body, name="sum_devices",
        in_specs=[pl.BlockSpec((N_DEV, n), lambda: (0, 0))], out_specs=pl.BlockSpec((N_DEV, n), lambda: (0, 0)),
        out_shape=jax.ShapeDtypeStruct((N_DEV, n), F32),
    )(parts)


def _adamw(ws, gs, ms, vs, *, name, comm=None):
    n = len(ws)
    _, rows, cols = ws[0].shape
    tr = _row_tile(rows, 256)

    def body(*refs):
        ins, outs = refs[:4 * n], refs[4 * n:]
        for k in range(n):
            w_ref, g_ref, m_ref, v_ref = ins[k], ins[n + k], ins[2 * n + k], ins[3 * n + k]
            go_ref, d_ref, mo_ref, vo_ref = outs[4 * k:4 * k + 4]
            gv = g_ref[...]
            mn = B1 * m_ref[0] + (1.0 - B1) * gv
            vn = B2 * v_ref[0] + (1.0 - B2) * (gv * gv)
            m_hat = mn / (1.0 - B1 ** STEP)
            v_hat = vn / (1.0 - B2 ** STEP)
            go_ref[0] = gv
            d_ref[0] = -LR * (m_hat / (jnp.sqrt(v_hat) + AEPS) + WD * w_ref[0])
            mo_ref[0] = mn
            vo_ref[0] = vn

    blk = pl.BlockSpec((1, tr, cols), lambda i: (0, i, 0))
    gblk = pl.BlockSpec((tr, cols), lambda i: (i, 0))
    res, couts = _hosted_call(
        body, name=name, grid=(rows // tr,),
        in_specs=[blk] * n + [gblk] * n + [blk] * 2 * n, out_specs=[blk] * 4 * n,
        out_shape=[jax.ShapeDtypeStruct((1, rows, cols), F32)] * 4 * n,
        scratch_shapes=[], semantics=("parallel",), args=(*ws, *gs, *ms, *vs), comm=comm)
    return [res[4 * k:4 * k + 4] for k in range(n)], couts


def _small_allgather(v, *, name):
    m, n = v.shape

    def body(x_ref, out_ref, send_sems, recv_sems, local_sem):
        x, y, c = _place()
        me, sibling = (x, y, c), (x, y, 1 - c)
        chips = [(1 - x, y), (x, 1 - y), (1 - x, 1 - y)]

        def rows(px, py, pc):
            return out_ref.at[pl.ds((4 * px + 2 * py + pc) * m, m), :]

        def copy(k, block, to, src=None):
            return pltpu.make_async_remote_copy(
                src_ref=rows(*block) if src is None else src, dst_ref=rows(*block),
                send_sem=send_sems.at[k], recv_sem=recv_sems.at[k], device_id=to, device_id_type=MESH)

        mine = pltpu.make_async_copy(x_ref, rows(*me), local_sem)
        mine.start()
        first = [copy(0, me, sibling, src=x_ref)]
        first += [copy(1 + j, me, (*chip, c), src=x_ref) for j, chip in enumerate(chips)]
        for cp in first:
            cp.start()
        passed = [copy(4 + j, (*chip, c), sibling) for j, chip in enumerate(chips)]
        for j, chip in enumerate(chips):
            copy(1 + j, (*chip, c), me).wait_recv()
            passed[j].start()
        copy(0, sibling, me).wait_recv()
        for j, chip in enumerate(chips):
            copy(4 + j, (*chip, 1 - c), me).wait_recv()
        for cp in first + passed:
            cp.wait_send()
        mine.wait()

    return pl.pallas_call(
        body, name=name,
        out_shape=jax.ShapeDtypeStruct((N_DEV * m, n), v.dtype),
        in_specs=[pl.BlockSpec(memory_space=pltpu.VMEM)], out_specs=pl.BlockSpec(memory_space=pltpu.VMEM),
        scratch_shapes=[pltpu.SemaphoreType.DMA((7,)), pltpu.SemaphoreType.DMA((7,)), pltpu.SemaphoreType.DMA],
    )(v)


ADD_BLOCKS = 2


def _pair_add(place, gs, ts, *, name):
    n_a = len(gs)

    def body(pl_ref, *refs):
        g_refs, t_refs = refs[:n_a], refs[n_a:2 * n_a]
        pf_refs, pb_refs = refs[2 * n_a:3 * n_a], refs[3 * n_a:]
        for g_ref, t_ref, pf_ref, pb_ref in zip(g_refs, t_refs, pf_refs, pb_refs):
            s = g_ref[...] + t_ref[...]
            pf_ref[...] = s
            pb_ref[...] = s.astype(BF16)

    def blk(t, own_half):
        tr = t.shape[1] // ADD_BLOCKS
        if own_half:
            return pl.BlockSpec((1, tr, t.shape[2]), lambda q, r, p: (q, p[0] * ADD_BLOCKS + r, 0))
        return pl.BlockSpec((1, tr, t.shape[2]), lambda q, r, p: (q, r, 0))

    grid_spec = pltpu.PrefetchScalarGridSpec(
        num_scalar_prefetch=1, grid=(N_CHIPS, ADD_BLOCKS),
        in_specs=[blk(t, True) for t in ts] + [blk(t, False) for t in ts],
        out_specs=[blk(t, False) for t in ts] * 2)
    res = pl.pallas_call(
        body, name=name, grid_spec=grid_spec,
        out_shape=[jax.ShapeDtypeStruct(t.shape, F32) for t in ts] + [jax.ShapeDtypeStruct(t.shape, BF16) for t in ts],
        compiler_params=_params(("parallel", "parallel")),
    )(place, *gs, *ts)
    return list(res[:n_a]), list(res[n_a:])


def _chip_add(place, pfs, ts, *, name):
    n_a = len(pfs)

    def body(pl_ref, *refs):
        pf_refs, t_refs, o_refs = refs[:n_a], refs[n_a:4 * n_a], refs[4 * n_a:]
        for i, (pf_ref, o_ref) in enumerate(zip(pf_refs, o_refs)):
            t1, t2, t3 = t_refs[3 * i:3 * i + 3]
            o_ref[...] = ((pf_ref[0] + t1[0].astype(F32)) + t2[0].astype(F32)) + t3[0].astype(F32)

    def slot(t, j):
        return pl.BlockSpec((1, t.shape[1] // ADD_BLOCKS, t.shape[2]), lambda r, p: (p[1] ^ j, r, 0))

    grid_spec = pltpu.PrefetchScalarGridSpec(
        num_scalar_prefetch=1, grid=(ADD_BLOCKS,),
        in_specs=[slot(t, 0) for t in pfs] + [slot(t, j) for t in ts for j in (1, 2, 3)],
        out_specs=[pl.BlockSpec((t.shape[1] // ADD_BLOCKS, t.shape[2]), lambda r, p: (r, 0)) for t in pfs])
    res = pl.pallas_call(
        body, name=name, grid_spec=grid_spec,
        out_shape=[jax.ShapeDtypeStruct(t.shape[1:], F32) for t in pfs],
        compiler_params=_params(("parallel",)),
    )(place, *pfs, *[t for t in ts for _ in range(3)])
    return list(res)


BULK = [("ffn1_w1", "colsT"), ("ffn1_w3", "colsT"), ("ffn1_w2", "rows"), ("w_in", "cols"), ("w_uq", "colsT"),
        ("w_ukv", "cols"), ("w_out", "rows"), ("ffn2_w1", "colsT"), ("ffn2_w3", "colsT"), ("ffn2_w2", "rows")]
KIND = dict(BULK)


def _group(*names):
    return [b for b in BULK if b[0] in names]


W_FIRST = _group("ffn1_w1", "ffn1_w3")
W_REST = [b for b in BULK if b not in W_FIRST]
W_MIX = _group("ffn1_w2", "w_in", "w_uq", "w_ukv", "w_out")
W_FFN2 = _group("ffn2_w1", "ffn2_w3", "ffn2_w2")
G_FFN2 = _group("ffn2_w1", "ffn2_w3", "ffn2_w2")
G_MIX = _group("w_in", "w_uq", "w_ukv", "w_out")
G_FFN1 = _group("ffn1_w1", "ffn1_w3", "ffn1_w2")


def _gathered_weights(specs, shards, got, myq):
    out = {}
    for (name, kind), part in zip(specs, got):
        part = lax.dynamic_update_slice_in_dim(part, shards[name][None], myq, axis=0)
        out[name] = _full_weight(part, kind)
    return out


def _working_shard(w, kind):
    return jnp.swapaxes(w, 1, 2)[0] if kind == "colsT" else w[0]


def _full_weight(parts, kind):
    if kind == "cols":
        return jnp.transpose(parts, (1, 0, 2)).reshape(parts.shape[1], -1)
    return parts.reshape(-1, parts.shape[2])


def _quarters(g, kind):
    if kind == "cols":
        k, n = g.shape
        return jnp.transpose(g.reshape(k, N_CHIPS, n // N_CHIPS), (1, 0, 2))
    return g.reshape(N_CHIPS, g.shape[0] // N_CHIPS, g.shape[1])


def _pad_heads(w_uq_t):
    w = w_uq_t.reshape(HEADS, QK_NOPE + QK_ROPE, Q_LORA)
    return jnp.pad(w, ((0, 0), (0, HEAD_SLOT - QK_NOPE - QK_ROPE), (0, 0))).reshape(HEADS * HEAD_SLOT, Q_LORA)


def _unpad_heads(g):
    return g.reshape(HEADS, HEAD_SLOT, Q_LORA)[:, :QK_NOPE + QK_ROPE].reshape(HEADS * (QK_NOPE + QK_ROPE), Q_LORA)


def _split_kv(w_ukv):
    return jnp.transpose(w_ukv.reshape(KV_LORA, HEADS, 2, 128), (0, 2, 1, 3)).reshape(KV_LORA, 2 * HEADS * 128)


def _merge_kv(g):
    return jnp.transpose(g.reshape(KV_LORA, 2, HEADS, 128), (0, 2, 1, 3)).reshape(KV_LORA, 2 * HEADS * 128)


def _rope_tables(positions):
    inv_freq = ROPE_THETA ** (-jnp.arange(0, QK_ROPE, 2, dtype=F32) / QK_ROPE)
    ang = positions.astype(F32)[:, None] * inv_freq
    cos, sin, zero = jnp.cos(ang), jnp.sin(ang), jnp.zeros((positions.shape[0], 64), F32)
    return jnp.concatenate([cos, cos, zero], axis=1), jnp.concatenate([sin, sin, zero], axis=1)


def _assemble(place, specs, rhs, others):
    south = place[0] == 0
    return {b[0]: jnp.concatenate([jnp.where(south, rh, ot), jnp.where(south, ot, rh)], axis=0)
            for b, rh, ot in zip(specs, rhs, others)}


def _local_step(x, positions, target, mod, vec, conv_w, w_first, rest_shards, place, tail_host=None):
    row = lambda k: mod[k:k + 1]
    sh1, sc1, g1, sh2, sc2, g2, sh3, sc3, g3 = [row(k) for k in range(N_MOD)]
    cs, sn = _rope_tables(positions)
    cw8 = jnp.pad(conv_w, ((0, 5), (0, 0)))
    ga, gb = _group_mats()
    dist = place is not None
    comm = lambda prog: prog if dist else None

    w = dict(w_first) if dist else {**rest_shards, **w_first}

    def gather(specs):
        return _Gather([rest_shards[b[0]] for b in specs]) if dist else None

    def arrived(specs, got):
        if dist:
            w.update(_gathered_weights(specs, rest_shards, got, place[1]))

    (h1, a1, b1, u1), got = _ffn_up(x, vec["norm_ffn1_g"], sh1, sc1, w["ffn1_w1"], w["ffn1_w3"],
                                    name="ffn1_up", comm=gather(W_MIX))
    arrived(W_MIX, got)
    w_in = jnp.pad(w["w_in"].T, ((0, Z_COLS - IN_COLS), (0, 0)))
    wuq = _pad_heads(w["w_uq"])
    wukv = _split_kv(w["w_ukv"])
    (x1, f1), _ = _ffn_down(u1, w["ffn1_w2"], x, g1, name="ffn1_down")
    (h2, z), _ = _mix_in(x1, vec["norm_mix_g"], sh2, sc2, w_in)
    (ya, q, k, v, cqn, ckvn), _ = _mix_mid(z, cw8, vec["q_norm_g"], vec["kv_norm_g"], wuq, wukv, cs, sn)
    (o, lse), got = _attention(q, k, v, comm=gather(W_FFN2))
    arrived(W_FFN2, got)
    x2, yn, yo = _mix_out(ya, o, vec["out_norm_g"], w["w_out"], x1, g2, ga, gb)
    (h3, a3, b3, u3), _ = _ffn_up(x2, vec["norm_ffn2_g"], sh3, sc3, w["ffn2_w1"], w["ffn2_w3"], name="ffn2_up")
    dx3, f3, dgfin, loss_blk = _ffn_down_loss(u3, w["ffn2_w2"], x2, g3, vec["final_norm_g"], target)

    grads, reduced = {}, {}

    def tn(a, b, tm, tn_, name, prog=None):
        if prog is None:
            return _tn_matmul(a, b, tm=tm, tn=tn_, name=name), None
        return _tn_matmul(a, b, tm=tm, tn=tn_, name=name, comm=prog)

    def slab_of(specs):
        return [_quarters(grads[n], kind) for n, kind in specs]

    (df3, da3, db3, dg3), _ = _ffn_bwd_du(dx3, g3, f3, w["ffn2_w2"], a3, b3, name="ffn2_bwd_du")
    grads["ffn2_w2"], _ = tn(u3, df3, FF // 2, D, "ffn2_dw2")
    grads["ffn2_w1"], _ = tn(da3, h3, FF // 2, D, "ffn2_dw1")
    grads["ffn2_w3"], _ = tn(db3, h3, FF // 2, D, "ffn2_dw3")
    (dx2, s3), _ = _dh_normbwd([(da3, w["ffn2_w1"]), (db3, w["ffn2_w3"])], x2, vec["norm_ffn2_g"], sc3, dx3,
                               name="ffn2_bwd_dh")

    p1 = slab_of(G_FFN2) if dist else None
    (dyo, dya, do, delta, s_out), t1 = _mix_out_bwd(dx2, g2, yo, w["w_out"], ya, o, vec["out_norm_g"], ga, gb,
                                                    comm=comm(_PairExchange(p1) if dist else None))
    grads["w_out"], _ = tn(yn, dyo, D, D, "dw_out")
    if dist:
        pf1, pb1 = _pair_add(place, p1, t1, name="ffn2g_pair_add")
    dq, dk, dv = _attention_bwd(q, k, v, do, lse, delta)
    (dz, dqf, dkvf, s_mid), t2 = _mix_mid_bwd(z, dya, cw8, vec["q_norm_g"], vec["kv_norm_g"], wuq, wukv, cs, sn,
                                              dq, dk, dv, comm=comm(_ChipExchange(pb1) if dist else None))
    g_uq, _ = tn(dqf, cqn, HEADS * HEAD_SLOT, Q_LORA, "dw_uq")
    g_ukv, _ = tn(ckvn, dkvf, KV_LORA, 2 * HEADS * 128, "dw_ukv")
    grads["w_uq"], grads["w_ukv"] = _unpad_heads(g_uq), _merge_kv(g_ukv)
    g_in, _ = tn(h2, dz, D, Z_COLS // 2, "dw_in")
    grads["w_in"] = g_in[:, :IN_COLS]
    (dx1, s2), _ = _dh_normbwd([(dz, w_in)], x1, vec["norm_mix_g"], sc2, dx2, name="mix_bwd_dh")

    p2 = slab_of(G_MIX) if dist else None
    rh_ffn2 = _chip_add(place, pf1, t2, name="ffn2g_chip_add") if dist else None
    (df1, da1, db1, dg1), got = _ffn_bwd_du(dx1, g1, f1, w["ffn1_w2"], a1, b1, name="ffn1_bwd_du",
                                            comm=comm(_Multi([_PairExchange(p2), _PairShare(rh_ffn2)]) if dist else None))
    if dist:
        t1 = got[:len(p2)]
        reduced.update(_assemble(place, G_FFN2, rh_ffn2, got[len(p2):]))
    dh_pairs = [(da1, w["ffn1_w1"]), (db1, w["ffn1_w3"])]
    if dist:
        pf2, pb2 = _pair_add(place, p2, t1, name="mixg_pair_add")
        grads["ffn1_w2"], t2 = tn(u1, df1, FF // 2, D, "ffn1_dw2", _ChipExchange(pb2))
        rh_mix = _chip_add(place, pf2, t2, name="mixg_chip_add")
        q_w2 = [_quarters(grads["ffn1_w2"], KIND["ffn1_w2"])]
        grads["ffn1_w1"], got = tn(da1, h1, FF // 2, D, "ffn1_dw1", _Multi([_PairShare(rh_mix), _PairExchange(q_w2)]))
        reduced.update(_assemble(place, G_MIX, rh_mix, got[:len(rh_mix)]))
        pf_w2, pb_w2 = _pair_add(place, q_w2, got[len(rh_mix):], name="ffn1w2_pair_add")
        q_w1 = [_quarters(grads["ffn1_w1"], KIND["ffn1_w1"])]
        grads["ffn1_w3"], got = tn(db1, h1, FF // 2, D, "ffn1_dw3", _Multi([_ChipExchange(pb_w2), _PairExchange(q_w1)]))
        t2_w2 = got[:1]
        pf_w1, pb_w1 = _pair_add(place, q_w1, got[1:], name="ffn1w1_pair_add")
        q_w3 = [_quarters(grads["ffn1_w3"], KIND["ffn1_w3"])]
        (dx0, s1), got = _dh_normbwd(dh_pairs, x, vec["norm_ffn1_g"], sc1, dx1, name="ffn1_bwd_dh",
                                     comm=_Multi([_ChipExchange(pb_w1), _PairExchange(q_w3)]))
        t2_w1 = got[:1]
        pf_w3, pb_w3 = _pair_add(place, q_w3, got[1:], name="ffn1w3_pair_add")
        t2_w3 = tail_host(_ChipExchange(pb_w3), reduced)
        rh = _chip_add(place, pf_w1 + pf_w3 + pf_w2, t2_w1 + t2_w3 + t2_w2, name="ffn1g_chip_add")
        reduced.update(_assemble(place, G_FFN1, rh, _run_comm(_PairShare(rh), name="ffn1g_pair_share")))
    else:
        grads["ffn1_w2"], _ = tn(u1, df1, FF // 2, D, "ffn1_dw2")
        grads["ffn1_w1"], _ = tn(da1, h1, FF // 2, D, "ffn1_dw1")
        grads["ffn1_w3"], _ = tn(db1, h1, FF // 2, D, "ffn1_dw3")
        (dx0, s1), _ = _dh_normbwd(dh_pairs, x, vec["norm_ffn1_g"], sc1, dx1, name="ffn1_bwd_dh")
        reduced = grads

    def part(s, k):
        return s[0:1, k * D:(k + 1) * D]

    dmod = jnp.concatenate([part(s1, 1), part(s1, 0), dg1[0:1], part(s2, 1), part(s2, 0), part(s_out, 0),
                            part(s3, 1), part(s3, 0), dg3[0:1]], axis=1)
    small = {"norm_ffn1_g": part(s1, 2), "norm_mix_g": part(s2, 2), "out_norm_g": part(s_out, 1),
             "norm_ffn2_g": part(s3, 2), "final_norm_g": dgfin[0:1],
             "q_norm_g": s_mid[0:1, 3 * CONV_W:3 * CONV_W + Q_LORA],
             "kv_norm_g": s_mid[0:1, 3 * CONV_W + Q_LORA:MID_SUMS], "conv_w": s_mid[0:1, 0:3 * CONV_W]}
    return loss_blk, dx0, reduced, dmod, small


SMALL = [("norm_ffn1_g", D), ("norm_mix_g", D), ("out_norm_g", D), ("norm_ffn2_g", D), ("final_norm_g", D),
         ("q_norm_g", Q_LORA), ("kv_norm_g", KV_LORA), ("conv_w", 3 * CONV_W)]
WEIGHTS = ['ada_w', 'ada_b', 'norm_ffn1_g', 'ffn1_w1', 'ffn1_w3', 'ffn1_w2', 'norm_mix_g', 'w_in', 'conv_w',
           'q_norm_g', 'w_uq', 'kv_norm_g', 'w_ukv', 'out_norm_g', 'w_out', 'norm_ffn2_g', 'ffn2_w1', 'ffn2_w3',
           'ffn2_w2', 'final_norm_g']


def kernel(x, c, positions, ada_w, ada_b, norm_ffn1_g, ffn1_w1, ffn1_w3, ffn1_w2, norm_mix_g, w_in, conv_w, q_norm_g, w_uq, kv_norm_g, w_ukv, out_norm_g, w_out, norm_ffn2_g, ffn2_w1, ffn2_w3, ffn2_w2, final_norm_g, loss_target, m_ada_w, m_ada_b, m_norm_ffn1_g, m_ffn1_w1, m_ffn1_w3, m_ffn1_w2, m_norm_mix_g, m_w_in, m_conv_w, m_q_norm_g, m_w_uq, m_kv_norm_g, m_w_ukv, m_out_norm_g, m_w_out, m_norm_ffn2_g, m_ffn2_w1, m_ffn2_w3, m_ffn2_w2, m_final_norm_g, v_ada_w, v_ada_b, v_norm_ffn1_g, v_ffn1_w1, v_ffn1_w3, v_ffn1_w2, v_norm_mix_g, v_w_in, v_conv_w, v_q_norm_g, v_w_uq, v_kv_norm_g, v_w_ukv, v_out_norm_g, v_w_out, v_norm_ffn2_g, v_ffn2_w1, v_ffn2_w3, v_ffn2_w2, v_final_norm_g):
    args = dict(locals())
    wts = {n: args[n] for n in WEIGHTS}
    mom = {n: args["m_" + n] for n in WEIGHTS}
    var = {n: args["v_" + n] for n in WEIGHTS}
    ax, ay, ac = _place()
    myq = 2 * ax + ay
    me = 2 * myq + ac
    place = jnp.stack([ac, myq]).astype(jnp.int32)

    shards = {name: _working_shard(wts[name], kind).astype(BF16) for name, kind in BULK}
    first = _run_comm(_Gather([shards[b[0]] for b in W_FIRST]), name="gather_ffn1")
    w_first = _gathered_weights(W_FIRST, shards, first, myq)

    mine = jnp.concatenate([c, conv_w[0].reshape(1, 3 * CONV_W // N_CHIPS)], axis=1)
    seen = _small_allgather(jnp.pad(mine, ((0, 7), (0, 0))), name="gather_cond").reshape(N_DEV, 8, -1)[:, 0]
    c_all = jnp.pad(seen[:, :D], ((0, 8), (0, 0)))
    conv_full = jnp.transpose(seen[0::2, D:].reshape(N_CHIPS, 3, CONV_W // N_CHIPS), (1, 0, 2)).reshape(3, CONV_W)
    ada_b_q = lax.dynamic_slice_in_dim(ada_b, myq * ADA_Q, ADA_Q, axis=1)
    mod_q = _ada_forward(c_all, ada_w[0], ada_b_q)
    mod_all = _small_allgather(mod_q, name="gather_mod").reshape(N_DEV, 16, ADA_Q)
    mod_rows = jnp.transpose(mod_all[0::2, :N_DEV], (1, 0, 2)).reshape(N_DEV, N_MOD * D)
    mod = lax.dynamic_slice_in_dim(mod_rows, me, 1, axis=0).reshape(N_MOD, D)

    vec = {n: wts[n] for n in ("norm_ffn1_g", "norm_mix_g", "q_norm_g", "kv_norm_g", "out_norm_g", "norm_ffn2_g")}
    vec["final_norm_g"] = final_norm_g.reshape(1, D)
    g, delta, new_m, new_v = {}, {}, {}, {}

    def adam(names, grads, comm=None):
        views = [(lambda a: jnp.swapaxes(a, 1, 2)) if KIND.get(n) == "colsT" else (lambda a: a) for n in names]
        res, couts = _adamw([vw(wts[n]) for n, vw in zip(names, views)], [grads[n] for n in names],
                            [vw(mom[n]) for n, vw in zip(names, views)], [vw(var[n]) for n, vw in zip(names, views)],
                            name="adamw_" + names[0], comm=comm)
        for n, vw, r in zip(names, views, res):
            g[n], delta[n], new_m[n], new_v[n] = [vw(a) for a in r]
        return couts

    loss_blk, grad_x, gq, dmod, small = _local_step(
        x[0], positions[0], loss_target[0], mod, vec, conv_full, w_first, {b[0]: shards[b[0]] for b in W_REST}, place,
        tail_host=lambda prog, grads: adam([b[0] for b in G_FFN2], grads, prog))
    loss = lax.psum(loss_blk[0, 0], ("x", "y", "c"))

    rows = jnp.concatenate([dmod] + [small[n] for n, _ in SMALL], axis=1)
    width = rows.shape[1]
    fold = -(-width // (8 * 128)) * 128
    rows = jnp.pad(rows, ((0, 0), (0, 8 * fold - width))).reshape(8, fold)
    every = _small_allgather(rows, name="gather_small").reshape(N_DEV, 8 * fold)[:, :width]
    total = _sum_devices(every)[0:1]
    dmod_q = lax.dynamic_slice_in_dim(every[:, :N_MOD * D], myq * ADA_Q, ADA_Q, axis=1)
    sg = {name: gq[name] for name, *_ in BULK}
    sg["ada_w"] = _ada_wgrad(c_all, jnp.pad(dmod_q, ((0, 8), (0, 0))))
    sg["ada_b"] = total[:, :N_MOD * D]
    off = N_MOD * D
    for n, width in SMALL:
        sg[n] = total[:, off:off + width]
        off += width
    sg["conv_w"] = lax.dynamic_slice_in_dim(sg["conv_w"].reshape(3, CONV_W), myq * (CONV_W // N_CHIPS),
                                            CONV_W // N_CHIPS, axis=1)

    for name in ["ada_w"] + [b[0] for b in BULK if b not in G_FFN2]:
        adam([name], sg)
    smalls = ["ada_b"] + [n for n, _ in SMALL]

    def packed(d):
        flat = jnp.concatenate([d[n].reshape(1, -1) for n in smalls], axis=1)
        return jnp.pad(flat.reshape(-1, D), ((0, 1), (0, 0)))

    res = _adamw([packed(wts)[None]], [packed(sg)], [packed(mom)[None]], [packed(var)[None]],
                 name="adamw_small")[0][0][1:]
    off = 0
    for n in smalls:
        size = wts[n].size
        for d, r in zip((delta, new_m, new_v), res):
            d[n] = r.reshape(-1)[off:off + size].reshape(wts[n].shape)
        g[n] = sg[n].reshape(wts[n].shape)
        off += size

    return (loss, grad_x[None], *[g[n] for n in WEIGHTS], *[delta[n] for n in WEIGHTS],
            *[new_m[n] for n in WEIGHTS], *[new_v[n] for n in WEIGHTS])
```

```python
import numpy as np
import jax
import jax.numpy as jnp
from jax import lax
from jax.experimental import pallas as pl
from jax.experimental.pallas import tpu as pltpu

F32 = jnp.float32
BF16 = jnp.bfloat16
MESH = pl.DeviceIdType.MESH

D = 1024
FF = 2816
CONV_W = 512
CONV_GROUP = 64
HEADS = 4
QK_NOPE = 128
QK_ROPE = 64
V_HEAD = 128
Q_LORA = 384
KV_LORA = 256
HEAD_SLOT = 256
IN_COLS = 3 * CONV_W + Q_LORA + KV_LORA + QK_ROPE
Z_COLS = 2304
EPS = 1e-6
ROPE_THETA = 10000.0
CHUNK = 64
ATT_SCALE = (QK_NOPE + QK_ROPE) ** -0.5
NEG = -1e30
EXP2_SCALE = ATT_SCALE * 1.4426950408889634
N_MOD = 9

LR, B1, B2, AEPS, WD, STEP = 0.001, 0.9, 0.999, 1e-08, 0.01, 10

N_CHIPS = 4
N_DEV = 8
VMEM_LIMIT = 56 << 20


def _params(sem, vmem=VMEM_LIMIT):
    return pltpu.CompilerParams(dimension_semantics=sem, vmem_limit_bytes=vmem)


def _rms(v):
    return lax.rsqrt(jnp.mean(v * v, axis=-1, keepdims=True) + EPS)


def _rsum8(v):
    t, n = v.shape
    return jnp.sum(v.reshape(t // 8, 8, n), axis=0)


def _all_rows(ref):
    ref[...] = jnp.broadcast_to(jnp.sum(ref[...], axis=0, keepdims=True), ref.shape)


def _gsum(v, gmat, split=False):
    hi = v.astype(BF16)
    out = jnp.dot(hi, gmat, preferred_element_type=F32)
    if split:
        out = out + jnp.dot((v - hi.astype(F32)).astype(BF16), gmat, preferred_element_type=F32)
    return out


def _dot_nt(a, b):
    return lax.dot_general(a, b, (((1,), (1,)), ((), ())), preferred_element_type=F32)


def _silu_parts(a):
    sg = jax.nn.sigmoid(a)
    return sg, a * sg


def _rope(xr, cs, sn, lane):
    rh = jnp.where(lane < 32, -pltpu.roll(xr, 96, 1), pltpu.roll(xr, 32, 1))
    return xr * cs + rh * sn


def _rope_t(g, cs, sn, lane):
    y = g * sn
    rt = jnp.where(lane < 32, pltpu.roll(y, 96, 1), jnp.where(lane < 64, -pltpu.roll(y, 32, 1), 0.0))
    return g * cs + rt


def _row_tile(rows, pref, mult=8):
    t = min(rows, pref) // mult * mult
    while rows % t:
        t -= mult
    return t


def _place():
    return lax.axis_index("x"), lax.axis_index("y"), lax.axis_index("c")


ANY = pl.BlockSpec(memory_space=pl.ANY)


def _hosted_call(body, *, name, grid, in_specs, out_specs, out_shape, scratch_shapes, semantics, args, comm=None,
                 prefetch=()):
    n_in, n_out, n_scr, n_pf = len(in_specs), len(out_specs), len(scratch_shapes), len(prefetch)

    def call(fn, in_specs, out_specs, out_shape, scratch_shapes, semantics, operands):
        spec = pltpu.PrefetchScalarGridSpec(num_scalar_prefetch=n_pf, grid=grid, in_specs=list(in_specs),
                                            out_specs=list(out_specs), scratch_shapes=list(scratch_shapes))
        return pl.pallas_call(fn, name=name, grid_spec=spec, out_shape=list(out_shape),
                              compiler_params=_params(semantics))(*prefetch, *operands)

    if comm is None:
        return list(call(body, in_specs, out_specs, out_shape, scratch_shapes, semantics, args)), []
    n_ci, n_co = len(comm.inputs), len(comm.out_shapes)
    total = int(np.prod(grid))

    def hosted(*refs):
        tables, refs = refs[:n_pf], refs[n_pf:]
        ins, refs = refs[:n_in], refs[n_in:]
        cins, refs = refs[:n_ci], refs[n_ci:]
        outs, refs = refs[:n_out], refs[n_out:]
        couts, refs = refs[:n_co], refs[n_co:]
        scratch, sems = refs[:n_scr], refs[n_scr]
        step = pl.program_id(0)
        for ax in range(1, len(grid)):
            step = step * grid[ax] + pl.program_id(ax)

        @pl.when(step == 0)
        def _():
            comm.start(cins, couts, sems)

        body(*tables, *ins, *outs, *scratch)

        @pl.when(step == total - 1)
        def _():
            comm.finish(cins, couts, sems)

    res = call(hosted, list(in_specs) + [ANY] * n_ci, list(out_specs) + [ANY] * n_co,
               list(out_shape) + list(comm.out_shapes),
               list(scratch_shapes) + [pltpu.SemaphoreType.DMA((comm.n_sems,))],
               ("arbitrary",) * len(grid), (*args, *comm.inputs))
    return list(res[:n_out]), list(res[n_out:])


def _run_comm(comm, *, name):
    n_ci = len(comm.inputs)

    def body(*refs):
        cins, couts, sems = refs[:n_ci], refs[n_ci:-1], refs[-1]
        comm.start(cins, couts, sems)
        comm.finish(cins, couts, sems)

    return list(pl.pallas_call(
        body, name=name, out_shape=list(comm.out_shapes), in_specs=[ANY] * n_ci,
        out_specs=[ANY] * len(comm.out_shapes), scratch_shapes=[pltpu.SemaphoreType.DMA((comm.n_sems,))],
    )(*comm.inputs))


class _Gather:
    def __init__(self, slabs):
        self.inputs = list(slabs)
        self.out_shapes = [jax.ShapeDtypeStruct((N_CHIPS,) + s.shape, s.dtype) for s in slabs]
        self.n_sems = 12 * len(slabs)

    @staticmethod
    def _copy(out, sems, base, k, chip, hc, to, src=None):
        H = out.shape[1] // 2
        half = out.at[2 * chip[0] + chip[1], pl.ds(hc * H, H), :]
        return pltpu.make_async_remote_copy(
            src_ref=half if src is None else src, dst_ref=half, send_sem=sems.at[base + k],
            recv_sem=sems.at[base + 6 + k], device_id=to, device_id_type=MESH)

    def _firsts(self, src, out, sems, base):
        x, y, c = _place()
        H = src.shape[0] // 2
        chips = [(1 - x, y), (x, 1 - y), (1 - x, 1 - y)]
        return [self._copy(out, sems, base, j, (x, y), c, (*chip, c), src=src.at[pl.ds(c * H, H), :])
                for j, chip in enumerate(chips)]

    def start(self, ins, outs, sems):
        for i, (src, out) in enumerate(zip(ins, outs)):
            for cp in self._firsts(src, out, sems, 12 * i):
                cp.start()

    def finish(self, ins, outs, sems):
        x, y, c = _place()
        chips = [(1 - x, y), (x, 1 - y), (1 - x, 1 - y)]
        passed = []
        for i, out in enumerate(outs):
            for j, chip in enumerate(chips):
                self._copy(out, sems, 12 * i, j, chip, c, (x, y, c)).wait_recv()
                cp = self._copy(out, sems, 12 * i, 3 + j, chip, c, (x, y, 1 - c))
                cp.start()
                passed.append(cp)
        for i, out in enumerate(outs):
            for j, chip in enumerate(chips):
                self._copy(out, sems, 12 * i, 3 + j, chip, 1 - c, (x, y, c)).wait_recv()
        for cp in passed:
            cp.wait_send()
        for i, (src, out) in enumerate(zip(ins, outs)):
            for cp in self._firsts(src, out, sems, 12 * i):
                cp.wait_send()


class _PairExchange:
    def __init__(self, arrays):
        self.inputs = list(arrays)
        self.out_shapes = [jax.ShapeDtypeStruct((N_CHIPS, a.shape[1] // 2, a.shape[2]), a.dtype) for a in arrays]
        self.n_sems = 2 * len(arrays)

    def _copies(self, ins, outs, sems):
        x, y, c = _place()
        return [pltpu.make_async_remote_copy(
            src_ref=g.at[:, pl.ds((1 - c) * t.shape[1], t.shape[1]), :], dst_ref=t, send_sem=sems.at[2 * i],
            recv_sem=sems.at[2 * i + 1], device_id=(x, y, 1 - c), device_id_type=MESH)
            for i, (g, t) in enumerate(zip(ins, outs))]

    def start(self, ins, outs, sems):
        for cp in self._copies(ins, outs, sems):
            cp.start()

    def finish(self, ins, outs, sems):
        for cp in self._copies(ins, outs, sems):
            cp.wait()


class _ChipExchange:
    def __init__(self, arrays):
        self.inputs = list(arrays)
        self.out_shapes = [jax.ShapeDtypeStruct(a.shape, a.dtype) for a in arrays]
        self.n_sems = 6 * len(arrays)

    def _copies(self, p, t, sems, base):
        x, y, c = _place()
        myq = 2 * x + y
        chips = [(1 - x, y), (x, 1 - y), (1 - x, 1 - y)]
        sends = [pltpu.make_async_remote_copy(
            src_ref=p.at[2 * chip[0] + chip[1]], dst_ref=t.at[myq], send_sem=sems.at[base + j],
            recv_sem=sems.at[base + 3 + j], device_id=(*chip, c), device_id_type=MESH) for j, chip in enumerate(chips)]
        lands = [pltpu.make_async_remote_copy(
            src_ref=t.at[2 * chip[0] + chip[1]], dst_ref=t.at[2 * chip[0] + chip[1]], send_sem=sems.at[base + j],
            recv_sem=sems.at[base + 3 + j], device_id=(*chip, c), device_id_type=MESH) for j, chip in enumerate(chips)]
        return sends, lands

    def start(self, ins, outs, sems):
        for i, (p, t) in enumerate(zip(ins, outs)):
            for cp in self._copies(p, t, sems, 6 * i)[0]:
                cp.start()

    def finish(self, ins, outs, sems):
        for i, (p, t) in enumerate(zip(ins, outs)):
            sends, lands = self._copies(p, t, sems, 6 * i)
            for cp in lands:
                cp.wait_recv()
            for cp in sends:
                cp.wait_send()


class _SemView:
    def __init__(self, sems, base):
        self._sems, self._base = sems, base

    @property
    def at(self):
        return self

    def __getitem__(self, k):
        return self._sems.at[self._base + k]


class _Multi:
    def __init__(self, progs):
        self.progs = list(progs)
        self.inputs = [a for p in self.progs for a in p.inputs]
        self.out_shapes = [s for p in self.progs for s in p.out_shapes]
        self.n_sems = sum(p.n_sems for p in self.progs)

    def _each(self, ins, outs, sems):
        i = o = s = 0
        for p in self.progs:
            ni, no = len(p.inputs), len(p.out_shapes)
            yield p, ins[i:i + ni], outs[o:o + no], _SemView(sems, s)
            i, o, s = i + ni, o + no, s + p.n_sems

    def start(self, ins, outs, sems):
        for p, a, b, c in self._each(ins, outs, sems):
            p.start(a, b, c)

    def finish(self, ins, outs, sems):
        for p, a, b, c in self._each(ins, outs, sems):
            p.finish(a, b, c)


class _PairShare:
    def __init__(self, arrays):
        self.inputs = list(arrays)
        self.out_shapes = [jax.ShapeDtypeStruct(a.shape, a.dtype) for a in arrays]
        self.n_sems = 2 * len(arrays)

    def _copies(self, ins, outs, sems):
        x, y, c = _place()
        return [pltpu.make_async_remote_copy(
            src_ref=r, dst_ref=o, send_sem=sems.at[2 * i], recv_sem=sems.at[2 * i + 1],
            device_id=(x, y, 1 - c), device_id_type=MESH) for i, (r, o) in enumerate(zip(ins, outs))]

    def start(self, ins, outs, sems):
        for cp in self._copies(ins, outs, sems):
            cp.start()

    def finish(self, ins, outs, sems):
        for cp in self._copies(ins, outs, sems):
            cp.wait()


def _ffn_up(x, ng, sh, sc, w1, w3, *, name, comm=None):
    S = x.shape[0]
    tm, tn = _row_tile(S, 512), FF // 2

    def body(x_ref, g_ref, sh_ref, sc_ref, w1_ref, w3_ref, h_ref, a_ref, b_ref, u_ref, hs):
        @pl.when(pl.program_id(1) == 0)
        def _():
            xv = x_ref[...]
            h = ((xv * _rms(xv)) * g_ref[...]) * (1.0 + sc_ref[...]) + sh_ref[...]
            hb = h.astype(BF16)
            hs[...] = hb
            h_ref[...] = hb

        h = hs[...]
        cols = pl.ds(pl.multiple_of(pl.program_id(1) * tn, tn), tn)
        a = _dot_nt(h, w1_ref[cols, :])
        b = _dot_nt(h, w3_ref[cols, :])
        _, sa = _silu_parts(a)
        a_ref[...] = a.astype(BF16)
        b_ref[...] = b.astype(BF16)
        u_ref[...] = (sa * b).astype(BF16)

    row = pl.BlockSpec((tm, D), lambda i, j: (i, 0))
    vec = pl.BlockSpec((1, D), lambda i, j: (0, 0))
    wsp = pl.BlockSpec((FF, D), lambda i, j: (0, 0))
    osp = pl.BlockSpec((tm, tn), lambda i, j: (i, j))
    return _hosted_call(
        body, name=name, grid=(S // tm, FF // tn),
        in_specs=[row, vec, vec, vec, wsp, wsp],
        out_specs=[row, osp, osp, osp],
        out_shape=[jax.ShapeDtypeStruct((S, D), BF16)] + [jax.ShapeDtypeStruct((S, FF), BF16)] * 3,
        scratch_shapes=[pltpu.VMEM((tm, D), BF16)],
        semantics=("parallel", "arbitrary"), args=(x, ng, sh, sc, w1, w3), comm=comm)


def _ffn_down(u, w2, x, gate, *, name, comm=None):
    S = x.shape[0]
    tm = _row_tile(S, 512)

    def body(u_ref, w2_ref, x_ref, g_ref, xo_ref, f_ref):
        f = jnp.dot(u_ref[...], w2_ref[...], preferred_element_type=F32)
        xo_ref[...] = x_ref[...] + (0.5 * g_ref[...]) * f
        f_ref[...] = f.astype(BF16)

    return _hosted_call(
        body, name=name, grid=(S // tm,),
        in_specs=[pl.BlockSpec((tm, FF), lambda i: (i, 0)), pl.BlockSpec((FF, D), lambda i: (0, 0)),
                  pl.BlockSpec((tm, D), lambda i: (i, 0)), pl.BlockSpec((1, D), lambda i: (0, 0))],
        out_specs=[pl.BlockSpec((tm, D), lambda i: (i, 0))] * 2,
        out_shape=[jax.ShapeDtypeStruct((S, D), F32), jax.ShapeDtypeStruct((S, D), BF16)],
        scratch_shapes=[], semantics=("parallel",), args=(u, w2, x, gate), comm=comm)


def _ffn_bwd_du(dx, gate, f, w2, a, b, *, name, comm=None):
    S = dx.shape[0]
    tm, tn = _row_tile(S, 512), FF // 2
    n_i = S // tm

    def body(dx_ref, g_ref, f_ref, w_ref, a_ref, b_ref, df_ref, da_ref, db_ref, dg_ref, dfs):
        i, j = pl.program_id(0), pl.program_id(1)

        @pl.when((i == 0) & (j == 0))
        def _():
            dg_ref[...] = jnp.zeros_like(dg_ref)

        @pl.when(j == 0)
        def _():
            dxv = dx_ref[...]
            dfb = (dxv * (0.5 * g_ref[...])).astype(BF16)
            dfs[...] = dfb
            df_ref[...] = dfb
            dg_ref[...] += _rsum8(dxv * (0.5 * f_ref[...].astype(F32)))

        du = _dot_nt(dfs[...], w_ref[pl.ds(pl.multiple_of(j * tn, tn), tn), :])
        av = a_ref[...].astype(F32)
        sg, sa = _silu_parts(av)
        da_ref[...] = (du * b_ref[...].astype(F32) * (sg * (1.0 + av * (1.0 - sg)))).astype(BF16)
        db_ref[...] = (du * sa).astype(BF16)

        @pl.when((i == n_i - 1) & (j == FF // tn - 1))
        def _():
            _all_rows(dg_ref)

    row = pl.BlockSpec((tm, D), lambda i, j: (i, 0))
    blk = pl.BlockSpec((tm, tn), lambda i, j: (i, j))
    return _hosted_call(
        body, name=name, grid=(n_i, FF // tn),
        in_specs=[row, pl.BlockSpec((1, D), lambda i, j: (0, 0)), row,
                  pl.BlockSpec((FF, D), lambda i, j: (0, 0)), blk, blk],
        out_specs=[row, blk, blk, pl.BlockSpec((8, D), lambda i, j: (0, 0))],
        out_shape=[jax.ShapeDtypeStruct((S, D), BF16), jax.ShapeDtypeStruct((S, FF), BF16),
                   jax.ShapeDtypeStruct((S, FF), BF16), jax.ShapeDtypeStruct((8, D), F32)],
        scratch_shapes=[pltpu.VMEM((tm, D), BF16)],
        semantics=("arbitrary", "arbitrary"), args=(dx, gate, f, w2, a, b), comm=comm)


def _tn_matmul(a, b, *, tm, tn, name, comm=None):
    S, M = a.shape
    N = b.shape[1]
    ts = _row_tile(S, 2048)
    ns = S // ts

    def body(a_ref, b_ref, o_ref):
        s = pl.program_id(2)
        p = lax.dot_general(a_ref[...], b_ref[...], (((0,), (0,)), ((), ())), preferred_element_type=F32)

        @pl.when(s == 0)
        def _():
            o_ref[...] = p

        @pl.when(s > 0)
        def _():
            o_ref[...] += p

    (out,), couts = _hosted_call(
        body, name=name, grid=(M // tm, N // tn, ns),
        in_specs=[pl.BlockSpec((ts, tm), lambda i, j, s: (s, i)), pl.BlockSpec((ts, tn), lambda i, j, s: (s, j))],
        out_specs=[pl.BlockSpec((tm, tn), lambda i, j, s: (i, j))],
        out_shape=[jax.ShapeDtypeStruct((M, N), F32)],
        scratch_shapes=[], semantics=("parallel", "parallel", "arbitrary"), args=(a, b), comm=comm)
    return out if comm is None else (out, couts)


def _dh_normbwd(pairs, x, ng, sc, dx_next, *, name, comm=None):
    S = x.shape[0]
    n_p = len(pairs)
    tm = _row_tile(S, 512)
    n_i = S // tm

    def body(*refs):
        a_refs, w_refs = refs[:n_p], refs[n_p:2 * n_p]
        x_ref, g_ref, sc_ref, dxn_ref, dx_ref, p_ref = refs[2 * n_p:]
        i = pl.program_id(0)
        dh = jnp.dot(a_refs[0][...], w_refs[0][...], preferred_element_type=F32)
        for k in range(1, n_p):
            dh = dh + jnp.dot(a_refs[k][...], w_refs[k][...], preferred_element_type=F32)
        xv = x_ref[...]
        r = _rms(xv)
        xh = xv * r
        g = g_ref[...]
        dn = dh * (1.0 + sc_ref[...])
        dy = dn * g
        dx_ref[...] = dxn_ref[...] + r * (dy - xh * jnp.mean(dy * xh, axis=-1, keepdims=True))

        @pl.when(i == 0)
        def _():
            p_ref[...] = jnp.zeros_like(p_ref)

        p_ref[:, 0:D] += _rsum8(dh * (xh * g))
        p_ref[:, D:2 * D] += _rsum8(dh)
        p_ref[:, 2 * D:3 * D] += _rsum8(dn * xh)

        @pl.when(i == n_i - 1)
        def _():
            _all_rows(p_ref)

    row = pl.BlockSpec((tm, D), lambda i: (i, 0))
    vec = pl.BlockSpec((1, D), lambda i: (0, 0))
    in_specs = ([pl.BlockSpec((tm, a.shape[1]), lambda i: (i, 0)) for a, _ in pairs]
                + [pl.BlockSpec(w.shape, lambda i: (0, 0), pipeline_mode=pl.Buffered(1)) for _, w in pairs]
                + [row, vec, vec, row])
    return _hosted_call(
        body, name=name, grid=(n_i,), in_specs=in_specs,
        out_specs=[row, pl.BlockSpec((8, 3 * D), lambda i: (0, 0))],
        out_shape=[jax.ShapeDtypeStruct((S, D), F32), jax.ShapeDtypeStruct((8, 3 * D), F32)],
        scratch_shapes=[], semantics=("arbitrary",),
        args=(*[a for a, _ in pairs], *[w for _, w in pairs], x, ng, sc, dx_next), comm=comm)


def _ffn_down_loss(u, w2, x, gate, gfin, tgt):
    S = x.shape[0]
    tm = _row_tile(S, 512)
    n_i = S // tm

    def body(u_ref, w2_ref, x_ref, gt_ref, g_ref, t_ref, dx_ref, f_ref, dg_ref, loss_ref, lacc):
        i = pl.program_id(0)
        f = jnp.dot(u_ref[...], w2_ref[...], preferred_element_type=F32)
        f_ref[...] = f.astype(BF16)
        xv = x_ref[...] + (0.5 * gt_ref[...]) * f
        r = _rms(xv)
        xh = xv * r
        g = g_ref[...]
        e = xh * g - t_ref[...]
        dout = e * (1.0 / D)
        dy = dout * g
        dx_ref[...] = r * (dy - xh * jnp.mean(dy * xh, axis=-1, keepdims=True))

        @pl.when(i == 0)
        def _():
            dg_ref[...] = jnp.zeros_like(dg_ref)
            lacc[...] = jnp.zeros_like(lacc)

        dg_ref[...] += _rsum8(dout * xh)
        lacc[...] += _rsum8(e * e)

        @pl.when(i == n_i - 1)
        def _():
            _all_rows(dg_ref)
            tot = jnp.sum(jnp.sum(lacc[...], axis=0, keepdims=True), axis=1, keepdims=True)
            loss_ref[...] = jnp.broadcast_to(tot * (0.5 / D), loss_ref.shape)

    row = pl.BlockSpec((tm, D), lambda i: (i, 0))
    vec = pl.BlockSpec((1, D), lambda i: (0, 0))
    return pl.pallas_call(
        body, name="ffn2_down_loss", grid=(n_i,),
        in_specs=[pl.BlockSpec((tm, FF), lambda i: (i, 0)), pl.BlockSpec((FF, D), lambda i: (0, 0)), row, vec, vec, row],
        out_specs=[row, row, pl.BlockSpec((8, D), lambda i: (0, 0)), pl.BlockSpec((8, 128), lambda i: (0, 0))],
        out_shape=[jax.ShapeDtypeStruct((S, D), F32), jax.ShapeDtypeStruct((S, D), BF16),
                   jax.ShapeDtypeStruct((8, D), F32), jax.ShapeDtypeStruct((8, 128), F32)],
        scratch_shapes=[pltpu.VMEM((8, D), F32)],
        compiler_params=_params(("arbitrary",)),
    )(u, w2, x, gate, gfin, tgt)


def _mix_in(x, ng, sh, sc, w_in, comm=None):
    S = x.shape[0]
    tm = _row_tile(S, 512)

    def body(x_ref, g_ref, sh_ref, sc_ref, w_ref, h_ref, z_ref):
        xv = x_ref[...]
        hb = (((xv * _rms(xv)) * g_ref[...]) * (1.0 + sc_ref[...]) + sh_ref[...]).astype(BF16)
        h_ref[...] = hb
        z_ref[...] = _dot_nt(hb, w_ref[...])

    row = pl.BlockSpec((tm, D), lambda i: (i, 0))
    vec = pl.BlockSpec((1, D), lambda i: (0, 0))
    return _hosted_call(
        body, name="mix_in", grid=(S // tm,),
        in_specs=[row, vec, vec, vec, pl.BlockSpec((Z_COLS, D), lambda i: (0, 0))],
        out_specs=[row, pl.BlockSpec((tm, Z_COLS), lambda i: (i, 0))],
        out_shape=[jax.ShapeDtypeStruct((S, D), BF16), jax.ShapeDtypeStruct((S, Z_COLS), F32)],
        scratch_shapes=[], semantics=("parallel",), args=(x, ng, sh, sc, w_in), comm=comm)


def _conv_taps(u, halo, rows):
    u1 = jnp.where(rows == 0, halo[7:8, :], pltpu.roll(u, 1, 0))
    u2 = jnp.where(rows == 0, halo[6:7, :], jnp.where(rows == 1, halo[7:8, :], pltpu.roll(u, 2, 0)))
    return u1, u2


def _mix_mid(z, conv_w, gq, gkv, wuq, wukv, cs, sn, comm=None):
    S = z.shape[0]
    tm = _row_tile(S, 512)
    hb = tm // 8

    def body(z_ref, zh_ref, cw_ref, gq_ref, gkv_ref, wuq_ref, wukv_ref, cs_ref, sn_ref,
             ya_ref, q_ref, k_ref, v_ref, cqn_ref, ckvn_ref):
        i = pl.program_id(0)
        xb = z_ref[:, 0:CONV_W]
        u = z_ref[:, CONV_W:2 * CONV_W] * z_ref[:, 2 * CONV_W:3 * CONV_W]
        halo = zh_ref[:, CONV_W:2 * CONV_W] * zh_ref[:, 2 * CONV_W:3 * CONV_W]
        halo = jnp.where(i > 0, halo, 0.0)
        rows = lax.broadcasted_iota(jnp.int32, (tm, CONV_W), 0)
        u1, u2 = _conv_taps(u, halo, rows)
        y = cw_ref[0:1, :] * u2 + cw_ref[1:2, :] * u1 + cw_ref[2:3, :] * u
        ya_ref[...] = xb * y

        lane = lax.broadcasted_iota(jnp.int32, (tm, 128), 1)
        cs_v, sn_v = cs_ref[...], sn_ref[...]
        cq = z_ref[:, 3 * CONV_W:3 * CONV_W + Q_LORA]
        cqn = ((cq * _rms(cq)) * gq_ref[...]).astype(BF16)
        cqn_ref[...] = cqn
        q = _dot_nt(cqn, wuq_ref[...])
        for h in range(HEADS):
            o = h * HEAD_SLOT
            q_ref[:, o:o + 128] = q[:, o:o + 128].astype(BF16)
            q_ref[:, o + 128:o + 256] = _rope(q[:, o + 128:o + 256], cs_v, sn_v, lane).astype(BF16)

        c0 = 3 * CONV_W + Q_LORA
        ckv = z_ref[:, c0:c0 + KV_LORA]
        ckvn = ((ckv * _rms(ckv)) * gkv_ref[...]).astype(BF16)
        ckvn_ref[...] = ckvn
        kv = jnp.dot(ckvn, wukv_ref[...], preferred_element_type=F32)
        krot = _rope(z_ref[:, c0 + KV_LORA:Z_COLS], cs_v, sn_v, lane).astype(BF16)
        for h in range(HEADS):
            o = h * HEAD_SLOT
            k_ref[:, o:o + 128] = kv[:, h * 128:(h + 1) * 128].astype(BF16)
            k_ref[:, o + 128:o + 256] = krot
        v_ref[...] = kv[:, HEADS * 128:].astype(BF16)

    def rows_of(n):
        return pl.BlockSpec((tm, n), lambda i: (i, 0))

    def whole(shape):
        return pl.BlockSpec(shape, lambda i: (0, 0))

    return _hosted_call(
        body, name="mix_mid", grid=(S // tm,),
        in_specs=[rows_of(Z_COLS), pl.BlockSpec((8, Z_COLS), lambda i: (jnp.maximum(i * hb - 1, 0), 0)),
                  whole((8, CONV_W)), whole((1, Q_LORA)), whole((1, KV_LORA)),
                  whole((HEADS * HEAD_SLOT, Q_LORA)), whole((KV_LORA, 2 * HEADS * 128)),
                  rows_of(128), rows_of(128)],
        out_specs=[rows_of(CONV_W), rows_of(HEADS * HEAD_SLOT), rows_of(HEADS * HEAD_SLOT), rows_of(HEADS * V_HEAD),
                   rows_of(Q_LORA), rows_of(KV_LORA)],
        out_shape=[jax.ShapeDtypeStruct((S, CONV_W), F32), jax.ShapeDtypeStruct((S, HEADS * HEAD_SLOT), BF16),
                   jax.ShapeDtypeStruct((S, HEADS * HEAD_SLOT), BF16), jax.ShapeDtypeStruct((S, HEADS * V_HEAD), BF16),
                   jax.ShapeDtypeStruct((S, Q_LORA), BF16), jax.ShapeDtypeStruct((S, KV_LORA), BF16)],
        scratch_shapes=[], semantics=("parallel",), args=(z, z, conv_w, gq, gkv, wuq, wukv, cs, sn), comm=comm)


def _att_blocks(S):
    bk = min(1024, max(S // 4, 128))
    return bk, bk


def _pair_tables(S, k_major):
    bq, bk = _att_blocks(S)
    nq, nk = S // bq, S // bk
    vis = lambda qi, ki: ki * bk < (qi + 1) * bq
    if k_major:
        pairs = [(qi, ki) for ki in range(nk) for qi in range(nq) if vis(qi, ki)]
    else:
        pairs = [(qi, ki) for qi in range(nq) for ki in range(nk) if vis(qi, ki)]
    cols = [[p[0] for p in pairs], [p[1] for p in pairs], [int((p[1] + 1) * bk > p[0] * bq) for p in pairs]]
    return [jnp.asarray(np.array(c, np.int32)) for c in cols], len(pairs)


def _chunk_mask(r0, nr, nc):
    r = (r0 + lax.broadcasted_iota(jnp.int32, (nr, nc), 0)) // CHUNK
    c = lax.broadcasted_iota(jnp.int32, (nr, nc), 1) // CHUNK
    return c <= r


def _diag_parts(bq, bk):
    return [(0, bq // 2, bk // 2), (bq // 2, bq // 2, bk)]


def _attention(q, k, v, comm=None):
    S = q.shape[0]
    bq, bk = _att_blocks(S)
    last_k = bq // bk - 1
    tables, n_pairs = _pair_tables(S, k_major=False)

    def body(qi_ref, ki_ref, mk_ref, q_ref, k_ref, v_ref, o_ref, lse_ref, m_s, l_s, acc_s):
        p_id = pl.program_id(1)
        qi, ki = qi_ref[p_id], ki_ref[p_id]

        @pl.when(ki == 0)
        def _():
            m_s[...] = jnp.full_like(m_s, NEG)
            l_s[...] = jnp.zeros_like(l_s)
            acc_s[...] = jnp.zeros_like(acc_s)

        def update(r0, nr, nc, masked):
            rows = slice(r0, r0 + nr)
            s = _dot_nt(q_ref[rows, :], k_ref[0:nc, :])
            if masked:
                s = jnp.where(_chunk_mask(r0, nr, nc), s, NEG)
            m_prev = m_s[rows, :]
            m_new = jnp.maximum(m_prev, jnp.max(s, axis=1, keepdims=True))
            alpha = jnp.exp2((m_prev - m_new) * EXP2_SCALE)
            p = jnp.exp2((s - jnp.tile(m_new, (1, nc // 128))) * EXP2_SCALE)
            l_s[rows, :] = alpha * l_s[rows, :] + jnp.sum(p, axis=1, keepdims=True)
            acc_s[rows, :] = alpha * acc_s[rows, :] + jnp.dot(p.astype(BF16), v_ref[0:nc, :],
                                                              preferred_element_type=F32)
            m_s[rows, :] = m_new

        @pl.when(mk_ref[p_id] == 0)
        def _():
            update(0, bq, bk, False)

        @pl.when(mk_ref[p_id] == 1)
        def _():
            for part in _diag_parts(bq, bk):
                update(*part, True)

        @pl.when(ki == qi * (last_k + 1) + last_k)
        def _():
            l = l_s[...]
            o_ref[...] = acc_s[...] / l
            lse_ref[...] = m_s[...] * EXP2_SCALE + jnp.log2(l)

    return _hosted_call(
        body, name="attention", grid=(HEADS, n_pairs),
        in_specs=[pl.BlockSpec((bq, HEAD_SLOT), lambda h, p, qt, kt, mt: (qt[p], h)),
                  pl.BlockSpec((bk, HEAD_SLOT), lambda h, p, qt, kt, mt: (kt[p], h)),
                  pl.BlockSpec((bk, V_HEAD), lambda h, p, qt, kt, mt: (kt[p], h))],
        out_specs=[pl.BlockSpec((bq, V_HEAD), lambda h, p, qt, kt, mt: (qt[p], h))] * 2,
        out_shape=[jax.ShapeDtypeStruct((S, HEADS * V_HEAD), F32)] * 2,
        scratch_shapes=[pltpu.VMEM((bq, V_HEAD), F32)] * 3,
        semantics=("arbitrary", "arbitrary"), args=(q, k, v), comm=comm, prefetch=tables)


def _attention_bwd(q, k, v, do, lse2, delta):
    S = q.shape[0]
    bq, bk = _att_blocks(S)
    nq = S // bq
    tables, n_pairs = _pair_tables(S, k_major=True)

    def body(qi_ref, ki_ref, mk_ref, q_ref, k_ref, v_ref, do_ref, lse_ref, dl_ref, dq_hbm, dk_ref, dv_ref,
             dq_s, dq_b, dk_s, dv_s, sem):
        head, p_id = pl.program_id(0), pl.program_id(1)
        qi, ki = qi_ref[p_id], ki_ref[p_id]

        @pl.when(qi * bq <= ki * bk)
        def _():
            dk_s[...] = jnp.zeros_like(dk_s)
            dv_s[...] = jnp.zeros_like(dv_s)

        def update(r0, nr, nc, masked):
            rows, cols = slice(r0, r0 + nr), slice(0, nc)
            qv, kv, dov = q_ref[rows, :], k_ref[cols, :], do_ref[rows, :]
            s = _dot_nt(qv, kv)
            dp = _dot_nt(dov, v_ref[cols, :])
            if masked:
                s = jnp.where(_chunk_mask(r0, nr, nc), s, NEG)
            p = jnp.exp2(s * EXP2_SCALE - jnp.tile(lse_ref[rows, :], (1, nc // 128)))
            dv_s[cols, :] += lax.dot_general(p.astype(BF16), dov, (((0,), (0,)), ((), ())),
                                             preferred_element_type=F32)
            ds = (p * (dp - jnp.tile(dl_ref[rows, :], (1, nc // 128)))).astype(BF16)
            dk_s[cols, :] += lax.dot_general(ds, qv, (((0,), (0,)), ((), ())), preferred_element_type=F32)
            dq = jnp.dot(ds, kv, preferred_element_type=F32)
            out_rows = pl.ds(pl.multiple_of(qi * bq + r0, nr), nr)

            @pl.when(ki == 0)
            def _():
                dq_s[out_rows, :] = dq

            @pl.when(ki > 0)
            def _():
                dq_s[out_rows, :] += dq

        @pl.when(mk_ref[p_id] == 0)
        def _():
            update(0, bq, bk, False)

        @pl.when(mk_ref[p_id] == 1)
        def _():
            for part in _diag_parts(bq, bk):
                update(*part, True)

        @pl.when(qi == nq - 1)
        def _():
            dk_ref[...] = (dk_s[...] * ATT_SCALE).astype(BF16)
            dv_ref[...] = dv_s[...].astype(BF16)

        @pl.when(p_id == n_pairs - 1)
        def _():
            dq_b[...] = (dq_s[...] * ATT_SCALE).astype(BF16)
            out = pltpu.make_async_copy(
                dq_b, dq_hbm.at[:, pl.ds(pl.multiple_of(head * HEAD_SLOT, HEAD_SLOT), HEAD_SLOT)], sem)
            out.start()
            out.wait()

    grid_spec = pltpu.PrefetchScalarGridSpec(
        num_scalar_prefetch=3, grid=(HEADS, n_pairs),
        in_specs=[pl.BlockSpec((bq, HEAD_SLOT), lambda h, p, qt, kt, mt: (qt[p], h)),
                  pl.BlockSpec((bk, HEAD_SLOT), lambda h, p, qt, kt, mt: (kt[p], h)),
                  pl.BlockSpec((bk, V_HEAD), lambda h, p, qt, kt, mt: (kt[p], h)),
                  pl.BlockSpec((bq, V_HEAD), lambda h, p, qt, kt, mt: (qt[p], h)),
                  pl.BlockSpec((bq, V_HEAD), lambda h, p, qt, kt, mt: (qt[p], h)),
                  pl.BlockSpec((bq, V_HEAD), lambda h, p, qt, kt, mt: (qt[p], h))],
        out_specs=[pl.BlockSpec(memory_space=pl.ANY),
                   pl.BlockSpec((bk, HEAD_SLOT), lambda h, p, qt, kt, mt: (kt[p], h)),
                   pl.BlockSpec((bk, V_HEAD), lambda h, p, qt, kt, mt: (kt[p], h))],
        scratch_shapes=[pltpu.VMEM((S, HEAD_SLOT), F32), pltpu.VMEM((S, HEAD_SLOT), BF16),
                        pltpu.VMEM((bk, HEAD_SLOT), F32), pltpu.VMEM((bk, V_HEAD), F32), pltpu.SemaphoreType.DMA])
    return pl.pallas_call(
        body, name="attention_bwd", grid_spec=grid_spec,
        out_shape=[jax.ShapeDtypeStruct((S, HEADS * HEAD_SLOT), BF16), jax.ShapeDtypeStruct((S, HEADS * HEAD_SLOT), BF16),
                   jax.ShapeDtypeStruct((S, HEADS * V_HEAD), BF16)],
        compiler_params=_params(("arbitrary", "arbitrary")),
    )(*tables, q, k, v, do, lse2, delta)


def _group_mats():
    def blockdiag(n, g):
        idx = np.arange(n) // g
        return jnp.asarray((idx[:, None] == idx[None, :]).astype(np.float32), dtype=BF16)
    return blockdiag(CONV_W, CONV_GROUP), blockdiag(HEADS * V_HEAD, V_HEAD)


def _mix_out(ya, o, gout, w_out, x, gate, ga, gb):
    S = x.shape[0]
    tm = _row_tile(S, 512)

    def body(ya_ref, o_ref, go_ref, w_ref, x_ref, g_ref, ga_ref, gb_ref, xo_ref, yn_ref, yo_ref):
        yav, ov = ya_ref[...], o_ref[...]
        ra = lax.rsqrt(_gsum(yav * yav, ga_ref[...]) * (1.0 / CONV_GROUP) + EPS)
        rb = lax.rsqrt(_gsum(ov * ov, gb_ref[...]) * (1.0 / V_HEAD) + EPS)
        na = ((yav * ra) * go_ref[:, 0:CONV_W]).astype(BF16)
        nb = ((ov * rb) * go_ref[:, CONV_W:]).astype(BF16)
        yn_ref[:, 0:CONV_W] = na
        yn_ref[:, CONV_W:] = nb
        yo = (jnp.dot(na, w_ref[0:CONV_W, :], preferred_element_type=F32)
              + jnp.dot(nb, w_ref[CONV_W:, :], preferred_element_type=F32))
        xo_ref[...] = x_ref[...] + g_ref[...] * yo
        yo_ref[...] = yo.astype(BF16)

    row = pl.BlockSpec((tm, D), lambda i: (i, 0))
    half = pl.BlockSpec((tm, CONV_W), lambda i: (i, 0))
    vec = pl.BlockSpec((1, D), lambda i: (0, 0))
    sq = pl.BlockSpec((CONV_W, CONV_W), lambda i: (0, 0))
    return pl.pallas_call(
        body, name="mix_out", grid=(S // tm,),
        in_specs=[half, half, vec, pl.BlockSpec((D, D), lambda i: (0, 0)), row, vec, sq, sq],
        out_specs=[row, row, row],
        out_shape=[jax.ShapeDtypeStruct((S, D), F32), jax.ShapeDtypeStruct((S, D), BF16),
                   jax.ShapeDtypeStruct((S, D), BF16)],
        compiler_params=_params(("parallel",)),
    )(ya, o, gout, w_out, x, gate, ga, gb)


def _mix_out_bwd(dx, gate, yo, w_out, ya, o, gout, ga, gb, comm=None):
    S = dx.shape[0]
    tm = _row_tile(S, 512)
    n_i = S // tm

    def norm_bwd(v, dn, gain, gmat, inv_n):
        r = lax.rsqrt(_gsum(v * v, gmat) * inv_n + EPS)
        vh = v * r
        dy = dn * gain
        return r * (dy - vh * (_gsum(dy * vh, gmat) * inv_n)), dn * vh

    def body(dx_ref, g_ref, yo_ref, w_ref, ya_ref, o_ref, go_ref, ga_ref, gb_ref,
             dyo_ref, dya_ref, do_ref, dl_ref, p_ref):
        i = pl.program_id(0)
        dxv = dx_ref[...]
        dyo = (dxv * g_ref[...]).astype(BF16)
        dyo_ref[...] = dyo
        dyn = _dot_nt(dyo, w_ref[...])
        dya, dga = norm_bwd(ya_ref[...], dyn[:, 0:CONV_W], go_ref[:, 0:CONV_W], ga_ref[...], 1.0 / CONV_GROUP)
        ov = o_ref[...]
        do, dgb = norm_bwd(ov, dyn[:, CONV_W:], go_ref[:, CONV_W:], gb_ref[...], 1.0 / V_HEAD)
        dya_ref[...] = dya
        do_ref[...] = do.astype(BF16)
        dl_ref[...] = _gsum(do * ov, gb_ref[...], split=True)

        @pl.when(i == 0)
        def _():
            p_ref[...] = jnp.zeros_like(p_ref)

        p_ref[:, 0:D] += _rsum8(dxv * yo_ref[...].astype(F32))
        p_ref[:, D:D + CONV_W] += _rsum8(dga)
        p_ref[:, D + CONV_W:2 * D] += _rsum8(dgb)

        @pl.when(i == n_i - 1)
        def _():
            _all_rows(p_ref)

    row = pl.BlockSpec((tm, D), lambda i: (i, 0))
    half = pl.BlockSpec((tm, CONV_W), lambda i: (i, 0))
    vec = pl.BlockSpec((1, D), lambda i: (0, 0))
    sq = pl.BlockSpec((CONV_W, CONV_W), lambda i: (0, 0))
    return _hosted_call(
        body, name="mix_out_bwd", grid=(n_i,),
        in_specs=[row, vec, row, pl.BlockSpec((D, D), lambda i: (0, 0)), half, half, vec, sq, sq],
        out_specs=[row, half, half, half, pl.BlockSpec((8, 2 * D), lambda i: (0, 0))],
        out_shape=[jax.ShapeDtypeStruct((S, D), BF16), jax.ShapeDtypeStruct((S, CONV_W), F32),
                   jax.ShapeDtypeStruct((S, CONV_W), BF16), jax.ShapeDtypeStruct((S, CONV_W), F32),
                   jax.ShapeDtypeStruct((8, 2 * D), F32)],
        scratch_shapes=[], semantics=("arbitrary",), args=(dx, gate, yo, w_out, ya, o, gout, ga, gb), comm=comm)


MID_SUMS = 3 * CONV_W + Q_LORA + KV_LORA


def _mix_mid_bwd(z, dya, conv_w, gq, gkv, wuq, wukv, cs, sn, dq, dk, dv, comm=None):
    S = z.shape[0]
    tm = _row_tile(S, 256)
    n_i = S // tm
    hb = tm // 8
    last_blk = S // 8 - 1

    def latent_bwd(cv, dcn, gain):
        r = _rms(cv)
        ch = cv * r
        dy = dcn * gain
        return r * (dy - ch * jnp.mean(dy * ch, axis=-1, keepdims=True)), dcn * ch

    def body(z_ref, zp_ref, zn_ref, dya_ref, dyan_ref, cw_ref, gq_ref, gkv_ref, wuq_ref, wukv_ref, cs_ref, sn_ref,
             dq_ref, dk_ref, dv_ref, dz_ref, dqf_ref, dkvf_ref, p_ref):
        i = pl.program_id(0)
        xb, xc, xu = z_ref[:, 0:CONV_W], z_ref[:, CONV_W:2 * CONV_W], z_ref[:, 2 * CONV_W:3 * CONV_W]
        u = xc * xu
        halo = jnp.where(i > 0, zp_ref[:, CONV_W:2 * CONV_W] * zp_ref[:, 2 * CONV_W:3 * CONV_W], 0.0)
        rows = lax.broadcasted_iota(jnp.int32, (tm, CONV_W), 0)
        u1, u2 = _conv_taps(u, halo, rows)
        w0, w1, w2 = cw_ref[0:1, :], cw_ref[1:2, :], cw_ref[2:3, :]
        y = w0 * u2 + w1 * u1 + w2 * u
        dyav = dya_ref[...]
        dy = dyav * xb
        nxt = jnp.where(i < n_i - 1, dyan_ref[...] * zn_ref[:, 0:CONV_W], 0.0)
        dy1 = jnp.where(rows == tm - 1, nxt[0:1, :], pltpu.roll(dy, tm - 1, 0))
        dy2 = jnp.where(rows == tm - 1, nxt[1:2, :], jnp.where(rows == tm - 2, nxt[0:1, :], pltpu.roll(dy, tm - 2, 0)))
        du = w2 * dy + w1 * dy1 + w0 * dy2
        dz_ref[:, 0:CONV_W] = (dyav * y).astype(BF16)
        dz_ref[:, CONV_W:2 * CONV_W] = (du * xu).astype(BF16)
        dz_ref[:, 2 * CONV_W:3 * CONV_W] = (du * xc).astype(BF16)

        lane = lax.broadcasted_iota(jnp.int32, (tm, 128), 1)
        cs_v, sn_v = cs_ref[...], sn_ref[...]
        dkr = jnp.zeros((tm, 128), F32)
        for h in range(HEADS):
            o = h * HEAD_SLOT
            dqf_ref[:, o:o + 128] = dq_ref[:, o:o + 128]
            dqf_ref[:, o + 128:o + 256] = _rope_t(dq_ref[:, o + 128:o + 256].astype(F32), cs_v, sn_v, lane).astype(BF16)
            dkvf_ref[:, h * 128:(h + 1) * 128] = dk_ref[:, o:o + 128]
            dkr = dkr + dk_ref[:, o + 128:o + 256].astype(F32)
        dkvf_ref[:, HEADS * 128:] = dv_ref[...]

        c0 = 3 * CONV_W
        dcqn = jnp.dot(dqf_ref[...], wuq_ref[...], preferred_element_type=F32)
        dcq, dgq = latent_bwd(z_ref[:, c0:c0 + Q_LORA], dcqn, gq_ref[...])
        dz_ref[:, c0:c0 + Q_LORA] = dcq.astype(BF16)
        c1 = c0 + Q_LORA
        dckvn = _dot_nt(dkvf_ref[...], wukv_ref[...])
        dckv, dgkv = latent_bwd(z_ref[:, c1:c1 + KV_LORA], dckvn, gkv_ref[...])
        dz_ref[:, c1:c1 + KV_LORA] = dckv.astype(BF16)
        dz_ref[:, c1 + KV_LORA:Z_COLS] = _rope_t(dkr, cs_v, sn_v, lane).astype(BF16)

        @pl.when(i == 0)
        def _():
            p_ref[...] = jnp.zeros_like(p_ref)

        p_ref[:, 0:CONV_W] += _rsum8(dy * u2)
        p_ref[:, CONV_W:2 * CONV_W] += _rsum8(dy * u1)
        p_ref[:, 2 * CONV_W:3 * CONV_W] += _rsum8(dy * u)
        p_ref[:, c0:c0 + Q_LORA] += _rsum8(dgq)
        p_ref[:, c1:c1 + KV_LORA] += _rsum8(dgkv)

        @pl.when(i == n_i - 1)
        def _():
            _all_rows(p_ref)

    def rows_of(n):
        return pl.BlockSpec((tm, n), lambda i: (i, 0))

    def whole(shape):
        return pl.BlockSpec(shape, lambda i: (0, 0))

    def prev8(n):
        return pl.BlockSpec((8, n), lambda i: (jnp.maximum(i * hb - 1, 0), 0))

    def next8(n):
        return pl.BlockSpec((8, n), lambda i: (jnp.minimum((i + 1) * hb, last_blk), 0))

    return _hosted_call(
        body, name="mix_mid_bwd", grid=(n_i,),
        in_specs=[rows_of(Z_COLS), prev8(Z_COLS), next8(Z_COLS), rows_of(CONV_W), next8(CONV_W),
                  whole((8, CONV_W)), whole((1, Q_LORA)), whole((1, KV_LORA)),
                  whole((HEADS * HEAD_SLOT, Q_LORA)), whole((KV_LORA, 2 * HEADS * 128)),
                  rows_of(128), rows_of(128),
                  rows_of(HEADS * HEAD_SLOT), rows_of(HEADS * HEAD_SLOT), rows_of(HEADS * V_HEAD)],
        out_specs=[rows_of(Z_COLS), rows_of(HEADS * HEAD_SLOT), rows_of(2 * HEADS * 128), whole((8, MID_SUMS))],
        out_shape=[jax.ShapeDtypeStruct((S, Z_COLS), BF16), jax.ShapeDtypeStruct((S, HEADS * HEAD_SLOT), BF16),
                   jax.ShapeDtypeStruct((S, 2 * HEADS * 128), BF16), jax.ShapeDtypeStruct((8, MID_SUMS), F32)],
        scratch_shapes=[], semantics=("arbitrary",),
        args=(z, z, z, dya, dya, conv_w, gq, gkv, wuq, wukv, cs, sn, dq, dk, dv), comm=comm)


ADA_Q = N_MOD * D // N_CHIPS
ADA_TN = 768


def _ada_forward(c_all, ada_w_q, ada_b_q):
    def body(c_ref, w_ref, b_ref, o_ref):
        cv = c_ref[...]
        sc = (cv * jax.nn.sigmoid(cv)).astype(BF16)
        o_ref[...] = jnp.dot(sc, w_ref[...].astype(BF16), preferred_element_type=F32) + b_ref[...]

    return pl.pallas_call(
        body, name="ada_forward", grid=(ADA_Q // ADA_TN,),
        in_specs=[pl.BlockSpec((16, D), lambda j: (0, 0)), pl.BlockSpec((D, ADA_TN), lambda j: (0, j)),
                  pl.BlockSpec((1, ADA_TN), lambda j: (0, j))],
        out_specs=pl.BlockSpec((16, ADA_TN), lambda j: (0, j)),
        out_shape=jax.ShapeDtypeStruct((16, ADA_Q), F32),
        compiler_params=_params(("parallel",)),
    )(c_all, ada_w_q, ada_b_q)


def _ada_wgrad(c_all, dmod_q):
    def body(c_ref, d_ref, o_ref):
        cv = c_ref[...]
        sc = (cv * jax.nn.sigmoid(cv)).astype(BF16)
        o_ref[...] = lax.dot_general(sc, d_ref[...].astype(BF16), (((0,), (0,)), ((), ())),
                                     preferred_element_type=F32)

    return pl.pallas_call(
        body, name="ada_wgrad", grid=(ADA_Q // ADA_TN,),
        in_specs=[pl.BlockSpec((16, D), lambda j: (0, 0)), pl.BlockSpec((16, ADA_TN), lambda j: (0, j))],
        out_specs=pl.BlockSpec((D, ADA_TN), lambda j: (0, j)),
        out_shape=jax.ShapeDtypeStruct((D, ADA_Q), F32),
        compiler_params=_params(("parallel",)),
    )(c_all, dmod_q)


def _sum_devices(parts):
    n = parts.shape[1]

    def body(p_ref, o_ref):
        o_ref[...] = jnp.broadcast_to(jnp.sum(p_ref[...], axis=0, keepdims=True), o_ref.shape)

    return pl.pallas_call(
        body, name="sum_devices",
        in_specs=[pl.BlockSpec((N_DEV, n), lambda: (0, 0))], out_specs=pl.BlockSpec((N_DEV, n), lambda: (0, 0)),
        out_shape=jax.ShapeDtypeStruct((N_DEV, n), F32),
    )(parts)


def _adamw(ws, gs, ms, vs, *, name, comm=None):
    n = len(ws)
    _, rows, cols = ws[0].shape
    tr = _row_tile(rows, 256)

    def body(*refs):
        ins, outs = refs[:4 * n], refs[4 * n:]
        for k in range(n):
            w_ref, g_ref, m_ref, v_ref = ins[k], ins[n + k], ins[2 * n + k], ins[3 * n + k]
            go_ref, d_ref, mo_ref, vo_ref = outs[4 * k:4 * k + 4]
            gv = g_ref[...]
            mn = B1 * m_ref[0] + (1.0 - B1) * gv
            vn = B2 * v_ref[0] + (1.0 - B2) * (gv * gv)
            m_hat = mn / (1.0 - B1 ** STEP)
            v_hat = vn / (1.0 - B2 ** STEP)
            go_ref[0] = gv
            d_ref[0] = -LR * (m_hat / (jnp.sqrt(v_hat) + AEPS) + WD * w_ref[0])
            mo_ref[0] = mn
            vo_ref[0] = vn

    blk = pl.BlockSpec((1, tr, cols), lambda i: (0, i, 0))
    gblk = pl.BlockSpec((tr, cols), lambda i: (i, 0))
    res, couts = _hosted_call(
        body, name=name, grid=(rows // tr,),
        in_specs=[blk] * n + [gblk] * n + [blk] * 2 * n, out_specs=[blk] * 4 * n,
        out_shape=[jax.ShapeDtypeStruct((1, rows, cols), F32)] * 4 * n,
        scratch_shapes=[], semantics=("parallel",), args=(*ws, *gs, *ms, *vs), comm=comm)
    return [res[4 * k:4 * k + 4] for k in range(n)], couts


def _small_allgather(v, *, name):
    m, n = v.shape

    def body(x_ref, out_ref, send_sems, recv_sems, local_sem):
        x, y, c = _place()
        me, sibling = (x, y, c), (x, y, 1 - c)
        chips = [(1 - x, y), (x, 1 - y), (1 - x, 1 - y)]

        def rows(px, py, pc):
            return out_ref.at[pl.ds((4 * px + 2 * py + pc) * m, m), :]

        def copy(k, block, to, src=None):
            return pltpu.make_async_remote_copy(
                src_ref=rows(*block) if src is None else src, dst_ref=rows(*block),
                send_sem=send_sems.at[k], recv_sem=recv_sems.at[k], device_id=to, device_id_type=MESH)

        mine = pltpu.make_async_copy(x_ref, rows(*me), local_sem)
        mine.start()
        first = [copy(0, me, sibling, src=x_ref)]
        first += [copy(1 + j, me, (*chip, c), src=x_ref) for j, chip in enumerate(chips)]
        for cp in first:
            cp.start()
        passed = [copy(4 + j, (*chip, c), sibling) for j, chip in enumerate(chips)]
        for j, chip in enumerate(chips):
            copy(1 + j, (*chip, c), me).wait_recv()
            passed[j].start()
        copy(0, sibling, me).wait_recv()
        for j, chip in enumerate(chips):
            copy(4 + j, (*chip, 1 - c), me).wait_recv()
        for cp in first + passed:
            cp.wait_send()
        mine.wait()

    return pl.pallas_call(
        body, name=name,
        out_shape=jax.ShapeDtypeStruct((N_DEV * m, n), v.dtype),
        in_specs=[pl.BlockSpec(memory_space=pltpu.VMEM)], out_specs=pl.BlockSpec(memory_space=pltpu.VMEM),
        scratch_shapes=[pltpu.SemaphoreType.DMA((7,)), pltpu.SemaphoreType.DMA((7,)), pltpu.SemaphoreType.DMA],
    )(v)


ADD_BLOCKS = 2


def _pair_add(place, gs, ts, *, name):
    n_a = len(gs)

    def body(pl_ref, *refs):
        g_refs, t_refs = refs[:n_a], refs[n_a:2 * n_a]
        pf_refs, pb_refs = refs[2 * n_a:3 * n_a], refs[3 * n_a:]
        own = pl.program_id(1) == pl_ref[1]
        for g_ref, t_ref, pf_ref, pb_ref in zip(g_refs, t_refs, pf_refs, pb_refs):
            s = g_ref[...] + t_ref[...]
            pb_ref[...] = s.astype(BF16)

            @pl.when(own)
            def _():
                pf_ref[...] = s[0]

    def blk(t, own_half):
        tr = t.shape[1] // ADD_BLOCKS
        if own_half:
            return pl.BlockSpec((1, tr, t.shape[2]), lambda r, q, p: (q, p[0] * ADD_BLOCKS + r, 0))
        return pl.BlockSpec((1, tr, t.shape[2]), lambda r, q, p: (q, r, 0))

    def own_blk(t):
        return pl.BlockSpec((t.shape[1] // ADD_BLOCKS, t.shape[2]), lambda r, q, p: (r, 0))

    grid_spec = pltpu.PrefetchScalarGridSpec(
        num_scalar_prefetch=1, grid=(ADD_BLOCKS, N_CHIPS),
        in_specs=[blk(t, True) for t in ts] + [blk(t, False) for t in ts],
        out_specs=[own_blk(t) for t in ts] + [blk(t, False) for t in ts])
    res = pl.pallas_call(
        body, name=name, grid_spec=grid_spec,
        out_shape=([jax.ShapeDtypeStruct(t.shape[1:], F32) for t in ts]
                   + [jax.ShapeDtypeStruct(t.shape, BF16) for t in ts]),
        compiler_params=_params(("parallel", "arbitrary")),
    )(place, *gs, *ts)
    return list(res[:n_a]), list(res[n_a:])


def _chip_add(place, pfs, ts, *, name):
    n_a = len(pfs)

    def body(pl_ref, *refs):
        pf_refs, t_refs, o_refs = refs[:n_a], refs[n_a:4 * n_a], refs[4 * n_a:]
        for i, (pf_ref, o_ref) in enumerate(zip(pf_refs, o_refs)):
            t1, t2, t3 = t_refs[3 * i:3 * i + 3]
            o_ref[...] = ((pf_ref[...] + t1[0].astype(F32)) + t2[0].astype(F32)) + t3[0].astype(F32)

    def slot(t, j):
        return pl.BlockSpec((1, t.shape[1] // ADD_BLOCKS, t.shape[2]), lambda r, p: (p[1] ^ j, r, 0))

    def half(t):
        return pl.BlockSpec((t.shape[1] // ADD_BLOCKS, t.shape[2]), lambda r, p: (r, 0))

    grid_spec = pltpu.PrefetchScalarGridSpec(
        num_scalar_prefetch=1, grid=(ADD_BLOCKS,),
        in_specs=[half(t) for t in ts] + [slot(t, j) for t in ts for j in (1, 2, 3)],
        out_specs=[half(t) for t in ts])
    res = pl.pallas_call(
        body, name=name, grid_spec=grid_spec,
        out_shape=[jax.ShapeDtypeStruct(t.shape[1:], F32) for t in ts],
        compiler_params=_params(("parallel",)),
    )(place, *pfs, *[t for t in ts for _ in range(3)])
    return list(res)


BULK = [("ffn1_w1", "colsT"), ("ffn1_w3", "colsT"), ("ffn1_w2", "rows"), ("w_in", "cols"), ("w_uq", "colsT"),
        ("w_ukv", "cols"), ("w_out", "rows"), ("ffn2_w1", "colsT"), ("ffn2_w3", "colsT"), ("ffn2_w2", "rows")]
KIND = dict(BULK)


def _group(*names):
    return [b for b in BULK if b[0] in names]


W_FIRST = _group("ffn1_w1", "ffn1_w3")
W_REST = [b for b in BULK if b not in W_FIRST]
W_MIX = _group("ffn1_w2", "w_in", "w_uq", "w_ukv", "w_out")
W_FFN2 = _group("ffn2_w1", "ffn2_w3", "ffn2_w2")
G_FFN2 = _group("ffn2_w1", "ffn2_w3", "ffn2_w2")
G_MIX = _group("w_in", "w_uq", "w_ukv", "w_out")
G_FFN1 = _group("ffn1_w1", "ffn1_w3", "ffn1_w2")


def _gathered_weights(specs, shards, got, myq):
    out = {}
    for (name, kind), part in zip(specs, got):
        part = lax.dynamic_update_slice_in_dim(part, shards[name][None], myq, axis=0)
        out[name] = _full_weight(part, kind)
    return out


def _working_shard(w, kind):
    return jnp.swapaxes(w, 1, 2)[0] if kind == "colsT" else w[0]


def _full_weight(parts, kind):
    if kind == "cols":
        return jnp.transpose(parts, (1, 0, 2)).reshape(parts.shape[1], -1)
    return parts.reshape(-1, parts.shape[2])


def _quarters(g, kind):
    if kind == "cols":
        k, n = g.shape
        return jnp.transpose(g.reshape(k, N_CHIPS, n // N_CHIPS), (1, 0, 2))
    return g.reshape(N_CHIPS, g.shape[0] // N_CHIPS, g.shape[1])


def _pad_heads(w_uq_t):
    w = w_uq_t.reshape(HEADS, QK_NOPE + QK_ROPE, Q_LORA)
    return jnp.pad(w, ((0, 0), (0, HEAD_SLOT - QK_NOPE - QK_ROPE), (0, 0))).reshape(HEADS * HEAD_SLOT, Q_LORA)


def _unpad_heads(g):
    return g.reshape(HEADS, HEAD_SLOT, Q_LORA)[:, :QK_NOPE + QK_ROPE].reshape(HEADS * (QK_NOPE + QK_ROPE), Q_LORA)


def _split_kv(w_ukv):
    return jnp.transpose(w_ukv.reshape(KV_LORA, HEADS, 2, 128), (0, 2, 1, 3)).reshape(KV_LORA, 2 * HEADS * 128)


def _merge_kv(g):
    return jnp.transpose(g.reshape(KV_LORA, 2, HEADS, 128), (0, 2, 1, 3)).reshape(KV_LORA, 2 * HEADS * 128)


def _rope_tables(positions):
    inv_freq = ROPE_THETA ** (-jnp.arange(0, QK_ROPE, 2, dtype=F32) / QK_ROPE)
    ang = positions.astype(F32)[:, None] * inv_freq
    cos, sin, zero = jnp.cos(ang), jnp.sin(ang), jnp.zeros((positions.shape[0], 64), F32)
    return jnp.concatenate([cos, cos, zero], axis=1), jnp.concatenate([sin, sin, zero], axis=1)


def _assemble(place, specs, rhs, others):
    south = place[0] == 0
    return {b[0]: jnp.concatenate([jnp.where(south, rh, ot), jnp.where(south, ot, rh)], axis=0)
            for b, rh, ot in zip(specs, rhs, others)}


def _local_step(x, positions, target, mod, vec, conv_w, w_first, rest_shards, place, tail_host=None):
    row = lambda k: mod[k:k + 1]
    sh1, sc1, g1, sh2, sc2, g2, sh3, sc3, g3 = [row(k) for k in range(N_MOD)]
    cs, sn = _rope_tables(positions)
    cw8 = jnp.pad(conv_w, ((0, 5), (0, 0)))
    ga, gb = _group_mats()
    dist = place is not None
    comm = lambda prog: prog if dist else None

    w = dict(w_first) if dist else {**rest_shards, **w_first}

    def gather(specs):
        return _Gather([rest_shards[b[0]] for b in specs]) if dist else None

    def arrived(specs, got):
        if dist:
            w.update(_gathered_weights(specs, rest_shards, got, place[1]))

    (h1, a1, b1, u1), got = _ffn_up(x, vec["norm_ffn1_g"], sh1, sc1, w["ffn1_w1"], w["ffn1_w3"],
                                    name="ffn1_up", comm=gather(W_MIX))
    arrived(W_MIX, got)
    w_in = jnp.pad(w["w_in"].T, ((0, Z_COLS - IN_COLS), (0, 0)))
    wuq = _pad_heads(w["w_uq"])
    wukv = _split_kv(w["w_ukv"])
    (x1, f1), _ = _ffn_down(u1, w["ffn1_w2"], x, g1, name="ffn1_down")
    (h2, z), _ = _mix_in(x1, vec["norm_mix_g"], sh2, sc2, w_in)
    (ya, q, k, v, cqn, ckvn), _ = _mix_mid(z, cw8, vec["q_norm_g"], vec["kv_norm_g"], wuq, wukv, cs, sn)
    (o, lse), got = _attention(q, k, v, comm=gather(W_FFN2))
    arrived(W_FFN2, got)
    x2, yn, yo = _mix_out(ya, o, vec["out_norm_g"], w["w_out"], x1, g2, ga, gb)
    (h3, a3, b3, u3), _ = _ffn_up(x2, vec["norm_ffn2_g"], sh3, sc3, w["ffn2_w1"], w["ffn2_w3"], name="ffn2_up")
    dx3, f3, dgfin, loss_blk = _ffn_down_loss(u3, w["ffn2_w2"], x2, g3, vec["final_norm_g"], target)

    grads, reduced = {}, {}

    def tn(a, b, tm, tn_, name, prog=None):
        if prog is None:
            return _tn_matmul(a, b, tm=tm, tn=tn_, name=name), None
        return _tn_matmul(a, b, tm=tm, tn=tn_, name=name, comm=prog)

    def slab_of(specs):
        return [_quarters(grads[n], kind) for n, kind in specs]

    (df3, da3, db3, dg3), _ = _ffn_bwd_du(dx3, g3, f3, w["ffn2_w2"], a3, b3, name="ffn2_bwd_du")
    grads["ffn2_w2"], _ = tn(u3, df3, FF // 2, D, "ffn2_dw2")
    grads["ffn2_w1"], _ = tn(da3, h3, FF // 2, D, "ffn2_dw1")
    grads["ffn2_w3"], _ = tn(db3, h3, FF // 2, D, "ffn2_dw3")
    (dx2, s3), _ = _dh_normbwd([(da3, w["ffn2_w1"]), (db3, w["ffn2_w3"])], x2, vec["norm_ffn2_g"], sc3, dx3,
                               name="ffn2_bwd_dh")

    p1 = slab_of(G_FFN2) if dist else None
    (dyo, dya, do, delta, s_out), t1 = _mix_out_bwd(dx2, g2, yo, w["w_out"], ya, o, vec["out_norm_g"], ga, gb,
                                                    comm=comm(_PairExchange(p1) if dist else None))
    grads["w_out"], _ = tn(yn, dyo, D, D, "dw_out")
    if dist:
        pf1, pb1 = _pair_add(place, p1, t1, name="ffn2g_pair_add")
    dq, dk, dv = _attention_bwd(q, k, v, do, lse, delta)
    (dz, dqf, dkvf, s_mid), t2 = _mix_mid_bwd(z, dya, cw8, vec["q_norm_g"], vec["kv_norm_g"], wuq, wukv, cs, sn,
                                              dq, dk, dv, comm=comm(_ChipExchange(pb1) if dist else None))
    g_uq, _ = tn(dqf, cqn, HEADS * HEAD_SLOT, Q_LORA, "dw_uq")
    g_ukv, _ = tn(ckvn, dkvf, KV_LORA, 2 * HEADS * 128, "dw_ukv")
    grads["w_uq"], grads["w_ukv"] = _unpad_heads(g_uq), _merge_kv(g_ukv)
    g_in, _ = tn(h2, dz, D, Z_COLS // 2, "dw_in")
    grads["w_in"] = g_in[:, :IN_COLS]
    (dx1, s2), _ = _dh_normbwd([(dz, w_in)], x1, vec["norm_mix_g"], sc2, dx2, name="mix_bwd_dh")

    p2 = slab_of(G_MIX) if dist else None
    rh_ffn2 = _chip_add(place, pf1, t2, name="ffn2g_chip_add") if dist else None
    (df1, da1, db1, dg1), got = _ffn_bwd_du(dx1, g1, f1, w["ffn1_w2"], a1, b1, name="ffn1_bwd_du",
                                            comm=comm(_Multi([_PairExchange(p2), _PairShare(rh_ffn2)]) if dist else None))
    if dist:
        t1 = got[:len(p2)]
        reduced.update(_assemble(place, G_FFN2, rh_ffn2, got[len(p2):]))
    dh_pairs = [(da1, w["ffn1_w1"]), (db1, w["ffn1_w3"])]
    if dist:
        pf2, pb2 = _pair_add(place, p2, t1, name="mixg_pair_add")
        grads["ffn1_w2"], t2 = tn(u1, df1, FF // 2, D, "ffn1_dw2", _ChipExchange(pb2))
        rh_mix = _chip_add(place, pf2, t2, name="mixg_chip_add")
        q_w2 = [_quarters(grads["ffn1_w2"], KIND["ffn1_w2"])]
        grads["ffn1_w1"], got = tn(da1, h1, FF // 2, D, "ffn1_dw1", _Multi([_PairShare(rh_mix), _PairExchange(q_w2)]))
        reduced.update(_assemble(place, G_MIX, rh_mix, got[:len(rh_mix)]))
        pf_w2, pb_w2 = _pair_add(place, q_w2, got[len(rh_mix):], name="ffn1w2_pair_add")
        q_w1 = [_quarters(grads["ffn1_w1"], KIND["ffn1_w1"])]
        grads["ffn1_w3"], got = tn(db1, h1, FF // 2, D, "ffn1_dw3", _Multi([_ChipExchange(pb_w2), _PairExchange(q_w1)]))
        t2_w2 = got[:1]
        pf_w1, pb_w1 = _pair_add(place, q_w1, got[1:], name="ffn1w1_pair_add")
        q_w3 = [_quarters(grads["ffn1_w3"], KIND["ffn1_w3"])]
        (dx0, s1), got = _dh_normbwd(dh_pairs, x, vec["norm_ffn1_g"], sc1, dx1, name="ffn1_bwd_dh",
                                     comm=_Multi([_ChipExchange(pb_w1), _PairExchange(q_w3)]))
        t2_w1 = got[:1]
        pf_w3, pb_w3 = _pair_add(place, q_w3, got[1:], name="ffn1w3_pair_add")
        t2_w3 = tail_host(_ChipExchange(pb_w3), reduced)
        rh = _chip_add(place, pf_w1 + pf_w3 + pf_w2, t2_w1 + t2_w3 + t2_w2, name="ffn1g_chip_add")
        reduced.update(_assemble(place, G_FFN1, rh, _run_comm(_PairShare(rh), name="ffn1g_pair_share")))
    else:
        grads["ffn1_w2"], _ = tn(u1, df1, FF // 2, D, "ffn1_dw2")
        grads["ffn1_w1"], _ = tn(da1, h1, FF // 2, D, "ffn1_dw1")
        grads["ffn1_w3"], _ = tn(db1, h1, FF // 2, D, "ffn1_dw3")
        (dx0, s1), _ = _dh_normbwd(dh_pairs, x, vec["norm_ffn1_g"], sc1, dx1, name="ffn1_bwd_dh")
        reduced = grads

    def part(s, k):
        return s[0:1, k * D:(k + 1) * D]

    dmod = jnp.concatenate([part(s1, 1), part(s1, 0), dg1[0:1], part(s2, 1), part(s2, 0), part(s_out, 0),
                            part(s3, 1), part(s3, 0), dg3[0:1]], axis=1)
    small = {"norm_ffn1_g": part(s1, 2), "norm_mix_g": part(s2, 2), "out_norm_g": part(s_out, 1),
             "norm_ffn2_g": part(s3, 2), "final_norm_g": dgfin[0:1],
             "q_norm_g": s_mid[0:1, 3 * CONV_W:3 * CONV_W + Q_LORA],
             "kv_norm_g": s_mid[0:1, 3 * CONV_W + Q_LORA:MID_SUMS], "conv_w": s_mid[0:1, 0:3 * CONV_W]}
    return loss_blk, dx0, reduced, dmod, small


SMALL = [("norm_ffn1_g", D), ("norm_mix_g", D), ("out_norm_g", D), ("norm_ffn2_g", D), ("final_norm_g", D),
         ("q_norm_g", Q_LORA), ("kv_norm_g", KV_LORA), ("conv_w", 3 * CONV_W)]
WEIGHTS = ['ada_w', 'ada_b', 'norm_ffn1_g', 'ffn1_w1', 'ffn1_w3', 'ffn1_w2', 'norm_mix_g', 'w_in', 'conv_w',
           'q_norm_g', 'w_uq', 'kv_norm_g', 'w_ukv', 'out_norm_g', 'w_out', 'norm_ffn2_g', 'ffn2_w1', 'ffn2_w3',
           'ffn2_w2', 'final_norm_g']


def kernel(x, c, positions, ada_w, ada_b, norm_ffn1_g, ffn1_w1, ffn1_w3, ffn1_w2, norm_mix_g, w_in, conv_w, q_norm_g, w_uq, kv_norm_g, w_ukv, out_norm_g, w_out, norm_ffn2_g, ffn2_w1, ffn2_w3, ffn2_w2, final_norm_g, loss_target, m_ada_w, m_ada_b, m_norm_ffn1_g, m_ffn1_w1, m_ffn1_w3, m_ffn1_w2, m_norm_mix_g, m_w_in, m_conv_w, m_q_norm_g, m_w_uq, m_kv_norm_g, m_w_ukv, m_out_norm_g, m_w_out, m_norm_ffn2_g, m_ffn2_w1, m_ffn2_w3, m_ffn2_w2, m_final_norm_g, v_ada_w, v_ada_b, v_norm_ffn1_g, v_ffn1_w1, v_ffn1_w3, v_ffn1_w2, v_norm_mix_g, v_w_in, v_conv_w, v_q_norm_g, v_w_uq, v_kv_norm_g, v_w_ukv, v_out_norm_g, v_w_out, v_norm_ffn2_g, v_ffn2_w1, v_ffn2_w3, v_ffn2_w2, v_final_norm_g):
    args = dict(locals())
    wts = {n: args[n] for n in WEIGHTS}
    mom = {n: args["m_" + n] for n in WEIGHTS}
    var = {n: args["v_" + n] for n in WEIGHTS}
    ax, ay, ac = _place()
    myq = 2 * ax + ay
    me = 2 * myq + ac
    place = jnp.stack([ac, myq]).astype(jnp.int32)

    shards = {name: _working_shard(wts[name], kind).astype(BF16) for name, kind in BULK}
    first = _run_comm(_Gather([shards[b[0]] for b in W_FIRST]), name="gather_ffn1")
    w_first = _gathered_weights(W_FIRST, shards, first, myq)

    mine = jnp.concatenate([c, conv_w[0].reshape(1, 3 * CONV_W // N_CHIPS)], axis=1)
    seen = _small_allgather(jnp.pad(mine, ((0, 7), (0, 0))), name="gather_cond").reshape(N_DEV, 8, -1)[:, 0]
    c_all = jnp.pad(seen[:, :D], ((0, 8), (0, 0)))
    conv_full = jnp.transpose(seen[0::2, D:].reshape(N_CHIPS, 3, CONV_W // N_CHIPS), (1, 0, 2)).reshape(3, CONV_W)
    ada_b_q = lax.dynamic_slice_in_dim(ada_b, myq * ADA_Q, ADA_Q, axis=1)
    mod_q = _ada_forward(c_all, ada_w[0], ada_b_q)
    mod_all = _small_allgather(mod_q, name="gather_mod").reshape(N_DEV, 16, ADA_Q)
    mod_rows = jnp.transpose(mod_all[0::2, :N_DEV], (1, 0, 2)).reshape(N_DEV, N_MOD * D)
    mod = lax.dynamic_slice_in_dim(mod_rows, me, 1, axis=0).reshape(N_MOD, D)

    vec = {n: wts[n] for n in ("norm_ffn1_g", "norm_mix_g", "q_norm_g", "kv_norm_g", "out_norm_g", "norm_ffn2_g")}
    vec["final_norm_g"] = final_norm_g.reshape(1, D)
    g, delta, new_m, new_v = {}, {}, {}, {}

    def adam(names, grads, comm=None):
        views = [(lambda a: jnp.swapaxes(a, 1, 2)) if KIND.get(n) == "colsT" else (lambda a: a) for n in names]
        res, couts = _adamw([vw(wts[n]) for n, vw in zip(names, views)], [grads[n] for n in names],
                            [vw(mom[n]) for n, vw in zip(names, views)], [vw(var[n]) for n, vw in zip(names, views)],
                            name="adamw_" + names[0], comm=comm)
        for n, vw, r in zip(names, views, res):
            g[n], delta[n], new_m[n], new_v[n] = [vw(a) for a in r]
        return couts

    loss_blk, grad_x, gq, dmod, small = _local_step(
        x[0], positions[0], loss_target[0], mod, vec, conv_full, w_first, {b[0]: shards[b[0]] for b in W_REST}, place,
        tail_host=lambda prog, grads: adam([b[0] for b in G_FFN2], grads, prog))
    loss = lax.psum(loss_blk[0, 0], ("x", "y", "c"))

    rows = jnp.concatenate([dmod] + [small[n] for n, _ in SMALL], axis=1)
    width = rows.shape[1]
    fold = -(-width // (8 * 128)) * 128
    rows = jnp.pad(rows, ((0, 0), (0, 8 * fold - width))).reshape(8, fold)
    every = _small_allgather(rows, name="gather_small").reshape(N_DEV, 8 * fold)[:, :width]
    total = _sum_devices(every)[0:1]
    dmod_q = lax.dynamic_slice_in_dim(every[:, :N_MOD * D], myq * ADA_Q, ADA_Q, axis=1)
    sg = {name: gq[name] for name, *_ in BULK}
    sg["ada_w"] = _ada_wgrad(c_all, jnp.pad(dmod_q, ((0, 8), (0, 0))))
    sg["ada_b"] = total[:, :N_MOD * D]
    off = N_MOD * D
    for n, width in SMALL:
        sg[n] = total[:, off:off + width]
        off += width
    sg["conv_w"] = lax.dynamic_slice_in_dim(sg["conv_w"].reshape(3, CONV_W), myq * (CONV_W // N_CHIPS),
                                            CONV_W // N_CHIPS, axis=1)

    for name in ["ada_w"] + [b[0] for b in BULK if b not in G_FFN2]:
        adam([name], sg)
    smalls = ["ada_b"] + [n for n, _ in SMALL]

    def packed(d):
        flat = jnp.concatenate([d[n].reshape(1, -1) for n in smalls], axis=1)
        return jnp.pad(flat.reshape(-1, D), ((0, 1), (0, 0)))

    res = _adamw([packed(wts)[None]], [packed(sg)], [packed(mom)[None]], [packed(var)[None]],
                 name="adamw_small")[0][0][1:]
    off = 0
    for n in smalls:
        size = wts[n].size
        for d, r in zip((delta, new_m, new_v), res):
            d[n] = r.reshape(-1)[off:off + size].reshape(wts[n].shape)
        g[n] = sg[n].reshape(wts[n].shape)
        off += size

    return (loss, grad_x[None], *[g[n] for n in WEIGHTS], *[delta[n] for n in WEIGHTS],
            *[new_m[n] for n in WEIGHTS], *[new_v[n] for n in WEIGHTS])
```

```python
import numpy as np
import jax
import jax.numpy as jnp
from jax import lax
from jax.experimental import pallas as pl
from jax.experimental.pallas import tpu as pltpu

F32 = jnp.float32
BF16 = jnp.bfloat16
MESH = pl.DeviceIdType.MESH

D = 1024
FF = 2816
CONV_W = 512
CONV_GROUP = 64
HEADS = 4
QK_NOPE = 128
QK_ROPE = 64
V_HEAD = 128
Q_LORA = 384
KV_LORA = 256
HEAD_SLOT = 256
IN_COLS = 3 * CONV_W + Q_LORA + KV_LORA + QK_ROPE
Z_COLS = 2304
EPS = 1e-6
ROPE_THETA = 10000.0
CHUNK = 64
ATT_SCALE = (QK_NOPE + QK_ROPE) ** -0.5
NEG = -1e30
EXP2_SCALE = ATT_SCALE * 1.4426950408889634
N_MOD = 9

LR, B1, B2, AEPS, WD, STEP = 0.001, 0.9, 0.999, 1e-08, 0.01, 10

N_CHIPS = 4
N_DEV = 8
VMEM_LIMIT = 56 << 20


def _params(sem, vmem=VMEM_LIMIT):
    return pltpu.CompilerParams(dimension_semantics=sem, vmem_limit_bytes=vmem)


def _rms(v):
    return lax.rsqrt(jnp.mean(v * v, axis=-1, keepdims=True) + EPS)


def _rsum8(v):
    t, n = v.shape
    return jnp.sum(v.reshape(t // 8, 8, n), axis=0)


def _all_rows(ref):
    ref[...] = jnp.broadcast_to(jnp.sum(ref[...], axis=0, keepdims=True), ref.shape)


def _gsum(v, gmat, split=False):
    hi = v.astype(BF16)
    out = jnp.dot(hi, gmat, preferred_element_type=F32)
    if split:
        out = out + jnp.dot((v - hi.astype(F32)).astype(BF16), gmat, preferred_element_type=F32)
    return out


def _dot_nt(a, b):
    return lax.dot_general(a, b, (((1,), (1,)), ((), ())), preferred_element_type=F32)


def _silu_parts(a):
    sg = jax.nn.sigmoid(a)
    return sg, a * sg


def _rope(xr, cs, sn, lane):
    rh = jnp.where(lane < 32, -pltpu.roll(xr, 96, 1), pltpu.roll(xr, 32, 1))
    return xr * cs + rh * sn


def _rope_t(g, cs, sn, lane):
    y = g * sn
    rt = jnp.where(lane < 32, pltpu.roll(y, 96, 1), jnp.where(lane < 64, -pltpu.roll(y, 32, 1), 0.0))
    return g * cs + rt


def _row_tile(rows, pref, mult=8):
    t = min(rows, pref) // mult * mult
    while rows % t:
        t -= mult
    return t


def _place():
    return lax.axis_index("x"), lax.axis_index("y"), lax.axis_index("c")


ANY = pl.BlockSpec(memory_space=pl.ANY)


def _hosted_call(body, *, name, grid, in_specs, out_specs, out_shape, scratch_shapes, semantics, args, comm=None,
                 prefetch=()):
    n_in, n_out, n_scr, n_pf = len(in_specs), len(out_specs), len(scratch_shapes), len(prefetch)

    def call(fn, in_specs, out_specs, out_shape, scratch_shapes, semantics, operands):
        spec = pltpu.PrefetchScalarGridSpec(num_scalar_prefetch=n_pf, grid=grid, in_specs=list(in_specs),
                                            out_specs=list(out_specs), scratch_shapes=list(scratch_shapes))
        return pl.pallas_call(fn, name=name, grid_spec=spec, out_shape=list(out_shape),
                              compiler_params=_params(semantics))(*prefetch, *operands)

    if comm is None:
        return list(call(body, in_specs, out_specs, out_shape, scratch_shapes, semantics, args)), []
    n_ci, n_co = len(comm.inputs), len(comm.out_shapes)
    total = int(np.prod(grid))

    def hosted(*refs):
        tables, refs = refs[:n_pf], refs[n_pf:]
        ins, refs = refs[:n_in], refs[n_in:]
        cins, refs = refs[:n_ci], refs[n_ci:]
        outs, refs = refs[:n_out], refs[n_out:]
        couts, refs = refs[:n_co], refs[n_co:]
        scratch, sems = refs[:n_scr], refs[n_scr]
        step = pl.program_id(0)
        for ax in range(1, len(grid)):
            step = step * grid[ax] + pl.program_id(ax)

        @pl.when(step == 0)
        def _():
            comm.start(cins, couts, sems)

        body(*tables, *ins, *outs, *scratch)

        @pl.when(step == total - 1)
        def _():
            comm.finish(cins, couts, sems)

    res = call(hosted, list(in_specs) + [ANY] * n_ci, list(out_specs) + [ANY] * n_co,
               list(out_shape) + list(comm.out_shapes),
               list(scratch_shapes) + [pltpu.SemaphoreType.DMA((comm.n_sems,))],
               ("arbitrary",) * len(grid), (*args, *comm.inputs))
    return list(res[:n_out]), list(res[n_out:])


def _run_comm(comm, *, name):
    n_ci = len(comm.inputs)

    def body(*refs):
        cins, couts, sems = refs[:n_ci], refs[n_ci:-1], refs[-1]
        comm.start(cins, couts, sems)
        comm.finish(cins, couts, sems)

    return list(pl.pallas_call(
        body, name=name, out_shape=list(comm.out_shapes), in_specs=[ANY] * n_ci,
        out_specs=[ANY] * len(comm.out_shapes), scratch_shapes=[pltpu.SemaphoreType.DMA((comm.n_sems,))],
    )(*comm.inputs))


class _Gather:
    def __init__(self, slabs):
        self.inputs = list(slabs)
        self.out_shapes = [jax.ShapeDtypeStruct((N_CHIPS,) + s.shape, s.dtype) for s in slabs]
        self.n_sems = 12 * len(slabs)

    @staticmethod
    def _copy(out, sems, base, k, chip, hc, to, src=None):
        H = out.shape[1] // 2
        half = out.at[2 * chip[0] + chip[1], pl.ds(hc * H, H), :]
        return pltpu.make_async_remote_copy(
            src_ref=half if src is None else src, dst_ref=half, send_sem=sems.at[base + k],
            recv_sem=sems.at[base + 6 + k], device_id=to, device_id_type=MESH)

    def _firsts(self, src, out, sems, base):
        x, y, c = _place()
        H = src.shape[0] // 2
        chips = [(1 - x, y), (x, 1 - y), (1 - x, 1 - y)]
        return [self._copy(out, sems, base, j, (x, y), c, (*chip, c), src=src.at[pl.ds(c * H, H), :])
                for j, chip in enumerate(chips)]

    def start(self, ins, outs, sems):
        for i, (src, out) in enumerate(zip(ins, outs)):
            for cp in self._firsts(src, out, sems, 12 * i):
                cp.start()

    def finish(self, ins, outs, sems):
        x, y, c = _place()
        chips = [(1 - x, y), (x, 1 - y), (1 - x, 1 - y)]
        passed = []
        for i, out in enumerate(outs):
            for j, chip in enumerate(chips):
                self._copy(out, sems, 12 * i, j, chip, c, (x, y, c)).wait_recv()
                cp = self._copy(out, sems, 12 * i, 3 + j, chip, c, (x, y, 1 - c))
                cp.start()
                passed.append(cp)
        for i, out in enumerate(outs):
            for j, chip in enumerate(chips):
                self._copy(out, sems, 12 * i, 3 + j, chip, 1 - c, (x, y, c)).wait_recv()
        for cp in passed:
            cp.wait_send()
        for i, (src, out) in enumerate(zip(ins, outs)):
            for cp in self._firsts(src, out, sems, 12 * i):
                cp.wait_send()


class _PairExchange:
    def __init__(self, arrays):
        self.inputs = list(arrays)
        self.out_shapes = [jax.ShapeDtypeStruct((N_CHIPS, a.shape[1] // 2, a.shape[2]), a.dtype) for a in arrays]
        self.n_sems = 2 * len(arrays)

    def _copies(self, ins, outs, sems):
        x, y, c = _place()
        return [pltpu.make_async_remote_copy(
            src_ref=g.at[:, pl.ds((1 - c) * t.shape[1], t.shape[1]), :], dst_ref=t, send_sem=sems.at[2 * i],
            recv_sem=sems.at[2 * i + 1], device_id=(x, y, 1 - c), device_id_type=MESH)
            for i, (g, t) in enumerate(zip(ins, outs))]

    def start(self, ins, outs, sems):
        for cp in self._copies(ins, outs, sems):
            cp.start()

    def finish(self, ins, outs, sems):
        for cp in self._copies(ins, outs, sems):
            cp.wait()


class _ChipExchange:
    def __init__(self, arrays):
        self.inputs = list(arrays)
        self.out_shapes = [jax.ShapeDtypeStruct(a.shape, a.dtype) for a in arrays]
        self.n_sems = 6 * len(arrays)

    def _copies(self, p, t, sems, base):
        x, y, c = _place()
        myq = 2 * x + y
        chips = [(1 - x, y), (x, 1 - y), (1 - x, 1 - y)]
        sends = [pltpu.make_async_remote_copy(
            src_ref=p.at[2 * chip[0] + chip[1]], dst_ref=t.at[myq], send_sem=sems.at[base + j],
            recv_sem=sems.at[base + 3 + j], device_id=(*chip, c), device_id_type=MESH) for j, chip in enumerate(chips)]
        lands = [pltpu.make_async_remote_copy(
            src_ref=t.at[2 * chip[0] + chip[1]], dst_ref=t.at[2 * chip[0] + chip[1]], send_sem=sems.at[base + j],
            recv_sem=sems.at[base + 3 + j], device_id=(*chip, c), device_id_type=MESH) for j, chip in enumerate(chips)]
        return sends, lands

    def start(self, ins, outs, sems):
        for i, (p, t) in enumerate(zip(ins, outs)):
            for cp in self._copies(p, t, sems, 6 * i)[0]:
                cp.start()

    def finish(self, ins, outs, sems):
        for i, (p, t) in enumerate(zip(ins, outs)):
            sends, lands = self._copies(p, t, sems, 6 * i)
            for cp in lands:
                cp.wait_recv()
            for cp in sends:
                cp.wait_send()


class _SemView:
    def __init__(self, sems, base):
        self._sems, self._base = sems, base

    @property
    def at(self):
        return self

    def __getitem__(self, k):
        return self._sems.at[self._base + k]


class _Multi:
    def __init__(self, progs):
        self.progs = list(progs)
        self.inputs = [a for p in self.progs for a in p.inputs]
        self.out_shapes = [s for p in self.progs for s in p.out_shapes]
        self.n_sems = sum(p.n_sems for p in self.progs)

    def _each(self, ins, outs, sems):
        i = o = s = 0
        for p in self.progs:
            ni, no = len(p.inputs), len(p.out_shapes)
            yield p, ins[i:i + ni], outs[o:o + no], _SemView(sems, s)
            i, o, s = i + ni, o + no, s + p.n_sems

    def start(self, ins, outs, sems):
        for p, a, b, c in self._each(ins, outs, sems):
            p.start(a, b, c)

    def finish(self, ins, outs, sems):
        for p, a, b, c in self._each(ins, outs, sems):
            p.finish(a, b, c)


class _PairShare:
    def __init__(self, arrays):
        self.inputs = list(arrays)
        self.out_shapes = [jax.ShapeDtypeStruct(a.shape, a.dtype) for a in arrays]
        self.n_sems = 2 * len(arrays)

    def _copies(self, ins, outs, sems):
        x, y, c = _place()
        return [pltpu.make_async_remote_copy(
            src_ref=r, dst_ref=o, send_sem=sems.at[2 * i], recv_sem=sems.at[2 * i + 1],
            device_id=(x, y, 1 - c), device_id_type=MESH) for i, (r, o) in enumerate(zip(ins, outs))]

    def start(self, ins, outs, sems):
        for cp in self._copies(ins, outs, sems):
            cp.start()

    def finish(self, ins, outs, sems):
        for cp in self._copies(ins, outs, sems):
            cp.wait()


def _ffn_up(x, ng, sh, sc, w1, w3, *, name, comm=None):
    S = x.shape[0]
    tm, tn = _row_tile(S, 512), FF

    def body(x_ref, g_ref, sh_ref, sc_ref, w1_ref, w3_ref, h_ref, a_ref, b_ref, u_ref, hs):
        @pl.when(pl.program_id(1) == 0)
        def _():
            xv = x_ref[...]
            h = ((xv * _rms(xv)) * g_ref[...]) * (1.0 + sc_ref[...]) + sh_ref[...]
            hb = h.astype(BF16)
            hs[...] = hb
            h_ref[...] = hb

        h = hs[...]
        cols = pl.ds(pl.multiple_of(pl.program_id(1) * tn, tn), tn)
        a = _dot_nt(h, w1_ref[cols, :])
        b = _dot_nt(h, w3_ref[cols, :])
        _, sa = _silu_parts(a)
        a_ref[...] = a.astype(BF16)
        b_ref[...] = b.astype(BF16)
        u_ref[...] = (sa * b).astype(BF16)

    row = pl.BlockSpec((tm, D), lambda i, j: (i, 0))
    vec = pl.BlockSpec((1, D), lambda i, j: (0, 0))
    wsp = pl.BlockSpec((FF, D), lambda i, j: (0, 0))
    osp = pl.BlockSpec((tm, tn), lambda i, j: (i, j))
    return _hosted_call(
        body, name=name, grid=(S // tm, FF // tn),
        in_specs=[row, vec, vec, vec, wsp, wsp],
        out_specs=[row, osp, osp, osp],
        out_shape=[jax.ShapeDtypeStruct((S, D), BF16)] + [jax.ShapeDtypeStruct((S, FF), BF16)] * 3,
        scratch_shapes=[pltpu.VMEM((tm, D), BF16)],
        semantics=("parallel", "arbitrary"), args=(x, ng, sh, sc, w1, w3), comm=comm)


def _ffn_down(u, w2, x, gate, *, name, comm=None):
    S = x.shape[0]
    tm = _row_tile(S, 512)

    def body(u_ref, w2_ref, x_ref, g_ref, xo_ref, f_ref):
        f = jnp.dot(u_ref[...], w2_ref[...], preferred_element_type=F32)
        xo_ref[...] = x_ref[...] + (0.5 * g_ref[...]) * f
        f_ref[...] = f.astype(BF16)

    return _hosted_call(
        body, name=name, grid=(S // tm,),
        in_specs=[pl.BlockSpec((tm, FF), lambda i: (i, 0)), pl.BlockSpec((FF, D), lambda i: (0, 0)),
                  pl.BlockSpec((tm, D), lambda i: (i, 0)), pl.BlockSpec((1, D), lambda i: (0, 0))],
        out_specs=[pl.BlockSpec((tm, D), lambda i: (i, 0))] * 2,
        out_shape=[jax.ShapeDtypeStruct((S, D), F32), jax.ShapeDtypeStruct((S, D), BF16)],
        scratch_shapes=[], semantics=("parallel",), args=(u, w2, x, gate), comm=comm)


def _ffn_bwd_du(dx, gate, f, w2, a, b, *, name, comm=None):
    S = dx.shape[0]
    tm, tn = _row_tile(S, 256), FF
    n_i = S // tm

    def body(dx_ref, g_ref, f_ref, w_ref, a_ref, b_ref, df_ref, da_ref, db_ref, dg_ref, dfs):
        i, j = pl.program_id(0), pl.program_id(1)

        @pl.when((i == 0) & (j == 0))
        def _():
            dg_ref[...] = jnp.zeros_like(dg_ref)

        @pl.when(j == 0)
        def _():
            dxv = dx_ref[...]
            dfb = (dxv * (0.5 * g_ref[...])).astype(BF16)
            dfs[...] = dfb
            df_ref[...] = dfb
            dg_ref[...] += _rsum8(dxv * (0.5 * f_ref[...].astype(F32)))

        du = _dot_nt(dfs[...], w_ref[pl.ds(pl.multiple_of(j * tn, tn), tn), :])
        av = a_ref[...].astype(F32)
        sg, sa = _silu_parts(av)
        da_ref[...] = (du * b_ref[...].astype(F32) * (sg * (1.0 + av * (1.0 - sg)))).astype(BF16)
        db_ref[...] = (du * sa).astype(BF16)

        @pl.when((i == n_i - 1) & (j == FF // tn - 1))
        def _():
            _all_rows(dg_ref)

    row = pl.BlockSpec((tm, D), lambda i, j: (i, 0))
    blk = pl.BlockSpec((tm, tn), lambda i, j: (i, j))
    return _hosted_call(
        body, name=name, grid=(n_i, FF // tn),
        in_specs=[row, pl.BlockSpec((1, D), lambda i, j: (0, 0)), row,
                  pl.BlockSpec((FF, D), lambda i, j: (0, 0)), blk, blk],
        out_specs=[row, blk, blk, pl.BlockSpec((8, D), lambda i, j: (0, 0))],
        out_shape=[jax.ShapeDtypeStruct((S, D), BF16), jax.ShapeDtypeStruct((S, FF), BF16),
                   jax.ShapeDtypeStruct((S, FF), BF16), jax.ShapeDtypeStruct((8, D), F32)],
        scratch_shapes=[pltpu.VMEM((tm, D), BF16)],
        semantics=("arbitrary", "arbitrary"), args=(dx, gate, f, w2, a, b), comm=comm)


def _tn_matmul(a, b, *, tm, tn, name, comm=None):
    S, M = a.shape
    N = b.shape[1]
    ts = _row_tile(S, 2048 if tm * tn <= 1408 * 1024 else 1024)
    ns = S // ts

    def body(a_ref, b_ref, o_ref):
        s = pl.program_id(2)
        p = lax.dot_general(a_ref[...], b_ref[...], (((0,), (0,)), ((), ())), preferred_element_type=F32)

        @pl.when(s == 0)
        def _():
            o_ref[...] = p

        @pl.when(s > 0)
        def _():
            o_ref[...] += p

    (out,), couts = _hosted_call(
        body, name=name, grid=(M // tm, N // tn, ns),
        in_specs=[pl.BlockSpec((ts, tm), lambda i, j, s: (s, i)), pl.BlockSpec((ts, tn), lambda i, j, s: (s, j))],
        out_specs=[pl.BlockSpec((tm, tn), lambda i, j, s: (i, j))],
        out_shape=[jax.ShapeDtypeStruct((M, N), F32)],
        scratch_shapes=[], semantics=("parallel", "parallel", "arbitrary"), args=(a, b), comm=comm)
    return out if comm is None else (out, couts)


def _dh_normbwd(pairs, x, ng, sc, dx_next, *, name, comm=None):
    S = x.shape[0]
    n_p = len(pairs)
    tm = _row_tile(S, 512)
    n_i = S // tm

    def body(*refs):
        a_refs, w_refs = refs[:n_p], refs[n_p:2 * n_p]
        x_ref, g_ref, sc_ref, dxn_ref, dx_ref, p_ref = refs[2 * n_p:]
        i = pl.program_id(0)
        dh = jnp.dot(a_refs[0][...], w_refs[0][...], preferred_element_type=F32)
        for k in range(1, n_p):
            dh = dh + jnp.dot(a_refs[k][...], w_refs[k][...], preferred_element_type=F32)
        xv = x_ref[...]
        r = _rms(xv)
        xh = xv * r
        g = g_ref[...]
        dn = dh * (1.0 + sc_ref[...])
        dy = dn * g
        dx_ref[...] = dxn_ref[...] + r * (dy - xh * jnp.mean(dy * xh, axis=-1, keepdims=True))

        @pl.when(i == 0)
        def _():
            p_ref[...] = jnp.zeros_like(p_ref)

        p_ref[:, 0:D] += _rsum8(dh * (xh * g))
        p_ref[:, D:2 * D] += _rsum8(dh)
        p_ref[:, 2 * D:3 * D] += _rsum8(dn * xh)

        @pl.when(i == n_i - 1)
        def _():
            _all_rows(p_ref)

    row = pl.BlockSpec((tm, D), lambda i: (i, 0))
    vec = pl.BlockSpec((1, D), lambda i: (0, 0))
    in_specs = ([pl.BlockSpec((tm, a.shape[1]), lambda i: (i, 0)) for a, _ in pairs]
                + [pl.BlockSpec(w.shape, lambda i: (0, 0), pipeline_mode=pl.Buffered(1)) for _, w in pairs]
                + [row, vec, vec, row])
    return _hosted_call(
        body, name=name, grid=(n_i,), in_specs=in_specs,
        out_specs=[row, pl.BlockSpec((8, 3 * D), lambda i: (0, 0))],
        out_shape=[jax.ShapeDtypeStruct((S, D), F32), jax.ShapeDtypeStruct((8, 3 * D), F32)],
        scratch_shapes=[], semantics=("arbitrary",),
        args=(*[a for a, _ in pairs], *[w for _, w in pairs], x, ng, sc, dx_next), comm=comm)


def _ffn_down_loss(u, w2, x, gate, gfin, tgt):
    S = x.shape[0]
    tm = _row_tile(S, 512)
    n_i = S // tm

    def body(u_ref, w2_ref, x_ref, gt_ref, g_ref, t_ref, dx_ref, f_ref, dg_ref, loss_ref, lacc):
        i = pl.program_id(0)
        f = jnp.dot(u_ref[...], w2_ref[...], preferred_element_type=F32)
        f_ref[...] = f.astype(BF16)
        xv = x_ref[...] + (0.5 * gt_ref[...]) * f
        r = _rms(xv)
        xh = xv * r
        g = g_ref[...]
        e = xh * g - t_ref[...]
        dout = e * (1.0 / D)
        dy = dout * g
        dx_ref[...] = r * (dy - xh * jnp.mean(dy * xh, axis=-1, keepdims=True))

        @pl.when(i == 0)
        def _():
            dg_ref[...] = jnp.zeros_like(dg_ref)
            lacc[...] = jnp.zeros_like(lacc)

        dg_ref[...] += _rsum8(dout * xh)
        lacc[...] += _rsum8(e * e)

        @pl.when(i == n_i - 1)
        def _():
            _all_rows(dg_ref)
            tot = jnp.sum(jnp.sum(lacc[...], axis=0, keepdims=True), axis=1, keepdims=True)
            loss_ref[...] = jnp.broadcast_to(tot * (0.5 / D), loss_ref.shape)

    row = pl.BlockSpec((tm, D), lambda i: (i, 0))
    vec = pl.BlockSpec((1, D), lambda i: (0, 0))
    return pl.pallas_call(
        body, name="ffn2_down_loss", grid=(n_i,),
        in_specs=[pl.BlockSpec((tm, FF), lambda i: (i, 0)), pl.BlockSpec((FF, D), lambda i: (0, 0)), row, vec, vec, row],
        out_specs=[row, row, pl.BlockSpec((8, D), lambda i: (0, 0)), pl.BlockSpec((8, 128), lambda i: (0, 0))],
        out_shape=[jax.ShapeDtypeStruct((S, D), F32), jax.ShapeDtypeStruct((S, D), BF16),
                   jax.ShapeDtypeStruct((8, D), F32), jax.ShapeDtypeStruct((8, 128), F32)],
        scratch_shapes=[pltpu.VMEM((8, D), F32)],
        compiler_params=_params(("arbitrary",)),
    )(u, w2, x, gate, gfin, tgt)


def _mix_in(x, ng, sh, sc, w_in, comm=None):
    S = x.shape[0]
    tm = _row_tile(S, 512)

    def body(x_ref, g_ref, sh_ref, sc_ref, w_ref, h_ref, z_ref):
        xv = x_ref[...]
        hb = (((xv * _rms(xv)) * g_ref[...]) * (1.0 + sc_ref[...]) + sh_ref[...]).astype(BF16)
        h_ref[...] = hb
        z_ref[...] = _dot_nt(hb, w_ref[...])

    row = pl.BlockSpec((tm, D), lambda i: (i, 0))
    vec = pl.BlockSpec((1, D), lambda i: (0, 0))
    return _hosted_call(
        body, name="mix_in", grid=(S // tm,),
        in_specs=[row, vec, vec, vec, pl.BlockSpec((Z_COLS, D), lambda i: (0, 0))],
        out_specs=[row, pl.BlockSpec((tm, Z_COLS), lambda i: (i, 0))],
        out_shape=[jax.ShapeDtypeStruct((S, D), BF16), jax.ShapeDtypeStruct((S, Z_COLS), F32)],
        scratch_shapes=[], semantics=("parallel",), args=(x, ng, sh, sc, w_in), comm=comm)


def _conv_taps(u, halo, rows):
    u1 = jnp.where(rows == 0, halo[7:8, :], pltpu.roll(u, 1, 0))
    u2 = jnp.where(rows == 0, halo[6:7, :], jnp.where(rows == 1, halo[7:8, :], pltpu.roll(u, 2, 0)))
    return u1, u2


def _mix_mid(z, conv_w, gq, gkv, wuq, wukv, cs, sn, comm=None):
    S = z.shape[0]
    tm = _row_tile(S, 512)
    hb = tm // 8

    def body(z_ref, zh_ref, cw_ref, gq_ref, gkv_ref, wuq_ref, wukv_ref, cs_ref, sn_ref,
             ya_ref, q_ref, k_ref, v_ref, cqn_ref, ckvn_ref):
        i = pl.program_id(0)
        xb = z_ref[:, 0:CONV_W]
        u = z_ref[:, CONV_W:2 * CONV_W] * z_ref[:, 2 * CONV_W:3 * CONV_W]
        halo = zh_ref[:, CONV_W:2 * CONV_W] * zh_ref[:, 2 * CONV_W:3 * CONV_W]
        halo = jnp.where(i > 0, halo, 0.0)
        rows = lax.broadcasted_iota(jnp.int32, (tm, CONV_W), 0)
        u1, u2 = _conv_taps(u, halo, rows)
        y = cw_ref[0:1, :] * u2 + cw_ref[1:2, :] * u1 + cw_ref[2:3, :] * u
        ya_ref[...] = xb * y

        lane = lax.broadcasted_iota(jnp.int32, (tm, 128), 1)
        cs_v, sn_v = cs_ref[...], sn_ref[...]
        cq = z_ref[:, 3 * CONV_W:3 * CONV_W + Q_LORA]
        cqn = ((cq * _rms(cq)) * gq_ref[...]).astype(BF16)
        cqn_ref[...] = cqn
        q = _dot_nt(cqn, wuq_ref[...])
        for h in range(HEADS):
            o = h * HEAD_SLOT
            q_ref[:, o:o + 128] = q[:, o:o + 128].astype(BF16)
            q_ref[:, o + 128:o + 256] = _rope(q[:, o + 128:o + 256], cs_v, sn_v, lane).astype(BF16)

        c0 = 3 * CONV_W + Q_LORA
        ckv = z_ref[:, c0:c0 + KV_LORA]
        ckvn = ((ckv * _rms(ckv)) * gkv_ref[...]).astype(BF16)
        ckvn_ref[...] = ckvn
        kv = jnp.dot(ckvn, wukv_ref[...], preferred_element_type=F32)
        krot = _rope(z_ref[:, c0 + KV_LORA:Z_COLS], cs_v, sn_v, lane).astype(BF16)
        for h in range(HEADS):
            o = h * HEAD_SLOT
            k_ref[:, o:o + 128] = kv[:, h * 128:(h + 1) * 128].astype(BF16)
            k_ref[:, o + 128:o + 256] = krot
        v_ref[...] = kv[:, HEADS * 128:].astype(BF16)

    def rows_of(n):
        return pl.BlockSpec((tm, n), lambda i: (i, 0))

    def whole(shape):
        return pl.BlockSpec(shape, lambda i: (0, 0))

    return _hosted_call(
        body, name="mix_mid", grid=(S // tm,),
        in_specs=[rows_of(Z_COLS), pl.BlockSpec((8, Z_COLS), lambda i: (jnp.maximum(i * hb - 1, 0), 0)),
                  whole((8, CONV_W)), whole((1, Q_LORA)), whole((1, KV_LORA)),
                  whole((HEADS * HEAD_SLOT, Q_LORA)), whole((KV_LORA, 2 * HEADS * 128)),
                  rows_of(128), rows_of(128)],
        out_specs=[rows_of(CONV_W), rows_of(HEADS * HEAD_SLOT), rows_of(HEADS * HEAD_SLOT), rows_of(HEADS * V_HEAD),
                   rows_of(Q_LORA), rows_of(KV_LORA)],
        out_shape=[jax.ShapeDtypeStruct((S, CONV_W), F32), jax.ShapeDtypeStruct((S, HEADS * HEAD_SLOT), BF16),
                   jax.ShapeDtypeStruct((S, HEADS * HEAD_SLOT), BF16), jax.ShapeDtypeStruct((S, HEADS * V_HEAD), BF16),
                   jax.ShapeDtypeStruct((S, Q_LORA), BF16), jax.ShapeDtypeStruct((S, KV_LORA), BF16)],
        scratch_shapes=[], semantics=("parallel",), args=(z, z, conv_w, gq, gkv, wuq, wukv, cs, sn), comm=comm)


def _att_blocks(S):
    bk = min(1024, max(S // 4, 128))
    return bk, bk


def _pair_tables(S, k_major):
    bq, bk = _att_blocks(S)
    nq, nk = S // bq, S // bk
    vis = lambda qi, ki: ki * bk < (qi + 1) * bq
    if k_major:
        pairs = [(qi, ki) for ki in range(nk) for qi in range(nq) if vis(qi, ki)]
    else:
        pairs = [(qi, ki) for qi in range(nq) for ki in range(nk) if vis(qi, ki)]
    cols = [[p[0] for p in pairs], [p[1] for p in pairs], [int((p[1] + 1) * bk > p[0] * bq) for p in pairs]]
    return [jnp.asarray(np.array(c, np.int32)) for c in cols], len(pairs)


def _chunk_mask(r0, nr, nc):
    r = (r0 + lax.broadcasted_iota(jnp.int32, (nr, nc), 0)) // CHUNK
    c = lax.broadcasted_iota(jnp.int32, (nr, nc), 1) // CHUNK
    return c <= r


def _diag_parts(bq, bk):
    return [(0, bq // 2, bk // 2), (bq // 2, bq // 2, bk)]


def _attention(q, k, v, comm=None):
    S = q.shape[0]
    bq, bk = _att_blocks(S)
    last_k = bq // bk - 1
    tables, n_pairs = _pair_tables(S, k_major=False)

    def body(qi_ref, ki_ref, mk_ref, q_ref, k_ref, v_ref, o_ref, lse_ref, m_s, l_s, acc_s):
        p_id = pl.program_id(1)
        qi, ki = qi_ref[p_id], ki_ref[p_id]

        @pl.when(ki == 0)
        def _():
            m_s[...] = jnp.full_like(m_s, NEG)
            l_s[...] = jnp.zeros_like(l_s)
            acc_s[...] = jnp.zeros_like(acc_s)

        def update(r0, nr, nc, masked):
            rows = slice(r0, r0 + nr)
            s = _dot_nt(q_ref[rows, :], k_ref[0:nc, :])
            if masked:
                s = jnp.where(_chunk_mask(r0, nr, nc), s, NEG)
            m_prev = m_s[rows, :]
            m_new = jnp.maximum(m_prev, jnp.max(s, axis=1, keepdims=True))
            alpha = jnp.exp2((m_prev - m_new) * EXP2_SCALE)
            p = jnp.exp2((s - jnp.tile(m_new, (1, nc // 128))) * EXP2_SCALE)
            l_s[rows, :] = alpha * l_s[rows, :] + jnp.sum(p, axis=1, keepdims=True)
            acc_s[rows, :] = alpha * acc_s[rows, :] + jnp.dot(p.astype(BF16), v_ref[0:nc, :],
                                                              preferred_element_type=F32)
            m_s[rows, :] = m_new

        @pl.when(mk_ref[p_id] == 0)
        def _():
            update(0, bq, bk, False)

        @pl.when(mk_ref[p_id] == 1)
        def _():
            for part in _diag_parts(bq, bk):
                update(*part, True)

        @pl.when(ki == qi * (last_k + 1) + last_k)
        def _():
            l = l_s[...]
            o_ref[...] = acc_s[...] / l
            lse_ref[...] = m_s[...] * EXP2_SCALE + jnp.log2(l)

    return _hosted_call(
        body, name="attention", grid=(HEADS, n_pairs),
        in_specs=[pl.BlockSpec((bq, HEAD_SLOT), lambda h, p, qt, kt, mt: (qt[p], h)),
                  pl.BlockSpec((bk, HEAD_SLOT), lambda h, p, qt, kt, mt: (kt[p], h)),
                  pl.BlockSpec((bk, V_HEAD), lambda h, p, qt, kt, mt: (kt[p], h))],
        out_specs=[pl.BlockSpec((bq, V_HEAD), lambda h, p, qt, kt, mt: (qt[p], h))] * 2,
        out_shape=[jax.ShapeDtypeStruct((S, HEADS * V_HEAD), F32)] * 2,
        scratch_shapes=[pltpu.VMEM((bq, V_HEAD), F32)] * 3,
        semantics=("arbitrary", "arbitrary"), args=(q, k, v), comm=comm, prefetch=tables)


def _attention_bwd(q, k, v, do, lse2, delta):
    S = q.shape[0]
    bq, bk = _att_blocks(S)
    nq = S // bq
    tables, n_pairs = _pair_tables(S, k_major=True)

    def body(qi_ref, ki_ref, mk_ref, q_ref, k_ref, v_ref, do_ref, lse_ref, dl_ref, dq_hbm, dk_ref, dv_ref,
             dq_s, dq_b, dk_s, dv_s, sem):
        head, p_id = pl.program_id(0), pl.program_id(1)
        qi, ki = qi_ref[p_id], ki_ref[p_id]

        @pl.when(qi * bq <= ki * bk)
        def _():
            dk_s[...] = jnp.zeros_like(dk_s)
            dv_s[...] = jnp.zeros_like(dv_s)

        def update(r0, nr, nc, masked):
            rows, cols = slice(r0, r0 + nr), slice(0, nc)
            qv, kv, dov = q_ref[rows, :], k_ref[cols, :], do_ref[rows, :]
            s = _dot_nt(qv, kv)
            dp = _dot_nt(dov, v_ref[cols, :])
            if masked:
                s = jnp.where(_chunk_mask(r0, nr, nc), s, NEG)
            p = jnp.exp2(s * EXP2_SCALE - jnp.tile(lse_ref[rows, :], (1, nc // 128)))
            dv_s[cols, :] += lax.dot_general(p.astype(BF16), dov, (((0,), (0,)), ((), ())),
                                             preferred_element_type=F32)
            ds = (p * (dp - jnp.tile(dl_ref[rows, :], (1, nc // 128)))).astype(BF16)
            dk_s[cols, :] += lax.dot_general(ds, qv, (((0,), (0,)), ((), ())), preferred_element_type=F32)
            dq = jnp.dot(ds, kv, preferred_element_type=F32)
            out_rows = pl.ds(pl.multiple_of(qi * bq + r0, nr), nr)

            @pl.when(ki == 0)
            def _():
                dq_s[out_rows, :] = dq

            @pl.when(ki > 0)
            def _():
                dq_s[out_rows, :] += dq

        @pl.when(mk_ref[p_id] == 0)
        def _():
            update(0, bq, bk, False)

        @pl.when(mk_ref[p_id] == 1)
        def _():
            for part in _diag_parts(bq, bk):
                update(*part, True)

        @pl.when(qi == nq - 1)
        def _():
            dk_ref[...] = (dk_s[...] * ATT_SCALE).astype(BF16)
            dv_ref[...] = dv_s[...].astype(BF16)

        @pl.when(p_id == n_pairs - 1)
        def _():
            dq_b[...] = (dq_s[...] * ATT_SCALE).astype(BF16)
            out = pltpu.make_async_copy(
                dq_b, dq_hbm.at[:, pl.ds(pl.multiple_of(head * HEAD_SLOT, HEAD_SLOT), HEAD_SLOT)], sem)
            out.start()
            out.wait()

    grid_spec = pltpu.PrefetchScalarGridSpec(
        num_scalar_prefetch=3, grid=(HEADS, n_pairs),
        in_specs=[pl.BlockSpec((bq, HEAD_SLOT), lambda h, p, qt, kt, mt: (qt[p], h)),
                  pl.BlockSpec((bk, HEAD_SLOT), lambda h, p, qt, kt, mt: (kt[p], h)),
                  pl.BlockSpec((bk, V_HEAD), lambda h, p, qt, kt, mt: (kt[p], h)),
                  pl.BlockSpec((bq, V_HEAD), lambda h, p, qt, kt, mt: (qt[p], h)),
                  pl.BlockSpec((bq, V_HEAD), lambda h, p, qt, kt, mt: (qt[p], h)),
                  pl.BlockSpec((bq, V_HEAD), lambda h, p, qt, kt, mt: (qt[p], h))],
        out_specs=[pl.BlockSpec(memory_space=pl.ANY),
                   pl.BlockSpec((bk, HEAD_SLOT), lambda h, p, qt, kt, mt: (kt[p], h)),
                   pl.BlockSpec((bk, V_HEAD), lambda h, p, qt, kt, mt: (kt[p], h))],
        scratch_shapes=[pltpu.VMEM((S, HEAD_SLOT), F32), pltpu.VMEM((S, HEAD_SLOT), BF16),
                        pltpu.VMEM((bk, HEAD_SLOT), F32), pltpu.VMEM((bk, V_HEAD), F32), pltpu.SemaphoreType.DMA])
    return pl.pallas_call(
        body, name="attention_bwd", grid_spec=grid_spec,
        out_shape=[jax.ShapeDtypeStruct((S, HEADS * HEAD_SLOT), BF16), jax.ShapeDtypeStruct((S, HEADS * HEAD_SLOT), BF16),
                   jax.ShapeDtypeStruct((S, HEADS * V_HEAD), BF16)],
        compiler_params=_params(("arbitrary", "arbitrary")),
    )(*tables, q, k, v, do, lse2, delta)


def _group_mats():
    def blockdiag(n, g):
        idx = np.arange(n) // g
        return jnp.asarray((idx[:, None] == idx[None, :]).astype(np.float32), dtype=BF16)
    return blockdiag(CONV_W, CONV_GROUP), blockdiag(HEADS * V_HEAD, V_HEAD)


def _mix_out(ya, o, gout, w_out, x, gate, ga, gb):
    S = x.shape[0]
    tm = _row_tile(S, 512)

    def body(ya_ref, o_ref, go_ref, w_ref, x_ref, g_ref, ga_ref, gb_ref, xo_ref, yn_ref, yo_ref):
        yav, ov = ya_ref[...], o_ref[...]
        ra = lax.rsqrt(_gsum(yav * yav, ga_ref[...]) * (1.0 / CONV_GROUP) + EPS)
        rb = lax.rsqrt(_gsum(ov * ov, gb_ref[...]) * (1.0 / V_HEAD) + EPS)
        na = ((yav * ra) * go_ref[:, 0:CONV_W]).astype(BF16)
        nb = ((ov * rb) * go_ref[:, CONV_W:]).astype(BF16)
        yn_ref[:, 0:CONV_W] = na
        yn_ref[:, CONV_W:] = nb
        yo = (jnp.dot(na, w_ref[0:CONV_W, :], preferred_element_type=F32)
              + jnp.dot(nb, w_ref[CONV_W:, :], preferred_element_type=F32))
        xo_ref[...] = x_ref[...] + g_ref[...] * yo
        yo_ref[...] = yo.astype(BF16)

    row = pl.BlockSpec((tm, D), lambda i: (i, 0))
    half = pl.BlockSpec((tm, CONV_W), lambda i: (i, 0))
    vec = pl.BlockSpec((1, D), lambda i: (0, 0))
    sq = pl.BlockSpec((CONV_W, CONV_W), lambda i: (0, 0))
    return pl.pallas_call(
        body, name="mix_out", grid=(S // tm,),
        in_specs=[half, half, vec, pl.BlockSpec((D, D), lambda i: (0, 0)), row, vec, sq, sq],
        out_specs=[row, row, row],
        out_shape=[jax.ShapeDtypeStruct((S, D), F32), jax.ShapeDtypeStruct((S, D), BF16),
                   jax.ShapeDtypeStruct((S, D), BF16)],
        compiler_params=_params(("parallel",)),
    )(ya, o, gout, w_out, x, gate, ga, gb)


def _mix_out_bwd(dx, gate, yo, w_out, ya, o, gout, ga, gb, comm=None):
    S = dx.shape[0]
    tm = _row_tile(S, 512)
    n_i = S // tm

    def norm_bwd(v, dn, gain, gmat, inv_n):
        r = lax.rsqrt(_gsum(v * v, gmat) * inv_n + EPS)
        vh = v * r
        dy = dn * gain
        return r * (dy - vh * (_gsum(dy * vh, gmat) * inv_n)), dn * vh

    def body(dx_ref, g_ref, yo_ref, w_ref, ya_ref, o_ref, go_ref, ga_ref, gb_ref,
             dyo_ref, dya_ref, do_ref, dl_ref, p_ref):
        i = pl.program_id(0)
        dxv = dx_ref[...]
        dyo = (dxv * g_ref[...]).astype(BF16)
        dyo_ref[...] = dyo
        dyn = _dot_nt(dyo, w_ref[...])
        dya, dga = norm_bwd(ya_ref[...], dyn[:, 0:CONV_W], go_ref[:, 0:CONV_W], ga_ref[...], 1.0 / CONV_GROUP)
        ov = o_ref[...]
        do, dgb = norm_bwd(ov, dyn[:, CONV_W:], go_ref[:, CONV_W:], gb_ref[...], 1.0 / V_HEAD)
        dya_ref[...] = dya
        do_ref[...] = do.astype(BF16)
        dl_ref[...] = _gsum(do * ov, gb_ref[...], split=True)

        @pl.when(i == 0)
        def _():
            p_ref[...] = jnp.zeros_like(p_ref)

        p_ref[:, 0:D] += _rsum8(dxv * yo_ref[...].astype(F32))
        p_ref[:, D:D + CONV_W] += _rsum8(dga)
        p_ref[:, D + CONV_W:2 * D] += _rsum8(dgb)

        @pl.when(i == n_i - 1)
        def _():
            _all_rows(p_ref)

    row = pl.BlockSpec((tm, D), lambda i: (i, 0))
    half = pl.BlockSpec((tm, CONV_W), lambda i: (i, 0))
    vec = pl.BlockSpec((1, D), lambda i: (0, 0))
    sq = pl.BlockSpec((CONV_W, CONV_W), lambda i: (0, 0))
    return _hosted_call(
        body, name="mix_out_bwd", grid=(n_i,),
        in_specs=[row, vec, row, pl.BlockSpec((D, D), lambda i: (0, 0)), half, half, vec, sq, sq],
        out_specs=[row, half, half, half, pl.BlockSpec((8, 2 * D), lambda i: (0, 0))],
        out_shape=[jax.ShapeDtypeStruct((S, D), BF16), jax.ShapeDtypeStruct((S, CONV_W), F32),
                   jax.ShapeDtypeStruct((S, CONV_W), BF16), jax.ShapeDtypeStruct((S, CONV_W), F32),
                   jax.ShapeDtypeStruct((8, 2 * D), F32)],
        scratch_shapes=[], semantics=("arbitrary",), args=(dx, gate, yo, w_out, ya, o, gout, ga, gb), comm=comm)


MID_SUMS = 3 * CONV_W + Q_LORA + KV_LORA


def _mix_mid_bwd(z, dya, conv_w, gq, gkv, wuq, wukv, cs, sn, dq, dk, dv, comm=None):
    S = z.shape[0]
    tm = _row_tile(S, 256)
    n_i = S // tm
    hb = tm // 8
    last_blk = S // 8 - 1

    def latent_bwd(cv, dcn, gain):
        r = _rms(cv)
        ch = cv * r
        dy = dcn * gain
        return r * (dy - ch * jnp.mean(dy * ch, axis=-1, keepdims=True)), dcn * ch

    def body(z_ref, zp_ref, zn_ref, dya_ref, dyan_ref, cw_ref, gq_ref, gkv_ref, wuq_ref, wukv_ref, cs_ref, sn_ref,
             dq_ref, dk_ref, dv_ref, dz_ref, dqf_ref, dkvf_ref, p_ref):
        i = pl.program_id(0)
        xb, xc, xu = z_ref[:, 0:CONV_W], z_ref[:, CONV_W:2 * CONV_W], z_ref[:, 2 * CONV_W:3 * CONV_W]
        u = xc * xu
        halo = jnp.where(i > 0, zp_ref[:, CONV_W:2 * CONV_W] * zp_ref[:, 2 * CONV_W:3 * CONV_W], 0.0)
        rows = lax.broadcasted_iota(jnp.int32, (tm, CONV_W), 0)
        u1, u2 = _conv_taps(u, halo, rows)
        w0, w1, w2 = cw_ref[0:1, :], cw_ref[1:2, :], cw_ref[2:3, :]
        y = w0 * u2 + w1 * u1 + w2 * u
        dyav = dya_ref[...]
        dy = dyav * xb
        nxt = jnp.where(i < n_i - 1, dyan_ref[...] * zn_ref[:, 0:CONV_W], 0.0)
        dy1 = jnp.where(rows == tm - 1, nxt[0:1, :], pltpu.roll(dy, tm - 1, 0))
        dy2 = jnp.where(rows == tm - 1, nxt[1:2, :], jnp.where(rows == tm - 2, nxt[0:1, :], pltpu.roll(dy, tm - 2, 0)))
        du = w2 * dy + w1 * dy1 + w0 * dy2
        dz_ref[:, 0:CONV_W] = (dyav * y).astype(BF16)
        dz_ref[:, CONV_W:2 * CONV_W] = (du * xu).astype(BF16)
        dz_ref[:, 2 * CONV_W:3 * CONV_W] = (du * xc).astype(BF16)

        lane = lax.broadcasted_iota(jnp.int32, (tm, 128), 1)
        cs_v, sn_v = cs_ref[...], sn_ref[...]
        dkr = jnp.zeros((tm, 128), F32)
        for h in range(HEADS):
            o = h * HEAD_SLOT
            dqf_ref[:, o:o + 128] = dq_ref[:, o:o + 128]
            dqf_ref[:, o + 128:o + 256] = _rope_t(dq_ref[:, o + 128:o + 256].astype(F32), cs_v, sn_v, lane).astype(BF16)
            dkvf_ref[:, h * 128:(h + 1) * 128] = dk_ref[:, o:o + 128]
            dkr = dkr + dk_ref[:, o + 128:o + 256].astype(F32)
        dkvf_ref[:, HEADS * 128:] = dv_ref[...]

        c0 = 3 * CONV_W
        dcqn = jnp.dot(dqf_ref[...], wuq_ref[...], preferred_element_type=F32)
        dcq, dgq = latent_bwd(z_ref[:, c0:c0 + Q_LORA], dcqn, gq_ref[...])
        dz_ref[:, c0:c0 + Q_LORA] = dcq.astype(BF16)
        c1 = c0 + Q_LORA
        dckvn = _dot_nt(dkvf_ref[...], wukv_ref[...])
        dckv, dgkv = latent_bwd(z_ref[:, c1:c1 + KV_LORA], dckvn, gkv_ref[...])
        dz_ref[:, c1:c1 + KV_LORA] = dckv.astype(BF16)
        dz_ref[:, c1 + KV_LORA:Z_COLS] = _rope_t(dkr, cs_v, sn_v, lane).astype(BF16)

        @pl.when(i == 0)
        def _():
            p_ref[...] = jnp.zeros_like(p_ref)

        p_ref[:, 0:CONV_W] += _rsum8(dy * u2)
        p_ref[:, CONV_W:2 * CONV_W] += _rsum8(dy * u1)
        p_ref[:, 2 * CONV_W:3 * CONV_W] += _rsum8(dy * u)
        p_ref[:, c0:c0 + Q_LORA] += _rsum8(dgq)
        p_ref[:, c1:c1 + KV_LORA] += _rsum8(dgkv)

        @pl.when(i == n_i - 1)
        def _():
            _all_rows(p_ref)

    def rows_of(n):
        return pl.BlockSpec((tm, n), lambda i: (i, 0))

    def whole(shape):
        return pl.BlockSpec(shape, lambda i: (0, 0))

    def prev8(n):
        return pl.BlockSpec((8, n), lambda i: (jnp.maximum(i * hb - 1, 0), 0))

    def next8(n):
        return pl.BlockSpec((8, n), lambda i: (jnp.minimum((i + 1) * hb, last_blk), 0))

    return _hosted_call(
        body, name="mix_mid_bwd", grid=(n_i,),
        in_specs=[rows_of(Z_COLS), prev8(Z_COLS), next8(Z_COLS), rows_of(CONV_W), next8(CONV_W),
                  whole((8, CONV_W)), whole((1, Q_LORA)), whole((1, KV_LORA)),
                  whole((HEADS * HEAD_SLOT, Q_LORA)), whole((KV_LORA, 2 * HEADS * 128)),
                  rows_of(128), rows_of(128),
                  rows_of(HEADS * HEAD_SLOT), rows_of(HEADS * HEAD_SLOT), rows_of(HEADS * V_HEAD)],
        out_specs=[rows_of(Z_COLS), rows_of(HEADS * HEAD_SLOT), rows_of(2 * HEADS * 128), whole((8, MID_SUMS))],
        out_shape=[jax.ShapeDtypeStruct((S, Z_COLS), BF16), jax.ShapeDtypeStruct((S, HEADS * HEAD_SLOT), BF16),
                   jax.ShapeDtypeStruct((S, 2 * HEADS * 128), BF16), jax.ShapeDtypeStruct((8, MID_SUMS), F32)],
        scratch_shapes=[], semantics=("arbitrary",),
        args=(z, z, z, dya, dya, conv_w, gq, gkv, wuq, wukv, cs, sn, dq, dk, dv), comm=comm)


ADA_Q = N_MOD * D // N_CHIPS
ADA_TN = 768


def _ada_forward(c_all, ada_w_q, ada_b_q):
    def body(c_ref, w_ref, b_ref, o_ref):
        cv = c_ref[...]
        sc = (cv * jax.nn.sigmoid(cv)).astype(BF16)
        o_ref[...] = jnp.dot(sc, w_ref[...].astype(BF16), preferred_element_type=F32) + b_ref[...]

    return pl.pallas_call(
        body, name="ada_forward", grid=(ADA_Q // ADA_TN,),
        in_specs=[pl.BlockSpec((16, D), lambda j: (0, 0)), pl.BlockSpec((D, ADA_TN), lambda j: (0, j)),
                  pl.BlockSpec((1, ADA_TN), lambda j: (0, j))],
        out_specs=pl.BlockSpec((16, ADA_TN), lambda j: (0, j)),
        out_shape=jax.ShapeDtypeStruct((16, ADA_Q), F32),
        compiler_params=_params(("parallel",)),
    )(c_all, ada_w_q, ada_b_q)


def _ada_wgrad(c_all, dmod_q):
    def body(c_ref, d_ref, o_ref):
        cv = c_ref[...]
        sc = (cv * jax.nn.sigmoid(cv)).astype(BF16)
        o_ref[...] = lax.dot_general(sc, d_ref[...].astype(BF16), (((0,), (0,)), ((), ())),
                                     preferred_element_type=F32)

    return pl.pallas_call(
        body, name="ada_wgrad", grid=(ADA_Q // ADA_TN,),
        in_specs=[pl.BlockSpec((16, D), lambda j: (0, 0)), pl.BlockSpec((16, ADA_TN), lambda j: (0, j))],
        out_specs=pl.BlockSpec((D, ADA_TN), lambda j: (0, j)),
        out_shape=jax.ShapeDtypeStruct((D, ADA_Q), F32),
        compiler_params=_params(("parallel",)),
    )(c_all, dmod_q)


def _sum_devices(parts):
    n = parts.shape[1]

    def body(p_ref, o_ref):
        o_ref[...] = jnp.broadcast_to(jnp.sum(p_ref[...], axis=0, keepdims=True), o_ref.shape)

    return pl.pallas_call(
        body, name="sum_devices",
        in_specs=[pl.BlockSpec((N_DEV, n), lambda: (0, 0))], out_specs=pl.BlockSpec((N_DEV, n), lambda: (0, 0)),
        out_shape=jax.ShapeDtypeStruct((N_DEV, n), F32),
    )(parts)


def _adamw(ws, gs, ms, vs, *, name, comm=None):
    n = len(ws)
    _, rows, cols = ws[0].shape
    tr = _row_tile(rows, 256)

    def body(*refs):
        ins, outs = refs[:4 * n], refs[4 * n:]
        for k in range(n):
            w_ref, g_ref, m_ref, v_ref = ins[k], ins[n + k], ins[2 * n + k], ins[3 * n + k]
            go_ref, d_ref, mo_ref, vo_ref = outs[4 * k:4 * k + 4]
            gv = g_ref[...]
            mn = B1 * m_ref[0] + (1.0 - B1) * gv
            vn = B2 * v_ref[0] + (1.0 - B2) * (gv * gv)
            m_hat = mn / (1.0 - B1 ** STEP)
            v_hat = vn / (1.0 - B2 ** STEP)
            go_ref[0] = gv
            d_ref[0] = -LR * (m_hat / (jnp.sqrt(v_hat) + AEPS) + WD * w_ref[0])
            mo_ref[0] = mn
            vo_ref[0] = vn

    blk = pl.BlockSpec((1, tr, cols), lambda i: (0, i, 0))
    gblk = pl.BlockSpec((tr, cols), lambda i: (i, 0))
    res, couts = _hosted_call(
        body, name=name, grid=(rows // tr,),
        in_specs=[blk] * n + [gblk] * n + [blk] * 2 * n, out_specs=[blk] * 4 * n,
        out_shape=[jax.ShapeDtypeStruct((1, rows, cols), F32)] * 4 * n,
        scratch_shapes=[], semantics=("parallel",), args=(*ws, *gs, *ms, *vs), comm=comm)
    return [res[4 * k:4 * k + 4] for k in range(n)], couts


def _small_allgather(v, *, name):
    m, n = v.shape

    def body(x_ref, out_ref, send_sems, recv_sems, local_sem):
        x, y, c = _place()
        me, sibling = (x, y, c), (x, y, 1 - c)
        chips = [(1 - x, y), (x, 1 - y), (1 - x, 1 - y)]

        def rows(px, py, pc):
            return out_ref.at[pl.ds((4 * px + 2 * py + pc) * m, m), :]

        def copy(k, block, to, src=None):
            return pltpu.make_async_remote_copy(
                src_ref=rows(*block) if src is None else src, dst_ref=rows(*block),
                send_sem=send_sems.at[k], recv_sem=recv_sems.at[k], device_id=to, device_id_type=MESH)

        mine = pltpu.make_async_copy(x_ref, rows(*me), local_sem)
        mine.start()
        first = [copy(0, me, sibling, src=x_ref)]
        first += [copy(1 + j, me, (*chip, c), src=x_ref) for j, chip in enumerate(chips)]
        for cp in first:
            cp.start()
        passed = [copy(4 + j, (*chip, c), sibling) for j, chip in enumerate(chips)]
        for j, chip in enumerate(chips):
            copy(1 + j, (*chip, c), me).wait_recv()
            passed[j].start()
        copy(0, sibling, me).wait_recv()
        for j, chip in enumerate(chips):
            copy(4 + j, (*chip, 1 - c), me).wait_recv()
        for cp in first + passed:
            cp.wait_send()
        mine.wait()

    return pl.pallas_call(
        body, name=name,
        out_shape=jax.ShapeDtypeStruct((N_DEV * m, n), v.dtype),
        in_specs=[pl.BlockSpec(memory_space=pltpu.VMEM)], out_specs=pl.BlockSpec(memory_space=pltpu.VMEM),
        scratch_shapes=[pltpu.SemaphoreType.DMA((7,)), pltpu.SemaphoreType.DMA((7,)), pltpu.SemaphoreType.DMA],
    )(v)


ADD_BLOCKS = 2


def _pair_add(place, gs, ts, *, name):
    n_a = len(gs)

    def body(pl_ref, *refs):
        g_refs, t_refs = refs[:n_a], refs[n_a:2 * n_a]
        pf_refs, pb_refs = refs[2 * n_a:3 * n_a], refs[3 * n_a:]
        own = pl.program_id(1) == pl_ref[1]
        for g_ref, t_ref, pf_ref, pb_ref in zip(g_refs, t_refs, pf_refs, pb_refs):
            s = g_ref[...] + t_ref[...]
            pb_ref[...] = s.astype(BF16)

            @pl.when(own)
            def _():
                pf_ref[...] = s[0]

    def blk(t, own_half):
        tr = t.shape[1] // ADD_BLOCKS
        if own_half:
            return pl.BlockSpec((1, tr, t.shape[2]), lambda r, q, p: (q, p[0] * ADD_BLOCKS + r, 0))
        return pl.BlockSpec((1, tr, t.shape[2]), lambda r, q, p: (q, r, 0))

    def own_blk(t):
        return pl.BlockSpec((t.shape[1] // ADD_BLOCKS, t.shape[2]), lambda r, q, p: (r, 0))

    grid_spec = pltpu.PrefetchScalarGridSpec(
        num_scalar_prefetch=1, grid=(ADD_BLOCKS, N_CHIPS),
        in_specs=[blk(t, True) for t in ts] + [blk(t, False) for t in ts],
        out_specs=[own_blk(t) for t in ts] + [blk(t, False) for t in ts])
    res = pl.pallas_call(
        body, name=name, grid_spec=grid_spec,
        out_shape=([jax.ShapeDtypeStruct(t.shape[1:], F32) for t in ts]
                   + [jax.ShapeDtypeStruct(t.shape, BF16) for t in ts]),
        compiler_params=_params(("parallel", "arbitrary")),
    )(place, *gs, *ts)
    return list(res[:n_a]), list(res[n_a:])


def _chip_add(place, pfs, ts, *, name):
    n_a = len(pfs)

    def body(pl_ref, *refs):
        pf_refs, t_refs, o_refs = refs[:n_a], refs[n_a:4 * n_a], refs[4 * n_a:]
        for i, (pf_ref, o_ref) in enumerate(zip(pf_refs, o_refs)):
            t1, t2, t3 = t_refs[3 * i:3 * i + 3]
            o_ref[...] = ((pf_ref[...] + t1[0].astype(F32)) + t2[0].astype(F32)) + t3[0].astype(F32)

    def slot(t, j):
        return pl.BlockSpec((1, t.shape[1] // ADD_BLOCKS, t.shape[2]), lambda r, p: (p[1] ^ j, r, 0))

    def half(t):
        return pl.BlockSpec((t.shape[1] // ADD_BLOCKS, t.shape[2]), lambda r, p: (r, 0))

    grid_spec = pltpu.PrefetchScalarGridSpec(
        num_scalar_prefetch=1, grid=(ADD_BLOCKS,),
        in_specs=[half(t) for t in ts] + [slot(t, j) for t in ts for j in (1, 2, 3)],
        out_specs=[half(t) for t in ts])
    res = pl.pallas_call(
        body, name=name, grid_spec=grid_spec,
        out_shape=[jax.ShapeDtypeStruct(t.shape[1:], F32) for t in ts],
        compiler_params=_params(("parallel",)),
    )(place, *pfs, *[t for t in ts for _ in range(3)])
    return list(res)


BULK = [("ffn1_w1", "colsT"), ("ffn1_w3", "colsT"), ("ffn1_w2", "rows"), ("w_in", "cols"), ("w_uq", "colsT"),
        ("w_ukv", "cols"), ("w_out", "rows"), ("ffn2_w1", "colsT"), ("ffn2_w3", "colsT"), ("ffn2_w2", "rows")]
KIND = dict(BULK)


def _group(*names):
    return [b for b in BULK if b[0] in names]


W_FIRST = _group("ffn1_w1", "ffn1_w3")
W_REST = [b for b in BULK if b not in W_FIRST]
W_MIX = _group("ffn1_w2", "w_in", "w_uq", "w_ukv", "w_out")
W_FFN2 = _group("ffn2_w1", "ffn2_w3", "ffn2_w2")
G_FFN2 = _group("ffn2_w1", "ffn2_w3", "ffn2_w2")
G_MIX = _group("w_in", "w_uq", "w_ukv", "w_out")
G_FFN1 = _group("ffn1_w1", "ffn1_w3", "ffn1_w2")


def _gathered_weights(specs, shards, got, myq):
    out = {}
    for (name, kind), part in zip(specs, got):
        part = lax.dynamic_update_slice_in_dim(part, shards[name][None], myq, axis=0)
        out[name] = _full_weight(part, kind)
    return out


def _working_shard(w, kind):
    return jnp.swapaxes(w, 1, 2)[0] if kind == "colsT" else w[0]


def _full_weight(parts, kind):
    if kind == "cols":
        return jnp.transpose(parts, (1, 0, 2)).reshape(parts.shape[1], -1)
    return parts.reshape(-1, parts.shape[2])


def _quarters(g, kind):
    if kind == "cols":
        k, n = g.shape
        return jnp.transpose(g.reshape(k, N_CHIPS, n // N_CHIPS), (1, 0, 2))
    return g.reshape(N_CHIPS, g.shape[0] // N_CHIPS, g.shape[1])


def _pad_heads(w_uq_t):
    w = w_uq_t.reshape(HEADS, QK_NOPE + QK_ROPE, Q_LORA)
    return jnp.pad(w, ((0, 0), (0, HEAD_SLOT - QK_NOPE - QK_ROPE), (0, 0))).reshape(HEADS * HEAD_SLOT, Q_LORA)


def _unpad_heads(g):
    return g.reshape(HEADS, HEAD_SLOT, Q_LORA)[:, :QK_NOPE + QK_ROPE].reshape(HEADS * (QK_NOPE + QK_ROPE), Q_LORA)


def _split_kv(w_ukv):
    return jnp.transpose(w_ukv.reshape(KV_LORA, HEADS, 2, 128), (0, 2, 1, 3)).reshape(KV_LORA, 2 * HEADS * 128)


def _merge_kv(g):
    return jnp.transpose(g.reshape(KV_LORA, 2, HEADS, 128), (0, 2, 1, 3)).reshape(KV_LORA, 2 * HEADS * 128)


def _rope_tables(positions):
    inv_freq = ROPE_THETA ** (-jnp.arange(0, QK_ROPE, 2, dtype=F32) / QK_ROPE)
    ang = positions.astype(F32)[:, None] * inv_freq
    cos, sin, zero = jnp.cos(ang), jnp.sin(ang), jnp.zeros((positions.shape[0], 64), F32)
    return jnp.concatenate([cos, cos, zero], axis=1), jnp.concatenate([sin, sin, zero], axis=1)


def _assemble(place, specs, rhs, others):
    south = place[0] == 0
    return {b[0]: jnp.concatenate([jnp.where(south, rh, ot), jnp.where(south, ot, rh)], axis=0)
            for b, rh, ot in zip(specs, rhs, others)}


def _local_step(x, positions, target, mod, vec, conv_w, w_first, rest_shards, place, tail_host=None):
    row = lambda k: mod[k:k + 1]
    sh1, sc1, g1, sh2, sc2, g2, sh3, sc3, g3 = [row(k) for k in range(N_MOD)]
    cs, sn = _rope_tables(positions)
    cw8 = jnp.pad(conv_w, ((0, 5), (0, 0)))
    ga, gb = _group_mats()
    dist = place is not None
    comm = lambda prog: prog if dist else None

    w = dict(w_first) if dist else {**rest_shards, **w_first}

    def gather(specs):
        return _Gather([rest_shards[b[0]] for b in specs]) if dist else None

    def arrived(specs, got):
        if dist:
            w.update(_gathered_weights(specs, rest_shards, got, place[1]))

    (h1, a1, b1, u1), got = _ffn_up(x, vec["norm_ffn1_g"], sh1, sc1, w["ffn1_w1"], w["ffn1_w3"],
                                    name="ffn1_up", comm=gather(W_MIX))
    arrived(W_MIX, got)
    w_in = jnp.pad(w["w_in"].T, ((0, Z_COLS - IN_COLS), (0, 0)))
    wuq = _pad_heads(w["w_uq"])
    wukv = _split_kv(w["w_ukv"])
    (x1, f1), _ = _ffn_down(u1, w["ffn1_w2"], x, g1, name="ffn1_down")
    (h2, z), _ = _mix_in(x1, vec["norm_mix_g"], sh2, sc2, w_in)
    (ya, q, k, v, cqn, ckvn), _ = _mix_mid(z, cw8, vec["q_norm_g"], vec["kv_norm_g"], wuq, wukv, cs, sn)
    (o, lse), got = _attention(q, k, v, comm=gather(W_FFN2))
    arrived(W_FFN2, got)
    x2, yn, yo = _mix_out(ya, o, vec["out_norm_g"], w["w_out"], x1, g2, ga, gb)
    (h3, a3, b3, u3), _ = _ffn_up(x2, vec["norm_ffn2_g"], sh3, sc3, w["ffn2_w1"], w["ffn2_w3"], name="ffn2_up")
    dx3, f3, dgfin, loss_blk = _ffn_down_loss(u3, w["ffn2_w2"], x2, g3, vec["final_norm_g"], target)

    grads, reduced = {}, {}

    def tn(a, b, tm, tn_, name, prog=None):
        if prog is None:
            return _tn_matmul(a, b, tm=tm, tn=tn_, name=name), None
        return _tn_matmul(a, b, tm=tm, tn=tn_, name=name, comm=prog)

    def slab_of(specs):
        return [_quarters(grads[n], kind) for n, kind in specs]

    (df3, da3, db3, dg3), _ = _ffn_bwd_du(dx3, g3, f3, w["ffn2_w2"], a3, b3, name="ffn2_bwd_du")
    grads["ffn2_w2"], _ = tn(u3, df3, FF // 2, D, "ffn2_dw2")
    grads["ffn2_w1"], _ = tn(da3, h3, FF // 2, D, "ffn2_dw1")
    grads["ffn2_w3"], _ = tn(db3, h3, FF // 2, D, "ffn2_dw3")
    (dx2, s3), _ = _dh_normbwd([(da3, w["ffn2_w1"]), (db3, w["ffn2_w3"])], x2, vec["norm_ffn2_g"], sc3, dx3,
                               name="ffn2_bwd_dh")

    p1 = slab_of(G_FFN2) if dist else None
    (dyo, dya, do, delta, s_out), t1 = _mix_out_bwd(dx2, g2, yo, w["w_out"], ya, o, vec["out_norm_g"], ga, gb,
                                                    comm=comm(_PairExchange(p1) if dist else None))
    grads["w_out"], _ = tn(yn, dyo, D, D, "dw_out")
    if dist:
        pf1, pb1 = _pair_add(place, p1, t1, name="ffn2g_pair_add")
    dq, dk, dv = _attention_bwd(q, k, v, do, lse, delta)
    (dz, dqf, dkvf, s_mid), t2 = _mix_mid_bwd(z, dya, cw8, vec["q_norm_g"], vec["kv_norm_g"], wuq, wukv, cs, sn,
                                              dq, dk, dv, comm=comm(_ChipExchange(pb1) if dist else None))
    g_uq, _ = tn(dqf, cqn, HEADS * HEAD_SLOT, Q_LORA, "dw_uq")
    g_ukv, _ = tn(ckvn, dkvf, KV_LORA, 2 * HEADS * 128, "dw_ukv")
    grads["w_uq"], grads["w_ukv"] = _unpad_heads(g_uq), _merge_kv(g_ukv)
    g_in, _ = tn(h2, dz, D, Z_COLS, "dw_in")
    grads["w_in"] = g_in[:, :IN_COLS]
    (dx1, s2), _ = _dh_normbwd([(dz, w_in)], x1, vec["norm_mix_g"], sc2, dx2, name="mix_bwd_dh")

    p2 = slab_of(G_MIX) if dist else None
    rh_ffn2 = _chip_add(place, pf1, t2, name="ffn2g_chip_add") if dist else None
    (df1, da1, db1, dg1), got = _ffn_bwd_du(dx1, g1, f1, w["ffn1_w2"], a1, b1, name="ffn1_bwd_du",
                                            comm=comm(_Multi([_PairExchange(p2), _PairShare(rh_ffn2)]) if dist else None))
    if dist:
        t1 = got[:len(p2)]
        reduced.update(_assemble(place, G_FFN2, rh_ffn2, got[len(p2):]))
    dh_pairs = [(da1, w["ffn1_w1"]), (db1, w["ffn1_w3"])]
    if dist:
        pf2, pb2 = _pair_add(place, p2, t1, name="mixg_pair_add")
        grads["ffn1_w2"], t2 = tn(u1, df1, FF // 2, D, "ffn1_dw2", _ChipExchange(pb2))
        rh_mix = _chip_add(place, pf2, t2, name="mixg_chip_add")
        q_w2 = [_quarters(grads["ffn1_w2"], KIND["ffn1_w2"])]
        grads["ffn1_w1"], got = tn(da1, h1, FF // 2, D, "ffn1_dw1", _Multi([_PairShare(rh_mix), _PairExchange(q_w2)]))
        reduced.update(_assemble(place, G_MIX, rh_mix, got[:len(rh_mix)]))
        pf_w2, pb_w2 = _pair_add(place, q_w2, got[len(rh_mix):], name="ffn1w2_pair_add")
        q_w1 = [_quarters(grads["ffn1_w1"], KIND["ffn1_w1"])]
        grads["ffn1_w3"], got = tn(db1, h1, FF // 2, D, "ffn1_dw3", _Multi([_ChipExchange(pb_w2), _PairExchange(q_w1)]))
        t2_w2 = got[:1]
        pf_w1, pb_w1 = _pair_add(place, q_w1, got[1:], name="ffn1w1_pair_add")
        q_w3 = [_quarters(grads["ffn1_w3"], KIND["ffn1_w3"])]
        (dx0, s1), got = _dh_normbwd(dh_pairs, x, vec["norm_ffn1_g"], sc1, dx1, name="ffn1_bwd_dh",
                                     comm=_Multi([_ChipExchange(pb_w1), _PairExchange(q_w3)]))
        t2_w1 = got[:1]
        pf_w3, pb_w3 = _pair_add(place, q_w3, got[1:], name="ffn1w3_pair_add")
        t2_w3 = tail_host(_ChipExchange(pb_w3), reduced)
        rh = _chip_add(place, pf_w1 + pf_w3 + pf_w2, t2_w1 + t2_w3 + t2_w2, name="ffn1g_chip_add")
        reduced.update(_assemble(place, G_FFN1, rh, _run_comm(_PairShare(rh), name="ffn1g_pair_share")))
    else:
        grads["ffn1_w2"], _ = tn(u1, df1, FF // 2, D, "ffn1_dw2")
        grads["ffn1_w1"], _ = tn(da1, h1, FF // 2, D, "ffn1_dw1")
        grads["ffn1_w3"], _ = tn(db1, h1, FF // 2, D, "ffn1_dw3")
        (dx0, s1), _ = _dh_normbwd(dh_pairs, x, vec["norm_ffn1_g"], sc1, dx1, name="ffn1_bwd_dh")
        reduced = grads

    def part(s, k):
        return s[0:1, k * D:(k + 1) * D]

    dmod = jnp.concatenate([part(s1, 1), part(s1, 0), dg1[0:1], part(s2, 1), part(s2, 0), part(s_out, 0),
                            part(s3, 1), part(s3, 0), dg3[0:1]], axis=1)
    small = {"norm_ffn1_g": part(s1, 2), "norm_mix_g": part(s2, 2), "out_norm_g": part(s_out, 1),
             "norm_ffn2_g": part(s3, 2), "final_norm_g": dgfin[0:1],
             "q_norm_g": s_mid[0:1, 3 * CONV_W:3 * CONV_W + Q_LORA],
             "kv_norm_g": s_mid[0:1, 3 * CONV_W + Q_LORA:MID_SUMS], "conv_w": s_mid[0:1, 0:3 * CONV_W]}
    return loss_blk, dx0, reduced, dmod, small


SMALL = [("norm_ffn1_g", D), ("norm_mix_g", D), ("out_norm_g", D), ("norm_ffn2_g", D), ("final_norm_g", D),
         ("q_norm_g", Q_LORA), ("kv_norm_g", KV_LORA), ("conv_w", 3 * CONV_W)]
WEIGHTS = ['ada_w', 'ada_b', 'norm_ffn1_g', 'ffn1_w1', 'ffn1_w3', 'ffn1_w2', 'norm_mix_g', 'w_in', 'conv_w',
           'q_norm_g', 'w_uq', 'kv_norm_g', 'w_ukv', 'out_norm_g', 'w_out', 'norm_ffn2_g', 'ffn2_w1', 'ffn2_w3',
           'ffn2_w2', 'final_norm_g']


def kernel(x, c, positions, ada_w, ada_b, norm_ffn1_g, ffn1_w1, ffn1_w3, ffn1_w2, norm_mix_g, w_in, conv_w, q_norm_g, w_uq, kv_norm_g, w_ukv, out_norm_g, w_out, norm_ffn2_g, ffn2_w1, ffn2_w3, ffn2_w2, final_norm_g, loss_target, m_ada_w, m_ada_b, m_norm_ffn1_g, m_ffn1_w1, m_ffn1_w3, m_ffn1_w2, m_norm_mix_g, m_w_in, m_conv_w, m_q_norm_g, m_w_uq, m_kv_norm_g, m_w_ukv, m_out_norm_g, m_w_out, m_norm_ffn2_g, m_ffn2_w1, m_ffn2_w3, m_ffn2_w2, m_final_norm_g, v_ada_w, v_ada_b, v_norm_ffn1_g, v_ffn1_w1, v_ffn1_w3, v_ffn1_w2, v_norm_mix_g, v_w_in, v_conv_w, v_q_norm_g, v_w_uq, v_kv_norm_g, v_w_ukv, v_out_norm_g, v_w_out, v_norm_ffn2_g, v_ffn2_w1, v_ffn2_w3, v_ffn2_w2, v_final_norm_g):
    args = dict(locals())
    wts = {n: args[n] for n in WEIGHTS}
    mom = {n: args["m_" + n] for n in WEIGHTS}
    var = {n: args["v_" + n] for n in WEIGHTS}
    ax, ay, ac = _place()
    myq = 2 * ax + ay
    me = 2 * myq + ac
    place = jnp.stack([ac, myq]).astype(jnp.int32)

    shards = {name: _working_shard(wts[name], kind).astype(BF16) for name, kind in BULK}
    first = _run_comm(_Gather([shards[b[0]] for b in W_FIRST]), name="gather_ffn1")
    w_first = _gathered_weights(W_FIRST, shards, first, myq)

    mine = jnp.concatenate([c, conv_w[0].reshape(1, 3 * CONV_W // N_CHIPS)], axis=1)
    seen = _small_allgather(jnp.pad(mine, ((0, 7), (0, 0))), name="gather_cond").reshape(N_DEV, 8, -1)[:, 0]
    c_all = jnp.pad(seen[:, :D], ((0, 8), (0, 0)))
    conv_full = jnp.transpose(seen[0::2, D:].reshape(N_CHIPS, 3, CONV_W // N_CHIPS), (1, 0, 2)).reshape(3, CONV_W)
    ada_b_q = lax.dynamic_slice_in_dim(ada_b, myq * ADA_Q, ADA_Q, axis=1)
    mod_q = _ada_forward(c_all, ada_w[0], ada_b_q)
    mod_all = _small_allgather(mod_q, name="gather_mod").reshape(N_DEV, 16, ADA_Q)
    mod_rows = jnp.transpose(mod_all[0::2, :N_DEV], (1, 0, 2)).reshape(N_DEV, N_MOD * D)
    mod = lax.dynamic_slice_in_dim(mod_rows, me, 1, axis=0).reshape(N_MOD, D)

    vec = {n: wts[n] for n in ("norm_ffn1_g", "norm_mix_g", "q_norm_g", "kv_norm_g", "out_norm_g", "norm_ffn2_g")}
    vec["final_norm_g"] = final_norm_g.reshape(1, D)
    g, delta, new_m, new_v = {}, {}, {}, {}

    def adam(names, grads, comm=None):
        views = [(lambda a: jnp.swapaxes(a, 1, 2)) if KIND.get(n) == "colsT" else (lambda a: a) for n in names]
        res, couts = _adamw([vw(wts[n]) for n, vw in zip(names, views)], [grads[n] for n in names],
                            [vw(mom[n]) for n, vw in zip(names, views)], [vw(var[n]) for n, vw in zip(names, views)],
                            name="adamw_" + names[0], comm=comm)
        for n, vw, r in zip(names, views, res):
            g[n], delta[n], new_m[n], new_v[n] = [vw(a) for a in r]
        return couts

    loss_blk, grad_x, gq, dmod, small = _local_step(
        x[0], positions[0], loss_target[0], mod, vec, conv_full, w_first, {b[0]: shards[b[0]] for b in W_REST}, place,
        tail_host=lambda prog, grads: adam([b[0] for b in G_FFN2], grads, prog))
    loss = lax.psum(loss_blk[0, 0], ("x", "y", "c"))

    rows = jnp.concatenate([dmod] + [small[n] for n, _ in SMALL], axis=1)
    width = rows.shape[1]
    fold = -(-width // (8 * 128)) * 128
    rows = jnp.pad(rows, ((0, 0), (0, 8 * fold - width))).reshape(8, fold)
    every = _small_allgather(rows, name="gather_small").reshape(N_DEV, 8 * fold)[:, :width]
    total = _sum_devices(every)[0:1]
    dmod_q = lax.dynamic_slice_in_dim(every[:, :N_MOD * D], myq * ADA_Q, ADA_Q, axis=1)
    sg = {name: gq[name] for name, *_ in BULK}
    sg["ada_w"] = _ada_wgrad(c_all, jnp.pad(dmod_q, ((0, 8), (0, 0))))
    sg["ada_b"] = total[:, :N_MOD * D]
    off = N_MOD * D
    for n, width in SMALL:
        sg[n] = total[:, off:off + width]
        off += width
    sg["conv_w"] = lax.dynamic_slice_in_dim(sg["conv_w"].reshape(3, CONV_W), myq * (CONV_W // N_CHIPS),
                                            CONV_W // N_CHIPS, axis=1)

    for name in ["ada_w"] + [b[0] for b in BULK if b not in G_FFN2]:
        adam([name], sg)
    smalls = ["ada_b"] + [n for n, _ in SMALL]

    def packed(d):
        flat = jnp.concatenate([d[n].reshape(1, -1) for n in smalls], axis=1)
        return jnp.pad(flat.reshape(-1, D), ((0, 1), (0, 0)))

    res = _adamw([packed(wts)[None]], [packed(sg)], [packed(mom)[None]], [packed(var)[None]],
                 name="adamw_small")[0][0][1:]
    off = 0
    for n in smalls:
        size = wts[n].size
        for d, r in zip((delta, new_m, new_v), res):
            d[n] = r.reshape(-1)[off:off + size].reshape(wts[n].shape)
        g[n] = sg[n].reshape(wts[n].shape)
        off += size

    return (loss, grad_x[None], *[g[n] for n in WEIGHTS], *[delta[n] for n in WEIGHTS],
            *[new_m[n] for n in WEIGHTS], *[new_v[n] for n in WEIGHTS])
```

```python
import numpy as np
import jax
import jax.numpy as jnp
from jax import lax
from jax.experimental import pallas as pl
from jax.experimental.pallas import tpu as pltpu

F32 = jnp.float32
BF16 = jnp.bfloat16
MESH = pl.DeviceIdType.MESH

D = 1024
FF = 2816
CONV_W = 512
CONV_GROUP = 64
HEADS = 4
QK_NOPE = 128
QK_ROPE = 64
V_HEAD = 128
Q_LORA = 384
KV_LORA = 256
HEAD_SLOT = 256
IN_COLS = 3 * CONV_W + Q_LORA + KV_LORA + QK_ROPE
Z_COLS = 2304
EPS = 1e-6
ROPE_THETA = 10000.0
CHUNK = 64
ATT_SCALE = (QK_NOPE + QK_ROPE) ** -0.5
NEG = -1e30
EXP2_SCALE = ATT_SCALE * 1.4426950408889634
N_MOD = 9

LR, B1, B2, AEPS, WD, STEP = 0.001, 0.9, 0.999, 1e-08, 0.01, 10

N_CHIPS = 4
N_DEV = 8
VMEM_LIMIT = 56 << 20


def _params(sem, vmem=VMEM_LIMIT):
    return pltpu.CompilerParams(dimension_semantics=sem, vmem_limit_bytes=vmem)


def _rms(v):
    return lax.rsqrt(jnp.mean(v * v, axis=-1, keepdims=True) + EPS)


def _rsum8(v):
    t, n = v.shape
    return jnp.sum(v.reshape(t // 8, 8, n), axis=0)


def _all_rows(ref):
    ref[...] = jnp.broadcast_to(jnp.sum(ref[...], axis=0, keepdims=True), ref.shape)


def _gsum(v, gmat, split=False):
    hi = v.astype(BF16)
    out = jnp.dot(hi, gmat, preferred_element_type=F32)
    if split:
        out = out + jnp.dot((v - hi.astype(F32)).astype(BF16), gmat, preferred_element_type=F32)
    return out


def _dot_nt(a, b):
    return lax.dot_general(a, b, (((1,), (1,)), ((), ())), preferred_element_type=F32)


def _silu_parts(a):
    sg = jax.nn.sigmoid(a)
    return sg, a * sg


def _rope(xr, cs, sn, lane):
    rh = jnp.where(lane < 32, -pltpu.roll(xr, 96, 1), pltpu.roll(xr, 32, 1))
    return xr * cs + rh * sn


def _rope_t(g, cs, sn, lane):
    y = g * sn
    rt = jnp.where(lane < 32, pltpu.roll(y, 96, 1), jnp.where(lane < 64, -pltpu.roll(y, 32, 1), 0.0))
    return g * cs + rt


def _row_tile(rows, pref, mult=8):
    t = min(rows, pref) // mult * mult
    while rows % t:
        t -= mult
    return t


def _place():
    return lax.axis_index("x"), lax.axis_index("y"), lax.axis_index("c")


ANY = pl.BlockSpec(memory_space=pl.ANY)


def _hosted_call(body, *, name, grid, in_specs, out_specs, out_shape, scratch_shapes, semantics, args, comm=None,
                 prefetch=()):
    n_in, n_out, n_scr, n_pf = len(in_specs), len(out_specs), len(scratch_shapes), len(prefetch)

    def call(fn, in_specs, out_specs, out_shape, scratch_shapes, semantics, operands):
        spec = pltpu.PrefetchScalarGridSpec(num_scalar_prefetch=n_pf, grid=grid, in_specs=list(in_specs),
                                            out_specs=list(out_specs), scratch_shapes=list(scratch_shapes))
        return pl.pallas_call(fn, name=name, grid_spec=spec, out_shape=list(out_shape),
                              compiler_params=_params(semantics))(*prefetch, *operands)

    if comm is None:
        return list(call(body, in_specs, out_specs, out_shape, scratch_shapes, semantics, args)), []
    n_ci, n_co = len(comm.inputs), len(comm.out_shapes)
    total = int(np.prod(grid))

    def hosted(*refs):
        tables, refs = refs[:n_pf], refs[n_pf:]
        ins, refs = refs[:n_in], refs[n_in:]
        cins, refs = refs[:n_ci], refs[n_ci:]
        outs, refs = refs[:n_out], refs[n_out:]
        couts, refs = refs[:n_co], refs[n_co:]
        scratch, sems = refs[:n_scr], refs[n_scr]
        step = pl.program_id(0)
        for ax in range(1, len(grid)):
            step = step * grid[ax] + pl.program_id(ax)

        @pl.when(step == 0)
        def _():
            comm.start(cins, couts, sems)

        body(*tables, *ins, *outs, *scratch)

        @pl.when(step == total - 1)
        def _():
            comm.finish(cins, couts, sems)

    res = call(hosted, list(in_specs) + [ANY] * n_ci, list(out_specs) + [ANY] * n_co,
               list(out_shape) + list(comm.out_shapes),
               list(scratch_shapes) + [pltpu.SemaphoreType.DMA((comm.n_sems,))],
               ("arbitrary",) * len(grid), (*args, *comm.inputs))
    return list(res[:n_out]), list(res[n_out:])


def _run_comm(comm, *, name):
    n_ci = len(comm.inputs)

    def body(*refs):
        cins, couts, sems = refs[:n_ci], refs[n_ci:-1], refs[-1]
        comm.start(cins, couts, sems)
        comm.finish(cins, couts, sems)

    return list(pl.pallas_call(
        body, name=name, out_shape=list(comm.out_shapes), in_specs=[ANY] * n_ci,
        out_specs=[ANY] * len(comm.out_shapes), scratch_shapes=[pltpu.SemaphoreType.DMA((comm.n_sems,))],
    )(*comm.inputs))


class _Gather:
    def __init__(self, slabs):
        self.inputs = list(slabs)
        self.out_shapes = [jax.ShapeDtypeStruct((N_CHIPS,) + s.shape, s.dtype) for s in slabs]
        self.n_sems = 12 * len(slabs)

    @staticmethod
    def _copy(out, sems, base, k, chip, hc, to, src=None):
        H = out.shape[1] // 2
        half = out.at[2 * chip[0] + chip[1], pl.ds(hc * H, H), :]
        return pltpu.make_async_remote_copy(
            src_ref=half if src is None else src, dst_ref=half, send_sem=sems.at[base + k],
            recv_sem=sems.at[base + 6 + k], device_id=to, device_id_type=MESH)

    def _firsts(self, src, out, sems, base):
        x, y, c = _place()
        H = src.shape[0] // 2
        chips = [(1 - x, y), (x, 1 - y), (1 - x, 1 - y)]
        return [self._copy(out, sems, base, j, (x, y), c, (*chip, c), src=src.at[pl.ds(c * H, H), :])
                for j, chip in enumerate(chips)]

    def start(self, ins, outs, sems):
        for i, (src, out) in enumerate(zip(ins, outs)):
            for cp in self._firsts(src, out, sems, 12 * i):
                cp.start()

    def finish(self, ins, outs, sems):
        x, y, c = _place()
        chips = [(1 - x, y), (x, 1 - y), (1 - x, 1 - y)]
        passed = []
        for i, out in enumerate(outs):
            for j, chip in enumerate(chips):
                self._copy(out, sems, 12 * i, j, chip, c, (x, y, c)).wait_recv()
                cp = self._copy(out, sems, 12 * i, 3 + j, chip, c, (x, y, 1 - c))
                cp.start()
                passed.append(cp)
        for i, out in enumerate(outs):
            for j, chip in enumerate(chips):
                self._copy(out, sems, 12 * i, 3 + j, chip, 1 - c, (x, y, c)).wait_recv()
        for cp in passed:
            cp.wait_send()
        for i, (src, out) in enumerate(zip(ins, outs)):
            for cp in self._firsts(src, out, sems, 12 * i):
                cp.wait_send()


class _PairExchange:
    def __init__(self, arrays):
        self.inputs = list(arrays)
        self.out_shapes = [jax.ShapeDtypeStruct((N_CHIPS, a.shape[1] // 2, a.shape[2]), a.dtype) for a in arrays]
        self.n_sems = 2 * len(arrays)

    def _copies(self, ins, outs, sems):
        x, y, c = _place()
        return [pltpu.make_async_remote_copy(
            src_ref=g.at[:, pl.ds((1 - c) * t.shape[1], t.shape[1]), :], dst_ref=t, send_sem=sems.at[2 * i],
            recv_sem=sems.at[2 * i + 1], device_id=(x, y, 1 - c), device_id_type=MESH)
            for i, (g, t) in enumerate(zip(ins, outs))]

    def start(self, ins, outs, sems):
        for cp in self._copies(ins, outs, sems):
            cp.start()

    def finish(self, ins, outs, sems):
        for cp in self._copies(ins, outs, sems):
            cp.wait()


class _ChipExchange:
    def __init__(self, arrays):
        self.inputs = list(arrays)
        self.out_shapes = [jax.ShapeDtypeStruct(a.shape, a.dtype) for a in arrays]
        self.n_sems = 6 * len(arrays)

    def _copies(self, p, t, sems, base):
        x, y, c = _place()
        myq = 2 * x + y
        chips = [(1 - x, y), (x, 1 - y), (1 - x, 1 - y)]
        sends = [pltpu.make_async_remote_copy(
            src_ref=p.at[2 * chip[0] + chip[1]], dst_ref=t.at[myq], send_sem=sems.at[base + j],
            recv_sem=sems.at[base + 3 + j], device_id=(*chip, c), device_id_type=MESH) for j, chip in enumerate(chips)]
        lands = [pltpu.make_async_remote_copy(
            src_ref=t.at[2 * chip[0] + chip[1]], dst_ref=t.at[2 * chip[0] + chip[1]], send_sem=sems.at[base + j],
            recv_sem=sems.at[base + 3 + j], device_id=(*chip, c), device_id_type=MESH) for j, chip in enumerate(chips)]
        return sends, lands

    def start(self, ins, outs, sems):
        for i, (p, t) in enumerate(zip(ins, outs)):
            for cp in self._copies(p, t, sems, 6 * i)[0]:
                cp.start()

    def finish(self, ins, outs, sems):
        for i, (p, t) in enumerate(zip(ins, outs)):
            sends, lands = self._copies(p, t, sems, 6 * i)
            for cp in lands:
                cp.wait_recv()
            for cp in sends:
                cp.wait_send()


class _SemView:
    def __init__(self, sems, base):
        self._sems, self._base = sems, base

    @property
    def at(self):
        return self

    def __getitem__(self, k):
        return self._sems.at[self._base + k]


class _Multi:
    def __init__(self, progs):
        self.progs = list(progs)
        self.inputs = [a for p in self.progs for a in p.inputs]
        self.out_shapes = [s for p in self.progs for s in p.out_shapes]
        self.n_sems = sum(p.n_sems for p in self.progs)

    def _each(self, ins, outs, sems):
        i = o = s = 0
        for p in self.progs:
            ni, no = len(p.inputs), len(p.out_shapes)
            yield p, ins[i:i + ni], outs[o:o + no], _SemView(sems, s)
            i, o, s = i + ni, o + no, s + p.n_sems

    def start(self, ins, outs, sems):
        for p, a, b, c in self._each(ins, outs, sems):
            p.start(a, b, c)

    def finish(self, ins, outs, sems):
        for p, a, b, c in self._each(ins, outs, sems):
            p.finish(a, b, c)


class _PairShare:
    def __init__(self, arrays):
        self.inputs = list(arrays)
        self.out_shapes = [jax.ShapeDtypeStruct(a.shape, a.dtype) for a in arrays]
        self.n_sems = 2 * len(arrays)

    def _copies(self, ins, outs, sems):
        x, y, c = _place()
        return [pltpu.make_async_remote_copy(
            src_ref=r, dst_ref=o, send_sem=sems.at[2 * i], recv_sem=sems.at[2 * i + 1],
            device_id=(x, y, 1 - c), device_id_type=MESH) for i, (r, o) in enumerate(zip(ins, outs))]

    def start(self, ins, outs, sems):
        for cp in self._copies(ins, outs, sems):
            cp.start()

    def finish(self, ins, outs, sems):
        for cp in self._copies(ins, outs, sems):
            cp.wait()


def _ffn_up(x, ng, sh, sc, w1, w3, *, name, comm=None):
    S = x.shape[0]
    tm, tn = _row_tile(S, 512), FF

    def body(x_ref, g_ref, sh_ref, sc_ref, w1_ref, w3_ref, h_ref, a_ref, b_ref, u_ref, hs):
        @pl.when(pl.program_id(1) == 0)
        def _():
            xv = x_ref[...]
            h = ((xv * _rms(xv)) * g_ref[...]) * (1.0 + sc_ref[...]) + sh_ref[...]
            hb = h.astype(BF16)
            hs[...] = hb
            h_ref[...] = hb

        h = hs[...]
        cols = pl.ds(pl.multiple_of(pl.program_id(1) * tn, tn), tn)
        a = _dot_nt(h, w1_ref[cols, :])
        b = _dot_nt(h, w3_ref[cols, :])
        _, sa = _silu_parts(a)
        a_ref[...] = a.astype(BF16)
        b_ref[...] = b.astype(BF16)
        u_ref[...] = (sa * b).astype(BF16)

    row = pl.BlockSpec((tm, D), lambda i, j: (i, 0))
    vec = pl.BlockSpec((1, D), lambda i, j: (0, 0))
    wsp = pl.BlockSpec((FF, D), lambda i, j: (0, 0))
    osp = pl.BlockSpec((tm, tn), lambda i, j: (i, j))
    return _hosted_call(
        body, name=name, grid=(S // tm, FF // tn),
        in_specs=[row, vec, vec, vec, wsp, wsp],
        out_specs=[row, osp, osp, osp],
        out_shape=[jax.ShapeDtypeStruct((S, D), BF16)] + [jax.ShapeDtypeStruct((S, FF), BF16)] * 3,
        scratch_shapes=[pltpu.VMEM((tm, D), BF16)],
        semantics=("parallel", "arbitrary"), args=(x, ng, sh, sc, w1, w3), comm=comm)


def _ffn_down(u, w2, x, gate, *, name, comm=None):
    S = x.shape[0]
    tm = _row_tile(S, 512)

    def body(u_ref, w2_ref, x_ref, g_ref, xo_ref, f_ref):
        f = jnp.dot(u_ref[...], w2_ref[...], preferred_element_type=F32)
        xo_ref[...] = x_ref[...] + (0.5 * g_ref[...]) * f
        f_ref[...] = f.astype(BF16)

    return _hosted_call(
        body, name=name, grid=(S // tm,),
        in_specs=[pl.BlockSpec((tm, FF), lambda i: (i, 0)), pl.BlockSpec((FF, D), lambda i: (0, 0)),
                  pl.BlockSpec((tm, D), lambda i: (i, 0)), pl.BlockSpec((1, D), lambda i: (0, 0))],
        out_specs=[pl.BlockSpec((tm, D), lambda i: (i, 0))] * 2,
        out_shape=[jax.ShapeDtypeStruct((S, D), F32), jax.ShapeDtypeStruct((S, D), BF16)],
        scratch_shapes=[], semantics=("parallel",), args=(u, w2, x, gate), comm=comm)


def _ffn_bwd_du(dx, gate, f, w2, a, b, *, name, comm=None):
    S = dx.shape[0]
    tm, tn = _row_tile(S, 512), FF
    n_i = S // tm

    def body(dx_ref, g_ref, f_ref, w_ref, a_ref, b_ref, df_ref, da_ref, db_ref, dg_ref, dfs):
        i, j = pl.program_id(0), pl.program_id(1)

        @pl.when((i == 0) & (j == 0))
        def _():
            dg_ref[...] = jnp.zeros_like(dg_ref)

        @pl.when(j == 0)
        def _():
            dxv = dx_ref[...]
            dfb = (dxv * (0.5 * g_ref[...])).astype(BF16)
            dfs[...] = dfb
            df_ref[...] = dfb
            dg_ref[...] += _rsum8(dxv * (0.5 * f_ref[...].astype(F32)))

        du = _dot_nt(dfs[...], w_ref[pl.ds(pl.multiple_of(j * tn, tn), tn), :])
        av = a_ref[...].astype(F32)
        sg, sa = _silu_parts(av)
        da_ref[...] = (du * b_ref[...].astype(F32) * (sg * (1.0 + av * (1.0 - sg)))).astype(BF16)
        db_ref[...] = (du * sa).astype(BF16)

        @pl.when((i == n_i - 1) & (j == FF // tn - 1))
        def _():
            _all_rows(dg_ref)

    row = pl.BlockSpec((tm, D), lambda i, j: (i, 0))
    blk = pl.BlockSpec((tm, tn), lambda i, j: (i, j))
    return _hosted_call(
        body, name=name, grid=(n_i, FF // tn),
        in_specs=[row, pl.BlockSpec((1, D), lambda i, j: (0, 0)), row,
                  pl.BlockSpec((FF, D), lambda i, j: (0, 0)), blk, blk],
        out_specs=[row, blk, blk, pl.BlockSpec((8, D), lambda i, j: (0, 0))],
        out_shape=[jax.ShapeDtypeStruct((S, D), BF16), jax.ShapeDtypeStruct((S, FF), BF16),
                   jax.ShapeDtypeStruct((S, FF), BF16), jax.ShapeDtypeStruct((8, D), F32)],
        scratch_shapes=[pltpu.VMEM((tm, D), BF16)],
        semantics=("arbitrary", "arbitrary"), args=(dx, gate, f, w2, a, b), comm=comm)


def _tn_matmul(a, b, *, tm, tn, name, comm=None):
    S, M = a.shape
    N = b.shape[1]
    ts = _row_tile(S, 2048 if tm * tn <= 1408 * 1024 else 1024)
    ns = S // ts

    def body(a_ref, b_ref, o_ref):
        s = pl.program_id(2)
        p = lax.dot_general(a_ref[...], b_ref[...], (((0,), (0,)), ((), ())), preferred_element_type=F32)

        @pl.when(s == 0)
        def _():
            o_ref[...] = p

        @pl.when(s > 0)
        def _():
            o_ref[...] += p

    (out,), couts = _hosted_call(
        body, name=name, grid=(M // tm, N // tn, ns),
        in_specs=[pl.BlockSpec((ts, tm), lambda i, j, s: (s, i)), pl.BlockSpec((ts, tn), lambda i, j, s: (s, j))],
        out_specs=[pl.BlockSpec((tm, tn), lambda i, j, s: (i, j))],
        out_shape=[jax.ShapeDtypeStruct((M, N), F32)],
        scratch_shapes=[], semantics=("parallel", "parallel", "arbitrary"), args=(a, b), comm=comm)
    return out if comm is None else (out, couts)


def _dh_normbwd(pairs, x, ng, sc, dx_next, *, name, comm=None):
    S = x.shape[0]
    n_p = len(pairs)
    tm = _row_tile(S, 512)
    n_i = S // tm

    def body(*refs):
        a_refs, w_refs = refs[:n_p], refs[n_p:2 * n_p]
        x_ref, g_ref, sc_ref, dxn_ref, dx_ref, p_ref = refs[2 * n_p:]
        i = pl.program_id(0)
        dh = jnp.dot(a_refs[0][...], w_refs[0][...], preferred_element_type=F32)
        for k in range(1, n_p):
            dh = dh + jnp.dot(a_refs[k][...], w_refs[k][...], preferred_element_type=F32)
        xv = x_ref[...]
        r = _rms(xv)
        xh = xv * r
        g = g_ref[...]
        dn = dh * (1.0 + sc_ref[...])
        dy = dn * g
        dx_ref[...] = dxn_ref[...] + r * (dy - xh * jnp.mean(dy * xh, axis=-1, keepdims=True))

        @pl.when(i == 0)
        def _():
            p_ref[...] = jnp.zeros_like(p_ref)

        p_ref[:, 0:D] += _rsum8(dh * (xh * g))
        p_ref[:, D:2 * D] += _rsum8(dh)
        p_ref[:, 2 * D:3 * D] += _rsum8(dn * xh)

        @pl.when(i == n_i - 1)
        def _():
            _all_rows(p_ref)

    row = pl.BlockSpec((tm, D), lambda i: (i, 0))
    vec = pl.BlockSpec((1, D), lambda i: (0, 0))
    in_specs = ([pl.BlockSpec((tm, a.shape[1]), lambda i: (i, 0)) for a, _ in pairs]
                + [pl.BlockSpec(w.shape, lambda i: (0, 0), pipeline_mode=pl.Buffered(1)) for _, w in pairs]
                + [row, vec, vec, row])
    return _hosted_call(
        body, name=name, grid=(n_i,), in_specs=in_specs,
        out_specs=[row, pl.BlockSpec((8, 3 * D), lambda i: (0, 0))],
        out_shape=[jax.ShapeDtypeStruct((S, D), F32), jax.ShapeDtypeStruct((8, 3 * D), F32)],
        scratch_shapes=[], semantics=("arbitrary",),
        args=(*[a for a, _ in pairs], *[w for _, w in pairs], x, ng, sc, dx_next), comm=comm)


def _ffn_down_loss(u, w2, x, gate, gfin, tgt):
    S = x.shape[0]
    tm = _row_tile(S, 512)
    n_i = S // tm

    def body(u_ref, w2_ref, x_ref, gt_ref, g_ref, t_ref, dx_ref, f_ref, dg_ref, loss_ref, lacc):
        i = pl.program_id(0)
        f = jnp.dot(u_ref[...], w2_ref[...], preferred_element_type=F32)
        f_ref[...] = f.astype(BF16)
        xv = x_ref[...] + (0.5 * gt_ref[...]) * f
        r = _rms(xv)
        xh = xv * r
        g = g_ref[...]
        e = xh * g - t_ref[...]
        dout = e * (1.0 / D)
        dy = dout * g
        dx_ref[...] = r * (dy - xh * jnp.mean(dy * xh, axis=-1, keepdims=True))

        @pl.when(i == 0)
        def _():
            dg_ref[...] = jnp.zeros_like(dg_ref)
            lacc[...] = jnp.zeros_like(lacc)

        dg_ref[...] += _rsum8(dout * xh)
        lacc[...] += _rsum8(e * e)

        @pl.when(i == n_i - 1)
        def _():
            _all_rows(dg_ref)
            tot = jnp.sum(jnp.sum(lacc[...], axis=0, keepdims=True), axis=1, keepdims=True)
            loss_ref[...] = jnp.broadcast_to(tot * (0.5 / D), loss_ref.shape)

    row = pl.BlockSpec((tm, D), lambda i: (i, 0))
    vec = pl.BlockSpec((1, D), lambda i: (0, 0))
    return pl.pallas_call(
        body, name="ffn2_down_loss", grid=(n_i,),
        in_specs=[pl.BlockSpec((tm, FF), lambda i: (i, 0)), pl.BlockSpec((FF, D), lambda i: (0, 0)), row, vec, vec, row],
        out_specs=[row, row, pl.BlockSpec((8, D), lambda i: (0, 0)), pl.BlockSpec((8, 128), lambda i: (0, 0))],
        out_shape=[jax.ShapeDtypeStruct((S, D), F32), jax.ShapeDtypeStruct((S, D), BF16),
                   jax.ShapeDtypeStruct((8, D), F32), jax.ShapeDtypeStruct((8, 128), F32)],
        scratch_shapes=[pltpu.VMEM((8, D), F32)],
        compiler_params=_params(("arbitrary",)),
    )(u, w2, x, gate, gfin, tgt)


def _mix_in(x, ng, sh, sc, w_in, comm=None):
    S = x.shape[0]
    tm = _row_tile(S, 512)

    def body(x_ref, g_ref, sh_ref, sc_ref, w_ref, h_ref, z_ref):
        xv = x_ref[...]
        hb = (((xv * _rms(xv)) * g_ref[...]) * (1.0 + sc_ref[...]) + sh_ref[...]).astype(BF16)
        h_ref[...] = hb
        z_ref[...] = _dot_nt(hb, w_ref[...])

    row = pl.BlockSpec((tm, D), lambda i: (i, 0))
    vec = pl.BlockSpec((1, D), lambda i: (0, 0))
    return _hosted_call(
        body, name="mix_in", grid=(S // tm,),
        in_specs=[row, vec, vec, vec, pl.BlockSpec((Z_COLS, D), lambda i: (0, 0))],
        out_specs=[row, pl.BlockSpec((tm, Z_COLS), lambda i: (i, 0))],
        out_shape=[jax.ShapeDtypeStruct((S, D), BF16), jax.ShapeDtypeStruct((S, Z_COLS), F32)],
        scratch_shapes=[], semantics=("parallel",), args=(x, ng, sh, sc, w_in), comm=comm)


def _conv_taps(u, halo, rows):
    u1 = jnp.where(rows == 0, halo[7:8, :], pltpu.roll(u, 1, 0))
    u2 = jnp.where(rows == 0, halo[6:7, :], jnp.where(rows == 1, halo[7:8, :], pltpu.roll(u, 2, 0)))
    return u1, u2


def _mix_mid(z, conv_w, gq, gkv, wuq, wukv, cs, sn, comm=None):
    S = z.shape[0]
    tm = _row_tile(S, 512)
    hb = tm // 8

    def body(z_ref, zh_ref, cw_ref, gq_ref, gkv_ref, wuq_ref, wukv_ref, cs_ref, sn_ref,
             ya_ref, q_ref, k_ref, v_ref, cqn_ref, ckvn_ref):
        i = pl.program_id(0)
        xb = z_ref[:, 0:CONV_W]
        u = z_ref[:, CONV_W:2 * CONV_W] * z_ref[:, 2 * CONV_W:3 * CONV_W]
        halo = zh_ref[:, CONV_W:2 * CONV_W] * zh_ref[:, 2 * CONV_W:3 * CONV_W]
        halo = jnp.where(i > 0, halo, 0.0)
        rows = lax.broadcasted_iota(jnp.int32, (tm, CONV_W), 0)
        u1, u2 = _conv_taps(u, halo, rows)
        y = cw_ref[0:1, :] * u2 + cw_ref[1:2, :] * u1 + cw_ref[2:3, :] * u
        ya_ref[...] = xb * y

        lane = lax.broadcasted_iota(jnp.int32, (tm, 128), 1)
        cs_v, sn_v = cs_ref[...], sn_ref[...]
        cq = z_ref[:, 3 * CONV_W:3 * CONV_W + Q_LORA]
        cqn = ((cq * _rms(cq)) * gq_ref[...]).astype(BF16)
        cqn_ref[...] = cqn
        q = _dot_nt(cqn, wuq_ref[...])
        for h in range(HEADS):
            o = h * HEAD_SLOT
            q_ref[:, o:o + 128] = q[:, o:o + 128].astype(BF16)
            q_ref[:, o + 128:o + 256] = _rope(q[:, o + 128:o + 256], cs_v, sn_v, lane).astype(BF16)

        c0 = 3 * CONV_W + Q_LORA
        ckv = z_ref[:, c0:c0 + KV_LORA]
        ckvn = ((ckv * _rms(ckv)) * gkv_ref[...]).astype(BF16)
        ckvn_ref[...] = ckvn
        kv = jnp.dot(ckvn, wukv_ref[...], preferred_element_type=F32)
        krot = _rope(z_ref[:, c0 + KV_LORA:Z_COLS], cs_v, sn_v, lane).astype(BF16)
        for h in range(HEADS):
            o = h * HEAD_SLOT
            k_ref[:, o:o + 128] = kv[:, h * 128:(h + 1) * 128].astype(BF16)
            k_ref[:, o + 128:o + 256] = krot
        v_ref[...] = kv[:, HEADS * 128:].astype(BF16)

    def rows_of(n):
        return pl.BlockSpec((tm, n), lambda i: (i, 0))

    def whole(shape):
        return pl.BlockSpec(shape, lambda i: (0, 0))

    return _hosted_call(
        body, name="mix_mid", grid=(S // tm,),
        in_specs=[rows_of(Z_COLS), pl.BlockSpec((8, Z_COLS), lambda i: (jnp.maximum(i * hb - 1, 0), 0)),
                  whole((8, CONV_W)), whole((1, Q_LORA)), whole((1, KV_LORA)),
                  whole((HEADS * HEAD_SLOT, Q_LORA)), whole((KV_LORA, 2 * HEADS * 128)),
                  rows_of(128), rows_of(128)],
        out_specs=[rows_of(CONV_W), rows_of(HEADS * HEAD_SLOT), rows_of(HEADS * HEAD_SLOT), rows_of(HEADS * V_HEAD),
                   rows_of(Q_LORA), rows_of(KV_LORA)],
        out_shape=[jax.ShapeDtypeStruct((S, CONV_W), F32), jax.ShapeDtypeStruct((S, HEADS * HEAD_SLOT), BF16),
                   jax.ShapeDtypeStruct((S, HEADS * HEAD_SLOT), BF16), jax.ShapeDtypeStruct((S, HEADS * V_HEAD), BF16),
                   jax.ShapeDtypeStruct((S, Q_LORA), BF16), jax.ShapeDtypeStruct((S, KV_LORA), BF16)],
        scratch_shapes=[], semantics=("parallel",), args=(z, z, conv_w, gq, gkv, wuq, wukv, cs, sn), comm=comm)


def _att_blocks(S):
    bk = min(1024, max(S // 4, 128))
    return bk, bk


def _pair_tables(S, k_major):
    bq, bk = _att_blocks(S)
    nq, nk = S // bq, S // bk
    vis = lambda qi, ki: ki * bk < (qi + 1) * bq
    if k_major:
        pairs = [(qi, ki) for ki in range(nk) for qi in range(nq) if vis(qi, ki)]
    else:
        pairs = [(qi, ki) for qi in range(nq) for ki in range(nk) if vis(qi, ki)]
    cols = [[p[0] for p in pairs], [p[1] for p in pairs], [int((p[1] + 1) * bk > p[0] * bq) for p in pairs]]
    return [jnp.asarray(np.array(c, np.int32)) for c in cols], len(pairs)


def _chunk_mask(r0, nr, nc):
    r = (r0 + lax.broadcasted_iota(jnp.int32, (nr, nc), 0)) // CHUNK
    c = lax.broadcasted_iota(jnp.int32, (nr, nc), 1) // CHUNK
    return c <= r


def _diag_parts(bq, bk):
    return [(0, bq // 2, bk // 2), (bq // 2, bq // 2, bk)]


def _attention(q, k, v, comm=None):
    S = q.shape[0]
    bq, bk = _att_blocks(S)
    last_k = bq // bk - 1
    tables, n_pairs = _pair_tables(S, k_major=False)

    def body(qi_ref, ki_ref, mk_ref, q_ref, k_ref, v_ref, o_ref, lse_ref, m_s, l_s, acc_s):
        p_id = pl.program_id(1)
        qi, ki = qi_ref[p_id], ki_ref[p_id]

        @pl.when(ki == 0)
        def _():
            m_s[...] = jnp.full_like(m_s, NEG)
            l_s[...] = jnp.zeros_like(l_s)
            acc_s[...] = jnp.zeros_like(acc_s)

        def update(r0, nr, nc, masked):
            rows = slice(r0, r0 + nr)
            s = _dot_nt(q_ref[rows, :], k_ref[0:nc, :])
            if masked:
                s = jnp.where(_chunk_mask(r0, nr, nc), s, NEG)
            m_prev = m_s[rows, :]
            m_new = jnp.maximum(m_prev, jnp.max(s, axis=1, keepdims=True))
            alpha = jnp.exp2((m_prev - m_new) * EXP2_SCALE)
            p = jnp.exp2((s - jnp.tile(m_new, (1, nc // 128))) * EXP2_SCALE)
            l_s[rows, :] = alpha * l_s[rows, :] + jnp.sum(p, axis=1, keepdims=True)
            acc_s[rows, :] = alpha * acc_s[rows, :] + jnp.dot(p.astype(BF16), v_ref[0:nc, :],
                                                              preferred_element_type=F32)
            m_s[rows, :] = m_new

        @pl.when(mk_ref[p_id] == 0)
        def _():
            update(0, bq, bk, False)

        @pl.when(mk_ref[p_id] == 1)
        def _():
            for part in _diag_parts(bq, bk):
                update(*part, True)

        @pl.when(ki == qi * (last_k + 1) + last_k)
        def _():
            l = l_s[...]
            o_ref[...] = acc_s[...] / l
            lse_ref[...] = m_s[...] * EXP2_SCALE + jnp.log2(l)

    return _hosted_call(
        body, name="attention", grid=(HEADS, n_pairs),
        in_specs=[pl.BlockSpec((bq, HEAD_SLOT), lambda h, p, qt, kt, mt: (qt[p], h)),
                  pl.BlockSpec((bk, HEAD_SLOT), lambda h, p, qt, kt, mt: (kt[p], h)),
                  pl.BlockSpec((bk, V_HEAD), lambda h, p, qt, kt, mt: (kt[p], h))],
        out_specs=[pl.BlockSpec((bq, V_HEAD), lambda h, p, qt, kt, mt: (qt[p], h))] * 2,
        out_shape=[jax.ShapeDtypeStruct((S, HEADS * V_HEAD), F32)] * 2,
        scratch_shapes=[pltpu.VMEM((bq, V_HEAD), F32)] * 3,
        semantics=("arbitrary", "arbitrary"), args=(q, k, v), comm=comm, prefetch=tables)


def _attention_bwd(q, k, v, do, lse2, delta):
    S = q.shape[0]
    bq, bk = _att_blocks(S)
    nq = S // bq
    tables, n_pairs = _pair_tables(S, k_major=True)

    def body(qi_ref, ki_ref, mk_ref, q_ref, k_ref, v_ref, do_ref, lse_ref, dl_ref, dq_hbm, dk_ref, dv_ref,
             dq_s, dq_b, dk_s, dv_s, sem):
        head, p_id = pl.program_id(0), pl.program_id(1)
        qi, ki = qi_ref[p_id], ki_ref[p_id]

        @pl.when(qi * bq <= ki * bk)
        def _():
            dk_s[...] = jnp.zeros_like(dk_s)
            dv_s[...] = jnp.zeros_like(dv_s)

        def update(r0, nr, nc, masked):
            rows, cols = slice(r0, r0 + nr), slice(0, nc)
            qv, kv, dov = q_ref[rows, :], k_ref[cols, :], do_ref[rows, :]
            s = _dot_nt(qv, kv)
            dp = _dot_nt(dov, v_ref[cols, :])
            if masked:
                s = jnp.where(_chunk_mask(r0, nr, nc), s, NEG)
            p = jnp.exp2(s * EXP2_SCALE - jnp.tile(lse_ref[rows, :], (1, nc // 128)))
            dv_s[cols, :] += lax.dot_general(p.astype(BF16), dov, (((0,), (0,)), ((), ())),
                                             preferred_element_type=F32)
            ds = (p * (dp - jnp.tile(dl_ref[rows, :], (1, nc // 128)))).astype(BF16)
            dk_s[cols, :] += lax.dot_general(ds, qv, (((0,), (0,)), ((), ())), preferred_element_type=F32)
            dq = jnp.dot(ds, kv, preferred_element_type=F32)
            out_rows = pl.ds(pl.multiple_of(qi * bq + r0, nr), nr)

            @pl.when(ki == 0)
            def _():
                dq_s[out_rows, :] = dq

            @pl.when(ki > 0)
            def _():
                dq_s[out_rows, :] += dq

        @pl.when(mk_ref[p_id] == 0)
        def _():
            update(0, bq, bk, False)

        @pl.when(mk_ref[p_id] == 1)
        def _():
            for part in _diag_parts(bq, bk):
                update(*part, True)

        @pl.when(qi == nq - 1)
        def _():
            dk_ref[...] = (dk_s[...] * ATT_SCALE).astype(BF16)
            dv_ref[...] = dv_s[...].astype(BF16)

        @pl.when(p_id == n_pairs - 1)
        def _():
            dq_b[...] = (dq_s[...] * ATT_SCALE).astype(BF16)
            out = pltpu.make_async_copy(
                dq_b, dq_hbm.at[:, pl.ds(pl.multiple_of(head * HEAD_SLOT, HEAD_SLOT), HEAD_SLOT)], sem)
            out.start()
            out.wait()

    grid_spec = pltpu.PrefetchScalarGridSpec(
        num_scalar_prefetch=3, grid=(HEADS, n_pairs),
        in_specs=[pl.BlockSpec((bq, HEAD_SLOT), lambda h, p, qt, kt, mt: (qt[p], h)),
                  pl.BlockSpec((bk, HEAD_SLOT), lambda h, p, qt, kt, mt: (kt[p], h)),
                  pl.BlockSpec((bk, V_HEAD), lambda h, p, qt, kt, mt: (kt[p], h)),
                  pl.BlockSpec((bq, V_HEAD), lambda h, p, qt, kt, mt: (qt[p], h)),
                  pl.BlockSpec((bq, V_HEAD), lambda h, p, qt, kt, mt: (qt[p], h)),
                  pl.BlockSpec((bq, V_HEAD), lambda h, p, qt, kt, mt: (qt[p], h))],
        out_specs=[pl.BlockSpec(memory_space=pl.ANY),
                   pl.BlockSpec((bk, HEAD_SLOT), lambda h, p, qt, kt, mt: (kt[p], h)),
                   pl.BlockSpec((bk, V_HEAD), lambda h, p, qt, kt, mt: (kt[p], h))],
        scratch_shapes=[pltpu.VMEM((S, HEAD_SLOT), F32), pltpu.VMEM((S, HEAD_SLOT), BF16),
                        pltpu.VMEM((bk, HEAD_SLOT), F32), pltpu.VMEM((bk, V_HEAD), F32), pltpu.SemaphoreType.DMA])
    return pl.pallas_call(
        body, name="attention_bwd", grid_spec=grid_spec,
        out_shape=[jax.ShapeDtypeStruct((S, HEADS * HEAD_SLOT), BF16), jax.ShapeDtypeStruct((S, HEADS * HEAD_SLOT), BF16),
                   jax.ShapeDtypeStruct((S, HEADS * V_HEAD), BF16)],
        compiler_params=_params(("arbitrary", "arbitrary")),
    )(*tables, q, k, v, do, lse2, delta)


def _group_mats():
    def blockdiag(n, g):
        idx = np.arange(n) // g
        return jnp.asarray((idx[:, None] == idx[None, :]).astype(np.float32), dtype=BF16)
    return blockdiag(CONV_W, CONV_GROUP), blockdiag(HEADS * V_HEAD, V_HEAD)


def _mix_out(ya, o, gout, w_out, x, gate, ga, gb):
    S = x.shape[0]
    tm = _row_tile(S, 512)

    def body(ya_ref, o_ref, go_ref, w_ref, x_ref, g_ref, ga_ref, gb_ref, xo_ref, yn_ref, yo_ref):
        yav, ov = ya_ref[...], o_ref[...]
        ra = lax.rsqrt(_gsum(yav * yav, ga_ref[...]) * (1.0 / CONV_GROUP) + EPS)
        rb = lax.rsqrt(_gsum(ov * ov, gb_ref[...]) * (1.0 / V_HEAD) + EPS)
        na = ((yav * ra) * go_ref[:, 0:CONV_W]).astype(BF16)
        nb = ((ov * rb) * go_ref[:, CONV_W:]).astype(BF16)
        yn_ref[:, 0:CONV_W] = na
        yn_ref[:, CONV_W:] = nb
        yo = (jnp.dot(na, w_ref[0:CONV_W, :], preferred_element_type=F32)
              + jnp.dot(nb, w_ref[CONV_W:, :], preferred_element_type=F32))
        xo_ref[...] = x_ref[...] + g_ref[...] * yo
        yo_ref[...] = yo.astype(BF16)

    row = pl.BlockSpec((tm, D), lambda i: (i, 0))
    half = pl.BlockSpec((tm, CONV_W), lambda i: (i, 0))
    vec = pl.BlockSpec((1, D), lambda i: (0, 0))
    sq = pl.BlockSpec((CONV_W, CONV_W), lambda i: (0, 0))
    return pl.pallas_call(
        body, name="mix_out", grid=(S // tm,),
        in_specs=[half, half, vec, pl.BlockSpec((D, D), lambda i: (0, 0)), row, vec, sq, sq],
        out_specs=[row, row, row],
        out_shape=[jax.ShapeDtypeStruct((S, D), F32), jax.ShapeDtypeStruct((S, D), BF16),
                   jax.ShapeDtypeStruct((S, D), BF16)],
        compiler_params=_params(("parallel",)),
    )(ya, o, gout, w_out, x, gate, ga, gb)


def _mix_out_bwd(dx, gate, yo, w_out, ya, o, gout, ga, gb, comm=None):
    S = dx.shape[0]
    tm = _row_tile(S, 512)
    n_i = S // tm

    def norm_bwd(v, dn, gain, gmat, inv_n):
        r = lax.rsqrt(_gsum(v * v, gmat) * inv_n + EPS)
        vh = v * r
        dy = dn * gain
        return r * (dy - vh * (_gsum(dy * vh, gmat) * inv_n)), dn * vh

    def body(dx_ref, g_ref, yo_ref, w_ref, ya_ref, o_ref, go_ref, ga_ref, gb_ref,
             dyo_ref, dya_ref, do_ref, dl_ref, p_ref):
        i = pl.program_id(0)
        dxv = dx_ref[...]
        dyo = (dxv * g_ref[...]).astype(BF16)
        dyo_ref[...] = dyo
        dyn = _dot_nt(dyo, w_ref[...])
        dya, dga = norm_bwd(ya_ref[...], dyn[:, 0:CONV_W], go_ref[:, 0:CONV_W], ga_ref[...], 1.0 / CONV_GROUP)
        ov = o_ref[...]
        do, dgb = norm_bwd(ov, dyn[:, CONV_W:], go_ref[:, CONV_W:], gb_ref[...], 1.0 / V_HEAD)
        dya_ref[...] = dya
        do_ref[...] = do.astype(BF16)
        dl_ref[...] = _gsum(do * ov, gb_ref[...], split=True)

        @pl.when(i == 0)
        def _():
            p_ref[...] = jnp.zeros_like(p_ref)

        p_ref[:, 0:D] += _rsum8(dxv * yo_ref[...].astype(F32))
        p_ref[:, D:D + CONV_W] += _rsum8(dga)
        p_ref[:, D + CONV_W:2 * D] += _rsum8(dgb)

        @pl.when(i == n_i - 1)
        def _():
            _all_rows(p_ref)

    row = pl.BlockSpec((tm, D), lambda i: (i, 0))
    half = pl.BlockSpec((tm, CONV_W), lambda i: (i, 0))
    vec = pl.BlockSpec((1, D), lambda i: (0, 0))
    sq = pl.BlockSpec((CONV_W, CONV_W), lambda i: (0, 0))
    return _hosted_call(
        body, name="mix_out_bwd", grid=(n_i,),
        in_specs=[row, vec, row, pl.BlockSpec((D, D), lambda i: (0, 0)), half, half, vec, sq, sq],
        out_specs=[row, half, half, half, pl.BlockSpec((8, 2 * D), lambda i: (0, 0))],
        out_shape=[jax.ShapeDtypeStruct((S, D), BF16), jax.ShapeDtypeStruct((S, CONV_W), F32),
                   jax.ShapeDtypeStruct((S, CONV_W), BF16), jax.ShapeDtypeStruct((S, CONV_W), F32),
                   jax.ShapeDtypeStruct((8, 2 * D), F32)],
        scratch_shapes=[], semantics=("arbitrary",), args=(dx, gate, yo, w_out, ya, o, gout, ga, gb), comm=comm)


MID_SUMS = 3 * CONV_W + Q_LORA + KV_LORA


def _mix_mid_bwd(z, dya, conv_w, gq, gkv, wuq, wukv, cs, sn, dq, dk, dv, comm=None):
    S = z.shape[0]
    tm = _row_tile(S, 512)
    n_i = S // tm
    hb = tm // 8
    last_blk = S // 8 - 1

    def latent_bwd(cv, dcn, gain):
        r = _rms(cv)
        ch = cv * r
        dy = dcn * gain
        return r * (dy - ch * jnp.mean(dy * ch, axis=-1, keepdims=True)), dcn * ch

    def body(z_ref, zp_ref, zn_ref, dya_ref, dyan_ref, cw_ref, gq_ref, gkv_ref, wuq_ref, wukv_ref, cs_ref, sn_ref,
             dq_ref, dk_ref, dv_ref, dz_ref, dqf_ref, dkvf_ref, p_ref):
        i = pl.program_id(0)
        xb, xc, xu = z_ref[:, 0:CONV_W], z_ref[:, CONV_W:2 * CONV_W], z_ref[:, 2 * CONV_W:3 * CONV_W]
        u = xc * xu
        halo = jnp.where(i > 0, zp_ref[:, CONV_W:2 * CONV_W] * zp_ref[:, 2 * CONV_W:3 * CONV_W], 0.0)
        rows = lax.broadcasted_iota(jnp.int32, (tm, CONV_W), 0)
        u1, u2 = _conv_taps(u, halo, rows)
        w0, w1, w2 = cw_ref[0:1, :], cw_ref[1:2, :], cw_ref[2:3, :]
        y = w0 * u2 + w1 * u1 + w2 * u
        dyav = dya_ref[...]
        dy = dyav * xb
        nxt = jnp.where(i < n_i - 1, dyan_ref[...] * zn_ref[:, 0:CONV_W], 0.0)
        dy1 = jnp.where(rows == tm - 1, nxt[0:1, :], pltpu.roll(dy, tm - 1, 0))
        dy2 = jnp.where(rows == tm - 1, nxt[1:2, :], jnp.where(rows == tm - 2, nxt[0:1, :], pltpu.roll(dy, tm - 2, 0)))
        du = w2 * dy + w1 * dy1 + w0 * dy2
        dz_ref[:, 0:CONV_W] = (dyav * y).astype(BF16)
        dz_ref[:, CONV_W:2 * CONV_W] = (du * xu).astype(BF16)
        dz_ref[:, 2 * CONV_W:3 * CONV_W] = (du * xc).astype(BF16)

        lane = lax.broadcasted_iota(jnp.int32, (tm, 128), 1)
        cs_v, sn_v = cs_ref[...], sn_ref[...]
        dkr = jnp.zeros((tm, 128), F32)
        for h in range(HEADS):
            o = h * HEAD_SLOT
            dqf_ref[:, o:o + 128] = dq_ref[:, o:o + 128]
            dqf_ref[:, o + 128:o + 256] = _rope_t(dq_ref[:, o + 128:o + 256].astype(F32), cs_v, sn_v, lane).astype(BF16)
            dkvf_ref[:, h * 128:(h + 1) * 128] = dk_ref[:, o:o + 128]
            dkr = dkr + dk_ref[:, o + 128:o + 256].astype(F32)
        dkvf_ref[:, HEADS * 128:] = dv_ref[...]

        c0 = 3 * CONV_W
        dcqn = jnp.dot(dqf_ref[...], wuq_ref[...], preferred_element_type=F32)
        dcq, dgq = latent_bwd(z_ref[:, c0:c0 + Q_LORA], dcqn, gq_ref[...])
        dz_ref[:, c0:c0 + Q_LORA] = dcq.astype(BF16)
        c1 = c0 + Q_LORA
        dckvn = _dot_nt(dkvf_ref[...], wukv_ref[...])
        dckv, dgkv = latent_bwd(z_ref[:, c1:c1 + KV_LORA], dckvn, gkv_ref[...])
        dz_ref[:, c1:c1 + KV_LORA] = dckv.astype(BF16)
        dz_ref[:, c1 + KV_LORA:Z_COLS] = _rope_t(dkr, cs_v, sn_v, lane).astype(BF16)

        @pl.when(i == 0)
        def _():
            p_ref[...] = jnp.zeros_like(p_ref)

        p_ref[:, 0:CONV_W] += _rsum8(dy * u2)
        p_ref[:, CONV_W:2 * CONV_W] += _rsum8(dy * u1)
        p_ref[:, 2 * CONV_W:3 * CONV_W] += _rsum8(dy * u)
        p_ref[:, c0:c0 + Q_LORA] += _rsum8(dgq)
        p_ref[:, c1:c1 + KV_LORA] += _rsum8(dgkv)

        @pl.when(i == n_i - 1)
        def _():
            _all_rows(p_ref)

    def rows_of(n):
        return pl.BlockSpec((tm, n), lambda i: (i, 0))

    def whole(shape):
        return pl.BlockSpec(shape, lambda i: (0, 0))

    def prev8(n):
        return pl.BlockSpec((8, n), lambda i: (jnp.maximum(i * hb - 1, 0), 0))

    def next8(n):
        return pl.BlockSpec((8, n), lambda i: (jnp.minimum((i + 1) * hb, last_blk), 0))

    return _hosted_call(
        body, name="mix_mid_bwd", grid=(n_i,),
        in_specs=[rows_of(Z_COLS), prev8(Z_COLS), next8(Z_COLS), rows_of(CONV_W), next8(CONV_W),
                  whole((8, CONV_W)), whole((1, Q_LORA)), whole((1, KV_LORA)),
                  whole((HEADS * HEAD_SLOT, Q_LORA)), whole((KV_LORA, 2 * HEADS * 128)),
                  rows_of(128), rows_of(128),
                  rows_of(HEADS * HEAD_SLOT), rows_of(HEADS * HEAD_SLOT), rows_of(HEADS * V_HEAD)],
        out_specs=[rows_of(Z_COLS), rows_of(HEADS * HEAD_SLOT), rows_of(2 * HEADS * 128), whole((8, MID_SUMS))],
        out_shape=[jax.ShapeDtypeStruct((S, Z_COLS), BF16), jax.ShapeDtypeStruct((S, HEADS * HEAD_SLOT), BF16),
                   jax.ShapeDtypeStruct((S, 2 * HEADS * 128), BF16), jax.ShapeDtypeStruct((8, MID_SUMS), F32)],
        scratch_shapes=[], semantics=("arbitrary",),
        args=(z, z, z, dya, dya, conv_w, gq, gkv, wuq, wukv, cs, sn, dq, dk, dv), comm=comm)


ADA_Q = N_MOD * D // N_CHIPS
ADA_TN = 768


def _ada_forward(c_all, ada_w_q, ada_b_q):
    def body(c_ref, w_ref, b_ref, o_ref):
        cv = c_ref[...]
        sc = (cv * jax.nn.sigmoid(cv)).astype(BF16)
        o_ref[...] = jnp.dot(sc, w_ref[...].astype(BF16), preferred_element_type=F32) + b_ref[...]

    return pl.pallas_call(
        body, name="ada_forward", grid=(ADA_Q // ADA_TN,),
        in_specs=[pl.BlockSpec((16, D), lambda j: (0, 0)), pl.BlockSpec((D, ADA_TN), lambda j: (0, j)),
                  pl.BlockSpec((1, ADA_TN), lambda j: (0, j))],
        out_specs=pl.BlockSpec((16, ADA_TN), lambda j: (0, j)),
        out_shape=jax.ShapeDtypeStruct((16, ADA_Q), F32),
        compiler_params=_params(("parallel",)),
    )(c_all, ada_w_q, ada_b_q)


def _ada_wgrad(c_all, dmod_q):
    def body(c_ref, d_ref, o_ref):
        cv = c_ref[...]
        sc = (cv * jax.nn.sigmoid(cv)).astype(BF16)
        o_ref[...] = lax.dot_general(sc, d_ref[...].astype(BF16), (((0,), (0,)), ((), ())),
                                     preferred_element_type=F32)

    return pl.pallas_call(
        body, name="ada_wgrad", grid=(ADA_Q // ADA_TN,),
        in_specs=[pl.BlockSpec((16, D), lambda j: (0, 0)), pl.BlockSpec((16, ADA_TN), lambda j: (0, j))],
        out_specs=pl.BlockSpec((D, ADA_TN), lambda j: (0, j)),
        out_shape=jax.ShapeDtypeStruct((D, ADA_Q), F32),
        compiler_params=_params(("parallel",)),
    )(c_all, dmod_q)


def _sum_devices(parts):
    n = parts.shape[1]

    def body(p_ref, o_ref):
        o_ref[...] = jnp.broadcast_to(jnp.sum(p_ref[...], axis=0, keepdims=True), o_ref.shape)

    return pl.pallas_call(
        body, name="sum_devices",
        in_specs=[pl.BlockSpec((N_DEV, n), lambda: (0, 0))], out_specs=pl.BlockSpec((N_DEV, n), lambda: (0, 0)),
        out_shape=jax.ShapeDtypeStruct((N_DEV, n), F32),
    )(parts)


def _adamw(ws, gs, ms, vs, *, name, comm=None):
    n = len(ws)
    _, rows, cols = ws[0].shape
    tr = _row_tile(rows, 256)

    def body(*refs):
        ins, outs = refs[:4 * n], refs[4 * n:]
        for k in range(n):
            w_ref, g_ref, m_ref, v_ref = ins[k], ins[n + k], ins[2 * n + k], ins[3 * n + k]
            go_ref, d_ref, mo_ref, vo_ref = outs[4 * k:4 * k + 4]
            gv = g_ref[...]
            mn = B1 * m_ref[0] + (1.0 - B1) * gv
            vn = B2 * v_ref[0] + (1.0 - B2) * (gv * gv)
            m_hat = mn / (1.0 - B1 ** STEP)
            v_hat = vn / (1.0 - B2 ** STEP)
            go_ref[0] = gv
            d_ref[0] = -LR * (m_hat / (jnp.sqrt(v_hat) + AEPS) + WD * w_ref[0])
            mo_ref[0] = mn
            vo_ref[0] = vn

    blk = pl.BlockSpec((1, tr, cols), lambda i: (0, i, 0))
    gblk = pl.BlockSpec((tr, cols), lambda i: (i, 0))
    res, couts = _hosted_call(
        body, name=name, grid=(rows // tr,),
        in_specs=[blk] * n + [gblk] * n + [blk] * 2 * n, out_specs=[blk] * 4 * n,
        out_shape=[jax.ShapeDtypeStruct((1, rows, cols), F32)] * 4 * n,
        scratch_shapes=[], semantics=("parallel",), args=(*ws, *gs, *ms, *vs), comm=comm)
    return [res[4 * k:4 * k + 4] for k in range(n)], couts


def _small_allgather(v, *, name):
    m, n = v.shape

    def body(x_ref, out_ref, send_sems, recv_sems, local_sem):
        x, y, c = _place()
        me, sibling = (x, y, c), (x, y, 1 - c)
        chips = [(1 - x, y), (x, 1 - y), (1 - x, 1 - y)]

        def rows(px, py, pc):
            return out_ref.at[pl.ds((4 * px + 2 * py + pc) * m, m), :]

        def copy(k, block, to, src=None):
            return pltpu.make_async_remote_copy(
                src_ref=rows(*block) if src is None else src, dst_ref=rows(*block),
                send_sem=send_sems.at[k], recv_sem=recv_sems.at[k], device_id=to, device_id_type=MESH)

        mine = pltpu.make_async_copy(x_ref, rows(*me), local_sem)
        mine.start()
        first = [copy(0, me, sibling, src=x_ref)]
        first += [copy(1 + j, me, (*chip, c), src=x_ref) for j, chip in enumerate(chips)]
        for cp in first:
            cp.start()
        passed = [copy(4 + j, (*chip, c), sibling) for j, chip in enumerate(chips)]
        for j, chip in enumerate(chips):
            copy(1 + j, (*chip, c), me).wait_recv()
            passed[j].start()
        copy(0, sibling, me).wait_recv()
        for j, chip in enumerate(chips):
            copy(4 + j, (*chip, 1 - c), me).wait_recv()
        for cp in first + passed:
            cp.wait_send()
        mine.wait()

    return pl.pallas_call(
        body, name=name,
        out_shape=jax.ShapeDtypeStruct((N_DEV * m, n), v.dtype),
        in_specs=[pl.BlockSpec(memory_space=pltpu.VMEM)], out_specs=pl.BlockSpec(memory_space=pltpu.VMEM),
        scratch_shapes=[pltpu.SemaphoreType.DMA((7,)), pltpu.SemaphoreType.DMA((7,)), pltpu.SemaphoreType.DMA],
    )(v)


ADD_BLOCKS = 2


def _pair_add(place, gs, ts, *, name):
    n_a = len(gs)

    def body(pl_ref, *refs):
        g_refs, t_refs = refs[:n_a], refs[n_a:2 * n_a]
        pf_refs, pb_refs = refs[2 * n_a:3 * n_a], refs[3 * n_a:]
        own = pl.program_id(1) == pl_ref[1]
        for g_ref, t_ref, pf_ref, pb_ref in zip(g_refs, t_refs, pf_refs, pb_refs):
            s = g_ref[...] + t_ref[...]
            pb_ref[...] = s.astype(BF16)

            @pl.when(own)
            def _():
                pf_ref[...] = s[0]

    def blk(t, own_half):
        tr = t.shape[1] // ADD_BLOCKS
        if own_half:
            return pl.BlockSpec((1, tr, t.shape[2]), lambda r, q, p: (q, p[0] * ADD_BLOCKS + r, 0))
        return pl.BlockSpec((1, tr, t.shape[2]), lambda r, q, p: (q, r, 0))

    def own_blk(t):
        return pl.BlockSpec((t.shape[1] // ADD_BLOCKS, t.shape[2]), lambda r, q, p: (r, 0))

    grid_spec = pltpu.PrefetchScalarGridSpec(
        num_scalar_prefetch=1, grid=(ADD_BLOCKS, N_CHIPS),
        in_specs=[blk(t, True) for t in ts] + [blk(t, False) for t in ts],
        out_specs=[own_blk(t) for t in ts] + [blk(t, False) for t in ts])
    res = pl.pallas_call(
        body, name=name, grid_spec=grid_spec,
        out_shape=([jax.ShapeDtypeStruct(t.shape[1:], F32) for t in ts]
                   + [jax.ShapeDtypeStruct(t.shape, BF16) for t in ts]),
        compiler_params=_params(("parallel", "arbitrary")),
    )(place, *gs, *ts)
    return list(res[:n_a]), list(res[n_a:])


def _chip_add(place, pfs, ts, *, name):
    n_a = len(pfs)

    def body(pl_ref, *refs):
        pf_refs, t_refs, o_refs = refs[:n_a], refs[n_a:4 * n_a], refs[4 * n_a:]
        for i, (pf_ref, o_ref) in enumerate(zip(pf_refs, o_refs)):
            t1, t2, t3 = t_refs[3 * i:3 * i + 3]
            o_ref[...] = ((pf_ref[...] + t1[0].astype(F32)) + t2[0].astype(F32)) + t3[0].astype(F32)

    def slot(t, j):
        return pl.BlockSpec((1, t.shape[1] // ADD_BLOCKS, t.shape[2]), lambda r, p: (p[1] ^ j, r, 0))

    def half(t):
        return pl.BlockSpec((t.shape[1] // ADD_BLOCKS, t.shape[2]), lambda r, p: (r, 0))

    grid_spec = pltpu.PrefetchScalarGridSpec(
        num_scalar_prefetch=1, grid=(ADD_BLOCKS,),
        in_specs=[half(t) for t in ts] + [slot(t, j) for t in ts for j in (1, 2, 3)],
        out_specs=[half(t) for t in ts])
    res = pl.pallas_call(
        body, name=name, grid_spec=grid_spec,
        out_shape=[jax.ShapeDtypeStruct(t.shape[1:], F32) for t in ts],
        compiler_params=_params(("parallel",)),
    )(place, *pfs, *[t for t in ts for _ in range(3)])
    return list(res)


BULK = [("ffn1_w1", "colsT"), ("ffn1_w3", "colsT"), ("ffn1_w2", "rows"), ("w_in", "cols"), ("w_uq", "colsT"),
        ("w_ukv", "cols"), ("w_out", "rows"), ("ffn2_w1", "colsT"), ("ffn2_w3", "colsT"), ("ffn2_w2", "rows")]
KIND = dict(BULK)


def _group(*names):
    return [b for b in BULK if b[0] in names]


W_FIRST = _group("ffn1_w1", "ffn1_w3")
W_REST = [b for b in BULK if b not in W_FIRST]
W_MIX = _group("ffn1_w2", "w_in", "w_uq", "w_ukv", "w_out")
W_FFN2 = _group("ffn2_w1", "ffn2_w3", "ffn2_w2")
G_FFN2 = _group("ffn2_w1", "ffn2_w3", "ffn2_w2")
G_MIX = _group("w_in", "w_uq", "w_ukv", "w_out")
G_FFN1 = _group("ffn1_w1", "ffn1_w3", "ffn1_w2")


def _gathered_weights(specs, shards, got, myq):
    out = {}
    for (name, kind), part in zip(specs, got):
        part = lax.dynamic_update_slice_in_dim(part, shards[name][None], myq, axis=0)
        out[name] = _full_weight(part, kind)
    return out


def _working_shard(w, kind):
    return jnp.swapaxes(w, 1, 2)[0] if kind == "colsT" else w[0]


def _full_weight(parts, kind):
    if kind == "cols":
        return jnp.transpose(parts, (1, 0, 2)).reshape(parts.shape[1], -1)
    return parts.reshape(-1, parts.shape[2])


def _quarters(g, kind):
    if kind == "cols":
        k, n = g.shape
        return jnp.transpose(g.reshape(k, N_CHIPS, n // N_CHIPS), (1, 0, 2))
    return g.reshape(N_CHIPS, g.shape[0] // N_CHIPS, g.shape[1])


def _pad_heads(w_uq_t):
    w = w_uq_t.reshape(HEADS, QK_NOPE + QK_ROPE, Q_LORA)
    return jnp.pad(w, ((0, 0), (0, HEAD_SLOT - QK_NOPE - QK_ROPE), (0, 0))).reshape(HEADS * HEAD_SLOT, Q_LORA)


def _unpad_heads(g):
    return g.reshape(HEADS, HEAD_SLOT, Q_LORA)[:, :QK_NOPE + QK_ROPE].reshape(HEADS * (QK_NOPE + QK_ROPE), Q_LORA)


def _split_kv(w_ukv):
    return jnp.transpose(w_ukv.reshape(KV_LORA, HEADS, 2, 128), (0, 2, 1, 3)).reshape(KV_LORA, 2 * HEADS * 128)


def _merge_kv(g):
    return jnp.transpose(g.reshape(KV_LORA, 2, HEADS, 128), (0, 2, 1, 3)).reshape(KV_LORA, 2 * HEADS * 128)


def _rope_tables(positions):
    inv_freq = ROPE_THETA ** (-jnp.arange(0, QK_ROPE, 2, dtype=F32) / QK_ROPE)
    ang = positions.astype(F32)[:, None] * inv_freq
    cos, sin, zero = jnp.cos(ang), jnp.sin(ang), jnp.zeros((positions.shape[0], 64), F32)
    return jnp.concatenate([cos, cos, zero], axis=1), jnp.concatenate([sin, sin, zero], axis=1)


def _assemble(place, specs, rhs, others):
    south = place[0] == 0
    return {b[0]: jnp.concatenate([jnp.where(south, rh, ot), jnp.where(south, ot, rh)], axis=0)
            for b, rh, ot in zip(specs, rhs, others)}


def _local_step(x, positions, target, mod, vec, conv_w, w_first, rest_shards, place, tail_host=None):
    row = lambda k: mod[k:k + 1]
    sh1, sc1, g1, sh2, sc2, g2, sh3, sc3, g3 = [row(k) for k in range(N_MOD)]
    cs, sn = _rope_tables(positions)
    cw8 = jnp.pad(conv_w, ((0, 5), (0, 0)))
    ga, gb = _group_mats()
    dist = place is not None
    comm = lambda prog: prog if dist else None

    w = dict(w_first) if dist else {**rest_shards, **w_first}

    def gather(specs):
        return _Gather([rest_shards[b[0]] for b in specs]) if dist else None

    def arrived(specs, got):
        if dist:
            w.update(_gathered_weights(specs, rest_shards, got, place[1]))

    (h1, a1, b1, u1), got = _ffn_up(x, vec["norm_ffn1_g"], sh1, sc1, w["ffn1_w1"], w["ffn1_w3"],
                                    name="ffn1_up", comm=gather(W_MIX))
    arrived(W_MIX, got)
    w_in = jnp.pad(w["w_in"].T, ((0, Z_COLS - IN_COLS), (0, 0)))
    wuq = _pad_heads(w["w_uq"])
    wukv = _split_kv(w["w_ukv"])
    (x1, f1), _ = _ffn_down(u1, w["ffn1_w2"], x, g1, name="ffn1_down")
    (h2, z), _ = _mix_in(x1, vec["norm_mix_g"], sh2, sc2, w_in)
    (ya, q, k, v, cqn, ckvn), _ = _mix_mid(z, cw8, vec["q_norm_g"], vec["kv_norm_g"], wuq, wukv, cs, sn)
    (o, lse), got = _attention(q, k, v, comm=gather(W_FFN2))
    arrived(W_FFN2, got)
    x2, yn, yo = _mix_out(ya, o, vec["out_norm_g"], w["w_out"], x1, g2, ga, gb)
    (h3, a3, b3, u3), _ = _ffn_up(x2, vec["norm_ffn2_g"], sh3, sc3, w["ffn2_w1"], w["ffn2_w3"], name="ffn2_up")
    dx3, f3, dgfin, loss_blk = _ffn_down_loss(u3, w["ffn2_w2"], x2, g3, vec["final_norm_g"], target)

    grads, reduced = {}, {}

    def tn(a, b, tm, tn_, name, prog=None):
        if prog is None:
            return _tn_matmul(a, b, tm=tm, tn=tn_, name=name), None
        return _tn_matmul(a, b, tm=tm, tn=tn_, name=name, comm=prog)

    def slab_of(specs):
        return [_quarters(grads[n], kind) for n, kind in specs]

    (df3, da3, db3, dg3), _ = _ffn_bwd_du(dx3, g3, f3, w["ffn2_w2"], a3, b3, name="ffn2_bwd_du")
    grads["ffn2_w2"], _ = tn(u3, df3, FF // 2, D, "ffn2_dw2")
    grads["ffn2_w1"], _ = tn(da3, h3, FF // 2, D, "ffn2_dw1")
    grads["ffn2_w3"], _ = tn(db3, h3, FF // 2, D, "ffn2_dw3")
    (dx2, s3), _ = _dh_normbwd([(da3, w["ffn2_w1"]), (db3, w["ffn2_w3"])], x2, vec["norm_ffn2_g"], sc3, dx3,
                               name="ffn2_bwd_dh")

    p1 = slab_of(G_FFN2) if dist else None
    (dyo, dya, do, delta, s_out), t1 = _mix_out_bwd(dx2, g2, yo, w["w_out"], ya, o, vec["out_norm_g"], ga, gb,
                                                    comm=comm(_PairExchange(p1) if dist else None))
    grads["w_out"], _ = tn(yn, dyo, D, D, "dw_out")
    if dist:
        pf1, pb1 = _pair_add(place, p1, t1, name="ffn2g_pair_add")
    dq, dk, dv = _attention_bwd(q, k, v, do, lse, delta)
    (dz, dqf, dkvf, s_mid), t2 = _mix_mid_bwd(z, dya, cw8, vec["q_norm_g"], vec["kv_norm_g"], wuq, wukv, cs, sn,
                                              dq, dk, dv, comm=comm(_ChipExchange(pb1) if dist else None))
    g_uq, _ = tn(dqf, cqn, HEADS * HEAD_SLOT, Q_LORA, "dw_uq")
    g_ukv, _ = tn(ckvn, dkvf, KV_LORA, 2 * HEADS * 128, "dw_ukv")
    grads["w_uq"], grads["w_ukv"] = _unpad_heads(g_uq), _merge_kv(g_ukv)
    g_in, _ = tn(h2, dz, D, Z_COLS, "dw_in")
    grads["w_in"] = g_in[:, :IN_COLS]
    (dx1, s2), _ = _dh_normbwd([(dz, w_in)], x1, vec["norm_mix_g"], sc2, dx2, name="mix_bwd_dh")

    p2 = slab_of(G_MIX) if dist else None
    rh_ffn2 = _chip_add(place, pf1, t2, name="ffn2g_chip_add") if dist else None
    (df1, da1, db1, dg1), got = _ffn_bwd_du(dx1, g1, f1, w["ffn1_w2"], a1, b1, name="ffn1_bwd_du",
                                            comm=comm(_Multi([_PairExchange(p2), _PairShare(rh_ffn2)]) if dist else None))
    if dist:
        t1 = got[:len(p2)]
        reduced.update(_assemble(place, G_FFN2, rh_ffn2, got[len(p2):]))
    dh_pairs = [(da1, w["ffn1_w1"]), (db1, w["ffn1_w3"])]
    if dist:
        pf2, pb2 = _pair_add(place, p2, t1, name="mixg_pair_add")
        grads["ffn1_w2"], t2 = tn(u1, df1, FF // 2, D, "ffn1_dw2", _ChipExchange(pb2))
        rh_mix = _chip_add(place, pf2, t2, name="mixg_chip_add")
        q_w2 = [_quarters(grads["ffn1_w2"], KIND["ffn1_w2"])]
        grads["ffn1_w1"], got = tn(da1, h1, FF // 2, D, "ffn1_dw1", _Multi([_PairShare(rh_mix), _PairExchange(q_w2)]))
        reduced.update(_assemble(place, G_MIX, rh_mix, got[:len(rh_mix)]))
        pf_w2, pb_w2 = _pair_add(place, q_w2, got[len(rh_mix):], name="ffn1w2_pair_add")
        q_w1 = [_quarters(grads["ffn1_w1"], KIND["ffn1_w1"])]
        grads["ffn1_w3"], got = tn(db1, h1, FF // 2, D, "ffn1_dw3", _Multi([_ChipExchange(pb_w2), _PairExchange(q_w1)]))
        t2_w2 = got[:1]
        pf_w1, pb_w1 = _pair_add(place, q_w1, got[1:], name="ffn1w1_pair_add")
        q_w3 = [_quarters(grads["ffn1_w3"], KIND["ffn1_w3"])]
        (dx0, s1), got = _dh_normbwd(dh_pairs, x, vec["norm_ffn1_g"], sc1, dx1, name="ffn1_bwd_dh",
                                     comm=_Multi([_ChipExchange(pb_w1), _PairExchange(q_w3)]))
        t2_w1 = got[:1]
        pf_w3, pb_w3 = _pair_add(place, q_w3, got[1:], name="ffn1w3_pair_add")
        t2_w3 = tail_host(_ChipExchange(pb_w3), reduced)
        rh = _chip_add(place, pf_w1 + pf_w3 + pf_w2, t2_w1 + t2_w3 + t2_w2, name="ffn1g_chip_add")
        reduced.update(_assemble(place, G_FFN1, rh, _run_comm(_PairShare(rh), name="ffn1g_pair_share")))
    else:
        grads["ffn1_w2"], _ = tn(u1, df1, FF // 2, D, "ffn1_dw2")
        grads["ffn1_w1"], _ = tn(da1, h1, FF // 2, D, "ffn1_dw1")
        grads["ffn1_w3"], _ = tn(db1, h1, FF // 2, D, "ffn1_dw3")
        (dx0, s1), _ = _dh_normbwd(dh_pairs, x, vec["norm_ffn1_g"], sc1, dx1, name="ffn1_bwd_dh")
        reduced = grads

    def part(s, k):
        return s[0:1, k * D:(k + 1) * D]

    dmod = jnp.concatenate([part(s1, 1), part(s1, 0), dg1[0:1], part(s2, 1), part(s2, 0), part(s_out, 0),
                            part(s3, 1), part(s3, 0), dg3[0:1]], axis=1)
    small = {"norm_ffn1_g": part(s1, 2), "norm_mix_g": part(s2, 2), "out_norm_g": part(s_out, 1),
             "norm_ffn2_g": part(s3, 2), "final_norm_g": dgfin[0:1],
             "q_norm_g": s_mid[0:1, 3 * CONV_W:3 * CONV_W + Q_LORA],
             "kv_norm_g": s_mid[0:1, 3 * CONV_W + Q_LORA:MID_SUMS], "conv_w": s_mid[0:1, 0:3 * CONV_W]}
    return loss_blk, dx0, reduced, dmod, small


SMALL = [("norm_ffn1_g", D), ("norm_mix_g", D), ("out_norm_g", D), ("norm_ffn2_g", D), ("final_norm_g", D),
         ("q_norm_g", Q_LORA), ("kv_norm_g", KV_LORA), ("conv_w", 3 * CONV_W)]
WEIGHTS = ['ada_w', 'ada_b', 'norm_ffn1_g', 'ffn1_w1', 'ffn1_w3', 'ffn1_w2', 'norm_mix_g', 'w_in', 'conv_w',
           'q_norm_g', 'w_uq', 'kv_norm_g', 'w_ukv', 'out_norm_g', 'w_out', 'norm_ffn2_g', 'ffn2_w1', 'ffn2_w3',
           'ffn2_w2', 'final_norm_g']


def kernel(x, c, positions, ada_w, ada_b, norm_ffn1_g, ffn1_w1, ffn1_w3, ffn1_w2, norm_mix_g, w_in, conv_w, q_norm_g, w_uq, kv_norm_g, w_ukv, out_norm_g, w_out, norm_ffn2_g, ffn2_w1, ffn2_w3, ffn2_w2, final_norm_g, loss_target, m_ada_w, m_ada_b, m_norm_ffn1_g, m_ffn1_w1, m_ffn1_w3, m_ffn1_w2, m_norm_mix_g, m_w_in, m_conv_w, m_q_norm_g, m_w_uq, m_kv_norm_g, m_w_ukv, m_out_norm_g, m_w_out, m_norm_ffn2_g, m_ffn2_w1, m_ffn2_w3, m_ffn2_w2, m_final_norm_g, v_ada_w, v_ada_b, v_norm_ffn1_g, v_ffn1_w1, v_ffn1_w3, v_ffn1_w2, v_norm_mix_g, v_w_in, v_conv_w, v_q_norm_g, v_w_uq, v_kv_norm_g, v_w_ukv, v_out_norm_g, v_w_out, v_norm_ffn2_g, v_ffn2_w1, v_ffn2_w3, v_ffn2_w2, v_final_norm_g):
    args = dict(locals())
    wts = {n: args[n] for n in WEIGHTS}
    mom = {n: args["m_" + n] for n in WEIGHTS}
    var = {n: args["v_" + n] for n in WEIGHTS}
    ax, ay, ac = _place()
    myq = 2 * ax + ay
    me = 2 * myq + ac
    place = jnp.stack([ac, myq]).astype(jnp.int32)

    shards = {name: _working_shard(wts[name], kind).astype(BF16) for name, kind in BULK}
    first = _run_comm(_Gather([shards[b[0]] for b in W_FIRST]), name="gather_ffn1")
    w_first = _gathered_weights(W_FIRST, shards, first, myq)

    mine = jnp.concatenate([c, conv_w[0].reshape(1, 3 * CONV_W // N_CHIPS)], axis=1)
    seen = _small_allgather(jnp.pad(mine, ((0, 7), (0, 0))), name="gather_cond").reshape(N_DEV, 8, -1)[:, 0]
    c_all = jnp.pad(seen[:, :D], ((0, 8), (0, 0)))
    conv_full = jnp.transpose(seen[0::2, D:].reshape(N_CHIPS, 3, CONV_W // N_CHIPS), (1, 0, 2)).reshape(3, CONV_W)
    ada_b_q = lax.dynamic_slice_in_dim(ada_b, myq * ADA_Q, ADA_Q, axis=1)
    mod_q = _ada_forward(c_all, ada_w[0], ada_b_q)
    mod_all = _small_allgather(mod_q, name="gather_mod").reshape(N_DEV, 16, ADA_Q)
    mod_rows = jnp.transpose(mod_all[0::2, :N_DEV], (1, 0, 2)).reshape(N_DEV, N_MOD * D)
    mod = lax.dynamic_slice_in_dim(mod_rows, me, 1, axis=0).reshape(N_MOD, D)

    vec = {n: wts[n] for n in ("norm_ffn1_g", "norm_mix_g", "q_norm_g", "kv_norm_g", "out_norm_g", "norm_ffn2_g")}
    vec["final_norm_g"] = final_norm_g.reshape(1, D)
    g, delta, new_m, new_v = {}, {}, {}, {}

    def adam(names, grads, comm=None):
        views = [(lambda a: jnp.swapaxes(a, 1, 2)) if KIND.get(n) == "colsT" else (lambda a: a) for n in names]
        res, couts = _adamw([vw(wts[n]) for n, vw in zip(names, views)], [grads[n] for n in names],
                            [vw(mom[n]) for n, vw in zip(names, views)], [vw(var[n]) for n, vw in zip(names, views)],
                            name="adamw_" + names[0], comm=comm)
        for n, vw, r in zip(names, views, res):
            g[n], delta[n], new_m[n], new_v[n] = [vw(a) for a in r]
        return couts

    loss_blk, grad_x, gq, dmod, small = _local_step(
        x[0], positions[0], loss_target[0], mod, vec, conv_full, w_first, {b[0]: shards[b[0]] for b in W_REST}, place,
        tail_host=lambda prog, grads: adam([b[0] for b in G_FFN2], grads, prog))
    loss = lax.psum(loss_blk[0, 0], ("x", "y", "c"))

    rows = jnp.concatenate([dmod] + [small[n] for n, _ in SMALL], axis=1)
    width = rows.shape[1]
    fold = -(-width // (8 * 128)) * 128
    rows = jnp.pad(rows, ((0, 0), (0, 8 * fold - width))).reshape(8, fold)
    every = _small_allgather(rows, name="gather_small").reshape(N_DEV, 8 * fold)[:, :width]
    total = _sum_devices(every)[0:1]
    dmod_q = lax.dynamic_slice_in_dim(every[:, :N_MOD * D], myq * ADA_Q, ADA_Q, axis=1)
    sg = {name: gq[name] for name, *_ in BULK}
    sg["ada_w"] = _ada_wgrad(c_all, jnp.pad(dmod_q, ((0, 8), (0, 0))))
    sg["ada_b"] = total[:, :N_MOD * D]
    off = N_MOD * D
    for n, width in SMALL:
        sg[n] = total[:, off:off + width]
        off += width
    sg["conv_w"] = lax.dynamic_slice_in_dim(sg["conv_w"].reshape(3, CONV_W), myq * (CONV_W // N_CHIPS),
                                            CONV_W // N_CHIPS, axis=1)

    for name in ["ada_w"] + [b[0] for b in BULK if b not in G_FFN2]:
        adam([name], sg)
    smalls = ["ada_b"] + [n for n, _ in SMALL]

    def packed(d):
        flat = jnp.concatenate([d[n].reshape(1, -1) for n in smalls], axis=1)
        return jnp.pad(flat.reshape(-1, D), ((0, 1), (0, 0)))

    res = _adamw([packed(wts)[None]], [packed(sg)], [packed(mom)[None]], [packed(var)[None]],
                 name="adamw_small")[0][0][1:]
    off = 0
    for n in smalls:
        size = wts[n].size
        for d, r in zip((delta, new_m, new_v), res):
            d[n] = r.reshape(-1)[off:off + size].reshape(wts[n].shape)
        g[n] = sg[n].reshape(wts[n].shape)
        off += size

    return (loss, grad_x[None], *[g[n] for n in WEIGHTS], *[delta[n] for n in WEIGHTS],
            *[new_m[n] for n in WEIGHTS], *[new_v[n] for n in WEIGHTS])
```

```python
import numpy as np
import jax
import jax.numpy as jnp
from jax import lax
from jax.experimental import pallas as pl
from jax.experimental.pallas import tpu as pltpu

F32 = jnp.float32
BF16 = jnp.bfloat16
MESH = pl.DeviceIdType.MESH

D = 1024
FF = 2816
CONV_W = 512
CONV_GROUP = 64
HEADS = 4
QK_NOPE = 128
QK_ROPE = 64
V_HEAD = 128
Q_LORA = 384
KV_LORA = 256
HEAD_SLOT = 256
IN_COLS = 3 * CONV_W + Q_LORA + KV_LORA + QK_ROPE
Z_COLS = 2304
EPS = 1e-6
ROPE_THETA = 10000.0
CHUNK = 64
ATT_SCALE = (QK_NOPE + QK_ROPE) ** -0.5
NEG = -1e30
EXP2_SCALE = ATT_SCALE * 1.4426950408889634
N_MOD = 9

LR, B1, B2, AEPS, WD, STEP = 0.001, 0.9, 0.999, 1e-08, 0.01, 10

N_CHIPS = 4
N_DEV = 8
VMEM_LIMIT = 56 << 20


def _params(sem, vmem=VMEM_LIMIT):
    return pltpu.CompilerParams(dimension_semantics=sem, vmem_limit_bytes=vmem)


def _rms(v):
    return lax.rsqrt(jnp.mean(v * v, axis=-1, keepdims=True) + EPS)


def _rsum8(v):
    t, n = v.shape
    return jnp.sum(v.reshape(t // 8, 8, n), axis=0)


def _all_rows(ref):
    ref[...] = jnp.broadcast_to(jnp.sum(ref[...], axis=0, keepdims=True), ref.shape)


def _gsum(v, gmat, split=False):
    hi = v.astype(BF16)
    out = jnp.dot(hi, gmat, preferred_element_type=F32)
    if split:
        out = out + jnp.dot((v - hi.astype(F32)).astype(BF16), gmat, preferred_element_type=F32)
    return out


def _dot_nt(a, b):
    return lax.dot_general(a, b, (((1,), (1,)), ((), ())), preferred_element_type=F32)


def _silu_parts(a):
    sg = jax.nn.sigmoid(a)
    return sg, a * sg


def _rope(xr, cs, sn, lane):
    rh = jnp.where(lane < 32, -pltpu.roll(xr, 96, 1), pltpu.roll(xr, 32, 1))
    return xr * cs + rh * sn


def _rope_t(g, cs, sn, lane):
    y = g * sn
    rt = jnp.where(lane < 32, pltpu.roll(y, 96, 1), jnp.where(lane < 64, -pltpu.roll(y, 32, 1), 0.0))
    return g * cs + rt


def _row_tile(rows, pref, mult=8):
    t = min(rows, pref) // mult * mult
    while rows % t:
        t -= mult
    return t


def _place():
    return lax.axis_index("x"), lax.axis_index("y"), lax.axis_index("c")


ANY = pl.BlockSpec(memory_space=pl.ANY)


def _hosted_call(body, *, name, grid, in_specs, out_specs, out_shape, scratch_shapes, semantics, args, comm=None,
                 prefetch=()):
    n_in, n_out, n_scr, n_pf = len(in_specs), len(out_specs), len(scratch_shapes), len(prefetch)

    def call(fn, in_specs, out_specs, out_shape, scratch_shapes, semantics, operands):
        spec = pltpu.PrefetchScalarGridSpec(num_scalar_prefetch=n_pf, grid=grid, in_specs=list(in_specs),
                                            out_specs=list(out_specs), scratch_shapes=list(scratch_shapes))
        return pl.pallas_call(fn, name=name, grid_spec=spec, out_shape=list(out_shape),
                              compiler_params=_params(semantics))(*prefetch, *operands)

    if comm is None:
        return list(call(body, in_specs, out_specs, out_shape, scratch_shapes, semantics, args)), []
    n_ci, n_co = len(comm.inputs), len(comm.out_shapes)
    total = int(np.prod(grid))

    def hosted(*refs):
        tables, refs = refs[:n_pf], refs[n_pf:]
        ins, refs = refs[:n_in], refs[n_in:]
        cins, refs = refs[:n_ci], refs[n_ci:]
        outs, refs = refs[:n_out], refs[n_out:]
        couts, refs = refs[:n_co], refs[n_co:]
        scratch, sems = refs[:n_scr], refs[n_scr]
        step = pl.program_id(0)
        for ax in range(1, len(grid)):
            step = step * grid[ax] + pl.program_id(ax)

        @pl.when(step == 0)
        def _():
            comm.start(cins, couts, sems)

        body(*tables, *ins, *outs, *scratch)

        @pl.when(step == total - 1)
        def _():
            comm.finish(cins, couts, sems)

    res = call(hosted, list(in_specs) + [ANY] * n_ci, list(out_specs) + [ANY] * n_co,
               list(out_shape) + list(comm.out_shapes),
               list(scratch_shapes) + [pltpu.SemaphoreType.DMA((comm.n_sems,))],
               ("arbitrary",) * len(grid), (*args, *comm.inputs))
    return list(res[:n_out]), list(res[n_out:])


def _run_comm(comm, *, name):
    n_ci = len(comm.inputs)

    def body(*refs):
        cins, couts, sems = refs[:n_ci], refs[n_ci:-1], refs[-1]
        comm.start(cins, couts, sems)
        comm.finish(cins, couts, sems)

    return list(pl.pallas_call(
        body, name=name, out_shape=list(comm.out_shapes), in_specs=[ANY] * n_ci,
        out_specs=[ANY] * len(comm.out_shapes), scratch_shapes=[pltpu.SemaphoreType.DMA((comm.n_sems,))],
    )(*comm.inputs))


class _Gather:
    def __init__(self, slabs):
        self.inputs = list(slabs)
        self.out_shapes = [jax.ShapeDtypeStruct((N_CHIPS,) + s.shape, s.dtype) for s in slabs]
        self.n_sems = 12 * len(slabs)

    @staticmethod
    def _copy(out, sems, base, k, chip, hc, to, src=None):
        H = out.shape[1] // 2
        half = out.at[2 * chip[0] + chip[1], pl.ds(hc * H, H), :]
        return pltpu.make_async_remote_copy(
            src_ref=half if src is None else src, dst_ref=half, send_sem=sems.at[base + k],
            recv_sem=sems.at[base + 6 + k], device_id=to, device_id_type=MESH)

    def _firsts(self, src, out, sems, base):
        x, y, c = _place()
        H = src.shape[0] // 2
        chips = [(1 - x, y), (x, 1 - y), (1 - x, 1 - y)]
        return [self._copy(out, sems, base, j, (x, y), c, (*chip, c), src=src.at[pl.ds(c * H, H), :])
                for j, chip in enumerate(chips)]

    def start(self, ins, outs, sems):
        for i, (src, out) in enumerate(zip(ins, outs)):
            for cp in self._firsts(src, out, sems, 12 * i):
                cp.start()

    def finish(self, ins, outs, sems):
        x, y, c = _place()
        chips = [(1 - x, y), (x, 1 - y), (1 - x, 1 - y)]
        passed = []
        for i, out in enumerate(outs):
            for j, chip in enumerate(chips):
                self._copy(out, sems, 12 * i, j, chip, c, (x, y, c)).wait_recv()
                cp = self._copy(out, sems, 12 * i, 3 + j, chip, c, (x, y, 1 - c))
                cp.start()
                passed.append(cp)
        for i, out in enumerate(outs):
            for j, chip in enumerate(chips):
                self._copy(out, sems, 12 * i, 3 + j, chip, 1 - c, (x, y, c)).wait_recv()
        for cp in passed:
            cp.wait_send()
        for i, (src, out) in enumerate(zip(ins, outs)):
            for cp in self._firsts(src, out, sems, 12 * i):
                cp.wait_send()


class _PairExchange:
    def __init__(self, arrays):
        self.inputs = list(arrays)
        self.out_shapes = [jax.ShapeDtypeStruct((N_CHIPS, a.shape[1] // 2, a.shape[2]), a.dtype) for a in arrays]
        self.n_sems = 2 * len(arrays)

    def _copies(self, ins, outs, sems):
        x, y, c = _place()
        return [pltpu.make_async_remote_copy(
            src_ref=g.at[:, pl.ds((1 - c) * t.shape[1], t.shape[1]), :], dst_ref=t, send_sem=sems.at[2 * i],
            recv_sem=sems.at[2 * i + 1], device_id=(x, y, 1 - c), device_id_type=MESH)
            for i, (g, t) in enumerate(zip(ins, outs))]

    def start(self, ins, outs, sems):
        for cp in self._copies(ins, outs, sems):
            cp.start()

    def finish(self, ins, outs, sems):
        for cp in self._copies(ins, outs, sems):
            cp.wait()


class _ChipExchange:
    def __init__(self, arrays):
        self.inputs = list(arrays)
        self.out_shapes = [jax.ShapeDtypeStruct(a.shape, a.dtype) for a in arrays]
        self.n_sems = 6 * len(arrays)

    def _copies(self, p, t, sems, base):
        x, y, c = _place()
        myq = 2 * x + y
        chips = [(1 - x, y), (x, 1 - y), (1 - x, 1 - y)]
        sends = [pltpu.make_async_remote_copy(
            src_ref=p.at[2 * chip[0] + chip[1]], dst_ref=t.at[myq], send_sem=sems.at[base + j],
            recv_sem=sems.at[base + 3 + j], device_id=(*chip, c), device_id_type=MESH) for j, chip in enumerate(chips)]
        lands = [pltpu.make_async_remote_copy(
            src_ref=t.at[2 * chip[0] + chip[1]], dst_ref=t.at[2 * chip[0] + chip[1]], send_sem=sems.at[base + j],
            recv_sem=sems.at[base + 3 + j], device_id=(*chip, c), device_id_type=MESH) for j, chip in enumerate(chips)]
        return sends, lands

    def start(self, ins, outs, sems):
        for i, (p, t) in enumerate(zip(ins, outs)):
            for cp in self._copies(p, t, sems, 6 * i)[0]:
                cp.start()

    def finish(self, ins, outs, sems):
        for i, (p, t) in enumerate(zip(ins, outs)):
            sends, lands = self._copies(p, t, sems, 6 * i)
            for cp in lands:
                cp.wait_recv()
            for cp in sends:
                cp.wait_send()


class _SemView:
    def __init__(self, sems, base):
        self._sems, self._base = sems, base

    @property
    def at(self):
        return self

    def __getitem__(self, k):
        return self._sems.at[self._base + k]


class _Multi:
    def __init__(self, progs):
        self.progs = list(progs)
        self.inputs = [a for p in self.progs for a in p.inputs]
        self.out_shapes = [s for p in self.progs for s in p.out_shapes]
        self.n_sems = sum(p.n_sems for p in self.progs)

    def _each(self, ins, outs, sems):
        i = o = s = 0
        for p in self.progs:
            ni, no = len(p.inputs), len(p.out_shapes)
            yield p, ins[i:i + ni], outs[o:o + no], _SemView(sems, s)
            i, o, s = i + ni, o + no, s + p.n_sems

    def start(self, ins, outs, sems):
        for p, a, b, c in self._each(ins, outs, sems):
            p.start(a, b, c)

    def finish(self, ins, outs, sems):
        for p, a, b, c in self._each(ins, outs, sems):
            p.finish(a, b, c)


class _PairShare:
    def __init__(self, arrays):
        self.inputs = list(arrays)
        self.out_shapes = [jax.ShapeDtypeStruct(a.shape, a.dtype) for a in arrays]
        self.n_sems = 2 * len(arrays)

    def _copies(self, ins, outs, sems):
        x, y, c = _place()
        return [pltpu.make_async_remote_copy(
            src_ref=r, dst_ref=o, send_sem=sems.at[2 * i], recv_sem=sems.at[2 * i + 1],
            device_id=(x, y, 1 - c), device_id_type=MESH) for i, (r, o) in enumerate(zip(ins, outs))]

    def start(self, ins, outs, sems):
        for cp in self._copies(ins, outs, sems):
            cp.start()

    def finish(self, ins, outs, sems):
        for cp in self._copies(ins, outs, sems):
            cp.wait()


def _ffn_up(x, ng, sh, sc, w1, w3, *, name, comm=None):
    S = x.shape[0]
    tm, tn = _row_tile(S, 512), FF

    def body(x_ref, g_ref, sh_ref, sc_ref, w1_ref, w3_ref, h_ref, a_ref, b_ref, u_ref, hs):
        @pl.when(pl.program_id(1) == 0)
        def _():
            xv = x_ref[...]
            h = ((xv * _rms(xv)) * g_ref[...]) * (1.0 + sc_ref[...]) + sh_ref[...]
            hb = h.astype(BF16)
            hs[...] = hb
            h_ref[...] = hb

        h = hs[...]
        cols = pl.ds(pl.multiple_of(pl.program_id(1) * tn, tn), tn)
        a = _dot_nt(h, w1_ref[cols, :])
        b = _dot_nt(h, w3_ref[cols, :])
        _, sa = _silu_parts(a)
        a_ref[...] = a.astype(BF16)
        b_ref[...] = b.astype(BF16)
        u_ref[...] = (sa * b).astype(BF16)

    row = pl.BlockSpec((tm, D), lambda i, j: (i, 0))
    vec = pl.BlockSpec((1, D), lambda i, j: (0, 0))
    wsp = pl.BlockSpec((FF, D), lambda i, j: (0, 0))
    osp = pl.BlockSpec((tm, tn), lambda i, j: (i, j))
    return _hosted_call(
        body, name=name, grid=(S // tm, FF // tn),
        in_specs=[row, vec, vec, vec, wsp, wsp],
        out_specs=[row, osp, osp, osp],
        out_shape=[jax.ShapeDtypeStruct((S, D), BF16)] + [jax.ShapeDtypeStruct((S, FF), BF16)] * 3,
        scratch_shapes=[pltpu.VMEM((tm, D), BF16)],
        semantics=("parallel", "arbitrary"), args=(x, ng, sh, sc, w1, w3), comm=comm)


def _ffn_down(u, w2, x, gate, *, name, comm=None):
    S = x.shape[0]
    tm = _row_tile(S, 512)

    def body(u_ref, w2_ref, x_ref, g_ref, xo_ref, f_ref):
        f = jnp.dot(u_ref[...], w2_ref[...], preferred_element_type=F32)
        xo_ref[...] = x_ref[...] + (0.5 * g_ref[...]) * f
        f_ref[...] = f.astype(BF16)

    return _hosted_call(
        body, name=name, grid=(S // tm,),
        in_specs=[pl.BlockSpec((tm, FF), lambda i: (i, 0)), pl.BlockSpec((FF, D), lambda i: (0, 0)),
                  pl.BlockSpec((tm, D), lambda i: (i, 0)), pl.BlockSpec((1, D), lambda i: (0, 0))],
        out_specs=[pl.BlockSpec((tm, D), lambda i: (i, 0))] * 2,
        out_shape=[jax.ShapeDtypeStruct((S, D), F32), jax.ShapeDtypeStruct((S, D), BF16)],
        scratch_shapes=[], semantics=("parallel",), args=(u, w2, x, gate), comm=comm)


def _ffn_bwd_du(dx, gate, f, w2, a, b, *, name, comm=None):
    S = dx.shape[0]
    tm, tn = _row_tile(S, 512), FF
    n_i = S // tm

    def body(dx_ref, g_ref, f_ref, w_ref, a_ref, b_ref, df_ref, da_ref, db_ref, dg_ref, dfs):
        i, j = pl.program_id(0), pl.program_id(1)

        @pl.when((i == 0) & (j == 0))
        def _():
            dg_ref[...] = jnp.zeros_like(dg_ref)

        @pl.when(j == 0)
        def _():
            dxv = dx_ref[...]
            dfb = (dxv * (0.5 * g_ref[...])).astype(BF16)
            dfs[...] = dfb
            df_ref[...] = dfb
            dg_ref[...] += _rsum8(dxv * (0.5 * f_ref[...].astype(F32)))

        du = _dot_nt(dfs[...], w_ref[pl.ds(pl.multiple_of(j * tn, tn), tn), :])
        av = a_ref[...].astype(F32)
        sg, sa = _silu_parts(av)
        da_ref[...] = (du * b_ref[...].astype(F32) * (sg * (1.0 + av * (1.0 - sg)))).astype(BF16)
        db_ref[...] = (du * sa).astype(BF16)

        @pl.when((i == n_i - 1) & (j == FF // tn - 1))
        def _():
            _all_rows(dg_ref)

    row = pl.BlockSpec((tm, D), lambda i, j: (i, 0))
    blk = pl.BlockSpec((tm, tn), lambda i, j: (i, j))
    return _hosted_call(
        body, name=name, grid=(n_i, FF // tn),
        in_specs=[row, pl.BlockSpec((1, D), lambda i, j: (0, 0)), row,
                  pl.BlockSpec((FF, D), lambda i, j: (0, 0)), blk, blk],
        out_specs=[row, blk, blk, pl.BlockSpec((8, D), lambda i, j: (0, 0))],
        out_shape=[jax.ShapeDtypeStruct((S, D), BF16), jax.ShapeDtypeStruct((S, FF), BF16),
                   jax.ShapeDtypeStruct((S, FF), BF16), jax.ShapeDtypeStruct((8, D), F32)],
        scratch_shapes=[pltpu.VMEM((tm, D), BF16)],
        semantics=("arbitrary", "arbitrary"), args=(dx, gate, f, w2, a, b), comm=comm)


def _tn_matmul(a, b, *, tm, tn, name, comm=None):
    S, M = a.shape
    N = b.shape[1]
    ts = _row_tile(S, 2048 if tm * tn <= 1408 * 1024 else 1024)
    ns = S // ts

    def body(a_ref, b_ref, o_ref):
        s = pl.program_id(2)
        p = lax.dot_general(a_ref[...], b_ref[...], (((0,), (0,)), ((), ())), preferred_element_type=F32)

        @pl.when(s == 0)
        def _():
            o_ref[...] = p

        @pl.when(s > 0)
        def _():
            o_ref[...] += p

    (out,), couts = _hosted_call(
        body, name=name, grid=(M // tm, N // tn, ns),
        in_specs=[pl.BlockSpec((ts, tm), lambda i, j, s: (s, i)), pl.BlockSpec((ts, tn), lambda i, j, s: (s, j))],
        out_specs=[pl.BlockSpec((tm, tn), lambda i, j, s: (i, j))],
        out_shape=[jax.ShapeDtypeStruct((M, N), F32)],
        scratch_shapes=[], semantics=("parallel", "parallel", "arbitrary"), args=(a, b), comm=comm)
    return out if comm is None else (out, couts)


def _dh_normbwd(pairs, x, ng, sc, dx_next, *, name, comm=None):
    S = x.shape[0]
    n_p = len(pairs)
    tm = _row_tile(S, 512)
    n_i = S // tm

    def body(*refs):
        a_refs, w_refs = refs[:n_p], refs[n_p:2 * n_p]
        x_ref, g_ref, sc_ref, dxn_ref, dx_ref, p_ref = refs[2 * n_p:]
        i = pl.program_id(0)
        dh = jnp.dot(a_refs[0][...], w_refs[0][...], preferred_element_type=F32)
        for k in range(1, n_p):
            dh = dh + jnp.dot(a_refs[k][...], w_refs[k][...], preferred_element_type=F32)
        xv = x_ref[...]
        r = _rms(xv)
        xh = xv * r
        g = g_ref[...]
        dn = dh * (1.0 + sc_ref[...])
        dy = dn * g
        dx_ref[...] = dxn_ref[...] + r * (dy - xh * jnp.mean(dy * xh, axis=-1, keepdims=True))

        @pl.when(i == 0)
        def _():
            p_ref[...] = jnp.zeros_like(p_ref)

        p_ref[:, 0:D] += _rsum8(dh * (xh * g))
        p_ref[:, D:2 * D] += _rsum8(dh)
        p_ref[:, 2 * D:3 * D] += _rsum8(dn * xh)

        @pl.when(i == n_i - 1)
        def _():
            _all_rows(p_ref)

    row = pl.BlockSpec((tm, D), lambda i: (i, 0))
    vec = pl.BlockSpec((1, D), lambda i: (0, 0))
    in_specs = ([pl.BlockSpec((tm, a.shape[1]), lambda i: (i, 0)) for a, _ in pairs]
                + [pl.BlockSpec(w.shape, lambda i: (0, 0), pipeline_mode=pl.Buffered(1)) for _, w in pairs]
                + [row, vec, vec, row])
    return _hosted_call(
        body, name=name, grid=(n_i,), in_specs=in_specs,
        out_specs=[row, pl.BlockSpec((8, 3 * D), lambda i: (0, 0))],
        out_shape=[jax.ShapeDtypeStruct((S, D), F32), jax.ShapeDtypeStruct((8, 3 * D), F32)],
        scratch_shapes=[], semantics=("arbitrary",),
        args=(*[a for a, _ in pairs], *[w for _, w in pairs], x, ng, sc, dx_next), comm=comm)


def _ffn_down_loss(u, w2, x, gate, gfin, tgt):
    S = x.shape[0]
    tm = _row_tile(S, 512)
    n_i = S // tm

    def body(u_ref, w2_ref, x_ref, gt_ref, g_ref, t_ref, dx_ref, f_ref, dg_ref, loss_ref, lacc):
        i = pl.program_id(0)
        f = jnp.dot(u_ref[...], w2_ref[...], preferred_element_type=F32)
        f_ref[...] = f.astype(BF16)
        xv = x_ref[...] + (0.5 * gt_ref[...]) * f
        r = _rms(xv)
        xh = xv * r
        g = g_ref[...]
        e = xh * g - t_ref[...]
        dout = e * (1.0 / D)
        dy = dout * g
        dx_ref[...] = r * (dy - xh * jnp.mean(dy * xh, axis=-1, keepdims=True))

        @pl.when(i == 0)
        def _():
            dg_ref[...] = jnp.zeros_like(dg_ref)
            lacc[...] = jnp.zeros_like(lacc)

        dg_ref[...] += _rsum8(dout * xh)
        lacc[...] += _rsum8(e * e)

        @pl.when(i == n_i - 1)
        def _():
            _all_rows(dg_ref)
            tot = jnp.sum(jnp.sum(lacc[...], axis=0, keepdims=True), axis=1, keepdims=True)
            loss_ref[...] = jnp.broadcast_to(tot * (0.5 / D), loss_ref.shape)

    row = pl.BlockSpec((tm, D), lambda i: (i, 0))
    vec = pl.BlockSpec((1, D), lambda i: (0, 0))
    return pl.pallas_call(
        body, name="ffn2_down_loss", grid=(n_i,),
        in_specs=[pl.BlockSpec((tm, FF), lambda i: (i, 0)), pl.BlockSpec((FF, D), lambda i: (0, 0)), row, vec, vec, row],
        out_specs=[row, row, pl.BlockSpec((8, D), lambda i: (0, 0)), pl.BlockSpec((8, 128), lambda i: (0, 0))],
        out_shape=[jax.ShapeDtypeStruct((S, D), F32), jax.ShapeDtypeStruct((S, D), BF16),
                   jax.ShapeDtypeStruct((8, D), F32), jax.ShapeDtypeStruct((8, 128), F32)],
        scratch_shapes=[pltpu.VMEM((8, D), F32)],
        compiler_params=_params(("arbitrary",)),
    )(u, w2, x, gate, gfin, tgt)


def _mix_in(x, ng, sh, sc, w_in, comm=None):
    S = x.shape[0]
    tm = _row_tile(S, 512)

    def body(x_ref, g_ref, sh_ref, sc_ref, w_ref, h_ref, z_ref):
        xv = x_ref[...]
        hb = (((xv * _rms(xv)) * g_ref[...]) * (1.0 + sc_ref[...]) + sh_ref[...]).astype(BF16)
        h_ref[...] = hb
        z_ref[...] = _dot_nt(hb, w_ref[...])

    row = pl.BlockSpec((tm, D), lambda i: (i, 0))
    vec = pl.BlockSpec((1, D), lambda i: (0, 0))
    return _hosted_call(
        body, name="mix_in", grid=(S // tm,),
        in_specs=[row, vec, vec, vec, pl.BlockSpec((Z_COLS, D), lambda i: (0, 0))],
        out_specs=[row, pl.BlockSpec((tm, Z_COLS), lambda i: (i, 0))],
        out_shape=[jax.ShapeDtypeStruct((S, D), BF16), jax.ShapeDtypeStruct((S, Z_COLS), F32)],
        scratch_shapes=[], semantics=("parallel",), args=(x, ng, sh, sc, w_in), comm=comm)


def _conv_taps(u, halo, rows):
    u1 = jnp.where(rows == 0, halo[7:8, :], pltpu.roll(u, 1, 0))
    u2 = jnp.where(rows == 0, halo[6:7, :], jnp.where(rows == 1, halo[7:8, :], pltpu.roll(u, 2, 0)))
    return u1, u2


def _mix_mid(z, conv_w, gq, gkv, wuq, wukv, cs, sn, comm=None):
    S = z.shape[0]
    tm = _row_tile(S, 512)
    hb = tm // 8

    def body(z_ref, zh_ref, cw_ref, gq_ref, gkv_ref, wuq_ref, wukv_ref, cs_ref, sn_ref,
             ya_ref, q_ref, k_ref, v_ref, cqn_ref, ckvn_ref):
        i = pl.program_id(0)
        xb = z_ref[:, 0:CONV_W]
        u = z_ref[:, CONV_W:2 * CONV_W] * z_ref[:, 2 * CONV_W:3 * CONV_W]
        halo = zh_ref[:, CONV_W:2 * CONV_W] * zh_ref[:, 2 * CONV_W:3 * CONV_W]
        halo = jnp.where(i > 0, halo, 0.0)
        rows = lax.broadcasted_iota(jnp.int32, (tm, CONV_W), 0)
        u1, u2 = _conv_taps(u, halo, rows)
        y = cw_ref[0:1, :] * u2 + cw_ref[1:2, :] * u1 + cw_ref[2:3, :] * u
        ya_ref[...] = xb * y

        lane = lax.broadcasted_iota(jnp.int32, (tm, 128), 1)
        cs_v, sn_v = cs_ref[...], sn_ref[...]
        cq = z_ref[:, 3 * CONV_W:3 * CONV_W + Q_LORA]
        cqn = ((cq * _rms(cq)) * gq_ref[...]).astype(BF16)
        cqn_ref[...] = cqn
        q = _dot_nt(cqn, wuq_ref[...])
        for h in range(HEADS):
            o = h * HEAD_SLOT
            q_ref[:, o:o + 128] = q[:, o:o + 128].astype(BF16)
            q_ref[:, o + 128:o + 256] = _rope(q[:, o + 128:o + 256], cs_v, sn_v, lane).astype(BF16)

        c0 = 3 * CONV_W + Q_LORA
        ckv = z_ref[:, c0:c0 + KV_LORA]
        ckvn = ((ckv * _rms(ckv)) * gkv_ref[...]).astype(BF16)
        ckvn_ref[...] = ckvn
        kv = jnp.dot(ckvn, wukv_ref[...], preferred_element_type=F32)
        krot = _rope(z_ref[:, c0 + KV_LORA:Z_COLS], cs_v, sn_v, lane).astype(BF16)
        for h in range(HEADS):
            o = h * HEAD_SLOT
            k_ref[:, o:o + 128] = kv[:, h * 128:(h + 1) * 128].astype(BF16)
            k_ref[:, o + 128:o + 256] = krot
        v_ref[...] = kv[:, HEADS * 128:].astype(BF16)

    def rows_of(n):
        return pl.BlockSpec((tm, n), lambda i: (i, 0))

    def whole(shape):
        return pl.BlockSpec(shape, lambda i: (0, 0))

    return _hosted_call(
        body, name="mix_mid", grid=(S // tm,),
        in_specs=[rows_of(Z_COLS), pl.BlockSpec((8, Z_COLS), lambda i: (jnp.maximum(i * hb - 1, 0), 0)),
                  whole((8, CONV_W)), whole((1, Q_LORA)), whole((1, KV_LORA)),
                  whole((HEADS * HEAD_SLOT, Q_LORA)), whole((KV_LORA, 2 * HEADS * 128)),
                  rows_of(128), rows_of(128)],
        out_specs=[rows_of(CONV_W), rows_of(HEADS * HEAD_SLOT), rows_of(HEADS * HEAD_SLOT), rows_of(HEADS * V_HEAD),
                   rows_of(Q_LORA), rows_of(KV_LORA)],
        out_shape=[jax.ShapeDtypeStruct((S, CONV_W), F32), jax.ShapeDtypeStruct((S, HEADS * HEAD_SLOT), BF16),
                   jax.ShapeDtypeStruct((S, HEADS * HEAD_SLOT), BF16), jax.ShapeDtypeStruct((S, HEADS * V_HEAD), BF16),
                   jax.ShapeDtypeStruct((S, Q_LORA), BF16), jax.ShapeDtypeStruct((S, KV_LORA), BF16)],
        scratch_shapes=[], semantics=("parallel",), args=(z, z, conv_w, gq, gkv, wuq, wukv, cs, sn), comm=comm)


def _att_blocks(S):
    bk = min(1024, max(S // 4, 128))
    return bk, bk


def _pair_tables(S, k_major):
    bq, bk = _att_blocks(S)
    nq, nk = S // bq, S // bk
    vis = lambda qi, ki: ki * bk < (qi + 1) * bq
    if k_major:
        pairs = [(qi, ki) for ki in range(nk) for qi in range(nq) if vis(qi, ki)]
    else:
        pairs = [(qi, ki) for qi in range(nq) for ki in range(nk) if vis(qi, ki)]
    cols = [[p[0] for p in pairs], [p[1] for p in pairs], [int((p[1] + 1) * bk > p[0] * bq) for p in pairs]]
    return [jnp.asarray(np.array(c, np.int32)) for c in cols], len(pairs)


def _chunk_mask(r0, nr, nc):
    r = (r0 + lax.broadcasted_iota(jnp.int32, (nr, nc), 0)) // CHUNK
    c = lax.broadcasted_iota(jnp.int32, (nr, nc), 1) // CHUNK
    return c <= r


def _diag_parts(bq, bk):
    return [(0, bq // 2, bk // 2), (bq // 2, bq // 2, bk)]


def _attention(q, k, v, comm=None):
    S = q.shape[0]
    bq, bk = _att_blocks(S)
    last_k = bq // bk - 1
    tables, n_pairs = _pair_tables(S, k_major=False)

    def body(qi_ref, ki_ref, mk_ref, q_ref, k_ref, v_ref, o_ref, lse_ref, m_s, l_s, acc_s):
        p_id = pl.program_id(1)
        qi, ki = qi_ref[p_id], ki_ref[p_id]

        @pl.when(ki == 0)
        def _():
            m_s[...] = jnp.full_like(m_s, NEG)
            l_s[...] = jnp.zeros_like(l_s)
            acc_s[...] = jnp.zeros_like(acc_s)

        def update(r0, nr, nc, masked):
            rows = slice(r0, r0 + nr)
            s = _dot_nt(q_ref[rows, :], k_ref[0:nc, :])
            if masked:
                s = jnp.where(_chunk_mask(r0, nr, nc), s, NEG)
            m_prev = m_s[rows, :]
            m_new = jnp.maximum(m_prev, jnp.max(s, axis=1, keepdims=True))
            alpha = jnp.exp2((m_prev - m_new) * EXP2_SCALE)
            p = jnp.exp2((s - jnp.tile(m_new, (1, nc // 128))) * EXP2_SCALE)
            l_s[rows, :] = alpha * l_s[rows, :] + jnp.sum(p, axis=1, keepdims=True)
            acc_s[rows, :] = alpha * acc_s[rows, :] + jnp.dot(p.astype(BF16), v_ref[0:nc, :],
                                                              preferred_element_type=F32)
            m_s[rows, :] = m_new

        @pl.when(mk_ref[p_id] == 0)
        def _():
            update(0, bq, bk, False)

        @pl.when(mk_ref[p_id] == 1)
        def _():
            for part in _diag_parts(bq, bk):
                update(*part, True)

        @pl.when(ki == qi * (last_k + 1) + last_k)
        def _():
            l = l_s[...]
            o_ref[...] = acc_s[...] / l
            lse_ref[...] = m_s[...] * EXP2_SCALE + jnp.log2(l)

    return _hosted_call(
        body, name="attention", grid=(HEADS, n_pairs),
        in_specs=[pl.BlockSpec((bq, HEAD_SLOT), lambda h, p, qt, kt, mt: (qt[p], h)),
                  pl.BlockSpec((bk, HEAD_SLOT), lambda h, p, qt, kt, mt: (kt[p], h)),
                  pl.BlockSpec((bk, V_HEAD), lambda h, p, qt, kt, mt: (kt[p], h))],
        out_specs=[pl.BlockSpec((bq, V_HEAD), lambda h, p, qt, kt, mt: (qt[p], h))] * 2,
        out_shape=[jax.ShapeDtypeStruct((S, HEADS * V_HEAD), F32)] * 2,
        scratch_shapes=[pltpu.VMEM((bq, V_HEAD), F32)] * 3,
        semantics=("arbitrary", "arbitrary"), args=(q, k, v), comm=comm, prefetch=tables)


def _attention_bwd(q, k, v, do, lse2, delta):
    S = q.shape[0]
    bq, bk = _att_blocks(S)
    nq = S // bq
    tables, n_pairs = _pair_tables(S, k_major=True)

    def body(qi_ref, ki_ref, mk_ref, q_ref, k_ref, v_ref, do_ref, lse_ref, dl_ref, dq_hbm, dk_ref, dv_ref,
             dq_s, dq_b, dk_s, dv_s, sem):
        head, p_id = pl.program_id(0), pl.program_id(1)
        qi, ki = qi_ref[p_id], ki_ref[p_id]

        @pl.when(qi * bq <= ki * bk)
        def _():
            dk_s[...] = jnp.zeros_like(dk_s)
            dv_s[...] = jnp.zeros_like(dv_s)

        def update(r0, nr, nc, masked):
            rows, cols = slice(r0, r0 + nr), slice(0, nc)
            qv, kv, dov = q_ref[rows, :], k_ref[cols, :], do_ref[rows, :]
            s = _dot_nt(qv, kv)
            dp = _dot_nt(dov, v_ref[cols, :])
            if masked:
                s = jnp.where(_chunk_mask(r0, nr, nc), s, NEG)
            p = jnp.exp2(s * EXP2_SCALE - jnp.tile(lse_ref[rows, :], (1, nc // 128)))
            dv_s[cols, :] += lax.dot_general(p.astype(BF16), dov, (((0,), (0,)), ((), ())),
                                             preferred_element_type=F32)
            ds = (p * (dp - jnp.tile(dl_ref[rows, :], (1, nc // 128)))).astype(BF16)
            dk_s[cols, :] += lax.dot_general(ds, qv, (((0,), (0,)), ((), ())), preferred_element_type=F32)
            dq = jnp.dot(ds, kv, preferred_element_type=F32)
            out_rows = pl.ds(pl.multiple_of(qi * bq + r0, nr), nr)

            @pl.when(ki == 0)
            def _():
                dq_s[out_rows, :] = dq

            @pl.when(ki > 0)
            def _():
                dq_s[out_rows, :] += dq

        @pl.when(mk_ref[p_id] == 0)
        def _():
            update(0, bq, bk, False)

        @pl.when(mk_ref[p_id] == 1)
        def _():
            for part in _diag_parts(bq, bk):
                update(*part, True)

        @pl.when(qi == nq - 1)
        def _():
            dk_ref[...] = (dk_s[...] * ATT_SCALE).astype(BF16)
            dv_ref[...] = dv_s[...].astype(BF16)

        @pl.when(p_id == n_pairs - 1)
        def _():
            dq_b[...] = (dq_s[...] * ATT_SCALE).astype(BF16)
            out = pltpu.make_async_copy(
                dq_b, dq_hbm.at[:, pl.ds(pl.multiple_of(head * HEAD_SLOT, HEAD_SLOT), HEAD_SLOT)], sem)
            out.start()
            out.wait()

    grid_spec = pltpu.PrefetchScalarGridSpec(
        num_scalar_prefetch=3, grid=(HEADS, n_pairs),
        in_specs=[pl.BlockSpec((bq, HEAD_SLOT), lambda h, p, qt, kt, mt: (qt[p], h)),
                  pl.BlockSpec((bk, HEAD_SLOT), lambda h, p, qt, kt, mt: (kt[p], h)),
                  pl.BlockSpec((bk, V_HEAD), lambda h, p, qt, kt, mt: (kt[p], h)),
                  pl.BlockSpec((bq, V_HEAD), lambda h, p, qt, kt, mt: (qt[p], h)),
                  pl.BlockSpec((bq, V_HEAD), lambda h, p, qt, kt, mt: (qt[p], h)),
                  pl.BlockSpec((bq, V_HEAD), lambda h, p, qt, kt, mt: (qt[p], h))],
        out_specs=[pl.BlockSpec(memory_space=pl.ANY),
                   pl.BlockSpec((bk, HEAD_SLOT), lambda h, p, qt, kt, mt: (kt[p], h)),
                   pl.BlockSpec((bk, V_HEAD), lambda h, p, qt, kt, mt: (kt[p], h))],
        scratch_shapes=[pltpu.VMEM((S, HEAD_SLOT), F32), pltpu.VMEM((S, HEAD_SLOT), BF16),
                        pltpu.VMEM((bk, HEAD_SLOT), F32), pltpu.VMEM((bk, V_HEAD), F32), pltpu.SemaphoreType.DMA])
    return pl.pallas_call(
        body, name="attention_bwd", grid_spec=grid_spec,
        out_shape=[jax.ShapeDtypeStruct((S, HEADS * HEAD_SLOT), BF16), jax.ShapeDtypeStruct((S, HEADS * HEAD_SLOT), BF16),
                   jax.ShapeDtypeStruct((S, HEADS * V_HEAD), BF16)],
        compiler_params=_params(("arbitrary", "arbitrary")),
    )(*tables, q, k, v, do, lse2, delta)


def _group_mats():
    def blockdiag(n, g):
        idx = np.arange(n) // g
        return jnp.asarray((idx[:, None] == idx[None, :]).astype(np.float32), dtype=BF16)
    return blockdiag(CONV_W, CONV_GROUP), blockdiag(HEADS * V_HEAD, V_HEAD)


def _mix_out(ya, o, gout, w_out, x, gate, ga, gb):
    S = x.shape[0]
    tm = _row_tile(S, 512)

    def body(ya_ref, o_ref, go_ref, w_ref, x_ref, g_ref, ga_ref, gb_ref, xo_ref, yn_ref, yo_ref):
        yav, ov = ya_ref[...], o_ref[...]
        ra = lax.rsqrt(_gsum(yav * yav, ga_ref[...]) * (1.0 / CONV_GROUP) + EPS)
        rb = lax.rsqrt(_gsum(ov * ov, gb_ref[...]) * (1.0 / V_HEAD) + EPS)
        na = ((yav * ra) * go_ref[:, 0:CONV_W]).astype(BF16)
        nb = ((ov * rb) * go_ref[:, CONV_W:]).astype(BF16)
        yn_ref[:, 0:CONV_W] = na
        yn_ref[:, CONV_W:] = nb
        yo = (jnp.dot(na, w_ref[0:CONV_W, :], preferred_element_type=F32)
              + jnp.dot(nb, w_ref[CONV_W:, :], preferred_element_type=F32))
        xo_ref[...] = x_ref[...] + g_ref[...] * yo
        yo_ref[...] = yo.astype(BF16)

    row = pl.BlockSpec((tm, D), lambda i: (i, 0))
    half = pl.BlockSpec((tm, CONV_W), lambda i: (i, 0))
    vec = pl.BlockSpec((1, D), lambda i: (0, 0))
    sq = pl.BlockSpec((CONV_W, CONV_W), lambda i: (0, 0))
    return pl.pallas_call(
        body, name="mix_out", grid=(S // tm,),
        in_specs=[half, half, vec, pl.BlockSpec((D, D), lambda i: (0, 0)), row, vec, sq, sq],
        out_specs=[row, row, row],
        out_shape=[jax.ShapeDtypeStruct((S, D), F32), jax.ShapeDtypeStruct((S, D), BF16),
                   jax.ShapeDtypeStruct((S, D), BF16)],
        compiler_params=_params(("parallel",)),
    )(ya, o, gout, w_out, x, gate, ga, gb)


def _mix_out_bwd(dx, gate, yo, w_out, ya, o, gout, ga, gb, comm=None):
    S = dx.shape[0]
    tm = _row_tile(S, 512)
    n_i = S // tm

    def norm_bwd(v, dn, gain, gmat, inv_n):
        r = lax.rsqrt(_gsum(v * v, gmat) * inv_n + EPS)
        vh = v * r
        dy = dn * gain
        return r * (dy - vh * (_gsum(dy * vh, gmat) * inv_n)), dn * vh

    def body(dx_ref, g_ref, yo_ref, w_ref, ya_ref, o_ref, go_ref, ga_ref, gb_ref,
             dyo_ref, dya_ref, do_ref, dl_ref, p_ref):
        i = pl.program_id(0)
        dxv = dx_ref[...]
        dyo = (dxv * g_ref[...]).astype(BF16)
        dyo_ref[...] = dyo
        dyn = _dot_nt(dyo, w_ref[...])
        dya, dga = norm_bwd(ya_ref[...], dyn[:, 0:CONV_W], go_ref[:, 0:CONV_W], ga_ref[...], 1.0 / CONV_GROUP)
        ov = o_ref[...]
        do, dgb = norm_bwd(ov, dyn[:, CONV_W:], go_ref[:, CONV_W:], gb_ref[...], 1.0 / V_HEAD)
        dya_ref[...] = dya
        do_ref[...] = do.astype(BF16)
        dl_ref[...] = _gsum(do * ov, gb_ref[...], split=True)

        @pl.when(i == 0)
        def _():
            p_ref[...] = jnp.zeros_like(p_ref)

        p_ref[:, 0:D] += _rsum8(dxv * yo_ref[...].astype(F32))
        p_ref[:, D:D + CONV_W] += _rsum8(dga)
        p_ref[:, D + CONV_W:2 * D] += _rsum8(dgb)

        @pl.when(i == n_i - 1)
        def _():
            _all_rows(p_ref)

    row = pl.BlockSpec((tm, D), lambda i: (i, 0))
    half = pl.BlockSpec((tm, CONV_W), lambda i: (i, 0))
    vec = pl.BlockSpec((1, D), lambda i: (0, 0))
    sq = pl.BlockSpec((CONV_W, CONV_W), lambda i: (0, 0))
    return _hosted_call(
        body, name="mix_out_bwd", grid=(n_i,),
        in_specs=[row, vec, row, pl.BlockSpec((D, D), lambda i: (0, 0)), half, half, vec, sq, sq],
        out_specs=[row, half, half, half, pl.BlockSpec((8, 2 * D), lambda i: (0, 0))],
        out_shape=[jax.ShapeDtypeStruct((S, D), BF16), jax.ShapeDtypeStruct((S, CONV_W), F32),
                   jax.ShapeDtypeStruct((S, CONV_W), BF16), jax.ShapeDtypeStruct((S, CONV_W), F32),
                   jax.ShapeDtypeStruct((8, 2 * D), F32)],
        scratch_shapes=[], semantics=("arbitrary",), args=(dx, gate, yo, w_out, ya, o, gout, ga, gb), comm=comm)


MID_SUMS = 3 * CONV_W + Q_LORA + KV_LORA


def _mix_mid_bwd(z, dya, conv_w, gq, gkv, wuq, wukv, cs, sn, dq, dk, dv, comm=None):
    S = z.shape[0]
    tm = _row_tile(S, 256)
    n_i = S // tm
    hb = tm // 8
    last_blk = S // 8 - 1

    def latent_bwd(cv, dcn, gain):
        r = _rms(cv)
        ch = cv * r
        dy = dcn * gain
        return r * (dy - ch * jnp.mean(dy * ch, axis=-1, keepdims=True)), dcn * ch

    def body(z_ref, zp_ref, zn_ref, dya_ref, dyan_ref, cw_ref, gq_ref, gkv_ref, wuq_ref, wukv_ref, cs_ref, sn_ref,
             dq_ref, dk_ref, dv_ref, dz_ref, dqf_ref, dkvf_ref, p_ref):
        i = pl.program_id(0)
        xb, xc, xu = z_ref[:, 0:CONV_W], z_ref[:, CONV_W:2 * CONV_W], z_ref[:, 2 * CONV_W:3 * CONV_W]
        u = xc * xu
        halo = jnp.where(i > 0, zp_ref[:, CONV_W:2 * CONV_W] * zp_ref[:, 2 * CONV_W:3 * CONV_W], 0.0)
        rows = lax.broadcasted_iota(jnp.int32, (tm, CONV_W), 0)
        u1, u2 = _conv_taps(u, halo, rows)
        w0, w1, w2 = cw_ref[0:1, :], cw_ref[1:2, :], cw_ref[2:3, :]
        y = w0 * u2 + w1 * u1 + w2 * u
        dyav = dya_ref[...]
        dy = dyav * xb
        nxt = jnp.where(i < n_i - 1, dyan_ref[...] * zn_ref[:, 0:CONV_W], 0.0)
        dy1 = jnp.where(rows == tm - 1, nxt[0:1, :], pltpu.roll(dy, tm - 1, 0))
        dy2 = jnp.where(rows == tm - 1, nxt[1:2, :], jnp.where(rows == tm - 2, nxt[0:1, :], pltpu.roll(dy, tm - 2, 0)))
        du = w2 * dy + w1 * dy1 + w0 * dy2
        dz_ref[:, 0:CONV_W] = (dyav * y).astype(BF16)
        dz_ref[:, CONV_W:2 * CONV_W] = (du * xu).astype(BF16)
        dz_ref[:, 2 * CONV_W:3 * CONV_W] = (du * xc).astype(BF16)

        lane = lax.broadcasted_iota(jnp.int32, (tm, 128), 1)
        cs_v, sn_v = cs_ref[...], sn_ref[...]
        dkr = jnp.zeros((tm, 128), F32)
        for h in range(HEADS):
            o = h * HEAD_SLOT
            dqf_ref[:, o:o + 128] = dq_ref[:, o:o + 128]
            dqf_ref[:, o + 128:o + 256] = _rope_t(dq_ref[:, o + 128:o + 256].astype(F32), cs_v, sn_v, lane).astype(BF16)
            dkvf_ref[:, h * 128:(h + 1) * 128] = dk_ref[:, o:o + 128]
            dkr = dkr + dk_ref[:, o + 128:o + 256].astype(F32)
        dkvf_ref[:, HEADS * 128:] = dv_ref[...]

        c0 = 3 * CONV_W
        dcqn = jnp.dot(dqf_ref[...], wuq_ref[...], preferred_element_type=F32)
        dcq, dgq = latent_bwd(z_ref[:, c0:c0 + Q_LORA], dcqn, gq_ref[...])
        dz_ref[:, c0:c0 + Q_LORA] = dcq.astype(BF16)
        c1 = c0 + Q_LORA
        dckvn = _dot_nt(dkvf_ref[...], wukv_ref[...])
        dckv, dgkv = latent_bwd(z_ref[:, c1:c1 + KV_LORA], dckvn, gkv_ref[...])
        dz_ref[:, c1:c1 + KV_LORA] = dckv.astype(BF16)
        dz_ref[:, c1 + KV_LORA:Z_COLS] = _rope_t(dkr, cs_v, sn_v, lane).astype(BF16)

        @pl.when(i == 0)
        def _():
            p_ref[...] = jnp.zeros_like(p_ref)

        p_ref[:, 0:CONV_W] += _rsum8(dy * u2)
        p_ref[:, CONV_W:2 * CONV_W] += _rsum8(dy * u1)
        p_ref[:, 2 * CONV_W:3 * CONV_W] += _rsum8(dy * u)
        p_ref[:, c0:c0 + Q_LORA] += _rsum8(dgq)
        p_ref[:, c1:c1 + KV_LORA] += _rsum8(dgkv)

        @pl.when(i == n_i - 1)
        def _():
            _all_rows(p_ref)

    def rows_of(n):
        return pl.BlockSpec((tm, n), lambda i: (i, 0))

    def whole(shape):
        return pl.BlockSpec(shape, lambda i: (0, 0))

    def prev8(n):
        return pl.BlockSpec((8, n), lambda i: (jnp.maximum(i * hb - 1, 0), 0))

    def next8(n):
        return pl.BlockSpec((8, n), lambda i: (jnp.minimum((i + 1) * hb, last_blk), 0))

    return _hosted_call(
        body, name="mix_mid_bwd", grid=(n_i,),
        in_specs=[rows_of(Z_COLS), prev8(Z_COLS), next8(Z_COLS), rows_of(CONV_W), next8(CONV_W),
                  whole((8, CONV_W)), whole((1, Q_LORA)), whole((1, KV_LORA)),
                  whole((HEADS * HEAD_SLOT, Q_LORA)), whole((KV_LORA, 2 * HEADS * 128)),
                  rows_of(128), rows_of(128),
                  rows_of(HEADS * HEAD_SLOT), rows_of(HEADS * HEAD_SLOT), rows_of(HEADS * V_HEAD)],
        out_specs=[rows_of(Z_COLS), rows_of(HEADS * HEAD_SLOT), rows_of(2 * HEADS * 128), whole((8, MID_SUMS))],
        out_shape=[jax.ShapeDtypeStruct((S, Z_COLS), BF16), jax.ShapeDtypeStruct((S, HEADS * HEAD_SLOT), BF16),
                   jax.ShapeDtypeStruct((S, 2 * HEADS * 128), BF16), jax.ShapeDtypeStruct((8, MID_SUMS), F32)],
        scratch_shapes=[], semantics=("arbitrary",),
        args=(z, z, z, dya, dya, conv_w, gq, gkv, wuq, wukv, cs, sn, dq, dk, dv), comm=comm)


ADA_Q = N_MOD * D // N_CHIPS
ADA_TN = 768


def _ada_forward(c_all, ada_w_q, ada_b_q):
    def body(c_ref, w_ref, b_ref, o_ref):
        cv = c_ref[...]
        sc = (cv * jax.nn.sigmoid(cv)).astype(BF16)
        o_ref[...] = jnp.dot(sc, w_ref[...].astype(BF16), preferred_element_type=F32) + b_ref[...]

    return pl.pallas_call(
        body, name="ada_forward", grid=(ADA_Q // ADA_TN,),
        in_specs=[pl.BlockSpec((16, D), lambda j: (0, 0)), pl.BlockSpec((D, ADA_TN), lambda j: (0, j)),
                  pl.BlockSpec((1, ADA_TN), lambda j: (0, j))],
        out_specs=pl.BlockSpec((16, ADA_TN), lambda j: (0, j)),
        out_shape=jax.ShapeDtypeStruct((16, ADA_Q), F32),
        compiler_params=_params(("parallel",)),
    )(c_all, ada_w_q, ada_b_q)


def _ada_wgrad(c_all, dmod_q):
    def body(c_ref, d_ref, o_ref):
        cv = c_ref[...]
        sc = (cv * jax.nn.sigmoid(cv)).astype(BF16)
        o_ref[...] = lax.dot_general(sc, d_ref[...].astype(BF16), (((0,), (0,)), ((), ())),
                                     preferred_element_type=F32)

    return pl.pallas_call(
        body, name="ada_wgrad", grid=(ADA_Q // ADA_TN,),
        in_specs=[pl.BlockSpec((16, D), lambda j: (0, 0)), pl.BlockSpec((16, ADA_TN), lambda j: (0, j))],
        out_specs=pl.BlockSpec((D, ADA_TN), lambda j: (0, j)),
        out_shape=jax.ShapeDtypeStruct((D, ADA_Q), F32),
        compiler_params=_params(("parallel",)),
    )(c_all, dmod_q)


def _sum_devices(parts):
    n = parts.shape[1]

    def body(p_ref, o_ref):
        o_ref[...] = jnp.broadcast_to(jnp.sum(p_ref[...], axis=0, keepdims=True), o_ref.shape)

    return pl.pallas_call(
        body, name="sum_devices",
        in_specs=[pl.BlockSpec((N_DEV, n), lambda: (0, 0))], out_specs=pl.BlockSpec((N_DEV, n), lambda: (0, 0)),
        out_shape=jax.ShapeDtypeStruct((N_DEV, n), F32),
    )(parts)


def _adamw(ws, gs, ms, vs, *, name, comm=None):
    n = len(ws)
    _, rows, cols = ws[0].shape
    tr = _row_tile(rows, 256)

    def body(*refs):
        ins, outs = refs[:4 * n], refs[4 * n:]
        for k in range(n):
            w_ref, g_ref, m_ref, v_ref = ins[k], ins[n + k], ins[2 * n + k], ins[3 * n + k]
            go_ref, d_ref, mo_ref, vo_ref = outs[4 * k:4 * k + 4]
            gv = g_ref[...]
            mn = B1 * m_ref[0] + (1.0 - B1) * gv
            vn = B2 * v_ref[0] + (1.0 - B2) * (gv * gv)
            m_hat = mn / (1.0 - B1 ** STEP)
            v_hat = vn / (1.0 - B2 ** STEP)
            go_ref[0] = gv
            d_ref[0] = -LR * (m_hat / (jnp.sqrt(v_hat) + AEPS) + WD * w_ref[0])
            mo_ref[0] = mn
            vo_ref[0] = vn

    blk = pl.BlockSpec((1, tr, cols), lambda i: (0, i, 0))
    gblk = pl.BlockSpec((tr, cols), lambda i: (i, 0))
    res, couts = _hosted_call(
        body, name=name, grid=(rows // tr,),
        in_specs=[blk] * n + [gblk] * n + [blk] * 2 * n, out_specs=[blk] * 4 * n,
        out_shape=[jax.ShapeDtypeStruct((1, rows, cols), F32)] * 4 * n,
        scratch_shapes=[], semantics=("parallel",), args=(*ws, *gs, *ms, *vs), comm=comm)
    return [res[4 * k:4 * k + 4] for k in range(n)], couts


def _small_allgather(v, *, name):
    m, n = v.shape

    def body(x_ref, out_ref, send_sems, recv_sems, local_sem):
        x, y, c = _place()
        me, sibling = (x, y, c), (x, y, 1 - c)
        chips = [(1 - x, y), (x, 1 - y), (1 - x, 1 - y)]

        def rows(px, py, pc):
            return out_ref.at[pl.ds((4 * px + 2 * py + pc) * m, m), :]

        def copy(k, block, to, src=None):
            return pltpu.make_async_remote_copy(
                src_ref=rows(*block) if src is None else src, dst_ref=rows(*block),
                send_sem=send_sems.at[k], recv_sem=recv_sems.at[k], device_id=to, device_id_type=MESH)

        mine = pltpu.make_async_copy(x_ref, rows(*me), local_sem)
        mine.start()
        first = [copy(0, me, sibling, src=x_ref)]
        first += [copy(1 + j, me, (*chip, c), src=x_ref) for j, chip in enumerate(chips)]
        for cp in first:
            cp.start()
        passed = [copy(4 + j, (*chip, c), sibling) for j, chip in enumerate(chips)]
        for j, chip in enumerate(chips):
            copy(1 + j, (*chip, c), me).wait_recv()
            passed[j].start()
        copy(0, sibling, me).wait_recv()
        for j, chip in enumerate(chips):
            copy(4 + j, (*chip, 1 - c), me).wait_recv()
        for cp in first + passed:
            cp.wait_send()
        mine.wait()

    return pl.pallas_call(
        body, name=name,
        out_shape=jax.ShapeDtypeStruct((N_DEV * m, n), v.dtype),
        in_specs=[pl.BlockSpec(memory_space=pltpu.VMEM)], out_specs=pl.BlockSpec(memory_space=pltpu.VMEM),
        scratch_shapes=[pltpu.SemaphoreType.DMA((7,)), pltpu.SemaphoreType.DMA((7,)), pltpu.SemaphoreType.DMA],
    )(v)


ADD_BLOCKS = 2


def _pair_add(place, gs, ts, *, name):
    n_a = len(gs)

    def body(pl_ref, *refs):
        g_refs, t_refs = refs[:n_a], refs[n_a:2 * n_a]
        pf_refs, pb_refs = refs[2 * n_a:3 * n_a], refs[3 * n_a:]
        own = pl.program_id(1) == pl_ref[1]
        for g_ref, t_ref, pf_ref, pb_ref in zip(g_refs, t_refs, pf_refs, pb_refs):
            s = g_ref[...] + t_ref[...]
            pb_ref[...] = s.astype(BF16)

            @pl.when(own)
            def _():
                pf_ref[...] = s[0]

    def blk(t, own_half):
        tr = t.shape[1] // ADD_BLOCKS
        if own_half:
            return pl.BlockSpec((1, tr, t.shape[2]), lambda r, q, p: (q, p[0] * ADD_BLOCKS + r, 0))
        return pl.BlockSpec((1, tr, t.shape[2]), lambda r, q, p: (q, r, 0))

    def own_blk(t):
        return pl.BlockSpec((t.shape[1] // ADD_BLOCKS, t.shape[2]), lambda r, q, p: (r, 0))

    grid_spec = pltpu.PrefetchScalarGridSpec(
        num_scalar_prefetch=1, grid=(ADD_BLOCKS, N_CHIPS),
        in_specs=[blk(t, True) for t in ts] + [blk(t, False) for t in ts],
        out_specs=[own_blk(t) for t in ts] + [blk(t, False) for t in ts])
    res = pl.pallas_call(
        body, name=name, grid_spec=grid_spec,
        out_shape=([jax.ShapeDtypeStruct(t.shape[1:], F32) for t in ts]
                   + [jax.ShapeDtypeStruct(t.shape, BF16) for t in ts]),
        compiler_params=_params(("parallel", "arbitrary")),
    )(place, *gs, *ts)
    return list(res[:n_a]), list(res[n_a:])


def _chip_add(place, pfs, ts, *, name):
    n_a = len(pfs)

    def body(pl_ref, *refs):
        pf_refs, t_refs, o_refs = refs[:n_a], refs[n_a:4 * n_a], refs[4 * n_a:]
        for i, (pf_ref, o_ref) in enumerate(zip(pf_refs, o_refs)):
            t1, t2, t3 = t_refs[3 * i:3 * i + 3]
            o_ref[...] = ((pf_ref[...] + t1[0].astype(F32)) + t2[0].astype(F32)) + t3[0].astype(F32)

    def slot(t, j):
        return pl.BlockSpec((1, t.shape[1] // ADD_BLOCKS, t.shape[2]), lambda r, p: (p[1] ^ j, r, 0))

    def half(t):
        return pl.BlockSpec((t.shape[1] // ADD_BLOCKS, t.shape[2]), lambda r, p: (r, 0))

    grid_spec = pltpu.PrefetchScalarGridSpec(
        num_scalar_prefetch=1, grid=(ADD_BLOCKS,),
        in_specs=[half(t) for t in ts] + [slot(t, j) for t in ts for j in (1, 2, 3)],
        out_specs=[half(t) for t in ts])
    res = pl.pallas_call(
        body, name=name, grid_spec=grid_spec,
        out_shape=[jax.ShapeDtypeStruct(t.shape[1:], F32) for t in ts],
        compiler_params=_params(("parallel",)),
    )(place, *pfs, *[t for t in ts for _ in range(3)])
    return list(res)


BULK = [("ffn1_w1", "colsT"), ("ffn1_w3", "colsT"), ("ffn1_w2", "rows"), ("w_in", "cols"), ("w_uq", "colsT"),
        ("w_ukv", "cols"), ("w_out", "rows"), ("ffn2_w1", "colsT"), ("ffn2_w3", "colsT"), ("ffn2_w2", "rows")]
KIND = dict(BULK)


def _group(*names):
    return [b for b in BULK if b[0] in names]


W_FIRST = _group("ffn1_w1", "ffn1_w3")
W_REST = [b for b in BULK if b not in W_FIRST]
W_MIX = _group("ffn1_w2", "w_in", "w_uq", "w_ukv", "w_out")
W_FFN2 = _group("ffn2_w1", "ffn2_w3", "ffn2_w2")
G_FFN2 = _group("ffn2_w1", "ffn2_w3", "ffn2_w2")
G_MIX = _group("w_in", "w_uq", "w_ukv", "w_out")
G_FFN1 = _group("ffn1_w1", "ffn1_w3", "ffn1_w2")


def _gathered_weights(specs, shards, got, myq):
    out = {}
    for (name, kind), part in zip(specs, got):
        part = lax.dynamic_update_slice_in_dim(part, shards[name][None], myq, axis=0)
        out[name] = _full_weight(part, kind)
    return out


def _working_shard(w, kind):
    return jnp.swapaxes(w, 1, 2)[0] if kind == "colsT" else w[0]


def _full_weight(parts, kind):
    if kind == "cols":
        return jnp.transpose(parts, (1, 0, 2)).reshape(parts.shape[1], -1)
    return parts.reshape(-1, parts.shape[2])


def _quarters(g, kind):
    if kind == "cols":
        k, n = g.shape
        return jnp.transpose(g.reshape(k, N_CHIPS, n // N_CHIPS), (1, 0, 2))
    return g.reshape(N_CHIPS, g.shape[0] // N_CHIPS, g.shape[1])


def _pad_heads(w_uq_t):
    w = w_uq_t.reshape(HEADS, QK_NOPE + QK_ROPE, Q_LORA)
    return jnp.pad(w, ((0, 0), (0, HEAD_SLOT - QK_NOPE - QK_ROPE), (0, 0))).reshape(HEADS * HEAD_SLOT, Q_LORA)


def _unpad_heads(g):
    return g.reshape(HEADS, HEAD_SLOT, Q_LORA)[:, :QK_NOPE + QK_ROPE].reshape(HEADS * (QK_NOPE + QK_ROPE), Q_LORA)


def _split_kv(w_ukv):
    return jnp.transpose(w_ukv.reshape(KV_LORA, HEADS, 2, 128), (0, 2, 1, 3)).reshape(KV_LORA, 2 * HEADS * 128)


def _merge_kv(g):
    return jnp.transpose(g.reshape(KV_LORA, 2, HEADS, 128), (0, 2, 1, 3)).reshape(KV_LORA, 2 * HEADS * 128)


def _rope_tables(positions):
    inv_freq = ROPE_THETA ** (-jnp.arange(0, QK_ROPE, 2, dtype=F32) / QK_ROPE)
    ang = positions.astype(F32)[:, None] * inv_freq
    cos, sin, zero = jnp.cos(ang), jnp.sin(ang), jnp.zeros((positions.shape[0], 64), F32)
    return jnp.concatenate([cos, cos, zero], axis=1), jnp.concatenate([sin, sin, zero], axis=1)


def _assemble(place, specs, rhs, others):
    south = place[0] == 0
    return {b[0]: jnp.concatenate([jnp.where(south, rh, ot), jnp.where(south, ot, rh)], axis=0)
            for b, rh, ot in zip(specs, rhs, others)}


def _local_step(x, positions, target, mod, vec, conv_w, w_first, rest_shards, place, tail_host=None):
    row = lambda k: mod[k:k + 1]
    sh1, sc1, g1, sh2, sc2, g2, sh3, sc3, g3 = [row(k) for k in range(N_MOD)]
    cs, sn = _rope_tables(positions)
    cw8 = jnp.pad(conv_w, ((0, 5), (0, 0)))
    ga, gb = _group_mats()
    dist = place is not None
    comm = lambda prog: prog if dist else None

    w = dict(w_first) if dist else {**rest_shards, **w_first}

    def gather(specs):
        return _Gather([rest_shards[b[0]] for b in specs]) if dist else None

    def arrived(specs, got):
        if dist:
            w.update(_gathered_weights(specs, rest_shards, got, place[1]))

    (h1, a1, b1, u1), got = _ffn_up(x, vec["norm_ffn1_g"], sh1, sc1, w["ffn1_w1"], w["ffn1_w3"],
                                    name="ffn1_up", comm=gather(W_MIX))
    arrived(W_MIX, got)
    w_in = jnp.pad(w["w_in"].T, ((0, Z_COLS - IN_COLS), (0, 0)))
    wuq = _pad_heads(w["w_uq"])
    wukv = _split_kv(w["w_ukv"])
    (x1, f1), _ = _ffn_down(u1, w["ffn1_w2"], x, g1, name="ffn1_down")
    (h2, z), _ = _mix_in(x1, vec["norm_mix_g"], sh2, sc2, w_in)
    (ya, q, k, v, cqn, ckvn), _ = _mix_mid(z, cw8, vec["q_norm_g"], vec["kv_norm_g"], wuq, wukv, cs, sn)
    (o, lse), got = _attention(q, k, v, comm=gather(W_FFN2))
    arrived(W_FFN2, got)
    x2, yn, yo = _mix_out(ya, o, vec["out_norm_g"], w["w_out"], x1, g2, ga, gb)
    (h3, a3, b3, u3), _ = _ffn_up(x2, vec["norm_ffn2_g"], sh3, sc3, w["ffn2_w1"], w["ffn2_w3"], name="ffn2_up")
    dx3, f3, dgfin, loss_blk = _ffn_down_loss(u3, w["ffn2_w2"], x2, g3, vec["final_norm_g"], target)

    grads, reduced = {}, {}

    def tn(a, b, tm, tn_, name, prog=None):
        if prog is None:
            return _tn_matmul(a, b, tm=tm, tn=tn_, name=name), None
        return _tn_matmul(a, b, tm=tm, tn=tn_, name=name, comm=prog)

    def slab_of(specs):
        return [_quarters(grads[n], kind) for n, kind in specs]

    (df3, da3, db3, dg3), _ = _ffn_bwd_du(dx3, g3, f3, w["ffn2_w2"], a3, b3, name="ffn2_bwd_du")
    grads["ffn2_w2"], _ = tn(u3, df3, FF // 2, D, "ffn2_dw2")
    grads["ffn2_w1"], _ = tn(da3, h3, FF // 2, D, "ffn2_dw1")
    grads["ffn2_w3"], _ = tn(db3, h3, FF // 2, D, "ffn2_dw3")
    (dx2, s3), _ = _dh_normbwd([(da3, w["ffn2_w1"]), (db3, w["ffn2_w3"])], x2, vec["norm_ffn2_g"], sc3, dx3,
                               name="ffn2_bwd_dh")

    p1 = slab_of(G_FFN2) if dist else None
    (dyo, dya, do, delta, s_out), t1 = _mix_out_bwd(dx2, g2, yo, w["w_out"], ya, o, vec["out_norm_g"], ga, gb,
                                                    comm=comm(_PairExchange(p1) if dist else None))
    grads["w_out"], _ = tn(yn, dyo, D, D, "dw_out")
    if dist:
        pf1, pb1 = _pair_add(place, p1, t1, name="ffn2g_pair_add")
    dq, dk, dv = _attention_bwd(q, k, v, do, lse, delta)
    (dz, dqf, dkvf, s_mid), t2 = _mix_mid_bwd(z, dya, cw8, vec["q_norm_g"], vec["kv_norm_g"], wuq, wukv, cs, sn,
                                              dq, dk, dv, comm=comm(_ChipExchange(pb1) if dist else None))
    g_uq, _ = tn(dqf, cqn, HEADS * HEAD_SLOT, Q_LORA, "dw_uq")
    g_ukv, _ = tn(ckvn, dkvf, KV_LORA, 2 * HEADS * 128, "dw_ukv")
    grads["w_uq"], grads["w_ukv"] = _unpad_heads(g_uq), _merge_kv(g_ukv)
    g_in, _ = tn(h2, dz, D, Z_COLS, "dw_in")
    grads["w_in"] = g_in[:, :IN_COLS]
    (dx1, s2), _ = _dh_normbwd([(dz, w_in)], x1, vec["norm_mix_g"], sc2, dx2, name="mix_bwd_dh")

    p2 = slab_of(G_MIX) if dist else None
    rh_ffn2 = _chip_add(place, pf1, t2, name="ffn2g_chip_add") if dist else None
    (df1, da1, db1, dg1), got = _ffn_bwd_du(dx1, g1, f1, w["ffn1_w2"], a1, b1, name="ffn1_bwd_du",
                                            comm=comm(_Multi([_PairExchange(p2), _PairShare(rh_ffn2)]) if dist else None))
    if dist:
        t1 = got[:len(p2)]
        reduced.update(_assemble(place, G_FFN2, rh_ffn2, got[len(p2):]))
    dh_pairs = [(da1, w["ffn1_w1"]), (db1, w["ffn1_w3"])]
    if dist:
        pf2, pb2 = _pair_add(place, p2, t1, name="mixg_pair_add")
        grads["ffn1_w2"], t2 = tn(u1, df1, FF // 2, D, "ffn1_dw2", _ChipExchange(pb2))
        rh_mix = _chip_add(place, pf2, t2, name="mixg_chip_add")
        q_w2 = [_quarters(grads["ffn1_w2"], KIND["ffn1_w2"])]
        grads["ffn1_w1"], got = tn(da1, h1, FF // 2, D, "ffn1_dw1", _Multi([_PairShare(rh_mix), _PairExchange(q_w2)]))
        reduced.update(_assemble(place, G_MIX, rh_mix, got[:len(rh_mix)]))
        pf_w2, pb_w2 = _pair_add(place, q_w2, got[len(rh_mix):], name="ffn1w2_pair_add")
        q_w1 = [_quarters(grads["ffn1_w1"], KIND["ffn1_w1"])]
        grads["ffn1_w3"], got = tn(db1, h1, FF // 2, D, "ffn1_dw3", _Multi([_ChipExchange(pb_w2), _PairExchange(q_w1)]))
        t2_w2 = got[:1]
        pf_w1, pb_w1 = _pair_add(place, q_w1, got[1:], name="ffn1w1_pair_add")
        q_w3 = [_quarters(grads["ffn1_w3"], KIND["ffn1_w3"])]
        (dx0, s1), got = _dh_normbwd(dh_pairs, x, vec["norm_ffn1_g"], sc1, dx1, name="ffn1_bwd_dh",
                                     comm=_Multi([_ChipExchange(pb_w1), _PairExchange(q_w3)]))
        t2_w1 = got[:1]
        pf_w3, pb_w3 = _pair_add(place, q_w3, got[1:], name="ffn1w3_pair_add")
        t2_w3 = tail_host(_ChipExchange(pb_w3), reduced)
        rh = _chip_add(place, pf_w1 + pf_w3 + pf_w2, t2_w1 + t2_w3 + t2_w2, name="ffn1g_chip_add")
        reduced.update(_assemble(place, G_FFN1, rh, _run_comm(_PairShare(rh), name="ffn1g_pair_share")))
    else:
        grads["ffn1_w2"], _ = tn(u1, df1, FF // 2, D, "ffn1_dw2")
        grads["ffn1_w1"], _ = tn(da1, h1, FF // 2, D, "ffn1_dw1")
        grads["ffn1_w3"], _ = tn(db1, h1, FF // 2, D, "ffn1_dw3")
        (dx0, s1), _ = _dh_normbwd(dh_pairs, x, vec["norm_ffn1_g"], sc1, dx1, name="ffn1_bwd_dh")
        reduced = grads

    def part(s, k):
        return s[0:1, k * D:(k + 1) * D]

    dmod = jnp.concatenate([part(s1, 1), part(s1, 0), dg1[0:1], part(s2, 1), part(s2, 0), part(s_out, 0),
                            part(s3, 1), part(s3, 0), dg3[0:1]], axis=1)
    small = {"norm_ffn1_g": part(s1, 2), "norm_mix_g": part(s2, 2), "out_norm_g": part(s_out, 1),
             "norm_ffn2_g": part(s3, 2), "final_norm_g": dgfin[0:1],
             "q_norm_g": s_mid[0:1, 3 * CONV_W:3 * CONV_W + Q_LORA],
             "kv_norm_g": s_mid[0:1, 3 * CONV_W + Q_LORA:MID_SUMS], "conv_w": s_mid[0:1, 0:3 * CONV_W]}
    return loss_blk, dx0, reduced, dmod, small


SMALL = [("norm_ffn1_g", D), ("norm_mix_g", D), ("out_norm_g", D), ("norm_ffn2_g", D), ("final_norm_g", D),
         ("q_norm_g", Q_LORA), ("kv_norm_g", KV_LORA), ("conv_w", 3 * CONV_W)]
WEIGHTS = ['ada_w', 'ada_b', 'norm_ffn1_g', 'ffn1_w1', 'ffn1_w3', 'ffn1_w2', 'norm_mix_g', 'w_in', 'conv_w',
           'q_norm_g', 'w_uq', 'kv_norm_g', 'w_ukv', 'out_norm_g', 'w_out', 'norm_ffn2_g', 'ffn2_w1', 'ffn2_w3',
           'ffn2_w2', 'final_norm_g']


def kernel(x, c, positions, ada_w, ada_b, norm_ffn1_g, ffn1_w1, ffn1_w3, ffn1_w2, norm_mix_g, w_in, conv_w, q_norm_g, w_uq, kv_norm_g, w_ukv, out_norm_g, w_out, norm_ffn2_g, ffn2_w1, ffn2_w3, ffn2_w2, final_norm_g, loss_target, m_ada_w, m_ada_b, m_norm_ffn1_g, m_ffn1_w1, m_ffn1_w3, m_ffn1_w2, m_norm_mix_g, m_w_in, m_conv_w, m_q_norm_g, m_w_uq, m_kv_norm_g, m_w_ukv, m_out_norm_g, m_w_out, m_norm_ffn2_g, m_ffn2_w1, m_ffn2_w3, m_ffn2_w2, m_final_norm_g, v_ada_w, v_ada_b, v_norm_ffn1_g, v_ffn1_w1, v_ffn1_w3, v_ffn1_w2, v_norm_mix_g, v_w_in, v_conv_w, v_q_norm_g, v_w_uq, v_kv_norm_g, v_w_ukv, v_out_norm_g, v_w_out, v_norm_ffn2_g, v_ffn2_w1, v_ffn2_w3, v_ffn2_w2, v_final_norm_g):
    args = dict(locals())
    wts = {n: args[n] for n in WEIGHTS}
    mom = {n: args["m_" + n] for n in WEIGHTS}
    var = {n: args["v_" + n] for n in WEIGHTS}
    ax, ay, ac = _place()
    myq = 2 * ax + ay
    me = 2 * myq + ac
    place = jnp.stack([ac, myq]).astype(jnp.int32)

    shards = {name: _working_shard(wts[name], kind).astype(BF16) for name, kind in BULK}
    first = _run_comm(_Gather([shards[b[0]] for b in W_FIRST]), name="gather_ffn1")
    w_first = _gathered_weights(W_FIRST, shards, first, myq)

    mine = jnp.concatenate([c, conv_w[0].reshape(1, 3 * CONV_W // N_CHIPS)], axis=1)
    seen = _small_allgather(jnp.pad(mine, ((0, 7), (0, 0))), name="gather_cond").reshape(N_DEV, 8, -1)[:, 0]
    c_all = jnp.pad(seen[:, :D], ((0, 8), (0, 0)))
    conv_full = jnp.transpose(seen[0::2, D:].reshape(N_CHIPS, 3, CONV_W // N_CHIPS), (1, 0, 2)).reshape(3, CONV_W)
    ada_b_q = lax.dynamic_slice_in_dim(ada_b, myq * ADA_Q, ADA_Q, axis=1)
    mod_q = _ada_forward(c_all, ada_w[0], ada_b_q)
    mod_all = _small_allgather(mod_q, name="gather_mod").reshape(N_DEV, 16, ADA_Q)
    mod_rows = jnp.transpose(mod_all[0::2, :N_DEV], (1, 0, 2)).reshape(N_DEV, N_MOD * D)
    mod = lax.dynamic_slice_in_dim(mod_rows, me, 1, axis=0).reshape(N_MOD, D)

    vec = {n: wts[n] for n in ("norm_ffn1_g", "norm_mix_g", "q_norm_g", "kv_norm_g", "out_norm_g", "norm_ffn2_g")}
    vec["final_norm_g"] = final_norm_g.reshape(1, D)
    g, delta, new_m, new_v = {}, {}, {}, {}

    def adam(names, grads, comm=None):
        views = [(lambda a: jnp.swapaxes(a, 1, 2)) if KIND.get(n) == "colsT" else (lambda a: a) for n in names]
        res, couts = _adamw([vw(wts[n]) for n, vw in zip(names, views)], [grads[n] for n in names],
                            [vw(mom[n]) for n, vw in zip(names, views)], [vw(var[n]) for n, vw in zip(names, views)],
                            name="adamw_" + names[0], comm=comm)
        for n, vw, r in zip(names, views, res):
            g[n], delta[n], new_m[n], new_v[n] = [vw(a) for a in r]
        return couts

    loss_blk, grad_x, gq, dmod, small = _local_step(
        x[0], positions[0], loss_target[0], mod, vec, conv_full, w_first, {b[0]: shards[b[0]] for b in W_REST}, place,
        tail_host=lambda prog, grads: adam([b[0] for b in G_FFN2], grads, prog))
    loss = lax.psum(loss_blk[0, 0], ("x", "y", "c"))

    rows = jnp.concatenate([dmod] + [small[n] for n, _ in SMALL], axis=1)
    width = rows.shape[1]
    fold = -(-width // (8 * 128)) * 128
    rows = jnp.pad(rows, ((0, 0), (0, 8 * fold - width))).reshape(8, fold)
    every = _small_allgather(rows, name="gather_small").reshape(N_DEV, 8 * fold)[:, :width]
    total = _sum_devices(every)[0:1]
    dmod_q = lax.dynamic_slice_in_dim(every[:, :N_MOD * D], myq * ADA_Q, ADA_Q, axis=1)
    sg = {name: gq[name] for name, *_ in BULK}
    sg["ada_w"] = _ada_wgrad(c_all, jnp.pad(dmod_q, ((0, 8), (0, 0))))
    sg["ada_b"] = total[:, :N_MOD * D]
    off = N_MOD * D
    for n, width in SMALL:
        sg[n] = total[:, off:off + width]
        off += width
    sg["conv_w"] = lax.dynamic_slice_in_dim(sg["conv_w"].reshape(3, CONV_W), myq * (CONV_W // N_CHIPS),
                                            CONV_W // N_CHIPS, axis=1)

    for name in ["ada_w"] + [b[0] for b in BULK if b not in G_FFN2]:
        adam([name], sg)
    smalls = ["ada_b"] + [n for n, _ in SMALL]

    def packed(d):
        flat = jnp.concatenate([d[n].reshape(1, -1) for n in smalls], axis=1)
        return jnp.pad(flat.reshape(-1, D), ((0, 1), (0, 0)))

    res = _adamw([packed(wts)[None]], [packed(sg)], [packed(mom)[None]], [packed(var)[None]],
                 name="adamw_small")[0][0][1:]
    off = 0
    for n in smalls:
        size = wts[n].size
        for d, r in zip((delta, new_m, new_v), res):
            d[n] = r.reshape(-1)[off:off + size].reshape(wts[n].shape)
        g[n] = sg[n].reshape(wts[n].shape)
        off += size

    return (loss, grad_x[None], *[g[n] for n in WEIGHTS], *[delta[n] for n in WEIGHTS],
            *[new_m[n] for n in WEIGHTS], *[new_v[n] for n in WEIGHTS])
```

```python
import numpy as np
import jax
import jax.numpy as jnp
from jax import lax
from jax.experimental import pallas as pl
from jax.experimental.pallas import tpu as pltpu

F32 = jnp.float32
BF16 = jnp.bfloat16
MESH = pl.DeviceIdType.MESH

D = 1024
FF = 2816
CONV_W = 512
CONV_GROUP = 64
HEADS = 4
QK_NOPE = 128
QK_ROPE = 64
V_HEAD = 128
Q_LORA = 384
KV_LORA = 256
HEAD_SLOT = 256
IN_COLS = 3 * CONV_W + Q_LORA + KV_LORA + QK_ROPE
Z_COLS = 2304
EPS = 1e-6
ROPE_THETA = 10000.0
CHUNK = 64
ATT_SCALE = (QK_NOPE + QK_ROPE) ** -0.5
NEG = -1e30
EXP2_SCALE = ATT_SCALE * 1.4426950408889634
N_MOD = 9

LR, B1, B2, AEPS, WD, STEP = 0.001, 0.9, 0.999, 1e-08, 0.01, 10

N_CHIPS = 4
N_DEV = 8
VMEM_LIMIT = 56 << 20


def _params(sem, vmem=VMEM_LIMIT):
    return pltpu.CompilerParams(dimension_semantics=sem, vmem_limit_bytes=vmem)


def _rms(v):
    return lax.rsqrt(jnp.mean(v * v, axis=-1, keepdims=True) + EPS)


def _rsum8(v):
    t, n = v.shape
    return jnp.sum(v.reshape(t // 8, 8, n), axis=0)


def _all_rows(ref):
    ref[...] = jnp.broadcast_to(jnp.sum(ref[...], axis=0, keepdims=True), ref.shape)


def _gsum(v, gmat, split=False):
    hi = v.astype(BF16)
    out = jnp.dot(hi, gmat, preferred_element_type=F32)
    if split:
        out = out + jnp.dot((v - hi.astype(F32)).astype(BF16), gmat, preferred_element_type=F32)
    return out


def _dot_nt(a, b):
    return lax.dot_general(a, b, (((1,), (1,)), ((), ())), preferred_element_type=F32)


def _silu_parts(a):
    sg = jax.nn.sigmoid(a)
    return sg, a * sg


def _rope(xr, cs, sn, lane):
    rh = jnp.where(lane < 32, -pltpu.roll(xr, 96, 1), pltpu.roll(xr, 32, 1))
    return xr * cs + rh * sn


def _rope_t(g, cs, sn, lane):
    y = g * sn
    rt = jnp.where(lane < 32, pltpu.roll(y, 96, 1), jnp.where(lane < 64, -pltpu.roll(y, 32, 1), 0.0))
    return g * cs + rt


def _row_tile(rows, pref, mult=8):
    t = min(rows, pref) // mult * mult
    while rows % t:
        t -= mult
    return t


def _place():
    return lax.axis_index("x"), lax.axis_index("y"), lax.axis_index("c")


ANY = pl.BlockSpec(memory_space=pl.ANY)


def _hosted_call(body, *, name, grid, in_specs, out_specs, out_shape, scratch_shapes, semantics, args, comm=None,
                 prefetch=()):
    n_in, n_out, n_scr, n_pf = len(in_specs), len(out_specs), len(scratch_shapes), len(prefetch)

    def call(fn, in_specs, out_specs, out_shape, scratch_shapes, semantics, operands):
        spec = pltpu.PrefetchScalarGridSpec(num_scalar_prefetch=n_pf, grid=grid, in_specs=list(in_specs),
                                            out_specs=list(out_specs), scratch_shapes=list(scratch_shapes))
        return pl.pallas_call(fn, name=name, grid_spec=spec, out_shape=list(out_shape),
                              compiler_params=_params(semantics))(*prefetch, *operands)

    if comm is None:
        return list(call(body, in_specs, out_specs, out_shape, scratch_shapes, semantics, args)), []
    n_ci, n_co = len(comm.inputs), len(comm.out_shapes)
    total = int(np.prod(grid))

    def hosted(*refs):
        tables, refs = refs[:n_pf], refs[n_pf:]
        ins, refs = refs[:n_in], refs[n_in:]
        cins, refs = refs[:n_ci], refs[n_ci:]
        outs, refs = refs[:n_out], refs[n_out:]
        couts, refs = refs[:n_co], refs[n_co:]
        scratch, sems = refs[:n_scr], refs[n_scr]
        step = pl.program_id(0)
        for ax in range(1, len(grid)):
            step = step * grid[ax] + pl.program_id(ax)

        @pl.when(step == 0)
        def _():
            comm.start(cins, couts, sems)

        body(*tables, *ins, *outs, *scratch)

        @pl.when(step == total - 1)
        def _():
            comm.finish(cins, couts, sems)

    res = call(hosted, list(in_specs) + [ANY] * n_ci, list(out_specs) + [ANY] * n_co,
               list(out_shape) + list(comm.out_shapes),
               list(scratch_shapes) + [pltpu.SemaphoreType.DMA((comm.n_sems,))],
               ("arbitrary",) * len(grid), (*args, *comm.inputs))
    return list(res[:n_out]), list(res[n_out:])


def _run_comm(comm, *, name):
    n_ci = len(comm.inputs)

    def body(*refs):
        cins, couts, sems = refs[:n_ci], refs[n_ci:-1], refs[-1]
        comm.start(cins, couts, sems)
        comm.finish(cins, couts, sems)

    return list(pl.pallas_call(
        body, name=name, out_shape=list(comm.out_shapes), in_specs=[ANY] * n_ci,
        out_specs=[ANY] * len(comm.out_shapes), scratch_shapes=[pltpu.SemaphoreType.DMA((comm.n_sems,))],
    )(*comm.inputs))


class _Gather:
    def __init__(self, slabs):
        self.inputs = list(slabs)
        self.out_shapes = [jax.ShapeDtypeStruct((N_CHIPS,) + s.shape, s.dtype) for s in slabs]
        self.n_sems = 12 * len(slabs)

    @staticmethod
    def _copy(out, sems, base, k, chip, hc, to, src=None):
        H = out.shape[1] // 2
        half = out.at[2 * chip[0] + chip[1], pl.ds(hc * H, H), :]
        return pltpu.make_async_remote_copy(
            src_ref=half if src is None else src, dst_ref=half, send_sem=sems.at[base + k],
            recv_sem=sems.at[base + 6 + k], device_id=to, device_id_type=MESH)

    def _firsts(self, src, out, sems, base):
        x, y, c = _place()
        H = src.shape[0] // 2
        chips = [(1 - x, y), (x, 1 - y), (1 - x, 1 - y)]
        return [self._copy(out, sems, base, j, (x, y), c, (*chip, c), src=src.at[pl.ds(c * H, H), :])
                for j, chip in enumerate(chips)]

    def start(self, ins, outs, sems):
        for i, (src, out) in enumerate(zip(ins, outs)):
            for cp in self._firsts(src, out, sems, 12 * i):
                cp.start()

    def finish(self, ins, outs, sems):
        x, y, c = _place()
        chips = [(1 - x, y), (x, 1 - y), (1 - x, 1 - y)]
        passed = []
        for i, out in enumerate(outs):
            for j, chip in enumerate(chips):
                self._copy(out, sems, 12 * i, j, chip, c, (x, y, c)).wait_recv()
                cp = self._copy(out, sems, 12 * i, 3 + j, chip, c, (x, y, 1 - c))
                cp.start()
                passed.append(cp)
        for i, out in enumerate(outs):
            for j, chip in enumerate(chips):
                self._copy(out, sems, 12 * i, 3 + j, chip, 1 - c, (x, y, c)).wait_recv()
        for cp in passed:
            cp.wait_send()
        for i, (src, out) in enumerate(zip(ins, outs)):
            for cp in self._firsts(src, out, sems, 12 * i):
                cp.wait_send()


class _PairExchange:
    def __init__(self, arrays):
        self.inputs = list(arrays)
        self.out_shapes = [jax.ShapeDtypeStruct((N_CHIPS, a.shape[1] // 2, a.shape[2]), a.dtype) for a in arrays]
        self.n_sems = 2 * len(arrays)

    def _copies(self, ins, outs, sems):
        x, y, c = _place()
        return [pltpu.make_async_remote_copy(
            src_ref=g.at[:, pl.ds((1 - c) * t.shape[1], t.shape[1]), :], dst_ref=t, send_sem=sems.at[2 * i],
            recv_sem=sems.at[2 * i + 1], device_id=(x, y, 1 - c), device_id_type=MESH)
            for i, (g, t) in enumerate(zip(ins, outs))]

    def start(self, ins, outs, sems):
        for cp in self._copies(ins, outs, sems):
            cp.start()

    def finish(self, ins, outs, sems):
        for cp in self._copies(ins, outs, sems):
            cp.wait()


class _ChipExchange:
    def __init__(self, arrays):
        self.inputs = list(arrays)
        self.out_shapes = [jax.ShapeDtypeStruct(a.shape, a.dtype) for a in arrays]
        self.n_sems = 6 * len(arrays)

    def _copies(self, p, t, sems, base):
        x, y, c = _place()
        myq = 2 * x + y
        chips = [(1 - x, y), (x, 1 - y), (1 - x, 1 - y)]
        sends = [pltpu.make_async_remote_copy(
            src_ref=p.at[2 * chip[0] + chip[1]], dst_ref=t.at[myq], send_sem=sems.at[base + j],
            recv_sem=sems.at[base + 3 + j], device_id=(*chip, c), device_id_type=MESH) for j, chip in enumerate(chips)]
        lands = [pltpu.make_async_remote_copy(
            src_ref=t.at[2 * chip[0] + chip[1]], dst_ref=t.at[2 * chip[0] + chip[1]], send_sem=sems.at[base + j],
            recv_sem=sems.at[base + 3 + j], device_id=(*chip, c), device_id_type=MESH) for j, chip in enumerate(chips)]
        return sends, lands

    def start(self, ins, outs, sems):
        for i, (p, t) in enumerate(zip(ins, outs)):
            for cp in self._copies(p, t, sems, 6 * i)[0]:
                cp.start()

    def finish(self, ins, outs, sems):
        for i, (p, t) in enumerate(zip(ins, outs)):
            sends, lands = self._copies(p, t, sems, 6 * i)
            for cp in lands:
                cp.wait_recv()
            for cp in sends:
                cp.wait_send()


class _SemView:
    def __init__(self, sems, base):
        self._sems, self._base = sems, base

    @property
    def at(self):
        return self

    def __getitem__(self, k):
        return self._sems.at[self._base + k]


class _Multi:
    def __init__(self, progs):
        self.progs = list(progs)
        self.inputs = [a for p in self.progs for a in p.inputs]
        self.out_shapes = [s for p in self.progs for s in p.out_shapes]
        self.n_sems = sum(p.n_sems for p in self.progs)

    def _each(self, ins, outs, sems):
        i = o = s = 0
        for p in self.progs:
            ni, no = len(p.inputs), len(p.out_shapes)
            yield p, ins[i:i + ni], outs[o:o + no], _SemView(sems, s)
            i, o, s = i + ni, o + no, s + p.n_sems

    def start(self, ins, outs, sems):
        for p, a, b, c in self._each(ins, outs, sems):
            p.start(a, b, c)

    def finish(self, ins, outs, sems):
        for p, a, b, c in self._each(ins, outs, sems):
            p.finish(a, b, c)


class _PairShare:
    def __init__(self, arrays):
        self.inputs = list(arrays)
        self.out_shapes = [jax.ShapeDtypeStruct(a.shape, a.dtype) for a in arrays]
        self.n_sems = 2 * len(arrays)

    def _copies(self, ins, outs, sems):
        x, y, c = _place()
        return [pltpu.make_async_remote_copy(
            src_ref=r, dst_ref=o, send_sem=sems.at[2 * i], recv_sem=sems.at[2 * i + 1],
            device_id=(x, y, 1 - c), device_id_type=MESH) for i, (r, o) in enumerate(zip(ins, outs))]

    def start(self, ins, outs, sems):
        for cp in self._copies(ins, outs, sems):
            cp.start()

    def finish(self, ins, outs, sems):
        for cp in self._copies(ins, outs, sems):
            cp.wait()


def _ffn_up(x, ng, sh, sc, w1, w3, *, name, comm=None):
    S = x.shape[0]
    tm, tn = _row_tile(S, 512), FF

    def body(x_ref, g_ref, sh_ref, sc_ref, w1_ref, w3_ref, h_ref, a_ref, b_ref, u_ref, hs):
        @pl.when(pl.program_id(1) == 0)
        def _():
            xv = x_ref[...]
            h = ((xv * _rms(xv)) * g_ref[...]) * (1.0 + sc_ref[...]) + sh_ref[...]
            hb = h.astype(BF16)
            hs[...] = hb
            h_ref[...] = hb

        h = hs[...]
        cols = pl.ds(pl.multiple_of(pl.program_id(1) * tn, tn), tn)
        a = _dot_nt(h, w1_ref[cols, :])
        b = _dot_nt(h, w3_ref[cols, :])
        _, sa = _silu_parts(a)
        a_ref[...] = a.astype(BF16)
        b_ref[...] = b.astype(BF16)
        u_ref[...] = (sa * b).astype(BF16)

    row = pl.BlockSpec((tm, D), lambda i, j: (i, 0))
    vec = pl.BlockSpec((1, D), lambda i, j: (0, 0))
    wsp = pl.BlockSpec((FF, D), lambda i, j: (0, 0))
    osp = pl.BlockSpec((tm, tn), lambda i, j: (i, j))
    return _hosted_call(
        body, name=name, grid=(S // tm, FF // tn),
        in_specs=[row, vec, vec, vec, wsp, wsp],
        out_specs=[row, osp, osp, osp],
        out_shape=[jax.ShapeDtypeStruct((S, D), BF16)] + [jax.ShapeDtypeStruct((S, FF), BF16)] * 3,
        scratch_shapes=[pltpu.VMEM((tm, D), BF16)],
        semantics=("parallel", "arbitrary"), args=(x, ng, sh, sc, w1, w3), comm=comm)


def _ffn_down(u, w2, x, gate, *, name, comm=None):
    S = x.shape[0]
    tm = _row_tile(S, 1024)

    def body(u_ref, w2_ref, x_ref, g_ref, xo_ref, f_ref):
        f = jnp.dot(u_ref[...], w2_ref[...], preferred_element_type=F32)
        xo_ref[...] = x_ref[...] + (0.5 * g_ref[...]) * f
        f_ref[...] = f.astype(BF16)

    return _hosted_call(
        body, name=name, grid=(S // tm,),
        in_specs=[pl.BlockSpec((tm, FF), lambda i: (i, 0)), pl.BlockSpec((FF, D), lambda i: (0, 0)),
                  pl.BlockSpec((tm, D), lambda i: (i, 0)), pl.BlockSpec((1, D), lambda i: (0, 0))],
        out_specs=[pl.BlockSpec((tm, D), lambda i: (i, 0))] * 2,
        out_shape=[jax.ShapeDtypeStruct((S, D), F32), jax.ShapeDtypeStruct((S, D), BF16)],
        scratch_shapes=[], semantics=("parallel",), args=(u, w2, x, gate), comm=comm)


def _ffn_bwd_du(dx, gate, f, w2, a, b, *, name, comm=None):
    S = dx.shape[0]
    tm, tn = _row_tile(S, 512), FF
    n_i = S // tm

    def body(dx_ref, g_ref, f_ref, w_ref, a_ref, b_ref, df_ref, da_ref, db_ref, dg_ref, dfs):
        i, j = pl.program_id(0), pl.program_id(1)

        @pl.when((i == 0) & (j == 0))
        def _():
            dg_ref[...] = jnp.zeros_like(dg_ref)

        @pl.when(j == 0)
        def _():
            dxv = dx_ref[...]
            dfb = (dxv * (0.5 * g_ref[...])).astype(BF16)
            dfs[...] = dfb
            df_ref[...] = dfb
            dg_ref[...] += _rsum8(dxv * (0.5 * f_ref[...].astype(F32)))

        du = _dot_nt(dfs[...], w_ref[pl.ds(pl.multiple_of(j * tn, tn), tn), :])
        av = a_ref[...].astype(F32)
        sg, sa = _silu_parts(av)
        da_ref[...] = (du * b_ref[...].astype(F32) * (sg * (1.0 + av * (1.0 - sg)))).astype(BF16)
        db_ref[...] = (du * sa).astype(BF16)

        @pl.when((i == n_i - 1) & (j == FF // tn - 1))
        def _():
            _all_rows(dg_ref)

    row = pl.BlockSpec((tm, D), lambda i, j: (i, 0))
    blk = pl.BlockSpec((tm, tn), lambda i, j: (i, j))
    return _hosted_call(
        body, name=name, grid=(n_i, FF // tn),
        in_specs=[row, pl.BlockSpec((1, D), lambda i, j: (0, 0)), row,
                  pl.BlockSpec((FF, D), lambda i, j: (0, 0)), blk, blk],
        out_specs=[row, blk, blk, pl.BlockSpec((8, D), lambda i, j: (0, 0))],
        out_shape=[jax.ShapeDtypeStruct((S, D), BF16), jax.ShapeDtypeStruct((S, FF), BF16),
                   jax.ShapeDtypeStruct((S, FF), BF16), jax.ShapeDtypeStruct((8, D), F32)],
        scratch_shapes=[pltpu.VMEM((tm, D), BF16)],
        semantics=("arbitrary", "arbitrary"), args=(dx, gate, f, w2, a, b), comm=comm)


def _tn_matmul(a, b, *, tm, tn, name, comm=None):
    S, M = a.shape
    N = b.shape[1]
    ts = _row_tile(S, 2048 if tm * tn <= 1408 * 1024 else 1024)
    ns = S // ts

    def body(a_ref, b_ref, o_ref):
        s = pl.program_id(2)
        p = lax.dot_general(a_ref[...], b_ref[...], (((0,), (0,)), ((), ())), preferred_element_type=F32)

        @pl.when(s == 0)
        def _():
            o_ref[...] = p

        @pl.when(s > 0)
        def _():
            o_ref[...] += p

    (out,), couts = _hosted_call(
        body, name=name, grid=(M // tm, N // tn, ns),
        in_specs=[pl.BlockSpec((ts, tm), lambda i, j, s: (s, i)), pl.BlockSpec((ts, tn), lambda i, j, s: (s, j))],
        out_specs=[pl.BlockSpec((tm, tn), lambda i, j, s: (i, j))],
        out_shape=[jax.ShapeDtypeStruct((M, N), F32)],
        scratch_shapes=[], semantics=("parallel", "parallel", "arbitrary"), args=(a, b), comm=comm)
    return out if comm is None else (out, couts)


def _dh_normbwd(pairs, x, ng, sc, dx_next, *, name, comm=None):
    S = x.shape[0]
    n_p = len(pairs)
    tm = _row_tile(S, 512)
    n_i = S // tm

    def body(*refs):
        a_refs, w_refs = refs[:n_p], refs[n_p:2 * n_p]
        x_ref, g_ref, sc_ref, dxn_ref, dx_ref, p_ref = refs[2 * n_p:]
        i = pl.program_id(0)
        dh = jnp.dot(a_refs[0][...], w_refs[0][...], preferred_element_type=F32)
        for k in range(1, n_p):
            dh = dh + jnp.dot(a_refs[k][...], w_refs[k][...], preferred_element_type=F32)
        xv = x_ref[...]
        r = _rms(xv)
        xh = xv * r
        g = g_ref[...]
        dn = dh * (1.0 + sc_ref[...])
        dy = dn * g
        dx_ref[...] = dxn_ref[...] + r * (dy - xh * jnp.mean(dy * xh, axis=-1, keepdims=True))

        @pl.when(i == 0)
        def _():
            p_ref[...] = jnp.zeros_like(p_ref)

        p_ref[:, 0:D] += _rsum8(dh * (xh * g))
        p_ref[:, D:2 * D] += _rsum8(dh)
        p_ref[:, 2 * D:3 * D] += _rsum8(dn * xh)

        @pl.when(i == n_i - 1)
        def _():
            _all_rows(p_ref)

    row = pl.BlockSpec((tm, D), lambda i: (i, 0))
    vec = pl.BlockSpec((1, D), lambda i: (0, 0))
    in_specs = ([pl.BlockSpec((tm, a.shape[1]), lambda i: (i, 0)) for a, _ in pairs]
                + [pl.BlockSpec(w.shape, lambda i: (0, 0), pipeline_mode=pl.Buffered(1)) for _, w in pairs]
                + [row, vec, vec, row])
    return _hosted_call(
        body, name=name, grid=(n_i,), in_specs=in_specs,
        out_specs=[row, pl.BlockSpec((8, 3 * D), lambda i: (0, 0))],
        out_shape=[jax.ShapeDtypeStruct((S, D), F32), jax.ShapeDtypeStruct((8, 3 * D), F32)],
        scratch_shapes=[], semantics=("arbitrary",),
        args=(*[a for a, _ in pairs], *[w for _, w in pairs], x, ng, sc, dx_next), comm=comm)


def _ffn_down_loss(u, w2, x, gate, gfin, tgt):
    S = x.shape[0]
    tm = _row_tile(S, 512)
    n_i = S // tm

    def body(u_ref, w2_ref, x_ref, gt_ref, g_ref, t_ref, dx_ref, f_ref, dg_ref, loss_ref, lacc):
        i = pl.program_id(0)
        f = jnp.dot(u_ref[...], w2_ref[...], preferred_element_type=F32)
        f_ref[...] = f.astype(BF16)
        xv = x_ref[...] + (0.5 * gt_ref[...]) * f
        r = _rms(xv)
        xh = xv * r
        g = g_ref[...]
        e = xh * g - t_ref[...]
        dout = e * (1.0 / D)
        dy = dout * g
        dx_ref[...] = r * (dy - xh * jnp.mean(dy * xh, axis=-1, keepdims=True))

        @pl.when(i == 0)
        def _():
            dg_ref[...] = jnp.zeros_like(dg_ref)
            lacc[...] = jnp.zeros_like(lacc)

        dg_ref[...] += _rsum8(dout * xh)
        lacc[...] += _rsum8(e * e)

        @pl.when(i == n_i - 1)
        def _():
            _all_rows(dg_ref)
            tot = jnp.sum(jnp.sum(lacc[...], axis=0, keepdims=True), axis=1, keepdims=True)
            loss_ref[...] = jnp.broadcast_to(tot * (0.5 / D), loss_ref.shape)

    row = pl.BlockSpec((tm, D), lambda i: (i, 0))
    vec = pl.BlockSpec((1, D), lambda i: (0, 0))
    return pl.pallas_call(
        body, name="ffn2_down_loss", grid=(n_i,),
        in_specs=[pl.BlockSpec((tm, FF), lambda i: (i, 0)), pl.BlockSpec((FF, D), lambda i: (0, 0)), row, vec, vec, row],
        out_specs=[row, row, pl.BlockSpec((8, D), lambda i: (0, 0)), pl.BlockSpec((8, 128), lambda i: (0, 0))],
        out_shape=[jax.ShapeDtypeStruct((S, D), F32), jax.ShapeDtypeStruct((S, D), BF16),
                   jax.ShapeDtypeStruct((8, D), F32), jax.ShapeDtypeStruct((8, 128), F32)],
        scratch_shapes=[pltpu.VMEM((8, D), F32)],
        compiler_params=_params(("arbitrary",)),
    )(u, w2, x, gate, gfin, tgt)


def _mix_in(x, ng, sh, sc, w_in, comm=None):
    S = x.shape[0]
    tm = _row_tile(S, 512)

    def body(x_ref, g_ref, sh_ref, sc_ref, w_ref, h_ref, z_ref):
        xv = x_ref[...]
        hb = (((xv * _rms(xv)) * g_ref[...]) * (1.0 + sc_ref[...]) + sh_ref[...]).astype(BF16)
        h_ref[...] = hb
        z_ref[...] = _dot_nt(hb, w_ref[...])

    row = pl.BlockSpec((tm, D), lambda i: (i, 0))
    vec = pl.BlockSpec((1, D), lambda i: (0, 0))
    return _hosted_call(
        body, name="mix_in", grid=(S // tm,),
        in_specs=[row, vec, vec, vec, pl.BlockSpec((Z_COLS, D), lambda i: (0, 0))],
        out_specs=[row, pl.BlockSpec((tm, Z_COLS), lambda i: (i, 0))],
        out_shape=[jax.ShapeDtypeStruct((S, D), BF16), jax.ShapeDtypeStruct((S, Z_COLS), F32)],
        scratch_shapes=[], semantics=("parallel",), args=(x, ng, sh, sc, w_in), comm=comm)


def _conv_taps(u, halo, rows):
    u1 = jnp.where(rows == 0, halo[7:8, :], pltpu.roll(u, 1, 0))
    u2 = jnp.where(rows == 0, halo[6:7, :], jnp.where(rows == 1, halo[7:8, :], pltpu.roll(u, 2, 0)))
    return u1, u2


def _mix_mid(z, conv_w, gq, gkv, wuq, wukv, cs, sn, comm=None):
    S = z.shape[0]
    tm = _row_tile(S, 512)
    hb = tm // 8

    def body(z_ref, zh_ref, cw_ref, gq_ref, gkv_ref, wuq_ref, wukv_ref, cs_ref, sn_ref,
             ya_ref, q_ref, k_ref, v_ref, cqn_ref, ckvn_ref):
        i = pl.program_id(0)
        xb = z_ref[:, 0:CONV_W]
        u = z_ref[:, CONV_W:2 * CONV_W] * z_ref[:, 2 * CONV_W:3 * CONV_W]
        halo = zh_ref[:, CONV_W:2 * CONV_W] * zh_ref[:, 2 * CONV_W:3 * CONV_W]
        halo = jnp.where(i > 0, halo, 0.0)
        rows = lax.broadcasted_iota(jnp.int32, (tm, CONV_W), 0)
        u1, u2 = _conv_taps(u, halo, rows)
        y = cw_ref[0:1, :] * u2 + cw_ref[1:2, :] * u1 + cw_ref[2:3, :] * u
        ya_ref[...] = xb * y

        lane = lax.broadcasted_iota(jnp.int32, (tm, 128), 1)
        cs_v, sn_v = cs_ref[...], sn_ref[...]
        cq = z_ref[:, 3 * CONV_W:3 * CONV_W + Q_LORA]
        cqn = ((cq * _rms(cq)) * gq_ref[...]).astype(BF16)
        cqn_ref[...] = cqn
        q = _dot_nt(cqn, wuq_ref[...])
        for h in range(HEADS):
            o = h * HEAD_SLOT
            q_ref[:, o:o + 128] = q[:, o:o + 128].astype(BF16)
            q_ref[:, o + 128:o + 256] = _rope(q[:, o + 128:o + 256], cs_v, sn_v, lane).astype(BF16)

        c0 = 3 * CONV_W + Q_LORA
        ckv = z_ref[:, c0:c0 + KV_LORA]
        ckvn = ((ckv * _rms(ckv)) * gkv_ref[...]).astype(BF16)
        ckvn_ref[...] = ckvn
        kv = jnp.dot(ckvn, wukv_ref[...], preferred_element_type=F32)
        krot = _rope(z_ref[:, c0 + KV_LORA:Z_COLS], cs_v, sn_v, lane).astype(BF16)
        for h in range(HEADS):
            o = h * HEAD_SLOT
            k_ref[:, o:o + 128] = kv[:, h * 128:(h + 1) * 128].astype(BF16)
            k_ref[:, o + 128:o + 256] = krot
        v_ref[...] = kv[:, HEADS * 128:].astype(BF16)

    def rows_of(n):
        return pl.BlockSpec((tm, n), lambda i: (i, 0))

    def whole(shape):
        return pl.BlockSpec(shape, lambda i: (0, 0))

    return _hosted_call(
        body, name="mix_mid", grid=(S // tm,),
        in_specs=[rows_of(Z_COLS), pl.BlockSpec((8, Z_COLS), lambda i: (jnp.maximum(i * hb - 1, 0), 0)),
                  whole((8, CONV_W)), whole((1, Q_LORA)), whole((1, KV_LORA)),
                  whole((HEADS * HEAD_SLOT, Q_LORA)), whole((KV_LORA, 2 * HEADS * 128)),
                  rows_of(128), rows_of(128)],
        out_specs=[rows_of(CONV_W), rows_of(HEADS * HEAD_SLOT), rows_of(HEADS * HEAD_SLOT), rows_of(HEADS * V_HEAD),
                   rows_of(Q_LORA), rows_of(KV_LORA)],
        out_shape=[jax.ShapeDtypeStruct((S, CONV_W), F32), jax.ShapeDtypeStruct((S, HEADS * HEAD_SLOT), BF16),
                   jax.ShapeDtypeStruct((S, HEADS * HEAD_SLOT), BF16), jax.ShapeDtypeStruct((S, HEADS * V_HEAD), BF16),
                   jax.ShapeDtypeStruct((S, Q_LORA), BF16), jax.ShapeDtypeStruct((S, KV_LORA), BF16)],
        scratch_shapes=[], semantics=("parallel",), args=(z, z, conv_w, gq, gkv, wuq, wukv, cs, sn), comm=comm)


def _att_blocks(S):
    bk = min(1024, max(S // 4, 128))
    return bk, bk


def _pair_tables(S, k_major):
    bq, bk = _att_blocks(S)
    nq, nk = S // bq, S // bk
    vis = lambda qi, ki: ki * bk < (qi + 1) * bq
    if k_major:
        pairs = [(qi, ki) for ki in range(nk) for qi in range(nq) if vis(qi, ki)]
    else:
        pairs = [(qi, ki) for qi in range(nq) for ki in range(nk) if vis(qi, ki)]
    cols = [[p[0] for p in pairs], [p[1] for p in pairs], [int((p[1] + 1) * bk > p[0] * bq) for p in pairs]]
    return [jnp.asarray(np.array(c, np.int32)) for c in cols], len(pairs)


def _chunk_mask(r0, nr, nc):
    r = (r0 + lax.broadcasted_iota(jnp.int32, (nr, nc), 0)) // CHUNK
    c = lax.broadcasted_iota(jnp.int32, (nr, nc), 1) // CHUNK
    return c <= r


def _diag_parts(bq, bk):
    return [(0, bq // 2, bk // 2), (bq // 2, bq // 2, bk)]


def _attention(q, k, v, comm=None):
    S = q.shape[0]
    bq, bk = _att_blocks(S)
    last_k = bq // bk - 1
    tables, n_pairs = _pair_tables(S, k_major=False)

    def body(qi_ref, ki_ref, mk_ref, q_ref, k_ref, v_ref, o_ref, lse_ref, m_s, l_s, acc_s):
        p_id = pl.program_id(1)
        qi, ki = qi_ref[p_id], ki_ref[p_id]

        @pl.when(ki == 0)
        def _():
            m_s[...] = jnp.full_like(m_s, NEG)
            l_s[...] = jnp.zeros_like(l_s)
            acc_s[...] = jnp.zeros_like(acc_s)

        def update(r0, nr, nc, masked):
            rows = slice(r0, r0 + nr)
            s = _dot_nt(q_ref[rows, :], k_ref[0:nc, :])
            if masked:
                s = jnp.where(_chunk_mask(r0, nr, nc), s, NEG)
            m_prev = m_s[rows, :]
            m_new = jnp.maximum(m_prev, jnp.max(s, axis=1, keepdims=True))
            alpha = jnp.exp2((m_prev - m_new) * EXP2_SCALE)
            p = jnp.exp2((s - jnp.tile(m_new, (1, nc // 128))) * EXP2_SCALE)
            l_s[rows, :] = alpha * l_s[rows, :] + jnp.sum(p, axis=1, keepdims=True)
            acc_s[rows, :] = alpha * acc_s[rows, :] + jnp.dot(p.astype(BF16), v_ref[0:nc, :],
                                                              preferred_element_type=F32)
            m_s[rows, :] = m_new

        @pl.when(mk_ref[p_id] == 0)
        def _():
            update(0, bq, bk, False)

        @pl.when(mk_ref[p_id] == 1)
        def _():
            for part in _diag_parts(bq, bk):
                update(*part, True)

        @pl.when(ki == qi * (last_k + 1) + last_k)
        def _():
            l = l_s[...]
            o_ref[...] = acc_s[...] / l
            lse_ref[...] = m_s[...] * EXP2_SCALE + jnp.log2(l)

    return _hosted_call(
        body, name="attention", grid=(HEADS, n_pairs),
        in_specs=[pl.BlockSpec((bq, HEAD_SLOT), lambda h, p, qt, kt, mt: (qt[p], h)),
                  pl.BlockSpec((bk, HEAD_SLOT), lambda h, p, qt, kt, mt: (kt[p], h)),
                  pl.BlockSpec((bk, V_HEAD), lambda h, p, qt, kt, mt: (kt[p], h))],
        out_specs=[pl.BlockSpec((bq, V_HEAD), lambda h, p, qt, kt, mt: (qt[p], h))] * 2,
        out_shape=[jax.ShapeDtypeStruct((S, HEADS * V_HEAD), F32)] * 2,
        scratch_shapes=[pltpu.VMEM((bq, V_HEAD), F32)] * 3,
        semantics=("arbitrary", "arbitrary"), args=(q, k, v), comm=comm, prefetch=tables)


def _attention_bwd(q, k, v, do, lse2, delta):
    S = q.shape[0]
    bq, bk = _att_blocks(S)
    nq = S // bq
    tables, n_pairs = _pair_tables(S, k_major=True)

    def body(qi_ref, ki_ref, mk_ref, q_ref, k_ref, v_ref, do_ref, lse_ref, dl_ref, dq_hbm, dk_ref, dv_ref,
             dq_s, dq_b, dk_s, dv_s, sem):
        head, p_id = pl.program_id(0), pl.program_id(1)
        qi, ki = qi_ref[p_id], ki_ref[p_id]

        @pl.when(qi * bq <= ki * bk)
        def _():
            dk_s[...] = jnp.zeros_like(dk_s)
            dv_s[...] = jnp.zeros_like(dv_s)

        def update(r0, nr, nc, masked):
            rows, cols = slice(r0, r0 + nr), slice(0, nc)
            qv, kv, dov = q_ref[rows, :], k_ref[cols, :], do_ref[rows, :]
            s = _dot_nt(qv, kv)
            dp = _dot_nt(dov, v_ref[cols, :])
            if masked:
                s = jnp.where(_chunk_mask(r0, nr, nc), s, NEG)
            p = jnp.exp2(s * EXP2_SCALE - jnp.tile(lse_ref[rows, :], (1, nc // 128)))
            dv_s[cols, :] += lax.dot_general(p.astype(BF16), dov, (((0,), (0,)), ((), ())),
                                             preferred_element_type=F32)
            ds = (p * (dp - jnp.tile(dl_ref[rows, :], (1, nc // 128)))).astype(BF16)
            dk_s[cols, :] += lax.dot_general(ds, qv, (((0,), (0,)), ((), ())), preferred_element_type=F32)
            dq = jnp.dot(ds, kv, preferred_element_type=F32)
            out_rows = pl.ds(pl.multiple_of(qi * bq + r0, nr), nr)

            @pl.when(ki == 0)
            def _():
                dq_s[out_rows, :] = dq

            @pl.when(ki > 0)
            def _():
                dq_s[out_rows, :] += dq

        @pl.when(mk_ref[p_id] == 0)
        def _():
            update(0, bq, bk, False)

        @pl.when(mk_ref[p_id] == 1)
        def _():
            for part in _diag_parts(bq, bk):
                update(*part, True)

        @pl.when(qi == nq - 1)
        def _():
            dk_ref[...] = (dk_s[...] * ATT_SCALE).astype(BF16)
            dv_ref[...] = dv_s[...].astype(BF16)

        @pl.when(p_id == n_pairs - 1)
        def _():
            dq_b[...] = (dq_s[...] * ATT_SCALE).astype(BF16)
            out = pltpu.make_async_copy(
                dq_b, dq_hbm.at[:, pl.ds(pl.multiple_of(head * HEAD_SLOT, HEAD_SLOT), HEAD_SLOT)], sem)
            out.start()
            out.wait()

    grid_spec = pltpu.PrefetchScalarGridSpec(
        num_scalar_prefetch=3, grid=(HEADS, n_pairs),
        in_specs=[pl.BlockSpec((bq, HEAD_SLOT), lambda h, p, qt, kt, mt: (qt[p], h)),
                  pl.BlockSpec((bk, HEAD_SLOT), lambda h, p, qt, kt, mt: (kt[p], h)),
                  pl.BlockSpec((bk, V_HEAD), lambda h, p, qt, kt, mt: (kt[p], h)),
                  pl.BlockSpec((bq, V_HEAD), lambda h, p, qt, kt, mt: (qt[p], h)),
                  pl.BlockSpec((bq, V_HEAD), lambda h, p, qt, kt, mt: (qt[p], h)),
                  pl.BlockSpec((bq, V_HEAD), lambda h, p, qt, kt, mt: (qt[p], h))],
        out_specs=[pl.BlockSpec(memory_space=pl.ANY),
                   pl.BlockSpec((bk, HEAD_SLOT), lambda h, p, qt, kt, mt: (kt[p], h)),
                   pl.BlockSpec((bk, V_HEAD), lambda h, p, qt, kt, mt: (kt[p], h))],
        scratch_shapes=[pltpu.VMEM((S, HEAD_SLOT), F32), pltpu.VMEM((S, HEAD_SLOT), BF16),
                        pltpu.VMEM((bk, HEAD_SLOT), F32), pltpu.VMEM((bk, V_HEAD), F32), pltpu.SemaphoreType.DMA])
    return pl.pallas_call(
        body, name="attention_bwd", grid_spec=grid_spec,
        out_shape=[jax.ShapeDtypeStruct((S, HEADS * HEAD_SLOT), BF16), jax.ShapeDtypeStruct((S, HEADS * HEAD_SLOT), BF16),
                   jax.ShapeDtypeStruct((S, HEADS * V_HEAD), BF16)],
        compiler_params=_params(("arbitrary", "arbitrary")),
    )(*tables, q, k, v, do, lse2, delta)


def _group_mats():
    def blockdiag(n, g):
        idx = np.arange(n) // g
        return jnp.asarray((idx[:, None] == idx[None, :]).astype(np.float32), dtype=BF16)
    return blockdiag(CONV_W, CONV_GROUP), blockdiag(HEADS * V_HEAD, V_HEAD)


def _mix_out(ya, o, gout, w_out, x, gate, ga, gb):
    S = x.shape[0]
    tm = _row_tile(S, 512)

    def body(ya_ref, o_ref, go_ref, w_ref, x_ref, g_ref, ga_ref, gb_ref, xo_ref, yn_ref, yo_ref):
        yav, ov = ya_ref[...], o_ref[...]
        ra = lax.rsqrt(_gsum(yav * yav, ga_ref[...]) * (1.0 / CONV_GROUP) + EPS)
        rb = lax.rsqrt(_gsum(ov * ov, gb_ref[...]) * (1.0 / V_HEAD) + EPS)
        na = ((yav * ra) * go_ref[:, 0:CONV_W]).astype(BF16)
        nb = ((ov * rb) * go_ref[:, CONV_W:]).astype(BF16)
        yn_ref[:, 0:CONV_W] = na
        yn_ref[:, CONV_W:] = nb
        yo = (jnp.dot(na, w_ref[0:CONV_W, :], preferred_element_type=F32)
              + jnp.dot(nb, w_ref[CONV_W:, :], preferred_element_type=F32))
        xo_ref[...] = x_ref[...] + g_ref[...] * yo
        yo_ref[...] = yo.astype(BF16)

    row = pl.BlockSpec((tm, D), lambda i: (i, 0))
    half = pl.BlockSpec((tm, CONV_W), lambda i: (i, 0))
    vec = pl.BlockSpec((1, D), lambda i: (0, 0))
    sq = pl.BlockSpec((CONV_W, CONV_W), lambda i: (0, 0))
    return pl.pallas_call(
        body, name="mix_out", grid=(S // tm,),
        in_specs=[half, half, vec, pl.BlockSpec((D, D), lambda i: (0, 0)), row, vec, sq, sq],
        out_specs=[row, row, row],
        out_shape=[jax.ShapeDtypeStruct((S, D), F32), jax.ShapeDtypeStruct((S, D), BF16),
                   jax.ShapeDtypeStruct((S, D), BF16)],
        compiler_params=_params(("parallel",)),
    )(ya, o, gout, w_out, x, gate, ga, gb)


def _mix_out_bwd(dx, gate, yo, w_out, ya, o, gout, ga, gb, comm=None):
    S = dx.shape[0]
    tm = _row_tile(S, 512)
    n_i = S // tm

    def norm_bwd(v, dn, gain, gmat, inv_n):
        r = lax.rsqrt(_gsum(v * v, gmat) * inv_n + EPS)
        vh = v * r
        dy = dn * gain
        return r * (dy - vh * (_gsum(dy * vh, gmat) * inv_n)), dn * vh

    def body(dx_ref, g_ref, yo_ref, w_ref, ya_ref, o_ref, go_ref, ga_ref, gb_ref,
             dyo_ref, dya_ref, do_ref, dl_ref, p_ref):
        i = pl.program_id(0)
        dxv = dx_ref[...]
        dyo = (dxv * g_ref[...]).astype(BF16)
        dyo_ref[...] = dyo
        dyn = _dot_nt(dyo, w_ref[...])
        dya, dga = norm_bwd(ya_ref[...], dyn[:, 0:CONV_W], go_ref[:, 0:CONV_W], ga_ref[...], 1.0 / CONV_GROUP)
        ov = o_ref[...]
        do, dgb = norm_bwd(ov, dyn[:, CONV_W:], go_ref[:, CONV_W:], gb_ref[...], 1.0 / V_HEAD)
        dya_ref[...] = dya
        do_ref[...] = do.astype(BF16)
        dl_ref[...] = _gsum(do * ov, gb_ref[...], split=True)

        @pl.when(i == 0)
        def _():
            p_ref[...] = jnp.zeros_like(p_ref)

        p_ref[:, 0:D] += _rsum8(dxv * yo_ref[...].astype(F32))
        p_ref[:, D:D + CONV_W] += _rsum8(dga)
        p_ref[:, D + CONV_W:2 * D] += _rsum8(dgb)

        @pl.when(i == n_i - 1)
        def _():
            _all_rows(p_ref)

    row = pl.BlockSpec((tm, D), lambda i: (i, 0))
    half = pl.BlockSpec((tm, CONV_W), lambda i: (i, 0))
    vec = pl.BlockSpec((1, D), lambda i: (0, 0))
    sq = pl.BlockSpec((CONV_W, CONV_W), lambda i: (0, 0))
    return _hosted_call(
        body, name="mix_out_bwd", grid=(n_i,),
        in_specs=[row, vec, row, pl.BlockSpec((D, D), lambda i: (0, 0)), half, half, vec, sq, sq],
        out_specs=[row, half, half, half, pl.BlockSpec((8, 2 * D), lambda i: (0, 0))],
        out_shape=[jax.ShapeDtypeStruct((S, D), BF16), jax.ShapeDtypeStruct((S, CONV_W), F32),
                   jax.ShapeDtypeStruct((S, CONV_W), BF16), jax.ShapeDtypeStruct((S, CONV_W), F32),
                   jax.ShapeDtypeStruct((8, 2 * D), F32)],
        scratch_shapes=[], semantics=("arbitrary",), args=(dx, gate, yo, w_out, ya, o, gout, ga, gb), comm=comm)


MID_SUMS = 3 * CONV_W + Q_LORA + KV_LORA


def _mix_mid_bwd(z, dya, conv_w, gq, gkv, wuq, wukv, cs, sn, dq, dk, dv, comm=None):
    S = z.shape[0]
    tm = _row_tile(S, 256)
    n_i = S // tm
    hb = tm // 8
    last_blk = S // 8 - 1

    def latent_bwd(cv, dcn, gain):
        r = _rms(cv)
        ch = cv * r
        dy = dcn * gain
        return r * (dy - ch * jnp.mean(dy * ch, axis=-1, keepdims=True)), dcn * ch

    def body(z_ref, zp_ref, zn_ref, dya_ref, dyan_ref, cw_ref, gq_ref, gkv_ref, wuq_ref, wukv_ref, cs_ref, sn_ref,
             dq_ref, dk_ref, dv_ref, dz_ref, dqf_ref, dkvf_ref, p_ref):
        i = pl.program_id(0)
        xb, xc, xu = z_ref[:, 0:CONV_W], z_ref[:, CONV_W:2 * CONV_W], z_ref[:, 2 * CONV_W:3 * CONV_W]
        u = xc * xu
        halo = jnp.where(i > 0, zp_ref[:, CONV_W:2 * CONV_W] * zp_ref[:, 2 * CONV_W:3 * CONV_W], 0.0)
        rows = lax.broadcasted_iota(jnp.int32, (tm, CONV_W), 0)
        u1, u2 = _conv_taps(u, halo, rows)
        w0, w1, w2 = cw_ref[0:1, :], cw_ref[1:2, :], cw_ref[2:3, :]
        y = w0 * u2 + w1 * u1 + w2 * u
        dyav = dya_ref[...]
        dy = dyav * xb
        nxt = jnp.where(i < n_i - 1, dyan_ref[...] * zn_ref[:, 0:CONV_W], 0.0)
        dy1 = jnp.where(rows == tm - 1, nxt[0:1, :], pltpu.roll(dy, tm - 1, 0))
        dy2 = jnp.where(rows == tm - 1, nxt[1:2, :], jnp.where(rows == tm - 2, nxt[0:1, :], pltpu.roll(dy, tm - 2, 0)))
        du = w2 * dy + w1 * dy1 + w0 * dy2
        dz_ref[:, 0:CONV_W] = (dyav * y).astype(BF16)
        dz_ref[:, CONV_W:2 * CONV_W] = (du * xu).astype(BF16)
        dz_ref[:, 2 * CONV_W:3 * CONV_W] = (du * xc).astype(BF16)

        lane = lax.broadcasted_iota(jnp.int32, (tm, 128), 1)
        cs_v, sn_v = cs_ref[...], sn_ref[...]
        dkr = jnp.zeros((tm, 128), F32)
        for h in range(HEADS):
            o = h * HEAD_SLOT
            dqf_ref[:, o:o + 128] = dq_ref[:, o:o + 128]
            dqf_ref[:, o + 128:o + 256] = _rope_t(dq_ref[:, o + 128:o + 256].astype(F32), cs_v, sn_v, lane).astype(BF16)
            dkvf_ref[:, h * 128:(h + 1) * 128] = dk_ref[:, o:o + 128]
            dkr = dkr + dk_ref[:, o + 128:o + 256].astype(F32)
        dkvf_ref[:, HEADS * 128:] = dv_ref[...]

        c0 = 3 * CONV_W
        dcqn = jnp.dot(dqf_ref[...], wuq_ref[...], preferred_element_type=F32)
        dcq, dgq = latent_bwd(z_ref[:, c0:c0 + Q_LORA], dcqn, gq_ref[...])
        dz_ref[:, c0:c0 + Q_LORA] = dcq.astype(BF16)
        c1 = c0 + Q_LORA
        dckvn = _dot_nt(dkvf_ref[...], wukv_ref[...])
        dckv, dgkv = latent_bwd(z_ref[:, c1:c1 + KV_LORA], dckvn, gkv_ref[...])
        dz_ref[:, c1:c1 + KV_LORA] = dckv.astype(BF16)
        dz_ref[:, c1 + KV_LORA:Z_COLS] = _rope_t(dkr, cs_v, sn_v, lane).astype(BF16)

        @pl.when(i == 0)
        def _():
            p_ref[...] = jnp.zeros_like(p_ref)

        p_ref[:, 0:CONV_W] += _rsum8(dy * u2)
        p_ref[:, CONV_W:2 * CONV_W] += _rsum8(dy * u1)
        p_ref[:, 2 * CONV_W:3 * CONV_W] += _rsum8(dy * u)
        p_ref[:, c0:c0 + Q_LORA] += _rsum8(dgq)
        p_ref[:, c1:c1 + KV_LORA] += _rsum8(dgkv)

        @pl.when(i == n_i - 1)
        def _():
            _all_rows(p_ref)

    def rows_of(n):
        return pl.BlockSpec((tm, n), lambda i: (i, 0))

    def whole(shape):
        return pl.BlockSpec(shape, lambda i: (0, 0))

    def prev8(n):
        return pl.BlockSpec((8, n), lambda i: (jnp.maximum(i * hb - 1, 0), 0))

    def next8(n):
        return pl.BlockSpec((8, n), lambda i: (jnp.minimum((i + 1) * hb, last_blk), 0))

    return _hosted_call(
        body, name="mix_mid_bwd", grid=(n_i,),
        in_specs=[rows_of(Z_COLS), prev8(Z_COLS), next8(Z_COLS), rows_of(CONV_W), next8(CONV_W),
                  whole((8, CONV_W)), whole((1, Q_LORA)), whole((1, KV_LORA)),
                  whole((HEADS * HEAD_SLOT, Q_LORA)), whole((KV_LORA, 2 * HEADS * 128)),
                  rows_of(128), rows_of(128),
                  rows_of(HEADS * HEAD_SLOT), rows_of(HEADS * HEAD_SLOT), rows_of(HEADS * V_HEAD)],
        out_specs=[rows_of(Z_COLS), rows_of(HEADS * HEAD_SLOT), rows_of(2 * HEADS * 128), whole((8, MID_SUMS))],
        out_shape=[jax.ShapeDtypeStruct((S, Z_COLS), BF16), jax.ShapeDtypeStruct((S, HEADS * HEAD_SLOT), BF16),
                   jax.ShapeDtypeStruct((S, 2 * HEADS * 128), BF16), jax.ShapeDtypeStruct((8, MID_SUMS), F32)],
        scratch_shapes=[], semantics=("arbitrary",),
        args=(z, z, z, dya, dya, conv_w, gq, gkv, wuq, wukv, cs, sn, dq, dk, dv), comm=comm)


ADA_Q = N_MOD * D // N_CHIPS
ADA_TN = 768


def _ada_forward(c_all, ada_w_q, ada_b_q):
    def body(c_ref, w_ref, b_ref, o_ref):
        cv = c_ref[...]
        sc = (cv * jax.nn.sigmoid(cv)).astype(BF16)
        o_ref[...] = jnp.dot(sc, w_ref[...].astype(BF16), preferred_element_type=F32) + b_ref[...]

    return pl.pallas_call(
        body, name="ada_forward", grid=(ADA_Q // ADA_TN,),
        in_specs=[pl.BlockSpec((16, D), lambda j: (0, 0)), pl.BlockSpec((D, ADA_TN), lambda j: (0, j)),
                  pl.BlockSpec((1, ADA_TN), lambda j: (0, j))],
        out_specs=pl.BlockSpec((16, ADA_TN), lambda j: (0, j)),
        out_shape=jax.ShapeDtypeStruct((16, ADA_Q), F32),
        compiler_params=_params(("parallel",)),
    )(c_all, ada_w_q, ada_b_q)


def _ada_wgrad(c_all, dmod_q):
    def body(c_ref, d_ref, o_ref):
        cv = c_ref[...]
        sc = (cv * jax.nn.sigmoid(cv)).astype(BF16)
        o_ref[...] = lax.dot_general(sc, d_ref[...].astype(BF16), (((0,), (0,)), ((), ())),
                                     preferred_element_type=F32)

    return pl.pallas_call(
        body, name="ada_wgrad", grid=(ADA_Q // ADA_TN,),
        in_specs=[pl.BlockSpec((16, D), lambda j: (0, 0)), pl.BlockSpec((16, ADA_TN), lambda j: (0, j))],
        out_specs=pl.BlockSpec((D, ADA_TN), lambda j: (0, j)),
        out_shape=jax.ShapeDtypeStruct((D, ADA_Q), F32),
        compiler_params=_params(("parallel",)),
    )(c_all, dmod_q)


def _sum_devices(parts):
    n = parts.shape[1]

    def body(p_ref, o_ref):
        o_ref[...] = jnp.broadcast_to(jnp.sum(p_ref[...], axis=0, keepdims=True), o_ref.shape)

    return pl.pallas_call(
        body, name="sum_devices",
        in_specs=[pl.BlockSpec((N_DEV, n), lambda: (0, 0))], out_specs=pl.BlockSpec((N_DEV, n), lambda: (0, 0)),
        out_shape=jax.ShapeDtypeStruct((N_DEV, n), F32),
    )(parts)


def _adamw(ws, gs, ms, vs, *, name, comm=None):
    n = len(ws)
    _, rows, cols = ws[0].shape
    tr = _row_tile(rows, 256)

    def body(*refs):
        ins, outs = refs[:4 * n], refs[4 * n:]
        for k in range(n):
            w_ref, g_ref, m_ref, v_ref = ins[k], ins[n + k], ins[2 * n + k], ins[3 * n + k]
            go_ref, d_ref, mo_ref, vo_ref = outs[4 * k:4 * k + 4]
            gv = g_ref[...]
            mn = B1 * m_ref[0] + (1.0 - B1) * gv
            vn = B2 * v_ref[0] + (1.0 - B2) * (gv * gv)
            m_hat = mn / (1.0 - B1 ** STEP)
            v_hat = vn / (1.0 - B2 ** STEP)
            go_ref[0] = gv
            d_ref[0] = -LR * (m_hat / (jnp.sqrt(v_hat) + AEPS) + WD * w_ref[0])
            mo_ref[0] = mn
            vo_ref[0] = vn

    blk = pl.BlockSpec((1, tr, cols), lambda i: (0, i, 0))
    gblk = pl.BlockSpec((tr, cols), lambda i: (i, 0))
    res, couts = _hosted_call(
        body, name=name, grid=(rows // tr,),
        in_specs=[blk] * n + [gblk] * n + [blk] * 2 * n, out_specs=[blk] * 4 * n,
        out_shape=[jax.ShapeDtypeStruct((1, rows, cols), F32)] * 4 * n,
        scratch_shapes=[], semantics=("parallel",), args=(*ws, *gs, *ms, *vs), comm=comm)
    return [res[4 * k:4 * k + 4] for k in range(n)], couts


def _small_allgather(v, *, name):
    m, n = v.shape

    def body(x_ref, out_ref, send_sems, recv_sems, local_sem):
        x, y, c = _place()
        me, sibling = (x, y, c), (x, y, 1 - c)
        chips = [(1 - x, y), (x, 1 - y), (1 - x, 1 - y)]

        def rows(px, py, pc):
            return out_ref.at[pl.ds((4 * px + 2 * py + pc) * m, m), :]

        def copy(k, block, to, src=None):
            return pltpu.make_async_remote_copy(
                src_ref=rows(*block) if src is None else src, dst_ref=rows(*block),
                send_sem=send_sems.at[k], recv_sem=recv_sems.at[k], device_id=to, device_id_type=MESH)

        mine = pltpu.make_async_copy(x_ref, rows(*me), local_sem)
        mine.start()
        first = [copy(0, me, sibling, src=x_ref)]
        first += [copy(1 + j, me, (*chip, c), src=x_ref) for j, chip in enumerate(chips)]
        for cp in first:
            cp.start()
        passed = [copy(4 + j, (*chip, c), sibling) for j, chip in enumerate(chips)]
        for j, chip in enumerate(chips):
            copy(1 + j, (*chip, c), me).wait_recv()
            passed[j].start()
        copy(0, sibling, me).wait_recv()
        for j, chip in enumerate(chips):
            copy(4 + j, (*chip, 1 - c), me).wait_recv()
        for cp in first + passed:
            cp.wait_send()
        mine.wait()

    return pl.pallas_call(
        body, name=name,
        out_shape=jax.ShapeDtypeStruct((N_DEV * m, n), v.dtype),
        in_specs=[pl.BlockSpec(memory_space=pltpu.VMEM)], out_specs=pl.BlockSpec(memory_space=pltpu.VMEM),
        scratch_shapes=[pltpu.SemaphoreType.DMA((7,)), pltpu.SemaphoreType.DMA((7,)), pltpu.SemaphoreType.DMA],
    )(v)


ADD_BLOCKS = 2


def _pair_add(place, gs, ts, *, name):
    n_a = len(gs)

    def body(pl_ref, *refs):
        g_refs, t_refs = refs[:n_a], refs[n_a:2 * n_a]
        pf_refs, pb_refs = refs[2 * n_a:3 * n_a], refs[3 * n_a:]
        own = pl.program_id(1) == pl_ref[1]
        for g_ref, t_ref, pf_ref, pb_ref in zip(g_refs, t_refs, pf_refs, pb_refs):
            s = g_ref[...] + t_ref[...]
            pb_ref[...] = s.astype(BF16)

            @pl.when(own)
            def _():
                pf_ref[...] = s[0]

    def blk(t, own_half):
        tr = t.shape[1] // ADD_BLOCKS
        if own_half:
            return pl.BlockSpec((1, tr, t.shape[2]), lambda r, q, p: (q, p[0] * ADD_BLOCKS + r, 0))
        return pl.BlockSpec((1, tr, t.shape[2]), lambda r, q, p: (q, r, 0))

    def own_blk(t):
        return pl.BlockSpec((t.shape[1] // ADD_BLOCKS, t.shape[2]), lambda r, q, p: (r, 0))

    grid_spec = pltpu.PrefetchScalarGridSpec(
        num_scalar_prefetch=1, grid=(ADD_BLOCKS, N_CHIPS),
        in_specs=[blk(t, True) for t in ts] + [blk(t, False) for t in ts],
        out_specs=[own_blk(t) for t in ts] + [blk(t, False) for t in ts])
    res = pl.pallas_call(
        body, name=name, grid_spec=grid_spec,
        out_shape=([jax.ShapeDtypeStruct(t.shape[1:], F32) for t in ts]
                   + [jax.ShapeDtypeStruct(t.shape, BF16) for t in ts]),
        compiler_params=_params(("parallel", "arbitrary")),
    )(place, *gs, *ts)
    return list(res[:n_a]), list(res[n_a:])


def _chip_add(place, pfs, ts, *, name):
    n_a = len(pfs)

    def body(pl_ref, *refs):
        pf_refs, t_refs, o_refs = refs[:n_a], refs[n_a:4 * n_a], refs[4 * n_a:]
        for i, (pf_ref, o_ref) in enumerate(zip(pf_refs, o_refs)):
            t1, t2, t3 = t_refs[3 * i:3 * i + 3]
            o_ref[...] = ((pf_ref[...] + t1[0].astype(F32)) + t2[0].astype(F32)) + t3[0].astype(F32)

    def slot(t, j):
        return pl.BlockSpec((1, t.shape[1] // ADD_BLOCKS, t.shape[2]), lambda r, p: (p[1] ^ j, r, 0))

    def half(t):
        return pl.BlockSpec((t.shape[1] // ADD_BLOCKS, t.shape[2]), lambda r, p: (r, 0))

    grid_spec = pltpu.PrefetchScalarGridSpec(
        num_scalar_prefetch=1, grid=(ADD_BLOCKS,),
        in_specs=[half(t) for t in ts] + [slot(t, j) for t in ts for j in (1, 2, 3)],
        out_specs=[half(t) for t in ts])
    res = pl.pallas_call(
        body, name=name, grid_spec=grid_spec,
        out_shape=[jax.ShapeDtypeStruct(t.shape[1:], F32) for t in ts],
        compiler_params=_params(("parallel",)),
    )(place, *pfs, *[t for t in ts for _ in range(3)])
    return list(res)


BULK = [("ffn1_w1", "colsT"), ("ffn1_w3", "colsT"), ("ffn1_w2", "rows"), ("w_in", "cols"), ("w_uq", "colsT"),
        ("w_ukv", "cols"), ("w_out", "rows"), ("ffn2_w1", "colsT"), ("ffn2_w3", "colsT"), ("ffn2_w2", "rows")]
KIND = dict(BULK)


def _group(*names):
    return [b for b in BULK if b[0] in names]


W_FIRST = _group("ffn1_w1", "ffn1_w3")
W_REST = [b for b in BULK if b not in W_FIRST]
W_MIX = _group("ffn1_w2", "w_in", "w_uq", "w_ukv", "w_out")
W_FFN2 = _group("ffn2_w1", "ffn2_w3", "ffn2_w2")
G_FFN2 = _group("ffn2_w1", "ffn2_w3", "ffn2_w2")
G_MIX = _group("w_in", "w_uq", "w_ukv", "w_out")
G_FFN1 = _group("ffn1_w1", "ffn1_w3", "ffn1_w2")


def _gathered_weights(specs, shards, got, myq):
    out = {}
    for (name, kind), part in zip(specs, got):
        part = lax.dynamic_update_slice_in_dim(part, shards[name][None], myq, axis=0)
        out[name] = _full_weight(part, kind)
    return out


def _working_shard(w, kind):
    return jnp.swapaxes(w, 1, 2)[0] if kind == "colsT" else w[0]


def _full_weight(parts, kind):
    if kind == "cols":
        return jnp.transpose(parts, (1, 0, 2)).reshape(parts.shape[1], -1)
    return parts.reshape(-1, parts.shape[2])


def _quarters(g, kind):
    if kind == "cols":
        k, n = g.shape
        return jnp.transpose(g.reshape(k, N_CHIPS, n // N_CHIPS), (1, 0, 2))
    return g.reshape(N_CHIPS, g.shape[0] // N_CHIPS, g.shape[1])


def _pad_heads(w_uq_t):
    w = w_uq_t.reshape(HEADS, QK_NOPE + QK_ROPE, Q_LORA)
    return jnp.pad(w, ((0, 0), (0, HEAD_SLOT - QK_NOPE - QK_ROPE), (0, 0))).reshape(HEADS * HEAD_SLOT, Q_LORA)


def _unpad_heads(g):
    return g.reshape(HEADS, HEAD_SLOT, Q_LORA)[:, :QK_NOPE + QK_ROPE].reshape(HEADS * (QK_NOPE + QK_ROPE), Q_LORA)


def _split_kv(w_ukv):
    return jnp.transpose(w_ukv.reshape(KV_LORA, HEADS, 2, 128), (0, 2, 1, 3)).reshape(KV_LORA, 2 * HEADS * 128)


def _merge_kv(g):
    return jnp.transpose(g.reshape(KV_LORA, 2, HEADS, 128), (0, 2, 1, 3)).reshape(KV_LORA, 2 * HEADS * 128)


def _rope_tables(positions):
    inv_freq = ROPE_THETA ** (-jnp.arange(0, QK_ROPE, 2, dtype=F32) / QK_ROPE)
    ang = positions.astype(F32)[:, None] * inv_freq
    cos, sin, zero = jnp.cos(ang), jnp.sin(ang), jnp.zeros((positions.shape[0], 64), F32)
    return jnp.concatenate([cos, cos, zero], axis=1), jnp.concatenate([sin, sin, zero], axis=1)


def _assemble(place, specs, rhs, others):
    south = place[0] == 0
    return {b[0]: jnp.concatenate([jnp.where(south, rh, ot), jnp.where(south, ot, rh)], axis=0)
            for b, rh, ot in zip(specs, rhs, others)}


def _local_step(x, positions, target, mod, vec, conv_w, w_first, rest_shards, place, tail_host=None):
    row = lambda k: mod[k:k + 1]
    sh1, sc1, g1, sh2, sc2, g2, sh3, sc3, g3 = [row(k) for k in range(N_MOD)]
    cs, sn = _rope_tables(positions)
    cw8 = jnp.pad(conv_w, ((0, 5), (0, 0)))
    ga, gb = _group_mats()
    dist = place is not None
    comm = lambda prog: prog if dist else None

    w = dict(w_first) if dist else {**rest_shards, **w_first}

    def gather(specs):
        return _Gather([rest_shards[b[0]] for b in specs]) if dist else None

    def arrived(specs, got):
        if dist:
            w.update(_gathered_weights(specs, rest_shards, got, place[1]))

    (h1, a1, b1, u1), got = _ffn_up(x, vec["norm_ffn1_g"], sh1, sc1, w["ffn1_w1"], w["ffn1_w3"],
                                    name="ffn1_up", comm=gather(W_MIX))
    arrived(W_MIX, got)
    w_in = jnp.pad(w["w_in"].T, ((0, Z_COLS - IN_COLS), (0, 0)))
    wuq = _pad_heads(w["w_uq"])
    wukv = _split_kv(w["w_ukv"])
    (x1, f1), _ = _ffn_down(u1, w["ffn1_w2"], x, g1, name="ffn1_down")
    (h2, z), _ = _mix_in(x1, vec["norm_mix_g"], sh2, sc2, w_in)
    (ya, q, k, v, cqn, ckvn), _ = _mix_mid(z, cw8, vec["q_norm_g"], vec["kv_norm_g"], wuq, wukv, cs, sn)
    (o, lse), got = _attention(q, k, v, comm=gather(W_FFN2))
    arrived(W_FFN2, got)
    x2, yn, yo = _mix_out(ya, o, vec["out_norm_g"], w["w_out"], x1, g2, ga, gb)
    (h3, a3, b3, u3), _ = _ffn_up(x2, vec["norm_ffn2_g"], sh3, sc3, w["ffn2_w1"], w["ffn2_w3"], name="ffn2_up")
    dx3, f3, dgfin, loss_blk = _ffn_down_loss(u3, w["ffn2_w2"], x2, g3, vec["final_norm_g"], target)

    grads, reduced = {}, {}

    def tn(a, b, tm, tn_, name, prog=None):
        if prog is None:
            return _tn_matmul(a, b, tm=tm, tn=tn_, name=name), None
        return _tn_matmul(a, b, tm=tm, tn=tn_, name=name, comm=prog)

    def slab_of(specs):
        return [_quarters(grads[n], kind) for n, kind in specs]

    (df3, da3, db3, dg3), _ = _ffn_bwd_du(dx3, g3, f3, w["ffn2_w2"], a3, b3, name="ffn2_bwd_du")
    grads["ffn2_w2"], _ = tn(u3, df3, FF // 2, D, "ffn2_dw2")
    grads["ffn2_w1"], _ = tn(da3, h3, FF // 2, D, "ffn2_dw1")
    grads["ffn2_w3"], _ = tn(db3, h3, FF // 2, D, "ffn2_dw3")
    (dx2, s3), _ = _dh_normbwd([(da3, w["ffn2_w1"]), (db3, w["ffn2_w3"])], x2, vec["norm_ffn2_g"], sc3, dx3,
                               name="ffn2_bwd_dh")

    p1 = slab_of(G_FFN2) if dist else None
    (dyo, dya, do, delta, s_out), t1 = _mix_out_bwd(dx2, g2, yo, w["w_out"], ya, o, vec["out_norm_g"], ga, gb,
                                                    comm=comm(_PairExchange(p1) if dist else None))
    grads["w_out"], _ = tn(yn, dyo, D, D, "dw_out")
    if dist:
        pf1, pb1 = _pair_add(place, p1, t1, name="ffn2g_pair_add")
    dq, dk, dv = _attention_bwd(q, k, v, do, lse, delta)
    (dz, dqf, dkvf, s_mid), t2 = _mix_mid_bwd(z, dya, cw8, vec["q_norm_g"], vec["kv_norm_g"], wuq, wukv, cs, sn,
                                              dq, dk, dv, comm=comm(_ChipExchange(pb1) if dist else None))
    g_uq, _ = tn(dqf, cqn, HEADS * HEAD_SLOT, Q_LORA, "dw_uq")
    g_ukv, _ = tn(ckvn, dkvf, KV_LORA, 2 * HEADS * 128, "dw_ukv")
    grads["w_uq"], grads["w_ukv"] = _unpad_heads(g_uq), _merge_kv(g_ukv)
    g_in, _ = tn(h2, dz, D, Z_COLS, "dw_in")
    grads["w_in"] = g_in[:, :IN_COLS]
    (dx1, s2), _ = _dh_normbwd([(dz, w_in)], x1, vec["norm_mix_g"], sc2, dx2, name="mix_bwd_dh")

    p2 = slab_of(G_MIX) if dist else None
    rh_ffn2 = _chip_add(place, pf1, t2, name="ffn2g_chip_add") if dist else None
    (df1, da1, db1, dg1), got = _ffn_bwd_du(dx1, g1, f1, w["ffn1_w2"], a1, b1, name="ffn1_bwd_du",
                                            comm=comm(_Multi([_PairExchange(p2), _PairShare(rh_ffn2)]) if dist else None))
    if dist:
        t1 = got[:len(p2)]
        reduced.update(_assemble(place, G_FFN2, rh_ffn2, got[len(p2):]))
    dh_pairs = [(da1, w["ffn1_w1"]), (db1, w["ffn1_w3"])]
    if dist:
        pf2, pb2 = _pair_add(place, p2, t1, name="mixg_pair_add")
        grads["ffn1_w2"], t2 = tn(u1, df1, FF // 2, D, "ffn1_dw2", _ChipExchange(pb2))
        rh_mix = _chip_add(place, pf2, t2, name="mixg_chip_add")
        q_w2 = [_quarters(grads["ffn1_w2"], KIND["ffn1_w2"])]
        grads["ffn1_w1"], got = tn(da1, h1, FF // 2, D, "ffn1_dw1", _Multi([_PairShare(rh_mix), _PairExchange(q_w2)]))
        reduced.update(_assemble(place, G_MIX, rh_mix, got[:len(rh_mix)]))
        pf_w2, pb_w2 = _pair_add(place, q_w2, got[len(rh_mix):], name="ffn1w2_pair_add")
        q_w1 = [_quarters(grads["ffn1_w1"], KIND["ffn1_w1"])]
        grads["ffn1_w3"], got = tn(db1, h1, FF // 2, D, "ffn1_dw3", _Multi([_ChipExchange(pb_w2), _PairExchange(q_w1)]))
        t2_w2 = got[:1]
        pf_w1, pb_w1 = _pair_add(place, q_w1, got[1:], name="ffn1w1_pair_add")
        q_w3 = [_quarters(grads["ffn1_w3"], KIND["ffn1_w3"])]
        (dx0, s1), got = _dh_normbwd(dh_pairs, x, vec["norm_ffn1_g"], sc1, dx1, name="ffn1_bwd_dh",
                                     comm=_Multi([_ChipExchange(pb_w1), _PairExchange(q_w3)]))
        t2_w1 = got[:1]
        pf_w3, pb_w3 = _pair_add(place, q_w3, got[1:], name="ffn1w3_pair_add")
        t2_w3 = tail_host(_ChipExchange(pb_w3), reduced)
        rh = _chip_add(place, pf_w1 + pf_w3 + pf_w2, t2_w1 + t2_w3 + t2_w2, name="ffn1g_chip_add")
        reduced.update(_assemble(place, G_FFN1, rh, _run_comm(_PairShare(rh), name="ffn1g_pair_share")))
    else:
        grads["ffn1_w2"], _ = tn(u1, df1, FF // 2, D, "ffn1_dw2")
        grads["ffn1_w1"], _ = tn(da1, h1, FF // 2, D, "ffn1_dw1")
        grads["ffn1_w3"], _ = tn(db1, h1, FF // 2, D, "ffn1_dw3")
        (dx0, s1), _ = _dh_normbwd(dh_pairs, x, vec["norm_ffn1_g"], sc1, dx1, name="ffn1_bwd_dh")
        reduced = grads

    def part(s, k):
        return s[0:1, k * D:(k + 1) * D]

    dmod = jnp.concatenate([part(s1, 1), part(s1, 0), dg1[0:1], part(s2, 1), part(s2, 0), part(s_out, 0),
                            part(s3, 1), part(s3, 0), dg3[0:1]], axis=1)
    small = {"norm_ffn1_g": part(s1, 2), "norm_mix_g": part(s2, 2), "out_norm_g": part(s_out, 1),
             "norm_ffn2_g": part(s3, 2), "final_norm_g": dgfin[0:1],
             "q_norm_g": s_mid[0:1, 3 * CONV_W:3 * CONV_W + Q_LORA],
             "kv_norm_g": s_mid[0:1, 3 * CONV_W + Q_LORA:MID_SUMS], "conv_w": s_mid[0:1, 0:3 * CONV_W]}
    return loss_blk, dx0, reduced, dmod, small


SMALL = [("norm_ffn1_g", D), ("norm_mix_g", D), ("out_norm_g", D), ("norm_ffn2_g", D), ("final_norm_g", D),
         ("q_norm_g", Q_LORA), ("kv_norm_g", KV_LORA), ("conv_w", 3 * CONV_W)]
WEIGHTS = ['ada_w', 'ada_b', 'norm_ffn1_g', 'ffn1_w1', 'ffn1_w3', 'ffn1_w2', 'norm_mix_g', 'w_in', 'conv_w',
           'q_norm_g', 'w_uq', 'kv_norm_g', 'w_ukv', 'out_norm_g', 'w_out', 'norm_ffn2_g', 'ffn2_w1', 'ffn2_w3',
           'ffn2_w2', 'final_norm_g']


def kernel(x, c, positions, ada_w, ada_b, norm_ffn1_g, ffn1_w1, ffn1_w3, ffn1_w2, norm_mix_g, w_in, conv_w, q_norm_g, w_uq, kv_norm_g, w_ukv, out_norm_g, w_out, norm_ffn2_g, ffn2_w1, ffn2_w3, ffn2_w2, final_norm_g, loss_target, m_ada_w, m_ada_b, m_norm_ffn1_g, m_ffn1_w1, m_ffn1_w3, m_ffn1_w2, m_norm_mix_g, m_w_in, m_conv_w, m_q_norm_g, m_w_uq, m_kv_norm_g, m_w_ukv, m_out_norm_g, m_w_out, m_norm_ffn2_g, m_ffn2_w1, m_ffn2_w3, m_ffn2_w2, m_final_norm_g, v_ada_w, v_ada_b, v_norm_ffn1_g, v_ffn1_w1, v_ffn1_w3, v_ffn1_w2, v_norm_mix_g, v_w_in, v_conv_w, v_q_norm_g, v_w_uq, v_kv_norm_g, v_w_ukv, v_out_norm_g, v_w_out, v_norm_ffn2_g, v_ffn2_w1, v_ffn2_w3, v_ffn2_w2, v_final_norm_g):
    args = dict(locals())
    wts = {n: args[n] for n in WEIGHTS}
    mom = {n: args["m_" + n] for n in WEIGHTS}
    var = {n: args["v_" + n] for n in WEIGHTS}
    ax, ay, ac = _place()
    myq = 2 * ax + ay
    me = 2 * myq + ac
    place = jnp.stack([ac, myq]).astype(jnp.int32)

    shards = {name: _working_shard(wts[name], kind).astype(BF16) for name, kind in BULK}
    first = _run_comm(_Gather([shards[b[0]] for b in W_FIRST]), name="gather_ffn1")
    w_first = _gathered_weights(W_FIRST, shards, first, myq)

    mine = jnp.concatenate([c, conv_w[0].reshape(1, 3 * CONV_W // N_CHIPS)], axis=1)
    seen = _small_allgather(jnp.pad(mine, ((0, 7), (0, 0))), name="gather_cond").reshape(N_DEV, 8, -1)[:, 0]
    c_all = jnp.pad(seen[:, :D], ((0, 8), (0, 0)))
    conv_full = jnp.transpose(seen[0::2, D:].reshape(N_CHIPS, 3, CONV_W // N_CHIPS), (1, 0, 2)).reshape(3, CONV_W)
    ada_b_q = lax.dynamic_slice_in_dim(ada_b, myq * ADA_Q, ADA_Q, axis=1)
    mod_q = _ada_forward(c_all, ada_w[0], ada_b_q)
    mod_all = _small_allgather(mod_q, name="gather_mod").reshape(N_DEV, 16, ADA_Q)
    mod_rows = jnp.transpose(mod_all[0::2, :N_DEV], (1, 0, 2)).reshape(N_DEV, N_MOD * D)
    mod = lax.dynamic_slice_in_dim(mod_rows, me, 1, axis=0).reshape(N_MOD, D)

    vec = {n: wts[n] for n in ("norm_ffn1_g", "norm_mix_g", "q_norm_g", "kv_norm_g", "out_norm_g", "norm_ffn2_g")}
    vec["final_norm_g"] = final_norm_g.reshape(1, D)
    g, delta, new_m, new_v = {}, {}, {}, {}

    def adam(names, grads, comm=None):
        views = [(lambda a: jnp.swapaxes(a, 1, 2)) if KIND.get(n) == "colsT" else (lambda a: a) for n in names]
        res, couts = _adamw([vw(wts[n]) for n, vw in zip(names, views)], [grads[n] for n in names],
                            [vw(mom[n]) for n, vw in zip(names, views)], [vw(var[n]) for n, vw in zip(names, views)],
                            name="adamw_" + names[0], comm=comm)
        for n, vw, r in zip(names, views, res):
            g[n], delta[n], new_m[n], new_v[n] = [vw(a) for a in r]
        return couts

    loss_blk, grad_x, gq, dmod, small = _local_step(
        x[0], positions[0], loss_target[0], mod, vec, conv_full, w_first, {b[0]: shards[b[0]] for b in W_REST}, place,
        tail_host=lambda prog, grads: adam([b[0] for b in G_FFN2], grads, prog))
    loss = lax.psum(loss_blk[0, 0], ("x", "y", "c"))

    rows = jnp.concatenate([dmod] + [small[n] for n, _ in SMALL], axis=1)
    width = rows.shape[1]
    fold = -(-width // (8 * 128)) * 128
    rows = jnp.pad(rows, ((0, 0), (0, 8 * fold - width))).reshape(8, fold)
    every = _small_allgather(rows, name="gather_small").reshape(N_DEV, 8 * fold)[:, :width]
    total = _sum_devices(every)[0:1]
    dmod_q = lax.dynamic_slice_in_dim(every[:, :N_MOD * D], myq * ADA_Q, ADA_Q, axis=1)
    sg = {name: gq[name] for name, *_ in BULK}
    sg["ada_w"] = _ada_wgrad(c_all, jnp.pad(dmod_q, ((0, 8), (0, 0))))
    sg["ada_b"] = total[:, :N_MOD * D]
    off = N_MOD * D
    for n, width in SMALL:
        sg[n] = total[:, off:off + width]
        off += width
    sg["conv_w"] = lax.dynamic_slice_in_dim(sg["conv_w"].reshape(3, CONV_W), myq * (CONV_W // N_CHIPS),
                                            CONV_W // N_CHIPS, axis=1)

    for name in ["ada_w"] + [b[0] for b in BULK if b not in G_FFN2]:
        adam([name], sg)
    smalls = ["ada_b"] + [n for n, _ in SMALL]

    def packed(d):
        flat = jnp.concatenate([d[n].reshape(1, -1) for n in smalls], axis=1)
        return jnp.pad(flat.reshape(-1, D), ((0, 1), (0, 0)))

    res = _adamw([packed(wts)[None]], [packed(sg)], [packed(mom)[None]], [packed(var)[None]],
                 name="adamw_small")[0][0][1:]
    off = 0
    for n in smalls:
        size = wts[n].size
        for d, r in zip((delta, new_m, new_v), res):
            d[n] = r.reshape(-1)[off:off + size].reshape(wts[n].shape)
        g[n] = sg[n].reshape(wts[n].shape)
        off += size

    return (loss, grad_x[None], *[g[n] for n in WEIGHTS], *[delta[n] for n in WEIGHTS],
            *[new_m[n] for n in WEIGHTS], *[new_v[n] for n in WEIGHTS])
```

```python
import numpy as np
import jax
import jax.numpy as jnp
from jax import lax
from jax.experimental import pallas as pl
from jax.experimental.pallas import tpu as pltpu

F32 = jnp.float32
BF16 = jnp.bfloat16
MESH = pl.DeviceIdType.MESH

D = 1024
FF = 2816
CONV_W = 512
CONV_GROUP = 64
HEADS = 4
QK_NOPE = 128
QK_ROPE = 64
V_HEAD = 128
Q_LORA = 384
KV_LORA = 256
HEAD_SLOT = 256
IN_COLS = 3 * CONV_W + Q_LORA + KV_LORA + QK_ROPE
Z_COLS = 2304
EPS = 1e-6
ROPE_THETA = 10000.0
CHUNK = 64
ATT_SCALE = (QK_NOPE + QK_ROPE) ** -0.5
NEG = -1e30
EXP2_SCALE = ATT_SCALE * 1.4426950408889634
N_MOD = 9

LR, B1, B2, AEPS, WD, STEP = 0.001, 0.9, 0.999, 1e-08, 0.01, 10

N_CHIPS = 4
N_DEV = 8
VMEM_LIMIT = 56 << 20


def _params(sem, vmem=VMEM_LIMIT):
    return pltpu.CompilerParams(dimension_semantics=sem, vmem_limit_bytes=vmem)


def _rms(v):
    return lax.rsqrt(jnp.mean(v * v, axis=-1, keepdims=True) + EPS)


def _rsum8(v):
    t, n = v.shape
    return jnp.sum(v.reshape(t // 8, 8, n), axis=0)


def _all_rows(ref):
    ref[...] = jnp.broadcast_to(jnp.sum(ref[...], axis=0, keepdims=True), ref.shape)


def _gsum(v, gmat, split=False):
    hi = v.astype(BF16)
    out = jnp.dot(hi, gmat, preferred_element_type=F32)
    if split:
        out = out + jnp.dot((v - hi.astype(F32)).astype(BF16), gmat, preferred_element_type=F32)
    return out


def _dot_nt(a, b):
    return lax.dot_general(a, b, (((1,), (1,)), ((), ())), preferred_element_type=F32)


def _silu_parts(a):
    sg = jax.nn.sigmoid(a)
    return sg, a * sg


def _rope(xr, cs, sn, lane):
    rh = jnp.where(lane < 32, -pltpu.roll(xr, 96, 1), pltpu.roll(xr, 32, 1))
    return xr * cs + rh * sn


def _rope_t(g, cs, sn, lane):
    y = g * sn
    rt = jnp.where(lane < 32, pltpu.roll(y, 96, 1), jnp.where(lane < 64, -pltpu.roll(y, 32, 1), 0.0))
    return g * cs + rt


def _row_tile(rows, pref, mult=8):
    t = min(rows, pref) // mult * mult
    while rows % t:
        t -= mult
    return t


def _place():
    return lax.axis_index("x"), lax.axis_index("y"), lax.axis_index("c")


ANY = pl.BlockSpec(memory_space=pl.ANY)


def _hosted_call(body, *, name, grid, in_specs, out_specs, out_shape, scratch_shapes, semantics, args, comm=None,
                 prefetch=()):
    n_in, n_out, n_scr, n_pf = len(in_specs), len(out_specs), len(scratch_shapes), len(prefetch)

    def call(fn, in_specs, out_specs, out_shape, scratch_shapes, semantics, operands):
        spec = pltpu.PrefetchScalarGridSpec(num_scalar_prefetch=n_pf, grid=grid, in_specs=list(in_specs),
                                            out_specs=list(out_specs), scratch_shapes=list(scratch_shapes))
        return pl.pallas_call(fn, name=name, grid_spec=spec, out_shape=list(out_shape),
                              compiler_params=_params(semantics))(*prefetch, *operands)

    if comm is None:
        return list(call(body, in_specs, out_specs, out_shape, scratch_shapes, semantics, args)), []
    n_ci, n_co = len(comm.inputs), len(comm.out_shapes)
    total = int(np.prod(grid))

    def hosted(*refs):
        tables, refs = refs[:n_pf], refs[n_pf:]
        ins, refs = refs[:n_in], refs[n_in:]
        cins, refs = refs[:n_ci], refs[n_ci:]
        outs, refs = refs[:n_out], refs[n_out:]
        couts, refs = refs[:n_co], refs[n_co:]
        scratch, sems = refs[:n_scr], refs[n_scr]
        step = pl.program_id(0)
        for ax in range(1, len(grid)):
            step = step * grid[ax] + pl.program_id(ax)

        @pl.when(step == 0)
        def _():
            comm.start(cins, couts, sems)

        body(*tables, *ins, *outs, *scratch)

        @pl.when(step == total - 1)
        def _():
            comm.finish(cins, couts, sems)

    res = call(hosted, list(in_specs) + [ANY] * n_ci, list(out_specs) + [ANY] * n_co,
               list(out_shape) + list(comm.out_shapes),
               list(scratch_shapes) + [pltpu.SemaphoreType.DMA((comm.n_sems,))],
               ("arbitrary",) * len(grid), (*args, *comm.inputs))
    return list(res[:n_out]), list(res[n_out:])


def _run_comm(comm, *, name):
    n_ci = len(comm.inputs)

    def body(*refs):
        cins, couts, sems = refs[:n_ci], refs[n_ci:-1], refs[-1]
        comm.start(cins, couts, sems)
        comm.finish(cins, couts, sems)

    return list(pl.pallas_call(
        body, name=name, out_shape=list(comm.out_shapes), in_specs=[ANY] * n_ci,
        out_specs=[ANY] * len(comm.out_shapes), scratch_shapes=[pltpu.SemaphoreType.DMA((comm.n_sems,))],
    )(*comm.inputs))


class _Gather:
    def __init__(self, slabs):
        self.inputs = list(slabs)
        self.out_shapes = [jax.ShapeDtypeStruct((N_CHIPS,) + s.shape, s.dtype) for s in slabs]
        self.n_sems = 12 * len(slabs)

    @staticmethod
    def _copy(out, sems, base, k, chip, hc, to, src=None):
        H = out.shape[1] // 2
        half = out.at[2 * chip[0] + chip[1], pl.ds(hc * H, H), :]
        return pltpu.make_async_remote_copy(
            src_ref=half if src is None else src, dst_ref=half, send_sem=sems.at[base + k],
            recv_sem=sems.at[base + 6 + k], device_id=to, device_id_type=MESH)

    def _firsts(self, src, out, sems, base):
        x, y, c = _place()
        H = src.shape[0] // 2
        chips = [(1 - x, y), (x, 1 - y), (1 - x, 1 - y)]
        return [self._copy(out, sems, base, j, (x, y), c, (*chip, c), src=src.at[pl.ds(c * H, H), :])
                for j, chip in enumerate(chips)]

    def start(self, ins, outs, sems):
        for i, (src, out) in enumerate(zip(ins, outs)):
            for cp in self._firsts(src, out, sems, 12 * i):
                cp.start()

    def finish(self, ins, outs, sems):
        x, y, c = _place()
        chips = [(1 - x, y), (x, 1 - y), (1 - x, 1 - y)]
        passed = []
        for i, out in enumerate(outs):
            for j, chip in enumerate(chips):
                self._copy(out, sems, 12 * i, j, chip, c, (x, y, c)).wait_recv()
                cp = self._copy(out, sems, 12 * i, 3 + j, chip, c, (x, y, 1 - c))
                cp.start()
                passed.append(cp)
        for i, out in enumerate(outs):
            for j, chip in enumerate(chips):
                self._copy(out, sems, 12 * i, 3 + j, chip, 1 - c, (x, y, c)).wait_recv()
        for cp in passed:
            cp.wait_send()
        for i, (src, out) in enumerate(zip(ins, outs)):
            for cp in self._firsts(src, out, sems, 12 * i):
                cp.wait_send()


class _PairExchange:
    def __init__(self, arrays):
        self.inputs = list(arrays)
        self.out_shapes = [jax.ShapeDtypeStruct((N_CHIPS, a.shape[1] // 2, a.shape[2]), a.dtype) for a in arrays]
        self.n_sems = 2 * len(arrays)

    def _copies(self, ins, outs, sems):
        x, y, c = _place()
        return [pltpu.make_async_remote_copy(
            src_ref=g.at[:, pl.ds((1 - c) * t.shape[1], t.shape[1]), :], dst_ref=t, send_sem=sems.at[2 * i],
            recv_sem=sems.at[2 * i + 1], device_id=(x, y, 1 - c), device_id_type=MESH)
            for i, (g, t) in enumerate(zip(ins, outs))]

    def start(self, ins, outs, sems):
        for cp in self._copies(ins, outs, sems):
            cp.start()

    def finish(self, ins, outs, sems):
        for cp in self._copies(ins, outs, sems):
            cp.wait()


class _ChipExchange:
    def __init__(self, arrays):
        self.inputs = list(arrays)
        self.out_shapes = [jax.ShapeDtypeStruct(a.shape, a.dtype) for a in arrays]
        self.n_sems = 6 * len(arrays)

    def _copies(self, p, t, sems, base):
        x, y, c = _place()
        myq = 2 * x + y
        chips = [(1 - x, y), (x, 1 - y), (1 - x, 1 - y)]
        sends = [pltpu.make_async_remote_copy(
            src_ref=p.at[2 * chip[0] + chip[1]], dst_ref=t.at[myq], send_sem=sems.at[base + j],
            recv_sem=sems.at[base + 3 + j], device_id=(*chip, c), device_id_type=MESH) for j, chip in enumerate(chips)]
        lands = [pltpu.make_async_remote_copy(
            src_ref=t.at[2 * chip[0] + chip[1]], dst_ref=t.at[2 * chip[0] + chip[1]], send_sem=sems.at[base + j],
            recv_sem=sems.at[base + 3 + j], device_id=(*chip, c), device_id_type=MESH) for j, chip in enumerate(chips)]
        return sends, lands

    def start(self, ins, outs, sems):
        for i, (p, t) in enumerate(zip(ins, outs)):
            for cp in self._copies(p, t, sems, 6 * i)[0]:
                cp.start()

    def finish(self, ins, outs, sems):
        for i, (p, t) in enumerate(zip(ins, outs)):
            sends, lands = self._copies(p, t, sems, 6 * i)
            for cp in lands:
                cp.wait_recv()
            for cp in sends:
                cp.wait_send()


class _SemView:
    def __init__(self, sems, base):
        self._sems, self._base = sems, base

    @property
    def at(self):
        return self

    def __getitem__(self, k):
        return self._sems.at[self._base + k]


class _Multi:
    def __init__(self, progs):
        self.progs = list(progs)
        self.inputs = [a for p in self.progs for a in p.inputs]
        self.out_shapes = [s for p in self.progs for s in p.out_shapes]
        self.n_sems = sum(p.n_sems for p in self.progs)

    def _each(self, ins, outs, sems):
        i = o = s = 0
        for p in self.progs:
            ni, no = len(p.inputs), len(p.out_shapes)
            yield p, ins[i:i + ni], outs[o:o + no], _SemView(sems, s)
            i, o, s = i + ni, o + no, s + p.n_sems

    def start(self, ins, outs, sems):
        for p, a, b, c in self._each(ins, outs, sems):
            p.start(a, b, c)

    def finish(self, ins, outs, sems):
        for p, a, b, c in self._each(ins, outs, sems):
            p.finish(a, b, c)


class _PairShare:
    def __init__(self, arrays):
        self.inputs = list(arrays)
        self.out_shapes = [jax.ShapeDtypeStruct(a.shape, a.dtype) for a in arrays]
        self.n_sems = 2 * len(arrays)

    def _copies(self, ins, outs, sems):
        x, y, c = _place()
        return [pltpu.make_async_remote_copy(
            src_ref=r, dst_ref=o, send_sem=sems.at[2 * i], recv_sem=sems.at[2 * i + 1],
            device_id=(x, y, 1 - c), device_id_type=MESH) for i, (r, o) in enumerate(zip(ins, outs))]

    def start(self, ins, outs, sems):
        for cp in self._copies(ins, outs, sems):
            cp.start()

    def finish(self, ins, outs, sems):
        for cp in self._copies(ins, outs, sems):
            cp.wait()


def _ffn_up(x, ng, sh, sc, w1, w3, *, name, comm=None):
    S = x.shape[0]
    tm, tn = _row_tile(S, 512), FF

    def body(x_ref, g_ref, sh_ref, sc_ref, w1_ref, w3_ref, h_ref, a_ref, b_ref, u_ref, hs):
        @pl.when(pl.program_id(1) == 0)
        def _():
            xv = x_ref[...]
            h = ((xv * _rms(xv)) * g_ref[...]) * (1.0 + sc_ref[...]) + sh_ref[...]
            hb = h.astype(BF16)
            hs[...] = hb
            h_ref[...] = hb

        h = hs[...]
        cols = pl.ds(pl.multiple_of(pl.program_id(1) * tn, tn), tn)
        a = _dot_nt(h, w1_ref[cols, :])
        b = _dot_nt(h, w3_ref[cols, :])
        _, sa = _silu_parts(a)
        a_ref[...] = a.astype(BF16)
        b_ref[...] = b.astype(BF16)
        u_ref[...] = (sa * b).astype(BF16)

    row = pl.BlockSpec((tm, D), lambda i, j: (i, 0))
    vec = pl.BlockSpec((1, D), lambda i, j: (0, 0))
    wsp = pl.BlockSpec((FF, D), lambda i, j: (0, 0))
    osp = pl.BlockSpec((tm, tn), lambda i, j: (i, j))
    return _hosted_call(
        body, name=name, grid=(S // tm, FF // tn),
        in_specs=[row, vec, vec, vec, wsp, wsp],
        out_specs=[row, osp, osp, osp],
        out_shape=[jax.ShapeDtypeStruct((S, D), BF16)] + [jax.ShapeDtypeStruct((S, FF), BF16)] * 3,
        scratch_shapes=[pltpu.VMEM((tm, D), BF16)],
        semantics=("parallel", "arbitrary"), args=(x, ng, sh, sc, w1, w3), comm=comm)


def _ffn_down(u, w2, x, gate, *, name, comm=None):
    S = x.shape[0]
    tm = _row_tile(S, 1024)

    def body(u_ref, w2_ref, x_ref, g_ref, xo_ref, f_ref):
        f = jnp.dot(u_ref[...], w2_ref[...], preferred_element_type=F32)
        xo_ref[...] = x_ref[...] + (0.5 * g_ref[...]) * f
        f_ref[...] = f.astype(BF16)

    return _hosted_call(
        body, name=name, grid=(S // tm,),
        in_specs=[pl.BlockSpec((tm, FF), lambda i: (i, 0)), pl.BlockSpec((FF, D), lambda i: (0, 0)),
                  pl.BlockSpec((tm, D), lambda i: (i, 0)), pl.BlockSpec((1, D), lambda i: (0, 0))],
        out_specs=[pl.BlockSpec((tm, D), lambda i: (i, 0))] * 2,
        out_shape=[jax.ShapeDtypeStruct((S, D), F32), jax.ShapeDtypeStruct((S, D), BF16)],
        scratch_shapes=[], semantics=("parallel",), args=(u, w2, x, gate), comm=comm)


def _ffn_bwd_du(dx, gate, f, w2, a, b, *, name, comm=None):
    S = dx.shape[0]
    tm, tn = _row_tile(S, 512), FF
    n_i = S // tm

    def body(dx_ref, g_ref, f_ref, w_ref, a_ref, b_ref, df_ref, da_ref, db_ref, dg_ref, dfs):
        i, j = pl.program_id(0), pl.program_id(1)

        @pl.when((i == 0) & (j == 0))
        def _():
            dg_ref[...] = jnp.zeros_like(dg_ref)

        @pl.when(j == 0)
        def _():
            dxv = dx_ref[...]
            dfb = (dxv * (0.5 * g_ref[...])).astype(BF16)
            dfs[...] = dfb
            df_ref[...] = dfb
            dg_ref[...] += _rsum8(dxv * (0.5 * f_ref[...].astype(F32)))

        du = _dot_nt(dfs[...], w_ref[pl.ds(pl.multiple_of(j * tn, tn), tn), :])
        av = a_ref[...].astype(F32)
        sg, sa = _silu_parts(av)
        da_ref[...] = (du * b_ref[...].astype(F32) * (sg * (1.0 + av * (1.0 - sg)))).astype(BF16)
        db_ref[...] = (du * sa).astype(BF16)

        @pl.when((i == n_i - 1) & (j == FF // tn - 1))
        def _():
            _all_rows(dg_ref)

    row = pl.BlockSpec((tm, D), lambda i, j: (i, 0))
    blk = pl.BlockSpec((tm, tn), lambda i, j: (i, j))
    return _hosted_call(
        body, name=name, grid=(n_i, FF // tn),
        in_specs=[row, pl.BlockSpec((1, D), lambda i, j: (0, 0)), row,
                  pl.BlockSpec((FF, D), lambda i, j: (0, 0)), blk, blk],
        out_specs=[row, blk, blk, pl.BlockSpec((8, D), lambda i, j: (0, 0))],
        out_shape=[jax.ShapeDtypeStruct((S, D), BF16), jax.ShapeDtypeStruct((S, FF), BF16),
                   jax.ShapeDtypeStruct((S, FF), BF16), jax.ShapeDtypeStruct((8, D), F32)],
        scratch_shapes=[pltpu.VMEM((tm, D), BF16)],
        semantics=("arbitrary", "arbitrary"), args=(dx, gate, f, w2, a, b), comm=comm)


def _tn_matmul(a, b, *, tm, tn, name, comm=None):
    S, M = a.shape
    N = b.shape[1]
    ts = _row_tile(S, 2048 if tm * tn <= 1408 * 1024 else 1024)
    ns = S // ts

    def body(a_ref, b_ref, o_ref):
        s = pl.program_id(2)
        p = lax.dot_general(a_ref[...], b_ref[...], (((0,), (0,)), ((), ())), preferred_element_type=F32)

        @pl.when(s == 0)
        def _():
            o_ref[...] = p

        @pl.when(s > 0)
        def _():
            o_ref[...] += p

    (out,), couts = _hosted_call(
        body, name=name, grid=(M // tm, N // tn, ns),
        in_specs=[pl.BlockSpec((ts, tm), lambda i, j, s: (s, i)), pl.BlockSpec((ts, tn), lambda i, j, s: (s, j))],
        out_specs=[pl.BlockSpec((tm, tn), lambda i, j, s: (i, j))],
        out_shape=[jax.ShapeDtypeStruct((M, N), F32)],
        scratch_shapes=[], semantics=("parallel", "parallel", "arbitrary"), args=(a, b), comm=comm)
    return out if comm is None else (out, couts)


def _dh_normbwd(pairs, x, ng, sc, dx_next, *, name, comm=None):
    S = x.shape[0]
    n_p = len(pairs)
    tm = _row_tile(S, 512)
    n_i = S // tm

    def body(*refs):
        a_refs, w_refs = refs[:n_p], refs[n_p:2 * n_p]
        x_ref, g_ref, sc_ref, dxn_ref, dx_ref, p_ref = refs[2 * n_p:]
        i = pl.program_id(0)
        dh = jnp.dot(a_refs[0][...], w_refs[0][...], preferred_element_type=F32)
        for k in range(1, n_p):
            dh = dh + jnp.dot(a_refs[k][...], w_refs[k][...], preferred_element_type=F32)
        xv = x_ref[...]
        r = _rms(xv)
        xh = xv * r
        g = g_ref[...]
        dn = dh * (1.0 + sc_ref[...])
        dy = dn * g
        dx_ref[...] = dxn_ref[...] + r * (dy - xh * jnp.mean(dy * xh, axis=-1, keepdims=True))

        @pl.when(i == 0)
        def _():
            p_ref[...] = jnp.zeros_like(p_ref)

        p_ref[:, 0:D] += _rsum8(dh * (xh * g))
        p_ref[:, D:2 * D] += _rsum8(dh)
        p_ref[:, 2 * D:3 * D] += _rsum8(dn * xh)

        @pl.when(i == n_i - 1)
        def _():
            _all_rows(p_ref)

    row = pl.BlockSpec((tm, D), lambda i: (i, 0))
    vec = pl.BlockSpec((1, D), lambda i: (0, 0))
    in_specs = ([pl.BlockSpec((tm, a.shape[1]), lambda i: (i, 0)) for a, _ in pairs]
                + [pl.BlockSpec(w.shape, lambda i: (0, 0), pipeline_mode=pl.Buffered(1)) for _, w in pairs]
                + [row, vec, vec, row])
    return _hosted_call(
        body, name=name, grid=(n_i,), in_specs=in_specs,
        out_specs=[row, pl.BlockSpec((8, 3 * D), lambda i: (0, 0))],
        out_shape=[jax.ShapeDtypeStruct((S, D), F32), jax.ShapeDtypeStruct((8, 3 * D), F32)],
        scratch_shapes=[], semantics=("arbitrary",),
        args=(*[a for a, _ in pairs], *[w for _, w in pairs], x, ng, sc, dx_next), comm=comm)


def _ffn_down_loss(u, w2, x, gate, gfin, tgt):
    S = x.shape[0]
    tm = _row_tile(S, 512)
    n_i = S // tm

    def body(u_ref, w2_ref, x_ref, gt_ref, g_ref, t_ref, dx_ref, f_ref, dg_ref, loss_ref, lacc):
        i = pl.program_id(0)
        f = jnp.dot(u_ref[...], w2_ref[...], preferred_element_type=F32)
        f_ref[...] = f.astype(BF16)
        xv = x_ref[...] + (0.5 * gt_ref[...]) * f
        r = _rms(xv)
        xh = xv * r
        g = g_ref[...]
        e = xh * g - t_ref[...]
        dout = e * (1.0 / D)
        dy = dout * g
        dx_ref[...] = r * (dy - xh * jnp.mean(dy * xh, axis=-1, keepdims=True))

        @pl.when(i == 0)
        def _():
            dg_ref[...] = jnp.zeros_like(dg_ref)
            lacc[...] = jnp.zeros_like(lacc)

        dg_ref[...] += _rsum8(dout * xh)
        lacc[...] += _rsum8(e * e)

        @pl.when(i == n_i - 1)
        def _():
            _all_rows(dg_ref)
            tot = jnp.sum(jnp.sum(lacc[...], axis=0, keepdims=True), axis=1, keepdims=True)
            loss_ref[...] = jnp.broadcast_to(tot * (0.5 / D), loss_ref.shape)

    row = pl.BlockSpec((tm, D), lambda i: (i, 0))
    vec = pl.BlockSpec((1, D), lambda i: (0, 0))
    return pl.pallas_call(
        body, name="ffn2_down_loss", grid=(n_i,),
        in_specs=[pl.BlockSpec((tm, FF), lambda i: (i, 0)), pl.BlockSpec((FF, D), lambda i: (0, 0)), row, vec, vec, row],
        out_specs=[row, row, pl.BlockSpec((8, D), lambda i: (0, 0)), pl.BlockSpec((8, 128), lambda i: (0, 0))],
        out_shape=[jax.ShapeDtypeStruct((S, D), F32), jax.ShapeDtypeStruct((S, D), BF16),
                   jax.ShapeDtypeStruct((8, D), F32), jax.ShapeDtypeStruct((8, 128), F32)],
        scratch_shapes=[pltpu.VMEM((8, D), F32)],
        compiler_params=_params(("arbitrary",)),
    )(u, w2, x, gate, gfin, tgt)


def _mix_in(x, ng, sh, sc, w_in, comm=None):
    S = x.shape[0]
    tm = _row_tile(S, 512)

    def body(x_ref, g_ref, sh_ref, sc_ref, w_ref, h_ref, z_ref):
        xv = x_ref[...]
        hb = (((xv * _rms(xv)) * g_ref[...]) * (1.0 + sc_ref[...]) + sh_ref[...]).astype(BF16)
        h_ref[...] = hb
        z_ref[...] = _dot_nt(hb, w_ref[...])

    row = pl.BlockSpec((tm, D), lambda i: (i, 0))
    vec = pl.BlockSpec((1, D), lambda i: (0, 0))
    return _hosted_call(
        body, name="mix_in", grid=(S // tm,),
        in_specs=[row, vec, vec, vec, pl.BlockSpec((Z_COLS, D), lambda i: (0, 0))],
        out_specs=[row, pl.BlockSpec((tm, Z_COLS), lambda i: (i, 0))],
        out_shape=[jax.ShapeDtypeStruct((S, D), BF16), jax.ShapeDtypeStruct((S, Z_COLS), F32)],
        scratch_shapes=[], semantics=("parallel",), args=(x, ng, sh, sc, w_in), comm=comm)


def _conv_taps(u, halo, rows):
    u1 = jnp.where(rows == 0, halo[7:8, :], pltpu.roll(u, 1, 0))
    u2 = jnp.where(rows == 0, halo[6:7, :], jnp.where(rows == 1, halo[7:8, :], pltpu.roll(u, 2, 0)))
    return u1, u2


def _mix_mid(z, conv_w, gq, gkv, wuq, wukv, cs, sn, comm=None):
    S = z.shape[0]
    tm = _row_tile(S, 512)
    hb = tm // 8

    def body(z_ref, zh_ref, cw_ref, gq_ref, gkv_ref, wuq_ref, wukv_ref, cs_ref, sn_ref,
             ya_ref, q_ref, k_ref, v_ref, cqn_ref, ckvn_ref):
        i = pl.program_id(0)
        xb = z_ref[:, 0:CONV_W]
        u = z_ref[:, CONV_W:2 * CONV_W] * z_ref[:, 2 * CONV_W:3 * CONV_W]
        halo = zh_ref[:, CONV_W:2 * CONV_W] * zh_ref[:, 2 * CONV_W:3 * CONV_W]
        halo = jnp.where(i > 0, halo, 0.0)
        rows = lax.broadcasted_iota(jnp.int32, (tm, CONV_W), 0)
        u1, u2 = _conv_taps(u, halo, rows)
        y = cw_ref[0:1, :] * u2 + cw_ref[1:2, :] * u1 + cw_ref[2:3, :] * u
        ya_ref[...] = xb * y

        lane = lax.broadcasted_iota(jnp.int32, (tm, 128), 1)
        cs_v, sn_v = cs_ref[...], sn_ref[...]
        cq = z_ref[:, 3 * CONV_W:3 * CONV_W + Q_LORA]
        cqn = ((cq * _rms(cq)) * gq_ref[...]).astype(BF16)
        cqn_ref[...] = cqn
        q = _dot_nt(cqn, wuq_ref[...])
        for h in range(HEADS):
            o = h * HEAD_SLOT
            q_ref[:, o:o + 128] = q[:, o:o + 128].astype(BF16)
            q_ref[:, o + 128:o + 256] = _rope(q[:, o + 128:o + 256], cs_v, sn_v, lane).astype(BF16)

        c0 = 3 * CONV_W + Q_LORA
        ckv = z_ref[:, c0:c0 + KV_LORA]
        ckvn = ((ckv * _rms(ckv)) * gkv_ref[...]).astype(BF16)
        ckvn_ref[...] = ckvn
        kv = jnp.dot(ckvn, wukv_ref[...], preferred_element_type=F32)
        krot = _rope(z_ref[:, c0 + KV_LORA:Z_COLS], cs_v, sn_v, lane).astype(BF16)
        for h in range(HEADS):
            o = h * HEAD_SLOT
            k_ref[:, o:o + 128] = kv[:, h * 128:(h + 1) * 128].astype(BF16)
            k_ref[:, o + 128:o + 256] = krot
        v_ref[...] = kv[:, HEADS * 128:].astype(BF16)

    def rows_of(n):
        return pl.BlockSpec((tm, n), lambda i: (i, 0))

    def whole(shape):
        return pl.BlockSpec(shape, lambda i: (0, 0))

    return _hosted_call(
        body, name="mix_mid", grid=(S // tm,),
        in_specs=[rows_of(Z_COLS), pl.BlockSpec((8, Z_COLS), lambda i: (jnp.maximum(i * hb - 1, 0), 0)),
                  whole((8, CONV_W)), whole((1, Q_LORA)), whole((1, KV_LORA)),
                  whole((HEADS * HEAD_SLOT, Q_LORA)), whole((KV_LORA, 2 * HEADS * 128)),
                  rows_of(128), rows_of(128)],
        out_specs=[rows_of(CONV_W), rows_of(HEADS * HEAD_SLOT), rows_of(HEADS * HEAD_SLOT), rows_of(HEADS * V_HEAD),
                   rows_of(Q_LORA), rows_of(KV_LORA)],
        out_shape=[jax.ShapeDtypeStruct((S, CONV_W), F32), jax.ShapeDtypeStruct((S, HEADS * HEAD_SLOT), BF16),
                   jax.ShapeDtypeStruct((S, HEADS * HEAD_SLOT), BF16), jax.ShapeDtypeStruct((S, HEADS * V_HEAD), BF16),
                   jax.ShapeDtypeStruct((S, Q_LORA), BF16), jax.ShapeDtypeStruct((S, KV_LORA), BF16)],
        scratch_shapes=[], semantics=("parallel",), args=(z, z, conv_w, gq, gkv, wuq, wukv, cs, sn), comm=comm)


def _att_blocks(S):
    bk = min(1024, max(S // 4, 128))
    return bk, bk


def _pair_tables(S, k_major):
    bq, bk = _att_blocks(S)
    nq, nk = S // bq, S // bk
    vis = lambda qi, ki: ki * bk < (qi + 1) * bq
    if k_major:
        pairs = [(qi, ki) for ki in range(nk) for qi in range(nq) if vis(qi, ki)]
    else:
        pairs = [(qi, ki) for qi in range(nq) for ki in range(nk) if vis(qi, ki)]
    cols = [[p[0] for p in pairs], [p[1] for p in pairs], [int((p[1] + 1) * bk > p[0] * bq) for p in pairs]]
    return [jnp.asarray(np.array(c, np.int32)) for c in cols], len(pairs)


def _chunk_mask(r0, nr, nc):
    r = (r0 + lax.broadcasted_iota(jnp.int32, (nr, nc), 0)) // CHUNK
    c = lax.broadcasted_iota(jnp.int32, (nr, nc), 1) // CHUNK
    return c <= r


def _diag_parts(bq, bk):
    return [(0, bq // 2, bk // 2), (bq // 2, bq // 2, bk)]


def _attention(q, k, v, comm=None):
    S = q.shape[0]
    bq, bk = _att_blocks(S)
    last_k = bq // bk - 1
    tables, n_pairs = _pair_tables(S, k_major=False)

    def body(qi_ref, ki_ref, mk_ref, q_ref, k_ref, v_ref, o_ref, lse_ref, m_s, l_s, acc_s):
        p_id = pl.program_id(1)
        qi, ki = qi_ref[p_id], ki_ref[p_id]

        @pl.when(ki == 0)
        def _():
            m_s[...] = jnp.full_like(m_s, NEG)
            l_s[...] = jnp.zeros_like(l_s)
            acc_s[...] = jnp.zeros_like(acc_s)

        def update(r0, nr, nc, masked):
            rows = slice(r0, r0 + nr)
            s = _dot_nt(q_ref[rows, :], k_ref[0:nc, :])
            if masked:
                s = jnp.where(_chunk_mask(r0, nr, nc), s, NEG)
            m_prev = m_s[rows, :]
            m_new = jnp.maximum(m_prev, jnp.max(s, axis=1, keepdims=True))
            alpha = jnp.exp2((m_prev - m_new) * EXP2_SCALE)
            p = jnp.exp2((s - jnp.tile(m_new, (1, nc // 128))) * EXP2_SCALE)
            l_s[rows, :] = alpha * l_s[rows, :] + jnp.sum(p, axis=1, keepdims=True)
            acc_s[rows, :] = alpha * acc_s[rows, :] + jnp.dot(p.astype(BF16), v_ref[0:nc, :],
                                                              preferred_element_type=F32)
            m_s[rows, :] = m_new

        @pl.when(mk_ref[p_id] == 0)
        def _():
            update(0, bq, bk, False)

        @pl.when(mk_ref[p_id] == 1)
        def _():
            for part in _diag_parts(bq, bk):
                update(*part, True)

        @pl.when(ki == qi * (last_k + 1) + last_k)
        def _():
            l = l_s[...]
            o_ref[...] = acc_s[...] / l
            lse_ref[...] = m_s[...] * EXP2_SCALE + jnp.log2(l)

    return _hosted_call(
        body, name="attention", grid=(HEADS, n_pairs),
        in_specs=[pl.BlockSpec((bq, HEAD_SLOT), lambda h, p, qt, kt, mt: (qt[p], h)),
                  pl.BlockSpec((bk, HEAD_SLOT), lambda h, p, qt, kt, mt: (kt[p], h)),
                  pl.BlockSpec((bk, V_HEAD), lambda h, p, qt, kt, mt: (kt[p], h))],
        out_specs=[pl.BlockSpec((bq, V_HEAD), lambda h, p, qt, kt, mt: (qt[p], h))] * 2,
        out_shape=[jax.ShapeDtypeStruct((S, HEADS * V_HEAD), F32)] * 2,
        scratch_shapes=[pltpu.VMEM((bq, V_HEAD), F32)] * 3,
        semantics=("arbitrary", "arbitrary"), args=(q, k, v), comm=comm, prefetch=tables)


def _attention_bwd(q, k, v, do, lse2, delta):
    S = q.shape[0]
    bq, bk = _att_blocks(S)
    nq = S // bq
    tables, n_pairs = _pair_tables(S, k_major=True)

    def body(qi_ref, ki_ref, mk_ref, q_ref, k_ref, v_ref, do_ref, lse_ref, dl_ref, dq_hbm, dk_ref, dv_ref,
             dq_s, dq_b, dk_s, dv_s, sem):
        head, p_id = pl.program_id(0), pl.program_id(1)
        qi, ki = qi_ref[p_id], ki_ref[p_id]

        @pl.when(qi * bq <= ki * bk)
        def _():
            dk_s[...] = jnp.zeros_like(dk_s)
            dv_s[...] = jnp.zeros_like(dv_s)

        def update(r0, nr, nc, masked):
            rows, cols = slice(r0, r0 + nr), slice(0, nc)
            qv, kv, dov = q_ref[rows, :], k_ref[cols, :], do_ref[rows, :]
            s = _dot_nt(qv, kv)
            dp = _dot_nt(dov, v_ref[cols, :])
            if masked:
                s = jnp.where(_chunk_mask(r0, nr, nc), s, NEG)
            p = jnp.exp2(s * EXP2_SCALE - jnp.tile(lse_ref[rows, :], (1, nc // 128)))
            dv_s[cols, :] += lax.dot_general(p.astype(BF16), dov, (((0,), (0,)), ((), ())),
                                             preferred_element_type=F32)
            ds = (p * (dp - jnp.tile(dl_ref[rows, :], (1, nc // 128)))).astype(BF16)
            dk_s[cols, :] += lax.dot_general(ds, qv, (((0,), (0,)), ((), ())), preferred_element_type=F32)
            dq = jnp.dot(ds, kv, preferred_element_type=F32)
            out_rows = pl.ds(pl.multiple_of(qi * bq + r0, nr), nr)

            @pl.when(ki == 0)
            def _():
                dq_s[out_rows, :] = dq

            @pl.when(ki > 0)
            def _():
                dq_s[out_rows, :] += dq

        @pl.when(mk_ref[p_id] == 0)
        def _():
            update(0, bq, bk, False)

        @pl.when(mk_ref[p_id] == 1)
        def _():
            for part in _diag_parts(bq, bk):
                update(*part, True)

        @pl.when(qi == nq - 1)
        def _():
            dk_ref[...] = (dk_s[...] * ATT_SCALE).astype(BF16)
            dv_ref[...] = dv_s[...].astype(BF16)

        @pl.when(p_id == n_pairs - 1)
        def _():
            dq_b[...] = (dq_s[...] * ATT_SCALE).astype(BF16)
            out = pltpu.make_async_copy(
                dq_b, dq_hbm.at[:, pl.ds(pl.multiple_of(head * HEAD_SLOT, HEAD_SLOT), HEAD_SLOT)], sem)
            out.start()
            out.wait()

    grid_spec = pltpu.PrefetchScalarGridSpec(
        num_scalar_prefetch=3, grid=(HEADS, n_pairs),
        in_specs=[pl.BlockSpec((bq, HEAD_SLOT), lambda h, p, qt, kt, mt: (qt[p], h)),
                  pl.BlockSpec((bk, HEAD_SLOT), lambda h, p, qt, kt, mt: (kt[p], h)),
                  pl.BlockSpec((bk, V_HEAD), lambda h, p, qt, kt, mt: (kt[p], h)),
                  pl.BlockSpec((bq, V_HEAD), lambda h, p, qt, kt, mt: (qt[p], h)),
                  pl.BlockSpec((bq, V_HEAD), lambda h, p, qt, kt, mt: (qt[p], h)),
                  pl.BlockSpec((bq, V_HEAD), lambda h, p, qt, kt, mt: (qt[p], h))],
        out_specs=[pl.BlockSpec(memory_space=pl.ANY),
                   pl.BlockSpec((bk, HEAD_SLOT), lambda h, p, qt, kt, mt: (kt[p], h)),
                   pl.BlockSpec((bk, V_HEAD), lambda h, p, qt, kt, mt: (kt[p], h))],
        scratch_shapes=[pltpu.VMEM((S, HEAD_SLOT), F32), pltpu.VMEM((S, HEAD_SLOT), BF16),
                        pltpu.VMEM((bk, HEAD_SLOT), F32), pltpu.VMEM((bk, V_HEAD), F32), pltpu.SemaphoreType.DMA])
    return pl.pallas_call(
        body, name="attention_bwd", grid_spec=grid_spec,
        out_shape=[jax.ShapeDtypeStruct((S, HEADS * HEAD_SLOT), BF16), jax.ShapeDtypeStruct((S, HEADS * HEAD_SLOT), BF16),
                   jax.ShapeDtypeStruct((S, HEADS * V_HEAD), BF16)],
        compiler_params=_params(("arbitrary", "arbitrary")),
    )(*tables, q, k, v, do, lse2, delta)


def _group_mats():
    def blockdiag(n, g):
        idx = np.arange(n) // g
        return jnp.asarray((idx[:, None] == idx[None, :]).astype(np.float32), dtype=BF16)
    return blockdiag(CONV_W, CONV_GROUP), blockdiag(HEADS * V_HEAD, V_HEAD)


def _mix_out(ya, o, gout, w_out, x, gate, ga, gb):
    S = x.shape[0]
    tm = _row_tile(S, 512)

    def body(ya_ref, o_ref, go_ref, w_ref, x_ref, g_ref, ga_ref, gb_ref, xo_ref, yn_ref, yo_ref):
        yav, ov = ya_ref[...], o_ref[...]
        ra = lax.rsqrt(_gsum(yav * yav, ga_ref[...]) * (1.0 / CONV_GROUP) + EPS)
        rb = lax.rsqrt(_gsum(ov * ov, gb_ref[...]) * (1.0 / V_HEAD) + EPS)
        na = ((yav * ra) * go_ref[:, 0:CONV_W]).astype(BF16)
        nb = ((ov * rb) * go_ref[:, CONV_W:]).astype(BF16)
        yn_ref[:, 0:CONV_W] = na
        yn_ref[:, CONV_W:] = nb
        yo = (jnp.dot(na, w_ref[0:CONV_W, :], preferred_element_type=F32)
              + jnp.dot(nb, w_ref[CONV_W:, :], preferred_element_type=F32))
        xo_ref[...] = x_ref[...] + g_ref[...] * yo
        yo_ref[...] = yo.astype(BF16)

    row = pl.BlockSpec((tm, D), lambda i: (i, 0))
    half = pl.BlockSpec((tm, CONV_W), lambda i: (i, 0))
    vec = pl.BlockSpec((1, D), lambda i: (0, 0))
    sq = pl.BlockSpec((CONV_W, CONV_W), lambda i: (0, 0))
    return pl.pallas_call(
        body, name="mix_out", grid=(S // tm,),
        in_specs=[half, half, vec, pl.BlockSpec((D, D), lambda i: (0, 0)), row, vec, sq, sq],
        out_specs=[row, row, row],
        out_shape=[jax.ShapeDtypeStruct((S, D), F32), jax.ShapeDtypeStruct((S, D), BF16),
                   jax.ShapeDtypeStruct((S, D), BF16)],
        compiler_params=_params(("parallel",)),
    )(ya, o, gout, w_out, x, gate, ga, gb)


def _mix_out_bwd(dx, gate, yo, w_out, ya, o, gout, ga, gb, comm=None):
    S = dx.shape[0]
    tm = _row_tile(S, 512)
    n_i = S // tm

    def norm_bwd(v, dn, gain, gmat, inv_n):
        r = lax.rsqrt(_gsum(v * v, gmat) * inv_n + EPS)
        vh = v * r
        dy = dn * gain
        return r * (dy - vh * (_gsum(dy * vh, gmat) * inv_n)), dn * vh

    def body(dx_ref, g_ref, yo_ref, w_ref, ya_ref, o_ref, go_ref, ga_ref, gb_ref,
             dyo_ref, dya_ref, do_ref, dl_ref, p_ref):
        i = pl.program_id(0)
        dxv = dx_ref[...]
        dyo = (dxv * g_ref[...]).astype(BF16)
        dyo_ref[...] = dyo
        dyn = _dot_nt(dyo, w_ref[...])
        dya, dga = norm_bwd(ya_ref[...], dyn[:, 0:CONV_W], go_ref[:, 0:CONV_W], ga_ref[...], 1.0 / CONV_GROUP)
        ov = o_ref[...]
        do, dgb = norm_bwd(ov, dyn[:, CONV_W:], go_ref[:, CONV_W:], gb_ref[...], 1.0 / V_HEAD)
        dya_ref[...] = dya
        do_ref[...] = do.astype(BF16)
        dl_ref[...] = _gsum(do * ov, gb_ref[...], split=True)

        @pl.when(i == 0)
        def _():
            p_ref[...] = jnp.zeros_like(p_ref)

        p_ref[:, 0:D] += _rsum8(dxv * yo_ref[...].astype(F32))
        p_ref[:, D:D + CONV_W] += _rsum8(dga)
        p_ref[:, D + CONV_W:2 * D] += _rsum8(dgb)

        @pl.when(i == n_i - 1)
        def _():
            _all_rows(p_ref)

    row = pl.BlockSpec((tm, D), lambda i: (i, 0))
    half = pl.BlockSpec((tm, CONV_W), lambda i: (i, 0))
    vec = pl.BlockSpec((1, D), lambda i: (0, 0))
    sq = pl.BlockSpec((CONV_W, CONV_W), lambda i: (0, 0))
    return _hosted_call(
        body, name="mix_out_bwd", grid=(n_i,),
        in_specs=[row, vec, row, pl.BlockSpec((D, D), lambda i: (0, 0)), half, half, vec, sq, sq],
        out_specs=[row, half, half, half, pl.BlockSpec((8, 2 * D), lambda i: (0, 0))],
        out_shape=[jax.ShapeDtypeStruct((S, D), BF16), jax.ShapeDtypeStruct((S, CONV_W), F32),
                   jax.ShapeDtypeStruct((S, CONV_W), BF16), jax.ShapeDtypeStruct((S, CONV_W), F32),
                   jax.ShapeDtypeStruct((8, 2 * D), F32)],
        scratch_shapes=[], semantics=("arbitrary",), args=(dx, gate, yo, w_out, ya, o, gout, ga, gb), comm=comm)


MID_SUMS = 3 * CONV_W + Q_LORA + KV_LORA


def _mix_mid_bwd(z, dya, conv_w, gq, gkv, wuq, wukv, cs, sn, dq, dk, dv, comm=None):
    S = z.shape[0]
    tm = _row_tile(S, 256)
    n_i = S // tm
    hb = tm // 8
    last_blk = S // 8 - 1

    def latent_bwd(cv, dcn, gain):
        r = _rms(cv)
        ch = cv * r
        dy = dcn * gain
        return r * (dy - ch * jnp.mean(dy * ch, axis=-1, keepdims=True)), dcn * ch

    def body(z_ref, zp_ref, zn_ref, dya_ref, dyan_ref, cw_ref, gq_ref, gkv_ref, wuq_ref, wukv_ref, cs_ref, sn_ref,
             dq_ref, dk_ref, dv_ref, dz_ref, dqf_ref, dkvf_ref, p_ref):
        i = pl.program_id(0)
        xb, xc, xu = z_ref[:, 0:CONV_W], z_ref[:, CONV_W:2 * CONV_W], z_ref[:, 2 * CONV_W:3 * CONV_W]
        u = xc * xu
        halo = jnp.where(i > 0, zp_ref[:, CONV_W:2 * CONV_W] * zp_ref[:, 2 * CONV_W:3 * CONV_W], 0.0)
        rows = lax.broadcasted_iota(jnp.int32, (tm, CONV_W), 0)
        u1, u2 = _conv_taps(u, halo, rows)
        w0, w1, w2 = cw_ref[0:1, :], cw_ref[1:2, :], cw_ref[2:3, :]
        y = w0 * u2 + w1 * u1 + w2 * u
        dyav = dya_ref[...]
        dy = dyav * xb
        nxt = jnp.where(i < n_i - 1, dyan_ref[...] * zn_ref[:, 0:CONV_W], 0.0)
        dy1 = jnp.where(rows == tm - 1, nxt[0:1, :], pltpu.roll(dy, tm - 1, 0))
        dy2 = jnp.where(rows == tm - 1, nxt[1:2, :], jnp.where(rows == tm - 2, nxt[0:1, :], pltpu.roll(dy, tm - 2, 0)))
        du = w2 * dy + w1 * dy1 + w0 * dy2
        dz_ref[:, 0:CONV_W] = (dyav * y).astype(BF16)
        dz_ref[:, CONV_W:2 * CONV_W] = (du * xu).astype(BF16)
        dz_ref[:, 2 * CONV_W:3 * CONV_W] = (du * xc).astype(BF16)

        lane = lax.broadcasted_iota(jnp.int32, (tm, 128), 1)
        cs_v, sn_v = cs_ref[...], sn_ref[...]
        dkr = jnp.zeros((tm, 128), F32)
        for h in range(HEADS):
            o = h * HEAD_SLOT
            dqf_ref[:, o:o + 128] = dq_ref[:, o:o + 128]
            dqf_ref[:, o + 128:o + 256] = _rope_t(dq_ref[:, o + 128:o + 256].astype(F32), cs_v, sn_v, lane).astype(BF16)
            dkvf_ref[:, h * 128:(h + 1) * 128] = dk_ref[:, o:o + 128]
            dkr = dkr + dk_ref[:, o + 128:o + 256].astype(F32)
        dkvf_ref[:, HEADS * 128:] = dv_ref[...]

        c0 = 3 * CONV_W
        dcqn = jnp.dot(dqf_ref[...], wuq_ref[...], preferred_element_type=F32)
        dcq, dgq = latent_bwd(z_ref[:, c0:c0 + Q_LORA], dcqn, gq_ref[...])
        dz_ref[:, c0:c0 + Q_LORA] = dcq.astype(BF16)
        c1 = c0 + Q_LORA
        dckvn = _dot_nt(dkvf_ref[...], wukv_ref[...])
        dckv, dgkv = latent_bwd(z_ref[:, c1:c1 + KV_LORA], dckvn, gkv_ref[...])
        dz_ref[:, c1:c1 + KV_LORA] = dckv.astype(BF16)
        dz_ref[:, c1 + KV_LORA:Z_COLS] = _rope_t(dkr, cs_v, sn_v, lane).astype(BF16)

        @pl.when(i == 0)
        def _():
            p_ref[...] = jnp.zeros_like(p_ref)

        p_ref[:, 0:CONV_W] += _rsum8(dy * u2)
        p_ref[:, CONV_W:2 * CONV_W] += _rsum8(dy * u1)
        p_ref[:, 2 * CONV_W:3 * CONV_W] += _rsum8(dy * u)
        p_ref[:, c0:c0 + Q_LORA] += _rsum8(dgq)
        p_ref[:, c1:c1 + KV_LORA] += _rsum8(dgkv)

        @pl.when(i == n_i - 1)
        def _():
            _all_rows(p_ref)

    def rows_of(n):
        return pl.BlockSpec((tm, n), lambda i: (i, 0))

    def whole(shape):
        return pl.BlockSpec(shape, lambda i: (0, 0))

    def prev8(n):
        return pl.BlockSpec((8, n), lambda i: (jnp.maximum(i * hb - 1, 0), 0))

    def next8(n):
        return pl.BlockSpec((8, n), lambda i: (jnp.minimum((i + 1) * hb, last_blk), 0))

    return _hosted_call(
        body, name="mix_mid_bwd", grid=(n_i,),
        in_specs=[rows_of(Z_COLS), prev8(Z_COLS), next8(Z_COLS), rows_of(CONV_W), next8(CONV_W),
                  whole((8, CONV_W)), whole((1, Q_LORA)), whole((1, KV_LORA)),
                  whole((HEADS * HEAD_SLOT, Q_LORA)), whole((KV_LORA, 2 * HEADS * 128)),
                  rows_of(128), rows_of(128),
                  rows_of(HEADS * HEAD_SLOT), rows_of(HEADS * HEAD_SLOT), rows_of(HEADS * V_HEAD)],
        out_specs=[rows_of(Z_COLS), rows_of(HEADS * HEAD_SLOT), rows_of(2 * HEADS * 128), whole((8, MID_SUMS))],
        out_shape=[jax.ShapeDtypeStruct((S, Z_COLS), BF16), jax.ShapeDtypeStruct((S, HEADS * HEAD_SLOT), BF16),
                   jax.ShapeDtypeStruct((S, 2 * HEADS * 128), BF16), jax.ShapeDtypeStruct((8, MID_SUMS), F32)],
        scratch_shapes=[], semantics=("arbitrary",),
        args=(z, z, z, dya, dya, conv_w, gq, gkv, wuq, wukv, cs, sn, dq, dk, dv), comm=comm)


ADA_Q = N_MOD * D // N_CHIPS
ADA_TN = 768


def _ada_forward(c_all, ada_w_q, ada_b_q):
    def body(c_ref, w_ref, b_ref, o_ref):
        cv = c_ref[...]
        sc = (cv * jax.nn.sigmoid(cv)).astype(BF16)
        o_ref[...] = jnp.dot(sc, w_ref[...].astype(BF16), preferred_element_type=F32) + b_ref[...]

    return pl.pallas_call(
        body, name="ada_forward", grid=(ADA_Q // ADA_TN,),
        in_specs=[pl.BlockSpec((16, D), lambda j: (0, 0)), pl.BlockSpec((D, ADA_TN), lambda j: (0, j)),
                  pl.BlockSpec((1, ADA_TN), lambda j: (0, j))],
        out_specs=pl.BlockSpec((16, ADA_TN), lambda j: (0, j)),
        out_shape=jax.ShapeDtypeStruct((16, ADA_Q), F32),
        compiler_params=_params(("parallel",)),
    )(c_all, ada_w_q, ada_b_q)


def _ada_wgrad(c_all, dmod_q):
    def body(c_ref, d_ref, o_ref):
        cv = c_ref[...]
        sc = (cv * jax.nn.sigmoid(cv)).astype(BF16)
        o_ref[...] = lax.dot_general(sc, d_ref[...].astype(BF16), (((0,), (0,)), ((), ())),
                                     preferred_element_type=F32)

    return pl.pallas_call(
        body, name="ada_wgrad", grid=(ADA_Q // ADA_TN,),
        in_specs=[pl.BlockSpec((16, D), lambda j: (0, 0)), pl.BlockSpec((16, ADA_TN), lambda j: (0, j))],
        out_specs=pl.BlockSpec((D, ADA_TN), lambda j: (0, j)),
        out_shape=jax.ShapeDtypeStruct((D, ADA_Q), F32),
        compiler_params=_params(("parallel",)),
    )(c_all, dmod_q)


def _sum_devices(parts):
    n = parts.shape[1]

    def body(p_ref, o_ref):
        o_ref[...] = jnp.broadcast_to(jnp.sum(p_ref[...], axis=0, keepdims=True), o_ref.shape)

    return pl.pallas_call(
        body, name="sum_devices",
        in_specs=[pl.BlockSpec((N_DEV, n), lambda: (0, 0))], out_specs=pl.BlockSpec((N_DEV, n), lambda: (0, 0)),
        out_shape=jax.ShapeDtypeStruct((N_DEV, n), F32),
    )(parts)


def _adamw(ws, gs, ms, vs, *, name, comm=None):
    n = len(ws)
    _, rows, cols = ws[0].shape
    tr = _row_tile(rows, 256)

    def body(*refs):
        ins, outs = refs[:4 * n], refs[4 * n:]
        for k in range(n):
            w_ref, g_ref, m_ref, v_ref = ins[k], ins[n + k], ins[2 * n + k], ins[3 * n + k]
            go_ref, d_ref, mo_ref, vo_ref = outs[4 * k:4 * k + 4]
            gv = g_ref[...]
            mn = B1 * m_ref[0] + (1.0 - B1) * gv
            vn = B2 * v_ref[0] + (1.0 - B2) * (gv * gv)
            m_hat = mn / (1.0 - B1 ** STEP)
            v_hat = vn / (1.0 - B2 ** STEP)
            go_ref[0] = gv
            d_ref[0] = -LR * (m_hat / (jnp.sqrt(v_hat) + AEPS) + WD * w_ref[0])
            mo_ref[0] = mn
            vo_ref[0] = vn

    blk = pl.BlockSpec((1, tr, cols), lambda i: (0, i, 0))
    gblk = pl.BlockSpec((tr, cols), lambda i: (i, 0))
    res, couts = _hosted_call(
        body, name=name, grid=(rows // tr,),
        in_specs=[blk] * n + [gblk] * n + [blk] * 2 * n, out_specs=[blk] * 4 * n,
        out_shape=[jax.ShapeDtypeStruct((1, rows, cols), F32)] * 4 * n,
        scratch_shapes=[], semantics=("parallel",), args=(*ws, *gs, *ms, *vs), comm=comm)
    return [res[4 * k:4 * k + 4] for k in range(n)], couts


def _small_allgather(v, *, name):
    m, n = v.shape

    def body(x_ref, out_ref, send_sems, recv_sems, local_sem):
        x, y, c = _place()
        me, sibling = (x, y, c), (x, y, 1 - c)
        chips = [(1 - x, y), (x, 1 - y), (1 - x, 1 - y)]

        def rows(px, py, pc):
            return out_ref.at[pl.ds((4 * px + 2 * py + pc) * m, m), :]

        def copy(k, block, to, src=None):
            return pltpu.make_async_remote_copy(
                src_ref=rows(*block) if src is None else src, dst_ref=rows(*block),
                send_sem=send_sems.at[k], recv_sem=recv_sems.at[k], device_id=to, device_id_type=MESH)

        mine = pltpu.make_async_copy(x_ref, rows(*me), local_sem)
        mine.start()
        first = [copy(0, me, sibling, src=x_ref)]
        first += [copy(1 + j, me, (*chip, c), src=x_ref) for j, chip in enumerate(chips)]
        for cp in first:
            cp.start()
        passed = [copy(4 + j, (*chip, c), sibling) for j, chip in enumerate(chips)]
        for j, chip in enumerate(chips):
            copy(1 + j, (*chip, c), me).wait_recv()
            passed[j].start()
        copy(0, sibling, me).wait_recv()
        for j, chip in enumerate(chips):
            copy(4 + j, (*chip, 1 - c), me).wait_recv()
        for cp in first + passed:
            cp.wait_send()
        mine.wait()

    return pl.pallas_call(
        body, name=name,
        out_shape=jax.ShapeDtypeStruct((N_DEV * m, n), v.dtype),
        in_specs=[pl.BlockSpec(memory_space=pltpu.VMEM)], out_specs=pl.BlockSpec(memory_space=pltpu.VMEM),
        scratch_shapes=[pltpu.SemaphoreType.DMA((7,)), pltpu.SemaphoreType.DMA((7,)), pltpu.SemaphoreType.DMA],
    )(v)


ADD_BLOCKS = 2


def _pair_add(place, gs, ts, *, name):
    n_a = len(gs)

    def body(pl_ref, *refs):
        g_refs, t_refs = refs[:n_a], refs[n_a:2 * n_a]
        pf_refs, pb_refs = refs[2 * n_a:3 * n_a], refs[3 * n_a:]
        own = pl.program_id(1) == pl_ref[1]
        for g_ref, t_ref, pf_ref, pb_ref in zip(g_refs, t_refs, pf_refs, pb_refs):
            s = g_ref[...] + t_ref[...]
            pb_ref[...] = s.astype(BF16)

            @pl.when(own)
            def _():
                pf_ref[...] = s[0]

    def blk(t, own_half):
        tr = t.shape[1] // ADD_BLOCKS
        if own_half:
            return pl.BlockSpec((1, tr, t.shape[2]), lambda r, q, p: (q, p[0] * ADD_BLOCKS + r, 0))
        return pl.BlockSpec((1, tr, t.shape[2]), lambda r, q, p: (q, r, 0))

    def own_blk(t):
        return pl.BlockSpec((t.shape[1] // ADD_BLOCKS, t.shape[2]), lambda r, q, p: (r, 0))

    grid_spec = pltpu.PrefetchScalarGridSpec(
        num_scalar_prefetch=1, grid=(ADD_BLOCKS, N_CHIPS),
        in_specs=[blk(t, True) for t in ts] + [blk(t, False) for t in ts],
        out_specs=[own_blk(t) for t in ts] + [blk(t, False) for t in ts])
    res = pl.pallas_call(
        body, name=name, grid_spec=grid_spec,
        out_shape=([jax.ShapeDtypeStruct(t.shape[1:], F32) for t in ts]
                   + [jax.ShapeDtypeStruct(t.shape, BF16) for t in ts]),
        compiler_params=_params(("parallel", "arbitrary")),
    )(place, *gs, *ts)
    return list(res[:n_a]), list(res[n_a:])


def _chip_add(place, pfs, ts, *, name):
    n_a = len(pfs)

    def body(pl_ref, *refs):
        pf_refs, t_refs, o_refs = refs[:n_a], refs[n_a:4 * n_a], refs[4 * n_a:]
        for i, (pf_ref, o_ref) in enumerate(zip(pf_refs, o_refs)):
            t1, t2, t3 = t_refs[3 * i:3 * i + 3]
            o_ref[...] = ((pf_ref[...] + t1[0].astype(F32)) + t2[0].astype(F32)) + t3[0].astype(F32)

    def slot(t, j):
        return pl.BlockSpec((1, t.shape[1] // ADD_BLOCKS, t.shape[2]), lambda r, p: (p[1] ^ j, r, 0))

    def half(t):
        return pl.BlockSpec((t.shape[1] // ADD_BLOCKS, t.shape[2]), lambda r, p: (r, 0))

    grid_spec = pltpu.PrefetchScalarGridSpec(
        num_scalar_prefetch=1, grid=(ADD_BLOCKS,),
        in_specs=[half(t) for t in ts] + [slot(t, j) for t in ts for j in (1, 2, 3)],
        out_specs=[half(t) for t in ts])
    res = pl.pallas_call(
        body, name=name, grid_spec=grid_spec,
        out_shape=[jax.ShapeDtypeStruct(t.shape[1:], F32) for t in ts],
        compiler_params=_params(("parallel",)),
    )(place, *pfs, *[t for t in ts for _ in range(3)])
    return list(res)


BULK = [("ffn1_w1", "colsT"), ("ffn1_w3", "colsT"), ("ffn1_w2", "rows"), ("w_in", "cols"), ("w_uq", "colsT"),
        ("w_ukv", "cols"), ("w_out", "rows"), ("ffn2_w1", "colsT"), ("ffn2_w3", "colsT"), ("ffn2_w2", "rows")]
KIND = dict(BULK)


def _group(*names):
    return [b for b in BULK if b[0] in names]


W_FIRST = _group("ffn1_w1", "ffn1_w3")
W_REST = [b for b in BULK if b not in W_FIRST]
W_MIX = _group("ffn1_w2", "w_in", "w_uq", "w_ukv", "w_out")
W_FFN2 = _group("ffn2_w1", "ffn2_w3", "ffn2_w2")
G_FFN2 = _group("ffn2_w1", "ffn2_w3", "ffn2_w2")
G_MIX = _group("w_in", "w_uq", "w_ukv", "w_out")
G_FFN1 = _group("ffn1_w1", "ffn1_w3", "ffn1_w2")


def _gathered_weights(specs, shards, got, myq):
    out = {}
    for (name, kind), part in zip(specs, got):
        part = lax.dynamic_update_slice_in_dim(part, shards[name][None], myq, axis=0)
        out[name] = _full_weight(part, kind)
    return out


def _working_shard(w, kind):
    return jnp.swapaxes(w, 1, 2)[0] if kind == "colsT" else w[0]


def _full_weight(parts, kind):
    if kind == "cols":
        return jnp.transpose(parts, (1, 0, 2)).reshape(parts.shape[1], -1)
    return parts.reshape(-1, parts.shape[2])


def _quarters(g, kind):
    if kind == "cols":
        k, n = g.shape
        return jnp.transpose(g.reshape(k, N_CHIPS, n // N_CHIPS), (1, 0, 2))
    return g.reshape(N_CHIPS, g.shape[0] // N_CHIPS, g.shape[1])


def _pad_heads(w_uq_t):
    w = w_uq_t.reshape(HEADS, QK_NOPE + QK_ROPE, Q_LORA)
    return jnp.pad(w, ((0, 0), (0, HEAD_SLOT - QK_NOPE - QK_ROPE), (0, 0))).reshape(HEADS * HEAD_SLOT, Q_LORA)


def _unpad_heads(g):
    return g.reshape(HEADS, HEAD_SLOT, Q_LORA)[:, :QK_NOPE + QK_ROPE].reshape(HEADS * (QK_NOPE + QK_ROPE), Q_LORA)


def _split_kv(w_ukv):
    return jnp.transpose(w_ukv.reshape(KV_LORA, HEADS, 2, 128), (0, 2, 1, 3)).reshape(KV_LORA, 2 * HEADS * 128)


def _merge_kv(g):
    return jnp.transpose(g.reshape(KV_LORA, 2, HEADS, 128), (0, 2, 1, 3)).reshape(KV_LORA, 2 * HEADS * 128)


def _rope_tables(positions):
    inv_freq = ROPE_THETA ** (-jnp.arange(0, QK_ROPE, 2, dtype=F32) / QK_ROPE)
    ang = positions.astype(F32)[:, None] * inv_freq
    cos, sin, zero = jnp.cos(ang), jnp.sin(ang), jnp.zeros((positions.shape[0], 64), F32)
    return jnp.concatenate([cos, cos, zero], axis=1), jnp.concatenate([sin, sin, zero], axis=1)


def _assemble(place, specs, rhs, others):
    south = place[0] == 0
    return {b[0]: jnp.concatenate([jnp.where(south, rh, ot), jnp.where(south, ot, rh)], axis=0)
            for b, rh, ot in zip(specs, rhs, others)}


def _local_step(x, positions, target, mod, vec, conv_w, w_first, rest_shards, place, tail_host=None):
    row = lambda k: mod[k:k + 1]
    sh1, sc1, g1, sh2, sc2, g2, sh3, sc3, g3 = [row(k) for k in range(N_MOD)]
    cs, sn = _rope_tables(positions)
    cw8 = jnp.pad(conv_w, ((0, 5), (0, 0)))
    ga, gb = _group_mats()
    dist = place is not None
    comm = lambda prog: prog if dist else None

    w = dict(w_first) if dist else {**rest_shards, **w_first}

    def gather(specs):
        return _Gather([rest_shards[b[0]] for b in specs]) if dist else None

    def arrived(specs, got):
        if dist:
            w.update(_gathered_weights(specs, rest_shards, got, place[1]))

    (h1, a1, b1, u1), got = _ffn_up(x, vec["norm_ffn1_g"], sh1, sc1, w["ffn1_w1"], w["ffn1_w3"],
                                    name="ffn1_up", comm=gather(W_MIX))
    arrived(W_MIX, got)
    w_in = jnp.pad(w["w_in"].T, ((0, Z_COLS - IN_COLS), (0, 0)))
    wuq = _pad_heads(w["w_uq"])
    wukv = _split_kv(w["w_ukv"])
    (x1, f1), _ = _ffn_down(u1, w["ffn1_w2"], x, g1, name="ffn1_down")
    (h2, z), _ = _mix_in(x1, vec["norm_mix_g"], sh2, sc2, w_in)
    (ya, q, k, v, cqn, ckvn), _ = _mix_mid(z, cw8, vec["q_norm_g"], vec["kv_norm_g"], wuq, wukv, cs, sn)
    (o, lse), got = _attention(q, k, v, comm=gather(W_FFN2))
    arrived(W_FFN2, got)
    x2, yn, yo = _mix_out(ya, o, vec["out_norm_g"], w["w_out"], x1, g2, ga, gb)
    (h3, a3, b3, u3), _ = _ffn_up(x2, vec["norm_ffn2_g"], sh3, sc3, w["ffn2_w1"], w["ffn2_w3"], name="ffn2_up")
    dx3, f3, dgfin, loss_blk = _ffn_down_loss(u3, w["ffn2_w2"], x2, g3, vec["final_norm_g"], target)

    grads, reduced = {}, {}

    def tn(a, b, tm, tn_, name, prog=None):
        if prog is None:
            return _tn_matmul(a, b, tm=tm, tn=tn_, name=name), None
        return _tn_matmul(a, b, tm=tm, tn=tn_, name=name, comm=prog)

    def slab_of(specs):
        return [_quarters(grads[n], kind) for n, kind in specs]

    (df3, da3, db3, dg3), _ = _ffn_bwd_du(dx3, g3, f3, w["ffn2_w2"], a3, b3, name="ffn2_bwd_du")
    grads["ffn2_w2"], _ = tn(u3, df3, FF // 2, D, "ffn2_dw2")
    grads["ffn2_w1"], _ = tn(da3, h3, FF // 2, D, "ffn2_dw1")
    grads["ffn2_w3"], _ = tn(db3, h3, FF // 2, D, "ffn2_dw3")
    (dx2, s3), _ = _dh_normbwd([(da3, w["ffn2_w1"]), (db3, w["ffn2_w3"])], x2, vec["norm_ffn2_g"], sc3, dx3,
                               name="ffn2_bwd_dh")

    p1 = slab_of(G_FFN2) if dist else None
    (dyo, dya, do, delta, s_out), t1 = _mix_out_bwd(dx2, g2, yo, w["w_out"], ya, o, vec["out_norm_g"], ga, gb,
                                                    comm=comm(_PairExchange(p1) if dist else None))
    grads["w_out"], _ = tn(yn, dyo, D, D, "dw_out")
    if dist:
        pf1, pb1 = _pair_add(place, p1, t1, name="ffn2g_pair_add")
    dq, dk, dv = _attention_bwd(q, k, v, do, lse, delta)
    (dz, dqf, dkvf, s_mid), t2 = _mix_mid_bwd(z, dya, cw8, vec["q_norm_g"], vec["kv_norm_g"], wuq, wukv, cs, sn,
                                              dq, dk, dv, comm=comm(_ChipExchange(pb1) if dist else None))
    g_uq, _ = tn(dqf, cqn, HEADS * HEAD_SLOT, Q_LORA, "dw_uq")
    g_ukv, _ = tn(ckvn, dkvf, KV_LORA, 2 * HEADS * 128, "dw_ukv")
    grads["w_uq"], grads["w_ukv"] = _unpad_heads(g_uq), _merge_kv(g_ukv)
    g_in, _ = tn(h2, dz, D, Z_COLS, "dw_in")
    grads["w_in"] = g_in[:, :IN_COLS]
    (dx1, s2), _ = _dh_normbwd([(dz, w_in)], x1, vec["norm_mix_g"], sc2, dx2, name="mix_bwd_dh")

    p2 = slab_of(G_MIX) if dist else None
    rh_ffn2 = _chip_add(place, pf1, t2, name="ffn2g_chip_add") if dist else None
    (df1, da1, db1, dg1), got = _ffn_bwd_du(dx1, g1, f1, w["ffn1_w2"], a1, b1, name="ffn1_bwd_du",
                                            comm=comm(_Multi([_PairExchange(p2), _PairShare(rh_ffn2)]) if dist else None))
    if dist:
        t1 = got[:len(p2)]
        reduced.update(_assemble(place, G_FFN2, rh_ffn2, got[len(p2):]))
    dh_pairs = [(da1, w["ffn1_w1"]), (db1, w["ffn1_w3"])]
    if dist:
        pf2, pb2 = _pair_add(place, p2, t1, name="mixg_pair_add")
        grads["ffn1_w2"], t2 = tn(u1, df1, FF // 2, D, "ffn1_dw2", _ChipExchange(pb2))
        rh_mix = _chip_add(place, pf2, t2, name="mixg_chip_add")
        q_w2 = [_quarters(grads["ffn1_w2"], KIND["ffn1_w2"])]
        grads["ffn1_w1"], got = tn(da1, h1, FF // 2, D, "ffn1_dw1", _Multi([_PairShare(rh_mix), _PairExchange(q_w2)]))
        reduced.update(_assemble(place, G_MIX, rh_mix, got[:len(rh_mix)]))
        pf_w2, pb_w2 = _pair_add(place, q_w2, got[len(rh_mix):], name="ffn1w2_pair_add")
        q_w1 = [_quarters(grads["ffn1_w1"], KIND["ffn1_w1"])]
        grads["ffn1_w3"], got = tn(db1, h1, FF // 2, D, "ffn1_dw3", _Multi([_ChipExchange(pb_w2), _PairExchange(q_w1)]))
        t2_w2 = got[:1]
        pf_w1, pb_w1 = _pair_add(place, q_w1, got[1:], name="ffn1w1_pair_add")
        q_w3 = [_quarters(grads["ffn1_w3"], KIND["ffn1_w3"])]
        (dx0, s1), got = _dh_normbwd(dh_pairs, x, vec["norm_ffn1_g"], sc1, dx1, name="ffn1_bwd_dh",
                                     comm=_Multi([_ChipExchange(pb_w1), _PairExchange(q_w3)]))
        t2_w1 = got[:1]
        pf_w3, pb_w3 = _pair_add(place, q_w3, got[1:], name="ffn1w3_pair_add")
        t2_w3 = tail_host(_ChipExchange(pb_w3), reduced)
        rh = _chip_add(place, pf_w1 + pf_w3 + pf_w2, t2_w1 + t2_w3 + t2_w2, name="ffn1g_chip_add")
        reduced.update(_assemble(place, G_FFN1, rh, _run_comm(_PairShare(rh), name="ffn1g_pair_share")))
    else:
        grads["ffn1_w2"], _ = tn(u1, df1, FF // 2, D, "ffn1_dw2")
        grads["ffn1_w1"], _ = tn(da1, h1, FF // 2, D, "ffn1_dw1")
        grads["ffn1_w3"], _ = tn(db1, h1, FF // 2, D, "ffn1_dw3")
        (dx0, s1), _ = _dh_normbwd(dh_pairs, x, vec["norm_ffn1_g"], sc1, dx1, name="ffn1_bwd_dh")
        reduced = grads

    def part(s, k):
        return s[0:1, k * D:(k + 1) * D]

    dmod = jnp.concatenate([part(s1, 1), part(s1, 0), dg1[0:1], part(s2, 1), part(s2, 0), part(s_out, 0),
                            part(s3, 1), part(s3, 0), dg3[0:1]], axis=1)
    small = {"norm_ffn1_g": part(s1, 2), "norm_mix_g": part(s2, 2), "out_norm_g": part(s_out, 1),
             "norm_ffn2_g": part(s3, 2), "final_norm_g": dgfin[0:1],
             "q_norm_g": s_mid[0:1, 3 * CONV_W:3 * CONV_W + Q_LORA],
             "kv_norm_g": s_mid[0:1, 3 * CONV_W + Q_LORA:MID_SUMS], "conv_w": s_mid[0:1, 0:3 * CONV_W]}
    return loss_blk, dx0, reduced, dmod, small


SMALL = [("norm_ffn1_g", D), ("norm_mix_g", D), ("out_norm_g", D), ("norm_ffn2_g", D), ("final_norm_g", D),
         ("q_norm_g", Q_LORA), ("kv_norm_g", KV_LORA), ("conv_w", 3 * CONV_W)]
WEIGHTS = ['ada_w', 'ada_b', 'norm_ffn1_g', 'ffn1_w1', 'ffn1_w3', 'ffn1_w2', 'norm_mix_g', 'w_in', 'conv_w',
           'q_norm_g', 'w_uq', 'kv_norm_g', 'w_ukv', 'out_norm_g', 'w_out', 'norm_ffn2_g', 'ffn2_w1', 'ffn2_w3',
           'ffn2_w2', 'final_norm_g']


def kernel(x, c, positions, ada_w, ada_b, norm_ffn1_g, ffn1_w1, ffn1_w3, ffn1_w2, norm_mix_g, w_in, conv_w, q_norm_g, w_uq, kv_norm_g, w_ukv, out_norm_g, w_out, norm_ffn2_g, ffn2_w1, ffn2_w3, ffn2_w2, final_norm_g, loss_target, m_ada_w, m_ada_b, m_norm_ffn1_g, m_ffn1_w1, m_ffn1_w3, m_ffn1_w2, m_norm_mix_g, m_w_in, m_conv_w, m_q_norm_g, m_w_uq, m_kv_norm_g, m_w_ukv, m_out_norm_g, m_w_out, m_norm_ffn2_g, m_ffn2_w1, m_ffn2_w3, m_ffn2_w2, m_final_norm_g, v_ada_w, v_ada_b, v_norm_ffn1_g, v_ffn1_w1, v_ffn1_w3, v_ffn1_w2, v_norm_mix_g, v_w_in, v_conv_w, v_q_norm_g, v_w_uq, v_kv_norm_g, v_w_ukv, v_out_norm_g, v_w_out, v_norm_ffn2_g, v_ffn2_w1, v_ffn2_w3, v_ffn2_w2, v_final_norm_g):
    args = dict(locals())
    wts = {n: args[n] for n in WEIGHTS}
    mom = {n: args["m_" + n] for n in WEIGHTS}
    var = {n: args["v_" + n] for n in WEIGHTS}
    ax, ay, ac = _place()
    myq = 2 * ax + ay
    me = 2 * myq + ac
    place = jnp.stack([ac, myq]).astype(jnp.int32)

    shards = {name: _working_shard(wts[name], kind).astype(BF16) for name, kind in BULK}
    first = _run_comm(_Gather([shards[b[0]] for b in W_FIRST]), name="gather_ffn1")
    w_first = _gathered_weights(W_FIRST, shards, first, myq)

    mine = jnp.concatenate([c, conv_w[0].reshape(1, 3 * CONV_W // N_CHIPS)], axis=1)
    seen = _small_allgather(jnp.pad(mine, ((0, 7), (0, 0))), name="gather_cond").reshape(N_DEV, 8, -1)[:, 0]
    c_all = jnp.pad(seen[:, :D], ((0, 8), (0, 0)))
    conv_full = jnp.transpose(seen[0::2, D:].reshape(N_CHIPS, 3, CONV_W // N_CHIPS), (1, 0, 2)).reshape(3, CONV_W)
    ada_b_q = lax.dynamic_slice_in_dim(ada_b, myq * ADA_Q, ADA_Q, axis=1)
    mod_q = _ada_forward(c_all, ada_w[0], ada_b_q)
    mod_all = _small_allgather(mod_q, name="gather_mod").reshape(N_DEV, 16, ADA_Q)
    mod_rows = jnp.transpose(mod_all[0::2, :N_DEV], (1, 0, 2)).reshape(N_DEV, N_MOD * D)
    mod = lax.dynamic_slice_in_dim(mod_rows, me, 1, axis=0).reshape(N_MOD, D)

    vec = {n: wts[n] for n in ("norm_ffn1_g", "norm_mix_g", "q_norm_g", "kv_norm_g", "out_norm_g", "norm_ffn2_g")}
    vec["final_norm_g"] = final_norm_g.reshape(1, D)
    g, delta, new_m, new_v = {}, {}, {}, {}

    def adam(names, grads, comm=None):
        views = [(lambda a: jnp.swapaxes(a, 1, 2)) if KIND.get(n) == "colsT" else (lambda a: a) for n in names]
        res, couts = _adamw([vw(wts[n]) for n, vw in zip(names, views)], [grads[n] for n in names],
                            [vw(mom[n]) for n, vw in zip(names, views)], [vw(var[n]) for n, vw in zip(names, views)],
                            name="adamw_" + names[0], comm=comm)
        for n, vw, r in zip(names, views, res):
            g[n], delta[n], new_m[n], new_v[n] = [vw(a) for a in r]
        return couts

    loss_blk, grad_x, gq, dmod, small = _local_step(
        x[0], positions[0], loss_target[0], mod, vec, conv_full, w_first, {b[0]: shards[b[0]] for b in W_REST}, place,
        tail_host=lambda prog, grads: adam([b[0] for b in G_FFN2], grads, prog))
    loss = lax.psum(loss_blk[0, 0], ("x", "y", "c"))

    rows = jnp.concatenate([dmod] + [small[n] for n, _ in SMALL], axis=1)
    width = rows.shape[1]
    fold = -(-width // (8 * 128)) * 128
    rows = jnp.pad(rows, ((0, 0), (0, 8 * fold - width))).reshape(8, fold)
    every = _small_allgather(rows, name="gather_small").reshape(N_DEV, 8 * fold)[:, :width]
    total = _sum_devices(every)[0:1]
    dmod_q = lax.dynamic_slice_in_dim(every[:, :N_MOD * D], myq * ADA_Q, ADA_Q, axis=1)
    sg = {name: gq[name] for name, *_ in BULK}
    sg["ada_w"] = _ada_wgrad(c_all, jnp.pad(dmod_q, ((0, 8), (0, 0))))
    sg["ada_b"] = total[:, :N_MOD * D]
    off = N_MOD * D
    for n, width in SMALL:
        sg[n] = total[:, off:off + width]
        off += width
    sg["conv_w"] = lax.dynamic_slice_in_dim(sg["conv_w"].reshape(3, CONV_W), myq * (CONV_W // N_CHIPS),
                                            CONV_W // N_CHIPS, axis=1)

    adam([b[0] for b in G_FFN1], sg)
    for name in ["ada_w"] + [b[0] for b in G_MIX]:
        adam([name], sg)
    smalls = ["ada_b"] + [n for n, _ in SMALL]

    def packed(d):
        flat = jnp.concatenate([d[n].reshape(1, -1) for n in smalls], axis=1)
        return jnp.pad(flat.reshape(-1, D), ((0, 1), (0, 0)))

    res = _adamw([packed(wts)[None]], [packed(sg)], [packed(mom)[None]], [packed(var)[None]],
                 name="adamw_small")[0][0][1:]
    off = 0
    for n in smalls:
        size = wts[n].size
        for d, r in zip((delta, new_m, new_v), res):
            d[n] = r.reshape(-1)[off:off + size].reshape(wts[n].shape)
        g[n] = sg[n].reshape(wts[n].shape)
        off += size

    return (loss, grad_x[None], *[g[n] for n in WEIGHTS], *[delta[n] for n in WEIGHTS],
            *[new_m[n] for n in WEIGHTS], *[new_v[n] for n in WEIGHTS])
```

```python
import numpy as np
import jax
import jax.numpy as jnp
from jax import lax
from jax.experimental import pallas as pl
from jax.experimental.pallas import tpu as pltpu

F32 = jnp.float32
BF16 = jnp.bfloat16
MESH = pl.DeviceIdType.MESH

D = 1024
FF = 2816
CONV_W = 512
CONV_GROUP = 64
HEADS = 4
QK_NOPE = 128
QK_ROPE = 64
V_HEAD = 128
Q_LORA = 384
KV_LORA = 256
HEAD_SLOT = 256
IN_COLS = 3 * CONV_W + Q_LORA + KV_LORA + QK_ROPE
Z_COLS = 2304
EPS = 1e-6
ROPE_THETA = 10000.0
CHUNK = 64
ATT_SCALE = (QK_NOPE + QK_ROPE) ** -0.5
NEG = -1e30
EXP2_SCALE = ATT_SCALE * 1.4426950408889634
N_MOD = 9

LR, B1, B2, AEPS, WD, STEP = 0.001, 0.9, 0.999, 1e-08, 0.01, 10

N_CHIPS = 4
N_DEV = 8
VMEM_LIMIT = 56 << 20


def _params(sem, vmem=VMEM_LIMIT):
    return pltpu.CompilerParams(dimension_semantics=sem, vmem_limit_bytes=vmem)


def _rms(v):
    return lax.rsqrt(jnp.mean(v * v, axis=-1, keepdims=True) + EPS)


def _rsum8(v):
    t, n = v.shape
    return jnp.sum(v.reshape(t // 8, 8, n), axis=0)


def _all_rows(ref):
    ref[...] = jnp.broadcast_to(jnp.sum(ref[...], axis=0, keepdims=True), ref.shape)


def _gsum(v, gmat, split=False):
    hi = v.astype(BF16)
    out = jnp.dot(hi, gmat, preferred_element_type=F32)
    if split:
        out = out + jnp.dot((v - hi.astype(F32)).astype(BF16), gmat, preferred_element_type=F32)
    return out


def _dot_nt(a, b):
    return lax.dot_general(a, b, (((1,), (1,)), ((), ())), preferred_element_type=F32)


def _silu_parts(a):
    sg = jax.nn.sigmoid(a)
    return sg, a * sg


def _rope(xr, cs, sn, lane):
    rh = jnp.where(lane < 32, -pltpu.roll(xr, 96, 1), pltpu.roll(xr, 32, 1))
    return xr * cs + rh * sn


def _rope_t(g, cs, sn, lane):
    y = g * sn
    rt = jnp.where(lane < 32, pltpu.roll(y, 96, 1), jnp.where(lane < 64, -pltpu.roll(y, 32, 1), 0.0))
    return g * cs + rt


def _row_tile(rows, pref, mult=8):
    t = min(rows, pref) // mult * mult
    while rows % t:
        t -= mult
    return t


def _place():
    return lax.axis_index("x"), lax.axis_index("y"), lax.axis_index("c")


ANY = pl.BlockSpec(memory_space=pl.ANY)


def _hosted_call(body, *, name, grid, in_specs, out_specs, out_shape, scratch_shapes, semantics, args, comm=None,
                 prefetch=()):
    n_in, n_out, n_scr, n_pf = len(in_specs), len(out_specs), len(scratch_shapes), len(prefetch)

    def call(fn, in_specs, out_specs, out_shape, scratch_shapes, semantics, operands):
        spec = pltpu.PrefetchScalarGridSpec(num_scalar_prefetch=n_pf, grid=grid, in_specs=list(in_specs),
                                            out_specs=list(out_specs), scratch_shapes=list(scratch_shapes))
        return pl.pallas_call(fn, name=name, grid_spec=spec, out_shape=list(out_shape),
                              compiler_params=_params(semantics))(*prefetch, *operands)

    if comm is None:
        return list(call(body, in_specs, out_specs, out_shape, scratch_shapes, semantics, args)), []
    n_ci, n_co = len(comm.inputs), len(comm.out_shapes)
    total = int(np.prod(grid))

    def hosted(*refs):
        tables, refs = refs[:n_pf], refs[n_pf:]
        ins, refs = refs[:n_in], refs[n_in:]
        cins, refs = refs[:n_ci], refs[n_ci:]
        outs, refs = refs[:n_out], refs[n_out:]
        couts, refs = refs[:n_co], refs[n_co:]
        scratch, sems = refs[:n_scr], refs[n_scr]
        step = pl.program_id(0)
        for ax in range(1, len(grid)):
            step = step * grid[ax] + pl.program_id(ax)

        @pl.when(step == 0)
        def _():
            comm.start(cins, couts, sems)

        body(*tables, *ins, *outs, *scratch)

        @pl.when(step == total - 1)
        def _():
            comm.finish(cins, couts, sems)

    res = call(hosted, list(in_specs) + [ANY] * n_ci, list(out_specs) + [ANY] * n_co,
               list(out_shape) + list(comm.out_shapes),
               list(scratch_shapes) + [pltpu.SemaphoreType.DMA((comm.n_sems,))],
               ("arbitrary",) * len(grid), (*args, *comm.inputs))
    return list(res[:n_out]), list(res[n_out:])


def _run_comm(comm, *, name):
    n_ci = len(comm.inputs)

    def body(*refs):
        cins, couts, sems = refs[:n_ci], refs[n_ci:-1], refs[-1]
        comm.start(cins, couts, sems)
        comm.finish(cins, couts, sems)

    return list(pl.pallas_call(
        body, name=name, out_shape=list(comm.out_shapes), in_specs=[ANY] * n_ci,
        out_specs=[ANY] * len(comm.out_shapes), scratch_shapes=[pltpu.SemaphoreType.DMA((comm.n_sems,))],
    )(*comm.inputs))


class _Gather:
    def __init__(self, slabs):
        self.inputs = list(slabs)
        self.out_shapes = [jax.ShapeDtypeStruct((N_CHIPS,) + s.shape, s.dtype) for s in slabs]
        self.n_sems = 12 * len(slabs)

    @staticmethod
    def _copy(out, sems, base, k, chip, hc, to, src=None):
        H = out.shape[1] // 2
        half = out.at[2 * chip[0] + chip[1], pl.ds(hc * H, H), :]
        return pltpu.make_async_remote_copy(
            src_ref=half if src is None else src, dst_ref=half, send_sem=sems.at[base + k],
            recv_sem=sems.at[base + 6 + k], device_id=to, device_id_type=MESH)

    def _firsts(self, src, out, sems, base):
        x, y, c = _place()
        H = src.shape[0] // 2
        chips = [(1 - x, y), (x, 1 - y), (1 - x, 1 - y)]
        return [self._copy(out, sems, base, j, (x, y), c, (*chip, c), src=src.at[pl.ds(c * H, H), :])
                for j, chip in enumerate(chips)]

    def start(self, ins, outs, sems):
        for i, (src, out) in enumerate(zip(ins, outs)):
            for cp in self._firsts(src, out, sems, 12 * i):
                cp.start()

    def finish(self, ins, outs, sems):
        x, y, c = _place()
        chips = [(1 - x, y), (x, 1 - y), (1 - x, 1 - y)]
        passed = []
        for i, out in enumerate(outs):
            for j, chip in enumerate(chips):
                self._copy(out, sems, 12 * i, j, chip, c, (x, y, c)).wait_recv()
                cp = self._copy(out, sems, 12 * i, 3 + j, chip, c, (x, y, 1 - c))
                cp.start()
                passed.append(cp)
        for i, out in enumerate(outs):
            for j, chip in enumerate(chips):
                self._copy(out, sems, 12 * i, 3 + j, chip, 1 - c, (x, y, c)).wait_recv()
        for cp in passed:
            cp.wait_send()
        for i, (src, out) in enumerate(zip(ins, outs)):
            for cp in self._firsts(src, out, sems, 12 * i):
                cp.wait_send()


class _PairExchange:
    def __init__(self, arrays):
        self.inputs = list(arrays)
        self.out_shapes = [jax.ShapeDtypeStruct((N_CHIPS, a.shape[1] // 2, a.shape[2]), a.dtype) for a in arrays]
        self.n_sems = 2 * len(arrays)

    def _copies(self, ins, outs, sems):
        x, y, c = _place()
        return [pltpu.make_async_remote_copy(
            src_ref=g.at[:, pl.ds((1 - c) * t.shape[1], t.shape[1]), :], dst_ref=t, send_sem=sems.at[2 * i],
            recv_sem=sems.at[2 * i + 1], device_id=(x, y, 1 - c), device_id_type=MESH)
            for i, (g, t) in enumerate(zip(ins, outs))]

    def start(self, ins, outs, sems):
        for cp in self._copies(ins, outs, sems):
            cp.start()

    def finish(self, ins, outs, sems):
        for cp in self._copies(ins, outs, sems):
            cp.wait()


class _ChipExchange:
    def __init__(self, arrays):
        self.inputs = list(arrays)
        self.out_shapes = [jax.ShapeDtypeStruct(a.shape, a.dtype) for a in arrays]
        self.n_sems = 6 * len(arrays)

    def _copies(self, p, t, sems, base):
        x, y, c = _place()
        myq = 2 * x + y
        chips = [(1 - x, y), (x, 1 - y), (1 - x, 1 - y)]
        sends = [pltpu.make_async_remote_copy(
            src_ref=p.at[2 * chip[0] + chip[1]], dst_ref=t.at[myq], send_sem=sems.at[base + j],
            recv_sem=sems.at[base + 3 + j], device_id=(*chip, c), device_id_type=MESH) for j, chip in enumerate(chips)]
        lands = [pltpu.make_async_remote_copy(
            src_ref=t.at[2 * chip[0] + chip[1]], dst_ref=t.at[2 * chip[0] + chip[1]], send_sem=sems.at[base + j],
            recv_sem=sems.at[base + 3 + j], device_id=(*chip, c), device_id_type=MESH) for j, chip in enumerate(chips)]
        return sends, lands

    def start(self, ins, outs, sems):
        for i, (p, t) in enumerate(zip(ins, outs)):
            for cp in self._copies(p, t, sems, 6 * i)[0]:
                cp.start()

    def finish(self, ins, outs, sems):
        for i, (p, t) in enumerate(zip(ins, outs)):
            sends, lands = self._copies(p, t, sems, 6 * i)
            for cp in lands:
                cp.wait_recv()
            for cp in sends:
                cp.wait_send()


class _SemView:
    def __init__(self, sems, base):
        self._sems, self._base = sems, base

    @property
    def at(self):
        return self

    def __getitem__(self, k):
        return self._sems.at[self._base + k]


class _Multi:
    def __init__(self, progs):
        self.progs = list(progs)
        self.inputs = [a for p in self.progs for a in p.inputs]
        self.out_shapes = [s for p in self.progs for s in p.out_shapes]
        self.n_sems = sum(p.n_sems for p in self.progs)

    def _each(self, ins, outs, sems):
        i = o = s = 0
        for p in self.progs:
            ni, no = len(p.inputs), len(p.out_shapes)
            yield p, ins[i:i + ni], outs[o:o + no], _SemView(sems, s)
            i, o, s = i + ni, o + no, s + p.n_sems

    def start(self, ins, outs, sems):
        for p, a, b, c in self._each(ins, outs, sems):
            p.start(a, b, c)

    def finish(self, ins, outs, sems):
        for p, a, b, c in self._each(ins, outs, sems):
            p.finish(a, b, c)


class _PairShare:
    def __init__(self, arrays):
        self.inputs = list(arrays)
        self.out_shapes = [jax.ShapeDtypeStruct(a.shape, a.dtype) for a in arrays]
        self.n_sems = 2 * len(arrays)

    def _copies(self, ins, outs, sems):
        x, y, c = _place()
        return [pltpu.make_async_remote_copy(
            src_ref=r, dst_ref=o, send_sem=sems.at[2 * i], recv_sem=sems.at[2 * i + 1],
            device_id=(x, y, 1 - c), device_id_type=MESH) for i, (r, o) in enumerate(zip(ins, outs))]

    def start(self, ins, outs, sems):
        for cp in self._copies(ins, outs, sems):
            cp.start()

    def finish(self, ins, outs, sems):
        for cp in self._copies(ins, outs, sems):
            cp.wait()


def _ffn_up(x, ng, sh, sc, w1, w3, *, name, comm=None):
    S = x.shape[0]
    tm, tn = _row_tile(S, 512), FF

    def body(x_ref, g_ref, sh_ref, sc_ref, w1_ref, w3_ref, h_ref, a_ref, b_ref, u_ref, hs):
        @pl.when(pl.program_id(1) == 0)
        def _():
            xv = x_ref[...]
            h = ((xv * _rms(xv)) * g_ref[...]) * (1.0 + sc_ref[...]) + sh_ref[...]
            hb = h.astype(BF16)
            hs[...] = hb
            h_ref[...] = hb

        h = hs[...]
        cols = pl.ds(pl.multiple_of(pl.program_id(1) * tn, tn), tn)
        a = _dot_nt(h, w1_ref[cols, :])
        b = _dot_nt(h, w3_ref[cols, :])
        _, sa = _silu_parts(a)
        a_ref[...] = a.astype(BF16)
        b_ref[...] = b.astype(BF16)
        u_ref[...] = (sa * b).astype(BF16)

    row = pl.BlockSpec((tm, D), lambda i, j: (i, 0))
    vec = pl.BlockSpec((1, D), lambda i, j: (0, 0))
    wsp = pl.BlockSpec((FF, D), lambda i, j: (0, 0))
    osp = pl.BlockSpec((tm, tn), lambda i, j: (i, j))
    return _hosted_call(
        body, name=name, grid=(S // tm, FF // tn),
        in_specs=[row, vec, vec, vec, wsp, wsp],
        out_specs=[row, osp, osp, osp],
        out_shape=[jax.ShapeDtypeStruct((S, D), BF16)] + [jax.ShapeDtypeStruct((S, FF), BF16)] * 3,
        scratch_shapes=[pltpu.VMEM((tm, D), BF16)],
        semantics=("parallel", "arbitrary"), args=(x, ng, sh, sc, w1, w3), comm=comm)


def _ffn_down(u, w2, x, gate, *, name, comm=None):
    S = x.shape[0]
    tm = _row_tile(S, 1024)

    def body(u_ref, w2_ref, x_ref, g_ref, xo_ref, f_ref):
        f = jnp.dot(u_ref[...], w2_ref[...], preferred_element_type=F32)
        xo_ref[...] = x_ref[...] + (0.5 * g_ref[...]) * f
        f_ref[...] = f.astype(BF16)

    return _hosted_call(
        body, name=name, grid=(S // tm,),
        in_specs=[pl.BlockSpec((tm, FF), lambda i: (i, 0)), pl.BlockSpec((FF, D), lambda i: (0, 0)),
                  pl.BlockSpec((tm, D), lambda i: (i, 0)), pl.BlockSpec((1, D), lambda i: (0, 0))],
        out_specs=[pl.BlockSpec((tm, D), lambda i: (i, 0))] * 2,
        out_shape=[jax.ShapeDtypeStruct((S, D), F32), jax.ShapeDtypeStruct((S, D), BF16)],
        scratch_shapes=[], semantics=("parallel",), args=(u, w2, x, gate), comm=comm)


def _ffn_bwd_du(dx, gate, f, w2, a, b, *, name, comm=None):
    S = dx.shape[0]
    tm, tn = _row_tile(S, 512), FF
    n_i = S // tm

    def body(dx_ref, g_ref, f_ref, w_ref, a_ref, b_ref, df_ref, da_ref, db_ref, dg_ref, dfs):
        i, j = pl.program_id(0), pl.program_id(1)

        @pl.when((i == 0) & (j == 0))
        def _():
            dg_ref[...] = jnp.zeros_like(dg_ref)

        @pl.when(j == 0)
        def _():
            dxv = dx_ref[...]
            dfb = (dxv * (0.5 * g_ref[...])).astype(BF16)
            dfs[...] = dfb
            df_ref[...] = dfb
            dg_ref[...] += _rsum8(dxv * (0.5 * f_ref[...].astype(F32)))

        du = _dot_nt(dfs[...], w_ref[pl.ds(pl.multiple_of(j * tn, tn), tn), :])
        av = a_ref[...].astype(F32)
        sg, sa = _silu_parts(av)
        da_ref[...] = (du * b_ref[...].astype(F32) * (sg * (1.0 + av * (1.0 - sg)))).astype(BF16)
        db_ref[...] = (du * sa).astype(BF16)

        @pl.when((i == n_i - 1) & (j == FF // tn - 1))
        def _():
            _all_rows(dg_ref)

    row = pl.BlockSpec((tm, D), lambda i, j: (i, 0))
    blk = pl.BlockSpec((tm, tn), lambda i, j: (i, j))
    return _hosted_call(
        body, name=name, grid=(n_i, FF // tn),
        in_specs=[row, pl.BlockSpec((1, D), lambda i, j: (0, 0)), row,
                  pl.BlockSpec((FF, D), lambda i, j: (0, 0)), blk, blk],
        out_specs=[row, blk, blk, pl.BlockSpec((8, D), lambda i, j: (0, 0))],
        out_shape=[jax.ShapeDtypeStruct((S, D), BF16), jax.ShapeDtypeStruct((S, FF), BF16),
                   jax.ShapeDtypeStruct((S, FF), BF16), jax.ShapeDtypeStruct((8, D), F32)],
        scratch_shapes=[pltpu.VMEM((tm, D), BF16)],
        semantics=("arbitrary", "arbitrary"), args=(dx, gate, f, w2, a, b), comm=comm)


def _tn_matmul(a, b, *, tm, tn, name, comm=None):
    S, M = a.shape
    N = b.shape[1]
    ts = _row_tile(S, 2048 if tm * tn <= 1408 * 1024 else 1024)
    ns = S // ts

    def body(a_ref, b_ref, o_ref):
        s = pl.program_id(2)
        p = lax.dot_general(a_ref[...], b_ref[...], (((0,), (0,)), ((), ())), preferred_element_type=F32)

        @pl.when(s == 0)
        def _():
            o_ref[...] = p

        @pl.when(s > 0)
        def _():
            o_ref[...] += p

    (out,), couts = _hosted_call(
        body, name=name, grid=(M // tm, N // tn, ns),
        in_specs=[pl.BlockSpec((ts, tm), lambda i, j, s: (s, i)), pl.BlockSpec((ts, tn), lambda i, j, s: (s, j))],
        out_specs=[pl.BlockSpec((tm, tn), lambda i, j, s: (i, j))],
        out_shape=[jax.ShapeDtypeStruct((M, N), F32)],
        scratch_shapes=[], semantics=("parallel", "parallel", "arbitrary"), args=(a, b), comm=comm)
    return out if comm is None else (out, couts)


def _dh_normbwd(pairs, x, ng, sc, dx_next, *, name, comm=None):
    S = x.shape[0]
    n_p = len(pairs)
    tm = _row_tile(S, 512)
    n_i = S // tm

    def body(*refs):
        a_refs, w_refs = refs[:n_p], refs[n_p:2 * n_p]
        x_ref, g_ref, sc_ref, dxn_ref, dx_ref, p_ref = refs[2 * n_p:]
        i = pl.program_id(0)
        dh = jnp.dot(a_refs[0][...], w_refs[0][...], preferred_element_type=F32)
        for k in range(1, n_p):
            dh = dh + jnp.dot(a_refs[k][...], w_refs[k][...], preferred_element_type=F32)
        xv = x_ref[...]
        r = _rms(xv)
        xh = xv * r
        g = g_ref[...]
        dn = dh * (1.0 + sc_ref[...])
        dy = dn * g
        dx_ref[...] = dxn_ref[...] + r * (dy - xh * jnp.mean(dy * xh, axis=-1, keepdims=True))

        @pl.when(i == 0)
        def _():
            p_ref[...] = jnp.zeros_like(p_ref)

        p_ref[:, 0:D] += _rsum8(dh * (xh * g))
        p_ref[:, D:2 * D] += _rsum8(dh)
        p_ref[:, 2 * D:3 * D] += _rsum8(dn * xh)

        @pl.when(i == n_i - 1)
        def _():
            _all_rows(p_ref)

    row = pl.BlockSpec((tm, D), lambda i: (i, 0))
    vec = pl.BlockSpec((1, D), lambda i: (0, 0))
    in_specs = ([pl.BlockSpec((tm, a.shape[1]), lambda i: (i, 0)) for a, _ in pairs]
                + [pl.BlockSpec(w.shape, lambda i: (0, 0), pipeline_mode=pl.Buffered(1)) for _, w in pairs]
                + [row, vec, vec, row])
    return _hosted_call(
        body, name=name, grid=(n_i,), in_specs=in_specs,
        out_specs=[row, pl.BlockSpec((8, 3 * D), lambda i: (0, 0))],
        out_shape=[jax.ShapeDtypeStruct((S, D), F32), jax.ShapeDtypeStruct((8, 3 * D), F32)],
        scratch_shapes=[], semantics=("arbitrary",),
        args=(*[a for a, _ in pairs], *[w for _, w in pairs], x, ng, sc, dx_next), comm=comm)


def _ffn_down_loss(u, w2, x, gate, gfin, tgt):
    S = x.shape[0]
    tm = _row_tile(S, 512)
    n_i = S // tm

    def body(u_ref, w2_ref, x_ref, gt_ref, g_ref, t_ref, dx_ref, f_ref, dg_ref, loss_ref, lacc):
        i = pl.program_id(0)
        f = jnp.dot(u_ref[...], w2_ref[...], preferred_element_type=F32)
        f_ref[...] = f.astype(BF16)
        xv = x_ref[...] + (0.5 * gt_ref[...]) * f
        r = _rms(xv)
        xh = xv * r
        g = g_ref[...]
        e = xh * g - t_ref[...]
        dout = e * (1.0 / D)
        dy = dout * g
        dx_ref[...] = r * (dy - xh * jnp.mean(dy * xh, axis=-1, keepdims=True))

        @pl.when(i == 0)
        def _():
            dg_ref[...] = jnp.zeros_like(dg_ref)
            lacc[...] = jnp.zeros_like(lacc)

        dg_ref[...] += _rsum8(dout * xh)
        lacc[...] += _rsum8(e * e)

        @pl.when(i == n_i - 1)
        def _():
            _all_rows(dg_ref)
            tot = jnp.sum(jnp.sum(lacc[...], axis=0, keepdims=True), axis=1, keepdims=True)
            loss_ref[...] = jnp.broadcast_to(tot * (0.5 / D), loss_ref.shape)

    row = pl.BlockSpec((tm, D), lambda i: (i, 0))
    vec = pl.BlockSpec((1, D), lambda i: (0, 0))
    return pl.pallas_call(
        body, name="ffn2_down_loss", grid=(n_i,),
        in_specs=[pl.BlockSpec((tm, FF), lambda i: (i, 0)), pl.BlockSpec((FF, D), lambda i: (0, 0)), row, vec, vec, row],
        out_specs=[row, row, pl.BlockSpec((8, D), lambda i: (0, 0)), pl.BlockSpec((8, 128), lambda i: (0, 0))],
        out_shape=[jax.ShapeDtypeStruct((S, D), F32), jax.ShapeDtypeStruct((S, D), BF16),
                   jax.ShapeDtypeStruct((8, D), F32), jax.ShapeDtypeStruct((8, 128), F32)],
        scratch_shapes=[pltpu.VMEM((8, D), F32)],
        compiler_params=_params(("arbitrary",)),
    )(u, w2, x, gate, gfin, tgt)


def _mix_in(x, ng, sh, sc, w_in, comm=None):
    S = x.shape[0]
    tm = _row_tile(S, 512)

    def body(x_ref, g_ref, sh_ref, sc_ref, w_ref, h_ref, z_ref):
        xv = x_ref[...]
        hb = (((xv * _rms(xv)) * g_ref[...]) * (1.0 + sc_ref[...]) + sh_ref[...]).astype(BF16)
        h_ref[...] = hb
        z_ref[...] = _dot_nt(hb, w_ref[...])

    row = pl.BlockSpec((tm, D), lambda i: (i, 0))
    vec = pl.BlockSpec((1, D), lambda i: (0, 0))
    return _hosted_call(
        body, name="mix_in", grid=(S // tm,),
        in_specs=[row, vec, vec, vec, pl.BlockSpec((Z_COLS, D), lambda i: (0, 0))],
        out_specs=[row, pl.BlockSpec((tm, Z_COLS), lambda i: (i, 0))],
        out_shape=[jax.ShapeDtypeStruct((S, D), BF16), jax.ShapeDtypeStruct((S, Z_COLS), F32)],
        scratch_shapes=[], semantics=("parallel",), args=(x, ng, sh, sc, w_in), comm=comm)


def _conv_taps(u, halo, rows):
    u1 = jnp.where(rows == 0, halo[7:8, :], pltpu.roll(u, 1, 0))
    u2 = jnp.where(rows == 0, halo[6:7, :], jnp.where(rows == 1, halo[7:8, :], pltpu.roll(u, 2, 0)))
    return u1, u2


def _mix_mid(z, conv_w, gq, gkv, wuq, wukv, cs, sn, comm=None):
    S = z.shape[0]
    tm = _row_tile(S, 512)
    hb = tm // 8

    def body(z_ref, zh_ref, cw_ref, gq_ref, gkv_ref, wuq_ref, wukv_ref, cs_ref, sn_ref,
             ya_ref, q_ref, k_ref, v_ref, cqn_ref, ckvn_ref):
        i = pl.program_id(0)
        xb = z_ref[:, 0:CONV_W]
        u = z_ref[:, CONV_W:2 * CONV_W] * z_ref[:, 2 * CONV_W:3 * CONV_W]
        halo = zh_ref[:, CONV_W:2 * CONV_W] * zh_ref[:, 2 * CONV_W:3 * CONV_W]
        halo = jnp.where(i > 0, halo, 0.0)
        rows = lax.broadcasted_iota(jnp.int32, (tm, CONV_W), 0)
        u1, u2 = _conv_taps(u, halo, rows)
        y = cw_ref[0:1, :] * u2 + cw_ref[1:2, :] * u1 + cw_ref[2:3, :] * u
        ya_ref[...] = xb * y

        lane = lax.broadcasted_iota(jnp.int32, (tm, 128), 1)
        cs_v, sn_v = cs_ref[...], sn_ref[...]
        cq = z_ref[:, 3 * CONV_W:3 * CONV_W + Q_LORA]
        cqn = ((cq * _rms(cq)) * gq_ref[...]).astype(BF16)
        cqn_ref[...] = cqn
        q = _dot_nt(cqn, wuq_ref[...])
        for h in range(HEADS):
            o = h * HEAD_SLOT
            q_ref[:, o:o + 128] = q[:, o:o + 128].astype(BF16)
            q_ref[:, o + 128:o + 256] = _rope(q[:, o + 128:o + 256], cs_v, sn_v, lane).astype(BF16)

        c0 = 3 * CONV_W + Q_LORA
        ckv = z_ref[:, c0:c0 + KV_LORA]
        ckvn = ((ckv * _rms(ckv)) * gkv_ref[...]).astype(BF16)
        ckvn_ref[...] = ckvn
        kv = jnp.dot(ckvn, wukv_ref[...], preferred_element_type=F32)
        krot = _rope(z_ref[:, c0 + KV_LORA:Z_COLS], cs_v, sn_v, lane).astype(BF16)
        for h in range(HEADS):
            o = h * HEAD_SLOT
            k_ref[:, o:o + 128] = kv[:, h * 128:(h + 1) * 128].astype(BF16)
            k_ref[:, o + 128:o + 256] = krot
        v_ref[...] = kv[:, HEADS * 128:].astype(BF16)

    def rows_of(n):
        return pl.BlockSpec((tm, n), lambda i: (i, 0))

    def whole(shape):
        return pl.BlockSpec(shape, lambda i: (0, 0))

    return _hosted_call(
        body, name="mix_mid", grid=(S // tm,),
        in_specs=[rows_of(Z_COLS), pl.BlockSpec((8, Z_COLS), lambda i: (jnp.maximum(i * hb - 1, 0), 0)),
                  whole((8, CONV_W)), whole((1, Q_LORA)), whole((1, KV_LORA)),
                  whole((HEADS * HEAD_SLOT, Q_LORA)), whole((KV_LORA, 2 * HEADS * 128)),
                  rows_of(128), rows_of(128)],
        out_specs=[rows_of(CONV_W), rows_of(HEADS * HEAD_SLOT), rows_of(HEADS * HEAD_SLOT), rows_of(HEADS * V_HEAD),
                   rows_of(Q_LORA), rows_of(KV_LORA)],
        out_shape=[jax.ShapeDtypeStruct((S, CONV_W), F32), jax.ShapeDtypeStruct((S, HEADS * HEAD_SLOT), BF16),
                   jax.ShapeDtypeStruct((S, HEADS * HEAD_SLOT), BF16), jax.ShapeDtypeStruct((S, HEADS * V_HEAD), BF16),
                   jax.ShapeDtypeStruct((S, Q_LORA), BF16), jax.ShapeDtypeStruct((S, KV_LORA), BF16)],
        scratch_shapes=[], semantics=("parallel",), args=(z, z, conv_w, gq, gkv, wuq, wukv, cs, sn), comm=comm)


def _att_blocks(S):
    bk = min(1024, max(S // 4, 128))
    return bk, bk


def _pair_tables(S, k_major):
    bq, bk = _att_blocks(S)
    nq, nk = S // bq, S // bk
    vis = lambda qi, ki: ki * bk < (qi + 1) * bq
    if k_major:
        pairs = [(qi, ki) for ki in range(nk) for qi in range(nq) if vis(qi, ki)]
    else:
        pairs = [(qi, ki) for qi in range(nq) for ki in range(nk) if vis(qi, ki)]
    cols = [[p[0] for p in pairs], [p[1] for p in pairs], [int((p[1] + 1) * bk > p[0] * bq) for p in pairs]]
    return [jnp.asarray(np.array(c, np.int32)) for c in cols], len(pairs)


def _chunk_mask(r0, nr, nc):
    r = (r0 + lax.broadcasted_iota(jnp.int32, (nr, nc), 0)) // CHUNK
    c = lax.broadcasted_iota(jnp.int32, (nr, nc), 1) // CHUNK
    return c <= r


def _diag_parts(bq, bk):
    return [(0, bq // 2, bk // 2), (bq // 2, bq // 2, bk)]


def _attention(q, k, v, comm=None):
    S = q.shape[0]
    bq, bk = _att_blocks(S)
    last_k = bq // bk - 1
    tables, n_pairs = _pair_tables(S, k_major=False)

    def body(qi_ref, ki_ref, mk_ref, q_ref, k_ref, v_ref, o_ref, lse_ref, m_s, l_s, acc_s):
        p_id = pl.program_id(1)
        qi, ki = qi_ref[p_id], ki_ref[p_id]

        @pl.when(ki == 0)
        def _():
            m_s[...] = jnp.full_like(m_s, NEG)
            l_s[...] = jnp.zeros_like(l_s)
            acc_s[...] = jnp.zeros_like(acc_s)

        def update(r0, nr, nc, masked):
            rows = slice(r0, r0 + nr)
            s = _dot_nt(q_ref[rows, :], k_ref[0:nc, :])
            if masked:
                s = jnp.where(_chunk_mask(r0, nr, nc), s, NEG)
            m_prev = m_s[rows, :]
            m_new = jnp.maximum(m_prev, jnp.max(s, axis=1, keepdims=True))
            alpha = jnp.exp2((m_prev - m_new) * EXP2_SCALE)
            p = jnp.exp2((s - jnp.tile(m_new, (1, nc // 128))) * EXP2_SCALE)
            l_s[rows, :] = alpha * l_s[rows, :] + jnp.sum(p, axis=1, keepdims=True)
            acc_s[rows, :] = alpha * acc_s[rows, :] + jnp.dot(p.astype(BF16), v_ref[0:nc, :],
                                                              preferred_element_type=F32)
            m_s[rows, :] = m_new

        @pl.when(mk_ref[p_id] == 0)
        def _():
            update(0, bq, bk, False)

        @pl.when(mk_ref[p_id] == 1)
        def _():
            for part in _diag_parts(bq, bk):
                update(*part, True)

        @pl.when(ki == qi * (last_k + 1) + last_k)
        def _():
            l = l_s[...]
            o_ref[...] = acc_s[...] / l
            lse_ref[...] = m_s[...] * EXP2_SCALE + jnp.log2(l)

    return _hosted_call(
        body, name="attention", grid=(HEADS, n_pairs),
        in_specs=[pl.BlockSpec((bq, HEAD_SLOT), lambda h, p, qt, kt, mt: (qt[p], h)),
                  pl.BlockSpec((bk, HEAD_SLOT), lambda h, p, qt, kt, mt: (kt[p], h)),
                  pl.BlockSpec((bk, V_HEAD), lambda h, p, qt, kt, mt: (kt[p], h))],
        out_specs=[pl.BlockSpec((bq, V_HEAD), lambda h, p, qt, kt, mt: (qt[p], h))] * 2,
        out_shape=[jax.ShapeDtypeStruct((S, HEADS * V_HEAD), F32)] * 2,
        scratch_shapes=[pltpu.VMEM((bq, V_HEAD), F32)] * 3,
        semantics=("arbitrary", "arbitrary"), args=(q, k, v), comm=comm, prefetch=tables)


def _attention_bwd(q, k, v, do, lse2, delta):
    S = q.shape[0]
    bq, bk = _att_blocks(S)
    nq = S // bq
    tables, n_pairs = _pair_tables(S, k_major=True)

    def body(qi_ref, ki_ref, mk_ref, q_ref, k_ref, v_ref, do_ref, lse_ref, dl_ref, dq_hbm, dk_ref, dv_ref,
             dq_s, dq_b, dk_s, dv_s, sem):
        head, p_id = pl.program_id(0), pl.program_id(1)
        qi, ki = qi_ref[p_id], ki_ref[p_id]

        @pl.when(qi * bq <= ki * bk)
        def _():
            dk_s[...] = jnp.zeros_like(dk_s)
            dv_s[...] = jnp.zeros_like(dv_s)

        def update(r0, nr, nc, masked):
            rows, cols = slice(r0, r0 + nr), slice(0, nc)
            qv, kv, dov = q_ref[rows, :], k_ref[cols, :], do_ref[rows, :]
            s = _dot_nt(qv, kv)
            dp = _dot_nt(dov, v_ref[cols, :])
            if masked:
                s = jnp.where(_chunk_mask(r0, nr, nc), s, NEG)
            p = jnp.exp2(s * EXP2_SCALE - jnp.tile(lse_ref[rows, :], (1, nc // 128)))
            dv_s[cols, :] += lax.dot_general(p.astype(BF16), dov, (((0,), (0,)), ((), ())),
                                             preferred_element_type=F32)
            ds = (p * (dp - jnp.tile(dl_ref[rows, :], (1, nc // 128)))).astype(BF16)
            dk_s[cols, :] += lax.dot_general(ds, qv, (((0,), (0,)), ((), ())), preferred_element_type=F32)
            dq = jnp.dot(ds, kv, preferred_element_type=F32)
            out_rows = pl.ds(pl.multiple_of(qi * bq + r0, nr), nr)

            @pl.when(ki == 0)
            def _():
                dq_s[out_rows, :] = dq

            @pl.when(ki > 0)
            def _():
                dq_s[out_rows, :] += dq

        @pl.when(mk_ref[p_id] == 0)
        def _():
            update(0, bq, bk, False)

        @pl.when(mk_ref[p_id] == 1)
        def _():
            for part in _diag_parts(bq, bk):
                update(*part, True)

        @pl.when(qi == nq - 1)
        def _():
            dk_ref[...] = (dk_s[...] * ATT_SCALE).astype(BF16)
            dv_ref[...] = dv_s[...].astype(BF16)

        @pl.when(p_id == n_pairs - 1)
        def _():
            dq_b[...] = (dq_s[...] * ATT_SCALE).astype(BF16)
            out = pltpu.make_async_copy(
                dq_b, dq_hbm.at[:, pl.ds(pl.multiple_of(head * HEAD_SLOT, HEAD_SLOT), HEAD_SLOT)], sem)
            out.start()
            out.wait()

    grid_spec = pltpu.PrefetchScalarGridSpec(
        num_scalar_prefetch=3, grid=(HEADS, n_pairs),
        in_specs=[pl.BlockSpec((bq, HEAD_SLOT), lambda h, p, qt, kt, mt: (qt[p], h)),
                  pl.BlockSpec((bk, HEAD_SLOT), lambda h, p, qt, kt, mt: (kt[p], h)),
                  pl.BlockSpec((bk, V_HEAD), lambda h, p, qt, kt, mt: (kt[p], h)),
                  pl.BlockSpec((bq, V_HEAD), lambda h, p, qt, kt, mt: (qt[p], h)),
                  pl.BlockSpec((bq, V_HEAD), lambda h, p, qt, kt, mt: (qt[p], h)),
                  pl.BlockSpec((bq, V_HEAD), lambda h, p, qt, kt, mt: (qt[p], h))],
        out_specs=[pl.BlockSpec(memory_space=pl.ANY),
                   pl.BlockSpec((bk, HEAD_SLOT), lambda h, p, qt, kt, mt: (kt[p], h)),
                   pl.BlockSpec((bk, V_HEAD), lambda h, p, qt, kt, mt: (kt[p], h))],
        scratch_shapes=[pltpu.VMEM((S, HEAD_SLOT), F32), pltpu.VMEM((S, HEAD_SLOT), BF16),
                        pltpu.VMEM((bk, HEAD_SLOT), F32), pltpu.VMEM((bk, V_HEAD), F32), pltpu.SemaphoreType.DMA])
    return pl.pallas_call(
        body, name="attention_bwd", grid_spec=grid_spec,
        out_shape=[jax.ShapeDtypeStruct((S, HEADS * HEAD_SLOT), BF16), jax.ShapeDtypeStruct((S, HEADS * HEAD_SLOT), BF16),
                   jax.ShapeDtypeStruct((S, HEADS * V_HEAD), BF16)],
        compiler_params=_params(("arbitrary", "arbitrary")),
    )(*tables, q, k, v, do, lse2, delta)


def _group_mats():
    def blockdiag(n, g):
        idx = np.arange(n) // g
        return jnp.asarray((idx[:, None] == idx[None, :]).astype(np.float32), dtype=BF16)
    return blockdiag(CONV_W, CONV_GROUP), blockdiag(HEADS * V_HEAD, V_HEAD)


def _mix_out(ya, o, gout, w_out, x, gate, ga, gb):
    S = x.shape[0]
    tm = _row_tile(S, 1024)

    def body(ya_ref, o_ref, go_ref, w_ref, x_ref, g_ref, ga_ref, gb_ref, xo_ref, yn_ref, yo_ref):
        yav, ov = ya_ref[...], o_ref[...]
        ra = lax.rsqrt(_gsum(yav * yav, ga_ref[...]) * (1.0 / CONV_GROUP) + EPS)
        rb = lax.rsqrt(_gsum(ov * ov, gb_ref[...]) * (1.0 / V_HEAD) + EPS)
        na = ((yav * ra) * go_ref[:, 0:CONV_W]).astype(BF16)
        nb = ((ov * rb) * go_ref[:, CONV_W:]).astype(BF16)
        yn_ref[:, 0:CONV_W] = na
        yn_ref[:, CONV_W:] = nb
        yo = (jnp.dot(na, w_ref[0:CONV_W, :], preferred_element_type=F32)
              + jnp.dot(nb, w_ref[CONV_W:, :], preferred_element_type=F32))
        xo_ref[...] = x_ref[...] + g_ref[...] * yo
        yo_ref[...] = yo.astype(BF16)

    row = pl.BlockSpec((tm, D), lambda i: (i, 0))
    half = pl.BlockSpec((tm, CONV_W), lambda i: (i, 0))
    vec = pl.BlockSpec((1, D), lambda i: (0, 0))
    sq = pl.BlockSpec((CONV_W, CONV_W), lambda i: (0, 0))
    return pl.pallas_call(
        body, name="mix_out", grid=(S // tm,),
        in_specs=[half, half, vec, pl.BlockSpec((D, D), lambda i: (0, 0)), row, vec, sq, sq],
        out_specs=[row, row, row],
        out_shape=[jax.ShapeDtypeStruct((S, D), F32), jax.ShapeDtypeStruct((S, D), BF16),
                   jax.ShapeDtypeStruct((S, D), BF16)],
        compiler_params=_params(("parallel",)),
    )(ya, o, gout, w_out, x, gate, ga, gb)


def _mix_out_bwd(dx, gate, yo, w_out, ya, o, gout, ga, gb, comm=None):
    S = dx.shape[0]
    tm = _row_tile(S, 512)
    n_i = S // tm

    def norm_bwd(v, dn, gain, gmat, inv_n):
        r = lax.rsqrt(_gsum(v * v, gmat) * inv_n + EPS)
        vh = v * r
        dy = dn * gain
        return r * (dy - vh * (_gsum(dy * vh, gmat) * inv_n)), dn * vh

    def body(dx_ref, g_ref, yo_ref, w_ref, ya_ref, o_ref, go_ref, ga_ref, gb_ref,
             dyo_ref, dya_ref, do_ref, dl_ref, p_ref):
        i = pl.program_id(0)
        dxv = dx_ref[...]
        dyo = (dxv * g_ref[...]).astype(BF16)
        dyo_ref[...] = dyo
        dyn = _dot_nt(dyo, w_ref[...])
        dya, dga = norm_bwd(ya_ref[...], dyn[:, 0:CONV_W], go_ref[:, 0:CONV_W], ga_ref[...], 1.0 / CONV_GROUP)
        ov = o_ref[...]
        do, dgb = norm_bwd(ov, dyn[:, CONV_W:], go_ref[:, CONV_W:], gb_ref[...], 1.0 / V_HEAD)
        dya_ref[...] = dya
        do_ref[...] = do.astype(BF16)
        dl_ref[...] = _gsum(do * ov, gb_ref[...], split=True)

        @pl.when(i == 0)
        def _():
            p_ref[...] = jnp.zeros_like(p_ref)

        p_ref[:, 0:D] += _rsum8(dxv * yo_ref[...].astype(F32))
        p_ref[:, D:D + CONV_W] += _rsum8(dga)
        p_ref[:, D + CONV_W:2 * D] += _rsum8(dgb)

        @pl.when(i == n_i - 1)
        def _():
            _all_rows(p_ref)

    row = pl.BlockSpec((tm, D), lambda i: (i, 0))
    half = pl.BlockSpec((tm, CONV_W), lambda i: (i, 0))
    vec = pl.BlockSpec((1, D), lambda i: (0, 0))
    sq = pl.BlockSpec((CONV_W, CONV_W), lambda i: (0, 0))
    return _hosted_call(
        body, name="mix_out_bwd", grid=(n_i,),
        in_specs=[row, vec, row, pl.BlockSpec((D, D), lambda i: (0, 0)), half, half, vec, sq, sq],
        out_specs=[row, half, half, half, pl.BlockSpec((8, 2 * D), lambda i: (0, 0))],
        out_shape=[jax.ShapeDtypeStruct((S, D), BF16), jax.ShapeDtypeStruct((S, CONV_W), F32),
                   jax.ShapeDtypeStruct((S, CONV_W), BF16), jax.ShapeDtypeStruct((S, CONV_W), F32),
                   jax.ShapeDtypeStruct((8, 2 * D), F32)],
        scratch_shapes=[], semantics=("arbitrary",), args=(dx, gate, yo, w_out, ya, o, gout, ga, gb), comm=comm)


MID_SUMS = 3 * CONV_W + Q_LORA + KV_LORA


def _mix_mid_bwd(z, dya, conv_w, gq, gkv, wuq, wukv, cs, sn, dq, dk, dv, comm=None):
    S = z.shape[0]
    tm = _row_tile(S, 256)
    n_i = S // tm
    hb = tm // 8
    last_blk = S // 8 - 1

    def latent_bwd(cv, dcn, gain):
        r = _rms(cv)
        ch = cv * r
        dy = dcn * gain
        return r * (dy - ch * jnp.mean(dy * ch, axis=-1, keepdims=True)), dcn * ch

    def body(z_ref, zp_ref, zn_ref, dya_ref, dyan_ref, cw_ref, gq_ref, gkv_ref, wuq_ref, wukv_ref, cs_ref, sn_ref,
             dq_ref, dk_ref, dv_ref, dz_ref, dqf_ref, dkvf_ref, p_ref):
        i = pl.program_id(0)
        xb, xc, xu = z_ref[:, 0:CONV_W], z_ref[:, CONV_W:2 * CONV_W], z_ref[:, 2 * CONV_W:3 * CONV_W]
        u = xc * xu
        halo = jnp.where(i > 0, zp_ref[:, CONV_W:2 * CONV_W] * zp_ref[:, 2 * CONV_W:3 * CONV_W], 0.0)
        rows = lax.broadcasted_iota(jnp.int32, (tm, CONV_W), 0)
        u1, u2 = _conv_taps(u, halo, rows)
        w0, w1, w2 = cw_ref[0:1, :], cw_ref[1:2, :], cw_ref[2:3, :]
        y = w0 * u2 + w1 * u1 + w2 * u
        dyav = dya_ref[...]
        dy = dyav * xb
        nxt = jnp.where(i < n_i - 1, dyan_ref[...] * zn_ref[:, 0:CONV_W], 0.0)
        dy1 = jnp.where(rows == tm - 1, nxt[0:1, :], pltpu.roll(dy, tm - 1, 0))
        dy2 = jnp.where(rows == tm - 1, nxt[1:2, :], jnp.where(rows == tm - 2, nxt[0:1, :], pltpu.roll(dy, tm - 2, 0)))
        du = w2 * dy + w1 * dy1 + w0 * dy2
        dz_ref[:, 0:CONV_W] = (dyav * y).astype(BF16)
        dz_ref[:, CONV_W:2 * CONV_W] = (du * xu).astype(BF16)
        dz_ref[:, 2 * CONV_W:3 * CONV_W] = (du * xc).astype(BF16)

        lane = lax.broadcasted_iota(jnp.int32, (tm, 128), 1)
        cs_v, sn_v = cs_ref[...], sn_ref[...]
        dkr = jnp.zeros((tm, 128), F32)
        for h in range(HEADS):
            o = h * HEAD_SLOT
            dqf_ref[:, o:o + 128] = dq_ref[:, o:o + 128]
            dqf_ref[:, o + 128:o + 256] = _rope_t(dq_ref[:, o + 128:o + 256].astype(F32), cs_v, sn_v, lane).astype(BF16)
            dkvf_ref[:, h * 128:(h + 1) * 128] = dk_ref[:, o:o + 128]
            dkr = dkr + dk_ref[:, o + 128:o + 256].astype(F32)
        dkvf_ref[:, HEADS * 128:] = dv_ref[...]

        c0 = 3 * CONV_W
        dcqn = jnp.dot(dqf_ref[...], wuq_ref[...], preferred_element_type=F32)
        dcq, dgq = latent_bwd(z_ref[:, c0:c0 + Q_LORA], dcqn, gq_ref[...])
        dz_ref[:, c0:c0 + Q_LORA] = dcq.astype(BF16)
        c1 = c0 + Q_LORA
        dckvn = _dot_nt(dkvf_ref[...], wukv_ref[...])
        dckv, dgkv = latent_bwd(z_ref[:, c1:c1 + KV_LORA], dckvn, gkv_ref[...])
        dz_ref[:, c1:c1 + KV_LORA] = dckv.astype(BF16)
        dz_ref[:, c1 + KV_LORA:Z_COLS] = _rope_t(dkr, cs_v, sn_v, lane).astype(BF16)

        @pl.when(i == 0)
        def _():
            p_ref[...] = jnp.zeros_like(p_ref)

        p_ref[:, 0:CONV_W] += _rsum8(dy * u2)
        p_ref[:, CONV_W:2 * CONV_W] += _rsum8(dy * u1)
        p_ref[:, 2 * CONV_W:3 * CONV_W] += _rsum8(dy * u)
        p_ref[:, c0:c0 + Q_LORA] += _rsum8(dgq)
        p_ref[:, c1:c1 + KV_LORA] += _rsum8(dgkv)

        @pl.when(i == n_i - 1)
        def _():
            _all_rows(p_ref)

    def rows_of(n):
        return pl.BlockSpec((tm, n), lambda i: (i, 0))

    def whole(shape):
        return pl.BlockSpec(shape, lambda i: (0, 0))

    def prev8(n):
        return pl.BlockSpec((8, n), lambda i: (jnp.maximum(i * hb - 1, 0), 0))

    def next8(n):
        return pl.BlockSpec((8, n), lambda i: (jnp.minimum((i + 1) * hb, last_blk), 0))

    return _hosted_call(
        body, name="mix_mid_bwd", grid=(n_i,),
        in_specs=[rows_of(Z_COLS), prev8(Z_COLS), next8(Z_COLS), rows_of(CONV_W), next8(CONV_W),
                  whole((8, CONV_W)), whole((1, Q_LORA)), whole((1, KV_LORA)),
                  whole((HEADS * HEAD_SLOT, Q_LORA)), whole((KV_LORA, 2 * HEADS * 128)),
                  rows_of(128), rows_of(128),
                  rows_of(HEADS * HEAD_SLOT), rows_of(HEADS * HEAD_SLOT), rows_of(HEADS * V_HEAD)],
        out_specs=[rows_of(Z_COLS), rows_of(HEADS * HEAD_SLOT), rows_of(2 * HEADS * 128), whole((8, MID_SUMS))],
        out_shape=[jax.ShapeDtypeStruct((S, Z_COLS), BF16), jax.ShapeDtypeStruct((S, HEADS * HEAD_SLOT), BF16),
                   jax.ShapeDtypeStruct((S, 2 * HEADS * 128), BF16), jax.ShapeDtypeStruct((8, MID_SUMS), F32)],
        scratch_shapes=[], semantics=("arbitrary",),
        args=(z, z, z, dya, dya, conv_w, gq, gkv, wuq, wukv, cs, sn, dq, dk, dv), comm=comm)


ADA_Q = N_MOD * D // N_CHIPS
ADA_TN = 768


def _ada_forward(c_all, ada_w_q, ada_b_q):
    def body(c_ref, w_ref, b_ref, o_ref):
        cv = c_ref[...]
        sc = (cv * jax.nn.sigmoid(cv)).astype(BF16)
        o_ref[...] = jnp.dot(sc, w_ref[...].astype(BF16), preferred_element_type=F32) + b_ref[...]

    return pl.pallas_call(
        body, name="ada_forward", grid=(ADA_Q // ADA_TN,),
        in_specs=[pl.BlockSpec((16, D), lambda j: (0, 0)), pl.BlockSpec((D, ADA_TN), lambda j: (0, j)),
                  pl.BlockSpec((1, ADA_TN), lambda j: (0, j))],
        out_specs=pl.BlockSpec((16, ADA_TN), lambda j: (0, j)),
        out_shape=jax.ShapeDtypeStruct((16, ADA_Q), F32),
        compiler_params=_params(("parallel",)),
    )(c_all, ada_w_q, ada_b_q)


def _ada_wgrad(c_all, dmod_q):
    def body(c_ref, d_ref, o_ref):
        cv = c_ref[...]
        sc = (cv * jax.nn.sigmoid(cv)).astype(BF16)
        o_ref[...] = lax.dot_general(sc, d_ref[...].astype(BF16), (((0,), (0,)), ((), ())),
                                     preferred_element_type=F32)

    return pl.pallas_call(
        body, name="ada_wgrad", grid=(ADA_Q // ADA_TN,),
        in_specs=[pl.BlockSpec((16, D), lambda j: (0, 0)), pl.BlockSpec((16, ADA_TN), lambda j: (0, j))],
        out_specs=pl.BlockSpec((D, ADA_TN), lambda j: (0, j)),
        out_shape=jax.ShapeDtypeStruct((D, ADA_Q), F32),
        compiler_params=_params(("parallel",)),
    )(c_all, dmod_q)


def _sum_devices(parts):
    n = parts.shape[1]

    def body(p_ref, o_ref):
        o_ref[...] = jnp.broadcast_to(jnp.sum(p_ref[...], axis=0, keepdims=True), o_ref.shape)

    return pl.pallas_call(
        body, name="sum_devices",
        in_specs=[pl.BlockSpec((N_DEV, n), lambda: (0, 0))], out_specs=pl.BlockSpec((N_DEV, n), lambda: (0, 0)),
        out_shape=jax.ShapeDtypeStruct((N_DEV, n), F32),
    )(parts)


def _adamw(ws, gs, ms, vs, *, name, comm=None):
    n = len(ws)
    _, rows, cols = ws[0].shape
    tr = _row_tile(rows, 256)

    def body(*refs):
        ins, outs = refs[:4 * n], refs[4 * n:]
        for k in range(n):
            w_ref, g_ref, m_ref, v_ref = ins[k], ins[n + k], ins[2 * n + k], ins[3 * n + k]
            go_ref, d_ref, mo_ref, vo_ref = outs[4 * k:4 * k + 4]
            gv = g_ref[...]
            mn = B1 * m_ref[0] + (1.0 - B1) * gv
            vn = B2 * v_ref[0] + (1.0 - B2) * (gv * gv)
            m_hat = mn / (1.0 - B1 ** STEP)
            v_hat = vn / (1.0 - B2 ** STEP)
            go_ref[0] = gv
            d_ref[0] = -LR * (m_hat / (jnp.sqrt(v_hat) + AEPS) + WD * w_ref[0])
            mo_ref[0] = mn
            vo_ref[0] = vn

    blk = pl.BlockSpec((1, tr, cols), lambda i: (0, i, 0))
    gblk = pl.BlockSpec((tr, cols), lambda i: (i, 0))
    res, couts = _hosted_call(
        body, name=name, grid=(rows // tr,),
        in_specs=[blk] * n + [gblk] * n + [blk] * 2 * n, out_specs=[blk] * 4 * n,
        out_shape=[jax.ShapeDtypeStruct((1, rows, cols), F32)] * 4 * n,
        scratch_shapes=[], semantics=("parallel",), args=(*ws, *gs, *ms, *vs), comm=comm)
    return [res[4 * k:4 * k + 4] for k in range(n)], couts


def _small_allgather(v, *, name):
    m, n = v.shape

    def body(x_ref, out_ref, send_sems, recv_sems, local_sem):
        x, y, c = _place()
        me, sibling = (x, y, c), (x, y, 1 - c)
        chips = [(1 - x, y), (x, 1 - y), (1 - x, 1 - y)]

        def rows(px, py, pc):
            return out_ref.at[pl.ds((4 * px + 2 * py + pc) * m, m), :]

        def copy(k, block, to, src=None):
            return pltpu.make_async_remote_copy(
                src_ref=rows(*block) if src is None else src, dst_ref=rows(*block),
                send_sem=send_sems.at[k], recv_sem=recv_sems.at[k], device_id=to, device_id_type=MESH)

        mine = pltpu.make_async_copy(x_ref, rows(*me), local_sem)
        mine.start()
        first = [copy(0, me, sibling, src=x_ref)]
        first += [copy(1 + j, me, (*chip, c), src=x_ref) for j, chip in enumerate(chips)]
        for cp in first:
            cp.start()
        passed = [copy(4 + j, (*chip, c), sibling) for j, chip in enumerate(chips)]
        for j, chip in enumerate(chips):
            copy(1 + j, (*chip, c), me).wait_recv()
            passed[j].start()
        copy(0, sibling, me).wait_recv()
        for j, chip in enumerate(chips):
            copy(4 + j, (*chip, 1 - c), me).wait_recv()
        for cp in first + passed:
            cp.wait_send()
        mine.wait()

    return pl.pallas_call(
        body, name=name,
        out_shape=jax.ShapeDtypeStruct((N_DEV * m, n), v.dtype),
        in_specs=[pl.BlockSpec(memory_space=pltpu.VMEM)], out_specs=pl.BlockSpec(memory_space=pltpu.VMEM),
        scratch_shapes=[pltpu.SemaphoreType.DMA((7,)), pltpu.SemaphoreType.DMA((7,)), pltpu.SemaphoreType.DMA],
    )(v)


ADD_BLOCKS = 2


def _pair_add(place, gs, ts, *, name):
    n_a = len(gs)

    def body(pl_ref, *refs):
        g_refs, t_refs = refs[:n_a], refs[n_a:2 * n_a]
        pf_refs, pb_refs = refs[2 * n_a:3 * n_a], refs[3 * n_a:]
        own = pl.program_id(1) == pl_ref[1]
        for g_ref, t_ref, pf_ref, pb_ref in zip(g_refs, t_refs, pf_refs, pb_refs):
            s = g_ref[...] + t_ref[...]
            pb_ref[...] = s.astype(BF16)

            @pl.when(own)
            def _():
                pf_ref[...] = s[0]

    def blk(t, own_half):
        tr = t.shape[1] // ADD_BLOCKS
        if own_half:
            return pl.BlockSpec((1, tr, t.shape[2]), lambda r, q, p: (q, p[0] * ADD_BLOCKS + r, 0))
        return pl.BlockSpec((1, tr, t.shape[2]), lambda r, q, p: (q, r, 0))

    def own_blk(t):
        return pl.BlockSpec((t.shape[1] // ADD_BLOCKS, t.shape[2]), lambda r, q, p: (r, 0))

    grid_spec = pltpu.PrefetchScalarGridSpec(
        num_scalar_prefetch=1, grid=(ADD_BLOCKS, N_CHIPS),
        in_specs=[blk(t, True) for t in ts] + [blk(t, False) for t in ts],
        out_specs=[own_blk(t) for t in ts] + [blk(t, False) for t in ts])
    res = pl.pallas_call(
        body, name=name, grid_spec=grid_spec,
        out_shape=([jax.ShapeDtypeStruct(t.shape[1:], F32) for t in ts]
                   + [jax.ShapeDtypeStruct(t.shape, BF16) for t in ts]),
        compiler_params=_params(("parallel", "arbitrary")),
    )(place, *gs, *ts)
    return list(res[:n_a]), list(res[n_a:])


def _chip_add(place, pfs, ts, *, name):
    n_a = len(pfs)

    def body(pl_ref, *refs):
        pf_refs, t_refs, o_refs = refs[:n_a], refs[n_a:4 * n_a], refs[4 * n_a:]
        for i, (pf_ref, o_ref) in enumerate(zip(pf_refs, o_refs)):
            t1, t2, t3 = t_refs[3 * i:3 * i + 3]
            o_ref[...] = ((pf_ref[...] + t1[0].astype(F32)) + t2[0].astype(F32)) + t3[0].astype(F32)

    def slot(t, j):
        return pl.BlockSpec((1, t.shape[1] // ADD_BLOCKS, t.shape[2]), lambda r, p: (p[1] ^ j, r, 0))

    def half(t):
        return pl.BlockSpec((t.shape[1] // ADD_BLOCKS, t.shape[2]), lambda r, p: (r, 0))

    grid_spec = pltpu.PrefetchScalarGridSpec(
        num_scalar_prefetch=1, grid=(ADD_BLOCKS,),
        in_specs=[half(t) for t in ts] + [slot(t, j) for t in ts for j in (1, 2, 3)],
        out_specs=[half(t) for t in ts])
    res = pl.pallas_call(
        body, name=name, grid_spec=grid_spec,
        out_shape=[jax.ShapeDtypeStruct(t.shape[1:], F32) for t in ts],
        compiler_params=_params(("parallel",)),
    )(place, *pfs, *[t for t in ts for _ in range(3)])
    return list(res)


BULK = [("ffn1_w1", "colsT"), ("ffn1_w3", "colsT"), ("ffn1_w2", "rows"), ("w_in", "cols"), ("w_uq", "colsT"),
        ("w_ukv", "cols"), ("w_out", "rows"), ("ffn2_w1", "colsT"), ("ffn2_w3", "colsT"), ("ffn2_w2", "rows")]
KIND = dict(BULK)


def _group(*names):
    return [b for b in BULK if b[0] in names]


W_FIRST = _group("ffn1_w1", "ffn1_w3")
W_REST = [b for b in BULK if b not in W_FIRST]
W_MIX = _group("ffn1_w2", "w_in", "w_uq", "w_ukv", "w_out")
W_FFN2 = _group("ffn2_w1", "ffn2_w3", "ffn2_w2")
G_FFN2 = _group("ffn2_w1", "ffn2_w3", "ffn2_w2")
G_MIX = _group("w_in", "w_uq", "w_ukv", "w_out")
G_FFN1 = _group("ffn1_w1", "ffn1_w3", "ffn1_w2")


def _gathered_weights(specs, shards, got, myq):
    out = {}
    for (name, kind), part in zip(specs, got):
        part = lax.dynamic_update_slice_in_dim(part, shards[name][None], myq, axis=0)
        out[name] = _full_weight(part, kind)
    return out


def _working_shard(w, kind):
    return jnp.swapaxes(w, 1, 2)[0] if kind == "colsT" else w[0]


def _full_weight(parts, kind):
    if kind == "cols":
        return jnp.transpose(parts, (1, 0, 2)).reshape(parts.shape[1], -1)
    return parts.reshape(-1, parts.shape[2])


def _quarters(g, kind):
    if kind == "cols":
        k, n = g.shape
        return jnp.transpose(g.reshape(k, N_CHIPS, n // N_CHIPS), (1, 0, 2))
    return g.reshape(N_CHIPS, g.shape[0] // N_CHIPS, g.shape[1])


def _pad_heads(w_uq_t):
    w = w_uq_t.reshape(HEADS, QK_NOPE + QK_ROPE, Q_LORA)
    return jnp.pad(w, ((0, 0), (0, HEAD_SLOT - QK_NOPE - QK_ROPE), (0, 0))).reshape(HEADS * HEAD_SLOT, Q_LORA)


def _unpad_heads(g):
    return g.reshape(HEADS, HEAD_SLOT, Q_LORA)[:, :QK_NOPE + QK_ROPE].reshape(HEADS * (QK_NOPE + QK_ROPE), Q_LORA)


def _split_kv(w_ukv):
    return jnp.transpose(w_ukv.reshape(KV_LORA, HEADS, 2, 128), (0, 2, 1, 3)).reshape(KV_LORA, 2 * HEADS * 128)


def _merge_kv(g):
    return jnp.transpose(g.reshape(KV_LORA, 2, HEADS, 128), (0, 2, 1, 3)).reshape(KV_LORA, 2 * HEADS * 128)


def _rope_tables(positions):
    inv_freq = ROPE_THETA ** (-jnp.arange(0, QK_ROPE, 2, dtype=F32) / QK_ROPE)
    ang = positions.astype(F32)[:, None] * inv_freq
    cos, sin, zero = jnp.cos(ang), jnp.sin(ang), jnp.zeros((positions.shape[0], 64), F32)
    return jnp.concatenate([cos, cos, zero], axis=1), jnp.concatenate([sin, sin, zero], axis=1)


def _assemble(place, specs, rhs, others):
    south = place[0] == 0
    return {b[0]: jnp.concatenate([jnp.where(south, rh, ot), jnp.where(south, ot, rh)], axis=0)
            for b, rh, ot in zip(specs, rhs, others)}


def _local_step(x, positions, target, mod, vec, conv_w, w_first, rest_shards, place, tail_host=None):
    row = lambda k: mod[k:k + 1]
    sh1, sc1, g1, sh2, sc2, g2, sh3, sc3, g3 = [row(k) for k in range(N_MOD)]
    cs, sn = _rope_tables(positions)
    cw8 = jnp.pad(conv_w, ((0, 5), (0, 0)))
    ga, gb = _group_mats()
    dist = place is not None
    comm = lambda prog: prog if dist else None

    w = dict(w_first) if dist else {**rest_shards, **w_first}

    def gather(specs):
        return _Gather([rest_shards[b[0]] for b in specs]) if dist else None

    def arrived(specs, got):
        if dist:
            w.update(_gathered_weights(specs, rest_shards, got, place[1]))

    (h1, a1, b1, u1), got = _ffn_up(x, vec["norm_ffn1_g"], sh1, sc1, w["ffn1_w1"], w["ffn1_w3"],
                                    name="ffn1_up", comm=gather(W_MIX))
    arrived(W_MIX, got)
    w_in = jnp.pad(w["w_in"].T, ((0, Z_COLS - IN_COLS), (0, 0)))
    wuq = _pad_heads(w["w_uq"])
    wukv = _split_kv(w["w_ukv"])
    (x1, f1), _ = _ffn_down(u1, w["ffn1_w2"], x, g1, name="ffn1_down")
    (h2, z), _ = _mix_in(x1, vec["norm_mix_g"], sh2, sc2, w_in)
    (ya, q, k, v, cqn, ckvn), _ = _mix_mid(z, cw8, vec["q_norm_g"], vec["kv_norm_g"], wuq, wukv, cs, sn)
    (o, lse), got = _attention(q, k, v, comm=gather(W_FFN2))
    arrived(W_FFN2, got)
    x2, yn, yo = _mix_out(ya, o, vec["out_norm_g"], w["w_out"], x1, g2, ga, gb)
    (h3, a3, b3, u3), _ = _ffn_up(x2, vec["norm_ffn2_g"], sh3, sc3, w["ffn2_w1"], w["ffn2_w3"], name="ffn2_up")
    dx3, f3, dgfin, loss_blk = _ffn_down_loss(u3, w["ffn2_w2"], x2, g3, vec["final_norm_g"], target)

    grads, reduced = {}, {}

    def tn(a, b, tm, tn_, name, prog=None):
        if prog is None:
            return _tn_matmul(a, b, tm=tm, tn=tn_, name=name), None
        return _tn_matmul(a, b, tm=tm, tn=tn_, name=name, comm=prog)

    def slab_of(specs):
        return [_quarters(grads[n], kind) for n, kind in specs]

    (df3, da3, db3, dg3), _ = _ffn_bwd_du(dx3, g3, f3, w["ffn2_w2"], a3, b3, name="ffn2_bwd_du")
    grads["ffn2_w2"], _ = tn(u3, df3, FF // 2, D, "ffn2_dw2")
    grads["ffn2_w1"], _ = tn(da3, h3, FF // 2, D, "ffn2_dw1")
    grads["ffn2_w3"], _ = tn(db3, h3, FF // 2, D, "ffn2_dw3")
    (dx2, s3), _ = _dh_normbwd([(da3, w["ffn2_w1"]), (db3, w["ffn2_w3"])], x2, vec["norm_ffn2_g"], sc3, dx3,
                               name="ffn2_bwd_dh")

    p1 = slab_of(G_FFN2) if dist else None
    (dyo, dya, do, delta, s_out), t1 = _mix_out_bwd(dx2, g2, yo, w["w_out"], ya, o, vec["out_norm_g"], ga, gb,
                                                    comm=comm(_PairExchange(p1) if dist else None))
    grads["w_out"], _ = tn(yn, dyo, D, D, "dw_out")
    if dist:
        pf1, pb1 = _pair_add(place, p1, t1, name="ffn2g_pair_add")
    dq, dk, dv = _attention_bwd(q, k, v, do, lse, delta)
    (dz, dqf, dkvf, s_mid), t2 = _mix_mid_bwd(z, dya, cw8, vec["q_norm_g"], vec["kv_norm_g"], wuq, wukv, cs, sn,
                                              dq, dk, dv, comm=comm(_ChipExchange(pb1) if dist else None))
    g_uq, _ = tn(dqf, cqn, HEADS * HEAD_SLOT, Q_LORA, "dw_uq")
    g_ukv, _ = tn(ckvn, dkvf, KV_LORA, 2 * HEADS * 128, "dw_ukv")
    grads["w_uq"], grads["w_ukv"] = _unpad_heads(g_uq), _merge_kv(g_ukv)
    g_in, _ = tn(h2, dz, D, Z_COLS, "dw_in")
    grads["w_in"] = g_in[:, :IN_COLS]
    (dx1, s2), _ = _dh_normbwd([(dz, w_in)], x1, vec["norm_mix_g"], sc2, dx2, name="mix_bwd_dh")

    p2 = slab_of(G_MIX) if dist else None
    rh_ffn2 = _chip_add(place, pf1, t2, name="ffn2g_chip_add") if dist else None
    (df1, da1, db1, dg1), got = _ffn_bwd_du(dx1, g1, f1, w["ffn1_w2"], a1, b1, name="ffn1_bwd_du",
                                            comm=comm(_Multi([_PairExchange(p2), _PairShare(rh_ffn2)]) if dist else None))
    if dist:
        t1 = got[:len(p2)]
        reduced.update(_assemble(place, G_FFN2, rh_ffn2, got[len(p2):]))
    dh_pairs = [(da1, w["ffn1_w1"]), (db1, w["ffn1_w3"])]
    if dist:
        pf2, pb2 = _pair_add(place, p2, t1, name="mixg_pair_add")
        grads["ffn1_w2"], t2 = tn(u1, df1, FF // 2, D, "ffn1_dw2", _ChipExchange(pb2))
        rh_mix = _chip_add(place, pf2, t2, name="mixg_chip_add")
        q_w2 = [_quarters(grads["ffn1_w2"], KIND["ffn1_w2"])]
        grads["ffn1_w1"], got = tn(da1, h1, FF // 2, D, "ffn1_dw1", _Multi([_PairShare(rh_mix), _PairExchange(q_w2)]))
        reduced.update(_assemble(place, G_MIX, rh_mix, got[:len(rh_mix)]))
        pf_w2, pb_w2 = _pair_add(place, q_w2, got[len(rh_mix):], name="ffn1w2_pair_add")
        q_w1 = [_quarters(grads["ffn1_w1"], KIND["ffn1_w1"])]
        grads["ffn1_w3"], got = tn(db1, h1, FF // 2, D, "ffn1_dw3", _Multi([_ChipExchange(pb_w2), _PairExchange(q_w1)]))
        t2_w2 = got[:1]
        pf_w1, pb_w1 = _pair_add(place, q_w1, got[1:], name="ffn1w1_pair_add")
        q_w3 = [_quarters(grads["ffn1_w3"], KIND["ffn1_w3"])]
        (dx0, s1), got = _dh_normbwd(dh_pairs, x, vec["norm_ffn1_g"], sc1, dx1, name="ffn1_bwd_dh",
                                     comm=_Multi([_ChipExchange(pb_w1), _PairExchange(q_w3)]))
        t2_w1 = got[:1]
        pf_w3, pb_w3 = _pair_add(place, q_w3, got[1:], name="ffn1w3_pair_add")
        t2_w3 = tail_host(_ChipExchange(pb_w3), reduced)
        rh = _chip_add(place, pf_w1 + pf_w3 + pf_w2, t2_w1 + t2_w3 + t2_w2, name="ffn1g_chip_add")
        reduced.update(_assemble(place, G_FFN1, rh, _run_comm(_PairShare(rh), name="ffn1g_pair_share")))
    else:
        grads["ffn1_w2"], _ = tn(u1, df1, FF // 2, D, "ffn1_dw2")
        grads["ffn1_w1"], _ = tn(da1, h1, FF // 2, D, "ffn1_dw1")
        grads["ffn1_w3"], _ = tn(db1, h1, FF // 2, D, "ffn1_dw3")
        (dx0, s1), _ = _dh_normbwd(dh_pairs, x, vec["norm_ffn1_g"], sc1, dx1, name="ffn1_bwd_dh")
        reduced = grads

    def part(s, k):
        return s[0:1, k * D:(k + 1) * D]

    dmod = jnp.concatenate([part(s1, 1), part(s1, 0), dg1[0:1], part(s2, 1), part(s2, 0), part(s_out, 0),
                            part(s3, 1), part(s3, 0), dg3[0:1]], axis=1)
    small = {"norm_ffn1_g": part(s1, 2), "norm_mix_g": part(s2, 2), "out_norm_g": part(s_out, 1),
             "norm_ffn2_g": part(s3, 2), "final_norm_g": dgfin[0:1],
             "q_norm_g": s_mid[0:1, 3 * CONV_W:3 * CONV_W + Q_LORA],
             "kv_norm_g": s_mid[0:1, 3 * CONV_W + Q_LORA:MID_SUMS], "conv_w": s_mid[0:1, 0:3 * CONV_W]}
    return loss_blk, dx0, reduced, dmod, small


SMALL = [("norm_ffn1_g", D), ("norm_mix_g", D), ("out_norm_g", D), ("norm_ffn2_g", D), ("final_norm_g", D),
         ("q_norm_g", Q_LORA), ("kv_norm_g", KV_LORA), ("conv_w", 3 * CONV_W)]
WEIGHTS = ['ada_w', 'ada_b', 'norm_ffn1_g', 'ffn1_w1', 'ffn1_w3', 'ffn1_w2', 'norm_mix_g', 'w_in', 'conv_w',
           'q_norm_g', 'w_uq', 'kv_norm_g', 'w_ukv', 'out_norm_g', 'w_out', 'norm_ffn2_g', 'ffn2_w1', 'ffn2_w3',
           'ffn2_w2', 'final_norm_g']


def kernel(x, c, positions, ada_w, ada_b, norm_ffn1_g, ffn1_w1, ffn1_w3, ffn1_w2, norm_mix_g, w_in, conv_w, q_norm_g, w_uq, kv_norm_g, w_ukv, out_norm_g, w_out, norm_ffn2_g, ffn2_w1, ffn2_w3, ffn2_w2, final_norm_g, loss_target, m_ada_w, m_ada_b, m_norm_ffn1_g, m_ffn1_w1, m_ffn1_w3, m_ffn1_w2, m_norm_mix_g, m_w_in, m_conv_w, m_q_norm_g, m_w_uq, m_kv_norm_g, m_w_ukv, m_out_norm_g, m_w_out, m_norm_ffn2_g, m_ffn2_w1, m_ffn2_w3, m_ffn2_w2, m_final_norm_g, v_ada_w, v_ada_b, v_norm_ffn1_g, v_ffn1_w1, v_ffn1_w3, v_ffn1_w2, v_norm_mix_g, v_w_in, v_conv_w, v_q_norm_g, v_w_uq, v_kv_norm_g, v_w_ukv, v_out_norm_g, v_w_out, v_norm_ffn2_g, v_ffn2_w1, v_ffn2_w3, v_ffn2_w2, v_final_norm_g):
    args = dict(locals())
    wts = {n: args[n] for n in WEIGHTS}
    mom = {n: args["m_" + n] for n in WEIGHTS}
    var = {n: args["v_" + n] for n in WEIGHTS}
    ax, ay, ac = _place()
    myq = 2 * ax + ay
    me = 2 * myq + ac
    place = jnp.stack([ac, myq]).astype(jnp.int32)

    shards = {name: _working_shard(wts[name], kind).astype(BF16) for name, kind in BULK}
    first = _run_comm(_Gather([shards[b[0]] for b in W_FIRST]), name="gather_ffn1")
    w_first = _gathered_weights(W_FIRST, shards, first, myq)

    mine = jnp.concatenate([c, conv_w[0].reshape(1, 3 * CONV_W // N_CHIPS)], axis=1)
    seen = _small_allgather(jnp.pad(mine, ((0, 7), (0, 0))), name="gather_cond").reshape(N_DEV, 8, -1)[:, 0]
    c_all = jnp.pad(seen[:, :D], ((0, 8), (0, 0)))
    conv_full = jnp.transpose(seen[0::2, D:].reshape(N_CHIPS, 3, CONV_W // N_CHIPS), (1, 0, 2)).reshape(3, CONV_W)
    ada_b_q = lax.dynamic_slice_in_dim(ada_b, myq * ADA_Q, ADA_Q, axis=1)
    mod_q = _ada_forward(c_all, ada_w[0], ada_b_q)
    mod_all = _small_allgather(mod_q, name="gather_mod").reshape(N_DEV, 16, ADA_Q)
    mod_rows = jnp.transpose(mod_all[0::2, :N_DEV], (1, 0, 2)).reshape(N_DEV, N_MOD * D)
    mod = lax.dynamic_slice_in_dim(mod_rows, me, 1, axis=0).reshape(N_MOD, D)

    vec = {n: wts[n] for n in ("norm_ffn1_g", "norm_mix_g", "q_norm_g", "kv_norm_g", "out_norm_g", "norm_ffn2_g")}
    vec["final_norm_g"] = final_norm_g.reshape(1, D)
    g, delta, new_m, new_v = {}, {}, {}, {}

    def adam(names, grads, comm=None):
        views = [(lambda a: jnp.swapaxes(a, 1, 2)) if KIND.get(n) == "colsT" else (lambda a: a) for n in names]
        res, couts = _adamw([vw(wts[n]) for n, vw in zip(names, views)], [grads[n] for n in names],
                            [vw(mom[n]) for n, vw in zip(names, views)], [vw(var[n]) for n, vw in zip(names, views)],
                            name="adamw_" + names[0], comm=comm)
        for n, vw, r in zip(names, views, res):
            g[n], delta[n], new_m[n], new_v[n] = [vw(a) for a in r]
        return couts

    loss_blk, grad_x, gq, dmod, small = _local_step(
        x[0], positions[0], loss_target[0], mod, vec, conv_full, w_first, {b[0]: shards[b[0]] for b in W_REST}, place,
        tail_host=lambda prog, grads: adam([b[0] for b in G_FFN2], grads, prog))
    loss = lax.psum(loss_blk[0, 0], ("x", "y", "c"))

    rows = jnp.concatenate([dmod] + [small[n] for n, _ in SMALL], axis=1)
    width = rows.shape[1]
    fold = -(-width // (8 * 128)) * 128
    rows = jnp.pad(rows, ((0, 0), (0, 8 * fold - width))).reshape(8, fold)
    every = _small_allgather(rows, name="gather_small").reshape(N_DEV, 8 * fold)[:, :width]
    total = _sum_devices(every)[0:1]
    dmod_q = lax.dynamic_slice_in_dim(every[:, :N_MOD * D], myq * ADA_Q, ADA_Q, axis=1)
    sg = {name: gq[name] for name, *_ in BULK}
    sg["ada_w"] = _ada_wgrad(c_all, jnp.pad(dmod_q, ((0, 8), (0, 0))))
    sg["ada_b"] = total[:, :N_MOD * D]
    off = N_MOD * D
    for n, width in SMALL:
        sg[n] = total[:, off:off + width]
        off += width
    sg["conv_w"] = lax.dynamic_slice_in_dim(sg["conv_w"].reshape(3, CONV_W), myq * (CONV_W // N_CHIPS),
                                            CONV_W // N_CHIPS, axis=1)

    for name in ["ada_w"] + [b[0] for b in BULK if b not in G_FFN2]:
        adam([name], sg)
    smalls = ["ada_b"] + [n for n, _ in SMALL]

    def packed(d):
        flat = jnp.concatenate([d[n].reshape(1, -1) for n in smalls], axis=1)
        return jnp.pad(flat.reshape(-1, D), ((0, 1), (0, 0)))

    res = _adamw([packed(wts)[None]], [packed(sg)], [packed(mom)[None]], [packed(var)[None]],
                 name="adamw_small")[0][0][1:]
    off = 0
    for n in smalls:
        size = wts[n].size
        for d, r in zip((delta, new_m, new_v), res):
            d[n] = r.reshape(-1)[off:off + size].reshape(wts[n].shape)
        g[n] = sg[n].reshape(wts[n].shape)
        off += size

    return (loss, grad_x[None], *[g[n] for n in WEIGHTS], *[delta[n] for n in WEIGHTS],
            *[new_m[n] for n in WEIGHTS], *[new_v[n] for n in WEIGHTS])
```
